```python
import math
import jax, jax.numpy as jnp
from jax import lax
import numpy as np

D_MODEL = 1024
BATCH = 4
SEQ = 4096
DEPTH = 1

SSM_HEAD_DIM = 64
D_SSM = D_MODEL
SSM_HEADS = D_SSM // SSM_HEAD_DIM
SSM_GROUPS = 4
D_STATE = 128
CONV_K = 4
CONV_CH = D_SSM + 2 * SSM_GROUPS * D_STATE
CHUNK = 128
ATT_HEAD_DIM = 64
D_ATT = D_MODEL
ATT_HEADS = D_ATT // ATT_HEAD_DIM
KV_HEADS = ATT_HEADS // 4
Q_PER_KV = ATT_HEADS // KV_HEADS
D_KV = KV_HEADS * ATT_HEAD_DIM
D_MIX = D_SSM + D_ATT
WINDOW = 128
ATT_BLOCK = 128
REL_BUCKETS = 32
REL_MAX_DIST = 128
IN_COLS = D_SSM + CONV_CH + SSM_HEADS + D_ATT + 2 * D_KV
N_EXPERTS = 64
TOP_K = 8
EXPERT_DIM = D_MODEL // 4
SHARED_DIM = D_MODEL // 4
ROUTE_GROUPS = 8
ROUTE_TOPK_GROUPS = 4
ROUTED_SCALE = 2.5
MOE_BLOCK = 128
EPS = 1e-6

kernel_name = 'hybrid_ssd_swa_sink_moe_block'


def rms_norm(x, g):
    xf = x.astype(jnp.float32)
    y = xf * lax.rsqrt(jnp.mean(xf * xf, axis=-1, keepdims=True) + EPS)
    return (y * g.astype(jnp.float32)).astype(x.dtype)


def causal_depthwise_conv(u, w, b):
    out = lax.conv_general_dilated(u, w[:, None, :].astype(u.dtype), window_strides=(1,),
                                   padding=[(CONV_K - 1, 0)], dimension_numbers=('NWC', 'WIO', 'NWC'),
                                   feature_group_count=u.shape[-1])
    return out + b.astype(u.dtype)


def ssd_scan(xs, Bm, Cm, dt_raw, dt_bias, a_log, d_skip):
    f32 = jnp.float32
    Bsz, L, _ = xs.shape
    nc = L // CHUNK
    R = SSM_HEADS // SSM_GROUPS
    dt = jax.nn.softplus(dt_raw.astype(f32) + dt_bias.astype(f32))
    A = -jnp.exp(a_log.astype(f32))
    x = xs.astype(f32).reshape(Bsz, L, SSM_HEADS, SSM_HEAD_DIM)
    Xd = x.reshape(Bsz, nc, CHUNK, SSM_GROUPS, R, SSM_HEAD_DIM) * dt.reshape(Bsz, nc, CHUNK, SSM_GROUPS, R)[..., None]
    a = jnp.moveaxis((dt * A).reshape(Bsz, nc, CHUNK, SSM_GROUPS, R), 2, -1)
    cs = jnp.cumsum(a, axis=-1)
    Bc = Bm.astype(f32).reshape(Bsz, nc, CHUNK, SSM_GROUPS, D_STATE)
    Cc = Cm.astype(f32).reshape(Bsz, nc, CHUNK, SSM_GROUPS, D_STATE)
    causal = jnp.tril(jnp.ones((CHUNK, CHUNK), dtype=bool))
    Lmat = jnp.exp(jnp.where(causal, cs[..., :, None] - cs[..., None, :], -jnp.inf))
    CB = jnp.einsum('bclgn,bcsgn->bcgls', Cc, Bc)
    y_diag = jnp.einsum('bcgls,bcgrls,bcsgrp->bclgrp', CB, Lmat, Xd)
    decay_states = jnp.exp(cs[..., -1:] - cs)
    chunk_states = jnp.einsum('bclgn,bcgrl,bclgrp->bcgrpn', Bc, decay_states, Xd)
    chunk_decay = jnp.exp(cs[..., -1])

    def step(state, inp):
        s_c, d_c = inp
        return state * d_c[..., None, None] + s_c, state

    init = jnp.zeros((Bsz, SSM_GROUPS, R, SSM_HEAD_DIM, D_STATE), f32)
    _, states_in = lax.scan(step, init, (jnp.moveaxis(chunk_states, 1, 0), jnp.moveaxis(chunk_decay, 1, 0)))
    states_in = jnp.moveaxis(states_in, 0, 1)
    y_off = jnp.einsum('bclgn,bcgrpn,bcgrl->bclgrp', Cc, states_in, jnp.exp(cs))
    y = (y_diag + y_off).reshape(Bsz, L, SSM_HEADS, SSM_HEAD_DIM) + x * d_skip.astype(f32)[:, None]
    return y.reshape(Bsz, L, D_SSM)


def t5_causal_bucket(dist):
    max_exact = REL_BUCKETS // 2
    d = jnp.maximum(dist, 1).astype(jnp.float32)
    large = max_exact + (jnp.log(d / max_exact) / math.log(REL_MAX_DIST / max_exact)
                         * (REL_BUCKETS - max_exact)).astype(jnp.int32)
    large = jnp.minimum(large, REL_BUCKETS - 1)
    return jnp.where(dist < max_exact, dist, large)


def sliding_window_attention(q, k, v, sinks, rel_bias):
    f32 = jnp.float32
    Bsz, L, _ = q.shape
    nb = L // ATT_BLOCK
    qb = q.reshape(Bsz, nb, ATT_BLOCK, KV_HEADS, Q_PER_KV, ATT_HEAD_DIM)

    def band(t):
        tp = jnp.pad(t.reshape(Bsz, L, KV_HEADS, ATT_HEAD_DIM), ((0, 0), (ATT_BLOCK, 0), (0, 0), (0, 0)))
        tp = tp.reshape(Bsz, nb + 1, ATT_BLOCK, KV_HEADS, ATT_HEAD_DIM)
        return jnp.concatenate([tp[:, :-1], tp[:, 1:]], axis=2)

    kb, vb = band(k), band(v)
    qi = jnp.arange(ATT_BLOCK)[:, None]
    sj = jnp.arange(2 * ATT_BLOCK)[None, :]
    dist = qi + ATT_BLOCK - sj
    in_win = (dist >= 0) & (dist < WINDOW)
    key_abs = jnp.arange(nb)[:, None, None] * ATT_BLOCK - ATT_BLOCK + sj[None]
    mask = in_win[None] & (key_abs >= 0)
    bias = rel_bias.astype(f32)[t5_causal_bucket(jnp.maximum(dist, 0))]
    bias = bias.transpose(2, 0, 1).reshape(KV_HEADS, Q_PER_KV, ATT_BLOCK, 2 * ATT_BLOCK)
    s = jnp.einsum('bnqkgd,bnskd->bnkgqs', qb, kb).astype(f32) * (ATT_HEAD_DIM ** -0.5) + bias
    s = jnp.where(mask[None, :, None, None], s, -jnp.inf)
    sink = sinks.astype(f32).reshape(KV_HEADS, Q_PER_KV)[None, None, :, :, None, None]
    m = jnp.maximum(jnp.max(s, axis=-1, keepdims=True), sink)
    p = jnp.exp(s - m)
    p = p / (jnp.sum(p, axis=-1, keepdims=True) + jnp.exp(sink - m))
    o = jnp.einsum('bnkgqs,bnskd->bnqkgd', p.astype(v.dtype), vb)
    return o.reshape(Bsz, L, D_ATT)


def hybrid_mixer(h, w_in, conv_w, conv_b, dt_bias, a_log, d_skip, ssm_norm_g, att_norm_g, sinks, rel_bias, w_out):
    Bsz, L, _ = h.shape
    proj = h @ w_in
    i1 = D_SSM
    i2 = i1 + CONV_CH
    i3 = i2 + SSM_HEADS
    i4 = i3 + D_ATT
    i5 = i4 + D_KV
    z, xbc, dt_raw, q, k, v = jnp.split(proj, [i1, i2, i3, i4, i5], axis=-1)
    xbc = jax.nn.silu(causal_depthwise_conv(xbc, conv_w, conv_b))
    xs, Bm, Cm = jnp.split(xbc, [D_SSM, D_SSM + SSM_GROUPS * D_STATE], axis=-1)
    y = ssd_scan(xs, Bm, Cm, dt_raw, dt_bias, a_log, d_skip)
    yz = (y * jax.nn.silu(z.astype(jnp.float32))).reshape(Bsz, L, SSM_GROUPS, D_SSM // SSM_GROUPS)
    yz = yz * lax.rsqrt(jnp.mean(yz * yz, axis=-1, keepdims=True) + EPS)
    y_ssm = (yz.reshape(Bsz, L, D_SSM) * ssm_norm_g.astype(jnp.float32)).astype(h.dtype)
    y_att = rms_norm(sliding_window_attention(q, k, v, sinks, rel_bias), att_norm_g)
    return jnp.concatenate([y_ssm, y_att], axis=-1) @ w_out


def moe_ffn(h, router_w, router_bias, exp_w_gate, exp_w_up, exp_w_down, sh_w_gate, sh_w_up, sh_w_down):
    f32 = jnp.float32
    N = h.shape[0]
    scores = jax.nn.sigmoid(h.astype(f32) @ router_w.astype(f32))
    sel = scores + router_bias.astype(f32)
    per_group = N_EXPERTS // ROUTE_GROUPS
    grp_score = lax.top_k(sel.reshape(N, ROUTE_GROUPS, per_group), 2)[0].sum(-1)
    _, gidx = lax.top_k(grp_score, ROUTE_TOPK_GROUPS)
    gmask = jax.nn.one_hot(gidx, ROUTE_GROUPS, dtype=f32).sum(1)
    allowed = jnp.repeat(gmask, per_group, axis=1) > 0
    _, idx = lax.top_k(jnp.where(allowed, sel, -jnp.inf), TOP_K)
    w = jnp.take_along_axis(scores, idx, axis=1)
    gates = w / jnp.sum(w, axis=-1, keepdims=True) * ROUTED_SCALE

    A = N * TOP_K
    NB = (A + N_EXPERTS * (MOE_BLOCK - 1) + MOE_BLOCK - 1) // MOE_BLOCK
    P = NB * MOE_BLOCK
    flat_e = idx.reshape(-1)
    flat_tok = jnp.arange(A, dtype=jnp.int32) // TOP_K
    order = jnp.argsort(flat_e)
    se, stok, sg = flat_e[order], flat_tok[order], gates.reshape(-1)[order]
    counts = jnp.bincount(flat_e, length=N_EXPERTS)
    padded = (counts + MOE_BLOCK - 1) // MOE_BLOCK * MOE_BLOCK
    pad_end = jnp.cumsum(padded)
    pad_start = pad_end - padded
    grp_start = jnp.cumsum(counts) - counts
    dest = pad_start[se] + (jnp.arange(A) - grp_start[se])
    row_tok = jnp.zeros((P,), jnp.int32).at[dest].set(stok)
    row_gate = jnp.zeros((P,), f32).at[dest].set(sg)
    block_expert = jnp.minimum(jnp.searchsorted(pad_end, jnp.arange(NB) * MOE_BLOCK, side='right'), N_EXPERTS - 1)

    def expert_block(args):
        toks, e, g = args
        xb = h[toks]
        u = jax.nn.silu(xb @ exp_w_gate[e]) * (xb @ exp_w_up[e])
        return (u @ exp_w_down[e]).astype(f32) * g[:, None]

    y_rows = lax.map(expert_block, (row_tok.reshape(NB, MOE_BLOCK), block_expert, row_gate.reshape(NB, MOE_BLOCK)))
    routed = jax.ops.segment_sum(y_rows.reshape(P, -1), row_tok, num_segments=N)
    shared = (jax.nn.silu(h @ sh_w_gate) * (h @ sh_w_up)) @ sh_w_down
    return (routed + shared.astype(f32)).astype(h.dtype)


def setup_inputs(seed: int = 0) -> dict:
    key = jax.random.key(seed)
    ks = jax.random.split(key, 26)
    f32 = jnp.float32

    def nrm(k, shape, scale):
        return jax.random.normal(k, shape, f32) * scale

    dt0 = jnp.exp(jax.random.uniform(ks[9], (DEPTH, SSM_HEADS), f32, math.log(1e-3), math.log(1e-1)))
    return {
        'x': nrm(ks[0], (BATCH, SEQ, D_MODEL), 1.0),
        'c': nrm(ks[1], (BATCH, D_MODEL), 1.0),
        'mod_w': nrm(ks[2], (DEPTH, D_MODEL, 6 * D_MODEL), 0.5 * D_MODEL ** -0.5),
        'mod_b': nrm(ks[3], (DEPTH, 6 * D_MODEL), 0.02),
        'norm1_g': 1.0 + nrm(ks[4], (DEPTH, D_MODEL), 0.02),
        'norm2_g': 1.0 + nrm(ks[5], (DEPTH, D_MODEL), 0.02),
        'w_in': nrm(ks[6], (DEPTH, D_MODEL, IN_COLS), D_MODEL ** -0.5),
        'conv_w': nrm(ks[7], (DEPTH, CONV_K, CONV_CH), CONV_K ** -0.5),
        'conv_b': nrm(ks[8], (DEPTH, CONV_CH), 0.02),
        'dt_bias': dt0 + jnp.log(-jnp.expm1(-dt0)),
        'a_log': jnp.log(jax.random.uniform(ks[10], (DEPTH, SSM_HEADS), f32, 1.0, 16.0)),
        'd_skip': 1.0 + nrm(ks[11], (DEPTH, SSM_HEADS), 0.1),
        'ssm_norm_g': 1.0 + nrm(ks[12], (DEPTH, D_SSM), 0.02),
        'att_norm_g': 1.0 + nrm(ks[13], (DEPTH, D_ATT), 0.02),
        'sinks': nrm(ks[14], (DEPTH, ATT_HEADS), 0.5),
        'rel_bias': nrm(ks[15], (REL_BUCKETS, ATT_HEADS), 0.5),
        'w_out': nrm(ks[16], (DEPTH, D_MIX, D_MODEL), D_MIX ** -0.5),
        'router_w': nrm(ks[17], (DEPTH, D_MODEL, N_EXPERTS), D_MODEL ** -0.5),
        'router_bias': nrm(ks[18], (DEPTH, N_EXPERTS), 0.01),
        'exp_w_gate': nrm(ks[19], (DEPTH, N_EXPERTS, D_MODEL, EXPERT_DIM), D_MODEL ** -0.5),
        'exp_w_up': nrm(ks[20], (DEPTH, N_EXPERTS, D_MODEL, EXPERT_DIM), D_MODEL ** -0.5),
        'exp_w_down': nrm(ks[21], (DEPTH, N_EXPERTS, EXPERT_DIM, D_MODEL), EXPERT_DIM ** -0.5),
        'sh_w_gate': nrm(ks[22], (DEPTH, D_MODEL, SHARED_DIM), D_MODEL ** -0.5),
        'sh_w_up': nrm(ks[23], (DEPTH, D_MODEL, SHARED_DIM), D_MODEL ** -0.5),
        'sh_w_down': nrm(ks[24], (DEPTH, SHARED_DIM, D_MODEL), SHARED_DIM ** -0.5),
        'final_g': 1.0 + nrm(ks[25], (D_MODEL,), 0.02),
    }


def reference(x, c, mod_w, mod_b, norm1_g, norm2_g, w_in, conv_w, conv_b, dt_bias, a_log, d_skip,
              ssm_norm_g, att_norm_g, sinks, rel_bias, w_out, router_w, router_bias, exp_w_gate, exp_w_up,
              exp_w_down, sh_w_gate, sh_w_up, sh_w_down, final_g):
    Bsz, L, D = x.shape
    for layer in range(DEPTH):
        mod = jax.nn.silu(c) @ mod_w[layer] + mod_b[layer]
        sh1, sc1, g1, sh2, sc2, g2 = [m[:, None, :] for m in jnp.split(mod, 6, axis=-1)]
        h = rms_norm(x, norm1_g[layer]) * (1.0 + sc1) + sh1
        x = x + g1 * hybrid_mixer(h, w_in[layer], conv_w[layer], conv_b[layer], dt_bias[layer], a_log[layer],
                                  d_skip[layer], ssm_norm_g[layer], att_norm_g[layer], sinks[layer], rel_bias,
                                  w_out[layer])
        h = rms_norm(x, norm2_g[layer]) * (1.0 + sc2) + sh2
        y = moe_ffn(h.reshape(Bsz * L, D), router_w[layer], router_bias[layer], exp_w_gate[layer], exp_w_up[layer],
                    exp_w_down[layer], sh_w_gate[layer], sh_w_up[layer], sh_w_down[layer])
        x = x + g2 * y.reshape(Bsz, L, D)
    return rms_norm(x, final_g)
```

```python
import functools
import math

import numpy as np
import jax
import jax.numpy as jnp
from jax import lax
from jax.experimental import pallas as pl
from jax.experimental.pallas import tpu as pltpu

F32 = jnp.float32
BF16 = jnp.bfloat16

D_MODEL = 1024
SSM_HEAD_DIM = 64
D_SSM = D_MODEL
SSM_HEADS = D_SSM // SSM_HEAD_DIM
SSM_GROUPS = 4
D_STATE = 128
CONV_K = 4
CONV_CH = D_SSM + 2 * SSM_GROUPS * D_STATE
CHUNK = 128
ATT_HEAD_DIM = 64
D_ATT = D_MODEL
ATT_HEADS = D_ATT // ATT_HEAD_DIM
KV_HEADS = ATT_HEADS // 4
Q_PER_KV = ATT_HEADS // KV_HEADS
D_KV = KV_HEADS * ATT_HEAD_DIM
WINDOW = 128
ATT_BLOCK = 128
REL_BUCKETS = 32
REL_MAX_DIST = 128
N_EXPERTS = 64
TOP_K = 8
EXPERT_DIM = D_MODEL // 4
SHARED_DIM = D_MODEL // 4
ROUTE_GROUPS = 8
ROUTE_TOPK_GROUPS = 4
ROUTED_SCALE = 2.5
EPS = 1e-6

LANES = 128
SUBLANES = 8
HALF = LANES // 2

ROW_TILE = 512
ROUTE_TILE = 512
DISPATCH_TILE = 256
COMBINE_TILE = 128
EXPERT_ROWS = 256
VMEM_LIMIT = 48 * 1024 * 1024

NEG_INF = float("-inf")


def _silu(v):
    return v * (1.0 / (1.0 + jnp.exp(-v)))


def _softplus(v):
    return jnp.maximum(v, 0.0) + jnp.log(1.0 + jnp.exp(-jnp.abs(v)))


def _bdot(a, b):
    return jnp.dot(a.astype(BF16), b.astype(BF16), preferred_element_type=F32)


def _split_hi_lo(v):
    hi = v.astype(BF16)
    lo = (v - hi.astype(F32)).astype(BF16)
    return hi, lo


def _lane_half_mask(shape):
    return lax.broadcasted_iota(jnp.int32, shape, len(shape) - 1) < HALF


def _mod_kernel(c_ref, w_ref, b_ref, o_ref):
    a = _silu(c_ref[...])
    o_ref[...] = jnp.dot(a, w_ref[...], precision=lax.Precision.HIGHEST,
                         preferred_element_type=F32) + b_ref[...]


def _mod_call(c_pad, mod_w, mod_b):
    rows, d = c_pad.shape
    cols = mod_w.shape[1]
    return pl.pallas_call(
        _mod_kernel,
        grid=(cols // d,),
        in_specs=[pl.BlockSpec((rows, d), lambda j: (0, 0)),
                  pl.BlockSpec((d, d), lambda j: (0, j)),
                  pl.BlockSpec((1, d), lambda j: (0, j))],
        out_specs=pl.BlockSpec((rows, d), lambda j: (0, j)),
        out_shape=jax.ShapeDtypeStruct((rows, cols), F32),
        compiler_params=pltpu.CompilerParams(dimension_semantics=("arbitrary",),
                                             vmem_limit_bytes=VMEM_LIMIT),
    )(c_pad, mod_w, mod_b)


def _in_proj_kernel(x_ref, sc_ref, sh_ref, g_ref, wz_ref, wx_ref, wdt_ref, wq_ref, wk_ref, wv_ref,
                    z_ref, xbc_ref, dt_ref, q_ref, k_ref, v_ref):
    xf = x_ref[...]
    ms = jnp.mean(xf * xf, axis=-1, keepdims=True)
    h = xf * lax.rsqrt(ms + EPS) * g_ref[...]
    h = h * (1.0 + sc_ref[0]) + sh_ref[0]
    hb = h.astype(BF16)
    z_ref[...] = jnp.dot(hb, wz_ref[...], preferred_element_type=F32).astype(BF16)
    xbc_ref[...] = jnp.dot(hb, wx_ref[...], preferred_element_type=F32).astype(BF16)
    dt_ref[...] = jnp.dot(hb, wdt_ref[...], preferred_element_type=F32)
    q_ref[...] = jnp.dot(hb, wq_ref[...], preferred_element_type=F32).astype(BF16)
    k_ref[...] = jnp.dot(hb, wk_ref[...], preferred_element_type=F32).astype(BF16)
    v_ref[...] = jnp.dot(hb, wv_ref[...], preferred_element_type=F32).astype(BF16)


def _in_proj_call(x2, sc1, sh1, g1n, wz, wx, wdt, wq, wk, wv, tiles_per_batch, tm):
    n, d = x2.shape
    row = lambda w: pl.BlockSpec((tm, w), lambda i: (i, 0))
    full = lambda a: pl.BlockSpec(a.shape, lambda i: (0, 0))
    per_batch = pl.BlockSpec((1, 1, d), lambda i: (i // tiles_per_batch, 0, 0))
    outs = [(wz.shape[1], BF16), (wx.shape[1], BF16), (wdt.shape[1], F32),
            (wq.shape[1], BF16), (wk.shape[1], BF16), (wv.shape[1], BF16)]
    return pl.pallas_call(
        _in_proj_kernel,
        grid=(n // tm,),
        in_specs=[row(d), per_batch, per_batch, full(g1n), full(wz), full(wx), full(wdt), full(wq),
                  full(wk), full(wv)],
        out_specs=[row(w) for w, _ in outs],
        out_shape=[jax.ShapeDtypeStruct((n, w), dt) for w, dt in outs],
        compiler_params=pltpu.CompilerParams(dimension_semantics=("arbitrary",),
                                             vmem_limit_bytes=VMEM_LIMIT),
    )(x2, sc1, sh1, g1n, wz, wx, wdt, wq, wk, wv)


def _ssd_kernel(xbc_ref, z_ref, dt_ref, cw_ref, cb_ref, dtb_ref, alog_ref, dskip_ref, ng_ref, tril_ref,
                y_ref, state_ref, tail_ref, ext_ref, ybuf_ref):
    c = pl.program_id(1)

    @pl.when(c == 0)
    def _():
        state_ref[...] = jnp.zeros_like(state_ref)
        tail_ref[...] = jnp.zeros_like(tail_ref)

    u = xbc_ref[0].astype(F32)
    ext_ref[0:SUBLANES, :] = tail_ref[...]
    ext_ref[SUBLANES:SUBLANES + CHUNK, :] = u
    tail_ref[...] = u[CHUNK - SUBLANES:, :]
    acc = cb_ref[...] + jnp.zeros_like(u)
    for kk in range(CONV_K):
        off = SUBLANES - (CONV_K - 1) + kk
        acc = acc + cw_ref[kk:kk + 1, :] * ext_ref[off:off + CHUNK, :]
    act = _silu(acc)
    xs = act[:, :D_SSM]
    gn = SSM_GROUPS * D_STATE

    dt = _softplus(dt_ref[0] + dtb_ref[...])
    a = dt * (-jnp.exp(alog_ref[...]))
    cs = jnp.dot(tril_ref[...], a, precision=lax.Precision.HIGHEST, preferred_element_type=F32)
    cs_t = cs.T
    dt_t = dt.T
    exp_cs = jnp.exp(cs)
    cs_last = cs[CHUNK - 1:CHUNK, :]
    chunk_decay = jnp.exp(cs_last)

    li = lax.broadcasted_iota(jnp.int32, (CHUNK, CHUNK), 0)
    si = lax.broadcasted_iota(jnp.int32, (CHUNK, CHUNK), 1)
    causal = li >= si
    low = _lane_half_mask((CHUNK, LANES))
    low_row = _lane_half_mask((1, LANES))

    heads_per_group = SSM_HEADS // SSM_GROUPS
    for g in range(SSM_GROUPS):
        b_g = act[:, D_SSM + g * D_STATE:D_SSM + (g + 1) * D_STATE]
        c_g = act[:, D_SSM + gn + g * D_STATE:D_SSM + gn + (g + 1) * D_STATE]
        b_gb = b_g.astype(BF16)
        c_gb = c_g.astype(BF16)
        cb = lax.dot_general(c_gb, b_gb, (((1,), (1,)), ((), ())), preferred_element_type=F32)
        b_t = b_g.T
        for jp in range(heads_per_group // 2):
            j = g * (heads_per_group // 2) + jp
            lanes = slice(j * LANES, (j + 1) * LANES)
            xp = xs[:, lanes]
            ydiag = jnp.zeros((CHUNK, LANES), F32)
            snew = jnp.zeros((D_STATE, LANES), F32)
            for half in range(2):
                h = 2 * j + half
                row_dt = dt_t[h:h + 1, :]
                diff = cs[:, h:h + 1] - cs_t[h:h + 1, :]
                lmat = jnp.exp(jnp.where(causal, diff, NEG_INF))
                m = (cb * lmat * row_dt).astype(BF16)
                keep = low if half == 0 else jnp.logical_not(low)
                xh = jnp.where(keep, xp, 0.0).astype(BF16)
                ydiag = ydiag + jnp.dot(m, xh, preferred_element_type=F32)
                w_t = jnp.exp(cs_t[h:h + 1, CHUNK - 1:CHUNK] - cs_t[h:h + 1, :]) * row_dt
                snew = snew + jnp.dot((b_t * w_t).astype(BF16), xh, preferred_element_type=F32)
            s_in = state_ref[:, lanes]
            yoff = jnp.dot(c_gb, s_in.astype(BF16), preferred_element_type=F32)
            h0 = 2 * j
            escale = jnp.where(low, exp_cs[:, h0:h0 + 1], exp_cs[:, h0 + 1:h0 + 2])
            cdec = jnp.where(low_row, chunk_decay[:, h0:h0 + 1], chunk_decay[:, h0 + 1:h0 + 2])
            ybuf_ref[:, lanes] = ydiag + yoff * escale + xp * dskip_ref[:, lanes]
            state_ref[:, lanes] = s_in * cdec + snew

    yz = ybuf_ref[...] * _silu(z_ref[0].astype(F32))
    gw = D_SSM // SSM_GROUPS
    for g in range(SSM_GROUPS):
        part = yz[:, g * gw:(g + 1) * gw]
        ms = jnp.mean(part * part, axis=-1, keepdims=True)
        y_ref[0, :, g * gw:(g + 1) * gw] = (part * lax.rsqrt(ms + EPS)
                                            * ng_ref[:, g * gw:(g + 1) * gw]).astype(BF16)


def _ssd_call(xbc, z, dt, conv_w, conv_b, dtb, alog, dskip, ng, tril):
    bsz, l, _ = xbc.shape
    nc = l // CHUNK
    chunk = lambda w: pl.BlockSpec((1, CHUNK, w), lambda b, c: (b, c, 0))
    full = lambda a: pl.BlockSpec(a.shape, lambda b, c: (0, 0))
    return pl.pallas_call(
        _ssd_kernel,
        grid=(bsz, nc),
        in_specs=[chunk(CONV_CH), chunk(D_SSM), chunk(LANES), full(conv_w), full(conv_b), full(dtb),
                  full(alog), full(dskip), full(ng), full(tril)],
        out_specs=chunk(D_SSM),
        out_shape=jax.ShapeDtypeStruct((bsz, l, D_SSM), BF16),
        scratch_shapes=[pltpu.VMEM((D_STATE, D_SSM), F32),
                        pltpu.VMEM((SUBLANES, CONV_CH), F32),
                        pltpu.VMEM((SUBLANES + CHUNK, CONV_CH), F32),
                        pltpu.VMEM((CHUNK, D_SSM), F32)],
        compiler_params=pltpu.CompilerParams(dimension_semantics=("arbitrary", "arbitrary"),
                                             vmem_limit_bytes=VMEM_LIMIT),
    )(xbc, z, dt, conv_w, conv_b, dtb, alog, dskip, ng, tril)


def _rel_bucket_table():
    qi = np.arange(ATT_BLOCK)[:, None]
    sj = np.arange(2 * ATT_BLOCK)[None, :]
    dist = qi + ATT_BLOCK - sj
    in_win = (dist >= 0) & (dist < WINDOW)
    dcl = np.maximum(dist, 0)
    max_exact = REL_BUCKETS // 2
    d = np.maximum(dcl, 1).astype(np.float32)
    large = max_exact + (np.log(d / np.float32(max_exact)) / np.float32(math.log(REL_MAX_DIST / max_exact))
                         * np.float32(REL_BUCKETS - max_exact)).astype(np.int32)
    large = np.minimum(large, REL_BUCKETS - 1)
    bucket = np.where(dcl < max_exact, dcl, large).astype(np.int32)
    return bucket, in_win.astype(np.int32)


def _bias_kernel(rb_ref, bucket_ref, win_ref, o_ref):
    h = pl.program_id(0)
    bucket = bucket_ref[...]
    acc = jnp.zeros(bucket.shape, F32)
    for b in range(REL_BUCKETS):
        acc = jnp.where(bucket == b, rb_ref[b, h], acc)
    o_ref[0] = jnp.where(win_ref[...] > 0, acc, NEG_INF)


def _bias_call(rel_bias, bucket, win):
    return pl.pallas_call(
        _bias_kernel,
        grid=(ATT_HEADS,),
        in_specs=[pl.BlockSpec(memory_space=pltpu.SMEM),
                  pl.BlockSpec(bucket.shape, lambda h: (0, 0)),
                  pl.BlockSpec(win.shape, lambda h: (0, 0))],
        out_specs=pl.BlockSpec((1,) + bucket.shape, lambda h: (h, 0, 0)),
        out_shape=jax.ShapeDtypeStruct((ATT_HEADS,) + bucket.shape, F32),
        compiler_params=pltpu.CompilerParams(dimension_semantics=("arbitrary",)),
    )(rel_bias, bucket, win)


def _attn_kernel(sink_ref, q_ref, kp_ref, kc_ref, vp_ref, vc_ref, bias_ref, ng_ref, o_ref, obuf_ref):
    i = pl.program_id(1)
    kband = jnp.concatenate([kp_ref[0], kc_ref[0]], axis=0).astype(F32)
    vband = jnp.concatenate([vp_ref[0], vc_ref[0]], axis=0).astype(F32)
    sj = lax.broadcasted_iota(jnp.int32, (ATT_BLOCK, 2 * ATT_BLOCK), 1)
    valid = jnp.logical_or(sj >= ATT_BLOCK, i > 0)
    low = _lane_half_mask((ATT_BLOCK, LANES))
    scale = ATT_HEAD_DIM ** -0.5

    kv_cols = []
    for cpair in range(KV_HEADS // 2):
        kk = kband[:, cpair * LANES:(cpair + 1) * LANES]
        vv = vband[:, cpair * LANES:(cpair + 1) * LANES]
        kv_cols.append(((kk.astype(BF16), pltpu.roll(kk, HALF, 1).astype(BF16)),
                        (vv.astype(BF16), pltpu.roll(vv, HALF, 1).astype(BF16))))

    for j in range(ATT_HEADS // 2):
        qp = q_ref[0, :, j * LANES:(j + 1) * LANES]
        out_pair = jnp.zeros((ATT_BLOCK, LANES), F32)
        for half in range(2):
            h = 2 * j + half
            g = h // Q_PER_KV
            swapped = int((g % 2) != half)
            ksel = kv_cols[g // 2][0][swapped]
            vsel = kv_cols[g // 2][1][swapped]
            keep = low if half == 0 else jnp.logical_not(low)
            qh = jnp.where(keep, qp, jnp.zeros_like(qp))
            s = lax.dot_general(qh, ksel, (((1,), (1,)), ((), ())), preferred_element_type=F32)
            s = s * scale + bias_ref[h]
            s = jnp.where(valid, s, NEG_INF)
            sink = sink_ref[h]
            m = jnp.maximum(jnp.max(s, axis=-1, keepdims=True), sink)
            p = jnp.exp(s - m)
            denom = jnp.sum(p, axis=-1, keepdims=True) + jnp.exp(sink - m)
            o = jnp.dot(p.astype(BF16), vsel, preferred_element_type=F32) / denom
            out_pair = out_pair + jnp.where(keep, o, 0.0)
        obuf_ref[:, j * LANES:(j + 1) * LANES] = out_pair

    att = obuf_ref[...]
    ms = jnp.mean(att * att, axis=-1, keepdims=True)
    o_ref[0] = (att * lax.rsqrt(ms + EPS) * ng_ref[...]).astype(BF16)


def _attn_call(sinks, q, k, v, bias, ng):
    bsz, l, _ = q.shape
    nb = l // ATT_BLOCK
    cur = lambda w: pl.BlockSpec((1, ATT_BLOCK, w), lambda b, i: (b, i, 0))
    prev = lambda w: pl.BlockSpec((1, ATT_BLOCK, w), lambda b, i: (b, jnp.maximum(i - 1, 0), 0))
    return pl.pallas_call(
        _attn_kernel,
        grid=(bsz, nb),
        in_specs=[pl.BlockSpec(memory_space=pltpu.SMEM),
                  cur(D_ATT), prev(D_KV), cur(D_KV), prev(D_KV), cur(D_KV),
                  pl.BlockSpec(bias.shape, lambda b, i: (0, 0, 0)),
                  pl.BlockSpec(ng.shape, lambda b, i: (0, 0))],
        out_specs=cur(D_ATT),
        out_shape=jax.ShapeDtypeStruct((bsz, l, D_ATT), BF16),
        scratch_shapes=[pltpu.VMEM((ATT_BLOCK, D_ATT), F32)],
        compiler_params=pltpu.CompilerParams(dimension_semantics=("arbitrary", "arbitrary"),
                                             vmem_limit_bytes=VMEM_LIMIT),
    )(sinks, q, k, k, v, v, bias, ng)


def _out_proj_kernel(x_ref, ys_ref, ya_ref, g1_ref, sc_ref, sh_ref, g2_ref, ng_ref, wos_ref, woa_ref,
                     rwh_ref, rwl_ref, sg_ref, su_ref, sd_ref, base_ref, h_ref, lg_ref):
    mix = (jnp.dot(ys_ref[...], wos_ref[...], preferred_element_type=F32)
           + jnp.dot(ya_ref[...], woa_ref[...], preferred_element_type=F32))
    x1 = x_ref[...] + g1_ref[0] * mix
    ms = jnp.mean(x1 * x1, axis=-1, keepdims=True)
    h = x1 * lax.rsqrt(ms + EPS) * ng_ref[...]
    h = h * (1.0 + sc_ref[0]) + sh_ref[0]
    h_ref[...] = h
    hi, lo = _split_hi_lo(h)
    logits = (jnp.dot(hi, rwh_ref[...], preferred_element_type=F32)
              + jnp.dot(lo, rwh_ref[...], preferred_element_type=F32)
              + jnp.dot(hi, rwl_ref[...], preferred_element_type=F32))
    lg_ref[...] = logits.T
    u = _silu(jnp.dot(hi, sg_ref[...], preferred_element_type=F32)) * jnp.dot(hi, su_ref[...],
                                                                              preferred_element_type=F32)
    shared = jnp.dot(u.astype(BF16), sd_ref[...], preferred_element_type=F32)
    base_ref[...] = x1 + g2_ref[0] * shared


def _out_proj_call(x2, ys, ya, g1, sc2, sh2, g2, ng, wos, woa, rwh, rwl, sg, su, sd, tiles_per_batch, tm):
    n, d = x2.shape
    row = lambda w: pl.BlockSpec((tm, w), lambda i: (i, 0))
    full = lambda a: pl.BlockSpec(a.shape, lambda i: (0, 0))
    per_batch = pl.BlockSpec((1, 1, d), lambda i: (i // tiles_per_batch, 0, 0))
    return pl.pallas_call(
        _out_proj_kernel,
        grid=(n // tm,),
        in_specs=[row(d), row(D_SSM), row(D_ATT), per_batch, per_batch, per_batch, per_batch, full(ng),
                  full(wos), full(woa), full(rwh), full(rwl), full(sg), full(su), full(sd)],
        out_specs=[row(d), row(d), pl.BlockSpec((LANES, tm), lambda i: (0, i))],
        out_shape=[jax.ShapeDtypeStruct((n, d), F32), jax.ShapeDtypeStruct((n, d), F32),
                   jax.ShapeDtypeStruct((LANES, n), F32)],
        compiler_params=pltpu.CompilerParams(dimension_semantics=("arbitrary",),
                                             vmem_limit_bytes=VMEM_LIMIT),
    )(x2, ys, ya, g1, sc2, sh2, g2, ng, wos, woa, rwh, rwl, sg, su, sd)


def _route_kernel(lg_ref, rb_ref, upper_ref, idx_ref, gate_ref, rank_ref, cnt_ref, carry_ref):
    step = pl.program_id(0)

    @pl.when(step == 0)
    def _():
        carry_ref[...] = jnp.zeros_like(carry_ref)

    t = lg_ref.shape[1]
    per_group = N_EXPERTS // ROUTE_GROUPS
    scores = 1.0 / (1.0 + jnp.exp(-lg_ref[0:N_EXPERTS, :]))
    sel = scores + rb_ref[...]
    e_iota = lax.broadcasted_iota(jnp.int32, (N_EXPERTS, t), 0)

    sel3 = sel.reshape(ROUTE_GROUPS, per_group, t)
    w_iota = lax.broadcasted_iota(jnp.int32, sel3.shape, 1)
    m1 = jnp.max(sel3, axis=1, keepdims=True)
    first = jnp.min(jnp.where(sel3 == m1, w_iota, per_group), axis=1, keepdims=True)
    m2 = jnp.max(jnp.where(w_iota == first, NEG_INF, sel3), axis=1, keepdims=True)
    grp = (m1 + m2).reshape(ROUTE_GROUPS, t)

    g_iota = lax.broadcasted_iota(jnp.int32, (ROUTE_GROUPS, t), 0)
    gmask = jnp.zeros((ROUTE_GROUPS, t), jnp.bool_)
    for _ in range(ROUTE_TOPK_GROUPS):
        gm = jnp.max(grp, axis=0, keepdims=True)
        gfirst = jnp.min(jnp.where(grp == gm, g_iota, ROUTE_GROUPS), axis=0, keepdims=True)
        hit = g_iota == gfirst
        gmask = jnp.logical_or(gmask, hit)
        grp = jnp.where(hit, NEG_INF, grp)
    allowed = jnp.broadcast_to(gmask.reshape(ROUTE_GROUPS, 1, t),
                               (ROUTE_GROUPS, per_group, t)).reshape(N_EXPERTS, t)
    masked = jnp.where(allowed, sel, NEG_INF)

    picked = jnp.zeros((N_EXPERTS, t), jnp.bool_)
    idx_rows = []
    w_rows = []
    for _ in range(TOP_K):
        mm = jnp.max(masked, axis=0, keepdims=True)
        efirst = jnp.min(jnp.where(masked == mm, e_iota, N_EXPERTS), axis=0, keepdims=True)
        hit = e_iota == efirst
        idx_rows.append(efirst)
        w_rows.append(jnp.sum(jnp.where(hit, scores, 0.0), axis=0, keepdims=True))
        picked = jnp.logical_or(picked, hit)
        masked = jnp.where(hit, NEG_INF, masked)
    idx = jnp.concatenate(idx_rows, axis=0)
    w = jnp.concatenate(w_rows, axis=0)
    gate_ref[...] = w / jnp.sum(w, axis=0, keepdims=True) * ROUTED_SCALE
    idx_ref[...] = idx

    onehot = jnp.where(picked, 1.0, 0.0)
    before = jnp.dot(onehot.astype(BF16), upper_ref[...], preferred_element_type=F32)
    rank_full = before + carry_ref[:, 0:1]
    rank_rows = [jnp.sum(jnp.where(e_iota == idx_rows[k], rank_full, 0.0), axis=0, keepdims=True)
                 for k in range(TOP_K)]
    rank_ref[...] = jnp.concatenate(rank_rows, axis=0).astype(jnp.int32)
    carry_ref[...] = carry_ref[...] + jnp.sum(onehot, axis=1, keepdims=True)
    cnt_ref[...] = carry_ref[...]


def _route_call(logits_t, router_bias, upper, tile):
    n = logits_t.shape[1]
    tok = lambda r: pl.BlockSpec((r, tile), lambda i: (0, i))
    return pl.pallas_call(
        _route_kernel,
        grid=(n // tile,),
        in_specs=[tok(LANES), pl.BlockSpec((N_EXPERTS, 1), lambda i: (0, 0)),
                  pl.BlockSpec(upper.shape, lambda i: (0, 0))],
        out_specs=[tok(TOP_K), tok(TOP_K), tok(TOP_K), pl.BlockSpec((N_EXPERTS, LANES), lambda i: (0, 0))],
        out_shape=[jax.ShapeDtypeStruct((TOP_K, n), jnp.int32), jax.ShapeDtypeStruct((TOP_K, n), F32),
                   jax.ShapeDtypeStruct((TOP_K, n), jnp.int32),
                   jax.ShapeDtypeStruct((N_EXPERTS, LANES), F32)],
        scratch_shapes=[pltpu.VMEM((N_EXPERTS, LANES), F32)],
        compiler_params=pltpu.CompilerParams(dimension_semantics=("arbitrary",),
                                             vmem_limit_bytes=VMEM_LIMIT),
    )(logits_t, router_bias, upper)


def _dispatch_kernel(dest_ref, h_ref, init_ref, xs_ref, sem):
    del init_ref
    t = h_ref.shape[0]

    def issue(tok, carry):
        for k in range(TOP_K):
            pltpu.make_async_copy(h_ref.at[pl.ds(tok, 1)], xs_ref.at[pl.ds(dest_ref[k, tok], 1)], sem).start()
        return carry

    lax.fori_loop(0, t, issue, 0)
    for k in range(TOP_K):
        pltpu.make_async_copy(h_ref, xs_ref.at[pl.ds(0, t)], sem).wait()


def _dispatch_call(dest, h2, xs_init, tile):
    n, d = h2.shape
    return pl.pallas_call(
        _dispatch_kernel,
        grid=(n // tile,),
        in_specs=[pl.BlockSpec((TOP_K, tile), lambda i: (0, i), memory_space=pltpu.SMEM),
                  pl.BlockSpec((tile, d), lambda i: (i, 0)),
                  pl.BlockSpec(memory_space=pl.ANY)],
        out_specs=pl.BlockSpec(memory_space=pl.ANY),
        out_shape=jax.ShapeDtypeStruct(xs_init.shape, xs_init.dtype),
        scratch_shapes=[pltpu.SemaphoreType.DMA(())],
        input_output_aliases={2: 0},
        compiler_params=pltpu.CompilerParams(dimension_semantics=("arbitrary",),
                                             vmem_limit_bytes=VMEM_LIMIT),
    )(dest, h2, xs_init)


def _expert_kernel(be_ref, nused_ref, xs_ref, wg_ref, wu_ref, wd_ref, y_ref):
    i = pl.program_id(0)

    @pl.when(i < nused_ref[0])
    def _():
        xb = xs_ref[...].astype(BF16)
        gate = jnp.dot(xb, wg_ref[0].astype(BF16), preferred_element_type=F32)
        up = jnp.dot(xb, wu_ref[0].astype(BF16), preferred_element_type=F32)
        u = _silu(gate) * up
        y_ref[...] = jnp.dot(u.astype(BF16), wd_ref[0].astype(BF16), preferred_element_type=F32)

    @pl.when(i >= nused_ref[0])
    def _():
        y_ref[...] = jnp.zeros_like(y_ref)


def _expert_call(block_expert, nused, xs, wg, wu, wd, rows):
    p, d = xs.shape
    f = wg.shape[2]
    grid_spec = pltpu.PrefetchScalarGridSpec(
        num_scalar_prefetch=2,
        grid=(p // rows,),
        in_specs=[pl.BlockSpec((rows, d), lambda i, be, nu: (i, 0)),
                  pl.BlockSpec((1, d, f), lambda i, be, nu: (be[i], 0, 0)),
                  pl.BlockSpec((1, d, f), lambda i, be, nu: (be[i], 0, 0)),
                  pl.BlockSpec((1, f, d), lambda i, be, nu: (be[i], 0, 0))],
        out_specs=pl.BlockSpec((rows, d), lambda i, be, nu: (i, 0)),
    )
    return pl.pallas_call(
        _expert_kernel,
        grid_spec=grid_spec,
        out_shape=jax.ShapeDtypeStruct((p, d), F32),
        compiler_params=pltpu.CompilerParams(dimension_semantics=("arbitrary",),
                                             vmem_limit_bytes=VMEM_LIMIT),
    )(block_expert, nused, xs, wg, wu, wd)


def _combine_kernel(dest_ref, ys_ref, gate_ref, base_ref, g2_ref, fg_ref, o_ref, buf_ref, sem):
    t = base_ref.shape[0]

    def issue(tok, carry):
        for k in range(TOP_K):
            pltpu.make_async_copy(ys_ref.at[pl.ds(dest_ref[k, tok], 1)], buf_ref.at[k, pl.ds(tok, 1)], sem).start()
        return carry

    lax.fori_loop(0, t, issue, 0)
    for k in range(TOP_K):
        pltpu.make_async_copy(ys_ref.at[pl.ds(0, t)], buf_ref.at[k], sem).wait()

    gates = gate_ref[...]
    routed = jnp.zeros(base_ref.shape, F32)
    for k in range(TOP_K):
        routed = routed + gates[:, k:k + 1] * buf_ref[k]
    xo = base_ref[...] + g2_ref[0] * routed
    ms = jnp.mean(xo * xo, axis=-1, keepdims=True)
    o_ref[...] = xo * lax.rsqrt(ms + EPS) * fg_ref[...]


def _combine_call(dest, ys, gates_t, base, g2, fg, tiles_per_batch, tile):
    n, d = base.shape
    row = lambda w: pl.BlockSpec((tile, w), lambda i: (i, 0))
    return pl.pallas_call(
        _combine_kernel,
        grid=(n // tile,),
        in_specs=[pl.BlockSpec((TOP_K, tile), lambda i: (0, i), memory_space=pltpu.SMEM),
                  pl.BlockSpec(memory_space=pl.ANY),
                  row(TOP_K), row(d),
                  pl.BlockSpec((1, 1, d), lambda i: (i // tiles_per_batch, 0, 0)),
                  pl.BlockSpec((1, d), lambda i: (0, 0))],
        out_specs=row(d),
        out_shape=jax.ShapeDtypeStruct((n, d), F32),
        scratch_shapes=[pltpu.VMEM((TOP_K, tile, d), F32), pltpu.SemaphoreType.DMA(())],
        compiler_params=pltpu.CompilerParams(dimension_semantics=("arbitrary",),
                                             vmem_limit_bytes=VMEM_LIMIT),
    )(dest, ys, gates_t, base, g2, fg)


def _pad_cols(a, width):
    return jnp.pad(a, ((0, 0), (0, width - a.shape[1])))


def _layer(x, c_pad, mod_w, mod_b, norm1_g, norm2_g, w_in, conv_w, conv_b, dt_bias, a_log, d_skip, ssm_norm_g,
           att_norm_g, sinks, rel_bias, w_out, router_w, router_bias, exp_w_gate, exp_w_up, exp_w_down,
           sh_w_gate, sh_w_up, sh_w_down, final_g):
    bsz, l, d = x.shape
    n = bsz * l
    tm = min(ROW_TILE, l)

    mod = _mod_call(c_pad, mod_w, mod_b[None, :])[:bsz]
    sh1, sc1, g1, sh2, sc2, g2 = [m[:, None, :] for m in jnp.split(mod, 6, axis=-1)]

    i1 = D_SSM
    i2 = i1 + CONV_CH
    i3 = i2 + SSM_HEADS
    i4 = i3 + D_ATT
    i5 = i4 + D_KV
    wz, wx, wdt, wq, wk, wv = jnp.split(w_in, [i1, i2, i3, i4, i5], axis=-1)
    wdt = _pad_cols(wdt, LANES)
    x2 = x.reshape(n, d)
    z, xbc, dt, q, k, v = _in_proj_call(x2, sc1, sh1, norm1_g[None, :], wz.astype(BF16), wx.astype(BF16),
                                        wdt.astype(BF16), wq.astype(BF16), wk.astype(BF16), wv.astype(BF16),
                                        l // tm, tm)

    tril = jnp.asarray(np.tril(np.ones((CHUNK, CHUNK), np.float32)))
    y_ssm = _ssd_call(xbc.reshape(bsz, l, CONV_CH), z.reshape(bsz, l, D_SSM), dt.reshape(bsz, l, LANES),
                      conv_w, conv_b[None, :], _pad_cols(dt_bias[None, :], LANES), _pad_cols(a_log[None, :], LANES),
                      jnp.repeat(d_skip, SSM_HEAD_DIM)[None, :], ssm_norm_g[None, :], tril)

    bucket, win = _rel_bucket_table()
    bias = _bias_call(rel_bias, jnp.asarray(bucket), jnp.asarray(win))
    y_att = _attn_call(sinks, q.reshape(bsz, l, D_ATT), k.reshape(bsz, l, D_KV), v.reshape(bsz, l, D_KV), bias,
                       att_norm_g[None, :])

    rw = _pad_cols(router_w, LANES)
    rwh = rw.astype(BF16)
    rwl = (rw - rwh.astype(F32)).astype(BF16)
    base, h2, logits_t = _out_proj_call(
        x2, y_ssm.reshape(n, D_SSM), y_att.reshape(n, D_ATT), g1, sc2, sh2, g2, norm2_g[None, :],
        w_out[:D_SSM].astype(BF16), w_out[D_SSM:].astype(BF16), rwh, rwl,
        sh_w_gate.astype(BF16), sh_w_up.astype(BF16), sh_w_down.astype(BF16), l // tm, tm)

    rt = min(ROUTE_TILE, n)
    upper = jnp.asarray(np.triu(np.ones((rt, rt), np.float32), 1)).astype(BF16)
    idx, gates, rank, counts = _route_call(logits_t, router_bias[:, None], upper, rt)

    nblocks = (n * TOP_K + N_EXPERTS * (EXPERT_ROWS - 1) + EXPERT_ROWS - 1) // EXPERT_ROWS
    cnt = counts[:, 0].astype(jnp.int32)
    padded = (cnt + EXPERT_ROWS - 1) // EXPERT_ROWS * EXPERT_ROWS
    pad_end = jnp.cumsum(padded)
    pad_start = pad_end - padded
    dest = pad_start[idx] + rank
    block_expert = jnp.minimum(jnp.searchsorted(pad_end, jnp.arange(nblocks, dtype=jnp.int32) * EXPERT_ROWS,
                                                side='right'), N_EXPERTS - 1).astype(jnp.int32)
    nused = (pad_end[-1:] // EXPERT_ROWS).astype(jnp.int32)

    dtile = min(DISPATCH_TILE, n)
    xs = _dispatch_call(dest, h2, jnp.zeros((nblocks * EXPERT_ROWS, d), F32), dtile)
    ys = _expert_call(block_expert, nused, xs, exp_w_gate, exp_w_up, exp_w_down, EXPERT_ROWS)
    ctile = min(COMBINE_TILE, l)
    out = _combine_call(dest, ys, gates.T, base, g2, final_g[None, :], l // ctile, ctile)
    return out.reshape(bsz, l, d)


def kernel(x, c, mod_w, mod_b, norm1_g, norm2_g, w_in, conv_w, conv_b, dt_bias, a_log, d_skip, ssm_norm_g,
           att_norm_g, sinks, rel_bias, w_out, router_w, router_bias, exp_w_gate, exp_w_up, exp_w_down,
           sh_w_gate, sh_w_up, sh_w_down, final_g):
    assert mod_w.shape[0] == 1, "single-layer block"
    bsz = x.shape[0]
    c_pad = jnp.pad(c, ((0, SUBLANES - bsz % SUBLANES if bsz % SUBLANES else 0), (0, 0)))
    return _layer(x, c_pad, mod_w[0], mod_b[0], norm1_g[0], norm2_g[0], w_in[0], conv_w[0], conv_b[0],
                  dt_bias[0], a_log[0], d_skip[0], ssm_norm_g[0], att_norm_g[0], sinks[0], rel_bias, w_out[0],
                  router_w[0], router_bias[0], exp_w_gate[0], exp_w_up[0], exp_w_down[0], sh_w_gate[0],
                  sh_w_up[0], sh_w_down[0], final_g)
```

```python
import functools
import math

import numpy as np
import jax
import jax.numpy as jnp
from jax import lax
from jax.experimental import pallas as pl
from jax.experimental.pallas import tpu as pltpu

F32 = jnp.float32
BF16 = jnp.bfloat16

D_MODEL = 1024
SSM_HEAD_DIM = 64
D_SSM = D_MODEL
SSM_HEADS = D_SSM // SSM_HEAD_DIM
SSM_GROUPS = 4
D_STATE = 128
CONV_K = 4
CONV_CH = D_SSM + 2 * SSM_GROUPS * D_STATE
CHUNK = 128
ATT_HEAD_DIM = 64
D_ATT = D_MODEL
ATT_HEADS = D_ATT // ATT_HEAD_DIM
KV_HEADS = ATT_HEADS // 4
Q_PER_KV = ATT_HEADS // KV_HEADS
D_KV = KV_HEADS * ATT_HEAD_DIM
WINDOW = 128
ATT_BLOCK = 128
REL_BUCKETS = 32
REL_MAX_DIST = 128
N_EXPERTS = 64
TOP_K = 8
EXPERT_DIM = D_MODEL // 4
SHARED_DIM = D_MODEL // 4
ROUTE_GROUPS = 8
ROUTE_TOPK_GROUPS = 4
ROUTED_SCALE = 2.5
EPS = 1e-6

LANES = 128
SUBLANES = 8
HALF = LANES // 2

ROW_TILE = 512
ROUTE_TILE = 512
DISPATCH_TILE = 256
COMBINE_TILE = 128
EXPERT_ROWS = 256
VMEM_LIMIT = 48 * 1024 * 1024

NEG_INF = float("-inf")


def _silu(v):
    return v * (1.0 / (1.0 + jnp.exp(-v)))


def _softplus(v):
    return jnp.maximum(v, 0.0) + jnp.log(1.0 + jnp.exp(-jnp.abs(v)))


def _bdot(a, b):
    return jnp.dot(a.astype(BF16), b.astype(BF16), preferred_element_type=F32)


def _split_hi_lo(v):
    hi = v.astype(BF16)
    lo = (v - hi.astype(F32)).astype(BF16)
    return hi, lo


def _pack_bf16_pair(a, b):
    ar = lax.bitcast_convert_type(a.astype(BF16).astype(F32), jnp.uint32)
    br = lax.bitcast_convert_type(b.astype(BF16).astype(F32), jnp.uint32)
    return lax.shift_right_logical(ar, jnp.uint32(16)) | br


def _unpack_bf16_pair(w):
    a = lax.bitcast_convert_type(lax.shift_left(w, jnp.uint32(16)), F32)
    b = lax.bitcast_convert_type(w & jnp.uint32(0xFFFF0000), F32)
    return a, b


def _lane_half_mask(shape):
    return lax.broadcasted_iota(jnp.int32, shape, len(shape) - 1) < HALF


def _mod_kernel(c_ref, w_ref, b_ref, o_ref):
    a = _silu(c_ref[...])
    o_ref[...] = jnp.dot(a, w_ref[...], precision=lax.Precision.HIGHEST,
                         preferred_element_type=F32) + b_ref[...]


def _mod_call(c_pad, mod_w, mod_b):
    rows, d = c_pad.shape
    cols = mod_w.shape[1]
    return pl.pallas_call(
        _mod_kernel,
        grid=(cols // d,),
        in_specs=[pl.BlockSpec((rows, d), lambda j: (0, 0)),
                  pl.BlockSpec((d, d), lambda j: (0, j)),
                  pl.BlockSpec((1, d), lambda j: (0, j))],
        out_specs=pl.BlockSpec((rows, d), lambda j: (0, j)),
        out_shape=jax.ShapeDtypeStruct((rows, cols), F32),
        compiler_params=pltpu.CompilerParams(dimension_semantics=("arbitrary",),
                                             vmem_limit_bytes=VMEM_LIMIT),
    )(c_pad, mod_w, mod_b)


def _in_proj_kernel(x_ref, sc_ref, sh_ref, g_ref, wz_ref, wx_ref, wdt_ref, wq_ref, wk_ref, wv_ref,
                    z_ref, xbc_ref, dt_ref, q_ref, k_ref, v_ref):
    xf = x_ref[...]
    ms = jnp.mean(xf * xf, axis=-1, keepdims=True)
    h = xf * lax.rsqrt(ms + EPS) * g_ref[...]
    h = h * (1.0 + sc_ref[0]) + sh_ref[0]
    hb = h.astype(BF16)
    z_ref[...] = jnp.dot(hb, wz_ref[...], preferred_element_type=F32).astype(BF16)
    xbc_ref[...] = jnp.dot(hb, wx_ref[...], preferred_element_type=F32).astype(BF16)
    dt_ref[...] = jnp.dot(hb, wdt_ref[...], preferred_element_type=F32)
    q_ref[...] = jnp.dot(hb, wq_ref[...], preferred_element_type=F32).astype(BF16)
    k_ref[...] = jnp.dot(hb, wk_ref[...], preferred_element_type=F32).astype(BF16)
    v_ref[...] = jnp.dot(hb, wv_ref[...], preferred_element_type=F32).astype(BF16)


def _in_proj_call(x2, sc1, sh1, g1n, wz, wx, wdt, wq, wk, wv, tiles_per_batch, tm):
    n, d = x2.shape
    row = lambda w: pl.BlockSpec((tm, w), lambda i: (i, 0))
    full = lambda a: pl.BlockSpec(a.shape, lambda i: (0, 0))
    per_batch = pl.BlockSpec((1, 1, d), lambda i: (i // tiles_per_batch, 0, 0))
    outs = [(wz.shape[1], BF16), (wx.shape[1], BF16), (wdt.shape[1], F32),
            (wq.shape[1], BF16), (wk.shape[1], BF16), (wv.shape[1], BF16)]
    return pl.pallas_call(
        _in_proj_kernel,
        grid=(n // tm,),
        in_specs=[row(d), per_batch, per_batch, full(g1n), full(wz), full(wx), full(wdt), full(wq),
                  full(wk), full(wv)],
        out_specs=[row(w) for w, _ in outs],
        out_shape=[jax.ShapeDtypeStruct((n, w), dt) for w, dt in outs],
        compiler_params=pltpu.CompilerParams(dimension_semantics=("arbitrary",),
                                             vmem_limit_bytes=VMEM_LIMIT),
    )(x2, sc1, sh1, g1n, wz, wx, wdt, wq, wk, wv)


def _ssd_kernel(xbc_ref, z_ref, dt_ref, cw_ref, cb_ref, dtb_ref, alog_ref, dskip_ref, ng_ref, tril_ref,
                y_ref, state_ref, tail_ref, ext_ref, ybuf_ref):
    c = pl.program_id(1)

    @pl.when(c == 0)
    def _():
        state_ref[...] = jnp.zeros_like(state_ref)
        tail_ref[...] = jnp.zeros_like(tail_ref)

    u = xbc_ref[0].astype(F32)
    ext_ref[0:SUBLANES, :] = tail_ref[...]
    ext_ref[SUBLANES:SUBLANES + CHUNK, :] = u
    tail_ref[...] = u[CHUNK - SUBLANES:, :]
    acc = cb_ref[...] + jnp.zeros_like(u)
    for kk in range(CONV_K):
        off = SUBLANES - (CONV_K - 1) + kk
        acc = acc + cw_ref[kk:kk + 1, :] * ext_ref[off:off + CHUNK, :]
    act = _silu(acc)
    xs = act[:, :D_SSM]
    gn = SSM_GROUPS * D_STATE

    dt = _softplus(dt_ref[0] + dtb_ref[...])
    a = dt * (-jnp.exp(alog_ref[...]))
    cs = jnp.dot(tril_ref[...], a, precision=lax.Precision.HIGHEST, preferred_element_type=F32)
    cs_t = cs.T
    dt_t = dt.T
    exp_cs = jnp.exp(cs)
    cs_last = cs[CHUNK - 1:CHUNK, :]
    chunk_decay = jnp.exp(cs_last)

    li = lax.broadcasted_iota(jnp.int32, (CHUNK, CHUNK), 0)
    si = lax.broadcasted_iota(jnp.int32, (CHUNK, CHUNK), 1)
    causal = li >= si
    low = _lane_half_mask((CHUNK, LANES))
    low_row = _lane_half_mask((1, LANES))

    heads_per_group = SSM_HEADS // SSM_GROUPS
    for g in range(SSM_GROUPS):
        b_g = act[:, D_SSM + g * D_STATE:D_SSM + (g + 1) * D_STATE]
        c_g = act[:, D_SSM + gn + g * D_STATE:D_SSM + gn + (g + 1) * D_STATE]
        b_gb = b_g.astype(BF16)
        c_gb = c_g.astype(BF16)
        cb = lax.dot_general(c_gb, b_gb, (((1,), (1,)), ((), ())), preferred_element_type=F32)
        b_t = b_g.T
        for jp in range(heads_per_group // 2):
            j = g * (heads_per_group // 2) + jp
            lanes = slice(j * LANES, (j + 1) * LANES)
            xp = xs[:, lanes]
            ydiag = jnp.zeros((CHUNK, LANES), F32)
            snew = jnp.zeros((D_STATE, LANES), F32)
            for half in range(2):
                h = 2 * j + half
                row_dt = dt_t[h:h + 1, :]
                diff = cs[:, h:h + 1] - cs_t[h:h + 1, :]
                lmat = jnp.exp(jnp.where(causal, diff, NEG_INF))
                m = (cb * lmat * row_dt).astype(BF16)
                keep = low if half == 0 else jnp.logical_not(low)
                xh = jnp.where(keep, xp, 0.0).astype(BF16)
                ydiag = ydiag + jnp.dot(m, xh, preferred_element_type=F32)
                w_t = jnp.exp(cs_t[h:h + 1, CHUNK - 1:CHUNK] - cs_t[h:h + 1, :]) * row_dt
                snew = snew + jnp.dot((b_t * w_t).astype(BF16), xh, preferred_element_type=F32)
            s_in = state_ref[:, lanes]
            yoff = jnp.dot(c_gb, s_in.astype(BF16), preferred_element_type=F32)
            h0 = 2 * j
            escale = jnp.where(low, exp_cs[:, h0:h0 + 1], exp_cs[:, h0 + 1:h0 + 2])
            cdec = jnp.where(low_row, chunk_decay[:, h0:h0 + 1], chunk_decay[:, h0 + 1:h0 + 2])
            ybuf_ref[:, lanes] = ydiag + yoff * escale + xp * dskip_ref[:, lanes]
            state_ref[:, lanes] = s_in * cdec + snew

    yz = ybuf_ref[...] * _silu(z_ref[0].astype(F32))
    gw = D_SSM // SSM_GROUPS
    for g in range(SSM_GROUPS):
        part = yz[:, g * gw:(g + 1) * gw]
        ms = jnp.mean(part * part, axis=-1, keepdims=True)
        y_ref[0, :, g * gw:(g + 1) * gw] = (part * lax.rsqrt(ms + EPS)
                                            * ng_ref[:, g * gw:(g + 1) * gw]).astype(BF16)


def _ssd_call(xbc, z, dt, conv_w, conv_b, dtb, alog, dskip, ng, tril):
    bsz, l, _ = xbc.shape
    nc = l // CHUNK
    chunk = lambda w: pl.BlockSpec((1, CHUNK, w), lambda b, c: (b, c, 0))
    full = lambda a: pl.BlockSpec(a.shape, lambda b, c: (0, 0))
    return pl.pallas_call(
        _ssd_kernel,
        grid=(bsz, nc),
        in_specs=[chunk(CONV_CH), chunk(D_SSM), chunk(LANES), full(conv_w), full(conv_b), full(dtb),
                  full(alog), full(dskip), full(ng), full(tril)],
        out_specs=chunk(D_SSM),
        out_shape=jax.ShapeDtypeStruct((bsz, l, D_SSM), BF16),
        scratch_shapes=[pltpu.VMEM((D_STATE, D_SSM), F32),
                        pltpu.VMEM((SUBLANES, CONV_CH), F32),
                        pltpu.VMEM((SUBLANES + CHUNK, CONV_CH), F32),
                        pltpu.VMEM((CHUNK, D_SSM), F32)],
        compiler_params=pltpu.CompilerParams(dimension_semantics=("arbitrary", "arbitrary"),
                                             vmem_limit_bytes=VMEM_LIMIT),
    )(xbc, z, dt, conv_w, conv_b, dtb, alog, dskip, ng, tril)


def _rel_bucket_table():
    qi = np.arange(ATT_BLOCK)[:, None]
    sj = np.arange(2 * ATT_BLOCK)[None, :]
    dist = qi + ATT_BLOCK - sj
    in_win = (dist >= 0) & (dist < WINDOW)
    dcl = np.maximum(dist, 0)
    max_exact = REL_BUCKETS // 2
    d = np.maximum(dcl, 1).astype(np.float32)
    large = max_exact + (np.log(d / np.float32(max_exact)) / np.float32(math.log(REL_MAX_DIST / max_exact))
                         * np.float32(REL_BUCKETS - max_exact)).astype(np.int32)
    large = np.minimum(large, REL_BUCKETS - 1)
    bucket = np.where(dcl < max_exact, dcl, large).astype(np.int32)
    return bucket, in_win.astype(np.int32)


def _bias_kernel(rb_ref, bucket_ref, win_ref, o_ref):
    h = pl.program_id(0)
    bucket = bucket_ref[...]
    acc = jnp.zeros(bucket.shape, F32)
    for b in range(REL_BUCKETS):
        acc = jnp.where(bucket == b, rb_ref[b, h], acc)
    o_ref[0] = jnp.where(win_ref[...] > 0, acc, NEG_INF)


def _bias_call(rel_bias, bucket, win):
    return pl.pallas_call(
        _bias_kernel,
        grid=(ATT_HEADS,),
        in_specs=[pl.BlockSpec(memory_space=pltpu.SMEM),
                  pl.BlockSpec(bucket.shape, lambda h: (0, 0)),
                  pl.BlockSpec(win.shape, lambda h: (0, 0))],
        out_specs=pl.BlockSpec((1,) + bucket.shape, lambda h: (h, 0, 0)),
        out_shape=jax.ShapeDtypeStruct((ATT_HEADS,) + bucket.shape, F32),
        compiler_params=pltpu.CompilerParams(dimension_semantics=("arbitrary",)),
    )(rel_bias, bucket, win)


def _attn_kernel(sink_ref, q_ref, kp_ref, kc_ref, vp_ref, vc_ref, bias_ref, ng_ref, o_ref, obuf_ref):
    i = pl.program_id(1)
    kband = jnp.concatenate([kp_ref[0], kc_ref[0]], axis=0).astype(F32)
    vband = jnp.concatenate([vp_ref[0], vc_ref[0]], axis=0).astype(F32)
    sj = lax.broadcasted_iota(jnp.int32, (ATT_BLOCK, 2 * ATT_BLOCK), 1)
    valid = jnp.logical_or(sj >= ATT_BLOCK, i > 0)
    low = _lane_half_mask((ATT_BLOCK, LANES))
    scale = ATT_HEAD_DIM ** -0.5

    kv_cols = []
    for cpair in range(KV_HEADS // 2):
        kk = kband[:, cpair * LANES:(cpair + 1) * LANES]
        vv = vband[:, cpair * LANES:(cpair + 1) * LANES]
        kv_cols.append(((kk.astype(BF16), pltpu.roll(kk, HALF, 1).astype(BF16)),
                        (vv.astype(BF16), pltpu.roll(vv, HALF, 1).astype(BF16))))

    for j in range(ATT_HEADS // 2):
        qp = q_ref[0, :, j * LANES:(j + 1) * LANES]
        out_pair = jnp.zeros((ATT_BLOCK, LANES), F32)
        for half in range(2):
            h = 2 * j + half
            g = h // Q_PER_KV
            swapped = int((g % 2) != half)
            ksel = kv_cols[g // 2][0][swapped]
            vsel = kv_cols[g // 2][1][swapped]
            keep = low if half == 0 else jnp.logical_not(low)
            qh = jnp.where(keep, qp, jnp.zeros_like(qp))
            s = lax.dot_general(qh, ksel, (((1,), (1,)), ((), ())), preferred_element_type=F32)
            s = s * scale + bias_ref[h]
            s = jnp.where(valid, s, NEG_INF)
            sink = sink_ref[h]
            m = jnp.maximum(jnp.max(s, axis=-1, keepdims=True), sink)
            p = jnp.exp(s - m)
            denom = jnp.sum(p, axis=-1, keepdims=True) + jnp.exp(sink - m)
            o = jnp.dot(p.astype(BF16), vsel, preferred_element_type=F32) / denom
            out_pair = out_pair + jnp.where(keep, o, 0.0)
        obuf_ref[:, j * LANES:(j + 1) * LANES] = out_pair

    att = obuf_ref[...]
    ms = jnp.mean(att * att, axis=-1, keepdims=True)
    o_ref[0] = (att * lax.rsqrt(ms + EPS) * ng_ref[...]).astype(BF16)


def _attn_call(sinks, q, k, v, bias, ng):
    bsz, l, _ = q.shape
    nb = l // ATT_BLOCK
    cur = lambda w: pl.BlockSpec((1, ATT_BLOCK, w), lambda b, i: (b, i, 0))
    prev = lambda w: pl.BlockSpec((1, ATT_BLOCK, w), lambda b, i: (b, jnp.maximum(i - 1, 0), 0))
    return pl.pallas_call(
        _attn_kernel,
        grid=(bsz, nb),
        in_specs=[pl.BlockSpec(memory_space=pltpu.SMEM),
                  cur(D_ATT), prev(D_KV), cur(D_KV), prev(D_KV), cur(D_KV),
                  pl.BlockSpec(bias.shape, lambda b, i: (0, 0, 0)),
                  pl.BlockSpec(ng.shape, lambda b, i: (0, 0))],
        out_specs=cur(D_ATT),
        out_shape=jax.ShapeDtypeStruct((bsz, l, D_ATT), BF16),
        scratch_shapes=[pltpu.VMEM((ATT_BLOCK, D_ATT), F32)],
        compiler_params=pltpu.CompilerParams(dimension_semantics=("arbitrary", "arbitrary"),
                                             vmem_limit_bytes=VMEM_LIMIT),
    )(sinks, q, k, k, v, v, bias, ng)


def _out_proj_kernel(x_ref, ys_ref, ya_ref, g1_ref, sc_ref, sh_ref, g2_ref, ng_ref, wos_ref, woa_ref,
                     rwh_ref, rwl_ref, sg_ref, su_ref, sd_ref, base_ref, h_ref, lg_ref):
    mix = (jnp.dot(ys_ref[...], wos_ref[...], preferred_element_type=F32)
           + jnp.dot(ya_ref[...], woa_ref[...], preferred_element_type=F32))
    x1 = x_ref[...] + g1_ref[0] * mix
    ms = jnp.mean(x1 * x1, axis=-1, keepdims=True)
    h = x1 * lax.rsqrt(ms + EPS) * ng_ref[...]
    h = h * (1.0 + sc_ref[0]) + sh_ref[0]
    half = h.shape[1] // 2
    h_ref[...] = _pack_bf16_pair(h[:, :half], h[:, half:])
    hi, lo = _split_hi_lo(h)
    logits = (jnp.dot(hi, rwh_ref[...], preferred_element_type=F32)
              + jnp.dot(lo, rwh_ref[...], preferred_element_type=F32)
              + jnp.dot(hi, rwl_ref[...], preferred_element_type=F32))
    lg_ref[...] = logits.T
    u = _silu(jnp.dot(hi, sg_ref[...], preferred_element_type=F32)) * jnp.dot(hi, su_ref[...],
                                                                              preferred_element_type=F32)
    shared = jnp.dot(u.astype(BF16), sd_ref[...], preferred_element_type=F32)
    base_ref[...] = x1 + g2_ref[0] * shared


def _out_proj_call(x2, ys, ya, g1, sc2, sh2, g2, ng, wos, woa, rwh, rwl, sg, su, sd, tiles_per_batch, tm):
    n, d = x2.shape
    row = lambda w: pl.BlockSpec((tm, w), lambda i: (i, 0))
    full = lambda a: pl.BlockSpec(a.shape, lambda i: (0, 0))
    per_batch = pl.BlockSpec((1, 1, d), lambda i: (i // tiles_per_batch, 0, 0))
    return pl.pallas_call(
        _out_proj_kernel,
        grid=(n // tm,),
        in_specs=[row(d), row(D_SSM), row(D_ATT), per_batch, per_batch, per_batch, per_batch, full(ng),
                  full(wos), full(woa), full(rwh), full(rwl), full(sg), full(su), full(sd)],
        out_specs=[row(d), row(d // 2), pl.BlockSpec((LANES, tm), lambda i: (0, i))],
        out_shape=[jax.ShapeDtypeStruct((n, d), F32), jax.ShapeDtypeStruct((n, d // 2), jnp.uint32),
                   jax.ShapeDtypeStruct((LANES, n), F32)],
        compiler_params=pltpu.CompilerParams(dimension_semantics=("arbitrary",),
                                             vmem_limit_bytes=VMEM_LIMIT),
    )(x2, ys, ya, g1, sc2, sh2, g2, ng, wos, woa, rwh, rwl, sg, su, sd)


def _route_kernel(lg_ref, rb_ref, upper_ref, idx_ref, gate_ref, rank_ref, cnt_ref, carry_ref):
    step = pl.program_id(0)

    @pl.when(step == 0)
    def _():
        carry_ref[...] = jnp.zeros_like(carry_ref)

    t = lg_ref.shape[1]
    per_group = N_EXPERTS // ROUTE_GROUPS
    scores = 1.0 / (1.0 + jnp.exp(-lg_ref[0:N_EXPERTS, :]))
    sel = scores + rb_ref[...]
    e_iota = lax.broadcasted_iota(jnp.int32, (N_EXPERTS, t), 0)

    sel3 = sel.reshape(ROUTE_GROUPS, per_group, t)
    w_iota = lax.broadcasted_iota(jnp.int32, sel3.shape, 1)
    m1 = jnp.max(sel3, axis=1, keepdims=True)
    first = jnp.min(jnp.where(sel3 == m1, w_iota, per_group), axis=1, keepdims=True)
    m2 = jnp.max(jnp.where(w_iota == first, NEG_INF, sel3), axis=1, keepdims=True)
    grp = (m1 + m2).reshape(ROUTE_GROUPS, t)

    g_iota = lax.broadcasted_iota(jnp.int32, (ROUTE_GROUPS, t), 0)
    gmask = jnp.zeros((ROUTE_GROUPS, t), jnp.bool_)
    for _ in range(ROUTE_TOPK_GROUPS):
        gm = jnp.max(grp, axis=0, keepdims=True)
        gfirst = jnp.min(jnp.where(grp == gm, g_iota, ROUTE_GROUPS), axis=0, keepdims=True)
        hit = g_iota == gfirst
        gmask = jnp.logical_or(gmask, hit)
        grp = jnp.where(hit, NEG_INF, grp)
    allowed = jnp.broadcast_to(gmask.reshape(ROUTE_GROUPS, 1, t),
                               (ROUTE_GROUPS, per_group, t)).reshape(N_EXPERTS, t)
    masked = jnp.where(allowed, sel, NEG_INF)

    picked = jnp.zeros((N_EXPERTS, t), jnp.bool_)
    idx_rows = []
    w_rows = []
    for _ in range(TOP_K):
        mm = jnp.max(masked, axis=0, keepdims=True)
        efirst = jnp.min(jnp.where(masked == mm, e_iota, N_EXPERTS), axis=0, keepdims=True)
        hit = e_iota == efirst
        idx_rows.append(efirst)
        w_rows.append(jnp.sum(jnp.where(hit, scores, 0.0), axis=0, keepdims=True))
        picked = jnp.logical_or(picked, hit)
        masked = jnp.where(hit, NEG_INF, masked)
    idx = jnp.concatenate(idx_rows, axis=0)
    w = jnp.concatenate(w_rows, axis=0)
    gate_ref[...] = w / jnp.sum(w, axis=0, keepdims=True) * ROUTED_SCALE
    idx_ref[...] = idx

    onehot = jnp.where(picked, 1.0, 0.0)
    before = jnp.dot(onehot.astype(BF16), upper_ref[...], preferred_element_type=F32)
    rank_full = before + carry_ref[:, 0:1]
    rank_rows = [jnp.sum(jnp.where(e_iota == idx_rows[k], rank_full, 0.0), axis=0, keepdims=True)
                 for k in range(TOP_K)]
    rank_ref[...] = jnp.concatenate(rank_rows, axis=0).astype(jnp.int32)
    carry_ref[...] = carry_ref[...] + jnp.sum(onehot, axis=1, keepdims=True)
    cnt_ref[...] = carry_ref[...]


def _route_call(logits_t, router_bias, upper, tile):
    n = logits_t.shape[1]
    tok = lambda r: pl.BlockSpec((r, tile), lambda i: (0, i))
    return pl.pallas_call(
        _route_kernel,
        grid=(n // tile,),
        in_specs=[tok(LANES), pl.BlockSpec((N_EXPERTS, 1), lambda i: (0, 0)),
                  pl.BlockSpec(upper.shape, lambda i: (0, 0))],
        out_specs=[tok(TOP_K), tok(TOP_K), tok(TOP_K), pl.BlockSpec((N_EXPERTS, LANES), lambda i: (0, 0))],
        out_shape=[jax.ShapeDtypeStruct((TOP_K, n), jnp.int32), jax.ShapeDtypeStruct((TOP_K, n), F32),
                   jax.ShapeDtypeStruct((TOP_K, n), jnp.int32),
                   jax.ShapeDtypeStruct((N_EXPERTS, LANES), F32)],
        scratch_shapes=[pltpu.VMEM((N_EXPERTS, LANES), F32)],
        compiler_params=pltpu.CompilerParams(dimension_semantics=("arbitrary",),
                                             vmem_limit_bytes=VMEM_LIMIT),
    )(logits_t, router_bias, upper)


def _dest_kernel(idx_ref, rank_ref, start_ref, dest_ref):
    t = idx_ref.shape[1]
    e_iota = lax.broadcasted_iota(jnp.int32, (N_EXPERTS, t), 0)
    rows = [jnp.sum(jnp.where(e_iota == idx_ref[k:k + 1, :], start_ref[...], 0), axis=0, keepdims=True)
            for k in range(TOP_K)]
    dest_ref[...] = jnp.concatenate(rows, axis=0) + rank_ref[...]


def _dest_call(idx, rank, pad_start, tile):
    n = idx.shape[1]
    tok = pl.BlockSpec((TOP_K, tile), lambda i: (0, i))
    return pl.pallas_call(
        _dest_kernel,
        grid=(n // tile,),
        in_specs=[tok, tok, pl.BlockSpec((N_EXPERTS, 1), lambda i: (0, 0))],
        out_specs=tok,
        out_shape=jax.ShapeDtypeStruct((TOP_K, n), jnp.int32),
        compiler_params=pltpu.CompilerParams(dimension_semantics=("arbitrary",)),
    )(idx, rank, pad_start)


def _dispatch_kernel(fill_start_ref, fill_on_ref, dest_ref, h_ref, xs_ref, zero_ref, sem, fill_sem):
    t = h_ref.shape[0]

    @pl.when(pl.program_id(0) == 0)
    def _():
        zero_ref[...] = jnp.zeros_like(zero_ref)

        def fill_copy(e):
            first = pl.multiple_of(fill_start_ref[e], EXPERT_ROWS)
            return pltpu.make_async_copy(zero_ref, xs_ref.at[pl.ds(first, EXPERT_ROWS)], fill_sem)

        def start(e, carry):
            @pl.when(fill_on_ref[e] > 0)
            def _():
                fill_copy(e).start()
            return carry

        def wait(e, carry):
            @pl.when(fill_on_ref[e] > 0)
            def _():
                fill_copy(e).wait()
            return carry

        lax.fori_loop(0, fill_on_ref.shape[0], start, 0)
        lax.fori_loop(0, fill_on_ref.shape[0], wait, 0)

    def issue(tok, carry):
        for k in range(TOP_K):
            pltpu.make_async_copy(h_ref.at[pl.ds(tok, 1)], xs_ref.at[pl.ds(dest_ref[k, tok], 1)], sem).start()
        return carry

    lax.fori_loop(0, t, issue, 0)
    for k in range(TOP_K):
        pltpu.make_async_copy(h_ref, xs_ref.at[pl.ds(0, t)], sem).wait()


def _dispatch_call(fill_start, fill_on, dest, h2, total_rows, tile):
    n, w = h2.shape
    grid_spec = pltpu.PrefetchScalarGridSpec(
        num_scalar_prefetch=2,
        grid=(n // tile,),
        in_specs=[pl.BlockSpec((TOP_K, tile), lambda i, fs, fo: (0, i), memory_space=pltpu.SMEM),
                  pl.BlockSpec((tile, w), lambda i, fs, fo: (i, 0))],
        out_specs=pl.BlockSpec(memory_space=pl.ANY),
        scratch_shapes=[pltpu.VMEM((EXPERT_ROWS, w), h2.dtype), pltpu.SemaphoreType.DMA(()),
                        pltpu.SemaphoreType.DMA(())],
    )
    return pl.pallas_call(
        _dispatch_kernel,
        grid_spec=grid_spec,
        out_shape=jax.ShapeDtypeStruct((total_rows, w), h2.dtype),
        compiler_params=pltpu.CompilerParams(dimension_semantics=("arbitrary",),
                                             vmem_limit_bytes=VMEM_LIMIT),
    )(fill_start, fill_on, dest, h2)


def _expert_kernel(be_ref, nused_ref, fresh_ref, xs_ref, wg_ref, wu_ref, wd_ref, y_ref, wgb_ref, wub_ref,
                   wdb_ref):
    i = pl.program_id(0)

    @pl.when(fresh_ref[i] > 0)
    def _():
        wgb_ref[...] = wg_ref[0].astype(BF16)
        wub_ref[...] = wu_ref[0].astype(BF16)
        wdb_ref[...] = wd_ref[0].astype(BF16)

    @pl.when(i < nused_ref[0])
    def _():
        half = xs_ref.shape[1]
        x_lo, x_hi = _unpack_bf16_pair(xs_ref[...])
        x_lo = x_lo.astype(BF16)
        x_hi = x_hi.astype(BF16)
        gate = (jnp.dot(x_lo, wgb_ref[:half, :], preferred_element_type=F32)
                + jnp.dot(x_hi, wgb_ref[half:, :], preferred_element_type=F32))
        up = (jnp.dot(x_lo, wub_ref[:half, :], preferred_element_type=F32)
              + jnp.dot(x_hi, wub_ref[half:, :], preferred_element_type=F32))
        u = (_silu(gate) * up).astype(BF16)
        y_lo = jnp.dot(u, wdb_ref[:, :half], preferred_element_type=F32)
        y_hi = jnp.dot(u, wdb_ref[:, half:], preferred_element_type=F32)
        y_ref[...] = _pack_bf16_pair(y_lo, y_hi)

    @pl.when(i >= nused_ref[0])
    def _():
        y_ref[...] = jnp.zeros_like(y_ref)


def _expert_call(block_expert, nused, fresh, xs, wg, wu, wd, rows):
    p, w = xs.shape
    d, f = wg.shape[1], wg.shape[2]
    x_map = lambda i, be, nu, fr: (jnp.minimum(i, nu[0] - 1), 0)
    w_map = lambda i, be, nu, fr: (be[i], 0, 0)
    grid_spec = pltpu.PrefetchScalarGridSpec(
        num_scalar_prefetch=3,
        grid=(p // rows,),
        in_specs=[pl.BlockSpec((rows, w), x_map),
                  pl.BlockSpec((1, d, f), w_map), pl.BlockSpec((1, d, f), w_map), pl.BlockSpec((1, f, d), w_map)],
        out_specs=pl.BlockSpec((rows, w), lambda i, be, nu, fr: (i, 0)),
        scratch_shapes=[pltpu.VMEM((d, f), BF16), pltpu.VMEM((d, f), BF16), pltpu.VMEM((f, d), BF16)],
    )
    return pl.pallas_call(
        _expert_kernel,
        grid_spec=grid_spec,
        out_shape=jax.ShapeDtypeStruct((p, w), jnp.uint32),
        compiler_params=pltpu.CompilerParams(dimension_semantics=("arbitrary",),
                                             vmem_limit_bytes=VMEM_LIMIT),
    )(block_expert, nused, fresh, xs, wg, wu, wd)


def _combine_kernel(dest_ref, ys_ref, gate_ref, base_ref, g2_ref, fg_ref, o_ref, buf_ref, sem):
    t = base_ref.shape[0]
    half = ys_ref.shape[1]

    def issue(tok, carry):
        for k in range(TOP_K):
            pltpu.make_async_copy(ys_ref.at[pl.ds(dest_ref[k, tok], 1)], buf_ref.at[k, pl.ds(tok, 1)], sem).start()
        return carry

    lax.fori_loop(0, t, issue, 0)
    for k in range(TOP_K):
        pltpu.make_async_copy(ys_ref.at[pl.ds(0, t)], buf_ref.at[k], sem).wait()

    gates = gate_ref[...]
    r_lo = jnp.zeros((t, half), F32)
    r_hi = jnp.zeros((t, half), F32)
    for k in range(TOP_K):
        y_lo, y_hi = _unpack_bf16_pair(buf_ref[k])
        r_lo = r_lo + gates[:, k:k + 1] * y_lo
        r_hi = r_hi + gates[:, k:k + 1] * y_hi
    g2 = g2_ref[0]
    x_lo = base_ref[:, :half] + g2[:, :half] * r_lo
    x_hi = base_ref[:, half:] + g2[:, half:] * r_hi
    ms = (jnp.sum(x_lo * x_lo, axis=-1, keepdims=True)
          + jnp.sum(x_hi * x_hi, axis=-1, keepdims=True)) * (1.0 / (2 * half))
    inv = lax.rsqrt(ms + EPS)
    o_ref[:, :half] = x_lo * inv * fg_ref[:, :half]
    o_ref[:, half:] = x_hi * inv * fg_ref[:, half:]


def _combine_call(dest, ys, gates_t, base, g2, fg, tiles_per_batch, tile):
    n, d = base.shape
    row = lambda w: pl.BlockSpec((tile, w), lambda i: (i, 0))
    return pl.pallas_call(
        _combine_kernel,
        grid=(n // tile,),
        in_specs=[pl.BlockSpec((TOP_K, tile), lambda i: (0, i), memory_space=pltpu.SMEM),
                  pl.BlockSpec(memory_space=pl.ANY),
                  row(TOP_K), row(d),
                  pl.BlockSpec((1, 1, d), lambda i: (i // tiles_per_batch, 0, 0)),
                  pl.BlockSpec((1, d), lambda i: (0, 0))],
        out_specs=row(d),
        out_shape=jax.ShapeDtypeStruct((n, d), F32),
        scratch_shapes=[pltpu.VMEM((TOP_K, tile, ys.shape[1]), ys.dtype), pltpu.SemaphoreType.DMA(())],
        compiler_params=pltpu.CompilerParams(dimension_semantics=("arbitrary",),
                                             vmem_limit_bytes=VMEM_LIMIT),
    )(dest, ys, gates_t, base, g2, fg)


def _pad_cols(a, width):
    return jnp.pad(a, ((0, 0), (0, width - a.shape[1])))


def _layer(x, c_pad, mod_w, mod_b, norm1_g, norm2_g, w_in, conv_w, conv_b, dt_bias, a_log, d_skip, ssm_norm_g,
           att_norm_g, sinks, rel_bias, w_out, router_w, router_bias, exp_w_gate, exp_w_up, exp_w_down,
           sh_w_gate, sh_w_up, sh_w_down, final_g):
    bsz, l, d = x.shape
    n = bsz * l
    tm = min(ROW_TILE, l)

    mod = _mod_call(c_pad, mod_w, mod_b[None, :])[:bsz]
    sh1, sc1, g1, sh2, sc2, g2 = [m[:, None, :] for m in jnp.split(mod, 6, axis=-1)]

    i1 = D_SSM
    i2 = i1 + CONV_CH
    i3 = i2 + SSM_HEADS
    i4 = i3 + D_ATT
    i5 = i4 + D_KV
    wz, wx, wdt, wq, wk, wv = jnp.split(w_in, [i1, i2, i3, i4, i5], axis=-1)
    wdt = _pad_cols(wdt, LANES)
    x2 = x.reshape(n, d)
    z, xbc, dt, q, k, v = _in_proj_call(x2, sc1, sh1, norm1_g[None, :], wz.astype(BF16), wx.astype(BF16),
                                        wdt.astype(BF16), wq.astype(BF16), wk.astype(BF16), wv.astype(BF16),
                                        l // tm, tm)

    tril = jnp.asarray(np.tril(np.ones((CHUNK, CHUNK), np.float32)))
    y_ssm = _ssd_call(xbc.reshape(bsz, l, CONV_CH), z.reshape(bsz, l, D_SSM), dt.reshape(bsz, l, LANES),
                      conv_w, conv_b[None, :], _pad_cols(dt_bias[None, :], LANES), _pad_cols(a_log[None, :], LANES),
                      jnp.repeat(d_skip, SSM_HEAD_DIM)[None, :], ssm_norm_g[None, :], tril)

    bucket, win = _rel_bucket_table()
    bias = _bias_call(rel_bias, jnp.asarray(bucket), jnp.asarray(win))
    y_att = _attn_call(sinks, q.reshape(bsz, l, D_ATT), k.reshape(bsz, l, D_KV), v.reshape(bsz, l, D_KV), bias,
                       att_norm_g[None, :])

    rw = _pad_cols(router_w, LANES)
    rwh = rw.astype(BF16)
    rwl = (rw - rwh.astype(F32)).astype(BF16)
    base, h2, logits_t = _out_proj_call(
        x2, y_ssm.reshape(n, D_SSM), y_att.reshape(n, D_ATT), g1, sc2, sh2, g2, norm2_g[None, :],
        w_out[:D_SSM].astype(BF16), w_out[D_SSM:].astype(BF16), rwh, rwl,
        sh_w_gate.astype(BF16), sh_w_up.astype(BF16), sh_w_down.astype(BF16), l // tm, tm)

    rt = min(ROUTE_TILE, n)
    upper = jnp.asarray(np.triu(np.ones((rt, rt), np.float32), 1)).astype(BF16)
    idx, gates, rank, counts = _route_call(logits_t, router_bias[:, None], upper, rt)

    nblocks = (n * TOP_K + N_EXPERTS * (EXPERT_ROWS - 1) + EXPERT_ROWS - 1) // EXPERT_ROWS
    cnt = counts[:, 0].astype(jnp.int32)
    padded = (cnt + EXPERT_ROWS - 1) // EXPERT_ROWS * EXPERT_ROWS
    pad_end = jnp.cumsum(padded)
    pad_start = pad_end - padded
    dest = _dest_call(idx, rank, pad_start[:, None], rt)
    block_first_row = jnp.arange(nblocks, dtype=jnp.int32) * EXPERT_ROWS
    block_expert = jnp.minimum(jnp.sum((pad_end[None, :] <= block_first_row[:, None]).astype(jnp.int32), axis=1),
                               N_EXPERTS - 1)
    fresh = jnp.concatenate([jnp.ones((1,), jnp.int32),
                             (block_expert[1:] != block_expert[:-1]).astype(jnp.int32)])
    nused = (pad_end[-1:] // EXPERT_ROWS).astype(jnp.int32)
    tail_block = nused + jnp.arange(N_EXPERTS, dtype=jnp.int32)
    fill_on = jnp.concatenate([padded > 0, tail_block < nblocks]).astype(jnp.int32)
    fill_start = jnp.concatenate([jnp.maximum(pad_end - EXPERT_ROWS, 0),
                                  jnp.minimum(tail_block, nblocks - 1) * EXPERT_ROWS])

    dtile = min(DISPATCH_TILE, n)
    xs = _dispatch_call(fill_start, fill_on, dest, h2, nblocks * EXPERT_ROWS, dtile)
    ys = _expert_call(block_expert, nused, fresh, xs, exp_w_gate, exp_w_up, exp_w_down, EXPERT_ROWS)
    ctile = min(COMBINE_TILE, l)
    out = _combine_call(dest, ys, gates.T, base, g2, final_g[None, :], l // ctile, ctile)
    return out.reshape(bsz, l, d)


def kernel(x, c, mod_w, mod_b, norm1_g, norm2_g, w_in, conv_w, conv_b, dt_bias, a_log, d_skip, ssm_norm_g,
           att_norm_g, sinks, rel_bias, w_out, router_w, router_bias, exp_w_gate, exp_w_up, exp_w_down,
           sh_w_gate, sh_w_up, sh_w_down, final_g):
    assert mod_w.shape[0] == 1, "single-layer block"
    bsz = x.shape[0]
    c_pad = jnp.pad(c, ((0, SUBLANES - bsz % SUBLANES if bsz % SUBLANES else 0), (0, 0)))
    return _layer(x, c_pad, mod_w[0], mod_b[0], norm1_g[0], norm2_g[0], w_in[0], conv_w[0], conv_b[0],
                  dt_bias[0], a_log[0], d_skip[0], ssm_norm_g[0], att_norm_g[0], sinks[0], rel_bias, w_out[0],
                  router_w[0], router_bias[0], exp_w_gate[0], exp_w_up[0], exp_w_down[0], sh_w_gate[0],
                  sh_w_up[0], sh_w_down[0], final_g)
```

```python
import functools
import math

import numpy as np
import jax
import jax.numpy as jnp
from jax import lax
from jax.experimental import pallas as pl
from jax.experimental.pallas import tpu as pltpu
from jax.experimental.pallas import tpu_sc as plsc

F32 = jnp.float32
BF16 = jnp.bfloat16

D_MODEL = 1024
SSM_HEAD_DIM = 64
D_SSM = D_MODEL
SSM_HEADS = D_SSM // SSM_HEAD_DIM
SSM_GROUPS = 4
D_STATE = 128
CONV_K = 4
CONV_CH = D_SSM + 2 * SSM_GROUPS * D_STATE
CHUNK = 128
ATT_HEAD_DIM = 64
D_ATT = D_MODEL
ATT_HEADS = D_ATT // ATT_HEAD_DIM
KV_HEADS = ATT_HEADS // 4
Q_PER_KV = ATT_HEADS // KV_HEADS
D_KV = KV_HEADS * ATT_HEAD_DIM
WINDOW = 128
ATT_BLOCK = 128
REL_BUCKETS = 32
REL_MAX_DIST = 128
N_EXPERTS = 64
TOP_K = 8
EXPERT_DIM = D_MODEL // 4
SHARED_DIM = D_MODEL // 4
ROUTE_GROUPS = 8
ROUTE_TOPK_GROUPS = 4
ROUTED_SCALE = 2.5
EPS = 1e-6

LANES = 128
SUBLANES = 8
HALF = LANES // 2

ROW_TILE = 512
ROUTE_TILE = 512
DISPATCH_TILE = 256
COMBINE_TILE = 256
GATHER_CHUNK = 128
EXPERT_ROWS = 256
VMEM_LIMIT = 48 * 1024 * 1024

NEG_INF = float("-inf")


def _silu(v):
    return v * (1.0 / (1.0 + jnp.exp(-v)))


def _softplus(v):
    return jnp.maximum(v, 0.0) + jnp.log(1.0 + jnp.exp(-jnp.abs(v)))


def _bdot(a, b):
    return jnp.dot(a.astype(BF16), b.astype(BF16), preferred_element_type=F32)


def _split_hi_lo(v):
    hi = v.astype(BF16)
    lo = (v - hi.astype(F32)).astype(BF16)
    return hi, lo


def _pack_bf16_pair(a, b):
    ar = lax.bitcast_convert_type(a.astype(BF16).astype(F32), jnp.uint32)
    br = lax.bitcast_convert_type(b.astype(BF16).astype(F32), jnp.uint32)
    return lax.shift_right_logical(ar, jnp.uint32(16)) | br


def _unpack_bf16_pair(w):
    a = lax.bitcast_convert_type(lax.shift_left(w, jnp.uint32(16)), F32)
    b = lax.bitcast_convert_type(w & jnp.uint32(0xFFFF0000), F32)
    return a, b


def _lane_half_mask(shape):
    return lax.broadcasted_iota(jnp.int32, shape, len(shape) - 1) < HALF


def _mod_kernel(c_ref, w_ref, b_ref, o_ref):
    a = _silu(c_ref[...])
    o_ref[...] = jnp.dot(a, w_ref[...], precision=lax.Precision.HIGHEST,
                         preferred_element_type=F32) + b_ref[...]


def _mod_call(c_pad, mod_w, mod_b):
    rows, d = c_pad.shape
    cols = mod_w.shape[1]
    return pl.pallas_call(
        _mod_kernel,
        grid=(cols // d,),
        in_specs=[pl.BlockSpec((rows, d), lambda j: (0, 0)),
                  pl.BlockSpec((d, d), lambda j: (0, j)),
                  pl.BlockSpec((1, d), lambda j: (0, j))],
        out_specs=pl.BlockSpec((rows, d), lambda j: (0, j)),
        out_shape=jax.ShapeDtypeStruct((rows, cols), F32),
        compiler_params=pltpu.CompilerParams(dimension_semantics=("arbitrary",),
                                             vmem_limit_bytes=VMEM_LIMIT),
    )(c_pad, mod_w, mod_b)


def _in_proj_kernel(x_ref, sc_ref, sh_ref, g_ref, wz_ref, wx_ref, wdt_ref, wq_ref, wk_ref, wv_ref,
                    z_ref, xbc_ref, dt_ref, q_ref, k_ref, v_ref):
    xf = x_ref[...]
    ms = jnp.mean(xf * xf, axis=-1, keepdims=True)
    h = xf * lax.rsqrt(ms + EPS) * g_ref[...]
    h = h * (1.0 + sc_ref[0]) + sh_ref[0]
    hb = h.astype(BF16)
    z_ref[...] = jnp.dot(hb, wz_ref[...], preferred_element_type=F32).astype(BF16)
    xbc_ref[...] = jnp.dot(hb, wx_ref[...], preferred_element_type=F32).astype(BF16)
    dt_ref[...] = jnp.dot(hb, wdt_ref[...], preferred_element_type=F32)
    q_ref[...] = jnp.dot(hb, wq_ref[...], preferred_element_type=F32).astype(BF16)
    k_ref[...] = jnp.dot(hb, wk_ref[...], preferred_element_type=F32).astype(BF16)
    v_ref[...] = jnp.dot(hb, wv_ref[...], preferred_element_type=F32).astype(BF16)


def _in_proj_call(x2, sc1, sh1, g1n, wz, wx, wdt, wq, wk, wv, tiles_per_batch, tm):
    n, d = x2.shape
    row = lambda w: pl.BlockSpec((tm, w), lambda i: (i, 0))
    full = lambda a: pl.BlockSpec(a.shape, lambda i: (0, 0))
    per_batch = pl.BlockSpec((1, 1, d), lambda i: (i // tiles_per_batch, 0, 0))
    outs = [(wz.shape[1], BF16), (wx.shape[1], BF16), (wdt.shape[1], F32),
            (wq.shape[1], BF16), (wk.shape[1], BF16), (wv.shape[1], BF16)]
    return pl.pallas_call(
        _in_proj_kernel,
        grid=(n // tm,),
        in_specs=[row(d), per_batch, per_batch, full(g1n), full(wz), full(wx), full(wdt), full(wq),
                  full(wk), full(wv)],
        out_specs=[row(w) for w, _ in outs],
        out_shape=[jax.ShapeDtypeStruct((n, w), dt) for w, dt in outs],
        compiler_params=pltpu.CompilerParams(dimension_semantics=("arbitrary",),
                                             vmem_limit_bytes=VMEM_LIMIT),
    )(x2, sc1, sh1, g1n, wz, wx, wdt, wq, wk, wv)


def _ssd_kernel(xbc_ref, z_ref, dt_ref, cw_ref, cb_ref, dtb_ref, alog_ref, dskip_ref, ng_ref, tril_ref,
                y_ref, state_ref, tail_ref, ext_ref, ybuf_ref):
    c = pl.program_id(1)

    @pl.when(c == 0)
    def _():
        state_ref[...] = jnp.zeros_like(state_ref)
        tail_ref[...] = jnp.zeros_like(tail_ref)

    u = xbc_ref[0].astype(F32)
    ext_ref[0:SUBLANES, :] = tail_ref[...]
    ext_ref[SUBLANES:SUBLANES + CHUNK, :] = u
    tail_ref[...] = u[CHUNK - SUBLANES:, :]
    acc = cb_ref[...] + jnp.zeros_like(u)
    for kk in range(CONV_K):
        off = SUBLANES - (CONV_K - 1) + kk
        acc = acc + cw_ref[kk:kk + 1, :] * ext_ref[off:off + CHUNK, :]
    act = _silu(acc)
    xs = act[:, :D_SSM]
    gn = SSM_GROUPS * D_STATE

    dt = _softplus(dt_ref[0] + dtb_ref[...])
    a = dt * (-jnp.exp(alog_ref[...]))
    cs = jnp.dot(tril_ref[...], a, precision=lax.Precision.HIGHEST, preferred_element_type=F32)
    cs_t = cs.T
    dt_t = dt.T
    exp_cs = jnp.exp(cs)
    cs_last = cs[CHUNK - 1:CHUNK, :]
    chunk_decay = jnp.exp(cs_last)

    li = lax.broadcasted_iota(jnp.int32, (CHUNK, CHUNK), 0)
    si = lax.broadcasted_iota(jnp.int32, (CHUNK, CHUNK), 1)
    causal = li >= si
    low = _lane_half_mask((CHUNK, LANES))
    low_row = _lane_half_mask((1, LANES))

    heads_per_group = SSM_HEADS // SSM_GROUPS
    for g in range(SSM_GROUPS):
        b_g = act[:, D_SSM + g * D_STATE:D_SSM + (g + 1) * D_STATE]
        c_g = act[:, D_SSM + gn + g * D_STATE:D_SSM + gn + (g + 1) * D_STATE]
        b_gb = b_g.astype(BF16)
        c_gb = c_g.astype(BF16)
        cb = lax.dot_general(c_gb, b_gb, (((1,), (1,)), ((), ())), preferred_element_type=F32)
        b_t = b_g.T
        for jp in range(heads_per_group // 2):
            j = g * (heads_per_group // 2) + jp
            lanes = slice(j * LANES, (j + 1) * LANES)
            xp = xs[:, lanes]
            ydiag = jnp.zeros((CHUNK, LANES), F32)
            snew = jnp.zeros((D_STATE, LANES), F32)
            for half in range(2):
                h = 2 * j + half
                row_dt = dt_t[h:h + 1, :]
                diff = cs[:, h:h + 1] - cs_t[h:h + 1, :]
                lmat = jnp.exp(jnp.where(causal, diff, NEG_INF))
                m = (cb * lmat * row_dt).astype(BF16)
                keep = low if half == 0 else jnp.logical_not(low)
                xh = jnp.where(keep, xp, 0.0).astype(BF16)
                ydiag = ydiag + jnp.dot(m, xh, preferred_element_type=F32)
                w_t = jnp.exp(cs_t[h:h + 1, CHUNK - 1:CHUNK] - cs_t[h:h + 1, :]) * row_dt
                snew = snew + jnp.dot((b_t * w_t).astype(BF16), xh, preferred_element_type=F32)
            s_in = state_ref[:, lanes]
            yoff = jnp.dot(c_gb, s_in.astype(BF16), preferred_element_type=F32)
            h0 = 2 * j
            escale = jnp.where(low, exp_cs[:, h0:h0 + 1], exp_cs[:, h0 + 1:h0 + 2])
            cdec = jnp.where(low_row, chunk_decay[:, h0:h0 + 1], chunk_decay[:, h0 + 1:h0 + 2])
            ybuf_ref[:, lanes] = ydiag + yoff * escale + xp * dskip_ref[:, lanes]
            state_ref[:, lanes] = s_in * cdec + snew

    yz = ybuf_ref[...] * _silu(z_ref[0].astype(F32))
    gw = D_SSM // SSM_GROUPS
    for g in range(SSM_GROUPS):
        part = yz[:, g * gw:(g + 1) * gw]
        ms = jnp.mean(part * part, axis=-1, keepdims=True)
        y_ref[0, :, g * gw:(g + 1) * gw] = (part * lax.rsqrt(ms + EPS)
                                            * ng_ref[:, g * gw:(g + 1) * gw]).astype(BF16)


def _ssd_call(xbc, z, dt, conv_w, conv_b, dtb, alog, dskip, ng, tril):
    bsz, l, _ = xbc.shape
    nc = l // CHUNK
    chunk = lambda w: pl.BlockSpec((1, CHUNK, w), lambda b, c: (b, c, 0))
    full = lambda a: pl.BlockSpec(a.shape, lambda b, c: (0, 0))
    return pl.pallas_call(
        _ssd_kernel,
        grid=(bsz, nc),
        in_specs=[chunk(CONV_CH), chunk(D_SSM), chunk(LANES), full(conv_w), full(conv_b), full(dtb),
                  full(alog), full(dskip), full(ng), full(tril)],
        out_specs=chunk(D_SSM),
        out_shape=jax.ShapeDtypeStruct((bsz, l, D_SSM), BF16),
        scratch_shapes=[pltpu.VMEM((D_STATE, D_SSM), F32),
                        pltpu.VMEM((SUBLANES, CONV_CH), F32),
                        pltpu.VMEM((SUBLANES + CHUNK, CONV_CH), F32),
                        pltpu.VMEM((CHUNK, D_SSM), F32)],
        compiler_params=pltpu.CompilerParams(dimension_semantics=("arbitrary", "arbitrary"),
                                             vmem_limit_bytes=VMEM_LIMIT),
    )(xbc, z, dt, conv_w, conv_b, dtb, alog, dskip, ng, tril)


def _rel_bucket_table():
    qi = np.arange(ATT_BLOCK)[:, None]
    sj = np.arange(2 * ATT_BLOCK)[None, :]
    dist = qi + ATT_BLOCK - sj
    in_win = (dist >= 0) & (dist < WINDOW)
    dcl = np.maximum(dist, 0)
    max_exact = REL_BUCKETS // 2
    d = np.maximum(dcl, 1).astype(np.float32)
    large = max_exact + (np.log(d / np.float32(max_exact)) / np.float32(math.log(REL_MAX_DIST / max_exact))
                         * np.float32(REL_BUCKETS - max_exact)).astype(np.int32)
    large = np.minimum(large, REL_BUCKETS - 1)
    bucket = np.where(dcl < max_exact, dcl, large).astype(np.int32)
    return bucket, in_win.astype(np.int32)


def _bias_kernel(rb_ref, bucket_ref, win_ref, o_ref):
    h = pl.program_id(0)
    bucket = bucket_ref[...]
    acc = jnp.zeros(bucket.shape, F32)
    for b in range(REL_BUCKETS):
        acc = jnp.where(bucket == b, rb_ref[b, h], acc)
    o_ref[0] = jnp.where(win_ref[...] > 0, acc, NEG_INF)


def _bias_call(rel_bias, bucket, win):
    return pl.pallas_call(
        _bias_kernel,
        grid=(ATT_HEADS,),
        in_specs=[pl.BlockSpec(memory_space=pltpu.SMEM),
                  pl.BlockSpec(bucket.shape, lambda h: (0, 0)),
                  pl.BlockSpec(win.shape, lambda h: (0, 0))],
        out_specs=pl.BlockSpec((1,) + bucket.shape, lambda h: (h, 0, 0)),
        out_shape=jax.ShapeDtypeStruct((ATT_HEADS,) + bucket.shape, F32),
        compiler_params=pltpu.CompilerParams(dimension_semantics=("arbitrary",)),
    )(rel_bias, bucket, win)


def _attn_kernel(sink_ref, q_ref, kp_ref, kc_ref, vp_ref, vc_ref, bias_ref, ng_ref, o_ref, obuf_ref):
    i = pl.program_id(1)
    kband = jnp.concatenate([kp_ref[0], kc_ref[0]], axis=0).astype(F32)
    vband = jnp.concatenate([vp_ref[0], vc_ref[0]], axis=0).astype(F32)
    sj = lax.broadcasted_iota(jnp.int32, (ATT_BLOCK, 2 * ATT_BLOCK), 1)
    valid = jnp.logical_or(sj >= ATT_BLOCK, i > 0)
    low = _lane_half_mask((ATT_BLOCK, LANES))
    scale = ATT_HEAD_DIM ** -0.5

    kv_cols = []
    for cpair in range(KV_HEADS // 2):
        kk = kband[:, cpair * LANES:(cpair + 1) * LANES]
        vv = vband[:, cpair * LANES:(cpair + 1) * LANES]
        kv_cols.append(((kk.astype(BF16), pltpu.roll(kk, HALF, 1).astype(BF16)),
                        (vv.astype(BF16), pltpu.roll(vv, HALF, 1).astype(BF16))))

    for j in range(ATT_HEADS // 2):
        qp = q_ref[0, :, j * LANES:(j + 1) * LANES]
        out_pair = jnp.zeros((ATT_BLOCK, LANES), F32)
        for half in range(2):
            h = 2 * j + half
            g = h // Q_PER_KV
            swapped = int((g % 2) != half)
            ksel = kv_cols[g // 2][0][swapped]
            vsel = kv_cols[g // 2][1][swapped]
            keep = low if half == 0 else jnp.logical_not(low)
            qh = jnp.where(keep, qp, jnp.zeros_like(qp))
            s = lax.dot_general(qh, ksel, (((1,), (1,)), ((), ())), preferred_element_type=F32)
            s = s * scale + bias_ref[h]
            s = jnp.where(valid, s, NEG_INF)
            sink = sink_ref[h]
            m = jnp.maximum(jnp.max(s, axis=-1, keepdims=True), sink)
            p = jnp.exp(s - m)
            denom = jnp.sum(p, axis=-1, keepdims=True) + jnp.exp(sink - m)
            o = jnp.dot(p.astype(BF16), vsel, preferred_element_type=F32) / denom
            out_pair = out_pair + jnp.where(keep, o, 0.0)
        obuf_ref[:, j * LANES:(j + 1) * LANES] = out_pair

    att = obuf_ref[...]
    ms = jnp.mean(att * att, axis=-1, keepdims=True)
    o_ref[0] = (att * lax.rsqrt(ms + EPS) * ng_ref[...]).astype(BF16)


def _attn_call(sinks, q, k, v, bias, ng):
    bsz, l, _ = q.shape
    nb = l // ATT_BLOCK
    cur = lambda w: pl.BlockSpec((1, ATT_BLOCK, w), lambda b, i: (b, i, 0))
    prev = lambda w: pl.BlockSpec((1, ATT_BLOCK, w), lambda b, i: (b, jnp.maximum(i - 1, 0), 0))
    return pl.pallas_call(
        _attn_kernel,
        grid=(bsz, nb),
        in_specs=[pl.BlockSpec(memory_space=pltpu.SMEM),
                  cur(D_ATT), prev(D_KV), cur(D_KV), prev(D_KV), cur(D_KV),
                  pl.BlockSpec(bias.shape, lambda b, i: (0, 0, 0)),
                  pl.BlockSpec(ng.shape, lambda b, i: (0, 0))],
        out_specs=cur(D_ATT),
        out_shape=jax.ShapeDtypeStruct((bsz, l, D_ATT), BF16),
        scratch_shapes=[pltpu.VMEM((ATT_BLOCK, D_ATT), F32)],
        compiler_params=pltpu.CompilerParams(dimension_semantics=("arbitrary", "arbitrary"),
                                             vmem_limit_bytes=VMEM_LIMIT),
    )(sinks, q, k, k, v, v, bias, ng)


def _out_proj_kernel(x_ref, ys_ref, ya_ref, g1_ref, sc_ref, sh_ref, g2_ref, ng_ref, wos_ref, woa_ref,
                     rwh_ref, rwl_ref, sg_ref, su_ref, sd_ref, base_ref, h_ref, lg_ref):
    mix = (jnp.dot(ys_ref[...], wos_ref[...], preferred_element_type=F32)
           + jnp.dot(ya_ref[...], woa_ref[...], preferred_element_type=F32))
    x1 = x_ref[...] + g1_ref[0] * mix
    ms = jnp.mean(x1 * x1, axis=-1, keepdims=True)
    h = x1 * lax.rsqrt(ms + EPS) * ng_ref[...]
    h = h * (1.0 + sc_ref[0]) + sh_ref[0]
    half = h.shape[1] // 2
    h_ref[...] = _pack_bf16_pair(h[:, :half], h[:, half:])
    hi, lo = _split_hi_lo(h)
    logits = (jnp.dot(hi, rwh_ref[...], preferred_element_type=F32)
              + jnp.dot(lo, rwh_ref[...], preferred_element_type=F32)
              + jnp.dot(hi, rwl_ref[...], preferred_element_type=F32))
    lg_ref[...] = logits.T
    u = _silu(jnp.dot(hi, sg_ref[...], preferred_element_type=F32)) * jnp.dot(hi, su_ref[...],
                                                                              preferred_element_type=F32)
    shared = jnp.dot(u.astype(BF16), sd_ref[...], preferred_element_type=F32)
    base_ref[...] = x1 + g2_ref[0] * shared


def _out_proj_call(x2, ys, ya, g1, sc2, sh2, g2, ng, wos, woa, rwh, rwl, sg, su, sd, tiles_per_batch, tm):
    n, d = x2.shape
    row = lambda w: pl.BlockSpec((tm, w), lambda i: (i, 0))
    full = lambda a: pl.BlockSpec(a.shape, lambda i: (0, 0))
    per_batch = pl.BlockSpec((1, 1, d), lambda i: (i // tiles_per_batch, 0, 0))
    return pl.pallas_call(
        _out_proj_kernel,
        grid=(n // tm,),
        in_specs=[row(d), row(D_SSM), row(D_ATT), per_batch, per_batch, per_batch, per_batch, full(ng),
                  full(wos), full(woa), full(rwh), full(rwl), full(sg), full(su), full(sd)],
        out_specs=[row(d), row(d // 2), pl.BlockSpec((LANES, tm), lambda i: (0, i))],
        out_shape=[jax.ShapeDtypeStruct((n, d), F32), jax.ShapeDtypeStruct((n, d // 2), jnp.uint32),
                   jax.ShapeDtypeStruct((LANES, n), F32)],
        compiler_params=pltpu.CompilerParams(dimension_semantics=("arbitrary",),
                                             vmem_limit_bytes=VMEM_LIMIT),
    )(x2, ys, ya, g1, sc2, sh2, g2, ng, wos, woa, rwh, rwl, sg, su, sd)


def _route_kernel(lg_ref, rb_ref, upper_ref, idx_ref, gate_ref, rank_ref, cnt_ref, carry_ref):
    step = pl.program_id(0)

    @pl.when(step == 0)
    def _():
        carry_ref[...] = jnp.zeros_like(carry_ref)

    t = lg_ref.shape[1]
    per_group = N_EXPERTS // ROUTE_GROUPS
    scores = 1.0 / (1.0 + jnp.exp(-lg_ref[0:N_EXPERTS, :]))
    sel = scores + rb_ref[...]
    e_iota = lax.broadcasted_iota(jnp.int32, (N_EXPERTS, t), 0)

    sel3 = sel.reshape(ROUTE_GROUPS, per_group, t)
    w_iota = lax.broadcasted_iota(jnp.int32, sel3.shape, 1)
    m1 = jnp.max(sel3, axis=1, keepdims=True)
    first = jnp.min(jnp.where(sel3 == m1, w_iota, per_group), axis=1, keepdims=True)
    m2 = jnp.max(jnp.where(w_iota == first, NEG_INF, sel3), axis=1, keepdims=True)
    grp = (m1 + m2).reshape(ROUTE_GROUPS, t)

    g_iota = lax.broadcasted_iota(jnp.int32, (ROUTE_GROUPS, t), 0)
    gmask = jnp.zeros((ROUTE_GROUPS, t), jnp.bool_)
    for _ in range(ROUTE_TOPK_GROUPS):
        gm = jnp.max(grp, axis=0, keepdims=True)
        gfirst = jnp.min(jnp.where(grp == gm, g_iota, ROUTE_GROUPS), axis=0, keepdims=True)
        hit = g_iota == gfirst
        gmask = jnp.logical_or(gmask, hit)
        grp = jnp.where(hit, NEG_INF, grp)
    allowed = jnp.broadcast_to(gmask.reshape(ROUTE_GROUPS, 1, t),
                               (ROUTE_GROUPS, per_group, t)).reshape(N_EXPERTS, t)
    masked = jnp.where(allowed, sel, NEG_INF)

    picked = jnp.zeros((N_EXPERTS, t), jnp.bool_)
    idx_rows = []
    w_rows = []
    for _ in range(TOP_K):
        mm = jnp.max(masked, axis=0, keepdims=True)
        efirst = jnp.min(jnp.where(masked == mm, e_iota, N_EXPERTS), axis=0, keepdims=True)
        hit = e_iota == efirst
        idx_rows.append(efirst)
        w_rows.append(jnp.sum(jnp.where(hit, scores, 0.0), axis=0, keepdims=True))
        picked = jnp.logical_or(picked, hit)
        masked = jnp.where(hit, NEG_INF, masked)
    idx = jnp.concatenate(idx_rows, axis=0)
    w = jnp.concatenate(w_rows, axis=0)
    gate_ref[...] = w / jnp.sum(w, axis=0, keepdims=True) * ROUTED_SCALE
    idx_ref[...] = idx

    onehot = jnp.where(picked, 1.0, 0.0)
    before = jnp.dot(onehot.astype(BF16), upper_ref[...], preferred_element_type=F32)
    rank_full = before + carry_ref[:, 0:1]
    rank_rows = [jnp.sum(jnp.where(e_iota == idx_rows[k], rank_full, 0.0), axis=0, keepdims=True)
                 for k in range(TOP_K)]
    rank_ref[...] = jnp.concatenate(rank_rows, axis=0).astype(jnp.int32)
    carry_ref[...] = carry_ref[...] + jnp.sum(onehot, axis=1, keepdims=True)
    cnt_ref[...] = carry_ref[...]


def _route_call(logits_t, router_bias, upper, tile):
    n = logits_t.shape[1]
    tok = lambda r: pl.BlockSpec((r, tile), lambda i: (0, i))
    return pl.pallas_call(
        _route_kernel,
        grid=(n // tile,),
        in_specs=[tok(LANES), pl.BlockSpec((N_EXPERTS, 1), lambda i: (0, 0)),
                  pl.BlockSpec(upper.shape, lambda i: (0, 0))],
        out_specs=[tok(TOP_K), tok(TOP_K), tok(TOP_K), pl.BlockSpec((N_EXPERTS, LANES), lambda i: (0, 0))],
        out_shape=[jax.ShapeDtypeStruct((TOP_K, n), jnp.int32), jax.ShapeDtypeStruct((TOP_K, n), F32),
                   jax.ShapeDtypeStruct((TOP_K, n), jnp.int32),
                   jax.ShapeDtypeStruct((N_EXPERTS, LANES), F32)],
        scratch_shapes=[pltpu.VMEM((N_EXPERTS, LANES), F32)],
        compiler_params=pltpu.CompilerParams(dimension_semantics=("arbitrary",),
                                             vmem_limit_bytes=VMEM_LIMIT),
    )(logits_t, router_bias, upper)


def _dest_kernel(idx_ref, rank_ref, start_ref, dest_ref):
    t = idx_ref.shape[1]
    e_iota = lax.broadcasted_iota(jnp.int32, (N_EXPERTS, t), 0)
    rows = [jnp.sum(jnp.where(e_iota == idx_ref[k:k + 1, :], start_ref[...], 0), axis=0, keepdims=True)
            for k in range(TOP_K)]
    dest_ref[...] = jnp.concatenate(rows, axis=0) + rank_ref[...]


def _dest_call(idx, rank, pad_start, tile):
    n = idx.shape[1]
    tok = pl.BlockSpec((TOP_K, tile), lambda i: (0, i))
    return pl.pallas_call(
        _dest_kernel,
        grid=(n // tile,),
        in_specs=[tok, tok, pl.BlockSpec((N_EXPERTS, 1), lambda i: (0, 0))],
        out_specs=tok,
        out_shape=jax.ShapeDtypeStruct((TOP_K, n), jnp.int32),
        compiler_params=pltpu.CompilerParams(dimension_semantics=("arbitrary",)),
    )(idx, rank, pad_start)


def _dispatch_kernel(fill_start_ref, fill_on_ref, dest_ref, h_ref, xs_ref, zero_ref, sem, fill_sem):
    t = h_ref.shape[0]

    @pl.when(pl.program_id(0) == 0)
    def _():
        zero_ref[...] = jnp.zeros_like(zero_ref)

        def fill_copy(e):
            first = pl.multiple_of(fill_start_ref[e], EXPERT_ROWS)
            return pltpu.make_async_copy(zero_ref, xs_ref.at[pl.ds(first, EXPERT_ROWS)], fill_sem)

        def start(e, carry):
            @pl.when(fill_on_ref[e] > 0)
            def _():
                fill_copy(e).start()
            return carry

        def wait(e, carry):
            @pl.when(fill_on_ref[e] > 0)
            def _():
                fill_copy(e).wait()
            return carry

        lax.fori_loop(0, fill_on_ref.shape[0], start, 0)
        lax.fori_loop(0, fill_on_ref.shape[0], wait, 0)

    def issue(tok, carry):
        for k in range(TOP_K):
            pltpu.make_async_copy(h_ref.at[pl.ds(tok, 1)], xs_ref.at[pl.ds(dest_ref[k, tok], 1)], sem).start()
        return carry

    lax.fori_loop(0, t, issue, 0)
    for k in range(TOP_K):
        pltpu.make_async_copy(h_ref, xs_ref.at[pl.ds(0, t)], sem).wait()


def _dispatch_call(fill_start, fill_on, dest, h2, total_rows, tile):
    n, w = h2.shape
    grid_spec = pltpu.PrefetchScalarGridSpec(
        num_scalar_prefetch=2,
        grid=(n // tile,),
        in_specs=[pl.BlockSpec((TOP_K, tile), lambda i, fs, fo: (0, i), memory_space=pltpu.SMEM),
                  pl.BlockSpec((tile, w), lambda i, fs, fo: (i, 0))],
        out_specs=pl.BlockSpec(memory_space=pl.ANY),
        scratch_shapes=[pltpu.VMEM((EXPERT_ROWS, w), h2.dtype), pltpu.SemaphoreType.DMA(()),
                        pltpu.SemaphoreType.DMA(())],
    )
    return pl.pallas_call(
        _dispatch_kernel,
        grid_spec=grid_spec,
        out_shape=jax.ShapeDtypeStruct((total_rows, w), h2.dtype),
        compiler_params=pltpu.CompilerParams(dimension_semantics=("arbitrary",),
                                             vmem_limit_bytes=VMEM_LIMIT),
    )(fill_start, fill_on, dest, h2)


def _expert_kernel(be_ref, nused_ref, fresh_ref, xs_ref, wg_ref, wu_ref, wd_ref, y_ref, wgb_ref, wub_ref,
                   wdb_ref):
    i = pl.program_id(0)

    @pl.when(fresh_ref[i] > 0)
    def _():
        wgb_ref[...] = wg_ref[0].astype(BF16)
        wub_ref[...] = wu_ref[0].astype(BF16)
        wdb_ref[...] = wd_ref[0].astype(BF16)

    @pl.when(i < nused_ref[0])
    def _():
        half = xs_ref.shape[1]
        x_lo, x_hi = _unpack_bf16_pair(xs_ref[...])
        x_lo = x_lo.astype(BF16)
        x_hi = x_hi.astype(BF16)
        gate = (jnp.dot(x_lo, wgb_ref[:half, :], preferred_element_type=F32)
                + jnp.dot(x_hi, wgb_ref[half:, :], preferred_element_type=F32))
        up = (jnp.dot(x_lo, wub_ref[:half, :], preferred_element_type=F32)
              + jnp.dot(x_hi, wub_ref[half:, :], preferred_element_type=F32))
        u = (_silu(gate) * up).astype(BF16)
        y_lo = jnp.dot(u, wdb_ref[:, :half], preferred_element_type=F32)
        y_hi = jnp.dot(u, wdb_ref[:, half:], preferred_element_type=F32)
        y_ref[...] = _pack_bf16_pair(y_lo, y_hi)

    @pl.when(i >= nused_ref[0])
    def _():
        y_ref[...] = jnp.zeros_like(y_ref)


def _expert_call(block_expert, nused, fresh, xs, wg, wu, wd, rows):
    p, w = xs.shape
    d, f = wg.shape[1], wg.shape[2]
    x_map = lambda i, be, nu, fr: (jnp.minimum(i, nu[0] - 1), 0)
    w_map = lambda i, be, nu, fr: (be[i], 0, 0)
    grid_spec = pltpu.PrefetchScalarGridSpec(
        num_scalar_prefetch=3,
        grid=(p // rows,),
        in_specs=[pl.BlockSpec((rows, w), x_map),
                  pl.BlockSpec((1, d, f), w_map), pl.BlockSpec((1, d, f), w_map), pl.BlockSpec((1, f, d), w_map)],
        out_specs=pl.BlockSpec((rows, w), lambda i, be, nu, fr: (i, 0)),
        scratch_shapes=[pltpu.VMEM((d, f), BF16), pltpu.VMEM((d, f), BF16), pltpu.VMEM((f, d), BF16)],
    )
    return pl.pallas_call(
        _expert_kernel,
        grid_spec=grid_spec,
        out_shape=jax.ShapeDtypeStruct((p, w), jnp.uint32),
        compiler_params=pltpu.CompilerParams(dimension_semantics=("arbitrary",),
                                             vmem_limit_bytes=VMEM_LIMIT),
    )(block_expert, nused, fresh, xs, wg, wu, wd)


def _gather_rows_sc(table, idx, chunk):
    m = idx.shape[0]
    w = table.shape[1]
    info = plsc.get_sparse_core_info()
    nc = info.num_cores
    per_worker = m // (nc * info.num_subcores)
    assert per_worker * nc * info.num_subcores == m and per_worker % chunk == 0
    mesh = plsc.VectorSubcoreMesh(core_axis_name="c", subcore_axis_name="s")

    @functools.partial(
        pl.kernel, mesh=mesh,
        out_type=jax.ShapeDtypeStruct((m, w), table.dtype),
        scratch_types=[pltpu.VMEM((chunk,), jnp.int32), pltpu.VMEM((chunk, w), table.dtype),
                       pltpu.SemaphoreType.DMA],
    )
    def gather(table_hbm, idx_hbm, out_hbm, idx_v, rows_v, sem):
        worker = lax.axis_index("s") * nc + lax.axis_index("c")

        @pl.loop(0, per_worker // chunk)
        def _(j):
            base = worker * per_worker + j * chunk
            pltpu.sync_copy(idx_hbm.at[pl.ds(base, chunk)], idx_v)
            pltpu.async_copy(table_hbm.at[idx_v], rows_v, sem).wait()
            pltpu.sync_copy(rows_v, out_hbm.at[pl.ds(base, chunk)])

    return gather(table, idx)


def _combine_kernel(yk_ref, gate_ref, base_ref, g2_ref, fg_ref, o_ref):
    t = base_ref.shape[0]
    half = yk_ref.shape[2]
    gates = gate_ref[...]
    r_lo = jnp.zeros((t, half), F32)
    r_hi = jnp.zeros((t, half), F32)
    for k in range(TOP_K):
        y_lo, y_hi = _unpack_bf16_pair(yk_ref[k])
        r_lo = r_lo + gates[:, k:k + 1] * y_lo
        r_hi = r_hi + gates[:, k:k + 1] * y_hi
    g2 = g2_ref[0]
    x_lo = base_ref[:, :half] + g2[:, :half] * r_lo
    x_hi = base_ref[:, half:] + g2[:, half:] * r_hi
    ms = (jnp.sum(x_lo * x_lo, axis=-1, keepdims=True)
          + jnp.sum(x_hi * x_hi, axis=-1, keepdims=True)) * (1.0 / (2 * half))
    inv = lax.rsqrt(ms + EPS)
    o_ref[:, :half] = x_lo * inv * fg_ref[:, :half]
    o_ref[:, half:] = x_hi * inv * fg_ref[:, half:]


def _combine_call(yk, gates_t, base, g2, fg, tiles_per_batch, tile):
    n, d = base.shape
    row = lambda w: pl.BlockSpec((tile, w), lambda i: (i, 0))
    return pl.pallas_call(
        _combine_kernel,
        grid=(n // tile,),
        in_specs=[pl.BlockSpec((TOP_K, tile, yk.shape[2]), lambda i: (0, i, 0)),
                  row(TOP_K), row(d),
                  pl.BlockSpec((1, 1, d), lambda i: (i // tiles_per_batch, 0, 0)),
                  pl.BlockSpec((1, d), lambda i: (0, 0))],
        out_specs=row(d),
        out_shape=jax.ShapeDtypeStruct((n, d), F32),
        compiler_params=pltpu.CompilerParams(dimension_semantics=("arbitrary",),
                                             vmem_limit_bytes=VMEM_LIMIT),
    )(yk, gates_t, base, g2, fg)


def _pad_cols(a, width):
    return jnp.pad(a, ((0, 0), (0, width - a.shape[1])))


def _layer(x, c_pad, mod_w, mod_b, norm1_g, norm2_g, w_in, conv_w, conv_b, dt_bias, a_log, d_skip, ssm_norm_g,
           att_norm_g, sinks, rel_bias, w_out, router_w, router_bias, exp_w_gate, exp_w_up, exp_w_down,
           sh_w_gate, sh_w_up, sh_w_down, final_g):
    bsz, l, d = x.shape
    n = bsz * l
    tm = min(ROW_TILE, l)

    mod = _mod_call(c_pad, mod_w, mod_b[None, :])[:bsz]
    sh1, sc1, g1, sh2, sc2, g2 = [m[:, None, :] for m in jnp.split(mod, 6, axis=-1)]

    i1 = D_SSM
    i2 = i1 + CONV_CH
    i3 = i2 + SSM_HEADS
    i4 = i3 + D_ATT
    i5 = i4 + D_KV
    wz, wx, wdt, wq, wk, wv = jnp.split(w_in, [i1, i2, i3, i4, i5], axis=-1)
    wdt = _pad_cols(wdt, LANES)
    x2 = x.reshape(n, d)
    z, xbc, dt, q, k, v = _in_proj_call(x2, sc1, sh1, norm1_g[None, :], wz.astype(BF16), wx.astype(BF16),
                                        wdt.astype(BF16), wq.astype(BF16), wk.astype(BF16), wv.astype(BF16),
                                        l // tm, tm)

    tril = jnp.asarray(np.tril(np.ones((CHUNK, CHUNK), np.float32)))
    y_ssm = _ssd_call(xbc.reshape(bsz, l, CONV_CH), z.reshape(bsz, l, D_SSM), dt.reshape(bsz, l, LANES),
                      conv_w, conv_b[None, :], _pad_cols(dt_bias[None, :], LANES), _pad_cols(a_log[None, :], LANES),
                      jnp.repeat(d_skip, SSM_HEAD_DIM)[None, :], ssm_norm_g[None, :], tril)

    bucket, win = _rel_bucket_table()
    bias = _bias_call(rel_bias, jnp.asarray(bucket), jnp.asarray(win))
    y_att = _attn_call(sinks, q.reshape(bsz, l, D_ATT), k.reshape(bsz, l, D_KV), v.reshape(bsz, l, D_KV), bias,
                       att_norm_g[None, :])

    rw = _pad_cols(router_w, LANES)
    rwh = rw.astype(BF16)
    rwl = (rw - rwh.astype(F32)).astype(BF16)
    base, h2, logits_t = _out_proj_call(
        x2, y_ssm.reshape(n, D_SSM), y_att.reshape(n, D_ATT), g1, sc2, sh2, g2, norm2_g[None, :],
        w_out[:D_SSM].astype(BF16), w_out[D_SSM:].astype(BF16), rwh, rwl,
        sh_w_gate.astype(BF16), sh_w_up.astype(BF16), sh_w_down.astype(BF16), l // tm, tm)

    rt = min(ROUTE_TILE, n)
    upper = jnp.asarray(np.triu(np.ones((rt, rt), np.float32), 1)).astype(BF16)
    idx, gates, rank, counts = _route_call(logits_t, router_bias[:, None], upper, rt)

    nblocks = (n * TOP_K + N_EXPERTS * (EXPERT_ROWS - 1) + EXPERT_ROWS - 1) // EXPERT_ROWS
    cnt = counts[:, 0].astype(jnp.int32)
    padded = (cnt + EXPERT_ROWS - 1) // EXPERT_ROWS * EXPERT_ROWS
    pad_end = jnp.cumsum(padded)
    pad_start = pad_end - padded
    dest = _dest_call(idx, rank, pad_start[:, None], rt)
    block_first_row = jnp.arange(nblocks, dtype=jnp.int32) * EXPERT_ROWS
    block_expert = jnp.minimum(jnp.sum((pad_end[None, :] <= block_first_row[:, None]).astype(jnp.int32), axis=1),
                               N_EXPERTS - 1)
    fresh = jnp.concatenate([jnp.ones((1,), jnp.int32),
                             (block_expert[1:] != block_expert[:-1]).astype(jnp.int32)])
    nused = (pad_end[-1:] // EXPERT_ROWS).astype(jnp.int32)
    tail_block = nused + jnp.arange(N_EXPERTS, dtype=jnp.int32)
    fill_on = jnp.concatenate([padded > 0, tail_block < nblocks]).astype(jnp.int32)
    fill_start = jnp.concatenate([jnp.maximum(pad_end - EXPERT_ROWS, 0),
                                  jnp.minimum(tail_block, nblocks - 1) * EXPERT_ROWS])

    dtile = min(DISPATCH_TILE, n)
    xs = _dispatch_call(fill_start, fill_on, dest, h2, nblocks * EXPERT_ROWS, dtile)
    ys = _expert_call(block_expert, nused, fresh, xs, exp_w_gate, exp_w_up, exp_w_down, EXPERT_ROWS)
    yk = _gather_rows_sc(ys, dest.reshape(-1), GATHER_CHUNK).reshape(TOP_K, n, ys.shape[1])
    ctile = min(COMBINE_TILE, l)
    out = _combine_call(yk, gates.T, base, g2, final_g[None, :], l // ctile, ctile)
    return out.reshape(bsz, l, d)


def kernel(x, c, mod_w, mod_b, norm1_g, norm2_g, w_in, conv_w, conv_b, dt_bias, a_log, d_skip, ssm_norm_g,
           att_norm_g, sinks, rel_bias, w_out, router_w, router_bias, exp_w_gate, exp_w_up, exp_w_down,
           sh_w_gate, sh_w_up, sh_w_down, final_g):
    assert mod_w.shape[0] == 1, "single-layer block"
    bsz = x.shape[0]
    c_pad = jnp.pad(c, ((0, SUBLANES - bsz % SUBLANES if bsz % SUBLANES else 0), (0, 0)))
    return _layer(x, c_pad, mod_w[0], mod_b[0], norm1_g[0], norm2_g[0], w_in[0], conv_w[0], conv_b[0],
                  dt_bias[0], a_log[0], d_skip[0], ssm_norm_g[0], att_norm_g[0], sinks[0], rel_bias, w_out[0],
                  router_w[0], router_bias[0], exp_w_gate[0], exp_w_up[0], exp_w_down[0], sh_w_gate[0],
                  sh_w_up[0], sh_w_down[0], final_g)
```

```python
import functools
import math

import numpy as np
import jax
import jax.numpy as jnp
from jax import lax
from jax.experimental import pallas as pl
from jax.experimental.pallas import tpu as pltpu
from jax.experimental.pallas import tpu_sc as plsc

F32 = jnp.float32
BF16 = jnp.bfloat16

D_MODEL = 1024
SSM_HEAD_DIM = 64
D_SSM = D_MODEL
SSM_HEADS = D_SSM // SSM_HEAD_DIM
SSM_GROUPS = 4
D_STATE = 128
CONV_K = 4
CONV_CH = D_SSM + 2 * SSM_GROUPS * D_STATE
CHUNK = 128
ATT_HEAD_DIM = 64
D_ATT = D_MODEL
ATT_HEADS = D_ATT // ATT_HEAD_DIM
KV_HEADS = ATT_HEADS // 4
Q_PER_KV = ATT_HEADS // KV_HEADS
D_KV = KV_HEADS * ATT_HEAD_DIM
WINDOW = 128
ATT_BLOCK = 128
REL_BUCKETS = 32
REL_MAX_DIST = 128
N_EXPERTS = 64
TOP_K = 8
EXPERT_DIM = D_MODEL // 4
SHARED_DIM = D_MODEL // 4
ROUTE_GROUPS = 8
ROUTE_TOPK_GROUPS = 4
ROUTED_SCALE = 2.5
EPS = 1e-6

LANES = 128
SUBLANES = 8
HALF = LANES // 2

ROW_TILE = 512
ROUTE_TILE = 512
COMBINE_TILE = 256
SC_CHUNK = 128
EXPERT_ROWS = 256
VMEM_LIMIT = 48 * 1024 * 1024

NEG_INF = float("-inf")


def _silu(v):
    return v * (1.0 / (1.0 + jnp.exp(-v)))


def _softplus(v):
    return jnp.maximum(v, 0.0) + jnp.log(1.0 + jnp.exp(-jnp.abs(v)))


def _bdot(a, b):
    return jnp.dot(a.astype(BF16), b.astype(BF16), preferred_element_type=F32)


def _split_hi_lo(v):
    hi = v.astype(BF16)
    lo = (v - hi.astype(F32)).astype(BF16)
    return hi, lo


def _pack_bf16_pair(a, b):
    ar = lax.bitcast_convert_type(a.astype(BF16).astype(F32), jnp.uint32)
    br = lax.bitcast_convert_type(b.astype(BF16).astype(F32), jnp.uint32)
    return lax.shift_right_logical(ar, jnp.uint32(16)) | br


def _unpack_bf16_pair(w):
    a = lax.bitcast_convert_type(lax.shift_left(w, jnp.uint32(16)), F32)
    b = lax.bitcast_convert_type(w & jnp.uint32(0xFFFF0000), F32)
    return a, b


def _lane_half_mask(shape):
    return lax.broadcasted_iota(jnp.int32, shape, len(shape) - 1) < HALF


def _mod_kernel(c_ref, w_ref, b_ref, o_ref):
    a = _silu(c_ref[...])
    o_ref[...] = jnp.dot(a, w_ref[...], precision=lax.Precision.HIGHEST,
                         preferred_element_type=F32) + b_ref[...]


def _mod_call(c_pad, mod_w, mod_b):
    rows, d = c_pad.shape
    cols = mod_w.shape[1]
    return pl.pallas_call(
        _mod_kernel,
        grid=(cols // d,),
        in_specs=[pl.BlockSpec((rows, d), lambda j: (0, 0)),
                  pl.BlockSpec((d, d), lambda j: (0, j)),
                  pl.BlockSpec((1, d), lambda j: (0, j))],
        out_specs=pl.BlockSpec((rows, d), lambda j: (0, j)),
        out_shape=jax.ShapeDtypeStruct((rows, cols), F32),
        compiler_params=pltpu.CompilerParams(dimension_semantics=("arbitrary",),
                                             vmem_limit_bytes=VMEM_LIMIT),
    )(c_pad, mod_w, mod_b)


def _in_proj_kernel(x_ref, sc_ref, sh_ref, g_ref, wz_ref, wx_ref, wdt_ref, wq_ref, wk_ref, wv_ref,
                    z_ref, xbc_ref, dt_ref, q_ref, k_ref, v_ref):
    xf = x_ref[...]
    ms = jnp.mean(xf * xf, axis=-1, keepdims=True)
    h = xf * lax.rsqrt(ms + EPS) * g_ref[...]
    h = h * (1.0 + sc_ref[0]) + sh_ref[0]
    hb = h.astype(BF16)
    z_ref[...] = jnp.dot(hb, wz_ref[...], preferred_element_type=F32).astype(BF16)
    xbc_ref[...] = jnp.dot(hb, wx_ref[...], preferred_element_type=F32).astype(BF16)
    dt_ref[...] = jnp.dot(hb, wdt_ref[...], preferred_element_type=F32)
    q_ref[...] = jnp.dot(hb, wq_ref[...], preferred_element_type=F32).astype(BF16)
    k_ref[...] = jnp.dot(hb, wk_ref[...], preferred_element_type=F32).astype(BF16)
    v_ref[...] = jnp.dot(hb, wv_ref[...], preferred_element_type=F32).astype(BF16)


def _in_proj_call(x2, sc1, sh1, g1n, wz, wx, wdt, wq, wk, wv, tiles_per_batch, tm):
    n, d = x2.shape
    row = lambda w: pl.BlockSpec((tm, w), lambda i: (i, 0))
    full = lambda a: pl.BlockSpec(a.shape, lambda i: (0, 0))
    per_batch = pl.BlockSpec((1, 1, d), lambda i: (i // tiles_per_batch, 0, 0))
    outs = [(wz.shape[1], BF16), (wx.shape[1], BF16), (wdt.shape[1], F32),
            (wq.shape[1], BF16), (wk.shape[1], BF16), (wv.shape[1], BF16)]
    return pl.pallas_call(
        _in_proj_kernel,
        grid=(n // tm,),
        in_specs=[row(d), per_batch, per_batch, full(g1n), full(wz), full(wx), full(wdt), full(wq),
                  full(wk), full(wv)],
        out_specs=[row(w) for w, _ in outs],
        out_shape=[jax.ShapeDtypeStruct((n, w), dt) for w, dt in outs],
        compiler_params=pltpu.CompilerParams(dimension_semantics=("arbitrary",),
                                             vmem_limit_bytes=VMEM_LIMIT),
    )(x2, sc1, sh1, g1n, wz, wx, wdt, wq, wk, wv)


def _ssd_kernel(xbc_ref, z_ref, dt_ref, cw_ref, cb_ref, dtb_ref, alog_ref, dskip_ref, ng_ref, tril_ref,
                y_ref, state_ref, tail_ref, ext_ref, ybuf_ref):
    c = pl.program_id(1)

    @pl.when(c == 0)
    def _():
        state_ref[...] = jnp.zeros_like(state_ref)
        tail_ref[...] = jnp.zeros_like(tail_ref)

    u = xbc_ref[0].astype(F32)
    ext_ref[0:SUBLANES, :] = tail_ref[...]
    ext_ref[SUBLANES:SUBLANES + CHUNK, :] = u
    tail_ref[...] = u[CHUNK - SUBLANES:, :]
    acc = cb_ref[...] + jnp.zeros_like(u)
    for kk in range(CONV_K):
        off = SUBLANES - (CONV_K - 1) + kk
        acc = acc + cw_ref[kk:kk + 1, :] * ext_ref[off:off + CHUNK, :]
    act = _silu(acc)
    xs = act[:, :D_SSM]
    gn = SSM_GROUPS * D_STATE

    dt = _softplus(dt_ref[0] + dtb_ref[...])
    a = dt * (-jnp.exp(alog_ref[...]))
    cs = jnp.dot(tril_ref[...], a, precision=lax.Precision.HIGHEST, preferred_element_type=F32)
    cs_t = cs.T
    dt_t = dt.T
    exp_cs = jnp.exp(cs)
    cs_last = cs[CHUNK - 1:CHUNK, :]
    chunk_decay = jnp.exp(cs_last)

    li = lax.broadcasted_iota(jnp.int32, (CHUNK, CHUNK), 0)
    si = lax.broadcasted_iota(jnp.int32, (CHUNK, CHUNK), 1)
    causal = li >= si
    low = _lane_half_mask((CHUNK, LANES))
    low_row = _lane_half_mask((1, LANES))

    heads_per_group = SSM_HEADS // SSM_GROUPS
    for g in range(SSM_GROUPS):
        b_g = act[:, D_SSM + g * D_STATE:D_SSM + (g + 1) * D_STATE]
        c_g = act[:, D_SSM + gn + g * D_STATE:D_SSM + gn + (g + 1) * D_STATE]
        b_gb = b_g.astype(BF16)
        c_gb = c_g.astype(BF16)
        cb = lax.dot_general(c_gb, b_gb, (((1,), (1,)), ((), ())), preferred_element_type=F32)
        b_t = b_g.T
        for jp in range(heads_per_group // 2):
            j = g * (heads_per_group // 2) + jp
            lanes = slice(j * LANES, (j + 1) * LANES)
            xp = xs[:, lanes]
            ydiag = jnp.zeros((CHUNK, LANES), F32)
            snew = jnp.zeros((D_STATE, LANES), F32)
            for half in range(2):
                h = 2 * j + half
                row_dt = dt_t[h:h + 1, :]
                diff = cs[:, h:h + 1] - cs_t[h:h + 1, :]
                lmat = jnp.exp(jnp.where(causal, diff, NEG_INF))
                m = (cb * lmat * row_dt).astype(BF16)
                keep = low if half == 0 else jnp.logical_not(low)
                xh = jnp.where(keep, xp, 0.0).astype(BF16)
                ydiag = ydiag + jnp.dot(m, xh, preferred_element_type=F32)
                w_t = jnp.exp(cs_t[h:h + 1, CHUNK - 1:CHUNK] - cs_t[h:h + 1, :]) * row_dt
                snew = snew + jnp.dot((b_t * w_t).astype(BF16), xh, preferred_element_type=F32)
            s_in = state_ref[:, lanes]
            yoff = jnp.dot(c_gb, s_in.astype(BF16), preferred_element_type=F32)
            h0 = 2 * j
            escale = jnp.where(low, exp_cs[:, h0:h0 + 1], exp_cs[:, h0 + 1:h0 + 2])
            cdec = jnp.where(low_row, chunk_decay[:, h0:h0 + 1], chunk_decay[:, h0 + 1:h0 + 2])
            ybuf_ref[:, lanes] = ydiag + yoff * escale + xp * dskip_ref[:, lanes]
            state_ref[:, lanes] = s_in * cdec + snew

    yz = ybuf_ref[...] * _silu(z_ref[0].astype(F32))
    gw = D_SSM // SSM_GROUPS
    for g in range(SSM_GROUPS):
        part = yz[:, g * gw:(g + 1) * gw]
        ms = jnp.mean(part * part, axis=-1, keepdims=True)
        y_ref[0, :, g * gw:(g + 1) * gw] = (part * lax.rsqrt(ms + EPS)
                                            * ng_ref[:, g * gw:(g + 1) * gw]).astype(BF16)


def _ssd_call(xbc, z, dt, conv_w, conv_b, dtb, alog, dskip, ng, tril):
    bsz, l, _ = xbc.shape
    nc = l // CHUNK
    chunk = lambda w: pl.BlockSpec((1, CHUNK, w), lambda b, c: (b, c, 0))
    full = lambda a: pl.BlockSpec(a.shape, lambda b, c: (0, 0))
    return pl.pallas_call(
        _ssd_kernel,
        grid=(bsz, nc),
        in_specs=[chunk(CONV_CH), chunk(D_SSM), chunk(LANES), full(conv_w), full(conv_b), full(dtb),
                  full(alog), full(dskip), full(ng), full(tril)],
        out_specs=chunk(D_SSM),
        out_shape=jax.ShapeDtypeStruct((bsz, l, D_SSM), BF16),
        scratch_shapes=[pltpu.VMEM((D_STATE, D_SSM), F32),
                        pltpu.VMEM((SUBLANES, CONV_CH), F32),
                        pltpu.VMEM((SUBLANES + CHUNK, CONV_CH), F32),
                        pltpu.VMEM((CHUNK, D_SSM), F32)],
        compiler_params=pltpu.CompilerParams(dimension_semantics=("arbitrary", "arbitrary"),
                                             vmem_limit_bytes=VMEM_LIMIT),
    )(xbc, z, dt, conv_w, conv_b, dtb, alog, dskip, ng, tril)


def _rel_bucket_table():
    qi = np.arange(ATT_BLOCK)[:, None]
    sj = np.arange(2 * ATT_BLOCK)[None, :]
    dist = qi + ATT_BLOCK - sj
    in_win = (dist >= 0) & (dist < WINDOW)
    dcl = np.maximum(dist, 0)
    max_exact = REL_BUCKETS // 2
    d = np.maximum(dcl, 1).astype(np.float32)
    large = max_exact + (np.log(d / np.float32(max_exact)) / np.float32(math.log(REL_MAX_DIST / max_exact))
                         * np.float32(REL_BUCKETS - max_exact)).astype(np.int32)
    large = np.minimum(large, REL_BUCKETS - 1)
    bucket = np.where(dcl < max_exact, dcl, large).astype(np.int32)
    return bucket, in_win.astype(np.int32)


def _bias_kernel(rb_ref, bucket_ref, win_ref, o_ref):
    h = pl.program_id(0)
    bucket = bucket_ref[...]
    acc = jnp.zeros(bucket.shape, F32)
    for b in range(REL_BUCKETS):
        acc = jnp.where(bucket == b, rb_ref[b, h], acc)
    o_ref[0] = jnp.where(win_ref[...] > 0, acc, NEG_INF)


def _bias_call(rel_bias, bucket, win):
    return pl.pallas_call(
        _bias_kernel,
        grid=(ATT_HEADS,),
        in_specs=[pl.BlockSpec(memory_space=pltpu.SMEM),
                  pl.BlockSpec(bucket.shape, lambda h: (0, 0)),
                  pl.BlockSpec(win.shape, lambda h: (0, 0))],
        out_specs=pl.BlockSpec((1,) + bucket.shape, lambda h: (h, 0, 0)),
        out_shape=jax.ShapeDtypeStruct((ATT_HEADS,) + bucket.shape, F32),
        compiler_params=pltpu.CompilerParams(dimension_semantics=("arbitrary",)),
    )(rel_bias, bucket, win)


def _attn_kernel(sink_ref, q_ref, kp_ref, kc_ref, vp_ref, vc_ref, bias_ref, ng_ref, o_ref, obuf_ref):
    i = pl.program_id(1)
    kband = jnp.concatenate([kp_ref[0], kc_ref[0]], axis=0).astype(F32)
    vband = jnp.concatenate([vp_ref[0], vc_ref[0]], axis=0).astype(F32)
    sj = lax.broadcasted_iota(jnp.int32, (ATT_BLOCK, 2 * ATT_BLOCK), 1)
    valid = jnp.logical_or(sj >= ATT_BLOCK, i > 0)
    low = _lane_half_mask((ATT_BLOCK, LANES))
    scale = ATT_HEAD_DIM ** -0.5

    kv_cols = []
    for cpair in range(KV_HEADS // 2):
        kk = kband[:, cpair * LANES:(cpair + 1) * LANES]
        vv = vband[:, cpair * LANES:(cpair + 1) * LANES]
        kv_cols.append(((kk.astype(BF16), pltpu.roll(kk, HALF, 1).astype(BF16)),
                        (vv.astype(BF16), pltpu.roll(vv, HALF, 1).astype(BF16))))

    for j in range(ATT_HEADS // 2):
        qp = q_ref[0, :, j * LANES:(j + 1) * LANES]
        out_pair = jnp.zeros((ATT_BLOCK, LANES), F32)
        for half in range(2):
            h = 2 * j + half
            g = h // Q_PER_KV
            swapped = int((g % 2) != half)
            ksel = kv_cols[g // 2][0][swapped]
            vsel = kv_cols[g // 2][1][swapped]
            keep = low if half == 0 else jnp.logical_not(low)
            qh = jnp.where(keep, qp, jnp.zeros_like(qp))
            s = lax.dot_general(qh, ksel, (((1,), (1,)), ((), ())), preferred_element_type=F32)
            s = s * scale + bias_ref[h]
            s = jnp.where(valid, s, NEG_INF)
            sink = sink_ref[h]
            m = jnp.maximum(jnp.max(s, axis=-1, keepdims=True), sink)
            p = jnp.exp(s - m)
            denom = jnp.sum(p, axis=-1, keepdims=True) + jnp.exp(sink - m)
            o = jnp.dot(p.astype(BF16), vsel, preferred_element_type=F32) / denom
            out_pair = out_pair + jnp.where(keep, o, 0.0)
        obuf_ref[:, j * LANES:(j + 1) * LANES] = out_pair

    att = obuf_ref[...]
    ms = jnp.mean(att * att, axis=-1, keepdims=True)
    o_ref[0] = (att * lax.rsqrt(ms + EPS) * ng_ref[...]).astype(BF16)


def _attn_call(sinks, q, k, v, bias, ng):
    bsz, l, _ = q.shape
    nb = l // ATT_BLOCK
    cur = lambda w: pl.BlockSpec((1, ATT_BLOCK, w), lambda b, i: (b, i, 0))
    prev = lambda w: pl.BlockSpec((1, ATT_BLOCK, w), lambda b, i: (b, jnp.maximum(i - 1, 0), 0))
    return pl.pallas_call(
        _attn_kernel,
        grid=(bsz, nb),
        in_specs=[pl.BlockSpec(memory_space=pltpu.SMEM),
                  cur(D_ATT), prev(D_KV), cur(D_KV), prev(D_KV), cur(D_KV),
                  pl.BlockSpec(bias.shape, lambda b, i: (0, 0, 0)),
                  pl.BlockSpec(ng.shape, lambda b, i: (0, 0))],
        out_specs=cur(D_ATT),
        out_shape=jax.ShapeDtypeStruct((bsz, l, D_ATT), BF16),
        scratch_shapes=[pltpu.VMEM((ATT_BLOCK, D_ATT), F32)],
        compiler_params=pltpu.CompilerParams(dimension_semantics=("arbitrary", "arbitrary"),
                                             vmem_limit_bytes=VMEM_LIMIT),
    )(sinks, q, k, k, v, v, bias, ng)


def _out_proj_kernel(x_ref, ys_ref, ya_ref, g1_ref, sc_ref, sh_ref, g2_ref, ng_ref, wos_ref, woa_ref,
                     rwh_ref, rwl_ref, sg_ref, su_ref, sd_ref, base_ref, h_ref, lg_ref):
    mix = (jnp.dot(ys_ref[...], wos_ref[...], preferred_element_type=F32)
           + jnp.dot(ya_ref[...], woa_ref[...], preferred_element_type=F32))
    x1 = x_ref[...] + g1_ref[0] * mix
    ms = jnp.mean(x1 * x1, axis=-1, keepdims=True)
    h = x1 * lax.rsqrt(ms + EPS) * ng_ref[...]
    h = h * (1.0 + sc_ref[0]) + sh_ref[0]
    half = h.shape[1] // 2
    h_ref[...] = _pack_bf16_pair(h[:, :half], h[:, half:])
    hi, lo = _split_hi_lo(h)
    logits = (jnp.dot(hi, rwh_ref[...], preferred_element_type=F32)
              + jnp.dot(lo, rwh_ref[...], preferred_element_type=F32)
              + jnp.dot(hi, rwl_ref[...], preferred_element_type=F32))
    lg_ref[...] = logits.T
    u = _silu(jnp.dot(hi, sg_ref[...], preferred_element_type=F32)) * jnp.dot(hi, su_ref[...],
                                                                              preferred_element_type=F32)
    shared = jnp.dot(u.astype(BF16), sd_ref[...], preferred_element_type=F32)
    base_ref[...] = x1 + g2_ref[0] * shared


def _out_proj_call(x2, ys, ya, g1, sc2, sh2, g2, ng, wos, woa, rwh, rwl, sg, su, sd, tiles_per_batch, tm):
    n, d = x2.shape
    row = lambda w: pl.BlockSpec((tm, w), lambda i: (i, 0))
    full = lambda a: pl.BlockSpec(a.shape, lambda i: (0, 0))
    per_batch = pl.BlockSpec((1, 1, d), lambda i: (i // tiles_per_batch, 0, 0))
    return pl.pallas_call(
        _out_proj_kernel,
        grid=(n // tm,),
        in_specs=[row(d), row(D_SSM), row(D_ATT), per_batch, per_batch, per_batch, per_batch, full(ng),
                  full(wos), full(woa), full(rwh), full(rwl), full(sg), full(su), full(sd)],
        out_specs=[row(d), row(d // 2), pl.BlockSpec((LANES, tm), lambda i: (0, i))],
        out_shape=[jax.ShapeDtypeStruct((n, d), F32), jax.ShapeDtypeStruct((n, d // 2), jnp.uint32),
                   jax.ShapeDtypeStruct((LANES, n), F32)],
        compiler_params=pltpu.CompilerParams(dimension_semantics=("arbitrary",),
                                             vmem_limit_bytes=VMEM_LIMIT),
    )(x2, ys, ya, g1, sc2, sh2, g2, ng, wos, woa, rwh, rwl, sg, su, sd)


def _route_kernel(lg_ref, rb_ref, upper_ref, idx_ref, gate_ref, rank_ref, cnt_ref, carry_ref):
    step = pl.program_id(0)

    @pl.when(step == 0)
    def _():
        carry_ref[...] = jnp.zeros_like(carry_ref)

    t = lg_ref.shape[1]
    per_group = N_EXPERTS // ROUTE_GROUPS
    scores = 1.0 / (1.0 + jnp.exp(-lg_ref[0:N_EXPERTS, :]))
    sel = scores + rb_ref[...]
    e_iota = lax.broadcasted_iota(jnp.int32, (N_EXPERTS, t), 0)

    sel3 = sel.reshape(ROUTE_GROUPS, per_group, t)
    w_iota = lax.broadcasted_iota(jnp.int32, sel3.shape, 1)
    m1 = jnp.max(sel3, axis=1, keepdims=True)
    first = jnp.min(jnp.where(sel3 == m1, w_iota, per_group), axis=1, keepdims=True)
    m2 = jnp.max(jnp.where(w_iota == first, NEG_INF, sel3), axis=1, keepdims=True)
    grp = (m1 + m2).reshape(ROUTE_GROUPS, t)

    g_iota = lax.broadcasted_iota(jnp.int32, (ROUTE_GROUPS, t), 0)
    gmask = jnp.zeros((ROUTE_GROUPS, t), jnp.bool_)
    for _ in range(ROUTE_TOPK_GROUPS):
        gm = jnp.max(grp, axis=0, keepdims=True)
        gfirst = jnp.min(jnp.where(grp == gm, g_iota, ROUTE_GROUPS), axis=0, keepdims=True)
        hit = g_iota == gfirst
        gmask = jnp.logical_or(gmask, hit)
        grp = jnp.where(hit, NEG_INF, grp)
    allowed = jnp.broadcast_to(gmask.reshape(ROUTE_GROUPS, 1, t),
                               (ROUTE_GROUPS, per_group, t)).reshape(N_EXPERTS, t)
    masked = jnp.where(allowed, sel, NEG_INF)

    picked = jnp.zeros((N_EXPERTS, t), jnp.bool_)
    idx_rows = []
    w_rows = []
    for _ in range(TOP_K):
        mm = jnp.max(masked, axis=0, keepdims=True)
        efirst = jnp.min(jnp.where(masked == mm, e_iota, N_EXPERTS), axis=0, keepdims=True)
        hit = e_iota == efirst
        idx_rows.append(efirst)
        w_rows.append(jnp.sum(jnp.where(hit, scores, 0.0), axis=0, keepdims=True))
        picked = jnp.logical_or(picked, hit)
        masked = jnp.where(hit, NEG_INF, masked)
    idx = jnp.concatenate(idx_rows, axis=0)
    w = jnp.concatenate(w_rows, axis=0)
    gate_ref[...] = w / jnp.sum(w, axis=0, keepdims=True) * ROUTED_SCALE
    idx_ref[...] = idx

    onehot = jnp.where(picked, 1.0, 0.0)
    before = jnp.dot(onehot.astype(BF16), upper_ref[...], preferred_element_type=F32)
    rank_full = before + carry_ref[:, 0:1]
    rank_rows = [jnp.sum(jnp.where(e_iota == idx_rows[k], rank_full, 0.0), axis=0, keepdims=True)
                 for k in range(TOP_K)]
    rank_ref[...] = jnp.concatenate(rank_rows, axis=0).astype(jnp.int32)
    carry_ref[...] = carry_ref[...] + jnp.sum(onehot, axis=1, keepdims=True)
    cnt_ref[...] = carry_ref[...]


def _route_call(logits_t, router_bias, upper, tile):
    n = logits_t.shape[1]
    tok = lambda r: pl.BlockSpec((r, tile), lambda i: (0, i))
    return pl.pallas_call(
        _route_kernel,
        grid=(n // tile,),
        in_specs=[tok(LANES), pl.BlockSpec((N_EXPERTS, 1), lambda i: (0, 0)),
                  pl.BlockSpec(upper.shape, lambda i: (0, 0))],
        out_specs=[tok(TOP_K), tok(TOP_K), tok(TOP_K), pl.BlockSpec((N_EXPERTS, LANES), lambda i: (0, 0))],
        out_shape=[jax.ShapeDtypeStruct((TOP_K, n), jnp.int32), jax.ShapeDtypeStruct((TOP_K, n), F32),
                   jax.ShapeDtypeStruct((TOP_K, n), jnp.int32),
                   jax.ShapeDtypeStruct((N_EXPERTS, LANES), F32)],
        scratch_shapes=[pltpu.VMEM((N_EXPERTS, LANES), F32)],
        compiler_params=pltpu.CompilerParams(dimension_semantics=("arbitrary",),
                                             vmem_limit_bytes=VMEM_LIMIT),
    )(logits_t, router_bias, upper)


def _dest_kernel(idx_ref, rank_ref, start_ref, dest_ref):
    t = idx_ref.shape[1]
    e_iota = lax.broadcasted_iota(jnp.int32, (N_EXPERTS, t), 0)
    rows = [jnp.sum(jnp.where(e_iota == idx_ref[k:k + 1, :], start_ref[...], 0), axis=0, keepdims=True)
            for k in range(TOP_K)]
    dest_ref[...] = jnp.concatenate(rows, axis=0) + rank_ref[...]


def _dest_call(idx, rank, pad_start, tile):
    n = idx.shape[1]
    tok = pl.BlockSpec((TOP_K, tile), lambda i: (0, i))
    return pl.pallas_call(
        _dest_kernel,
        grid=(n // tile,),
        in_specs=[tok, tok, pl.BlockSpec((N_EXPERTS, 1), lambda i: (0, 0))],
        out_specs=tok,
        out_shape=jax.ShapeDtypeStruct((TOP_K, n), jnp.int32),
        compiler_params=pltpu.CompilerParams(dimension_semantics=("arbitrary",)),
    )(idx, rank, pad_start)


def _scatter_rows_sc(rows, dest_flat, total_rows, chunk):
    n, w = rows.shape
    copies = dest_flat.shape[0] // n
    info = plsc.get_sparse_core_info()
    nc = info.num_cores
    per_worker = n // (nc * info.num_subcores)
    assert per_worker * nc * info.num_subcores == n and per_worker % chunk == 0
    mesh = plsc.VectorSubcoreMesh(core_axis_name="c", subcore_axis_name="s")

    @functools.partial(
        pl.kernel, mesh=mesh,
        out_type=jax.ShapeDtypeStruct((total_rows, w), rows.dtype),
        scratch_types=[pltpu.VMEM((chunk,), jnp.int32), pltpu.VMEM((chunk, w), rows.dtype),
                       pltpu.SemaphoreType.DMA],
    )
    def scatter(rows_hbm, idx_hbm, out_hbm, idx_v, rows_v, sem):
        worker = lax.axis_index("s") * nc + lax.axis_index("c")

        @pl.loop(0, per_worker // chunk)
        def _(j):
            base = worker * per_worker + j * chunk
            pltpu.sync_copy(rows_hbm.at[pl.ds(base, chunk)], rows_v)
            for k in range(copies):
                pltpu.sync_copy(idx_hbm.at[pl.ds(k * n + base, chunk)], idx_v)
                pltpu.async_copy(rows_v, out_hbm.at[idx_v], sem).wait()

    return scatter(rows, dest_flat)


def _expert_kernel(be_ref, nused_ref, fresh_ref, valid_ref, xs_ref, wg_ref, wu_ref, wd_ref, y_ref, wgb_ref, wub_ref,
                   wdb_ref):
    i = pl.program_id(0)

    @pl.when(fresh_ref[i] > 0)
    def _():
        wgb_ref[...] = wg_ref[0].astype(BF16)
        wub_ref[...] = wu_ref[0].astype(BF16)
        wdb_ref[...] = wd_ref[0].astype(BF16)

    @pl.when(i < nused_ref[0])
    def _():
        half = xs_ref.shape[1]
        row = lax.broadcasted_iota(jnp.int32, xs_ref.shape, 0)
        x_lo, x_hi = _unpack_bf16_pair(jnp.where(row < valid_ref[i], xs_ref[...], jnp.uint32(0)))
        x_lo = x_lo.astype(BF16)
        x_hi = x_hi.astype(BF16)
        gate = (jnp.dot(x_lo, wgb_ref[:half, :], preferred_element_type=F32)
                + jnp.dot(x_hi, wgb_ref[half:, :], preferred_element_type=F32))
        up = (jnp.dot(x_lo, wub_ref[:half, :], preferred_element_type=F32)
              + jnp.dot(x_hi, wub_ref[half:, :], preferred_element_type=F32))
        u = (_silu(gate) * up).astype(BF16)
        y_lo = jnp.dot(u, wdb_ref[:, :half], preferred_element_type=F32)
        y_hi = jnp.dot(u, wdb_ref[:, half:], preferred_element_type=F32)
        y_ref[...] = _pack_bf16_pair(y_lo, y_hi)

    @pl.when(i >= nused_ref[0])
    def _():
        y_ref[...] = jnp.zeros_like(y_ref)


def _expert_call(block_expert, nused, fresh, valid, xs, wg, wu, wd, rows):
    p, w = xs.shape
    d, f = wg.shape[1], wg.shape[2]
    x_map = lambda i, be, nu, fr, va: (jnp.minimum(i, nu[0] - 1), 0)
    w_map = lambda i, be, nu, fr, va: (be[i], 0, 0)
    grid_spec = pltpu.PrefetchScalarGridSpec(
        num_scalar_prefetch=4,
        grid=(p // rows,),
        in_specs=[pl.BlockSpec((rows, w), x_map),
                  pl.BlockSpec((1, d, f), w_map), pl.BlockSpec((1, d, f), w_map), pl.BlockSpec((1, f, d), w_map)],
        out_specs=pl.BlockSpec((rows, w), lambda i, be, nu, fr, va: (i, 0)),
        scratch_shapes=[pltpu.VMEM((d, f), BF16), pltpu.VMEM((d, f), BF16), pltpu.VMEM((f, d), BF16)],
    )
    return pl.pallas_call(
        _expert_kernel,
        grid_spec=grid_spec,
        out_shape=jax.ShapeDtypeStruct((p, w), jnp.uint32),
        compiler_params=pltpu.CompilerParams(dimension_semantics=("arbitrary",),
                                             vmem_limit_bytes=VMEM_LIMIT),
    )(block_expert, nused, fresh, valid, xs, wg, wu, wd)


def _gather_rows_sc(table, idx, chunk):
    m = idx.shape[0]
    w = table.shape[1]
    info = plsc.get_sparse_core_info()
    nc = info.num_cores
    per_worker = m // (nc * info.num_subcores)
    assert per_worker * nc * info.num_subcores == m and per_worker % chunk == 0
    mesh = plsc.VectorSubcoreMesh(core_axis_name="c", subcore_axis_name="s")

    @functools.partial(
        pl.kernel, mesh=mesh,
        out_type=jax.ShapeDtypeStruct((m, w), table.dtype),
        scratch_types=[pltpu.VMEM((chunk,), jnp.int32), pltpu.VMEM((chunk, w), table.dtype),
                       pltpu.SemaphoreType.DMA],
    )
    def gather(table_hbm, idx_hbm, out_hbm, idx_v, rows_v, sem):
        worker = lax.axis_index("s") * nc + lax.axis_index("c")

        @pl.loop(0, per_worker // chunk)
        def _(j):
            base = worker * per_worker + j * chunk
            pltpu.sync_copy(idx_hbm.at[pl.ds(base, chunk)], idx_v)
            pltpu.async_copy(table_hbm.at[idx_v], rows_v, sem).wait()
            pltpu.sync_copy(rows_v, out_hbm.at[pl.ds(base, chunk)])

    return gather(table, idx)


def _combine_kernel(yk_ref, gate_ref, base_ref, g2_ref, fg_ref, o_ref):
    t = base_ref.shape[0]
    half = yk_ref.shape[2]
    gates = gate_ref[...]
    r_lo = jnp.zeros((t, half), F32)
    r_hi = jnp.zeros((t, half), F32)
    for k in range(TOP_K):
        y_lo, y_hi = _unpack_bf16_pair(yk_ref[k])
        r_lo = r_lo + gates[:, k:k + 1] * y_lo
        r_hi = r_hi + gates[:, k:k + 1] * y_hi
    g2 = g2_ref[0]
    x_lo = base_ref[:, :half] + g2[:, :half] * r_lo
    x_hi = base_ref[:, half:] + g2[:, half:] * r_hi
    ms = (jnp.sum(x_lo * x_lo, axis=-1, keepdims=True)
          + jnp.sum(x_hi * x_hi, axis=-1, keepdims=True)) * (1.0 / (2 * half))
    inv = lax.rsqrt(ms + EPS)
    o_ref[:, :half] = x_lo * inv * fg_ref[:, :half]
    o_ref[:, half:] = x_hi * inv * fg_ref[:, half:]


def _combine_call(yk, gates_t, base, g2, fg, tiles_per_batch, tile):
    n, d = base.shape
    row = lambda w: pl.BlockSpec((tile, w), lambda i: (i, 0))
    return pl.pallas_call(
        _combine_kernel,
        grid=(n // tile,),
        in_specs=[pl.BlockSpec((TOP_K, tile, yk.shape[2]), lambda i: (0, i, 0)),
                  row(TOP_K), row(d),
                  pl.BlockSpec((1, 1, d), lambda i: (i // tiles_per_batch, 0, 0)),
                  pl.BlockSpec((1, d), lambda i: (0, 0))],
        out_specs=row(d),
        out_shape=jax.ShapeDtypeStruct((n, d), F32),
        compiler_params=pltpu.CompilerParams(dimension_semantics=("arbitrary",),
                                             vmem_limit_bytes=VMEM_LIMIT),
    )(yk, gates_t, base, g2, fg)


def _pad_cols(a, width):
    return jnp.pad(a, ((0, 0), (0, width - a.shape[1])))


def _layer(x, c_pad, mod_w, mod_b, norm1_g, norm2_g, w_in, conv_w, conv_b, dt_bias, a_log, d_skip, ssm_norm_g,
           att_norm_g, sinks, rel_bias, w_out, router_w, router_bias, exp_w_gate, exp_w_up, exp_w_down,
           sh_w_gate, sh_w_up, sh_w_down, final_g):
    bsz, l, d = x.shape
    n = bsz * l
    tm = min(ROW_TILE, l)

    mod = _mod_call(c_pad, mod_w, mod_b[None, :])[:bsz]
    sh1, sc1, g1, sh2, sc2, g2 = [m[:, None, :] for m in jnp.split(mod, 6, axis=-1)]

    i1 = D_SSM
    i2 = i1 + CONV_CH
    i3 = i2 + SSM_HEADS
    i4 = i3 + D_ATT
    i5 = i4 + D_KV
    wz, wx, wdt, wq, wk, wv = jnp.split(w_in, [i1, i2, i3, i4, i5], axis=-1)
    wdt = _pad_cols(wdt, LANES)
    x2 = x.reshape(n, d)
    z, xbc, dt, q, k, v = _in_proj_call(x2, sc1, sh1, norm1_g[None, :], wz.astype(BF16), wx.astype(BF16),
                                        wdt.astype(BF16), wq.astype(BF16), wk.astype(BF16), wv.astype(BF16),
                                        l // tm, tm)

    tril = jnp.asarray(np.tril(np.ones((CHUNK, CHUNK), np.float32)))
    y_ssm = _ssd_call(xbc.reshape(bsz, l, CONV_CH), z.reshape(bsz, l, D_SSM), dt.reshape(bsz, l, LANES),
                      conv_w, conv_b[None, :], _pad_cols(dt_bias[None, :], LANES), _pad_cols(a_log[None, :], LANES),
                      jnp.repeat(d_skip, SSM_HEAD_DIM)[None, :], ssm_norm_g[None, :], tril)

    bucket, win = _rel_bucket_table()
    bias = _bias_call(rel_bias, jnp.asarray(bucket), jnp.asarray(win))
    y_att = _attn_call(sinks, q.reshape(bsz, l, D_ATT), k.reshape(bsz, l, D_KV), v.reshape(bsz, l, D_KV), bias,
                       att_norm_g[None, :])

    rw = _pad_cols(router_w, LANES)
    rwh = rw.astype(BF16)
    rwl = (rw - rwh.astype(F32)).astype(BF16)
    base, h2, logits_t = _out_proj_call(
        x2, y_ssm.reshape(n, D_SSM), y_att.reshape(n, D_ATT), g1, sc2, sh2, g2, norm2_g[None, :],
        w_out[:D_SSM].astype(BF16), w_out[D_SSM:].astype(BF16), rwh, rwl,
        sh_w_gate.astype(BF16), sh_w_up.astype(BF16), sh_w_down.astype(BF16), l // tm, tm)

    rt = min(ROUTE_TILE, n)
    upper = jnp.asarray(np.triu(np.ones((rt, rt), np.float32), 1)).astype(BF16)
    idx, gates, rank, counts = _route_call(logits_t, router_bias[:, None], upper, rt)

    nblocks = (n * TOP_K + N_EXPERTS * (EXPERT_ROWS - 1) + EXPERT_ROWS - 1) // EXPERT_ROWS
    cnt = counts[:, 0].astype(jnp.int32)
    padded = (cnt + EXPERT_ROWS - 1) // EXPERT_ROWS * EXPERT_ROWS
    pad_end = jnp.cumsum(padded)
    pad_start = pad_end - padded
    dest = _dest_call(idx, rank, pad_start[:, None], rt)
    block_first_row = jnp.arange(nblocks, dtype=jnp.int32) * EXPERT_ROWS
    block_expert = jnp.minimum(jnp.sum((pad_end[None, :] <= block_first_row[:, None]).astype(jnp.int32), axis=1),
                               N_EXPERTS - 1)
    fresh = jnp.concatenate([jnp.ones((1,), jnp.int32),
                             (block_expert[1:] != block_expert[:-1]).astype(jnp.int32)])
    nused = (pad_end[-1:] // EXPERT_ROWS).astype(jnp.int32)
    seg_row = block_first_row - jnp.sum(jnp.where(pad_end[None, :] <= block_first_row[:, None], padded[None, :], 0),
                                        axis=1)
    block_cnt = jnp.sum(jnp.where(block_expert[:, None] == jnp.arange(N_EXPERTS, dtype=jnp.int32)[None, :],
                                  cnt[None, :], 0), axis=1)
    valid = jnp.clip(block_cnt - seg_row, 0, EXPERT_ROWS).astype(jnp.int32)

    dest_flat = dest.reshape(-1)
    xs = _scatter_rows_sc(h2, dest_flat, nblocks * EXPERT_ROWS, SC_CHUNK)
    ys = _expert_call(block_expert, nused, fresh, valid, xs, exp_w_gate, exp_w_up, exp_w_down, EXPERT_ROWS)
    yk = _gather_rows_sc(ys, dest_flat, SC_CHUNK).reshape(TOP_K, n, ys.shape[1])
    ctile = min(COMBINE_TILE, l)
    out = _combine_call(yk, gates.T, base, g2, final_g[None, :], l // ctile, ctile)
    return out.reshape(bsz, l, d)


def kernel(x, c, mod_w, mod_b, norm1_g, norm2_g, w_in, conv_w, conv_b, dt_bias, a_log, d_skip, ssm_norm_g,
           att_norm_g, sinks, rel_bias, w_out, router_w, router_bias, exp_w_gate, exp_w_up, exp_w_down,
           sh_w_gate, sh_w_up, sh_w_down, final_g):
    assert mod_w.shape[0] == 1, "single-layer block"
    bsz = x.shape[0]
    c_pad = jnp.pad(c, ((0, SUBLANES - bsz % SUBLANES if bsz % SUBLANES else 0), (0, 0)))
    return _layer(x, c_pad, mod_w[0], mod_b[0], norm1_g[0], norm2_g[0], w_in[0], conv_w[0], conv_b[0],
                  dt_bias[0], a_log[0], d_skip[0], ssm_norm_g[0], att_norm_g[0], sinks[0], rel_bias, w_out[0],
                  router_w[0], router_bias[0], exp_w_gate[0], exp_w_up[0], exp_w_down[0], sh_w_gate[0],
                  sh_w_up[0], sh_w_down[0], final_g)
```

```python
import functools
import math

import numpy as np
import jax
import jax.numpy as jnp
from jax import lax
from jax.experimental import pallas as pl
from jax.experimental.pallas import tpu as pltpu
from jax.experimental.pallas import tpu_sc as plsc

F32 = jnp.float32
BF16 = jnp.bfloat16

D_MODEL = 1024
SSM_HEAD_DIM = 64
D_SSM = D_MODEL
SSM_HEADS = D_SSM // SSM_HEAD_DIM
SSM_GROUPS = 4
D_STATE = 128
CONV_K = 4
CONV_CH = D_SSM + 2 * SSM_GROUPS * D_STATE
CHUNK = 128
ATT_HEAD_DIM = 64
D_ATT = D_MODEL
ATT_HEADS = D_ATT // ATT_HEAD_DIM
KV_HEADS = ATT_HEADS // 4
Q_PER_KV = ATT_HEADS // KV_HEADS
D_KV = KV_HEADS * ATT_HEAD_DIM
WINDOW = 128
ATT_BLOCK = 128
REL_BUCKETS = 32
REL_MAX_DIST = 128
N_EXPERTS = 64
TOP_K = 8
EXPERT_DIM = D_MODEL // 4
SHARED_DIM = D_MODEL // 4
ROUTE_GROUPS = 8
ROUTE_TOPK_GROUPS = 4
ROUTED_SCALE = 2.5
EPS = 1e-6

LANES = 128
SUBLANES = 8
HALF = LANES // 2

ROW_TILE = 512
ROUTE_TILE = 512
COMBINE_TILE = 256
SC_CHUNK = 128
EXPERT_ROWS = 512
VMEM_LIMIT = 48 * 1024 * 1024

NEG_INF = float("-inf")


def _silu(v):
    return v * (1.0 / (1.0 + jnp.exp(-v)))


def _softplus(v):
    return jnp.maximum(v, 0.0) + jnp.log(1.0 + jnp.exp(-jnp.abs(v)))


def _bdot(a, b):
    return jnp.dot(a.astype(BF16), b.astype(BF16), preferred_element_type=F32)


def _split_hi_lo(v):
    hi = v.astype(BF16)
    lo = (v - hi.astype(F32)).astype(BF16)
    return hi, lo


def _pack_bf16_pair(a, b):
    ar = lax.bitcast_convert_type(a.astype(BF16).astype(F32), jnp.uint32)
    br = lax.bitcast_convert_type(b.astype(BF16).astype(F32), jnp.uint32)
    return lax.shift_right_logical(ar, jnp.uint32(16)) | br


def _unpack_bf16_pair(w):
    a = lax.bitcast_convert_type(lax.shift_left(w, jnp.uint32(16)), F32)
    b = lax.bitcast_convert_type(w & jnp.uint32(0xFFFF0000), F32)
    return a, b


def _lane_half_mask(shape):
    return lax.broadcasted_iota(jnp.int32, shape, len(shape) - 1) < HALF


def _mod_kernel(c_ref, w_ref, b_ref, o_ref):
    a = _silu(c_ref[...])
    o_ref[...] = jnp.dot(a, w_ref[...], precision=lax.Precision.HIGHEST,
                         preferred_element_type=F32) + b_ref[...]


def _mod_call(c_pad, mod_w, mod_b):
    rows, d = c_pad.shape
    cols = mod_w.shape[1]
    return pl.pallas_call(
        _mod_kernel,
        grid=(cols // d,),
        in_specs=[pl.BlockSpec((rows, d), lambda j: (0, 0)),
                  pl.BlockSpec((d, d), lambda j: (0, j)),
                  pl.BlockSpec((1, d), lambda j: (0, j))],
        out_specs=pl.BlockSpec((rows, d), lambda j: (0, j)),
        out_shape=jax.ShapeDtypeStruct((rows, cols), F32),
        compiler_params=pltpu.CompilerParams(dimension_semantics=("arbitrary",),
                                             vmem_limit_bytes=VMEM_LIMIT),
    )(c_pad, mod_w, mod_b)


def _in_proj_kernel(x_ref, sc_ref, sh_ref, g_ref, wz_ref, wx_ref, wdt_ref, wq_ref, wk_ref, wv_ref,
                    z_ref, xbc_ref, dt_ref, q_ref, k_ref, v_ref):
    xf = x_ref[...]
    ms = jnp.mean(xf * xf, axis=-1, keepdims=True)
    h = xf * lax.rsqrt(ms + EPS) * g_ref[...]
    h = h * (1.0 + sc_ref[0]) + sh_ref[0]
    hb = h.astype(BF16)
    z_ref[...] = jnp.dot(hb, wz_ref[...], preferred_element_type=F32).astype(BF16)
    xbc_ref[...] = jnp.dot(hb, wx_ref[...], preferred_element_type=F32).astype(BF16)
    dt_ref[...] = jnp.dot(hb, wdt_ref[...], preferred_element_type=F32)
    q_ref[...] = jnp.dot(hb, wq_ref[...], preferred_element_type=F32).astype(BF16)
    k_ref[...] = jnp.dot(hb, wk_ref[...], preferred_element_type=F32).astype(BF16)
    v_ref[...] = jnp.dot(hb, wv_ref[...], preferred_element_type=F32).astype(BF16)


def _in_proj_call(x2, sc1, sh1, g1n, wz, wx, wdt, wq, wk, wv, tiles_per_batch, tm):
    n, d = x2.shape
    row = lambda w: pl.BlockSpec((tm, w), lambda i: (i, 0))
    full = lambda a: pl.BlockSpec(a.shape, lambda i: (0, 0))
    per_batch = pl.BlockSpec((1, 1, d), lambda i: (i // tiles_per_batch, 0, 0))
    outs = [(wz.shape[1], BF16), (wx.shape[1], BF16), (wdt.shape[1], F32),
            (wq.shape[1], BF16), (wk.shape[1], BF16), (wv.shape[1], BF16)]
    return pl.pallas_call(
        _in_proj_kernel,
        grid=(n // tm,),
        in_specs=[row(d), per_batch, per_batch, full(g1n), full(wz), full(wx), full(wdt), full(wq),
                  full(wk), full(wv)],
        out_specs=[row(w) for w, _ in outs],
        out_shape=[jax.ShapeDtypeStruct((n, w), dt) for w, dt in outs],
        compiler_params=pltpu.CompilerParams(dimension_semantics=("arbitrary",),
                                             vmem_limit_bytes=VMEM_LIMIT),
    )(x2, sc1, sh1, g1n, wz, wx, wdt, wq, wk, wv)


def _ssd_kernel(xbc_ref, z_ref, dt_ref, cw_ref, cb_ref, dtb_ref, alog_ref, dskip_ref, ng_ref, tril_ref,
                y_ref, state_ref, tail_ref, ext_ref, ybuf_ref):
    c = pl.program_id(1)

    @pl.when(c == 0)
    def _():
        state_ref[...] = jnp.zeros_like(state_ref)
        tail_ref[...] = jnp.zeros_like(tail_ref)

    u = xbc_ref[0].astype(F32)
    ext_ref[0:SUBLANES, :] = tail_ref[...]
    ext_ref[SUBLANES:SUBLANES + CHUNK, :] = u
    tail_ref[...] = u[CHUNK - SUBLANES:, :]
    acc = cb_ref[...] + jnp.zeros_like(u)
    for kk in range(CONV_K):
        off = SUBLANES - (CONV_K - 1) + kk
        acc = acc + cw_ref[kk:kk + 1, :] * ext_ref[off:off + CHUNK, :]
    act = _silu(acc)
    xs = act[:, :D_SSM]
    gn = SSM_GROUPS * D_STATE

    dt = _softplus(dt_ref[0] + dtb_ref[...])
    a = dt * (-jnp.exp(alog_ref[...]))
    cs = jnp.dot(tril_ref[...], a, precision=lax.Precision.HIGHEST, preferred_element_type=F32)
    cs_t = cs.T
    dt_t = dt.T
    exp_cs = jnp.exp(cs)
    cs_last = cs[CHUNK - 1:CHUNK, :]
    chunk_decay = jnp.exp(cs_last)

    li = lax.broadcasted_iota(jnp.int32, (CHUNK, CHUNK), 0)
    si = lax.broadcasted_iota(jnp.int32, (CHUNK, CHUNK), 1)
    causal = li >= si
    low = _lane_half_mask((CHUNK, LANES))
    low_row = _lane_half_mask((1, LANES))

    heads_per_group = SSM_HEADS // SSM_GROUPS
    for g in range(SSM_GROUPS):
        b_g = act[:, D_SSM + g * D_STATE:D_SSM + (g + 1) * D_STATE]
        c_g = act[:, D_SSM + gn + g * D_STATE:D_SSM + gn + (g + 1) * D_STATE]
        b_gb = b_g.astype(BF16)
        c_gb = c_g.astype(BF16)
        cb = lax.dot_general(c_gb, b_gb, (((1,), (1,)), ((), ())), preferred_element_type=F32)
        b_t = b_g.T
        for jp in range(heads_per_group // 2):
            j = g * (heads_per_group // 2) + jp
            lanes = slice(j * LANES, (j + 1) * LANES)
            xp = xs[:, lanes]
            ydiag = jnp.zeros((CHUNK, LANES), F32)
            snew = jnp.zeros((D_STATE, LANES), F32)
            for half in range(2):
                h = 2 * j + half
                row_dt = dt_t[h:h + 1, :]
                diff = cs[:, h:h + 1] - cs_t[h:h + 1, :]
                lmat = jnp.exp(jnp.where(causal, diff, NEG_INF))
                m = (cb * lmat * row_dt).astype(BF16)
                keep = low if half == 0 else jnp.logical_not(low)
                xh = jnp.where(keep, xp, 0.0).astype(BF16)
                ydiag = ydiag + jnp.dot(m, xh, preferred_element_type=F32)
                w_t = jnp.exp(cs_t[h:h + 1, CHUNK - 1:CHUNK] - cs_t[h:h + 1, :]) * row_dt
                snew = snew + jnp.dot((b_t * w_t).astype(BF16), xh, preferred_element_type=F32)
            s_in = state_ref[:, lanes]
            yoff = jnp.dot(c_gb, s_in.astype(BF16), preferred_element_type=F32)
            h0 = 2 * j
            escale = jnp.where(low, exp_cs[:, h0:h0 + 1], exp_cs[:, h0 + 1:h0 + 2])
            cdec = jnp.where(low_row, chunk_decay[:, h0:h0 + 1], chunk_decay[:, h0 + 1:h0 + 2])
            ybuf_ref[:, lanes] = ydiag + yoff * escale + xp * dskip_ref[:, lanes]
            state_ref[:, lanes] = s_in * cdec + snew

    yz = ybuf_ref[...] * _silu(z_ref[0].astype(F32))
    gw = D_SSM // SSM_GROUPS
    for g in range(SSM_GROUPS):
        part = yz[:, g * gw:(g + 1) * gw]
        ms = jnp.mean(part * part, axis=-1, keepdims=True)
        y_ref[0, :, g * gw:(g + 1) * gw] = (part * lax.rsqrt(ms + EPS)
                                            * ng_ref[:, g * gw:(g + 1) * gw]).astype(BF16)


def _ssd_call(xbc, z, dt, conv_w, conv_b, dtb, alog, dskip, ng, tril):
    bsz, l, _ = xbc.shape
    nc = l // CHUNK
    chunk = lambda w: pl.BlockSpec((1, CHUNK, w), lambda b, c: (b, c, 0))
    full = lambda a: pl.BlockSpec(a.shape, lambda b, c: (0, 0))
    return pl.pallas_call(
        _ssd_kernel,
        grid=(bsz, nc),
        in_specs=[chunk(CONV_CH), chunk(D_SSM), chunk(LANES), full(conv_w), full(conv_b), full(dtb),
                  full(alog), full(dskip), full(ng), full(tril)],
        out_specs=chunk(D_SSM),
        out_shape=jax.ShapeDtypeStruct((bsz, l, D_SSM), BF16),
        scratch_shapes=[pltpu.VMEM((D_STATE, D_SSM), F32),
                        pltpu.VMEM((SUBLANES, CONV_CH), F32),
                        pltpu.VMEM((SUBLANES + CHUNK, CONV_CH), F32),
                        pltpu.VMEM((CHUNK, D_SSM), F32)],
        compiler_params=pltpu.CompilerParams(dimension_semantics=("arbitrary", "arbitrary"),
                                             vmem_limit_bytes=VMEM_LIMIT),
    )(xbc, z, dt, conv_w, conv_b, dtb, alog, dskip, ng, tril)


def _rel_bucket_table():
    qi = np.arange(ATT_BLOCK)[:, None]
    sj = np.arange(2 * ATT_BLOCK)[None, :]
    dist = qi + ATT_BLOCK - sj
    in_win = (dist >= 0) & (dist < WINDOW)
    dcl = np.maximum(dist, 0)
    max_exact = REL_BUCKETS // 2
    d = np.maximum(dcl, 1).astype(np.float32)
    large = max_exact + (np.log(d / np.float32(max_exact)) / np.float32(math.log(REL_MAX_DIST / max_exact))
                         * np.float32(REL_BUCKETS - max_exact)).astype(np.int32)
    large = np.minimum(large, REL_BUCKETS - 1)
    bucket = np.where(dcl < max_exact, dcl, large).astype(np.int32)
    return bucket, in_win.astype(np.int32)


def _bias_kernel(rb_ref, bucket_ref, win_ref, o_ref):
    h = pl.program_id(0)
    bucket = bucket_ref[...]
    acc = jnp.zeros(bucket.shape, F32)
    for b in range(REL_BUCKETS):
        acc = jnp.where(bucket == b, rb_ref[b, h], acc)
    o_ref[0] = jnp.where(win_ref[...] > 0, acc, NEG_INF)


def _bias_call(rel_bias, bucket, win):
    return pl.pallas_call(
        _bias_kernel,
        grid=(ATT_HEADS,),
        in_specs=[pl.BlockSpec(memory_space=pltpu.SMEM),
                  pl.BlockSpec(bucket.shape, lambda h: (0, 0)),
                  pl.BlockSpec(win.shape, lambda h: (0, 0))],
        out_specs=pl.BlockSpec((1,) + bucket.shape, lambda h: (h, 0, 0)),
        out_shape=jax.ShapeDtypeStruct((ATT_HEADS,) + bucket.shape, F32),
        compiler_params=pltpu.CompilerParams(dimension_semantics=("arbitrary",)),
    )(rel_bias, bucket, win)


def _attn_kernel(sink_ref, q_ref, kp_ref, kc_ref, vp_ref, vc_ref, bias_ref, ng_ref, o_ref, obuf_ref):
    i = pl.program_id(1)
    kband = jnp.concatenate([kp_ref[0], kc_ref[0]], axis=0).astype(F32)
    vband = jnp.concatenate([vp_ref[0], vc_ref[0]], axis=0).astype(F32)
    sj = lax.broadcasted_iota(jnp.int32, (ATT_BLOCK, 2 * ATT_BLOCK), 1)
    valid = jnp.logical_or(sj >= ATT_BLOCK, i > 0)
    low = _lane_half_mask((ATT_BLOCK, LANES))
    scale = ATT_HEAD_DIM ** -0.5

    kv_cols = []
    for cpair in range(KV_HEADS // 2):
        kk = kband[:, cpair * LANES:(cpair + 1) * LANES]
        vv = vband[:, cpair * LANES:(cpair + 1) * LANES]
        kv_cols.append(((kk.astype(BF16), pltpu.roll(kk, HALF, 1).astype(BF16)),
                        (vv.astype(BF16), pltpu.roll(vv, HALF, 1).astype(BF16))))

    for j in range(ATT_HEADS // 2):
        qp = q_ref[0, :, j * LANES:(j + 1) * LANES]
        out_pair = jnp.zeros((ATT_BLOCK, LANES), F32)
        for half in range(2):
            h = 2 * j + half
            g = h // Q_PER_KV
            swapped = int((g % 2) != half)
            ksel = kv_cols[g // 2][0][swapped]
            vsel = kv_cols[g // 2][1][swapped]
            keep = low if half == 0 else jnp.logical_not(low)
            qh = jnp.where(keep, qp, jnp.zeros_like(qp))
            s = lax.dot_general(qh, ksel, (((1,), (1,)), ((), ())), preferred_element_type=F32)
            s = s * scale + bias_ref[h]
            s = jnp.where(valid, s, NEG_INF)
            sink = sink_ref[h]
            m = jnp.maximum(jnp.max(s, axis=-1, keepdims=True), sink)
            p = jnp.exp(s - m)
            denom = jnp.sum(p, axis=-1, keepdims=True) + jnp.exp(sink - m)
            o = jnp.dot(p.astype(BF16), vsel, preferred_element_type=F32) / denom
            out_pair = out_pair + jnp.where(keep, o, 0.0)
        obuf_ref[:, j * LANES:(j + 1) * LANES] = out_pair

    att = obuf_ref[...]
    ms = jnp.mean(att * att, axis=-1, keepdims=True)
    o_ref[0] = (att * lax.rsqrt(ms + EPS) * ng_ref[...]).astype(BF16)


def _attn_call(sinks, q, k, v, bias, ng):
    bsz, l, _ = q.shape
    nb = l // ATT_BLOCK
    cur = lambda w: pl.BlockSpec((1, ATT_BLOCK, w), lambda b, i: (b, i, 0))
    prev = lambda w: pl.BlockSpec((1, ATT_BLOCK, w), lambda b, i: (b, jnp.maximum(i - 1, 0), 0))
    return pl.pallas_call(
        _attn_kernel,
        grid=(bsz, nb),
        in_specs=[pl.BlockSpec(memory_space=pltpu.SMEM),
                  cur(D_ATT), prev(D_KV), cur(D_KV), prev(D_KV), cur(D_KV),
                  pl.BlockSpec(bias.shape, lambda b, i: (0, 0, 0)),
                  pl.BlockSpec(ng.shape, lambda b, i: (0, 0))],
        out_specs=cur(D_ATT),
        out_shape=jax.ShapeDtypeStruct((bsz, l, D_ATT), BF16),
        scratch_shapes=[pltpu.VMEM((ATT_BLOCK, D_ATT), F32)],
        compiler_params=pltpu.CompilerParams(dimension_semantics=("arbitrary", "arbitrary"),
                                             vmem_limit_bytes=VMEM_LIMIT),
    )(sinks, q, k, k, v, v, bias, ng)


def _out_proj_kernel(x_ref, ys_ref, ya_ref, g1_ref, sc_ref, sh_ref, g2_ref, ng_ref, wos_ref, woa_ref,
                     rwh_ref, rwl_ref, sg_ref, su_ref, sd_ref, base_ref, h_ref, lg_ref):
    mix = (jnp.dot(ys_ref[...], wos_ref[...], preferred_element_type=F32)
           + jnp.dot(ya_ref[...], woa_ref[...], preferred_element_type=F32))
    x1 = x_ref[...] + g1_ref[0] * mix
    ms = jnp.mean(x1 * x1, axis=-1, keepdims=True)
    h = x1 * lax.rsqrt(ms + EPS) * ng_ref[...]
    h = h * (1.0 + sc_ref[0]) + sh_ref[0]
    half = h.shape[1] // 2
    h_ref[...] = _pack_bf16_pair(h[:, :half], h[:, half:])
    hi, lo = _split_hi_lo(h)
    logits = (jnp.dot(hi, rwh_ref[...], preferred_element_type=F32)
              + jnp.dot(lo, rwh_ref[...], preferred_element_type=F32)
              + jnp.dot(hi, rwl_ref[...], preferred_element_type=F32))
    lg_ref[...] = logits.T
    u = _silu(jnp.dot(hi, sg_ref[...], preferred_element_type=F32)) * jnp.dot(hi, su_ref[...],
                                                                              preferred_element_type=F32)
    shared = jnp.dot(u.astype(BF16), sd_ref[...], preferred_element_type=F32)
    base_ref[...] = x1 + g2_ref[0] * shared


def _out_proj_call(x2, ys, ya, g1, sc2, sh2, g2, ng, wos, woa, rwh, rwl, sg, su, sd, tiles_per_batch, tm):
    n, d = x2.shape
    row = lambda w: pl.BlockSpec((tm, w), lambda i: (i, 0))
    full = lambda a: pl.BlockSpec(a.shape, lambda i: (0, 0))
    per_batch = pl.BlockSpec((1, 1, d), lambda i: (i // tiles_per_batch, 0, 0))
    return pl.pallas_call(
        _out_proj_kernel,
        grid=(n // tm,),
        in_specs=[row(d), row(D_SSM), row(D_ATT), per_batch, per_batch, per_batch, per_batch, full(ng),
                  full(wos), full(woa), full(rwh), full(rwl), full(sg), full(su), full(sd)],
        out_specs=[row(d), row(d // 2), pl.BlockSpec((LANES, tm), lambda i: (0, i))],
        out_shape=[jax.ShapeDtypeStruct((n, d), F32), jax.ShapeDtypeStruct((n, d // 2), jnp.uint32),
                   jax.ShapeDtypeStruct((LANES, n), F32)],
        compiler_params=pltpu.CompilerParams(dimension_semantics=("arbitrary",),
                                             vmem_limit_bytes=VMEM_LIMIT),
    )(x2, ys, ya, g1, sc2, sh2, g2, ng, wos, woa, rwh, rwl, sg, su, sd)


def _route_kernel(lg_ref, rb_ref, upper_ref, idx_ref, gate_ref, rank_ref, cnt_ref, carry_ref):
    step = pl.program_id(0)

    @pl.when(step == 0)
    def _():
        carry_ref[...] = jnp.zeros_like(carry_ref)

    t = lg_ref.shape[1]
    per_group = N_EXPERTS // ROUTE_GROUPS
    scores = 1.0 / (1.0 + jnp.exp(-lg_ref[0:N_EXPERTS, :]))
    sel = scores + rb_ref[...]
    e_iota = lax.broadcasted_iota(jnp.int32, (N_EXPERTS, t), 0)

    sel3 = sel.reshape(ROUTE_GROUPS, per_group, t)
    w_iota = lax.broadcasted_iota(jnp.int32, sel3.shape, 1)
    m1 = jnp.max(sel3, axis=1, keepdims=True)
    first = jnp.min(jnp.where(sel3 == m1, w_iota, per_group), axis=1, keepdims=True)
    m2 = jnp.max(jnp.where(w_iota == first, NEG_INF, sel3), axis=1, keepdims=True)
    grp = (m1 + m2).reshape(ROUTE_GROUPS, t)

    g_iota = lax.broadcasted_iota(jnp.int32, (ROUTE_GROUPS, t), 0)
    gmask = jnp.zeros((ROUTE_GROUPS, t), jnp.bool_)
    for _ in range(ROUTE_TOPK_GROUPS):
        gm = jnp.max(grp, axis=0, keepdims=True)
        gfirst = jnp.min(jnp.where(grp == gm, g_iota, ROUTE_GROUPS), axis=0, keepdims=True)
        hit = g_iota == gfirst
        gmask = jnp.logical_or(gmask, hit)
        grp = jnp.where(hit, NEG_INF, grp)
    allowed = jnp.broadcast_to(gmask.reshape(ROUTE_GROUPS, 1, t),
                               (ROUTE_GROUPS, per_group, t)).reshape(N_EXPERTS, t)
    masked = jnp.where(allowed, sel, NEG_INF)

    picked = jnp.zeros((N_EXPERTS, t), jnp.bool_)
    idx_rows = []
    w_rows = []
    for _ in range(TOP_K):
        mm = jnp.max(masked, axis=0, keepdims=True)
        efirst = jnp.min(jnp.where(masked == mm, e_iota, N_EXPERTS), axis=0, keepdims=True)
        hit = e_iota == efirst
        idx_rows.append(efirst)
        w_rows.append(jnp.sum(jnp.where(hit, scores, 0.0), axis=0, keepdims=True))
        picked = jnp.logical_or(picked, hit)
        masked = jnp.where(hit, NEG_INF, masked)
    idx = jnp.concatenate(idx_rows, axis=0)
    w = jnp.concatenate(w_rows, axis=0)
    gate_ref[...] = w / jnp.sum(w, axis=0, keepdims=True) * ROUTED_SCALE
    idx_ref[...] = idx

    onehot = jnp.where(picked, 1.0, 0.0)
    before = jnp.dot(onehot.astype(BF16), upper_ref[...], preferred_element_type=F32)
    rank_full = before + carry_ref[:, 0:1]
    rank_rows = [jnp.sum(jnp.where(e_iota == idx_rows[k], rank_full, 0.0), axis=0, keepdims=True)
                 for k in range(TOP_K)]
    rank_ref[...] = jnp.concatenate(rank_rows, axis=0).astype(jnp.int32)
    carry_ref[...] = carry_ref[...] + jnp.sum(onehot, axis=1, keepdims=True)
    cnt_ref[...] = carry_ref[...]


def _route_call(logits_t, router_bias, upper, tile):
    n = logits_t.shape[1]
    tok = lambda r: pl.BlockSpec((r, tile), lambda i: (0, i))
    return pl.pallas_call(
        _route_kernel,
        grid=(n // tile,),
        in_specs=[tok(LANES), pl.BlockSpec((N_EXPERTS, 1), lambda i: (0, 0)),
                  pl.BlockSpec(upper.shape, lambda i: (0, 0))],
        out_specs=[tok(TOP_K), tok(TOP_K), tok(TOP_K), pl.BlockSpec((N_EXPERTS, LANES), lambda i: (0, 0))],
        out_shape=[jax.ShapeDtypeStruct((TOP_K, n), jnp.int32), jax.ShapeDtypeStruct((TOP_K, n), F32),
                   jax.ShapeDtypeStruct((TOP_K, n), jnp.int32),
                   jax.ShapeDtypeStruct((N_EXPERTS, LANES), F32)],
        scratch_shapes=[pltpu.VMEM((N_EXPERTS, LANES), F32)],
        compiler_params=pltpu.CompilerParams(dimension_semantics=("arbitrary",),
                                             vmem_limit_bytes=VMEM_LIMIT),
    )(logits_t, router_bias, upper)


def _dest_kernel(idx_ref, rank_ref, start_ref, dest_ref):
    t = idx_ref.shape[1]
    e_iota = lax.broadcasted_iota(jnp.int32, (N_EXPERTS, t), 0)
    rows = [jnp.sum(jnp.where(e_iota == idx_ref[k:k + 1, :], start_ref[...], 0), axis=0, keepdims=True)
            for k in range(TOP_K)]
    dest_ref[...] = jnp.concatenate(rows, axis=0) + rank_ref[...]


def _dest_call(idx, rank, pad_start, tile):
    n = idx.shape[1]
    tok = pl.BlockSpec((TOP_K, tile), lambda i: (0, i))
    return pl.pallas_call(
        _dest_kernel,
        grid=(n // tile,),
        in_specs=[tok, tok, pl.BlockSpec((N_EXPERTS, 1), lambda i: (0, 0))],
        out_specs=tok,
        out_shape=jax.ShapeDtypeStruct((TOP_K, n), jnp.int32),
        compiler_params=pltpu.CompilerParams(dimension_semantics=("arbitrary",)),
    )(idx, rank, pad_start)


def _scatter_rows_sc(rows, dest_flat, total_rows, chunk):
    n, w = rows.shape
    copies = dest_flat.shape[0] // n
    info = plsc.get_sparse_core_info()
    nc = info.num_cores
    per_worker = n // (nc * info.num_subcores)
    assert per_worker * nc * info.num_subcores == n and per_worker % chunk == 0
    mesh = plsc.VectorSubcoreMesh(core_axis_name="c", subcore_axis_name="s")

    @functools.partial(
        pl.kernel, mesh=mesh,
        out_type=jax.ShapeDtypeStruct((total_rows, w), rows.dtype),
        scratch_types=[pltpu.VMEM((chunk,), jnp.int32), pltpu.VMEM((chunk, w), rows.dtype),
                       pltpu.SemaphoreType.DMA],
    )
    def scatter(rows_hbm, idx_hbm, out_hbm, idx_v, rows_v, sem):
        worker = lax.axis_index("s") * nc + lax.axis_index("c")

        @pl.loop(0, per_worker // chunk)
        def _(j):
            base = worker * per_worker + j * chunk
            pltpu.sync_copy(rows_hbm.at[pl.ds(base, chunk)], rows_v)
            for k in range(copies):
                pltpu.sync_copy(idx_hbm.at[pl.ds(k * n + base, chunk)], idx_v)
                pltpu.async_copy(rows_v, out_hbm.at[idx_v], sem).wait()

    return scatter(rows, dest_flat)


def _expert_kernel(be_ref, nused_ref, fresh_ref, valid_ref, xs_ref, wg_ref, wu_ref, wd_ref, y_ref, wgb_ref, wub_ref,
                   wdb_ref):
    i = pl.program_id(0)

    @pl.when(fresh_ref[i] > 0)
    def _():
        wgb_ref[...] = wg_ref[0].astype(BF16)
        wub_ref[...] = wu_ref[0].astype(BF16)
        wdb_ref[...] = wd_ref[0].astype(BF16)

    @pl.when(i < nused_ref[0])
    def _():
        half = xs_ref.shape[1]
        row = lax.broadcasted_iota(jnp.int32, xs_ref.shape, 0)
        x_lo, x_hi = _unpack_bf16_pair(jnp.where(row < valid_ref[i], xs_ref[...], jnp.uint32(0)))
        x_lo = x_lo.astype(BF16)
        x_hi = x_hi.astype(BF16)
        gate = (jnp.dot(x_lo, wgb_ref[:half, :], preferred_element_type=F32)
                + jnp.dot(x_hi, wgb_ref[half:, :], preferred_element_type=F32))
        up = (jnp.dot(x_lo, wub_ref[:half, :], preferred_element_type=F32)
              + jnp.dot(x_hi, wub_ref[half:, :], preferred_element_type=F32))
        u = (_silu(gate) * up).astype(BF16)
        y_lo = jnp.dot(u, wdb_ref[:, :half], preferred_element_type=F32)
        y_hi = jnp.dot(u, wdb_ref[:, half:], preferred_element_type=F32)
        y_ref[...] = _pack_bf16_pair(y_lo, y_hi)

    @pl.when(i >= nused_ref[0])
    def _():
        y_ref[...] = jnp.zeros_like(y_ref)


def _expert_call(block_expert, nused, fresh, valid, xs, wg, wu, wd, rows):
    p, w = xs.shape
    d, f = wg.shape[1], wg.shape[2]
    x_map = lambda i, be, nu, fr, va: (jnp.minimum(i, nu[0] - 1), 0)
    w_map = lambda i, be, nu, fr, va: (be[i], 0, 0)
    grid_spec = pltpu.PrefetchScalarGridSpec(
        num_scalar_prefetch=4,
        grid=(p // rows,),
        in_specs=[pl.BlockSpec((rows, w), x_map),
                  pl.BlockSpec((1, d, f), w_map), pl.BlockSpec((1, d, f), w_map), pl.BlockSpec((1, f, d), w_map)],
        out_specs=pl.BlockSpec((rows, w), lambda i, be, nu, fr, va: (i, 0)),
        scratch_shapes=[pltpu.VMEM((d, f), BF16), pltpu.VMEM((d, f), BF16), pltpu.VMEM((f, d), BF16)],
    )
    return pl.pallas_call(
        _expert_kernel,
        grid_spec=grid_spec,
        out_shape=jax.ShapeDtypeStruct((p, w), jnp.uint32),
        compiler_params=pltpu.CompilerParams(dimension_semantics=("arbitrary",),
                                             vmem_limit_bytes=VMEM_LIMIT),
    )(block_expert, nused, fresh, valid, xs, wg, wu, wd)


def _gather_rows_sc(table, idx, chunk):
    m = idx.shape[0]
    w = table.shape[1]
    info = plsc.get_sparse_core_info()
    nc = info.num_cores
    per_worker = m // (nc * info.num_subcores)
    assert per_worker * nc * info.num_subcores == m and per_worker % chunk == 0
    mesh = plsc.VectorSubcoreMesh(core_axis_name="c", subcore_axis_name="s")

    @functools.partial(
        pl.kernel, mesh=mesh,
        out_type=jax.ShapeDtypeStruct((m, w), table.dtype),
        scratch_types=[pltpu.VMEM((chunk,), jnp.int32), pltpu.VMEM((chunk, w), table.dtype),
                       pltpu.SemaphoreType.DMA],
    )
    def gather(table_hbm, idx_hbm, out_hbm, idx_v, rows_v, sem):
        worker = lax.axis_index("s") * nc + lax.axis_index("c")

        @pl.loop(0, per_worker // chunk)
        def _(j):
            base = worker * per_worker + j * chunk
            pltpu.sync_copy(idx_hbm.at[pl.ds(base, chunk)], idx_v)
            pltpu.async_copy(table_hbm.at[idx_v], rows_v, sem).wait()
            pltpu.sync_copy(rows_v, out_hbm.at[pl.ds(base, chunk)])

    return gather(table, idx)


def _combine_kernel(yk_ref, gate_ref, base_ref, g2_ref, fg_ref, o_ref):
    t = base_ref.shape[0]
    half = yk_ref.shape[2]
    gates = gate_ref[...]
    r_lo = jnp.zeros((t, half), F32)
    r_hi = jnp.zeros((t, half), F32)
    for k in range(TOP_K):
        y_lo, y_hi = _unpack_bf16_pair(yk_ref[k])
        r_lo = r_lo + gates[:, k:k + 1] * y_lo
        r_hi = r_hi + gates[:, k:k + 1] * y_hi
    g2 = g2_ref[0]
    x_lo = base_ref[:, :half] + g2[:, :half] * r_lo
    x_hi = base_ref[:, half:] + g2[:, half:] * r_hi
    ms = (jnp.sum(x_lo * x_lo, axis=-1, keepdims=True)
          + jnp.sum(x_hi * x_hi, axis=-1, keepdims=True)) * (1.0 / (2 * half))
    inv = lax.rsqrt(ms + EPS)
    o_ref[:, :half] = x_lo * inv * fg_ref[:, :half]
    o_ref[:, half:] = x_hi * inv * fg_ref[:, half:]


def _combine_call(yk, gates_t, base, g2, fg, tiles_per_batch, tile):
    n, d = base.shape
    row = lambda w: pl.BlockSpec((tile, w), lambda i: (i, 0))
    return pl.pallas_call(
        _combine_kernel,
        grid=(n // tile,),
        in_specs=[pl.BlockSpec((TOP_K, tile, yk.shape[2]), lambda i: (0, i, 0)),
                  row(TOP_K), row(d),
                  pl.BlockSpec((1, 1, d), lambda i: (i // tiles_per_batch, 0, 0)),
                  pl.BlockSpec((1, d), lambda i: (0, 0))],
        out_specs=row(d),
        out_shape=jax.ShapeDtypeStruct((n, d), F32),
        compiler_params=pltpu.CompilerParams(dimension_semantics=("arbitrary",),
                                             vmem_limit_bytes=VMEM_LIMIT),
    )(yk, gates_t, base, g2, fg)


def _pad_cols(a, width):
    return jnp.pad(a, ((0, 0), (0, width - a.shape[1])))


def _layer(x, c_pad, mod_w, mod_b, norm1_g, norm2_g, w_in, conv_w, conv_b, dt_bias, a_log, d_skip, ssm_norm_g,
           att_norm_g, sinks, rel_bias, w_out, router_w, router_bias, exp_w_gate, exp_w_up, exp_w_down,
           sh_w_gate, sh_w_up, sh_w_down, final_g):
    bsz, l, d = x.shape
    n = bsz * l
    tm = min(ROW_TILE, l)

    mod = _mod_call(c_pad, mod_w, mod_b[None, :])[:bsz]
    sh1, sc1, g1, sh2, sc2, g2 = [m[:, None, :] for m in jnp.split(mod, 6, axis=-1)]

    i1 = D_SSM
    i2 = i1 + CONV_CH
    i3 = i2 + SSM_HEADS
    i4 = i3 + D_ATT
    i5 = i4 + D_KV
    wz, wx, wdt, wq, wk, wv = jnp.split(w_in, [i1, i2, i3, i4, i5], axis=-1)
    wdt = _pad_cols(wdt, LANES)
    x2 = x.reshape(n, d)
    z, xbc, dt, q, k, v = _in_proj_call(x2, sc1, sh1, norm1_g[None, :], wz.astype(BF16), wx.astype(BF16),
                                        wdt.astype(BF16), wq.astype(BF16), wk.astype(BF16), wv.astype(BF16),
                                        l // tm, tm)

    tril = jnp.asarray(np.tril(np.ones((CHUNK, CHUNK), np.float32)))
    y_ssm = _ssd_call(xbc.reshape(bsz, l, CONV_CH), z.reshape(bsz, l, D_SSM), dt.reshape(bsz, l, LANES),
                      conv_w, conv_b[None, :], _pad_cols(dt_bias[None, :], LANES), _pad_cols(a_log[None, :], LANES),
                      jnp.repeat(d_skip, SSM_HEAD_DIM)[None, :], ssm_norm_g[None, :], tril)

    bucket, win = _rel_bucket_table()
    bias = _bias_call(rel_bias, jnp.asarray(bucket), jnp.asarray(win))
    y_att = _attn_call(sinks, q.reshape(bsz, l, D_ATT), k.reshape(bsz, l, D_KV), v.reshape(bsz, l, D_KV), bias,
                       att_norm_g[None, :])

    rw = _pad_cols(router_w, LANES)
    rwh = rw.astype(BF16)
    rwl = (rw - rwh.astype(F32)).astype(BF16)
    base, h2, logits_t = _out_proj_call(
        x2, y_ssm.reshape(n, D_SSM), y_att.reshape(n, D_ATT), g1, sc2, sh2, g2, norm2_g[None, :],
        w_out[:D_SSM].astype(BF16), w_out[D_SSM:].astype(BF16), rwh, rwl,
        sh_w_gate.astype(BF16), sh_w_up.astype(BF16), sh_w_down.astype(BF16), l // tm, tm)

    rt = min(ROUTE_TILE, n)
    upper = jnp.asarray(np.triu(np.ones((rt, rt), np.float32), 1)).astype(BF16)
    idx, gates, rank, counts = _route_call(logits_t, router_bias[:, None], upper, rt)

    nblocks = (n * TOP_K + N_EXPERTS * (EXPERT_ROWS - 1) + EXPERT_ROWS - 1) // EXPERT_ROWS
    cnt = counts[:, 0].astype(jnp.int32)
    padded = (cnt + EXPERT_ROWS - 1) // EXPERT_ROWS * EXPERT_ROWS
    pad_end = jnp.cumsum(padded)
    pad_start = pad_end - padded
    dest = _dest_call(idx, rank, pad_start[:, None], rt)
    block_first_row = jnp.arange(nblocks, dtype=jnp.int32) * EXPERT_ROWS
    block_expert = jnp.minimum(jnp.sum((pad_end[None, :] <= block_first_row[:, None]).astype(jnp.int32), axis=1),
                               N_EXPERTS - 1)
    fresh = jnp.concatenate([jnp.ones((1,), jnp.int32),
                             (block_expert[1:] != block_expert[:-1]).astype(jnp.int32)])
    nused = (pad_end[-1:] // EXPERT_ROWS).astype(jnp.int32)
    seg_row = block_first_row - jnp.sum(jnp.where(pad_end[None, :] <= block_first_row[:, None], padded[None, :], 0),
                                        axis=1)
    block_cnt = jnp.sum(jnp.where(block_expert[:, None] == jnp.arange(N_EXPERTS, dtype=jnp.int32)[None, :],
                                  cnt[None, :], 0), axis=1)
    valid = jnp.clip(block_cnt - seg_row, 0, EXPERT_ROWS).astype(jnp.int32)

    dest_flat = dest.reshape(-1)
    xs = _scatter_rows_sc(h2, dest_flat, nblocks * EXPERT_ROWS, SC_CHUNK)
    ys = _expert_call(block_expert, nused, fresh, valid, xs, exp_w_gate, exp_w_up, exp_w_down, EXPERT_ROWS)
    yk = _gather_rows_sc(ys, dest_flat, SC_CHUNK).reshape(TOP_K, n, ys.shape[1])
    ctile = min(COMBINE_TILE, l)
    out = _combine_call(yk, gates.T, base, g2, final_g[None, :], l // ctile, ctile)
    return out.reshape(bsz, l, d)


def kernel(x, c, mod_w, mod_b, norm1_g, norm2_g, w_in, conv_w, conv_b, dt_bias, a_log, d_skip, ssm_norm_g,
           att_norm_g, sinks, rel_bias, w_out, router_w, router_bias, exp_w_gate, exp_w_up, exp_w_down,
           sh_w_gate, sh_w_up, sh_w_down, final_g):
    assert mod_w.shape[0] == 1, "single-layer block"
    bsz = x.shape[0]
    c_pad = jnp.pad(c, ((0, SUBLANES - bsz % SUBLANES if bsz % SUBLANES else 0), (0, 0)))
    return _layer(x, c_pad, mod_w[0], mod_b[0], norm1_g[0], norm2_g[0], w_in[0], conv_w[0], conv_b[0],
                  dt_bias[0], a_log[0], d_skip[0], ssm_norm_g[0], att_norm_g[0], sinks[0], rel_bias, w_out[0],
                  router_w[0], router_bias[0], exp_w_gate[0], exp_w_up[0], exp_w_down[0], sh_w_gate[0],
                  sh_w_up[0], sh_w_down[0], final_g)
```

```python
import functools
import math

import numpy as np
import jax
import jax.numpy as jnp
from jax import lax
from jax.experimental import pallas as pl
from jax.experimental.pallas import tpu as pltpu
from jax.experimental.pallas import tpu_sc as plsc

F32 = jnp.float32
BF16 = jnp.bfloat16

D_MODEL = 1024
SSM_HEAD_DIM = 64
D_SSM = D_MODEL
SSM_HEADS = D_SSM // SSM_HEAD_DIM
SSM_GROUPS = 4
D_STATE = 128
CONV_K = 4
CONV_CH = D_SSM + 2 * SSM_GROUPS * D_STATE
CHUNK = 128
ATT_HEAD_DIM = 64
D_ATT = D_MODEL
ATT_HEADS = D_ATT // ATT_HEAD_DIM
KV_HEADS = ATT_HEADS // 4
Q_PER_KV = ATT_HEADS // KV_HEADS
D_KV = KV_HEADS * ATT_HEAD_DIM
WINDOW = 128
ATT_BLOCK = 128
REL_BUCKETS = 32
REL_MAX_DIST = 128
N_EXPERTS = 64
TOP_K = 8
EXPERT_DIM = D_MODEL // 4
SHARED_DIM = D_MODEL // 4
ROUTE_GROUPS = 8
ROUTE_TOPK_GROUPS = 4
ROUTED_SCALE = 2.5
EPS = 1e-6

LANES = 128
SUBLANES = 8
HALF = LANES // 2

ROW_TILE = 512
ROUTE_TILE = 512
COMBINE_TILE = 256
SC_CHUNK = 128
EXPERT_ROWS = 512
VMEM_LIMIT = 48 * 1024 * 1024

NEG_INF = float("-inf")


def _silu(v):
    return v * (1.0 / (1.0 + jnp.exp(-v)))


def _softplus(v):
    return jnp.maximum(v, 0.0) + jnp.log(1.0 + jnp.exp(-jnp.abs(v)))


def _bdot(a, b):
    return jnp.dot(a.astype(BF16), b.astype(BF16), preferred_element_type=F32)


def _split_hi_lo(v):
    hi = v.astype(BF16)
    lo = (v - hi.astype(F32)).astype(BF16)
    return hi, lo


def _pack_bf16_pair(a, b):
    w = pltpu.pack_elementwise([a, b], packed_dtype=BF16)
    return w if w.dtype == jnp.uint32 else lax.bitcast_convert_type(w, jnp.uint32)


def _unpack_bf16_pair(w):
    a = pltpu.unpack_elementwise(w, index=0, packed_dtype=BF16, unpacked_dtype=F32)
    b = pltpu.unpack_elementwise(w, index=1, packed_dtype=BF16, unpacked_dtype=F32)
    return a, b


def _lane_half_mask(shape):
    return lax.broadcasted_iota(jnp.int32, shape, len(shape) - 1) < HALF


def _mod_kernel(c_ref, w_ref, b_ref, o_ref):
    a = _silu(c_ref[...])
    o_ref[...] = jnp.dot(a, w_ref[...], precision=lax.Precision.HIGHEST,
                         preferred_element_type=F32) + b_ref[...]


def _mod_call(c_pad, mod_w, mod_b):
    rows, d = c_pad.shape
    cols = mod_w.shape[1]
    return pl.pallas_call(
        _mod_kernel,
        grid=(cols // d,),
        in_specs=[pl.BlockSpec((rows, d), lambda j: (0, 0)),
                  pl.BlockSpec((d, d), lambda j: (0, j)),
                  pl.BlockSpec((1, d), lambda j: (0, j))],
        out_specs=pl.BlockSpec((rows, d), lambda j: (0, j)),
        out_shape=jax.ShapeDtypeStruct((rows, cols), F32),
        compiler_params=pltpu.CompilerParams(dimension_semantics=("arbitrary",),
                                             vmem_limit_bytes=VMEM_LIMIT),
    )(c_pad, mod_w, mod_b)


def _in_proj_kernel(x_ref, sc_ref, sh_ref, g_ref, wz_ref, wx_ref, wdt_ref, wq_ref, wk_ref, wv_ref,
                    z_ref, xbc_ref, dt_ref, q_ref, k_ref, v_ref):
    xf = x_ref[...]
    ms = jnp.mean(xf * xf, axis=-1, keepdims=True)
    h = xf * lax.rsqrt(ms + EPS) * g_ref[...]
    h = h * (1.0 + sc_ref[0]) + sh_ref[0]
    hb = h.astype(BF16)
    z_ref[...] = jnp.dot(hb, wz_ref[...], preferred_element_type=F32).astype(BF16)
    xbc_ref[...] = jnp.dot(hb, wx_ref[...], preferred_element_type=F32).astype(BF16)
    dt_ref[...] = jnp.dot(hb, wdt_ref[...], preferred_element_type=F32)
    q_ref[...] = jnp.dot(hb, wq_ref[...], preferred_element_type=F32).astype(BF16)
    k_ref[...] = jnp.dot(hb, wk_ref[...], preferred_element_type=F32).astype(BF16)
    v_ref[...] = jnp.dot(hb, wv_ref[...], preferred_element_type=F32).astype(BF16)


def _in_proj_call(x2, sc1, sh1, g1n, wz, wx, wdt, wq, wk, wv, tiles_per_batch, tm):
    n, d = x2.shape
    row = lambda w: pl.BlockSpec((tm, w), lambda i: (i, 0))
    full = lambda a: pl.BlockSpec(a.shape, lambda i: (0, 0))
    per_batch = pl.BlockSpec((1, 1, d), lambda i: (i // tiles_per_batch, 0, 0))
    outs = [(wz.shape[1], BF16), (wx.shape[1], BF16), (wdt.shape[1], F32),
            (wq.shape[1], BF16), (wk.shape[1], BF16), (wv.shape[1], BF16)]
    return pl.pallas_call(
        _in_proj_kernel,
        grid=(n // tm,),
        in_specs=[row(d), per_batch, per_batch, full(g1n), full(wz), full(wx), full(wdt), full(wq),
                  full(wk), full(wv)],
        out_specs=[row(w) for w, _ in outs],
        out_shape=[jax.ShapeDtypeStruct((n, w), dt) for w, dt in outs],
        compiler_params=pltpu.CompilerParams(dimension_semantics=("arbitrary",),
                                             vmem_limit_bytes=VMEM_LIMIT),
    )(x2, sc1, sh1, g1n, wz, wx, wdt, wq, wk, wv)


def _ssd_kernel(xbc_ref, z_ref, dt_ref, cw_ref, cb_ref, dtb_ref, alog_ref, dskip_ref, ng_ref, tril_ref,
                y_ref, state_ref, tail_ref, ext_ref, ybuf_ref):
    c = pl.program_id(1)

    @pl.when(c == 0)
    def _():
        state_ref[...] = jnp.zeros_like(state_ref)
        tail_ref[...] = jnp.zeros_like(tail_ref)

    u = xbc_ref[0].astype(F32)
    ext_ref[0:SUBLANES, :] = tail_ref[...]
    ext_ref[SUBLANES:SUBLANES + CHUNK, :] = u
    tail_ref[...] = u[CHUNK - SUBLANES:, :]
    acc = cb_ref[...] + jnp.zeros_like(u)
    for kk in range(CONV_K):
        off = SUBLANES - (CONV_K - 1) + kk
        acc = acc + cw_ref[kk:kk + 1, :] * ext_ref[off:off + CHUNK, :]
    act = _silu(acc)
    xs = act[:, :D_SSM]
    gn = SSM_GROUPS * D_STATE

    dt = _softplus(dt_ref[0] + dtb_ref[...])
    a = dt * (-jnp.exp(alog_ref[...]))
    cs = jnp.dot(tril_ref[...], a, precision=lax.Precision.HIGHEST, preferred_element_type=F32)
    cs_t = cs.T
    dt_t = dt.T
    exp_cs = jnp.exp(cs)
    cs_last = cs[CHUNK - 1:CHUNK, :]
    chunk_decay = jnp.exp(cs_last)

    li = lax.broadcasted_iota(jnp.int32, (CHUNK, CHUNK), 0)
    si = lax.broadcasted_iota(jnp.int32, (CHUNK, CHUNK), 1)
    causal = li >= si
    low = _lane_half_mask((CHUNK, LANES))
    low_row = _lane_half_mask((1, LANES))

    heads_per_group = SSM_HEADS // SSM_GROUPS
    for g in range(SSM_GROUPS):
        b_g = act[:, D_SSM + g * D_STATE:D_SSM + (g + 1) * D_STATE]
        c_g = act[:, D_SSM + gn + g * D_STATE:D_SSM + gn + (g + 1) * D_STATE]
        b_gb = b_g.astype(BF16)
        c_gb = c_g.astype(BF16)
        cb = lax.dot_general(c_gb, b_gb, (((1,), (1,)), ((), ())), preferred_element_type=F32)
        b_t = b_g.T
        for jp in range(heads_per_group // 2):
            j = g * (heads_per_group // 2) + jp
            lanes = slice(j * LANES, (j + 1) * LANES)
            xp = xs[:, lanes]
            ydiag = jnp.zeros((CHUNK, LANES), F32)
            snew = jnp.zeros((D_STATE, LANES), F32)
            for half in range(2):
                h = 2 * j + half
                row_dt = dt_t[h:h + 1, :]
                diff = cs[:, h:h + 1] - cs_t[h:h + 1, :]
                lmat = jnp.exp(jnp.where(causal, diff, NEG_INF))
                m = (cb * lmat * row_dt).astype(BF16)
                keep = low if half == 0 else jnp.logical_not(low)
                xh = jnp.where(keep, xp, 0.0).astype(BF16)
                ydiag = ydiag + jnp.dot(m, xh, preferred_element_type=F32)
                w_t = jnp.exp(cs_t[h:h + 1, CHUNK - 1:CHUNK] - cs_t[h:h + 1, :]) * row_dt
                snew = snew + jnp.dot((b_t * w_t).astype(BF16), xh, preferred_element_type=F32)
            s_in = state_ref[:, lanes]
            yoff = jnp.dot(c_gb, s_in.astype(BF16), preferred_element_type=F32)
            h0 = 2 * j
            escale = jnp.where(low, exp_cs[:, h0:h0 + 1], exp_cs[:, h0 + 1:h0 + 2])
            cdec = jnp.where(low_row, chunk_decay[:, h0:h0 + 1], chunk_decay[:, h0 + 1:h0 + 2])
            ybuf_ref[:, lanes] = ydiag + yoff * escale + xp * dskip_ref[:, lanes]
            state_ref[:, lanes] = s_in * cdec + snew

    yz = ybuf_ref[...] * _silu(z_ref[0].astype(F32))
    gw = D_SSM // SSM_GROUPS
    for g in range(SSM_GROUPS):
        part = yz[:, g * gw:(g + 1) * gw]
        ms = jnp.mean(part * part, axis=-1, keepdims=True)
        y_ref[0, :, g * gw:(g + 1) * gw] = (part * lax.rsqrt(ms + EPS)
                                            * ng_ref[:, g * gw:(g + 1) * gw]).astype(BF16)


def _ssd_call(xbc, z, dt, conv_w, conv_b, dtb, alog, dskip, ng, tril):
    bsz, l, _ = xbc.shape
    nc = l // CHUNK
    chunk = lambda w: pl.BlockSpec((1, CHUNK, w), lambda b, c: (b, c, 0))
    full = lambda a: pl.BlockSpec(a.shape, lambda b, c: (0, 0))
    return pl.pallas_call(
        _ssd_kernel,
        grid=(bsz, nc),
        in_specs=[chunk(CONV_CH), chunk(D_SSM), chunk(LANES), full(conv_w), full(conv_b), full(dtb),
                  full(alog), full(dskip), full(ng), full(tril)],
        out_specs=chunk(D_SSM),
        out_shape=jax.ShapeDtypeStruct((bsz, l, D_SSM), BF16),
        scratch_shapes=[pltpu.VMEM((D_STATE, D_SSM), F32),
                        pltpu.VMEM((SUBLANES, CONV_CH), F32),
                        pltpu.VMEM((SUBLANES + CHUNK, CONV_CH), F32),
                        pltpu.VMEM((CHUNK, D_SSM), F32)],
        compiler_params=pltpu.CompilerParams(dimension_semantics=("arbitrary", "arbitrary"),
                                             vmem_limit_bytes=VMEM_LIMIT),
    )(xbc, z, dt, conv_w, conv_b, dtb, alog, dskip, ng, tril)


def _rel_bucket_table():
    qi = np.arange(ATT_BLOCK)[:, None]
    sj = np.arange(2 * ATT_BLOCK)[None, :]
    dist = qi + ATT_BLOCK - sj
    in_win = (dist >= 0) & (dist < WINDOW)
    dcl = np.maximum(dist, 0)
    max_exact = REL_BUCKETS // 2
    d = np.maximum(dcl, 1).astype(np.float32)
    large = max_exact + (np.log(d / np.float32(max_exact)) / np.float32(math.log(REL_MAX_DIST / max_exact))
                         * np.float32(REL_BUCKETS - max_exact)).astype(np.int32)
    large = np.minimum(large, REL_BUCKETS - 1)
    bucket = np.where(dcl < max_exact, dcl, large).astype(np.int32)
    return bucket, in_win.astype(np.int32)


def _bias_kernel(rb_ref, bucket_ref, win_ref, o_ref):
    h = pl.program_id(0)
    bucket = bucket_ref[...]
    acc = jnp.zeros(bucket.shape, F32)
    for b in range(REL_BUCKETS):
        acc = jnp.where(bucket == b, rb_ref[b, h], acc)
    o_ref[0] = jnp.where(win_ref[...] > 0, acc, NEG_INF)


def _bias_call(rel_bias, bucket, win):
    return pl.pallas_call(
        _bias_kernel,
        grid=(ATT_HEADS,),
        in_specs=[pl.BlockSpec(memory_space=pltpu.SMEM),
                  pl.BlockSpec(bucket.shape, lambda h: (0, 0)),
                  pl.BlockSpec(win.shape, lambda h: (0, 0))],
        out_specs=pl.BlockSpec((1,) + bucket.shape, lambda h: (h, 0, 0)),
        out_shape=jax.ShapeDtypeStruct((ATT_HEADS,) + bucket.shape, F32),
        compiler_params=pltpu.CompilerParams(dimension_semantics=("arbitrary",)),
    )(rel_bias, bucket, win)


def _attn_kernel(sink_ref, q_ref, kp_ref, kc_ref, vp_ref, vc_ref, bias_ref, ng_ref, o_ref, obuf_ref):
    i = pl.program_id(1)
    kband = jnp.concatenate([kp_ref[0], kc_ref[0]], axis=0).astype(F32)
    vband = jnp.concatenate([vp_ref[0], vc_ref[0]], axis=0).astype(F32)
    sj = lax.broadcasted_iota(jnp.int32, (ATT_BLOCK, 2 * ATT_BLOCK), 1)
    valid = jnp.logical_or(sj >= ATT_BLOCK, i > 0)
    low = _lane_half_mask((ATT_BLOCK, LANES))
    scale = ATT_HEAD_DIM ** -0.5

    kv_cols = []
    for cpair in range(KV_HEADS // 2):
        kk = kband[:, cpair * LANES:(cpair + 1) * LANES]
        vv = vband[:, cpair * LANES:(cpair + 1) * LANES]
        kv_cols.append(((kk.astype(BF16), pltpu.roll(kk, HALF, 1).astype(BF16)),
                        (vv.astype(BF16), pltpu.roll(vv, HALF, 1).astype(BF16))))

    for j in range(ATT_HEADS // 2):
        qp = q_ref[0, :, j * LANES:(j + 1) * LANES]
        out_pair = jnp.zeros((ATT_BLOCK, LANES), F32)
        for half in range(2):
            h = 2 * j + half
            g = h // Q_PER_KV
            swapped = int((g % 2) != half)
            ksel = kv_cols[g // 2][0][swapped]
            vsel = kv_cols[g // 2][1][swapped]
            keep = low if half == 0 else jnp.logical_not(low)
            qh = jnp.where(keep, qp, jnp.zeros_like(qp))
            s = lax.dot_general(qh, ksel, (((1,), (1,)), ((), ())), preferred_element_type=F32)
            s = s * scale + bias_ref[h]
            s = jnp.where(valid, s, NEG_INF)
            sink = sink_ref[h]
            m = jnp.maximum(jnp.max(s, axis=-1, keepdims=True), sink)
            p = jnp.exp(s - m)
            denom = jnp.sum(p, axis=-1, keepdims=True) + jnp.exp(sink - m)
            o = jnp.dot(p.astype(BF16), vsel, preferred_element_type=F32) / denom
            out_pair = out_pair + jnp.where(keep, o, 0.0)
        obuf_ref[:, j * LANES:(j + 1) * LANES] = out_pair

    att = obuf_ref[...]
    ms = jnp.mean(att * att, axis=-1, keepdims=True)
    o_ref[0] = (att * lax.rsqrt(ms + EPS) * ng_ref[...]).astype(BF16)


def _attn_call(sinks, q, k, v, bias, ng):
    bsz, l, _ = q.shape
    nb = l // ATT_BLOCK
    cur = lambda w: pl.BlockSpec((1, ATT_BLOCK, w), lambda b, i: (b, i, 0))
    prev = lambda w: pl.BlockSpec((1, ATT_BLOCK, w), lambda b, i: (b, jnp.maximum(i - 1, 0), 0))
    return pl.pallas_call(
        _attn_kernel,
        grid=(bsz, nb),
        in_specs=[pl.BlockSpec(memory_space=pltpu.SMEM),
                  cur(D_ATT), prev(D_KV), cur(D_KV), prev(D_KV), cur(D_KV),
                  pl.BlockSpec(bias.shape, lambda b, i: (0, 0, 0)),
                  pl.BlockSpec(ng.shape, lambda b, i: (0, 0))],
        out_specs=cur(D_ATT),
        out_shape=jax.ShapeDtypeStruct((bsz, l, D_ATT), BF16),
        scratch_shapes=[pltpu.VMEM((ATT_BLOCK, D_ATT), F32)],
        compiler_params=pltpu.CompilerParams(dimension_semantics=("arbitrary", "arbitrary"),
                                             vmem_limit_bytes=VMEM_LIMIT),
    )(sinks, q, k, k, v, v, bias, ng)


def _out_proj_kernel(x_ref, ys_ref, ya_ref, g1_ref, sc_ref, sh_ref, g2_ref, ng_ref, wos_ref, woa_ref,
                     rwh_ref, rwl_ref, sg_ref, su_ref, sd_ref, base_ref, h_ref, lg_ref):
    mix = (jnp.dot(ys_ref[...], wos_ref[...], preferred_element_type=F32)
           + jnp.dot(ya_ref[...], woa_ref[...], preferred_element_type=F32))
    x1 = x_ref[...] + g1_ref[0] * mix
    ms = jnp.mean(x1 * x1, axis=-1, keepdims=True)
    h = x1 * lax.rsqrt(ms + EPS) * ng_ref[...]
    h = h * (1.0 + sc_ref[0]) + sh_ref[0]
    half = h.shape[1] // 2
    h_ref[...] = _pack_bf16_pair(h[:, :half], h[:, half:])
    hi, lo = _split_hi_lo(h)
    logits = (jnp.dot(hi, rwh_ref[...], preferred_element_type=F32)
              + jnp.dot(lo, rwh_ref[...], preferred_element_type=F32)
              + jnp.dot(hi, rwl_ref[...], preferred_element_type=F32))
    lg_ref[...] = logits.T
    u = _silu(jnp.dot(hi, sg_ref[...], preferred_element_type=F32)) * jnp.dot(hi, su_ref[...],
                                                                              preferred_element_type=F32)
    shared = jnp.dot(u.astype(BF16), sd_ref[...], preferred_element_type=F32)
    base_ref[...] = x1 + g2_ref[0] * shared


def _out_proj_call(x2, ys, ya, g1, sc2, sh2, g2, ng, wos, woa, rwh, rwl, sg, su, sd, tiles_per_batch, tm):
    n, d = x2.shape
    row = lambda w: pl.BlockSpec((tm, w), lambda i: (i, 0))
    full = lambda a: pl.BlockSpec(a.shape, lambda i: (0, 0))
    per_batch = pl.BlockSpec((1, 1, d), lambda i: (i // tiles_per_batch, 0, 0))
    return pl.pallas_call(
        _out_proj_kernel,
        grid=(n // tm,),
        in_specs=[row(d), row(D_SSM), row(D_ATT), per_batch, per_batch, per_batch, per_batch, full(ng),
                  full(wos), full(woa), full(rwh), full(rwl), full(sg), full(su), full(sd)],
        out_specs=[row(d), row(d // 2), pl.BlockSpec((LANES, tm), lambda i: (0, i))],
        out_shape=[jax.ShapeDtypeStruct((n, d), F32), jax.ShapeDtypeStruct((n, d // 2), jnp.uint32),
                   jax.ShapeDtypeStruct((LANES, n), F32)],
        compiler_params=pltpu.CompilerParams(dimension_semantics=("arbitrary",),
                                             vmem_limit_bytes=VMEM_LIMIT),
    )(x2, ys, ya, g1, sc2, sh2, g2, ng, wos, woa, rwh, rwl, sg, su, sd)


def _route_kernel(lg_ref, rb_ref, upper_ref, idx_ref, gate_ref, rank_ref, cnt_ref, carry_ref):
    step = pl.program_id(0)

    @pl.when(step == 0)
    def _():
        carry_ref[...] = jnp.zeros_like(carry_ref)

    t = lg_ref.shape[1]
    per_group = N_EXPERTS // ROUTE_GROUPS
    scores = 1.0 / (1.0 + jnp.exp(-lg_ref[0:N_EXPERTS, :]))
    sel = scores + rb_ref[...]
    e_iota = lax.broadcasted_iota(jnp.int32, (N_EXPERTS, t), 0)

    sel3 = sel.reshape(ROUTE_GROUPS, per_group, t)
    w_iota = lax.broadcasted_iota(jnp.int32, sel3.shape, 1)
    m1 = jnp.max(sel3, axis=1, keepdims=True)
    first = jnp.min(jnp.where(sel3 == m1, w_iota, per_group), axis=1, keepdims=True)
    m2 = jnp.max(jnp.where(w_iota == first, NEG_INF, sel3), axis=1, keepdims=True)
    grp = (m1 + m2).reshape(ROUTE_GROUPS, t)

    g_iota = lax.broadcasted_iota(jnp.int32, (ROUTE_GROUPS, t), 0)
    gmask = jnp.zeros((ROUTE_GROUPS, t), jnp.bool_)
    for _ in range(ROUTE_TOPK_GROUPS):
        gm = jnp.max(grp, axis=0, keepdims=True)
        gfirst = jnp.min(jnp.where(grp == gm, g_iota, ROUTE_GROUPS), axis=0, keepdims=True)
        hit = g_iota == gfirst
        gmask = jnp.logical_or(gmask, hit)
        grp = jnp.where(hit, NEG_INF, grp)
    allowed = jnp.broadcast_to(gmask.reshape(ROUTE_GROUPS, 1, t),
                               (ROUTE_GROUPS, per_group, t)).reshape(N_EXPERTS, t)
    masked = jnp.where(allowed, sel, NEG_INF)

    picked = jnp.zeros((N_EXPERTS, t), jnp.bool_)
    idx_rows = []
    w_rows = []
    for _ in range(TOP_K):
        mm = jnp.max(masked, axis=0, keepdims=True)
        efirst = jnp.min(jnp.where(masked == mm, e_iota, N_EXPERTS), axis=0, keepdims=True)
        hit = e_iota == efirst
        idx_rows.append(efirst)
        w_rows.append(jnp.sum(jnp.where(hit, scores, 0.0), axis=0, keepdims=True))
        picked = jnp.logical_or(picked, hit)
        masked = jnp.where(hit, NEG_INF, masked)
    idx = jnp.concatenate(idx_rows, axis=0)
    w = jnp.concatenate(w_rows, axis=0)
    gate_ref[...] = w / jnp.sum(w, axis=0, keepdims=True) * ROUTED_SCALE
    idx_ref[...] = idx

    onehot = jnp.where(picked, 1.0, 0.0)
    before = jnp.dot(onehot.astype(BF16), upper_ref[...], preferred_element_type=F32)
    rank_full = before + carry_ref[:, 0:1]
    rank_rows = [jnp.sum(jnp.where(e_iota == idx_rows[k], rank_full, 0.0), axis=0, keepdims=True)
                 for k in range(TOP_K)]
    rank_ref[...] = jnp.concatenate(rank_rows, axis=0).astype(jnp.int32)
    carry_ref[...] = carry_ref[...] + jnp.sum(onehot, axis=1, keepdims=True)
    cnt_ref[...] = carry_ref[...]


def _route_call(logits_t, router_bias, upper, tile):
    n = logits_t.shape[1]
    tok = lambda r: pl.BlockSpec((r, tile), lambda i: (0, i))
    return pl.pallas_call(
        _route_kernel,
        grid=(n // tile,),
        in_specs=[tok(LANES), pl.BlockSpec((N_EXPERTS, 1), lambda i: (0, 0)),
                  pl.BlockSpec(upper.shape, lambda i: (0, 0))],
        out_specs=[tok(TOP_K), tok(TOP_K), tok(TOP_K), pl.BlockSpec((N_EXPERTS, LANES), lambda i: (0, 0))],
        out_shape=[jax.ShapeDtypeStruct((TOP_K, n), jnp.int32), jax.ShapeDtypeStruct((TOP_K, n), F32),
                   jax.ShapeDtypeStruct((TOP_K, n), jnp.int32),
                   jax.ShapeDtypeStruct((N_EXPERTS, LANES), F32)],
        scratch_shapes=[pltpu.VMEM((N_EXPERTS, LANES), F32)],
        compiler_params=pltpu.CompilerParams(dimension_semantics=("arbitrary",),
                                             vmem_limit_bytes=VMEM_LIMIT),
    )(logits_t, router_bias, upper)


def _dest_kernel(idx_ref, rank_ref, start_ref, dest_ref):
    t = idx_ref.shape[1]
    e_iota = lax.broadcasted_iota(jnp.int32, (N_EXPERTS, t), 0)
    rows = [jnp.sum(jnp.where(e_iota == idx_ref[k:k + 1, :], start_ref[...], 0), axis=0, keepdims=True)
            for k in range(TOP_K)]
    dest_ref[...] = jnp.concatenate(rows, axis=0) + rank_ref[...]


def _dest_call(idx, rank, pad_start, tile):
    n = idx.shape[1]
    tok = pl.BlockSpec((TOP_K, tile), lambda i: (0, i))
    return pl.pallas_call(
        _dest_kernel,
        grid=(n // tile,),
        in_specs=[tok, tok, pl.BlockSpec((N_EXPERTS, 1), lambda i: (0, 0))],
        out_specs=tok,
        out_shape=jax.ShapeDtypeStruct((TOP_K, n), jnp.int32),
        compiler_params=pltpu.CompilerParams(dimension_semantics=("arbitrary",)),
    )(idx, rank, pad_start)


def _scatter_rows_sc(rows, dest_flat, total_rows, chunk):
    n, w = rows.shape
    copies = dest_flat.shape[0] // n
    info = plsc.get_sparse_core_info()
    nc = info.num_cores
    per_worker = n // (nc * info.num_subcores)
    assert per_worker * nc * info.num_subcores == n and per_worker % chunk == 0
    mesh = plsc.VectorSubcoreMesh(core_axis_name="c", subcore_axis_name="s")

    @functools.partial(
        pl.kernel, mesh=mesh,
        out_type=jax.ShapeDtypeStruct((total_rows, w), rows.dtype),
        scratch_types=[pltpu.VMEM((chunk,), jnp.int32), pltpu.VMEM((chunk, w), rows.dtype),
                       pltpu.SemaphoreType.DMA],
    )
    def scatter(rows_hbm, idx_hbm, out_hbm, idx_v, rows_v, sem):
        worker = lax.axis_index("s") * nc + lax.axis_index("c")

        @pl.loop(0, per_worker // chunk)
        def _(j):
            base = worker * per_worker + j * chunk
            pltpu.sync_copy(rows_hbm.at[pl.ds(base, chunk)], rows_v)
            for k in range(copies):
                pltpu.sync_copy(idx_hbm.at[pl.ds(k * n + base, chunk)], idx_v)
                pltpu.async_copy(rows_v, out_hbm.at[idx_v], sem).wait()

    return scatter(rows, dest_flat)


def _expert_kernel(be_ref, nused_ref, fresh_ref, valid_ref, xs_ref, wg_ref, wu_ref, wd_ref, y_ref, wgb_ref, wub_ref,
                   wdb_ref):
    i = pl.program_id(0)

    @pl.when(fresh_ref[i] > 0)
    def _():
        wgb_ref[...] = wg_ref[0].astype(BF16)
        wub_ref[...] = wu_ref[0].astype(BF16)
        wdb_ref[...] = wd_ref[0].astype(BF16)

    @pl.when(i < nused_ref[0])
    def _():
        half = xs_ref.shape[1]
        row = lax.broadcasted_iota(jnp.int32, xs_ref.shape, 0)
        x_lo, x_hi = _unpack_bf16_pair(jnp.where(row < valid_ref[i], xs_ref[...], jnp.uint32(0)))
        x_lo = x_lo.astype(BF16)
        x_hi = x_hi.astype(BF16)
        gate = (jnp.dot(x_lo, wgb_ref[:half, :], preferred_element_type=F32)
                + jnp.dot(x_hi, wgb_ref[half:, :], preferred_element_type=F32))
        up = (jnp.dot(x_lo, wub_ref[:half, :], preferred_element_type=F32)
              + jnp.dot(x_hi, wub_ref[half:, :], preferred_element_type=F32))
        u = (_silu(gate) * up).astype(BF16)
        y_lo = jnp.dot(u, wdb_ref[:, :half], preferred_element_type=F32)
        y_hi = jnp.dot(u, wdb_ref[:, half:], preferred_element_type=F32)
        y_ref[...] = _pack_bf16_pair(y_lo, y_hi)

    @pl.when(i >= nused_ref[0])
    def _():
        y_ref[...] = jnp.zeros_like(y_ref)


def _expert_call(block_expert, nused, fresh, valid, xs, wg, wu, wd, rows):
    p, w = xs.shape
    d, f = wg.shape[1], wg.shape[2]
    x_map = lambda i, be, nu, fr, va: (jnp.minimum(i, nu[0] - 1), 0)
    w_map = lambda i, be, nu, fr, va: (be[i], 0, 0)
    grid_spec = pltpu.PrefetchScalarGridSpec(
        num_scalar_prefetch=4,
        grid=(p // rows,),
        in_specs=[pl.BlockSpec((rows, w), x_map),
                  pl.BlockSpec((1, d, f), w_map), pl.BlockSpec((1, d, f), w_map), pl.BlockSpec((1, f, d), w_map)],
        out_specs=pl.BlockSpec((rows, w), lambda i, be, nu, fr, va: (i, 0)),
        scratch_shapes=[pltpu.VMEM((d, f), BF16), pltpu.VMEM((d, f), BF16), pltpu.VMEM((f, d), BF16)],
    )
    return pl.pallas_call(
        _expert_kernel,
        grid_spec=grid_spec,
        out_shape=jax.ShapeDtypeStruct((p, w), jnp.uint32),
        compiler_params=pltpu.CompilerParams(dimension_semantics=("arbitrary",),
                                             vmem_limit_bytes=VMEM_LIMIT),
    )(block_expert, nused, fresh, valid, xs, wg, wu, wd)


def _gather_rows_sc(table, idx, chunk):
    m = idx.shape[0]
    w = table.shape[1]
    info = plsc.get_sparse_core_info()
    nc = info.num_cores
    per_worker = m // (nc * info.num_subcores)
    assert per_worker * nc * info.num_subcores == m and per_worker % chunk == 0
    mesh = plsc.VectorSubcoreMesh(core_axis_name="c", subcore_axis_name="s")

    @functools.partial(
        pl.kernel, mesh=mesh,
        out_type=jax.ShapeDtypeStruct((m, w), table.dtype),
        scratch_types=[pltpu.VMEM((chunk,), jnp.int32), pltpu.VMEM((chunk, w), table.dtype),
                       pltpu.SemaphoreType.DMA],
    )
    def gather(table_hbm, idx_hbm, out_hbm, idx_v, rows_v, sem):
        worker = lax.axis_index("s") * nc + lax.axis_index("c")

        @pl.loop(0, per_worker // chunk)
        def _(j):
            base = worker * per_worker + j * chunk
            pltpu.sync_copy(idx_hbm.at[pl.ds(base, chunk)], idx_v)
            pltpu.async_copy(table_hbm.at[idx_v], rows_v, sem).wait()
            pltpu.sync_copy(rows_v, out_hbm.at[pl.ds(base, chunk)])

    return gather(table, idx)


def _combine_kernel(yk_ref, gate_ref, base_ref, g2_ref, fg_ref, o_ref):
    t = base_ref.shape[0]
    half = yk_ref.shape[2]
    gates = gate_ref[...]
    r_lo = jnp.zeros((t, half), F32)
    r_hi = jnp.zeros((t, half), F32)
    for k in range(TOP_K):
        y_lo, y_hi = _unpack_bf16_pair(yk_ref[k])
        r_lo = r_lo + gates[:, k:k + 1] * y_lo
        r_hi = r_hi + gates[:, k:k + 1] * y_hi
    g2 = g2_ref[0]
    x_lo = base_ref[:, :half] + g2[:, :half] * r_lo
    x_hi = base_ref[:, half:] + g2[:, half:] * r_hi
    ms = (jnp.sum(x_lo * x_lo, axis=-1, keepdims=True)
          + jnp.sum(x_hi * x_hi, axis=-1, keepdims=True)) * (1.0 / (2 * half))
    inv = lax.rsqrt(ms + EPS)
    o_ref[:, :half] = x_lo * inv * fg_ref[:, :half]
    o_ref[:, half:] = x_hi * inv * fg_ref[:, half:]


def _combine_call(yk, gates_t, base, g2, fg, tiles_per_batch, tile):
    n, d = base.shape
    row = lambda w: pl.BlockSpec((tile, w), lambda i: (i, 0))
    return pl.pallas_call(
        _combine_kernel,
        grid=(n // tile,),
        in_specs=[pl.BlockSpec((TOP_K, tile, yk.shape[2]), lambda i: (0, i, 0)),
                  row(TOP_K), row(d),
                  pl.BlockSpec((1, 1, d), lambda i: (i // tiles_per_batch, 0, 0)),
                  pl.BlockSpec((1, d), lambda i: (0, 0))],
        out_specs=row(d),
        out_shape=jax.ShapeDtypeStruct((n, d), F32),
        compiler_params=pltpu.CompilerParams(dimension_semantics=("arbitrary",),
                                             vmem_limit_bytes=VMEM_LIMIT),
    )(yk, gates_t, base, g2, fg)


def _pad_cols(a, width):
    return jnp.pad(a, ((0, 0), (0, width - a.shape[1])))


def _layer(x, c_pad, mod_w, mod_b, norm1_g, norm2_g, w_in, conv_w, conv_b, dt_bias, a_log, d_skip, ssm_norm_g,
           att_norm_g, sinks, rel_bias, w_out, router_w, router_bias, exp_w_gate, exp_w_up, exp_w_down,
           sh_w_gate, sh_w_up, sh_w_down, final_g):
    bsz, l, d = x.shape
    n = bsz * l
    tm = min(ROW_TILE, l)

    mod = _mod_call(c_pad, mod_w, mod_b[None, :])[:bsz]
    sh1, sc1, g1, sh2, sc2, g2 = [m[:, None, :] for m in jnp.split(mod, 6, axis=-1)]

    i1 = D_SSM
    i2 = i1 + CONV_CH
    i3 = i2 + SSM_HEADS
    i4 = i3 + D_ATT
    i5 = i4 + D_KV
    wz, wx, wdt, wq, wk, wv = jnp.split(w_in, [i1, i2, i3, i4, i5], axis=-1)
    wdt = _pad_cols(wdt, LANES)
    x2 = x.reshape(n, d)
    z, xbc, dt, q, k, v = _in_proj_call(x2, sc1, sh1, norm1_g[None, :], wz.astype(BF16), wx.astype(BF16),
                                        wdt.astype(BF16), wq.astype(BF16), wk.astype(BF16), wv.astype(BF16),
                                        l // tm, tm)

    tril = jnp.asarray(np.tril(np.ones((CHUNK, CHUNK), np.float32)))
    y_ssm = _ssd_call(xbc.reshape(bsz, l, CONV_CH), z.reshape(bsz, l, D_SSM), dt.reshape(bsz, l, LANES),
                      conv_w, conv_b[None, :], _pad_cols(dt_bias[None, :], LANES), _pad_cols(a_log[None, :], LANES),
                      jnp.repeat(d_skip, SSM_HEAD_DIM)[None, :], ssm_norm_g[None, :], tril)

    bucket, win = _rel_bucket_table()
    bias = _bias_call(rel_bias, jnp.asarray(bucket), jnp.asarray(win))
    y_att = _attn_call(sinks, q.reshape(bsz, l, D_ATT), k.reshape(bsz, l, D_KV), v.reshape(bsz, l, D_KV), bias,
                       att_norm_g[None, :])

    rw = _pad_cols(router_w, LANES)
    rwh = rw.astype(BF16)
    rwl = (rw - rwh.astype(F32)).astype(BF16)
    base, h2, logits_t = _out_proj_call(
        x2, y_ssm.reshape(n, D_SSM), y_att.reshape(n, D_ATT), g1, sc2, sh2, g2, norm2_g[None, :],
        w_out[:D_SSM].astype(BF16), w_out[D_SSM:].astype(BF16), rwh, rwl,
        sh_w_gate.astype(BF16), sh_w_up.astype(BF16), sh_w_down.astype(BF16), l // tm, tm)

    rt = min(ROUTE_TILE, n)
    upper = jnp.asarray(np.triu(np.ones((rt, rt), np.float32), 1)).astype(BF16)
    idx, gates, rank, counts = _route_call(logits_t, router_bias[:, None], upper, rt)

    nblocks = (n * TOP_K + N_EXPERTS * (EXPERT_ROWS - 1) + EXPERT_ROWS - 1) // EXPERT_ROWS
    cnt = counts[:, 0].astype(jnp.int32)
    padded = (cnt + EXPERT_ROWS - 1) // EXPERT_ROWS * EXPERT_ROWS
    pad_end = jnp.cumsum(padded)
    pad_start = pad_end - padded
    dest = _dest_call(idx, rank, pad_start[:, None], rt)
    block_first_row = jnp.arange(nblocks, dtype=jnp.int32) * EXPERT_ROWS
    block_expert = jnp.minimum(jnp.sum((pad_end[None, :] <= block_first_row[:, None]).astype(jnp.int32), axis=1),
                               N_EXPERTS - 1)
    fresh = jnp.concatenate([jnp.ones((1,), jnp.int32),
                             (block_expert[1:] != block_expert[:-1]).astype(jnp.int32)])
    nused = (pad_end[-1:] // EXPERT_ROWS).astype(jnp.int32)
    seg_row = block_first_row - jnp.sum(jnp.where(pad_end[None, :] <= block_first_row[:, None], padded[None, :], 0),
                                        axis=1)
    block_cnt = jnp.sum(jnp.where(block_expert[:, None] == jnp.arange(N_EXPERTS, dtype=jnp.int32)[None, :],
                                  cnt[None, :], 0), axis=1)
    valid = jnp.clip(block_cnt - seg_row, 0, EXPERT_ROWS).astype(jnp.int32)

    dest_flat = dest.reshape(-1)
    xs = _scatter_rows_sc(h2, dest_flat, nblocks * EXPERT_ROWS, SC_CHUNK)
    ys = _expert_call(block_expert, nused, fresh, valid, xs, exp_w_gate, exp_w_up, exp_w_down, EXPERT_ROWS)
    yk = _gather_rows_sc(ys, dest_flat, SC_CHUNK).reshape(TOP_K, n, ys.shape[1])
    ctile = min(COMBINE_TILE, l)
    out = _combine_call(yk, gates.T, base, g2, final_g[None, :], l // ctile, ctile)
    return out.reshape(bsz, l, d)


def kernel(x, c, mod_w, mod_b, norm1_g, norm2_g, w_in, conv_w, conv_b, dt_bias, a_log, d_skip, ssm_norm_g,
           att_norm_g, sinks, rel_bias, w_out, router_w, router_bias, exp_w_gate, exp_w_up, exp_w_down,
           sh_w_gate, sh_w_up, sh_w_down, final_g):
    assert mod_w.shape[0] == 1, "single-layer block"
    bsz = x.shape[0]
    c_pad = jnp.pad(c, ((0, SUBLANES - bsz % SUBLANES if bsz % SUBLANES else 0), (0, 0)))
    return _layer(x, c_pad, mod_w[0], mod_b[0], norm1_g[0], norm2_g[0], w_in[0], conv_w[0], conv_b[0],
                  dt_bias[0], a_log[0], d_skip[0], ssm_norm_g[0], att_norm_g[0], sinks[0], rel_bias, w_out[0],
                  router_w[0], router_bias[0], exp_w_gate[0], exp_w_up[0], exp_w_down[0], sh_w_gate[0],
                  sh_w_up[0], sh_w_down[0], final_g)
```

```python
import functools
import math

import numpy as np
import jax
import jax.numpy as jnp
from jax import lax
from jax.experimental import pallas as pl
from jax.experimental.pallas import tpu as pltpu
from jax.experimental.pallas import tpu_sc as plsc

F32 = jnp.float32
BF16 = jnp.bfloat16

D_MODEL = 1024
SSM_HEAD_DIM = 64
D_SSM = D_MODEL
SSM_HEADS = D_SSM // SSM_HEAD_DIM
SSM_GROUPS = 4
D_STATE = 128
CONV_K = 4
CONV_CH = D_SSM + 2 * SSM_GROUPS * D_STATE
CHUNK = 128
ATT_HEAD_DIM = 64
D_ATT = D_MODEL
ATT_HEADS = D_ATT // ATT_HEAD_DIM
KV_HEADS = ATT_HEADS // 4
Q_PER_KV = ATT_HEADS // KV_HEADS
D_KV = KV_HEADS * ATT_HEAD_DIM
WINDOW = 128
ATT_BLOCK = 128
REL_BUCKETS = 32
REL_MAX_DIST = 128
N_EXPERTS = 64
TOP_K = 8
EXPERT_DIM = D_MODEL // 4
SHARED_DIM = D_MODEL // 4
ROUTE_GROUPS = 8
ROUTE_TOPK_GROUPS = 4
ROUTED_SCALE = 2.5
EPS = 1e-6

LANES = 128
SUBLANES = 8
HALF = LANES // 2

ROW_TILE = 512
ROUTE_TILE = 512
COMBINE_TILE = 256
SC_CHUNK = 128
EXPERT_ROWS = 512
VMEM_LIMIT = 48 * 1024 * 1024

NEG_INF = float("-inf")


def _silu(v):
    return v * (1.0 / (1.0 + jnp.exp(-v)))


def _softplus(v):
    return jnp.maximum(v, 0.0) + jnp.log(1.0 + jnp.exp(-jnp.abs(v)))


def _bdot(a, b):
    return jnp.dot(a.astype(BF16), b.astype(BF16), preferred_element_type=F32)


def _split_hi_lo(v):
    hi = v.astype(BF16)
    lo = (v - hi.astype(F32)).astype(BF16)
    return hi, lo


def _pack_bf16_pair(a, b):
    w = pltpu.pack_elementwise([a, b], packed_dtype=BF16)
    return w if w.dtype == jnp.uint32 else lax.bitcast_convert_type(w, jnp.uint32)


def _unpack_bf16_pair(w):
    a = pltpu.unpack_elementwise(w, index=0, packed_dtype=BF16, unpacked_dtype=F32)
    b = pltpu.unpack_elementwise(w, index=1, packed_dtype=BF16, unpacked_dtype=F32)
    return a, b


def _lane_half_mask(shape):
    return lax.broadcasted_iota(jnp.int32, shape, len(shape) - 1) < HALF


def _mod_kernel(c_ref, w_ref, b_ref, o_ref):
    a = _silu(c_ref[...])
    o_ref[...] = jnp.dot(a, w_ref[...], precision=lax.Precision.HIGHEST,
                         preferred_element_type=F32) + b_ref[...]


def _mod_call(c_pad, mod_w, mod_b):
    rows, d = c_pad.shape
    cols = mod_w.shape[1]
    return pl.pallas_call(
        _mod_kernel,
        grid=(cols // d,),
        in_specs=[pl.BlockSpec((rows, d), lambda j: (0, 0)),
                  pl.BlockSpec((d, d), lambda j: (0, j)),
                  pl.BlockSpec((1, d), lambda j: (0, j))],
        out_specs=pl.BlockSpec((rows, d), lambda j: (0, j)),
        out_shape=jax.ShapeDtypeStruct((rows, cols), F32),
        compiler_params=pltpu.CompilerParams(dimension_semantics=("arbitrary",),
                                             vmem_limit_bytes=VMEM_LIMIT),
    )(c_pad, mod_w, mod_b)


def _in_proj_kernel(x_ref, sc_ref, sh_ref, g_ref, wz_ref, wx_ref, wdt_ref, wq_ref, wk_ref, wv_ref,
                    z_ref, xbc_ref, dt_ref, q_ref, k_ref, v_ref):
    xf = x_ref[...]
    ms = jnp.mean(xf * xf, axis=-1, keepdims=True)
    h = xf * lax.rsqrt(ms + EPS) * g_ref[...]
    h = h * (1.0 + sc_ref[0]) + sh_ref[0]
    hb = h.astype(BF16)
    z_ref[...] = jnp.dot(hb, wz_ref[...], preferred_element_type=F32).astype(BF16)
    xbc_ref[...] = jnp.dot(hb, wx_ref[...], preferred_element_type=F32).astype(BF16)
    dt_ref[...] = jnp.dot(hb, wdt_ref[...], preferred_element_type=F32)
    q_ref[...] = jnp.dot(hb, wq_ref[...], preferred_element_type=F32).astype(BF16)
    k_ref[...] = jnp.dot(hb, wk_ref[...], preferred_element_type=F32).astype(BF16)
    v_ref[...] = jnp.dot(hb, wv_ref[...], preferred_element_type=F32).astype(BF16)


def _in_proj_call(x2, sc1, sh1, g1n, wz, wx, wdt, wq, wk, wv, tiles_per_batch, tm):
    n, d = x2.shape
    row = lambda w: pl.BlockSpec((tm, w), lambda i: (i, 0))
    full = lambda a: pl.BlockSpec(a.shape, lambda i: (0, 0))
    per_batch = pl.BlockSpec((1, 1, d), lambda i: (i // tiles_per_batch, 0, 0))
    outs = [(wz.shape[1], BF16), (wx.shape[1], BF16), (wdt.shape[1], F32),
            (wq.shape[1], BF16), (wk.shape[1], BF16), (wv.shape[1], BF16)]
    return pl.pallas_call(
        _in_proj_kernel,
        grid=(n // tm,),
        in_specs=[row(d), per_batch, per_batch, full(g1n), full(wz), full(wx), full(wdt), full(wq),
                  full(wk), full(wv)],
        out_specs=[row(w) for w, _ in outs],
        out_shape=[jax.ShapeDtypeStruct((n, w), dt) for w, dt in outs],
        compiler_params=pltpu.CompilerParams(dimension_semantics=("arbitrary",),
                                             vmem_limit_bytes=VMEM_LIMIT),
    )(x2, sc1, sh1, g1n, wz, wx, wdt, wq, wk, wv)


def _ssd_kernel(xbc_ref, z_ref, dt_ref, cw_ref, cb_ref, dtb_ref, alog_ref, dskip_ref, ng_ref, tril_ref,
                y_ref, state_ref, tail_ref, ext_ref, ybuf_ref):
    c = pl.program_id(1)

    @pl.when(c == 0)
    def _():
        state_ref[...] = jnp.zeros_like(state_ref)
        tail_ref[...] = jnp.zeros_like(tail_ref)

    u = xbc_ref[0].astype(F32)
    ext_ref[0:SUBLANES, :] = tail_ref[...]
    ext_ref[SUBLANES:SUBLANES + CHUNK, :] = u
    tail_ref[...] = u[CHUNK - SUBLANES:, :]
    acc = cb_ref[...] + jnp.zeros_like(u)
    for kk in range(CONV_K):
        off = SUBLANES - (CONV_K - 1) + kk
        acc = acc + cw_ref[kk:kk + 1, :] * ext_ref[off:off + CHUNK, :]
    act = _silu(acc)
    xs = act[:, :D_SSM]
    gn = SSM_GROUPS * D_STATE

    dt = _softplus(dt_ref[0] + dtb_ref[...])
    a = dt * (-jnp.exp(alog_ref[...]))
    cs = jnp.dot(tril_ref[...], a, precision=lax.Precision.HIGHEST, preferred_element_type=F32)
    cs_t = cs.T
    dt_t = dt.T
    exp_cs = jnp.exp(cs)
    cs_last = cs[CHUNK - 1:CHUNK, :]
    chunk_decay = jnp.exp(cs_last)

    li = lax.broadcasted_iota(jnp.int32, (CHUNK, CHUNK), 0)
    si = lax.broadcasted_iota(jnp.int32, (CHUNK, CHUNK), 1)
    causal = li >= si
    low = _lane_half_mask((CHUNK, LANES))
    low_row = _lane_half_mask((1, LANES))

    heads_per_group = SSM_HEADS // SSM_GROUPS
    for g in range(SSM_GROUPS):
        b_g = act[:, D_SSM + g * D_STATE:D_SSM + (g + 1) * D_STATE]
        c_g = act[:, D_SSM + gn + g * D_STATE:D_SSM + gn + (g + 1) * D_STATE]
        b_gb = b_g.astype(BF16)
        c_gb = c_g.astype(BF16)
        cb = lax.dot_general(c_gb, b_gb, (((1,), (1,)), ((), ())), preferred_element_type=F32)
        b_t = b_g.T
        for jp in range(heads_per_group // 2):
            j = g * (heads_per_group // 2) + jp
            lanes = slice(j * LANES, (j + 1) * LANES)
            xp = xs[:, lanes]
            ydiag = jnp.zeros((CHUNK, LANES), F32)
            snew = jnp.zeros((D_STATE, LANES), F32)
            for half in range(2):
                h = 2 * j + half
                row_dt = dt_t[h:h + 1, :]
                diff = cs[:, h:h + 1] - cs_t[h:h + 1, :]
                lmat = jnp.exp(jnp.where(causal, diff, NEG_INF))
                m = (cb * lmat * row_dt).astype(BF16)
                keep = low if half == 0 else jnp.logical_not(low)
                xh = jnp.where(keep, xp, 0.0).astype(BF16)
                ydiag = ydiag + jnp.dot(m, xh, preferred_element_type=F32)
                w_t = jnp.exp(cs_t[h:h + 1, CHUNK - 1:CHUNK] - cs_t[h:h + 1, :]) * row_dt
                snew = snew + jnp.dot((b_t * w_t).astype(BF16), xh, preferred_element_type=F32)
            s_in = state_ref[:, lanes]
            yoff = jnp.dot(c_gb, s_in.astype(BF16), preferred_element_type=F32)
            h0 = 2 * j
            escale = jnp.where(low, exp_cs[:, h0:h0 + 1], exp_cs[:, h0 + 1:h0 + 2])
            cdec = jnp.where(low_row, chunk_decay[:, h0:h0 + 1], chunk_decay[:, h0 + 1:h0 + 2])
            ybuf_ref[:, lanes] = ydiag + yoff * escale + xp * dskip_ref[:, lanes]
            state_ref[:, lanes] = s_in * cdec + snew

    yz = ybuf_ref[...] * _silu(z_ref[0].astype(F32))
    gw = D_SSM // SSM_GROUPS
    for g in range(SSM_GROUPS):
        part = yz[:, g * gw:(g + 1) * gw]
        ms = jnp.mean(part * part, axis=-1, keepdims=True)
        y_ref[0, :, g * gw:(g + 1) * gw] = (part * lax.rsqrt(ms + EPS)
                                            * ng_ref[:, g * gw:(g + 1) * gw]).astype(BF16)


def _ssd_call(xbc, z, dt, conv_w, conv_b, dtb, alog, dskip, ng, tril):
    bsz, l, _ = xbc.shape
    nc = l // CHUNK
    chunk = lambda w: pl.BlockSpec((1, CHUNK, w), lambda b, c: (b, c, 0))
    full = lambda a: pl.BlockSpec(a.shape, lambda b, c: (0, 0))
    return pl.pallas_call(
        _ssd_kernel,
        grid=(bsz, nc),
        in_specs=[chunk(CONV_CH), chunk(D_SSM), chunk(LANES), full(conv_w), full(conv_b), full(dtb),
                  full(alog), full(dskip), full(ng), full(tril)],
        out_specs=chunk(D_SSM),
        out_shape=jax.ShapeDtypeStruct((bsz, l, D_SSM), BF16),
        scratch_shapes=[pltpu.VMEM((D_STATE, D_SSM), F32),
                        pltpu.VMEM((SUBLANES, CONV_CH), F32),
                        pltpu.VMEM((SUBLANES + CHUNK, CONV_CH), F32),
                        pltpu.VMEM((CHUNK, D_SSM), F32)],
        compiler_params=pltpu.CompilerParams(dimension_semantics=("arbitrary", "arbitrary"),
                                             vmem_limit_bytes=VMEM_LIMIT),
    )(xbc, z, dt, conv_w, conv_b, dtb, alog, dskip, ng, tril)


def _rel_bucket_table():
    qi = np.arange(ATT_BLOCK)[:, None]
    sj = np.arange(2 * ATT_BLOCK)[None, :]
    dist = qi + ATT_BLOCK - sj
    in_win = (dist >= 0) & (dist < WINDOW)
    dcl = np.maximum(dist, 0)
    max_exact = REL_BUCKETS // 2
    d = np.maximum(dcl, 1).astype(np.float32)
    large = max_exact + (np.log(d / np.float32(max_exact)) / np.float32(math.log(REL_MAX_DIST / max_exact))
                         * np.float32(REL_BUCKETS - max_exact)).astype(np.int32)
    large = np.minimum(large, REL_BUCKETS - 1)
    bucket = np.where(dcl < max_exact, dcl, large).astype(np.int32)
    return bucket, in_win.astype(np.int32)


def _bias_kernel(rb_ref, bucket_ref, win_ref, o_ref):
    h = pl.program_id(0)
    bucket = bucket_ref[...]
    acc = jnp.zeros(bucket.shape, F32)
    for b in range(REL_BUCKETS):
        acc = jnp.where(bucket == b, rb_ref[b, h], acc)
    o_ref[0] = jnp.where(win_ref[...] > 0, acc, NEG_INF)


def _bias_call(rel_bias, bucket, win):
    return pl.pallas_call(
        _bias_kernel,
        grid=(ATT_HEADS,),
        in_specs=[pl.BlockSpec(memory_space=pltpu.SMEM),
                  pl.BlockSpec(bucket.shape, lambda h: (0, 0)),
                  pl.BlockSpec(win.shape, lambda h: (0, 0))],
        out_specs=pl.BlockSpec((1,) + bucket.shape, lambda h: (h, 0, 0)),
        out_shape=jax.ShapeDtypeStruct((ATT_HEADS,) + bucket.shape, F32),
        compiler_params=pltpu.CompilerParams(dimension_semantics=("arbitrary",)),
    )(rel_bias, bucket, win)


def _attn_kernel(sink_ref, q_ref, kp_ref, kc_ref, vp_ref, vc_ref, bias_ref, ng_ref, o_ref, obuf_ref):
    i = pl.program_id(1)
    kband = jnp.concatenate([kp_ref[0], kc_ref[0]], axis=0).astype(F32)
    vband = jnp.concatenate([vp_ref[0], vc_ref[0]], axis=0).astype(F32)
    sj = lax.broadcasted_iota(jnp.int32, (ATT_BLOCK, 2 * ATT_BLOCK), 1)
    valid = jnp.logical_or(sj >= ATT_BLOCK, i > 0)
    low = _lane_half_mask((ATT_BLOCK, LANES))
    scale = ATT_HEAD_DIM ** -0.5

    kv_cols = []
    for cpair in range(KV_HEADS // 2):
        kk = kband[:, cpair * LANES:(cpair + 1) * LANES]
        vv = vband[:, cpair * LANES:(cpair + 1) * LANES]
        kv_cols.append(((kk.astype(BF16), pltpu.roll(kk, HALF, 1).astype(BF16)),
                        (vv.astype(BF16), pltpu.roll(vv, HALF, 1).astype(BF16))))

    for j in range(ATT_HEADS // 2):
        qp = q_ref[0, :, j * LANES:(j + 1) * LANES]
        out_pair = jnp.zeros((ATT_BLOCK, LANES), F32)
        for half in range(2):
            h = 2 * j + half
            g = h // Q_PER_KV
            swapped = int((g % 2) != half)
            ksel = kv_cols[g // 2][0][swapped]
            vsel = kv_cols[g // 2][1][swapped]
            keep = low if half == 0 else jnp.logical_not(low)
            qh = jnp.where(keep, qp, jnp.zeros_like(qp))
            s = lax.dot_general(qh, ksel, (((1,), (1,)), ((), ())), preferred_element_type=F32)
            s = s * scale + bias_ref[h]
            s = jnp.where(valid, s, NEG_INF)
            sink = sink_ref[h]
            m = jnp.maximum(jnp.max(s, axis=-1, keepdims=True), sink)
            p = jnp.exp(s - m)
            denom = jnp.sum(p, axis=-1, keepdims=True) + jnp.exp(sink - m)
            o = jnp.dot(p.astype(BF16), vsel, preferred_element_type=F32) / denom
            out_pair = out_pair + jnp.where(keep, o, 0.0)
        obuf_ref[:, j * LANES:(j + 1) * LANES] = out_pair

    att = obuf_ref[...]
    ms = jnp.mean(att * att, axis=-1, keepdims=True)
    o_ref[0] = (att * lax.rsqrt(ms + EPS) * ng_ref[...]).astype(BF16)


def _attn_call(sinks, q, k, v, bias, ng):
    bsz, l, _ = q.shape
    nb = l // ATT_BLOCK
    cur = lambda w: pl.BlockSpec((1, ATT_BLOCK, w), lambda b, i: (b, i, 0))
    prev = lambda w: pl.BlockSpec((1, ATT_BLOCK, w), lambda b, i: (b, jnp.maximum(i - 1, 0), 0))
    return pl.pallas_call(
        _attn_kernel,
        grid=(bsz, nb),
        in_specs=[pl.BlockSpec(memory_space=pltpu.SMEM),
                  cur(D_ATT), prev(D_KV), cur(D_KV), prev(D_KV), cur(D_KV),
                  pl.BlockSpec(bias.shape, lambda b, i: (0, 0, 0)),
                  pl.BlockSpec(ng.shape, lambda b, i: (0, 0))],
        out_specs=cur(D_ATT),
        out_shape=jax.ShapeDtypeStruct((bsz, l, D_ATT), BF16),
        scratch_shapes=[pltpu.VMEM((ATT_BLOCK, D_ATT), F32)],
        compiler_params=pltpu.CompilerParams(dimension_semantics=("arbitrary", "arbitrary"),
                                             vmem_limit_bytes=VMEM_LIMIT),
    )(sinks, q, k, k, v, v, bias, ng)


def _out_proj_kernel(x_ref, ys_ref, ya_ref, g1_ref, sc_ref, sh_ref, g2_ref, ng_ref, wos_ref, woa_ref,
                     rwh_ref, rwl_ref, sg_ref, su_ref, sd_ref, base_ref, h_ref, lg_ref):
    mix = (jnp.dot(ys_ref[...], wos_ref[...], preferred_element_type=F32)
           + jnp.dot(ya_ref[...], woa_ref[...], preferred_element_type=F32))
    x1 = x_ref[...] + g1_ref[0] * mix
    ms = jnp.mean(x1 * x1, axis=-1, keepdims=True)
    h = x1 * lax.rsqrt(ms + EPS) * ng_ref[...]
    h = h * (1.0 + sc_ref[0]) + sh_ref[0]
    half = h.shape[1] // 2
    h_ref[...] = _pack_bf16_pair(h[:, :half], h[:, half:])
    hi, lo = _split_hi_lo(h)
    logits = (jnp.dot(hi, rwh_ref[...], preferred_element_type=F32)
              + jnp.dot(lo, rwh_ref[...], preferred_element_type=F32)
              + jnp.dot(hi, rwl_ref[...], preferred_element_type=F32))
    lg_ref[...] = logits.T
    u = _silu(jnp.dot(hi, sg_ref[...], preferred_element_type=F32)) * jnp.dot(hi, su_ref[...],
                                                                              preferred_element_type=F32)
    shared = jnp.dot(u.astype(BF16), sd_ref[...], preferred_element_type=F32)
    base_ref[...] = x1 + g2_ref[0] * shared


def _out_proj_call(x2, ys, ya, g1, sc2, sh2, g2, ng, wos, woa, rwh, rwl, sg, su, sd, tiles_per_batch, tm):
    n, d = x2.shape
    row = lambda w: pl.BlockSpec((tm, w), lambda i: (i, 0))
    full = lambda a: pl.BlockSpec(a.shape, lambda i: (0, 0))
    per_batch = pl.BlockSpec((1, 1, d), lambda i: (i // tiles_per_batch, 0, 0))
    return pl.pallas_call(
        _out_proj_kernel,
        grid=(n // tm,),
        in_specs=[row(d), row(D_SSM), row(D_ATT), per_batch, per_batch, per_batch, per_batch, full(ng),
                  full(wos), full(woa), full(rwh), full(rwl), full(sg), full(su), full(sd)],
        out_specs=[row(d), row(d // 2), pl.BlockSpec((LANES, tm), lambda i: (0, i))],
        out_shape=[jax.ShapeDtypeStruct((n, d), F32), jax.ShapeDtypeStruct((n, d // 2), jnp.uint32),
                   jax.ShapeDtypeStruct((LANES, n), F32)],
        compiler_params=pltpu.CompilerParams(dimension_semantics=("arbitrary",),
                                             vmem_limit_bytes=VMEM_LIMIT),
    )(x2, ys, ya, g1, sc2, sh2, g2, ng, wos, woa, rwh, rwl, sg, su, sd)


def _route_kernel(lg_ref, rb_ref, upper_ref, idx_ref, gate_ref, rank_ref, cnt_ref, carry_ref):
    step = pl.program_id(0)

    @pl.when(step == 0)
    def _():
        carry_ref[...] = jnp.zeros_like(carry_ref)

    t = lg_ref.shape[1]
    per_group = N_EXPERTS // ROUTE_GROUPS
    scores = 1.0 / (1.0 + jnp.exp(-lg_ref[0:N_EXPERTS, :]))
    sel = scores + rb_ref[...]
    e_iota = lax.broadcasted_iota(jnp.int32, (N_EXPERTS, t), 0)

    sel3 = sel.reshape(ROUTE_GROUPS, per_group, t)
    w_iota = lax.broadcasted_iota(jnp.int32, sel3.shape, 1)
    m1 = jnp.max(sel3, axis=1, keepdims=True)
    first = jnp.min(jnp.where(sel3 == m1, w_iota, per_group), axis=1, keepdims=True)
    m2 = jnp.max(jnp.where(w_iota == first, NEG_INF, sel3), axis=1, keepdims=True)
    grp = (m1 + m2).reshape(ROUTE_GROUPS, t)

    g_iota = lax.broadcasted_iota(jnp.int32, (ROUTE_GROUPS, t), 0)
    gmask = jnp.zeros((ROUTE_GROUPS, t), jnp.bool_)
    for _ in range(ROUTE_TOPK_GROUPS):
        gm = jnp.max(grp, axis=0, keepdims=True)
        gfirst = jnp.min(jnp.where(grp == gm, g_iota, ROUTE_GROUPS), axis=0, keepdims=True)
        hit = g_iota == gfirst
        gmask = jnp.logical_or(gmask, hit)
        grp = jnp.where(hit, NEG_INF, grp)
    allowed = jnp.broadcast_to(gmask.reshape(ROUTE_GROUPS, 1, t),
                               (ROUTE_GROUPS, per_group, t)).reshape(N_EXPERTS, t)
    masked = jnp.where(allowed, sel, NEG_INF)

    picked = jnp.zeros((N_EXPERTS, t), jnp.bool_)
    idx_rows = []
    w_rows = []
    for _ in range(TOP_K):
        mm = jnp.max(masked, axis=0, keepdims=True)
        efirst = jnp.min(jnp.where(masked == mm, e_iota, N_EXPERTS), axis=0, keepdims=True)
        hit = e_iota == efirst
        idx_rows.append(efirst)
        w_rows.append(jnp.sum(jnp.where(hit, scores, 0.0), axis=0, keepdims=True))
        picked = jnp.logical_or(picked, hit)
        masked = jnp.where(hit, NEG_INF, masked)
    idx = jnp.concatenate(idx_rows, axis=0)
    w = jnp.concatenate(w_rows, axis=0)
    gate_ref[...] = w / jnp.sum(w, axis=0, keepdims=True) * ROUTED_SCALE
    idx_ref[...] = idx

    onehot = jnp.where(picked, 1.0, 0.0)
    before = jnp.dot(onehot.astype(BF16), upper_ref[...], preferred_element_type=F32)
    rank_full = before + carry_ref[:, 0:1]
    rank_rows = [jnp.sum(jnp.where(e_iota == idx_rows[k], rank_full, 0.0), axis=0, keepdims=True)
                 for k in range(TOP_K)]
    rank_ref[...] = jnp.concatenate(rank_rows, axis=0).astype(jnp.int32)
    carry_ref[...] = carry_ref[...] + jnp.sum(onehot, axis=1, keepdims=True)
    cnt_ref[...] = carry_ref[...]


def _route_call(logits_t, router_bias, upper, tile):
    n = logits_t.shape[1]
    tok = lambda r: pl.BlockSpec((r, tile), lambda i: (0, i))
    return pl.pallas_call(
        _route_kernel,
        grid=(n // tile,),
        in_specs=[tok(LANES), pl.BlockSpec((N_EXPERTS, 1), lambda i: (0, 0)),
                  pl.BlockSpec(upper.shape, lambda i: (0, 0))],
        out_specs=[tok(TOP_K), tok(TOP_K), tok(TOP_K), pl.BlockSpec((N_EXPERTS, LANES), lambda i: (0, 0))],
        out_shape=[jax.ShapeDtypeStruct((TOP_K, n), jnp.int32), jax.ShapeDtypeStruct((TOP_K, n), F32),
                   jax.ShapeDtypeStruct((TOP_K, n), jnp.int32),
                   jax.ShapeDtypeStruct((N_EXPERTS, LANES), F32)],
        scratch_shapes=[pltpu.VMEM((N_EXPERTS, LANES), F32)],
        compiler_params=pltpu.CompilerParams(dimension_semantics=("arbitrary",),
                                             vmem_limit_bytes=VMEM_LIMIT),
    )(logits_t, router_bias, upper)


def _dest_kernel(idx_ref, rank_ref, start_ref, dest_ref):
    t = idx_ref.shape[1]
    e_iota = lax.broadcasted_iota(jnp.int32, (N_EXPERTS, t), 0)
    rows = [jnp.sum(jnp.where(e_iota == idx_ref[k:k + 1, :], start_ref[...], 0), axis=0, keepdims=True)
            for k in range(TOP_K)]
    dest_ref[...] = jnp.concatenate(rows, axis=0) + rank_ref[...]


def _dest_call(idx, rank, pad_start, tile):
    n = idx.shape[1]
    tok = pl.BlockSpec((TOP_K, tile), lambda i: (0, i))
    return pl.pallas_call(
        _dest_kernel,
        grid=(n // tile,),
        in_specs=[tok, tok, pl.BlockSpec((N_EXPERTS, 1), lambda i: (0, 0))],
        out_specs=tok,
        out_shape=jax.ShapeDtypeStruct((TOP_K, n), jnp.int32),
        compiler_params=pltpu.CompilerParams(dimension_semantics=("arbitrary",)),
    )(idx, rank, pad_start)


def _scatter_rows_sc(rows, dest_flat, total_rows, chunk):
    n, w = rows.shape
    copies = dest_flat.shape[0] // n
    info = plsc.get_sparse_core_info()
    nc = info.num_cores
    per_worker = n // (nc * info.num_subcores)
    assert per_worker * nc * info.num_subcores == n and per_worker % chunk == 0
    mesh = plsc.VectorSubcoreMesh(core_axis_name="c", subcore_axis_name="s")

    @functools.partial(
        pl.kernel, mesh=mesh,
        out_type=jax.ShapeDtypeStruct((total_rows, w), rows.dtype),
        scratch_types=[pltpu.VMEM((chunk,), jnp.int32), pltpu.VMEM((chunk, w), rows.dtype),
                       pltpu.SemaphoreType.DMA],
    )
    def scatter(rows_hbm, idx_hbm, out_hbm, idx_v, rows_v, sem):
        worker = lax.axis_index("s") * nc + lax.axis_index("c")

        @pl.loop(0, per_worker // chunk)
        def _(j):
            base = worker * per_worker + j * chunk
            pltpu.sync_copy(rows_hbm.at[pl.ds(base, chunk)], rows_v)
            for k in range(copies):
                pltpu.sync_copy(idx_hbm.at[pl.ds(k * n + base, chunk)], idx_v)
                pltpu.async_copy(rows_v, out_hbm.at[idx_v], sem).wait()

    return scatter(rows, dest_flat)


X_RING = 3


def _expert_kernel(be_ref, nused_ref, fresh_ref, valid_ref, xs_hbm, wg_ref, wu_ref, wd_ref, y_ref, wgb_ref, wub_ref,
                   wdb_ref, xbuf_ref, xsem):
    i = pl.program_id(0)
    rows, half = xbuf_ref.shape[1], xbuf_ref.shape[2]
    nused = nused_ref[0]

    def x_copy(block):
        first = pl.multiple_of(block * rows, rows)
        slot = block % X_RING
        return pltpu.make_async_copy(xs_hbm.at[pl.ds(first, rows)], xbuf_ref.at[slot], xsem.at[slot])

    @pl.when(i == 0)
    def _():
        for b in range(X_RING - 1):
            @pl.when(b < nused)
            def _():
                x_copy(b).start()

    @pl.when(i + (X_RING - 1) < nused)
    def _():
        x_copy(i + (X_RING - 1)).start()

    @pl.when(fresh_ref[i] > 0)
    def _():
        wgb_ref[...] = wg_ref[0].astype(BF16)
        wub_ref[...] = wu_ref[0].astype(BF16)
        wdb_ref[...] = wd_ref[0].astype(BF16)

    @pl.when(i < nused)
    def _():
        x_copy(i).wait()
        xw = xbuf_ref[i % X_RING]
        row = lax.broadcasted_iota(jnp.int32, xw.shape, 0)
        x_lo, x_hi = _unpack_bf16_pair(jnp.where(row < valid_ref[i], xw, jnp.uint32(0)))
        x_lo = x_lo.astype(BF16)
        x_hi = x_hi.astype(BF16)
        gate = (jnp.dot(x_lo, wgb_ref[:half, :], preferred_element_type=F32)
                + jnp.dot(x_hi, wgb_ref[half:, :], preferred_element_type=F32))
        up = (jnp.dot(x_lo, wub_ref[:half, :], preferred_element_type=F32)
              + jnp.dot(x_hi, wub_ref[half:, :], preferred_element_type=F32))
        u = (_silu(gate) * up).astype(BF16)
        y_lo = jnp.dot(u, wdb_ref[:, :half], preferred_element_type=F32)
        y_hi = jnp.dot(u, wdb_ref[:, half:], preferred_element_type=F32)
        y_ref[...] = _pack_bf16_pair(y_lo, y_hi)

    @pl.when(i >= nused_ref[0])
    def _():
        y_ref[...] = jnp.zeros_like(y_ref)


def _expert_call(block_expert, nused, fresh, valid, xs, wg, wu, wd, rows):
    p, w = xs.shape
    d, f = wg.shape[1], wg.shape[2]
    w_map = lambda i, be, nu, fr, va: (be[i], 0, 0)
    grid_spec = pltpu.PrefetchScalarGridSpec(
        num_scalar_prefetch=4,
        grid=(p // rows,),
        in_specs=[pl.BlockSpec(memory_space=pl.ANY),
                  pl.BlockSpec((1, d, f), w_map), pl.BlockSpec((1, d, f), w_map), pl.BlockSpec((1, f, d), w_map)],
        out_specs=pl.BlockSpec((rows, w), lambda i, be, nu, fr, va: (i, 0)),
        scratch_shapes=[pltpu.VMEM((d, f), BF16), pltpu.VMEM((d, f), BF16), pltpu.VMEM((f, d), BF16),
                        pltpu.VMEM((X_RING, rows, w), xs.dtype), pltpu.SemaphoreType.DMA((X_RING,))],
    )
    return pl.pallas_call(
        _expert_kernel,
        grid_spec=grid_spec,
        out_shape=jax.ShapeDtypeStruct((p, w), jnp.uint32),
        compiler_params=pltpu.CompilerParams(dimension_semantics=("arbitrary",),
                                             vmem_limit_bytes=VMEM_LIMIT),
    )(block_expert, nused, fresh, valid, xs, wg, wu, wd)


def _gather_rows_sc(table, idx, chunk):
    m = idx.shape[0]
    w = table.shape[1]
    info = plsc.get_sparse_core_info()
    nc = info.num_cores
    per_worker = m // (nc * info.num_subcores)
    assert per_worker * nc * info.num_subcores == m and per_worker % chunk == 0
    mesh = plsc.VectorSubcoreMesh(core_axis_name="c", subcore_axis_name="s")

    @functools.partial(
        pl.kernel, mesh=mesh,
        out_type=jax.ShapeDtypeStruct((m, w), table.dtype),
        scratch_types=[pltpu.VMEM((chunk,), jnp.int32), pltpu.VMEM((chunk, w), table.dtype),
                       pltpu.SemaphoreType.DMA],
    )
    def gather(table_hbm, idx_hbm, out_hbm, idx_v, rows_v, sem):
        worker = lax.axis_index("s") * nc + lax.axis_index("c")

        @pl.loop(0, per_worker // chunk)
        def _(j):
            base = worker * per_worker + j * chunk
            pltpu.sync_copy(idx_hbm.at[pl.ds(base, chunk)], idx_v)
            pltpu.async_copy(table_hbm.at[idx_v], rows_v, sem).wait()
            pltpu.sync_copy(rows_v, out_hbm.at[pl.ds(base, chunk)])

    return gather(table, idx)


def _combine_kernel(yk_ref, gate_ref, base_ref, g2_ref, fg_ref, o_ref):
    t = base_ref.shape[0]
    half = yk_ref.shape[2]
    gates = gate_ref[...]
    r_lo = jnp.zeros((t, half), F32)
    r_hi = jnp.zeros((t, half), F32)
    for k in range(TOP_K):
        y_lo, y_hi = _unpack_bf16_pair(yk_ref[k])
        r_lo = r_lo + gates[:, k:k + 1] * y_lo
        r_hi = r_hi + gates[:, k:k + 1] * y_hi
    g2 = g2_ref[0]
    x_lo = base_ref[:, :half] + g2[:, :half] * r_lo
    x_hi = base_ref[:, half:] + g2[:, half:] * r_hi
    ms = (jnp.sum(x_lo * x_lo, axis=-1, keepdims=True)
          + jnp.sum(x_hi * x_hi, axis=-1, keepdims=True)) * (1.0 / (2 * half))
    inv = lax.rsqrt(ms + EPS)
    o_ref[:, :half] = x_lo * inv * fg_ref[:, :half]
    o_ref[:, half:] = x_hi * inv * fg_ref[:, half:]


def _combine_call(yk, gates_t, base, g2, fg, tiles_per_batch, tile):
    n, d = base.shape
    row = lambda w: pl.BlockSpec((tile, w), lambda i: (i, 0))
    return pl.pallas_call(
        _combine_kernel,
        grid=(n // tile,),
        in_specs=[pl.BlockSpec((TOP_K, tile, yk.shape[2]), lambda i: (0, i, 0)),
                  row(TOP_K), row(d),
                  pl.BlockSpec((1, 1, d), lambda i: (i // tiles_per_batch, 0, 0)),
                  pl.BlockSpec((1, d), lambda i: (0, 0))],
        out_specs=row(d),
        out_shape=jax.ShapeDtypeStruct((n, d), F32),
        compiler_params=pltpu.CompilerParams(dimension_semantics=("arbitrary",),
                                             vmem_limit_bytes=VMEM_LIMIT),
    )(yk, gates_t, base, g2, fg)


def _pad_cols(a, width):
    return jnp.pad(a, ((0, 0), (0, width - a.shape[1])))


def _layer(x, c_pad, mod_w, mod_b, norm1_g, norm2_g, w_in, conv_w, conv_b, dt_bias, a_log, d_skip, ssm_norm_g,
           att_norm_g, sinks, rel_bias, w_out, router_w, router_bias, exp_w_gate, exp_w_up, exp_w_down,
           sh_w_gate, sh_w_up, sh_w_down, final_g):
    bsz, l, d = x.shape
    n = bsz * l
    tm = min(ROW_TILE, l)

    mod = _mod_call(c_pad, mod_w, mod_b[None, :])[:bsz]
    sh1, sc1, g1, sh2, sc2, g2 = [m[:, None, :] for m in jnp.split(mod, 6, axis=-1)]

    i1 = D_SSM
    i2 = i1 + CONV_CH
    i3 = i2 + SSM_HEADS
    i4 = i3 + D_ATT
    i5 = i4 + D_KV
    wz, wx, wdt, wq, wk, wv = jnp.split(w_in, [i1, i2, i3, i4, i5], axis=-1)
    wdt = _pad_cols(wdt, LANES)
    x2 = x.reshape(n, d)
    z, xbc, dt, q, k, v = _in_proj_call(x2, sc1, sh1, norm1_g[None, :], wz.astype(BF16), wx.astype(BF16),
                                        wdt.astype(BF16), wq.astype(BF16), wk.astype(BF16), wv.astype(BF16),
                                        l // tm, tm)

    tril = jnp.asarray(np.tril(np.ones((CHUNK, CHUNK), np.float32)))
    y_ssm = _ssd_call(xbc.reshape(bsz, l, CONV_CH), z.reshape(bsz, l, D_SSM), dt.reshape(bsz, l, LANES),
                      conv_w, conv_b[None, :], _pad_cols(dt_bias[None, :], LANES), _pad_cols(a_log[None, :], LANES),
                      jnp.repeat(d_skip, SSM_HEAD_DIM)[None, :], ssm_norm_g[None, :], tril)

    bucket, win = _rel_bucket_table()
    bias = _bias_call(rel_bias, jnp.asarray(bucket), jnp.asarray(win))
    y_att = _attn_call(sinks, q.reshape(bsz, l, D_ATT), k.reshape(bsz, l, D_KV), v.reshape(bsz, l, D_KV), bias,
                       att_norm_g[None, :])

    rw = _pad_cols(router_w, LANES)
    rwh = rw.astype(BF16)
    rwl = (rw - rwh.astype(F32)).astype(BF16)
    base, h2, logits_t = _out_proj_call(
        x2, y_ssm.reshape(n, D_SSM), y_att.reshape(n, D_ATT), g1, sc2, sh2, g2, norm2_g[None, :],
        w_out[:D_SSM].astype(BF16), w_out[D_SSM:].astype(BF16), rwh, rwl,
        sh_w_gate.astype(BF16), sh_w_up.astype(BF16), sh_w_down.astype(BF16), l // tm, tm)

    rt = min(ROUTE_TILE, n)
    upper = jnp.asarray(np.triu(np.ones((rt, rt), np.float32), 1)).astype(BF16)
    idx, gates, rank, counts = _route_call(logits_t, router_bias[:, None], upper, rt)

    nblocks = (n * TOP_K + N_EXPERTS * (EXPERT_ROWS - 1) + EXPERT_ROWS - 1) // EXPERT_ROWS
    cnt = counts[:, 0].astype(jnp.int32)
    padded = (cnt + EXPERT_ROWS - 1) // EXPERT_ROWS * EXPERT_ROWS
    pad_end = jnp.cumsum(padded)
    pad_start = pad_end - padded
    dest = _dest_call(idx, rank, pad_start[:, None], rt)
    block_first_row = jnp.arange(nblocks, dtype=jnp.int32) * EXPERT_ROWS
    block_expert = jnp.minimum(jnp.sum((pad_end[None, :] <= block_first_row[:, None]).astype(jnp.int32), axis=1),
                               N_EXPERTS - 1)
    fresh = jnp.concatenate([jnp.ones((1,), jnp.int32),
                             (block_expert[1:] != block_expert[:-1]).astype(jnp.int32)])
    nused = (pad_end[-1:] // EXPERT_ROWS).astype(jnp.int32)
    seg_row = block_first_row - jnp.sum(jnp.where(pad_end[None, :] <= block_first_row[:, None], padded[None, :], 0),
                                        axis=1)
    block_cnt = jnp.sum(jnp.where(block_expert[:, None] == jnp.arange(N_EXPERTS, dtype=jnp.int32)[None, :],
                                  cnt[None, :], 0), axis=1)
    valid = jnp.clip(block_cnt - seg_row, 0, EXPERT_ROWS).astype(jnp.int32)

    dest_flat = dest.reshape(-1)
    xs = _scatter_rows_sc(h2, dest_flat, nblocks * EXPERT_ROWS, SC_CHUNK)
    ys = _expert_call(block_expert, nused, fresh, valid, xs, exp_w_gate, exp_w_up, exp_w_down, EXPERT_ROWS)
    yk = _gather_rows_sc(ys, dest_flat, SC_CHUNK).reshape(TOP_K, n, ys.shape[1])
    ctile = min(COMBINE_TILE, l)
    out = _combine_call(yk, gates.T, base, g2, final_g[None, :], l // ctile, ctile)
    return out.reshape(bsz, l, d)


def kernel(x, c, mod_w, mod_b, norm1_g, norm2_g, w_in, conv_w, conv_b, dt_bias, a_log, d_skip, ssm_norm_g,
           att_norm_g, sinks, rel_bias, w_out, router_w, router_bias, exp_w_gate, exp_w_up, exp_w_down,
           sh_w_gate, sh_w_up, sh_w_down, final_g):
    assert mod_w.shape[0] == 1, "single-layer block"
    bsz = x.shape[0]
    c_pad = jnp.pad(c, ((0, SUBLANES - bsz % SUBLANES if bsz % SUBLANES else 0), (0, 0)))
    return _layer(x, c_pad, mod_w[0], mod_b[0], norm1_g[0], norm2_g[0], w_in[0], conv_w[0], conv_b[0],
                  dt_bias[0], a_log[0], d_skip[0], ssm_norm_g[0], att_norm_g[0], sinks[0], rel_bias, w_out[0],
                  router_w[0], router_bias[0], exp_w_gate[0], exp_w_up[0], exp_w_down[0], sh_w_gate[0],
                  sh_w_up[0], sh_w_down[0], final_g)
```

```python
import functools
import math

import numpy as np
import jax
import jax.numpy as jnp
from jax import lax
from jax.experimental import pallas as pl
from jax.experimental.pallas import tpu as pltpu
from jax.experimental.pallas import tpu_sc as plsc

F32 = jnp.float32
BF16 = jnp.bfloat16

D_MODEL = 1024
SSM_HEAD_DIM = 64
D_SSM = D_MODEL
SSM_HEADS = D_SSM // SSM_HEAD_DIM
SSM_GROUPS = 4
D_STATE = 128
CONV_K = 4
CONV_CH = D_SSM + 2 * SSM_GROUPS * D_STATE
CHUNK = 128
ATT_HEAD_DIM = 64
D_ATT = D_MODEL
ATT_HEADS = D_ATT // ATT_HEAD_DIM
KV_HEADS = ATT_HEADS // 4
Q_PER_KV = ATT_HEADS // KV_HEADS
D_KV = KV_HEADS * ATT_HEAD_DIM
WINDOW = 128
ATT_BLOCK = 128
REL_BUCKETS = 32
REL_MAX_DIST = 128
N_EXPERTS = 64
TOP_K = 8
EXPERT_DIM = D_MODEL // 4
SHARED_DIM = D_MODEL // 4
ROUTE_GROUPS = 8
ROUTE_TOPK_GROUPS = 4
ROUTED_SCALE = 2.5
EPS = 1e-6

LANES = 128
SUBLANES = 8
HALF = LANES // 2

ROW_TILE = 512
ROUTE_TILE = 512
COMBINE_TILE = 256
SC_CHUNK = 128
EXPERT_ROWS = 512
SEQ_GROUPS = 2
VMEM_LIMIT = 48 * 1024 * 1024

NEG_INF = float("-inf")


def _silu(v):
    return v * (1.0 / (1.0 + jnp.exp(-v)))


def _softplus(v):
    return jnp.maximum(v, 0.0) + jnp.log(1.0 + jnp.exp(-jnp.abs(v)))


def _bdot(a, b):
    return jnp.dot(a.astype(BF16), b.astype(BF16), preferred_element_type=F32)


def _split_hi_lo(v):
    hi = v.astype(BF16)
    lo = (v - hi.astype(F32)).astype(BF16)
    return hi, lo


def _pack_bf16_pair(a, b):
    w = pltpu.pack_elementwise([a, b], packed_dtype=BF16)
    return w if w.dtype == jnp.uint32 else lax.bitcast_convert_type(w, jnp.uint32)


def _unpack_bf16_pair(w):
    a = pltpu.unpack_elementwise(w, index=0, packed_dtype=BF16, unpacked_dtype=F32)
    b = pltpu.unpack_elementwise(w, index=1, packed_dtype=BF16, unpacked_dtype=F32)
    return a, b


def _lane_half_mask(shape):
    return lax.broadcasted_iota(jnp.int32, shape, len(shape) - 1) < HALF


def _mod_kernel(c_ref, w_ref, b_ref, o_ref):
    a = _silu(c_ref[...])
    o_ref[...] = jnp.dot(a, w_ref[...], precision=lax.Precision.HIGHEST,
                         preferred_element_type=F32) + b_ref[...]


def _mod_call(c_pad, mod_w, mod_b):
    rows, d = c_pad.shape
    cols = mod_w.shape[1]
    return pl.pallas_call(
        _mod_kernel,
        grid=(cols // d,),
        in_specs=[pl.BlockSpec((rows, d), lambda j: (0, 0)),
                  pl.BlockSpec((d, d), lambda j: (0, j)),
                  pl.BlockSpec((1, d), lambda j: (0, j))],
        out_specs=pl.BlockSpec((rows, d), lambda j: (0, j)),
        out_shape=jax.ShapeDtypeStruct((rows, cols), F32),
        compiler_params=pltpu.CompilerParams(dimension_semantics=("arbitrary",),
                                             vmem_limit_bytes=VMEM_LIMIT),
    )(c_pad, mod_w, mod_b)


def _in_proj_kernel(x_ref, sc_ref, sh_ref, g_ref, wz_ref, wx_ref, wdt_ref, wq_ref, wk_ref, wv_ref,
                    z_ref, xbc_ref, dt_ref, q_ref, k_ref, v_ref):
    xf = x_ref[...]
    ms = jnp.mean(xf * xf, axis=-1, keepdims=True)
    h = xf * lax.rsqrt(ms + EPS) * g_ref[...]
    h = h * (1.0 + sc_ref[0]) + sh_ref[0]
    hb = h.astype(BF16)
    z_ref[...] = jnp.dot(hb, wz_ref[...], preferred_element_type=F32).astype(BF16)
    xbc_ref[...] = jnp.dot(hb, wx_ref[...], preferred_element_type=F32).astype(BF16)
    dt_ref[...] = jnp.dot(hb, wdt_ref[...], preferred_element_type=F32)
    q_ref[...] = jnp.dot(hb, wq_ref[...], preferred_element_type=F32).astype(BF16)
    k_ref[...] = jnp.dot(hb, wk_ref[...], preferred_element_type=F32).astype(BF16)
    v_ref[...] = jnp.dot(hb, wv_ref[...], preferred_element_type=F32).astype(BF16)


def _in_proj_call(x2, sc1, sh1, g1n, wz, wx, wdt, wq, wk, wv, tiles_per_batch, tm):
    n, d = x2.shape
    row = lambda w: pl.BlockSpec((tm, w), lambda i: (i, 0))
    full = lambda a: pl.BlockSpec(a.shape, lambda i: (0, 0))
    per_batch = pl.BlockSpec((1, 1, d), lambda i: (i // tiles_per_batch, 0, 0))
    outs = [(wz.shape[1], BF16), (wx.shape[1], BF16), (wdt.shape[1], F32),
            (wq.shape[1], BF16), (wk.shape[1], BF16), (wv.shape[1], BF16)]
    return pl.pallas_call(
        _in_proj_kernel,
        grid=(n // tm,),
        in_specs=[row(d), per_batch, per_batch, full(g1n), full(wz), full(wx), full(wdt), full(wq),
                  full(wk), full(wv)],
        out_specs=[row(w) for w, _ in outs],
        out_shape=[jax.ShapeDtypeStruct((n, w), dt) for w, dt in outs],
        compiler_params=pltpu.CompilerParams(dimension_semantics=("arbitrary",),
                                             vmem_limit_bytes=VMEM_LIMIT),
    )(x2, sc1, sh1, g1n, wz, wx, wdt, wq, wk, wv)


def _ssd_kernel(xbc_ref, z_ref, dt_ref, cw_ref, cb_ref, dtb_ref, alog_ref, dskip_ref, ng_ref, tril_ref,
                y_ref, state_ref, tail_ref, ext_ref, ybuf_ref):
    c = pl.program_id(1)

    @pl.when(c == 0)
    def _():
        state_ref[...] = jnp.zeros_like(state_ref)
        tail_ref[...] = jnp.zeros_like(tail_ref)

    u = xbc_ref[0].astype(F32)
    ext_ref[0:SUBLANES, :] = tail_ref[...]
    ext_ref[SUBLANES:SUBLANES + CHUNK, :] = u
    tail_ref[...] = u[CHUNK - SUBLANES:, :]
    acc = cb_ref[...] + jnp.zeros_like(u)
    for kk in range(CONV_K):
        off = SUBLANES - (CONV_K - 1) + kk
        acc = acc + cw_ref[kk:kk + 1, :] * ext_ref[off:off + CHUNK, :]
    act = _silu(acc)
    xs = act[:, :D_SSM]
    gn = SSM_GROUPS * D_STATE

    dt = _softplus(dt_ref[0] + dtb_ref[...])
    a = dt * (-jnp.exp(alog_ref[...]))
    cs = jnp.dot(tril_ref[...], a, precision=lax.Precision.HIGHEST, preferred_element_type=F32)
    cs_t = cs.T
    dt_t = dt.T
    exp_cs = jnp.exp(cs)
    cs_last = cs[CHUNK - 1:CHUNK, :]
    chunk_decay = jnp.exp(cs_last)

    li = lax.broadcasted_iota(jnp.int32, (CHUNK, CHUNK), 0)
    si = lax.broadcasted_iota(jnp.int32, (CHUNK, CHUNK), 1)
    causal = li >= si
    low = _lane_half_mask((CHUNK, LANES))
    low_row = _lane_half_mask((1, LANES))

    heads_per_group = SSM_HEADS // SSM_GROUPS
    for g in range(SSM_GROUPS):
        b_g = act[:, D_SSM + g * D_STATE:D_SSM + (g + 1) * D_STATE]
        c_g = act[:, D_SSM + gn + g * D_STATE:D_SSM + gn + (g + 1) * D_STATE]
        b_gb = b_g.astype(BF16)
        c_gb = c_g.astype(BF16)
        cb = lax.dot_general(c_gb, b_gb, (((1,), (1,)), ((), ())), preferred_element_type=F32)
        b_t = b_g.T
        for jp in range(heads_per_group // 2):
            j = g * (heads_per_group // 2) + jp
            lanes = slice(j * LANES, (j + 1) * LANES)
            xp = xs[:, lanes]
            ydiag = jnp.zeros((CHUNK, LANES), F32)
            snew = jnp.zeros((D_STATE, LANES), F32)
            for half in range(2):
                h = 2 * j + half
                row_dt = dt_t[h:h + 1, :]
                diff = cs[:, h:h + 1] - cs_t[h:h + 1, :]
                lmat = jnp.exp(jnp.where(causal, diff, NEG_INF))
                m = (cb * lmat * row_dt).astype(BF16)
                keep = low if half == 0 else jnp.logical_not(low)
                xh = jnp.where(keep, xp, 0.0).astype(BF16)
                ydiag = ydiag + jnp.dot(m, xh, preferred_element_type=F32)
                w_t = jnp.exp(cs_t[h:h + 1, CHUNK - 1:CHUNK] - cs_t[h:h + 1, :]) * row_dt
                snew = snew + jnp.dot((b_t * w_t).astype(BF16), xh, preferred_element_type=F32)
            s_in = state_ref[:, lanes]
            yoff = jnp.dot(c_gb, s_in.astype(BF16), preferred_element_type=F32)
            h0 = 2 * j
            escale = jnp.where(low, exp_cs[:, h0:h0 + 1], exp_cs[:, h0 + 1:h0 + 2])
            cdec = jnp.where(low_row, chunk_decay[:, h0:h0 + 1], chunk_decay[:, h0 + 1:h0 + 2])
            ybuf_ref[:, lanes] = ydiag + yoff * escale + xp * dskip_ref[:, lanes]
            state_ref[:, lanes] = s_in * cdec + snew

    yz = ybuf_ref[...] * _silu(z_ref[0].astype(F32))
    gw = D_SSM // SSM_GROUPS
    for g in range(SSM_GROUPS):
        part = yz[:, g * gw:(g + 1) * gw]
        ms = jnp.mean(part * part, axis=-1, keepdims=True)
        y_ref[0, :, g * gw:(g + 1) * gw] = (part * lax.rsqrt(ms + EPS)
                                            * ng_ref[:, g * gw:(g + 1) * gw]).astype(BF16)


def _ssd_call(xbc, z, dt, conv_w, conv_b, dtb, alog, dskip, ng, tril):
    bsz, l, _ = xbc.shape
    nc = l // CHUNK
    chunk = lambda w: pl.BlockSpec((1, CHUNK, w), lambda b, c: (b, c, 0))
    full = lambda a: pl.BlockSpec(a.shape, lambda b, c: (0, 0))
    return pl.pallas_call(
        _ssd_kernel,
        grid=(bsz, nc),
        in_specs=[chunk(CONV_CH), chunk(D_SSM), chunk(LANES), full(conv_w), full(conv_b), full(dtb),
                  full(alog), full(dskip), full(ng), full(tril)],
        out_specs=chunk(D_SSM),
        out_shape=jax.ShapeDtypeStruct((bsz, l, D_SSM), BF16),
        scratch_shapes=[pltpu.VMEM((D_STATE, D_SSM), F32),
                        pltpu.VMEM((SUBLANES, CONV_CH), F32),
                        pltpu.VMEM((SUBLANES + CHUNK, CONV_CH), F32),
                        pltpu.VMEM((CHUNK, D_SSM), F32)],
        compiler_params=pltpu.CompilerParams(dimension_semantics=("arbitrary", "arbitrary"),
                                             vmem_limit_bytes=VMEM_LIMIT),
    )(xbc, z, dt, conv_w, conv_b, dtb, alog, dskip, ng, tril)


def _rel_bucket_table():
    qi = np.arange(ATT_BLOCK)[:, None]
    sj = np.arange(2 * ATT_BLOCK)[None, :]
    dist = qi + ATT_BLOCK - sj
    in_win = (dist >= 0) & (dist < WINDOW)
    dcl = np.maximum(dist, 0)
    max_exact = REL_BUCKETS // 2
    d = np.maximum(dcl, 1).astype(np.float32)
    large = max_exact + (np.log(d / np.float32(max_exact)) / np.float32(math.log(REL_MAX_DIST / max_exact))
                         * np.float32(REL_BUCKETS - max_exact)).astype(np.int32)
    large = np.minimum(large, REL_BUCKETS - 1)
    bucket = np.where(dcl < max_exact, dcl, large).astype(np.int32)
    return bucket, in_win.astype(np.int32)


def _bias_kernel(rb_ref, bucket_ref, win_ref, o_ref):
    h = pl.program_id(0)
    bucket = bucket_ref[...]
    acc = jnp.zeros(bucket.shape, F32)
    for b in range(REL_BUCKETS):
        acc = jnp.where(bucket == b, rb_ref[b, h], acc)
    o_ref[0] = jnp.where(win_ref[...] > 0, acc, NEG_INF)


def _bias_call(rel_bias, bucket, win):
    return pl.pallas_call(
        _bias_kernel,
        grid=(ATT_HEADS,),
        in_specs=[pl.BlockSpec(memory_space=pltpu.SMEM),
                  pl.BlockSpec(bucket.shape, lambda h: (0, 0)),
                  pl.BlockSpec(win.shape, lambda h: (0, 0))],
        out_specs=pl.BlockSpec((1,) + bucket.shape, lambda h: (h, 0, 0)),
        out_shape=jax.ShapeDtypeStruct((ATT_HEADS,) + bucket.shape, F32),
        compiler_params=pltpu.CompilerParams(dimension_semantics=("arbitrary",)),
    )(rel_bias, bucket, win)


def _attn_kernel(sink_ref, q_ref, kp_ref, kc_ref, vp_ref, vc_ref, bias_ref, ng_ref, o_ref, obuf_ref):
    i = pl.program_id(1)
    kband = jnp.concatenate([kp_ref[0], kc_ref[0]], axis=0).astype(F32)
    vband = jnp.concatenate([vp_ref[0], vc_ref[0]], axis=0).astype(F32)
    sj = lax.broadcasted_iota(jnp.int32, (ATT_BLOCK, 2 * ATT_BLOCK), 1)
    valid = jnp.logical_or(sj >= ATT_BLOCK, i > 0)
    low = _lane_half_mask((ATT_BLOCK, LANES))
    scale = ATT_HEAD_DIM ** -0.5

    kv_cols = []
    for cpair in range(KV_HEADS // 2):
        kk = kband[:, cpair * LANES:(cpair + 1) * LANES]
        vv = vband[:, cpair * LANES:(cpair + 1) * LANES]
        kv_cols.append(((kk.astype(BF16), pltpu.roll(kk, HALF, 1).astype(BF16)),
                        (vv.astype(BF16), pltpu.roll(vv, HALF, 1).astype(BF16))))

    for j in range(ATT_HEADS // 2):
        qp = q_ref[0, :, j * LANES:(j + 1) * LANES]
        out_pair = jnp.zeros((ATT_BLOCK, LANES), F32)
        for half in range(2):
            h = 2 * j + half
            g = h // Q_PER_KV
            swapped = int((g % 2) != half)
            ksel = kv_cols[g // 2][0][swapped]
            vsel = kv_cols[g // 2][1][swapped]
            keep = low if half == 0 else jnp.logical_not(low)
            qh = jnp.where(keep, qp, jnp.zeros_like(qp))
            s = lax.dot_general(qh, ksel, (((1,), (1,)), ((), ())), preferred_element_type=F32)
            s = s * scale + bias_ref[h]
            s = jnp.where(valid, s, NEG_INF)
            sink = sink_ref[h]
            m = jnp.maximum(jnp.max(s, axis=-1, keepdims=True), sink)
            p = jnp.exp(s - m)
            denom = jnp.sum(p, axis=-1, keepdims=True) + jnp.exp(sink - m)
            o = jnp.dot(p.astype(BF16), vsel, preferred_element_type=F32) / denom
            out_pair = out_pair + jnp.where(keep, o, 0.0)
        obuf_ref[:, j * LANES:(j + 1) * LANES] = out_pair

    att = obuf_ref[...]
    ms = jnp.mean(att * att, axis=-1, keepdims=True)
    o_ref[0] = (att * lax.rsqrt(ms + EPS) * ng_ref[...]).astype(BF16)


def _attn_call(sinks, q, k, v, bias, ng):
    bsz, l, _ = q.shape
    nb = l // ATT_BLOCK
    cur = lambda w: pl.BlockSpec((1, ATT_BLOCK, w), lambda b, i: (b, i, 0))
    prev = lambda w: pl.BlockSpec((1, ATT_BLOCK, w), lambda b, i: (b, jnp.maximum(i - 1, 0), 0))
    return pl.pallas_call(
        _attn_kernel,
        grid=(bsz, nb),
        in_specs=[pl.BlockSpec(memory_space=pltpu.SMEM),
                  cur(D_ATT), prev(D_KV), cur(D_KV), prev(D_KV), cur(D_KV),
                  pl.BlockSpec(bias.shape, lambda b, i: (0, 0, 0)),
                  pl.BlockSpec(ng.shape, lambda b, i: (0, 0))],
        out_specs=cur(D_ATT),
        out_shape=jax.ShapeDtypeStruct((bsz, l, D_ATT), BF16),
        scratch_shapes=[pltpu.VMEM((ATT_BLOCK, D_ATT), F32)],
        compiler_params=pltpu.CompilerParams(dimension_semantics=("arbitrary", "arbitrary"),
                                             vmem_limit_bytes=VMEM_LIMIT),
    )(sinks, q, k, k, v, v, bias, ng)


def _out_proj_kernel(x_ref, ys_ref, ya_ref, g1_ref, sc_ref, sh_ref, g2_ref, ng_ref, wos_ref, woa_ref,
                     rwh_ref, rwl_ref, sg_ref, su_ref, sd_ref, base_ref, h_ref, lg_ref):
    mix = (jnp.dot(ys_ref[...], wos_ref[...], preferred_element_type=F32)
           + jnp.dot(ya_ref[...], woa_ref[...], preferred_element_type=F32))
    x1 = x_ref[...] + g1_ref[0] * mix
    ms = jnp.mean(x1 * x1, axis=-1, keepdims=True)
    h = x1 * lax.rsqrt(ms + EPS) * ng_ref[...]
    h = h * (1.0 + sc_ref[0]) + sh_ref[0]
    half = h.shape[1] // 2
    h_ref[...] = _pack_bf16_pair(h[:, :half], h[:, half:])
    hi, lo = _split_hi_lo(h)
    logits = (jnp.dot(hi, rwh_ref[...], preferred_element_type=F32)
              + jnp.dot(lo, rwh_ref[...], preferred_element_type=F32)
              + jnp.dot(hi, rwl_ref[...], preferred_element_type=F32))
    lg_ref[...] = logits.T
    u = _silu(jnp.dot(hi, sg_ref[...], preferred_element_type=F32)) * jnp.dot(hi, su_ref[...],
                                                                              preferred_element_type=F32)
    shared = jnp.dot(u.astype(BF16), sd_ref[...], preferred_element_type=F32)
    base_ref[...] = x1 + g2_ref[0] * shared


def _out_proj_call(x2, ys, ya, g1, sc2, sh2, g2, ng, wos, woa, rwh, rwl, sg, su, sd, tiles_per_batch, tm):
    n, d = x2.shape
    row = lambda w: pl.BlockSpec((tm, w), lambda i: (i, 0))
    full = lambda a: pl.BlockSpec(a.shape, lambda i: (0, 0))
    per_batch = pl.BlockSpec((1, 1, d), lambda i: (i // tiles_per_batch, 0, 0))
    return pl.pallas_call(
        _out_proj_kernel,
        grid=(n // tm,),
        in_specs=[row(d), row(D_SSM), row(D_ATT), per_batch, per_batch, per_batch, per_batch, full(ng),
                  full(wos), full(woa), full(rwh), full(rwl), full(sg), full(su), full(sd)],
        out_specs=[row(d), row(d // 2), pl.BlockSpec((LANES, tm), lambda i: (0, i))],
        out_shape=[jax.ShapeDtypeStruct((n, d), F32), jax.ShapeDtypeStruct((n, d // 2), jnp.uint32),
                   jax.ShapeDtypeStruct((LANES, n), F32)],
        compiler_params=pltpu.CompilerParams(dimension_semantics=("arbitrary",),
                                             vmem_limit_bytes=VMEM_LIMIT),
    )(x2, ys, ya, g1, sc2, sh2, g2, ng, wos, woa, rwh, rwl, sg, su, sd)


def _route_kernel(lg_ref, rb_ref, upper_ref, idx_ref, gate_ref, rank_ref, cnt_ref, carry_ref):
    step = pl.program_id(0)

    @pl.when(step == 0)
    def _():
        carry_ref[...] = jnp.zeros_like(carry_ref)

    t = lg_ref.shape[1]
    per_group = N_EXPERTS // ROUTE_GROUPS
    scores = 1.0 / (1.0 + jnp.exp(-lg_ref[0:N_EXPERTS, :]))
    sel = scores + rb_ref[...]
    e_iota = lax.broadcasted_iota(jnp.int32, (N_EXPERTS, t), 0)

    sel3 = sel.reshape(ROUTE_GROUPS, per_group, t)
    w_iota = lax.broadcasted_iota(jnp.int32, sel3.shape, 1)
    m1 = jnp.max(sel3, axis=1, keepdims=True)
    first = jnp.min(jnp.where(sel3 == m1, w_iota, per_group), axis=1, keepdims=True)
    m2 = jnp.max(jnp.where(w_iota == first, NEG_INF, sel3), axis=1, keepdims=True)
    grp = (m1 + m2).reshape(ROUTE_GROUPS, t)

    g_iota = lax.broadcasted_iota(jnp.int32, (ROUTE_GROUPS, t), 0)
    gmask = jnp.zeros((ROUTE_GROUPS, t), jnp.bool_)
    for _ in range(ROUTE_TOPK_GROUPS):
        gm = jnp.max(grp, axis=0, keepdims=True)
        gfirst = jnp.min(jnp.where(grp == gm, g_iota, ROUTE_GROUPS), axis=0, keepdims=True)
        hit = g_iota == gfirst
        gmask = jnp.logical_or(gmask, hit)
        grp = jnp.where(hit, NEG_INF, grp)
    allowed = jnp.broadcast_to(gmask.reshape(ROUTE_GROUPS, 1, t),
                               (ROUTE_GROUPS, per_group, t)).reshape(N_EXPERTS, t)
    masked = jnp.where(allowed, sel, NEG_INF)

    picked = jnp.zeros((N_EXPERTS, t), jnp.bool_)
    idx_rows = []
    w_rows = []
    for _ in range(TOP_K):
        mm = jnp.max(masked, axis=0, keepdims=True)
        efirst = jnp.min(jnp.where(masked == mm, e_iota, N_EXPERTS), axis=0, keepdims=True)
        hit = e_iota == efirst
        idx_rows.append(efirst)
        w_rows.append(jnp.sum(jnp.where(hit, scores, 0.0), axis=0, keepdims=True))
        picked = jnp.logical_or(picked, hit)
        masked = jnp.where(hit, NEG_INF, masked)
    idx = jnp.concatenate(idx_rows, axis=0)
    w = jnp.concatenate(w_rows, axis=0)
    gate_ref[...] = w / jnp.sum(w, axis=0, keepdims=True) * ROUTED_SCALE
    idx_ref[...] = idx

    onehot = jnp.where(picked, 1.0, 0.0)
    before = jnp.dot(onehot.astype(BF16), upper_ref[...], preferred_element_type=F32)
    rank_full = before + carry_ref[:, 0:1]
    rank_rows = [jnp.sum(jnp.where(e_iota == idx_rows[k], rank_full, 0.0), axis=0, keepdims=True)
                 for k in range(TOP_K)]
    rank_ref[...] = jnp.concatenate(rank_rows, axis=0).astype(jnp.int32)
    carry_ref[...] = carry_ref[...] + jnp.sum(onehot, axis=1, keepdims=True)
    cnt_ref[...] = carry_ref[...]


def _route_call(logits_t, router_bias, upper, tile):
    n = logits_t.shape[1]
    tok = lambda r: pl.BlockSpec((r, tile), lambda i: (0, i))
    return pl.pallas_call(
        _route_kernel,
        grid=(n // tile,),
        in_specs=[tok(LANES), pl.BlockSpec((N_EXPERTS, 1), lambda i: (0, 0)),
                  pl.BlockSpec(upper.shape, lambda i: (0, 0))],
        out_specs=[tok(TOP_K), tok(TOP_K), tok(TOP_K), pl.BlockSpec((N_EXPERTS, LANES), lambda i: (0, 0))],
        out_shape=[jax.ShapeDtypeStruct((TOP_K, n), jnp.int32), jax.ShapeDtypeStruct((TOP_K, n), F32),
                   jax.ShapeDtypeStruct((TOP_K, n), jnp.int32),
                   jax.ShapeDtypeStruct((N_EXPERTS, LANES), F32)],
        scratch_shapes=[pltpu.VMEM((N_EXPERTS, LANES), F32)],
        compiler_params=pltpu.CompilerParams(dimension_semantics=("arbitrary",),
                                             vmem_limit_bytes=VMEM_LIMIT),
    )(logits_t, router_bias, upper)


def _dest_kernel(idx_ref, rank_ref, start_ref, dest_ref):
    t = idx_ref.shape[1]
    e_iota = lax.broadcasted_iota(jnp.int32, (N_EXPERTS, t), 0)
    rows = [jnp.sum(jnp.where(e_iota == idx_ref[k:k + 1, :], start_ref[...], 0), axis=0, keepdims=True)
            for k in range(TOP_K)]
    dest_ref[...] = jnp.concatenate(rows, axis=0) + rank_ref[...]


def _dest_call(idx, rank, pad_start, tile):
    n = idx.shape[1]
    tok = pl.BlockSpec((TOP_K, tile), lambda i: (0, i))
    return pl.pallas_call(
        _dest_kernel,
        grid=(n // tile,),
        in_specs=[tok, tok, pl.BlockSpec((N_EXPERTS, 1), lambda i: (0, 0))],
        out_specs=tok,
        out_shape=jax.ShapeDtypeStruct((TOP_K, n), jnp.int32),
        compiler_params=pltpu.CompilerParams(dimension_semantics=("arbitrary",)),
    )(idx, rank, pad_start)


def _scatter_rows_sc(rows, dest_flat, total_rows, chunk):
    n, w = rows.shape
    copies = dest_flat.shape[0] // n
    info = plsc.get_sparse_core_info()
    nc = info.num_cores
    per_worker = n // (nc * info.num_subcores)
    assert per_worker * nc * info.num_subcores == n and per_worker % chunk == 0
    mesh = plsc.VectorSubcoreMesh(core_axis_name="c", subcore_axis_name="s")

    @functools.partial(
        pl.kernel, mesh=mesh,
        out_type=jax.ShapeDtypeStruct((total_rows, w), rows.dtype),
        scratch_types=[pltpu.VMEM((chunk,), jnp.int32), pltpu.VMEM((chunk, w), rows.dtype),
                       pltpu.SemaphoreType.DMA],
    )
    def scatter(rows_hbm, idx_hbm, out_hbm, idx_v, rows_v, sem):
        worker = lax.axis_index("s") * nc + lax.axis_index("c")

        @pl.loop(0, per_worker // chunk)
        def _(j):
            base = worker * per_worker + j * chunk
            pltpu.sync_copy(rows_hbm.at[pl.ds(base, chunk)], rows_v)
            for k in range(copies):
                pltpu.sync_copy(idx_hbm.at[pl.ds(k * n + base, chunk)], idx_v)
                pltpu.async_copy(rows_v, out_hbm.at[idx_v], sem).wait()

    return scatter(rows, dest_flat)


X_RING = 3


def _expert_kernel(be_ref, nused_ref, fresh_ref, valid_ref, xs_hbm, wg_ref, wu_ref, wd_ref, y_ref, wgb_ref, wub_ref,
                   wdb_ref, xbuf_ref, xsem):
    i = pl.program_id(0)
    rows, half = xbuf_ref.shape[1], xbuf_ref.shape[2]
    nused = nused_ref[0]

    def x_copy(block):
        first = pl.multiple_of(block * rows, rows)
        slot = block % X_RING
        return pltpu.make_async_copy(xs_hbm.at[pl.ds(first, rows)], xbuf_ref.at[slot], xsem.at[slot])

    @pl.when(i == 0)
    def _():
        for b in range(X_RING - 1):
            @pl.when(b < nused)
            def _():
                x_copy(b).start()

    @pl.when(i + (X_RING - 1) < nused)
    def _():
        x_copy(i + (X_RING - 1)).start()

    @pl.when(fresh_ref[i] > 0)
    def _():
        wgb_ref[...] = wg_ref[0].astype(BF16)
        wub_ref[...] = wu_ref[0].astype(BF16)
        wdb_ref[...] = wd_ref[0].astype(BF16)

    @pl.when(i < nused)
    def _():
        x_copy(i).wait()
        xw = xbuf_ref[i % X_RING]
        row = lax.broadcasted_iota(jnp.int32, xw.shape, 0)
        x_lo, x_hi = _unpack_bf16_pair(jnp.where(row < valid_ref[i], xw, jnp.uint32(0)))
        x_lo = x_lo.astype(BF16)
        x_hi = x_hi.astype(BF16)
        gate = (jnp.dot(x_lo, wgb_ref[:half, :], preferred_element_type=F32)
                + jnp.dot(x_hi, wgb_ref[half:, :], preferred_element_type=F32))
        up = (jnp.dot(x_lo, wub_ref[:half, :], preferred_element_type=F32)
              + jnp.dot(x_hi, wub_ref[half:, :], preferred_element_type=F32))
        u = (_silu(gate) * up).astype(BF16)
        y_lo = jnp.dot(u, wdb_ref[:, :half], preferred_element_type=F32)
        y_hi = jnp.dot(u, wdb_ref[:, half:], preferred_element_type=F32)
        y_ref[...] = _pack_bf16_pair(y_lo, y_hi)

    @pl.when(i >= nused_ref[0])
    def _():
        y_ref[...] = jnp.zeros_like(y_ref)


def _expert_call(block_expert, nused, fresh, valid, xs, wg, wu, wd, rows):
    p, w = xs.shape
    d, f = wg.shape[1], wg.shape[2]
    w_map = lambda i, be, nu, fr, va: (be[i], 0, 0)
    grid_spec = pltpu.PrefetchScalarGridSpec(
        num_scalar_prefetch=4,
        grid=(p // rows,),
        in_specs=[pl.BlockSpec(memory_space=pl.ANY),
                  pl.BlockSpec((1, d, f), w_map), pl.BlockSpec((1, d, f), w_map), pl.BlockSpec((1, f, d), w_map)],
        out_specs=pl.BlockSpec((rows, w), lambda i, be, nu, fr, va: (i, 0)),
        scratch_shapes=[pltpu.VMEM((d, f), BF16), pltpu.VMEM((d, f), BF16), pltpu.VMEM((f, d), BF16),
                        pltpu.VMEM((X_RING, rows, w), xs.dtype), pltpu.SemaphoreType.DMA((X_RING,))],
    )
    return pl.pallas_call(
        _expert_kernel,
        grid_spec=grid_spec,
        out_shape=jax.ShapeDtypeStruct((p, w), jnp.uint32),
        compiler_params=pltpu.CompilerParams(dimension_semantics=("arbitrary",),
                                             vmem_limit_bytes=VMEM_LIMIT),
    )(block_expert, nused, fresh, valid, xs, wg, wu, wd)


def _gather_rows_sc(table, idx, chunk):
    m = idx.shape[0]
    w = table.shape[1]
    info = plsc.get_sparse_core_info()
    nc = info.num_cores
    per_worker = m // (nc * info.num_subcores)
    assert per_worker * nc * info.num_subcores == m and per_worker % chunk == 0
    mesh = plsc.VectorSubcoreMesh(core_axis_name="c", subcore_axis_name="s")

    @functools.partial(
        pl.kernel, mesh=mesh,
        out_type=jax.ShapeDtypeStruct((m, w), table.dtype),
        scratch_types=[pltpu.VMEM((chunk,), jnp.int32), pltpu.VMEM((chunk, w), table.dtype),
                       pltpu.SemaphoreType.DMA],
    )
    def gather(table_hbm, idx_hbm, out_hbm, idx_v, rows_v, sem):
        worker = lax.axis_index("s") * nc + lax.axis_index("c")

        @pl.loop(0, per_worker // chunk)
        def _(j):
            base = worker * per_worker + j * chunk
            pltpu.sync_copy(idx_hbm.at[pl.ds(base, chunk)], idx_v)
            pltpu.async_copy(table_hbm.at[idx_v], rows_v, sem).wait()
            pltpu.sync_copy(rows_v, out_hbm.at[pl.ds(base, chunk)])

    return gather(table, idx)


def _combine_kernel(yk_ref, gate_ref, base_ref, g2_ref, fg_ref, o_ref):
    t = base_ref.shape[0]
    half = yk_ref.shape[2]
    gates = gate_ref[...]
    r_lo = jnp.zeros((t, half), F32)
    r_hi = jnp.zeros((t, half), F32)
    for k in range(TOP_K):
        y_lo, y_hi = _unpack_bf16_pair(yk_ref[k])
        r_lo = r_lo + gates[:, k:k + 1] * y_lo
        r_hi = r_hi + gates[:, k:k + 1] * y_hi
    g2 = g2_ref[0]
    x_lo = base_ref[:, :half] + g2[:, :half] * r_lo
    x_hi = base_ref[:, half:] + g2[:, half:] * r_hi
    ms = (jnp.sum(x_lo * x_lo, axis=-1, keepdims=True)
          + jnp.sum(x_hi * x_hi, axis=-1, keepdims=True)) * (1.0 / (2 * half))
    inv = lax.rsqrt(ms + EPS)
    o_ref[:, :half] = x_lo * inv * fg_ref[:, :half]
    o_ref[:, half:] = x_hi * inv * fg_ref[:, half:]


def _combine_call(yk, gates_t, base, g2, fg, tiles_per_batch, tile):
    n, d = base.shape
    row = lambda w: pl.BlockSpec((tile, w), lambda i: (i, 0))
    return pl.pallas_call(
        _combine_kernel,
        grid=(n // tile,),
        in_specs=[pl.BlockSpec((TOP_K, tile, yk.shape[2]), lambda i: (0, i, 0)),
                  row(TOP_K), row(d),
                  pl.BlockSpec((1, 1, d), lambda i: (i // tiles_per_batch, 0, 0)),
                  pl.BlockSpec((1, d), lambda i: (0, 0))],
        out_specs=row(d),
        out_shape=jax.ShapeDtypeStruct((n, d), F32),
        compiler_params=pltpu.CompilerParams(dimension_semantics=("arbitrary",),
                                             vmem_limit_bytes=VMEM_LIMIT),
    )(yk, gates_t, base, g2, fg)


def _pad_cols(a, width):
    return jnp.pad(a, ((0, 0), (0, width - a.shape[1])))


def _layer(x, mod, norm1_g, norm2_g, w_in, conv_w, conv_b, dt_bias, a_log, d_skip, ssm_norm_g,
           att_norm_g, sinks, rel_bias, w_out, router_w, router_bias, exp_w_gate, exp_w_up, exp_w_down,
           sh_w_gate, sh_w_up, sh_w_down, final_g):
    bsz, l, d = x.shape
    n = bsz * l
    tm = min(ROW_TILE, l)

    sh1, sc1, g1, sh2, sc2, g2 = [m[:, None, :] for m in jnp.split(mod, 6, axis=-1)]

    i1 = D_SSM
    i2 = i1 + CONV_CH
    i3 = i2 + SSM_HEADS
    i4 = i3 + D_ATT
    i5 = i4 + D_KV
    wz, wx, wdt, wq, wk, wv = jnp.split(w_in, [i1, i2, i3, i4, i5], axis=-1)
    wdt = _pad_cols(wdt, LANES)
    x2 = x.reshape(n, d)
    z, xbc, dt, q, k, v = _in_proj_call(x2, sc1, sh1, norm1_g[None, :], wz.astype(BF16), wx.astype(BF16),
                                        wdt.astype(BF16), wq.astype(BF16), wk.astype(BF16), wv.astype(BF16),
                                        l // tm, tm)

    tril = jnp.asarray(np.tril(np.ones((CHUNK, CHUNK), np.float32)))
    y_ssm = _ssd_call(xbc.reshape(bsz, l, CONV_CH), z.reshape(bsz, l, D_SSM), dt.reshape(bsz, l, LANES),
                      conv_w, conv_b[None, :], _pad_cols(dt_bias[None, :], LANES), _pad_cols(a_log[None, :], LANES),
                      jnp.repeat(d_skip, SSM_HEAD_DIM)[None, :], ssm_norm_g[None, :], tril)

    bucket, win = _rel_bucket_table()
    bias = _bias_call(rel_bias, jnp.asarray(bucket), jnp.asarray(win))
    y_att = _attn_call(sinks, q.reshape(bsz, l, D_ATT), k.reshape(bsz, l, D_KV), v.reshape(bsz, l, D_KV), bias,
                       att_norm_g[None, :])

    rw = _pad_cols(router_w, LANES)
    rwh = rw.astype(BF16)
    rwl = (rw - rwh.astype(F32)).astype(BF16)
    base, h2, logits_t = _out_proj_call(
        x2, y_ssm.reshape(n, D_SSM), y_att.reshape(n, D_ATT), g1, sc2, sh2, g2, norm2_g[None, :],
        w_out[:D_SSM].astype(BF16), w_out[D_SSM:].astype(BF16), rwh, rwl,
        sh_w_gate.astype(BF16), sh_w_up.astype(BF16), sh_w_down.astype(BF16), l // tm, tm)

    rt = min(ROUTE_TILE, n)
    upper = jnp.asarray(np.triu(np.ones((rt, rt), np.float32), 1)).astype(BF16)
    idx, gates, rank, counts = _route_call(logits_t, router_bias[:, None], upper, rt)

    nblocks = (n * TOP_K + N_EXPERTS * (EXPERT_ROWS - 1) + EXPERT_ROWS - 1) // EXPERT_ROWS
    cnt = counts[:, 0].astype(jnp.int32)
    padded = (cnt + EXPERT_ROWS - 1) // EXPERT_ROWS * EXPERT_ROWS
    pad_end = jnp.cumsum(padded)
    pad_start = pad_end - padded
    dest = _dest_call(idx, rank, pad_start[:, None], rt)
    block_first_row = jnp.arange(nblocks, dtype=jnp.int32) * EXPERT_ROWS
    block_expert = jnp.minimum(jnp.sum((pad_end[None, :] <= block_first_row[:, None]).astype(jnp.int32), axis=1),
                               N_EXPERTS - 1)
    fresh = jnp.concatenate([jnp.ones((1,), jnp.int32),
                             (block_expert[1:] != block_expert[:-1]).astype(jnp.int32)])
    nused = (pad_end[-1:] // EXPERT_ROWS).astype(jnp.int32)
    seg_row = block_first_row - jnp.sum(jnp.where(pad_end[None, :] <= block_first_row[:, None], padded[None, :], 0),
                                        axis=1)
    block_cnt = jnp.sum(jnp.where(block_expert[:, None] == jnp.arange(N_EXPERTS, dtype=jnp.int32)[None, :],
                                  cnt[None, :], 0), axis=1)
    valid = jnp.clip(block_cnt - seg_row, 0, EXPERT_ROWS).astype(jnp.int32)

    dest_flat = dest.reshape(-1)
    xs = _scatter_rows_sc(h2, dest_flat, nblocks * EXPERT_ROWS, SC_CHUNK)
    ys = _expert_call(block_expert, nused, fresh, valid, xs, exp_w_gate, exp_w_up, exp_w_down, EXPERT_ROWS)
    yk = _gather_rows_sc(ys, dest_flat, SC_CHUNK).reshape(TOP_K, n, ys.shape[1])
    ctile = min(COMBINE_TILE, l)
    out = _combine_call(yk, gates.T, base, g2, final_g[None, :], l // ctile, ctile)
    return out.reshape(bsz, l, d)


def kernel(x, c, mod_w, mod_b, norm1_g, norm2_g, w_in, conv_w, conv_b, dt_bias, a_log, d_skip, ssm_norm_g,
           att_norm_g, sinks, rel_bias, w_out, router_w, router_bias, exp_w_gate, exp_w_up, exp_w_down,
           sh_w_gate, sh_w_up, sh_w_down, final_g):
    assert mod_w.shape[0] == 1, "single-layer block"
    bsz = x.shape[0]
    c_pad = jnp.pad(c, ((0, SUBLANES - bsz % SUBLANES if bsz % SUBLANES else 0), (0, 0)))
    mod = _mod_call(c_pad, mod_w[0], mod_b[0][None, :])[:bsz]
    groups = SEQ_GROUPS if bsz % SEQ_GROUPS == 0 else 1
    per = bsz // groups
    outs = [_layer(x[i * per:(i + 1) * per], mod[i * per:(i + 1) * per], norm1_g[0], norm2_g[0], w_in[0], conv_w[0],
                   conv_b[0], dt_bias[0], a_log[0], d_skip[0], ssm_norm_g[0], att_norm_g[0], sinks[0], rel_bias,
                   w_out[0], router_w[0], router_bias[0], exp_w_gate[0], exp_w_up[0], exp_w_down[0], sh_w_gate[0],
                   sh_w_up[0], sh_w_down[0], final_g)
            for i in range(groups)]
    return outs[0] if groups == 1 else jnp.concatenate(outs, axis=0)
```

```python
import functools
import math

import numpy as np
import jax
import jax.numpy as jnp
from jax import lax
from jax.experimental import pallas as pl
from jax.experimental.pallas import tpu as pltpu
from jax.experimental.pallas import tpu_sc as plsc

F32 = jnp.float32
BF16 = jnp.bfloat16

D_MODEL = 1024
SSM_HEAD_DIM = 64
D_SSM = D_MODEL
SSM_HEADS = D_SSM // SSM_HEAD_DIM
SSM_GROUPS = 4
D_STATE = 128
CONV_K = 4
CONV_CH = D_SSM + 2 * SSM_GROUPS * D_STATE
CHUNK = 128
ATT_HEAD_DIM = 64
D_ATT = D_MODEL
ATT_HEADS = D_ATT // ATT_HEAD_DIM
KV_HEADS = ATT_HEADS // 4
Q_PER_KV = ATT_HEADS // KV_HEADS
D_KV = KV_HEADS * ATT_HEAD_DIM
WINDOW = 128
ATT_BLOCK = 128
REL_BUCKETS = 32
REL_MAX_DIST = 128
N_EXPERTS = 64
TOP_K = 8
EXPERT_DIM = D_MODEL // 4
SHARED_DIM = D_MODEL // 4
ROUTE_GROUPS = 8
ROUTE_TOPK_GROUPS = 4
ROUTED_SCALE = 2.5
EPS = 1e-6

LANES = 128
SUBLANES = 8
HALF = LANES // 2

ROW_TILE = 512
ROUTE_TILE = 512
COMBINE_TILE = 256
SC_CHUNK = 128
EXPERT_ROWS = 512
COMBINE_GROUPS = 2
VMEM_LIMIT = 48 * 1024 * 1024

NEG_INF = float("-inf")


def _silu(v):
    return v * (1.0 / (1.0 + jnp.exp(-v)))


def _softplus(v):
    return jnp.maximum(v, 0.0) + jnp.log(1.0 + jnp.exp(-jnp.abs(v)))


def _bdot(a, b):
    return jnp.dot(a.astype(BF16), b.astype(BF16), preferred_element_type=F32)


def _split_hi_lo(v):
    hi = v.astype(BF16)
    lo = (v - hi.astype(F32)).astype(BF16)
    return hi, lo


def _pack_bf16_pair(a, b):
    w = pltpu.pack_elementwise([a, b], packed_dtype=BF16)
    return w if w.dtype == jnp.uint32 else lax.bitcast_convert_type(w, jnp.uint32)


def _unpack_bf16_pair(w):
    a = pltpu.unpack_elementwise(w, index=0, packed_dtype=BF16, unpacked_dtype=F32)
    b = pltpu.unpack_elementwise(w, index=1, packed_dtype=BF16, unpacked_dtype=F32)
    return a, b


def _lane_half_mask(shape):
    return lax.broadcasted_iota(jnp.int32, shape, len(shape) - 1) < HALF


def _mod_kernel(c_ref, w_ref, b_ref, o_ref):
    a = _silu(c_ref[...])
    o_ref[...] = jnp.dot(a, w_ref[...], precision=lax.Precision.HIGHEST,
                         preferred_element_type=F32) + b_ref[...]


def _mod_call(c_pad, mod_w, mod_b):
    rows, d = c_pad.shape
    cols = mod_w.shape[1]
    return pl.pallas_call(
        _mod_kernel,
        grid=(cols // d,),
        in_specs=[pl.BlockSpec((rows, d), lambda j: (0, 0)),
                  pl.BlockSpec((d, d), lambda j: (0, j)),
                  pl.BlockSpec((1, d), lambda j: (0, j))],
        out_specs=pl.BlockSpec((rows, d), lambda j: (0, j)),
        out_shape=jax.ShapeDtypeStruct((rows, cols), F32),
        compiler_params=pltpu.CompilerParams(dimension_semantics=("arbitrary",),
                                             vmem_limit_bytes=VMEM_LIMIT),
    )(c_pad, mod_w, mod_b)


def _in_proj_kernel(x_ref, sc_ref, sh_ref, g_ref, wz_ref, wx_ref, wdt_ref, wq_ref, wk_ref, wv_ref,
                    z_ref, xbc_ref, dt_ref, q_ref, k_ref, v_ref):
    xf = x_ref[...]
    ms = jnp.mean(xf * xf, axis=-1, keepdims=True)
    h = xf * lax.rsqrt(ms + EPS) * g_ref[...]
    h = h * (1.0 + sc_ref[0]) + sh_ref[0]
    hb = h.astype(BF16)
    z_ref[...] = jnp.dot(hb, wz_ref[...], preferred_element_type=F32).astype(BF16)
    xbc_ref[...] = jnp.dot(hb, wx_ref[...], preferred_element_type=F32).astype(BF16)
    dt_ref[...] = jnp.dot(hb, wdt_ref[...], preferred_element_type=F32)
    q_ref[...] = jnp.dot(hb, wq_ref[...], preferred_element_type=F32).astype(BF16)
    k_ref[...] = jnp.dot(hb, wk_ref[...], preferred_element_type=F32).astype(BF16)
    v_ref[...] = jnp.dot(hb, wv_ref[...], preferred_element_type=F32).astype(BF16)


def _in_proj_call(x2, sc1, sh1, g1n, wz, wx, wdt, wq, wk, wv, tiles_per_batch, tm):
    n, d = x2.shape
    row = lambda w: pl.BlockSpec((tm, w), lambda i: (i, 0))
    full = lambda a: pl.BlockSpec(a.shape, lambda i: (0, 0))
    per_batch = pl.BlockSpec((1, 1, d), lambda i: (i // tiles_per_batch, 0, 0))
    outs = [(wz.shape[1], BF16), (wx.shape[1], BF16), (wdt.shape[1], F32),
            (wq.shape[1], BF16), (wk.shape[1], BF16), (wv.shape[1], BF16)]
    return pl.pallas_call(
        _in_proj_kernel,
        grid=(n // tm,),
        in_specs=[row(d), per_batch, per_batch, full(g1n), full(wz), full(wx), full(wdt), full(wq),
                  full(wk), full(wv)],
        out_specs=[row(w) for w, _ in outs],
        out_shape=[jax.ShapeDtypeStruct((n, w), dt) for w, dt in outs],
        compiler_params=pltpu.CompilerParams(dimension_semantics=("arbitrary",),
                                             vmem_limit_bytes=VMEM_LIMIT),
    )(x2, sc1, sh1, g1n, wz, wx, wdt, wq, wk, wv)


def _ssd_kernel(xbc_ref, z_ref, dt_ref, cw_ref, cb_ref, dtb_ref, alog_ref, dskip_ref, ng_ref, tril_ref,
                y_ref, state_ref, tail_ref, ext_ref, ybuf_ref):
    c = pl.program_id(1)

    @pl.when(c == 0)
    def _():
        state_ref[...] = jnp.zeros_like(state_ref)
        tail_ref[...] = jnp.zeros_like(tail_ref)

    u = xbc_ref[0].astype(F32)
    ext_ref[0:SUBLANES, :] = tail_ref[...]
    ext_ref[SUBLANES:SUBLANES + CHUNK, :] = u
    tail_ref[...] = u[CHUNK - SUBLANES:, :]
    acc = cb_ref[...] + jnp.zeros_like(u)
    for kk in range(CONV_K):
        off = SUBLANES - (CONV_K - 1) + kk
        acc = acc + cw_ref[kk:kk + 1, :] * ext_ref[off:off + CHUNK, :]
    act = _silu(acc)
    xs = act[:, :D_SSM]
    gn = SSM_GROUPS * D_STATE

    dt = _softplus(dt_ref[0] + dtb_ref[...])
    a = dt * (-jnp.exp(alog_ref[...]))
    cs = jnp.dot(tril_ref[...], a, precision=lax.Precision.HIGHEST, preferred_element_type=F32)
    cs_t = cs.T
    dt_t = dt.T
    exp_cs = jnp.exp(cs)
    cs_last = cs[CHUNK - 1:CHUNK, :]
    chunk_decay = jnp.exp(cs_last)

    li = lax.broadcasted_iota(jnp.int32, (CHUNK, CHUNK), 0)
    si = lax.broadcasted_iota(jnp.int32, (CHUNK, CHUNK), 1)
    causal = li >= si
    low = _lane_half_mask((CHUNK, LANES))
    low_row = _lane_half_mask((1, LANES))

    heads_per_group = SSM_HEADS // SSM_GROUPS
    for g in range(SSM_GROUPS):
        b_g = act[:, D_SSM + g * D_STATE:D_SSM + (g + 1) * D_STATE]
        c_g = act[:, D_SSM + gn + g * D_STATE:D_SSM + gn + (g + 1) * D_STATE]
        b_gb = b_g.astype(BF16)
        c_gb = c_g.astype(BF16)
        cb = lax.dot_general(c_gb, b_gb, (((1,), (1,)), ((), ())), preferred_element_type=F32)
        b_t = b_g.T
        for jp in range(heads_per_group // 2):
            j = g * (heads_per_group // 2) + jp
            lanes = slice(j * LANES, (j + 1) * LANES)
            xp = xs[:, lanes]
            ydiag = jnp.zeros((CHUNK, LANES), F32)
            snew = jnp.zeros((D_STATE, LANES), F32)
            for half in range(2):
                h = 2 * j + half
                row_dt = dt_t[h:h + 1, :]
                diff = cs[:, h:h + 1] - cs_t[h:h + 1, :]
                lmat = jnp.exp(jnp.where(causal, diff, NEG_INF))
                m = (cb * lmat * row_dt).astype(BF16)
                keep = low if half == 0 else jnp.logical_not(low)
                xh = jnp.where(keep, xp, 0.0).astype(BF16)
                ydiag = ydiag + jnp.dot(m, xh, preferred_element_type=F32)
                w_t = jnp.exp(cs_t[h:h + 1, CHUNK - 1:CHUNK] - cs_t[h:h + 1, :]) * row_dt
                snew = snew + jnp.dot((b_t * w_t).astype(BF16), xh, preferred_element_type=F32)
            s_in = state_ref[:, lanes]
            yoff = jnp.dot(c_gb, s_in.astype(BF16), preferred_element_type=F32)
            h0 = 2 * j
            escale = jnp.where(low, exp_cs[:, h0:h0 + 1], exp_cs[:, h0 + 1:h0 + 2])
            cdec = jnp.where(low_row, chunk_decay[:, h0:h0 + 1], chunk_decay[:, h0 + 1:h0 + 2])
            ybuf_ref[:, lanes] = ydiag + yoff * escale + xp * dskip_ref[:, lanes]
            state_ref[:, lanes] = s_in * cdec + snew

    yz = ybuf_ref[...] * _silu(z_ref[0].astype(F32))
    gw = D_SSM // SSM_GROUPS
    for g in range(SSM_GROUPS):
        part = yz[:, g * gw:(g + 1) * gw]
        ms = jnp.mean(part * part, axis=-1, keepdims=True)
        y_ref[0, :, g * gw:(g + 1) * gw] = (part * lax.rsqrt(ms + EPS)
                                            * ng_ref[:, g * gw:(g + 1) * gw]).astype(BF16)


def _ssd_call(xbc, z, dt, conv_w, conv_b, dtb, alog, dskip, ng, tril):
    bsz, l, _ = xbc.shape
    nc = l // CHUNK
    chunk = lambda w: pl.BlockSpec((1, CHUNK, w), lambda b, c: (b, c, 0))
    full = lambda a: pl.BlockSpec(a.shape, lambda b, c: (0, 0))
    return pl.pallas_call(
        _ssd_kernel,
        grid=(bsz, nc),
        in_specs=[chunk(CONV_CH), chunk(D_SSM), chunk(LANES), full(conv_w), full(conv_b), full(dtb),
                  full(alog), full(dskip), full(ng), full(tril)],
        out_specs=chunk(D_SSM),
        out_shape=jax.ShapeDtypeStruct((bsz, l, D_SSM), BF16),
        scratch_shapes=[pltpu.VMEM((D_STATE, D_SSM), F32),
                        pltpu.VMEM((SUBLANES, CONV_CH), F32),
                        pltpu.VMEM((SUBLANES + CHUNK, CONV_CH), F32),
                        pltpu.VMEM((CHUNK, D_SSM), F32)],
        compiler_params=pltpu.CompilerParams(dimension_semantics=("arbitrary", "arbitrary"),
                                             vmem_limit_bytes=VMEM_LIMIT),
    )(xbc, z, dt, conv_w, conv_b, dtb, alog, dskip, ng, tril)


def _rel_bucket_table():
    qi = np.arange(ATT_BLOCK)[:, None]
    sj = np.arange(2 * ATT_BLOCK)[None, :]
    dist = qi + ATT_BLOCK - sj
    in_win = (dist >= 0) & (dist < WINDOW)
    dcl = np.maximum(dist, 0)
    max_exact = REL_BUCKETS // 2
    d = np.maximum(dcl, 1).astype(np.float32)
    large = max_exact + (np.log(d / np.float32(max_exact)) / np.float32(math.log(REL_MAX_DIST / max_exact))
                         * np.float32(REL_BUCKETS - max_exact)).astype(np.int32)
    large = np.minimum(large, REL_BUCKETS - 1)
    bucket = np.where(dcl < max_exact, dcl, large).astype(np.int32)
    return bucket, in_win.astype(np.int32)


def _bias_kernel(rb_ref, bucket_ref, win_ref, o_ref):
    h = pl.program_id(0)
    bucket = bucket_ref[...]
    acc = jnp.zeros(bucket.shape, F32)
    for b in range(REL_BUCKETS):
        acc = jnp.where(bucket == b, rb_ref[b, h], acc)
    o_ref[0] = jnp.where(win_ref[...] > 0, acc, NEG_INF)


def _bias_call(rel_bias, bucket, win):
    return pl.pallas_call(
        _bias_kernel,
        grid=(ATT_HEADS,),
        in_specs=[pl.BlockSpec(memory_space=pltpu.SMEM),
                  pl.BlockSpec(bucket.shape, lambda h: (0, 0)),
                  pl.BlockSpec(win.shape, lambda h: (0, 0))],
        out_specs=pl.BlockSpec((1,) + bucket.shape, lambda h: (h, 0, 0)),
        out_shape=jax.ShapeDtypeStruct((ATT_HEADS,) + bucket.shape, F32),
        compiler_params=pltpu.CompilerParams(dimension_semantics=("arbitrary",)),
    )(rel_bias, bucket, win)


def _attn_kernel(sink_ref, q_ref, kp_ref, kc_ref, vp_ref, vc_ref, bias_ref, ng_ref, o_ref, obuf_ref):
    i = pl.program_id(1)
    kband = jnp.concatenate([kp_ref[0], kc_ref[0]], axis=0).astype(F32)
    vband = jnp.concatenate([vp_ref[0], vc_ref[0]], axis=0).astype(F32)
    sj = lax.broadcasted_iota(jnp.int32, (ATT_BLOCK, 2 * ATT_BLOCK), 1)
    valid = jnp.logical_or(sj >= ATT_BLOCK, i > 0)
    low = _lane_half_mask((ATT_BLOCK, LANES))
    scale = ATT_HEAD_DIM ** -0.5

    kv_cols = []
    for cpair in range(KV_HEADS // 2):
        kk = kband[:, cpair * LANES:(cpair + 1) * LANES]
        vv = vband[:, cpair * LANES:(cpair + 1) * LANES]
        kv_cols.append(((kk.astype(BF16), pltpu.roll(kk, HALF, 1).astype(BF16)),
                        (vv.astype(BF16), pltpu.roll(vv, HALF, 1).astype(BF16))))

    for j in range(ATT_HEADS // 2):
        qp = q_ref[0, :, j * LANES:(j + 1) * LANES]
        out_pair = jnp.zeros((ATT_BLOCK, LANES), F32)
        for half in range(2):
            h = 2 * j + half
            g = h // Q_PER_KV
            swapped = int((g % 2) != half)
            ksel = kv_cols[g // 2][0][swapped]
            vsel = kv_cols[g // 2][1][swapped]
            keep = low if half == 0 else jnp.logical_not(low)
            qh = jnp.where(keep, qp, jnp.zeros_like(qp))
            s = lax.dot_general(qh, ksel, (((1,), (1,)), ((), ())), preferred_element_type=F32)
            s = s * scale + bias_ref[h]
            s = jnp.where(valid, s, NEG_INF)
            sink = sink_ref[h]
            m = jnp.maximum(jnp.max(s, axis=-1, keepdims=True), sink)
            p = jnp.exp(s - m)
            denom = jnp.sum(p, axis=-1, keepdims=True) + jnp.exp(sink - m)
            o = jnp.dot(p.astype(BF16), vsel, preferred_element_type=F32) / denom
            out_pair = out_pair + jnp.where(keep, o, 0.0)
        obuf_ref[:, j * LANES:(j + 1) * LANES] = out_pair

    att = obuf_ref[...]
    ms = jnp.mean(att * att, axis=-1, keepdims=True)
    o_ref[0] = (att * lax.rsqrt(ms + EPS) * ng_ref[...]).astype(BF16)


def _attn_call(sinks, q, k, v, bias, ng):
    bsz, l, _ = q.shape
    nb = l // ATT_BLOCK
    cur = lambda w: pl.BlockSpec((1, ATT_BLOCK, w), lambda b, i: (b, i, 0))
    prev = lambda w: pl.BlockSpec((1, ATT_BLOCK, w), lambda b, i: (b, jnp.maximum(i - 1, 0), 0))
    return pl.pallas_call(
        _attn_kernel,
        grid=(bsz, nb),
        in_specs=[pl.BlockSpec(memory_space=pltpu.SMEM),
                  cur(D_ATT), prev(D_KV), cur(D_KV), prev(D_KV), cur(D_KV),
                  pl.BlockSpec(bias.shape, lambda b, i: (0, 0, 0)),
                  pl.BlockSpec(ng.shape, lambda b, i: (0, 0))],
        out_specs=cur(D_ATT),
        out_shape=jax.ShapeDtypeStruct((bsz, l, D_ATT), BF16),
        scratch_shapes=[pltpu.VMEM((ATT_BLOCK, D_ATT), F32)],
        compiler_params=pltpu.CompilerParams(dimension_semantics=("arbitrary", "arbitrary"),
                                             vmem_limit_bytes=VMEM_LIMIT),
    )(sinks, q, k, k, v, v, bias, ng)


def _out_proj_kernel(x_ref, ys_ref, ya_ref, g1_ref, sc_ref, sh_ref, g2_ref, ng_ref, wos_ref, woa_ref,
                     rwh_ref, rwl_ref, sg_ref, su_ref, sd_ref, base_ref, h_ref, lg_ref):
    mix = (jnp.dot(ys_ref[...], wos_ref[...], preferred_element_type=F32)
           + jnp.dot(ya_ref[...], woa_ref[...], preferred_element_type=F32))
    x1 = x_ref[...] + g1_ref[0] * mix
    ms = jnp.mean(x1 * x1, axis=-1, keepdims=True)
    h = x1 * lax.rsqrt(ms + EPS) * ng_ref[...]
    h = h * (1.0 + sc_ref[0]) + sh_ref[0]
    half = h.shape[1] // 2
    h_ref[...] = _pack_bf16_pair(h[:, :half], h[:, half:])
    hi, lo = _split_hi_lo(h)
    logits = (jnp.dot(hi, rwh_ref[...], preferred_element_type=F32)
              + jnp.dot(lo, rwh_ref[...], preferred_element_type=F32)
              + jnp.dot(hi, rwl_ref[...], preferred_element_type=F32))
    lg_ref[...] = logits.T
    u = _silu(jnp.dot(hi, sg_ref[...], preferred_element_type=F32)) * jnp.dot(hi, su_ref[...],
                                                                              preferred_element_type=F32)
    shared = jnp.dot(u.astype(BF16), sd_ref[...], preferred_element_type=F32)
    base_ref[...] = x1 + g2_ref[0] * shared


def _out_proj_call(x2, ys, ya, g1, sc2, sh2, g2, ng, wos, woa, rwh, rwl, sg, su, sd, tiles_per_batch, tm):
    n, d = x2.shape
    row = lambda w: pl.BlockSpec((tm, w), lambda i: (i, 0))
    full = lambda a: pl.BlockSpec(a.shape, lambda i: (0, 0))
    per_batch = pl.BlockSpec((1, 1, d), lambda i: (i // tiles_per_batch, 0, 0))
    return pl.pallas_call(
        _out_proj_kernel,
        grid=(n // tm,),
        in_specs=[row(d), row(D_SSM), row(D_ATT), per_batch, per_batch, per_batch, per_batch, full(ng),
                  full(wos), full(woa), full(rwh), full(rwl), full(sg), full(su), full(sd)],
        out_specs=[row(d), row(d // 2), pl.BlockSpec((LANES, tm), lambda i: (0, i))],
        out_shape=[jax.ShapeDtypeStruct((n, d), F32), jax.ShapeDtypeStruct((n, d // 2), jnp.uint32),
                   jax.ShapeDtypeStruct((LANES, n), F32)],
        compiler_params=pltpu.CompilerParams(dimension_semantics=("arbitrary",),
                                             vmem_limit_bytes=VMEM_LIMIT),
    )(x2, ys, ya, g1, sc2, sh2, g2, ng, wos, woa, rwh, rwl, sg, su, sd)


def _route_kernel(lg_ref, rb_ref, upper_ref, idx_ref, gate_ref, rank_ref, cnt_ref, carry_ref):
    step = pl.program_id(0)

    @pl.when(step == 0)
    def _():
        carry_ref[...] = jnp.zeros_like(carry_ref)

    t = lg_ref.shape[1]
    per_group = N_EXPERTS // ROUTE_GROUPS
    scores = 1.0 / (1.0 + jnp.exp(-lg_ref[0:N_EXPERTS, :]))
    sel = scores + rb_ref[...]
    e_iota = lax.broadcasted_iota(jnp.int32, (N_EXPERTS, t), 0)

    sel3 = sel.reshape(ROUTE_GROUPS, per_group, t)
    w_iota = lax.broadcasted_iota(jnp.int32, sel3.shape, 1)
    m1 = jnp.max(sel3, axis=1, keepdims=True)
    first = jnp.min(jnp.where(sel3 == m1, w_iota, per_group), axis=1, keepdims=True)
    m2 = jnp.max(jnp.where(w_iota == first, NEG_INF, sel3), axis=1, keepdims=True)
    grp = (m1 + m2).reshape(ROUTE_GROUPS, t)

    g_iota = lax.broadcasted_iota(jnp.int32, (ROUTE_GROUPS, t), 0)
    gmask = jnp.zeros((ROUTE_GROUPS, t), jnp.bool_)
    for _ in range(ROUTE_TOPK_GROUPS):
        gm = jnp.max(grp, axis=0, keepdims=True)
        gfirst = jnp.min(jnp.where(grp == gm, g_iota, ROUTE_GROUPS), axis=0, keepdims=True)
        hit = g_iota == gfirst
        gmask = jnp.logical_or(gmask, hit)
        grp = jnp.where(hit, NEG_INF, grp)
    allowed = jnp.broadcast_to(gmask.reshape(ROUTE_GROUPS, 1, t),
                               (ROUTE_GROUPS, per_group, t)).reshape(N_EXPERTS, t)
    masked = jnp.where(allowed, sel, NEG_INF)

    picked = jnp.zeros((N_EXPERTS, t), jnp.bool_)
    idx_rows = []
    w_rows = []
    for _ in range(TOP_K):
        mm = jnp.max(masked, axis=0, keepdims=True)
        efirst = jnp.min(jnp.where(masked == mm, e_iota, N_EXPERTS), axis=0, keepdims=True)
        hit = e_iota == efirst
        idx_rows.append(efirst)
        w_rows.append(jnp.sum(jnp.where(hit, scores, 0.0), axis=0, keepdims=True))
        picked = jnp.logical_or(picked, hit)
        masked = jnp.where(hit, NEG_INF, masked)
    idx = jnp.concatenate(idx_rows, axis=0)
    w = jnp.concatenate(w_rows, axis=0)
    gate_ref[...] = w / jnp.sum(w, axis=0, keepdims=True) * ROUTED_SCALE
    idx_ref[...] = idx

    onehot = jnp.where(picked, 1.0, 0.0)
    before = jnp.dot(onehot.astype(BF16), upper_ref[...], preferred_element_type=F32)
    rank_full = before + carry_ref[:, 0:1]
    rank_rows = [jnp.sum(jnp.where(e_iota == idx_rows[k], rank_full, 0.0), axis=0, keepdims=True)
                 for k in range(TOP_K)]
    rank_ref[...] = jnp.concatenate(rank_rows, axis=0).astype(jnp.int32)
    carry_ref[...] = carry_ref[...] + jnp.sum(onehot, axis=1, keepdims=True)
    cnt_ref[...] = carry_ref[...]


def _route_call(logits_t, router_bias, upper, tile):
    n = logits_t.shape[1]
    tok = lambda r: pl.BlockSpec((r, tile), lambda i: (0, i))
    return pl.pallas_call(
        _route_kernel,
        grid=(n // tile,),
        in_specs=[tok(LANES), pl.BlockSpec((N_EXPERTS, 1), lambda i: (0, 0)),
                  pl.BlockSpec(upper.shape, lambda i: (0, 0))],
        out_specs=[tok(TOP_K), tok(TOP_K), tok(TOP_K), pl.BlockSpec((N_EXPERTS, LANES), lambda i: (0, 0))],
        out_shape=[jax.ShapeDtypeStruct((TOP_K, n), jnp.int32), jax.ShapeDtypeStruct((TOP_K, n), F32),
                   jax.ShapeDtypeStruct((TOP_K, n), jnp.int32),
                   jax.ShapeDtypeStruct((N_EXPERTS, LANES), F32)],
        scratch_shapes=[pltpu.VMEM((N_EXPERTS, LANES), F32)],
        compiler_params=pltpu.CompilerParams(dimension_semantics=("arbitrary",),
                                             vmem_limit_bytes=VMEM_LIMIT),
    )(logits_t, router_bias, upper)


def _dest_kernel(idx_ref, rank_ref, start_ref, dest_ref):
    t = idx_ref.shape[1]
    e_iota = lax.broadcasted_iota(jnp.int32, (N_EXPERTS, t), 0)
    rows = [jnp.sum(jnp.where(e_iota == idx_ref[k:k + 1, :], start_ref[...], 0), axis=0, keepdims=True)
            for k in range(TOP_K)]
    dest_ref[...] = jnp.concatenate(rows, axis=0) + rank_ref[...]


def _dest_call(idx, rank, pad_start, tile):
    n = idx.shape[1]
    tok = pl.BlockSpec((TOP_K, tile), lambda i: (0, i))
    return pl.pallas_call(
        _dest_kernel,
        grid=(n // tile,),
        in_specs=[tok, tok, pl.BlockSpec((N_EXPERTS, 1), lambda i: (0, 0))],
        out_specs=tok,
        out_shape=jax.ShapeDtypeStruct((TOP_K, n), jnp.int32),
        compiler_params=pltpu.CompilerParams(dimension_semantics=("arbitrary",)),
    )(idx, rank, pad_start)


def _scatter_rows_sc(rows, dest_flat, total_rows, chunk):
    n, w = rows.shape
    copies = dest_flat.shape[0] // n
    info = plsc.get_sparse_core_info()
    nc = info.num_cores
    per_worker = n // (nc * info.num_subcores)
    assert per_worker * nc * info.num_subcores == n and per_worker % chunk == 0
    mesh = plsc.VectorSubcoreMesh(core_axis_name="c", subcore_axis_name="s")

    @functools.partial(
        pl.kernel, mesh=mesh,
        out_type=jax.ShapeDtypeStruct((total_rows, w), rows.dtype),
        scratch_types=[pltpu.VMEM((chunk,), jnp.int32), pltpu.VMEM((chunk, w), rows.dtype),
                       pltpu.SemaphoreType.DMA],
    )
    def scatter(rows_hbm, idx_hbm, out_hbm, idx_v, rows_v, sem):
        worker = lax.axis_index("s") * nc + lax.axis_index("c")

        @pl.loop(0, per_worker // chunk)
        def _(j):
            base = worker * per_worker + j * chunk
            pltpu.sync_copy(rows_hbm.at[pl.ds(base, chunk)], rows_v)
            for k in range(copies):
                pltpu.sync_copy(idx_hbm.at[pl.ds(k * n + base, chunk)], idx_v)
                pltpu.async_copy(rows_v, out_hbm.at[idx_v], sem).wait()

    return scatter(rows, dest_flat)


X_RING = 3


def _expert_kernel(be_ref, nused_ref, fresh_ref, valid_ref, xs_hbm, wg_ref, wu_ref, wd_ref, y_ref, wgb_ref, wub_ref,
                   wdb_ref, xbuf_ref, xsem):
    i = pl.program_id(0)
    rows, half = xbuf_ref.shape[1], xbuf_ref.shape[2]
    nused = nused_ref[0]

    def x_copy(block):
        first = pl.multiple_of(block * rows, rows)
        slot = block % X_RING
        return pltpu.make_async_copy(xs_hbm.at[pl.ds(first, rows)], xbuf_ref.at[slot], xsem.at[slot])

    @pl.when(i == 0)
    def _():
        for b in range(X_RING - 1):
            @pl.when(b < nused)
            def _():
                x_copy(b).start()

    @pl.when(i + (X_RING - 1) < nused)
    def _():
        x_copy(i + (X_RING - 1)).start()

    @pl.when(fresh_ref[i] > 0)
    def _():
        wgb_ref[...] = wg_ref[0].astype(BF16)
        wub_ref[...] = wu_ref[0].astype(BF16)
        wdb_ref[...] = wd_ref[0].astype(BF16)

    @pl.when(i < nused)
    def _():
        x_copy(i).wait()
        xw = xbuf_ref[i % X_RING]
        row = lax.broadcasted_iota(jnp.int32, xw.shape, 0)
        x_lo, x_hi = _unpack_bf16_pair(jnp.where(row < valid_ref[i], xw, jnp.uint32(0)))
        x_lo = x_lo.astype(BF16)
        x_hi = x_hi.astype(BF16)
        gate = (jnp.dot(x_lo, wgb_ref[:half, :], preferred_element_type=F32)
                + jnp.dot(x_hi, wgb_ref[half:, :], preferred_element_type=F32))
        up = (jnp.dot(x_lo, wub_ref[:half, :], preferred_element_type=F32)
              + jnp.dot(x_hi, wub_ref[half:, :], preferred_element_type=F32))
        u = (_silu(gate) * up).astype(BF16)
        y_lo = jnp.dot(u, wdb_ref[:, :half], preferred_element_type=F32)
        y_hi = jnp.dot(u, wdb_ref[:, half:], preferred_element_type=F32)
        y_ref[...] = _pack_bf16_pair(y_lo, y_hi)

    @pl.when(i >= nused_ref[0])
    def _():
        y_ref[...] = jnp.zeros_like(y_ref)


def _expert_call(block_expert, nused, fresh, valid, xs, wg, wu, wd, rows):
    p, w = xs.shape
    d, f = wg.shape[1], wg.shape[2]
    w_map = lambda i, be, nu, fr, va: (be[i], 0, 0)
    grid_spec = pltpu.PrefetchScalarGridSpec(
        num_scalar_prefetch=4,
        grid=(p // rows,),
        in_specs=[pl.BlockSpec(memory_space=pl.ANY),
                  pl.BlockSpec((1, d, f), w_map), pl.BlockSpec((1, d, f), w_map), pl.BlockSpec((1, f, d), w_map)],
        out_specs=pl.BlockSpec((rows, w), lambda i, be, nu, fr, va: (i, 0)),
        scratch_shapes=[pltpu.VMEM((d, f), BF16), pltpu.VMEM((d, f), BF16), pltpu.VMEM((f, d), BF16),
                        pltpu.VMEM((X_RING, rows, w), xs.dtype), pltpu.SemaphoreType.DMA((X_RING,))],
    )
    return pl.pallas_call(
        _expert_kernel,
        grid_spec=grid_spec,
        out_shape=jax.ShapeDtypeStruct((p, w), jnp.uint32),
        compiler_params=pltpu.CompilerParams(dimension_semantics=("arbitrary",),
                                             vmem_limit_bytes=VMEM_LIMIT),
    )(block_expert, nused, fresh, valid, xs, wg, wu, wd)


def _gather_rows_sc(table, idx, chunk):
    m = idx.shape[0]
    w = table.shape[1]
    info = plsc.get_sparse_core_info()
    nc = info.num_cores
    per_worker = m // (nc * info.num_subcores)
    assert per_worker * nc * info.num_subcores == m and per_worker % chunk == 0
    mesh = plsc.VectorSubcoreMesh(core_axis_name="c", subcore_axis_name="s")

    @functools.partial(
        pl.kernel, mesh=mesh,
        out_type=jax.ShapeDtypeStruct((m, w), table.dtype),
        scratch_types=[pltpu.VMEM((chunk,), jnp.int32), pltpu.VMEM((chunk, w), table.dtype),
                       pltpu.SemaphoreType.DMA],
    )
    def gather(table_hbm, idx_hbm, out_hbm, idx_v, rows_v, sem):
        worker = lax.axis_index("s") * nc + lax.axis_index("c")

        @pl.loop(0, per_worker // chunk)
        def _(j):
            base = worker * per_worker + j * chunk
            pltpu.sync_copy(idx_hbm.at[pl.ds(base, chunk)], idx_v)
            pltpu.async_copy(table_hbm.at[idx_v], rows_v, sem).wait()
            pltpu.sync_copy(rows_v, out_hbm.at[pl.ds(base, chunk)])

    return gather(table, idx)


def _combine_kernel(yk_ref, gate_ref, base_ref, g2_ref, fg_ref, o_ref):
    t = base_ref.shape[0]
    half = yk_ref.shape[2]
    gates = gate_ref[...]
    r_lo = jnp.zeros((t, half), F32)
    r_hi = jnp.zeros((t, half), F32)
    for k in range(TOP_K):
        y_lo, y_hi = _unpack_bf16_pair(yk_ref[k])
        r_lo = r_lo + gates[:, k:k + 1] * y_lo
        r_hi = r_hi + gates[:, k:k + 1] * y_hi
    g2 = g2_ref[0]
    x_lo = base_ref[:, :half] + g2[:, :half] * r_lo
    x_hi = base_ref[:, half:] + g2[:, half:] * r_hi
    ms = (jnp.sum(x_lo * x_lo, axis=-1, keepdims=True)
          + jnp.sum(x_hi * x_hi, axis=-1, keepdims=True)) * (1.0 / (2 * half))
    inv = lax.rsqrt(ms + EPS)
    o_ref[:, :half] = x_lo * inv * fg_ref[:, :half]
    o_ref[:, half:] = x_hi * inv * fg_ref[:, half:]


def _combine_call(yk, gates_t, base, g2, fg, tiles_per_batch, tile, tile0):
    n, d = base.shape
    row = lambda w: pl.BlockSpec((tile, w), lambda i: (i + tile0, 0))
    return pl.pallas_call(
        _combine_kernel,
        grid=(yk.shape[1] // tile,),
        in_specs=[pl.BlockSpec((TOP_K, tile, yk.shape[2]), lambda i: (0, i, 0)),
                  row(TOP_K), row(d),
                  pl.BlockSpec((1, 1, d), lambda i: ((i + tile0) // tiles_per_batch, 0, 0)),
                  pl.BlockSpec((1, d), lambda i: (0, 0))],
        out_specs=row(d),
        out_shape=jax.ShapeDtypeStruct((n, d), F32),
        input_output_aliases={2: 0},
        compiler_params=pltpu.CompilerParams(dimension_semantics=("arbitrary",),
                                             vmem_limit_bytes=VMEM_LIMIT),
    )(yk, gates_t, base, g2, fg)


def _pad_cols(a, width):
    return jnp.pad(a, ((0, 0), (0, width - a.shape[1])))


def _layer(x, mod, norm1_g, norm2_g, w_in, conv_w, conv_b, dt_bias, a_log, d_skip, ssm_norm_g,
           att_norm_g, sinks, rel_bias, w_out, router_w, router_bias, exp_w_gate, exp_w_up, exp_w_down,
           sh_w_gate, sh_w_up, sh_w_down, final_g):
    bsz, l, d = x.shape
    n = bsz * l
    tm = min(ROW_TILE, l)

    sh1, sc1, g1, sh2, sc2, g2 = [m[:, None, :] for m in jnp.split(mod, 6, axis=-1)]

    i1 = D_SSM
    i2 = i1 + CONV_CH
    i3 = i2 + SSM_HEADS
    i4 = i3 + D_ATT
    i5 = i4 + D_KV
    wz, wx, wdt, wq, wk, wv = jnp.split(w_in, [i1, i2, i3, i4, i5], axis=-1)
    wdt = _pad_cols(wdt, LANES)
    x2 = x.reshape(n, d)
    z, xbc, dt, q, k, v = _in_proj_call(x2, sc1, sh1, norm1_g[None, :], wz.astype(BF16), wx.astype(BF16),
                                        wdt.astype(BF16), wq.astype(BF16), wk.astype(BF16), wv.astype(BF16),
                                        l // tm, tm)

    tril = jnp.asarray(np.tril(np.ones((CHUNK, CHUNK), np.float32)))
    y_ssm = _ssd_call(xbc.reshape(bsz, l, CONV_CH), z.reshape(bsz, l, D_SSM), dt.reshape(bsz, l, LANES),
                      conv_w, conv_b[None, :], _pad_cols(dt_bias[None, :], LANES), _pad_cols(a_log[None, :], LANES),
                      jnp.repeat(d_skip, SSM_HEAD_DIM)[None, :], ssm_norm_g[None, :], tril)

    bucket, win = _rel_bucket_table()
    bias = _bias_call(rel_bias, jnp.asarray(bucket), jnp.asarray(win))
    y_att = _attn_call(sinks, q.reshape(bsz, l, D_ATT), k.reshape(bsz, l, D_KV), v.reshape(bsz, l, D_KV), bias,
                       att_norm_g[None, :])

    rw = _pad_cols(router_w, LANES)
    rwh = rw.astype(BF16)
    rwl = (rw - rwh.astype(F32)).astype(BF16)
    base, h2, logits_t = _out_proj_call(
        x2, y_ssm.reshape(n, D_SSM), y_att.reshape(n, D_ATT), g1, sc2, sh2, g2, norm2_g[None, :],
        w_out[:D_SSM].astype(BF16), w_out[D_SSM:].astype(BF16), rwh, rwl,
        sh_w_gate.astype(BF16), sh_w_up.astype(BF16), sh_w_down.astype(BF16), l // tm, tm)

    rt = min(ROUTE_TILE, n)
    upper = jnp.asarray(np.triu(np.ones((rt, rt), np.float32), 1)).astype(BF16)
    idx, gates, rank, counts = _route_call(logits_t, router_bias[:, None], upper, rt)

    nblocks = (n * TOP_K + N_EXPERTS * (EXPERT_ROWS - 1) + EXPERT_ROWS - 1) // EXPERT_ROWS
    cnt = counts[:, 0].astype(jnp.int32)
    padded = (cnt + EXPERT_ROWS - 1) // EXPERT_ROWS * EXPERT_ROWS
    pad_end = jnp.cumsum(padded)
    pad_start = pad_end - padded
    dest = _dest_call(idx, rank, pad_start[:, None], rt)
    block_first_row = jnp.arange(nblocks, dtype=jnp.int32) * EXPERT_ROWS
    block_expert = jnp.minimum(jnp.sum((pad_end[None, :] <= block_first_row[:, None]).astype(jnp.int32), axis=1),
                               N_EXPERTS - 1)
    fresh = jnp.concatenate([jnp.ones((1,), jnp.int32),
                             (block_expert[1:] != block_expert[:-1]).astype(jnp.int32)])
    nused = (pad_end[-1:] // EXPERT_ROWS).astype(jnp.int32)
    seg_row = block_first_row - jnp.sum(jnp.where(pad_end[None, :] <= block_first_row[:, None], padded[None, :], 0),
                                        axis=1)
    block_cnt = jnp.sum(jnp.where(block_expert[:, None] == jnp.arange(N_EXPERTS, dtype=jnp.int32)[None, :],
                                  cnt[None, :], 0), axis=1)
    valid = jnp.clip(block_cnt - seg_row, 0, EXPERT_ROWS).astype(jnp.int32)

    dest_flat = dest.reshape(-1)
    xs = _scatter_rows_sc(h2, dest_flat, nblocks * EXPERT_ROWS, SC_CHUNK)
    ys = _expert_call(block_expert, nused, fresh, valid, xs, exp_w_gate, exp_w_up, exp_w_down, EXPERT_ROWS)
    ctile = min(COMBINE_TILE, l)
    groups = COMBINE_GROUPS if bsz % COMBINE_GROUPS == 0 else 1
    ng = n // groups
    gates_t = gates.T
    out = base
    for g in range(groups):
        idx_g = dest[:, g * ng:(g + 1) * ng].reshape(-1)
        yk = _gather_rows_sc(ys, idx_g, SC_CHUNK).reshape(TOP_K, ng, ys.shape[1])
        out = _combine_call(yk, gates_t, out, g2, final_g[None, :], l // ctile, ctile, g * ng // ctile)
    return out.reshape(bsz, l, d)


def kernel(x, c, mod_w, mod_b, norm1_g, norm2_g, w_in, conv_w, conv_b, dt_bias, a_log, d_skip, ssm_norm_g,
           att_norm_g, sinks, rel_bias, w_out, router_w, router_bias, exp_w_gate, exp_w_up, exp_w_down,
           sh_w_gate, sh_w_up, sh_w_down, final_g):
    assert mod_w.shape[0] == 1, "single-layer block"
    bsz = x.shape[0]
    c_pad = jnp.pad(c, ((0, SUBLANES - bsz % SUBLANES if bsz % SUBLANES else 0), (0, 0)))
    mod = _mod_call(c_pad, mod_w[0], mod_b[0][None, :])[:bsz]
    return _layer(x, mod, norm1_g[0], norm2_g[0], w_in[0], conv_w[0], conv_b[0], dt_bias[0], a_log[0], d_skip[0],
                  ssm_norm_g[0], att_norm_g[0], sinks[0], rel_bias, w_out[0], router_w[0], router_bias[0],
                  exp_w_gate[0], exp_w_up[0], exp_w_down[0], sh_w_gate[0], sh_w_up[0], sh_w_down[0], final_g)
```

```python
import functools
import math

import numpy as np
import jax
import jax.numpy as jnp
from jax import lax
from jax.experimental import pallas as pl
from jax.experimental.pallas import tpu as pltpu
from jax.experimental.pallas import tpu_sc as plsc

F32 = jnp.float32
BF16 = jnp.bfloat16

D_MODEL = 1024
SSM_HEAD_DIM = 64
D_SSM = D_MODEL
SSM_HEADS = D_SSM // SSM_HEAD_DIM
SSM_GROUPS = 4
D_STATE = 128
CONV_K = 4
CONV_CH = D_SSM + 2 * SSM_GROUPS * D_STATE
CHUNK = 128
ATT_HEAD_DIM = 64
D_ATT = D_MODEL
ATT_HEADS = D_ATT // ATT_HEAD_DIM
KV_HEADS = ATT_HEADS // 4
Q_PER_KV = ATT_HEADS // KV_HEADS
D_KV = KV_HEADS * ATT_HEAD_DIM
WINDOW = 128
ATT_BLOCK = 128
REL_BUCKETS = 32
REL_MAX_DIST = 128
N_EXPERTS = 64
TOP_K = 8
EXPERT_DIM = D_MODEL // 4
SHARED_DIM = D_MODEL // 4
ROUTE_GROUPS = 8
ROUTE_TOPK_GROUPS = 4
ROUTED_SCALE = 2.5
EPS = 1e-6

LANES = 128
SUBLANES = 8
HALF = LANES // 2

ROW_TILE = 512
ROUTE_TILE = 512
COMBINE_TILE = 256
SC_CHUNK = 128
EXPERT_ROWS = 512
COMBINE_GROUPS = 2
VMEM_LIMIT = 48 * 1024 * 1024

NEG_INF = float("-inf")


def _silu(v):
    return v * (1.0 / (1.0 + jnp.exp(-v)))


def _softplus(v):
    return jnp.maximum(v, 0.0) + jnp.log(1.0 + jnp.exp(-jnp.abs(v)))


def _bdot(a, b):
    return jnp.dot(a.astype(BF16), b.astype(BF16), preferred_element_type=F32)


def _split_hi_lo(v):
    hi = v.astype(BF16)
    lo = (v - hi.astype(F32)).astype(BF16)
    return hi, lo


def _pack_bf16_pair(a, b):
    w = pltpu.pack_elementwise([a, b], packed_dtype=BF16)
    return w if w.dtype == jnp.uint32 else lax.bitcast_convert_type(w, jnp.uint32)


def _unpack_bf16_pair(w):
    a = pltpu.unpack_elementwise(w, index=0, packed_dtype=BF16, unpacked_dtype=F32)
    b = pltpu.unpack_elementwise(w, index=1, packed_dtype=BF16, unpacked_dtype=F32)
    return a, b


def _lane_half_mask(shape):
    return lax.broadcasted_iota(jnp.int32, shape, len(shape) - 1) < HALF


def _mod_kernel(c_ref, w_ref, b_ref, o_ref):
    a = _silu(c_ref[...])
    o_ref[...] = jnp.dot(a, w_ref[...], precision=lax.Precision.HIGHEST,
                         preferred_element_type=F32) + b_ref[...]


def _mod_call(c_pad, mod_w, mod_b):
    rows, d = c_pad.shape
    cols = mod_w.shape[1]
    return pl.pallas_call(
        _mod_kernel,
        grid=(cols // d,),
        in_specs=[pl.BlockSpec((rows, d), lambda j: (0, 0)),
                  pl.BlockSpec((d, d), lambda j: (0, j)),
                  pl.BlockSpec((1, d), lambda j: (0, j))],
        out_specs=pl.BlockSpec((rows, d), lambda j: (0, j)),
        out_shape=jax.ShapeDtypeStruct((rows, cols), F32),
        compiler_params=pltpu.CompilerParams(dimension_semantics=("arbitrary",),
                                             vmem_limit_bytes=VMEM_LIMIT),
    )(c_pad, mod_w, mod_b)


def _in_proj_kernel(x_ref, sc_ref, sh_ref, g_ref, wz_ref, wx_ref, wdt_ref, wq_ref, wk_ref, wv_ref,
                    z_ref, xbc_ref, dt_ref, q_ref, k_ref, v_ref):
    xf = x_ref[...]
    ms = jnp.mean(xf * xf, axis=-1, keepdims=True)
    h = xf * lax.rsqrt(ms + EPS) * g_ref[...]
    h = h * (1.0 + sc_ref[0]) + sh_ref[0]
    hb = h.astype(BF16)
    z_ref[...] = jnp.dot(hb, wz_ref[...], preferred_element_type=F32).astype(BF16)
    xbc_ref[...] = jnp.dot(hb, wx_ref[...], preferred_element_type=F32).astype(BF16)
    dt_ref[...] = jnp.dot(hb, wdt_ref[...], preferred_element_type=F32)
    q_ref[...] = jnp.dot(hb, wq_ref[...], preferred_element_type=F32).astype(BF16)
    k_ref[...] = jnp.dot(hb, wk_ref[...], preferred_element_type=F32).astype(BF16)
    v_ref[...] = jnp.dot(hb, wv_ref[...], preferred_element_type=F32).astype(BF16)


def _in_proj_call(x2, sc1, sh1, g1n, wz, wx, wdt, wq, wk, wv, tiles_per_batch, tm):
    n, d = x2.shape
    row = lambda w: pl.BlockSpec((tm, w), lambda i: (i, 0))
    full = lambda a: pl.BlockSpec(a.shape, lambda i: (0, 0))
    per_batch = pl.BlockSpec((1, 1, d), lambda i: (i // tiles_per_batch, 0, 0))
    outs = [(wz.shape[1], BF16), (wx.shape[1], BF16), (wdt.shape[1], F32),
            (wq.shape[1], BF16), (wk.shape[1], BF16), (wv.shape[1], BF16)]
    return pl.pallas_call(
        _in_proj_kernel,
        grid=(n // tm,),
        in_specs=[row(d), per_batch, per_batch, full(g1n), full(wz), full(wx), full(wdt), full(wq),
                  full(wk), full(wv)],
        out_specs=[row(w) for w, _ in outs],
        out_shape=[jax.ShapeDtypeStruct((n, w), dt) for w, dt in outs],
        compiler_params=pltpu.CompilerParams(dimension_semantics=("arbitrary",),
                                             vmem_limit_bytes=VMEM_LIMIT),
    )(x2, sc1, sh1, g1n, wz, wx, wdt, wq, wk, wv)


SSD_SEQS = 2
CONV_HALO = 16


def _conv_shift_matrix():
    s = np.zeros((CONV_K * CHUNK, CONV_HALO + CHUNK), np.float32)
    for k in range(CONV_K):
        t = np.arange(CHUNK)
        s[k * CHUNK + t, CONV_HALO + t - (CONV_K - 1) + k] = 1.0
    return s


def _silu_tanh(v):
    hv = 0.5 * v
    return hv + hv * jnp.tanh(hv)


def _ssd_kernel(xbc_ref, z_ref, dt_ref, cw_ref, cb_ref, dtb_ref, alog_ref, dskip_ref, ng_ref, triu_ref, shift_ref,
                y_ref, state_ref, ucat_ref, ybuf_ref):
    nseq = xbc_ref.shape[0]

    @pl.when(pl.program_id(1) == 0)
    def _():
        state_ref[...] = jnp.zeros_like(state_ref)
        ucat_ref[:, 0:CONV_HALO, :] = jnp.zeros((nseq, CONV_HALO, CONV_CH), BF16)

    for q in range(nseq):
        _ssd_chunk(xbc_ref.at[q], z_ref.at[q], dt_ref.at[q], cw_ref, cb_ref, dtb_ref, alog_ref, dskip_ref, ng_ref,
                   triu_ref, shift_ref, y_ref.at[q], state_ref.at[q], ucat_ref.at[q], ybuf_ref.at[q])


def _ssd_chunk(xbc_ref, z_ref, dt_ref, cw_ref, cb_ref, dtb_ref, alog_ref, dskip_ref, ng_ref, triu_ref, shift_ref,
               y_ref, state_ref, ucat_ref, ybuf_ref):
    ucat_ref[CONV_HALO:, :] = xbc_ref[...]
    shifted = jnp.dot(shift_ref[...], ucat_ref[...], preferred_element_type=F32)
    ucat_ref[0:CONV_HALO, :] = ucat_ref[CHUNK:CHUNK + CONV_HALO, :]
    acc = cb_ref[...] + cw_ref[0:1, :] * shifted[0:CHUNK]
    for kk in range(1, CONV_K):
        acc = acc + cw_ref[kk:kk + 1, :] * shifted[kk * CHUNK:(kk + 1) * CHUNK]
    act = _silu_tanh(acc)
    xs = act[:, :D_SSM]
    gn = SSM_GROUPS * D_STATE

    dt_t = _softplus(dt_ref[...].T[0:SSM_HEADS, :] + dtb_ref[...])
    a_t = dt_t * (-jnp.exp(alog_ref[...]))
    a_hi = a_t.astype(BF16)
    a_mid = (a_t - a_hi.astype(F32)).astype(BF16)
    a_lo = (a_t - a_hi.astype(F32) - a_mid.astype(F32)).astype(BF16)
    triu = triu_ref[...]
    cs_t = (jnp.dot(a_hi, triu, preferred_element_type=F32) + jnp.dot(a_mid, triu, preferred_element_type=F32)
            + jnp.dot(a_lo, triu, preferred_element_type=F32))
    cs_end = cs_t[:, CHUNK - 1:CHUNK]
    r_t = cs_t - jnp.log(dt_t)
    w_t = jnp.exp(cs_end - cs_t) * dt_t
    chunk_decay = jnp.exp(cs_end)
    cols = jnp.concatenate([cs_t, jnp.exp(cs_t), jnp.zeros((LANES - 2 * SSM_HEADS, CHUNK), F32)], axis=0).T

    li = lax.broadcasted_iota(jnp.int32, (CHUNK, CHUNK), 0)
    si = lax.broadcasted_iota(jnp.int32, (CHUNK, CHUNK), 1)
    causal = li >= si
    low = _lane_half_mask((CHUNK, LANES))
    low_row = _lane_half_mask((1, LANES))

    heads_per_group = SSM_HEADS // SSM_GROUPS
    for g in range(SSM_GROUPS):
        b_g = act[:, D_SSM + g * D_STATE:D_SSM + (g + 1) * D_STATE]
        c_g = act[:, D_SSM + gn + g * D_STATE:D_SSM + gn + (g + 1) * D_STATE]
        b_gb = b_g.astype(BF16)
        c_gb = c_g.astype(BF16)
        cb = lax.dot_general(c_gb, b_gb, (((1,), (1,)), ((), ())), preferred_element_type=F32)
        b_t = b_g.T
        for jp in range(heads_per_group // 2):
            j = g * (heads_per_group // 2) + jp
            lanes = slice(j * LANES, (j + 1) * LANES)
            xp = xs[:, lanes]
            xpb = xp.astype(BF16)
            ydiag = jnp.zeros((CHUNK, LANES), F32)
            snew = jnp.zeros((D_STATE, LANES), F32)
            for half in range(2):
                h = 2 * j + half
                diff = cols[:, h:h + 1] - r_t[h:h + 1, :]
                m = (cb * jnp.exp(jnp.where(causal, diff, NEG_INF))).astype(BF16)
                keep = low if half == 0 else jnp.logical_not(low)
                xh = jnp.where(keep, xpb, jnp.zeros_like(xpb))
                ydiag = ydiag + jnp.dot(m, xh, preferred_element_type=F32)
                snew = snew + jnp.dot((b_t * w_t[h:h + 1, :]).astype(BF16), xh, preferred_element_type=F32)
            s_in = state_ref[:, lanes]
            yoff = jnp.dot(c_gb, s_in.astype(BF16), preferred_element_type=F32)
            h0 = 2 * j
            e0 = SSM_HEADS + h0
            escale = jnp.where(low, cols[:, e0:e0 + 1], cols[:, e0 + 1:e0 + 2])
            cdec = jnp.where(low_row, chunk_decay[h0:h0 + 1, :], chunk_decay[h0 + 1:h0 + 2, :])
            ybuf_ref[:, lanes] = ydiag + yoff * escale + xp * dskip_ref[:, lanes]
            state_ref[:, lanes] = s_in * cdec + snew

    yz = ybuf_ref[...] * _silu_tanh(z_ref[...].astype(F32))
    gw = D_SSM // SSM_GROUPS
    for g in range(SSM_GROUPS):
        part = yz[:, g * gw:(g + 1) * gw]
        ms = jnp.mean(part * part, axis=-1, keepdims=True)
        y_ref[:, g * gw:(g + 1) * gw] = (part * lax.rsqrt(ms + EPS)
                                            * ng_ref[:, g * gw:(g + 1) * gw]).astype(BF16)


def _ssd_call(xbc, z, dt, conv_w, conv_b, dtb, alog, dskip, ng, triu, shift):
    bsz, l, _ = xbc.shape
    nc = l // CHUNK
    nseq = SSD_SEQS if bsz % SSD_SEQS == 0 else 1
    chunk = lambda w: pl.BlockSpec((nseq, CHUNK, w), lambda b, c: (b, c, 0))
    full = lambda a: pl.BlockSpec(a.shape, lambda b, c: (0, 0))
    return pl.pallas_call(
        _ssd_kernel,
        grid=(bsz // nseq, nc),
        in_specs=[chunk(CONV_CH), chunk(D_SSM), chunk(LANES), full(conv_w), full(conv_b), full(dtb),
                  full(alog), full(dskip), full(ng), full(triu), full(shift)],
        out_specs=chunk(D_SSM),
        out_shape=jax.ShapeDtypeStruct((bsz, l, D_SSM), BF16),
        scratch_shapes=[pltpu.VMEM((nseq, D_STATE, D_SSM), F32),
                        pltpu.VMEM((nseq, CONV_HALO + CHUNK, CONV_CH), BF16),
                        pltpu.VMEM((nseq, CHUNK, D_SSM), F32)],
        compiler_params=pltpu.CompilerParams(dimension_semantics=("arbitrary", "arbitrary"),
                                             vmem_limit_bytes=VMEM_LIMIT),
    )(xbc, z, dt, conv_w, conv_b, dtb, alog, dskip, ng, triu, shift)


def _rel_bucket_table():
    qi = np.arange(ATT_BLOCK)[:, None]
    sj = np.arange(2 * ATT_BLOCK)[None, :]
    dist = qi + ATT_BLOCK - sj
    in_win = (dist >= 0) & (dist < WINDOW)
    dcl = np.maximum(dist, 0)
    max_exact = REL_BUCKETS // 2
    d = np.maximum(dcl, 1).astype(np.float32)
    large = max_exact + (np.log(d / np.float32(max_exact)) / np.float32(math.log(REL_MAX_DIST / max_exact))
                         * np.float32(REL_BUCKETS - max_exact)).astype(np.int32)
    large = np.minimum(large, REL_BUCKETS - 1)
    bucket = np.where(dcl < max_exact, dcl, large).astype(np.int32)
    return bucket, in_win.astype(np.int32)


def _bias_kernel(rb_ref, bucket_ref, win_ref, o_ref):
    h = pl.program_id(0)
    bucket = bucket_ref[...]
    acc = jnp.zeros(bucket.shape, F32)
    for b in range(REL_BUCKETS):
        acc = jnp.where(bucket == b, rb_ref[b, h], acc)
    o_ref[0] = jnp.where(win_ref[...] > 0, acc, NEG_INF)


def _bias_call(rel_bias, bucket, win):
    return pl.pallas_call(
        _bias_kernel,
        grid=(ATT_HEADS,),
        in_specs=[pl.BlockSpec(memory_space=pltpu.SMEM),
                  pl.BlockSpec(bucket.shape, lambda h: (0, 0)),
                  pl.BlockSpec(win.shape, lambda h: (0, 0))],
        out_specs=pl.BlockSpec((1,) + bucket.shape, lambda h: (h, 0, 0)),
        out_shape=jax.ShapeDtypeStruct((ATT_HEADS,) + bucket.shape, F32),
        compiler_params=pltpu.CompilerParams(dimension_semantics=("arbitrary",)),
    )(rel_bias, bucket, win)


def _attn_kernel(sink_ref, q_ref, kp_ref, kc_ref, vp_ref, vc_ref, bias_ref, ng_ref, o_ref, obuf_ref):
    i = pl.program_id(1)
    kband = jnp.concatenate([kp_ref[0], kc_ref[0]], axis=0).astype(F32)
    vband = jnp.concatenate([vp_ref[0], vc_ref[0]], axis=0).astype(F32)
    sj = lax.broadcasted_iota(jnp.int32, (ATT_BLOCK, 2 * ATT_BLOCK), 1)
    valid = jnp.logical_or(sj >= ATT_BLOCK, i > 0)
    low = _lane_half_mask((ATT_BLOCK, LANES))
    scale = ATT_HEAD_DIM ** -0.5

    kv_cols = []
    for cpair in range(KV_HEADS // 2):
        kk = kband[:, cpair * LANES:(cpair + 1) * LANES]
        vv = vband[:, cpair * LANES:(cpair + 1) * LANES]
        kv_cols.append(((kk.astype(BF16), pltpu.roll(kk, HALF, 1).astype(BF16)),
                        (vv.astype(BF16), pltpu.roll(vv, HALF, 1).astype(BF16))))

    for j in range(ATT_HEADS // 2):
        qp = q_ref[0, :, j * LANES:(j + 1) * LANES]
        out_pair = jnp.zeros((ATT_BLOCK, LANES), F32)
        for half in range(2):
            h = 2 * j + half
            g = h // Q_PER_KV
            swapped = int((g % 2) != half)
            ksel = kv_cols[g // 2][0][swapped]
            vsel = kv_cols[g // 2][1][swapped]
            keep = low if half == 0 else jnp.logical_not(low)
            qh = jnp.where(keep, qp, jnp.zeros_like(qp))
            s = lax.dot_general(qh, ksel, (((1,), (1,)), ((), ())), preferred_element_type=F32)
            s = s * scale + bias_ref[h]
            s = jnp.where(valid, s, NEG_INF)
            sink = sink_ref[h]
            m = jnp.maximum(jnp.max(s, axis=-1, keepdims=True), sink)
            p = jnp.exp(s - m)
            denom = jnp.sum(p, axis=-1, keepdims=True) + jnp.exp(sink - m)
            o = jnp.dot(p.astype(BF16), vsel, preferred_element_type=F32) / denom
            out_pair = out_pair + jnp.where(keep, o, 0.0)
        obuf_ref[:, j * LANES:(j + 1) * LANES] = out_pair

    att = obuf_ref[...]
    ms = jnp.mean(att * att, axis=-1, keepdims=True)
    o_ref[0] = (att * lax.rsqrt(ms + EPS) * ng_ref[...]).astype(BF16)


def _attn_call(sinks, q, k, v, bias, ng):
    bsz, l, _ = q.shape
    nb = l // ATT_BLOCK
    cur = lambda w: pl.BlockSpec((1, ATT_BLOCK, w), lambda b, i: (b, i, 0))
    prev = lambda w: pl.BlockSpec((1, ATT_BLOCK, w), lambda b, i: (b, jnp.maximum(i - 1, 0), 0))
    return pl.pallas_call(
        _attn_kernel,
        grid=(bsz, nb),
        in_specs=[pl.BlockSpec(memory_space=pltpu.SMEM),
                  cur(D_ATT), prev(D_KV), cur(D_KV), prev(D_KV), cur(D_KV),
                  pl.BlockSpec(bias.shape, lambda b, i: (0, 0, 0)),
                  pl.BlockSpec(ng.shape, lambda b, i: (0, 0))],
        out_specs=cur(D_ATT),
        out_shape=jax.ShapeDtypeStruct((bsz, l, D_ATT), BF16),
        scratch_shapes=[pltpu.VMEM((ATT_BLOCK, D_ATT), F32)],
        compiler_params=pltpu.CompilerParams(dimension_semantics=("arbitrary", "arbitrary"),
                                             vmem_limit_bytes=VMEM_LIMIT),
    )(sinks, q, k, k, v, v, bias, ng)


def _out_proj_kernel(x_ref, ys_ref, ya_ref, g1_ref, sc_ref, sh_ref, g2_ref, ng_ref, wos_ref, woa_ref,
                     rwh_ref, rwl_ref, sg_ref, su_ref, sd_ref, base_ref, h_ref, lg_ref):
    mix = (jnp.dot(ys_ref[...], wos_ref[...], preferred_element_type=F32)
           + jnp.dot(ya_ref[...], woa_ref[...], preferred_element_type=F32))
    x1 = x_ref[...] + g1_ref[0] * mix
    ms = jnp.mean(x1 * x1, axis=-1, keepdims=True)
    h = x1 * lax.rsqrt(ms + EPS) * ng_ref[...]
    h = h * (1.0 + sc_ref[0]) + sh_ref[0]
    half = h.shape[1] // 2
    h_ref[...] = _pack_bf16_pair(h[:, :half], h[:, half:])
    hi, lo = _split_hi_lo(h)
    logits = (jnp.dot(hi, rwh_ref[...], preferred_element_type=F32)
              + jnp.dot(lo, rwh_ref[...], preferred_element_type=F32)
              + jnp.dot(hi, rwl_ref[...], preferred_element_type=F32))
    lg_ref[...] = logits.T
    u = _silu(jnp.dot(hi, sg_ref[...], preferred_element_type=F32)) * jnp.dot(hi, su_ref[...],
                                                                              preferred_element_type=F32)
    shared = jnp.dot(u.astype(BF16), sd_ref[...], preferred_element_type=F32)
    base_ref[...] = x1 + g2_ref[0] * shared


def _out_proj_call(x2, ys, ya, g1, sc2, sh2, g2, ng, wos, woa, rwh, rwl, sg, su, sd, tiles_per_batch, tm):
    n, d = x2.shape
    row = lambda w: pl.BlockSpec((tm, w), lambda i: (i, 0))
    full = lambda a: pl.BlockSpec(a.shape, lambda i: (0, 0))
    per_batch = pl.BlockSpec((1, 1, d), lambda i: (i // tiles_per_batch, 0, 0))
    return pl.pallas_call(
        _out_proj_kernel,
        grid=(n // tm,),
        in_specs=[row(d), row(D_SSM), row(D_ATT), per_batch, per_batch, per_batch, per_batch, full(ng),
                  full(wos), full(woa), full(rwh), full(rwl), full(sg), full(su), full(sd)],
        out_specs=[row(d), row(d // 2), pl.BlockSpec((LANES, tm), lambda i: (0, i))],
        out_shape=[jax.ShapeDtypeStruct((n, d), F32), jax.ShapeDtypeStruct((n, d // 2), jnp.uint32),
                   jax.ShapeDtypeStruct((LANES, n), F32)],
        compiler_params=pltpu.CompilerParams(dimension_semantics=("arbitrary",),
                                             vmem_limit_bytes=VMEM_LIMIT),
    )(x2, ys, ya, g1, sc2, sh2, g2, ng, wos, woa, rwh, rwl, sg, su, sd)


def _route_kernel(lg_ref, rb_ref, upper_ref, idx_ref, gate_ref, rank_ref, cnt_ref, carry_ref):
    step = pl.program_id(0)

    @pl.when(step == 0)
    def _():
        carry_ref[...] = jnp.zeros_like(carry_ref)

    t = lg_ref.shape[1]
    per_group = N_EXPERTS // ROUTE_GROUPS
    scores = 1.0 / (1.0 + jnp.exp(-lg_ref[0:N_EXPERTS, :]))
    sel = scores + rb_ref[...]
    e_iota = lax.broadcasted_iota(jnp.int32, (N_EXPERTS, t), 0)

    sel3 = sel.reshape(ROUTE_GROUPS, per_group, t)
    w_iota = lax.broadcasted_iota(jnp.int32, sel3.shape, 1)
    m1 = jnp.max(sel3, axis=1, keepdims=True)
    first = jnp.min(jnp.where(sel3 == m1, w_iota, per_group), axis=1, keepdims=True)
    m2 = jnp.max(jnp.where(w_iota == first, NEG_INF, sel3), axis=1, keepdims=True)
    grp = (m1 + m2).reshape(ROUTE_GROUPS, t)

    g_iota = lax.broadcasted_iota(jnp.int32, (ROUTE_GROUPS, t), 0)
    gmask = jnp.zeros((ROUTE_GROUPS, t), jnp.bool_)
    for _ in range(ROUTE_TOPK_GROUPS):
        gm = jnp.max(grp, axis=0, keepdims=True)
        gfirst = jnp.min(jnp.where(grp == gm, g_iota, ROUTE_GROUPS), axis=0, keepdims=True)
        hit = g_iota == gfirst
        gmask = jnp.logical_or(gmask, hit)
        grp = jnp.where(hit, NEG_INF, grp)
    allowed = jnp.broadcast_to(gmask.reshape(ROUTE_GROUPS, 1, t),
                               (ROUTE_GROUPS, per_group, t)).reshape(N_EXPERTS, t)
    masked = jnp.where(allowed, sel, NEG_INF)

    picked = jnp.zeros((N_EXPERTS, t), jnp.bool_)
    idx_rows = []
    w_rows = []
    for _ in range(TOP_K):
        mm = jnp.max(masked, axis=0, keepdims=True)
        efirst = jnp.min(jnp.where(masked == mm, e_iota, N_EXPERTS), axis=0, keepdims=True)
        hit = e_iota == efirst
        idx_rows.append(efirst)
        w_rows.append(jnp.sum(jnp.where(hit, scores, 0.0), axis=0, keepdims=True))
        picked = jnp.logical_or(picked, hit)
        masked = jnp.where(hit, NEG_INF, masked)
    idx = jnp.concatenate(idx_rows, axis=0)
    w = jnp.concatenate(w_rows, axis=0)
    gate_ref[...] = w / jnp.sum(w, axis=0, keepdims=True) * ROUTED_SCALE
    idx_ref[...] = idx

    onehot = jnp.where(picked, 1.0, 0.0)
    before = jnp.dot(onehot.astype(BF16), upper_ref[...], preferred_element_type=F32)
    rank_full = before + carry_ref[:, 0:1]
    rank_rows = [jnp.sum(jnp.where(e_iota == idx_rows[k], rank_full, 0.0), axis=0, keepdims=True)
                 for k in range(TOP_K)]
    rank_ref[...] = jnp.concatenate(rank_rows, axis=0).astype(jnp.int32)
    carry_ref[...] = carry_ref[...] + jnp.sum(onehot, axis=1, keepdims=True)
    cnt_ref[...] = carry_ref[...]


def _route_call(logits_t, router_bias, upper, tile):
    n = logits_t.shape[1]
    tok = lambda r: pl.BlockSpec((r, tile), lambda i: (0, i))
    return pl.pallas_call(
        _route_kernel,
        grid=(n // tile,),
        in_specs=[tok(LANES), pl.BlockSpec((N_EXPERTS, 1), lambda i: (0, 0)),
                  pl.BlockSpec(upper.shape, lambda i: (0, 0))],
        out_specs=[tok(TOP_K), tok(TOP_K), tok(TOP_K), pl.BlockSpec((N_EXPERTS, LANES), lambda i: (0, 0))],
        out_shape=[jax.ShapeDtypeStruct((TOP_K, n), jnp.int32), jax.ShapeDtypeStruct((TOP_K, n), F32),
                   jax.ShapeDtypeStruct((TOP_K, n), jnp.int32),
                   jax.ShapeDtypeStruct((N_EXPERTS, LANES), F32)],
        scratch_shapes=[pltpu.VMEM((N_EXPERTS, LANES), F32)],
        compiler_params=pltpu.CompilerParams(dimension_semantics=("arbitrary",),
                                             vmem_limit_bytes=VMEM_LIMIT),
    )(logits_t, router_bias, upper)


def _dest_kernel(idx_ref, rank_ref, start_ref, dest_ref):
    t = idx_ref.shape[1]
    e_iota = lax.broadcasted_iota(jnp.int32, (N_EXPERTS, t), 0)
    rows = [jnp.sum(jnp.where(e_iota == idx_ref[k:k + 1, :], start_ref[...], 0), axis=0, keepdims=True)
            for k in range(TOP_K)]
    dest_ref[...] = jnp.concatenate(rows, axis=0) + rank_ref[...]


def _dest_call(idx, rank, pad_start, tile):
    n = idx.shape[1]
    tok = pl.BlockSpec((TOP_K, tile), lambda i: (0, i))
    return pl.pallas_call(
        _dest_kernel,
        grid=(n // tile,),
        in_specs=[tok, tok, pl.BlockSpec((N_EXPERTS, 1), lambda i: (0, 0))],
        out_specs=tok,
        out_shape=jax.ShapeDtypeStruct((TOP_K, n), jnp.int32),
        compiler_params=pltpu.CompilerParams(dimension_semantics=("arbitrary",)),
    )(idx, rank, pad_start)


def _scatter_rows_sc(rows, dest_flat, total_rows, chunk):
    n, w = rows.shape
    copies = dest_flat.shape[0] // n
    info = plsc.get_sparse_core_info()
    nc = info.num_cores
    per_worker = n // (nc * info.num_subcores)
    assert per_worker * nc * info.num_subcores == n and per_worker % chunk == 0
    mesh = plsc.VectorSubcoreMesh(core_axis_name="c", subcore_axis_name="s")

    @functools.partial(
        pl.kernel, mesh=mesh,
        out_type=jax.ShapeDtypeStruct((total_rows, w), rows.dtype),
        scratch_types=[pltpu.VMEM((chunk,), jnp.int32), pltpu.VMEM((chunk, w), rows.dtype),
                       pltpu.SemaphoreType.DMA],
    )
    def scatter(rows_hbm, idx_hbm, out_hbm, idx_v, rows_v, sem):
        worker = lax.axis_index("s") * nc + lax.axis_index("c")

        @pl.loop(0, per_worker // chunk)
        def _(j):
            base = worker * per_worker + j * chunk
            pltpu.sync_copy(rows_hbm.at[pl.ds(base, chunk)], rows_v)
            for k in range(copies):
                pltpu.sync_copy(idx_hbm.at[pl.ds(k * n + base, chunk)], idx_v)
                pltpu.async_copy(rows_v, out_hbm.at[idx_v], sem).wait()

    return scatter(rows, dest_flat)


X_RING = 3


def _expert_kernel(be_ref, nused_ref, fresh_ref, valid_ref, xs_hbm, wg_ref, wu_ref, wd_ref, y_ref, wgb_ref, wub_ref,
                   wdb_ref, xbuf_ref, xsem):
    i = pl.program_id(0)
    rows, half = xbuf_ref.shape[1], xbuf_ref.shape[2]
    nused = nused_ref[0]

    def x_copy(block):
        first = pl.multiple_of(block * rows, rows)
        slot = block % X_RING
        return pltpu.make_async_copy(xs_hbm.at[pl.ds(first, rows)], xbuf_ref.at[slot], xsem.at[slot])

    @pl.when(i == 0)
    def _():
        for b in range(X_RING - 1):
            @pl.when(b < nused)
            def _():
                x_copy(b).start()

    @pl.when(i + (X_RING - 1) < nused)
    def _():
        x_copy(i + (X_RING - 1)).start()

    @pl.when(fresh_ref[i] > 0)
    def _():
        wgb_ref[...] = wg_ref[0].astype(BF16)
        wub_ref[...] = wu_ref[0].astype(BF16)
        wdb_ref[...] = wd_ref[0].astype(BF16)

    @pl.when(i < nused)
    def _():
        x_copy(i).wait()
        xw = xbuf_ref[i % X_RING]
        row = lax.broadcasted_iota(jnp.int32, xw.shape, 0)
        x_lo, x_hi = _unpack_bf16_pair(jnp.where(row < valid_ref[i], xw, jnp.uint32(0)))
        x_lo = x_lo.astype(BF16)
        x_hi = x_hi.astype(BF16)
        gate = (jnp.dot(x_lo, wgb_ref[:half, :], preferred_element_type=F32)
                + jnp.dot(x_hi, wgb_ref[half:, :], preferred_element_type=F32))
        up = (jnp.dot(x_lo, wub_ref[:half, :], preferred_element_type=F32)
              + jnp.dot(x_hi, wub_ref[half:, :], preferred_element_type=F32))
        u = (_silu(gate) * up).astype(BF16)
        y_lo = jnp.dot(u, wdb_ref[:, :half], preferred_element_type=F32)
        y_hi = jnp.dot(u, wdb_ref[:, half:], preferred_element_type=F32)
        y_ref[...] = _pack_bf16_pair(y_lo, y_hi)

    @pl.when(i >= nused_ref[0])
    def _():
        y_ref[...] = jnp.zeros_like(y_ref)


def _expert_call(block_expert, nused, fresh, valid, xs, wg, wu, wd, rows):
    p, w = xs.shape
    d, f = wg.shape[1], wg.shape[2]
    w_map = lambda i, be, nu, fr, va: (be[i], 0, 0)
    grid_spec = pltpu.PrefetchScalarGridSpec(
        num_scalar_prefetch=4,
        grid=(p // rows,),
        in_specs=[pl.BlockSpec(memory_space=pl.ANY),
                  pl.BlockSpec((1, d, f), w_map), pl.BlockSpec((1, d, f), w_map), pl.BlockSpec((1, f, d), w_map)],
        out_specs=pl.BlockSpec((rows, w), lambda i, be, nu, fr, va: (i, 0)),
        scratch_shapes=[pltpu.VMEM((d, f), BF16), pltpu.VMEM((d, f), BF16), pltpu.VMEM((f, d), BF16),
                        pltpu.VMEM((X_RING, rows, w), xs.dtype), pltpu.SemaphoreType.DMA((X_RING,))],
    )
    return pl.pallas_call(
        _expert_kernel,
        grid_spec=grid_spec,
        out_shape=jax.ShapeDtypeStruct((p, w), jnp.uint32),
        compiler_params=pltpu.CompilerParams(dimension_semantics=("arbitrary",),
                                             vmem_limit_bytes=VMEM_LIMIT),
    )(block_expert, nused, fresh, valid, xs, wg, wu, wd)


def _gather_rows_sc(table, idx, chunk):
    m = idx.shape[0]
    w = table.shape[1]
    info = plsc.get_sparse_core_info()
    nc = info.num_cores
    per_worker = m // (nc * info.num_subcores)
    assert per_worker * nc * info.num_subcores == m and per_worker % chunk == 0
    mesh = plsc.VectorSubcoreMesh(core_axis_name="c", subcore_axis_name="s")

    @functools.partial(
        pl.kernel, mesh=mesh,
        out_type=jax.ShapeDtypeStruct((m, w), table.dtype),
        scratch_types=[pltpu.VMEM((chunk,), jnp.int32), pltpu.VMEM((chunk, w), table.dtype),
                       pltpu.SemaphoreType.DMA],
    )
    def gather(table_hbm, idx_hbm, out_hbm, idx_v, rows_v, sem):
        worker = lax.axis_index("s") * nc + lax.axis_index("c")

        @pl.loop(0, per_worker // chunk)
        def _(j):
            base = worker * per_worker + j * chunk
            pltpu.sync_copy(idx_hbm.at[pl.ds(base, chunk)], idx_v)
            pltpu.async_copy(table_hbm.at[idx_v], rows_v, sem).wait()
            pltpu.sync_copy(rows_v, out_hbm.at[pl.ds(base, chunk)])

    return gather(table, idx)


def _combine_kernel(yk_ref, gate_ref, base_ref, g2_ref, fg_ref, o_ref):
    t = base_ref.shape[0]
    half = yk_ref.shape[2]
    gates = gate_ref[...]
    r_lo = jnp.zeros((t, half), F32)
    r_hi = jnp.zeros((t, half), F32)
    for k in range(TOP_K):
        y_lo, y_hi = _unpack_bf16_pair(yk_ref[k])
        r_lo = r_lo + gates[:, k:k + 1] * y_lo
        r_hi = r_hi + gates[:, k:k + 1] * y_hi
    g2 = g2_ref[0]
    x_lo = base_ref[:, :half] + g2[:, :half] * r_lo
    x_hi = base_ref[:, half:] + g2[:, half:] * r_hi
    ms = (jnp.sum(x_lo * x_lo, axis=-1, keepdims=True)
          + jnp.sum(x_hi * x_hi, axis=-1, keepdims=True)) * (1.0 / (2 * half))
    inv = lax.rsqrt(ms + EPS)
    o_ref[:, :half] = x_lo * inv * fg_ref[:, :half]
    o_ref[:, half:] = x_hi * inv * fg_ref[:, half:]


def _combine_call(yk, gates_t, base, g2, fg, tiles_per_batch, tile, tile0):
    n, d = base.shape
    row = lambda w: pl.BlockSpec((tile, w), lambda i: (i + tile0, 0))
    return pl.pallas_call(
        _combine_kernel,
        grid=(yk.shape[1] // tile,),
        in_specs=[pl.BlockSpec((TOP_K, tile, yk.shape[2]), lambda i: (0, i, 0)),
                  row(TOP_K), row(d),
                  pl.BlockSpec((1, 1, d), lambda i: ((i + tile0) // tiles_per_batch, 0, 0)),
                  pl.BlockSpec((1, d), lambda i: (0, 0))],
        out_specs=row(d),
        out_shape=jax.ShapeDtypeStruct((n, d), F32),
        input_output_aliases={2: 0},
        compiler_params=pltpu.CompilerParams(dimension_semantics=("arbitrary",),
                                             vmem_limit_bytes=VMEM_LIMIT),
    )(yk, gates_t, base, g2, fg)


def _pad_cols(a, width):
    return jnp.pad(a, ((0, 0), (0, width - a.shape[1])))


def _layer(x, mod, norm1_g, norm2_g, w_in, conv_w, conv_b, dt_bias, a_log, d_skip, ssm_norm_g,
           att_norm_g, sinks, rel_bias, w_out, router_w, router_bias, exp_w_gate, exp_w_up, exp_w_down,
           sh_w_gate, sh_w_up, sh_w_down, final_g):
    bsz, l, d = x.shape
    n = bsz * l
    tm = min(ROW_TILE, l)

    sh1, sc1, g1, sh2, sc2, g2 = [m[:, None, :] for m in jnp.split(mod, 6, axis=-1)]

    i1 = D_SSM
    i2 = i1 + CONV_CH
    i3 = i2 + SSM_HEADS
    i4 = i3 + D_ATT
    i5 = i4 + D_KV
    wz, wx, wdt, wq, wk, wv = jnp.split(w_in, [i1, i2, i3, i4, i5], axis=-1)
    wdt = _pad_cols(wdt, LANES)
    x2 = x.reshape(n, d)
    z, xbc, dt, q, k, v = _in_proj_call(x2, sc1, sh1, norm1_g[None, :], wz.astype(BF16), wx.astype(BF16),
                                        wdt.astype(BF16), wq.astype(BF16), wk.astype(BF16), wv.astype(BF16),
                                        l // tm, tm)

    triu = jnp.asarray(np.triu(np.ones((CHUNK, CHUNK), np.float32))).astype(BF16)
    shift = jnp.asarray(_conv_shift_matrix()).astype(BF16)
    y_ssm = _ssd_call(xbc.reshape(bsz, l, CONV_CH), z.reshape(bsz, l, D_SSM), dt.reshape(bsz, l, LANES),
                      conv_w, conv_b[None, :], dt_bias[:, None], a_log[:, None],
                      jnp.repeat(d_skip, SSM_HEAD_DIM)[None, :], ssm_norm_g[None, :], triu, shift)

    bucket, win = _rel_bucket_table()
    bias = _bias_call(rel_bias, jnp.asarray(bucket), jnp.asarray(win))
    y_att = _attn_call(sinks, q.reshape(bsz, l, D_ATT), k.reshape(bsz, l, D_KV), v.reshape(bsz, l, D_KV), bias,
                       att_norm_g[None, :])

    rw = _pad_cols(router_w, LANES)
    rwh = rw.astype(BF16)
    rwl = (rw - rwh.astype(F32)).astype(BF16)
    base, h2, logits_t = _out_proj_call(
        x2, y_ssm.reshape(n, D_SSM), y_att.reshape(n, D_ATT), g1, sc2, sh2, g2, norm2_g[None, :],
        w_out[:D_SSM].astype(BF16), w_out[D_SSM:].astype(BF16), rwh, rwl,
        sh_w_gate.astype(BF16), sh_w_up.astype(BF16), sh_w_down.astype(BF16), l // tm, tm)

    rt = min(ROUTE_TILE, n)
    upper = jnp.asarray(np.triu(np.ones((rt, rt), np.float32), 1)).astype(BF16)
    idx, gates, rank, counts = _route_call(logits_t, router_bias[:, None], upper, rt)

    nblocks = (n * TOP_K + N_EXPERTS * (EXPERT_ROWS - 1) + EXPERT_ROWS - 1) // EXPERT_ROWS
    cnt = counts[:, 0].astype(jnp.int32)
    padded = (cnt + EXPERT_ROWS - 1) // EXPERT_ROWS * EXPERT_ROWS
    pad_end = jnp.cumsum(padded)
    pad_start = pad_end - padded
    dest = _dest_call(idx, rank, pad_start[:, None], rt)
    block_first_row = jnp.arange(nblocks, dtype=jnp.int32) * EXPERT_ROWS
    block_expert = jnp.minimum(jnp.sum((pad_end[None, :] <= block_first_row[:, None]).astype(jnp.int32), axis=1),
                               N_EXPERTS - 1)
    fresh = jnp.concatenate([jnp.ones((1,), jnp.int32),
                             (block_expert[1:] != block_expert[:-1]).astype(jnp.int32)])
    nused = (pad_end[-1:] // EXPERT_ROWS).astype(jnp.int32)
    seg_row = block_first_row - jnp.sum(jnp.where(pad_end[None, :] <= block_first_row[:, None], padded[None, :], 0),
                                        axis=1)
    block_cnt = jnp.sum(jnp.where(block_expert[:, None] == jnp.arange(N_EXPERTS, dtype=jnp.int32)[None, :],
                                  cnt[None, :], 0), axis=1)
    valid = jnp.clip(block_cnt - seg_row, 0, EXPERT_ROWS).astype(jnp.int32)

    dest_flat = dest.reshape(-1)
    xs = _scatter_rows_sc(h2, dest_flat, nblocks * EXPERT_ROWS, SC_CHUNK)
    ys = _expert_call(block_expert, nused, fresh, valid, xs, exp_w_gate, exp_w_up, exp_w_down, EXPERT_ROWS)
    ctile = min(COMBINE_TILE, l)
    groups = COMBINE_GROUPS if bsz % COMBINE_GROUPS == 0 else 1
    ng = n // groups
    gates_t = gates.T
    out = base
    for g in range(groups):
        idx_g = dest[:, g * ng:(g + 1) * ng].reshape(-1)
        yk = _gather_rows_sc(ys, idx_g, SC_CHUNK).reshape(TOP_K, ng, ys.shape[1])
        out = _combine_call(yk, gates_t, out, g2, final_g[None, :], l // ctile, ctile, g * ng // ctile)
    return out.reshape(bsz, l, d)


def kernel(x, c, mod_w, mod_b, norm1_g, norm2_g, w_in, conv_w, conv_b, dt_bias, a_log, d_skip, ssm_norm_g,
           att_norm_g, sinks, rel_bias, w_out, router_w, router_bias, exp_w_gate, exp_w_up, exp_w_down,
           sh_w_gate, sh_w_up, sh_w_down, final_g):
    assert mod_w.shape[0] == 1, "single-layer block"
    bsz = x.shape[0]
    c_pad = jnp.pad(c, ((0, SUBLANES - bsz % SUBLANES if bsz % SUBLANES else 0), (0, 0)))
    mod = _mod_call(c_pad, mod_w[0], mod_b[0][None, :])[:bsz]
    return _layer(x, mod, norm1_g[0], norm2_g[0], w_in[0], conv_w[0], conv_b[0], dt_bias[0], a_log[0], d_skip[0],
                  ssm_norm_g[0], att_norm_g[0], sinks[0], rel_bias, w_out[0], router_w[0], router_bias[0],
                  exp_w_gate[0], exp_w_up[0], exp_w_down[0], sh_w_gate[0], sh_w_up[0], sh_w_down[0], final_g)
```

```python
import functools
import math

import numpy as np
import jax
import jax.numpy as jnp
from jax import lax
from jax.experimental import pallas as pl
from jax.experimental.pallas import tpu as pltpu
from jax.experimental.pallas import tpu_sc as plsc

F32 = jnp.float32
BF16 = jnp.bfloat16

D_MODEL = 1024
SSM_HEAD_DIM = 64
D_SSM = D_MODEL
SSM_HEADS = D_SSM // SSM_HEAD_DIM
SSM_GROUPS = 4
D_STATE = 128
CONV_K = 4
CONV_CH = D_SSM + 2 * SSM_GROUPS * D_STATE
CHUNK = 128
ATT_HEAD_DIM = 64
D_ATT = D_MODEL
ATT_HEADS = D_ATT // ATT_HEAD_DIM
KV_HEADS = ATT_HEADS // 4
Q_PER_KV = ATT_HEADS // KV_HEADS
D_KV = KV_HEADS * ATT_HEAD_DIM
WINDOW = 128
ATT_BLOCK = 128
REL_BUCKETS = 32
REL_MAX_DIST = 128
N_EXPERTS = 64
TOP_K = 8
EXPERT_DIM = D_MODEL // 4
SHARED_DIM = D_MODEL // 4
ROUTE_GROUPS = 8
ROUTE_TOPK_GROUPS = 4
ROUTED_SCALE = 2.5
EPS = 1e-6

LANES = 128
SUBLANES = 8
HALF = LANES // 2

ROW_TILE = 512
ROUTE_TILE = 512
COMBINE_TILE = 256
SC_CHUNK = 128
EXPERT_ROWS = 512
COMBINE_GROUPS = 2
VMEM_LIMIT = 48 * 1024 * 1024

NEG_INF = float("-inf")


def _silu(v):
    return v * (1.0 / (1.0 + jnp.exp(-v)))


def _softplus(v):
    return jnp.maximum(v, 0.0) + jnp.log(1.0 + jnp.exp(-jnp.abs(v)))


def _bdot(a, b):
    return jnp.dot(a.astype(BF16), b.astype(BF16), preferred_element_type=F32)


def _split_hi_lo(v):
    hi = v.astype(BF16)
    lo = (v - hi.astype(F32)).astype(BF16)
    return hi, lo


def _pack_bf16_pair(a, b):
    w = pltpu.pack_elementwise([a, b], packed_dtype=BF16)
    return w if w.dtype == jnp.uint32 else lax.bitcast_convert_type(w, jnp.uint32)


def _unpack_bf16_pair(w):
    a = pltpu.unpack_elementwise(w, index=0, packed_dtype=BF16, unpacked_dtype=F32)
    b = pltpu.unpack_elementwise(w, index=1, packed_dtype=BF16, unpacked_dtype=F32)
    return a, b


def _lane_half_mask(shape):
    return lax.broadcasted_iota(jnp.int32, shape, len(shape) - 1) < HALF


def _mod_kernel(c_ref, w_ref, b_ref, o_ref):
    a = _silu(c_ref[...])
    o_ref[...] = jnp.dot(a, w_ref[...], precision=lax.Precision.HIGHEST,
                         preferred_element_type=F32) + b_ref[...]


def _mod_call(c_pad, mod_w, mod_b):
    rows, d = c_pad.shape
    cols = mod_w.shape[1]
    return pl.pallas_call(
        _mod_kernel,
        grid=(cols // d,),
        in_specs=[pl.BlockSpec((rows, d), lambda j: (0, 0)),
                  pl.BlockSpec((d, d), lambda j: (0, j)),
                  pl.BlockSpec((1, d), lambda j: (0, j))],
        out_specs=pl.BlockSpec((rows, d), lambda j: (0, j)),
        out_shape=jax.ShapeDtypeStruct((rows, cols), F32),
        compiler_params=pltpu.CompilerParams(dimension_semantics=("arbitrary",),
                                             vmem_limit_bytes=VMEM_LIMIT),
    )(c_pad, mod_w, mod_b)


def _in_proj_kernel(x_ref, sc_ref, sh_ref, g_ref, wz_ref, wx_ref, wdt_ref, wq_ref, wk_ref, wv_ref,
                    z_ref, xbc_ref, dt_ref, q_ref, k_ref, v_ref):
    xf = x_ref[...]
    ms = jnp.mean(xf * xf, axis=-1, keepdims=True)
    h = xf * lax.rsqrt(ms + EPS) * g_ref[...]
    h = h * (1.0 + sc_ref[0]) + sh_ref[0]
    hb = h.astype(BF16)
    z_ref[...] = jnp.dot(hb, wz_ref[...], preferred_element_type=F32).astype(BF16)
    xbc_ref[...] = jnp.dot(hb, wx_ref[...], preferred_element_type=F32).astype(BF16)
    dt_ref[...] = jnp.dot(hb, wdt_ref[...], preferred_element_type=F32)
    q_ref[...] = jnp.dot(hb, wq_ref[...], preferred_element_type=F32).astype(BF16)
    k_ref[...] = jnp.dot(hb, wk_ref[...], preferred_element_type=F32).astype(BF16)
    v_ref[...] = jnp.dot(hb, wv_ref[...], preferred_element_type=F32).astype(BF16)


def _in_proj_call(x2, sc1, sh1, g1n, wz, wx, wdt, wq, wk, wv, tiles_per_batch, tm):
    n, d = x2.shape
    row = lambda w: pl.BlockSpec((tm, w), lambda i: (i, 0))
    full = lambda a: pl.BlockSpec(a.shape, lambda i: (0, 0))
    per_batch = pl.BlockSpec((1, 1, d), lambda i: (i // tiles_per_batch, 0, 0))
    outs = [(wz.shape[1], BF16), (wx.shape[1], BF16), (wdt.shape[1], F32),
            (wq.shape[1], BF16), (wk.shape[1], BF16), (wv.shape[1], BF16)]
    return pl.pallas_call(
        _in_proj_kernel,
        grid=(n // tm,),
        in_specs=[row(d), per_batch, per_batch, full(g1n), full(wz), full(wx), full(wdt), full(wq),
                  full(wk), full(wv)],
        out_specs=[row(w) for w, _ in outs],
        out_shape=[jax.ShapeDtypeStruct((n, w), dt) for w, dt in outs],
        compiler_params=pltpu.CompilerParams(dimension_semantics=("arbitrary",),
                                             vmem_limit_bytes=VMEM_LIMIT),
    )(x2, sc1, sh1, g1n, wz, wx, wdt, wq, wk, wv)


SSD_SEQS = 4
CONV_HALO = 16


def _conv_shift_matrix():
    s = np.zeros((CONV_K * CHUNK, CONV_HALO + CHUNK), np.float32)
    for k in range(CONV_K):
        t = np.arange(CHUNK)
        s[k * CHUNK + t, CONV_HALO + t - (CONV_K - 1) + k] = 1.0
    return s


def _silu_tanh(v):
    hv = 0.5 * v
    return hv + hv * jnp.tanh(hv)


def _ssd_kernel(xbc_ref, z_ref, dt_ref, cw_ref, cb_ref, dtb_ref, alog_ref, dskip_ref, ng_ref, triu_ref, shift_ref,
                y_ref, state_ref, ucat_ref, ybuf_ref):
    nseq = xbc_ref.shape[0]

    @pl.when(pl.program_id(1) == 0)
    def _():
        state_ref[...] = jnp.zeros_like(state_ref)
        ucat_ref[:, 0:CONV_HALO, :] = jnp.zeros((nseq, CONV_HALO, CONV_CH), BF16)

    for q in range(nseq):
        _ssd_chunk(xbc_ref.at[q], z_ref.at[q], dt_ref.at[q], cw_ref, cb_ref, dtb_ref, alog_ref, dskip_ref, ng_ref,
                   triu_ref, shift_ref, y_ref.at[q], state_ref.at[q], ucat_ref.at[q], ybuf_ref.at[q])


def _ssd_chunk(xbc_ref, z_ref, dt_ref, cw_ref, cb_ref, dtb_ref, alog_ref, dskip_ref, ng_ref, triu_ref, shift_ref,
               y_ref, state_ref, ucat_ref, ybuf_ref):
    ucat_ref[CONV_HALO:, :] = xbc_ref[...]
    shifted = jnp.dot(shift_ref[...], ucat_ref[...], preferred_element_type=F32)
    ucat_ref[0:CONV_HALO, :] = ucat_ref[CHUNK:CHUNK + CONV_HALO, :]
    acc = cb_ref[...] + cw_ref[0:1, :] * shifted[0:CHUNK]
    for kk in range(1, CONV_K):
        acc = acc + cw_ref[kk:kk + 1, :] * shifted[kk * CHUNK:(kk + 1) * CHUNK]
    act = _silu_tanh(acc)
    xs = act[:, :D_SSM]
    gn = SSM_GROUPS * D_STATE

    dt_t = _softplus(dt_ref[...].T[0:SSM_HEADS, :] + dtb_ref[...])
    a_t = dt_t * (-jnp.exp(alog_ref[...]))
    a_hi = a_t.astype(BF16)
    a_mid = (a_t - a_hi.astype(F32)).astype(BF16)
    a_lo = (a_t - a_hi.astype(F32) - a_mid.astype(F32)).astype(BF16)
    triu = triu_ref[...]
    cs_t = (jnp.dot(a_hi, triu, preferred_element_type=F32) + jnp.dot(a_mid, triu, preferred_element_type=F32)
            + jnp.dot(a_lo, triu, preferred_element_type=F32))
    cs_end = cs_t[:, CHUNK - 1:CHUNK]
    r_t = cs_t - jnp.log(dt_t)
    w_t = jnp.exp(cs_end - cs_t) * dt_t
    chunk_decay = jnp.exp(cs_end)
    cols = jnp.concatenate([cs_t, jnp.exp(cs_t), jnp.zeros((LANES - 2 * SSM_HEADS, CHUNK), F32)], axis=0).T

    li = lax.broadcasted_iota(jnp.int32, (CHUNK, CHUNK), 0)
    si = lax.broadcasted_iota(jnp.int32, (CHUNK, CHUNK), 1)
    causal = li >= si
    low = _lane_half_mask((CHUNK, LANES))
    low_row = _lane_half_mask((1, LANES))

    heads_per_group = SSM_HEADS // SSM_GROUPS
    for g in range(SSM_GROUPS):
        b_g = act[:, D_SSM + g * D_STATE:D_SSM + (g + 1) * D_STATE]
        c_g = act[:, D_SSM + gn + g * D_STATE:D_SSM + gn + (g + 1) * D_STATE]
        b_gb = b_g.astype(BF16)
        c_gb = c_g.astype(BF16)
        cb = lax.dot_general(c_gb, b_gb, (((1,), (1,)), ((), ())), preferred_element_type=F32)
        b_t = b_g.T
        for jp in range(heads_per_group // 2):
            j = g * (heads_per_group // 2) + jp
            lanes = slice(j * LANES, (j + 1) * LANES)
            xp = xs[:, lanes]
            xpb = xp.astype(BF16)
            ydiag = jnp.zeros((CHUNK, LANES), F32)
            snew = jnp.zeros((D_STATE, LANES), F32)
            for half in range(2):
                h = 2 * j + half
                diff = cols[:, h:h + 1] - r_t[h:h + 1, :]
                m = (cb * jnp.exp(jnp.where(causal, diff, NEG_INF))).astype(BF16)
                keep = low if half == 0 else jnp.logical_not(low)
                xh = jnp.where(keep, xpb, jnp.zeros_like(xpb))
                ydiag = ydiag + jnp.dot(m, xh, preferred_element_type=F32)
                snew = snew + jnp.dot((b_t * w_t[h:h + 1, :]).astype(BF16), xh, preferred_element_type=F32)
            s_in = state_ref[:, lanes]
            yoff = jnp.dot(c_gb, s_in.astype(BF16), preferred_element_type=F32)
            h0 = 2 * j
            e0 = SSM_HEADS + h0
            escale = jnp.where(low, cols[:, e0:e0 + 1], cols[:, e0 + 1:e0 + 2])
            cdec = jnp.where(low_row, chunk_decay[h0:h0 + 1, :], chunk_decay[h0 + 1:h0 + 2, :])
            ybuf_ref[:, lanes] = ydiag + yoff * escale + xp * dskip_ref[:, lanes]
            state_ref[:, lanes] = s_in * cdec + snew

    yz = ybuf_ref[...] * _silu_tanh(z_ref[...].astype(F32))
    gw = D_SSM // SSM_GROUPS
    for g in range(SSM_GROUPS):
        part = yz[:, g * gw:(g + 1) * gw]
        ms = jnp.mean(part * part, axis=-1, keepdims=True)
        y_ref[:, g * gw:(g + 1) * gw] = (part * lax.rsqrt(ms + EPS)
                                            * ng_ref[:, g * gw:(g + 1) * gw]).astype(BF16)


def _ssd_call(xbc, z, dt, conv_w, conv_b, dtb, alog, dskip, ng, triu, shift):
    bsz, l, _ = xbc.shape
    nc = l // CHUNK
    nseq = SSD_SEQS if bsz % SSD_SEQS == 0 else 1
    chunk = lambda w: pl.BlockSpec((nseq, CHUNK, w), lambda b, c: (b, c, 0))
    full = lambda a: pl.BlockSpec(a.shape, lambda b, c: (0, 0))
    return pl.pallas_call(
        _ssd_kernel,
        grid=(bsz // nseq, nc),
        in_specs=[chunk(CONV_CH), chunk(D_SSM), chunk(LANES), full(conv_w), full(conv_b), full(dtb),
                  full(alog), full(dskip), full(ng), full(triu), full(shift)],
        out_specs=chunk(D_SSM),
        out_shape=jax.ShapeDtypeStruct((bsz, l, D_SSM), BF16),
        scratch_shapes=[pltpu.VMEM((nseq, D_STATE, D_SSM), F32),
                        pltpu.VMEM((nseq, CONV_HALO + CHUNK, CONV_CH), BF16),
                        pltpu.VMEM((nseq, CHUNK, D_SSM), F32)],
        compiler_params=pltpu.CompilerParams(dimension_semantics=("arbitrary", "arbitrary"),
                                             vmem_limit_bytes=VMEM_LIMIT),
    )(xbc, z, dt, conv_w, conv_b, dtb, alog, dskip, ng, triu, shift)


assert WINDOW == ATT_BLOCK


def _rel_bucket_table():
    qi = np.arange(ATT_BLOCK)[:, None]
    c = np.arange(ATT_BLOCK)[None, :]
    dist = np.where(c > qi, qi + ATT_BLOCK - c, qi - c)
    max_exact = REL_BUCKETS // 2
    d = np.maximum(dist, 1).astype(np.float32)
    large = max_exact + (np.log(d / np.float32(max_exact)) / np.float32(math.log(REL_MAX_DIST / max_exact))
                         * np.float32(REL_BUCKETS - max_exact)).astype(np.int32)
    large = np.minimum(large, REL_BUCKETS - 1)
    return np.where(dist < max_exact, dist, large).astype(np.int32)


def _bias_kernel(rb_ref, bucket_ref, o_ref):
    h = pl.program_id(0)
    bucket = bucket_ref[...]
    acc = jnp.zeros(bucket.shape, F32)
    for b in range(REL_BUCKETS):
        acc = jnp.where(bucket == b, rb_ref[b, h], acc)
    from_prev = (lax.broadcasted_iota(jnp.int32, bucket.shape, 1)
                 > lax.broadcasted_iota(jnp.int32, bucket.shape, 0))
    o_ref[1, 0] = acc
    o_ref[0, 0] = jnp.where(from_prev, NEG_INF, acc)


def _bias_call(rel_bias, bucket):
    return pl.pallas_call(
        _bias_kernel,
        grid=(ATT_HEADS,),
        in_specs=[pl.BlockSpec(memory_space=pltpu.SMEM),
                  pl.BlockSpec(bucket.shape, lambda h: (0, 0))],
        out_specs=pl.BlockSpec((2, 1) + bucket.shape, lambda h: (0, h, 0, 0)),
        out_shape=jax.ShapeDtypeStruct((2, ATT_HEADS) + bucket.shape, F32),
        compiler_params=pltpu.CompilerParams(dimension_semantics=("arbitrary",)),
    )(rel_bias, bucket)


ATT_SEQS = 4


def _attn_kernel(sink_ref, q_ref, kp_ref, kc_ref, vp_ref, vc_ref, bias_ref, ng_ref, o_ref, obuf_ref):
    for s in range(q_ref.shape[0]):
        _attn_block(sink_ref, q_ref.at[s], kp_ref.at[s], kc_ref.at[s], vp_ref.at[s], vc_ref.at[s], bias_ref.at[0],
                    ng_ref, o_ref.at[s], obuf_ref.at[s])


def _attn_block(sink_ref, q_ref, kp_ref, kc_ref, vp_ref, vc_ref, bias_ref, ng_ref, o_ref, obuf_ref):
    qi = lax.broadcasted_iota(jnp.int32, (ATT_BLOCK, ATT_BLOCK), 0)
    ci = lax.broadcasted_iota(jnp.int32, (ATT_BLOCK, ATT_BLOCK), 1)
    from_prev = ci > qi
    low = _lane_half_mask((ATT_BLOCK, LANES))

    def band_variants(prev_ref, cur_ref):
        out = []
        for cpair in range(KV_HEADS // 2):
            lanes = slice(cpair * LANES, (cpair + 1) * LANES)
            t = jnp.concatenate([prev_ref[:, lanes], cur_ref[:, lanes]], axis=0).astype(F32)
            out.append((t.astype(BF16), pltpu.roll(t, HALF, 1).astype(BF16)))
        return out

    k_band = band_variants(kp_ref, kc_ref)
    v_band = band_variants(vp_ref, vc_ref)
    nt = (((1,), (1,)), ((), ()))

    for j in range(ATT_HEADS // 2):
        qp = q_ref[:, j * LANES:(j + 1) * LANES]
        out_pair = jnp.zeros((ATT_BLOCK, LANES), F32)
        for half in range(2):
            h = 2 * j + half
            g = h // Q_PER_KV
            swapped = int((g % 2) != half)
            keep = low if half == 0 else jnp.logical_not(low)
            qh = jnp.where(keep, qp, jnp.zeros_like(qp))
            s_band = lax.dot_general(qh, k_band[g // 2][swapped], nt, preferred_element_type=F32)
            s = jnp.where(from_prev, s_band[:, :ATT_BLOCK], s_band[:, ATT_BLOCK:]) + bias_ref[h]
            sink = sink_ref[h]
            m = jnp.maximum(jnp.max(s, axis=-1, keepdims=True), sink)
            p = jnp.exp(s - m)
            denom = jnp.sum(p, axis=-1, keepdims=True) + jnp.exp(sink - m)
            p_band = jnp.concatenate([jnp.where(from_prev, p, 0.0), jnp.where(from_prev, 0.0, p)], axis=1)
            o = jnp.dot(p_band.astype(BF16), v_band[g // 2][swapped], preferred_element_type=F32) / denom
            out_pair = out_pair + jnp.where(keep, o, 0.0)
        obuf_ref[:, j * LANES:(j + 1) * LANES] = out_pair

    att = obuf_ref[...]
    ms = jnp.mean(att * att, axis=-1, keepdims=True)
    o_ref[...] = (att * lax.rsqrt(ms + EPS) * ng_ref[...]).astype(BF16)


def _attn_call(sinks, q, k, v, bias, ng):
    bsz, l, _ = q.shape
    nb = l // ATT_BLOCK
    nseq = ATT_SEQS if bsz % ATT_SEQS == 0 else 1
    cur = lambda w: pl.BlockSpec((nseq, ATT_BLOCK, w), lambda b, i: (b, i, 0))
    prev = lambda w: pl.BlockSpec((nseq, ATT_BLOCK, w), lambda b, i: (b, jnp.maximum(i - 1, 0), 0))
    return pl.pallas_call(
        _attn_kernel,
        grid=(bsz // nseq, nb),
        in_specs=[pl.BlockSpec(memory_space=pltpu.SMEM),
                  cur(D_ATT), prev(D_KV), cur(D_KV), prev(D_KV), cur(D_KV),
                  pl.BlockSpec((1,) + bias.shape[1:], lambda b, i: (jnp.minimum(i, 1), 0, 0, 0)),
                  pl.BlockSpec(ng.shape, lambda b, i: (0, 0))],
        out_specs=cur(D_ATT),
        out_shape=jax.ShapeDtypeStruct((bsz, l, D_ATT), BF16),
        scratch_shapes=[pltpu.VMEM((nseq, ATT_BLOCK, D_ATT), F32)],
        compiler_params=pltpu.CompilerParams(dimension_semantics=("arbitrary", "arbitrary"),
                                             vmem_limit_bytes=VMEM_LIMIT),
    )(sinks, q, k, k, v, v, bias, ng)


def _out_proj_kernel(x_ref, ys_ref, ya_ref, g1_ref, sc_ref, sh_ref, g2_ref, ng_ref, wos_ref, woa_ref,
                     rwh_ref, rwl_ref, sg_ref, su_ref, sd_ref, base_ref, h_ref, lg_ref):
    mix = (jnp.dot(ys_ref[...], wos_ref[...], preferred_element_type=F32)
           + jnp.dot(ya_ref[...], woa_ref[...], preferred_element_type=F32))
    x1 = x_ref[...] + g1_ref[0] * mix
    ms = jnp.mean(x1 * x1, axis=-1, keepdims=True)
    h = x1 * lax.rsqrt(ms + EPS) * ng_ref[...]
    h = h * (1.0 + sc_ref[0]) + sh_ref[0]
    half = h.shape[1] // 2
    h_ref[...] = _pack_bf16_pair(h[:, :half], h[:, half:])
    hi, lo = _split_hi_lo(h)
    logits = (jnp.dot(hi, rwh_ref[...], preferred_element_type=F32)
              + jnp.dot(lo, rwh_ref[...], preferred_element_type=F32)
              + jnp.dot(hi, rwl_ref[...], preferred_element_type=F32))
    lg_ref[...] = logits.T
    u = _silu(jnp.dot(hi, sg_ref[...], preferred_element_type=F32)) * jnp.dot(hi, su_ref[...],
                                                                              preferred_element_type=F32)
    shared = jnp.dot(u.astype(BF16), sd_ref[...], preferred_element_type=F32)
    base_ref[...] = x1 + g2_ref[0] * shared


def _out_proj_call(x2, ys, ya, g1, sc2, sh2, g2, ng, wos, woa, rwh, rwl, sg, su, sd, tiles_per_batch, tm):
    n, d = x2.shape
    row = lambda w: pl.BlockSpec((tm, w), lambda i: (i, 0))
    full = lambda a: pl.BlockSpec(a.shape, lambda i: (0, 0))
    per_batch = pl.BlockSpec((1, 1, d), lambda i: (i // tiles_per_batch, 0, 0))
    return pl.pallas_call(
        _out_proj_kernel,
        grid=(n // tm,),
        in_specs=[row(d), row(D_SSM), row(D_ATT), per_batch, per_batch, per_batch, per_batch, full(ng),
                  full(wos), full(woa), full(rwh), full(rwl), full(sg), full(su), full(sd)],
        out_specs=[row(d), row(d // 2), pl.BlockSpec((LANES, tm), lambda i: (0, i))],
        out_shape=[jax.ShapeDtypeStruct((n, d), F32), jax.ShapeDtypeStruct((n, d // 2), jnp.uint32),
                   jax.ShapeDtypeStruct((LANES, n), F32)],
        compiler_params=pltpu.CompilerParams(dimension_semantics=("arbitrary",),
                                             vmem_limit_bytes=VMEM_LIMIT),
    )(x2, ys, ya, g1, sc2, sh2, g2, ng, wos, woa, rwh, rwl, sg, su, sd)


def _route_kernel(lg_ref, rb_ref, upper_ref, idx_ref, gate_ref, rank_ref, cnt_ref, carry_ref):
    step = pl.program_id(0)

    @pl.when(step == 0)
    def _():
        carry_ref[...] = jnp.zeros_like(carry_ref)

    t = lg_ref.shape[1]
    per_group = N_EXPERTS // ROUTE_GROUPS
    scores = 1.0 / (1.0 + jnp.exp(-lg_ref[0:N_EXPERTS, :]))
    sel = scores + rb_ref[...]
    e_iota = lax.broadcasted_iota(jnp.int32, (N_EXPERTS, t), 0)

    sel3 = sel.reshape(ROUTE_GROUPS, per_group, t)
    w_iota = lax.broadcasted_iota(jnp.int32, sel3.shape, 1)
    m1 = jnp.max(sel3, axis=1, keepdims=True)
    first = jnp.min(jnp.where(sel3 == m1, w_iota, per_group), axis=1, keepdims=True)
    m2 = jnp.max(jnp.where(w_iota == first, NEG_INF, sel3), axis=1, keepdims=True)
    grp = (m1 + m2).reshape(ROUTE_GROUPS, t)

    g_iota = lax.broadcasted_iota(jnp.int32, (ROUTE_GROUPS, t), 0)
    gmask = jnp.zeros((ROUTE_GROUPS, t), jnp.bool_)
    for _ in range(ROUTE_TOPK_GROUPS):
        gm = jnp.max(grp, axis=0, keepdims=True)
        gfirst = jnp.min(jnp.where(grp == gm, g_iota, ROUTE_GROUPS), axis=0, keepdims=True)
        hit = g_iota == gfirst
        gmask = jnp.logical_or(gmask, hit)
        grp = jnp.where(hit, NEG_INF, grp)
    allowed = jnp.broadcast_to(gmask.reshape(ROUTE_GROUPS, 1, t),
                               (ROUTE_GROUPS, per_group, t)).reshape(N_EXPERTS, t)
    masked = jnp.where(allowed, sel, NEG_INF)

    picked = jnp.zeros((N_EXPERTS, t), jnp.bool_)
    idx_rows = []
    w_rows = []
    for _ in range(TOP_K):
        mm = jnp.max(masked, axis=0, keepdims=True)
        efirst = jnp.min(jnp.where(masked == mm, e_iota, N_EXPERTS), axis=0, keepdims=True)
        hit = e_iota == efirst
        idx_rows.append(efirst)
        w_rows.append(jnp.sum(jnp.where(hit, scores, 0.0), axis=0, keepdims=True))
        picked = jnp.logical_or(picked, hit)
        masked = jnp.where(hit, NEG_INF, masked)
    idx = jnp.concatenate(idx_rows, axis=0)
    w = jnp.concatenate(w_rows, axis=0)
    gate_ref[...] = w / jnp.sum(w, axis=0, keepdims=True) * ROUTED_SCALE
    idx_ref[...] = idx

    onehot = jnp.where(picked, 1.0, 0.0)
    before = jnp.dot(onehot.astype(BF16), upper_ref[...], preferred_element_type=F32)
    rank_full = before + carry_ref[:, 0:1]
    rank_rows = [jnp.sum(jnp.where(e_iota == idx_rows[k], rank_full, 0.0), axis=0, keepdims=True)
                 for k in range(TOP_K)]
    rank_ref[...] = jnp.concatenate(rank_rows, axis=0).astype(jnp.int32)
    carry_ref[...] = carry_ref[...] + jnp.sum(onehot, axis=1, keepdims=True)
    cnt_ref[...] = carry_ref[...]


def _route_call(logits_t, router_bias, upper, tile):
    n = logits_t.shape[1]
    tok = lambda r: pl.BlockSpec((r, tile), lambda i: (0, i))
    return pl.pallas_call(
        _route_kernel,
        grid=(n // tile,),
        in_specs=[tok(LANES), pl.BlockSpec((N_EXPERTS, 1), lambda i: (0, 0)),
                  pl.BlockSpec(upper.shape, lambda i: (0, 0))],
        out_specs=[tok(TOP_K), tok(TOP_K), tok(TOP_K), pl.BlockSpec((N_EXPERTS, LANES), lambda i: (0, 0))],
        out_shape=[jax.ShapeDtypeStruct((TOP_K, n), jnp.int32), jax.ShapeDtypeStruct((TOP_K, n), F32),
                   jax.ShapeDtypeStruct((TOP_K, n), jnp.int32),
                   jax.ShapeDtypeStruct((N_EXPERTS, LANES), F32)],
        scratch_shapes=[pltpu.VMEM((N_EXPERTS, LANES), F32)],
        compiler_params=pltpu.CompilerParams(dimension_semantics=("arbitrary",),
                                             vmem_limit_bytes=VMEM_LIMIT),
    )(logits_t, router_bias, upper)


def _dest_kernel(idx_ref, rank_ref, start_ref, dest_ref):
    t = idx_ref.shape[1]
    e_iota = lax.broadcasted_iota(jnp.int32, (N_EXPERTS, t), 0)
    rows = [jnp.sum(jnp.where(e_iota == idx_ref[k:k + 1, :], start_ref[...], 0), axis=0, keepdims=True)
            for k in range(TOP_K)]
    dest_ref[...] = jnp.concatenate(rows, axis=0) + rank_ref[...]


def _dest_call(idx, rank, pad_start, tile):
    n = idx.shape[1]
    tok = pl.BlockSpec((TOP_K, tile), lambda i: (0, i))
    return pl.pallas_call(
        _dest_kernel,
        grid=(n // tile,),
        in_specs=[tok, tok, pl.BlockSpec((N_EXPERTS, 1), lambda i: (0, 0))],
        out_specs=tok,
        out_shape=jax.ShapeDtypeStruct((TOP_K, n), jnp.int32),
        compiler_params=pltpu.CompilerParams(dimension_semantics=("arbitrary",)),
    )(idx, rank, pad_start)


def _scatter_rows_sc(rows, dest_flat, total_rows, chunk):
    n, w = rows.shape
    copies = dest_flat.shape[0] // n
    info = plsc.get_sparse_core_info()
    nc = info.num_cores
    per_worker = n // (nc * info.num_subcores)
    assert per_worker * nc * info.num_subcores == n and per_worker % chunk == 0
    mesh = plsc.VectorSubcoreMesh(core_axis_name="c", subcore_axis_name="s")

    @functools.partial(
        pl.kernel, mesh=mesh,
        out_type=jax.ShapeDtypeStruct((total_rows, w), rows.dtype),
        scratch_types=[pltpu.VMEM((chunk,), jnp.int32), pltpu.VMEM((chunk, w), rows.dtype),
                       pltpu.SemaphoreType.DMA],
    )
    def scatter(rows_hbm, idx_hbm, out_hbm, idx_v, rows_v, sem):
        worker = lax.axis_index("s") * nc + lax.axis_index("c")

        @pl.loop(0, per_worker // chunk)
        def _(j):
            base = worker * per_worker + j * chunk
            pltpu.sync_copy(rows_hbm.at[pl.ds(base, chunk)], rows_v)
            for k in range(copies):
                pltpu.sync_copy(idx_hbm.at[pl.ds(k * n + base, chunk)], idx_v)
                pltpu.async_copy(rows_v, out_hbm.at[idx_v], sem).wait()

    return scatter(rows, dest_flat)


X_RING = 3


def _expert_kernel(be_ref, nused_ref, fresh_ref, valid_ref, xs_hbm, wg_ref, wu_ref, wd_ref, y_ref, wgb_ref, wub_ref,
                   wdb_ref, xbuf_ref, xsem):
    i = pl.program_id(0)
    rows, half = xbuf_ref.shape[1], xbuf_ref.shape[2]
    nused = nused_ref[0]

    def x_copy(block):
        first = pl.multiple_of(block * rows, rows)
        slot = block % X_RING
        return pltpu.make_async_copy(xs_hbm.at[pl.ds(first, rows)], xbuf_ref.at[slot], xsem.at[slot])

    @pl.when(i == 0)
    def _():
        for b in range(X_RING - 1):
            @pl.when(b < nused)
            def _():
                x_copy(b).start()

    @pl.when(i + (X_RING - 1) < nused)
    def _():
        x_copy(i + (X_RING - 1)).start()

    @pl.when(fresh_ref[i] > 0)
    def _():
        wgb_ref[...] = wg_ref[0].astype(BF16)
        wub_ref[...] = wu_ref[0].astype(BF16)
        wdb_ref[...] = wd_ref[0].astype(BF16)

    @pl.when(i < nused)
    def _():
        x_copy(i).wait()
        xw = xbuf_ref[i % X_RING]
        row = lax.broadcasted_iota(jnp.int32, xw.shape, 0)
        x_lo, x_hi = _unpack_bf16_pair(jnp.where(row < valid_ref[i], xw, jnp.uint32(0)))
        x_lo = x_lo.astype(BF16)
        x_hi = x_hi.astype(BF16)
        gate = (jnp.dot(x_lo, wgb_ref[:half, :], preferred_element_type=F32)
                + jnp.dot(x_hi, wgb_ref[half:, :], preferred_element_type=F32))
        up = (jnp.dot(x_lo, wub_ref[:half, :], preferred_element_type=F32)
              + jnp.dot(x_hi, wub_ref[half:, :], preferred_element_type=F32))
        u = (_silu(gate) * up).astype(BF16)
        y_lo = jnp.dot(u, wdb_ref[:, :half], preferred_element_type=F32)
        y_hi = jnp.dot(u, wdb_ref[:, half:], preferred_element_type=F32)
        y_ref[...] = _pack_bf16_pair(y_lo, y_hi)

    @pl.when(i >= nused_ref[0])
    def _():
        y_ref[...] = jnp.zeros_like(y_ref)


def _expert_call(block_expert, nused, fresh, valid, xs, wg, wu, wd, rows):
    p, w = xs.shape
    d, f = wg.shape[1], wg.shape[2]
    w_map = lambda i, be, nu, fr, va: (be[i], 0, 0)
    grid_spec = pltpu.PrefetchScalarGridSpec(
        num_scalar_prefetch=4,
        grid=(p // rows,),
        in_specs=[pl.BlockSpec(memory_space=pl.ANY),
                  pl.BlockSpec((1, d, f), w_map), pl.BlockSpec((1, d, f), w_map), pl.BlockSpec((1, f, d), w_map)],
        out_specs=pl.BlockSpec((rows, w), lambda i, be, nu, fr, va: (i, 0)),
        scratch_shapes=[pltpu.VMEM((d, f), BF16), pltpu.VMEM((d, f), BF16), pltpu.VMEM((f, d), BF16),
                        pltpu.VMEM((X_RING, rows, w), xs.dtype), pltpu.SemaphoreType.DMA((X_RING,))],
    )
    return pl.pallas_call(
        _expert_kernel,
        grid_spec=grid_spec,
        out_shape=jax.ShapeDtypeStruct((p, w), jnp.uint32),
        compiler_params=pltpu.CompilerParams(dimension_semantics=("arbitrary",),
                                             vmem_limit_bytes=VMEM_LIMIT),
    )(block_expert, nused, fresh, valid, xs, wg, wu, wd)


def _gather_rows_sc(table, idx, chunk):
    m = idx.shape[0]
    w = table.shape[1]
    info = plsc.get_sparse_core_info()
    nc = info.num_cores
    per_worker = m // (nc * info.num_subcores)
    assert per_worker * nc * info.num_subcores == m and per_worker % chunk == 0
    mesh = plsc.VectorSubcoreMesh(core_axis_name="c", subcore_axis_name="s")

    @functools.partial(
        pl.kernel, mesh=mesh,
        out_type=jax.ShapeDtypeStruct((m, w), table.dtype),
        scratch_types=[pltpu.VMEM((chunk,), jnp.int32), pltpu.VMEM((chunk, w), table.dtype),
                       pltpu.SemaphoreType.DMA],
    )
    def gather(table_hbm, idx_hbm, out_hbm, idx_v, rows_v, sem):
        worker = lax.axis_index("s") * nc + lax.axis_index("c")

        @pl.loop(0, per_worker // chunk)
        def _(j):
            base = worker * per_worker + j * chunk
            pltpu.sync_copy(idx_hbm.at[pl.ds(base, chunk)], idx_v)
            pltpu.async_copy(table_hbm.at[idx_v], rows_v, sem).wait()
            pltpu.sync_copy(rows_v, out_hbm.at[pl.ds(base, chunk)])

    return gather(table, idx)


def _combine_kernel(yk_ref, gate_ref, base_ref, g2_ref, fg_ref, o_ref):
    t = base_ref.shape[0]
    half = yk_ref.shape[2]
    gates = gate_ref[...]
    r_lo = jnp.zeros((t, half), F32)
    r_hi = jnp.zeros((t, half), F32)
    for k in range(TOP_K):
        y_lo, y_hi = _unpack_bf16_pair(yk_ref[k])
        r_lo = r_lo + gates[:, k:k + 1] * y_lo
        r_hi = r_hi + gates[:, k:k + 1] * y_hi
    g2 = g2_ref[0]
    x_lo = base_ref[:, :half] + g2[:, :half] * r_lo
    x_hi = base_ref[:, half:] + g2[:, half:] * r_hi
    ms = (jnp.sum(x_lo * x_lo, axis=-1, keepdims=True)
          + jnp.sum(x_hi * x_hi, axis=-1, keepdims=True)) * (1.0 / (2 * half))
    inv = lax.rsqrt(ms + EPS)
    o_ref[:, :half] = x_lo * inv * fg_ref[:, :half]
    o_ref[:, half:] = x_hi * inv * fg_ref[:, half:]


def _combine_call(yk, gates_t, base, g2, fg, tiles_per_batch, tile, tile0):
    n, d = base.shape
    row = lambda w: pl.BlockSpec((tile, w), lambda i: (i + tile0, 0))
    return pl.pallas_call(
        _combine_kernel,
        grid=(yk.shape[1] // tile,),
        in_specs=[pl.BlockSpec((TOP_K, tile, yk.shape[2]), lambda i: (0, i, 0)),
                  row(TOP_K), row(d),
                  pl.BlockSpec((1, 1, d), lambda i: ((i + tile0) // tiles_per_batch, 0, 0)),
                  pl.BlockSpec((1, d), lambda i: (0, 0))],
        out_specs=row(d),
        out_shape=jax.ShapeDtypeStruct((n, d), F32),
        input_output_aliases={2: 0},
        compiler_params=pltpu.CompilerParams(dimension_semantics=("arbitrary",),
                                             vmem_limit_bytes=VMEM_LIMIT),
    )(yk, gates_t, base, g2, fg)


def _pad_cols(a, width):
    return jnp.pad(a, ((0, 0), (0, width - a.shape[1])))


def _layer(x, mod, norm1_g, norm2_g, w_in, conv_w, conv_b, dt_bias, a_log, d_skip, ssm_norm_g,
           att_norm_g, sinks, rel_bias, w_out, router_w, router_bias, exp_w_gate, exp_w_up, exp_w_down,
           sh_w_gate, sh_w_up, sh_w_down, final_g):
    bsz, l, d = x.shape
    n = bsz * l
    tm = min(ROW_TILE, l)

    sh1, sc1, g1, sh2, sc2, g2 = [m[:, None, :] for m in jnp.split(mod, 6, axis=-1)]

    i1 = D_SSM
    i2 = i1 + CONV_CH
    i3 = i2 + SSM_HEADS
    i4 = i3 + D_ATT
    i5 = i4 + D_KV
    wz, wx, wdt, wq, wk, wv = jnp.split(w_in, [i1, i2, i3, i4, i5], axis=-1)
    wdt = _pad_cols(wdt, LANES)
    q_scale = ATT_HEAD_DIM ** -0.5
    assert math.frexp(q_scale)[0] == 0.5
    x2 = x.reshape(n, d)
    z, xbc, dt, q, k, v = _in_proj_call(x2, sc1, sh1, norm1_g[None, :], wz.astype(BF16), wx.astype(BF16),
                                        wdt.astype(BF16), (wq * q_scale).astype(BF16), wk.astype(BF16),
                                        wv.astype(BF16), l // tm, tm)

    triu = jnp.asarray(np.triu(np.ones((CHUNK, CHUNK), np.float32))).astype(BF16)
    shift = jnp.asarray(_conv_shift_matrix()).astype(BF16)
    y_ssm = _ssd_call(xbc.reshape(bsz, l, CONV_CH), z.reshape(bsz, l, D_SSM), dt.reshape(bsz, l, LANES),
                      conv_w, conv_b[None, :], dt_bias[:, None], a_log[:, None],
                      jnp.repeat(d_skip, SSM_HEAD_DIM)[None, :], ssm_norm_g[None, :], triu, shift)

    bias = _bias_call(rel_bias, jnp.asarray(_rel_bucket_table()))
    y_att = _attn_call(sinks, q.reshape(bsz, l, D_ATT), k.reshape(bsz, l, D_KV), v.reshape(bsz, l, D_KV), bias,
                       att_norm_g[None, :])

    rw = _pad_cols(router_w, LANES)
    rwh = rw.astype(BF16)
    rwl = (rw - rwh.astype(F32)).astype(BF16)
    base, h2, logits_t = _out_proj_call(
        x2, y_ssm.reshape(n, D_SSM), y_att.reshape(n, D_ATT), g1, sc2, sh2, g2, norm2_g[None, :],
        w_out[:D_SSM].astype(BF16), w_out[D_SSM:].astype(BF16), rwh, rwl,
        sh_w_gate.astype(BF16), sh_w_up.astype(BF16), sh_w_down.astype(BF16), l // tm, tm)

    rt = min(ROUTE_TILE, n)
    upper = jnp.asarray(np.triu(np.ones((rt, rt), np.float32), 1)).astype(BF16)
    idx, gates, rank, counts = _route_call(logits_t, router_bias[:, None], upper, rt)

    nblocks = (n * TOP_K + N_EXPERTS * (EXPERT_ROWS - 1) + EXPERT_ROWS - 1) // EXPERT_ROWS
    cnt = counts[:, 0].astype(jnp.int32)
    padded = (cnt + EXPERT_ROWS - 1) // EXPERT_ROWS * EXPERT_ROWS
    pad_end = jnp.cumsum(padded)
    pad_start = pad_end - padded
    dest = _dest_call(idx, rank, pad_start[:, None], rt)
    block_first_row = jnp.arange(nblocks, dtype=jnp.int32) * EXPERT_ROWS
    block_expert = jnp.minimum(jnp.sum((pad_end[None, :] <= block_first_row[:, None]).astype(jnp.int32), axis=1),
                               N_EXPERTS - 1)
    fresh = jnp.concatenate([jnp.ones((1,), jnp.int32),
                             (block_expert[1:] != block_expert[:-1]).astype(jnp.int32)])
    nused = (pad_end[-1:] // EXPERT_ROWS).astype(jnp.int32)
    seg_row = block_first_row - jnp.sum(jnp.where(pad_end[None, :] <= block_first_row[:, None], padded[None, :], 0),
                                        axis=1)
    block_cnt = jnp.sum(jnp.where(block_expert[:, None] == jnp.arange(N_EXPERTS, dtype=jnp.int32)[None, :],
                                  cnt[None, :], 0), axis=1)
    valid = jnp.clip(block_cnt - seg_row, 0, EXPERT_ROWS).astype(jnp.int32)

    dest_flat = dest.reshape(-1)
    xs = _scatter_rows_sc(h2, dest_flat, nblocks * EXPERT_ROWS, SC_CHUNK)
    ys = _expert_call(block_expert, nused, fresh, valid, xs, exp_w_gate, exp_w_up, exp_w_down, EXPERT_ROWS)
    ctile = min(COMBINE_TILE, l)
    groups = COMBINE_GROUPS if bsz % COMBINE_GROUPS == 0 else 1
    ng = n // groups
    gates_t = gates.T
    out = base
    for g in range(groups):
        idx_g = dest[:, g * ng:(g + 1) * ng].reshape(-1)
        yk = _gather_rows_sc(ys, idx_g, SC_CHUNK).reshape(TOP_K, ng, ys.shape[1])
        out = _combine_call(yk, gates_t, out, g2, final_g[None, :], l // ctile, ctile, g * ng // ctile)
    return out.reshape(bsz, l, d)


def kernel(x, c, mod_w, mod_b, norm1_g, norm2_g, w_in, conv_w, conv_b, dt_bias, a_log, d_skip, ssm_norm_g,
           att_norm_g, sinks, rel_bias, w_out, router_w, router_bias, exp_w_gate, exp_w_up, exp_w_down,
           sh_w_gate, sh_w_up, sh_w_down, final_g):
    assert mod_w.shape[0] == 1, "single-layer block"
    bsz = x.shape[0]
    c_pad = jnp.pad(c, ((0, SUBLANES - bsz % SUBLANES if bsz % SUBLANES else 0), (0, 0)))
    mod = _mod_call(c_pad, mod_w[0], mod_b[0][None, :])[:bsz]
    return _layer(x, mod, norm1_g[0], norm2_g[0], w_in[0], conv_w[0], conv_b[0], dt_bias[0], a_log[0], d_skip[0],
                  ssm_norm_g[0], att_norm_g[0], sinks[0], rel_bias, w_out[0], router_w[0], router_bias[0],
                  exp_w_gate[0], exp_w_up[0], exp_w_down[0], sh_w_gate[0], sh_w_up[0], sh_w_down[0], final_g)
```

```python
import functools
import math

import numpy as np
import jax
import jax.numpy as jnp
from jax import lax
from jax.experimental import pallas as pl
from jax.experimental.pallas import tpu as pltpu
from jax.experimental.pallas import tpu_sc as plsc

F32 = jnp.float32
BF16 = jnp.bfloat16

D_MODEL = 1024
SSM_HEAD_DIM = 64
D_SSM = D_MODEL
SSM_HEADS = D_SSM // SSM_HEAD_DIM
SSM_GROUPS = 4
D_STATE = 128
CONV_K = 4
CONV_CH = D_SSM + 2 * SSM_GROUPS * D_STATE
CHUNK = 128
ATT_HEAD_DIM = 64
D_ATT = D_MODEL
ATT_HEADS = D_ATT // ATT_HEAD_DIM
KV_HEADS = ATT_HEADS // 4
Q_PER_KV = ATT_HEADS // KV_HEADS
D_KV = KV_HEADS * ATT_HEAD_DIM
WINDOW = 128
ATT_BLOCK = 128
REL_BUCKETS = 32
REL_MAX_DIST = 128
N_EXPERTS = 64
TOP_K = 8
EXPERT_DIM = D_MODEL // 4
SHARED_DIM = D_MODEL // 4
ROUTE_GROUPS = 8
ROUTE_TOPK_GROUPS = 4
ROUTED_SCALE = 2.5
EPS = 1e-6

LANES = 128
SUBLANES = 8
HALF = LANES // 2

ROW_TILE = 512
ROUTE_TILE = 512
COMBINE_TILE = 256
SC_CHUNK = 128
EXPERT_ROWS = 512
COMBINE_GROUPS = 2
VMEM_LIMIT = 48 * 1024 * 1024

NEG_INF = float("-inf")


def _silu(v):
    return v * (1.0 / (1.0 + jnp.exp(-v)))


def _softplus(v):
    return jnp.maximum(v, 0.0) + jnp.log(1.0 + jnp.exp(-jnp.abs(v)))


def _bdot(a, b):
    return jnp.dot(a.astype(BF16), b.astype(BF16), preferred_element_type=F32)


def _split_hi_lo(v):
    hi = v.astype(BF16)
    lo = (v - hi.astype(F32)).astype(BF16)
    return hi, lo


def _pack_bf16_pair(a, b):
    w = pltpu.pack_elementwise([a, b], packed_dtype=BF16)
    return w if w.dtype == jnp.uint32 else lax.bitcast_convert_type(w, jnp.uint32)


def _unpack_bf16_pair(w):
    a = pltpu.unpack_elementwise(w, index=0, packed_dtype=BF16, unpacked_dtype=F32)
    b = pltpu.unpack_elementwise(w, index=1, packed_dtype=BF16, unpacked_dtype=F32)
    return a, b


def _lane_half_mask(shape):
    return lax.broadcasted_iota(jnp.int32, shape, len(shape) - 1) < HALF


def _mod_kernel(c_ref, w_ref, b_ref, o_ref):
    a = _silu(c_ref[...])
    o_ref[...] = jnp.dot(a, w_ref[...], precision=lax.Precision.HIGHEST,
                         preferred_element_type=F32) + b_ref[...]


def _mod_call(c_pad, mod_w, mod_b):
    rows, d = c_pad.shape
    cols = mod_w.shape[1]
    return pl.pallas_call(
        _mod_kernel,
        grid=(cols // d,),
        in_specs=[pl.BlockSpec((rows, d), lambda j: (0, 0)),
                  pl.BlockSpec((d, d), lambda j: (0, j)),
                  pl.BlockSpec((1, d), lambda j: (0, j))],
        out_specs=pl.BlockSpec((rows, d), lambda j: (0, j)),
        out_shape=jax.ShapeDtypeStruct((rows, cols), F32),
        compiler_params=pltpu.CompilerParams(dimension_semantics=("arbitrary",),
                                             vmem_limit_bytes=VMEM_LIMIT),
    )(c_pad, mod_w, mod_b)


def _in_proj_kernel(x_ref, sc_ref, sh_ref, g_ref, wz_ref, wx_ref, wdt_ref, wq_ref, wk_ref, wv_ref,
                    z_ref, xbc_ref, dt_ref, q_ref, k_ref, v_ref):
    xf = x_ref[...]
    ms = jnp.mean(xf * xf, axis=-1, keepdims=True)
    h = xf * lax.rsqrt(ms + EPS) * g_ref[...]
    h = h * (1.0 + sc_ref[0]) + sh_ref[0]
    hb = h.astype(BF16)
    z_ref[...] = jnp.dot(hb, wz_ref[...], preferred_element_type=F32).astype(BF16)
    xbc_ref[...] = jnp.dot(hb, wx_ref[...], preferred_element_type=F32).astype(BF16)
    dt_ref[...] = jnp.dot(hb, wdt_ref[...], preferred_element_type=F32)
    q_ref[...] = jnp.dot(hb, wq_ref[...], preferred_element_type=F32).astype(BF16)
    k_ref[...] = jnp.dot(hb, wk_ref[...], preferred_element_type=F32).astype(BF16)
    v_ref[...] = jnp.dot(hb, wv_ref[...], preferred_element_type=F32).astype(BF16)


def _in_proj_call(x2, sc1, sh1, g1n, wz, wx, wdt, wq, wk, wv, tiles_per_batch, tm):
    n, d = x2.shape
    row = lambda w: pl.BlockSpec((tm, w), lambda i: (i, 0))
    full = lambda a: pl.BlockSpec(a.shape, lambda i: (0, 0))
    per_batch = pl.BlockSpec((1, 1, d), lambda i: (i // tiles_per_batch, 0, 0))
    outs = [(wz.shape[1], BF16), (wx.shape[1], BF16), (wdt.shape[1], F32),
            (wq.shape[1], BF16), (wk.shape[1], BF16), (wv.shape[1], BF16)]
    return pl.pallas_call(
        _in_proj_kernel,
        grid=(n // tm,),
        in_specs=[row(d), per_batch, per_batch, full(g1n), full(wz), full(wx), full(wdt), full(wq),
                  full(wk), full(wv)],
        out_specs=[row(w) for w, _ in outs],
        out_shape=[jax.ShapeDtypeStruct((n, w), dt) for w, dt in outs],
        compiler_params=pltpu.CompilerParams(dimension_semantics=("arbitrary",),
                                             vmem_limit_bytes=VMEM_LIMIT),
    )(x2, sc1, sh1, g1n, wz, wx, wdt, wq, wk, wv)


SSD_SEQS = 4
CONV_HALO = 16


def _conv_shift_matrix():
    s = np.zeros((CONV_K * CHUNK, CONV_HALO + CHUNK), np.float32)
    for k in range(CONV_K):
        t = np.arange(CHUNK)
        s[k * CHUNK + t, CONV_HALO + t - (CONV_K - 1) + k] = 1.0
    return s


def _silu_tanh(v):
    hv = 0.5 * v
    return hv + hv * jnp.tanh(hv)


def _ssd_kernel(xbc_ref, z_ref, dt_ref, cw_ref, cb_ref, dtb_ref, alog_ref, dskip_ref, ng_ref, triu_ref, shift_ref,
                y_ref, state_ref, ucat_ref, ybuf_ref):
    nseq = xbc_ref.shape[0]

    @pl.when(pl.program_id(1) == 0)
    def _():
        state_ref[...] = jnp.zeros_like(state_ref)
        ucat_ref[:, 0:CONV_HALO, :] = jnp.zeros((nseq, CONV_HALO, CONV_CH), BF16)

    for q in range(nseq):
        _ssd_chunk(xbc_ref.at[q], z_ref.at[q], dt_ref.at[q], cw_ref, cb_ref, dtb_ref, alog_ref, dskip_ref, ng_ref,
                   triu_ref, shift_ref, y_ref.at[q], state_ref.at[q], ucat_ref.at[q], ybuf_ref.at[q])


def _ssd_chunk(xbc_ref, z_ref, dt_ref, cw_ref, cb_ref, dtb_ref, alog_ref, dskip_ref, ng_ref, triu_ref, shift_ref,
               y_ref, state_ref, ucat_ref, ybuf_ref):
    ucat_ref[CONV_HALO:, :] = xbc_ref[...]
    shifted = jnp.dot(shift_ref[...], ucat_ref[...], preferred_element_type=F32)
    ucat_ref[0:CONV_HALO, :] = ucat_ref[CHUNK:CHUNK + CONV_HALO, :]
    acc = cb_ref[...] + cw_ref[0:1, :] * shifted[0:CHUNK]
    for kk in range(1, CONV_K):
        acc = acc + cw_ref[kk:kk + 1, :] * shifted[kk * CHUNK:(kk + 1) * CHUNK]
    act = _silu_tanh(acc)
    xs = act[:, :D_SSM]
    gn = SSM_GROUPS * D_STATE

    dt_t = _softplus(dt_ref[...].T[0:SSM_HEADS, :] + dtb_ref[...])
    a_t = dt_t * (-jnp.exp(alog_ref[...]))
    a_hi = a_t.astype(BF16)
    a_mid = (a_t - a_hi.astype(F32)).astype(BF16)
    a_lo = (a_t - a_hi.astype(F32) - a_mid.astype(F32)).astype(BF16)
    triu = triu_ref[...]
    cs_t = (jnp.dot(a_hi, triu, preferred_element_type=F32) + jnp.dot(a_mid, triu, preferred_element_type=F32)
            + jnp.dot(a_lo, triu, preferred_element_type=F32))
    cs_end = cs_t[:, CHUNK - 1:CHUNK]
    r_t = cs_t - jnp.log(dt_t)
    w_t = jnp.exp(cs_end - cs_t) * dt_t
    chunk_decay = jnp.exp(cs_end)
    cols = jnp.concatenate([cs_t, jnp.exp(cs_t), jnp.zeros((LANES - 2 * SSM_HEADS, CHUNK), F32)], axis=0).T

    li = lax.broadcasted_iota(jnp.int32, (CHUNK, CHUNK), 0)
    si = lax.broadcasted_iota(jnp.int32, (CHUNK, CHUNK), 1)
    causal = li >= si
    low = _lane_half_mask((CHUNK, LANES))
    low_row = _lane_half_mask((1, LANES))

    heads_per_group = SSM_HEADS // SSM_GROUPS
    for g in range(SSM_GROUPS):
        b_g = act[:, D_SSM + g * D_STATE:D_SSM + (g + 1) * D_STATE]
        c_g = act[:, D_SSM + gn + g * D_STATE:D_SSM + gn + (g + 1) * D_STATE]
        b_gb = b_g.astype(BF16)
        c_gb = c_g.astype(BF16)
        cb = lax.dot_general(c_gb, b_gb, (((1,), (1,)), ((), ())), preferred_element_type=F32)
        b_t = b_g.T
        for jp in range(heads_per_group // 2):
            j = g * (heads_per_group // 2) + jp
            lanes = slice(j * LANES, (j + 1) * LANES)
            xp = xs[:, lanes]
            xpb = xp.astype(BF16)
            ydiag = jnp.zeros((CHUNK, LANES), F32)
            snew = jnp.zeros((D_STATE, LANES), F32)
            for half in range(2):
                h = 2 * j + half
                diff = cols[:, h:h + 1] - r_t[h:h + 1, :]
                m = (cb * jnp.exp(jnp.where(causal, diff, NEG_INF))).astype(BF16)
                keep = low if half == 0 else jnp.logical_not(low)
                xh = jnp.where(keep, xpb, jnp.zeros_like(xpb))
                ydiag = ydiag + jnp.dot(m, xh, preferred_element_type=F32)
                snew = snew + jnp.dot((b_t * w_t[h:h + 1, :]).astype(BF16), xh, preferred_element_type=F32)
            s_in = state_ref[:, lanes]
            yoff = jnp.dot(c_gb, s_in.astype(BF16), preferred_element_type=F32)
            h0 = 2 * j
            e0 = SSM_HEADS + h0
            escale = jnp.where(low, cols[:, e0:e0 + 1], cols[:, e0 + 1:e0 + 2])
            cdec = jnp.where(low_row, chunk_decay[h0:h0 + 1, :], chunk_decay[h0 + 1:h0 + 2, :])
            ybuf_ref[:, lanes] = ydiag + yoff * escale + xp * dskip_ref[:, lanes]
            state_ref[:, lanes] = s_in * cdec + snew

    yz = ybuf_ref[...] * _silu_tanh(z_ref[...].astype(F32))
    gw = D_SSM // SSM_GROUPS
    for g in range(SSM_GROUPS):
        part = yz[:, g * gw:(g + 1) * gw]
        ms = jnp.mean(part * part, axis=-1, keepdims=True)
        y_ref[:, g * gw:(g + 1) * gw] = (part * lax.rsqrt(ms + EPS)
                                            * ng_ref[:, g * gw:(g + 1) * gw]).astype(BF16)


def _ssd_call(xbc, z, dt, conv_w, conv_b, dtb, alog, dskip, ng, triu, shift):
    bsz, l, _ = xbc.shape
    nc = l // CHUNK
    nseq = SSD_SEQS if bsz % SSD_SEQS == 0 else 1
    chunk = lambda w: pl.BlockSpec((nseq, CHUNK, w), lambda b, c: (b, c, 0))
    full = lambda a: pl.BlockSpec(a.shape, lambda b, c: (0, 0))
    return pl.pallas_call(
        _ssd_kernel,
        grid=(bsz // nseq, nc),
        in_specs=[chunk(CONV_CH), chunk(D_SSM), chunk(LANES), full(conv_w), full(conv_b), full(dtb),
                  full(alog), full(dskip), full(ng), full(triu), full(shift)],
        out_specs=chunk(D_SSM),
        out_shape=jax.ShapeDtypeStruct((bsz, l, D_SSM), BF16),
        scratch_shapes=[pltpu.VMEM((nseq, D_STATE, D_SSM), F32),
                        pltpu.VMEM((nseq, CONV_HALO + CHUNK, CONV_CH), BF16),
                        pltpu.VMEM((nseq, CHUNK, D_SSM), F32)],
        compiler_params=pltpu.CompilerParams(dimension_semantics=("arbitrary", "arbitrary"),
                                             vmem_limit_bytes=VMEM_LIMIT),
    )(xbc, z, dt, conv_w, conv_b, dtb, alog, dskip, ng, triu, shift)


assert WINDOW == ATT_BLOCK


def _rel_bucket_table():
    qi = np.arange(ATT_BLOCK)[:, None]
    c = np.arange(ATT_BLOCK)[None, :]
    dist = np.where(c > qi, qi + ATT_BLOCK - c, qi - c)
    max_exact = REL_BUCKETS // 2
    d = np.maximum(dist, 1).astype(np.float32)
    large = max_exact + (np.log(d / np.float32(max_exact)) / np.float32(math.log(REL_MAX_DIST / max_exact))
                         * np.float32(REL_BUCKETS - max_exact)).astype(np.int32)
    large = np.minimum(large, REL_BUCKETS - 1)
    return np.where(dist < max_exact, dist, large).astype(np.int32)


def _bias_kernel(rb_ref, bucket_ref, o_ref):
    h = pl.program_id(0)
    bucket = bucket_ref[...]
    acc = jnp.zeros(bucket.shape, F32)
    for b in range(REL_BUCKETS):
        acc = jnp.where(bucket == b, rb_ref[b, h], acc)
    from_prev = (lax.broadcasted_iota(jnp.int32, bucket.shape, 1)
                 > lax.broadcasted_iota(jnp.int32, bucket.shape, 0))
    o_ref[1, 0] = acc
    o_ref[0, 0] = jnp.where(from_prev, NEG_INF, acc)


def _bias_call(rel_bias, bucket):
    return pl.pallas_call(
        _bias_kernel,
        grid=(ATT_HEADS,),
        in_specs=[pl.BlockSpec(memory_space=pltpu.SMEM),
                  pl.BlockSpec(bucket.shape, lambda h: (0, 0))],
        out_specs=pl.BlockSpec((2, 1) + bucket.shape, lambda h: (0, h, 0, 0)),
        out_shape=jax.ShapeDtypeStruct((2, ATT_HEADS) + bucket.shape, F32),
        compiler_params=pltpu.CompilerParams(dimension_semantics=("arbitrary",)),
    )(rel_bias, bucket)


ATT_SEQS = 4


def _attn_kernel(sink_ref, q_ref, kp_ref, kc_ref, vp_ref, vc_ref, bias_ref, ng_ref, o_ref, obuf_ref):
    for s in range(q_ref.shape[0]):
        _attn_block(sink_ref, q_ref.at[s], kp_ref.at[s], kc_ref.at[s], vp_ref.at[s], vc_ref.at[s], bias_ref.at[0],
                    ng_ref, o_ref.at[s], obuf_ref.at[s])


def _attn_block(sink_ref, q_ref, kp_ref, kc_ref, vp_ref, vc_ref, bias_ref, ng_ref, o_ref, obuf_ref):
    qi = lax.broadcasted_iota(jnp.int32, (ATT_BLOCK, ATT_BLOCK), 0)
    ci = lax.broadcasted_iota(jnp.int32, (ATT_BLOCK, ATT_BLOCK), 1)
    from_prev = ci > qi
    low = _lane_half_mask((ATT_BLOCK, LANES))

    def band_variants(prev_ref, cur_ref):
        out = []
        for cpair in range(KV_HEADS // 2):
            lanes = slice(cpair * LANES, (cpair + 1) * LANES)
            t = jnp.concatenate([prev_ref[:, lanes], cur_ref[:, lanes]], axis=0).astype(F32)
            out.append((t.astype(BF16), pltpu.roll(t, HALF, 1).astype(BF16)))
        return out

    k_band = band_variants(kp_ref, kc_ref)
    v_band = band_variants(vp_ref, vc_ref)
    nt = (((1,), (1,)), ((), ()))

    for j in range(ATT_HEADS // 2):
        qp = q_ref[:, j * LANES:(j + 1) * LANES]
        out_pair = jnp.zeros((ATT_BLOCK, LANES), F32)
        for half in range(2):
            h = 2 * j + half
            g = h // Q_PER_KV
            swapped = int((g % 2) != half)
            keep = low if half == 0 else jnp.logical_not(low)
            qh = jnp.where(keep, qp, jnp.zeros_like(qp))
            s_band = lax.dot_general(qh, k_band[g // 2][swapped], nt, preferred_element_type=F32)
            s = jnp.where(from_prev, s_band[:, :ATT_BLOCK], s_band[:, ATT_BLOCK:]) + bias_ref[h]
            sink = sink_ref[h]
            m = jnp.maximum(jnp.max(s, axis=-1, keepdims=True), sink)
            p = jnp.exp(s - m)
            denom = jnp.sum(p, axis=-1, keepdims=True) + jnp.exp(sink - m)
            p_band = jnp.concatenate([jnp.where(from_prev, p, 0.0), jnp.where(from_prev, 0.0, p)], axis=1)
            o = jnp.dot(p_band.astype(BF16), v_band[g // 2][swapped], preferred_element_type=F32) / denom
            out_pair = out_pair + jnp.where(keep, o, 0.0)
        obuf_ref[:, j * LANES:(j + 1) * LANES] = out_pair

    att = obuf_ref[...]
    ms = jnp.mean(att * att, axis=-1, keepdims=True)
    o_ref[...] = (att * lax.rsqrt(ms + EPS) * ng_ref[...]).astype(BF16)


def _attn_call(sinks, q, k, v, bias, ng):
    bsz, l, _ = q.shape
    nb = l // ATT_BLOCK
    nseq = ATT_SEQS if bsz % ATT_SEQS == 0 else 1
    cur = lambda w: pl.BlockSpec((nseq, ATT_BLOCK, w), lambda b, i: (b, i, 0))
    prev = lambda w: pl.BlockSpec((nseq, ATT_BLOCK, w), lambda b, i: (b, jnp.maximum(i - 1, 0), 0))
    return pl.pallas_call(
        _attn_kernel,
        grid=(bsz // nseq, nb),
        in_specs=[pl.BlockSpec(memory_space=pltpu.SMEM),
                  cur(D_ATT), prev(D_KV), cur(D_KV), prev(D_KV), cur(D_KV),
                  pl.BlockSpec((1,) + bias.shape[1:], lambda b, i: (jnp.minimum(i, 1), 0, 0, 0)),
                  pl.BlockSpec(ng.shape, lambda b, i: (0, 0))],
        out_specs=cur(D_ATT),
        out_shape=jax.ShapeDtypeStruct((bsz, l, D_ATT), BF16),
        scratch_shapes=[pltpu.VMEM((nseq, ATT_BLOCK, D_ATT), F32)],
        compiler_params=pltpu.CompilerParams(dimension_semantics=("arbitrary", "arbitrary"),
                                             vmem_limit_bytes=VMEM_LIMIT),
    )(sinks, q, k, k, v, v, bias, ng)


def _out_proj_kernel(x_ref, ys_ref, ya_ref, g1_ref, sc_ref, sh_ref, g2_ref, ng_ref, wos_ref, woa_ref,
                     rwh_ref, rwl_ref, sg_ref, su_ref, sd_ref, base_ref, h_ref, lg_ref):
    mix = (jnp.dot(ys_ref[...], wos_ref[...], preferred_element_type=F32)
           + jnp.dot(ya_ref[...], woa_ref[...], preferred_element_type=F32))
    x1 = x_ref[...] + g1_ref[0] * mix
    ms = jnp.mean(x1 * x1, axis=-1, keepdims=True)
    h = x1 * lax.rsqrt(ms + EPS) * ng_ref[...]
    h = h * (1.0 + sc_ref[0]) + sh_ref[0]
    half = h.shape[1] // 2
    h_ref[...] = _pack_bf16_pair(h[:, :half], h[:, half:])
    hi, lo = _split_hi_lo(h)
    logits = (jnp.dot(hi, rwh_ref[...], preferred_element_type=F32)
              + jnp.dot(lo, rwh_ref[...], preferred_element_type=F32)
              + jnp.dot(hi, rwl_ref[...], preferred_element_type=F32))
    lg_ref[...] = logits.T
    u = _silu(jnp.dot(hi, sg_ref[...], preferred_element_type=F32)) * jnp.dot(hi, su_ref[...],
                                                                              preferred_element_type=F32)
    shared = jnp.dot(u.astype(BF16), sd_ref[...], preferred_element_type=F32)
    base_ref[...] = x1 + g2_ref[0] * shared


def _out_proj_call(x2, ys, ya, g1, sc2, sh2, g2, ng, wos, woa, rwh, rwl, sg, su, sd, tiles_per_batch, tm):
    n, d = x2.shape
    row = lambda w: pl.BlockSpec((tm, w), lambda i: (i, 0))
    full = lambda a: pl.BlockSpec(a.shape, lambda i: (0, 0))
    per_batch = pl.BlockSpec((1, 1, d), lambda i: (i // tiles_per_batch, 0, 0))
    return pl.pallas_call(
        _out_proj_kernel,
        grid=(n // tm,),
        in_specs=[row(d), row(D_SSM), row(D_ATT), per_batch, per_batch, per_batch, per_batch, full(ng),
                  full(wos), full(woa), full(rwh), full(rwl), full(sg), full(su), full(sd)],
        out_specs=[row(d), row(d // 2), pl.BlockSpec((LANES, tm), lambda i: (0, i))],
        out_shape=[jax.ShapeDtypeStruct((n, d), F32), jax.ShapeDtypeStruct((n, d // 2), jnp.uint32),
                   jax.ShapeDtypeStruct((LANES, n), F32)],
        compiler_params=pltpu.CompilerParams(dimension_semantics=("arbitrary",),
                                             vmem_limit_bytes=VMEM_LIMIT),
    )(x2, ys, ya, g1, sc2, sh2, g2, ng, wos, woa, rwh, rwl, sg, su, sd)


def _route_kernel(lg_ref, rb_ref, upper_ref, idx_ref, gate_ref, rank_ref, cnt_ref, carry_ref):
    step = pl.program_id(0)

    @pl.when(step == 0)
    def _():
        carry_ref[...] = jnp.zeros_like(carry_ref)

    t = lg_ref.shape[1]
    per_group = N_EXPERTS // ROUTE_GROUPS
    scores = 1.0 / (1.0 + jnp.exp(-lg_ref[0:N_EXPERTS, :]))
    sel = scores + rb_ref[...]
    e_iota = lax.broadcasted_iota(jnp.int32, (N_EXPERTS, t), 0)

    sel3 = sel.reshape(ROUTE_GROUPS, per_group, t)
    w_iota = lax.broadcasted_iota(jnp.int32, sel3.shape, 1)
    m1 = jnp.max(sel3, axis=1, keepdims=True)
    first = jnp.min(jnp.where(sel3 == m1, w_iota, per_group), axis=1, keepdims=True)
    m2 = jnp.max(jnp.where(w_iota == first, NEG_INF, sel3), axis=1, keepdims=True)
    grp = (m1 + m2).reshape(ROUTE_GROUPS, t)

    g_iota = lax.broadcasted_iota(jnp.int32, (ROUTE_GROUPS, t), 0)
    gmask = jnp.zeros((ROUTE_GROUPS, t), jnp.bool_)
    for _ in range(ROUTE_TOPK_GROUPS):
        gm = jnp.max(grp, axis=0, keepdims=True)
        gfirst = jnp.min(jnp.where(grp == gm, g_iota, ROUTE_GROUPS), axis=0, keepdims=True)
        hit = g_iota == gfirst
        gmask = jnp.logical_or(gmask, hit)
        grp = jnp.where(hit, NEG_INF, grp)
    allowed = jnp.broadcast_to(gmask.reshape(ROUTE_GROUPS, 1, t),
                               (ROUTE_GROUPS, per_group, t)).reshape(N_EXPERTS, t)
    masked = jnp.where(allowed, sel, NEG_INF)

    picked = jnp.zeros((N_EXPERTS, t), jnp.bool_)
    idx_rows = []
    w_rows = []
    for _ in range(TOP_K):
        mm = jnp.max(masked, axis=0, keepdims=True)
        efirst = jnp.min(jnp.where(masked == mm, e_iota, N_EXPERTS), axis=0, keepdims=True)
        hit = e_iota == efirst
        idx_rows.append(efirst)
        w_rows.append(jnp.sum(jnp.where(hit, scores, 0.0), axis=0, keepdims=True))
        picked = jnp.logical_or(picked, hit)
        masked = jnp.where(hit, NEG_INF, masked)
    idx = jnp.concatenate(idx_rows, axis=0)
    w = jnp.concatenate(w_rows, axis=0)
    gate_ref[...] = w / jnp.sum(w, axis=0, keepdims=True) * ROUTED_SCALE
    idx_ref[...] = idx

    onehot = jnp.where(picked, 1.0, 0.0)
    before = jnp.dot(onehot.astype(BF16), upper_ref[...], preferred_element_type=F32)
    rank_full = before + carry_ref[:, 0:1]
    rank_rows = [jnp.sum(jnp.where(e_iota == idx_rows[k], rank_full, 0.0), axis=0, keepdims=True)
                 for k in range(TOP_K)]
    rank_ref[...] = jnp.concatenate(rank_rows, axis=0).astype(jnp.int32)
    carry_ref[...] = carry_ref[...] + jnp.sum(onehot, axis=1, keepdims=True)
    cnt_ref[...] = carry_ref[...]


def _route_call(logits_t, router_bias, upper, tile):
    n = logits_t.shape[1]
    tok = lambda r: pl.BlockSpec((r, tile), lambda i: (0, i))
    return pl.pallas_call(
        _route_kernel,
        grid=(n // tile,),
        in_specs=[tok(LANES), pl.BlockSpec((N_EXPERTS, 1), lambda i: (0, 0)),
                  pl.BlockSpec(upper.shape, lambda i: (0, 0))],
        out_specs=[tok(TOP_K), tok(TOP_K), tok(TOP_K), pl.BlockSpec((N_EXPERTS, LANES), lambda i: (0, 0))],
        out_shape=[jax.ShapeDtypeStruct((TOP_K, n), jnp.int32), jax.ShapeDtypeStruct((TOP_K, n), F32),
                   jax.ShapeDtypeStruct((TOP_K, n), jnp.int32),
                   jax.ShapeDtypeStruct((N_EXPERTS, LANES), F32)],
        scratch_shapes=[pltpu.VMEM((N_EXPERTS, LANES), F32)],
        compiler_params=pltpu.CompilerParams(dimension_semantics=("arbitrary",),
                                             vmem_limit_bytes=VMEM_LIMIT),
    )(logits_t, router_bias, upper)


def _dest_kernel(idx_ref, rank_ref, start_ref, dest_ref):
    t = idx_ref.shape[1]
    e_iota = lax.broadcasted_iota(jnp.int32, (N_EXPERTS, t), 0)
    rows = [jnp.sum(jnp.where(e_iota == idx_ref[k:k + 1, :], start_ref[...], 0), axis=0, keepdims=True)
            for k in range(TOP_K)]
    dest_ref[...] = jnp.concatenate(rows, axis=0) + rank_ref[...]


def _dest_call(idx, rank, pad_start, tile):
    n = idx.shape[1]
    tok = pl.BlockSpec((TOP_K, tile), lambda i: (0, i))
    return pl.pallas_call(
        _dest_kernel,
        grid=(n // tile,),
        in_specs=[tok, tok, pl.BlockSpec((N_EXPERTS, 1), lambda i: (0, 0))],
        out_specs=tok,
        out_shape=jax.ShapeDtypeStruct((TOP_K, n), jnp.int32),
        compiler_params=pltpu.CompilerParams(dimension_semantics=("arbitrary",)),
    )(idx, rank, pad_start)


def _scatter_rows_sc(rows, dest_flat, total_rows, chunk):
    n, w = rows.shape
    copies = dest_flat.shape[0] // n
    info = plsc.get_sparse_core_info()
    nc = info.num_cores
    per_worker = n // (nc * info.num_subcores)
    assert per_worker * nc * info.num_subcores == n and per_worker % chunk == 0
    mesh = plsc.VectorSubcoreMesh(core_axis_name="c", subcore_axis_name="s")

    @functools.partial(
        pl.kernel, mesh=mesh,
        out_type=jax.ShapeDtypeStruct((total_rows, w), rows.dtype),
        scratch_types=[pltpu.VMEM((chunk,), jnp.int32), pltpu.VMEM((chunk, w), rows.dtype),
                       pltpu.SemaphoreType.DMA],
    )
    def scatter(rows_hbm, idx_hbm, out_hbm, idx_v, rows_v, sem):
        worker = lax.axis_index("s") * nc + lax.axis_index("c")

        @pl.loop(0, per_worker // chunk)
        def _(j):
            base = worker * per_worker + j * chunk
            pltpu.sync_copy(rows_hbm.at[pl.ds(base, chunk)], rows_v)
            for k in range(copies):
                pltpu.sync_copy(idx_hbm.at[pl.ds(k * n + base, chunk)], idx_v)
                pltpu.async_copy(rows_v, out_hbm.at[idx_v], sem).wait()

    return scatter(rows, dest_flat)


X_RING = 3


def _expert_kernel(be_ref, nused_ref, fresh_ref, valid_ref, seg_ref, next_ref, xs_hbm, wg_hbm, wu_hbm, wd_hbm, y_ref,
                   wgb_ref, wub_ref, wdb_ref, xbuf_ref, xsem, wg_raw, wu_raw, wd_raw, wsem):
    i = pl.program_id(0)
    rows, half = xbuf_ref.shape[1], xbuf_ref.shape[2]
    nused = nused_ref[0]

    def w_copies(expert, slot):
        return [pltpu.make_async_copy(src.at[expert], dst.at[slot], wsem.at[slot])
                for src, dst in ((wg_hbm, wg_raw), (wu_hbm, wu_raw), (wd_hbm, wd_raw))]

    def x_copy(block):
        first = pl.multiple_of(block * rows, rows)
        slot = block % X_RING
        return pltpu.make_async_copy(xs_hbm.at[pl.ds(first, rows)], xbuf_ref.at[slot], xsem.at[slot])

    @pl.when(i == 0)
    def _():
        for b in range(X_RING - 1):
            @pl.when(b < nused)
            def _():
                x_copy(b).start()

    @pl.when(i + (X_RING - 1) < nused)
    def _():
        x_copy(i + (X_RING - 1)).start()

    @pl.when(i == 0)
    def _():
        for c in w_copies(be_ref[0], 0):
            c.start()

    @pl.when(fresh_ref[i] > 0)
    def _():
        slot = seg_ref[i] % 2
        for c in w_copies(be_ref[i], slot):
            c.wait()

        @pl.when(next_ref[i] >= 0)
        def _():
            for c in w_copies(next_ref[i], 1 - slot):
                c.start()

        wgb_ref[...] = wg_raw[slot].astype(BF16)
        wub_ref[...] = wu_raw[slot].astype(BF16)
        wdb_ref[...] = wd_raw[slot].astype(BF16)

    @pl.when(i < nused)
    def _():
        x_copy(i).wait()
        xw = xbuf_ref[i % X_RING]
        row = lax.broadcasted_iota(jnp.int32, xw.shape, 0)
        x_lo, x_hi = _unpack_bf16_pair(jnp.where(row < valid_ref[i], xw, jnp.uint32(0)))
        x_lo = x_lo.astype(BF16)
        x_hi = x_hi.astype(BF16)
        gate = (jnp.dot(x_lo, wgb_ref[:half, :], preferred_element_type=F32)
                + jnp.dot(x_hi, wgb_ref[half:, :], preferred_element_type=F32))
        up = (jnp.dot(x_lo, wub_ref[:half, :], preferred_element_type=F32)
              + jnp.dot(x_hi, wub_ref[half:, :], preferred_element_type=F32))
        u = (_silu(gate) * up).astype(BF16)
        y_lo = jnp.dot(u, wdb_ref[:, :half], preferred_element_type=F32)
        y_hi = jnp.dot(u, wdb_ref[:, half:], preferred_element_type=F32)
        y_ref[...] = _pack_bf16_pair(y_lo, y_hi)

    @pl.when(i >= nused_ref[0])
    def _():
        y_ref[...] = jnp.zeros_like(y_ref)


def _expert_call(block_expert, nused, fresh, valid, seg, next_expert, xs, wg, wu, wd, rows):
    p, w = xs.shape
    d, f = wg.shape[1], wg.shape[2]
    hbm = pl.BlockSpec(memory_space=pl.ANY)
    grid_spec = pltpu.PrefetchScalarGridSpec(
        num_scalar_prefetch=6,
        grid=(p // rows,),
        in_specs=[hbm, hbm, hbm, hbm],
        out_specs=pl.BlockSpec((rows, w), lambda i, *_: (i, 0)),
        scratch_shapes=[pltpu.VMEM((d, f), BF16), pltpu.VMEM((d, f), BF16), pltpu.VMEM((f, d), BF16),
                        pltpu.VMEM((X_RING, rows, w), xs.dtype), pltpu.SemaphoreType.DMA((X_RING,)),
                        pltpu.VMEM((2, d, f), wg.dtype), pltpu.VMEM((2, d, f), wu.dtype),
                        pltpu.VMEM((2, f, d), wd.dtype), pltpu.SemaphoreType.DMA((2,))],
    )
    return pl.pallas_call(
        _expert_kernel,
        grid_spec=grid_spec,
        out_shape=jax.ShapeDtypeStruct((p, w), jnp.uint32),
        compiler_params=pltpu.CompilerParams(dimension_semantics=("arbitrary",),
                                             vmem_limit_bytes=VMEM_LIMIT),
    )(block_expert, nused, fresh, valid, seg, next_expert, xs, wg, wu, wd)


def _gather_rows_sc(table, idx, chunk):
    m = idx.shape[0]
    w = table.shape[1]
    info = plsc.get_sparse_core_info()
    nc = info.num_cores
    per_worker = m // (nc * info.num_subcores)
    assert per_worker * nc * info.num_subcores == m and per_worker % chunk == 0
    mesh = plsc.VectorSubcoreMesh(core_axis_name="c", subcore_axis_name="s")

    @functools.partial(
        pl.kernel, mesh=mesh,
        out_type=jax.ShapeDtypeStruct((m, w), table.dtype),
        scratch_types=[pltpu.VMEM((chunk,), jnp.int32), pltpu.VMEM((chunk, w), table.dtype),
                       pltpu.SemaphoreType.DMA],
    )
    def gather(table_hbm, idx_hbm, out_hbm, idx_v, rows_v, sem):
        worker = lax.axis_index("s") * nc + lax.axis_index("c")

        @pl.loop(0, per_worker // chunk)
        def _(j):
            base = worker * per_worker + j * chunk
            pltpu.sync_copy(idx_hbm.at[pl.ds(base, chunk)], idx_v)
            pltpu.async_copy(table_hbm.at[idx_v], rows_v, sem).wait()
            pltpu.sync_copy(rows_v, out_hbm.at[pl.ds(base, chunk)])

    return gather(table, idx)


def _combine_kernel(yk_ref, gate_ref, base_ref, g2_ref, fg_ref, o_ref):
    t = base_ref.shape[0]
    half = yk_ref.shape[2]
    gates = gate_ref[...]
    r_lo = jnp.zeros((t, half), F32)
    r_hi = jnp.zeros((t, half), F32)
    for k in range(TOP_K):
        y_lo, y_hi = _unpack_bf16_pair(yk_ref[k])
        r_lo = r_lo + gates[:, k:k + 1] * y_lo
        r_hi = r_hi + gates[:, k:k + 1] * y_hi
    g2 = g2_ref[0]
    x_lo = base_ref[:, :half] + g2[:, :half] * r_lo
    x_hi = base_ref[:, half:] + g2[:, half:] * r_hi
    ms = (jnp.sum(x_lo * x_lo, axis=-1, keepdims=True)
          + jnp.sum(x_hi * x_hi, axis=-1, keepdims=True)) * (1.0 / (2 * half))
    inv = lax.rsqrt(ms + EPS)
    o_ref[:, :half] = x_lo * inv * fg_ref[:, :half]
    o_ref[:, half:] = x_hi * inv * fg_ref[:, half:]


def _combine_call(yk, gates_t, base, g2, fg, tiles_per_batch, tile, tile0):
    n, d = base.shape
    row = lambda w: pl.BlockSpec((tile, w), lambda i: (i + tile0, 0))
    return pl.pallas_call(
        _combine_kernel,
        grid=(yk.shape[1] // tile,),
        in_specs=[pl.BlockSpec((TOP_K, tile, yk.shape[2]), lambda i: (0, i, 0)),
                  row(TOP_K), row(d),
                  pl.BlockSpec((1, 1, d), lambda i: ((i + tile0) // tiles_per_batch, 0, 0)),
                  pl.BlockSpec((1, d), lambda i: (0, 0))],
        out_specs=row(d),
        out_shape=jax.ShapeDtypeStruct((n, d), F32),
        input_output_aliases={2: 0},
        compiler_params=pltpu.CompilerParams(dimension_semantics=("arbitrary",),
                                             vmem_limit_bytes=VMEM_LIMIT),
    )(yk, gates_t, base, g2, fg)


def _pad_cols(a, width):
    return jnp.pad(a, ((0, 0), (0, width - a.shape[1])))


def _layer(x, mod, norm1_g, norm2_g, w_in, conv_w, conv_b, dt_bias, a_log, d_skip, ssm_norm_g,
           att_norm_g, sinks, rel_bias, w_out, router_w, router_bias, exp_w_gate, exp_w_up, exp_w_down,
           sh_w_gate, sh_w_up, sh_w_down, final_g):
    bsz, l, d = x.shape
    n = bsz * l
    tm = min(ROW_TILE, l)

    sh1, sc1, g1, sh2, sc2, g2 = [m[:, None, :] for m in jnp.split(mod, 6, axis=-1)]

    i1 = D_SSM
    i2 = i1 + CONV_CH
    i3 = i2 + SSM_HEADS
    i4 = i3 + D_ATT
    i5 = i4 + D_KV
    wz, wx, wdt, wq, wk, wv = jnp.split(w_in, [i1, i2, i3, i4, i5], axis=-1)
    wdt = _pad_cols(wdt, LANES)
    q_scale = ATT_HEAD_DIM ** -0.5
    assert math.frexp(q_scale)[0] == 0.5
    x2 = x.reshape(n, d)
    z, xbc, dt, q, k, v = _in_proj_call(x2, sc1, sh1, norm1_g[None, :], wz.astype(BF16), wx.astype(BF16),
                                        wdt.astype(BF16), (wq * q_scale).astype(BF16), wk.astype(BF16),
                                        wv.astype(BF16), l // tm, tm)

    triu = jnp.asarray(np.triu(np.ones((CHUNK, CHUNK), np.float32))).astype(BF16)
    shift = jnp.asarray(_conv_shift_matrix()).astype(BF16)
    y_ssm = _ssd_call(xbc.reshape(bsz, l, CONV_CH), z.reshape(bsz, l, D_SSM), dt.reshape(bsz, l, LANES),
                      conv_w, conv_b[None, :], dt_bias[:, None], a_log[:, None],
                      jnp.repeat(d_skip, SSM_HEAD_DIM)[None, :], ssm_norm_g[None, :], triu, shift)

    bias = _bias_call(rel_bias, jnp.asarray(_rel_bucket_table()))
    y_att = _attn_call(sinks, q.reshape(bsz, l, D_ATT), k.reshape(bsz, l, D_KV), v.reshape(bsz, l, D_KV), bias,
                       att_norm_g[None, :])

    rw = _pad_cols(router_w, LANES)
    rwh = rw.astype(BF16)
    rwl = (rw - rwh.astype(F32)).astype(BF16)
    base, h2, logits_t = _out_proj_call(
        x2, y_ssm.reshape(n, D_SSM), y_att.reshape(n, D_ATT), g1, sc2, sh2, g2, norm2_g[None, :],
        w_out[:D_SSM].astype(BF16), w_out[D_SSM:].astype(BF16), rwh, rwl,
        sh_w_gate.astype(BF16), sh_w_up.astype(BF16), sh_w_down.astype(BF16), l // tm, tm)

    rt = min(ROUTE_TILE, n)
    upper = jnp.asarray(np.triu(np.ones((rt, rt), np.float32), 1)).astype(BF16)
    idx, gates, rank, counts = _route_call(logits_t, router_bias[:, None], upper, rt)

    nblocks = (n * TOP_K + N_EXPERTS * (EXPERT_ROWS - 1) + EXPERT_ROWS - 1) // EXPERT_ROWS
    cnt = counts[:, 0].astype(jnp.int32)
    padded = (cnt + EXPERT_ROWS - 1) // EXPERT_ROWS * EXPERT_ROWS
    pad_end = jnp.cumsum(padded)
    pad_start = pad_end - padded
    dest = _dest_call(idx, rank, pad_start[:, None], rt)
    block_first_row = jnp.arange(nblocks, dtype=jnp.int32) * EXPERT_ROWS
    block_expert = jnp.minimum(jnp.sum((pad_end[None, :] <= block_first_row[:, None]).astype(jnp.int32), axis=1),
                               N_EXPERTS - 1)
    fresh = jnp.concatenate([jnp.ones((1,), jnp.int32),
                             (block_expert[1:] != block_expert[:-1]).astype(jnp.int32)])
    nused = (pad_end[-1:] // EXPERT_ROWS).astype(jnp.int32)
    seg = jnp.cumsum(fresh) - 1
    block_id = jnp.arange(nblocks, dtype=jnp.int32)
    later_start = jnp.where((fresh[None, :] > 0) & (block_id[None, :] > block_id[:, None]), block_id[None, :], nblocks)
    next_start = jnp.min(later_start, axis=1)
    next_expert = jnp.where(next_start < nblocks,
                            jnp.sum(jnp.where(block_id[None, :] == next_start[:, None], block_expert[None, :], 0), axis=1),
                            -1).astype(jnp.int32)
    seg_row = block_first_row - jnp.sum(jnp.where(pad_end[None, :] <= block_first_row[:, None], padded[None, :], 0),
                                        axis=1)
    block_cnt = jnp.sum(jnp.where(block_expert[:, None] == jnp.arange(N_EXPERTS, dtype=jnp.int32)[None, :],
                                  cnt[None, :], 0), axis=1)
    valid = jnp.clip(block_cnt - seg_row, 0, EXPERT_ROWS).astype(jnp.int32)

    dest_flat = dest.reshape(-1)
    xs = _scatter_rows_sc(h2, dest_flat, nblocks * EXPERT_ROWS, SC_CHUNK)
    ys = _expert_call(block_expert, nused, fresh, valid, seg.astype(jnp.int32), next_expert, xs, exp_w_gate,
                      exp_w_up, exp_w_down, EXPERT_ROWS)
    ctile = min(COMBINE_TILE, l)
    groups = COMBINE_GROUPS if bsz % COMBINE_GROUPS == 0 else 1
    ng = n // groups
    gates_t = gates.T
    out = base
    for g in range(groups):
        idx_g = dest[:, g * ng:(g + 1) * ng].reshape(-1)
        yk = _gather_rows_sc(ys, idx_g, SC_CHUNK).reshape(TOP_K, ng, ys.shape[1])
        out = _combine_call(yk, gates_t, out, g2, final_g[None, :], l // ctile, ctile, g * ng // ctile)
    return out.reshape(bsz, l, d)


def kernel(x, c, mod_w, mod_b, norm1_g, norm2_g, w_in, conv_w, conv_b, dt_bias, a_log, d_skip, ssm_norm_g,
           att_norm_g, sinks, rel_bias, w_out, router_w, router_bias, exp_w_gate, exp_w_up, exp_w_down,
           sh_w_gate, sh_w_up, sh_w_down, final_g):
    assert mod_w.shape[0] == 1, "single-layer block"
    bsz = x.shape[0]
    c_pad = jnp.pad(c, ((0, SUBLANES - bsz % SUBLANES if bsz % SUBLANES else 0), (0, 0)))
    mod = _mod_call(c_pad, mod_w[0], mod_b[0][None, :])[:bsz]
    return _layer(x, mod, norm1_g[0], norm2_g[0], w_in[0], conv_w[0], conv_b[0], dt_bias[0], a_log[0], d_skip[0],
                  ssm_norm_g[0], att_norm_g[0], sinks[0], rel_bias, w_out[0], router_w[0], router_bias[0],
                  exp_w_gate[0], exp_w_up[0], exp_w_down[0], sh_w_gate[0], sh_w_up[0], sh_w_down[0], final_g)
```

```python
import functools
import math

import numpy as np
import jax
import jax.numpy as jnp
from jax import lax
from jax.experimental import pallas as pl
from jax.experimental.pallas import tpu as pltpu
from jax.experimental.pallas import tpu_sc as plsc

F32 = jnp.float32
BF16 = jnp.bfloat16

D_MODEL = 1024
SSM_HEAD_DIM = 64
D_SSM = D_MODEL
SSM_HEADS = D_SSM // SSM_HEAD_DIM
SSM_GROUPS = 4
D_STATE = 128
CONV_K = 4
CONV_CH = D_SSM + 2 * SSM_GROUPS * D_STATE
CHUNK = 128
ATT_HEAD_DIM = 64
D_ATT = D_MODEL
ATT_HEADS = D_ATT // ATT_HEAD_DIM
KV_HEADS = ATT_HEADS // 4
Q_PER_KV = ATT_HEADS // KV_HEADS
D_KV = KV_HEADS * ATT_HEAD_DIM
WINDOW = 128
ATT_BLOCK = 128
REL_BUCKETS = 32
REL_MAX_DIST = 128
N_EXPERTS = 64
TOP_K = 8
EXPERT_DIM = D_MODEL // 4
SHARED_DIM = D_MODEL // 4
ROUTE_GROUPS = 8
ROUTE_TOPK_GROUPS = 4
ROUTED_SCALE = 2.5
EPS = 1e-6

LANES = 128
SUBLANES = 8
HALF = LANES // 2

ROW_TILE = 512
ROUTE_TILE = 2048
RANK_SUB = 512
COMBINE_TILE = 256
SC_CHUNK = 128
EXPERT_ROWS = 512
COMBINE_GROUPS = 2
VMEM_LIMIT = 48 * 1024 * 1024

NEG_INF = float("-inf")


def _silu(v):
    return v * (1.0 / (1.0 + jnp.exp(-v)))


def _softplus(v):
    return jnp.maximum(v, 0.0) + jnp.log(1.0 + jnp.exp(-jnp.abs(v)))


def _bdot(a, b):
    return jnp.dot(a.astype(BF16), b.astype(BF16), preferred_element_type=F32)


def _split_hi_lo(v):
    hi = v.astype(BF16)
    lo = (v - hi.astype(F32)).astype(BF16)
    return hi, lo


def _pack_bf16_pair(a, b):
    w = pltpu.pack_elementwise([a, b], packed_dtype=BF16)
    return w if w.dtype == jnp.uint32 else lax.bitcast_convert_type(w, jnp.uint32)


def _unpack_bf16_pair(w):
    a = pltpu.unpack_elementwise(w, index=0, packed_dtype=BF16, unpacked_dtype=F32)
    b = pltpu.unpack_elementwise(w, index=1, packed_dtype=BF16, unpacked_dtype=F32)
    return a, b


def _lane_half_mask(shape):
    return lax.broadcasted_iota(jnp.int32, shape, len(shape) - 1) < HALF


def _mod_kernel(c_ref, w_ref, b_ref, o_ref):
    a = _silu(c_ref[...])
    o_ref[...] = jnp.dot(a, w_ref[...], precision=lax.Precision.HIGHEST,
                         preferred_element_type=F32) + b_ref[...]


def _mod_call(c_pad, mod_w, mod_b):
    rows, d = c_pad.shape
    cols = mod_w.shape[1]
    return pl.pallas_call(
        _mod_kernel,
        grid=(cols // d,),
        in_specs=[pl.BlockSpec((rows, d), lambda j: (0, 0)),
                  pl.BlockSpec((d, d), lambda j: (0, j)),
                  pl.BlockSpec((1, d), lambda j: (0, j))],
        out_specs=pl.BlockSpec((rows, d), lambda j: (0, j)),
        out_shape=jax.ShapeDtypeStruct((rows, cols), F32),
        compiler_params=pltpu.CompilerParams(dimension_semantics=("arbitrary",),
                                             vmem_limit_bytes=VMEM_LIMIT),
    )(c_pad, mod_w, mod_b)


def _in_proj_kernel(x_ref, sc_ref, sh_ref, g_ref, wz_ref, wx_ref, wdt_ref, wq_ref, wk_ref, wv_ref,
                    z_ref, xbc_ref, dt_ref, q_ref, k_ref, v_ref):
    xf = x_ref[...]
    ms = jnp.mean(xf * xf, axis=-1, keepdims=True)
    h = xf * lax.rsqrt(ms + EPS) * g_ref[...]
    h = h * (1.0 + sc_ref[0]) + sh_ref[0]
    hb = h.astype(BF16)
    z_ref[...] = jnp.dot(hb, wz_ref[...], preferred_element_type=F32).astype(BF16)
    xbc_ref[...] = jnp.dot(hb, wx_ref[...], preferred_element_type=F32).astype(BF16)
    dt_ref[...] = jnp.dot(hb, wdt_ref[...], preferred_element_type=F32)
    q_ref[...] = jnp.dot(hb, wq_ref[...], preferred_element_type=F32).astype(BF16)
    k_ref[...] = jnp.dot(hb, wk_ref[...], preferred_element_type=F32).astype(BF16)
    v_ref[...] = jnp.dot(hb, wv_ref[...], preferred_element_type=F32).astype(BF16)


def _in_proj_call(x2, sc1, sh1, g1n, wz, wx, wdt, wq, wk, wv, tiles_per_batch, tm):
    n, d = x2.shape
    row = lambda w: pl.BlockSpec((tm, w), lambda i: (i, 0))
    full = lambda a: pl.BlockSpec(a.shape, lambda i: (0, 0))
    per_batch = pl.BlockSpec((1, 1, d), lambda i: (i // tiles_per_batch, 0, 0))
    outs = [(wz.shape[1], BF16), (wx.shape[1], BF16), (wdt.shape[1], F32),
            (wq.shape[1], BF16), (wk.shape[1], BF16), (wv.shape[1], BF16)]
    return pl.pallas_call(
        _in_proj_kernel,
        grid=(n // tm,),
        in_specs=[row(d), per_batch, per_batch, full(g1n), full(wz), full(wx), full(wdt), full(wq),
                  full(wk), full(wv)],
        out_specs=[row(w) for w, _ in outs],
        out_shape=[jax.ShapeDtypeStruct((n, w), dt) for w, dt in outs],
        compiler_params=pltpu.CompilerParams(dimension_semantics=("arbitrary",),
                                             vmem_limit_bytes=VMEM_LIMIT),
    )(x2, sc1, sh1, g1n, wz, wx, wdt, wq, wk, wv)


SSD_SEQS = 4
CONV_HALO = 16


def _conv_shift_matrix():
    s = np.zeros((CONV_K * CHUNK, CONV_HALO + CHUNK), np.float32)
    for k in range(CONV_K):
        t = np.arange(CHUNK)
        s[k * CHUNK + t, CONV_HALO + t - (CONV_K - 1) + k] = 1.0
    return s


def _silu_tanh(v):
    hv = 0.5 * v
    return hv + hv * jnp.tanh(hv)


def _ssd_kernel(xbc_ref, z_ref, dt_ref, cw_ref, cb_ref, dtb_ref, alog_ref, dskip_ref, ng_ref, triu_ref, shift_ref,
                y_ref, state_ref, ucat_ref, ybuf_ref):
    nseq = xbc_ref.shape[0]

    @pl.when(pl.program_id(1) == 0)
    def _():
        state_ref[...] = jnp.zeros_like(state_ref)
        ucat_ref[:, 0:CONV_HALO, :] = jnp.zeros((nseq, CONV_HALO, CONV_CH), BF16)

    for q in range(nseq):
        _ssd_chunk(xbc_ref.at[q], z_ref.at[q], dt_ref.at[q], cw_ref, cb_ref, dtb_ref, alog_ref, dskip_ref, ng_ref,
                   triu_ref, shift_ref, y_ref.at[q], state_ref.at[q], ucat_ref.at[q], ybuf_ref.at[q])


def _ssd_chunk(xbc_ref, z_ref, dt_ref, cw_ref, cb_ref, dtb_ref, alog_ref, dskip_ref, ng_ref, triu_ref, shift_ref,
               y_ref, state_ref, ucat_ref, ybuf_ref):
    ucat_ref[CONV_HALO:, :] = xbc_ref[...]
    shifted = jnp.dot(shift_ref[...], ucat_ref[...], preferred_element_type=F32)
    ucat_ref[0:CONV_HALO, :] = ucat_ref[CHUNK:CHUNK + CONV_HALO, :]
    acc = cb_ref[...] + cw_ref[0:1, :] * shifted[0:CHUNK]
    for kk in range(1, CONV_K):
        acc = acc + cw_ref[kk:kk + 1, :] * shifted[kk * CHUNK:(kk + 1) * CHUNK]
    act = _silu_tanh(acc)
    xs = act[:, :D_SSM]
    gn = SSM_GROUPS * D_STATE

    dt_t = _softplus(dt_ref[...].T[0:SSM_HEADS, :] + dtb_ref[...])
    a_t = dt_t * (-jnp.exp(alog_ref[...]))
    a_hi = a_t.astype(BF16)
    a_mid = (a_t - a_hi.astype(F32)).astype(BF16)
    a_lo = (a_t - a_hi.astype(F32) - a_mid.astype(F32)).astype(BF16)
    triu = triu_ref[...]
    cs_t = (jnp.dot(a_hi, triu, preferred_element_type=F32) + jnp.dot(a_mid, triu, preferred_element_type=F32)
            + jnp.dot(a_lo, triu, preferred_element_type=F32))
    cs_end = cs_t[:, CHUNK - 1:CHUNK]
    r_t = cs_t - jnp.log(dt_t)
    w_t = jnp.exp(cs_end - cs_t) * dt_t
    chunk_decay = jnp.exp(cs_end)
    cols = jnp.concatenate([cs_t, jnp.exp(cs_t), jnp.zeros((LANES - 2 * SSM_HEADS, CHUNK), F32)], axis=0).T

    li = lax.broadcasted_iota(jnp.int32, (CHUNK, CHUNK), 0)
    si = lax.broadcasted_iota(jnp.int32, (CHUNK, CHUNK), 1)
    causal = li >= si
    low = _lane_half_mask((CHUNK, LANES))
    low_row = _lane_half_mask((1, LANES))

    heads_per_group = SSM_HEADS // SSM_GROUPS
    for g in range(SSM_GROUPS):
        b_g = act[:, D_SSM + g * D_STATE:D_SSM + (g + 1) * D_STATE]
        c_g = act[:, D_SSM + gn + g * D_STATE:D_SSM + gn + (g + 1) * D_STATE]
        b_gb = b_g.astype(BF16)
        c_gb = c_g.astype(BF16)
        cb = lax.dot_general(c_gb, b_gb, (((1,), (1,)), ((), ())), preferred_element_type=F32)
        b_t = b_g.T
        for jp in range(heads_per_group // 2):
            j = g * (heads_per_group // 2) + jp
            lanes = slice(j * LANES, (j + 1) * LANES)
            xp = xs[:, lanes]
            xpb = xp.astype(BF16)
            ydiag = jnp.zeros((CHUNK, LANES), F32)
            snew = jnp.zeros((D_STATE, LANES), F32)
            for half in range(2):
                h = 2 * j + half
                diff = cols[:, h:h + 1] - r_t[h:h + 1, :]
                m = (cb * jnp.exp(jnp.where(causal, diff, NEG_INF))).astype(BF16)
                keep = low if half == 0 else jnp.logical_not(low)
                xh = jnp.where(keep, xpb, jnp.zeros_like(xpb))
                ydiag = ydiag + jnp.dot(m, xh, preferred_element_type=F32)
                snew = snew + jnp.dot((b_t * w_t[h:h + 1, :]).astype(BF16), xh, preferred_element_type=F32)
            s_in = state_ref[:, lanes]
            yoff = jnp.dot(c_gb, s_in.astype(BF16), preferred_element_type=F32)
            h0 = 2 * j
            e0 = SSM_HEADS + h0
            escale = jnp.where(low, cols[:, e0:e0 + 1], cols[:, e0 + 1:e0 + 2])
            cdec = jnp.where(low_row, chunk_decay[h0:h0 + 1, :], chunk_decay[h0 + 1:h0 + 2, :])
            ybuf_ref[:, lanes] = ydiag + yoff * escale + xp * dskip_ref[:, lanes]
            state_ref[:, lanes] = s_in * cdec + snew

    yz = ybuf_ref[...] * _silu_tanh(z_ref[...].astype(F32))
    gw = D_SSM // SSM_GROUPS
    for g in range(SSM_GROUPS):
        part = yz[:, g * gw:(g + 1) * gw]
        ms = jnp.mean(part * part, axis=-1, keepdims=True)
        y_ref[:, g * gw:(g + 1) * gw] = (part * lax.rsqrt(ms + EPS)
                                            * ng_ref[:, g * gw:(g + 1) * gw]).astype(BF16)


def _ssd_call(xbc, z, dt, conv_w, conv_b, dtb, alog, dskip, ng, triu, shift):
    bsz, l, _ = xbc.shape
    nc = l // CHUNK
    nseq = SSD_SEQS if bsz % SSD_SEQS == 0 else 1
    chunk = lambda w: pl.BlockSpec((nseq, CHUNK, w), lambda b, c: (b, c, 0))
    full = lambda a: pl.BlockSpec(a.shape, lambda b, c: (0, 0))
    return pl.pallas_call(
        _ssd_kernel,
        grid=(bsz // nseq, nc),
        in_specs=[chunk(CONV_CH), chunk(D_SSM), chunk(LANES), full(conv_w), full(conv_b), full(dtb),
                  full(alog), full(dskip), full(ng), full(triu), full(shift)],
        out_specs=chunk(D_SSM),
        out_shape=jax.ShapeDtypeStruct((bsz, l, D_SSM), BF16),
        scratch_shapes=[pltpu.VMEM((nseq, D_STATE, D_SSM), F32),
                        pltpu.VMEM((nseq, CONV_HALO + CHUNK, CONV_CH), BF16),
                        pltpu.VMEM((nseq, CHUNK, D_SSM), F32)],
        compiler_params=pltpu.CompilerParams(dimension_semantics=("arbitrary", "arbitrary"),
                                             vmem_limit_bytes=VMEM_LIMIT),
    )(xbc, z, dt, conv_w, conv_b, dtb, alog, dskip, ng, triu, shift)


assert WINDOW == ATT_BLOCK


def _rel_bucket_table():
    qi = np.arange(ATT_BLOCK)[:, None]
    c = np.arange(ATT_BLOCK)[None, :]
    dist = np.where(c > qi, qi + ATT_BLOCK - c, qi - c)
    max_exact = REL_BUCKETS // 2
    d = np.maximum(dist, 1).astype(np.float32)
    large = max_exact + (np.log(d / np.float32(max_exact)) / np.float32(math.log(REL_MAX_DIST / max_exact))
                         * np.float32(REL_BUCKETS - max_exact)).astype(np.int32)
    large = np.minimum(large, REL_BUCKETS - 1)
    return np.where(dist < max_exact, dist, large).astype(np.int32)


def _bias_kernel(rb_ref, bucket_ref, o_ref):
    h = pl.program_id(0)
    bucket = bucket_ref[...]
    acc = jnp.zeros(bucket.shape, F32)
    for b in range(REL_BUCKETS):
        acc = jnp.where(bucket == b, rb_ref[b, h], acc)
    from_prev = (lax.broadcasted_iota(jnp.int32, bucket.shape, 1)
                 > lax.broadcasted_iota(jnp.int32, bucket.shape, 0))
    o_ref[1, 0] = acc
    o_ref[0, 0] = jnp.where(from_prev, NEG_INF, acc)


def _bias_call(rel_bias, bucket):
    return pl.pallas_call(
        _bias_kernel,
        grid=(ATT_HEADS,),
        in_specs=[pl.BlockSpec(memory_space=pltpu.SMEM),
                  pl.BlockSpec(bucket.shape, lambda h: (0, 0))],
        out_specs=pl.BlockSpec((2, 1) + bucket.shape, lambda h: (0, h, 0, 0)),
        out_shape=jax.ShapeDtypeStruct((2, ATT_HEADS) + bucket.shape, F32),
        compiler_params=pltpu.CompilerParams(dimension_semantics=("arbitrary",)),
    )(rel_bias, bucket)


ATT_SEQS = 4


def _attn_kernel(sink_ref, q_ref, kp_ref, kc_ref, vp_ref, vc_ref, bias_ref, ng_ref, o_ref, obuf_ref):
    for s in range(q_ref.shape[0]):
        _attn_block(sink_ref, q_ref.at[s], kp_ref.at[s], kc_ref.at[s], vp_ref.at[s], vc_ref.at[s], bias_ref.at[0],
                    ng_ref, o_ref.at[s], obuf_ref.at[s])


def _attn_block(sink_ref, q_ref, kp_ref, kc_ref, vp_ref, vc_ref, bias_ref, ng_ref, o_ref, obuf_ref):
    qi = lax.broadcasted_iota(jnp.int32, (ATT_BLOCK, ATT_BLOCK), 0)
    ci = lax.broadcasted_iota(jnp.int32, (ATT_BLOCK, ATT_BLOCK), 1)
    from_prev = ci > qi
    low = _lane_half_mask((ATT_BLOCK, LANES))

    def band_variants(prev_ref, cur_ref):
        out = []
        for cpair in range(KV_HEADS // 2):
            lanes = slice(cpair * LANES, (cpair + 1) * LANES)
            t = jnp.concatenate([prev_ref[:, lanes], cur_ref[:, lanes]], axis=0).astype(F32)
            out.append((t.astype(BF16), pltpu.roll(t, HALF, 1).astype(BF16)))
        return out

    k_band = band_variants(kp_ref, kc_ref)
    v_band = band_variants(vp_ref, vc_ref)
    nt = (((1,), (1,)), ((), ()))

    for j in range(ATT_HEADS // 2):
        qp = q_ref[:, j * LANES:(j + 1) * LANES]
        out_pair = jnp.zeros((ATT_BLOCK, LANES), F32)
        for half in range(2):
            h = 2 * j + half
            g = h // Q_PER_KV
            swapped = int((g % 2) != half)
            keep = low if half == 0 else jnp.logical_not(low)
            qh = jnp.where(keep, qp, jnp.zeros_like(qp))
            s_band = lax.dot_general(qh, k_band[g // 2][swapped], nt, preferred_element_type=F32)
            s = jnp.where(from_prev, s_band[:, :ATT_BLOCK], s_band[:, ATT_BLOCK:]) + bias_ref[h]
            sink = sink_ref[h]
            m = jnp.maximum(jnp.max(s, axis=-1, keepdims=True), sink)
            p = jnp.exp(s - m)
            denom = jnp.sum(p, axis=-1, keepdims=True) + jnp.exp(sink - m)
            p_band = jnp.concatenate([jnp.where(from_prev, p, 0.0), jnp.where(from_prev, 0.0, p)], axis=1)
            o = jnp.dot(p_band.astype(BF16), v_band[g // 2][swapped], preferred_element_type=F32) / denom
            out_pair = out_pair + jnp.where(keep, o, 0.0)
        obuf_ref[:, j * LANES:(j + 1) * LANES] = out_pair

    att = obuf_ref[...]
    ms = jnp.mean(att * att, axis=-1, keepdims=True)
    o_ref[...] = (att * lax.rsqrt(ms + EPS) * ng_ref[...]).astype(BF16)


def _attn_call(sinks, q, k, v, bias, ng):
    bsz, l, _ = q.shape
    nb = l // ATT_BLOCK
    nseq = ATT_SEQS if bsz % ATT_SEQS == 0 else 1
    cur = lambda w: pl.BlockSpec((nseq, ATT_BLOCK, w), lambda b, i: (b, i, 0))
    prev = lambda w: pl.BlockSpec((nseq, ATT_BLOCK, w), lambda b, i: (b, jnp.maximum(i - 1, 0), 0))
    return pl.pallas_call(
        _attn_kernel,
        grid=(bsz // nseq, nb),
        in_specs=[pl.BlockSpec(memory_space=pltpu.SMEM),
                  cur(D_ATT), prev(D_KV), cur(D_KV), prev(D_KV), cur(D_KV),
                  pl.BlockSpec((1,) + bias.shape[1:], lambda b, i: (jnp.minimum(i, 1), 0, 0, 0)),
                  pl.BlockSpec(ng.shape, lambda b, i: (0, 0))],
        out_specs=cur(D_ATT),
        out_shape=jax.ShapeDtypeStruct((bsz, l, D_ATT), BF16),
        scratch_shapes=[pltpu.VMEM((nseq, ATT_BLOCK, D_ATT), F32)],
        compiler_params=pltpu.CompilerParams(dimension_semantics=("arbitrary", "arbitrary"),
                                             vmem_limit_bytes=VMEM_LIMIT),
    )(sinks, q, k, k, v, v, bias, ng)


def _out_proj_kernel(x_ref, ys_ref, ya_ref, g1_ref, sc_ref, sh_ref, g2_ref, ng_ref, wos_ref, woa_ref,
                     rwh_ref, rwl_ref, sg_ref, su_ref, sd_ref, base_ref, h_ref, lg_ref):
    mix = (jnp.dot(ys_ref[...], wos_ref[...], preferred_element_type=F32)
           + jnp.dot(ya_ref[...], woa_ref[...], preferred_element_type=F32))
    x1 = x_ref[...] + g1_ref[0] * mix
    ms = jnp.mean(x1 * x1, axis=-1, keepdims=True)
    h = x1 * lax.rsqrt(ms + EPS) * ng_ref[...]
    h = h * (1.0 + sc_ref[0]) + sh_ref[0]
    half = h.shape[1] // 2
    h_ref[...] = _pack_bf16_pair(h[:, :half], h[:, half:])
    hi, lo = _split_hi_lo(h)
    logits = (jnp.dot(hi, rwh_ref[...], preferred_element_type=F32)
              + jnp.dot(lo, rwh_ref[...], preferred_element_type=F32)
              + jnp.dot(hi, rwl_ref[...], preferred_element_type=F32))
    lg_ref[...] = logits.T
    u = _silu(jnp.dot(hi, sg_ref[...], preferred_element_type=F32)) * jnp.dot(hi, su_ref[...],
                                                                              preferred_element_type=F32)
    shared = jnp.dot(u.astype(BF16), sd_ref[...], preferred_element_type=F32)
    base_ref[...] = x1 + g2_ref[0] * shared


def _out_proj_call(x2, ys, ya, g1, sc2, sh2, g2, ng, wos, woa, rwh, rwl, sg, su, sd, tiles_per_batch, tm):
    n, d = x2.shape
    row = lambda w: pl.BlockSpec((tm, w), lambda i: (i, 0))
    full = lambda a: pl.BlockSpec(a.shape, lambda i: (0, 0))
    per_batch = pl.BlockSpec((1, 1, d), lambda i: (i // tiles_per_batch, 0, 0))
    return pl.pallas_call(
        _out_proj_kernel,
        grid=(n // tm,),
        in_specs=[row(d), row(D_SSM), row(D_ATT), per_batch, per_batch, per_batch, per_batch, full(ng),
                  full(wos), full(woa), full(rwh), full(rwl), full(sg), full(su), full(sd)],
        out_specs=[row(d), row(d // 2), pl.BlockSpec((LANES, tm), lambda i: (0, i))],
        out_shape=[jax.ShapeDtypeStruct((n, d), F32), jax.ShapeDtypeStruct((n, d // 2), jnp.uint32),
                   jax.ShapeDtypeStruct((LANES, n), F32)],
        compiler_params=pltpu.CompilerParams(dimension_semantics=("arbitrary",),
                                             vmem_limit_bytes=VMEM_LIMIT),
    )(x2, ys, ya, g1, sc2, sh2, g2, ng, wos, woa, rwh, rwl, sg, su, sd)


def _route_kernel(lg_ref, rb_ref, upper_ref, idx_ref, gate_ref, rank_ref, cnt_ref, carry_ref):
    step = pl.program_id(0)

    @pl.when(step == 0)
    def _():
        carry_ref[...] = jnp.zeros_like(carry_ref)

    t = lg_ref.shape[1]
    per_group = N_EXPERTS // ROUTE_GROUPS
    scores = 1.0 / (1.0 + jnp.exp(-lg_ref[0:N_EXPERTS, :]))
    sel = scores + rb_ref[...]
    e_iota = lax.broadcasted_iota(jnp.int32, (N_EXPERTS, t), 0)

    sel3 = sel.reshape(ROUTE_GROUPS, per_group, t)
    w_iota = lax.broadcasted_iota(jnp.int32, sel3.shape, 1)
    m1 = jnp.max(sel3, axis=1, keepdims=True)
    first = jnp.min(jnp.where(sel3 == m1, w_iota, per_group), axis=1, keepdims=True)
    m2 = jnp.max(jnp.where(w_iota == first, NEG_INF, sel3), axis=1, keepdims=True)
    grp = (m1 + m2).reshape(ROUTE_GROUPS, t)

    g_iota = lax.broadcasted_iota(jnp.int32, (ROUTE_GROUPS, t), 0)
    gmask = jnp.zeros((ROUTE_GROUPS, t), jnp.bool_)
    for _ in range(ROUTE_TOPK_GROUPS):
        gm = jnp.max(grp, axis=0, keepdims=True)
        gfirst = jnp.min(jnp.where(grp == gm, g_iota, ROUTE_GROUPS), axis=0, keepdims=True)
        hit = g_iota == gfirst
        gmask = jnp.logical_or(gmask, hit)
        grp = jnp.where(hit, NEG_INF, grp)
    allowed = jnp.broadcast_to(gmask.reshape(ROUTE_GROUPS, 1, t),
                               (ROUTE_GROUPS, per_group, t)).reshape(N_EXPERTS, t)
    masked = jnp.where(allowed, sel, NEG_INF)

    picked = jnp.zeros((N_EXPERTS, t), jnp.bool_)
    idx_rows = []
    w_rows = []
    for _ in range(TOP_K):
        mm = jnp.max(masked, axis=0, keepdims=True)
        efirst = jnp.min(jnp.where(masked == mm, e_iota, N_EXPERTS), axis=0, keepdims=True)
        hit = e_iota == efirst
        idx_rows.append(efirst)
        w_rows.append(jnp.sum(jnp.where(hit, scores, 0.0), axis=0, keepdims=True))
        picked = jnp.logical_or(picked, hit)
        masked = jnp.where(hit, NEG_INF, masked)
    idx = jnp.concatenate(idx_rows, axis=0)
    w = jnp.concatenate(w_rows, axis=0)
    gate_ref[...] = w / jnp.sum(w, axis=0, keepdims=True) * ROUTED_SCALE
    idx_ref[...] = idx

    onehot = jnp.where(picked, 1.0, 0.0)
    sub = upper_ref.shape[0]
    carry = carry_ref[:, 0:1]
    parts = []
    for s0 in range(0, t, sub):
        oh = onehot[:, s0:s0 + sub]
        parts.append(jnp.dot(oh.astype(BF16), upper_ref[...], preferred_element_type=F32) + carry)
        carry = carry + jnp.sum(oh, axis=1, keepdims=True)
    rank_full = jnp.concatenate(parts, axis=1)
    rank_rows = [jnp.sum(jnp.where(e_iota == idx_rows[k], rank_full, 0.0), axis=0, keepdims=True)
                 for k in range(TOP_K)]
    rank_ref[...] = jnp.concatenate(rank_rows, axis=0).astype(jnp.int32)
    carry_ref[...] = jnp.broadcast_to(carry, carry_ref.shape)
    cnt_ref[...] = carry_ref[...]


def _route_call(logits_t, router_bias, upper, tile):
    n = logits_t.shape[1]
    tok = lambda r: pl.BlockSpec((r, tile), lambda i: (0, i))
    return pl.pallas_call(
        _route_kernel,
        grid=(n // tile,),
        in_specs=[tok(LANES), pl.BlockSpec((N_EXPERTS, 1), lambda i: (0, 0)),
                  pl.BlockSpec(upper.shape, lambda i: (0, 0))],
        out_specs=[tok(TOP_K), tok(TOP_K), tok(TOP_K), pl.BlockSpec((N_EXPERTS, LANES), lambda i: (0, 0))],
        out_shape=[jax.ShapeDtypeStruct((TOP_K, n), jnp.int32), jax.ShapeDtypeStruct((TOP_K, n), F32),
                   jax.ShapeDtypeStruct((TOP_K, n), jnp.int32),
                   jax.ShapeDtypeStruct((N_EXPERTS, LANES), F32)],
        scratch_shapes=[pltpu.VMEM((N_EXPERTS, LANES), F32)],
        compiler_params=pltpu.CompilerParams(dimension_semantics=("arbitrary",),
                                             vmem_limit_bytes=VMEM_LIMIT),
    )(logits_t, router_bias, upper)


def _dest_kernel(idx_ref, rank_ref, start_ref, dest_ref):
    t = idx_ref.shape[1]
    e_iota = lax.broadcasted_iota(jnp.int32, (N_EXPERTS, t), 0)
    rows = [jnp.sum(jnp.where(e_iota == idx_ref[k:k + 1, :], start_ref[...], 0), axis=0, keepdims=True)
            for k in range(TOP_K)]
    dest_ref[...] = jnp.concatenate(rows, axis=0) + rank_ref[...]


def _dest_call(idx, rank, pad_start, tile):
    n = idx.shape[1]
    tok = pl.BlockSpec((TOP_K, tile), lambda i: (0, i))
    return pl.pallas_call(
        _dest_kernel,
        grid=(n // tile,),
        in_specs=[tok, tok, pl.BlockSpec((N_EXPERTS, 1), lambda i: (0, 0))],
        out_specs=tok,
        out_shape=jax.ShapeDtypeStruct((TOP_K, n), jnp.int32),
        compiler_params=pltpu.CompilerParams(dimension_semantics=("arbitrary",)),
    )(idx, rank, pad_start)


def _scatter_rows_sc(rows, dest_flat, total_rows, chunk):
    n, w = rows.shape
    copies = dest_flat.shape[0] // n
    info = plsc.get_sparse_core_info()
    nc = info.num_cores
    per_worker = n // (nc * info.num_subcores)
    assert per_worker * nc * info.num_subcores == n and per_worker % chunk == 0
    mesh = plsc.VectorSubcoreMesh(core_axis_name="c", subcore_axis_name="s")

    @functools.partial(
        pl.kernel, mesh=mesh,
        out_type=jax.ShapeDtypeStruct((total_rows, w), rows.dtype),
        scratch_types=[pltpu.VMEM((chunk,), jnp.int32), pltpu.VMEM((chunk, w), rows.dtype),
                       pltpu.SemaphoreType.DMA],
    )
    def scatter(rows_hbm, idx_hbm, out_hbm, idx_v, rows_v, sem):
        worker = lax.axis_index("s") * nc + lax.axis_index("c")

        @pl.loop(0, per_worker // chunk)
        def _(j):
            base = worker * per_worker + j * chunk
            pltpu.sync_copy(rows_hbm.at[pl.ds(base, chunk)], rows_v)
            for k in range(copies):
                pltpu.sync_copy(idx_hbm.at[pl.ds(k * n + base, chunk)], idx_v)
                pltpu.async_copy(rows_v, out_hbm.at[idx_v], sem).wait()

    return scatter(rows, dest_flat)


X_RING = 3


def _expert_kernel(be_ref, nused_ref, fresh_ref, valid_ref, seg_ref, next_ref, xs_hbm, wg_hbm, wu_hbm, wd_hbm, y_ref,
                   wgb_ref, wub_ref, wdb_ref, xbuf_ref, xsem, wg_raw, wu_raw, wd_raw, wsem):
    i = pl.program_id(0)
    rows, half = xbuf_ref.shape[1], xbuf_ref.shape[2]
    nused = nused_ref[0]

    def w_copies(expert, slot):
        return [pltpu.make_async_copy(src.at[expert], dst.at[slot], wsem.at[slot])
                for src, dst in ((wg_hbm, wg_raw), (wu_hbm, wu_raw), (wd_hbm, wd_raw))]

    def x_copy(block):
        first = pl.multiple_of(block * rows, rows)
        slot = block % X_RING
        return pltpu.make_async_copy(xs_hbm.at[pl.ds(first, rows)], xbuf_ref.at[slot], xsem.at[slot])

    @pl.when(i == 0)
    def _():
        for b in range(X_RING - 1):
            @pl.when(b < nused)
            def _():
                x_copy(b).start()

    @pl.when(i + (X_RING - 1) < nused)
    def _():
        x_copy(i + (X_RING - 1)).start()

    @pl.when(i == 0)
    def _():
        for c in w_copies(be_ref[0], 0):
            c.start()

    @pl.when(fresh_ref[i] > 0)
    def _():
        slot = seg_ref[i] % 2
        for c in w_copies(be_ref[i], slot):
            c.wait()

        @pl.when(next_ref[i] >= 0)
        def _():
            for c in w_copies(next_ref[i], 1 - slot):
                c.start()

        wgb_ref[...] = wg_raw[slot].astype(BF16)
        wub_ref[...] = wu_raw[slot].astype(BF16)
        wdb_ref[...] = wd_raw[slot].astype(BF16)

    @pl.when(i < nused)
    def _():
        x_copy(i).wait()
        xw = xbuf_ref[i % X_RING]
        row = lax.broadcasted_iota(jnp.int32, xw.shape, 0)
        x_lo, x_hi = _unpack_bf16_pair(jnp.where(row < valid_ref[i], xw, jnp.uint32(0)))
        x_lo = x_lo.astype(BF16)
        x_hi = x_hi.astype(BF16)
        gate = (jnp.dot(x_lo, wgb_ref[:half, :], preferred_element_type=F32)
                + jnp.dot(x_hi, wgb_ref[half:, :], preferred_element_type=F32))
        up = (jnp.dot(x_lo, wub_ref[:half, :], preferred_element_type=F32)
              + jnp.dot(x_hi, wub_ref[half:, :], preferred_element_type=F32))
        u = (_silu(gate) * up).astype(BF16)
        y_lo = jnp.dot(u, wdb_ref[:, :half], preferred_element_type=F32)
        y_hi = jnp.dot(u, wdb_ref[:, half:], preferred_element_type=F32)
        y_ref[...] = _pack_bf16_pair(y_lo, y_hi)

    @pl.when(i >= nused_ref[0])
    def _():
        y_ref[...] = jnp.zeros_like(y_ref)


def _expert_call(block_expert, nused, fresh, valid, seg, next_expert, xs, wg, wu, wd, rows):
    p, w = xs.shape
    d, f = wg.shape[1], wg.shape[2]
    hbm = pl.BlockSpec(memory_space=pl.ANY)
    grid_spec = pltpu.PrefetchScalarGridSpec(
        num_scalar_prefetch=6,
        grid=(p // rows,),
        in_specs=[hbm, hbm, hbm, hbm],
        out_specs=pl.BlockSpec((rows, w), lambda i, *_: (i, 0)),
        scratch_shapes=[pltpu.VMEM((d, f), BF16), pltpu.VMEM((d, f), BF16), pltpu.VMEM((f, d), BF16),
                        pltpu.VMEM((X_RING, rows, w), xs.dtype), pltpu.SemaphoreType.DMA((X_RING,)),
                        pltpu.VMEM((2, d, f), wg.dtype), pltpu.VMEM((2, d, f), wu.dtype),
                        pltpu.VMEM((2, f, d), wd.dtype), pltpu.SemaphoreType.DMA((2,))],
    )
    return pl.pallas_call(
        _expert_kernel,
        grid_spec=grid_spec,
        out_shape=jax.ShapeDtypeStruct((p, w), jnp.uint32),
        compiler_params=pltpu.CompilerParams(dimension_semantics=("arbitrary",),
                                             vmem_limit_bytes=VMEM_LIMIT),
    )(block_expert, nused, fresh, valid, seg, next_expert, xs, wg, wu, wd)


def _gather_rows_sc(table, idx, chunk):
    m = idx.shape[0]
    w = table.shape[1]
    info = plsc.get_sparse_core_info()
    nc = info.num_cores
    per_worker = m // (nc * info.num_subcores)
    assert per_worker * nc * info.num_subcores == m and per_worker % chunk == 0
    mesh = plsc.VectorSubcoreMesh(core_axis_name="c", subcore_axis_name="s")

    @functools.partial(
        pl.kernel, mesh=mesh,
        out_type=jax.ShapeDtypeStruct((m, w), table.dtype),
        scratch_types=[pltpu.VMEM((chunk,), jnp.int32), pltpu.VMEM((chunk, w), table.dtype),
                       pltpu.SemaphoreType.DMA],
    )
    def gather(table_hbm, idx_hbm, out_hbm, idx_v, rows_v, sem):
        worker = lax.axis_index("s") * nc + lax.axis_index("c")

        @pl.loop(0, per_worker // chunk)
        def _(j):
            base = worker * per_worker + j * chunk
            pltpu.sync_copy(idx_hbm.at[pl.ds(base, chunk)], idx_v)
            pltpu.async_copy(table_hbm.at[idx_v], rows_v, sem).wait()
            pltpu.sync_copy(rows_v, out_hbm.at[pl.ds(base, chunk)])

    return gather(table, idx)


def _combine_kernel(yk_ref, gate_ref, base_ref, g2_ref, fg_ref, o_ref):
    t = base_ref.shape[0]
    half = yk_ref.shape[2]
    gates = gate_ref[...]
    r_lo = jnp.zeros((t, half), F32)
    r_hi = jnp.zeros((t, half), F32)
    for k in range(TOP_K):
        y_lo, y_hi = _unpack_bf16_pair(yk_ref[k])
        r_lo = r_lo + gates[:, k:k + 1] * y_lo
        r_hi = r_hi + gates[:, k:k + 1] * y_hi
    g2 = g2_ref[0]
    x_lo = base_ref[:, :half] + g2[:, :half] * r_lo
    x_hi = base_ref[:, half:] + g2[:, half:] * r_hi
    ms = (jnp.sum(x_lo * x_lo, axis=-1, keepdims=True)
          + jnp.sum(x_hi * x_hi, axis=-1, keepdims=True)) * (1.0 / (2 * half))
    inv = lax.rsqrt(ms + EPS)
    o_ref[:, :half] = x_lo * inv * fg_ref[:, :half]
    o_ref[:, half:] = x_hi * inv * fg_ref[:, half:]


def _combine_call(yk, gates_t, base, g2, fg, tiles_per_batch, tile, tile0):
    n, d = base.shape
    row = lambda w: pl.BlockSpec((tile, w), lambda i: (i + tile0, 0))
    return pl.pallas_call(
        _combine_kernel,
        grid=(yk.shape[1] // tile,),
        in_specs=[pl.BlockSpec((TOP_K, tile, yk.shape[2]), lambda i: (0, i, 0)),
                  row(TOP_K), row(d),
                  pl.BlockSpec((1, 1, d), lambda i: ((i + tile0) // tiles_per_batch, 0, 0)),
                  pl.BlockSpec((1, d), lambda i: (0, 0))],
        out_specs=row(d),
        out_shape=jax.ShapeDtypeStruct((n, d), F32),
        input_output_aliases={2: 0},
        compiler_params=pltpu.CompilerParams(dimension_semantics=("arbitrary",),
                                             vmem_limit_bytes=VMEM_LIMIT),
    )(yk, gates_t, base, g2, fg)


def _pad_cols(a, width):
    return jnp.pad(a, ((0, 0), (0, width - a.shape[1])))


def _layer(x, mod, norm1_g, norm2_g, w_in, conv_w, conv_b, dt_bias, a_log, d_skip, ssm_norm_g,
           att_norm_g, sinks, rel_bias, w_out, router_w, router_bias, exp_w_gate, exp_w_up, exp_w_down,
           sh_w_gate, sh_w_up, sh_w_down, final_g):
    bsz, l, d = x.shape
    n = bsz * l
    tm = min(ROW_TILE, l)

    sh1, sc1, g1, sh2, sc2, g2 = [m[:, None, :] for m in jnp.split(mod, 6, axis=-1)]

    i1 = D_SSM
    i2 = i1 + CONV_CH
    i3 = i2 + SSM_HEADS
    i4 = i3 + D_ATT
    i5 = i4 + D_KV
    wz, wx, wdt, wq, wk, wv = jnp.split(w_in, [i1, i2, i3, i4, i5], axis=-1)
    wdt = _pad_cols(wdt, LANES)
    q_scale = ATT_HEAD_DIM ** -0.5
    assert math.frexp(q_scale)[0] == 0.5
    x2 = x.reshape(n, d)
    z, xbc, dt, q, k, v = _in_proj_call(x2, sc1, sh1, norm1_g[None, :], wz.astype(BF16), wx.astype(BF16),
                                        wdt.astype(BF16), (wq * q_scale).astype(BF16), wk.astype(BF16),
                                        wv.astype(BF16), l // tm, tm)

    triu = jnp.asarray(np.triu(np.ones((CHUNK, CHUNK), np.float32))).astype(BF16)
    shift = jnp.asarray(_conv_shift_matrix()).astype(BF16)
    y_ssm = _ssd_call(xbc.reshape(bsz, l, CONV_CH), z.reshape(bsz, l, D_SSM), dt.reshape(bsz, l, LANES),
                      conv_w, conv_b[None, :], dt_bias[:, None], a_log[:, None],
                      jnp.repeat(d_skip, SSM_HEAD_DIM)[None, :], ssm_norm_g[None, :], triu, shift)

    bias = _bias_call(rel_bias, jnp.asarray(_rel_bucket_table()))
    y_att = _attn_call(sinks, q.reshape(bsz, l, D_ATT), k.reshape(bsz, l, D_KV), v.reshape(bsz, l, D_KV), bias,
                       att_norm_g[None, :])

    rw = _pad_cols(router_w, LANES)
    rwh = rw.astype(BF16)
    rwl = (rw - rwh.astype(F32)).astype(BF16)
    base, h2, logits_t = _out_proj_call(
        x2, y_ssm.reshape(n, D_SSM), y_att.reshape(n, D_ATT), g1, sc2, sh2, g2, norm2_g[None, :],
        w_out[:D_SSM].astype(BF16), w_out[D_SSM:].astype(BF16), rwh, rwl,
        sh_w_gate.astype(BF16), sh_w_up.astype(BF16), sh_w_down.astype(BF16), l // tm, tm)

    rt = min(ROUTE_TILE, n)
    rs = min(RANK_SUB, rt)
    upper = jnp.asarray(np.triu(np.ones((rs, rs), np.float32), 1)).astype(BF16)
    idx, gates, rank, counts = _route_call(logits_t, router_bias[:, None], upper, rt)

    nblocks = (n * TOP_K + N_EXPERTS * (EXPERT_ROWS - 1) + EXPERT_ROWS - 1) // EXPERT_ROWS
    cnt = counts[:, 0].astype(jnp.int32)
    padded = (cnt + EXPERT_ROWS - 1) // EXPERT_ROWS * EXPERT_ROWS
    pad_end = jnp.cumsum(padded)
    pad_start = pad_end - padded
    dest = _dest_call(idx, rank, pad_start[:, None], rt)
    block_first_row = jnp.arange(nblocks, dtype=jnp.int32) * EXPERT_ROWS
    block_expert = jnp.minimum(jnp.sum((pad_end[None, :] <= block_first_row[:, None]).astype(jnp.int32), axis=1),
                               N_EXPERTS - 1)
    fresh = jnp.concatenate([jnp.ones((1,), jnp.int32),
                             (block_expert[1:] != block_expert[:-1]).astype(jnp.int32)])
    nused = (pad_end[-1:] // EXPERT_ROWS).astype(jnp.int32)
    seg = jnp.cumsum(fresh) - 1
    block_id = jnp.arange(nblocks, dtype=jnp.int32)
    later_start = jnp.where((fresh[None, :] > 0) & (block_id[None, :] > block_id[:, None]), block_id[None, :], nblocks)
    next_start = jnp.min(later_start, axis=1)
    next_expert = jnp.where(next_start < nblocks,
                            jnp.sum(jnp.where(block_id[None, :] == next_start[:, None], block_expert[None, :], 0), axis=1),
                            -1).astype(jnp.int32)
    seg_row = block_first_row - jnp.sum(jnp.where(pad_end[None, :] <= block_first_row[:, None], padded[None, :], 0),
                                        axis=1)
    block_cnt = jnp.sum(jnp.where(block_expert[:, None] == jnp.arange(N_EXPERTS, dtype=jnp.int32)[None, :],
                                  cnt[None, :], 0), axis=1)
    valid = jnp.clip(block_cnt - seg_row, 0, EXPERT_ROWS).astype(jnp.int32)

    dest_flat = dest.reshape(-1)
    xs = _scatter_rows_sc(h2, dest_flat, nblocks * EXPERT_ROWS, SC_CHUNK)
    ys = _expert_call(block_expert, nused, fresh, valid, seg.astype(jnp.int32), next_expert, xs, exp_w_gate,
                      exp_w_up, exp_w_down, EXPERT_ROWS)
    ctile = min(COMBINE_TILE, l)
    groups = COMBINE_GROUPS if bsz % COMBINE_GROUPS == 0 else 1
    ng = n // groups
    gates_t = gates.T
    out = base
    for g in range(groups):
        idx_g = dest[:, g * ng:(g + 1) * ng].reshape(-1)
        yk = _gather_rows_sc(ys, idx_g, SC_CHUNK).reshape(TOP_K, ng, ys.shape[1])
        out = _combine_call(yk, gates_t, out, g2, final_g[None, :], l // ctile, ctile, g * ng // ctile)
    return out.reshape(bsz, l, d)


def kernel(x, c, mod_w, mod_b, norm1_g, norm2_g, w_in, conv_w, conv_b, dt_bias, a_log, d_skip, ssm_norm_g,
           att_norm_g, sinks, rel_bias, w_out, router_w, router_bias, exp_w_gate, exp_w_up, exp_w_down,
           sh_w_gate, sh_w_up, sh_w_down, final_g):
    assert mod_w.shape[0] == 1, "single-layer block"
    bsz = x.shape[0]
    c_pad = jnp.pad(c, ((0, SUBLANES - bsz % SUBLANES if bsz % SUBLANES else 0), (0, 0)))
    mod = _mod_call(c_pad, mod_w[0], mod_b[0][None, :])[:bsz]
    return _layer(x, mod, norm1_g[0], norm2_g[0], w_in[0], conv_w[0], conv_b[0], dt_bias[0], a_log[0], d_skip[0],
                  ssm_norm_g[0], att_norm_g[0], sinks[0], rel_bias, w_out[0], router_w[0], router_bias[0],
                  exp_w_gate[0], exp_w_up[0], exp_w_down[0], sh_w_gate[0], sh_w_up[0], sh_w_down[0], final_g)
```

```python
import functools
import math

import numpy as np
import jax
import jax.numpy as jnp
from jax import lax
from jax.experimental import pallas as pl
from jax.experimental.pallas import tpu as pltpu
from jax.experimental.pallas import tpu_sc as plsc

F32 = jnp.float32
BF16 = jnp.bfloat16

D_MODEL = 1024
SSM_HEAD_DIM = 64
D_SSM = D_MODEL
SSM_HEADS = D_SSM // SSM_HEAD_DIM
SSM_GROUPS = 4
D_STATE = 128
CONV_K = 4
CONV_CH = D_SSM + 2 * SSM_GROUPS * D_STATE
CHUNK = 128
ATT_HEAD_DIM = 64
D_ATT = D_MODEL
ATT_HEADS = D_ATT // ATT_HEAD_DIM
KV_HEADS = ATT_HEADS // 4
Q_PER_KV = ATT_HEADS // KV_HEADS
D_KV = KV_HEADS * ATT_HEAD_DIM
WINDOW = 128
ATT_BLOCK = 128
REL_BUCKETS = 32
REL_MAX_DIST = 128
N_EXPERTS = 64
TOP_K = 8
EXPERT_DIM = D_MODEL // 4
SHARED_DIM = D_MODEL // 4
ROUTE_GROUPS = 8
ROUTE_TOPK_GROUPS = 4
ROUTED_SCALE = 2.5
EPS = 1e-6

LANES = 128
SUBLANES = 8
HALF = LANES // 2

ROW_TILE = 512
ROUTE_TILE = 2048
RANK_SUB = 512
COMBINE_TILE = 256
SC_CHUNK = 128
EXPERT_ROWS = 512
COMBINE_GROUPS = 2
VMEM_LIMIT = 48 * 1024 * 1024

NEG_INF = float("-inf")


def _silu(v):
    return v * (1.0 / (1.0 + jnp.exp(-v)))


def _softplus(v):
    return jnp.maximum(v, 0.0) + jnp.log(1.0 + jnp.exp(-jnp.abs(v)))


def _bdot(a, b):
    return jnp.dot(a.astype(BF16), b.astype(BF16), preferred_element_type=F32)


def _split_hi_lo(v):
    hi = v.astype(BF16)
    lo = (v - hi.astype(F32)).astype(BF16)
    return hi, lo


def _pack_bf16_pair(a, b):
    w = pltpu.pack_elementwise([a, b], packed_dtype=BF16)
    return w if w.dtype == jnp.uint32 else lax.bitcast_convert_type(w, jnp.uint32)


def _unpack_bf16_pair(w):
    a = pltpu.unpack_elementwise(w, index=0, packed_dtype=BF16, unpacked_dtype=F32)
    b = pltpu.unpack_elementwise(w, index=1, packed_dtype=BF16, unpacked_dtype=F32)
    return a, b


def _lane_half_mask(shape):
    return lax.broadcasted_iota(jnp.int32, shape, len(shape) - 1) < HALF


def _mod_kernel(c_ref, w_ref, b_ref, o_ref):
    a = _silu(c_ref[...])
    o_ref[...] = jnp.dot(a, w_ref[...], precision=lax.Precision.HIGHEST,
                         preferred_element_type=F32) + b_ref[...]


def _mod_call(c_pad, mod_w, mod_b):
    rows, d = c_pad.shape
    cols = mod_w.shape[1]
    return pl.pallas_call(
        _mod_kernel,
        grid=(cols // d,),
        in_specs=[pl.BlockSpec((rows, d), lambda j: (0, 0)),
                  pl.BlockSpec((d, d), lambda j: (0, j)),
                  pl.BlockSpec((1, d), lambda j: (0, j))],
        out_specs=pl.BlockSpec((rows, d), lambda j: (0, j)),
        out_shape=jax.ShapeDtypeStruct((rows, cols), F32),
        compiler_params=pltpu.CompilerParams(dimension_semantics=("arbitrary",),
                                             vmem_limit_bytes=VMEM_LIMIT),
    )(c_pad, mod_w, mod_b)


def _in_proj_kernel(x_ref, sc_ref, sh_ref, g_ref, wz_ref, wx_ref, wdt_ref, wq_ref, wk_ref, wv_ref,
                    z_ref, xbc_ref, dt_ref, q_ref, k_ref, v_ref):
    xf = x_ref[...]
    ms = jnp.mean(xf * xf, axis=-1, keepdims=True)
    h = xf * lax.rsqrt(ms + EPS) * g_ref[...]
    h = h * (1.0 + sc_ref[0]) + sh_ref[0]
    hb = h.astype(BF16)
    z_ref[...] = jnp.dot(hb, wz_ref[...], preferred_element_type=F32).astype(BF16)
    xbc_ref[...] = jnp.dot(hb, wx_ref[...], preferred_element_type=F32).astype(BF16)
    dt_ref[...] = jnp.dot(hb, wdt_ref[...], preferred_element_type=F32)
    q_ref[...] = jnp.dot(hb, wq_ref[...], preferred_element_type=F32).astype(BF16)
    k_ref[...] = jnp.dot(hb, wk_ref[...], preferred_element_type=F32).astype(BF16)
    v_ref[...] = jnp.dot(hb, wv_ref[...], preferred_element_type=F32).astype(BF16)


def _in_proj_call(x2, sc1, sh1, g1n, wz, wx, wdt, wq, wk, wv, tiles_per_batch, tm):
    n, d = x2.shape
    row = lambda w: pl.BlockSpec((tm, w), lambda i: (i, 0))
    full = lambda a: pl.BlockSpec(a.shape, lambda i: (0, 0))
    per_batch = pl.BlockSpec((1, 1, d), lambda i: (i // tiles_per_batch, 0, 0))
    outs = [(wz.shape[1], BF16), (wx.shape[1], BF16), (wdt.shape[1], F32),
            (wq.shape[1], BF16), (wk.shape[1], BF16), (wv.shape[1], BF16)]
    return pl.pallas_call(
        _in_proj_kernel,
        grid=(n // tm,),
        in_specs=[row(d), per_batch, per_batch, full(g1n), full(wz), full(wx), full(wdt), full(wq),
                  full(wk), full(wv)],
        out_specs=[row(w) for w, _ in outs],
        out_shape=[jax.ShapeDtypeStruct((n, w), dt) for w, dt in outs],
        compiler_params=pltpu.CompilerParams(dimension_semantics=("arbitrary",),
                                             vmem_limit_bytes=VMEM_LIMIT),
    )(x2, sc1, sh1, g1n, wz, wx, wdt, wq, wk, wv)


SSD_SEQS = 4
CONV_HALO = 16


def _conv_shift_matrix():
    s = np.zeros((CONV_K * CHUNK, CONV_HALO + CHUNK), np.float32)
    for k in range(CONV_K):
        t = np.arange(CHUNK)
        s[k * CHUNK + t, CONV_HALO + t - (CONV_K - 1) + k] = 1.0
    return s


def _silu_tanh(v):
    hv = 0.5 * v
    return hv + hv * jnp.tanh(hv)


def _ssd_kernel(xbc_ref, z_ref, dt_ref, cw_ref, cb_ref, dtb_ref, alog_ref, dskip_ref, ng_ref, triu_ref, shift_ref,
                y_ref, state_ref, ucat_ref, ybuf_ref):
    nseq = xbc_ref.shape[0]

    @pl.when(pl.program_id(1) == 0)
    def _():
        state_ref[...] = jnp.zeros_like(state_ref)
        ucat_ref[:, 0:CONV_HALO, :] = jnp.zeros((nseq, CONV_HALO, CONV_CH), BF16)

    for q in range(nseq):
        _ssd_chunk(xbc_ref.at[q], z_ref.at[q], dt_ref.at[q], cw_ref, cb_ref, dtb_ref, alog_ref, dskip_ref, ng_ref,
                   triu_ref, shift_ref, y_ref.at[q], state_ref.at[q], ucat_ref.at[q], ybuf_ref.at[q])


def _ssd_chunk(xbc_ref, z_ref, dt_ref, cw_ref, cb_ref, dtb_ref, alog_ref, dskip_ref, ng_ref, triu_ref, shift_ref,
               y_ref, state_ref, ucat_ref, ybuf_ref):
    ucat_ref[CONV_HALO:, :] = xbc_ref[...]
    shifted = jnp.dot(shift_ref[...], ucat_ref[...], preferred_element_type=F32)
    ucat_ref[0:CONV_HALO, :] = ucat_ref[CHUNK:CHUNK + CONV_HALO, :]
    acc = cb_ref[...] + cw_ref[0:1, :] * shifted[0:CHUNK]
    for kk in range(1, CONV_K):
        acc = acc + cw_ref[kk:kk + 1, :] * shifted[kk * CHUNK:(kk + 1) * CHUNK]
    act = _silu_tanh(acc)
    xs = act[:, :D_SSM]
    gn = SSM_GROUPS * D_STATE

    dt_t = _softplus(dt_ref[...].T[0:SSM_HEADS, :] + dtb_ref[...])
    a_t = dt_t * (-jnp.exp(alog_ref[...]))
    a_hi = a_t.astype(BF16)
    a_mid = (a_t - a_hi.astype(F32)).astype(BF16)
    a_lo = (a_t - a_hi.astype(F32) - a_mid.astype(F32)).astype(BF16)
    triu = triu_ref[...]
    cs_t = (jnp.dot(a_hi, triu, preferred_element_type=F32) + jnp.dot(a_mid, triu, preferred_element_type=F32)
            + jnp.dot(a_lo, triu, preferred_element_type=F32))
    cs_end = cs_t[:, CHUNK - 1:CHUNK]
    r_t = cs_t - jnp.log(dt_t)
    w_t = jnp.exp(cs_end - cs_t) * dt_t
    chunk_decay = jnp.exp(cs_end)
    cols = jnp.concatenate([cs_t, jnp.exp(cs_t), jnp.zeros((LANES - 2 * SSM_HEADS, CHUNK), F32)], axis=0).T

    li = lax.broadcasted_iota(jnp.int32, (CHUNK, CHUNK), 0)
    si = lax.broadcasted_iota(jnp.int32, (CHUNK, CHUNK), 1)
    causal = li >= si
    low = _lane_half_mask((CHUNK, LANES))
    low_row = _lane_half_mask((1, LANES))

    heads_per_group = SSM_HEADS // SSM_GROUPS
    for g in range(SSM_GROUPS):
        b_g = act[:, D_SSM + g * D_STATE:D_SSM + (g + 1) * D_STATE]
        c_g = act[:, D_SSM + gn + g * D_STATE:D_SSM + gn + (g + 1) * D_STATE]
        b_gb = b_g.astype(BF16)
        c_gb = c_g.astype(BF16)
        cb = lax.dot_general(c_gb, b_gb, (((1,), (1,)), ((), ())), preferred_element_type=F32)
        b_t = b_g.T
        for jp in range(heads_per_group // 2):
            j = g * (heads_per_group // 2) + jp
            lanes = slice(j * LANES, (j + 1) * LANES)
            xp = xs[:, lanes]
            xpb = xp.astype(BF16)
            ydiag = jnp.zeros((CHUNK, LANES), F32)
            snew = jnp.zeros((D_STATE, LANES), F32)
            for half in range(2):
                h = 2 * j + half
                diff = cols[:, h:h + 1] - r_t[h:h + 1, :]
                m = (cb * jnp.exp(jnp.where(causal, diff, NEG_INF))).astype(BF16)
                keep = low if half == 0 else jnp.logical_not(low)
                xh = jnp.where(keep, xpb, jnp.zeros_like(xpb))
                ydiag = ydiag + jnp.dot(m, xh, preferred_element_type=F32)
                snew = snew + jnp.dot((b_t * w_t[h:h + 1, :]).astype(BF16), xh, preferred_element_type=F32)
            s_in = state_ref[:, lanes]
            yoff = jnp.dot(c_gb, s_in.astype(BF16), preferred_element_type=F32)
            h0 = 2 * j
            e0 = SSM_HEADS + h0
            escale = jnp.where(low, cols[:, e0:e0 + 1], cols[:, e0 + 1:e0 + 2])
            cdec = jnp.where(low_row, chunk_decay[h0:h0 + 1, :], chunk_decay[h0 + 1:h0 + 2, :])
            ybuf_ref[:, lanes] = ydiag + yoff * escale + xp * dskip_ref[:, lanes]
            state_ref[:, lanes] = s_in * cdec + snew

    yz = ybuf_ref[...] * _silu_tanh(z_ref[...].astype(F32))
    gw = D_SSM // SSM_GROUPS
    for g in range(SSM_GROUPS):
        part = yz[:, g * gw:(g + 1) * gw]
        ms = jnp.mean(part * part, axis=-1, keepdims=True)
        y_ref[:, g * gw:(g + 1) * gw] = (part * lax.rsqrt(ms + EPS)
                                            * ng_ref[:, g * gw:(g + 1) * gw]).astype(BF16)


def _ssd_call(xbc, z, dt, conv_w, conv_b, dtb, alog, dskip, ng, triu, shift):
    bsz, l, _ = xbc.shape
    nc = l // CHUNK
    nseq = SSD_SEQS if bsz % SSD_SEQS == 0 else 1
    chunk = lambda w: pl.BlockSpec((nseq, CHUNK, w), lambda b, c: (b, c, 0))
    full = lambda a: pl.BlockSpec(a.shape, lambda b, c: (0, 0))
    return pl.pallas_call(
        _ssd_kernel,
        grid=(bsz // nseq, nc),
        in_specs=[chunk(CONV_CH), chunk(D_SSM), chunk(LANES), full(conv_w), full(conv_b), full(dtb),
                  full(alog), full(dskip), full(ng), full(triu), full(shift)],
        out_specs=chunk(D_SSM),
        out_shape=jax.ShapeDtypeStruct((bsz, l, D_SSM), BF16),
        scratch_shapes=[pltpu.VMEM((nseq, D_STATE, D_SSM), F32),
                        pltpu.VMEM((nseq, CONV_HALO + CHUNK, CONV_CH), BF16),
                        pltpu.VMEM((nseq, CHUNK, D_SSM), F32)],
        compiler_params=pltpu.CompilerParams(dimension_semantics=("arbitrary", "arbitrary"),
                                             vmem_limit_bytes=VMEM_LIMIT),
    )(xbc, z, dt, conv_w, conv_b, dtb, alog, dskip, ng, triu, shift)


assert WINDOW == ATT_BLOCK


def _rel_bucket_table():
    qi = np.arange(ATT_BLOCK)[:, None]
    c = np.arange(ATT_BLOCK)[None, :]
    dist = np.where(c > qi, qi + ATT_BLOCK - c, qi - c)
    max_exact = REL_BUCKETS // 2
    d = np.maximum(dist, 1).astype(np.float32)
    large = max_exact + (np.log(d / np.float32(max_exact)) / np.float32(math.log(REL_MAX_DIST / max_exact))
                         * np.float32(REL_BUCKETS - max_exact)).astype(np.int32)
    large = np.minimum(large, REL_BUCKETS - 1)
    return np.where(dist < max_exact, dist, large).astype(np.int32)


def _bias_kernel(rb_ref, bucket_ref, o_ref):
    h = pl.program_id(0)
    bucket = bucket_ref[...]
    acc = jnp.zeros(bucket.shape, F32)
    for b in range(REL_BUCKETS):
        acc = jnp.where(bucket == b, rb_ref[b, h], acc)
    from_prev = (lax.broadcasted_iota(jnp.int32, bucket.shape, 1)
                 > lax.broadcasted_iota(jnp.int32, bucket.shape, 0))
    o_ref[1, 0] = acc
    o_ref[0, 0] = jnp.where(from_prev, NEG_INF, acc)


def _bias_call(rel_bias, bucket):
    return pl.pallas_call(
        _bias_kernel,
        grid=(ATT_HEADS,),
        in_specs=[pl.BlockSpec(memory_space=pltpu.SMEM),
                  pl.BlockSpec(bucket.shape, lambda h: (0, 0))],
        out_specs=pl.BlockSpec((2, 1) + bucket.shape, lambda h: (0, h, 0, 0)),
        out_shape=jax.ShapeDtypeStruct((2, ATT_HEADS) + bucket.shape, F32),
        compiler_params=pltpu.CompilerParams(dimension_semantics=("arbitrary",)),
    )(rel_bias, bucket)


ATT_SEQS = 4


def _attn_kernel(sink_ref, q_ref, kp_ref, kc_ref, vp_ref, vc_ref, bias_ref, ng_ref, o_ref, obuf_ref):
    for s in range(q_ref.shape[0]):
        _attn_block(sink_ref, q_ref.at[s], kp_ref.at[s], kc_ref.at[s], vp_ref.at[s], vc_ref.at[s], bias_ref.at[0],
                    ng_ref, o_ref.at[s], obuf_ref.at[s])


def _attn_block(sink_ref, q_ref, kp_ref, kc_ref, vp_ref, vc_ref, bias_ref, ng_ref, o_ref, obuf_ref):
    qi = lax.broadcasted_iota(jnp.int32, (ATT_BLOCK, ATT_BLOCK), 0)
    ci = lax.broadcasted_iota(jnp.int32, (ATT_BLOCK, ATT_BLOCK), 1)
    from_prev = ci > qi
    low = _lane_half_mask((ATT_BLOCK, LANES))

    def band_variants(prev_ref, cur_ref):
        out = []
        for cpair in range(KV_HEADS // 2):
            lanes = slice(cpair * LANES, (cpair + 1) * LANES)
            t = jnp.concatenate([prev_ref[:, lanes], cur_ref[:, lanes]], axis=0).astype(F32)
            out.append((t.astype(BF16), pltpu.roll(t, HALF, 1).astype(BF16)))
        return out

    k_band = band_variants(kp_ref, kc_ref)
    v_band = band_variants(vp_ref, vc_ref)
    nt = (((1,), (1,)), ((), ()))

    for j in range(ATT_HEADS // 2):
        qp = q_ref[:, j * LANES:(j + 1) * LANES]
        out_pair = jnp.zeros((ATT_BLOCK, LANES), F32)
        for half in range(2):
            h = 2 * j + half
            g = h // Q_PER_KV
            swapped = int((g % 2) != half)
            keep = low if half == 0 else jnp.logical_not(low)
            qh = jnp.where(keep, qp, jnp.zeros_like(qp))
            s_band = lax.dot_general(qh, k_band[g // 2][swapped], nt, preferred_element_type=F32)
            s = jnp.where(from_prev, s_band[:, :ATT_BLOCK], s_band[:, ATT_BLOCK:]) + bias_ref[h]
            sink = sink_ref[h]
            m = jnp.maximum(jnp.max(s, axis=-1, keepdims=True), sink)
            p = jnp.exp(s - m)
            denom = jnp.sum(p, axis=-1, keepdims=True) + jnp.exp(sink - m)
            p_band = jnp.concatenate([jnp.where(from_prev, p, 0.0), jnp.where(from_prev, 0.0, p)], axis=1)
            o = jnp.dot(p_band.astype(BF16), v_band[g // 2][swapped], preferred_element_type=F32) / denom
            out_pair = out_pair + jnp.where(keep, o, 0.0)
        obuf_ref[:, j * LANES:(j + 1) * LANES] = out_pair

    att = obuf_ref[...]
    ms = jnp.mean(att * att, axis=-1, keepdims=True)
    o_ref[...] = (att * lax.rsqrt(ms + EPS) * ng_ref[...]).astype(BF16)


def _attn_call(sinks, q, k, v, bias, ng):
    bsz, l, _ = q.shape
    nb = l // ATT_BLOCK
    nseq = ATT_SEQS if bsz % ATT_SEQS == 0 else 1
    cur = lambda w: pl.BlockSpec((nseq, ATT_BLOCK, w), lambda b, i: (b, i, 0))
    prev = lambda w: pl.BlockSpec((nseq, ATT_BLOCK, w), lambda b, i: (b, jnp.maximum(i - 1, 0), 0))
    return pl.pallas_call(
        _attn_kernel,
        grid=(bsz // nseq, nb),
        in_specs=[pl.BlockSpec(memory_space=pltpu.SMEM),
                  cur(D_ATT), prev(D_KV), cur(D_KV), prev(D_KV), cur(D_KV),
                  pl.BlockSpec((1,) + bias.shape[1:], lambda b, i: (jnp.minimum(i, 1), 0, 0, 0)),
                  pl.BlockSpec(ng.shape, lambda b, i: (0, 0))],
        out_specs=cur(D_ATT),
        out_shape=jax.ShapeDtypeStruct((bsz, l, D_ATT), BF16),
        scratch_shapes=[pltpu.VMEM((nseq, ATT_BLOCK, D_ATT), F32)],
        compiler_params=pltpu.CompilerParams(dimension_semantics=("arbitrary", "arbitrary"),
                                             vmem_limit_bytes=VMEM_LIMIT),
    )(sinks, q, k, k, v, v, bias, ng)


def _out_proj_kernel(x_ref, ys_ref, ya_ref, g1_ref, sc_ref, sh_ref, g2_ref, ng_ref, wos_ref, woa_ref,
                     rwh_ref, rwl_ref, sg_ref, su_ref, sd_ref, base_ref, h_ref, lg_ref):
    mix = (jnp.dot(ys_ref[...], wos_ref[...], preferred_element_type=F32)
           + jnp.dot(ya_ref[...], woa_ref[...], preferred_element_type=F32))
    x1 = x_ref[...] + g1_ref[0] * mix
    ms = jnp.mean(x1 * x1, axis=-1, keepdims=True)
    h = x1 * lax.rsqrt(ms + EPS) * ng_ref[...]
    h = h * (1.0 + sc_ref[0]) + sh_ref[0]
    half = h.shape[1] // 2
    h_ref[...] = _pack_bf16_pair(h[:, :half], h[:, half:])
    hi, lo = _split_hi_lo(h)
    logits = (jnp.dot(hi, rwh_ref[...], preferred_element_type=F32)
              + jnp.dot(lo, rwh_ref[...], preferred_element_type=F32)
              + jnp.dot(hi, rwl_ref[...], preferred_element_type=F32))
    lg_ref[...] = logits.T
    u = _silu(jnp.dot(hi, sg_ref[...], preferred_element_type=F32)) * jnp.dot(hi, su_ref[...],
                                                                              preferred_element_type=F32)
    shared = jnp.dot(u.astype(BF16), sd_ref[...], preferred_element_type=F32)
    base_ref[...] = x1 + g2_ref[0] * shared


def _out_proj_call(x2, ys, ya, g1, sc2, sh2, g2, ng, wos, woa, rwh, rwl, sg, su, sd, tiles_per_batch, tm):
    n, d = x2.shape
    row = lambda w: pl.BlockSpec((tm, w), lambda i: (i, 0))
    full = lambda a: pl.BlockSpec(a.shape, lambda i: (0, 0))
    per_batch = pl.BlockSpec((1, 1, d), lambda i: (i // tiles_per_batch, 0, 0))
    return pl.pallas_call(
        _out_proj_kernel,
        grid=(n // tm,),
        in_specs=[row(d), row(D_SSM), row(D_ATT), per_batch, per_batch, per_batch, per_batch, full(ng),
                  full(wos), full(woa), full(rwh), full(rwl), full(sg), full(su), full(sd)],
        out_specs=[row(d), row(d // 2), pl.BlockSpec((LANES, tm), lambda i: (0, i))],
        out_shape=[jax.ShapeDtypeStruct((n, d), F32), jax.ShapeDtypeStruct((n, d // 2), jnp.uint32),
                   jax.ShapeDtypeStruct((LANES, n), F32)],
        compiler_params=pltpu.CompilerParams(dimension_semantics=("arbitrary",),
                                             vmem_limit_bytes=VMEM_LIMIT),
    )(x2, ys, ya, g1, sc2, sh2, g2, ng, wos, woa, rwh, rwl, sg, su, sd)


def _route_kernel(lg_ref, rb_ref, upper_ref, idx_ref, gate_ref, rank_ref, cnt_ref, carry_ref):
    step = pl.program_id(0)

    @pl.when(step == 0)
    def _():
        carry_ref[...] = jnp.zeros_like(carry_ref)

    t = lg_ref.shape[1]
    per_group = N_EXPERTS // ROUTE_GROUPS
    scores = 1.0 / (1.0 + jnp.exp(-lg_ref[0:N_EXPERTS, :]))
    sel = scores + rb_ref[...]
    e_iota = lax.broadcasted_iota(jnp.int32, (N_EXPERTS, t), 0)

    sel3 = sel.reshape(ROUTE_GROUPS, per_group, t)
    w_iota = lax.broadcasted_iota(jnp.int32, sel3.shape, 1)
    m1 = jnp.max(sel3, axis=1, keepdims=True)
    first = jnp.min(jnp.where(sel3 == m1, w_iota, per_group), axis=1, keepdims=True)
    m2 = jnp.max(jnp.where(w_iota == first, NEG_INF, sel3), axis=1, keepdims=True)
    grp = (m1 + m2).reshape(ROUTE_GROUPS, t)

    g_iota = lax.broadcasted_iota(jnp.int32, (ROUTE_GROUPS, t), 0)
    gmask = jnp.zeros((ROUTE_GROUPS, t), jnp.bool_)
    for _ in range(ROUTE_TOPK_GROUPS):
        gm = jnp.max(grp, axis=0, keepdims=True)
        gfirst = jnp.min(jnp.where(grp == gm, g_iota, ROUTE_GROUPS), axis=0, keepdims=True)
        hit = g_iota == gfirst
        gmask = jnp.logical_or(gmask, hit)
        grp = jnp.where(hit, NEG_INF, grp)
    allowed = jnp.broadcast_to(gmask.reshape(ROUTE_GROUPS, 1, t),
                               (ROUTE_GROUPS, per_group, t)).reshape(N_EXPERTS, t)
    masked = jnp.where(allowed, sel, NEG_INF)

    picked = jnp.zeros((N_EXPERTS, t), jnp.bool_)
    idx_rows = []
    w_rows = []
    for _ in range(TOP_K):
        mm = jnp.max(masked, axis=0, keepdims=True)
        efirst = jnp.min(jnp.where(masked == mm, e_iota, N_EXPERTS), axis=0, keepdims=True)
        hit = e_iota == efirst
        idx_rows.append(efirst)
        w_rows.append(jnp.sum(jnp.where(hit, scores, 0.0), axis=0, keepdims=True))
        picked = jnp.logical_or(picked, hit)
        masked = jnp.where(hit, NEG_INF, masked)
    idx = jnp.concatenate(idx_rows, axis=0)
    w = jnp.concatenate(w_rows, axis=0)
    gate_ref[...] = w / jnp.sum(w, axis=0, keepdims=True) * ROUTED_SCALE
    idx_ref[...] = idx

    onehot = jnp.where(picked, 1.0, 0.0)
    sub = upper_ref.shape[0]
    carry = carry_ref[:, 0:1]
    parts = []
    for s0 in range(0, t, sub):
        oh = onehot[:, s0:s0 + sub]
        parts.append(jnp.dot(oh.astype(BF16), upper_ref[...], preferred_element_type=F32) + carry)
        carry = carry + jnp.sum(oh, axis=1, keepdims=True)
    rank_full = jnp.concatenate(parts, axis=1)
    rank_rows = [jnp.sum(jnp.where(e_iota == idx_rows[k], rank_full, 0.0), axis=0, keepdims=True)
                 for k in range(TOP_K)]
    rank_ref[...] = jnp.concatenate(rank_rows, axis=0).astype(jnp.int32)
    carry_ref[...] = jnp.broadcast_to(carry, carry_ref.shape)
    cnt_ref[...] = carry_ref[...]


def _route_call(logits_t, router_bias, upper, tile):
    n = logits_t.shape[1]
    tok = lambda r: pl.BlockSpec((r, tile), lambda i: (0, i))
    return pl.pallas_call(
        _route_kernel,
        grid=(n // tile,),
        in_specs=[tok(LANES), pl.BlockSpec((N_EXPERTS, 1), lambda i: (0, 0)),
                  pl.BlockSpec(upper.shape, lambda i: (0, 0))],
        out_specs=[tok(TOP_K), tok(TOP_K), tok(TOP_K), pl.BlockSpec((N_EXPERTS, LANES), lambda i: (0, 0))],
        out_shape=[jax.ShapeDtypeStruct((TOP_K, n), jnp.int32), jax.ShapeDtypeStruct((TOP_K, n), F32),
                   jax.ShapeDtypeStruct((TOP_K, n), jnp.int32),
                   jax.ShapeDtypeStruct((N_EXPERTS, LANES), F32)],
        scratch_shapes=[pltpu.VMEM((N_EXPERTS, LANES), F32)],
        compiler_params=pltpu.CompilerParams(dimension_semantics=("arbitrary",),
                                             vmem_limit_bytes=VMEM_LIMIT),
    )(logits_t, router_bias, upper)


PLAN_EXPERT, PLAN_FRESH, PLAN_VALID, PLAN_SEG, PLAN_NEXT, PLAN_NUSED = range(6)
PLAN_ROWS = SUBLANES


def _plan_kernel(cnt_ref, tri_ref, start_ref, plan_ref, *, nblocks):
    nbp = plan_ref.shape[1]
    cnt = cnt_ref[...].astype(jnp.int32)
    blocks = (cnt + (EXPERT_ROWS - 1)) // EXPERT_ROWS
    end = jnp.dot(tri_ref[...], blocks.astype(F32), precision=lax.Precision.HIGHEST,
                  preferred_element_type=F32).astype(jnp.int32)
    start = end - blocks
    start_ref[...] = start[:, 0:1] * EXPERT_ROWS
    nused = end[N_EXPERTS - 1:N_EXPERTS, 0:1]

    e_iota = lax.broadcasted_iota(jnp.int32, (N_EXPERTS, nbp), 0)
    blk = lax.broadcasted_iota(jnp.int32, (1, nbp), 1)
    expert = jnp.minimum(jnp.sum((end[:, 0:1] <= blk).astype(jnp.int32), axis=0, keepdims=True), N_EXPERTS - 1)
    mine = e_iota == expert
    pick = lambda col: jnp.sum(jnp.where(mine, col, 0), axis=0, keepdims=True)
    first = pick(start[:, 0:1])
    valid = jnp.clip(pick(cnt[:, 0:1]) - (blk - first) * EXPERT_ROWS, 0, EXPERT_ROWS)
    present = jnp.logical_or(blocks[:, 0:1] > 0,
                             jnp.logical_and(e_iota[:, 0:1] == N_EXPERTS - 1, nused < nblocks))
    seg = jnp.sum(jnp.logical_and(present, e_iota <= expert).astype(jnp.int32), axis=0, keepdims=True) - 1
    nxt = jnp.min(jnp.where(jnp.logical_and(present, e_iota > expert), e_iota, N_EXPERTS), axis=0, keepdims=True)
    rows = {PLAN_EXPERT: expert, PLAN_FRESH: (blk == first).astype(jnp.int32), PLAN_VALID: valid, PLAN_SEG: seg,
            PLAN_NEXT: jnp.where(nxt == N_EXPERTS, -1, nxt), PLAN_NUSED: jnp.broadcast_to(nused, (1, nbp))}
    zero = jnp.zeros((1, nbp), jnp.int32)
    plan_ref[...] = jnp.concatenate([rows.get(r, zero) for r in range(PLAN_ROWS)], axis=0)


def _plan_call(counts, nblocks):
    nbp = -(-nblocks // LANES) * LANES
    tri = jnp.asarray(np.tril(np.ones((N_EXPERTS, N_EXPERTS), np.float32)))
    return pl.pallas_call(
        functools.partial(_plan_kernel, nblocks=nblocks),
        out_shape=[jax.ShapeDtypeStruct((N_EXPERTS, 1), jnp.int32),
                   jax.ShapeDtypeStruct((PLAN_ROWS, nbp), jnp.int32)],
    )(counts, tri)


def _dest_kernel(idx_ref, rank_ref, start_ref, dest_ref):
    t = idx_ref.shape[1]
    e_iota = lax.broadcasted_iota(jnp.int32, (N_EXPERTS, t), 0)
    rows = [jnp.sum(jnp.where(e_iota == idx_ref[k:k + 1, :], start_ref[...], 0), axis=0, keepdims=True)
            for k in range(TOP_K)]
    dest_ref[...] = jnp.concatenate(rows, axis=0) + rank_ref[...]


def _dest_call(idx, rank, pad_start, tile):
    n = idx.shape[1]
    tok = pl.BlockSpec((TOP_K, tile), lambda i: (0, i))
    return pl.pallas_call(
        _dest_kernel,
        grid=(n // tile,),
        in_specs=[tok, tok, pl.BlockSpec((N_EXPERTS, 1), lambda i: (0, 0))],
        out_specs=tok,
        out_shape=jax.ShapeDtypeStruct((TOP_K, n), jnp.int32),
        compiler_params=pltpu.CompilerParams(dimension_semantics=("arbitrary",)),
    )(idx, rank, pad_start)


def _scatter_rows_sc(rows, dest_flat, total_rows, chunk):
    n, w = rows.shape
    copies = dest_flat.shape[0] // n
    info = plsc.get_sparse_core_info()
    nc = info.num_cores
    per_worker = n // (nc * info.num_subcores)
    assert per_worker * nc * info.num_subcores == n and per_worker % chunk == 0
    mesh = plsc.VectorSubcoreMesh(core_axis_name="c", subcore_axis_name="s")

    @functools.partial(
        pl.kernel, mesh=mesh,
        out_type=jax.ShapeDtypeStruct((total_rows, w), rows.dtype),
        scratch_types=[pltpu.VMEM((chunk,), jnp.int32), pltpu.VMEM((chunk, w), rows.dtype),
                       pltpu.SemaphoreType.DMA],
    )
    def scatter(rows_hbm, idx_hbm, out_hbm, idx_v, rows_v, sem):
        worker = lax.axis_index("s") * nc + lax.axis_index("c")

        @pl.loop(0, per_worker // chunk)
        def _(j):
            base = worker * per_worker + j * chunk
            pltpu.sync_copy(rows_hbm.at[pl.ds(base, chunk)], rows_v)
            for k in range(copies):
                pltpu.sync_copy(idx_hbm.at[pl.ds(k * n + base, chunk)], idx_v)
                pltpu.async_copy(rows_v, out_hbm.at[idx_v], sem).wait()

    return scatter(rows, dest_flat)


X_RING = 3


def _expert_kernel(plan_ref, xs_hbm, wg_hbm, wu_hbm, wd_hbm, y_ref,
                   wgb_ref, wub_ref, wdb_ref, xbuf_ref, xsem, wg_raw, wu_raw, wd_raw, wsem):
    i = pl.program_id(0)
    rows, half = xbuf_ref.shape[1], xbuf_ref.shape[2]
    nused = plan_ref[PLAN_NUSED, 0]

    def w_copies(expert, slot):
        return [pltpu.make_async_copy(src.at[expert], dst.at[slot], wsem.at[slot])
                for src, dst in ((wg_hbm, wg_raw), (wu_hbm, wu_raw), (wd_hbm, wd_raw))]

    def x_copy(block):
        first = pl.multiple_of(block * rows, rows)
        slot = block % X_RING
        return pltpu.make_async_copy(xs_hbm.at[pl.ds(first, rows)], xbuf_ref.at[slot], xsem.at[slot])

    @pl.when(i == 0)
    def _():
        for b in range(X_RING - 1):
            @pl.when(b < nused)
            def _():
                x_copy(b).start()

    @pl.when(i + (X_RING - 1) < nused)
    def _():
        x_copy(i + (X_RING - 1)).start()

    @pl.when(i == 0)
    def _():
        for c in w_copies(plan_ref[PLAN_EXPERT, 0], 0):
            c.start()

    @pl.when(plan_ref[PLAN_FRESH, i] > 0)
    def _():
        slot = plan_ref[PLAN_SEG, i] % 2
        for c in w_copies(plan_ref[PLAN_EXPERT, i], slot):
            c.wait()

        @pl.when(plan_ref[PLAN_NEXT, i] >= 0)
        def _():
            for c in w_copies(plan_ref[PLAN_NEXT, i], 1 - slot):
                c.start()

        wgb_ref[...] = wg_raw[slot].astype(BF16)
        wub_ref[...] = wu_raw[slot].astype(BF16)
        wdb_ref[...] = wd_raw[slot].astype(BF16)

    @pl.when(i < nused)
    def _():
        x_copy(i).wait()
        xw = xbuf_ref[i % X_RING]
        row = lax.broadcasted_iota(jnp.int32, xw.shape, 0)
        x_lo, x_hi = _unpack_bf16_pair(jnp.where(row < plan_ref[PLAN_VALID, i], xw, jnp.uint32(0)))
        x_lo = x_lo.astype(BF16)
        x_hi = x_hi.astype(BF16)
        gate = (jnp.dot(x_lo, wgb_ref[:half, :], preferred_element_type=F32)
                + jnp.dot(x_hi, wgb_ref[half:, :], preferred_element_type=F32))
        up = (jnp.dot(x_lo, wub_ref[:half, :], preferred_element_type=F32)
              + jnp.dot(x_hi, wub_ref[half:, :], preferred_element_type=F32))
        u = (_silu(gate) * up).astype(BF16)
        y_lo = jnp.dot(u, wdb_ref[:, :half], preferred_element_type=F32)
        y_hi = jnp.dot(u, wdb_ref[:, half:], preferred_element_type=F32)
        y_ref[...] = _pack_bf16_pair(y_lo, y_hi)

    @pl.when(i >= nused)
    def _():
        y_ref[...] = jnp.zeros_like(y_ref)


def _expert_call(plan, xs, wg, wu, wd, rows):
    p, w = xs.shape
    d, f = wg.shape[1], wg.shape[2]
    hbm = pl.BlockSpec(memory_space=pl.ANY)
    grid_spec = pltpu.PrefetchScalarGridSpec(
        num_scalar_prefetch=1,
        grid=(p // rows,),
        in_specs=[hbm, hbm, hbm, hbm],
        out_specs=pl.BlockSpec((rows, w), lambda i, *_: (i, 0)),
        scratch_shapes=[pltpu.VMEM((d, f), BF16), pltpu.VMEM((d, f), BF16), pltpu.VMEM((f, d), BF16),
                        pltpu.VMEM((X_RING, rows, w), xs.dtype), pltpu.SemaphoreType.DMA((X_RING,)),
                        pltpu.VMEM((2, d, f), wg.dtype), pltpu.VMEM((2, d, f), wu.dtype),
                        pltpu.VMEM((2, f, d), wd.dtype), pltpu.SemaphoreType.DMA((2,))],
    )
    return pl.pallas_call(
        _expert_kernel,
        grid_spec=grid_spec,
        out_shape=jax.ShapeDtypeStruct((p, w), jnp.uint32),
        compiler_params=pltpu.CompilerParams(dimension_semantics=("arbitrary",),
                                             vmem_limit_bytes=VMEM_LIMIT),
    )(plan, xs, wg, wu, wd)


def _gather_rows_sc(table, idx, chunk):
    m = idx.shape[0]
    w = table.shape[1]
    info = plsc.get_sparse_core_info()
    nc = info.num_cores
    per_worker = m // (nc * info.num_subcores)
    assert per_worker * nc * info.num_subcores == m and per_worker % chunk == 0
    mesh = plsc.VectorSubcoreMesh(core_axis_name="c", subcore_axis_name="s")

    @functools.partial(
        pl.kernel, mesh=mesh,
        out_type=jax.ShapeDtypeStruct((m, w), table.dtype),
        scratch_types=[pltpu.VMEM((chunk,), jnp.int32), pltpu.VMEM((chunk, w), table.dtype),
                       pltpu.SemaphoreType.DMA],
    )
    def gather(table_hbm, idx_hbm, out_hbm, idx_v, rows_v, sem):
        worker = lax.axis_index("s") * nc + lax.axis_index("c")

        @pl.loop(0, per_worker // chunk)
        def _(j):
            base = worker * per_worker + j * chunk
            pltpu.sync_copy(idx_hbm.at[pl.ds(base, chunk)], idx_v)
            pltpu.async_copy(table_hbm.at[idx_v], rows_v, sem).wait()
            pltpu.sync_copy(rows_v, out_hbm.at[pl.ds(base, chunk)])

    return gather(table, idx)


def _combine_kernel(yk_ref, gate_ref, base_ref, g2_ref, fg_ref, o_ref):
    t = base_ref.shape[0]
    half = yk_ref.shape[2]
    gates = jnp.concatenate([gate_ref[...], jnp.zeros((LANES - TOP_K, t), F32)], axis=0).T
    r_lo = jnp.zeros((t, half), F32)
    r_hi = jnp.zeros((t, half), F32)
    for k in range(TOP_K):
        y_lo, y_hi = _unpack_bf16_pair(yk_ref[k])
        r_lo = r_lo + gates[:, k:k + 1] * y_lo
        r_hi = r_hi + gates[:, k:k + 1] * y_hi
    g2 = g2_ref[0]
    x_lo = base_ref[:, :half] + g2[:, :half] * r_lo
    x_hi = base_ref[:, half:] + g2[:, half:] * r_hi
    ms = (jnp.sum(x_lo * x_lo, axis=-1, keepdims=True)
          + jnp.sum(x_hi * x_hi, axis=-1, keepdims=True)) * (1.0 / (2 * half))
    inv = lax.rsqrt(ms + EPS)
    o_ref[:, :half] = x_lo * inv * fg_ref[:, :half]
    o_ref[:, half:] = x_hi * inv * fg_ref[:, half:]


def _combine_call(yk, gates, base, g2, fg, tiles_per_batch, tile, tile0):
    n, d = base.shape
    row = lambda w: pl.BlockSpec((tile, w), lambda i: (i + tile0, 0))
    return pl.pallas_call(
        _combine_kernel,
        grid=(yk.shape[1] // tile,),
        in_specs=[pl.BlockSpec((TOP_K, tile, yk.shape[2]), lambda i: (0, i, 0)),
                  pl.BlockSpec((TOP_K, tile), lambda i: (0, i + tile0)), row(d),
                  pl.BlockSpec((1, 1, d), lambda i: ((i + tile0) // tiles_per_batch, 0, 0)),
                  pl.BlockSpec((1, d), lambda i: (0, 0))],
        out_specs=row(d),
        out_shape=jax.ShapeDtypeStruct((n, d), F32),
        input_output_aliases={2: 0},
        compiler_params=pltpu.CompilerParams(dimension_semantics=("arbitrary",),
                                             vmem_limit_bytes=VMEM_LIMIT),
    )(yk, gates, base, g2, fg)


def _pad_cols(a, width):
    return jnp.pad(a, ((0, 0), (0, width - a.shape[1])))


def _layer(x, mod, norm1_g, norm2_g, w_in, conv_w, conv_b, dt_bias, a_log, d_skip, ssm_norm_g,
           att_norm_g, sinks, rel_bias, w_out, router_w, router_bias, exp_w_gate, exp_w_up, exp_w_down,
           sh_w_gate, sh_w_up, sh_w_down, final_g):
    bsz, l, d = x.shape
    n = bsz * l
    tm = min(ROW_TILE, l)

    sh1, sc1, g1, sh2, sc2, g2 = [m[:, None, :] for m in jnp.split(mod, 6, axis=-1)]

    i1 = D_SSM
    i2 = i1 + CONV_CH
    i3 = i2 + SSM_HEADS
    i4 = i3 + D_ATT
    i5 = i4 + D_KV
    wz, wx, wdt, wq, wk, wv = jnp.split(w_in, [i1, i2, i3, i4, i5], axis=-1)
    wdt = _pad_cols(wdt, LANES)
    q_scale = ATT_HEAD_DIM ** -0.5
    assert math.frexp(q_scale)[0] == 0.5
    x2 = x.reshape(n, d)
    z, xbc, dt, q, k, v = _in_proj_call(x2, sc1, sh1, norm1_g[None, :], wz.astype(BF16), wx.astype(BF16),
                                        wdt.astype(BF16), (wq * q_scale).astype(BF16), wk.astype(BF16),
                                        wv.astype(BF16), l // tm, tm)

    triu = jnp.asarray(np.triu(np.ones((CHUNK, CHUNK), np.float32))).astype(BF16)
    shift = jnp.asarray(_conv_shift_matrix()).astype(BF16)
    y_ssm = _ssd_call(xbc.reshape(bsz, l, CONV_CH), z.reshape(bsz, l, D_SSM), dt.reshape(bsz, l, LANES),
                      conv_w, conv_b[None, :], dt_bias[:, None], a_log[:, None],
                      jnp.repeat(d_skip, SSM_HEAD_DIM)[None, :], ssm_norm_g[None, :], triu, shift)

    bias = _bias_call(rel_bias, jnp.asarray(_rel_bucket_table()))
    y_att = _attn_call(sinks, q.reshape(bsz, l, D_ATT), k.reshape(bsz, l, D_KV), v.reshape(bsz, l, D_KV), bias,
                       att_norm_g[None, :])

    rw = _pad_cols(router_w, LANES)
    rwh = rw.astype(BF16)
    rwl = (rw - rwh.astype(F32)).astype(BF16)
    base, h2, logits_t = _out_proj_call(
        x2, y_ssm.reshape(n, D_SSM), y_att.reshape(n, D_ATT), g1, sc2, sh2, g2, norm2_g[None, :],
        w_out[:D_SSM].astype(BF16), w_out[D_SSM:].astype(BF16), rwh, rwl,
        sh_w_gate.astype(BF16), sh_w_up.astype(BF16), sh_w_down.astype(BF16), l // tm, tm)

    rt = min(ROUTE_TILE, n)
    rs = min(RANK_SUB, rt)
    upper = jnp.asarray(np.triu(np.ones((rs, rs), np.float32), 1)).astype(BF16)
    idx, gates, rank, counts = _route_call(logits_t, router_bias[:, None], upper, rt)

    nblocks = (n * TOP_K + N_EXPERTS * (EXPERT_ROWS - 1) + EXPERT_ROWS - 1) // EXPERT_ROWS
    pad_start, plan = _plan_call(counts, nblocks)
    dest = _dest_call(idx, rank, pad_start, rt)

    xs = _scatter_rows_sc(h2, dest.reshape(-1), nblocks * EXPERT_ROWS, SC_CHUNK)
    ys = _expert_call(plan, xs, exp_w_gate, exp_w_up, exp_w_down, EXPERT_ROWS)
    ctile = min(COMBINE_TILE, l)
    groups = COMBINE_GROUPS if bsz % COMBINE_GROUPS == 0 else 1
    ng = n // groups
    out = base
    for g in range(groups):
        idx_g = dest[:, g * ng:(g + 1) * ng].reshape(-1)
        yk = _gather_rows_sc(ys, idx_g, SC_CHUNK).reshape(TOP_K, ng, ys.shape[1])
        out = _combine_call(yk, gates, out, g2, final_g[None, :], l // ctile, ctile, g * ng // ctile)
    return out.reshape(bsz, l, d)


def kernel(x, c, mod_w, mod_b, norm1_g, norm2_g, w_in, conv_w, conv_b, dt_bias, a_log, d_skip, ssm_norm_g,
           att_norm_g, sinks, rel_bias, w_out, router_w, router_bias, exp_w_gate, exp_w_up, exp_w_down,
           sh_w_gate, sh_w_up, sh_w_down, final_g):
    assert mod_w.shape[0] == 1, "single-layer block"
    bsz = x.shape[0]
    c_pad = jnp.pad(c, ((0, SUBLANES - bsz % SUBLANES if bsz % SUBLANES else 0), (0, 0)))
    mod = _mod_call(c_pad, mod_w[0], mod_b[0][None, :])[:bsz]
    return _layer(x, mod, norm1_g[0], norm2_g[0], w_in[0], conv_w[0], conv_b[0], dt_bias[0], a_log[0], d_skip[0],
                  ssm_norm_g[0], att_norm_g[0], sinks[0], rel_bias, w_out[0], router_w[0], router_bias[0],
                  exp_w_gate[0], exp_w_up[0], exp_w_down[0], sh_w_gate[0], sh_w_up[0], sh_w_down[0], final_g)
```

```python
import functools
import math

import numpy as np
import jax
import jax.numpy as jnp
from jax import lax
from jax.experimental import pallas as pl
from jax.experimental.pallas import tpu as pltpu
from jax.experimental.pallas import tpu_sc as plsc

F32 = jnp.float32
BF16 = jnp.bfloat16

D_MODEL = 1024
SSM_HEAD_DIM = 64
D_SSM = D_MODEL
SSM_HEADS = D_SSM // SSM_HEAD_DIM
SSM_GROUPS = 4
D_STATE = 128
CONV_K = 4
CONV_CH = D_SSM + 2 * SSM_GROUPS * D_STATE
CHUNK = 128
ATT_HEAD_DIM = 64
D_ATT = D_MODEL
ATT_HEADS = D_ATT // ATT_HEAD_DIM
KV_HEADS = ATT_HEADS // 4
Q_PER_KV = ATT_HEADS // KV_HEADS
D_KV = KV_HEADS * ATT_HEAD_DIM
WINDOW = 128
ATT_BLOCK = 128
REL_BUCKETS = 32
REL_MAX_DIST = 128
N_EXPERTS = 64
TOP_K = 8
EXPERT_DIM = D_MODEL // 4
SHARED_DIM = D_MODEL // 4
ROUTE_GROUPS = 8
ROUTE_TOPK_GROUPS = 4
ROUTED_SCALE = 2.5
EPS = 1e-6

LANES = 128
SUBLANES = 8
HALF = LANES // 2

ROW_TILE = 512
ROUTE_TILE = 2048
RANK_SUB = 512
COMBINE_TILE = 256
SC_CHUNK = 128
EXPERT_ROWS = 512
COMBINE_GROUPS = 2
VMEM_LIMIT = 48 * 1024 * 1024

NEG_INF = float("-inf")


def _silu(v):
    return v * (1.0 / (1.0 + jnp.exp(-v)))


def _softplus(v):
    return jnp.maximum(v, 0.0) + jnp.log(1.0 + jnp.exp(-jnp.abs(v)))


def _bdot(a, b):
    return jnp.dot(a.astype(BF16), b.astype(BF16), preferred_element_type=F32)


def _split_hi_lo(v):
    hi = v.astype(BF16)
    lo = (v - hi.astype(F32)).astype(BF16)
    return hi, lo


def _pack_bf16_pair(a, b):
    w = pltpu.pack_elementwise([a, b], packed_dtype=BF16)
    return w if w.dtype == jnp.uint32 else lax.bitcast_convert_type(w, jnp.uint32)


def _unpack_bf16_pair(w):
    a = pltpu.unpack_elementwise(w, index=0, packed_dtype=BF16, unpacked_dtype=F32)
    b = pltpu.unpack_elementwise(w, index=1, packed_dtype=BF16, unpacked_dtype=F32)
    return a, b


def _lane_half_mask(shape):
    return lax.broadcasted_iota(jnp.int32, shape, len(shape) - 1) < HALF


def _mod_kernel(c_ref, w_ref, b_ref, o_ref):
    a = _silu(c_ref[...])
    o_ref[...] = jnp.dot(a, w_ref[...], precision=lax.Precision.HIGHEST,
                         preferred_element_type=F32) + b_ref[...]


def _mod_call(c_pad, mod_w, mod_b):
    rows, d = c_pad.shape
    cols = mod_w.shape[1]
    return pl.pallas_call(
        _mod_kernel,
        grid=(cols // d,),
        in_specs=[pl.BlockSpec((rows, d), lambda j: (0, 0)),
                  pl.BlockSpec((d, d), lambda j: (0, j)),
                  pl.BlockSpec((1, d), lambda j: (0, j))],
        out_specs=pl.BlockSpec((rows, d), lambda j: (0, j)),
        out_shape=jax.ShapeDtypeStruct((rows, cols), F32),
        compiler_params=pltpu.CompilerParams(dimension_semantics=("arbitrary",),
                                             vmem_limit_bytes=VMEM_LIMIT),
    )(c_pad, mod_w, mod_b)


def _in_proj_kernel(x_ref, sc_ref, sh_ref, g_ref, wz_ref, wx_ref, wdt_ref, wq_ref, wk_ref, wv_ref,
                    z_ref, xbc_ref, dt_ref, q_ref, k_ref, v_ref):
    xf = x_ref[...]
    ms = jnp.mean(xf * xf, axis=-1, keepdims=True)
    h = xf * lax.rsqrt(ms + EPS) * g_ref[...]
    h = h * (1.0 + sc_ref[0]) + sh_ref[0]
    hb = h.astype(BF16)
    z_ref[...] = jnp.dot(hb, wz_ref[...], preferred_element_type=F32).astype(BF16)
    xbc_ref[...] = jnp.dot(hb, wx_ref[...], preferred_element_type=F32).astype(BF16)
    dt_ref[...] = jnp.dot(hb, wdt_ref[...], preferred_element_type=F32)
    q_ref[...] = jnp.dot(hb, wq_ref[...], preferred_element_type=F32).astype(BF16)
    k_ref[...] = jnp.dot(hb, wk_ref[...], preferred_element_type=F32).astype(BF16)
    v_ref[...] = jnp.dot(hb, wv_ref[...], preferred_element_type=F32).astype(BF16)


def _in_proj_call(x2, sc1, sh1, g1n, wz, wx, wdt, wq, wk, wv, tiles_per_batch, tm):
    n, d = x2.shape
    row = lambda w: pl.BlockSpec((tm, w), lambda i: (i, 0))
    full = lambda a: pl.BlockSpec(a.shape, lambda i: (0, 0))
    per_batch = pl.BlockSpec((1, 1, d), lambda i: (i // tiles_per_batch, 0, 0))
    outs = [(wz.shape[1], BF16), (wx.shape[1], BF16), (wdt.shape[1], F32),
            (wq.shape[1], BF16), (wk.shape[1], BF16), (wv.shape[1], BF16)]
    return pl.pallas_call(
        _in_proj_kernel,
        grid=(n // tm,),
        in_specs=[row(d), per_batch, per_batch, full(g1n), full(wz), full(wx), full(wdt), full(wq),
                  full(wk), full(wv)],
        out_specs=[row(w) for w, _ in outs],
        out_shape=[jax.ShapeDtypeStruct((n, w), dt) for w, dt in outs],
        compiler_params=pltpu.CompilerParams(dimension_semantics=("arbitrary",),
                                             vmem_limit_bytes=VMEM_LIMIT),
    )(x2, sc1, sh1, g1n, wz, wx, wdt, wq, wk, wv)


SSD_SEQS = 4
CONV_HALO = 16


def _conv_shift_matrix():
    s = np.zeros((CONV_K * CHUNK, CONV_HALO + CHUNK), np.float32)
    for k in range(CONV_K):
        t = np.arange(CHUNK)
        s[k * CHUNK + t, CONV_HALO + t - (CONV_K - 1) + k] = 1.0
    return s


def _silu_tanh(v):
    hv = 0.5 * v
    return hv + hv * jnp.tanh(hv)


def _ssd_kernel(xbc_ref, z_ref, dt_ref, cw_ref, cb_ref, dtb_ref, alog_ref, dskip_ref, ng_ref, triu_ref, shift_ref,
                y_ref, state_ref, ucat_ref, ybuf_ref):
    nseq = xbc_ref.shape[0]

    @pl.when(pl.program_id(1) == 0)
    def _():
        state_ref[...] = jnp.zeros_like(state_ref)
        ucat_ref[:, 0:CONV_HALO, :] = jnp.zeros((nseq, CONV_HALO, CONV_CH), BF16)

    for q in range(nseq):
        _ssd_chunk(xbc_ref.at[q], z_ref.at[q], dt_ref.at[q], cw_ref, cb_ref, dtb_ref, alog_ref, dskip_ref, ng_ref,
                   triu_ref, shift_ref, y_ref.at[q], state_ref.at[q], ucat_ref.at[q], ybuf_ref.at[q])


def _ssd_chunk(xbc_ref, z_ref, dt_ref, cw_ref, cb_ref, dtb_ref, alog_ref, dskip_ref, ng_ref, triu_ref, shift_ref,
               y_ref, state_ref, ucat_ref, ybuf_ref):
    ucat_ref[CONV_HALO:, :] = xbc_ref[...]
    shifted = jnp.dot(shift_ref[...], ucat_ref[...], preferred_element_type=F32)
    ucat_ref[0:CONV_HALO, :] = ucat_ref[CHUNK:CHUNK + CONV_HALO, :]
    acc = cb_ref[...] + cw_ref[0:1, :] * shifted[0:CHUNK]
    for kk in range(1, CONV_K):
        acc = acc + cw_ref[kk:kk + 1, :] * shifted[kk * CHUNK:(kk + 1) * CHUNK]
    act = _silu_tanh(acc)
    xs = act[:, :D_SSM]
    gn = SSM_GROUPS * D_STATE

    dt_t = _softplus(dt_ref[...].T[0:SSM_HEADS, :] + dtb_ref[...])
    a_t = dt_t * (-jnp.exp(alog_ref[...]))
    a_hi = a_t.astype(BF16)
    a_mid = (a_t - a_hi.astype(F32)).astype(BF16)
    a_lo = (a_t - a_hi.astype(F32) - a_mid.astype(F32)).astype(BF16)
    triu = triu_ref[...]
    cs_t = (jnp.dot(a_hi, triu, preferred_element_type=F32) + jnp.dot(a_mid, triu, preferred_element_type=F32)
            + jnp.dot(a_lo, triu, preferred_element_type=F32))
    cs_end = cs_t[:, CHUNK - 1:CHUNK]
    r_t = cs_t - jnp.log(dt_t)
    w_t = jnp.exp(cs_end - cs_t) * dt_t
    chunk_decay = jnp.exp(cs_end)
    cols = jnp.concatenate([cs_t, jnp.exp(cs_t), jnp.zeros((LANES - 2 * SSM_HEADS, CHUNK), F32)], axis=0).T

    li = lax.broadcasted_iota(jnp.int32, (CHUNK, CHUNK), 0)
    si = lax.broadcasted_iota(jnp.int32, (CHUNK, CHUNK), 1)
    causal = li >= si
    low = _lane_half_mask((CHUNK, LANES))
    low_row = _lane_half_mask((1, LANES))

    heads_per_group = SSM_HEADS // SSM_GROUPS
    for g in range(SSM_GROUPS):
        b_g = act[:, D_SSM + g * D_STATE:D_SSM + (g + 1) * D_STATE]
        c_g = act[:, D_SSM + gn + g * D_STATE:D_SSM + gn + (g + 1) * D_STATE]
        b_gb = b_g.astype(BF16)
        c_gb = c_g.astype(BF16)
        cb = lax.dot_general(c_gb, b_gb, (((1,), (1,)), ((), ())), preferred_element_type=F32)
        b_t = b_g.T
        for jp in range(heads_per_group // 2):
            j = g * (heads_per_group // 2) + jp
            lanes = slice(j * LANES, (j + 1) * LANES)
            xp = xs[:, lanes]
            xpb = xp.astype(BF16)
            ydiag = jnp.zeros((CHUNK, LANES), F32)
            snew = jnp.zeros((D_STATE, LANES), F32)
            for half in range(2):
                h = 2 * j + half
                diff = cols[:, h:h + 1] - r_t[h:h + 1, :]
                m = (cb * jnp.exp(jnp.where(causal, diff, NEG_INF))).astype(BF16)
                keep = low if half == 0 else jnp.logical_not(low)
                xh = jnp.where(keep, xpb, jnp.zeros_like(xpb))
                ydiag = ydiag + jnp.dot(m, xh, preferred_element_type=F32)
                snew = snew + jnp.dot((b_t * w_t[h:h + 1, :]).astype(BF16), xh, preferred_element_type=F32)
            s_in = state_ref[:, lanes]
            yoff = jnp.dot(c_gb, s_in.astype(BF16), preferred_element_type=F32)
            h0 = 2 * j
            e0 = SSM_HEADS + h0
            escale = jnp.where(low, cols[:, e0:e0 + 1], cols[:, e0 + 1:e0 + 2])
            cdec = jnp.where(low_row, chunk_decay[h0:h0 + 1, :], chunk_decay[h0 + 1:h0 + 2, :])
            ybuf_ref[:, lanes] = ydiag + yoff * escale + xp * dskip_ref[:, lanes]
            state_ref[:, lanes] = s_in * cdec + snew

    yz = ybuf_ref[...] * _silu_tanh(z_ref[...].astype(F32))
    gw = D_SSM // SSM_GROUPS
    for g in range(SSM_GROUPS):
        part = yz[:, g * gw:(g + 1) * gw]
        ms = jnp.mean(part * part, axis=-1, keepdims=True)
        y_ref[:, g * gw:(g + 1) * gw] = (part * lax.rsqrt(ms + EPS)
                                            * ng_ref[:, g * gw:(g + 1) * gw]).astype(BF16)


def _ssd_call(xbc, z, dt, conv_w, conv_b, dtb, alog, dskip, ng, triu, shift):
    bsz, l, _ = xbc.shape
    nc = l // CHUNK
    nseq = SSD_SEQS if bsz % SSD_SEQS == 0 else 1
    chunk = lambda w: pl.BlockSpec((nseq, CHUNK, w), lambda b, c: (b, c, 0))
    full = lambda a: pl.BlockSpec(a.shape, lambda b, c: (0, 0))
    return pl.pallas_call(
        _ssd_kernel,
        grid=(bsz // nseq, nc),
        in_specs=[chunk(CONV_CH), chunk(D_SSM), chunk(LANES), full(conv_w), full(conv_b), full(dtb),
                  full(alog), full(dskip), full(ng), full(triu), full(shift)],
        out_specs=chunk(D_SSM),
        out_shape=jax.ShapeDtypeStruct((bsz, l, D_SSM), BF16),
        scratch_shapes=[pltpu.VMEM((nseq, D_STATE, D_SSM), F32),
                        pltpu.VMEM((nseq, CONV_HALO + CHUNK, CONV_CH), BF16),
                        pltpu.VMEM((nseq, CHUNK, D_SSM), F32)],
        compiler_params=pltpu.CompilerParams(dimension_semantics=("arbitrary", "arbitrary"),
                                             vmem_limit_bytes=VMEM_LIMIT),
    )(xbc, z, dt, conv_w, conv_b, dtb, alog, dskip, ng, triu, shift)


assert WINDOW == ATT_BLOCK


def _rel_bucket_table():
    qi = np.arange(ATT_BLOCK)[:, None]
    c = np.arange(ATT_BLOCK)[None, :]
    dist = np.where(c > qi, qi + ATT_BLOCK - c, qi - c)
    max_exact = REL_BUCKETS // 2
    d = np.maximum(dist, 1).astype(np.float32)
    large = max_exact + (np.log(d / np.float32(max_exact)) / np.float32(math.log(REL_MAX_DIST / max_exact))
                         * np.float32(REL_BUCKETS - max_exact)).astype(np.int32)
    large = np.minimum(large, REL_BUCKETS - 1)
    return np.where(dist < max_exact, dist, large).astype(np.int32)


def _bias_kernel(rb_ref, bucket_ref, o_ref):
    h = pl.program_id(0)
    bucket = bucket_ref[...]
    acc = jnp.zeros(bucket.shape, F32)
    for b in range(REL_BUCKETS):
        acc = jnp.where(bucket == b, rb_ref[b, h], acc)
    from_prev = (lax.broadcasted_iota(jnp.int32, bucket.shape, 1)
                 > lax.broadcasted_iota(jnp.int32, bucket.shape, 0))
    o_ref[1, 0] = acc
    o_ref[0, 0] = jnp.where(from_prev, NEG_INF, acc)


def _bias_call(rel_bias, bucket):
    return pl.pallas_call(
        _bias_kernel,
        grid=(ATT_HEADS,),
        in_specs=[pl.BlockSpec(memory_space=pltpu.SMEM),
                  pl.BlockSpec(bucket.shape, lambda h: (0, 0))],
        out_specs=pl.BlockSpec((2, 1) + bucket.shape, lambda h: (0, h, 0, 0)),
        out_shape=jax.ShapeDtypeStruct((2, ATT_HEADS) + bucket.shape, F32),
        compiler_params=pltpu.CompilerParams(dimension_semantics=("arbitrary",)),
    )(rel_bias, bucket)


ATT_SEQS = 4


def _attn_kernel(sink_ref, q_ref, kp_ref, kc_ref, vp_ref, vc_ref, bias_ref, ng_ref, o_ref, obuf_ref):
    for s in range(q_ref.shape[0]):
        _attn_block(sink_ref, q_ref.at[s], kp_ref.at[s], kc_ref.at[s], vp_ref.at[s], vc_ref.at[s], bias_ref.at[0],
                    ng_ref, o_ref.at[s], obuf_ref.at[s])


def _attn_block(sink_ref, q_ref, kp_ref, kc_ref, vp_ref, vc_ref, bias_ref, ng_ref, o_ref, obuf_ref):
    qi = lax.broadcasted_iota(jnp.int32, (ATT_BLOCK, ATT_BLOCK), 0)
    ci = lax.broadcasted_iota(jnp.int32, (ATT_BLOCK, ATT_BLOCK), 1)
    from_prev = ci > qi
    low = _lane_half_mask((ATT_BLOCK, LANES))

    def band_variants(prev_ref, cur_ref):
        out = []
        for cpair in range(KV_HEADS // 2):
            lanes = slice(cpair * LANES, (cpair + 1) * LANES)
            t = jnp.concatenate([prev_ref[:, lanes], cur_ref[:, lanes]], axis=0).astype(F32)
            out.append((t.astype(BF16), pltpu.roll(t, HALF, 1).astype(BF16)))
        return out

    k_band = band_variants(kp_ref, kc_ref)
    v_band = band_variants(vp_ref, vc_ref)
    nt = (((1,), (1,)), ((), ()))

    for j in range(ATT_HEADS // 2):
        qp = q_ref[:, j * LANES:(j + 1) * LANES]
        out_pair = jnp.zeros((ATT_BLOCK, LANES), F32)
        for half in range(2):
            h = 2 * j + half
            g = h // Q_PER_KV
            swapped = int((g % 2) != half)
            keep = low if half == 0 else jnp.logical_not(low)
            qh = jnp.where(keep, qp, jnp.zeros_like(qp))
            s_band = lax.dot_general(qh, k_band[g // 2][swapped], nt, preferred_element_type=F32)
            s = jnp.where(from_prev, s_band[:, :ATT_BLOCK], s_band[:, ATT_BLOCK:]) + bias_ref[h]
            sink = sink_ref[h]
            m = jnp.maximum(jnp.max(s, axis=-1, keepdims=True), sink)
            p = jnp.exp(s - m)
            denom = jnp.sum(p, axis=-1, keepdims=True) + jnp.exp(sink - m)
            p_band = jnp.concatenate([jnp.where(from_prev, p, 0.0), jnp.where(from_prev, 0.0, p)], axis=1)
            o = jnp.dot(p_band.astype(BF16), v_band[g // 2][swapped], preferred_element_type=F32) / denom
            out_pair = out_pair + jnp.where(keep, o, 0.0)
        obuf_ref[:, j * LANES:(j + 1) * LANES] = out_pair

    att = obuf_ref[...]
    ms = jnp.mean(att * att, axis=-1, keepdims=True)
    o_ref[...] = (att * lax.rsqrt(ms + EPS) * ng_ref[...]).astype(BF16)


def _attn_call(sinks, q, k, v, bias, ng):
    bsz, l, _ = q.shape
    nb = l // ATT_BLOCK
    nseq = ATT_SEQS if bsz % ATT_SEQS == 0 else 1
    cur = lambda w: pl.BlockSpec((nseq, ATT_BLOCK, w), lambda b, i: (b, i, 0))
    prev = lambda w: pl.BlockSpec((nseq, ATT_BLOCK, w), lambda b, i: (b, jnp.maximum(i - 1, 0), 0))
    return pl.pallas_call(
        _attn_kernel,
        grid=(bsz // nseq, nb),
        in_specs=[pl.BlockSpec(memory_space=pltpu.SMEM),
                  cur(D_ATT), prev(D_KV), cur(D_KV), prev(D_KV), cur(D_KV),
                  pl.BlockSpec((1,) + bias.shape[1:], lambda b, i: (jnp.minimum(i, 1), 0, 0, 0)),
                  pl.BlockSpec(ng.shape, lambda b, i: (0, 0))],
        out_specs=cur(D_ATT),
        out_shape=jax.ShapeDtypeStruct((bsz, l, D_ATT), BF16),
        scratch_shapes=[pltpu.VMEM((nseq, ATT_BLOCK, D_ATT), F32)],
        compiler_params=pltpu.CompilerParams(dimension_semantics=("arbitrary", "arbitrary"),
                                             vmem_limit_bytes=VMEM_LIMIT),
    )(sinks, q, k, k, v, v, bias, ng)


def _out_proj_kernel(x_ref, ys_ref, ya_ref, g1_ref, sc_ref, sh_ref, g2_ref, ng_ref, wos_ref, woa_ref,
                     rwh_ref, rwl_ref, sg_ref, su_ref, sd_ref, rb_ref, upper_ref,
                     base_ref, h_ref, idx_ref, gate_ref, rank_ref, cnt_ref, carry_ref):
    @pl.when(pl.program_id(0) == 0)
    def _():
        carry_ref[...] = jnp.zeros_like(carry_ref)

    mix = (jnp.dot(ys_ref[...], wos_ref[...], preferred_element_type=F32)
           + jnp.dot(ya_ref[...], woa_ref[...], preferred_element_type=F32))
    x1 = x_ref[...] + g1_ref[0] * mix
    ms = jnp.mean(x1 * x1, axis=-1, keepdims=True)
    h = x1 * lax.rsqrt(ms + EPS) * ng_ref[...]
    h = h * (1.0 + sc_ref[0]) + sh_ref[0]
    half = h.shape[1] // 2
    h_ref[...] = _pack_bf16_pair(h[:, :half], h[:, half:])
    hi, lo = _split_hi_lo(h)
    hi_terms = jnp.dot(hi, rwl_ref[...], preferred_element_type=F32)
    logits = (hi_terms[:, :LANES] + hi_terms[:, LANES:]
              + jnp.dot(lo, rwh_ref[...], preferred_element_type=F32))
    _route_tokens(logits.T[0:N_EXPERTS, :], rb_ref, upper_ref, idx_ref, gate_ref, rank_ref, cnt_ref, carry_ref)
    u = _silu(jnp.dot(hi, sg_ref[...], preferred_element_type=F32)) * jnp.dot(hi, su_ref[...],
                                                                              preferred_element_type=F32)
    shared = jnp.dot(u.astype(BF16), sd_ref[...], preferred_element_type=F32)
    base_ref[...] = x1 + g2_ref[0] * shared


def _out_proj_call(x2, ys, ya, g1, sc2, sh2, g2, ng, wos, woa, rwh, rwl, sg, su, sd, router_bias, upper,
                   tiles_per_batch, tm):
    n, d = x2.shape
    row = lambda w: pl.BlockSpec((tm, w), lambda i: (i, 0))
    tok = lambda r: pl.BlockSpec((r, tm), lambda i: (0, i))
    full = lambda a: pl.BlockSpec(a.shape, lambda i: (0, 0))
    per_batch = pl.BlockSpec((1, 1, d), lambda i: (i // tiles_per_batch, 0, 0))
    return pl.pallas_call(
        _out_proj_kernel,
        grid=(n // tm,),
        in_specs=[row(d), row(D_SSM), row(D_ATT), per_batch, per_batch, per_batch, per_batch, full(ng),
                  full(wos), full(woa), full(rwh), full(rwl), full(sg), full(su), full(sd), full(router_bias),
                  full(upper)],
        out_specs=[row(d), row(d // 2), tok(TOP_K), tok(TOP_K), tok(TOP_K),
                   pl.BlockSpec((N_EXPERTS, LANES), lambda i: (0, 0))],
        out_shape=[jax.ShapeDtypeStruct((n, d), F32), jax.ShapeDtypeStruct((n, d // 2), jnp.uint32),
                   jax.ShapeDtypeStruct((TOP_K, n), jnp.int32), jax.ShapeDtypeStruct((TOP_K, n), F32),
                   jax.ShapeDtypeStruct((TOP_K, n), jnp.int32), jax.ShapeDtypeStruct((N_EXPERTS, LANES), F32)],
        scratch_shapes=[pltpu.VMEM((N_EXPERTS, LANES), F32)],
        compiler_params=pltpu.CompilerParams(dimension_semantics=("arbitrary",),
                                             vmem_limit_bytes=VMEM_LIMIT),
    )(x2, ys, ya, g1, sc2, sh2, g2, ng, wos, woa, rwh, rwl, sg, su, sd, router_bias, upper)


def _route_tokens(logits_t, rb_ref, upper_ref, idx_ref, gate_ref, rank_ref, cnt_ref, carry_ref):
    t = logits_t.shape[1]
    per_group = N_EXPERTS // ROUTE_GROUPS
    scores = 1.0 / (1.0 + jnp.exp(-logits_t))
    sel = scores + rb_ref[...]
    e_iota = lax.broadcasted_iota(jnp.int32, (N_EXPERTS, t), 0)

    sel3 = sel.reshape(ROUTE_GROUPS, per_group, t)
    w_iota = lax.broadcasted_iota(jnp.int32, sel3.shape, 1)
    m1 = jnp.max(sel3, axis=1, keepdims=True)
    first = jnp.min(jnp.where(sel3 == m1, w_iota, per_group), axis=1, keepdims=True)
    m2 = jnp.max(jnp.where(w_iota == first, NEG_INF, sel3), axis=1, keepdims=True)
    grp = (m1 + m2).reshape(ROUTE_GROUPS, t)

    g_iota = lax.broadcasted_iota(jnp.int32, (ROUTE_GROUPS, t), 0)
    gmask = jnp.zeros((ROUTE_GROUPS, t), jnp.bool_)
    for _ in range(ROUTE_TOPK_GROUPS):
        gm = jnp.max(grp, axis=0, keepdims=True)
        gfirst = jnp.min(jnp.where(grp == gm, g_iota, ROUTE_GROUPS), axis=0, keepdims=True)
        hit = g_iota == gfirst
        gmask = jnp.logical_or(gmask, hit)
        grp = jnp.where(hit, NEG_INF, grp)
    allowed = jnp.broadcast_to(gmask.reshape(ROUTE_GROUPS, 1, t),
                               (ROUTE_GROUPS, per_group, t)).reshape(N_EXPERTS, t)
    masked = jnp.where(allowed, sel, NEG_INF)

    picked = jnp.zeros((N_EXPERTS, t), jnp.bool_)
    idx_rows = []
    w_rows = []
    for _ in range(TOP_K):
        mm = jnp.max(masked, axis=0, keepdims=True)
        efirst = jnp.min(jnp.where(masked == mm, e_iota, N_EXPERTS), axis=0, keepdims=True)
        hit = e_iota == efirst
        idx_rows.append(efirst)
        w_rows.append(jnp.sum(jnp.where(hit, scores, 0.0), axis=0, keepdims=True))
        picked = jnp.logical_or(picked, hit)
        masked = jnp.where(hit, NEG_INF, masked)
    idx = jnp.concatenate(idx_rows, axis=0)
    w = jnp.concatenate(w_rows, axis=0)
    gate_ref[...] = w / jnp.sum(w, axis=0, keepdims=True) * ROUTED_SCALE
    idx_ref[...] = idx

    onehot = jnp.where(picked, 1.0, 0.0)
    sub = upper_ref.shape[0]
    carry = carry_ref[:, 0:1]
    parts = []
    for s0 in range(0, t, sub):
        oh = onehot[:, s0:s0 + sub]
        parts.append(jnp.dot(oh.astype(BF16), upper_ref[...], preferred_element_type=F32) + carry)
        carry = carry + jnp.sum(oh, axis=1, keepdims=True)
    rank_full = jnp.concatenate(parts, axis=1)
    rank_rows = [jnp.sum(jnp.where(e_iota == idx_rows[k], rank_full, 0.0), axis=0, keepdims=True)
                 for k in range(TOP_K)]
    rank_ref[...] = jnp.concatenate(rank_rows, axis=0).astype(jnp.int32)
    carry_ref[...] = jnp.broadcast_to(carry, carry_ref.shape)
    cnt_ref[...] = carry_ref[...]


PLAN_EXPERT, PLAN_FRESH, PLAN_VALID, PLAN_SEG, PLAN_NEXT, PLAN_NUSED = range(6)
PLAN_ROWS = SUBLANES


def _plan_kernel(cnt_ref, tri_ref, start_ref, plan_ref, *, nblocks):
    nbp = plan_ref.shape[1]
    cnt = cnt_ref[...].astype(jnp.int32)
    blocks = (cnt + (EXPERT_ROWS - 1)) // EXPERT_ROWS
    end = jnp.dot(tri_ref[...], blocks.astype(F32), precision=lax.Precision.HIGHEST,
                  preferred_element_type=F32).astype(jnp.int32)
    start = end - blocks
    start_ref[...] = start[:, 0:1] * EXPERT_ROWS
    nused = end[N_EXPERTS - 1:N_EXPERTS, 0:1]

    e_iota = lax.broadcasted_iota(jnp.int32, (N_EXPERTS, nbp), 0)
    blk = lax.broadcasted_iota(jnp.int32, (1, nbp), 1)
    expert = jnp.minimum(jnp.sum((end[:, 0:1] <= blk).astype(jnp.int32), axis=0, keepdims=True), N_EXPERTS - 1)
    mine = e_iota == expert
    pick = lambda col: jnp.sum(jnp.where(mine, col, 0), axis=0, keepdims=True)
    first = pick(start[:, 0:1])
    valid = jnp.clip(pick(cnt[:, 0:1]) - (blk - first) * EXPERT_ROWS, 0, EXPERT_ROWS)
    present = jnp.logical_or(blocks[:, 0:1] > 0,
                             jnp.logical_and(e_iota[:, 0:1] == N_EXPERTS - 1, nused < nblocks))
    seg = jnp.sum(jnp.logical_and(present, e_iota <= expert).astype(jnp.int32), axis=0, keepdims=True) - 1
    nxt = jnp.min(jnp.where(jnp.logical_and(present, e_iota > expert), e_iota, N_EXPERTS), axis=0, keepdims=True)
    rows = {PLAN_EXPERT: expert, PLAN_FRESH: (blk == first).astype(jnp.int32), PLAN_VALID: valid, PLAN_SEG: seg,
            PLAN_NEXT: jnp.where(nxt == N_EXPERTS, -1, nxt), PLAN_NUSED: jnp.broadcast_to(nused, (1, nbp))}
    zero = jnp.zeros((1, nbp), jnp.int32)
    plan_ref[...] = jnp.concatenate([rows.get(r, zero) for r in range(PLAN_ROWS)], axis=0)


def _plan_call(counts, nblocks):
    nbp = -(-nblocks // LANES) * LANES
    tri = jnp.asarray(np.tril(np.ones((N_EXPERTS, N_EXPERTS), np.float32)))
    return pl.pallas_call(
        functools.partial(_plan_kernel, nblocks=nblocks),
        out_shape=[jax.ShapeDtypeStruct((N_EXPERTS, 1), jnp.int32),
                   jax.ShapeDtypeStruct((PLAN_ROWS, nbp), jnp.int32)],
    )(counts, tri)


def _dest_kernel(idx_ref, rank_ref, start_ref, dest_ref):
    t = idx_ref.shape[1]
    e_iota = lax.broadcasted_iota(jnp.int32, (N_EXPERTS, t), 0)
    rows = [jnp.sum(jnp.where(e_iota == idx_ref[k:k + 1, :], start_ref[...], 0), axis=0, keepdims=True)
            for k in range(TOP_K)]
    dest = jnp.concatenate(rows, axis=0) + rank_ref[...]
    for k in range(TOP_K):
        for c in range(t // LANES):
            dest_ref[k, c:c + 1, :] = dest[k:k + 1, c * LANES:(c + 1) * LANES]


def _dest_call(idx, rank, pad_start, tile):
    n = idx.shape[1]
    tok = pl.BlockSpec((TOP_K, tile), lambda i: (0, i))
    return pl.pallas_call(
        _dest_kernel,
        grid=(n // tile,),
        in_specs=[tok, tok, pl.BlockSpec((N_EXPERTS, 1), lambda i: (0, 0))],
        out_specs=pl.BlockSpec((TOP_K, tile // LANES, LANES), lambda i: (0, i, 0)),
        out_shape=jax.ShapeDtypeStruct((TOP_K, n // LANES, LANES), jnp.int32),
        compiler_params=pltpu.CompilerParams(dimension_semantics=("arbitrary",)),
    )(idx, rank, pad_start)


def _scatter_rows_sc(rows, dest_flat, total_rows, chunk):
    n, w = rows.shape
    copies = dest_flat.shape[0] // n
    info = plsc.get_sparse_core_info()
    nc = info.num_cores
    per_worker = n // (nc * info.num_subcores)
    assert per_worker * nc * info.num_subcores == n and per_worker % chunk == 0
    mesh = plsc.VectorSubcoreMesh(core_axis_name="c", subcore_axis_name="s")

    @functools.partial(
        pl.kernel, mesh=mesh,
        out_type=jax.ShapeDtypeStruct((total_rows, w), rows.dtype),
        scratch_types=[pltpu.VMEM((chunk,), jnp.int32), pltpu.VMEM((chunk, w), rows.dtype),
                       pltpu.SemaphoreType.DMA],
    )
    def scatter(rows_hbm, idx_hbm, out_hbm, idx_v, rows_v, sem):
        worker = lax.axis_index("s") * nc + lax.axis_index("c")

        @pl.loop(0, per_worker // chunk)
        def _(j):
            base = worker * per_worker + j * chunk
            pltpu.sync_copy(rows_hbm.at[pl.ds(base, chunk)], rows_v)
            for k in range(copies):
                pltpu.sync_copy(idx_hbm.at[pl.ds(k * n + base, chunk)], idx_v)
                pltpu.async_copy(rows_v, out_hbm.at[idx_v], sem).wait()

    return scatter(rows, dest_flat)


X_RING = 3


def _expert_kernel(plan_ref, xs_hbm, wg_hbm, wu_hbm, wd_hbm, y_ref,
                   wgb_ref, wub_ref, wdb_ref, xbuf_ref, xsem, wg_raw, wu_raw, wd_raw, wsem):
    i = pl.program_id(0)
    rows, half = xbuf_ref.shape[1], xbuf_ref.shape[2]
    nused = plan_ref[PLAN_NUSED, 0]

    def w_copies(expert, slot):
        return [pltpu.make_async_copy(src.at[expert], dst.at[slot], wsem.at[slot])
                for src, dst in ((wg_hbm, wg_raw), (wu_hbm, wu_raw), (wd_hbm, wd_raw))]

    def x_copy(block):
        first = pl.multiple_of(block * rows, rows)
        slot = block % X_RING
        return pltpu.make_async_copy(xs_hbm.at[pl.ds(first, rows)], xbuf_ref.at[slot], xsem.at[slot])

    @pl.when(i == 0)
    def _():
        for b in range(X_RING - 1):
            @pl.when(b < nused)
            def _():
                x_copy(b).start()

    @pl.when(i + (X_RING - 1) < nused)
    def _():
        x_copy(i + (X_RING - 1)).start()

    @pl.when(i == 0)
    def _():
        for c in w_copies(plan_ref[PLAN_EXPERT, 0], 0):
            c.start()

    @pl.when(plan_ref[PLAN_FRESH, i] > 0)
    def _():
        slot = plan_ref[PLAN_SEG, i] % 2
        for c in w_copies(plan_ref[PLAN_EXPERT, i], slot):
            c.wait()

        @pl.when(plan_ref[PLAN_NEXT, i] >= 0)
        def _():
            for c in w_copies(plan_ref[PLAN_NEXT, i], 1 - slot):
                c.start()

        wgb_ref[...] = wg_raw[slot].astype(BF16)
        wub_ref[...] = wu_raw[slot].astype(BF16)
        wdb_ref[...] = wd_raw[slot].astype(BF16)

    @pl.when(i < nused)
    def _():
        x_copy(i).wait()
        xw = xbuf_ref[i % X_RING]
        row = lax.broadcasted_iota(jnp.int32, xw.shape, 0)
        x_lo, x_hi = _unpack_bf16_pair(jnp.where(row < plan_ref[PLAN_VALID, i], xw, jnp.uint32(0)))
        x_lo = x_lo.astype(BF16)
        x_hi = x_hi.astype(BF16)
        gate = (jnp.dot(x_lo, wgb_ref[:half, :], preferred_element_type=F32)
                + jnp.dot(x_hi, wgb_ref[half:, :], preferred_element_type=F32))
        up = (jnp.dot(x_lo, wub_ref[:half, :], preferred_element_type=F32)
              + jnp.dot(x_hi, wub_ref[half:, :], preferred_element_type=F32))
        u = (_silu(gate) * up).astype(BF16)
        y_lo = jnp.dot(u, wdb_ref[:, :half], preferred_element_type=F32)
        y_hi = jnp.dot(u, wdb_ref[:, half:], preferred_element_type=F32)
        y_ref[...] = _pack_bf16_pair(y_lo, y_hi)

    @pl.when(i >= nused)
    def _():
        y_ref[...] = jnp.zeros_like(y_ref)


def _expert_call(plan, xs, wg, wu, wd, rows):
    p, w = xs.shape
    d, f = wg.shape[1], wg.shape[2]
    hbm = pl.BlockSpec(memory_space=pl.ANY)
    grid_spec = pltpu.PrefetchScalarGridSpec(
        num_scalar_prefetch=1,
        grid=(p // rows,),
        in_specs=[hbm, hbm, hbm, hbm],
        out_specs=pl.BlockSpec((rows, w), lambda i, *_: (i, 0)),
        scratch_shapes=[pltpu.VMEM((d, f), BF16), pltpu.VMEM((d, f), BF16), pltpu.VMEM((f, d), BF16),
                        pltpu.VMEM((X_RING, rows, w), xs.dtype), pltpu.SemaphoreType.DMA((X_RING,)),
                        pltpu.VMEM((2, d, f), wg.dtype), pltpu.VMEM((2, d, f), wu.dtype),
                        pltpu.VMEM((2, f, d), wd.dtype), pltpu.SemaphoreType.DMA((2,))],
    )
    return pl.pallas_call(
        _expert_kernel,
        grid_spec=grid_spec,
        out_shape=jax.ShapeDtypeStruct((p, w), jnp.uint32),
        compiler_params=pltpu.CompilerParams(dimension_semantics=("arbitrary",),
                                             vmem_limit_bytes=VMEM_LIMIT),
    )(plan, xs, wg, wu, wd)


def _gather_rows_sc(table, idx, chunk):
    m = idx.shape[0]
    w = table.shape[1]
    info = plsc.get_sparse_core_info()
    nc = info.num_cores
    per_worker = m // (nc * info.num_subcores)
    assert per_worker * nc * info.num_subcores == m and per_worker % chunk == 0
    mesh = plsc.VectorSubcoreMesh(core_axis_name="c", subcore_axis_name="s")

    @functools.partial(
        pl.kernel, mesh=mesh,
        out_type=jax.ShapeDtypeStruct((m, w), table.dtype),
        scratch_types=[pltpu.VMEM((chunk,), jnp.int32), pltpu.VMEM((chunk, w), table.dtype),
                       pltpu.SemaphoreType.DMA],
    )
    def gather(table_hbm, idx_hbm, out_hbm, idx_v, rows_v, sem):
        worker = lax.axis_index("s") * nc + lax.axis_index("c")

        @pl.loop(0, per_worker // chunk)
        def _(j):
            base = worker * per_worker + j * chunk
            pltpu.sync_copy(idx_hbm.at[pl.ds(base, chunk)], idx_v)
            pltpu.async_copy(table_hbm.at[idx_v], rows_v, sem).wait()
            pltpu.sync_copy(rows_v, out_hbm.at[pl.ds(base, chunk)])

    return gather(table, idx)


def _combine_kernel(yk_ref, gate_ref, base_ref, g2_ref, fg_ref, o_ref):
    t = base_ref.shape[0]
    half = yk_ref.shape[2]
    gates = jnp.concatenate([gate_ref[...], jnp.zeros((LANES - TOP_K, t), F32)], axis=0).T
    r_lo = jnp.zeros((t, half), F32)
    r_hi = jnp.zeros((t, half), F32)
    for k in range(TOP_K):
        y_lo, y_hi = _unpack_bf16_pair(yk_ref[k])
        r_lo = r_lo + gates[:, k:k + 1] * y_lo
        r_hi = r_hi + gates[:, k:k + 1] * y_hi
    g2 = g2_ref[0]
    x_lo = base_ref[:, :half] + g2[:, :half] * r_lo
    x_hi = base_ref[:, half:] + g2[:, half:] * r_hi
    ms = (jnp.sum(x_lo * x_lo, axis=-1, keepdims=True)
          + jnp.sum(x_hi * x_hi, axis=-1, keepdims=True)) * (1.0 / (2 * half))
    inv = lax.rsqrt(ms + EPS)
    o_ref[:, :half] = x_lo * inv * fg_ref[:, :half]
    o_ref[:, half:] = x_hi * inv * fg_ref[:, half:]


def _combine_call(yk, gates, base, g2, fg, tiles_per_batch, tile, tile0):
    n, d = base.shape
    row = lambda w: pl.BlockSpec((tile, w), lambda i: (i + tile0, 0))
    return pl.pallas_call(
        _combine_kernel,
        grid=(yk.shape[1] // tile,),
        in_specs=[pl.BlockSpec((TOP_K, tile, yk.shape[2]), lambda i: (0, i, 0)),
                  pl.BlockSpec((TOP_K, tile), lambda i: (0, i + tile0)), row(d),
                  pl.BlockSpec((1, 1, d), lambda i: ((i + tile0) // tiles_per_batch, 0, 0)),
                  pl.BlockSpec((1, d), lambda i: (0, 0))],
        out_specs=row(d),
        out_shape=jax.ShapeDtypeStruct((n, d), F32),
        input_output_aliases={2: 0},
        compiler_params=pltpu.CompilerParams(dimension_semantics=("arbitrary",),
                                             vmem_limit_bytes=VMEM_LIMIT),
    )(yk, gates, base, g2, fg)


def _pad_cols(a, width):
    return jnp.pad(a, ((0, 0), (0, width - a.shape[1])))


def _layer(x, mod, norm1_g, norm2_g, w_in, conv_w, conv_b, dt_bias, a_log, d_skip, ssm_norm_g,
           att_norm_g, sinks, rel_bias, w_out, router_w, router_bias, exp_w_gate, exp_w_up, exp_w_down,
           sh_w_gate, sh_w_up, sh_w_down, final_g):
    bsz, l, d = x.shape
    n = bsz * l
    tm = min(ROW_TILE, l)

    sh1, sc1, g1, sh2, sc2, g2 = [m[:, None, :] for m in jnp.split(mod, 6, axis=-1)]

    i1 = D_SSM
    i2 = i1 + CONV_CH
    i3 = i2 + SSM_HEADS
    i4 = i3 + D_ATT
    i5 = i4 + D_KV
    wz, wx, wdt, wq, wk, wv = jnp.split(w_in, [i1, i2, i3, i4, i5], axis=-1)
    wdt = _pad_cols(wdt, LANES)
    q_scale = ATT_HEAD_DIM ** -0.5
    assert math.frexp(q_scale)[0] == 0.5
    x2 = x.reshape(n, d)
    z, xbc, dt, q, k, v = _in_proj_call(x2, sc1, sh1, norm1_g[None, :], wz.astype(BF16), wx.astype(BF16),
                                        wdt.astype(BF16), (wq * q_scale).astype(BF16), wk.astype(BF16),
                                        wv.astype(BF16), l // tm, tm)

    triu = jnp.asarray(np.triu(np.ones((CHUNK, CHUNK), np.float32))).astype(BF16)
    shift = jnp.asarray(_conv_shift_matrix()).astype(BF16)
    y_ssm = _ssd_call(xbc.reshape(bsz, l, CONV_CH), z.reshape(bsz, l, D_SSM), dt.reshape(bsz, l, LANES),
                      conv_w, conv_b[None, :], dt_bias[:, None], a_log[:, None],
                      jnp.repeat(d_skip, SSM_HEAD_DIM)[None, :], ssm_norm_g[None, :], triu, shift)

    bias = _bias_call(rel_bias, jnp.asarray(_rel_bucket_table()))
    y_att = _attn_call(sinks, q.reshape(bsz, l, D_ATT), k.reshape(bsz, l, D_KV), v.reshape(bsz, l, D_KV), bias,
                       att_norm_g[None, :])

    rw = _pad_cols(router_w, LANES)
    rwh = rw.astype(BF16)
    rwl = jnp.concatenate([rwh, (rw - rwh.astype(F32)).astype(BF16)], axis=1)
    rs = min(RANK_SUB, tm)
    upper = jnp.asarray(np.triu(np.ones((rs, rs), np.float32), 1)).astype(BF16)
    base, h2, idx, gates, rank, counts = _out_proj_call(
        x2, y_ssm.reshape(n, D_SSM), y_att.reshape(n, D_ATT), g1, sc2, sh2, g2, norm2_g[None, :],
        w_out[:D_SSM].astype(BF16), w_out[D_SSM:].astype(BF16), rwh, rwl,
        sh_w_gate.astype(BF16), sh_w_up.astype(BF16), sh_w_down.astype(BF16), router_bias[:, None], upper,
        l // tm, tm)
    rt = min(ROUTE_TILE, n)

    nblocks = (n * TOP_K + N_EXPERTS * (EXPERT_ROWS - 1) + EXPERT_ROWS - 1) // EXPERT_ROWS
    pad_start, plan = _plan_call(counts, nblocks)
    dest = _dest_call(idx, rank, pad_start, rt)

    xs = _scatter_rows_sc(h2, dest.reshape(-1), nblocks * EXPERT_ROWS, SC_CHUNK)
    ys = _expert_call(plan, xs, exp_w_gate, exp_w_up, exp_w_down, EXPERT_ROWS)
    ctile = min(COMBINE_TILE, l)
    groups = COMBINE_GROUPS if bsz % COMBINE_GROUPS == 0 else 1
    ng = n // groups
    out = base
    for g in range(groups):
        idx_g = dest[:, g * ng // LANES:(g + 1) * ng // LANES, :].reshape(-1)
        yk = _gather_rows_sc(ys, idx_g, SC_CHUNK).reshape(TOP_K, ng, ys.shape[1])
        out = _combine_call(yk, gates, out, g2, final_g[None, :], l // ctile, ctile, g * ng // ctile)
    return out.reshape(bsz, l, d)


def kernel(x, c, mod_w, mod_b, norm1_g, norm2_g, w_in, conv_w, conv_b, dt_bias, a_log, d_skip, ssm_norm_g,
           att_norm_g, sinks, rel_bias, w_out, router_w, router_bias, exp_w_gate, exp_w_up, exp_w_down,
           sh_w_gate, sh_w_up, sh_w_down, final_g):
    assert mod_w.shape[0] == 1, "single-layer block"
    bsz = x.shape[0]
    c_pad = jnp.pad(c, ((0, SUBLANES - bsz % SUBLANES if bsz % SUBLANES else 0), (0, 0)))
    mod = _mod_call(c_pad, mod_w[0], mod_b[0][None, :])[:bsz]
    return _layer(x, mod, norm1_g[0], norm2_g[0], w_in[0], conv_w[0], conv_b[0], dt_bias[0], a_log[0], d_skip[0],
                  ssm_norm_g[0], att_norm_g[0], sinks[0], rel_bias, w_out[0], router_w[0], router_bias[0],
                  exp_w_gate[0], exp_w_up[0], exp_w_down[0], sh_w_gate[0], sh_w_up[0], sh_w_down[0], final_g)
```

```python
import functools
import math

import numpy as np
import jax
import jax.numpy as jnp
from jax import lax
from jax.experimental import pallas as pl
from jax.experimental.pallas import tpu as pltpu
from jax.experimental.pallas import tpu_sc as plsc

F32 = jnp.float32
BF16 = jnp.bfloat16

D_MODEL = 1024
SSM_HEAD_DIM = 64
D_SSM = D_MODEL
SSM_HEADS = D_SSM // SSM_HEAD_DIM
SSM_GROUPS = 4
D_STATE = 128
CONV_K = 4
CONV_CH = D_SSM + 2 * SSM_GROUPS * D_STATE
CHUNK = 128
ATT_HEAD_DIM = 64
D_ATT = D_MODEL
ATT_HEADS = D_ATT // ATT_HEAD_DIM
KV_HEADS = ATT_HEADS // 4
Q_PER_KV = ATT_HEADS // KV_HEADS
D_KV = KV_HEADS * ATT_HEAD_DIM
WINDOW = 128
ATT_BLOCK = 128
REL_BUCKETS = 32
REL_MAX_DIST = 128
N_EXPERTS = 64
TOP_K = 8
EXPERT_DIM = D_MODEL // 4
SHARED_DIM = D_MODEL // 4
ROUTE_GROUPS = 8
ROUTE_TOPK_GROUPS = 4
ROUTED_SCALE = 2.5
EPS = 1e-6

LANES = 128
SUBLANES = 8
HALF = LANES // 2

ROW_TILE = 512
ROUTE_TILE = 2048
RANK_SUB = 512
COMBINE_TILE = 256
SC_CHUNK = 128
EXPERT_ROWS = 512
COMBINE_GROUPS = 2
VMEM_LIMIT = 48 * 1024 * 1024

NEG_INF = float("-inf")


def _silu(v):
    return v * (1.0 / (1.0 + jnp.exp(-v)))


def _softplus(v):
    return jnp.maximum(v, 0.0) + jnp.log(1.0 + jnp.exp(-jnp.abs(v)))


def _bdot(a, b):
    return jnp.dot(a.astype(BF16), b.astype(BF16), preferred_element_type=F32)


def _split_hi_lo(v):
    hi = v.astype(BF16)
    lo = (v - hi.astype(F32)).astype(BF16)
    return hi, lo


def _pack_bf16_pair(a, b):
    w = pltpu.pack_elementwise([a, b], packed_dtype=BF16)
    return w if w.dtype == jnp.uint32 else lax.bitcast_convert_type(w, jnp.uint32)


def _unpack_bf16_pair(w):
    a = pltpu.unpack_elementwise(w, index=0, packed_dtype=BF16, unpacked_dtype=F32)
    b = pltpu.unpack_elementwise(w, index=1, packed_dtype=BF16, unpacked_dtype=F32)
    return a, b


def _lane_half_mask(shape):
    return lax.broadcasted_iota(jnp.int32, shape, len(shape) - 1) < HALF


def _mod_kernel(c_ref, w_ref, b_ref, o_ref):
    a = _silu(c_ref[...])
    o_ref[...] = jnp.dot(a, w_ref[...], precision=lax.Precision.HIGHEST,
                         preferred_element_type=F32) + b_ref[...]


def _mod_call(c_pad, mod_w, mod_b):
    rows, d = c_pad.shape
    cols = mod_w.shape[1]
    return pl.pallas_call(
        _mod_kernel,
        grid=(cols // d,),
        in_specs=[pl.BlockSpec((rows, d), lambda j: (0, 0)),
                  pl.BlockSpec((d, d), lambda j: (0, j)),
                  pl.BlockSpec((1, d), lambda j: (0, j))],
        out_specs=pl.BlockSpec((rows, d), lambda j: (0, j)),
        out_shape=jax.ShapeDtypeStruct((rows, cols), F32),
        compiler_params=pltpu.CompilerParams(dimension_semantics=("arbitrary",),
                                             vmem_limit_bytes=VMEM_LIMIT),
    )(c_pad, mod_w, mod_b)


IN_PROJ_SEGMENTS = ((D_SSM, BF16), (CONV_CH, BF16), (LANES, F32), (D_ATT, BF16), (D_KV, BF16), (D_KV, BF16))


def _in_proj_kernel(x_ref, sc_ref, sh_ref, g_ref, w_hbm, *refs):
    out_refs = refs[:len(IN_PROJ_SEGMENTS)]
    wraw_ref, w_ref, wsem = refs[len(IN_PROJ_SEGMENTS):]

    @pl.when(pl.program_id(0) == 0)
    def _():
        cp = pltpu.make_async_copy(w_hbm, wraw_ref, wsem)
        cp.start()
        cp.wait()
        src_dt = D_SSM + CONV_CH
        dst_q = src_dt + LANES
        w_ref[:, :src_dt] = wraw_ref[:, :src_dt].astype(BF16)
        dt_tile = wraw_ref[:, src_dt:src_dt + LANES]
        lane = lax.broadcasted_iota(jnp.int32, dt_tile.shape, 1)
        w_ref[:, src_dt:dst_q] = jnp.where(lane < SSM_HEADS, dt_tile, 0.0).astype(BF16)
        src_q = src_dt + SSM_HEADS
        w_ref[:, dst_q:dst_q + D_ATT] = (wraw_ref[:, src_q:src_q + D_ATT] * (ATT_HEAD_DIM ** -0.5)).astype(BF16)
        w_ref[:, dst_q + D_ATT:] = wraw_ref[:, src_q + D_ATT:].astype(BF16)

    xf = x_ref[...]
    ms = jnp.mean(xf * xf, axis=-1, keepdims=True)
    h = xf * lax.rsqrt(ms + EPS) * g_ref[...]
    h = h * (1.0 + sc_ref[0]) + sh_ref[0]
    hb = h.astype(BF16)
    col = 0
    for (width, dtype), o_ref in zip(IN_PROJ_SEGMENTS, out_refs):
        o_ref[...] = jnp.dot(hb, w_ref[:, col:col + width], preferred_element_type=F32).astype(dtype)
        col += width


def _in_proj_call(x2, sc1, sh1, g1n, w_in, tiles_per_batch, tm):
    n, d = x2.shape
    row = lambda w: pl.BlockSpec((tm, w), lambda i: (i, 0))
    full = lambda a: pl.BlockSpec(a.shape, lambda i: (0, 0))
    per_batch = pl.BlockSpec((1, 1, d), lambda i: (i // tiles_per_batch, 0, 0))
    cols = sum(w for w, _ in IN_PROJ_SEGMENTS)
    assert w_in.shape[1] == cols - (LANES - SSM_HEADS)
    return pl.pallas_call(
        _in_proj_kernel,
        grid=(n // tm,),
        in_specs=[row(d), per_batch, per_batch, full(g1n), pl.BlockSpec(memory_space=pl.ANY)],
        out_specs=[row(w) for w, _ in IN_PROJ_SEGMENTS],
        out_shape=[jax.ShapeDtypeStruct((n, w), dt) for w, dt in IN_PROJ_SEGMENTS],
        scratch_shapes=[pltpu.VMEM(w_in.shape, w_in.dtype), pltpu.VMEM((d, cols), BF16), pltpu.SemaphoreType.DMA(())],
        compiler_params=pltpu.CompilerParams(dimension_semantics=("arbitrary",),
                                             vmem_limit_bytes=VMEM_LIMIT),
    )(x2, sc1, sh1, g1n, w_in)


SSD_SEQS = 4
CONV_HALO = 16


def _conv_shift_matrix():
    s = np.zeros((CONV_K * CHUNK, CONV_HALO + CHUNK), np.float32)
    for k in range(CONV_K):
        t = np.arange(CHUNK)
        s[k * CHUNK + t, CONV_HALO + t - (CONV_K - 1) + k] = 1.0
    return s


def _silu_tanh(v):
    hv = 0.5 * v
    return hv + hv * jnp.tanh(hv)


def _ssd_kernel(xbc_ref, z_ref, dt_ref, cw_ref, cb_ref, dtb_ref, alog_ref, dskip_ref, ng_ref, triu_ref, shift_ref,
                y_ref, state_ref, ucat_ref, ybuf_ref):
    nseq = xbc_ref.shape[0]

    @pl.when(pl.program_id(1) == 0)
    def _():
        state_ref[...] = jnp.zeros_like(state_ref)
        ucat_ref[:, 0:CONV_HALO, :] = jnp.zeros((nseq, CONV_HALO, CONV_CH), BF16)

    for q in range(nseq):
        _ssd_chunk(xbc_ref.at[q], z_ref.at[q], dt_ref.at[q], cw_ref, cb_ref, dtb_ref, alog_ref, dskip_ref, ng_ref,
                   triu_ref, shift_ref, y_ref.at[q], state_ref.at[q], ucat_ref.at[q], ybuf_ref.at[q])


def _ssd_chunk(xbc_ref, z_ref, dt_ref, cw_ref, cb_ref, dtb_ref, alog_ref, dskip_ref, ng_ref, triu_ref, shift_ref,
               y_ref, state_ref, ucat_ref, ybuf_ref):
    ucat_ref[CONV_HALO:, :] = xbc_ref[...]
    shifted = jnp.dot(shift_ref[...], ucat_ref[...], preferred_element_type=F32)
    ucat_ref[0:CONV_HALO, :] = ucat_ref[CHUNK:CHUNK + CONV_HALO, :]
    acc = cb_ref[...] + cw_ref[0:1, :] * shifted[0:CHUNK]
    for kk in range(1, CONV_K):
        acc = acc + cw_ref[kk:kk + 1, :] * shifted[kk * CHUNK:(kk + 1) * CHUNK]
    act = _silu_tanh(acc)
    xs = act[:, :D_SSM]
    gn = SSM_GROUPS * D_STATE

    dt_t = _softplus(dt_ref[...].T[0:SSM_HEADS, :] + dtb_ref[...])
    a_t = dt_t * (-jnp.exp(alog_ref[...]))
    a_hi = a_t.astype(BF16)
    a_mid = (a_t - a_hi.astype(F32)).astype(BF16)
    a_lo = (a_t - a_hi.astype(F32) - a_mid.astype(F32)).astype(BF16)
    triu = triu_ref[...]
    cs_t = (jnp.dot(a_hi, triu, preferred_element_type=F32) + jnp.dot(a_mid, triu, preferred_element_type=F32)
            + jnp.dot(a_lo, triu, preferred_element_type=F32))
    cs_end = cs_t[:, CHUNK - 1:CHUNK]
    r_t = cs_t - jnp.log(dt_t)
    w_t = jnp.exp(cs_end - cs_t) * dt_t
    chunk_decay = jnp.exp(cs_end)
    cols = jnp.concatenate([cs_t, jnp.exp(cs_t), jnp.zeros((LANES - 2 * SSM_HEADS, CHUNK), F32)], axis=0).T

    li = lax.broadcasted_iota(jnp.int32, (CHUNK, CHUNK), 0)
    si = lax.broadcasted_iota(jnp.int32, (CHUNK, CHUNK), 1)
    causal = li >= si
    low = _lane_half_mask((CHUNK, LANES))
    low_row = _lane_half_mask((1, LANES))

    heads_per_group = SSM_HEADS // SSM_GROUPS
    for g in range(SSM_GROUPS):
        b_g = act[:, D_SSM + g * D_STATE:D_SSM + (g + 1) * D_STATE]
        c_g = act[:, D_SSM + gn + g * D_STATE:D_SSM + gn + (g + 1) * D_STATE]
        b_gb = b_g.astype(BF16)
        c_gb = c_g.astype(BF16)
        cb = lax.dot_general(c_gb, b_gb, (((1,), (1,)), ((), ())), preferred_element_type=F32)
        b_t = b_g.T
        for jp in range(heads_per_group // 2):
            j = g * (heads_per_group // 2) + jp
            lanes = slice(j * LANES, (j + 1) * LANES)
            xp = xs[:, lanes]
            xpb = xp.astype(BF16)
            ydiag = jnp.zeros((CHUNK, LANES), F32)
            snew = jnp.zeros((D_STATE, LANES), F32)
            for half in range(2):
                h = 2 * j + half
                diff = cols[:, h:h + 1] - r_t[h:h + 1, :]
                m = (cb * jnp.exp(jnp.where(causal, diff, NEG_INF))).astype(BF16)
                keep = low if half == 0 else jnp.logical_not(low)
                xh = jnp.where(keep, xpb, jnp.zeros_like(xpb))
                ydiag = ydiag + jnp.dot(m, xh, preferred_element_type=F32)
                snew = snew + jnp.dot((b_t * w_t[h:h + 1, :]).astype(BF16), xh, preferred_element_type=F32)
            s_in = state_ref[:, lanes]
            yoff = jnp.dot(c_gb, s_in.astype(BF16), preferred_element_type=F32)
            h0 = 2 * j
            e0 = SSM_HEADS + h0
            escale = jnp.where(low, cols[:, e0:e0 + 1], cols[:, e0 + 1:e0 + 2])
            cdec = jnp.where(low_row, chunk_decay[h0:h0 + 1, :], chunk_decay[h0 + 1:h0 + 2, :])
            ybuf_ref[:, lanes] = ydiag + yoff * escale + xp * dskip_ref[:, lanes]
            state_ref[:, lanes] = s_in * cdec + snew

    yz = ybuf_ref[...] * _silu_tanh(z_ref[...].astype(F32))
    gw = D_SSM // SSM_GROUPS
    for g in range(SSM_GROUPS):
        part = yz[:, g * gw:(g + 1) * gw]
        ms = jnp.mean(part * part, axis=-1, keepdims=True)
        y_ref[:, g * gw:(g + 1) * gw] = (part * lax.rsqrt(ms + EPS)
                                            * ng_ref[:, g * gw:(g + 1) * gw]).astype(BF16)


def _ssd_call(xbc, z, dt, conv_w, conv_b, dtb, alog, dskip, ng, triu, shift):
    bsz, l, _ = xbc.shape
    nc = l // CHUNK
    nseq = SSD_SEQS if bsz % SSD_SEQS == 0 else 1
    chunk = lambda w: pl.BlockSpec((nseq, CHUNK, w), lambda b, c: (b, c, 0))
    full = lambda a: pl.BlockSpec(a.shape, lambda b, c: (0, 0))
    return pl.pallas_call(
        _ssd_kernel,
        grid=(bsz // nseq, nc),
        in_specs=[chunk(CONV_CH), chunk(D_SSM), chunk(LANES), full(conv_w), full(conv_b), full(dtb),
                  full(alog), full(dskip), full(ng), full(triu), full(shift)],
        out_specs=chunk(D_SSM),
        out_shape=jax.ShapeDtypeStruct((bsz, l, D_SSM), BF16),
        scratch_shapes=[pltpu.VMEM((nseq, D_STATE, D_SSM), F32),
                        pltpu.VMEM((nseq, CONV_HALO + CHUNK, CONV_CH), BF16),
                        pltpu.VMEM((nseq, CHUNK, D_SSM), F32)],
        compiler_params=pltpu.CompilerParams(dimension_semantics=("arbitrary", "arbitrary"),
                                             vmem_limit_bytes=VMEM_LIMIT),
    )(xbc, z, dt, conv_w, conv_b, dtb, alog, dskip, ng, triu, shift)


assert WINDOW == ATT_BLOCK


def _rel_bucket_table():
    qi = np.arange(ATT_BLOCK)[:, None]
    c = np.arange(ATT_BLOCK)[None, :]
    dist = np.where(c > qi, qi + ATT_BLOCK - c, qi - c)
    max_exact = REL_BUCKETS // 2
    d = np.maximum(dist, 1).astype(np.float32)
    large = max_exact + (np.log(d / np.float32(max_exact)) / np.float32(math.log(REL_MAX_DIST / max_exact))
                         * np.float32(REL_BUCKETS - max_exact)).astype(np.int32)
    large = np.minimum(large, REL_BUCKETS - 1)
    return np.where(dist < max_exact, dist, large).astype(np.int32)


def _bias_kernel(rb_ref, bucket_ref, o_ref):
    bucket = bucket_ref[...]
    from_prev = (lax.broadcasted_iota(jnp.int32, bucket.shape, 1)
                 > lax.broadcasted_iota(jnp.int32, bucket.shape, 0))
    for h in range(ATT_HEADS):
        acc = jnp.zeros(bucket.shape, F32)
        for b in range(REL_BUCKETS):
            acc = jnp.where(bucket == b, rb_ref[b, h], acc)
        o_ref[1, h] = acc
        o_ref[0, h] = jnp.where(from_prev, NEG_INF, acc)


def _bias_call(rel_bias, bucket):
    return pl.pallas_call(
        _bias_kernel,
        in_specs=[pl.BlockSpec(memory_space=pltpu.SMEM), pl.BlockSpec(memory_space=pltpu.VMEM)],
        out_shape=jax.ShapeDtypeStruct((2, ATT_HEADS) + bucket.shape, F32),
    )(rel_bias, bucket)


ATT_SEQS = 4


def _attn_kernel(sink_ref, q_ref, kp_ref, kc_ref, vp_ref, vc_ref, bias_ref, ng_ref, o_ref, obuf_ref):
    for s in range(q_ref.shape[0]):
        _attn_block(sink_ref, q_ref.at[s], kp_ref.at[s], kc_ref.at[s], vp_ref.at[s], vc_ref.at[s], bias_ref.at[0],
                    ng_ref, o_ref.at[s], obuf_ref.at[s])


def _attn_block(sink_ref, q_ref, kp_ref, kc_ref, vp_ref, vc_ref, bias_ref, ng_ref, o_ref, obuf_ref):
    qi = lax.broadcasted_iota(jnp.int32, (ATT_BLOCK, ATT_BLOCK), 0)
    ci = lax.broadcasted_iota(jnp.int32, (ATT_BLOCK, ATT_BLOCK), 1)
    from_prev = ci > qi
    low = _lane_half_mask((ATT_BLOCK, LANES))

    def band_variants(prev_ref, cur_ref):
        out = []
        for cpair in range(KV_HEADS // 2):
            lanes = slice(cpair * LANES, (cpair + 1) * LANES)
            t = jnp.concatenate([prev_ref[:, lanes], cur_ref[:, lanes]], axis=0).astype(F32)
            out.append((t.astype(BF16), pltpu.roll(t, HALF, 1).astype(BF16)))
        return out

    k_band = band_variants(kp_ref, kc_ref)
    v_band = band_variants(vp_ref, vc_ref)
    nt = (((1,), (1,)), ((), ()))

    for j in range(ATT_HEADS // 2):
        qp = q_ref[:, j * LANES:(j + 1) * LANES]
        out_pair = jnp.zeros((ATT_BLOCK, LANES), F32)
        for half in range(2):
            h = 2 * j + half
            g = h // Q_PER_KV
            swapped = int((g % 2) != half)
            keep = low if half == 0 else jnp.logical_not(low)
            qh = jnp.where(keep, qp, jnp.zeros_like(qp))
            s_band = lax.dot_general(qh, k_band[g // 2][swapped], nt, preferred_element_type=F32)
            s = jnp.where(from_prev, s_band[:, :ATT_BLOCK], s_band[:, ATT_BLOCK:]) + bias_ref[h]
            sink = sink_ref[h]
            m = jnp.maximum(jnp.max(s, axis=-1, keepdims=True), sink)
            p = jnp.exp(s - m)
            denom = jnp.sum(p, axis=-1, keepdims=True) + jnp.exp(sink - m)
            p_band = jnp.concatenate([jnp.where(from_prev, p, 0.0), jnp.where(from_prev, 0.0, p)], axis=1)
            o = jnp.dot(p_band.astype(BF16), v_band[g // 2][swapped], preferred_element_type=F32) / denom
            out_pair = out_pair + jnp.where(keep, o, 0.0)
        obuf_ref[:, j * LANES:(j + 1) * LANES] = out_pair

    att = obuf_ref[...]
    ms = jnp.mean(att * att, axis=-1, keepdims=True)
    o_ref[...] = (att * lax.rsqrt(ms + EPS) * ng_ref[...]).astype(BF16)


def _attn_call(sinks, q, k, v, bias, ng):
    bsz, l, _ = q.shape
    nb = l // ATT_BLOCK
    nseq = ATT_SEQS if bsz % ATT_SEQS == 0 else 1
    cur = lambda w: pl.BlockSpec((nseq, ATT_BLOCK, w), lambda b, i: (b, i, 0))
    prev = lambda w: pl.BlockSpec((nseq, ATT_BLOCK, w), lambda b, i: (b, jnp.maximum(i - 1, 0), 0))
    return pl.pallas_call(
        _attn_kernel,
        grid=(bsz // nseq, nb),
        in_specs=[pl.BlockSpec(memory_space=pltpu.SMEM),
                  cur(D_ATT), prev(D_KV), cur(D_KV), prev(D_KV), cur(D_KV),
                  pl.BlockSpec((1,) + bias.shape[1:], lambda b, i: (jnp.minimum(i, 1), 0, 0, 0)),
                  pl.BlockSpec(ng.shape, lambda b, i: (0, 0))],
        out_specs=cur(D_ATT),
        out_shape=jax.ShapeDtypeStruct((bsz, l, D_ATT), BF16),
        scratch_shapes=[pltpu.VMEM((nseq, ATT_BLOCK, D_ATT), F32)],
        compiler_params=pltpu.CompilerParams(dimension_semantics=("arbitrary", "arbitrary"),
                                             vmem_limit_bytes=VMEM_LIMIT),
    )(sinks, q, k, k, v, v, bias, ng)


def _out_proj_kernel(x_ref, ys_ref, ya_ref, g1_ref, sc_ref, sh_ref, g2_ref, ng_ref, wos_ref, woa_ref,
                     rwh_ref, rwl_ref, sg_ref, su_ref, sd_ref, rb_ref, upper_ref,
                     base_ref, h_ref, idx_ref, gate_ref, rank_ref, cnt_ref, carry_ref):
    @pl.when(pl.program_id(0) == 0)
    def _():
        carry_ref[...] = jnp.zeros_like(carry_ref)

    mix = (jnp.dot(ys_ref[...], wos_ref[...], preferred_element_type=F32)
           + jnp.dot(ya_ref[...], woa_ref[...], preferred_element_type=F32))
    x1 = x_ref[...] + g1_ref[0] * mix
    ms = jnp.mean(x1 * x1, axis=-1, keepdims=True)
    h = x1 * lax.rsqrt(ms + EPS) * ng_ref[...]
    h = h * (1.0 + sc_ref[0]) + sh_ref[0]
    half = h.shape[1] // 2
    h_ref[...] = _pack_bf16_pair(h[:, :half], h[:, half:])
    hi, lo = _split_hi_lo(h)
    hi_terms = jnp.dot(hi, rwl_ref[...], preferred_element_type=F32)
    logits = (hi_terms[:, :LANES] + hi_terms[:, LANES:]
              + jnp.dot(lo, rwh_ref[...], preferred_element_type=F32))
    _route_tokens(logits.T[0:N_EXPERTS, :], rb_ref, upper_ref, idx_ref, gate_ref, rank_ref, cnt_ref, carry_ref)
    u = _silu(jnp.dot(hi, sg_ref[...], preferred_element_type=F32)) * jnp.dot(hi, su_ref[...],
                                                                              preferred_element_type=F32)
    shared = jnp.dot(u.astype(BF16), sd_ref[...], preferred_element_type=F32)
    base_ref[...] = x1 + g2_ref[0] * shared


def _out_proj_call(x2, ys, ya, g1, sc2, sh2, g2, ng, wos, woa, rwh, rwl, sg, su, sd, router_bias, upper,
                   tiles_per_batch, tm):
    n, d = x2.shape
    row = lambda w: pl.BlockSpec((tm, w), lambda i: (i, 0))
    tok = lambda r: pl.BlockSpec((r, tm), lambda i: (0, i))
    full = lambda a: pl.BlockSpec(a.shape, lambda i: (0, 0))
    per_batch = pl.BlockSpec((1, 1, d), lambda i: (i // tiles_per_batch, 0, 0))
    return pl.pallas_call(
        _out_proj_kernel,
        grid=(n // tm,),
        in_specs=[row(d), row(D_SSM), row(D_ATT), per_batch, per_batch, per_batch, per_batch, full(ng),
                  full(wos), full(woa), full(rwh), full(rwl), full(sg), full(su), full(sd), full(router_bias),
                  full(upper)],
        out_specs=[row(d), row(d // 2), tok(TOP_K), tok(TOP_K), tok(TOP_K),
                   pl.BlockSpec((N_EXPERTS, LANES), lambda i: (0, 0))],
        out_shape=[jax.ShapeDtypeStruct((n, d), F32), jax.ShapeDtypeStruct((n, d // 2), jnp.uint32),
                   jax.ShapeDtypeStruct((TOP_K, n), jnp.int32), jax.ShapeDtypeStruct((TOP_K, n), F32),
                   jax.ShapeDtypeStruct((TOP_K, n), jnp.int32), jax.ShapeDtypeStruct((N_EXPERTS, LANES), F32)],
        scratch_shapes=[pltpu.VMEM((N_EXPERTS, LANES), F32)],
        compiler_params=pltpu.CompilerParams(dimension_semantics=("arbitrary",),
                                             vmem_limit_bytes=VMEM_LIMIT),
    )(x2, ys, ya, g1, sc2, sh2, g2, ng, wos, woa, rwh, rwl, sg, su, sd, router_bias, upper)


def _route_tokens(logits_t, rb_ref, upper_ref, idx_ref, gate_ref, rank_ref, cnt_ref, carry_ref):
    t = logits_t.shape[1]
    per_group = N_EXPERTS // ROUTE_GROUPS
    scores = 1.0 / (1.0 + jnp.exp(-logits_t))
    sel = scores + rb_ref[...]
    e_iota = lax.broadcasted_iota(jnp.int32, (N_EXPERTS, t), 0)

    sel3 = sel.reshape(ROUTE_GROUPS, per_group, t)
    w_iota = lax.broadcasted_iota(jnp.int32, sel3.shape, 1)
    m1 = jnp.max(sel3, axis=1, keepdims=True)
    first = jnp.min(jnp.where(sel3 == m1, w_iota, per_group), axis=1, keepdims=True)
    m2 = jnp.max(jnp.where(w_iota == first, NEG_INF, sel3), axis=1, keepdims=True)
    grp = (m1 + m2).reshape(ROUTE_GROUPS, t)

    g_iota = lax.broadcasted_iota(jnp.int32, (ROUTE_GROUPS, t), 0)
    gmask = jnp.zeros((ROUTE_GROUPS, t), jnp.bool_)
    for _ in range(ROUTE_TOPK_GROUPS):
        gm = jnp.max(grp, axis=0, keepdims=True)
        gfirst = jnp.min(jnp.where(grp == gm, g_iota, ROUTE_GROUPS), axis=0, keepdims=True)
        hit = g_iota == gfirst
        gmask = jnp.logical_or(gmask, hit)
        grp = jnp.where(hit, NEG_INF, grp)
    allowed = jnp.broadcast_to(gmask.reshape(ROUTE_GROUPS, 1, t),
                               (ROUTE_GROUPS, per_group, t)).reshape(N_EXPERTS, t)
    masked = jnp.where(allowed, sel, NEG_INF)

    picked = jnp.zeros((N_EXPERTS, t), jnp.bool_)
    idx_rows = []
    w_rows = []
    for _ in range(TOP_K):
        mm = jnp.max(masked, axis=0, keepdims=True)
        efirst = jnp.min(jnp.where(masked == mm, e_iota, N_EXPERTS), axis=0, keepdims=True)
        hit = e_iota == efirst
        idx_rows.append(efirst)
        w_rows.append(jnp.sum(jnp.where(hit, scores, 0.0), axis=0, keepdims=True))
        picked = jnp.logical_or(picked, hit)
        masked = jnp.where(hit, NEG_INF, masked)
    idx = jnp.concatenate(idx_rows, axis=0)
    w = jnp.concatenate(w_rows, axis=0)
    gate_ref[...] = w / jnp.sum(w, axis=0, keepdims=True) * ROUTED_SCALE
    idx_ref[...] = idx

    onehot = jnp.where(picked, 1.0, 0.0)
    sub = upper_ref.shape[0]
    carry = carry_ref[:, 0:1]
    parts = []
    for s0 in range(0, t, sub):
        oh = onehot[:, s0:s0 + sub]
        parts.append(jnp.dot(oh.astype(BF16), upper_ref[...], preferred_element_type=F32) + carry)
        carry = carry + jnp.sum(oh, axis=1, keepdims=True)
    rank_full = jnp.concatenate(parts, axis=1)
    rank_rows = [jnp.sum(jnp.where(e_iota == idx_rows[k], rank_full, 0.0), axis=0, keepdims=True)
                 for k in range(TOP_K)]
    rank_ref[...] = jnp.concatenate(rank_rows, axis=0).astype(jnp.int32)
    carry_ref[...] = jnp.broadcast_to(carry, carry_ref.shape)
    cnt_ref[...] = carry_ref[...]


PLAN_EXPERT, PLAN_FRESH, PLAN_VALID, PLAN_SEG, PLAN_NEXT, PLAN_NUSED = range(6)
PLAN_ROWS = SUBLANES


def _plan_kernel(cnt_ref, tri_ref, start_ref, plan_ref, *, nblocks):
    nbp = plan_ref.shape[1]
    cnt = cnt_ref[...].astype(jnp.int32)
    blocks = (cnt + (EXPERT_ROWS - 1)) // EXPERT_ROWS
    end = jnp.dot(tri_ref[...], blocks.astype(F32), precision=lax.Precision.HIGHEST,
                  preferred_element_type=F32).astype(jnp.int32)
    start = end - blocks
    start_ref[...] = start[:, 0:1] * EXPERT_ROWS
    nused = end[N_EXPERTS - 1:N_EXPERTS, 0:1]

    e_iota = lax.broadcasted_iota(jnp.int32, (N_EXPERTS, nbp), 0)
    blk = lax.broadcasted_iota(jnp.int32, (1, nbp), 1)
    expert = jnp.minimum(jnp.sum((end[:, 0:1] <= blk).astype(jnp.int32), axis=0, keepdims=True), N_EXPERTS - 1)
    mine = e_iota == expert
    pick = lambda col: jnp.sum(jnp.where(mine, col, 0), axis=0, keepdims=True)
    first = pick(start[:, 0:1])
    valid = jnp.clip(pick(cnt[:, 0:1]) - (blk - first) * EXPERT_ROWS, 0, EXPERT_ROWS)
    present = jnp.logical_or(blocks[:, 0:1] > 0,
                             jnp.logical_and(e_iota[:, 0:1] == N_EXPERTS - 1, nused < nblocks))
    seg = jnp.sum(jnp.logical_and(present, e_iota <= expert).astype(jnp.int32), axis=0, keepdims=True) - 1
    nxt = jnp.min(jnp.where(jnp.logical_and(present, e_iota > expert), e_iota, N_EXPERTS), axis=0, keepdims=True)
    rows = {PLAN_EXPERT: expert, PLAN_FRESH: (blk == first).astype(jnp.int32), PLAN_VALID: valid, PLAN_SEG: seg,
            PLAN_NEXT: jnp.where(nxt == N_EXPERTS, -1, nxt), PLAN_NUSED: jnp.broadcast_to(nused, (1, nbp))}
    zero = jnp.zeros((1, nbp), jnp.int32)
    plan_ref[...] = jnp.concatenate([rows.get(r, zero) for r in range(PLAN_ROWS)], axis=0)


def _plan_call(counts, nblocks):
    nbp = -(-nblocks // LANES) * LANES
    tri = jnp.asarray(np.tril(np.ones((N_EXPERTS, N_EXPERTS), np.float32)))
    return pl.pallas_call(
        functools.partial(_plan_kernel, nblocks=nblocks),
        out_shape=[jax.ShapeDtypeStruct((N_EXPERTS, 1), jnp.int32),
                   jax.ShapeDtypeStruct((PLAN_ROWS, nbp), jnp.int32)],
    )(counts, tri)


def _dest_kernel(idx_ref, rank_ref, start_ref, dest_ref):
    t = idx_ref.shape[1]
    e_iota = lax.broadcasted_iota(jnp.int32, (N_EXPERTS, t), 0)
    rows = [jnp.sum(jnp.where(e_iota == idx_ref[k:k + 1, :], start_ref[...], 0), axis=0, keepdims=True)
            for k in range(TOP_K)]
    dest = jnp.concatenate(rows, axis=0) + rank_ref[...]
    for k in range(TOP_K):
        for c in range(t // LANES):
            dest_ref[k, c:c + 1, :] = dest[k:k + 1, c * LANES:(c + 1) * LANES]


def _dest_call(idx, rank, pad_start, tile):
    n = idx.shape[1]
    tok = pl.BlockSpec((TOP_K, tile), lambda i: (0, i))
    return pl.pallas_call(
        _dest_kernel,
        grid=(n // tile,),
        in_specs=[tok, tok, pl.BlockSpec((N_EXPERTS, 1), lambda i: (0, 0))],
        out_specs=pl.BlockSpec((TOP_K, tile // LANES, LANES), lambda i: (0, i, 0)),
        out_shape=jax.ShapeDtypeStruct((TOP_K, n // LANES, LANES), jnp.int32),
        compiler_params=pltpu.CompilerParams(dimension_semantics=("arbitrary",)),
    )(idx, rank, pad_start)


def _scatter_rows_sc(rows, dest_flat, total_rows, chunk):
    n, w = rows.shape
    copies = dest_flat.shape[0] // n
    info = plsc.get_sparse_core_info()
    nc = info.num_cores
    per_worker = n // (nc * info.num_subcores)
    assert per_worker * nc * info.num_subcores == n and per_worker % chunk == 0
    mesh = plsc.VectorSubcoreMesh(core_axis_name="c", subcore_axis_name="s")

    @functools.partial(
        pl.kernel, mesh=mesh,
        out_type=jax.ShapeDtypeStruct((total_rows, w), rows.dtype),
        scratch_types=[pltpu.VMEM((chunk,), jnp.int32), pltpu.VMEM((chunk, w), rows.dtype),
                       pltpu.SemaphoreType.DMA],
    )
    def scatter(rows_hbm, idx_hbm, out_hbm, idx_v, rows_v, sem):
        worker = lax.axis_index("s") * nc + lax.axis_index("c")

        @pl.loop(0, per_worker // chunk)
        def _(j):
            base = worker * per_worker + j * chunk
            pltpu.sync_copy(rows_hbm.at[pl.ds(base, chunk)], rows_v)
            for k in range(copies):
                pltpu.sync_copy(idx_hbm.at[pl.ds(k * n + base, chunk)], idx_v)
                pltpu.async_copy(rows_v, out_hbm.at[idx_v], sem).wait()

    return scatter(rows, dest_flat)


X_RING = 3


def _expert_kernel(plan_ref, xs_hbm, wg_hbm, wu_hbm, wd_hbm, y_ref,
                   wgb_ref, wub_ref, wdb_ref, xbuf_ref, xsem, wg_raw, wu_raw, wd_raw, wsem):
    i = pl.program_id(0)
    rows, half = xbuf_ref.shape[1], xbuf_ref.shape[2]
    nused = plan_ref[PLAN_NUSED, 0]

    def w_copies(expert, slot):
        return [pltpu.make_async_copy(src.at[expert], dst.at[slot], wsem.at[slot])
                for src, dst in ((wg_hbm, wg_raw), (wu_hbm, wu_raw), (wd_hbm, wd_raw))]

    def x_copy(block):
        first = pl.multiple_of(block * rows, rows)
        slot = block % X_RING
        return pltpu.make_async_copy(xs_hbm.at[pl.ds(first, rows)], xbuf_ref.at[slot], xsem.at[slot])

    @pl.when(i == 0)
    def _():
        for b in range(X_RING - 1):
            @pl.when(b < nused)
            def _():
                x_copy(b).start()

    @pl.when(i + (X_RING - 1) < nused)
    def _():
        x_copy(i + (X_RING - 1)).start()

    @pl.when(i == 0)
    def _():
        for c in w_copies(plan_ref[PLAN_EXPERT, 0], 0):
            c.start()

    @pl.when(plan_ref[PLAN_FRESH, i] > 0)
    def _():
        slot = plan_ref[PLAN_SEG, i] % 2
        for c in w_copies(plan_ref[PLAN_EXPERT, i], slot):
            c.wait()

        @pl.when(plan_ref[PLAN_NEXT, i] >= 0)
        def _():
            for c in w_copies(plan_ref[PLAN_NEXT, i], 1 - slot):
                c.start()

        wgb_ref[...] = wg_raw[slot].astype(BF16)
        wub_ref[...] = wu_raw[slot].astype(BF16)
        wdb_ref[...] = wd_raw[slot].astype(BF16)

    @pl.when(i < nused)
    def _():
        x_copy(i).wait()
        xw = xbuf_ref[i % X_RING]
        row = lax.broadcasted_iota(jnp.int32, xw.shape, 0)
        x_lo, x_hi = _unpack_bf16_pair(jnp.where(row < plan_ref[PLAN_VALID, i], xw, jnp.uint32(0)))
        x_lo = x_lo.astype(BF16)
        x_hi = x_hi.astype(BF16)
        gate = (jnp.dot(x_lo, wgb_ref[:half, :], preferred_element_type=F32)
                + jnp.dot(x_hi, wgb_ref[half:, :], preferred_element_type=F32))
        up = (jnp.dot(x_lo, wub_ref[:half, :], preferred_element_type=F32)
              + jnp.dot(x_hi, wub_ref[half:, :], preferred_element_type=F32))
        u = (_silu(gate) * up).astype(BF16)
        y_lo = jnp.dot(u, wdb_ref[:, :half], preferred_element_type=F32)
        y_hi = jnp.dot(u, wdb_ref[:, half:], preferred_element_type=F32)
        y_ref[...] = _pack_bf16_pair(y_lo, y_hi)

    @pl.when(i >= nused)
    def _():
        y_ref[...] = jnp.zeros_like(y_ref)


def _expert_call(plan, xs, wg, wu, wd, rows):
    p, w = xs.shape
    d, f = wg.shape[1], wg.shape[2]
    hbm = pl.BlockSpec(memory_space=pl.ANY)
    grid_spec = pltpu.PrefetchScalarGridSpec(
        num_scalar_prefetch=1,
        grid=(p // rows,),
        in_specs=[hbm, hbm, hbm, hbm],
        out_specs=pl.BlockSpec((rows, w), lambda i, *_: (i, 0)),
        scratch_shapes=[pltpu.VMEM((d, f), BF16), pltpu.VMEM((d, f), BF16), pltpu.VMEM((f, d), BF16),
                        pltpu.VMEM((X_RING, rows, w), xs.dtype), pltpu.SemaphoreType.DMA((X_RING,)),
                        pltpu.VMEM((2, d, f), wg.dtype), pltpu.VMEM((2, d, f), wu.dtype),
                        pltpu.VMEM((2, f, d), wd.dtype), pltpu.SemaphoreType.DMA((2,))],
    )
    return pl.pallas_call(
        _expert_kernel,
        grid_spec=grid_spec,
        out_shape=jax.ShapeDtypeStruct((p, w), jnp.uint32),
        compiler_params=pltpu.CompilerParams(dimension_semantics=("arbitrary",),
                                             vmem_limit_bytes=VMEM_LIMIT),
    )(plan, xs, wg, wu, wd)


def _gather_rows_sc(table, idx, chunk):
    m = idx.shape[0]
    w = table.shape[1]
    info = plsc.get_sparse_core_info()
    nc = info.num_cores
    per_worker = m // (nc * info.num_subcores)
    assert per_worker * nc * info.num_subcores == m and per_worker % chunk == 0
    mesh = plsc.VectorSubcoreMesh(core_axis_name="c", subcore_axis_name="s")

    @functools.partial(
        pl.kernel, mesh=mesh,
        out_type=jax.ShapeDtypeStruct((m, w), table.dtype),
        scratch_types=[pltpu.VMEM((chunk,), jnp.int32), pltpu.VMEM((chunk, w), table.dtype),
                       pltpu.SemaphoreType.DMA],
    )
    def gather(table_hbm, idx_hbm, out_hbm, idx_v, rows_v, sem):
        worker = lax.axis_index("s") * nc + lax.axis_index("c")

        @pl.loop(0, per_worker // chunk)
        def _(j):
            base = worker * per_worker + j * chunk
            pltpu.sync_copy(idx_hbm.at[pl.ds(base, chunk)], idx_v)
            pltpu.async_copy(table_hbm.at[idx_v], rows_v, sem).wait()
            pltpu.sync_copy(rows_v, out_hbm.at[pl.ds(base, chunk)])

    return gather(table, idx)


def _combine_kernel(yk_ref, gate_ref, base_ref, g2_ref, fg_ref, o_ref):
    t = base_ref.shape[0]
    half = yk_ref.shape[2]
    gates = jnp.concatenate([gate_ref[...], jnp.zeros((LANES - TOP_K, t), F32)], axis=0).T
    r_lo = jnp.zeros((t, half), F32)
    r_hi = jnp.zeros((t, half), F32)
    for k in range(TOP_K):
        y_lo, y_hi = _unpack_bf16_pair(yk_ref[k])
        r_lo = r_lo + gates[:, k:k + 1] * y_lo
        r_hi = r_hi + gates[:, k:k + 1] * y_hi
    g2 = g2_ref[0]
    x_lo = base_ref[:, :half] + g2[:, :half] * r_lo
    x_hi = base_ref[:, half:] + g2[:, half:] * r_hi
    ms = (jnp.sum(x_lo * x_lo, axis=-1, keepdims=True)
          + jnp.sum(x_hi * x_hi, axis=-1, keepdims=True)) * (1.0 / (2 * half))
    inv = lax.rsqrt(ms + EPS)
    o_ref[:, :half] = x_lo * inv * fg_ref[:, :half]
    o_ref[:, half:] = x_hi * inv * fg_ref[:, half:]


def _combine_call(yk, gates, base, g2, fg, tiles_per_batch, tile, tile0):
    n, d = base.shape
    row = lambda w: pl.BlockSpec((tile, w), lambda i: (i + tile0, 0))
    return pl.pallas_call(
        _combine_kernel,
        grid=(yk.shape[1] // tile,),
        in_specs=[pl.BlockSpec((TOP_K, tile, yk.shape[2]), lambda i: (0, i, 0)),
                  pl.BlockSpec((TOP_K, tile), lambda i: (0, i + tile0)), row(d),
                  pl.BlockSpec((1, 1, d), lambda i: ((i + tile0) // tiles_per_batch, 0, 0)),
                  pl.BlockSpec((1, d), lambda i: (0, 0))],
        out_specs=row(d),
        out_shape=jax.ShapeDtypeStruct((n, d), F32),
        input_output_aliases={2: 0},
        compiler_params=pltpu.CompilerParams(dimension_semantics=("arbitrary",),
                                             vmem_limit_bytes=VMEM_LIMIT),
    )(yk, gates, base, g2, fg)


def _pad_cols(a, width):
    return jnp.pad(a, ((0, 0), (0, width - a.shape[1])))


def _layer(x, mod, norm1_g, norm2_g, w_in, conv_w, conv_b, dt_bias, a_log, d_skip, ssm_norm_g,
           att_norm_g, sinks, rel_bias, w_out, router_w, router_bias, exp_w_gate, exp_w_up, exp_w_down,
           sh_w_gate, sh_w_up, sh_w_down, final_g):
    bsz, l, d = x.shape
    n = bsz * l
    tm = min(ROW_TILE, l)

    sh1, sc1, g1, sh2, sc2, g2 = [m[:, None, :] for m in jnp.split(mod, 6, axis=-1)]

    assert math.frexp(ATT_HEAD_DIM ** -0.5)[0] == 0.5
    x2 = x.reshape(n, d)
    z, xbc, dt, q, k, v = _in_proj_call(x2, sc1, sh1, norm1_g[None, :], w_in, l // tm, tm)

    triu = jnp.asarray(np.triu(np.ones((CHUNK, CHUNK), np.float32))).astype(BF16)
    shift = jnp.asarray(_conv_shift_matrix()).astype(BF16)
    y_ssm = _ssd_call(xbc.reshape(bsz, l, CONV_CH), z.reshape(bsz, l, D_SSM), dt.reshape(bsz, l, LANES),
                      conv_w, conv_b[None, :], dt_bias[:, None], a_log[:, None],
                      jnp.repeat(d_skip, SSM_HEAD_DIM)[None, :], ssm_norm_g[None, :], triu, shift)

    bias = _bias_call(rel_bias, jnp.asarray(_rel_bucket_table()))
    y_att = _attn_call(sinks, q.reshape(bsz, l, D_ATT), k.reshape(bsz, l, D_KV), v.reshape(bsz, l, D_KV), bias,
                       att_norm_g[None, :])

    rw = _pad_cols(router_w, LANES)
    rwh = rw.astype(BF16)
    rwl = jnp.concatenate([rwh, (rw - rwh.astype(F32)).astype(BF16)], axis=1)
    rs = min(RANK_SUB, tm)
    upper = jnp.asarray(np.triu(np.ones((rs, rs), np.float32), 1)).astype(BF16)
    base, h2, idx, gates, rank, counts = _out_proj_call(
        x2, y_ssm.reshape(n, D_SSM), y_att.reshape(n, D_ATT), g1, sc2, sh2, g2, norm2_g[None, :],
        w_out[:D_SSM].astype(BF16), w_out[D_SSM:].astype(BF16), rwh, rwl,
        sh_w_gate.astype(BF16), sh_w_up.astype(BF16), sh_w_down.astype(BF16), router_bias[:, None], upper,
        l // tm, tm)
    rt = min(ROUTE_TILE, n)

    nblocks = (n * TOP_K + N_EXPERTS * (EXPERT_ROWS - 1) + EXPERT_ROWS - 1) // EXPERT_ROWS
    pad_start, plan = _plan_call(counts, nblocks)
    dest = _dest_call(idx, rank, pad_start, rt)

    xs = _scatter_rows_sc(h2, dest.reshape(-1), nblocks * EXPERT_ROWS, SC_CHUNK)
    ys = _expert_call(plan, xs, exp_w_gate, exp_w_up, exp_w_down, EXPERT_ROWS)
    ctile = min(COMBINE_TILE, l)
    groups = COMBINE_GROUPS if bsz % COMBINE_GROUPS == 0 else 1
    ng = n // groups
    out = base
    for g in range(groups):
        idx_g = dest[:, g * ng // LANES:(g + 1) * ng // LANES, :].reshape(-1)
        yk = _gather_rows_sc(ys, idx_g, SC_CHUNK).reshape(TOP_K, ng, ys.shape[1])
        out = _combine_call(yk, gates, out, g2, final_g[None, :], l // ctile, ctile, g * ng // ctile)
    return out.reshape(bsz, l, d)


def kernel(x, c, mod_w, mod_b, norm1_g, norm2_g, w_in, conv_w, conv_b, dt_bias, a_log, d_skip, ssm_norm_g,
           att_norm_g, sinks, rel_bias, w_out, router_w, router_bias, exp_w_gate, exp_w_up, exp_w_down,
           sh_w_gate, sh_w_up, sh_w_down, final_g):
    assert mod_w.shape[0] == 1, "single-layer block"
    bsz = x.shape[0]
    c_pad = jnp.pad(c, ((0, SUBLANES - bsz % SUBLANES if bsz % SUBLANES else 0), (0, 0)))
    mod = _mod_call(c_pad, mod_w[0], mod_b[0][None, :])[:bsz]
    return _layer(x, mod, norm1_g[0], norm2_g[0], w_in[0], conv_w[0], conv_b[0], dt_bias[0], a_log[0], d_skip[0],
                  ssm_norm_g[0], att_norm_g[0], sinks[0], rel_bias, w_out[0], router_w[0], router_bias[0],
                  exp_w_gate[0], exp_w_up[0], exp_w_down[0], sh_w_gate[0], sh_w_up[0], sh_w_down[0], final_g)
```

```python
import functools
import math

import numpy as np
import jax
import jax.numpy as jnp
from jax import lax
from jax.experimental import pallas as pl
from jax.experimental.pallas import tpu as pltpu
from jax.experimental.pallas import tpu_sc as plsc

F32 = jnp.float32
BF16 = jnp.bfloat16

D_MODEL = 1024
SSM_HEAD_DIM = 64
D_SSM = D_MODEL
SSM_HEADS = D_SSM // SSM_HEAD_DIM
SSM_GROUPS = 4
D_STATE = 128
CONV_K = 4
CONV_CH = D_SSM + 2 * SSM_GROUPS * D_STATE
CHUNK = 128
ATT_HEAD_DIM = 64
D_ATT = D_MODEL
ATT_HEADS = D_ATT // ATT_HEAD_DIM
KV_HEADS = ATT_HEADS // 4
Q_PER_KV = ATT_HEADS // KV_HEADS
D_KV = KV_HEADS * ATT_HEAD_DIM
WINDOW = 128
ATT_BLOCK = 128
REL_BUCKETS = 32
REL_MAX_DIST = 128
N_EXPERTS = 64
TOP_K = 8
EXPERT_DIM = D_MODEL // 4
SHARED_DIM = D_MODEL // 4
ROUTE_GROUPS = 8
ROUTE_TOPK_GROUPS = 4
ROUTED_SCALE = 2.5
EPS = 1e-6

LANES = 128
SUBLANES = 8
HALF = LANES // 2

ROW_TILE = 512
ROUTE_TILE = 2048
RANK_SUB = 512
COMBINE_TILE = 256
SC_CHUNK = 128
EXPERT_ROWS = 512
COMBINE_GROUPS = 2
VMEM_LIMIT = 48 * 1024 * 1024

NEG_INF = float("-inf")


def _silu(v):
    return v * (1.0 / (1.0 + jnp.exp(-v)))


def _softplus(v):
    return jnp.maximum(v, 0.0) + jnp.log(1.0 + jnp.exp(-jnp.abs(v)))


def _bdot(a, b):
    return jnp.dot(a.astype(BF16), b.astype(BF16), preferred_element_type=F32)


def _split_hi_lo(v):
    hi = v.astype(BF16)
    lo = (v - hi.astype(F32)).astype(BF16)
    return hi, lo


def _pack_bf16_pair(a, b):
    w = pltpu.pack_elementwise([a, b], packed_dtype=BF16)
    return w if w.dtype == jnp.uint32 else lax.bitcast_convert_type(w, jnp.uint32)


def _unpack_bf16_pair(w):
    a = pltpu.unpack_elementwise(w, index=0, packed_dtype=BF16, unpacked_dtype=F32)
    b = pltpu.unpack_elementwise(w, index=1, packed_dtype=BF16, unpacked_dtype=F32)
    return a, b


def _lane_half_mask(shape):
    return lax.broadcasted_iota(jnp.int32, shape, len(shape) - 1) < HALF


def _mod_kernel(c_ref, w_ref, b_ref, o_ref):
    a = _silu(c_ref[...])
    o_ref[...] = jnp.dot(a, w_ref[...], precision=lax.Precision.HIGHEST,
                         preferred_element_type=F32) + b_ref[...]


def _mod_call(c_pad, mod_w, mod_b):
    rows, d = c_pad.shape
    cols = mod_w.shape[1]
    return pl.pallas_call(
        _mod_kernel,
        grid=(cols // d,),
        in_specs=[pl.BlockSpec((rows, d), lambda j: (0, 0)),
                  pl.BlockSpec((d, d), lambda j: (0, j)),
                  pl.BlockSpec((1, d), lambda j: (0, j))],
        out_specs=pl.BlockSpec((rows, d), lambda j: (0, j)),
        out_shape=jax.ShapeDtypeStruct((rows, cols), F32),
        compiler_params=pltpu.CompilerParams(dimension_semantics=("arbitrary",),
                                             vmem_limit_bytes=VMEM_LIMIT),
    )(c_pad, mod_w, mod_b)


IN_PROJ_SEGMENTS = ((D_SSM, BF16), (CONV_CH, BF16), (LANES, F32), (D_ATT, BF16), (D_KV, BF16), (D_KV, BF16))


def _in_proj_kernel(x_ref, sc_ref, sh_ref, g_ref, w_hbm, *refs):
    out_refs = refs[:len(IN_PROJ_SEGMENTS)]
    wraw_ref, w_ref, wsem = refs[len(IN_PROJ_SEGMENTS):]

    @pl.when(pl.program_id(0) == 0)
    def _():
        cp = pltpu.make_async_copy(w_hbm.at[0], wraw_ref, wsem)
        cp.start()
        cp.wait()
        src_dt = D_SSM + CONV_CH
        dst_q = src_dt + LANES
        src_q = src_dt + SSM_HEADS
        q_scale = ATT_HEAD_DIM ** -0.5

        def put(dst, src, scale=None):
            t = wraw_ref[src:src + LANES, :].T
            w_ref[:, dst:dst + LANES] = (t if scale is None else t * scale).astype(BF16)

        for c0 in range(0, src_dt, LANES):
            put(c0, c0)
        dt_tile = wraw_ref[src_dt:src_dt + LANES, :].T
        lane = lax.broadcasted_iota(jnp.int32, dt_tile.shape, 1)
        w_ref[:, src_dt:dst_q] = jnp.where(lane < SSM_HEADS, dt_tile, 0.0).astype(BF16)
        for c0 in range(0, D_ATT, LANES):
            put(dst_q + c0, src_q + c0, q_scale)
        for c0 in range(D_ATT, D_ATT + 2 * D_KV, LANES):
            put(dst_q + c0, src_q + c0)

    xf = x_ref[...]
    ms = jnp.mean(xf * xf, axis=-1, keepdims=True)
    h = xf * lax.rsqrt(ms + EPS) * g_ref[...]
    h = h * (1.0 + sc_ref[0]) + sh_ref[0]
    hb = h.astype(BF16)
    col = 0
    for (width, dtype), o_ref in zip(IN_PROJ_SEGMENTS, out_refs):
        o_ref[...] = jnp.dot(hb, w_ref[:, col:col + width], preferred_element_type=F32).astype(dtype)
        col += width


def _in_proj_call(x2, sc1, sh1, g1n, w_in_t, tiles_per_batch, tm):
    n, d = x2.shape
    row = lambda w: pl.BlockSpec((tm, w), lambda i: (i, 0))
    full = lambda a: pl.BlockSpec(a.shape, lambda i: (0, 0))
    per_batch = pl.BlockSpec((1, 1, d), lambda i: (i // tiles_per_batch, 0, 0))
    cols = sum(w for w, _ in IN_PROJ_SEGMENTS)
    w_in = w_in_t
    assert w_in.shape == (1, cols - (LANES - SSM_HEADS), d)
    return pl.pallas_call(
        _in_proj_kernel,
        grid=(n // tm,),
        in_specs=[row(d), per_batch, per_batch, full(g1n), pl.BlockSpec(memory_space=pl.ANY)],
        out_specs=[row(w) for w, _ in IN_PROJ_SEGMENTS],
        out_shape=[jax.ShapeDtypeStruct((n, w), dt) for w, dt in IN_PROJ_SEGMENTS],
        scratch_shapes=[pltpu.VMEM(w_in.shape[1:], w_in.dtype), pltpu.VMEM((d, cols), BF16), pltpu.SemaphoreType.DMA(())],
        compiler_params=pltpu.CompilerParams(dimension_semantics=("arbitrary",),
                                             vmem_limit_bytes=VMEM_LIMIT),
    )(x2, sc1, sh1, g1n, w_in)


SSD_SEQS = 4
CONV_HALO = 16


def _conv_shift_matrix():
    s = np.zeros((CONV_K * CHUNK, CONV_HALO + CHUNK), np.float32)
    for k in range(CONV_K):
        t = np.arange(CHUNK)
        s[k * CHUNK + t, CONV_HALO + t - (CONV_K - 1) + k] = 1.0
    return s


def _silu_tanh(v):
    hv = 0.5 * v
    return hv + hv * jnp.tanh(hv)


def _ssd_kernel(xbc_ref, z_ref, dt_ref, cw_ref, cb_ref, dtb_ref, alog_ref, dskip_ref, ng_ref, triu_ref, shift_ref,
                y_ref, state_ref, ucat_ref, ybuf_ref):
    nseq = xbc_ref.shape[0]

    @pl.when(pl.program_id(1) == 0)
    def _():
        state_ref[...] = jnp.zeros_like(state_ref)
        ucat_ref[:, 0:CONV_HALO, :] = jnp.zeros((nseq, CONV_HALO, CONV_CH), BF16)

    for q in range(nseq):
        _ssd_chunk(xbc_ref.at[q], z_ref.at[q], dt_ref.at[q], cw_ref, cb_ref, dtb_ref, alog_ref, dskip_ref, ng_ref,
                   triu_ref, shift_ref, y_ref.at[q], state_ref.at[q], ucat_ref.at[q], ybuf_ref.at[q])


def _ssd_chunk(xbc_ref, z_ref, dt_ref, cw_ref, cb_ref, dtb_ref, alog_ref, dskip_ref, ng_ref, triu_ref, shift_ref,
               y_ref, state_ref, ucat_ref, ybuf_ref):
    ucat_ref[CONV_HALO:, :] = xbc_ref[...]
    shifted = jnp.dot(shift_ref[...], ucat_ref[...], preferred_element_type=F32)
    ucat_ref[0:CONV_HALO, :] = ucat_ref[CHUNK:CHUNK + CONV_HALO, :]
    acc = cb_ref[...] + cw_ref[0:1, :] * shifted[0:CHUNK]
    for kk in range(1, CONV_K):
        acc = acc + cw_ref[kk:kk + 1, :] * shifted[kk * CHUNK:(kk + 1) * CHUNK]
    act = _silu_tanh(acc)
    xs = act[:, :D_SSM]
    gn = SSM_GROUPS * D_STATE

    dt_t = _softplus(dt_ref[...].T[0:SSM_HEADS, :] + dtb_ref[...])
    a_t = dt_t * (-jnp.exp(alog_ref[...]))
    a_hi = a_t.astype(BF16)
    a_mid = (a_t - a_hi.astype(F32)).astype(BF16)
    a_lo = (a_t - a_hi.astype(F32) - a_mid.astype(F32)).astype(BF16)
    triu = triu_ref[...]
    cs_t = (jnp.dot(a_hi, triu, preferred_element_type=F32) + jnp.dot(a_mid, triu, preferred_element_type=F32)
            + jnp.dot(a_lo, triu, preferred_element_type=F32))
    cs_end = cs_t[:, CHUNK - 1:CHUNK]
    r_t = cs_t - jnp.log(dt_t)
    w_t = jnp.exp(cs_end - cs_t) * dt_t
    chunk_decay = jnp.exp(cs_end)
    cols = jnp.concatenate([cs_t, jnp.exp(cs_t), jnp.zeros((LANES - 2 * SSM_HEADS, CHUNK), F32)], axis=0).T

    li = lax.broadcasted_iota(jnp.int32, (CHUNK, CHUNK), 0)
    si = lax.broadcasted_iota(jnp.int32, (CHUNK, CHUNK), 1)
    causal = li >= si
    low = _lane_half_mask((CHUNK, LANES))
    low_row = _lane_half_mask((1, LANES))

    heads_per_group = SSM_HEADS // SSM_GROUPS
    for g in range(SSM_GROUPS):
        b_g = act[:, D_SSM + g * D_STATE:D_SSM + (g + 1) * D_STATE]
        c_g = act[:, D_SSM + gn + g * D_STATE:D_SSM + gn + (g + 1) * D_STATE]
        b_gb = b_g.astype(BF16)
        c_gb = c_g.astype(BF16)
        cb = lax.dot_general(c_gb, b_gb, (((1,), (1,)), ((), ())), preferred_element_type=F32)
        b_t = b_g.T
        for jp in range(heads_per_group // 2):
            j = g * (heads_per_group // 2) + jp
            lanes = slice(j * LANES, (j + 1) * LANES)
            xp = xs[:, lanes]
            xpb = xp.astype(BF16)
            ydiag = jnp.zeros((CHUNK, LANES), F32)
            snew = jnp.zeros((D_STATE, LANES), F32)
            for half in range(2):
                h = 2 * j + half
                diff = cols[:, h:h + 1] - r_t[h:h + 1, :]
                m = (cb * jnp.exp(jnp.where(causal, diff, NEG_INF))).astype(BF16)
                keep = low if half == 0 else jnp.logical_not(low)
                xh = jnp.where(keep, xpb, jnp.zeros_like(xpb))
                ydiag = ydiag + jnp.dot(m, xh, preferred_element_type=F32)
                snew = snew + jnp.dot((b_t * w_t[h:h + 1, :]).astype(BF16), xh, preferred_element_type=F32)
            s_in = state_ref[:, lanes]
            yoff = jnp.dot(c_gb, s_in.astype(BF16), preferred_element_type=F32)
            h0 = 2 * j
            e0 = SSM_HEADS + h0
            escale = jnp.where(low, cols[:, e0:e0 + 1], cols[:, e0 + 1:e0 + 2])
            cdec = jnp.where(low_row, chunk_decay[h0:h0 + 1, :], chunk_decay[h0 + 1:h0 + 2, :])
            ybuf_ref[:, lanes] = ydiag + yoff * escale + xp * dskip_ref[:, lanes]
            state_ref[:, lanes] = s_in * cdec + snew

    yz = ybuf_ref[...] * _silu_tanh(z_ref[...].astype(F32))
    gw = D_SSM // SSM_GROUPS
    for g in range(SSM_GROUPS):
        part = yz[:, g * gw:(g + 1) * gw]
        ms = jnp.mean(part * part, axis=-1, keepdims=True)
        y_ref[:, g * gw:(g + 1) * gw] = (part * lax.rsqrt(ms + EPS)
                                            * ng_ref[:, g * gw:(g + 1) * gw]).astype(BF16)


def _ssd_call(xbc, z, dt, conv_w, conv_b, dtb, alog, dskip, ng, triu, shift):
    bsz, l, _ = xbc.shape
    nc = l // CHUNK
    nseq = SSD_SEQS if bsz % SSD_SEQS == 0 else 1
    chunk = lambda w: pl.BlockSpec((nseq, CHUNK, w), lambda b, c: (b, c, 0))
    full = lambda a: pl.BlockSpec(a.shape, lambda b, c: (0, 0))
    return pl.pallas_call(
        _ssd_kernel,
        grid=(bsz // nseq, nc),
        in_specs=[chunk(CONV_CH), chunk(D_SSM), chunk(LANES), full(conv_w), full(conv_b), full(dtb),
                  full(alog), full(dskip), full(ng), full(triu), full(shift)],
        out_specs=chunk(D_SSM),
        out_shape=jax.ShapeDtypeStruct((bsz, l, D_SSM), BF16),
        scratch_shapes=[pltpu.VMEM((nseq, D_STATE, D_SSM), F32),
                        pltpu.VMEM((nseq, CONV_HALO + CHUNK, CONV_CH), BF16),
                        pltpu.VMEM((nseq, CHUNK, D_SSM), F32)],
        compiler_params=pltpu.CompilerParams(dimension_semantics=("arbitrary", "arbitrary"),
                                             vmem_limit_bytes=VMEM_LIMIT),
    )(xbc, z, dt, conv_w, conv_b, dtb, alog, dskip, ng, triu, shift)


assert WINDOW == ATT_BLOCK


def _rel_bucket_table():
    qi = np.arange(ATT_BLOCK)[:, None]
    c = np.arange(ATT_BLOCK)[None, :]
    dist = np.where(c > qi, qi + ATT_BLOCK - c, qi - c)
    max_exact = REL_BUCKETS // 2
    d = np.maximum(dist, 1).astype(np.float32)
    large = max_exact + (np.log(d / np.float32(max_exact)) / np.float32(math.log(REL_MAX_DIST / max_exact))
                         * np.float32(REL_BUCKETS - max_exact)).astype(np.int32)
    large = np.minimum(large, REL_BUCKETS - 1)
    return np.where(dist < max_exact, dist, large).astype(np.int32)


def _bias_kernel(rb_ref, bucket_ref, o_ref):
    bucket = bucket_ref[...]
    from_prev = (lax.broadcasted_iota(jnp.int32, bucket.shape, 1)
                 > lax.broadcasted_iota(jnp.int32, bucket.shape, 0))
    for h in range(ATT_HEADS):
        acc = jnp.zeros(bucket.shape, F32)
        for b in range(REL_BUCKETS):
            acc = jnp.where(bucket == b, rb_ref[b, h], acc)
        o_ref[1, h] = acc
        o_ref[0, h] = jnp.where(from_prev, NEG_INF, acc)


def _bias_call(rel_bias, bucket):
    return pl.pallas_call(
        _bias_kernel,
        in_specs=[pl.BlockSpec(memory_space=pltpu.SMEM), pl.BlockSpec(memory_space=pltpu.VMEM)],
        out_shape=jax.ShapeDtypeStruct((2, ATT_HEADS) + bucket.shape, F32),
    )(rel_bias, bucket)


ATT_SEQS = 4


def _attn_kernel(sink_ref, q_ref, kp_ref, kc_ref, vp_ref, vc_ref, bias_ref, ng_ref, o_ref, obuf_ref):
    for s in range(q_ref.shape[0]):
        _attn_block(sink_ref, q_ref.at[s], kp_ref.at[s], kc_ref.at[s], vp_ref.at[s], vc_ref.at[s], bias_ref.at[0],
                    ng_ref, o_ref.at[s], obuf_ref.at[s])


def _attn_block(sink_ref, q_ref, kp_ref, kc_ref, vp_ref, vc_ref, bias_ref, ng_ref, o_ref, obuf_ref):
    qi = lax.broadcasted_iota(jnp.int32, (ATT_BLOCK, ATT_BLOCK), 0)
    ci = lax.broadcasted_iota(jnp.int32, (ATT_BLOCK, ATT_BLOCK), 1)
    from_prev = ci > qi
    low = _lane_half_mask((ATT_BLOCK, LANES))

    def band_variants(prev_ref, cur_ref):
        out = []
        for cpair in range(KV_HEADS // 2):
            lanes = slice(cpair * LANES, (cpair + 1) * LANES)
            t = jnp.concatenate([prev_ref[:, lanes], cur_ref[:, lanes]], axis=0).astype(F32)
            out.append((t.astype(BF16), pltpu.roll(t, HALF, 1).astype(BF16)))
        return out

    k_band = band_variants(kp_ref, kc_ref)
    v_band = band_variants(vp_ref, vc_ref)
    nt = (((1,), (1,)), ((), ()))

    for j in range(ATT_HEADS // 2):
        qp = q_ref[:, j * LANES:(j + 1) * LANES]
        out_pair = jnp.zeros((ATT_BLOCK, LANES), F32)
        for half in range(2):
            h = 2 * j + half
            g = h // Q_PER_KV
            swapped = int((g % 2) != half)
            keep = low if half == 0 else jnp.logical_not(low)
            qh = jnp.where(keep, qp, jnp.zeros_like(qp))
            s_band = lax.dot_general(qh, k_band[g // 2][swapped], nt, preferred_element_type=F32)
            s = jnp.where(from_prev, s_band[:, :ATT_BLOCK], s_band[:, ATT_BLOCK:]) + bias_ref[h]
            sink = sink_ref[h]
            m = jnp.maximum(jnp.max(s, axis=-1, keepdims=True), sink)
            p = jnp.exp(s - m)
            denom = jnp.sum(p, axis=-1, keepdims=True) + jnp.exp(sink - m)
            p_band = jnp.concatenate([jnp.where(from_prev, p, 0.0), jnp.where(from_prev, 0.0, p)], axis=1)
            o = jnp.dot(p_band.astype(BF16), v_band[g // 2][swapped], preferred_element_type=F32) / denom
            out_pair = out_pair + jnp.where(keep, o, 0.0)
        obuf_ref[:, j * LANES:(j + 1) * LANES] = out_pair

    att = obuf_ref[...]
    ms = jnp.mean(att * att, axis=-1, keepdims=True)
    o_ref[...] = (att * lax.rsqrt(ms + EPS) * ng_ref[...]).astype(BF16)


def _attn_call(sinks, q, k, v, bias, ng):
    bsz, l, _ = q.shape
    nb = l // ATT_BLOCK
    nseq = ATT_SEQS if bsz % ATT_SEQS == 0 else 1
    cur = lambda w: pl.BlockSpec((nseq, ATT_BLOCK, w), lambda b, i: (b, i, 0))
    prev = lambda w: pl.BlockSpec((nseq, ATT_BLOCK, w), lambda b, i: (b, jnp.maximum(i - 1, 0), 0))
    return pl.pallas_call(
        _attn_kernel,
        grid=(bsz // nseq, nb),
        in_specs=[pl.BlockSpec(memory_space=pltpu.SMEM),
                  cur(D_ATT), prev(D_KV), cur(D_KV), prev(D_KV), cur(D_KV),
                  pl.BlockSpec((1,) + bias.shape[1:], lambda b, i: (jnp.minimum(i, 1), 0, 0, 0)),
                  pl.BlockSpec(ng.shape, lambda b, i: (0, 0))],
        out_specs=cur(D_ATT),
        out_shape=jax.ShapeDtypeStruct((bsz, l, D_ATT), BF16),
        scratch_shapes=[pltpu.VMEM((nseq, ATT_BLOCK, D_ATT), F32)],
        compiler_params=pltpu.CompilerParams(dimension_semantics=("arbitrary", "arbitrary"),
                                             vmem_limit_bytes=VMEM_LIMIT),
    )(sinks, q, k, k, v, v, bias, ng)


def _out_proj_kernel(x_ref, ys_ref, ya_ref, g1_ref, sc_ref, sh_ref, g2_ref, ng_ref, wos_ref, woa_ref,
                     rwh_ref, rwl_ref, sg_ref, su_ref, sd_ref, rb_ref, upper_ref,
                     base_ref, h_ref, idx_ref, gate_ref, rank_ref, cnt_ref, carry_ref):
    @pl.when(pl.program_id(0) == 0)
    def _():
        carry_ref[...] = jnp.zeros_like(carry_ref)

    mix = (jnp.dot(ys_ref[...], wos_ref[...], preferred_element_type=F32)
           + jnp.dot(ya_ref[...], woa_ref[...], preferred_element_type=F32))
    x1 = x_ref[...] + g1_ref[0] * mix
    ms = jnp.mean(x1 * x1, axis=-1, keepdims=True)
    h = x1 * lax.rsqrt(ms + EPS) * ng_ref[...]
    h = h * (1.0 + sc_ref[0]) + sh_ref[0]
    half = h.shape[1] // 2
    h_ref[...] = _pack_bf16_pair(h[:, :half], h[:, half:])
    hi, lo = _split_hi_lo(h)
    hi_terms = jnp.dot(hi, rwl_ref[...], preferred_element_type=F32)
    logits = (hi_terms[:, :LANES] + hi_terms[:, LANES:]
              + jnp.dot(lo, rwh_ref[...], preferred_element_type=F32))
    _route_tokens(logits.T[0:N_EXPERTS, :], rb_ref, upper_ref, idx_ref, gate_ref, rank_ref, cnt_ref, carry_ref)
    u = _silu(jnp.dot(hi, sg_ref[...], preferred_element_type=F32)) * jnp.dot(hi, su_ref[...],
                                                                              preferred_element_type=F32)
    shared = jnp.dot(u.astype(BF16), sd_ref[...], preferred_element_type=F32)
    base_ref[...] = x1 + g2_ref[0] * shared


def _out_proj_call(x2, ys, ya, g1, sc2, sh2, g2, ng, wos, woa, rwh, rwl, sg, su, sd, router_bias, upper,
                   tiles_per_batch, tm):
    n, d = x2.shape
    row = lambda w: pl.BlockSpec((tm, w), lambda i: (i, 0))
    tok = lambda r: pl.BlockSpec((r, tm), lambda i: (0, i))
    full = lambda a: pl.BlockSpec(a.shape, lambda i: (0, 0))
    per_batch = pl.BlockSpec((1, 1, d), lambda i: (i // tiles_per_batch, 0, 0))
    return pl.pallas_call(
        _out_proj_kernel,
        grid=(n // tm,),
        in_specs=[row(d), row(D_SSM), row(D_ATT), per_batch, per_batch, per_batch, per_batch, full(ng),
                  full(wos), full(woa), full(rwh), full(rwl), full(sg), full(su), full(sd), full(router_bias),
                  full(upper)],
        out_specs=[row(d), row(d // 2), tok(TOP_K), tok(TOP_K), tok(TOP_K),
                   pl.BlockSpec((N_EXPERTS, LANES), lambda i: (0, 0))],
        out_shape=[jax.ShapeDtypeStruct((n, d), F32), jax.ShapeDtypeStruct((n, d // 2), jnp.uint32),
                   jax.ShapeDtypeStruct((TOP_K, n), jnp.int32), jax.ShapeDtypeStruct((TOP_K, n), F32),
                   jax.ShapeDtypeStruct((TOP_K, n), jnp.int32), jax.ShapeDtypeStruct((N_EXPERTS, LANES), F32)],
        scratch_shapes=[pltpu.VMEM((N_EXPERTS, LANES), F32)],
        compiler_params=pltpu.CompilerParams(dimension_semantics=("arbitrary",),
                                             vmem_limit_bytes=VMEM_LIMIT),
    )(x2, ys, ya, g1, sc2, sh2, g2, ng, wos, woa, rwh, rwl, sg, su, sd, router_bias, upper)


def _route_tokens(logits_t, rb_ref, upper_ref, idx_ref, gate_ref, rank_ref, cnt_ref, carry_ref):
    t = logits_t.shape[1]
    per_group = N_EXPERTS // ROUTE_GROUPS
    scores = 1.0 / (1.0 + jnp.exp(-logits_t))
    sel = scores + rb_ref[...]
    e_iota = lax.broadcasted_iota(jnp.int32, (N_EXPERTS, t), 0)

    sel3 = sel.reshape(ROUTE_GROUPS, per_group, t)
    w_iota = lax.broadcasted_iota(jnp.int32, sel3.shape, 1)
    m1 = jnp.max(sel3, axis=1, keepdims=True)
    first = jnp.min(jnp.where(sel3 == m1, w_iota, per_group), axis=1, keepdims=True)
    m2 = jnp.max(jnp.where(w_iota == first, NEG_INF, sel3), axis=1, keepdims=True)
    grp = (m1 + m2).reshape(ROUTE_GROUPS, t)

    g_iota = lax.broadcasted_iota(jnp.int32, (ROUTE_GROUPS, t), 0)
    gmask = jnp.zeros((ROUTE_GROUPS, t), jnp.bool_)
    for _ in range(ROUTE_TOPK_GROUPS):
        gm = jnp.max(grp, axis=0, keepdims=True)
        gfirst = jnp.min(jnp.where(grp == gm, g_iota, ROUTE_GROUPS), axis=0, keepdims=True)
        hit = g_iota == gfirst
        gmask = jnp.logical_or(gmask, hit)
        grp = jnp.where(hit, NEG_INF, grp)
    allowed = jnp.broadcast_to(gmask.reshape(ROUTE_GROUPS, 1, t),
                               (ROUTE_GROUPS, per_group, t)).reshape(N_EXPERTS, t)
    masked = jnp.where(allowed, sel, NEG_INF)

    picked = jnp.zeros((N_EXPERTS, t), jnp.bool_)
    idx_rows = []
    w_rows = []
    for _ in range(TOP_K):
        mm = jnp.max(masked, axis=0, keepdims=True)
        efirst = jnp.min(jnp.where(masked == mm, e_iota, N_EXPERTS), axis=0, keepdims=True)
        hit = e_iota == efirst
        idx_rows.append(efirst)
        w_rows.append(jnp.sum(jnp.where(hit, scores, 0.0), axis=0, keepdims=True))
        picked = jnp.logical_or(picked, hit)
        masked = jnp.where(hit, NEG_INF, masked)
    idx = jnp.concatenate(idx_rows, axis=0)
    w = jnp.concatenate(w_rows, axis=0)
    gate_ref[...] = w / jnp.sum(w, axis=0, keepdims=True) * ROUTED_SCALE
    idx_ref[...] = idx

    onehot = jnp.where(picked, 1.0, 0.0)
    sub = upper_ref.shape[0]
    carry = carry_ref[:, 0:1]
    parts = []
    for s0 in range(0, t, sub):
        oh = onehot[:, s0:s0 + sub]
        parts.append(jnp.dot(oh.astype(BF16), upper_ref[...], preferred_element_type=F32) + carry)
        carry = carry + jnp.sum(oh, axis=1, keepdims=True)
    rank_full = jnp.concatenate(parts, axis=1)
    rank_rows = [jnp.sum(jnp.where(e_iota == idx_rows[k], rank_full, 0.0), axis=0, keepdims=True)
                 for k in range(TOP_K)]
    rank_ref[...] = jnp.concatenate(rank_rows, axis=0).astype(jnp.int32)
    carry_ref[...] = jnp.broadcast_to(carry, carry_ref.shape)
    cnt_ref[...] = carry_ref[...]


PLAN_EXPERT, PLAN_FRESH, PLAN_VALID, PLAN_SEG, PLAN_NEXT, PLAN_NUSED = range(6)
PLAN_ROWS = SUBLANES


def _plan_kernel(cnt_ref, tri_ref, start_ref, plan_ref, *, nblocks):
    nbp = plan_ref.shape[1]
    cnt = cnt_ref[...].astype(jnp.int32)
    blocks = (cnt + (EXPERT_ROWS - 1)) // EXPERT_ROWS
    end = jnp.dot(tri_ref[...], blocks.astype(F32), precision=lax.Precision.HIGHEST,
                  preferred_element_type=F32).astype(jnp.int32)
    start = end - blocks
    start_ref[...] = start[:, 0:1] * EXPERT_ROWS
    nused = end[N_EXPERTS - 1:N_EXPERTS, 0:1]

    e_iota = lax.broadcasted_iota(jnp.int32, (N_EXPERTS, nbp), 0)
    blk = lax.broadcasted_iota(jnp.int32, (1, nbp), 1)
    expert = jnp.minimum(jnp.sum((end[:, 0:1] <= blk).astype(jnp.int32), axis=0, keepdims=True), N_EXPERTS - 1)
    mine = e_iota == expert
    pick = lambda col: jnp.sum(jnp.where(mine, col, 0), axis=0, keepdims=True)
    first = pick(start[:, 0:1])
    valid = jnp.clip(pick(cnt[:, 0:1]) - (blk - first) * EXPERT_ROWS, 0, EXPERT_ROWS)
    present = jnp.logical_or(blocks[:, 0:1] > 0,
                             jnp.logical_and(e_iota[:, 0:1] == N_EXPERTS - 1, nused < nblocks))
    seg = jnp.sum(jnp.logical_and(present, e_iota <= expert).astype(jnp.int32), axis=0, keepdims=True) - 1
    nxt = jnp.min(jnp.where(jnp.logical_and(present, e_iota > expert), e_iota, N_EXPERTS), axis=0, keepdims=True)
    rows = {PLAN_EXPERT: expert, PLAN_FRESH: (blk == first).astype(jnp.int32), PLAN_VALID: valid, PLAN_SEG: seg,
            PLAN_NEXT: jnp.where(nxt == N_EXPERTS, -1, nxt), PLAN_NUSED: jnp.broadcast_to(nused, (1, nbp))}
    zero = jnp.zeros((1, nbp), jnp.int32)
    plan_ref[...] = jnp.concatenate([rows.get(r, zero) for r in range(PLAN_ROWS)], axis=0)


def _plan_call(counts, nblocks):
    nbp = -(-nblocks // LANES) * LANES
    tri = jnp.asarray(np.tril(np.ones((N_EXPERTS, N_EXPERTS), np.float32)))
    return pl.pallas_call(
        functools.partial(_plan_kernel, nblocks=nblocks),
        out_shape=[jax.ShapeDtypeStruct((N_EXPERTS, 1), jnp.int32),
                   jax.ShapeDtypeStruct((PLAN_ROWS, nbp), jnp.int32)],
    )(counts, tri)


def _dest_kernel(idx_ref, rank_ref, start_ref, dest_ref):
    t = idx_ref.shape[1]
    e_iota = lax.broadcasted_iota(jnp.int32, (N_EXPERTS, t), 0)
    rows = [jnp.sum(jnp.where(e_iota == idx_ref[k:k + 1, :], start_ref[...], 0), axis=0, keepdims=True)
            for k in range(TOP_K)]
    dest = jnp.concatenate(rows, axis=0) + rank_ref[...]
    for k in range(TOP_K):
        for c in range(t // LANES):
            dest_ref[k, c:c + 1, :] = dest[k:k + 1, c * LANES:(c + 1) * LANES]


def _dest_call(idx, rank, pad_start, tile):
    n = idx.shape[1]
    tok = pl.BlockSpec((TOP_K, tile), lambda i: (0, i))
    return pl.pallas_call(
        _dest_kernel,
        grid=(n // tile,),
        in_specs=[tok, tok, pl.BlockSpec((N_EXPERTS, 1), lambda i: (0, 0))],
        out_specs=pl.BlockSpec((TOP_K, tile // LANES, LANES), lambda i: (0, i, 0)),
        out_shape=jax.ShapeDtypeStruct((TOP_K, n // LANES, LANES), jnp.int32),
        compiler_params=pltpu.CompilerParams(dimension_semantics=("arbitrary",)),
    )(idx, rank, pad_start)


def _scatter_rows_sc(rows, dest_flat, total_rows, chunk):
    n, w = rows.shape
    copies = dest_flat.shape[0] // n
    info = plsc.get_sparse_core_info()
    nc = info.num_cores
    per_worker = n // (nc * info.num_subcores)
    assert per_worker * nc * info.num_subcores == n and per_worker % chunk == 0
    mesh = plsc.VectorSubcoreMesh(core_axis_name="c", subcore_axis_name="s")

    @functools.partial(
        pl.kernel, mesh=mesh,
        out_type=jax.ShapeDtypeStruct((total_rows, w), rows.dtype),
        scratch_types=[pltpu.VMEM((chunk,), jnp.int32), pltpu.VMEM((chunk, w), rows.dtype),
                       pltpu.SemaphoreType.DMA],
    )
    def scatter(rows_hbm, idx_hbm, out_hbm, idx_v, rows_v, sem):
        worker = lax.axis_index("s") * nc + lax.axis_index("c")

        @pl.loop(0, per_worker // chunk)
        def _(j):
            base = worker * per_worker + j * chunk
            pltpu.sync_copy(rows_hbm.at[pl.ds(base, chunk)], rows_v)
            for k in range(copies):
                pltpu.sync_copy(idx_hbm.at[pl.ds(k * n + base, chunk)], idx_v)
                pltpu.async_copy(rows_v, out_hbm.at[idx_v], sem).wait()

    return scatter(rows, dest_flat)


X_RING = 3


def _expert_kernel(plan_ref, xs_hbm, wg_hbm, wu_hbm, wd_hbm, y_ref,
                   wgb_ref, wub_ref, wdb_ref, xbuf_ref, xsem, wg_raw, wu_raw, wd_raw, wsem):
    i = pl.program_id(0)
    rows, half = xbuf_ref.shape[1], xbuf_ref.shape[2]
    nused = plan_ref[PLAN_NUSED, 0]

    def w_copies(expert, slot):
        return [pltpu.make_async_copy(src.at[expert], dst.at[slot], wsem.at[slot])
                for src, dst in ((wg_hbm, wg_raw), (wu_hbm, wu_raw), (wd_hbm, wd_raw))]

    def x_copy(block):
        first = pl.multiple_of(block * rows, rows)
        slot = block % X_RING
        return pltpu.make_async_copy(xs_hbm.at[pl.ds(first, rows)], xbuf_ref.at[slot], xsem.at[slot])

    @pl.when(i == 0)
    def _():
        for b in range(X_RING - 1):
            @pl.when(b < nused)
            def _():
                x_copy(b).start()

    @pl.when(i + (X_RING - 1) < nused)
    def _():
        x_copy(i + (X_RING - 1)).start()

    @pl.when(i == 0)
    def _():
        for c in w_copies(plan_ref[PLAN_EXPERT, 0], 0):
            c.start()

    @pl.when(plan_ref[PLAN_FRESH, i] > 0)
    def _():
        slot = plan_ref[PLAN_SEG, i] % 2
        for c in w_copies(plan_ref[PLAN_EXPERT, i], slot):
            c.wait()

        @pl.when(plan_ref[PLAN_NEXT, i] >= 0)
        def _():
            for c in w_copies(plan_ref[PLAN_NEXT, i], 1 - slot):
                c.start()

        wgb_ref[...] = wg_raw[slot].astype(BF16)
        wub_ref[...] = wu_raw[slot].astype(BF16)
        wdb_ref[...] = wd_raw[slot].astype(BF16)

    @pl.when(i < nused)
    def _():
        x_copy(i).wait()
        xw = xbuf_ref[i % X_RING]
        row = lax.broadcasted_iota(jnp.int32, xw.shape, 0)
        x_lo, x_hi = _unpack_bf16_pair(jnp.where(row < plan_ref[PLAN_VALID, i], xw, jnp.uint32(0)))
        x_lo = x_lo.astype(BF16)
        x_hi = x_hi.astype(BF16)
        gate = (jnp.dot(x_lo, wgb_ref[:half, :], preferred_element_type=F32)
                + jnp.dot(x_hi, wgb_ref[half:, :], preferred_element_type=F32))
        up = (jnp.dot(x_lo, wub_ref[:half, :], preferred_element_type=F32)
              + jnp.dot(x_hi, wub_ref[half:, :], preferred_element_type=F32))
        u = (_silu(gate) * up).astype(BF16)
        y_lo = jnp.dot(u, wdb_ref[:, :half], preferred_element_type=F32)
        y_hi = jnp.dot(u, wdb_ref[:, half:], preferred_element_type=F32)
        y_ref[...] = _pack_bf16_pair(y_lo, y_hi)

    @pl.when(i >= nused)
    def _():
        y_ref[...] = jnp.zeros_like(y_ref)


def _expert_call(plan, xs, wg, wu, wd, rows):
    p, w = xs.shape
    d, f = wg.shape[1], wg.shape[2]
    hbm = pl.BlockSpec(memory_space=pl.ANY)
    grid_spec = pltpu.PrefetchScalarGridSpec(
        num_scalar_prefetch=1,
        grid=(p // rows,),
        in_specs=[hbm, hbm, hbm, hbm],
        out_specs=pl.BlockSpec((rows, w), lambda i, *_: (i, 0)),
        scratch_shapes=[pltpu.VMEM((d, f), BF16), pltpu.VMEM((d, f), BF16), pltpu.VMEM((f, d), BF16),
                        pltpu.VMEM((X_RING, rows, w), xs.dtype), pltpu.SemaphoreType.DMA((X_RING,)),
                        pltpu.VMEM((2, d, f), wg.dtype), pltpu.VMEM((2, d, f), wu.dtype),
                        pltpu.VMEM((2, f, d), wd.dtype), pltpu.SemaphoreType.DMA((2,))],
    )
    return pl.pallas_call(
        _expert_kernel,
        grid_spec=grid_spec,
        out_shape=jax.ShapeDtypeStruct((p, w), jnp.uint32),
        compiler_params=pltpu.CompilerParams(dimension_semantics=("arbitrary",),
                                             vmem_limit_bytes=VMEM_LIMIT),
    )(plan, xs, wg, wu, wd)


def _gather_rows_sc(table, idx, chunk):
    m = idx.shape[0]
    w = table.shape[1]
    info = plsc.get_sparse_core_info()
    nc = info.num_cores
    per_worker = m // (nc * info.num_subcores)
    assert per_worker * nc * info.num_subcores == m and per_worker % chunk == 0
    mesh = plsc.VectorSubcoreMesh(core_axis_name="c", subcore_axis_name="s")

    @functools.partial(
        pl.kernel, mesh=mesh,
        out_type=jax.ShapeDtypeStruct((m, w), table.dtype),
        scratch_types=[pltpu.VMEM((chunk,), jnp.int32), pltpu.VMEM((chunk, w), table.dtype),
                       pltpu.SemaphoreType.DMA],
    )
    def gather(table_hbm, idx_hbm, out_hbm, idx_v, rows_v, sem):
        worker = lax.axis_index("s") * nc + lax.axis_index("c")

        @pl.loop(0, per_worker // chunk)
        def _(j):
            base = worker * per_worker + j * chunk
            pltpu.sync_copy(idx_hbm.at[pl.ds(base, chunk)], idx_v)
            pltpu.async_copy(table_hbm.at[idx_v], rows_v, sem).wait()
            pltpu.sync_copy(rows_v, out_hbm.at[pl.ds(base, chunk)])

    return gather(table, idx)


def _combine_kernel(yk_ref, gate_ref, base_ref, g2_ref, fg_ref, o_ref):
    t = base_ref.shape[0]
    half = yk_ref.shape[2]
    gates = jnp.concatenate([gate_ref[...], jnp.zeros((LANES - TOP_K, t), F32)], axis=0).T
    r_lo = jnp.zeros((t, half), F32)
    r_hi = jnp.zeros((t, half), F32)
    for k in range(TOP_K):
        y_lo, y_hi = _unpack_bf16_pair(yk_ref[k])
        r_lo = r_lo + gates[:, k:k + 1] * y_lo
        r_hi = r_hi + gates[:, k:k + 1] * y_hi
    g2 = g2_ref[0]
    x_lo = base_ref[:, :half] + g2[:, :half] * r_lo
    x_hi = base_ref[:, half:] + g2[:, half:] * r_hi
    ms = (jnp.sum(x_lo * x_lo, axis=-1, keepdims=True)
          + jnp.sum(x_hi * x_hi, axis=-1, keepdims=True)) * (1.0 / (2 * half))
    inv = lax.rsqrt(ms + EPS)
    o_ref[:, :half] = x_lo * inv * fg_ref[:, :half]
    o_ref[:, half:] = x_hi * inv * fg_ref[:, half:]


def _combine_call(yk, gates, base, g2, fg, tiles_per_batch, tile, tile0):
    n, d = base.shape
    row = lambda w: pl.BlockSpec((tile, w), lambda i: (i + tile0, 0))
    return pl.pallas_call(
        _combine_kernel,
        grid=(yk.shape[1] // tile,),
        in_specs=[pl.BlockSpec((TOP_K, tile, yk.shape[2]), lambda i: (0, i, 0)),
                  pl.BlockSpec((TOP_K, tile), lambda i: (0, i + tile0)), row(d),
                  pl.BlockSpec((1, 1, d), lambda i: ((i + tile0) // tiles_per_batch, 0, 0)),
                  pl.BlockSpec((1, d), lambda i: (0, 0))],
        out_specs=row(d),
        out_shape=jax.ShapeDtypeStruct((n, d), F32),
        input_output_aliases={2: 0},
        compiler_params=pltpu.CompilerParams(dimension_semantics=("arbitrary",),
                                             vmem_limit_bytes=VMEM_LIMIT),
    )(yk, gates, base, g2, fg)


def _pad_cols(a, width):
    return jnp.pad(a, ((0, 0), (0, width - a.shape[1])))


def _layer(x, mod, norm1_g, norm2_g, w_in, conv_w, conv_b, dt_bias, a_log, d_skip, ssm_norm_g,
           att_norm_g, sinks, rel_bias, w_out, router_w, router_bias, exp_w_gate, exp_w_up, exp_w_down,
           sh_w_gate, sh_w_up, sh_w_down, final_g):
    bsz, l, d = x.shape
    n = bsz * l
    tm = min(ROW_TILE, l)

    sh1, sc1, g1, sh2, sc2, g2 = [m[:, None, :] for m in jnp.split(mod, 6, axis=-1)]

    assert math.frexp(ATT_HEAD_DIM ** -0.5)[0] == 0.5
    x2 = x.reshape(n, d)
    z, xbc, dt, q, k, v = _in_proj_call(x2, sc1, sh1, norm1_g[None, :], jnp.swapaxes(w_in, 1, 2), l // tm, tm)

    triu = jnp.asarray(np.triu(np.ones((CHUNK, CHUNK), np.float32))).astype(BF16)
    shift = jnp.asarray(_conv_shift_matrix()).astype(BF16)
    y_ssm = _ssd_call(xbc.reshape(bsz, l, CONV_CH), z.reshape(bsz, l, D_SSM), dt.reshape(bsz, l, LANES),
                      conv_w, conv_b[None, :], dt_bias[:, None], a_log[:, None],
                      jnp.repeat(d_skip, SSM_HEAD_DIM)[None, :], ssm_norm_g[None, :], triu, shift)

    bias = _bias_call(rel_bias, jnp.asarray(_rel_bucket_table()))
    y_att = _attn_call(sinks, q.reshape(bsz, l, D_ATT), k.reshape(bsz, l, D_KV), v.reshape(bsz, l, D_KV), bias,
                       att_norm_g[None, :])

    rw = _pad_cols(router_w, LANES)
    rwh = rw.astype(BF16)
    rwl = jnp.concatenate([rwh, (rw - rwh.astype(F32)).astype(BF16)], axis=1)
    rs = min(RANK_SUB, tm)
    upper = jnp.asarray(np.triu(np.ones((rs, rs), np.float32), 1)).astype(BF16)
    base, h2, idx, gates, rank, counts = _out_proj_call(
        x2, y_ssm.reshape(n, D_SSM), y_att.reshape(n, D_ATT), g1, sc2, sh2, g2, norm2_g[None, :],
        w_out[:D_SSM].astype(BF16), w_out[D_SSM:].astype(BF16), rwh, rwl,
        sh_w_gate.astype(BF16), sh_w_up.astype(BF16), sh_w_down.astype(BF16), router_bias[:, None], upper,
        l // tm, tm)
    rt = min(ROUTE_TILE, n)

    nblocks = (n * TOP_K + N_EXPERTS * (EXPERT_ROWS - 1) + EXPERT_ROWS - 1) // EXPERT_ROWS
    pad_start, plan = _plan_call(counts, nblocks)
    dest = _dest_call(idx, rank, pad_start, rt)

    xs = _scatter_rows_sc(h2, dest.reshape(-1), nblocks * EXPERT_ROWS, SC_CHUNK)
    ys = _expert_call(plan, xs, exp_w_gate, exp_w_up, exp_w_down, EXPERT_ROWS)
    ctile = min(COMBINE_TILE, l)
    groups = COMBINE_GROUPS if bsz % COMBINE_GROUPS == 0 else 1
    ng = n // groups
    out = base
    for g in range(groups):
        idx_g = dest[:, g * ng // LANES:(g + 1) * ng // LANES, :].reshape(-1)
        yk = _gather_rows_sc(ys, idx_g, SC_CHUNK).reshape(TOP_K, ng, ys.shape[1])
        out = _combine_call(yk, gates, out, g2, final_g[None, :], l // ctile, ctile, g * ng // ctile)
    return out.reshape(bsz, l, d)


def kernel(x, c, mod_w, mod_b, norm1_g, norm2_g, w_in, conv_w, conv_b, dt_bias, a_log, d_skip, ssm_norm_g,
           att_norm_g, sinks, rel_bias, w_out, router_w, router_bias, exp_w_gate, exp_w_up, exp_w_down,
           sh_w_gate, sh_w_up, sh_w_down, final_g):
    assert mod_w.shape[0] == 1, "single-layer block"
    bsz = x.shape[0]
    c_pad = jnp.pad(c, ((0, SUBLANES - bsz % SUBLANES if bsz % SUBLANES else 0), (0, 0)))
    mod = _mod_call(c_pad, mod_w[0], mod_b[0][None, :])[:bsz]
    return _layer(x, mod, norm1_g[0], norm2_g[0], w_in, conv_w[0], conv_b[0], dt_bias[0], a_log[0], d_skip[0],
                  ssm_norm_g[0], att_norm_g[0], sinks[0], rel_bias, w_out[0], router_w[0], router_bias[0],
                  exp_w_gate[0], exp_w_up[0], exp_w_down[0], sh_w_gate[0], sh_w_up[0], sh_w_down[0], final_g)
```

```python
import functools
import math

import numpy as np
import jax
import jax.numpy as jnp
from jax import lax
from jax.experimental import pallas as pl
from jax.experimental.pallas import tpu as pltpu
from jax.experimental.pallas import tpu_sc as plsc

F32 = jnp.float32
BF16 = jnp.bfloat16

D_MODEL = 1024
SSM_HEAD_DIM = 64
D_SSM = D_MODEL
SSM_HEADS = D_SSM // SSM_HEAD_DIM
SSM_GROUPS = 4
D_STATE = 128
CONV_K = 4
CONV_CH = D_SSM + 2 * SSM_GROUPS * D_STATE
CHUNK = 128
ATT_HEAD_DIM = 64
D_ATT = D_MODEL
ATT_HEADS = D_ATT // ATT_HEAD_DIM
KV_HEADS = ATT_HEADS // 4
Q_PER_KV = ATT_HEADS // KV_HEADS
D_KV = KV_HEADS * ATT_HEAD_DIM
WINDOW = 128
ATT_BLOCK = 128
REL_BUCKETS = 32
REL_MAX_DIST = 128
N_EXPERTS = 64
TOP_K = 8
EXPERT_DIM = D_MODEL // 4
SHARED_DIM = D_MODEL // 4
ROUTE_GROUPS = 8
ROUTE_TOPK_GROUPS = 4
ROUTED_SCALE = 2.5
EPS = 1e-6

LANES = 128
SUBLANES = 8
HALF = LANES // 2

ROW_TILE = 512
ROUTE_TILE = 2048
RANK_SUB = 512
COMBINE_TILE = 256
SC_CHUNK = 128
EXPERT_ROWS = 512
COMBINE_GROUPS = 2
VMEM_LIMIT = 48 * 1024 * 1024

NEG_INF = float("-inf")


def _silu(v):
    return v * (1.0 / (1.0 + jnp.exp(-v)))


def _softplus(v):
    return jnp.maximum(v, 0.0) + jnp.log(1.0 + jnp.exp(-jnp.abs(v)))


def _bdot(a, b):
    return jnp.dot(a.astype(BF16), b.astype(BF16), preferred_element_type=F32)


def _split_hi_lo(v):
    hi = v.astype(BF16)
    lo = (v - hi.astype(F32)).astype(BF16)
    return hi, lo


def _pack_bf16_pair(a, b):
    w = pltpu.pack_elementwise([a, b], packed_dtype=BF16)
    return w if w.dtype == jnp.uint32 else lax.bitcast_convert_type(w, jnp.uint32)


def _unpack_bf16_pair(w):
    a = pltpu.unpack_elementwise(w, index=0, packed_dtype=BF16, unpacked_dtype=F32)
    b = pltpu.unpack_elementwise(w, index=1, packed_dtype=BF16, unpacked_dtype=F32)
    return a, b


def _lane_half_mask(shape):
    return lax.broadcasted_iota(jnp.int32, shape, len(shape) - 1) < HALF


def _mod_kernel(c_ref, w_ref, b_ref, o_ref):
    a = _silu(c_ref[...])
    o_ref[...] = jnp.dot(a, w_ref[...], precision=lax.Precision.HIGHEST,
                         preferred_element_type=F32) + b_ref[...]


def _mod_call(c_pad, mod_w, mod_b):
    rows, d = c_pad.shape
    cols = mod_w.shape[1]
    return pl.pallas_call(
        _mod_kernel,
        grid=(cols // d,),
        in_specs=[pl.BlockSpec((rows, d), lambda j: (0, 0)),
                  pl.BlockSpec((d, d), lambda j: (0, j)),
                  pl.BlockSpec((1, d), lambda j: (0, j))],
        out_specs=pl.BlockSpec((rows, d), lambda j: (0, j)),
        out_shape=jax.ShapeDtypeStruct((rows, cols), F32),
        compiler_params=pltpu.CompilerParams(dimension_semantics=("arbitrary",),
                                             vmem_limit_bytes=VMEM_LIMIT),
    )(c_pad, mod_w, mod_b)


IN_PROJ_SEGMENTS = ((D_SSM, BF16), (CONV_CH, BF16), (LANES, F32), (D_ATT, BF16), (D_KV, BF16), (D_KV, BF16))


def _in_proj_kernel(x_ref, sc_ref, sh_ref, g_ref, w_hbm, *refs):
    out_refs = refs[:len(IN_PROJ_SEGMENTS)]
    wraw_ref, w_ref, wsem = refs[len(IN_PROJ_SEGMENTS):]

    @pl.when(pl.program_id(0) == 0)
    def _():
        cp = pltpu.make_async_copy(w_hbm.at[0], wraw_ref, wsem)
        cp.start()
        cp.wait()
        src_dt = D_SSM + CONV_CH
        dst_q = src_dt + LANES
        src_q = src_dt + SSM_HEADS
        q_scale = ATT_HEAD_DIM ** -0.5

        def put(dst, src, scale=None):
            t = wraw_ref[src:src + LANES, :].T
            w_ref[:, dst:dst + LANES] = (t if scale is None else t * scale).astype(BF16)

        for c0 in range(0, src_dt, LANES):
            put(c0, c0)
        dt_tile = wraw_ref[src_dt:src_dt + LANES, :].T
        lane = lax.broadcasted_iota(jnp.int32, dt_tile.shape, 1)
        w_ref[:, src_dt:dst_q] = jnp.where(lane < SSM_HEADS, dt_tile, 0.0).astype(BF16)
        for c0 in range(0, D_ATT, LANES):
            put(dst_q + c0, src_q + c0, q_scale)
        for c0 in range(D_ATT, D_ATT + 2 * D_KV, LANES):
            put(dst_q + c0, src_q + c0)

    xf = x_ref[...]
    ms = jnp.mean(xf * xf, axis=-1, keepdims=True)
    h = xf * lax.rsqrt(ms + EPS) * g_ref[...]
    h = h * (1.0 + sc_ref[0]) + sh_ref[0]
    hb = h.astype(BF16)
    col = 0
    for (width, dtype), o_ref in zip(IN_PROJ_SEGMENTS, out_refs):
        o_ref[...] = jnp.dot(hb, w_ref[:, col:col + width], preferred_element_type=F32).astype(dtype)
        col += width


def _in_proj_call(x2, sc1, sh1, g1n, w_in_t, tiles_per_batch, tm):
    n, d = x2.shape
    row = lambda w: pl.BlockSpec((tm, w), lambda i: (i, 0))
    full = lambda a: pl.BlockSpec(a.shape, lambda i: (0, 0))
    per_batch = pl.BlockSpec((1, 1, d), lambda i: (i // tiles_per_batch, 0, 0))
    cols = sum(w for w, _ in IN_PROJ_SEGMENTS)
    w_in = w_in_t
    assert w_in.shape == (1, cols - (LANES - SSM_HEADS), d)
    return pl.pallas_call(
        _in_proj_kernel,
        grid=(n // tm,),
        in_specs=[row(d), per_batch, per_batch, full(g1n), pl.BlockSpec(memory_space=pl.ANY)],
        out_specs=[row(w) for w, _ in IN_PROJ_SEGMENTS],
        out_shape=[jax.ShapeDtypeStruct((n, w), dt) for w, dt in IN_PROJ_SEGMENTS],
        scratch_shapes=[pltpu.VMEM(w_in.shape[1:], w_in.dtype), pltpu.VMEM((d, cols), BF16), pltpu.SemaphoreType.DMA(())],
        compiler_params=pltpu.CompilerParams(dimension_semantics=("arbitrary",),
                                             vmem_limit_bytes=VMEM_LIMIT),
    )(x2, sc1, sh1, g1n, w_in)


SSD_SEQS = 4
CONV_HALO = 16


def _conv_shift_matrix():
    s = np.zeros((CONV_K * CHUNK, CONV_HALO + CHUNK), np.float32)
    for k in range(CONV_K):
        t = np.arange(CHUNK)
        s[k * CHUNK + t, CONV_HALO + t - (CONV_K - 1) + k] = 1.0
    return s


def _silu_tanh(v):
    hv = 0.5 * v
    return hv + hv * jnp.tanh(hv)


def _ssd_kernel(xbc_ref, z_ref, dt_ref, cw_ref, cb_ref, dtb_ref, alog_ref, dskip_ref, ng_ref, triu_ref, shift_ref,
                y_ref, state_ref, ucat_ref, ybuf_ref):
    nseq = xbc_ref.shape[0]

    @pl.when(pl.program_id(1) == 0)
    def _():
        state_ref[...] = jnp.zeros_like(state_ref)
        ucat_ref[:, 0:CONV_HALO, :] = jnp.zeros((nseq, CONV_HALO, CONV_CH), BF16)

    for q in range(nseq):
        _ssd_chunk(xbc_ref.at[q], z_ref.at[q], dt_ref.at[q], cw_ref, cb_ref, dtb_ref, alog_ref, dskip_ref, ng_ref,
                   triu_ref, shift_ref, y_ref.at[q], state_ref.at[q], ucat_ref.at[q], ybuf_ref.at[q])


def _ssd_chunk(xbc_ref, z_ref, dt_ref, cw_ref, cb_ref, dtb_ref, alog_ref, dskip_ref, ng_ref, triu_ref, shift_ref,
               y_ref, state_ref, ucat_ref, ybuf_ref):
    ucat_ref[CONV_HALO:, :] = xbc_ref[...]
    shifted = jnp.dot(shift_ref[...], ucat_ref[...], preferred_element_type=F32)
    ucat_ref[0:CONV_HALO, :] = ucat_ref[CHUNK:CHUNK + CONV_HALO, :]
    acc = cb_ref[...] + cw_ref[0:1, :] * shifted[0:CHUNK]
    for kk in range(1, CONV_K):
        acc = acc + cw_ref[kk:kk + 1, :] * shifted[kk * CHUNK:(kk + 1) * CHUNK]
    act = _silu_tanh(acc)
    xs = act[:, :D_SSM]
    gn = SSM_GROUPS * D_STATE

    dt_t = _softplus(dt_ref[...].T[0:SSM_HEADS, :] + dtb_ref[...])
    a_t = dt_t * (-jnp.exp(alog_ref[...]))
    a_hi = a_t.astype(BF16)
    a_mid = (a_t - a_hi.astype(F32)).astype(BF16)
    a_lo = (a_t - a_hi.astype(F32) - a_mid.astype(F32)).astype(BF16)
    triu = triu_ref[...]
    cs_t = (jnp.dot(a_hi, triu, preferred_element_type=F32) + jnp.dot(a_mid, triu, preferred_element_type=F32)
            + jnp.dot(a_lo, triu, preferred_element_type=F32))
    cs_end = cs_t[:, CHUNK - 1:CHUNK]
    r_t = cs_t - jnp.log(dt_t)
    w_t = jnp.exp(cs_end - cs_t) * dt_t
    chunk_decay = jnp.exp(cs_end)
    cols = jnp.concatenate([cs_t, jnp.exp(cs_t), jnp.zeros((LANES - 2 * SSM_HEADS, CHUNK), F32)], axis=0).T

    li = lax.broadcasted_iota(jnp.int32, (CHUNK, CHUNK), 0)
    si = lax.broadcasted_iota(jnp.int32, (CHUNK, CHUNK), 1)
    causal = li >= si
    low = _lane_half_mask((CHUNK, LANES))
    low_row = _lane_half_mask((1, LANES))

    heads_per_group = SSM_HEADS // SSM_GROUPS
    for g in range(SSM_GROUPS):
        b_g = act[:, D_SSM + g * D_STATE:D_SSM + (g + 1) * D_STATE]
        c_g = act[:, D_SSM + gn + g * D_STATE:D_SSM + gn + (g + 1) * D_STATE]
        b_gb = b_g.astype(BF16)
        c_gb = c_g.astype(BF16)
        cb = lax.dot_general(c_gb, b_gb, (((1,), (1,)), ((), ())), preferred_element_type=F32)
        b_t = b_g.T
        for jp in range(heads_per_group // 2):
            j = g * (heads_per_group // 2) + jp
            lanes = slice(j * LANES, (j + 1) * LANES)
            xp = xs[:, lanes]
            xpb = xp.astype(BF16)
            ydiag = jnp.zeros((CHUNK, LANES), F32)
            snew = jnp.zeros((D_STATE, LANES), F32)
            for half in range(2):
                h = 2 * j + half
                diff = cols[:, h:h + 1] - r_t[h:h + 1, :]
                m = (cb * jnp.exp(jnp.where(causal, diff, NEG_INF))).astype(BF16)
                keep = low if half == 0 else jnp.logical_not(low)
                xh = jnp.where(keep, xpb, jnp.zeros_like(xpb))
                ydiag = ydiag + jnp.dot(m, xh, preferred_element_type=F32)
                snew = snew + jnp.dot((b_t * w_t[h:h + 1, :]).astype(BF16), xh, preferred_element_type=F32)
            s_in = state_ref[:, lanes]
            yoff = jnp.dot(c_gb, s_in.astype(BF16), preferred_element_type=F32)
            h0 = 2 * j
            e0 = SSM_HEADS + h0
            escale = jnp.where(low, cols[:, e0:e0 + 1], cols[:, e0 + 1:e0 + 2])
            cdec = jnp.where(low_row, chunk_decay[h0:h0 + 1, :], chunk_decay[h0 + 1:h0 + 2, :])
            ybuf_ref[:, lanes] = ydiag + yoff * escale + xp * dskip_ref[:, lanes]
            state_ref[:, lanes] = s_in * cdec + snew

    yz = ybuf_ref[...] * _silu_tanh(z_ref[...].astype(F32))
    gw = D_SSM // SSM_GROUPS
    for g in range(SSM_GROUPS):
        part = yz[:, g * gw:(g + 1) * gw]
        ms = jnp.mean(part * part, axis=-1, keepdims=True)
        y_ref[:, g * gw:(g + 1) * gw] = (part * lax.rsqrt(ms + EPS)
                                            * ng_ref[:, g * gw:(g + 1) * gw]).astype(BF16)


def _ssd_call(xbc, z, dt, conv_w, conv_b, dtb, alog, dskip, ng, triu, shift):
    bsz, l, _ = xbc.shape
    nc = l // CHUNK
    nseq = SSD_SEQS if bsz % SSD_SEQS == 0 else 1
    chunk = lambda w: pl.BlockSpec((nseq, CHUNK, w), lambda b, c: (b, c, 0))
    full = lambda a: pl.BlockSpec(a.shape, lambda b, c: (0, 0))
    return pl.pallas_call(
        _ssd_kernel,
        grid=(bsz // nseq, nc),
        in_specs=[chunk(CONV_CH), chunk(D_SSM), chunk(LANES), full(conv_w), full(conv_b), full(dtb),
                  full(alog), full(dskip), full(ng), full(triu), full(shift)],
        out_specs=chunk(D_SSM),
        out_shape=jax.ShapeDtypeStruct((bsz, l, D_SSM), BF16),
        scratch_shapes=[pltpu.VMEM((nseq, D_STATE, D_SSM), F32),
                        pltpu.VMEM((nseq, CONV_HALO + CHUNK, CONV_CH), BF16),
                        pltpu.VMEM((nseq, CHUNK, D_SSM), F32)],
        compiler_params=pltpu.CompilerParams(dimension_semantics=("arbitrary", "arbitrary"),
                                             vmem_limit_bytes=VMEM_LIMIT),
    )(xbc, z, dt, conv_w, conv_b, dtb, alog, dskip, ng, triu, shift)


assert WINDOW == ATT_BLOCK


def _rel_bucket_table():
    qi = np.arange(ATT_BLOCK)[:, None]
    c = np.arange(ATT_BLOCK)[None, :]
    dist = np.where(c > qi, qi + ATT_BLOCK - c, qi - c)
    max_exact = REL_BUCKETS // 2
    d = np.maximum(dist, 1).astype(np.float32)
    large = max_exact + (np.log(d / np.float32(max_exact)) / np.float32(math.log(REL_MAX_DIST / max_exact))
                         * np.float32(REL_BUCKETS - max_exact)).astype(np.int32)
    large = np.minimum(large, REL_BUCKETS - 1)
    return np.where(dist < max_exact, dist, large).astype(np.int32)


def _bias_kernel(rb_ref, bucket_ref, o_ref):
    bucket = bucket_ref[...]
    from_prev = (lax.broadcasted_iota(jnp.int32, bucket.shape, 1)
                 > lax.broadcasted_iota(jnp.int32, bucket.shape, 0))
    for h in range(ATT_HEADS):
        acc = jnp.zeros(bucket.shape, F32)
        for b in range(REL_BUCKETS):
            acc = jnp.where(bucket == b, rb_ref[b, h], acc)
        o_ref[1, h] = acc
        o_ref[0, h] = jnp.where(from_prev, NEG_INF, acc)


def _bias_call(rel_bias, bucket):
    return pl.pallas_call(
        _bias_kernel,
        in_specs=[pl.BlockSpec(memory_space=pltpu.SMEM), pl.BlockSpec(memory_space=pltpu.VMEM)],
        out_shape=jax.ShapeDtypeStruct((2, ATT_HEADS) + bucket.shape, F32),
    )(rel_bias, bucket)


ATT_SEQS = 4


def _attn_kernel(sink_ref, q_ref, kp_ref, kc_ref, vp_ref, vc_ref, bias_ref, ng_ref, o_ref, obuf_ref):
    for s in range(q_ref.shape[0]):
        _attn_block(sink_ref, q_ref.at[s], kp_ref.at[s], kc_ref.at[s], vp_ref.at[s], vc_ref.at[s], bias_ref.at[0],
                    ng_ref, o_ref.at[s], obuf_ref.at[s])


def _attn_block(sink_ref, q_ref, kp_ref, kc_ref, vp_ref, vc_ref, bias_ref, ng_ref, o_ref, obuf_ref):
    qi = lax.broadcasted_iota(jnp.int32, (ATT_BLOCK, ATT_BLOCK), 0)
    ci = lax.broadcasted_iota(jnp.int32, (ATT_BLOCK, ATT_BLOCK), 1)
    from_prev = ci > qi
    low = _lane_half_mask((ATT_BLOCK, LANES))

    def band_variants(prev_ref, cur_ref):
        out = []
        for cpair in range(KV_HEADS // 2):
            lanes = slice(cpair * LANES, (cpair + 1) * LANES)
            t = jnp.concatenate([prev_ref[:, lanes], cur_ref[:, lanes]], axis=0).astype(F32)
            out.append((t.astype(BF16), pltpu.roll(t, HALF, 1).astype(BF16)))
        return out

    k_band = band_variants(kp_ref, kc_ref)
    v_band = band_variants(vp_ref, vc_ref)
    nt = (((1,), (1,)), ((), ()))

    for j in range(ATT_HEADS // 2):
        qp = q_ref[:, j * LANES:(j + 1) * LANES]
        out_pair = jnp.zeros((ATT_BLOCK, LANES), F32)
        for half in range(2):
            h = 2 * j + half
            g = h // Q_PER_KV
            swapped = int((g % 2) != half)
            keep = low if half == 0 else jnp.logical_not(low)
            qh = jnp.where(keep, qp, jnp.zeros_like(qp))
            s_band = lax.dot_general(qh, k_band[g // 2][swapped], nt, preferred_element_type=F32)
            s = jnp.where(from_prev, s_band[:, :ATT_BLOCK], s_band[:, ATT_BLOCK:]) + bias_ref[h]
            sink = sink_ref[h]
            m = jnp.maximum(jnp.max(s, axis=-1, keepdims=True), sink)
            p = jnp.exp(s - m)
            denom = jnp.sum(p, axis=-1, keepdims=True) + jnp.exp(sink - m)
            p_band = jnp.concatenate([jnp.where(from_prev, p, 0.0), jnp.where(from_prev, 0.0, p)], axis=1)
            o = jnp.dot(p_band.astype(BF16), v_band[g // 2][swapped], preferred_element_type=F32) / denom
            out_pair = out_pair + jnp.where(keep, o, 0.0)
        obuf_ref[:, j * LANES:(j + 1) * LANES] = out_pair

    att = obuf_ref[...]
    ms = jnp.mean(att * att, axis=-1, keepdims=True)
    o_ref[...] = (att * lax.rsqrt(ms + EPS) * ng_ref[...]).astype(BF16)


def _attn_call(sinks, q, k, v, bias, ng):
    bsz, l, _ = q.shape
    nb = l // ATT_BLOCK
    nseq = ATT_SEQS if bsz % ATT_SEQS == 0 else 1
    cur = lambda w: pl.BlockSpec((nseq, ATT_BLOCK, w), lambda b, i: (b, i, 0))
    prev = lambda w: pl.BlockSpec((nseq, ATT_BLOCK, w), lambda b, i: (b, jnp.maximum(i - 1, 0), 0))
    return pl.pallas_call(
        _attn_kernel,
        grid=(bsz // nseq, nb),
        in_specs=[pl.BlockSpec(memory_space=pltpu.SMEM),
                  cur(D_ATT), prev(D_KV), cur(D_KV), prev(D_KV), cur(D_KV),
                  pl.BlockSpec((1,) + bias.shape[1:], lambda b, i: (jnp.minimum(i, 1), 0, 0, 0)),
                  pl.BlockSpec(ng.shape, lambda b, i: (0, 0))],
        out_specs=cur(D_ATT),
        out_shape=jax.ShapeDtypeStruct((bsz, l, D_ATT), BF16),
        scratch_shapes=[pltpu.VMEM((nseq, ATT_BLOCK, D_ATT), F32)],
        compiler_params=pltpu.CompilerParams(dimension_semantics=("arbitrary", "arbitrary"),
                                             vmem_limit_bytes=VMEM_LIMIT),
    )(sinks, q, k, k, v, v, bias, ng)


def _out_proj_kernel(x_ref, ys_ref, ya_ref, g1_ref, sc_ref, sh_ref, g2_ref, ng_ref, wo_hbm, sg_hbm, su_hbm, sd_hbm,
                     rwh_ref, rwl_ref, rb_ref, upper_ref,
                     base_ref, h_ref, idx_ref, gate_ref, rank_ref, cnt_ref,
                     carry_ref, wo_raw, sg_raw, su_raw, sd_raw, wo_ref, sg_ref, su_ref, sd_ref, wsem):
    @pl.when(pl.program_id(0) == 0)
    def _():
        carry_ref[...] = jnp.zeros_like(carry_ref)
        staged = ((wo_hbm, wo_raw, wo_ref), (sg_hbm, sg_raw, sg_ref), (su_hbm, su_raw, su_ref),
                  (sd_hbm, sd_raw, sd_ref))
        copies = [pltpu.make_async_copy(src.at[0], raw, wsem.at[j]) for j, (src, raw, _) in enumerate(staged)]
        for cp in copies:
            cp.start()
        for cp, (_, raw, dst) in zip(copies, staged):
            cp.wait()
            dst[...] = raw[...].astype(BF16)

    mix = (jnp.dot(ys_ref[...], wo_ref[:D_SSM, :], preferred_element_type=F32)
           + jnp.dot(ya_ref[...], wo_ref[D_SSM:, :], preferred_element_type=F32))
    x1 = x_ref[...] + g1_ref[0] * mix
    ms = jnp.mean(x1 * x1, axis=-1, keepdims=True)
    h = x1 * lax.rsqrt(ms + EPS) * ng_ref[...]
    h = h * (1.0 + sc_ref[0]) + sh_ref[0]
    half = h.shape[1] // 2
    h_ref[...] = _pack_bf16_pair(h[:, :half], h[:, half:])
    hi, lo = _split_hi_lo(h)
    hi_terms = jnp.dot(hi, rwl_ref[...], preferred_element_type=F32)
    logits = (hi_terms[:, :LANES] + hi_terms[:, LANES:]
              + jnp.dot(lo, rwh_ref[...], preferred_element_type=F32))
    _route_tokens(logits.T[0:N_EXPERTS, :], rb_ref, upper_ref, idx_ref, gate_ref, rank_ref, cnt_ref, carry_ref)
    u = _silu(jnp.dot(hi, sg_ref[...], preferred_element_type=F32)) * jnp.dot(hi, su_ref[...],
                                                                              preferred_element_type=F32)
    shared = jnp.dot(u.astype(BF16), sd_ref[...], preferred_element_type=F32)
    base_ref[...] = x1 + g2_ref[0] * shared


def _out_proj_call(x2, ys, ya, g1, sc2, sh2, g2, ng, w_out, sg, su, sd, rwh, rwl, router_bias, upper,
                   tiles_per_batch, tm):
    n, d = x2.shape
    row = lambda w: pl.BlockSpec((tm, w), lambda i: (i, 0))
    tok = lambda r: pl.BlockSpec((r, tm), lambda i: (0, i))
    full = lambda a: pl.BlockSpec(a.shape, lambda i: (0, 0))
    per_batch = pl.BlockSpec((1, 1, d), lambda i: (i // tiles_per_batch, 0, 0))
    hbm = pl.BlockSpec(memory_space=pl.ANY)
    staged = (w_out, sg, su, sd)
    return pl.pallas_call(
        _out_proj_kernel,
        grid=(n // tm,),
        in_specs=[row(d), row(D_SSM), row(D_ATT), per_batch, per_batch, per_batch, per_batch, full(ng),
                  hbm, hbm, hbm, hbm, full(rwh), full(rwl), full(router_bias), full(upper)],
        out_specs=[row(d), row(d // 2), tok(TOP_K), tok(TOP_K), tok(TOP_K),
                   pl.BlockSpec((N_EXPERTS, LANES), lambda i: (0, 0))],
        out_shape=[jax.ShapeDtypeStruct((n, d), F32), jax.ShapeDtypeStruct((n, d // 2), jnp.uint32),
                   jax.ShapeDtypeStruct((TOP_K, n), jnp.int32), jax.ShapeDtypeStruct((TOP_K, n), F32),
                   jax.ShapeDtypeStruct((TOP_K, n), jnp.int32), jax.ShapeDtypeStruct((N_EXPERTS, LANES), F32)],
        scratch_shapes=([pltpu.VMEM((N_EXPERTS, LANES), F32)]
                        + [pltpu.VMEM(a.shape[1:], a.dtype) for a in staged]
                        + [pltpu.VMEM(a.shape[1:], BF16) for a in staged]
                        + [pltpu.SemaphoreType.DMA((len(staged),))]),
        compiler_params=pltpu.CompilerParams(dimension_semantics=("arbitrary",),
                                             vmem_limit_bytes=VMEM_LIMIT),
    )(x2, ys, ya, g1, sc2, sh2, g2, ng, w_out, sg, su, sd, rwh, rwl, router_bias, upper)


def _route_tokens(logits_t, rb_ref, upper_ref, idx_ref, gate_ref, rank_ref, cnt_ref, carry_ref):
    t = logits_t.shape[1]
    per_group = N_EXPERTS // ROUTE_GROUPS
    scores = 1.0 / (1.0 + jnp.exp(-logits_t))
    sel = scores + rb_ref[...]
    e_iota = lax.broadcasted_iota(jnp.int32, (N_EXPERTS, t), 0)

    sel3 = sel.reshape(ROUTE_GROUPS, per_group, t)
    w_iota = lax.broadcasted_iota(jnp.int32, sel3.shape, 1)
    m1 = jnp.max(sel3, axis=1, keepdims=True)
    first = jnp.min(jnp.where(sel3 == m1, w_iota, per_group), axis=1, keepdims=True)
    m2 = jnp.max(jnp.where(w_iota == first, NEG_INF, sel3), axis=1, keepdims=True)
    grp = (m1 + m2).reshape(ROUTE_GROUPS, t)

    g_iota = lax.broadcasted_iota(jnp.int32, (ROUTE_GROUPS, t), 0)
    gmask = jnp.zeros((ROUTE_GROUPS, t), jnp.bool_)
    for _ in range(ROUTE_TOPK_GROUPS):
        gm = jnp.max(grp, axis=0, keepdims=True)
        gfirst = jnp.min(jnp.where(grp == gm, g_iota, ROUTE_GROUPS), axis=0, keepdims=True)
        hit = g_iota == gfirst
        gmask = jnp.logical_or(gmask, hit)
        grp = jnp.where(hit, NEG_INF, grp)
    allowed = jnp.broadcast_to(gmask.reshape(ROUTE_GROUPS, 1, t),
                               (ROUTE_GROUPS, per_group, t)).reshape(N_EXPERTS, t)
    masked = jnp.where(allowed, sel, NEG_INF)

    picked = jnp.zeros((N_EXPERTS, t), jnp.bool_)
    idx_rows = []
    w_rows = []
    for _ in range(TOP_K):
        mm = jnp.max(masked, axis=0, keepdims=True)
        efirst = jnp.min(jnp.where(masked == mm, e_iota, N_EXPERTS), axis=0, keepdims=True)
        hit = e_iota == efirst
        idx_rows.append(efirst)
        w_rows.append(jnp.sum(jnp.where(hit, scores, 0.0), axis=0, keepdims=True))
        picked = jnp.logical_or(picked, hit)
        masked = jnp.where(hit, NEG_INF, masked)
    idx = jnp.concatenate(idx_rows, axis=0)
    w = jnp.concatenate(w_rows, axis=0)
    gate_ref[...] = w / jnp.sum(w, axis=0, keepdims=True) * ROUTED_SCALE
    idx_ref[...] = idx

    onehot = jnp.where(picked, 1.0, 0.0)
    sub = upper_ref.shape[0]
    carry = carry_ref[:, 0:1]
    parts = []
    for s0 in range(0, t, sub):
        oh = onehot[:, s0:s0 + sub]
        parts.append(jnp.dot(oh.astype(BF16), upper_ref[...], preferred_element_type=F32) + carry)
        carry = carry + jnp.sum(oh, axis=1, keepdims=True)
    rank_full = jnp.concatenate(parts, axis=1)
    rank_rows = [jnp.sum(jnp.where(e_iota == idx_rows[k], rank_full, 0.0), axis=0, keepdims=True)
                 for k in range(TOP_K)]
    rank_ref[...] = jnp.concatenate(rank_rows, axis=0).astype(jnp.int32)
    carry_ref[...] = jnp.broadcast_to(carry, carry_ref.shape)
    cnt_ref[...] = carry_ref[...]


PLAN_EXPERT, PLAN_FRESH, PLAN_VALID, PLAN_SEG, PLAN_NEXT, PLAN_NUSED = range(6)
PLAN_ROWS = SUBLANES


def _plan_kernel(cnt_ref, tri_ref, start_ref, plan_ref, *, nblocks):
    nbp = plan_ref.shape[1]
    cnt = cnt_ref[...].astype(jnp.int32)
    blocks = (cnt + (EXPERT_ROWS - 1)) // EXPERT_ROWS
    end = jnp.dot(tri_ref[...], blocks.astype(F32), precision=lax.Precision.HIGHEST,
                  preferred_element_type=F32).astype(jnp.int32)
    start = end - blocks
    start_ref[...] = start[:, 0:1] * EXPERT_ROWS
    nused = end[N_EXPERTS - 1:N_EXPERTS, 0:1]

    e_iota = lax.broadcasted_iota(jnp.int32, (N_EXPERTS, nbp), 0)
    blk = lax.broadcasted_iota(jnp.int32, (1, nbp), 1)
    expert = jnp.minimum(jnp.sum((end[:, 0:1] <= blk).astype(jnp.int32), axis=0, keepdims=True), N_EXPERTS - 1)
    mine = e_iota == expert
    pick = lambda col: jnp.sum(jnp.where(mine, col, 0), axis=0, keepdims=True)
    first = pick(start[:, 0:1])
    valid = jnp.clip(pick(cnt[:, 0:1]) - (blk - first) * EXPERT_ROWS, 0, EXPERT_ROWS)
    present = jnp.logical_or(blocks[:, 0:1] > 0,
                             jnp.logical_and(e_iota[:, 0:1] == N_EXPERTS - 1, nused < nblocks))
    seg = jnp.sum(jnp.logical_and(present, e_iota <= expert).astype(jnp.int32), axis=0, keepdims=True) - 1
    nxt = jnp.min(jnp.where(jnp.logical_and(present, e_iota > expert), e_iota, N_EXPERTS), axis=0, keepdims=True)
    rows = {PLAN_EXPERT: expert, PLAN_FRESH: (blk == first).astype(jnp.int32), PLAN_VALID: valid, PLAN_SEG: seg,
            PLAN_NEXT: jnp.where(nxt == N_EXPERTS, -1, nxt), PLAN_NUSED: jnp.broadcast_to(nused, (1, nbp))}
    zero = jnp.zeros((1, nbp), jnp.int32)
    plan_ref[...] = jnp.concatenate([rows.get(r, zero) for r in range(PLAN_ROWS)], axis=0)


def _plan_call(counts, nblocks):
    nbp = -(-nblocks // LANES) * LANES
    tri = jnp.asarray(np.tril(np.ones((N_EXPERTS, N_EXPERTS), np.float32)))
    return pl.pallas_call(
        functools.partial(_plan_kernel, nblocks=nblocks),
        out_shape=[jax.ShapeDtypeStruct((N_EXPERTS, 1), jnp.int32),
                   jax.ShapeDtypeStruct((PLAN_ROWS, nbp), jnp.int32)],
    )(counts, tri)


def _dest_kernel(idx_ref, rank_ref, start_ref, dest_ref):
    t = idx_ref.shape[1]
    e_iota = lax.broadcasted_iota(jnp.int32, (N_EXPERTS, t), 0)
    rows = [jnp.sum(jnp.where(e_iota == idx_ref[k:k + 1, :], start_ref[...], 0), axis=0, keepdims=True)
            for k in range(TOP_K)]
    dest = jnp.concatenate(rows, axis=0) + rank_ref[...]
    for k in range(TOP_K):
        for c in range(t // LANES):
            dest_ref[k, c:c + 1, :] = dest[k:k + 1, c * LANES:(c + 1) * LANES]


def _dest_call(idx, rank, pad_start, tile):
    n = idx.shape[1]
    tok = pl.BlockSpec((TOP_K, tile), lambda i: (0, i))
    return pl.pallas_call(
        _dest_kernel,
        grid=(n // tile,),
        in_specs=[tok, tok, pl.BlockSpec((N_EXPERTS, 1), lambda i: (0, 0))],
        out_specs=pl.BlockSpec((TOP_K, tile // LANES, LANES), lambda i: (0, i, 0)),
        out_shape=jax.ShapeDtypeStruct((TOP_K, n // LANES, LANES), jnp.int32),
        compiler_params=pltpu.CompilerParams(dimension_semantics=("arbitrary",)),
    )(idx, rank, pad_start)


def _scatter_rows_sc(rows, dest_flat, total_rows, chunk):
    n, w = rows.shape
    copies = dest_flat.shape[0] // n
    info = plsc.get_sparse_core_info()
    nc = info.num_cores
    per_worker = n // (nc * info.num_subcores)
    assert per_worker * nc * info.num_subcores == n and per_worker % chunk == 0
    mesh = plsc.VectorSubcoreMesh(core_axis_name="c", subcore_axis_name="s")

    @functools.partial(
        pl.kernel, mesh=mesh,
        out_type=jax.ShapeDtypeStruct((total_rows, w), rows.dtype),
        scratch_types=[pltpu.VMEM((chunk,), jnp.int32), pltpu.VMEM((chunk, w), rows.dtype),
                       pltpu.SemaphoreType.DMA],
    )
    def scatter(rows_hbm, idx_hbm, out_hbm, idx_v, rows_v, sem):
        worker = lax.axis_index("s") * nc + lax.axis_index("c")

        @pl.loop(0, per_worker // chunk)
        def _(j):
            base = worker * per_worker + j * chunk
            pltpu.sync_copy(rows_hbm.at[pl.ds(base, chunk)], rows_v)
            for k in range(copies):
                pltpu.sync_copy(idx_hbm.at[pl.ds(k * n + base, chunk)], idx_v)
                pltpu.async_copy(rows_v, out_hbm.at[idx_v], sem).wait()

    return scatter(rows, dest_flat)


X_RING = 3


def _expert_kernel(plan_ref, xs_hbm, wg_hbm, wu_hbm, wd_hbm, y_ref,
                   wgb_ref, wub_ref, wdb_ref, xbuf_ref, xsem, wg_raw, wu_raw, wd_raw, wsem):
    i = pl.program_id(0)
    rows, half = xbuf_ref.shape[1], xbuf_ref.shape[2]
    nused = plan_ref[PLAN_NUSED, 0]

    def w_copies(expert, slot):
        return [pltpu.make_async_copy(src.at[expert], dst.at[slot], wsem.at[slot])
                for src, dst in ((wg_hbm, wg_raw), (wu_hbm, wu_raw), (wd_hbm, wd_raw))]

    def x_copy(block):
        first = pl.multiple_of(block * rows, rows)
        slot = block % X_RING
        return pltpu.make_async_copy(xs_hbm.at[pl.ds(first, rows)], xbuf_ref.at[slot], xsem.at[slot])

    @pl.when(i == 0)
    def _():
        for b in range(X_RING - 1):
            @pl.when(b < nused)
            def _():
                x_copy(b).start()

    @pl.when(i + (X_RING - 1) < nused)
    def _():
        x_copy(i + (X_RING - 1)).start()

    @pl.when(i == 0)
    def _():
        for c in w_copies(plan_ref[PLAN_EXPERT, 0], 0):
            c.start()

    @pl.when(plan_ref[PLAN_FRESH, i] > 0)
    def _():
        slot = plan_ref[PLAN_SEG, i] % 2
        for c in w_copies(plan_ref[PLAN_EXPERT, i], slot):
            c.wait()

        @pl.when(plan_ref[PLAN_NEXT, i] >= 0)
        def _():
            for c in w_copies(plan_ref[PLAN_NEXT, i], 1 - slot):
                c.start()

        wgb_ref[...] = wg_raw[slot].astype(BF16)
        wub_ref[...] = wu_raw[slot].astype(BF16)
        wdb_ref[...] = wd_raw[slot].astype(BF16)

    @pl.when(i < nused)
    def _():
        x_copy(i).wait()
        xw = xbuf_ref[i % X_RING]
        row = lax.broadcasted_iota(jnp.int32, xw.shape, 0)
        x_lo, x_hi = _unpack_bf16_pair(jnp.where(row < plan_ref[PLAN_VALID, i], xw, jnp.uint32(0)))
        x_lo = x_lo.astype(BF16)
        x_hi = x_hi.astype(BF16)
        gate = (jnp.dot(x_lo, wgb_ref[:half, :], preferred_element_type=F32)
                + jnp.dot(x_hi, wgb_ref[half:, :], preferred_element_type=F32))
        up = (jnp.dot(x_lo, wub_ref[:half, :], preferred_element_type=F32)
              + jnp.dot(x_hi, wub_ref[half:, :], preferred_element_type=F32))
        u = (_silu(gate) * up).astype(BF16)
        y_lo = jnp.dot(u, wdb_ref[:, :half], preferred_element_type=F32)
        y_hi = jnp.dot(u, wdb_ref[:, half:], preferred_element_type=F32)
        y_ref[...] = _pack_bf16_pair(y_lo, y_hi)

    @pl.when(i >= nused)
    def _():
        y_ref[...] = jnp.zeros_like(y_ref)


def _expert_call(plan, xs, wg, wu, wd, rows):
    p, w = xs.shape
    d, f = wg.shape[1], wg.shape[2]
    hbm = pl.BlockSpec(memory_space=pl.ANY)
    grid_spec = pltpu.PrefetchScalarGridSpec(
        num_scalar_prefetch=1,
        grid=(p // rows,),
        in_specs=[hbm, hbm, hbm, hbm],
        out_specs=pl.BlockSpec((rows, w), lambda i, *_: (i, 0)),
        scratch_shapes=[pltpu.VMEM((d, f), BF16), pltpu.VMEM((d, f), BF16), pltpu.VMEM((f, d), BF16),
                        pltpu.VMEM((X_RING, rows, w), xs.dtype), pltpu.SemaphoreType.DMA((X_RING,)),
                        pltpu.VMEM((2, d, f), wg.dtype), pltpu.VMEM((2, d, f), wu.dtype),
                        pltpu.VMEM((2, f, d), wd.dtype), pltpu.SemaphoreType.DMA((2,))],
    )
    return pl.pallas_call(
        _expert_kernel,
        grid_spec=grid_spec,
        out_shape=jax.ShapeDtypeStruct((p, w), jnp.uint32),
        compiler_params=pltpu.CompilerParams(dimension_semantics=("arbitrary",),
                                             vmem_limit_bytes=VMEM_LIMIT),
    )(plan, xs, wg, wu, wd)


def _gather_rows_sc(table, idx, chunk):
    m = idx.shape[0]
    w = table.shape[1]
    info = plsc.get_sparse_core_info()
    nc = info.num_cores
    per_worker = m // (nc * info.num_subcores)
    assert per_worker * nc * info.num_subcores == m and per_worker % chunk == 0
    mesh = plsc.VectorSubcoreMesh(core_axis_name="c", subcore_axis_name="s")

    @functools.partial(
        pl.kernel, mesh=mesh,
        out_type=jax.ShapeDtypeStruct((m, w), table.dtype),
        scratch_types=[pltpu.VMEM((chunk,), jnp.int32), pltpu.VMEM((chunk, w), table.dtype),
                       pltpu.SemaphoreType.DMA],
    )
    def gather(table_hbm, idx_hbm, out_hbm, idx_v, rows_v, sem):
        worker = lax.axis_index("s") * nc + lax.axis_index("c")

        @pl.loop(0, per_worker // chunk)
        def _(j):
            base = worker * per_worker + j * chunk
            pltpu.sync_copy(idx_hbm.at[pl.ds(base, chunk)], idx_v)
            pltpu.async_copy(table_hbm.at[idx_v], rows_v, sem).wait()
            pltpu.sync_copy(rows_v, out_hbm.at[pl.ds(base, chunk)])

    return gather(table, idx)


def _combine_kernel(yk_ref, gate_ref, base_ref, g2_ref, fg_ref, o_ref):
    t = base_ref.shape[0]
    half = yk_ref.shape[2]
    gates = jnp.concatenate([gate_ref[...], jnp.zeros((LANES - TOP_K, t), F32)], axis=0).T
    r_lo = jnp.zeros((t, half), F32)
    r_hi = jnp.zeros((t, half), F32)
    for k in range(TOP_K):
        y_lo, y_hi = _unpack_bf16_pair(yk_ref[k])
        r_lo = r_lo + gates[:, k:k + 1] * y_lo
        r_hi = r_hi + gates[:, k:k + 1] * y_hi
    g2 = g2_ref[0]
    x_lo = base_ref[:, :half] + g2[:, :half] * r_lo
    x_hi = base_ref[:, half:] + g2[:, half:] * r_hi
    ms = (jnp.sum(x_lo * x_lo, axis=-1, keepdims=True)
          + jnp.sum(x_hi * x_hi, axis=-1, keepdims=True)) * (1.0 / (2 * half))
    inv = lax.rsqrt(ms + EPS)
    o_ref[:, :half] = x_lo * inv * fg_ref[:, :half]
    o_ref[:, half:] = x_hi * inv * fg_ref[:, half:]


def _combine_call(yk, gates, base, g2, fg, tiles_per_batch, tile, tile0):
    n, d = base.shape
    row = lambda w: pl.BlockSpec((tile, w), lambda i: (i + tile0, 0))
    return pl.pallas_call(
        _combine_kernel,
        grid=(yk.shape[1] // tile,),
        in_specs=[pl.BlockSpec((TOP_K, tile, yk.shape[2]), lambda i: (0, i, 0)),
                  pl.BlockSpec((TOP_K, tile), lambda i: (0, i + tile0)), row(d),
                  pl.BlockSpec((1, 1, d), lambda i: ((i + tile0) // tiles_per_batch, 0, 0)),
                  pl.BlockSpec((1, d), lambda i: (0, 0))],
        out_specs=row(d),
        out_shape=jax.ShapeDtypeStruct((n, d), F32),
        input_output_aliases={2: 0},
        compiler_params=pltpu.CompilerParams(dimension_semantics=("arbitrary",),
                                             vmem_limit_bytes=VMEM_LIMIT),
    )(yk, gates, base, g2, fg)


def _pad_cols(a, width):
    return jnp.pad(a, ((0, 0), (0, width - a.shape[1])))


def _layer(x, mod, norm1_g, norm2_g, w_in, conv_w, conv_b, dt_bias, a_log, d_skip, ssm_norm_g,
           att_norm_g, sinks, rel_bias, w_out, router_w, router_bias, exp_w_gate, exp_w_up, exp_w_down,
           sh_w_gate, sh_w_up, sh_w_down, final_g):
    bsz, l, d = x.shape
    n = bsz * l
    tm = min(ROW_TILE, l)

    sh1, sc1, g1, sh2, sc2, g2 = [m[:, None, :] for m in jnp.split(mod, 6, axis=-1)]

    assert math.frexp(ATT_HEAD_DIM ** -0.5)[0] == 0.5
    x2 = x.reshape(n, d)
    z, xbc, dt, q, k, v = _in_proj_call(x2, sc1, sh1, norm1_g[None, :], jnp.swapaxes(w_in, 1, 2), l // tm, tm)

    triu = jnp.asarray(np.triu(np.ones((CHUNK, CHUNK), np.float32))).astype(BF16)
    shift = jnp.asarray(_conv_shift_matrix()).astype(BF16)
    y_ssm = _ssd_call(xbc.reshape(bsz, l, CONV_CH), z.reshape(bsz, l, D_SSM), dt.reshape(bsz, l, LANES),
                      conv_w, conv_b[None, :], dt_bias[:, None], a_log[:, None],
                      jnp.repeat(d_skip, SSM_HEAD_DIM)[None, :], ssm_norm_g[None, :], triu, shift)

    bias = _bias_call(rel_bias, jnp.asarray(_rel_bucket_table()))
    y_att = _attn_call(sinks, q.reshape(bsz, l, D_ATT), k.reshape(bsz, l, D_KV), v.reshape(bsz, l, D_KV), bias,
                       att_norm_g[None, :])

    rw = _pad_cols(router_w, LANES)
    rwh = rw.astype(BF16)
    rwl = jnp.concatenate([rwh, (rw - rwh.astype(F32)).astype(BF16)], axis=1)
    rs = min(RANK_SUB, tm)
    upper = jnp.asarray(np.triu(np.ones((rs, rs), np.float32), 1)).astype(BF16)
    base, h2, idx, gates, rank, counts = _out_proj_call(
        x2, y_ssm.reshape(n, D_SSM), y_att.reshape(n, D_ATT), g1, sc2, sh2, g2, norm2_g[None, :],
        w_out, sh_w_gate, sh_w_up, sh_w_down, rwh, rwl, router_bias[:, None], upper, l // tm, tm)
    rt = min(ROUTE_TILE, n)

    nblocks = (n * TOP_K + N_EXPERTS * (EXPERT_ROWS - 1) + EXPERT_ROWS - 1) // EXPERT_ROWS
    pad_start, plan = _plan_call(counts, nblocks)
    dest = _dest_call(idx, rank, pad_start, rt)

    xs = _scatter_rows_sc(h2, dest.reshape(-1), nblocks * EXPERT_ROWS, SC_CHUNK)
    ys = _expert_call(plan, xs, exp_w_gate, exp_w_up, exp_w_down, EXPERT_ROWS)
    ctile = min(COMBINE_TILE, l)
    groups = COMBINE_GROUPS if bsz % COMBINE_GROUPS == 0 else 1
    ng = n // groups
    out = base
    for g in range(groups):
        idx_g = dest[:, g * ng // LANES:(g + 1) * ng // LANES, :].reshape(-1)
        yk = _gather_rows_sc(ys, idx_g, SC_CHUNK).reshape(TOP_K, ng, ys.shape[1])
        out = _combine_call(yk, gates, out, g2, final_g[None, :], l // ctile, ctile, g * ng // ctile)
    return out.reshape(bsz, l, d)


def kernel(x, c, mod_w, mod_b, norm1_g, norm2_g, w_in, conv_w, conv_b, dt_bias, a_log, d_skip, ssm_norm_g,
           att_norm_g, sinks, rel_bias, w_out, router_w, router_bias, exp_w_gate, exp_w_up, exp_w_down,
           sh_w_gate, sh_w_up, sh_w_down, final_g):
    assert mod_w.shape[0] == 1, "single-layer block"
    bsz = x.shape[0]
    c_pad = jnp.pad(c, ((0, SUBLANES - bsz % SUBLANES if bsz % SUBLANES else 0), (0, 0)))
    mod = _mod_call(c_pad, mod_w[0], mod_b[0][None, :])[:bsz]
    return _layer(x, mod, norm1_g[0], norm2_g[0], w_in, conv_w[0], conv_b[0], dt_bias[0], a_log[0], d_skip[0],
                  ssm_norm_g[0], att_norm_g[0], sinks[0], rel_bias, w_out, router_w[0], router_bias[0],
                  exp_w_gate[0], exp_w_up[0], exp_w_down[0], sh_w_gate, sh_w_up, sh_w_down, final_g)
```

```python
import functools
import math

import numpy as np
import jax
import jax.numpy as jnp
from jax import lax
from jax.experimental import pallas as pl
from jax.experimental.pallas import tpu as pltpu
from jax.experimental.pallas import tpu_sc as plsc

F32 = jnp.float32
BF16 = jnp.bfloat16

D_MODEL = 1024
SSM_HEAD_DIM = 64
D_SSM = D_MODEL
SSM_HEADS = D_SSM // SSM_HEAD_DIM
SSM_GROUPS = 4
D_STATE = 128
CONV_K = 4
CONV_CH = D_SSM + 2 * SSM_GROUPS * D_STATE
CHUNK = 128
ATT_HEAD_DIM = 64
D_ATT = D_MODEL
ATT_HEADS = D_ATT // ATT_HEAD_DIM
KV_HEADS = ATT_HEADS // 4
Q_PER_KV = ATT_HEADS // KV_HEADS
D_KV = KV_HEADS * ATT_HEAD_DIM
WINDOW = 128
ATT_BLOCK = 128
REL_BUCKETS = 32
REL_MAX_DIST = 128
N_EXPERTS = 64
TOP_K = 8
EXPERT_DIM = D_MODEL // 4
SHARED_DIM = D_MODEL // 4
ROUTE_GROUPS = 8
ROUTE_TOPK_GROUPS = 4
ROUTED_SCALE = 2.5
EPS = 1e-6

LANES = 128
SUBLANES = 8
HALF = LANES // 2

ROW_TILE = 512
ROUTE_TILE = 2048
RANK_SUB = 512
COMBINE_TILE = 256
SC_CHUNK = 128
EXPERT_ROWS = 512
COMBINE_GROUPS = 2
VMEM_LIMIT = 48 * 1024 * 1024

NEG_INF = float("-inf")


def _silu(v):
    return v * (1.0 / (1.0 + jnp.exp(-v)))


def _softplus(v):
    return jnp.maximum(v, 0.0) + jnp.log(1.0 + jnp.exp(-jnp.abs(v)))


def _bdot(a, b):
    return jnp.dot(a.astype(BF16), b.astype(BF16), preferred_element_type=F32)


def _split_hi_lo(v):
    hi = v.astype(BF16)
    lo = (v - hi.astype(F32)).astype(BF16)
    return hi, lo


def _pack_bf16_pair(a, b):
    w = pltpu.pack_elementwise([a, b], packed_dtype=BF16)
    return w if w.dtype == jnp.uint32 else lax.bitcast_convert_type(w, jnp.uint32)


def _unpack_bf16_pair(w):
    a = pltpu.unpack_elementwise(w, index=0, packed_dtype=BF16, unpacked_dtype=F32)
    b = pltpu.unpack_elementwise(w, index=1, packed_dtype=BF16, unpacked_dtype=F32)
    return a, b


def _lane_half_mask(shape):
    return lax.broadcasted_iota(jnp.int32, shape, len(shape) - 1) < HALF


def _mod_kernel(c_ref, w_ref, b_ref, o_ref):
    a = _silu(c_ref[...])
    o_ref[...] = jnp.dot(a, w_ref[...], precision=lax.Precision.HIGHEST,
                         preferred_element_type=F32) + b_ref[...]


def _mod_call(c_pad, mod_w, mod_b):
    rows, d = c_pad.shape
    cols = mod_w.shape[1]
    return pl.pallas_call(
        _mod_kernel,
        grid=(cols // d,),
        in_specs=[pl.BlockSpec((rows, d), lambda j: (0, 0)),
                  pl.BlockSpec((d, d), lambda j: (0, j)),
                  pl.BlockSpec((1, d), lambda j: (0, j))],
        out_specs=pl.BlockSpec((rows, d), lambda j: (0, j)),
        out_shape=jax.ShapeDtypeStruct((rows, cols), F32),
        compiler_params=pltpu.CompilerParams(dimension_semantics=("arbitrary",),
                                             vmem_limit_bytes=VMEM_LIMIT),
    )(c_pad, mod_w, mod_b)


IN_PROJ_SEGMENTS = ((D_SSM, BF16), (CONV_CH, BF16), (LANES, F32), (D_ATT, BF16), (D_KV, BF16), (D_KV, BF16))


def _in_proj_kernel(x_ref, sc_ref, sh_ref, g_ref, w_hbm, *refs):
    out_refs = refs[:len(IN_PROJ_SEGMENTS)]
    wraw_ref, w_ref, wsem = refs[len(IN_PROJ_SEGMENTS):]

    @pl.when(pl.program_id(0) == 0)
    def _():
        cp = pltpu.make_async_copy(w_hbm.at[0], wraw_ref, wsem)
        cp.start()
        cp.wait()
        src_dt = D_SSM + CONV_CH
        dst_q = src_dt + LANES
        src_q = src_dt + SSM_HEADS
        q_scale = ATT_HEAD_DIM ** -0.5

        def put(dst, src, scale=None):
            t = wraw_ref[src:src + LANES, :].T
            w_ref[:, dst:dst + LANES] = (t if scale is None else t * scale).astype(BF16)

        for c0 in range(0, src_dt, LANES):
            put(c0, c0)
        dt_tile = wraw_ref[src_dt:src_dt + LANES, :].T
        lane = lax.broadcasted_iota(jnp.int32, dt_tile.shape, 1)
        w_ref[:, src_dt:dst_q] = jnp.where(lane < SSM_HEADS, dt_tile, 0.0).astype(BF16)
        for c0 in range(0, D_ATT, LANES):
            put(dst_q + c0, src_q + c0, q_scale)
        for c0 in range(D_ATT, D_ATT + 2 * D_KV, LANES):
            put(dst_q + c0, src_q + c0)

    xf = x_ref[...]
    ms = jnp.mean(xf * xf, axis=-1, keepdims=True)
    h = xf * lax.rsqrt(ms + EPS) * g_ref[...]
    h = h * (1.0 + sc_ref[0]) + sh_ref[0]
    hb = h.astype(BF16)
    col = 0
    for (width, dtype), o_ref in zip(IN_PROJ_SEGMENTS, out_refs):
        o_ref[...] = jnp.dot(hb, w_ref[:, col:col + width], preferred_element_type=F32).astype(dtype)
        col += width


def _in_proj_call(x2, sc1, sh1, g1n, w_in_t, tiles_per_batch, tm):
    n, d = x2.shape
    row = lambda w: pl.BlockSpec((tm, w), lambda i: (i, 0))
    full = lambda a: pl.BlockSpec(a.shape, lambda i: (0, 0))
    per_batch = pl.BlockSpec((1, 1, d), lambda i: (i // tiles_per_batch, 0, 0))
    cols = sum(w for w, _ in IN_PROJ_SEGMENTS)
    w_in = w_in_t
    assert w_in.shape == (1, cols - (LANES - SSM_HEADS), d)
    return pl.pallas_call(
        _in_proj_kernel,
        grid=(n // tm,),
        in_specs=[row(d), per_batch, per_batch, full(g1n), pl.BlockSpec(memory_space=pl.ANY)],
        out_specs=[row(w) for w, _ in IN_PROJ_SEGMENTS],
        out_shape=[jax.ShapeDtypeStruct((n, w), dt) for w, dt in IN_PROJ_SEGMENTS],
        scratch_shapes=[pltpu.VMEM(w_in.shape[1:], w_in.dtype), pltpu.VMEM((d, cols), BF16), pltpu.SemaphoreType.DMA(())],
        compiler_params=pltpu.CompilerParams(dimension_semantics=("arbitrary",),
                                             vmem_limit_bytes=VMEM_LIMIT),
    )(x2, sc1, sh1, g1n, w_in)


SSD_SEQS = 4
CONV_HALO = 16


def _conv_shift_matrix():
    s = np.zeros((CONV_K * CHUNK, CONV_HALO + CHUNK), np.float32)
    for k in range(CONV_K):
        t = np.arange(CHUNK)
        s[k * CHUNK + t, CONV_HALO + t - (CONV_K - 1) + k] = 1.0
    return s


def _silu_tanh(v):
    hv = 0.5 * v
    return hv + hv * jnp.tanh(hv)


def _ssd_kernel(xbc_ref, z_ref, dt_ref, cw_ref, cb_ref, dtb_ref, alog_ref, dskip_ref, ng_ref, triu_ref, shift_ref,
                y_ref, state_ref, ucat_ref, ybuf_ref):
    nseq = xbc_ref.shape[0]

    @pl.when(pl.program_id(1) == 0)
    def _():
        state_ref[...] = jnp.zeros_like(state_ref)
        ucat_ref[:, 0:CONV_HALO, :] = jnp.zeros((nseq, CONV_HALO, CONV_CH), BF16)

    for q in range(nseq):
        _ssd_chunk(xbc_ref.at[q], z_ref.at[q], dt_ref.at[q], cw_ref, cb_ref, dtb_ref, alog_ref, dskip_ref, ng_ref,
                   triu_ref, shift_ref, y_ref.at[q], state_ref.at[q], ucat_ref.at[q], ybuf_ref.at[q])


def _ssd_chunk(xbc_ref, z_ref, dt_ref, cw_ref, cb_ref, dtb_ref, alog_ref, dskip_ref, ng_ref, triu_ref, shift_ref,
               y_ref, state_ref, ucat_ref, ybuf_ref):
    ucat_ref[CONV_HALO:, :] = xbc_ref[...]
    shifted = jnp.dot(shift_ref[...], ucat_ref[...], preferred_element_type=F32)
    ucat_ref[0:CONV_HALO, :] = ucat_ref[CHUNK:CHUNK + CONV_HALO, :]
    acc = cb_ref[...] + cw_ref[0:1, :] * shifted[0:CHUNK]
    for kk in range(1, CONV_K):
        acc = acc + cw_ref[kk:kk + 1, :] * shifted[kk * CHUNK:(kk + 1) * CHUNK]
    act = _silu_tanh(acc)
    xs = act[:, :D_SSM]
    gn = SSM_GROUPS * D_STATE

    dt_t = _softplus(dt_ref[...].T[0:SSM_HEADS, :] + dtb_ref[...])
    a_t = dt_t * (-jnp.exp(alog_ref[...]))
    a_hi = a_t.astype(BF16)
    a_mid = (a_t - a_hi.astype(F32)).astype(BF16)
    a_lo = (a_t - a_hi.astype(F32) - a_mid.astype(F32)).astype(BF16)
    triu = triu_ref[...]
    cs_t = (jnp.dot(a_hi, triu, preferred_element_type=F32) + jnp.dot(a_mid, triu, preferred_element_type=F32)
            + jnp.dot(a_lo, triu, preferred_element_type=F32))
    cs_end = cs_t[:, CHUNK - 1:CHUNK]
    r_t = cs_t - jnp.log(dt_t)
    w_t = jnp.exp(cs_end - cs_t) * dt_t
    chunk_decay = jnp.exp(cs_end)
    cols = jnp.concatenate([cs_t, jnp.exp(cs_t), jnp.zeros((LANES - 2 * SSM_HEADS, CHUNK), F32)], axis=0).T

    li = lax.broadcasted_iota(jnp.int32, (CHUNK, CHUNK), 0)
    si = lax.broadcasted_iota(jnp.int32, (CHUNK, CHUNK), 1)
    causal = li >= si
    low = _lane_half_mask((CHUNK, LANES))
    low_row = _lane_half_mask((1, LANES))

    heads_per_group = SSM_HEADS // SSM_GROUPS
    for g in range(SSM_GROUPS):
        b_g = act[:, D_SSM + g * D_STATE:D_SSM + (g + 1) * D_STATE]
        c_g = act[:, D_SSM + gn + g * D_STATE:D_SSM + gn + (g + 1) * D_STATE]
        b_gb = b_g.astype(BF16)
        c_gb = c_g.astype(BF16)
        cb = lax.dot_general(c_gb, b_gb, (((1,), (1,)), ((), ())), preferred_element_type=F32)
        b_t = b_g.T
        for jp in range(heads_per_group // 2):
            j = g * (heads_per_group // 2) + jp
            lanes = slice(j * LANES, (j + 1) * LANES)
            xp = xs[:, lanes]
            xpb = xp.astype(BF16)
            ydiag = jnp.zeros((CHUNK, LANES), F32)
            snew = jnp.zeros((D_STATE, LANES), F32)
            for half in range(2):
                h = 2 * j + half
                diff = cols[:, h:h + 1] - r_t[h:h + 1, :]
                m = (cb * jnp.exp(jnp.where(causal, diff, NEG_INF))).astype(BF16)
                keep = low if half == 0 else jnp.logical_not(low)
                xh = jnp.where(keep, xpb, jnp.zeros_like(xpb))
                ydiag = ydiag + jnp.dot(m, xh, preferred_element_type=F32)
                snew = snew + jnp.dot((b_t * w_t[h:h + 1, :]).astype(BF16), xh, preferred_element_type=F32)
            s_in = state_ref[:, lanes]
            yoff = jnp.dot(c_gb, s_in.astype(BF16), preferred_element_type=F32)
            h0 = 2 * j
            e0 = SSM_HEADS + h0
            escale = jnp.where(low, cols[:, e0:e0 + 1], cols[:, e0 + 1:e0 + 2])
            cdec = jnp.where(low_row, chunk_decay[h0:h0 + 1, :], chunk_decay[h0 + 1:h0 + 2, :])
            ybuf_ref[:, lanes] = ydiag + yoff * escale + xp * dskip_ref[:, lanes]
            state_ref[:, lanes] = s_in * cdec + snew

    yz = ybuf_ref[...] * _silu_tanh(z_ref[...].astype(F32))
    gw = D_SSM // SSM_GROUPS
    for g in range(SSM_GROUPS):
        part = yz[:, g * gw:(g + 1) * gw]
        ms = jnp.mean(part * part, axis=-1, keepdims=True)
        y_ref[:, g * gw:(g + 1) * gw] = (part * lax.rsqrt(ms + EPS)
                                            * ng_ref[:, g * gw:(g + 1) * gw]).astype(BF16)


def _ssd_call(xbc, z, dt, conv_w, conv_b, dtb, alog, dskip, ng, triu, shift):
    bsz, l, _ = xbc.shape
    nc = l // CHUNK
    nseq = SSD_SEQS if bsz % SSD_SEQS == 0 else 1
    chunk = lambda w: pl.BlockSpec((nseq, CHUNK, w), lambda b, c: (b, c, 0))
    full = lambda a: pl.BlockSpec(a.shape, lambda b, c: (0, 0))
    return pl.pallas_call(
        _ssd_kernel,
        grid=(bsz // nseq, nc),
        in_specs=[chunk(CONV_CH), chunk(D_SSM), chunk(LANES), full(conv_w), full(conv_b), full(dtb),
                  full(alog), full(dskip), full(ng), full(triu), full(shift)],
        out_specs=chunk(D_SSM),
        out_shape=jax.ShapeDtypeStruct((bsz, l, D_SSM), BF16),
        scratch_shapes=[pltpu.VMEM((nseq, D_STATE, D_SSM), F32),
                        pltpu.VMEM((nseq, CONV_HALO + CHUNK, CONV_CH), BF16),
                        pltpu.VMEM((nseq, CHUNK, D_SSM), F32)],
        compiler_params=pltpu.CompilerParams(dimension_semantics=("arbitrary", "arbitrary"),
                                             vmem_limit_bytes=VMEM_LIMIT),
    )(xbc, z, dt, conv_w, conv_b, dtb, alog, dskip, ng, triu, shift)


assert WINDOW == ATT_BLOCK


def _rel_bucket_table():
    qi = np.arange(ATT_BLOCK)[:, None]
    c = np.arange(ATT_BLOCK)[None, :]
    dist = np.where(c > qi, qi + ATT_BLOCK - c, qi - c)
    max_exact = REL_BUCKETS // 2
    d = np.maximum(dist, 1).astype(np.float32)
    large = max_exact + (np.log(d / np.float32(max_exact)) / np.float32(math.log(REL_MAX_DIST / max_exact))
                         * np.float32(REL_BUCKETS - max_exact)).astype(np.int32)
    large = np.minimum(large, REL_BUCKETS - 1)
    return np.where(dist < max_exact, dist, large).astype(np.int32)


def _bias_kernel(rb_ref, bucket_ref, o_ref):
    bucket = bucket_ref[...]
    from_prev = (lax.broadcasted_iota(jnp.int32, bucket.shape, 1)
                 > lax.broadcasted_iota(jnp.int32, bucket.shape, 0))
    for h in range(ATT_HEADS):
        acc = jnp.zeros(bucket.shape, F32)
        for b in range(REL_BUCKETS):
            acc = jnp.where(bucket == b, rb_ref[b, h], acc)
        o_ref[1, h] = acc
        o_ref[0, h] = jnp.where(from_prev, NEG_INF, acc)


def _bias_call(rel_bias, bucket):
    return pl.pallas_call(
        _bias_kernel,
        in_specs=[pl.BlockSpec(memory_space=pltpu.SMEM), pl.BlockSpec(memory_space=pltpu.VMEM)],
        out_shape=jax.ShapeDtypeStruct((2, ATT_HEADS) + bucket.shape, F32),
    )(rel_bias, bucket)


ATT_SEQS = 4


def _attn_kernel(sink_ref, q_ref, kp_ref, kc_ref, vp_ref, vc_ref, bias_ref, ng_ref, o_ref, obuf_ref):
    for s in range(q_ref.shape[0]):
        _attn_block(sink_ref, q_ref.at[s], kp_ref.at[s], kc_ref.at[s], vp_ref.at[s], vc_ref.at[s], bias_ref.at[0],
                    ng_ref, o_ref.at[s], obuf_ref.at[s])


def _attn_block(sink_ref, q_ref, kp_ref, kc_ref, vp_ref, vc_ref, bias_ref, ng_ref, o_ref, obuf_ref):
    qi = lax.broadcasted_iota(jnp.int32, (ATT_BLOCK, ATT_BLOCK), 0)
    ci = lax.broadcasted_iota(jnp.int32, (ATT_BLOCK, ATT_BLOCK), 1)
    from_prev = ci > qi
    low = _lane_half_mask((ATT_BLOCK, LANES))

    def band_variants(prev_ref, cur_ref):
        out = []
        for cpair in range(KV_HEADS // 2):
            lanes = slice(cpair * LANES, (cpair + 1) * LANES)
            t = jnp.concatenate([prev_ref[:, lanes], cur_ref[:, lanes]], axis=0).astype(F32)
            out.append((t.astype(BF16), pltpu.roll(t, HALF, 1).astype(BF16)))
        return out

    k_band = band_variants(kp_ref, kc_ref)
    v_band = band_variants(vp_ref, vc_ref)
    nt = (((1,), (1,)), ((), ()))

    for j in range(ATT_HEADS // 2):
        qp = q_ref[:, j * LANES:(j + 1) * LANES]
        out_pair = jnp.zeros((ATT_BLOCK, LANES), F32)
        for half in range(2):
            h = 2 * j + half
            g = h // Q_PER_KV
            swapped = int((g % 2) != half)
            keep = low if half == 0 else jnp.logical_not(low)
            qh = jnp.where(keep, qp, jnp.zeros_like(qp))
            s_band = lax.dot_general(qh, k_band[g // 2][swapped], nt, preferred_element_type=F32)
            s = jnp.where(from_prev, s_band[:, :ATT_BLOCK], s_band[:, ATT_BLOCK:]) + bias_ref[h]
            sink = sink_ref[h]
            m = jnp.maximum(jnp.max(s, axis=-1, keepdims=True), sink)
            p = jnp.exp(s - m)
            denom = jnp.sum(p, axis=-1, keepdims=True) + jnp.exp(sink - m)
            p_band = jnp.concatenate([jnp.where(from_prev, p, 0.0), jnp.where(from_prev, 0.0, p)], axis=1)
            o = jnp.dot(p_band.astype(BF16), v_band[g // 2][swapped], preferred_element_type=F32) / denom
            out_pair = out_pair + jnp.where(keep, o, 0.0)
        obuf_ref[:, j * LANES:(j + 1) * LANES] = out_pair

    att = obuf_ref[...]
    ms = jnp.mean(att * att, axis=-1, keepdims=True)
    o_ref[...] = (att * lax.rsqrt(ms + EPS) * ng_ref[...]).astype(BF16)


def _attn_call(sinks, q, k, v, bias, ng):
    bsz, l, _ = q.shape
    nb = l // ATT_BLOCK
    nseq = ATT_SEQS if bsz % ATT_SEQS == 0 else 1
    cur = lambda w: pl.BlockSpec((nseq, ATT_BLOCK, w), lambda b, i: (b, i, 0))
    prev = lambda w: pl.BlockSpec((nseq, ATT_BLOCK, w), lambda b, i: (b, jnp.maximum(i - 1, 0), 0))
    return pl.pallas_call(
        _attn_kernel,
        grid=(bsz // nseq, nb),
        in_specs=[pl.BlockSpec(memory_space=pltpu.SMEM),
                  cur(D_ATT), prev(D_KV), cur(D_KV), prev(D_KV), cur(D_KV),
                  pl.BlockSpec((1,) + bias.shape[1:], lambda b, i: (jnp.minimum(i, 1), 0, 0, 0)),
                  pl.BlockSpec(ng.shape, lambda b, i: (0, 0))],
        out_specs=cur(D_ATT),
        out_shape=jax.ShapeDtypeStruct((bsz, l, D_ATT), BF16),
        scratch_shapes=[pltpu.VMEM((nseq, ATT_BLOCK, D_ATT), F32)],
        compiler_params=pltpu.CompilerParams(dimension_semantics=("arbitrary", "arbitrary"),
                                             vmem_limit_bytes=VMEM_LIMIT),
    )(sinks, q, k, k, v, v, bias, ng)


def _out_proj_kernel(x_ref, ys_ref, ya_ref, g1_ref, sc_ref, sh_ref, g2_ref, ng_ref, wo_hbm, sg_hbm, su_hbm, sd_hbm,
                     rwh_ref, rwl_ref, rb_ref, upper_ref,
                     base_ref, h_ref, idx_ref, gate_ref, rank_ref, cnt_ref,
                     carry_ref, wo_raw, sg_raw, su_raw, sd_raw, wo_ref, sg_ref, su_ref, sd_ref, wsem):
    @pl.when(pl.program_id(0) == 0)
    def _():
        carry_ref[...] = jnp.zeros_like(carry_ref)
        staged = ((wo_hbm, wo_raw, wo_ref), (sg_hbm, sg_raw, sg_ref), (su_hbm, su_raw, su_ref),
                  (sd_hbm, sd_raw, sd_ref))
        copies = [pltpu.make_async_copy(src.at[0], raw, wsem.at[j]) for j, (src, raw, _) in enumerate(staged)]
        for cp in copies:
            cp.start()
        for cp, (_, raw, dst) in zip(copies, staged):
            cp.wait()
            dst[...] = raw[...].astype(BF16)

    mix = (jnp.dot(ys_ref[...], wo_ref[:D_SSM, :], preferred_element_type=F32)
           + jnp.dot(ya_ref[...], wo_ref[D_SSM:, :], preferred_element_type=F32))
    x1 = x_ref[...] + g1_ref[0] * mix
    ms = jnp.mean(x1 * x1, axis=-1, keepdims=True)
    h = x1 * lax.rsqrt(ms + EPS) * ng_ref[...]
    h = h * (1.0 + sc_ref[0]) + sh_ref[0]
    half = h.shape[1] // 2
    h_ref[...] = _pack_bf16_pair(h[:, :half], h[:, half:])
    hi, lo = _split_hi_lo(h)
    hi_terms = jnp.dot(hi, rwl_ref[...], preferred_element_type=F32)
    logits = (hi_terms[:, :LANES] + hi_terms[:, LANES:]
              + jnp.dot(lo, rwh_ref[...], preferred_element_type=F32))
    _route_tokens(logits.T[0:N_EXPERTS, :], rb_ref, upper_ref, idx_ref, gate_ref, rank_ref, cnt_ref, carry_ref)
    u = _silu(jnp.dot(hi, sg_ref[...], preferred_element_type=F32)) * jnp.dot(hi, su_ref[...],
                                                                              preferred_element_type=F32)
    shared = jnp.dot(u.astype(BF16), sd_ref[...], preferred_element_type=F32)
    base_ref[...] = x1 + g2_ref[0] * shared


def _out_proj_call(x2, ys, ya, g1, sc2, sh2, g2, ng, w_out, sg, su, sd, rwh, rwl, router_bias, upper,
                   tiles_per_batch, tm):
    n, d = x2.shape
    row = lambda w: pl.BlockSpec((tm, w), lambda i: (i, 0))
    tok = lambda r: pl.BlockSpec((r, tm), lambda i: (0, i))
    full = lambda a: pl.BlockSpec(a.shape, lambda i: (0, 0))
    per_batch = pl.BlockSpec((1, 1, d), lambda i: (i // tiles_per_batch, 0, 0))
    hbm = pl.BlockSpec(memory_space=pl.ANY)
    staged = (w_out, sg, su, sd)
    return pl.pallas_call(
        _out_proj_kernel,
        grid=(n // tm,),
        in_specs=[row(d), row(D_SSM), row(D_ATT), per_batch, per_batch, per_batch, per_batch, full(ng),
                  hbm, hbm, hbm, hbm, full(rwh), full(rwl), full(router_bias), full(upper)],
        out_specs=[row(d), row(d // 2), tok(TOP_K), tok(TOP_K), tok(TOP_K),
                   pl.BlockSpec((N_EXPERTS, LANES), lambda i: (0, 0))],
        out_shape=[jax.ShapeDtypeStruct((n, d), F32), jax.ShapeDtypeStruct((n, d // 2), jnp.uint32),
                   jax.ShapeDtypeStruct((TOP_K, n), jnp.int32), jax.ShapeDtypeStruct((TOP_K, n), F32),
                   jax.ShapeDtypeStruct((TOP_K, n), jnp.int32), jax.ShapeDtypeStruct((N_EXPERTS, LANES), F32)],
        scratch_shapes=([pltpu.VMEM((N_EXPERTS, LANES), F32)]
                        + [pltpu.VMEM(a.shape[1:], a.dtype) for a in staged]
                        + [pltpu.VMEM(a.shape[1:], BF16) for a in staged]
                        + [pltpu.SemaphoreType.DMA((len(staged),))]),
        compiler_params=pltpu.CompilerParams(dimension_semantics=("arbitrary",),
                                             vmem_limit_bytes=VMEM_LIMIT),
    )(x2, ys, ya, g1, sc2, sh2, g2, ng, w_out, sg, su, sd, rwh, rwl, router_bias, upper)


def _route_tokens(logits_t, rb_ref, upper_ref, idx_ref, gate_ref, rank_ref, cnt_ref, carry_ref):
    t = logits_t.shape[1]
    per_group = N_EXPERTS // ROUTE_GROUPS
    scores = 1.0 / (1.0 + jnp.exp(-logits_t))
    sel = scores + rb_ref[...]
    e_iota = lax.broadcasted_iota(jnp.int32, (N_EXPERTS, t), 0)

    sel3 = sel.reshape(ROUTE_GROUPS, per_group, t)
    w_iota = lax.broadcasted_iota(jnp.int32, sel3.shape, 1)
    m1 = jnp.max(sel3, axis=1, keepdims=True)
    first = jnp.min(jnp.where(sel3 == m1, w_iota, per_group), axis=1, keepdims=True)
    m2 = jnp.max(jnp.where(w_iota == first, NEG_INF, sel3), axis=1, keepdims=True)
    grp = (m1 + m2).reshape(ROUTE_GROUPS, t)

    g_iota = lax.broadcasted_iota(jnp.int32, (ROUTE_GROUPS, t), 0)
    gmask = jnp.zeros((ROUTE_GROUPS, t), jnp.bool_)
    for _ in range(ROUTE_TOPK_GROUPS):
        gm = jnp.max(grp, axis=0, keepdims=True)
        gfirst = jnp.min(jnp.where(grp == gm, g_iota, ROUTE_GROUPS), axis=0, keepdims=True)
        hit = g_iota == gfirst
        gmask = jnp.logical_or(gmask, hit)
        grp = jnp.where(hit, NEG_INF, grp)
    allowed = jnp.broadcast_to(gmask.reshape(ROUTE_GROUPS, 1, t),
                               (ROUTE_GROUPS, per_group, t)).reshape(N_EXPERTS, t)
    masked = jnp.where(allowed, sel, NEG_INF)

    picked = jnp.zeros((N_EXPERTS, t), jnp.bool_)
    idx_rows = []
    w_rows = []
    for _ in range(TOP_K):
        mm = jnp.max(masked, axis=0, keepdims=True)
        efirst = jnp.min(jnp.where(masked == mm, e_iota, N_EXPERTS), axis=0, keepdims=True)
        hit = e_iota == efirst
        idx_rows.append(efirst)
        w_rows.append(jnp.sum(jnp.where(hit, scores, 0.0), axis=0, keepdims=True))
        picked = jnp.logical_or(picked, hit)
        masked = jnp.where(hit, NEG_INF, masked)
    idx = jnp.concatenate(idx_rows, axis=0)
    w = jnp.concatenate(w_rows, axis=0)
    gate_ref[...] = w / jnp.sum(w, axis=0, keepdims=True) * ROUTED_SCALE
    idx_ref[...] = idx

    onehot = jnp.where(picked, 1.0, 0.0)
    sub = upper_ref.shape[0]
    carry = carry_ref[:, 0:1]
    parts = []
    for s0 in range(0, t, sub):
        oh = onehot[:, s0:s0 + sub]
        parts.append(jnp.dot(oh.astype(BF16), upper_ref[...], preferred_element_type=F32) + carry)
        carry = carry + jnp.sum(oh, axis=1, keepdims=True)
    rank_full = jnp.concatenate(parts, axis=1)
    rank_rows = [jnp.sum(jnp.where(e_iota == idx_rows[k], rank_full, 0.0), axis=0, keepdims=True)
                 for k in range(TOP_K)]
    rank_ref[...] = jnp.concatenate(rank_rows, axis=0).astype(jnp.int32)
    carry_ref[...] = jnp.broadcast_to(carry, carry_ref.shape)
    cnt_ref[...] = carry_ref[...]


PLAN_EXPERT, PLAN_FRESH, PLAN_VALID, PLAN_SEG, PLAN_NEXT, PLAN_NUSED = range(6)
PLAN_ROWS = SUBLANES


def _plan_kernel(cnt_ref, tri_ref, start_ref, plan_ref, *, nblocks):
    nbp = plan_ref.shape[1]
    cnt = cnt_ref[...].astype(jnp.int32)
    blocks = (cnt + (EXPERT_ROWS - 1)) // EXPERT_ROWS
    end = jnp.dot(tri_ref[...], blocks.astype(F32), precision=lax.Precision.HIGHEST,
                  preferred_element_type=F32).astype(jnp.int32)
    start = end - blocks
    start_ref[...] = start[:, 0:1] * EXPERT_ROWS
    nused = end[N_EXPERTS - 1:N_EXPERTS, 0:1]

    e_iota = lax.broadcasted_iota(jnp.int32, (N_EXPERTS, nbp), 0)
    blk = lax.broadcasted_iota(jnp.int32, (1, nbp), 1)
    expert = jnp.minimum(jnp.sum((end[:, 0:1] <= blk).astype(jnp.int32), axis=0, keepdims=True), N_EXPERTS - 1)
    mine = e_iota == expert
    pick = lambda col: jnp.sum(jnp.where(mine, col, 0), axis=0, keepdims=True)
    first = pick(start[:, 0:1])
    valid = jnp.clip(pick(cnt[:, 0:1]) - (blk - first) * EXPERT_ROWS, 0, EXPERT_ROWS)
    present = jnp.logical_or(blocks[:, 0:1] > 0,
                             jnp.logical_and(e_iota[:, 0:1] == N_EXPERTS - 1, nused < nblocks))
    seg = jnp.sum(jnp.logical_and(present, e_iota <= expert).astype(jnp.int32), axis=0, keepdims=True) - 1
    nxt = jnp.min(jnp.where(jnp.logical_and(present, e_iota > expert), e_iota, N_EXPERTS), axis=0, keepdims=True)
    rows = {PLAN_EXPERT: expert, PLAN_FRESH: (blk == first).astype(jnp.int32), PLAN_VALID: valid, PLAN_SEG: seg,
            PLAN_NEXT: jnp.where(nxt == N_EXPERTS, -1, nxt), PLAN_NUSED: jnp.broadcast_to(nused, (1, nbp))}
    zero = jnp.zeros((1, nbp), jnp.int32)
    plan_ref[...] = jnp.concatenate([rows.get(r, zero) for r in range(PLAN_ROWS)], axis=0)


def _plan_call(counts, nblocks):
    nbp = -(-nblocks // LANES) * LANES
    tri = jnp.asarray(np.tril(np.ones((N_EXPERTS, N_EXPERTS), np.float32)))
    return pl.pallas_call(
        functools.partial(_plan_kernel, nblocks=nblocks),
        out_shape=[jax.ShapeDtypeStruct((N_EXPERTS, 1), jnp.int32),
                   jax.ShapeDtypeStruct((PLAN_ROWS, nbp), jnp.int32)],
    )(counts, tri)


def _dest_kernel(idx_ref, rank_ref, start_ref, dest_ref):
    t = idx_ref.shape[1]
    e_iota = lax.broadcasted_iota(jnp.int32, (N_EXPERTS, t), 0)
    rows = [jnp.sum(jnp.where(e_iota == idx_ref[k:k + 1, :], start_ref[...], 0), axis=0, keepdims=True)
            for k in range(TOP_K)]
    dest = jnp.concatenate(rows, axis=0) + rank_ref[...]
    for k in range(TOP_K):
        for c in range(t // LANES):
            dest_ref[k, c:c + 1, :] = dest[k:k + 1, c * LANES:(c + 1) * LANES]


def _dest_call(idx, rank, pad_start, tile):
    n = idx.shape[1]
    tok = pl.BlockSpec((TOP_K, tile), lambda i: (0, i))
    return pl.pallas_call(
        _dest_kernel,
        grid=(n // tile,),
        in_specs=[tok, tok, pl.BlockSpec((N_EXPERTS, 1), lambda i: (0, 0))],
        out_specs=pl.BlockSpec((TOP_K, tile // LANES, LANES), lambda i: (0, i, 0)),
        out_shape=jax.ShapeDtypeStruct((TOP_K, n // LANES, LANES), jnp.int32),
        compiler_params=pltpu.CompilerParams(dimension_semantics=("arbitrary",)),
    )(idx, rank, pad_start)


def _scatter_rows_sc(rows, dest_flat, total_rows, chunk):
    n, w = rows.shape
    copies = dest_flat.shape[0] // n
    info = plsc.get_sparse_core_info()
    nc = info.num_cores
    per_worker = n // (nc * info.num_subcores)
    assert per_worker * nc * info.num_subcores == n and per_worker % chunk == 0
    mesh = plsc.VectorSubcoreMesh(core_axis_name="c", subcore_axis_name="s")

    @functools.partial(
        pl.kernel, mesh=mesh,
        out_type=jax.ShapeDtypeStruct((total_rows, w), rows.dtype),
        scratch_types=[pltpu.VMEM((chunk,), jnp.int32), pltpu.VMEM((chunk, w), rows.dtype),
                       pltpu.SemaphoreType.DMA],
    )
    def scatter(rows_hbm, idx_hbm, out_hbm, idx_v, rows_v, sem):
        worker = lax.axis_index("s") * nc + lax.axis_index("c")

        @pl.loop(0, per_worker // chunk)
        def _(j):
            base = worker * per_worker + j * chunk
            pltpu.sync_copy(rows_hbm.at[pl.ds(base, chunk)], rows_v)
            for k in range(copies):
                pltpu.sync_copy(idx_hbm.at[pl.ds(k * n + base, chunk)], idx_v)
                pltpu.async_copy(rows_v, out_hbm.at[idx_v], sem).wait()

    return scatter(rows, dest_flat)


EXPERT_LIVE_STEP = 128
X_RING = 3


def _expert_kernel(plan_ref, xs_hbm, wg_hbm, wu_hbm, wd_hbm, y_ref,
                   wgb_ref, wub_ref, wdb_ref, xbuf_ref, xsem, wg_raw, wu_raw, wd_raw, wsem):
    i = pl.program_id(0)
    rows, half = xbuf_ref.shape[1], xbuf_ref.shape[2]
    nused = plan_ref[PLAN_NUSED, 0]

    def w_copies(expert, slot):
        return [pltpu.make_async_copy(src.at[expert], dst.at[slot], wsem.at[slot])
                for src, dst in ((wg_hbm, wg_raw), (wu_hbm, wu_raw), (wd_hbm, wd_raw))]

    def x_copy(block):
        first = pl.multiple_of(block * rows, rows)
        slot = block % X_RING
        return pltpu.make_async_copy(xs_hbm.at[pl.ds(first, rows)], xbuf_ref.at[slot], xsem.at[slot])

    @pl.when(i == 0)
    def _():
        for b in range(X_RING - 1):
            @pl.when(b < nused)
            def _():
                x_copy(b).start()

    @pl.when(i + (X_RING - 1) < nused)
    def _():
        x_copy(i + (X_RING - 1)).start()

    @pl.when(i == 0)
    def _():
        for c in w_copies(plan_ref[PLAN_EXPERT, 0], 0):
            c.start()

    @pl.when(plan_ref[PLAN_FRESH, i] > 0)
    def _():
        slot = plan_ref[PLAN_SEG, i] % 2
        for c in w_copies(plan_ref[PLAN_EXPERT, i], slot):
            c.wait()

        @pl.when(plan_ref[PLAN_NEXT, i] >= 0)
        def _():
            for c in w_copies(plan_ref[PLAN_NEXT, i], 1 - slot):
                c.start()

        wgb_ref[...] = wg_raw[slot].astype(BF16)
        wub_ref[...] = wu_raw[slot].astype(BF16)
        wdb_ref[...] = wd_raw[slot].astype(BF16)

    @pl.when(i < nused)
    def _():
        x_copy(i).wait()

    real = plan_ref[PLAN_VALID, i]
    slot = i % X_RING

    def ffn(live):
        xw = xbuf_ref[slot, 0:live, :]
        row = lax.broadcasted_iota(jnp.int32, xw.shape, 0)
        x_lo, x_hi = _unpack_bf16_pair(jnp.where(row < real, xw, jnp.uint32(0)))
        x_lo = x_lo.astype(BF16)
        x_hi = x_hi.astype(BF16)
        gate = (jnp.dot(x_lo, wgb_ref[:half, :], preferred_element_type=F32)
                + jnp.dot(x_hi, wgb_ref[half:, :], preferred_element_type=F32))
        up = (jnp.dot(x_lo, wub_ref[:half, :], preferred_element_type=F32)
              + jnp.dot(x_hi, wub_ref[half:, :], preferred_element_type=F32))
        u = (_silu(gate) * up).astype(BF16)
        y_lo = jnp.dot(u, wdb_ref[:, :half], preferred_element_type=F32)
        y_hi = jnp.dot(u, wdb_ref[:, half:], preferred_element_type=F32)
        y_ref[0:live, :] = _pack_bf16_pair(y_lo, y_hi)
        if live < rows:
            y_ref[live:, :] = jnp.zeros((rows - live, half), y_ref.dtype)

    pieces = (real + (EXPERT_LIVE_STEP - 1)) // EXPERT_LIVE_STEP
    for p in range(1, rows // EXPERT_LIVE_STEP + 1):
        @pl.when(jnp.logical_and(i < nused, pieces == p))
        def _():
            ffn(p * EXPERT_LIVE_STEP)

    @pl.when(i >= nused)
    def _():
        y_ref[...] = jnp.zeros_like(y_ref)


def _expert_call(plan, xs, wg, wu, wd, rows):
    p, w = xs.shape
    d, f = wg.shape[1], wg.shape[2]
    hbm = pl.BlockSpec(memory_space=pl.ANY)
    grid_spec = pltpu.PrefetchScalarGridSpec(
        num_scalar_prefetch=1,
        grid=(p // rows,),
        in_specs=[hbm, hbm, hbm, hbm],
        out_specs=pl.BlockSpec((rows, w), lambda i, *_: (i, 0)),
        scratch_shapes=[pltpu.VMEM((d, f), BF16), pltpu.VMEM((d, f), BF16), pltpu.VMEM((f, d), BF16),
                        pltpu.VMEM((X_RING, rows, w), xs.dtype), pltpu.SemaphoreType.DMA((X_RING,)),
                        pltpu.VMEM((2, d, f), wg.dtype), pltpu.VMEM((2, d, f), wu.dtype),
                        pltpu.VMEM((2, f, d), wd.dtype), pltpu.SemaphoreType.DMA((2,))],
    )
    return pl.pallas_call(
        _expert_kernel,
        grid_spec=grid_spec,
        out_shape=jax.ShapeDtypeStruct((p, w), jnp.uint32),
        compiler_params=pltpu.CompilerParams(dimension_semantics=("arbitrary",),
                                             vmem_limit_bytes=VMEM_LIMIT),
    )(plan, xs, wg, wu, wd)


def _gather_rows_sc(table, idx, chunk):
    m = idx.shape[0]
    w = table.shape[1]
    info = plsc.get_sparse_core_info()
    nc = info.num_cores
    per_worker = m // (nc * info.num_subcores)
    assert per_worker * nc * info.num_subcores == m and per_worker % chunk == 0
    mesh = plsc.VectorSubcoreMesh(core_axis_name="c", subcore_axis_name="s")

    @functools.partial(
        pl.kernel, mesh=mesh,
        out_type=jax.ShapeDtypeStruct((m, w), table.dtype),
        scratch_types=[pltpu.VMEM((chunk,), jnp.int32), pltpu.VMEM((chunk, w), table.dtype),
                       pltpu.SemaphoreType.DMA],
    )
    def gather(table_hbm, idx_hbm, out_hbm, idx_v, rows_v, sem):
        worker = lax.axis_index("s") * nc + lax.axis_index("c")

        @pl.loop(0, per_worker // chunk)
        def _(j):
            base = worker * per_worker + j * chunk
            pltpu.sync_copy(idx_hbm.at[pl.ds(base, chunk)], idx_v)
            pltpu.async_copy(table_hbm.at[idx_v], rows_v, sem).wait()
            pltpu.sync_copy(rows_v, out_hbm.at[pl.ds(base, chunk)])

    return gather(table, idx)


def _combine_kernel(yk_ref, gate_ref, base_ref, g2_ref, fg_ref, o_ref):
    t = base_ref.shape[0]
    half = yk_ref.shape[2]
    gates = jnp.concatenate([gate_ref[...], jnp.zeros((LANES - TOP_K, t), F32)], axis=0).T
    r_lo = jnp.zeros((t, half), F32)
    r_hi = jnp.zeros((t, half), F32)
    for k in range(TOP_K):
        y_lo, y_hi = _unpack_bf16_pair(yk_ref[k])
        r_lo = r_lo + gates[:, k:k + 1] * y_lo
        r_hi = r_hi + gates[:, k:k + 1] * y_hi
    g2 = g2_ref[0]
    x_lo = base_ref[:, :half] + g2[:, :half] * r_lo
    x_hi = base_ref[:, half:] + g2[:, half:] * r_hi
    ms = (jnp.sum(x_lo * x_lo, axis=-1, keepdims=True)
          + jnp.sum(x_hi * x_hi, axis=-1, keepdims=True)) * (1.0 / (2 * half))
    inv = lax.rsqrt(ms + EPS)
    o_ref[:, :half] = x_lo * inv * fg_ref[:, :half]
    o_ref[:, half:] = x_hi * inv * fg_ref[:, half:]


def _combine_call(yk, gates, base, g2, fg, tiles_per_batch, tile, tile0):
    n, d = base.shape
    row = lambda w: pl.BlockSpec((tile, w), lambda i: (i + tile0, 0))
    return pl.pallas_call(
        _combine_kernel,
        grid=(yk.shape[1] // tile,),
        in_specs=[pl.BlockSpec((TOP_K, tile, yk.shape[2]), lambda i: (0, i, 0)),
                  pl.BlockSpec((TOP_K, tile), lambda i: (0, i + tile0)), row(d),
                  pl.BlockSpec((1, 1, d), lambda i: ((i + tile0) // tiles_per_batch, 0, 0)),
                  pl.BlockSpec((1, d), lambda i: (0, 0))],
        out_specs=row(d),
        out_shape=jax.ShapeDtypeStruct((n, d), F32),
        input_output_aliases={2: 0},
        compiler_params=pltpu.CompilerParams(dimension_semantics=("arbitrary",),
                                             vmem_limit_bytes=VMEM_LIMIT),
    )(yk, gates, base, g2, fg)


def _pad_cols(a, width):
    return jnp.pad(a, ((0, 0), (0, width - a.shape[1])))


def _layer(x, mod, norm1_g, norm2_g, w_in, conv_w, conv_b, dt_bias, a_log, d_skip, ssm_norm_g,
           att_norm_g, sinks, rel_bias, w_out, router_w, router_bias, exp_w_gate, exp_w_up, exp_w_down,
           sh_w_gate, sh_w_up, sh_w_down, final_g):
    bsz, l, d = x.shape
    n = bsz * l
    tm = min(ROW_TILE, l)

    sh1, sc1, g1, sh2, sc2, g2 = [m[:, None, :] for m in jnp.split(mod, 6, axis=-1)]

    assert math.frexp(ATT_HEAD_DIM ** -0.5)[0] == 0.5
    x2 = x.reshape(n, d)
    z, xbc, dt, q, k, v = _in_proj_call(x2, sc1, sh1, norm1_g[None, :], jnp.swapaxes(w_in, 1, 2), l // tm, tm)

    triu = jnp.asarray(np.triu(np.ones((CHUNK, CHUNK), np.float32))).astype(BF16)
    shift = jnp.asarray(_conv_shift_matrix()).astype(BF16)
    y_ssm = _ssd_call(xbc.reshape(bsz, l, CONV_CH), z.reshape(bsz, l, D_SSM), dt.reshape(bsz, l, LANES),
                      conv_w, conv_b[None, :], dt_bias[:, None], a_log[:, None],
                      jnp.repeat(d_skip, SSM_HEAD_DIM)[None, :], ssm_norm_g[None, :], triu, shift)

    bias = _bias_call(rel_bias, jnp.asarray(_rel_bucket_table()))
    y_att = _attn_call(sinks, q.reshape(bsz, l, D_ATT), k.reshape(bsz, l, D_KV), v.reshape(bsz, l, D_KV), bias,
                       att_norm_g[None, :])

    rw = _pad_cols(router_w, LANES)
    rwh = rw.astype(BF16)
    rwl = jnp.concatenate([rwh, (rw - rwh.astype(F32)).astype(BF16)], axis=1)
    rs = min(RANK_SUB, tm)
    upper = jnp.asarray(np.triu(np.ones((rs, rs), np.float32), 1)).astype(BF16)
    base, h2, idx, gates, rank, counts = _out_proj_call(
        x2, y_ssm.reshape(n, D_SSM), y_att.reshape(n, D_ATT), g1, sc2, sh2, g2, norm2_g[None, :],
        w_out, sh_w_gate, sh_w_up, sh_w_down, rwh, rwl, router_bias[:, None], upper, l // tm, tm)
    rt = min(ROUTE_TILE, n)

    nblocks = (n * TOP_K + N_EXPERTS * (EXPERT_ROWS - 1) + EXPERT_ROWS - 1) // EXPERT_ROWS
    pad_start, plan = _plan_call(counts, nblocks)
    dest = _dest_call(idx, rank, pad_start, rt)

    xs = _scatter_rows_sc(h2, dest.reshape(-1), nblocks * EXPERT_ROWS, SC_CHUNK)
    ys = _expert_call(plan, xs, exp_w_gate, exp_w_up, exp_w_down, EXPERT_ROWS)
    ctile = min(COMBINE_TILE, l)
    groups = COMBINE_GROUPS if bsz % COMBINE_GROUPS == 0 else 1
    ng = n // groups
    out = base
    for g in range(groups):
        idx_g = dest[:, g * ng // LANES:(g + 1) * ng // LANES, :].reshape(-1)
        yk = _gather_rows_sc(ys, idx_g, SC_CHUNK).reshape(TOP_K, ng, ys.shape[1])
        out = _combine_call(yk, gates, out, g2, final_g[None, :], l // ctile, ctile, g * ng // ctile)
    return out.reshape(bsz, l, d)


def kernel(x, c, mod_w, mod_b, norm1_g, norm2_g, w_in, conv_w, conv_b, dt_bias, a_log, d_skip, ssm_norm_g,
           att_norm_g, sinks, rel_bias, w_out, router_w, router_bias, exp_w_gate, exp_w_up, exp_w_down,
           sh_w_gate, sh_w_up, sh_w_down, final_g):
    assert mod_w.shape[0] == 1, "single-layer block"
    bsz = x.shape[0]
    c_pad = jnp.pad(c, ((0, SUBLANES - bsz % SUBLANES if bsz % SUBLANES else 0), (0, 0)))
    mod = _mod_call(c_pad, mod_w[0], mod_b[0][None, :])[:bsz]
    return _layer(x, mod, norm1_g[0], norm2_g[0], w_in, conv_w[0], conv_b[0], dt_bias[0], a_log[0], d_skip[0],
                  ssm_norm_g[0], att_norm_g[0], sinks[0], rel_bias, w_out, router_w[0], router_bias[0],
                  exp_w_gate[0], exp_w_up[0], exp_w_down[0], sh_w_gate, sh_w_up, sh_w_down, final_g)
```

```python
import functools
import math

import numpy as np
import jax
import jax.numpy as jnp
from jax import lax
from jax.experimental import pallas as pl
from jax.experimental.pallas import tpu as pltpu
from jax.experimental.pallas import tpu_sc as plsc

F32 = jnp.float32
BF16 = jnp.bfloat16

D_MODEL = 1024
SSM_HEAD_DIM = 64
D_SSM = D_MODEL
SSM_HEADS = D_SSM // SSM_HEAD_DIM
SSM_GROUPS = 4
D_STATE = 128
CONV_K = 4
CONV_CH = D_SSM + 2 * SSM_GROUPS * D_STATE
CHUNK = 128
ATT_HEAD_DIM = 64
D_ATT = D_MODEL
ATT_HEADS = D_ATT // ATT_HEAD_DIM
KV_HEADS = ATT_HEADS // 4
Q_PER_KV = ATT_HEADS // KV_HEADS
D_KV = KV_HEADS * ATT_HEAD_DIM
WINDOW = 128
ATT_BLOCK = 128
REL_BUCKETS = 32
REL_MAX_DIST = 128
N_EXPERTS = 64
TOP_K = 8
EXPERT_DIM = D_MODEL // 4
SHARED_DIM = D_MODEL // 4
ROUTE_GROUPS = 8
ROUTE_TOPK_GROUPS = 4
ROUTED_SCALE = 2.5
EPS = 1e-6

LANES = 128
SUBLANES = 8
HALF = LANES // 2

ROW_TILE = 512
ROUTE_TILE = 2048
RANK_SUB = 512
COMBINE_TILE = 256
SC_CHUNK = 128
EXPERT_ROWS = 512
COMBINE_GROUPS = 2
VMEM_LIMIT = 48 * 1024 * 1024

NEG_INF = float("-inf")


def _silu(v):
    return v * (1.0 / (1.0 + jnp.exp(-v)))


def _softplus(v):
    return jnp.maximum(v, 0.0) + jnp.log(1.0 + jnp.exp(-jnp.abs(v)))


def _bdot(a, b):
    return jnp.dot(a.astype(BF16), b.astype(BF16), preferred_element_type=F32)


def _split_hi_lo(v):
    hi = v.astype(BF16)
    lo = (v - hi.astype(F32)).astype(BF16)
    return hi, lo


def _pack_bf16_pair(a, b):
    w = pltpu.pack_elementwise([a, b], packed_dtype=BF16)
    return w if w.dtype == jnp.uint32 else lax.bitcast_convert_type(w, jnp.uint32)


def _unpack_bf16_pair(w):
    a = pltpu.unpack_elementwise(w, index=0, packed_dtype=BF16, unpacked_dtype=F32)
    b = pltpu.unpack_elementwise(w, index=1, packed_dtype=BF16, unpacked_dtype=F32)
    return a, b


def _lane_half_mask(shape):
    return lax.broadcasted_iota(jnp.int32, shape, len(shape) - 1) < HALF


def _mod_kernel(c_ref, w_ref, b_ref, o_ref):
    a = _silu(c_ref[...])
    o_ref[...] = jnp.dot(a, w_ref[...], precision=lax.Precision.HIGHEST,
                         preferred_element_type=F32) + b_ref[...]


def _mod_call(c_pad, mod_w, mod_b):
    rows, d = c_pad.shape
    cols = mod_w.shape[1]
    return pl.pallas_call(
        _mod_kernel,
        grid=(cols // d,),
        in_specs=[pl.BlockSpec((rows, d), lambda j: (0, 0)),
                  pl.BlockSpec((d, d), lambda j: (0, j)),
                  pl.BlockSpec((1, d), lambda j: (0, j))],
        out_specs=pl.BlockSpec((rows, d), lambda j: (0, j)),
        out_shape=jax.ShapeDtypeStruct((rows, cols), F32),
        compiler_params=pltpu.CompilerParams(dimension_semantics=("arbitrary",),
                                             vmem_limit_bytes=VMEM_LIMIT),
    )(c_pad, mod_w, mod_b)


IN_PROJ_SEGMENTS = ((D_SSM, BF16), (CONV_CH, BF16), (LANES, F32), (D_ATT, BF16), (D_KV, BF16), (D_KV, BF16))


def _in_proj_kernel(x_ref, sc_ref, sh_ref, g_ref, w_hbm, *refs):
    out_refs = refs[:len(IN_PROJ_SEGMENTS)]
    wraw_ref, w_ref, wsem = refs[len(IN_PROJ_SEGMENTS):]

    @pl.when(pl.program_id(0) == 0)
    def _():
        cp = pltpu.make_async_copy(w_hbm.at[0], wraw_ref, wsem)
        cp.start()
        cp.wait()
        src_dt = D_SSM + CONV_CH
        dst_q = src_dt + LANES
        src_q = src_dt + SSM_HEADS
        q_scale = ATT_HEAD_DIM ** -0.5

        def put(dst, src, scale=None):
            t = wraw_ref[src:src + LANES, :].T
            w_ref[:, dst:dst + LANES] = (t if scale is None else t * scale).astype(BF16)

        for c0 in range(0, src_dt, LANES):
            put(c0, c0)
        dt_tile = wraw_ref[src_dt:src_dt + LANES, :].T
        lane = lax.broadcasted_iota(jnp.int32, dt_tile.shape, 1)
        w_ref[:, src_dt:dst_q] = jnp.where(lane < SSM_HEADS, dt_tile, 0.0).astype(BF16)
        for c0 in range(0, D_ATT, LANES):
            put(dst_q + c0, src_q + c0, q_scale)
        for c0 in range(D_ATT, D_ATT + 2 * D_KV, LANES):
            put(dst_q + c0, src_q + c0)

    xf = x_ref[...]
    ms = jnp.mean(xf * xf, axis=-1, keepdims=True)
    h = xf * lax.rsqrt(ms + EPS) * g_ref[...]
    h = h * (1.0 + sc_ref[0]) + sh_ref[0]
    hb = h.astype(BF16)
    col = 0
    for (width, dtype), o_ref in zip(IN_PROJ_SEGMENTS, out_refs):
        o_ref[...] = jnp.dot(hb, w_ref[:, col:col + width], preferred_element_type=F32).astype(dtype)
        col += width


def _in_proj_call(x2, sc1, sh1, g1n, w_in_t, tiles_per_batch, tm):
    n, d = x2.shape
    row = lambda w: pl.BlockSpec((tm, w), lambda i: (i, 0))
    full = lambda a: pl.BlockSpec(a.shape, lambda i: (0, 0))
    per_batch = pl.BlockSpec((1, 1, d), lambda i: (i // tiles_per_batch, 0, 0))
    cols = sum(w for w, _ in IN_PROJ_SEGMENTS)
    w_in = w_in_t
    assert w_in.shape == (1, cols - (LANES - SSM_HEADS), d)
    return pl.pallas_call(
        _in_proj_kernel,
        grid=(n // tm,),
        in_specs=[row(d), per_batch, per_batch, full(g1n), pl.BlockSpec(memory_space=pl.ANY)],
        out_specs=[row(w) for w, _ in IN_PROJ_SEGMENTS],
        out_shape=[jax.ShapeDtypeStruct((n, w), dt) for w, dt in IN_PROJ_SEGMENTS],
        scratch_shapes=[pltpu.VMEM(w_in.shape[1:], w_in.dtype), pltpu.VMEM((d, cols), BF16), pltpu.SemaphoreType.DMA(())],
        compiler_params=pltpu.CompilerParams(dimension_semantics=("arbitrary",),
                                             vmem_limit_bytes=VMEM_LIMIT),
    )(x2, sc1, sh1, g1n, w_in)


SSD_SEQS = 4
CONV_HALO = 16


def _conv_shift_matrix():
    s = np.zeros((CONV_K * CHUNK, CONV_HALO + CHUNK), np.float32)
    for k in range(CONV_K):
        t = np.arange(CHUNK)
        s[k * CHUNK + t, CONV_HALO + t - (CONV_K - 1) + k] = 1.0
    return s


def _silu_tanh(v):
    hv = 0.5 * v
    return hv + hv * jnp.tanh(hv)


def _ssd_kernel(xbc_ref, z_ref, dt_ref, cw_ref, cb_ref, dtb_ref, alog_ref, dskip_ref, ng_ref, triu_ref, shift_ref,
                wg_ref, wu_ref, y_ref, wgb_ref, wub_ref, state_ref, ucat_ref, ybuf_ref):
    nseq = xbc_ref.shape[0]
    wgb_ref[...] = wg_ref[...].astype(BF16)
    wub_ref[...] = wu_ref[...].astype(BF16)

    @pl.when(pl.program_id(1) == 0)
    def _():
        state_ref[...] = jnp.zeros_like(state_ref)
        ucat_ref[:, 0:CONV_HALO, :] = jnp.zeros((nseq, CONV_HALO, CONV_CH), BF16)

    for q in range(nseq):
        _ssd_chunk(xbc_ref.at[q], z_ref.at[q], dt_ref.at[q], cw_ref, cb_ref, dtb_ref, alog_ref, dskip_ref, ng_ref,
                   triu_ref, shift_ref, y_ref.at[q], state_ref.at[q], ucat_ref.at[q], ybuf_ref.at[q])


def _ssd_chunk(xbc_ref, z_ref, dt_ref, cw_ref, cb_ref, dtb_ref, alog_ref, dskip_ref, ng_ref, triu_ref, shift_ref,
               y_ref, state_ref, ucat_ref, ybuf_ref):
    ucat_ref[CONV_HALO:, :] = xbc_ref[...]
    shifted = jnp.dot(shift_ref[...], ucat_ref[...], preferred_element_type=F32)
    ucat_ref[0:CONV_HALO, :] = ucat_ref[CHUNK:CHUNK + CONV_HALO, :]
    acc = cb_ref[...] + cw_ref[0:1, :] * shifted[0:CHUNK]
    for kk in range(1, CONV_K):
        acc = acc + cw_ref[kk:kk + 1, :] * shifted[kk * CHUNK:(kk + 1) * CHUNK]
    act = _silu_tanh(acc)
    xs = act[:, :D_SSM]
    gn = SSM_GROUPS * D_STATE

    dt_t = _softplus(dt_ref[...].T[0:SSM_HEADS, :] + dtb_ref[...])
    a_t = dt_t * (-jnp.exp(alog_ref[...]))
    a_hi = a_t.astype(BF16)
    a_mid = (a_t - a_hi.astype(F32)).astype(BF16)
    a_lo = (a_t - a_hi.astype(F32) - a_mid.astype(F32)).astype(BF16)
    triu = triu_ref[...]
    cs_t = (jnp.dot(a_hi, triu, preferred_element_type=F32) + jnp.dot(a_mid, triu, preferred_element_type=F32)
            + jnp.dot(a_lo, triu, preferred_element_type=F32))
    cs_end = cs_t[:, CHUNK - 1:CHUNK]
    r_t = cs_t - jnp.log(dt_t)
    w_t = jnp.exp(cs_end - cs_t) * dt_t
    chunk_decay = jnp.exp(cs_end)
    cols = jnp.concatenate([cs_t, jnp.exp(cs_t), jnp.zeros((LANES - 2 * SSM_HEADS, CHUNK), F32)], axis=0).T

    li = lax.broadcasted_iota(jnp.int32, (CHUNK, CHUNK), 0)
    si = lax.broadcasted_iota(jnp.int32, (CHUNK, CHUNK), 1)
    causal = li >= si
    low = _lane_half_mask((CHUNK, LANES))
    low_row = _lane_half_mask((1, LANES))

    heads_per_group = SSM_HEADS // SSM_GROUPS
    for g in range(SSM_GROUPS):
        b_g = act[:, D_SSM + g * D_STATE:D_SSM + (g + 1) * D_STATE]
        c_g = act[:, D_SSM + gn + g * D_STATE:D_SSM + gn + (g + 1) * D_STATE]
        b_gb = b_g.astype(BF16)
        c_gb = c_g.astype(BF16)
        cb = lax.dot_general(c_gb, b_gb, (((1,), (1,)), ((), ())), preferred_element_type=F32)
        b_t = b_g.T
        for jp in range(heads_per_group // 2):
            j = g * (heads_per_group // 2) + jp
            lanes = slice(j * LANES, (j + 1) * LANES)
            xp = xs[:, lanes]
            xpb = xp.astype(BF16)
            ydiag = jnp.zeros((CHUNK, LANES), F32)
            snew = jnp.zeros((D_STATE, LANES), F32)
            for half in range(2):
                h = 2 * j + half
                diff = cols[:, h:h + 1] - r_t[h:h + 1, :]
                m = (cb * jnp.exp(jnp.where(causal, diff, NEG_INF))).astype(BF16)
                keep = low if half == 0 else jnp.logical_not(low)
                xh = jnp.where(keep, xpb, jnp.zeros_like(xpb))
                ydiag = ydiag + jnp.dot(m, xh, preferred_element_type=F32)
                snew = snew + jnp.dot((b_t * w_t[h:h + 1, :]).astype(BF16), xh, preferred_element_type=F32)
            s_in = state_ref[:, lanes]
            yoff = jnp.dot(c_gb, s_in.astype(BF16), preferred_element_type=F32)
            h0 = 2 * j
            e0 = SSM_HEADS + h0
            escale = jnp.where(low, cols[:, e0:e0 + 1], cols[:, e0 + 1:e0 + 2])
            cdec = jnp.where(low_row, chunk_decay[h0:h0 + 1, :], chunk_decay[h0 + 1:h0 + 2, :])
            ybuf_ref[:, lanes] = ydiag + yoff * escale + xp * dskip_ref[:, lanes]
            state_ref[:, lanes] = s_in * cdec + snew

    yz = ybuf_ref[...] * _silu_tanh(z_ref[...].astype(F32))
    gw = D_SSM // SSM_GROUPS
    for g in range(SSM_GROUPS):
        part = yz[:, g * gw:(g + 1) * gw]
        ms = jnp.mean(part * part, axis=-1, keepdims=True)
        y_ref[:, g * gw:(g + 1) * gw] = (part * lax.rsqrt(ms + EPS)
                                            * ng_ref[:, g * gw:(g + 1) * gw]).astype(BF16)


def _side_cast_specs(weights, steps, step_of):
    specs, shapes = [], []
    for w in weights:
        per = w.shape[0] // steps
        assert per * steps == w.shape[0], "expert count must divide over the host kernel's grid"
        specs.append(pl.BlockSpec((per,) + w.shape[1:], lambda *g, _s=step_of: (_s(*g), 0, 0)))
        shapes.append(jax.ShapeDtypeStruct(w.shape, BF16))
    return specs, shapes


def _ssd_call(xbc, z, dt, conv_w, conv_b, dtb, alog, dskip, ng, triu, shift, wg, wu):
    bsz, l, _ = xbc.shape
    nc = l // CHUNK
    nseq = SSD_SEQS if bsz % SSD_SEQS == 0 else 1
    chunk = lambda w: pl.BlockSpec((nseq, CHUNK, w), lambda b, c: (b, c, 0))
    full = lambda a: pl.BlockSpec(a.shape, lambda b, c: (0, 0))
    side_specs, side_shapes = _side_cast_specs((wg, wu), (bsz // nseq) * nc, lambda b, c: b * nc + c)
    return pl.pallas_call(
        _ssd_kernel,
        grid=(bsz // nseq, nc),
        in_specs=[chunk(CONV_CH), chunk(D_SSM), chunk(LANES), full(conv_w), full(conv_b), full(dtb),
                  full(alog), full(dskip), full(ng), full(triu), full(shift)] + side_specs,
        out_specs=[chunk(D_SSM)] + side_specs,
        out_shape=[jax.ShapeDtypeStruct((bsz, l, D_SSM), BF16)] + side_shapes,
        scratch_shapes=[pltpu.VMEM((nseq, D_STATE, D_SSM), F32),
                        pltpu.VMEM((nseq, CONV_HALO + CHUNK, CONV_CH), BF16),
                        pltpu.VMEM((nseq, CHUNK, D_SSM), F32)],
        compiler_params=pltpu.CompilerParams(dimension_semantics=("arbitrary", "arbitrary"),
                                             vmem_limit_bytes=VMEM_LIMIT),
    )(xbc, z, dt, conv_w, conv_b, dtb, alog, dskip, ng, triu, shift, wg, wu)


assert WINDOW == ATT_BLOCK


def _rel_bucket_table():
    qi = np.arange(ATT_BLOCK)[:, None]
    c = np.arange(ATT_BLOCK)[None, :]
    dist = np.where(c > qi, qi + ATT_BLOCK - c, qi - c)
    max_exact = REL_BUCKETS // 2
    d = np.maximum(dist, 1).astype(np.float32)
    large = max_exact + (np.log(d / np.float32(max_exact)) / np.float32(math.log(REL_MAX_DIST / max_exact))
                         * np.float32(REL_BUCKETS - max_exact)).astype(np.int32)
    large = np.minimum(large, REL_BUCKETS - 1)
    return np.where(dist < max_exact, dist, large).astype(np.int32)


def _bias_kernel(rb_ref, bucket_ref, o_ref):
    bucket = bucket_ref[...]
    from_prev = (lax.broadcasted_iota(jnp.int32, bucket.shape, 1)
                 > lax.broadcasted_iota(jnp.int32, bucket.shape, 0))
    for h in range(ATT_HEADS):
        acc = jnp.zeros(bucket.shape, F32)
        for b in range(REL_BUCKETS):
            acc = jnp.where(bucket == b, rb_ref[b, h], acc)
        o_ref[1, h] = acc
        o_ref[0, h] = jnp.where(from_prev, NEG_INF, acc)


def _bias_call(rel_bias, bucket):
    return pl.pallas_call(
        _bias_kernel,
        in_specs=[pl.BlockSpec(memory_space=pltpu.SMEM), pl.BlockSpec(memory_space=pltpu.VMEM)],
        out_shape=jax.ShapeDtypeStruct((2, ATT_HEADS) + bucket.shape, F32),
    )(rel_bias, bucket)


ATT_SEQS = 4


def _attn_kernel(sink_ref, q_ref, kp_ref, kc_ref, vp_ref, vc_ref, bias_ref, ng_ref, wd_ref, o_ref, wdb_ref, obuf_ref):
    wdb_ref[...] = wd_ref[...].astype(BF16)
    for s in range(q_ref.shape[0]):
        _attn_block(sink_ref, q_ref.at[s], kp_ref.at[s], kc_ref.at[s], vp_ref.at[s], vc_ref.at[s], bias_ref.at[0],
                    ng_ref, o_ref.at[s], obuf_ref.at[s])


def _attn_block(sink_ref, q_ref, kp_ref, kc_ref, vp_ref, vc_ref, bias_ref, ng_ref, o_ref, obuf_ref):
    qi = lax.broadcasted_iota(jnp.int32, (ATT_BLOCK, ATT_BLOCK), 0)
    ci = lax.broadcasted_iota(jnp.int32, (ATT_BLOCK, ATT_BLOCK), 1)
    from_prev = ci > qi
    low = _lane_half_mask((ATT_BLOCK, LANES))

    def band_variants(prev_ref, cur_ref):
        out = []
        for cpair in range(KV_HEADS // 2):
            lanes = slice(cpair * LANES, (cpair + 1) * LANES)
            t = jnp.concatenate([prev_ref[:, lanes], cur_ref[:, lanes]], axis=0).astype(F32)
            out.append((t.astype(BF16), pltpu.roll(t, HALF, 1).astype(BF16)))
        return out

    k_band = band_variants(kp_ref, kc_ref)
    v_band = band_variants(vp_ref, vc_ref)
    nt = (((1,), (1,)), ((), ()))

    for j in range(ATT_HEADS // 2):
        qp = q_ref[:, j * LANES:(j + 1) * LANES]
        out_pair = jnp.zeros((ATT_BLOCK, LANES), F32)
        for half in range(2):
            h = 2 * j + half
            g = h // Q_PER_KV
            swapped = int((g % 2) != half)
            keep = low if half == 0 else jnp.logical_not(low)
            qh = jnp.where(keep, qp, jnp.zeros_like(qp))
            s_band = lax.dot_general(qh, k_band[g // 2][swapped], nt, preferred_element_type=F32)
            s = jnp.where(from_prev, s_band[:, :ATT_BLOCK], s_band[:, ATT_BLOCK:]) + bias_ref[h]
            sink = sink_ref[h]
            m = jnp.maximum(jnp.max(s, axis=-1, keepdims=True), sink)
            p = jnp.exp(s - m)
            denom = jnp.sum(p, axis=-1, keepdims=True) + jnp.exp(sink - m)
            p_band = jnp.concatenate([jnp.where(from_prev, p, 0.0), jnp.where(from_prev, 0.0, p)], axis=1)
            o = jnp.dot(p_band.astype(BF16), v_band[g // 2][swapped], preferred_element_type=F32) / denom
            out_pair = out_pair + jnp.where(keep, o, 0.0)
        obuf_ref[:, j * LANES:(j + 1) * LANES] = out_pair

    att = obuf_ref[...]
    ms = jnp.mean(att * att, axis=-1, keepdims=True)
    o_ref[...] = (att * lax.rsqrt(ms + EPS) * ng_ref[...]).astype(BF16)


def _attn_call(sinks, q, k, v, bias, ng, wd):
    bsz, l, _ = q.shape
    nb = l // ATT_BLOCK
    nseq = ATT_SEQS if bsz % ATT_SEQS == 0 else 1
    cur = lambda w: pl.BlockSpec((nseq, ATT_BLOCK, w), lambda b, i: (b, i, 0))
    prev = lambda w: pl.BlockSpec((nseq, ATT_BLOCK, w), lambda b, i: (b, jnp.maximum(i - 1, 0), 0))
    side_specs, side_shapes = _side_cast_specs((wd,), (bsz // nseq) * nb, lambda b, i: b * nb + i)
    return pl.pallas_call(
        _attn_kernel,
        grid=(bsz // nseq, nb),
        in_specs=[pl.BlockSpec(memory_space=pltpu.SMEM),
                  cur(D_ATT), prev(D_KV), cur(D_KV), prev(D_KV), cur(D_KV),
                  pl.BlockSpec((1,) + bias.shape[1:], lambda b, i: (jnp.minimum(i, 1), 0, 0, 0)),
                  pl.BlockSpec(ng.shape, lambda b, i: (0, 0))] + side_specs,
        out_specs=[cur(D_ATT)] + side_specs,
        out_shape=[jax.ShapeDtypeStruct((bsz, l, D_ATT), BF16)] + side_shapes,
        scratch_shapes=[pltpu.VMEM((nseq, ATT_BLOCK, D_ATT), F32)],
        compiler_params=pltpu.CompilerParams(dimension_semantics=("arbitrary", "arbitrary"),
                                             vmem_limit_bytes=VMEM_LIMIT),
    )(sinks, q, k, k, v, v, bias, ng, wd)


def _out_proj_kernel(x_ref, ys_ref, ya_ref, g1_ref, sc_ref, sh_ref, g2_ref, ng_ref, wo_hbm, sg_hbm, su_hbm, sd_hbm,
                     rwh_ref, rwl_ref, rb_ref, upper_ref,
                     base_ref, h_ref, idx_ref, gate_ref, rank_ref, cnt_ref,
                     carry_ref, wo_raw, sg_raw, su_raw, sd_raw, wo_ref, sg_ref, su_ref, sd_ref, wsem):
    @pl.when(pl.program_id(0) == 0)
    def _():
        carry_ref[...] = jnp.zeros_like(carry_ref)
        staged = ((wo_hbm, wo_raw, wo_ref), (sg_hbm, sg_raw, sg_ref), (su_hbm, su_raw, su_ref),
                  (sd_hbm, sd_raw, sd_ref))
        copies = [pltpu.make_async_copy(src.at[0], raw, wsem.at[j]) for j, (src, raw, _) in enumerate(staged)]
        for cp in copies:
            cp.start()
        for cp, (_, raw, dst) in zip(copies, staged):
            cp.wait()
            dst[...] = raw[...].astype(BF16)

    mix = (jnp.dot(ys_ref[...], wo_ref[:D_SSM, :], preferred_element_type=F32)
           + jnp.dot(ya_ref[...], wo_ref[D_SSM:, :], preferred_element_type=F32))
    x1 = x_ref[...] + g1_ref[0] * mix
    ms = jnp.mean(x1 * x1, axis=-1, keepdims=True)
    h = x1 * lax.rsqrt(ms + EPS) * ng_ref[...]
    h = h * (1.0 + sc_ref[0]) + sh_ref[0]
    half = h.shape[1] // 2
    h_ref[...] = _pack_bf16_pair(h[:, :half], h[:, half:])
    hi, lo = _split_hi_lo(h)
    hi_terms = jnp.dot(hi, rwl_ref[...], preferred_element_type=F32)
    logits = (hi_terms[:, :LANES] + hi_terms[:, LANES:]
              + jnp.dot(lo, rwh_ref[...], preferred_element_type=F32))
    _route_tokens(logits.T[0:N_EXPERTS, :], rb_ref, upper_ref, idx_ref, gate_ref, rank_ref, cnt_ref, carry_ref)
    u = _silu(jnp.dot(hi, sg_ref[...], preferred_element_type=F32)) * jnp.dot(hi, su_ref[...],
                                                                              preferred_element_type=F32)
    shared = jnp.dot(u.astype(BF16), sd_ref[...], preferred_element_type=F32)
    base_ref[...] = x1 + g2_ref[0] * shared


def _out_proj_call(x2, ys, ya, g1, sc2, sh2, g2, ng, w_out, sg, su, sd, rwh, rwl, router_bias, upper,
                   tiles_per_batch, tm):
    n, d = x2.shape
    row = lambda w: pl.BlockSpec((tm, w), lambda i: (i, 0))
    tok = lambda r: pl.BlockSpec((r, tm), lambda i: (0, i))
    full = lambda a: pl.BlockSpec(a.shape, lambda i: (0, 0))
    per_batch = pl.BlockSpec((1, 1, d), lambda i: (i // tiles_per_batch, 0, 0))
    hbm = pl.BlockSpec(memory_space=pl.ANY)
    staged = (w_out, sg, su, sd)
    return pl.pallas_call(
        _out_proj_kernel,
        grid=(n // tm,),
        in_specs=[row(d), row(D_SSM), row(D_ATT), per_batch, per_batch, per_batch, per_batch, full(ng),
                  hbm, hbm, hbm, hbm, full(rwh), full(rwl), full(router_bias), full(upper)],
        out_specs=[row(d), row(d // 2), tok(TOP_K), tok(TOP_K), tok(TOP_K),
                   pl.BlockSpec((N_EXPERTS, LANES), lambda i: (0, 0))],
        out_shape=[jax.ShapeDtypeStruct((n, d), F32), jax.ShapeDtypeStruct((n, d // 2), jnp.uint32),
                   jax.ShapeDtypeStruct((TOP_K, n), jnp.int32), jax.ShapeDtypeStruct((TOP_K, n), F32),
                   jax.ShapeDtypeStruct((TOP_K, n), jnp.int32), jax.ShapeDtypeStruct((N_EXPERTS, LANES), F32)],
        scratch_shapes=([pltpu.VMEM((N_EXPERTS, LANES), F32)]
                        + [pltpu.VMEM(a.shape[1:], a.dtype) for a in staged]
                        + [pltpu.VMEM(a.shape[1:], BF16) for a in staged]
                        + [pltpu.SemaphoreType.DMA((len(staged),))]),
        compiler_params=pltpu.CompilerParams(dimension_semantics=("arbitrary",),
                                             vmem_limit_bytes=VMEM_LIMIT),
    )(x2, ys, ya, g1, sc2, sh2, g2, ng, w_out, sg, su, sd, rwh, rwl, router_bias, upper)


def _route_tokens(logits_t, rb_ref, upper_ref, idx_ref, gate_ref, rank_ref, cnt_ref, carry_ref):
    t = logits_t.shape[1]
    per_group = N_EXPERTS // ROUTE_GROUPS
    scores = 1.0 / (1.0 + jnp.exp(-logits_t))
    sel = scores + rb_ref[...]
    e_iota = lax.broadcasted_iota(jnp.int32, (N_EXPERTS, t), 0)

    sel3 = sel.reshape(ROUTE_GROUPS, per_group, t)
    w_iota = lax.broadcasted_iota(jnp.int32, sel3.shape, 1)
    m1 = jnp.max(sel3, axis=1, keepdims=True)
    first = jnp.min(jnp.where(sel3 == m1, w_iota, per_group), axis=1, keepdims=True)
    m2 = jnp.max(jnp.where(w_iota == first, NEG_INF, sel3), axis=1, keepdims=True)
    grp = (m1 + m2).reshape(ROUTE_GROUPS, t)

    g_iota = lax.broadcasted_iota(jnp.int32, (ROUTE_GROUPS, t), 0)
    gmask = jnp.zeros((ROUTE_GROUPS, t), jnp.bool_)
    for _ in range(ROUTE_TOPK_GROUPS):
        gm = jnp.max(grp, axis=0, keepdims=True)
        gfirst = jnp.min(jnp.where(grp == gm, g_iota, ROUTE_GROUPS), axis=0, keepdims=True)
        hit = g_iota == gfirst
        gmask = jnp.logical_or(gmask, hit)
        grp = jnp.where(hit, NEG_INF, grp)
    allowed = jnp.broadcast_to(gmask.reshape(ROUTE_GROUPS, 1, t),
                               (ROUTE_GROUPS, per_group, t)).reshape(N_EXPERTS, t)
    masked = jnp.where(allowed, sel, NEG_INF)

    picked = jnp.zeros((N_EXPERTS, t), jnp.bool_)
    idx_rows = []
    w_rows = []
    for _ in range(TOP_K):
        mm = jnp.max(masked, axis=0, keepdims=True)
        efirst = jnp.min(jnp.where(masked == mm, e_iota, N_EXPERTS), axis=0, keepdims=True)
        hit = e_iota == efirst
        idx_rows.append(efirst)
        w_rows.append(jnp.sum(jnp.where(hit, scores, 0.0), axis=0, keepdims=True))
        picked = jnp.logical_or(picked, hit)
        masked = jnp.where(hit, NEG_INF, masked)
    idx = jnp.concatenate(idx_rows, axis=0)
    w = jnp.concatenate(w_rows, axis=0)
    gate_ref[...] = w / jnp.sum(w, axis=0, keepdims=True) * ROUTED_SCALE
    idx_ref[...] = idx

    onehot = jnp.where(picked, 1.0, 0.0)
    sub = upper_ref.shape[0]
    carry = carry_ref[:, 0:1]
    parts = []
    for s0 in range(0, t, sub):
        oh = onehot[:, s0:s0 + sub]
        parts.append(jnp.dot(oh.astype(BF16), upper_ref[...], preferred_element_type=F32) + carry)
        carry = carry + jnp.sum(oh, axis=1, keepdims=True)
    rank_full = jnp.concatenate(parts, axis=1)
    rank_rows = [jnp.sum(jnp.where(e_iota == idx_rows[k], rank_full, 0.0), axis=0, keepdims=True)
                 for k in range(TOP_K)]
    rank_ref[...] = jnp.concatenate(rank_rows, axis=0).astype(jnp.int32)
    carry_ref[...] = jnp.broadcast_to(carry, carry_ref.shape)
    cnt_ref[...] = carry_ref[...]


PLAN_EXPERT, PLAN_FRESH, PLAN_VALID, PLAN_SEG, PLAN_NEXT, PLAN_NUSED = range(6)
PLAN_ROWS = SUBLANES


def _plan_kernel(cnt_ref, tri_ref, start_ref, plan_ref, *, nblocks):
    nbp = plan_ref.shape[1]
    cnt = cnt_ref[...].astype(jnp.int32)
    blocks = (cnt + (EXPERT_ROWS - 1)) // EXPERT_ROWS
    end = jnp.dot(tri_ref[...], blocks.astype(F32), precision=lax.Precision.HIGHEST,
                  preferred_element_type=F32).astype(jnp.int32)
    start = end - blocks
    start_ref[...] = start[:, 0:1] * EXPERT_ROWS
    nused = end[N_EXPERTS - 1:N_EXPERTS, 0:1]

    e_iota = lax.broadcasted_iota(jnp.int32, (N_EXPERTS, nbp), 0)
    blk = lax.broadcasted_iota(jnp.int32, (1, nbp), 1)
    expert = jnp.minimum(jnp.sum((end[:, 0:1] <= blk).astype(jnp.int32), axis=0, keepdims=True), N_EXPERTS - 1)
    mine = e_iota == expert
    pick = lambda col: jnp.sum(jnp.where(mine, col, 0), axis=0, keepdims=True)
    first = pick(start[:, 0:1])
    valid = jnp.clip(pick(cnt[:, 0:1]) - (blk - first) * EXPERT_ROWS, 0, EXPERT_ROWS)
    present = jnp.logical_or(blocks[:, 0:1] > 0,
                             jnp.logical_and(e_iota[:, 0:1] == N_EXPERTS - 1, nused < nblocks))
    seg = jnp.sum(jnp.logical_and(present, e_iota <= expert).astype(jnp.int32), axis=0, keepdims=True) - 1
    nxt = jnp.min(jnp.where(jnp.logical_and(present, e_iota > expert), e_iota, N_EXPERTS), axis=0, keepdims=True)
    rows = {PLAN_EXPERT: expert, PLAN_FRESH: (blk == first).astype(jnp.int32), PLAN_VALID: valid, PLAN_SEG: seg,
            PLAN_NEXT: jnp.where(nxt == N_EXPERTS, -1, nxt), PLAN_NUSED: jnp.broadcast_to(nused, (1, nbp))}
    zero = jnp.zeros((1, nbp), jnp.int32)
    plan_ref[...] = jnp.concatenate([rows.get(r, zero) for r in range(PLAN_ROWS)], axis=0)


def _plan_call(counts, nblocks):
    nbp = -(-nblocks // LANES) * LANES
    tri = jnp.asarray(np.tril(np.ones((N_EXPERTS, N_EXPERTS), np.float32)))
    return pl.pallas_call(
        functools.partial(_plan_kernel, nblocks=nblocks),
        out_shape=[jax.ShapeDtypeStruct((N_EXPERTS, 1), jnp.int32),
                   jax.ShapeDtypeStruct((PLAN_ROWS, nbp), jnp.int32)],
    )(counts, tri)


def _dest_kernel(idx_ref, rank_ref, start_ref, dest_ref):
    t = idx_ref.shape[1]
    e_iota = lax.broadcasted_iota(jnp.int32, (N_EXPERTS, t), 0)
    rows = [jnp.sum(jnp.where(e_iota == idx_ref[k:k + 1, :], start_ref[...], 0), axis=0, keepdims=True)
            for k in range(TOP_K)]
    dest = jnp.concatenate(rows, axis=0) + rank_ref[...]
    for k in range(TOP_K):
        for c in range(t // LANES):
            dest_ref[k, c:c + 1, :] = dest[k:k + 1, c * LANES:(c + 1) * LANES]


def _dest_call(idx, rank, pad_start, tile):
    n = idx.shape[1]
    tok = pl.BlockSpec((TOP_K, tile), lambda i: (0, i))
    return pl.pallas_call(
        _dest_kernel,
        grid=(n // tile,),
        in_specs=[tok, tok, pl.BlockSpec((N_EXPERTS, 1), lambda i: (0, 0))],
        out_specs=pl.BlockSpec((TOP_K, tile // LANES, LANES), lambda i: (0, i, 0)),
        out_shape=jax.ShapeDtypeStruct((TOP_K, n // LANES, LANES), jnp.int32),
        compiler_params=pltpu.CompilerParams(dimension_semantics=("arbitrary",)),
    )(idx, rank, pad_start)


def _scatter_rows_sc(rows, dest_flat, total_rows, chunk):
    n, w = rows.shape
    copies = dest_flat.shape[0] // n
    info = plsc.get_sparse_core_info()
    nc = info.num_cores
    per_worker = n // (nc * info.num_subcores)
    assert per_worker * nc * info.num_subcores == n and per_worker % chunk == 0
    mesh = plsc.VectorSubcoreMesh(core_axis_name="c", subcore_axis_name="s")

    @functools.partial(
        pl.kernel, mesh=mesh,
        out_type=jax.ShapeDtypeStruct((total_rows, w), rows.dtype),
        scratch_types=[pltpu.VMEM((chunk,), jnp.int32), pltpu.VMEM((chunk, w), rows.dtype),
                       pltpu.SemaphoreType.DMA],
    )
    def scatter(rows_hbm, idx_hbm, out_hbm, idx_v, rows_v, sem):
        worker = lax.axis_index("s") * nc + lax.axis_index("c")

        @pl.loop(0, per_worker // chunk)
        def _(j):
            base = worker * per_worker + j * chunk
            pltpu.sync_copy(rows_hbm.at[pl.ds(base, chunk)], rows_v)
            for k in range(copies):
                pltpu.sync_copy(idx_hbm.at[pl.ds(k * n + base, chunk)], idx_v)
                pltpu.async_copy(rows_v, out_hbm.at[idx_v], sem).wait()

    return scatter(rows, dest_flat)


X_RING = 3


def _expert_kernel(plan_ref, xs_hbm, wg_hbm, wu_hbm, wd_hbm, y_ref, xbuf_ref, xsem, wg_buf, wu_buf, wd_buf, wsem):
    i = pl.program_id(0)
    rows, half = xbuf_ref.shape[1], xbuf_ref.shape[2]
    nused = plan_ref[PLAN_NUSED, 0]

    def w_copies(expert, slot):
        return [pltpu.make_async_copy(src.at[expert], dst.at[slot], wsem.at[slot])
                for src, dst in ((wg_hbm, wg_buf), (wu_hbm, wu_buf), (wd_hbm, wd_buf))]

    def x_copy(block):
        first = pl.multiple_of(block * rows, rows)
        slot = block % X_RING
        return pltpu.make_async_copy(xs_hbm.at[pl.ds(first, rows)], xbuf_ref.at[slot], xsem.at[slot])

    @pl.when(i == 0)
    def _():
        for b in range(X_RING - 1):
            @pl.when(b < nused)
            def _():
                x_copy(b).start()

    @pl.when(i + (X_RING - 1) < nused)
    def _():
        x_copy(i + (X_RING - 1)).start()

    wslot = plan_ref[PLAN_SEG, i] % 2

    @pl.when(i == 0)
    def _():
        for c in w_copies(plan_ref[PLAN_EXPERT, 0], 0):
            c.start()

    @pl.when(plan_ref[PLAN_FRESH, i] > 0)
    def _():
        for c in w_copies(plan_ref[PLAN_EXPERT, i], wslot):
            c.wait()

        @pl.when(plan_ref[PLAN_NEXT, i] >= 0)
        def _():
            for c in w_copies(plan_ref[PLAN_NEXT, i], 1 - wslot):
                c.start()

    @pl.when(i < nused)
    def _():
        x_copy(i).wait()
        xw = xbuf_ref[i % X_RING]
        row = lax.broadcasted_iota(jnp.int32, xw.shape, 0)
        x_lo, x_hi = _unpack_bf16_pair(jnp.where(row < plan_ref[PLAN_VALID, i], xw, jnp.uint32(0)))
        x_lo = x_lo.astype(BF16)
        x_hi = x_hi.astype(BF16)
        gate = (jnp.dot(x_lo, wg_buf[wslot, :half, :], preferred_element_type=F32)
                + jnp.dot(x_hi, wg_buf[wslot, half:, :], preferred_element_type=F32))
        up = (jnp.dot(x_lo, wu_buf[wslot, :half, :], preferred_element_type=F32)
              + jnp.dot(x_hi, wu_buf[wslot, half:, :], preferred_element_type=F32))
        u = (_silu(gate) * up).astype(BF16)
        y_lo = jnp.dot(u, wd_buf[wslot, :, :half], preferred_element_type=F32)
        y_hi = jnp.dot(u, wd_buf[wslot, :, half:], preferred_element_type=F32)
        y_ref[...] = _pack_bf16_pair(y_lo, y_hi)

    @pl.when(i >= nused)
    def _():
        y_ref[...] = jnp.zeros_like(y_ref)


def _expert_call(plan, xs, wg, wu, wd, rows):
    p, w = xs.shape
    d, f = wg.shape[1], wg.shape[2]
    hbm = pl.BlockSpec(memory_space=pl.ANY)
    grid_spec = pltpu.PrefetchScalarGridSpec(
        num_scalar_prefetch=1,
        grid=(p // rows,),
        in_specs=[hbm, hbm, hbm, hbm],
        out_specs=pl.BlockSpec((rows, w), lambda i, *_: (i, 0)),
        scratch_shapes=[pltpu.VMEM((X_RING, rows, w), xs.dtype), pltpu.SemaphoreType.DMA((X_RING,)),
                        pltpu.VMEM((2, d, f), wg.dtype), pltpu.VMEM((2, d, f), wu.dtype),
                        pltpu.VMEM((2, f, d), wd.dtype), pltpu.SemaphoreType.DMA((2,))],
    )
    return pl.pallas_call(
        _expert_kernel,
        grid_spec=grid_spec,
        out_shape=jax.ShapeDtypeStruct((p, w), jnp.uint32),
        compiler_params=pltpu.CompilerParams(dimension_semantics=("arbitrary",),
                                             vmem_limit_bytes=VMEM_LIMIT),
    )(plan, xs, wg, wu, wd)


def _gather_rows_sc(table, idx, chunk):
    m = idx.shape[0]
    w = table.shape[1]
    info = plsc.get_sparse_core_info()
    nc = info.num_cores
    per_worker = m // (nc * info.num_subcores)
    assert per_worker * nc * info.num_subcores == m and per_worker % chunk == 0
    mesh = plsc.VectorSubcoreMesh(core_axis_name="c", subcore_axis_name="s")

    @functools.partial(
        pl.kernel, mesh=mesh,
        out_type=jax.ShapeDtypeStruct((m, w), table.dtype),
        scratch_types=[pltpu.VMEM((chunk,), jnp.int32), pltpu.VMEM((chunk, w), table.dtype),
                       pltpu.SemaphoreType.DMA],
    )
    def gather(table_hbm, idx_hbm, out_hbm, idx_v, rows_v, sem):
        worker = lax.axis_index("s") * nc + lax.axis_index("c")

        @pl.loop(0, per_worker // chunk)
        def _(j):
            base = worker * per_worker + j * chunk
            pltpu.sync_copy(idx_hbm.at[pl.ds(base, chunk)], idx_v)
            pltpu.async_copy(table_hbm.at[idx_v], rows_v, sem).wait()
            pltpu.sync_copy(rows_v, out_hbm.at[pl.ds(base, chunk)])

    return gather(table, idx)


def _combine_kernel(yk_ref, gate_ref, base_ref, g2_ref, fg_ref, o_ref):
    t = base_ref.shape[0]
    half = yk_ref.shape[2]
    gates = jnp.concatenate([gate_ref[...], jnp.zeros((LANES - TOP_K, t), F32)], axis=0).T
    r_lo = jnp.zeros((t, half), F32)
    r_hi = jnp.zeros((t, half), F32)
    for k in range(TOP_K):
        y_lo, y_hi = _unpack_bf16_pair(yk_ref[k])
        r_lo = r_lo + gates[:, k:k + 1] * y_lo
        r_hi = r_hi + gates[:, k:k + 1] * y_hi
    g2 = g2_ref[0]
    x_lo = base_ref[:, :half] + g2[:, :half] * r_lo
    x_hi = base_ref[:, half:] + g2[:, half:] * r_hi
    ms = (jnp.sum(x_lo * x_lo, axis=-1, keepdims=True)
          + jnp.sum(x_hi * x_hi, axis=-1, keepdims=True)) * (1.0 / (2 * half))
    inv = lax.rsqrt(ms + EPS)
    o_ref[:, :half] = x_lo * inv * fg_ref[:, :half]
    o_ref[:, half:] = x_hi * inv * fg_ref[:, half:]


def _combine_call(yk, gates, base, g2, fg, tiles_per_batch, tile, tile0):
    n, d = base.shape
    row = lambda w: pl.BlockSpec((tile, w), lambda i: (i + tile0, 0))
    return pl.pallas_call(
        _combine_kernel,
        grid=(yk.shape[1] // tile,),
        in_specs=[pl.BlockSpec((TOP_K, tile, yk.shape[2]), lambda i: (0, i, 0)),
                  pl.BlockSpec((TOP_K, tile), lambda i: (0, i + tile0)), row(d),
                  pl.BlockSpec((1, 1, d), lambda i: ((i + tile0) // tiles_per_batch, 0, 0)),
                  pl.BlockSpec((1, d), lambda i: (0, 0))],
        out_specs=row(d),
        out_shape=jax.ShapeDtypeStruct((n, d), F32),
        input_output_aliases={2: 0},
        compiler_params=pltpu.CompilerParams(dimension_semantics=("arbitrary",),
                                             vmem_limit_bytes=VMEM_LIMIT),
    )(yk, gates, base, g2, fg)


def _pad_cols(a, width):
    return jnp.pad(a, ((0, 0), (0, width - a.shape[1])))


def _layer(x, mod, norm1_g, norm2_g, w_in, conv_w, conv_b, dt_bias, a_log, d_skip, ssm_norm_g,
           att_norm_g, sinks, rel_bias, w_out, router_w, router_bias, exp_w_gate, exp_w_up, exp_w_down,
           sh_w_gate, sh_w_up, sh_w_down, final_g):
    bsz, l, d = x.shape
    n = bsz * l
    tm = min(ROW_TILE, l)

    sh1, sc1, g1, sh2, sc2, g2 = [m[:, None, :] for m in jnp.split(mod, 6, axis=-1)]

    assert math.frexp(ATT_HEAD_DIM ** -0.5)[0] == 0.5
    x2 = x.reshape(n, d)
    z, xbc, dt, q, k, v = _in_proj_call(x2, sc1, sh1, norm1_g[None, :], jnp.swapaxes(w_in, 1, 2), l // tm, tm)

    triu = jnp.asarray(np.triu(np.ones((CHUNK, CHUNK), np.float32))).astype(BF16)
    shift = jnp.asarray(_conv_shift_matrix()).astype(BF16)
    y_ssm, exp_wg_b, exp_wu_b = _ssd_call(
        xbc.reshape(bsz, l, CONV_CH), z.reshape(bsz, l, D_SSM), dt.reshape(bsz, l, LANES), conv_w, conv_b[None, :],
        dt_bias[:, None], a_log[:, None], jnp.repeat(d_skip, SSM_HEAD_DIM)[None, :], ssm_norm_g[None, :], triu,
        shift, exp_w_gate, exp_w_up)

    bias = _bias_call(rel_bias, jnp.asarray(_rel_bucket_table()))
    y_att, exp_wd_b = _attn_call(sinks, q.reshape(bsz, l, D_ATT), k.reshape(bsz, l, D_KV), v.reshape(bsz, l, D_KV),
                                 bias, att_norm_g[None, :], exp_w_down)

    rw = _pad_cols(router_w, LANES)
    rwh = rw.astype(BF16)
    rwl = jnp.concatenate([rwh, (rw - rwh.astype(F32)).astype(BF16)], axis=1)
    rs = min(RANK_SUB, tm)
    upper = jnp.asarray(np.triu(np.ones((rs, rs), np.float32), 1)).astype(BF16)
    base, h2, idx, gates, rank, counts = _out_proj_call(
        x2, y_ssm.reshape(n, D_SSM), y_att.reshape(n, D_ATT), g1, sc2, sh2, g2, norm2_g[None, :],
        w_out, sh_w_gate, sh_w_up, sh_w_down, rwh, rwl, router_bias[:, None], upper, l // tm, tm)
    rt = min(ROUTE_TILE, n)

    nblocks = (n * TOP_K + N_EXPERTS * (EXPERT_ROWS - 1) + EXPERT_ROWS - 1) // EXPERT_ROWS
    pad_start, plan = _plan_call(counts, nblocks)
    dest = _dest_call(idx, rank, pad_start, rt)

    xs = _scatter_rows_sc(h2, dest.reshape(-1), nblocks * EXPERT_ROWS, SC_CHUNK)
    ys = _expert_call(plan, xs, exp_wg_b, exp_wu_b, exp_wd_b, EXPERT_ROWS)
    ctile = min(COMBINE_TILE, l)
    groups = COMBINE_GROUPS if bsz % COMBINE_GROUPS == 0 else 1
    ng = n // groups
    out = base
    for g in range(groups):
        idx_g = dest[:, g * ng // LANES:(g + 1) * ng // LANES, :].reshape(-1)
        yk = _gather_rows_sc(ys, idx_g, SC_CHUNK).reshape(TOP_K, ng, ys.shape[1])
        out = _combine_call(yk, gates, out, g2, final_g[None, :], l // ctile, ctile, g * ng // ctile)
    return out.reshape(bsz, l, d)


def kernel(x, c, mod_w, mod_b, norm1_g, norm2_g, w_in, conv_w, conv_b, dt_bias, a_log, d_skip, ssm_norm_g,
           att_norm_g, sinks, rel_bias, w_out, router_w, router_bias, exp_w_gate, exp_w_up, exp_w_down,
           sh_w_gate, sh_w_up, sh_w_down, final_g):
    assert mod_w.shape[0] == 1, "single-layer block"
    bsz = x.shape[0]
    c_pad = jnp.pad(c, ((0, SUBLANES - bsz % SUBLANES if bsz % SUBLANES else 0), (0, 0)))
    mod = _mod_call(c_pad, mod_w[0], mod_b[0][None, :])[:bsz]
    return _layer(x, mod, norm1_g[0], norm2_g[0], w_in, conv_w[0], conv_b[0], dt_bias[0], a_log[0], d_skip[0],
                  ssm_norm_g[0], att_norm_g[0], sinks[0], rel_bias, w_out, router_w[0], router_bias[0],
                  exp_w_gate[0], exp_w_up[0], exp_w_down[0], sh_w_gate, sh_w_up, sh_w_down, final_g)
```

```python
import functools
import math

import numpy as np
import jax
import jax.numpy as jnp
from jax import lax
from jax.experimental import pallas as pl
from jax.experimental.pallas import tpu as pltpu
from jax.experimental.pallas import tpu_sc as plsc

F32 = jnp.float32
BF16 = jnp.bfloat16

D_MODEL = 1024
SSM_HEAD_DIM = 64
D_SSM = D_MODEL
SSM_HEADS = D_SSM // SSM_HEAD_DIM
SSM_GROUPS = 4
D_STATE = 128
CONV_K = 4
CONV_CH = D_SSM + 2 * SSM_GROUPS * D_STATE
CHUNK = 128
ATT_HEAD_DIM = 64
D_ATT = D_MODEL
ATT_HEADS = D_ATT // ATT_HEAD_DIM
KV_HEADS = ATT_HEADS // 4
Q_PER_KV = ATT_HEADS // KV_HEADS
D_KV = KV_HEADS * ATT_HEAD_DIM
WINDOW = 128
ATT_BLOCK = 128
REL_BUCKETS = 32
REL_MAX_DIST = 128
N_EXPERTS = 64
TOP_K = 8
EXPERT_DIM = D_MODEL // 4
SHARED_DIM = D_MODEL // 4
ROUTE_GROUPS = 8
ROUTE_TOPK_GROUPS = 4
ROUTED_SCALE = 2.5
EPS = 1e-6

LANES = 128
SUBLANES = 8
HALF = LANES // 2

ROW_TILE = 512
ROUTE_TILE = 2048
RANK_SUB = 512
COMBINE_TILE = 256
SC_CHUNK = 128
EXPERT_ROWS = 512
COMBINE_GROUPS = 2
VMEM_LIMIT = 48 * 1024 * 1024

NEG_INF = float("-inf")


def _silu(v):
    return v * (1.0 / (1.0 + jnp.exp(-v)))


def _softplus(v):
    return jnp.maximum(v, 0.0) + jnp.log(1.0 + jnp.exp(-jnp.abs(v)))


def _bdot(a, b):
    return jnp.dot(a.astype(BF16), b.astype(BF16), preferred_element_type=F32)


def _split_hi_lo(v):
    hi = v.astype(BF16)
    lo = (v - hi.astype(F32)).astype(BF16)
    return hi, lo


def _pack_bf16_pair(a, b):
    w = pltpu.pack_elementwise([a, b], packed_dtype=BF16)
    return w if w.dtype == jnp.uint32 else lax.bitcast_convert_type(w, jnp.uint32)


def _unpack_bf16_pair(w):
    a = pltpu.unpack_elementwise(w, index=0, packed_dtype=BF16, unpacked_dtype=F32)
    b = pltpu.unpack_elementwise(w, index=1, packed_dtype=BF16, unpacked_dtype=F32)
    return a, b


def _lane_half_mask(shape):
    return lax.broadcasted_iota(jnp.int32, shape, len(shape) - 1) < HALF


def _mod_kernel(c_ref, w_ref, b_ref, o_ref):
    a = _silu(c_ref[...])
    o_ref[...] = jnp.dot(a, w_ref[...], precision=lax.Precision.HIGHEST,
                         preferred_element_type=F32) + b_ref[...]


def _mod_call(c_pad, mod_w, mod_b):
    rows, d = c_pad.shape
    cols = mod_w.shape[1]
    return pl.pallas_call(
        _mod_kernel,
        grid=(cols // d,),
        in_specs=[pl.BlockSpec((rows, d), lambda j: (0, 0)),
                  pl.BlockSpec((d, d), lambda j: (0, j)),
                  pl.BlockSpec((1, d), lambda j: (0, j))],
        out_specs=pl.BlockSpec((rows, d), lambda j: (0, j)),
        out_shape=jax.ShapeDtypeStruct((rows, cols), F32),
        compiler_params=pltpu.CompilerParams(dimension_semantics=("arbitrary",),
                                             vmem_limit_bytes=VMEM_LIMIT),
    )(c_pad, mod_w, mod_b)


IN_PROJ_SEGMENTS = ((D_SSM, BF16), (CONV_CH, BF16), (LANES, F32), (D_ATT, BF16), (D_KV, BF16), (D_KV, BF16))


def _in_proj_kernel(x_ref, sc_ref, sh_ref, g_ref, w_hbm, *refs):
    out_refs = refs[:len(IN_PROJ_SEGMENTS)]
    wraw_ref, w_ref, wsem = refs[len(IN_PROJ_SEGMENTS):]

    @pl.when(pl.program_id(0) == 0)
    def _():
        cp = pltpu.make_async_copy(w_hbm.at[0], wraw_ref, wsem)
        cp.start()
        cp.wait()
        src_dt = D_SSM + CONV_CH
        dst_q = src_dt + LANES
        src_q = src_dt + SSM_HEADS
        q_scale = ATT_HEAD_DIM ** -0.5

        def put(dst, src, scale=None):
            t = wraw_ref[src:src + LANES, :].T
            w_ref[:, dst:dst + LANES] = (t if scale is None else t * scale).astype(BF16)

        for c0 in range(0, src_dt, LANES):
            put(c0, c0)
        dt_tile = wraw_ref[src_dt:src_dt + LANES, :].T
        lane = lax.broadcasted_iota(jnp.int32, dt_tile.shape, 1)
        w_ref[:, src_dt:dst_q] = jnp.where(lane < SSM_HEADS, dt_tile, 0.0).astype(BF16)
        for c0 in range(0, D_ATT, LANES):
            put(dst_q + c0, src_q + c0, q_scale)
        for c0 in range(D_ATT, D_ATT + 2 * D_KV, LANES):
            put(dst_q + c0, src_q + c0)

    xf = x_ref[...]
    ms = jnp.mean(xf * xf, axis=-1, keepdims=True)
    h = xf * lax.rsqrt(ms + EPS) * g_ref[...]
    h = h * (1.0 + sc_ref[0]) + sh_ref[0]
    hb = h.astype(BF16)
    col = 0
    for (width, dtype), o_ref in zip(IN_PROJ_SEGMENTS, out_refs):
        o_ref[...] = jnp.dot(hb, w_ref[:, col:col + width], preferred_element_type=F32).astype(dtype)
        col += width


def _in_proj_call(x2, sc1, sh1, g1n, w_in_t, tiles_per_batch, tm):
    n, d = x2.shape
    row = lambda w: pl.BlockSpec((tm, w), lambda i: (i, 0))
    full = lambda a: pl.BlockSpec(a.shape, lambda i: (0, 0))
    per_batch = pl.BlockSpec((1, 1, d), lambda i: (i // tiles_per_batch, 0, 0))
    cols = sum(w for w, _ in IN_PROJ_SEGMENTS)
    w_in = w_in_t
    assert w_in.shape == (1, cols - (LANES - SSM_HEADS), d)
    return pl.pallas_call(
        _in_proj_kernel,
        grid=(n // tm,),
        in_specs=[row(d), per_batch, per_batch, full(g1n), pl.BlockSpec(memory_space=pl.ANY)],
        out_specs=[row(w) for w, _ in IN_PROJ_SEGMENTS],
        out_shape=[jax.ShapeDtypeStruct((n, w), dt) for w, dt in IN_PROJ_SEGMENTS],
        scratch_shapes=[pltpu.VMEM(w_in.shape[1:], w_in.dtype), pltpu.VMEM((d, cols), BF16), pltpu.SemaphoreType.DMA(())],
        compiler_params=pltpu.CompilerParams(dimension_semantics=("arbitrary",),
                                             vmem_limit_bytes=VMEM_LIMIT),
    )(x2, sc1, sh1, g1n, w_in)


SSD_SEQS = 4
CONV_HALO = 16


def _conv_shift_matrix():
    s = np.zeros((CONV_K * CHUNK, CONV_HALO + CHUNK), np.float32)
    for k in range(CONV_K):
        t = np.arange(CHUNK)
        s[k * CHUNK + t, CONV_HALO + t - (CONV_K - 1) + k] = 1.0
    return s


def _silu_tanh(v):
    hv = 0.5 * v
    return hv + hv * jnp.tanh(hv)


def _ssd_kernel(xbc_ref, z_ref, dt_ref, cw_ref, cb_ref, dtb_ref, alog_ref, dskip_ref, ng_ref, triu_ref, shift_ref,
                y_ref, state_ref, ucat_ref, ybuf_ref):
    nseq = xbc_ref.shape[0]

    @pl.when(pl.program_id(1) == 0)
    def _():
        state_ref[...] = jnp.zeros_like(state_ref)
        ucat_ref[:, 0:CONV_HALO, :] = jnp.zeros((nseq, CONV_HALO, CONV_CH), BF16)

    for q in range(nseq):
        _ssd_chunk(xbc_ref.at[q], z_ref.at[q], dt_ref.at[q], cw_ref, cb_ref, dtb_ref, alog_ref, dskip_ref, ng_ref,
                   triu_ref, shift_ref, y_ref.at[q], state_ref.at[q], ucat_ref.at[q], ybuf_ref.at[q])


def _ssd_chunk(xbc_ref, z_ref, dt_ref, cw_ref, cb_ref, dtb_ref, alog_ref, dskip_ref, ng_ref, triu_ref, shift_ref,
               y_ref, state_ref, ucat_ref, ybuf_ref):
    ucat_ref[CONV_HALO:, :] = xbc_ref[...]
    shifted = jnp.dot(shift_ref[...], ucat_ref[...], preferred_element_type=F32)
    ucat_ref[0:CONV_HALO, :] = ucat_ref[CHUNK:CHUNK + CONV_HALO, :]
    acc = cb_ref[...] + cw_ref[0:1, :] * shifted[0:CHUNK]
    for kk in range(1, CONV_K):
        acc = acc + cw_ref[kk:kk + 1, :] * shifted[kk * CHUNK:(kk + 1) * CHUNK]
    act = _silu_tanh(acc)
    xs = act[:, :D_SSM]
    gn = SSM_GROUPS * D_STATE

    dt_t = _softplus(dt_ref[...].T[0:SSM_HEADS, :] + dtb_ref[...])
    a_t = dt_t * (-jnp.exp(alog_ref[...]))
    a_hi = a_t.astype(BF16)
    a_mid = (a_t - a_hi.astype(F32)).astype(BF16)
    a_lo = (a_t - a_hi.astype(F32) - a_mid.astype(F32)).astype(BF16)
    triu = triu_ref[...]
    cs_t = (jnp.dot(a_hi, triu, preferred_element_type=F32) + jnp.dot(a_mid, triu, preferred_element_type=F32)
            + jnp.dot(a_lo, triu, preferred_element_type=F32))
    cs_end = cs_t[:, CHUNK - 1:CHUNK]
    r_t = cs_t - jnp.log(dt_t)
    w_t = jnp.exp(cs_end - cs_t) * dt_t
    chunk_decay = jnp.exp(cs_end)
    cols = jnp.concatenate([cs_t, jnp.exp(cs_t), jnp.zeros((LANES - 2 * SSM_HEADS, CHUNK), F32)], axis=0).T

    li = lax.broadcasted_iota(jnp.int32, (CHUNK, CHUNK), 0)
    si = lax.broadcasted_iota(jnp.int32, (CHUNK, CHUNK), 1)
    causal = li >= si
    low = _lane_half_mask((CHUNK, LANES))
    low_row = _lane_half_mask((1, LANES))

    heads_per_group = SSM_HEADS // SSM_GROUPS
    for g in range(SSM_GROUPS):
        b_g = act[:, D_SSM + g * D_STATE:D_SSM + (g + 1) * D_STATE]
        c_g = act[:, D_SSM + gn + g * D_STATE:D_SSM + gn + (g + 1) * D_STATE]
        b_gb = b_g.astype(BF16)
        c_gb = c_g.astype(BF16)
        cb = lax.dot_general(c_gb, b_gb, (((1,), (1,)), ((), ())), preferred_element_type=F32)
        b_t = b_g.T
        for jp in range(heads_per_group // 2):
            j = g * (heads_per_group // 2) + jp
            lanes = slice(j * LANES, (j + 1) * LANES)
            xp = xs[:, lanes]
            xpb = xp.astype(BF16)
            ydiag = jnp.zeros((CHUNK, LANES), F32)
            snew = jnp.zeros((D_STATE, LANES), F32)
            for half in range(2):
                h = 2 * j + half
                diff = cols[:, h:h + 1] - r_t[h:h + 1, :]
                m = (cb * jnp.exp(jnp.where(causal, diff, NEG_INF))).astype(BF16)
                keep = low if half == 0 else jnp.logical_not(low)
                xh = jnp.where(keep, xpb, jnp.zeros_like(xpb))
                ydiag = ydiag + jnp.dot(m, xh, preferred_element_type=F32)
                snew = snew + jnp.dot((b_t * w_t[h:h + 1, :]).astype(BF16), xh, preferred_element_type=F32)
            s_in = state_ref[:, lanes]
            yoff = jnp.dot(c_gb, s_in.astype(BF16), preferred_element_type=F32)
            h0 = 2 * j
            e0 = SSM_HEADS + h0
            escale = jnp.where(low, cols[:, e0:e0 + 1], cols[:, e0 + 1:e0 + 2])
            cdec = jnp.where(low_row, chunk_decay[h0:h0 + 1, :], chunk_decay[h0 + 1:h0 + 2, :])
            ybuf_ref[:, lanes] = ydiag + yoff * escale + xp * dskip_ref[:, lanes]
            state_ref[:, lanes] = s_in * cdec + snew

    yz = ybuf_ref[...] * _silu_tanh(z_ref[...].astype(F32))
    gw = D_SSM // SSM_GROUPS
    for g in range(SSM_GROUPS):
        part = yz[:, g * gw:(g + 1) * gw]
        ms = jnp.mean(part * part, axis=-1, keepdims=True)
        y_ref[:, g * gw:(g + 1) * gw] = (part * lax.rsqrt(ms + EPS)
                                            * ng_ref[:, g * gw:(g + 1) * gw]).astype(BF16)


def _ssd_call(xbc, z, dt, conv_w, conv_b, dtb, alog, dskip, ng, triu, shift):
    bsz, l, _ = xbc.shape
    nc = l // CHUNK
    nseq = SSD_SEQS if bsz % SSD_SEQS == 0 else 1
    chunk = lambda w: pl.BlockSpec((nseq, CHUNK, w), lambda b, c: (b, c, 0))
    full = lambda a: pl.BlockSpec(a.shape, lambda b, c: (0, 0))
    return pl.pallas_call(
        _ssd_kernel,
        grid=(bsz // nseq, nc),
        in_specs=[chunk(CONV_CH), chunk(D_SSM), chunk(LANES), full(conv_w), full(conv_b), full(dtb),
                  full(alog), full(dskip), full(ng), full(triu), full(shift)],
        out_specs=chunk(D_SSM),
        out_shape=jax.ShapeDtypeStruct((bsz, l, D_SSM), BF16),
        scratch_shapes=[pltpu.VMEM((nseq, D_STATE, D_SSM), F32),
                        pltpu.VMEM((nseq, CONV_HALO + CHUNK, CONV_CH), BF16),
                        pltpu.VMEM((nseq, CHUNK, D_SSM), F32)],
        compiler_params=pltpu.CompilerParams(dimension_semantics=("arbitrary", "arbitrary"),
                                             vmem_limit_bytes=VMEM_LIMIT),
    )(xbc, z, dt, conv_w, conv_b, dtb, alog, dskip, ng, triu, shift)


assert WINDOW == ATT_BLOCK


def _rel_bucket_table():
    qi = np.arange(ATT_BLOCK)[:, None]
    c = np.arange(ATT_BLOCK)[None, :]
    dist = np.where(c > qi, qi + ATT_BLOCK - c, qi - c)
    max_exact = REL_BUCKETS // 2
    d = np.maximum(dist, 1).astype(np.float32)
    large = max_exact + (np.log(d / np.float32(max_exact)) / np.float32(math.log(REL_MAX_DIST / max_exact))
                         * np.float32(REL_BUCKETS - max_exact)).astype(np.int32)
    large = np.minimum(large, REL_BUCKETS - 1)
    return np.where(dist < max_exact, dist, large).astype(np.int32)


def _bias_kernel(rb_ref, bucket_ref, o_ref):
    bucket = bucket_ref[...]
    from_prev = (lax.broadcasted_iota(jnp.int32, bucket.shape, 1)
                 > lax.broadcasted_iota(jnp.int32, bucket.shape, 0))
    for h in range(ATT_HEADS):
        acc = jnp.zeros(bucket.shape, F32)
        for b in range(REL_BUCKETS):
            acc = jnp.where(bucket == b, rb_ref[b, h], acc)
        o_ref[1, h] = acc
        o_ref[0, h] = jnp.where(from_prev, NEG_INF, acc)


def _bias_call(rel_bias, bucket):
    return pl.pallas_call(
        _bias_kernel,
        in_specs=[pl.BlockSpec(memory_space=pltpu.SMEM), pl.BlockSpec(memory_space=pltpu.VMEM)],
        out_shape=jax.ShapeDtypeStruct((2, ATT_HEADS) + bucket.shape, F32),
    )(rel_bias, bucket)


ATT_SEQS = 4


def _attn_kernel(sink_ref, q_ref, kp_ref, kc_ref, vp_ref, vc_ref, bias_ref, ng_ref, o_ref, obuf_ref):
    for s in range(q_ref.shape[0]):
        _attn_block(sink_ref, q_ref.at[s], kp_ref.at[s], kc_ref.at[s], vp_ref.at[s], vc_ref.at[s], bias_ref.at[0],
                    ng_ref, o_ref.at[s], obuf_ref.at[s])


def _attn_block(sink_ref, q_ref, kp_ref, kc_ref, vp_ref, vc_ref, bias_ref, ng_ref, o_ref, obuf_ref):
    qi = lax.broadcasted_iota(jnp.int32, (ATT_BLOCK, ATT_BLOCK), 0)
    ci = lax.broadcasted_iota(jnp.int32, (ATT_BLOCK, ATT_BLOCK), 1)
    from_prev = ci > qi
    low = _lane_half_mask((ATT_BLOCK, LANES))

    def band_variants(prev_ref, cur_ref):
        out = []
        for cpair in range(KV_HEADS // 2):
            lanes = slice(cpair * LANES, (cpair + 1) * LANES)
            t = jnp.concatenate([prev_ref[:, lanes], cur_ref[:, lanes]], axis=0).astype(F32)
            out.append((t.astype(BF16), pltpu.roll(t, HALF, 1).astype(BF16)))
        return out

    k_band = band_variants(kp_ref, kc_ref)
    v_band = band_variants(vp_ref, vc_ref)
    nt = (((1,), (1,)), ((), ()))

    for j in range(ATT_HEADS // 2):
        qp = q_ref[:, j * LANES:(j + 1) * LANES]
        out_pair = jnp.zeros((ATT_BLOCK, LANES), F32)
        for half in range(2):
            h = 2 * j + half
            g = h // Q_PER_KV
            swapped = int((g % 2) != half)
            keep = low if half == 0 else jnp.logical_not(low)
            qh = jnp.where(keep, qp, jnp.zeros_like(qp))
            s_band = lax.dot_general(qh, k_band[g // 2][swapped], nt, preferred_element_type=F32)
            s = jnp.where(from_prev, s_band[:, :ATT_BLOCK], s_band[:, ATT_BLOCK:]) + bias_ref[h]
            sink = sink_ref[h]
            m = jnp.maximum(jnp.max(s, axis=-1, keepdims=True), sink)
            p = jnp.exp(s - m)
            denom = jnp.sum(p, axis=-1, keepdims=True) + jnp.exp(sink - m)
            p_band = jnp.concatenate([jnp.where(from_prev, p, 0.0), jnp.where(from_prev, 0.0, p)], axis=1)
            o = jnp.dot(p_band.astype(BF16), v_band[g // 2][swapped], preferred_element_type=F32) / denom
            out_pair = out_pair + jnp.where(keep, o, 0.0)
        obuf_ref[:, j * LANES:(j + 1) * LANES] = out_pair

    att = obuf_ref[...]
    ms = jnp.mean(att * att, axis=-1, keepdims=True)
    o_ref[...] = (att * lax.rsqrt(ms + EPS) * ng_ref[...]).astype(BF16)


def _attn_call(sinks, q, k, v, bias, ng):
    bsz, l, _ = q.shape
    nb = l // ATT_BLOCK
    nseq = ATT_SEQS if bsz % ATT_SEQS == 0 else 1
    cur = lambda w: pl.BlockSpec((nseq, ATT_BLOCK, w), lambda b, i: (b, i, 0))
    prev = lambda w: pl.BlockSpec((nseq, ATT_BLOCK, w), lambda b, i: (b, jnp.maximum(i - 1, 0), 0))
    return pl.pallas_call(
        _attn_kernel,
        grid=(bsz // nseq, nb),
        in_specs=[pl.BlockSpec(memory_space=pltpu.SMEM),
                  cur(D_ATT), prev(D_KV), cur(D_KV), prev(D_KV), cur(D_KV),
                  pl.BlockSpec((1,) + bias.shape[1:], lambda b, i: (jnp.minimum(i, 1), 0, 0, 0)),
                  pl.BlockSpec(ng.shape, lambda b, i: (0, 0))],
        out_specs=cur(D_ATT),
        out_shape=jax.ShapeDtypeStruct((bsz, l, D_ATT), BF16),
        scratch_shapes=[pltpu.VMEM((nseq, ATT_BLOCK, D_ATT), F32)],
        compiler_params=pltpu.CompilerParams(dimension_semantics=("arbitrary", "arbitrary"),
                                             vmem_limit_bytes=VMEM_LIMIT),
    )(sinks, q, k, k, v, v, bias, ng)


def _out_proj_kernel(x_ref, ys_ref, ya_ref, g1_ref, sc_ref, sh_ref, g2_ref, ng_ref, wo_hbm, sg_hbm, su_hbm, sd_hbm,
                     rwh_ref, rwl_ref, rb_ref, upper_ref,
                     base_ref, h_ref, idx_ref, gate_ref, rank_ref, cnt_ref,
                     carry_ref, wo_raw, sg_raw, su_raw, sd_raw, wo_ref, sg_ref, su_ref, sd_ref, wsem):
    @pl.when(pl.program_id(0) == 0)
    def _():
        carry_ref[...] = jnp.zeros_like(carry_ref)
        staged = ((wo_hbm, wo_raw, wo_ref), (sg_hbm, sg_raw, sg_ref), (su_hbm, su_raw, su_ref),
                  (sd_hbm, sd_raw, sd_ref))
        copies = [pltpu.make_async_copy(src.at[0], raw, wsem.at[j]) for j, (src, raw, _) in enumerate(staged)]
        for cp in copies:
            cp.start()
        for cp, (_, raw, dst) in zip(copies, staged):
            cp.wait()
            dst[...] = raw[...].astype(BF16)

    mix = (jnp.dot(ys_ref[...], wo_ref[:D_SSM, :], preferred_element_type=F32)
           + jnp.dot(ya_ref[...], wo_ref[D_SSM:, :], preferred_element_type=F32))
    x1 = x_ref[...] + g1_ref[0] * mix
    ms = jnp.mean(x1 * x1, axis=-1, keepdims=True)
    h = x1 * lax.rsqrt(ms + EPS) * ng_ref[...]
    h = h * (1.0 + sc_ref[0]) + sh_ref[0]
    half = h.shape[1] // 2
    h_ref[...] = _pack_bf16_pair(h[:, :half], h[:, half:])
    hi, lo = _split_hi_lo(h)
    hi_terms = jnp.dot(hi, rwl_ref[...], preferred_element_type=F32)
    logits = (hi_terms[:, :LANES] + hi_terms[:, LANES:]
              + jnp.dot(lo, rwh_ref[...], preferred_element_type=F32))
    _route_tokens(logits.T[0:N_EXPERTS, :], rb_ref, upper_ref, idx_ref, gate_ref, rank_ref, cnt_ref, carry_ref)
    u = _silu(jnp.dot(hi, sg_ref[...], preferred_element_type=F32)) * jnp.dot(hi, su_ref[...],
                                                                              preferred_element_type=F32)
    shared = jnp.dot(u.astype(BF16), sd_ref[...], preferred_element_type=F32)
    base_ref[...] = x1 + g2_ref[0] * shared


def _out_proj_call(x2, ys, ya, g1, sc2, sh2, g2, ng, w_out, sg, su, sd, rwh, rwl, router_bias, upper,
                   tiles_per_batch, tm):
    n, d = x2.shape
    row = lambda w: pl.BlockSpec((tm, w), lambda i: (i, 0))
    tok = lambda r: pl.BlockSpec((r, tm), lambda i: (0, i))
    full = lambda a: pl.BlockSpec(a.shape, lambda i: (0, 0))
    per_batch = pl.BlockSpec((1, 1, d), lambda i: (i // tiles_per_batch, 0, 0))
    hbm = pl.BlockSpec(memory_space=pl.ANY)
    staged = (w_out, sg, su, sd)
    return pl.pallas_call(
        _out_proj_kernel,
        grid=(n // tm,),
        in_specs=[row(d), row(D_SSM), row(D_ATT), per_batch, per_batch, per_batch, per_batch, full(ng),
                  hbm, hbm, hbm, hbm, full(rwh), full(rwl), full(router_bias), full(upper)],
        out_specs=[row(d), row(d // 2), tok(TOP_K), tok(TOP_K), tok(TOP_K),
                   pl.BlockSpec((N_EXPERTS, LANES), lambda i: (0, 0))],
        out_shape=[jax.ShapeDtypeStruct((n, d), F32), jax.ShapeDtypeStruct((n, d // 2), jnp.uint32),
                   jax.ShapeDtypeStruct((TOP_K, n), jnp.int32), jax.ShapeDtypeStruct((TOP_K, n), F32),
                   jax.ShapeDtypeStruct((TOP_K, n), jnp.int32), jax.ShapeDtypeStruct((N_EXPERTS, LANES), F32)],
        scratch_shapes=([pltpu.VMEM((N_EXPERTS, LANES), F32)]
                        + [pltpu.VMEM(a.shape[1:], a.dtype) for a in staged]
                        + [pltpu.VMEM(a.shape[1:], BF16) for a in staged]
                        + [pltpu.SemaphoreType.DMA((len(staged),))]),
        compiler_params=pltpu.CompilerParams(dimension_semantics=("arbitrary",),
                                             vmem_limit_bytes=VMEM_LIMIT),
    )(x2, ys, ya, g1, sc2, sh2, g2, ng, w_out, sg, su, sd, rwh, rwl, router_bias, upper)


def _route_tokens(logits_t, rb_ref, upper_ref, idx_ref, gate_ref, rank_ref, cnt_ref, carry_ref):
    t = logits_t.shape[1]
    per_group = N_EXPERTS // ROUTE_GROUPS
    scores = 1.0 / (1.0 + jnp.exp(-logits_t))
    sel = scores + rb_ref[...]
    e_iota = lax.broadcasted_iota(jnp.int32, (N_EXPERTS, t), 0)

    sel3 = sel.reshape(ROUTE_GROUPS, per_group, t)
    w_iota = lax.broadcasted_iota(jnp.int32, sel3.shape, 1)
    m1 = jnp.max(sel3, axis=1, keepdims=True)
    first = jnp.min(jnp.where(sel3 == m1, w_iota, per_group), axis=1, keepdims=True)
    m2 = jnp.max(jnp.where(w_iota == first, NEG_INF, sel3), axis=1, keepdims=True)
    grp = (m1 + m2).reshape(ROUTE_GROUPS, t)

    g_iota = lax.broadcasted_iota(jnp.int32, (ROUTE_GROUPS, t), 0)
    gmask = jnp.zeros((ROUTE_GROUPS, t), jnp.bool_)
    for _ in range(ROUTE_TOPK_GROUPS):
        gm = jnp.max(grp, axis=0, keepdims=True)
        gfirst = jnp.min(jnp.where(grp == gm, g_iota, ROUTE_GROUPS), axis=0, keepdims=True)
        hit = g_iota == gfirst
        gmask = jnp.logical_or(gmask, hit)
        grp = jnp.where(hit, NEG_INF, grp)
    allowed = jnp.broadcast_to(gmask.reshape(ROUTE_GROUPS, 1, t),
                               (ROUTE_GROUPS, per_group, t)).reshape(N_EXPERTS, t)
    masked = jnp.where(allowed, sel, NEG_INF)

    picked = jnp.zeros((N_EXPERTS, t), jnp.bool_)
    idx_rows = []
    w_rows = []
    for _ in range(TOP_K):
        mm = jnp.max(masked, axis=0, keepdims=True)
        efirst = jnp.min(jnp.where(masked == mm, e_iota, N_EXPERTS), axis=0, keepdims=True)
        hit = e_iota == efirst
        idx_rows.append(efirst)
        w_rows.append(jnp.sum(jnp.where(hit, scores, 0.0), axis=0, keepdims=True))
        picked = jnp.logical_or(picked, hit)
        masked = jnp.where(hit, NEG_INF, masked)
    idx = jnp.concatenate(idx_rows, axis=0)
    w = jnp.concatenate(w_rows, axis=0)
    gate_ref[...] = w / jnp.sum(w, axis=0, keepdims=True) * ROUTED_SCALE
    idx_ref[...] = idx

    onehot = jnp.where(picked, 1.0, 0.0)
    sub = upper_ref.shape[0]
    carry = carry_ref[:, 0:1]
    parts = []
    for s0 in range(0, t, sub):
        oh = onehot[:, s0:s0 + sub]
        parts.append(jnp.dot(oh.astype(BF16), upper_ref[...], preferred_element_type=F32) + carry)
        carry = carry + jnp.sum(oh, axis=1, keepdims=True)
    rank_full = jnp.concatenate(parts, axis=1)
    rank_rows = [jnp.sum(jnp.where(e_iota == idx_rows[k], rank_full, 0.0), axis=0, keepdims=True)
                 for k in range(TOP_K)]
    rank_ref[...] = jnp.concatenate(rank_rows, axis=0).astype(jnp.int32)
    carry_ref[...] = jnp.broadcast_to(carry, carry_ref.shape)
    cnt_ref[...] = carry_ref[...]


PLAN_EXPERT, PLAN_FRESH, PLAN_VALID, PLAN_SEG, PLAN_NEXT, PLAN_NUSED = range(6)
PLAN_ROWS = SUBLANES


def _plan_kernel(cnt_ref, tri_ref, start_ref, plan_ref, *, nblocks):
    nbp = plan_ref.shape[1]
    cnt = cnt_ref[...].astype(jnp.int32)
    blocks = (cnt + (EXPERT_ROWS - 1)) // EXPERT_ROWS
    end = jnp.dot(tri_ref[...], blocks.astype(F32), precision=lax.Precision.HIGHEST,
                  preferred_element_type=F32).astype(jnp.int32)
    start = end - blocks
    start_ref[...] = start[:, 0:1] * EXPERT_ROWS
    nused = end[N_EXPERTS - 1:N_EXPERTS, 0:1]

    e_iota = lax.broadcasted_iota(jnp.int32, (N_EXPERTS, nbp), 0)
    blk = lax.broadcasted_iota(jnp.int32, (1, nbp), 1)
    expert = jnp.minimum(jnp.sum((end[:, 0:1] <= blk).astype(jnp.int32), axis=0, keepdims=True), N_EXPERTS - 1)
    mine = e_iota == expert
    pick = lambda col: jnp.sum(jnp.where(mine, col, 0), axis=0, keepdims=True)
    first = pick(start[:, 0:1])
    valid = jnp.clip(pick(cnt[:, 0:1]) - (blk - first) * EXPERT_ROWS, 0, EXPERT_ROWS)
    present = jnp.logical_or(blocks[:, 0:1] > 0,
                             jnp.logical_and(e_iota[:, 0:1] == N_EXPERTS - 1, nused < nblocks))
    seg = jnp.sum(jnp.logical_and(present, e_iota <= expert).astype(jnp.int32), axis=0, keepdims=True) - 1
    nxt = jnp.min(jnp.where(jnp.logical_and(present, e_iota > expert), e_iota, N_EXPERTS), axis=0, keepdims=True)
    rows = {PLAN_EXPERT: expert, PLAN_FRESH: (blk == first).astype(jnp.int32), PLAN_VALID: valid, PLAN_SEG: seg,
            PLAN_NEXT: jnp.where(nxt == N_EXPERTS, -1, nxt), PLAN_NUSED: jnp.broadcast_to(nused, (1, nbp))}
    zero = jnp.zeros((1, nbp), jnp.int32)
    plan_ref[...] = jnp.concatenate([rows.get(r, zero) for r in range(PLAN_ROWS)], axis=0)


def _plan_call(counts, nblocks):
    nbp = -(-nblocks // LANES) * LANES
    tri = jnp.asarray(np.tril(np.ones((N_EXPERTS, N_EXPERTS), np.float32)))
    return pl.pallas_call(
        functools.partial(_plan_kernel, nblocks=nblocks),
        out_shape=[jax.ShapeDtypeStruct((N_EXPERTS, 1), jnp.int32),
                   jax.ShapeDtypeStruct((PLAN_ROWS, nbp), jnp.int32)],
    )(counts, tri)


def _dest_kernel(idx_ref, rank_ref, start_ref, dest_ref):
    t = idx_ref.shape[1]
    e_iota = lax.broadcasted_iota(jnp.int32, (N_EXPERTS, t), 0)
    rows = [jnp.sum(jnp.where(e_iota == idx_ref[k:k + 1, :], start_ref[...], 0), axis=0, keepdims=True)
            for k in range(TOP_K)]
    dest = jnp.concatenate(rows, axis=0) + rank_ref[...]
    for k in range(TOP_K):
        for c in range(t // LANES):
            dest_ref[k, c:c + 1, :] = dest[k:k + 1, c * LANES:(c + 1) * LANES]


def _dest_call(idx, rank, pad_start, tile):
    n = idx.shape[1]
    tok = pl.BlockSpec((TOP_K, tile), lambda i: (0, i))
    return pl.pallas_call(
        _dest_kernel,
        grid=(n // tile,),
        in_specs=[tok, tok, pl.BlockSpec((N_EXPERTS, 1), lambda i: (0, 0))],
        out_specs=pl.BlockSpec((TOP_K, tile // LANES, LANES), lambda i: (0, i, 0)),
        out_shape=jax.ShapeDtypeStruct((TOP_K, n // LANES, LANES), jnp.int32),
        compiler_params=pltpu.CompilerParams(dimension_semantics=("arbitrary",)),
    )(idx, rank, pad_start)


def _scatter_rows_sc(rows, dest_flat, total_rows, chunk):
    n, w = rows.shape
    copies = dest_flat.shape[0] // n
    info = plsc.get_sparse_core_info()
    nc = info.num_cores
    per_worker = n // (nc * info.num_subcores)
    assert per_worker * nc * info.num_subcores == n and per_worker % chunk == 0
    mesh = plsc.VectorSubcoreMesh(core_axis_name="c", subcore_axis_name="s")

    @functools.partial(
        pl.kernel, mesh=mesh,
        out_type=jax.ShapeDtypeStruct((total_rows, w), rows.dtype),
        scratch_types=([pltpu.VMEM((chunk,), jnp.int32) for _ in range(copies)]
                       + [pltpu.VMEM((chunk, w), rows.dtype), pltpu.SemaphoreType.DMA, pltpu.SemaphoreType.DMA]),
    )
    def scatter(rows_hbm, idx_hbm, out_hbm, *scratch):
        idx_vs, (rows_v, isem, sem) = scratch[:copies], scratch[copies:]
        worker = lax.axis_index("s") * nc + lax.axis_index("c")

        @pl.loop(0, per_worker // chunk)
        def _(j):
            base = worker * per_worker + j * chunk
            loads = [pltpu.async_copy(idx_hbm.at[pl.ds(k * n + base, chunk)], idx_vs[k], isem)
                     for k in range(copies)]
            pltpu.sync_copy(rows_hbm.at[pl.ds(base, chunk)], rows_v)
            for ld in loads:
                ld.wait()
            stores = [pltpu.async_copy(rows_v, out_hbm.at[idx_vs[k]], sem) for k in range(copies)]
            for st in stores:
                st.wait()

    return scatter(rows, dest_flat)


X_RING = 3


def _expert_kernel(plan_ref, xs_hbm, wg_hbm, wu_hbm, wd_hbm, y_ref,
                   wgb_ref, wub_ref, wdb_ref, xbuf_ref, xsem, wg_raw, wu_raw, wd_raw, wsem):
    i = pl.program_id(0)
    rows, half = xbuf_ref.shape[1], xbuf_ref.shape[2]
    nused = plan_ref[PLAN_NUSED, 0]

    def w_copies(expert, slot):
        return [pltpu.make_async_copy(src.at[expert], dst.at[slot], wsem.at[slot])
                for src, dst in ((wg_hbm, wg_raw), (wu_hbm, wu_raw), (wd_hbm, wd_raw))]

    def x_copy(block):
        first = pl.multiple_of(block * rows, rows)
        slot = block % X_RING
        return pltpu.make_async_copy(xs_hbm.at[pl.ds(first, rows)], xbuf_ref.at[slot], xsem.at[slot])

    @pl.when(i == 0)
    def _():
        for b in range(X_RING - 1):
            @pl.when(b < nused)
            def _():
                x_copy(b).start()

    @pl.when(i + (X_RING - 1) < nused)
    def _():
        x_copy(i + (X_RING - 1)).start()

    @pl.when(i == 0)
    def _():
        for c in w_copies(plan_ref[PLAN_EXPERT, 0], 0):
            c.start()

    @pl.when(plan_ref[PLAN_FRESH, i] > 0)
    def _():
        slot = plan_ref[PLAN_SEG, i] % 2
        for c in w_copies(plan_ref[PLAN_EXPERT, i], slot):
            c.wait()

        @pl.when(plan_ref[PLAN_NEXT, i] >= 0)
        def _():
            for c in w_copies(plan_ref[PLAN_NEXT, i], 1 - slot):
                c.start()

        wgb_ref[...] = wg_raw[slot].astype(BF16)
        wub_ref[...] = wu_raw[slot].astype(BF16)
        wdb_ref[...] = wd_raw[slot].astype(BF16)

    @pl.when(i < nused)
    def _():
        x_copy(i).wait()
        xw = xbuf_ref[i % X_RING]
        row = lax.broadcasted_iota(jnp.int32, xw.shape, 0)
        x_lo, x_hi = _unpack_bf16_pair(jnp.where(row < plan_ref[PLAN_VALID, i], xw, jnp.uint32(0)))
        x_lo = x_lo.astype(BF16)
        x_hi = x_hi.astype(BF16)
        gate = (jnp.dot(x_lo, wgb_ref[:half, :], preferred_element_type=F32)
                + jnp.dot(x_hi, wgb_ref[half:, :], preferred_element_type=F32))
        up = (jnp.dot(x_lo, wub_ref[:half, :], preferred_element_type=F32)
              + jnp.dot(x_hi, wub_ref[half:, :], preferred_element_type=F32))
        u = (_silu(gate) * up).astype(BF16)
        y_lo = jnp.dot(u, wdb_ref[:, :half], preferred_element_type=F32)
        y_hi = jnp.dot(u, wdb_ref[:, half:], preferred_element_type=F32)
        y_ref[...] = _pack_bf16_pair(y_lo, y_hi)

    @pl.when(i >= nused)
    def _():
        y_ref[...] = jnp.zeros_like(y_ref)


def _expert_call(plan, xs, wg, wu, wd, rows):
    p, w = xs.shape
    d, f = wg.shape[1], wg.shape[2]
    hbm = pl.BlockSpec(memory_space=pl.ANY)
    grid_spec = pltpu.PrefetchScalarGridSpec(
        num_scalar_prefetch=1,
        grid=(p // rows,),
        in_specs=[hbm, hbm, hbm, hbm],
        out_specs=pl.BlockSpec((rows, w), lambda i, *_: (i, 0)),
        scratch_shapes=[pltpu.VMEM((d, f), BF16), pltpu.VMEM((d, f), BF16), pltpu.VMEM((f, d), BF16),
                        pltpu.VMEM((X_RING, rows, w), xs.dtype), pltpu.SemaphoreType.DMA((X_RING,)),
                        pltpu.VMEM((2, d, f), wg.dtype), pltpu.VMEM((2, d, f), wu.dtype),
                        pltpu.VMEM((2, f, d), wd.dtype), pltpu.SemaphoreType.DMA((2,))],
    )
    return pl.pallas_call(
        _expert_kernel,
        grid_spec=grid_spec,
        out_shape=jax.ShapeDtypeStruct((p, w), jnp.uint32),
        compiler_params=pltpu.CompilerParams(dimension_semantics=("arbitrary",),
                                             vmem_limit_bytes=VMEM_LIMIT),
    )(plan, xs, wg, wu, wd)


def _gather_rows_sc(table, idx, chunk):
    m = idx.shape[0]
    w = table.shape[1]
    info = plsc.get_sparse_core_info()
    nc = info.num_cores
    per_worker = m // (nc * info.num_subcores)
    assert per_worker * nc * info.num_subcores == m and per_worker % chunk == 0
    mesh = plsc.VectorSubcoreMesh(core_axis_name="c", subcore_axis_name="s")

    @functools.partial(
        pl.kernel, mesh=mesh,
        out_type=jax.ShapeDtypeStruct((m, w), table.dtype),
        scratch_types=[pltpu.VMEM((chunk,), jnp.int32), pltpu.VMEM((chunk, w), table.dtype),
                       pltpu.SemaphoreType.DMA],
    )
    def gather(table_hbm, idx_hbm, out_hbm, idx_v, rows_v, sem):
        worker = lax.axis_index("s") * nc + lax.axis_index("c")

        @pl.loop(0, per_worker // chunk)
        def _(j):
            base = worker * per_worker + j * chunk
            pltpu.sync_copy(idx_hbm.at[pl.ds(base, chunk)], idx_v)
            pltpu.async_copy(table_hbm.at[idx_v], rows_v, sem).wait()
            pltpu.sync_copy(rows_v, out_hbm.at[pl.ds(base, chunk)])

    return gather(table, idx)


def _combine_kernel(yk_ref, gate_ref, base_ref, g2_ref, fg_ref, o_ref):
    t = base_ref.shape[0]
    half = yk_ref.shape[2]
    gates = jnp.concatenate([gate_ref[...], jnp.zeros((LANES - TOP_K, t), F32)], axis=0).T
    r_lo = jnp.zeros((t, half), F32)
    r_hi = jnp.zeros((t, half), F32)
    for k in range(TOP_K):
        y_lo, y_hi = _unpack_bf16_pair(yk_ref[k])
        r_lo = r_lo + gates[:, k:k + 1] * y_lo
        r_hi = r_hi + gates[:, k:k + 1] * y_hi
    g2 = g2_ref[0]
    x_lo = base_ref[:, :half] + g2[:, :half] * r_lo
    x_hi = base_ref[:, half:] + g2[:, half:] * r_hi
    ms = (jnp.sum(x_lo * x_lo, axis=-1, keepdims=True)
          + jnp.sum(x_hi * x_hi, axis=-1, keepdims=True)) * (1.0 / (2 * half))
    inv = lax.rsqrt(ms + EPS)
    o_ref[:, :half] = x_lo * inv * fg_ref[:, :half]
    o_ref[:, half:] = x_hi * inv * fg_ref[:, half:]


def _combine_call(yk, gates, base, g2, fg, tiles_per_batch, tile, tile0):
    n, d = base.shape
    row = lambda w: pl.BlockSpec((tile, w), lambda i: (i + tile0, 0))
    return pl.pallas_call(
        _combine_kernel,
        grid=(yk.shape[1] // tile,),
        in_specs=[pl.BlockSpec((TOP_K, tile, yk.shape[2]), lambda i: (0, i, 0)),
                  pl.BlockSpec((TOP_K, tile), lambda i: (0, i + tile0)), row(d),
                  pl.BlockSpec((1, 1, d), lambda i: ((i + tile0) // tiles_per_batch, 0, 0)),
                  pl.BlockSpec((1, d), lambda i: (0, 0))],
        out_specs=row(d),
        out_shape=jax.ShapeDtypeStruct((n, d), F32),
        input_output_aliases={2: 0},
        compiler_params=pltpu.CompilerParams(dimension_semantics=("arbitrary",),
                                             vmem_limit_bytes=VMEM_LIMIT),
    )(yk, gates, base, g2, fg)


def _pad_cols(a, width):
    return jnp.pad(a, ((0, 0), (0, width - a.shape[1])))


def _layer(x, mod, norm1_g, norm2_g, w_in, conv_w, conv_b, dt_bias, a_log, d_skip, ssm_norm_g,
           att_norm_g, sinks, rel_bias, w_out, router_w, router_bias, exp_w_gate, exp_w_up, exp_w_down,
           sh_w_gate, sh_w_up, sh_w_down, final_g):
    bsz, l, d = x.shape
    n = bsz * l
    tm = min(ROW_TILE, l)

    sh1, sc1, g1, sh2, sc2, g2 = [m[:, None, :] for m in jnp.split(mod, 6, axis=-1)]

    assert math.frexp(ATT_HEAD_DIM ** -0.5)[0] == 0.5
    x2 = x.reshape(n, d)
    z, xbc, dt, q, k, v = _in_proj_call(x2, sc1, sh1, norm1_g[None, :], jnp.swapaxes(w_in, 1, 2), l // tm, tm)

    triu = jnp.asarray(np.triu(np.ones((CHUNK, CHUNK), np.float32))).astype(BF16)
    shift = jnp.asarray(_conv_shift_matrix()).astype(BF16)
    y_ssm = _ssd_call(xbc.reshape(bsz, l, CONV_CH), z.reshape(bsz, l, D_SSM), dt.reshape(bsz, l, LANES),
                      conv_w, conv_b[None, :], dt_bias[:, None], a_log[:, None],
                      jnp.repeat(d_skip, SSM_HEAD_DIM)[None, :], ssm_norm_g[None, :], triu, shift)

    bias = _bias_call(rel_bias, jnp.asarray(_rel_bucket_table()))
    y_att = _attn_call(sinks, q.reshape(bsz, l, D_ATT), k.reshape(bsz, l, D_KV), v.reshape(bsz, l, D_KV), bias,
                       att_norm_g[None, :])

    rw = _pad_cols(router_w, LANES)
    rwh = rw.astype(BF16)
    rwl = jnp.concatenate([rwh, (rw - rwh.astype(F32)).astype(BF16)], axis=1)
    rs = min(RANK_SUB, tm)
    upper = jnp.asarray(np.triu(np.ones((rs, rs), np.float32), 1)).astype(BF16)
    base, h2, idx, gates, rank, counts = _out_proj_call(
        x2, y_ssm.reshape(n, D_SSM), y_att.reshape(n, D_ATT), g1, sc2, sh2, g2, norm2_g[None, :],
        w_out, sh_w_gate, sh_w_up, sh_w_down, rwh, rwl, router_bias[:, None], upper, l // tm, tm)
    rt = min(ROUTE_TILE, n)

    nblocks = (n * TOP_K + N_EXPERTS * (EXPERT_ROWS - 1) + EXPERT_ROWS - 1) // EXPERT_ROWS
    pad_start, plan = _plan_call(counts, nblocks)
    dest = _dest_call(idx, rank, pad_start, rt)

    xs = _scatter_rows_sc(h2, dest.reshape(-1), nblocks * EXPERT_ROWS, SC_CHUNK)
    ys = _expert_call(plan, xs, exp_w_gate, exp_w_up, exp_w_down, EXPERT_ROWS)
    ctile = min(COMBINE_TILE, l)
    groups = COMBINE_GROUPS if bsz % COMBINE_GROUPS == 0 else 1
    ng = n // groups
    out = base
    for g in range(groups):
        idx_g = dest[:, g * ng // LANES:(g + 1) * ng // LANES, :].reshape(-1)
        yk = _gather_rows_sc(ys, idx_g, SC_CHUNK).reshape(TOP_K, ng, ys.shape[1])
        out = _combine_call(yk, gates, out, g2, final_g[None, :], l // ctile, ctile, g * ng // ctile)
    return out.reshape(bsz, l, d)


def kernel(x, c, mod_w, mod_b, norm1_g, norm2_g, w_in, conv_w, conv_b, dt_bias, a_log, d_skip, ssm_norm_g,
           att_norm_g, sinks, rel_bias, w_out, router_w, router_bias, exp_w_gate, exp_w_up, exp_w_down,
           sh_w_gate, sh_w_up, sh_w_down, final_g):
    assert mod_w.shape[0] == 1, "single-layer block"
    bsz = x.shape[0]
    c_pad = jnp.pad(c, ((0, SUBLANES - bsz % SUBLANES if bsz % SUBLANES else 0), (0, 0)))
    mod = _mod_call(c_pad, mod_w[0], mod_b[0][None, :])[:bsz]
    return _layer(x, mod, norm1_g[0], norm2_g[0], w_in, conv_w[0], conv_b[0], dt_bias[0], a_log[0], d_skip[0],
                  ssm_norm_g[0], att_norm_g[0], sinks[0], rel_bias, w_out, router_w[0], router_bias[0],
                  exp_w_gate[0], exp_w_up[0], exp_w_down[0], sh_w_gate, sh_w_up, sh_w_down, final_g)
```

```python
import functools
import math

import numpy as np
import jax
import jax.numpy as jnp
from jax import lax
from jax.experimental import pallas as pl
from jax.experimental.pallas import tpu as pltpu
from jax.experimental.pallas import tpu_sc as plsc

F32 = jnp.float32
BF16 = jnp.bfloat16

D_MODEL = 1024
SSM_HEAD_DIM = 64
D_SSM = D_MODEL
SSM_HEADS = D_SSM // SSM_HEAD_DIM
SSM_GROUPS = 4
D_STATE = 128
CONV_K = 4
CONV_CH = D_SSM + 2 * SSM_GROUPS * D_STATE
CHUNK = 128
ATT_HEAD_DIM = 64
D_ATT = D_MODEL
ATT_HEADS = D_ATT // ATT_HEAD_DIM
KV_HEADS = ATT_HEADS // 4
Q_PER_KV = ATT_HEADS // KV_HEADS
D_KV = KV_HEADS * ATT_HEAD_DIM
WINDOW = 128
ATT_BLOCK = 128
REL_BUCKETS = 32
REL_MAX_DIST = 128
N_EXPERTS = 64
TOP_K = 8
EXPERT_DIM = D_MODEL // 4
SHARED_DIM = D_MODEL // 4
ROUTE_GROUPS = 8
ROUTE_TOPK_GROUPS = 4
ROUTED_SCALE = 2.5
EPS = 1e-6

LANES = 128
SUBLANES = 8
HALF = LANES // 2

ROW_TILE = 512
ROUTE_TILE = 2048
RANK_SUB = 512
COMBINE_TILE = 256
SC_CHUNK = 128
SC_GATHER_PARTS = 4
EXPERT_ROWS = 512
COMBINE_GROUPS = 2
VMEM_LIMIT = 48 * 1024 * 1024

NEG_INF = float("-inf")


def _silu(v):
    return v * (1.0 / (1.0 + jnp.exp(-v)))


def _softplus(v):
    return jnp.maximum(v, 0.0) + jnp.log(1.0 + jnp.exp(-jnp.abs(v)))


def _bdot(a, b):
    return jnp.dot(a.astype(BF16), b.astype(BF16), preferred_element_type=F32)


def _split_hi_lo(v):
    hi = v.astype(BF16)
    lo = (v - hi.astype(F32)).astype(BF16)
    return hi, lo


def _pack_bf16_pair(a, b):
    w = pltpu.pack_elementwise([a, b], packed_dtype=BF16)
    return w if w.dtype == jnp.uint32 else lax.bitcast_convert_type(w, jnp.uint32)


def _unpack_bf16_pair(w):
    a = pltpu.unpack_elementwise(w, index=0, packed_dtype=BF16, unpacked_dtype=F32)
    b = pltpu.unpack_elementwise(w, index=1, packed_dtype=BF16, unpacked_dtype=F32)
    return a, b


def _lane_half_mask(shape):
    return lax.broadcasted_iota(jnp.int32, shape, len(shape) - 1) < HALF


def _mod_kernel(c_ref, w_ref, b_ref, o_ref):
    a = _silu(c_ref[...])
    o_ref[...] = jnp.dot(a, w_ref[...], precision=lax.Precision.HIGHEST,
                         preferred_element_type=F32) + b_ref[...]


def _mod_call(c_pad, mod_w, mod_b):
    rows, d = c_pad.shape
    cols = mod_w.shape[1]
    return pl.pallas_call(
        _mod_kernel,
        grid=(cols // d,),
        in_specs=[pl.BlockSpec((rows, d), lambda j: (0, 0)),
                  pl.BlockSpec((d, d), lambda j: (0, j)),
                  pl.BlockSpec((1, d), lambda j: (0, j))],
        out_specs=pl.BlockSpec((rows, d), lambda j: (0, j)),
        out_shape=jax.ShapeDtypeStruct((rows, cols), F32),
        compiler_params=pltpu.CompilerParams(dimension_semantics=("arbitrary",),
                                             vmem_limit_bytes=VMEM_LIMIT),
    )(c_pad, mod_w, mod_b)


IN_PROJ_SEGMENTS = ((D_SSM, BF16), (CONV_CH, BF16), (LANES, F32), (D_ATT, BF16), (D_KV, BF16), (D_KV, BF16))


def _in_proj_kernel(x_ref, sc_ref, sh_ref, g_ref, w_hbm, *refs):
    out_refs = refs[:len(IN_PROJ_SEGMENTS)]
    wraw_ref, w_ref, wsem = refs[len(IN_PROJ_SEGMENTS):]

    @pl.when(pl.program_id(0) == 0)
    def _():
        cp = pltpu.make_async_copy(w_hbm.at[0], wraw_ref, wsem)
        cp.start()
        cp.wait()
        src_dt = D_SSM + CONV_CH
        dst_q = src_dt + LANES
        src_q = src_dt + SSM_HEADS
        q_scale = ATT_HEAD_DIM ** -0.5

        def put(dst, src, scale=None):
            t = wraw_ref[src:src + LANES, :].T
            w_ref[:, dst:dst + LANES] = (t if scale is None else t * scale).astype(BF16)

        for c0 in range(0, src_dt, LANES):
            put(c0, c0)
        dt_tile = wraw_ref[src_dt:src_dt + LANES, :].T
        lane = lax.broadcasted_iota(jnp.int32, dt_tile.shape, 1)
        w_ref[:, src_dt:dst_q] = jnp.where(lane < SSM_HEADS, dt_tile, 0.0).astype(BF16)
        for c0 in range(0, D_ATT, LANES):
            put(dst_q + c0, src_q + c0, q_scale)
        for c0 in range(D_ATT, D_ATT + 2 * D_KV, LANES):
            put(dst_q + c0, src_q + c0)

    xf = x_ref[...]
    ms = jnp.mean(xf * xf, axis=-1, keepdims=True)
    h = xf * lax.rsqrt(ms + EPS) * g_ref[...]
    h = h * (1.0 + sc_ref[0]) + sh_ref[0]
    hb = h.astype(BF16)
    col = 0
    for (width, dtype), o_ref in zip(IN_PROJ_SEGMENTS, out_refs):
        o_ref[...] = jnp.dot(hb, w_ref[:, col:col + width], preferred_element_type=F32).astype(dtype)
        col += width


def _in_proj_call(x2, sc1, sh1, g1n, w_in_t, tiles_per_batch, tm):
    n, d = x2.shape
    row = lambda w: pl.BlockSpec((tm, w), lambda i: (i, 0))
    full = lambda a: pl.BlockSpec(a.shape, lambda i: (0, 0))
    per_batch = pl.BlockSpec((1, 1, d), lambda i: (i // tiles_per_batch, 0, 0))
    cols = sum(w for w, _ in IN_PROJ_SEGMENTS)
    w_in = w_in_t
    assert w_in.shape == (1, cols - (LANES - SSM_HEADS), d)
    return pl.pallas_call(
        _in_proj_kernel,
        grid=(n // tm,),
        in_specs=[row(d), per_batch, per_batch, full(g1n), pl.BlockSpec(memory_space=pl.ANY)],
        out_specs=[row(w) for w, _ in IN_PROJ_SEGMENTS],
        out_shape=[jax.ShapeDtypeStruct((n, w), dt) for w, dt in IN_PROJ_SEGMENTS],
        scratch_shapes=[pltpu.VMEM(w_in.shape[1:], w_in.dtype), pltpu.VMEM((d, cols), BF16), pltpu.SemaphoreType.DMA(())],
        compiler_params=pltpu.CompilerParams(dimension_semantics=("arbitrary",),
                                             vmem_limit_bytes=VMEM_LIMIT),
    )(x2, sc1, sh1, g1n, w_in)


SSD_SEQS = 4
CONV_HALO = 16


def _conv_shift_matrix():
    s = np.zeros((CONV_K * CHUNK, CONV_HALO + CHUNK), np.float32)
    for k in range(CONV_K):
        t = np.arange(CHUNK)
        s[k * CHUNK + t, CONV_HALO + t - (CONV_K - 1) + k] = 1.0
    return s


def _silu_tanh(v):
    hv = 0.5 * v
    return hv + hv * jnp.tanh(hv)


def _ssd_kernel(xbc_ref, z_ref, dt_ref, cw_ref, cb_ref, dtb_ref, alog_ref, dskip_ref, ng_ref, triu_ref, shift_ref,
                y_ref, state_ref, ucat_ref, ybuf_ref):
    nseq = xbc_ref.shape[0]

    @pl.when(pl.program_id(1) == 0)
    def _():
        state_ref[...] = jnp.zeros_like(state_ref)
        ucat_ref[:, 0:CONV_HALO, :] = jnp.zeros((nseq, CONV_HALO, CONV_CH), BF16)

    for q in range(nseq):
        _ssd_chunk(xbc_ref.at[q], z_ref.at[q], dt_ref.at[q], cw_ref, cb_ref, dtb_ref, alog_ref, dskip_ref, ng_ref,
                   triu_ref, shift_ref, y_ref.at[q], state_ref.at[q], ucat_ref.at[q], ybuf_ref.at[q])


def _ssd_chunk(xbc_ref, z_ref, dt_ref, cw_ref, cb_ref, dtb_ref, alog_ref, dskip_ref, ng_ref, triu_ref, shift_ref,
               y_ref, state_ref, ucat_ref, ybuf_ref):
    ucat_ref[CONV_HALO:, :] = xbc_ref[...]
    shifted = jnp.dot(shift_ref[...], ucat_ref[...], preferred_element_type=F32)
    ucat_ref[0:CONV_HALO, :] = ucat_ref[CHUNK:CHUNK + CONV_HALO, :]
    acc = cb_ref[...] + cw_ref[0:1, :] * shifted[0:CHUNK]
    for kk in range(1, CONV_K):
        acc = acc + cw_ref[kk:kk + 1, :] * shifted[kk * CHUNK:(kk + 1) * CHUNK]
    act = _silu_tanh(acc)
    xs = act[:, :D_SSM]
    gn = SSM_GROUPS * D_STATE

    dt_t = _softplus(dt_ref[...].T[0:SSM_HEADS, :] + dtb_ref[...])
    a_t = dt_t * (-jnp.exp(alog_ref[...]))
    a_hi = a_t.astype(BF16)
    a_mid = (a_t - a_hi.astype(F32)).astype(BF16)
    a_lo = (a_t - a_hi.astype(F32) - a_mid.astype(F32)).astype(BF16)
    triu = triu_ref[...]
    cs_t = (jnp.dot(a_hi, triu, preferred_element_type=F32) + jnp.dot(a_mid, triu, preferred_element_type=F32)
            + jnp.dot(a_lo, triu, preferred_element_type=F32))
    cs_end = cs_t[:, CHUNK - 1:CHUNK]
    r_t = cs_t - jnp.log(dt_t)
    w_t = jnp.exp(cs_end - cs_t) * dt_t
    chunk_decay = jnp.exp(cs_end)
    cols = jnp.concatenate([cs_t, jnp.exp(cs_t), jnp.zeros((LANES - 2 * SSM_HEADS, CHUNK), F32)], axis=0).T

    li = lax.broadcasted_iota(jnp.int32, (CHUNK, CHUNK), 0)
    si = lax.broadcasted_iota(jnp.int32, (CHUNK, CHUNK), 1)
    causal = li >= si
    low = _lane_half_mask((CHUNK, LANES))
    low_row = _lane_half_mask((1, LANES))

    heads_per_group = SSM_HEADS // SSM_GROUPS
    for g in range(SSM_GROUPS):
        b_g = act[:, D_SSM + g * D_STATE:D_SSM + (g + 1) * D_STATE]
        c_g = act[:, D_SSM + gn + g * D_STATE:D_SSM + gn + (g + 1) * D_STATE]
        b_gb = b_g.astype(BF16)
        c_gb = c_g.astype(BF16)
        cb = lax.dot_general(c_gb, b_gb, (((1,), (1,)), ((), ())), preferred_element_type=F32)
        b_t = b_g.T
        for jp in range(heads_per_group // 2):
            j = g * (heads_per_group // 2) + jp
            lanes = slice(j * LANES, (j + 1) * LANES)
            xp = xs[:, lanes]
            xpb = xp.astype(BF16)
            ydiag = jnp.zeros((CHUNK, LANES), F32)
            snew = jnp.zeros((D_STATE, LANES), F32)
            for half in range(2):
                h = 2 * j + half
                diff = cols[:, h:h + 1] - r_t[h:h + 1, :]
                m = (cb * jnp.exp(jnp.where(causal, diff, NEG_INF))).astype(BF16)
                keep = low if half == 0 else jnp.logical_not(low)
                xh = jnp.where(keep, xpb, jnp.zeros_like(xpb))
                ydiag = ydiag + jnp.dot(m, xh, preferred_element_type=F32)
                snew = snew + jnp.dot((b_t * w_t[h:h + 1, :]).astype(BF16), xh, preferred_element_type=F32)
            s_in = state_ref[:, lanes]
            yoff = jnp.dot(c_gb, s_in.astype(BF16), preferred_element_type=F32)
            h0 = 2 * j
            e0 = SSM_HEADS + h0
            escale = jnp.where(low, cols[:, e0:e0 + 1], cols[:, e0 + 1:e0 + 2])
            cdec = jnp.where(low_row, chunk_decay[h0:h0 + 1, :], chunk_decay[h0 + 1:h0 + 2, :])
            ybuf_ref[:, lanes] = ydiag + yoff * escale + xp * dskip_ref[:, lanes]
            state_ref[:, lanes] = s_in * cdec + snew

    yz = ybuf_ref[...] * _silu_tanh(z_ref[...].astype(F32))
    gw = D_SSM // SSM_GROUPS
    for g in range(SSM_GROUPS):
        part = yz[:, g * gw:(g + 1) * gw]
        ms = jnp.mean(part * part, axis=-1, keepdims=True)
        y_ref[:, g * gw:(g + 1) * gw] = (part * lax.rsqrt(ms + EPS)
                                            * ng_ref[:, g * gw:(g + 1) * gw]).astype(BF16)


def _ssd_call(xbc, z, dt, conv_w, conv_b, dtb, alog, dskip, ng, triu, shift):
    bsz, l, _ = xbc.shape
    nc = l // CHUNK
    nseq = SSD_SEQS if bsz % SSD_SEQS == 0 else 1
    chunk = lambda w: pl.BlockSpec((nseq, CHUNK, w), lambda b, c: (b, c, 0))
    full = lambda a: pl.BlockSpec(a.shape, lambda b, c: (0, 0))
    return pl.pallas_call(
        _ssd_kernel,
        grid=(bsz // nseq, nc),
        in_specs=[chunk(CONV_CH), chunk(D_SSM), chunk(LANES), full(conv_w), full(conv_b), full(dtb),
                  full(alog), full(dskip), full(ng), full(triu), full(shift)],
        out_specs=chunk(D_SSM),
        out_shape=jax.ShapeDtypeStruct((bsz, l, D_SSM), BF16),
        scratch_shapes=[pltpu.VMEM((nseq, D_STATE, D_SSM), F32),
                        pltpu.VMEM((nseq, CONV_HALO + CHUNK, CONV_CH), BF16),
                        pltpu.VMEM((nseq, CHUNK, D_SSM), F32)],
        compiler_params=pltpu.CompilerParams(dimension_semantics=("arbitrary", "arbitrary"),
                                             vmem_limit_bytes=VMEM_LIMIT),
    )(xbc, z, dt, conv_w, conv_b, dtb, alog, dskip, ng, triu, shift)


assert WINDOW == ATT_BLOCK


def _rel_bucket_table():
    qi = np.arange(ATT_BLOCK)[:, None]
    c = np.arange(ATT_BLOCK)[None, :]
    dist = np.where(c > qi, qi + ATT_BLOCK - c, qi - c)
    max_exact = REL_BUCKETS // 2
    d = np.maximum(dist, 1).astype(np.float32)
    large = max_exact + (np.log(d / np.float32(max_exact)) / np.float32(math.log(REL_MAX_DIST / max_exact))
                         * np.float32(REL_BUCKETS - max_exact)).astype(np.int32)
    large = np.minimum(large, REL_BUCKETS - 1)
    return np.where(dist < max_exact, dist, large).astype(np.int32)


def _bias_kernel(rb_ref, bucket_ref, o_ref):
    bucket = bucket_ref[...]
    from_prev = (lax.broadcasted_iota(jnp.int32, bucket.shape, 1)
                 > lax.broadcasted_iota(jnp.int32, bucket.shape, 0))
    for h in range(ATT_HEADS):
        acc = jnp.zeros(bucket.shape, F32)
        for b in range(REL_BUCKETS):
            acc = jnp.where(bucket == b, rb_ref[b, h], acc)
        o_ref[1, h] = acc
        o_ref[0, h] = jnp.where(from_prev, NEG_INF, acc)


def _bias_call(rel_bias, bucket):
    return pl.pallas_call(
        _bias_kernel,
        in_specs=[pl.BlockSpec(memory_space=pltpu.SMEM), pl.BlockSpec(memory_space=pltpu.VMEM)],
        out_shape=jax.ShapeDtypeStruct((2, ATT_HEADS) + bucket.shape, F32),
    )(rel_bias, bucket)


ATT_SEQS = 4


def _attn_kernel(sink_ref, q_ref, kp_ref, kc_ref, vp_ref, vc_ref, bias_ref, ng_ref, o_ref, obuf_ref):
    for s in range(q_ref.shape[0]):
        _attn_block(sink_ref, q_ref.at[s], kp_ref.at[s], kc_ref.at[s], vp_ref.at[s], vc_ref.at[s], bias_ref.at[0],
                    ng_ref, o_ref.at[s], obuf_ref.at[s])


def _attn_block(sink_ref, q_ref, kp_ref, kc_ref, vp_ref, vc_ref, bias_ref, ng_ref, o_ref, obuf_ref):
    qi = lax.broadcasted_iota(jnp.int32, (ATT_BLOCK, ATT_BLOCK), 0)
    ci = lax.broadcasted_iota(jnp.int32, (ATT_BLOCK, ATT_BLOCK), 1)
    from_prev = ci > qi
    low = _lane_half_mask((ATT_BLOCK, LANES))

    def band_variants(prev_ref, cur_ref):
        out = []
        for cpair in range(KV_HEADS // 2):
            lanes = slice(cpair * LANES, (cpair + 1) * LANES)
            t = jnp.concatenate([prev_ref[:, lanes], cur_ref[:, lanes]], axis=0).astype(F32)
            out.append((t.astype(BF16), pltpu.roll(t, HALF, 1).astype(BF16)))
        return out

    k_band = band_variants(kp_ref, kc_ref)
    v_band = band_variants(vp_ref, vc_ref)
    nt = (((1,), (1,)), ((), ()))

    for j in range(ATT_HEADS // 2):
        qp = q_ref[:, j * LANES:(j + 1) * LANES]
        out_pair = jnp.zeros((ATT_BLOCK, LANES), F32)
        for half in range(2):
            h = 2 * j + half
            g = h // Q_PER_KV
            swapped = int((g % 2) != half)
            keep = low if half == 0 else jnp.logical_not(low)
            qh = jnp.where(keep, qp, jnp.zeros_like(qp))
            s_band = lax.dot_general(qh, k_band[g // 2][swapped], nt, preferred_element_type=F32)
            s = jnp.where(from_prev, s_band[:, :ATT_BLOCK], s_band[:, ATT_BLOCK:]) + bias_ref[h]
            sink = sink_ref[h]
            m = jnp.maximum(jnp.max(s, axis=-1, keepdims=True), sink)
            p = jnp.exp(s - m)
            denom = jnp.sum(p, axis=-1, keepdims=True) + jnp.exp(sink - m)
            p_band = jnp.concatenate([jnp.where(from_prev, p, 0.0), jnp.where(from_prev, 0.0, p)], axis=1)
            o = jnp.dot(p_band.astype(BF16), v_band[g // 2][swapped], preferred_element_type=F32) / denom
            out_pair = out_pair + jnp.where(keep, o, 0.0)
        obuf_ref[:, j * LANES:(j + 1) * LANES] = out_pair

    att = obuf_ref[...]
    ms = jnp.mean(att * att, axis=-1, keepdims=True)
    o_ref[...] = (att * lax.rsqrt(ms + EPS) * ng_ref[...]).astype(BF16)


def _attn_call(sinks, q, k, v, bias, ng):
    bsz, l, _ = q.shape
    nb = l // ATT_BLOCK
    nseq = ATT_SEQS if bsz % ATT_SEQS == 0 else 1
    cur = lambda w: pl.BlockSpec((nseq, ATT_BLOCK, w), lambda b, i: (b, i, 0))
    prev = lambda w: pl.BlockSpec((nseq, ATT_BLOCK, w), lambda b, i: (b, jnp.maximum(i - 1, 0), 0))
    return pl.pallas_call(
        _attn_kernel,
        grid=(bsz // nseq, nb),
        in_specs=[pl.BlockSpec(memory_space=pltpu.SMEM),
                  cur(D_ATT), prev(D_KV), cur(D_KV), prev(D_KV), cur(D_KV),
                  pl.BlockSpec((1,) + bias.shape[1:], lambda b, i: (jnp.minimum(i, 1), 0, 0, 0)),
                  pl.BlockSpec(ng.shape, lambda b, i: (0, 0))],
        out_specs=cur(D_ATT),
        out_shape=jax.ShapeDtypeStruct((bsz, l, D_ATT), BF16),
        scratch_shapes=[pltpu.VMEM((nseq, ATT_BLOCK, D_ATT), F32)],
        compiler_params=pltpu.CompilerParams(dimension_semantics=("arbitrary", "arbitrary"),
                                             vmem_limit_bytes=VMEM_LIMIT),
    )(sinks, q, k, k, v, v, bias, ng)


def _out_proj_kernel(x_ref, ys_ref, ya_ref, g1_ref, sc_ref, sh_ref, g2_ref, ng_ref, wo_hbm, sg_hbm, su_hbm, sd_hbm,
                     rwh_ref, rwl_ref, rb_ref, upper_ref,
                     base_ref, h_ref, idx_ref, gate_ref, rank_ref, cnt_ref,
                     carry_ref, wo_raw, sg_raw, su_raw, sd_raw, wo_ref, sg_ref, su_ref, sd_ref, wsem):
    @pl.when(pl.program_id(0) == 0)
    def _():
        carry_ref[...] = jnp.zeros_like(carry_ref)
        staged = ((wo_hbm, wo_raw, wo_ref), (sg_hbm, sg_raw, sg_ref), (su_hbm, su_raw, su_ref),
                  (sd_hbm, sd_raw, sd_ref))
        copies = [pltpu.make_async_copy(src.at[0], raw, wsem.at[j]) for j, (src, raw, _) in enumerate(staged)]
        for cp in copies:
            cp.start()
        for cp, (_, raw, dst) in zip(copies, staged):
            cp.wait()
            dst[...] = raw[...].astype(BF16)

    mix = (jnp.dot(ys_ref[...], wo_ref[:D_SSM, :], preferred_element_type=F32)
           + jnp.dot(ya_ref[...], wo_ref[D_SSM:, :], preferred_element_type=F32))
    x1 = x_ref[...] + g1_ref[0] * mix
    ms = jnp.mean(x1 * x1, axis=-1, keepdims=True)
    h = x1 * lax.rsqrt(ms + EPS) * ng_ref[...]
    h = h * (1.0 + sc_ref[0]) + sh_ref[0]
    half = h.shape[1] // 2
    h_ref[...] = _pack_bf16_pair(h[:, :half], h[:, half:])
    hi, lo = _split_hi_lo(h)
    hi_terms = jnp.dot(hi, rwl_ref[...], preferred_element_type=F32)
    logits = (hi_terms[:, :LANES] + hi_terms[:, LANES:]
              + jnp.dot(lo, rwh_ref[...], preferred_element_type=F32))
    _route_tokens(logits.T[0:N_EXPERTS, :], rb_ref, upper_ref, idx_ref, gate_ref, rank_ref, cnt_ref, carry_ref)
    u = _silu(jnp.dot(hi, sg_ref[...], preferred_element_type=F32)) * jnp.dot(hi, su_ref[...],
                                                                              preferred_element_type=F32)
    shared = jnp.dot(u.astype(BF16), sd_ref[...], preferred_element_type=F32)
    base_ref[...] = x1 + g2_ref[0] * shared


def _out_proj_call(x2, ys, ya, g1, sc2, sh2, g2, ng, w_out, sg, su, sd, rwh, rwl, router_bias, upper,
                   tiles_per_batch, tm):
    n, d = x2.shape
    row = lambda w: pl.BlockSpec((tm, w), lambda i: (i, 0))
    tok = lambda r: pl.BlockSpec((r, tm), lambda i: (0, i))
    full = lambda a: pl.BlockSpec(a.shape, lambda i: (0, 0))
    per_batch = pl.BlockSpec((1, 1, d), lambda i: (i // tiles_per_batch, 0, 0))
    hbm = pl.BlockSpec(memory_space=pl.ANY)
    staged = (w_out, sg, su, sd)
    return pl.pallas_call(
        _out_proj_kernel,
        grid=(n // tm,),
        in_specs=[row(d), row(D_SSM), row(D_ATT), per_batch, per_batch, per_batch, per_batch, full(ng),
                  hbm, hbm, hbm, hbm, full(rwh), full(rwl), full(router_bias), full(upper)],
        out_specs=[row(d), row(d // 2), tok(TOP_K), tok(TOP_K), tok(TOP_K),
                   pl.BlockSpec((N_EXPERTS, LANES), lambda i: (0, 0))],
        out_shape=[jax.ShapeDtypeStruct((n, d), F32), jax.ShapeDtypeStruct((n, d // 2), jnp.uint32),
                   jax.ShapeDtypeStruct((TOP_K, n), jnp.int32), jax.ShapeDtypeStruct((TOP_K, n), F32),
                   jax.ShapeDtypeStruct((TOP_K, n), jnp.int32), jax.ShapeDtypeStruct((N_EXPERTS, LANES), F32)],
        scratch_shapes=([pltpu.VMEM((N_EXPERTS, LANES), F32)]
                        + [pltpu.VMEM(a.shape[1:], a.dtype) for a in staged]
                        + [pltpu.VMEM(a.shape[1:], BF16) for a in staged]
                        + [pltpu.SemaphoreType.DMA((len(staged),))]),
        compiler_params=pltpu.CompilerParams(dimension_semantics=("arbitrary",),
                                             vmem_limit_bytes=VMEM_LIMIT),
    )(x2, ys, ya, g1, sc2, sh2, g2, ng, w_out, sg, su, sd, rwh, rwl, router_bias, upper)


def _route_tokens(logits_t, rb_ref, upper_ref, idx_ref, gate_ref, rank_ref, cnt_ref, carry_ref):
    t = logits_t.shape[1]
    per_group = N_EXPERTS // ROUTE_GROUPS
    scores = 1.0 / (1.0 + jnp.exp(-logits_t))
    sel = scores + rb_ref[...]
    e_iota = lax.broadcasted_iota(jnp.int32, (N_EXPERTS, t), 0)

    sel3 = sel.reshape(ROUTE_GROUPS, per_group, t)
    w_iota = lax.broadcasted_iota(jnp.int32, sel3.shape, 1)
    m1 = jnp.max(sel3, axis=1, keepdims=True)
    first = jnp.min(jnp.where(sel3 == m1, w_iota, per_group), axis=1, keepdims=True)
    m2 = jnp.max(jnp.where(w_iota == first, NEG_INF, sel3), axis=1, keepdims=True)
    grp = (m1 + m2).reshape(ROUTE_GROUPS, t)

    g_iota = lax.broadcasted_iota(jnp.int32, (ROUTE_GROUPS, t), 0)
    gmask = jnp.zeros((ROUTE_GROUPS, t), jnp.bool_)
    for _ in range(ROUTE_TOPK_GROUPS):
        gm = jnp.max(grp, axis=0, keepdims=True)
        gfirst = jnp.min(jnp.where(grp == gm, g_iota, ROUTE_GROUPS), axis=0, keepdims=True)
        hit = g_iota == gfirst
        gmask = jnp.logical_or(gmask, hit)
        grp = jnp.where(hit, NEG_INF, grp)
    allowed = jnp.broadcast_to(gmask.reshape(ROUTE_GROUPS, 1, t),
                               (ROUTE_GROUPS, per_group, t)).reshape(N_EXPERTS, t)
    masked = jnp.where(allowed, sel, NEG_INF)

    picked = jnp.zeros((N_EXPERTS, t), jnp.bool_)
    idx_rows = []
    w_rows = []
    for _ in range(TOP_K):
        mm = jnp.max(masked, axis=0, keepdims=True)
        efirst = jnp.min(jnp.where(masked == mm, e_iota, N_EXPERTS), axis=0, keepdims=True)
        hit = e_iota == efirst
        idx_rows.append(efirst)
        w_rows.append(jnp.sum(jnp.where(hit, scores, 0.0), axis=0, keepdims=True))
        picked = jnp.logical_or(picked, hit)
        masked = jnp.where(hit, NEG_INF, masked)
    idx = jnp.concatenate(idx_rows, axis=0)
    w = jnp.concatenate(w_rows, axis=0)
    gate_ref[...] = w / jnp.sum(w, axis=0, keepdims=True) * ROUTED_SCALE
    idx_ref[...] = idx

    onehot = jnp.where(picked, 1.0, 0.0)
    sub = upper_ref.shape[0]
    carry = carry_ref[:, 0:1]
    parts = []
    for s0 in range(0, t, sub):
        oh = onehot[:, s0:s0 + sub]
        parts.append(jnp.dot(oh.astype(BF16), upper_ref[...], preferred_element_type=F32) + carry)
        carry = carry + jnp.sum(oh, axis=1, keepdims=True)
    rank_full = jnp.concatenate(parts, axis=1)
    rank_rows = [jnp.sum(jnp.where(e_iota == idx_rows[k], rank_full, 0.0), axis=0, keepdims=True)
                 for k in range(TOP_K)]
    rank_ref[...] = jnp.concatenate(rank_rows, axis=0).astype(jnp.int32)
    carry_ref[...] = jnp.broadcast_to(carry, carry_ref.shape)
    cnt_ref[...] = carry_ref[...]


PLAN_EXPERT, PLAN_FRESH, PLAN_VALID, PLAN_SEG, PLAN_NEXT, PLAN_NUSED = range(6)
PLAN_ROWS = SUBLANES


def _plan_kernel(cnt_ref, tri_ref, start_ref, plan_ref, *, nblocks):
    nbp = plan_ref.shape[1]
    cnt = cnt_ref[...].astype(jnp.int32)
    blocks = (cnt + (EXPERT_ROWS - 1)) // EXPERT_ROWS
    end = jnp.dot(tri_ref[...], blocks.astype(F32), precision=lax.Precision.HIGHEST,
                  preferred_element_type=F32).astype(jnp.int32)
    start = end - blocks
    start_ref[...] = start[:, 0:1] * EXPERT_ROWS
    nused = end[N_EXPERTS - 1:N_EXPERTS, 0:1]

    e_iota = lax.broadcasted_iota(jnp.int32, (N_EXPERTS, nbp), 0)
    blk = lax.broadcasted_iota(jnp.int32, (1, nbp), 1)
    expert = jnp.minimum(jnp.sum((end[:, 0:1] <= blk).astype(jnp.int32), axis=0, keepdims=True), N_EXPERTS - 1)
    mine = e_iota == expert
    pick = lambda col: jnp.sum(jnp.where(mine, col, 0), axis=0, keepdims=True)
    first = pick(start[:, 0:1])
    valid = jnp.clip(pick(cnt[:, 0:1]) - (blk - first) * EXPERT_ROWS, 0, EXPERT_ROWS)
    present = jnp.logical_or(blocks[:, 0:1] > 0,
                             jnp.logical_and(e_iota[:, 0:1] == N_EXPERTS - 1, nused < nblocks))
    seg = jnp.sum(jnp.logical_and(present, e_iota <= expert).astype(jnp.int32), axis=0, keepdims=True) - 1
    nxt = jnp.min(jnp.where(jnp.logical_and(present, e_iota > expert), e_iota, N_EXPERTS), axis=0, keepdims=True)
    rows = {PLAN_EXPERT: expert, PLAN_FRESH: (blk == first).astype(jnp.int32), PLAN_VALID: valid, PLAN_SEG: seg,
            PLAN_NEXT: jnp.where(nxt == N_EXPERTS, -1, nxt), PLAN_NUSED: jnp.broadcast_to(nused, (1, nbp))}
    zero = jnp.zeros((1, nbp), jnp.int32)
    plan_ref[...] = jnp.concatenate([rows.get(r, zero) for r in range(PLAN_ROWS)], axis=0)


def _plan_call(counts, nblocks):
    nbp = -(-nblocks // LANES) * LANES
    tri = jnp.asarray(np.tril(np.ones((N_EXPERTS, N_EXPERTS), np.float32)))
    return pl.pallas_call(
        functools.partial(_plan_kernel, nblocks=nblocks),
        out_shape=[jax.ShapeDtypeStruct((N_EXPERTS, 1), jnp.int32),
                   jax.ShapeDtypeStruct((PLAN_ROWS, nbp), jnp.int32)],
    )(counts, tri)


def _dest_kernel(idx_ref, rank_ref, start_ref, dest_ref):
    t = idx_ref.shape[1]
    e_iota = lax.broadcasted_iota(jnp.int32, (N_EXPERTS, t), 0)
    rows = [jnp.sum(jnp.where(e_iota == idx_ref[k:k + 1, :], start_ref[...], 0), axis=0, keepdims=True)
            for k in range(TOP_K)]
    dest = jnp.concatenate(rows, axis=0) + rank_ref[...]
    for k in range(TOP_K):
        for c in range(t // LANES):
            dest_ref[k, c:c + 1, :] = dest[k:k + 1, c * LANES:(c + 1) * LANES]


def _dest_call(idx, rank, pad_start, tile):
    n = idx.shape[1]
    tok = pl.BlockSpec((TOP_K, tile), lambda i: (0, i))
    return pl.pallas_call(
        _dest_kernel,
        grid=(n // tile,),
        in_specs=[tok, tok, pl.BlockSpec((N_EXPERTS, 1), lambda i: (0, 0))],
        out_specs=pl.BlockSpec((TOP_K, tile // LANES, LANES), lambda i: (0, i, 0)),
        out_shape=jax.ShapeDtypeStruct((TOP_K, n // LANES, LANES), jnp.int32),
        compiler_params=pltpu.CompilerParams(dimension_semantics=("arbitrary",)),
    )(idx, rank, pad_start)


def _scatter_rows_sc(rows, dest_flat, total_rows, chunk):
    n, w = rows.shape
    copies = dest_flat.shape[0] // n
    info = plsc.get_sparse_core_info()
    nc = info.num_cores
    per_worker = n // (nc * info.num_subcores)
    assert per_worker * nc * info.num_subcores == n and per_worker % chunk == 0
    mesh = plsc.VectorSubcoreMesh(core_axis_name="c", subcore_axis_name="s")

    @functools.partial(
        pl.kernel, mesh=mesh,
        out_type=jax.ShapeDtypeStruct((total_rows, w), rows.dtype),
        scratch_types=([pltpu.VMEM((chunk,), jnp.int32) for _ in range(copies)]
                       + [pltpu.VMEM((chunk, w), rows.dtype), pltpu.SemaphoreType.DMA, pltpu.SemaphoreType.DMA]),
    )
    def scatter(rows_hbm, idx_hbm, out_hbm, *scratch):
        idx_vs, (rows_v, isem, sem) = scratch[:copies], scratch[copies:]
        worker = lax.axis_index("s") * nc + lax.axis_index("c")

        @pl.loop(0, per_worker // chunk)
        def _(j):
            base = worker * per_worker + j * chunk
            loads = [pltpu.async_copy(idx_hbm.at[pl.ds(k * n + base, chunk)], idx_vs[k], isem)
                     for k in range(copies)]
            pltpu.sync_copy(rows_hbm.at[pl.ds(base, chunk)], rows_v)
            for ld in loads:
                ld.wait()
            stores = [pltpu.async_copy(rows_v, out_hbm.at[idx_vs[k]], sem) for k in range(copies)]
            for st in stores:
                st.wait()

    return scatter(rows, dest_flat)


X_RING = 3


def _expert_kernel(plan_ref, xs_hbm, wg_hbm, wu_hbm, wd_hbm, y_ref,
                   wgb_ref, wub_ref, wdb_ref, xbuf_ref, xsem, wg_raw, wu_raw, wd_raw, wsem):
    i = pl.program_id(0)
    rows, half = xbuf_ref.shape[1], xbuf_ref.shape[2]
    nused = plan_ref[PLAN_NUSED, 0]

    def w_copies(expert, slot):
        return [pltpu.make_async_copy(src.at[expert], dst.at[slot], wsem.at[slot])
                for src, dst in ((wg_hbm, wg_raw), (wu_hbm, wu_raw), (wd_hbm, wd_raw))]

    def x_copy(block):
        first = pl.multiple_of(block * rows, rows)
        slot = block % X_RING
        return pltpu.make_async_copy(xs_hbm.at[pl.ds(first, rows)], xbuf_ref.at[slot], xsem.at[slot])

    @pl.when(i == 0)
    def _():
        for b in range(X_RING - 1):
            @pl.when(b < nused)
            def _():
                x_copy(b).start()

    @pl.when(i + (X_RING - 1) < nused)
    def _():
        x_copy(i + (X_RING - 1)).start()

    @pl.when(i == 0)
    def _():
        for c in w_copies(plan_ref[PLAN_EXPERT, 0], 0):
            c.start()

    @pl.when(plan_ref[PLAN_FRESH, i] > 0)
    def _():
        slot = plan_ref[PLAN_SEG, i] % 2
        for c in w_copies(plan_ref[PLAN_EXPERT, i], slot):
            c.wait()

        @pl.when(plan_ref[PLAN_NEXT, i] >= 0)
        def _():
            for c in w_copies(plan_ref[PLAN_NEXT, i], 1 - slot):
                c.start()

        wgb_ref[...] = wg_raw[slot].astype(BF16)
        wub_ref[...] = wu_raw[slot].astype(BF16)
        wdb_ref[...] = wd_raw[slot].astype(BF16)

    @pl.when(i < nused)
    def _():
        x_copy(i).wait()
        xw = xbuf_ref[i % X_RING]
        row = lax.broadcasted_iota(jnp.int32, xw.shape, 0)
        x_lo, x_hi = _unpack_bf16_pair(jnp.where(row < plan_ref[PLAN_VALID, i], xw, jnp.uint32(0)))
        x_lo = x_lo.astype(BF16)
        x_hi = x_hi.astype(BF16)
        gate = (jnp.dot(x_lo, wgb_ref[:half, :], preferred_element_type=F32)
                + jnp.dot(x_hi, wgb_ref[half:, :], preferred_element_type=F32))
        up = (jnp.dot(x_lo, wub_ref[:half, :], preferred_element_type=F32)
              + jnp.dot(x_hi, wub_ref[half:, :], preferred_element_type=F32))
        u = (_silu(gate) * up).astype(BF16)
        y_lo = jnp.dot(u, wdb_ref[:, :half], preferred_element_type=F32)
        y_hi = jnp.dot(u, wdb_ref[:, half:], preferred_element_type=F32)
        y_ref[...] = _pack_bf16_pair(y_lo, y_hi)

    @pl.when(i >= nused)
    def _():
        y_ref[...] = jnp.zeros_like(y_ref)


def _expert_call(plan, xs, wg, wu, wd, rows):
    p, w = xs.shape
    d, f = wg.shape[1], wg.shape[2]
    hbm = pl.BlockSpec(memory_space=pl.ANY)
    grid_spec = pltpu.PrefetchScalarGridSpec(
        num_scalar_prefetch=1,
        grid=(p // rows,),
        in_specs=[hbm, hbm, hbm, hbm],
        out_specs=pl.BlockSpec((rows, w), lambda i, *_: (i, 0)),
        scratch_shapes=[pltpu.VMEM((d, f), BF16), pltpu.VMEM((d, f), BF16), pltpu.VMEM((f, d), BF16),
                        pltpu.VMEM((X_RING, rows, w), xs.dtype), pltpu.SemaphoreType.DMA((X_RING,)),
                        pltpu.VMEM((2, d, f), wg.dtype), pltpu.VMEM((2, d, f), wu.dtype),
                        pltpu.VMEM((2, f, d), wd.dtype), pltpu.SemaphoreType.DMA((2,))],
    )
    return pl.pallas_call(
        _expert_kernel,
        grid_spec=grid_spec,
        out_shape=jax.ShapeDtypeStruct((p, w), jnp.uint32),
        compiler_params=pltpu.CompilerParams(dimension_semantics=("arbitrary",),
                                             vmem_limit_bytes=VMEM_LIMIT),
    )(plan, xs, wg, wu, wd)


def _gather_rows_sc(table, idx, chunk):
    m = idx.shape[0]
    w = table.shape[1]
    info = plsc.get_sparse_core_info()
    nc = info.num_cores
    per_worker = m // (nc * info.num_subcores)
    assert per_worker * nc * info.num_subcores == m and per_worker % chunk == 0
    part = chunk // SC_GATHER_PARTS
    mesh = plsc.VectorSubcoreMesh(core_axis_name="c", subcore_axis_name="s")

    @functools.partial(
        pl.kernel, mesh=mesh,
        out_type=jax.ShapeDtypeStruct((m, w), table.dtype),
        scratch_types=([pltpu.VMEM((part,), jnp.int32) for _ in range(SC_GATHER_PARTS)]
                       + [pltpu.VMEM((part, w), table.dtype) for _ in range(SC_GATHER_PARTS)]
                       + [pltpu.SemaphoreType.DMA] * 3),
    )
    def gather(table_hbm, idx_hbm, out_hbm, *scratch):
        idx_vs = scratch[:SC_GATHER_PARTS]
        rows_vs = scratch[SC_GATHER_PARTS:2 * SC_GATHER_PARTS]
        isem, gsem, wsem = scratch[2 * SC_GATHER_PARTS:]
        worker = lax.axis_index("s") * nc + lax.axis_index("c")

        @pl.loop(0, per_worker // chunk)
        def _(j):
            base = worker * per_worker + j * chunk
            loads = [pltpu.async_copy(idx_hbm.at[pl.ds(base + p * part, part)], idx_vs[p], isem)
                     for p in range(SC_GATHER_PARTS)]
            gathers = []
            for p in range(SC_GATHER_PARTS):
                loads[p].wait()
                gathers.append(pltpu.async_copy(table_hbm.at[idx_vs[p]], rows_vs[p], gsem))
            writes = []
            for p in range(SC_GATHER_PARTS):
                gathers[p].wait()
                writes.append(pltpu.async_copy(rows_vs[p], out_hbm.at[pl.ds(base + p * part, part)], wsem))
            for wr in writes:
                wr.wait()

    return gather(table, idx)


def _combine_kernel(yk_ref, gate_ref, base_ref, g2_ref, fg_ref, o_ref):
    t = base_ref.shape[0]
    half = yk_ref.shape[2]
    gates = jnp.concatenate([gate_ref[...], jnp.zeros((LANES - TOP_K, t), F32)], axis=0).T
    r_lo = jnp.zeros((t, half), F32)
    r_hi = jnp.zeros((t, half), F32)
    for k in range(TOP_K):
        y_lo, y_hi = _unpack_bf16_pair(yk_ref[k])
        r_lo = r_lo + gates[:, k:k + 1] * y_lo
        r_hi = r_hi + gates[:, k:k + 1] * y_hi
    g2 = g2_ref[0]
    x_lo = base_ref[:, :half] + g2[:, :half] * r_lo
    x_hi = base_ref[:, half:] + g2[:, half:] * r_hi
    ms = (jnp.sum(x_lo * x_lo, axis=-1, keepdims=True)
          + jnp.sum(x_hi * x_hi, axis=-1, keepdims=True)) * (1.0 / (2 * half))
    inv = lax.rsqrt(ms + EPS)
    o_ref[:, :half] = x_lo * inv * fg_ref[:, :half]
    o_ref[:, half:] = x_hi * inv * fg_ref[:, half:]


def _combine_call(yk, gates, base, g2, fg, tiles_per_batch, tile, tile0):
    n, d = base.shape
    row = lambda w: pl.BlockSpec((tile, w), lambda i: (i + tile0, 0))
    return pl.pallas_call(
        _combine_kernel,
        grid=(yk.shape[1] // tile,),
        in_specs=[pl.BlockSpec((TOP_K, tile, yk.shape[2]), lambda i: (0, i, 0)),
                  pl.BlockSpec((TOP_K, tile), lambda i: (0, i + tile0)), row(d),
                  pl.BlockSpec((1, 1, d), lambda i: ((i + tile0) // tiles_per_batch, 0, 0)),
                  pl.BlockSpec((1, d), lambda i: (0, 0))],
        out_specs=row(d),
        out_shape=jax.ShapeDtypeStruct((n, d), F32),
        input_output_aliases={2: 0},
        compiler_params=pltpu.CompilerParams(dimension_semantics=("arbitrary",),
                                             vmem_limit_bytes=VMEM_LIMIT),
    )(yk, gates, base, g2, fg)


def _pad_cols(a, width):
    return jnp.pad(a, ((0, 0), (0, width - a.shape[1])))


def _layer(x, mod, norm1_g, norm2_g, w_in, conv_w, conv_b, dt_bias, a_log, d_skip, ssm_norm_g,
           att_norm_g, sinks, rel_bias, w_out, router_w, router_bias, exp_w_gate, exp_w_up, exp_w_down,
           sh_w_gate, sh_w_up, sh_w_down, final_g):
    bsz, l, d = x.shape
    n = bsz * l
    tm = min(ROW_TILE, l)

    sh1, sc1, g1, sh2, sc2, g2 = [m[:, None, :] for m in jnp.split(mod, 6, axis=-1)]

    assert math.frexp(ATT_HEAD_DIM ** -0.5)[0] == 0.5
    x2 = x.reshape(n, d)
    z, xbc, dt, q, k, v = _in_proj_call(x2, sc1, sh1, norm1_g[None, :], jnp.swapaxes(w_in, 1, 2), l // tm, tm)

    triu = jnp.asarray(np.triu(np.ones((CHUNK, CHUNK), np.float32))).astype(BF16)
    shift = jnp.asarray(_conv_shift_matrix()).astype(BF16)
    y_ssm = _ssd_call(xbc.reshape(bsz, l, CONV_CH), z.reshape(bsz, l, D_SSM), dt.reshape(bsz, l, LANES),
                      conv_w, conv_b[None, :], dt_bias[:, None], a_log[:, None],
                      jnp.repeat(d_skip, SSM_HEAD_DIM)[None, :], ssm_norm_g[None, :], triu, shift)

    bias = _bias_call(rel_bias, jnp.asarray(_rel_bucket_table()))
    y_att = _attn_call(sinks, q.reshape(bsz, l, D_ATT), k.reshape(bsz, l, D_KV), v.reshape(bsz, l, D_KV), bias,
                       att_norm_g[None, :])

    rw = _pad_cols(router_w, LANES)
    rwh = rw.astype(BF16)
    rwl = jnp.concatenate([rwh, (rw - rwh.astype(F32)).astype(BF16)], axis=1)
    rs = min(RANK_SUB, tm)
    upper = jnp.asarray(np.triu(np.ones((rs, rs), np.float32), 1)).astype(BF16)
    base, h2, idx, gates, rank, counts = _out_proj_call(
        x2, y_ssm.reshape(n, D_SSM), y_att.reshape(n, D_ATT), g1, sc2, sh2, g2, norm2_g[None, :],
        w_out, sh_w_gate, sh_w_up, sh_w_down, rwh, rwl, router_bias[:, None], upper, l // tm, tm)
    rt = min(ROUTE_TILE, n)

    nblocks = (n * TOP_K + N_EXPERTS * (EXPERT_ROWS - 1) + EXPERT_ROWS - 1) // EXPERT_ROWS
    pad_start, plan = _plan_call(counts, nblocks)
    dest = _dest_call(idx, rank, pad_start, rt)

    xs = _scatter_rows_sc(h2, dest.reshape(-1), nblocks * EXPERT_ROWS, SC_CHUNK)
    ys = _expert_call(plan, xs, exp_w_gate, exp_w_up, exp_w_down, EXPERT_ROWS)
    ctile = min(COMBINE_TILE, l)
    groups = COMBINE_GROUPS if bsz % COMBINE_GROUPS == 0 else 1
    ng = n // groups
    out = base
    for g in range(groups):
        idx_g = dest[:, g * ng // LANES:(g + 1) * ng // LANES, :].reshape(-1)
        yk = _gather_rows_sc(ys, idx_g, SC_CHUNK).reshape(TOP_K, ng, ys.shape[1])
        out = _combine_call(yk, gates, out, g2, final_g[None, :], l // ctile, ctile, g * ng // ctile)
    return out.reshape(bsz, l, d)


def kernel(x, c, mod_w, mod_b, norm1_g, norm2_g, w_in, conv_w, conv_b, dt_bias, a_log, d_skip, ssm_norm_g,
           att_norm_g, sinks, rel_bias, w_out, router_w, router_bias, exp_w_gate, exp_w_up, exp_w_down,
           sh_w_gate, sh_w_up, sh_w_down, final_g):
    assert mod_w.shape[0] == 1, "single-layer block"
    bsz = x.shape[0]
    c_pad = jnp.pad(c, ((0, SUBLANES - bsz % SUBLANES if bsz % SUBLANES else 0), (0, 0)))
    mod = _mod_call(c_pad, mod_w[0], mod_b[0][None, :])[:bsz]
    return _layer(x, mod, norm1_g[0], norm2_g[0], w_in, conv_w[0], conv_b[0], dt_bias[0], a_log[0], d_skip[0],
                  ssm_norm_g[0], att_norm_g[0], sinks[0], rel_bias, w_out, router_w[0], router_bias[0],
                  exp_w_gate[0], exp_w_up[0], exp_w_down[0], sh_w_gate, sh_w_up, sh_w_down, final_g)
```

```python
import functools
import math

import numpy as np
import jax
import jax.numpy as jnp
from jax import lax
from jax.experimental import pallas as pl
from jax.experimental.pallas import tpu as pltpu
from jax.experimental.pallas import tpu_sc as plsc

F32 = jnp.float32
BF16 = jnp.bfloat16

D_MODEL = 1024
SSM_HEAD_DIM = 64
D_SSM = D_MODEL
SSM_HEADS = D_SSM // SSM_HEAD_DIM
SSM_GROUPS = 4
D_STATE = 128
CONV_K = 4
CONV_CH = D_SSM + 2 * SSM_GROUPS * D_STATE
CHUNK = 128
ATT_HEAD_DIM = 64
D_ATT = D_MODEL
ATT_HEADS = D_ATT // ATT_HEAD_DIM
KV_HEADS = ATT_HEADS // 4
Q_PER_KV = ATT_HEADS // KV_HEADS
D_KV = KV_HEADS * ATT_HEAD_DIM
WINDOW = 128
ATT_BLOCK = 128
REL_BUCKETS = 32
REL_MAX_DIST = 128
N_EXPERTS = 64
TOP_K = 8
EXPERT_DIM = D_MODEL // 4
SHARED_DIM = D_MODEL // 4
ROUTE_GROUPS = 8
ROUTE_TOPK_GROUPS = 4
ROUTED_SCALE = 2.5
EPS = 1e-6

LANES = 128
SUBLANES = 8
HALF = LANES // 2

ROW_TILE = 512
ROUTE_TILE = 2048
RANK_SUB = 512
COMBINE_TILE = 256
SC_CHUNK = 128
SC_GATHER_PARTS = 4
EXPERT_ROWS = 1024
COMBINE_GROUPS = 2
VMEM_LIMIT = 48 * 1024 * 1024

NEG_INF = float("-inf")


def _silu(v):
    return v * (1.0 / (1.0 + jnp.exp(-v)))


def _softplus(v):
    return jnp.maximum(v, 0.0) + jnp.log(1.0 + jnp.exp(-jnp.abs(v)))


def _bdot(a, b):
    return jnp.dot(a.astype(BF16), b.astype(BF16), preferred_element_type=F32)


def _split_hi_lo(v):
    hi = v.astype(BF16)
    lo = (v - hi.astype(F32)).astype(BF16)
    return hi, lo


def _pack_bf16_pair(a, b):
    w = pltpu.pack_elementwise([a, b], packed_dtype=BF16)
    return w if w.dtype == jnp.uint32 else lax.bitcast_convert_type(w, jnp.uint32)


def _unpack_bf16_pair(w):
    a = pltpu.unpack_elementwise(w, index=0, packed_dtype=BF16, unpacked_dtype=F32)
    b = pltpu.unpack_elementwise(w, index=1, packed_dtype=BF16, unpacked_dtype=F32)
    return a, b


def _lane_half_mask(shape):
    return lax.broadcasted_iota(jnp.int32, shape, len(shape) - 1) < HALF


def _mod_kernel(c_ref, w_ref, b_ref, o_ref):
    a = _silu(c_ref[...])
    o_ref[...] = jnp.dot(a, w_ref[...], precision=lax.Precision.HIGHEST,
                         preferred_element_type=F32) + b_ref[...]


def _mod_call(c_pad, mod_w, mod_b):
    rows, d = c_pad.shape
    cols = mod_w.shape[1]
    return pl.pallas_call(
        _mod_kernel,
        grid=(cols // d,),
        in_specs=[pl.BlockSpec((rows, d), lambda j: (0, 0)),
                  pl.BlockSpec((d, d), lambda j: (0, j)),
                  pl.BlockSpec((1, d), lambda j: (0, j))],
        out_specs=pl.BlockSpec((rows, d), lambda j: (0, j)),
        out_shape=jax.ShapeDtypeStruct((rows, cols), F32),
        compiler_params=pltpu.CompilerParams(dimension_semantics=("arbitrary",),
                                             vmem_limit_bytes=VMEM_LIMIT),
    )(c_pad, mod_w, mod_b)


IN_PROJ_SEGMENTS = ((D_SSM, BF16), (CONV_CH, BF16), (LANES, F32), (D_ATT, BF16), (D_KV, BF16), (D_KV, BF16))


def _in_proj_kernel(x_ref, sc_ref, sh_ref, g_ref, w_hbm, *refs):
    out_refs = refs[:len(IN_PROJ_SEGMENTS)]
    wraw_ref, w_ref, wsem = refs[len(IN_PROJ_SEGMENTS):]

    @pl.when(pl.program_id(0) == 0)
    def _():
        cp = pltpu.make_async_copy(w_hbm.at[0], wraw_ref, wsem)
        cp.start()
        cp.wait()
        src_dt = D_SSM + CONV_CH
        dst_q = src_dt + LANES
        src_q = src_dt + SSM_HEADS
        q_scale = ATT_HEAD_DIM ** -0.5

        def put(dst, src, scale=None):
            t = wraw_ref[src:src + LANES, :].T
            w_ref[:, dst:dst + LANES] = (t if scale is None else t * scale).astype(BF16)

        for c0 in range(0, src_dt, LANES):
            put(c0, c0)
        dt_tile = wraw_ref[src_dt:src_dt + LANES, :].T
        lane = lax.broadcasted_iota(jnp.int32, dt_tile.shape, 1)
        w_ref[:, src_dt:dst_q] = jnp.where(lane < SSM_HEADS, dt_tile, 0.0).astype(BF16)
        for c0 in range(0, D_ATT, LANES):
            put(dst_q + c0, src_q + c0, q_scale)
        for c0 in range(D_ATT, D_ATT + 2 * D_KV, LANES):
            put(dst_q + c0, src_q + c0)

    xf = x_ref[...]
    ms = jnp.mean(xf * xf, axis=-1, keepdims=True)
    h = xf * lax.rsqrt(ms + EPS) * g_ref[...]
    h = h * (1.0 + sc_ref[0]) + sh_ref[0]
    hb = h.astype(BF16)
    col = 0
    for (width, dtype), o_ref in zip(IN_PROJ_SEGMENTS, out_refs):
        o_ref[...] = jnp.dot(hb, w_ref[:, col:col + width], preferred_element_type=F32).astype(dtype)
        col += width


def _in_proj_call(x2, sc1, sh1, g1n, w_in_t, tiles_per_batch, tm):
    n, d = x2.shape
    row = lambda w: pl.BlockSpec((tm, w), lambda i: (i, 0))
    full = lambda a: pl.BlockSpec(a.shape, lambda i: (0, 0))
    per_batch = pl.BlockSpec((1, 1, d), lambda i: (i // tiles_per_batch, 0, 0))
    cols = sum(w for w, _ in IN_PROJ_SEGMENTS)
    w_in = w_in_t
    assert w_in.shape == (1, cols - (LANES - SSM_HEADS), d)
    return pl.pallas_call(
        _in_proj_kernel,
        grid=(n // tm,),
        in_specs=[row(d), per_batch, per_batch, full(g1n), pl.BlockSpec(memory_space=pl.ANY)],
        out_specs=[row(w) for w, _ in IN_PROJ_SEGMENTS],
        out_shape=[jax.ShapeDtypeStruct((n, w), dt) for w, dt in IN_PROJ_SEGMENTS],
        scratch_shapes=[pltpu.VMEM(w_in.shape[1:], w_in.dtype), pltpu.VMEM((d, cols), BF16), pltpu.SemaphoreType.DMA(())],
        compiler_params=pltpu.CompilerParams(dimension_semantics=("arbitrary",),
                                             vmem_limit_bytes=VMEM_LIMIT),
    )(x2, sc1, sh1, g1n, w_in)


SSD_SEQS = 4
CONV_HALO = 16


def _conv_shift_matrix():
    s = np.zeros((CONV_K * CHUNK, CONV_HALO + CHUNK), np.float32)
    for k in range(CONV_K):
        t = np.arange(CHUNK)
        s[k * CHUNK + t, CONV_HALO + t - (CONV_K - 1) + k] = 1.0
    return s


def _silu_tanh(v):
    hv = 0.5 * v
    return hv + hv * jnp.tanh(hv)


def _ssd_kernel(xbc_ref, z_ref, dt_ref, cw_ref, cb_ref, dtb_ref, alog_ref, dskip_ref, ng_ref, triu_ref, shift_ref,
                y_ref, state_ref, ucat_ref, ybuf_ref):
    nseq = xbc_ref.shape[0]

    @pl.when(pl.program_id(1) == 0)
    def _():
        state_ref[...] = jnp.zeros_like(state_ref)
        ucat_ref[:, 0:CONV_HALO, :] = jnp.zeros((nseq, CONV_HALO, CONV_CH), BF16)

    for q in range(nseq):
        _ssd_chunk(xbc_ref.at[q], z_ref.at[q], dt_ref.at[q], cw_ref, cb_ref, dtb_ref, alog_ref, dskip_ref, ng_ref,
                   triu_ref, shift_ref, y_ref.at[q], state_ref.at[q], ucat_ref.at[q], ybuf_ref.at[q])


def _ssd_chunk(xbc_ref, z_ref, dt_ref, cw_ref, cb_ref, dtb_ref, alog_ref, dskip_ref, ng_ref, triu_ref, shift_ref,
               y_ref, state_ref, ucat_ref, ybuf_ref):
    ucat_ref[CONV_HALO:, :] = xbc_ref[...]
    shifted = jnp.dot(shift_ref[...], ucat_ref[...], preferred_element_type=F32)
    ucat_ref[0:CONV_HALO, :] = ucat_ref[CHUNK:CHUNK + CONV_HALO, :]
    acc = cb_ref[...] + cw_ref[0:1, :] * shifted[0:CHUNK]
    for kk in range(1, CONV_K):
        acc = acc + cw_ref[kk:kk + 1, :] * shifted[kk * CHUNK:(kk + 1) * CHUNK]
    act = _silu_tanh(acc)
    xs = act[:, :D_SSM]
    gn = SSM_GROUPS * D_STATE

    dt_t = _softplus(dt_ref[...].T[0:SSM_HEADS, :] + dtb_ref[...])
    a_t = dt_t * (-jnp.exp(alog_ref[...]))
    a_hi = a_t.astype(BF16)
    a_mid = (a_t - a_hi.astype(F32)).astype(BF16)
    a_lo = (a_t - a_hi.astype(F32) - a_mid.astype(F32)).astype(BF16)
    triu = triu_ref[...]
    cs_t = (jnp.dot(a_hi, triu, preferred_element_type=F32) + jnp.dot(a_mid, triu, preferred_element_type=F32)
            + jnp.dot(a_lo, triu, preferred_element_type=F32))
    cs_end = cs_t[:, CHUNK - 1:CHUNK]
    r_t = cs_t - jnp.log(dt_t)
    w_t = jnp.exp(cs_end - cs_t) * dt_t
    chunk_decay = jnp.exp(cs_end)
    cols = jnp.concatenate([cs_t, jnp.exp(cs_t), jnp.zeros((LANES - 2 * SSM_HEADS, CHUNK), F32)], axis=0).T

    li = lax.broadcasted_iota(jnp.int32, (CHUNK, CHUNK), 0)
    si = lax.broadcasted_iota(jnp.int32, (CHUNK, CHUNK), 1)
    causal = li >= si
    low = _lane_half_mask((CHUNK, LANES))
    low_row = _lane_half_mask((1, LANES))

    heads_per_group = SSM_HEADS // SSM_GROUPS
    for g in range(SSM_GROUPS):
        b_g = act[:, D_SSM + g * D_STATE:D_SSM + (g + 1) * D_STATE]
        c_g = act[:, D_SSM + gn + g * D_STATE:D_SSM + gn + (g + 1) * D_STATE]
        b_gb = b_g.astype(BF16)
        c_gb = c_g.astype(BF16)
        cb = lax.dot_general(c_gb, b_gb, (((1,), (1,)), ((), ())), preferred_element_type=F32)
        b_t = b_g.T
        for jp in range(heads_per_group // 2):
            j = g * (heads_per_group // 2) + jp
            lanes = slice(j * LANES, (j + 1) * LANES)
            xp = xs[:, lanes]
            xpb = xp.astype(BF16)
            ydiag = jnp.zeros((CHUNK, LANES), F32)
            snew = jnp.zeros((D_STATE, LANES), F32)
            for half in range(2):
                h = 2 * j + half
                diff = cols[:, h:h + 1] - r_t[h:h + 1, :]
                m = (cb * jnp.exp(jnp.where(causal, diff, NEG_INF))).astype(BF16)
                keep = low if half == 0 else jnp.logical_not(low)
                xh = jnp.where(keep, xpb, jnp.zeros_like(xpb))
                ydiag = ydiag + jnp.dot(m, xh, preferred_element_type=F32)
                snew = snew + jnp.dot((b_t * w_t[h:h + 1, :]).astype(BF16), xh, preferred_element_type=F32)
            s_in = state_ref[:, lanes]
            yoff = jnp.dot(c_gb, s_in.astype(BF16), preferred_element_type=F32)
            h0 = 2 * j
            e0 = SSM_HEADS + h0
            escale = jnp.where(low, cols[:, e0:e0 + 1], cols[:, e0 + 1:e0 + 2])
            cdec = jnp.where(low_row, chunk_decay[h0:h0 + 1, :], chunk_decay[h0 + 1:h0 + 2, :])
            ybuf_ref[:, lanes] = ydiag + yoff * escale + xp * dskip_ref[:, lanes]
            state_ref[:, lanes] = s_in * cdec + snew

    yz = ybuf_ref[...] * _silu_tanh(z_ref[...].astype(F32))
    gw = D_SSM // SSM_GROUPS
    for g in range(SSM_GROUPS):
        part = yz[:, g * gw:(g + 1) * gw]
        ms = jnp.mean(part * part, axis=-1, keepdims=True)
        y_ref[:, g * gw:(g + 1) * gw] = (part * lax.rsqrt(ms + EPS)
                                            * ng_ref[:, g * gw:(g + 1) * gw]).astype(BF16)


def _ssd_call(xbc, z, dt, conv_w, conv_b, dtb, alog, dskip, ng, triu, shift):
    bsz, l, _ = xbc.shape
    nc = l // CHUNK
    nseq = SSD_SEQS if bsz % SSD_SEQS == 0 else 1
    chunk = lambda w: pl.BlockSpec((nseq, CHUNK, w), lambda b, c: (b, c, 0))
    full = lambda a: pl.BlockSpec(a.shape, lambda b, c: (0, 0))
    return pl.pallas_call(
        _ssd_kernel,
        grid=(bsz // nseq, nc),
        in_specs=[chunk(CONV_CH), chunk(D_SSM), chunk(LANES), full(conv_w), full(conv_b), full(dtb),
                  full(alog), full(dskip), full(ng), full(triu), full(shift)],
        out_specs=chunk(D_SSM),
        out_shape=jax.ShapeDtypeStruct((bsz, l, D_SSM), BF16),
        scratch_shapes=[pltpu.VMEM((nseq, D_STATE, D_SSM), F32),
                        pltpu.VMEM((nseq, CONV_HALO + CHUNK, CONV_CH), BF16),
                        pltpu.VMEM((nseq, CHUNK, D_SSM), F32)],
        compiler_params=pltpu.CompilerParams(dimension_semantics=("arbitrary", "arbitrary"),
                                             vmem_limit_bytes=VMEM_LIMIT),
    )(xbc, z, dt, conv_w, conv_b, dtb, alog, dskip, ng, triu, shift)


assert WINDOW == ATT_BLOCK


def _rel_bucket_table():
    qi = np.arange(ATT_BLOCK)[:, None]
    c = np.arange(ATT_BLOCK)[None, :]
    dist = np.where(c > qi, qi + ATT_BLOCK - c, qi - c)
    max_exact = REL_BUCKETS // 2
    d = np.maximum(dist, 1).astype(np.float32)
    large = max_exact + (np.log(d / np.float32(max_exact)) / np.float32(math.log(REL_MAX_DIST / max_exact))
                         * np.float32(REL_BUCKETS - max_exact)).astype(np.int32)
    large = np.minimum(large, REL_BUCKETS - 1)
    return np.where(dist < max_exact, dist, large).astype(np.int32)


def _bias_kernel(rb_ref, bucket_ref, o_ref):
    bucket = bucket_ref[...]
    from_prev = (lax.broadcasted_iota(jnp.int32, bucket.shape, 1)
                 > lax.broadcasted_iota(jnp.int32, bucket.shape, 0))
    for h in range(ATT_HEADS):
        acc = jnp.zeros(bucket.shape, F32)
        for b in range(REL_BUCKETS):
            acc = jnp.where(bucket == b, rb_ref[b, h], acc)
        o_ref[1, h] = acc
        o_ref[0, h] = jnp.where(from_prev, NEG_INF, acc)


def _bias_call(rel_bias, bucket):
    return pl.pallas_call(
        _bias_kernel,
        in_specs=[pl.BlockSpec(memory_space=pltpu.SMEM), pl.BlockSpec(memory_space=pltpu.VMEM)],
        out_shape=jax.ShapeDtypeStruct((2, ATT_HEADS) + bucket.shape, F32),
    )(rel_bias, bucket)


ATT_SEQS = 4


def _attn_kernel(sink_ref, q_ref, kp_ref, kc_ref, vp_ref, vc_ref, bias_ref, ng_ref, o_ref, obuf_ref):
    for s in range(q_ref.shape[0]):
        _attn_block(sink_ref, q_ref.at[s], kp_ref.at[s], kc_ref.at[s], vp_ref.at[s], vc_ref.at[s], bias_ref.at[0],
                    ng_ref, o_ref.at[s], obuf_ref.at[s])


def _attn_block(sink_ref, q_ref, kp_ref, kc_ref, vp_ref, vc_ref, bias_ref, ng_ref, o_ref, obuf_ref):
    qi = lax.broadcasted_iota(jnp.int32, (ATT_BLOCK, ATT_BLOCK), 0)
    ci = lax.broadcasted_iota(jnp.int32, (ATT_BLOCK, ATT_BLOCK), 1)
    from_prev = ci > qi
    low = _lane_half_mask((ATT_BLOCK, LANES))

    def band_variants(prev_ref, cur_ref):
        out = []
        for cpair in range(KV_HEADS // 2):
            lanes = slice(cpair * LANES, (cpair + 1) * LANES)
            t = jnp.concatenate([prev_ref[:, lanes], cur_ref[:, lanes]], axis=0).astype(F32)
            out.append((t.astype(BF16), pltpu.roll(t, HALF, 1).astype(BF16)))
        return out

    k_band = band_variants(kp_ref, kc_ref)
    v_band = band_variants(vp_ref, vc_ref)
    nt = (((1,), (1,)), ((), ()))

    for j in range(ATT_HEADS // 2):
        qp = q_ref[:, j * LANES:(j + 1) * LANES]
        out_pair = jnp.zeros((ATT_BLOCK, LANES), F32)
        for half in range(2):
            h = 2 * j + half
            g = h // Q_PER_KV
            swapped = int((g % 2) != half)
            keep = low if half == 0 else jnp.logical_not(low)
            qh = jnp.where(keep, qp, jnp.zeros_like(qp))
            s_band = lax.dot_general(qh, k_band[g // 2][swapped], nt, preferred_element_type=F32)
            s = jnp.where(from_prev, s_band[:, :ATT_BLOCK], s_band[:, ATT_BLOCK:]) + bias_ref[h]
            sink = sink_ref[h]
            m = jnp.maximum(jnp.max(s, axis=-1, keepdims=True), sink)
            p = jnp.exp(s - m)
            denom = jnp.sum(p, axis=-1, keepdims=True) + jnp.exp(sink - m)
            p_band = jnp.concatenate([jnp.where(from_prev, p, 0.0), jnp.where(from_prev, 0.0, p)], axis=1)
            o = jnp.dot(p_band.astype(BF16), v_band[g // 2][swapped], preferred_element_type=F32) / denom
            out_pair = out_pair + jnp.where(keep, o, 0.0)
        obuf_ref[:, j * LANES:(j + 1) * LANES] = out_pair

    att = obuf_ref[...]
    ms = jnp.mean(att * att, axis=-1, keepdims=True)
    o_ref[...] = (att * lax.rsqrt(ms + EPS) * ng_ref[...]).astype(BF16)


def _attn_call(sinks, q, k, v, bias, ng):
    bsz, l, _ = q.shape
    nb = l // ATT_BLOCK
    nseq = ATT_SEQS if bsz % ATT_SEQS == 0 else 1
    cur = lambda w: pl.BlockSpec((nseq, ATT_BLOCK, w), lambda b, i: (b, i, 0))
    prev = lambda w: pl.BlockSpec((nseq, ATT_BLOCK, w), lambda b, i: (b, jnp.maximum(i - 1, 0), 0))
    return pl.pallas_call(
        _attn_kernel,
        grid=(bsz // nseq, nb),
        in_specs=[pl.BlockSpec(memory_space=pltpu.SMEM),
                  cur(D_ATT), prev(D_KV), cur(D_KV), prev(D_KV), cur(D_KV),
                  pl.BlockSpec((1,) + bias.shape[1:], lambda b, i: (jnp.minimum(i, 1), 0, 0, 0)),
                  pl.BlockSpec(ng.shape, lambda b, i: (0, 0))],
        out_specs=cur(D_ATT),
        out_shape=jax.ShapeDtypeStruct((bsz, l, D_ATT), BF16),
        scratch_shapes=[pltpu.VMEM((nseq, ATT_BLOCK, D_ATT), F32)],
        compiler_params=pltpu.CompilerParams(dimension_semantics=("arbitrary", "arbitrary"),
                                             vmem_limit_bytes=VMEM_LIMIT),
    )(sinks, q, k, k, v, v, bias, ng)


def _out_proj_kernel(x_ref, ys_ref, ya_ref, g1_ref, sc_ref, sh_ref, g2_ref, ng_ref, wo_hbm, sg_hbm, su_hbm, sd_hbm,
                     rwh_ref, rwl_ref, rb_ref, upper_ref,
                     base_ref, h_ref, idx_ref, gate_ref, rank_ref, cnt_ref,
                     carry_ref, wo_raw, sg_raw, su_raw, sd_raw, wo_ref, sg_ref, su_ref, sd_ref, wsem):
    @pl.when(pl.program_id(0) == 0)
    def _():
        carry_ref[...] = jnp.zeros_like(carry_ref)
        staged = ((wo_hbm, wo_raw, wo_ref), (sg_hbm, sg_raw, sg_ref), (su_hbm, su_raw, su_ref),
                  (sd_hbm, sd_raw, sd_ref))
        copies = [pltpu.make_async_copy(src.at[0], raw, wsem.at[j]) for j, (src, raw, _) in enumerate(staged)]
        for cp in copies:
            cp.start()
        for cp, (_, raw, dst) in zip(copies, staged):
            cp.wait()
            dst[...] = raw[...].astype(BF16)

    mix = (jnp.dot(ys_ref[...], wo_ref[:D_SSM, :], preferred_element_type=F32)
           + jnp.dot(ya_ref[...], wo_ref[D_SSM:, :], preferred_element_type=F32))
    x1 = x_ref[...] + g1_ref[0] * mix
    ms = jnp.mean(x1 * x1, axis=-1, keepdims=True)
    h = x1 * lax.rsqrt(ms + EPS) * ng_ref[...]
    h = h * (1.0 + sc_ref[0]) + sh_ref[0]
    half = h.shape[1] // 2
    h_ref[...] = _pack_bf16_pair(h[:, :half], h[:, half:])
    hi, lo = _split_hi_lo(h)
    hi_terms = jnp.dot(hi, rwl_ref[...], preferred_element_type=F32)
    logits = (hi_terms[:, :LANES] + hi_terms[:, LANES:]
              + jnp.dot(lo, rwh_ref[...], preferred_element_type=F32))
    _route_tokens(logits.T[0:N_EXPERTS, :], rb_ref, upper_ref, idx_ref, gate_ref, rank_ref, cnt_ref, carry_ref)
    u = _silu(jnp.dot(hi, sg_ref[...], preferred_element_type=F32)) * jnp.dot(hi, su_ref[...],
                                                                              preferred_element_type=F32)
    shared = jnp.dot(u.astype(BF16), sd_ref[...], preferred_element_type=F32)
    base_ref[...] = x1 + g2_ref[0] * shared


def _out_proj_call(x2, ys, ya, g1, sc2, sh2, g2, ng, w_out, sg, su, sd, rwh, rwl, router_bias, upper,
                   tiles_per_batch, tm):
    n, d = x2.shape
    row = lambda w: pl.BlockSpec((tm, w), lambda i: (i, 0))
    tok = lambda r: pl.BlockSpec((r, tm), lambda i: (0, i))
    full = lambda a: pl.BlockSpec(a.shape, lambda i: (0, 0))
    per_batch = pl.BlockSpec((1, 1, d), lambda i: (i // tiles_per_batch, 0, 0))
    hbm = pl.BlockSpec(memory_space=pl.ANY)
    staged = (w_out, sg, su, sd)
    return pl.pallas_call(
        _out_proj_kernel,
        grid=(n // tm,),
        in_specs=[row(d), row(D_SSM), row(D_ATT), per_batch, per_batch, per_batch, per_batch, full(ng),
                  hbm, hbm, hbm, hbm, full(rwh), full(rwl), full(router_bias), full(upper)],
        out_specs=[row(d), row(d // 2), tok(TOP_K), tok(TOP_K), tok(TOP_K),
                   pl.BlockSpec((N_EXPERTS, LANES), lambda i: (0, 0))],
        out_shape=[jax.ShapeDtypeStruct((n, d), F32), jax.ShapeDtypeStruct((n, d // 2), jnp.uint32),
                   jax.ShapeDtypeStruct((TOP_K, n), jnp.int32), jax.ShapeDtypeStruct((TOP_K, n), F32),
                   jax.ShapeDtypeStruct((TOP_K, n), jnp.int32), jax.ShapeDtypeStruct((N_EXPERTS, LANES), F32)],
        scratch_shapes=([pltpu.VMEM((N_EXPERTS, LANES), F32)]
                        + [pltpu.VMEM(a.shape[1:], a.dtype) for a in staged]
                        + [pltpu.VMEM(a.shape[1:], BF16) for a in staged]
                        + [pltpu.SemaphoreType.DMA((len(staged),))]),
        compiler_params=pltpu.CompilerParams(dimension_semantics=("arbitrary",),
                                             vmem_limit_bytes=VMEM_LIMIT),
    )(x2, ys, ya, g1, sc2, sh2, g2, ng, w_out, sg, su, sd, rwh, rwl, router_bias, upper)


def _route_tokens(logits_t, rb_ref, upper_ref, idx_ref, gate_ref, rank_ref, cnt_ref, carry_ref):
    t = logits_t.shape[1]
    per_group = N_EXPERTS // ROUTE_GROUPS
    scores = 1.0 / (1.0 + jnp.exp(-logits_t))
    sel = scores + rb_ref[...]
    e_iota = lax.broadcasted_iota(jnp.int32, (N_EXPERTS, t), 0)

    sel3 = sel.reshape(ROUTE_GROUPS, per_group, t)
    w_iota = lax.broadcasted_iota(jnp.int32, sel3.shape, 1)
    m1 = jnp.max(sel3, axis=1, keepdims=True)
    first = jnp.min(jnp.where(sel3 == m1, w_iota, per_group), axis=1, keepdims=True)
    m2 = jnp.max(jnp.where(w_iota == first, NEG_INF, sel3), axis=1, keepdims=True)
    grp = (m1 + m2).reshape(ROUTE_GROUPS, t)

    g_iota = lax.broadcasted_iota(jnp.int32, (ROUTE_GROUPS, t), 0)
    gmask = jnp.zeros((ROUTE_GROUPS, t), jnp.bool_)
    for _ in range(ROUTE_TOPK_GROUPS):
        gm = jnp.max(grp, axis=0, keepdims=True)
        gfirst = jnp.min(jnp.where(grp == gm, g_iota, ROUTE_GROUPS), axis=0, keepdims=True)
        hit = g_iota == gfirst
        gmask = jnp.logical_or(gmask, hit)
        grp = jnp.where(hit, NEG_INF, grp)
    allowed = jnp.broadcast_to(gmask.reshape(ROUTE_GROUPS, 1, t),
                               (ROUTE_GROUPS, per_group, t)).reshape(N_EXPERTS, t)
    masked = jnp.where(allowed, sel, NEG_INF)

    picked = jnp.zeros((N_EXPERTS, t), jnp.bool_)
    idx_rows = []
    w_rows = []
    for _ in range(TOP_K):
        mm = jnp.max(masked, axis=0, keepdims=True)
        efirst = jnp.min(jnp.where(masked == mm, e_iota, N_EXPERTS), axis=0, keepdims=True)
        hit = e_iota == efirst
        idx_rows.append(efirst)
        w_rows.append(jnp.sum(jnp.where(hit, scores, 0.0), axis=0, keepdims=True))
        picked = jnp.logical_or(picked, hit)
        masked = jnp.where(hit, NEG_INF, masked)
    idx = jnp.concatenate(idx_rows, axis=0)
    w = jnp.concatenate(w_rows, axis=0)
    gate_ref[...] = w / jnp.sum(w, axis=0, keepdims=True) * ROUTED_SCALE
    idx_ref[...] = idx

    onehot = jnp.where(picked, 1.0, 0.0)
    sub = upper_ref.shape[0]
    carry = carry_ref[:, 0:1]
    parts = []
    for s0 in range(0, t, sub):
        oh = onehot[:, s0:s0 + sub]
        parts.append(jnp.dot(oh.astype(BF16), upper_ref[...], preferred_element_type=F32) + carry)
        carry = carry + jnp.sum(oh, axis=1, keepdims=True)
    rank_full = jnp.concatenate(parts, axis=1)
    rank_rows = [jnp.sum(jnp.where(e_iota == idx_rows[k], rank_full, 0.0), axis=0, keepdims=True)
                 for k in range(TOP_K)]
    rank_ref[...] = jnp.concatenate(rank_rows, axis=0).astype(jnp.int32)
    carry_ref[...] = jnp.broadcast_to(carry, carry_ref.shape)
    cnt_ref[...] = carry_ref[...]


PLAN_EXPERT, PLAN_FRESH, PLAN_VALID, PLAN_SEG, PLAN_NEXT, PLAN_NUSED = range(6)
PLAN_ROWS = SUBLANES


def _plan_kernel(cnt_ref, tri_ref, start_ref, plan_ref, *, nblocks):
    nbp = plan_ref.shape[1]
    cnt = cnt_ref[...].astype(jnp.int32)
    blocks = (cnt + (EXPERT_ROWS - 1)) // EXPERT_ROWS
    end = jnp.dot(tri_ref[...], blocks.astype(F32), precision=lax.Precision.HIGHEST,
                  preferred_element_type=F32).astype(jnp.int32)
    start = end - blocks
    start_ref[...] = start[:, 0:1] * EXPERT_ROWS
    nused = end[N_EXPERTS - 1:N_EXPERTS, 0:1]

    e_iota = lax.broadcasted_iota(jnp.int32, (N_EXPERTS, nbp), 0)
    blk = lax.broadcasted_iota(jnp.int32, (1, nbp), 1)
    expert = jnp.minimum(jnp.sum((end[:, 0:1] <= blk).astype(jnp.int32), axis=0, keepdims=True), N_EXPERTS - 1)
    mine = e_iota == expert
    pick = lambda col: jnp.sum(jnp.where(mine, col, 0), axis=0, keepdims=True)
    first = pick(start[:, 0:1])
    valid = jnp.clip(pick(cnt[:, 0:1]) - (blk - first) * EXPERT_ROWS, 0, EXPERT_ROWS)
    present = jnp.logical_or(blocks[:, 0:1] > 0,
                             jnp.logical_and(e_iota[:, 0:1] == N_EXPERTS - 1, nused < nblocks))
    seg = jnp.sum(jnp.logical_and(present, e_iota <= expert).astype(jnp.int32), axis=0, keepdims=True) - 1
    nxt = jnp.min(jnp.where(jnp.logical_and(present, e_iota > expert), e_iota, N_EXPERTS), axis=0, keepdims=True)
    rows = {PLAN_EXPERT: expert, PLAN_FRESH: (blk == first).astype(jnp.int32), PLAN_VALID: valid, PLAN_SEG: seg,
            PLAN_NEXT: jnp.where(nxt == N_EXPERTS, -1, nxt), PLAN_NUSED: jnp.broadcast_to(nused, (1, nbp))}
    zero = jnp.zeros((1, nbp), jnp.int32)
    plan_ref[...] = jnp.concatenate([rows.get(r, zero) for r in range(PLAN_ROWS)], axis=0)


def _plan_call(counts, nblocks):
    nbp = -(-nblocks // LANES) * LANES
    tri = jnp.asarray(np.tril(np.ones((N_EXPERTS, N_EXPERTS), np.float32)))
    return pl.pallas_call(
        functools.partial(_plan_kernel, nblocks=nblocks),
        out_shape=[jax.ShapeDtypeStruct((N_EXPERTS, 1), jnp.int32),
                   jax.ShapeDtypeStruct((PLAN_ROWS, nbp), jnp.int32)],
    )(counts, tri)


def _dest_kernel(idx_ref, rank_ref, start_ref, dest_ref):
    t = idx_ref.shape[1]
    e_iota = lax.broadcasted_iota(jnp.int32, (N_EXPERTS, t), 0)
    rows = [jnp.sum(jnp.where(e_iota == idx_ref[k:k + 1, :], start_ref[...], 0), axis=0, keepdims=True)
            for k in range(TOP_K)]
    dest = jnp.concatenate(rows, axis=0) + rank_ref[...]
    for k in range(TOP_K):
        for c in range(t // LANES):
            dest_ref[k, c:c + 1, :] = dest[k:k + 1, c * LANES:(c + 1) * LANES]


def _dest_call(idx, rank, pad_start, tile):
    n = idx.shape[1]
    tok = pl.BlockSpec((TOP_K, tile), lambda i: (0, i))
    return pl.pallas_call(
        _dest_kernel,
        grid=(n // tile,),
        in_specs=[tok, tok, pl.BlockSpec((N_EXPERTS, 1), lambda i: (0, 0))],
        out_specs=pl.BlockSpec((TOP_K, tile // LANES, LANES), lambda i: (0, i, 0)),
        out_shape=jax.ShapeDtypeStruct((TOP_K, n // LANES, LANES), jnp.int32),
        compiler_params=pltpu.CompilerParams(dimension_semantics=("arbitrary",)),
    )(idx, rank, pad_start)


def _scatter_rows_sc(rows, dest_flat, total_rows, chunk):
    n, w = rows.shape
    copies = dest_flat.shape[0] // n
    info = plsc.get_sparse_core_info()
    nc = info.num_cores
    per_worker = n // (nc * info.num_subcores)
    assert per_worker * nc * info.num_subcores == n and per_worker % chunk == 0
    mesh = plsc.VectorSubcoreMesh(core_axis_name="c", subcore_axis_name="s")

    @functools.partial(
        pl.kernel, mesh=mesh,
        out_type=jax.ShapeDtypeStruct((total_rows, w), rows.dtype),
        scratch_types=([pltpu.VMEM((chunk,), jnp.int32) for _ in range(copies)]
                       + [pltpu.VMEM((chunk, w), rows.dtype), pltpu.SemaphoreType.DMA, pltpu.SemaphoreType.DMA]),
    )
    def scatter(rows_hbm, idx_hbm, out_hbm, *scratch):
        idx_vs, (rows_v, isem, sem) = scratch[:copies], scratch[copies:]
        worker = lax.axis_index("s") * nc + lax.axis_index("c")

        @pl.loop(0, per_worker // chunk)
        def _(j):
            base = worker * per_worker + j * chunk
            loads = [pltpu.async_copy(idx_hbm.at[pl.ds(k * n + base, chunk)], idx_vs[k], isem)
                     for k in range(copies)]
            pltpu.sync_copy(rows_hbm.at[pl.ds(base, chunk)], rows_v)
            for ld in loads:
                ld.wait()
            stores = [pltpu.async_copy(rows_v, out_hbm.at[idx_vs[k]], sem) for k in range(copies)]
            for st in stores:
                st.wait()

    return scatter(rows, dest_flat)


X_RING = 3


def _expert_kernel(plan_ref, xs_hbm, wg_hbm, wu_hbm, wd_hbm, y_ref,
                   wgb_ref, wub_ref, wdb_ref, xbuf_ref, xsem, wg_raw, wu_raw, wd_raw, wsem):
    i = pl.program_id(0)
    rows, half = xbuf_ref.shape[1], xbuf_ref.shape[2]
    nused = plan_ref[PLAN_NUSED, 0]

    def w_copies(expert, slot):
        return [pltpu.make_async_copy(src.at[expert], dst.at[slot], wsem.at[slot])
                for src, dst in ((wg_hbm, wg_raw), (wu_hbm, wu_raw), (wd_hbm, wd_raw))]

    def x_copy(block):
        first = pl.multiple_of(block * rows, rows)
        slot = block % X_RING
        return pltpu.make_async_copy(xs_hbm.at[pl.ds(first, rows)], xbuf_ref.at[slot], xsem.at[slot])

    @pl.when(i == 0)
    def _():
        for b in range(X_RING - 1):
            @pl.when(b < nused)
            def _():
                x_copy(b).start()

    @pl.when(i + (X_RING - 1) < nused)
    def _():
        x_copy(i + (X_RING - 1)).start()

    @pl.when(i == 0)
    def _():
        for c in w_copies(plan_ref[PLAN_EXPERT, 0], 0):
            c.start()

    @pl.when(plan_ref[PLAN_FRESH, i] > 0)
    def _():
        slot = plan_ref[PLAN_SEG, i] % 2
        for c in w_copies(plan_ref[PLAN_EXPERT, i], slot):
            c.wait()

        @pl.when(plan_ref[PLAN_NEXT, i] >= 0)
        def _():
            for c in w_copies(plan_ref[PLAN_NEXT, i], 1 - slot):
                c.start()

        wgb_ref[...] = wg_raw[slot].astype(BF16)
        wub_ref[...] = wu_raw[slot].astype(BF16)
        wdb_ref[...] = wd_raw[slot].astype(BF16)

    @pl.when(i < nused)
    def _():
        x_copy(i).wait()
        xw = xbuf_ref[i % X_RING]
        row = lax.broadcasted_iota(jnp.int32, xw.shape, 0)
        x_lo, x_hi = _unpack_bf16_pair(jnp.where(row < plan_ref[PLAN_VALID, i], xw, jnp.uint32(0)))
        x_lo = x_lo.astype(BF16)
        x_hi = x_hi.astype(BF16)
        gate = (jnp.dot(x_lo, wgb_ref[:half, :], preferred_element_type=F32)
                + jnp.dot(x_hi, wgb_ref[half:, :], preferred_element_type=F32))
        up = (jnp.dot(x_lo, wub_ref[:half, :], preferred_element_type=F32)
              + jnp.dot(x_hi, wub_ref[half:, :], preferred_element_type=F32))
        u = (_silu(gate) * up).astype(BF16)
        y_lo = jnp.dot(u, wdb_ref[:, :half], preferred_element_type=F32)
        y_hi = jnp.dot(u, wdb_ref[:, half:], preferred_element_type=F32)
        y_ref[...] = _pack_bf16_pair(y_lo, y_hi)

    @pl.when(i >= nused)
    def _():
        y_ref[...] = jnp.zeros_like(y_ref)


def _expert_call(plan, xs, wg, wu, wd, rows):
    p, w = xs.shape
    d, f = wg.shape[1], wg.shape[2]
    hbm = pl.BlockSpec(memory_space=pl.ANY)
    grid_spec = pltpu.PrefetchScalarGridSpec(
        num_scalar_prefetch=1,
        grid=(p // rows,),
        in_specs=[hbm, hbm, hbm, hbm],
        out_specs=pl.BlockSpec((rows, w), lambda i, *_: (i, 0)),
        scratch_shapes=[pltpu.VMEM((d, f), BF16), pltpu.VMEM((d, f), BF16), pltpu.VMEM((f, d), BF16),
                        pltpu.VMEM((X_RING, rows, w), xs.dtype), pltpu.SemaphoreType.DMA((X_RING,)),
                        pltpu.VMEM((2, d, f), wg.dtype), pltpu.VMEM((2, d, f), wu.dtype),
                        pltpu.VMEM((2, f, d), wd.dtype), pltpu.SemaphoreType.DMA((2,))],
    )
    return pl.pallas_call(
        _expert_kernel,
        grid_spec=grid_spec,
        out_shape=jax.ShapeDtypeStruct((p, w), jnp.uint32),
        compiler_params=pltpu.CompilerParams(dimension_semantics=("arbitrary",),
                                             vmem_limit_bytes=VMEM_LIMIT),
    )(plan, xs, wg, wu, wd)


def _gather_rows_sc(table, idx, chunk):
    m = idx.shape[0]
    w = table.shape[1]
    info = plsc.get_sparse_core_info()
    nc = info.num_cores
    per_worker = m // (nc * info.num_subcores)
    assert per_worker * nc * info.num_subcores == m and per_worker % chunk == 0
    part = chunk // SC_GATHER_PARTS
    mesh = plsc.VectorSubcoreMesh(core_axis_name="c", subcore_axis_name="s")

    @functools.partial(
        pl.kernel, mesh=mesh,
        out_type=jax.ShapeDtypeStruct((m, w), table.dtype),
        scratch_types=([pltpu.VMEM((part,), jnp.int32) for _ in range(SC_GATHER_PARTS)]
                       + [pltpu.VMEM((part, w), table.dtype) for _ in range(SC_GATHER_PARTS)]
                       + [pltpu.SemaphoreType.DMA] * 3),
    )
    def gather(table_hbm, idx_hbm, out_hbm, *scratch):
        idx_vs = scratch[:SC_GATHER_PARTS]
        rows_vs = scratch[SC_GATHER_PARTS:2 * SC_GATHER_PARTS]
        isem, gsem, wsem = scratch[2 * SC_GATHER_PARTS:]
        worker = lax.axis_index("s") * nc + lax.axis_index("c")

        @pl.loop(0, per_worker // chunk)
        def _(j):
            base = worker * per_worker + j * chunk
            loads = [pltpu.async_copy(idx_hbm.at[pl.ds(base + p * part, part)], idx_vs[p], isem)
                     for p in range(SC_GATHER_PARTS)]
            gathers = []
            for p in range(SC_GATHER_PARTS):
                loads[p].wait()
                gathers.append(pltpu.async_copy(table_hbm.at[idx_vs[p]], rows_vs[p], gsem))
            writes = []
            for p in range(SC_GATHER_PARTS):
                gathers[p].wait()
                writes.append(pltpu.async_copy(rows_vs[p], out_hbm.at[pl.ds(base + p * part, part)], wsem))
            for wr in writes:
                wr.wait()

    return gather(table, idx)


def _combine_kernel(yk_ref, gate_ref, base_ref, g2_ref, fg_ref, o_ref):
    t = base_ref.shape[0]
    half = yk_ref.shape[2]
    gates = jnp.concatenate([gate_ref[...], jnp.zeros((LANES - TOP_K, t), F32)], axis=0).T
    r_lo = jnp.zeros((t, half), F32)
    r_hi = jnp.zeros((t, half), F32)
    for k in range(TOP_K):
        y_lo, y_hi = _unpack_bf16_pair(yk_ref[k])
        r_lo = r_lo + gates[:, k:k + 1] * y_lo
        r_hi = r_hi + gates[:, k:k + 1] * y_hi
    g2 = g2_ref[0]
    x_lo = base_ref[:, :half] + g2[:, :half] * r_lo
    x_hi = base_ref[:, half:] + g2[:, half:] * r_hi
    ms = (jnp.sum(x_lo * x_lo, axis=-1, keepdims=True)
          + jnp.sum(x_hi * x_hi, axis=-1, keepdims=True)) * (1.0 / (2 * half))
    inv = lax.rsqrt(ms + EPS)
    o_ref[:, :half] = x_lo * inv * fg_ref[:, :half]
    o_ref[:, half:] = x_hi * inv * fg_ref[:, half:]


def _combine_call(yk, gates, base, g2, fg, tiles_per_batch, tile, tile0):
    n, d = base.shape
    row = lambda w: pl.BlockSpec((tile, w), lambda i: (i + tile0, 0))
    return pl.pallas_call(
        _combine_kernel,
        grid=(yk.shape[1] // tile,),
        in_specs=[pl.BlockSpec((TOP_K, tile, yk.shape[2]), lambda i: (0, i, 0)),
                  pl.BlockSpec((TOP_K, tile), lambda i: (0, i + tile0)), row(d),
                  pl.BlockSpec((1, 1, d), lambda i: ((i + tile0) // tiles_per_batch, 0, 0)),
                  pl.BlockSpec((1, d), lambda i: (0, 0))],
        out_specs=row(d),
        out_shape=jax.ShapeDtypeStruct((n, d), F32),
        input_output_aliases={2: 0},
        compiler_params=pltpu.CompilerParams(dimension_semantics=("arbitrary",),
                                             vmem_limit_bytes=VMEM_LIMIT),
    )(yk, gates, base, g2, fg)


def _pad_cols(a, width):
    return jnp.pad(a, ((0, 0), (0, width - a.shape[1])))


def _layer(x, mod, norm1_g, norm2_g, w_in, conv_w, conv_b, dt_bias, a_log, d_skip, ssm_norm_g,
           att_norm_g, sinks, rel_bias, w_out, router_w, router_bias, exp_w_gate, exp_w_up, exp_w_down,
           sh_w_gate, sh_w_up, sh_w_down, final_g):
    bsz, l, d = x.shape
    n = bsz * l
    tm = min(ROW_TILE, l)

    sh1, sc1, g1, sh2, sc2, g2 = [m[:, None, :] for m in jnp.split(mod, 6, axis=-1)]

    assert math.frexp(ATT_HEAD_DIM ** -0.5)[0] == 0.5
    x2 = x.reshape(n, d)
    z, xbc, dt, q, k, v = _in_proj_call(x2, sc1, sh1, norm1_g[None, :], jnp.swapaxes(w_in, 1, 2), l // tm, tm)

    triu = jnp.asarray(np.triu(np.ones((CHUNK, CHUNK), np.float32))).astype(BF16)
    shift = jnp.asarray(_conv_shift_matrix()).astype(BF16)
    y_ssm = _ssd_call(xbc.reshape(bsz, l, CONV_CH), z.reshape(bsz, l, D_SSM), dt.reshape(bsz, l, LANES),
                      conv_w, conv_b[None, :], dt_bias[:, None], a_log[:, None],
                      jnp.repeat(d_skip, SSM_HEAD_DIM)[None, :], ssm_norm_g[None, :], triu, shift)

    bias = _bias_call(rel_bias, jnp.asarray(_rel_bucket_table()))
    y_att = _attn_call(sinks, q.reshape(bsz, l, D_ATT), k.reshape(bsz, l, D_KV), v.reshape(bsz, l, D_KV), bias,
                       att_norm_g[None, :])

    rw = _pad_cols(router_w, LANES)
    rwh = rw.astype(BF16)
    rwl = jnp.concatenate([rwh, (rw - rwh.astype(F32)).astype(BF16)], axis=1)
    rs = min(RANK_SUB, tm)
    upper = jnp.asarray(np.triu(np.ones((rs, rs), np.float32), 1)).astype(BF16)
    base, h2, idx, gates, rank, counts = _out_proj_call(
        x2, y_ssm.reshape(n, D_SSM), y_att.reshape(n, D_ATT), g1, sc2, sh2, g2, norm2_g[None, :],
        w_out, sh_w_gate, sh_w_up, sh_w_down, rwh, rwl, router_bias[:, None], upper, l // tm, tm)
    rt = min(ROUTE_TILE, n)

    nblocks = (n * TOP_K + N_EXPERTS * (EXPERT_ROWS - 1) + EXPERT_ROWS - 1) // EXPERT_ROWS
    pad_start, plan = _plan_call(counts, nblocks)
    dest = _dest_call(idx, rank, pad_start, rt)

    xs = _scatter_rows_sc(h2, dest.reshape(-1), nblocks * EXPERT_ROWS, SC_CHUNK)
    ys = _expert_call(plan, xs, exp_w_gate, exp_w_up, exp_w_down, EXPERT_ROWS)
    ctile = min(COMBINE_TILE, l)
    groups = COMBINE_GROUPS if bsz % COMBINE_GROUPS == 0 else 1
    ng = n // groups
    out = base
    for g in range(groups):
        idx_g = dest[:, g * ng // LANES:(g + 1) * ng // LANES, :].reshape(-1)
        yk = _gather_rows_sc(ys, idx_g, SC_CHUNK).reshape(TOP_K, ng, ys.shape[1])
        out = _combine_call(yk, gates, out, g2, final_g[None, :], l // ctile, ctile, g * ng // ctile)
    return out.reshape(bsz, l, d)


def kernel(x, c, mod_w, mod_b, norm1_g, norm2_g, w_in, conv_w, conv_b, dt_bias, a_log, d_skip, ssm_norm_g,
           att_norm_g, sinks, rel_bias, w_out, router_w, router_bias, exp_w_gate, exp_w_up, exp_w_down,
           sh_w_gate, sh_w_up, sh_w_down, final_g):
    assert mod_w.shape[0] == 1, "single-layer block"
    bsz = x.shape[0]
    c_pad = jnp.pad(c, ((0, SUBLANES - bsz % SUBLANES if bsz % SUBLANES else 0), (0, 0)))
    mod = _mod_call(c_pad, mod_w[0], mod_b[0][None, :])[:bsz]
    return _layer(x, mod, norm1_g[0], norm2_g[0], w_in, conv_w[0], conv_b[0], dt_bias[0], a_log[0], d_skip[0],
                  ssm_norm_g[0], att_norm_g[0], sinks[0], rel_bias, w_out, router_w[0], router_bias[0],
                  exp_w_gate[0], exp_w_up[0], exp_w_down[0], sh_w_gate, sh_w_up, sh_w_down, final_g)
```

```python
import functools
import math

import numpy as np
import jax
import jax.numpy as jnp
from jax import lax
from jax.experimental import pallas as pl
from jax.experimental.pallas import tpu as pltpu
from jax.experimental.pallas import tpu_sc as plsc

F32 = jnp.float32
BF16 = jnp.bfloat16

D_MODEL = 1024
SSM_HEAD_DIM = 64
D_SSM = D_MODEL
SSM_HEADS = D_SSM // SSM_HEAD_DIM
SSM_GROUPS = 4
D_STATE = 128
CONV_K = 4
CONV_CH = D_SSM + 2 * SSM_GROUPS * D_STATE
CHUNK = 128
ATT_HEAD_DIM = 64
D_ATT = D_MODEL
ATT_HEADS = D_ATT // ATT_HEAD_DIM
KV_HEADS = ATT_HEADS // 4
Q_PER_KV = ATT_HEADS // KV_HEADS
D_KV = KV_HEADS * ATT_HEAD_DIM
WINDOW = 128
ATT_BLOCK = 128
REL_BUCKETS = 32
REL_MAX_DIST = 128
N_EXPERTS = 64
TOP_K = 8
EXPERT_DIM = D_MODEL // 4
SHARED_DIM = D_MODEL // 4
ROUTE_GROUPS = 8
ROUTE_TOPK_GROUPS = 4
ROUTED_SCALE = 2.5
EPS = 1e-6

LANES = 128
SUBLANES = 8
HALF = LANES // 2

ROW_TILE = 512
ROUTE_TILE = 2048
RANK_SUB = 512
COMBINE_TILE = 512
SC_CHUNK = 128
SC_GATHER_PARTS = 4
EXPERT_ROWS = 512
COMBINE_GROUPS = 2
VMEM_LIMIT = 48 * 1024 * 1024

NEG_INF = float("-inf")


def _silu(v):
    return v * (1.0 / (1.0 + jnp.exp(-v)))


def _softplus(v):
    return jnp.maximum(v, 0.0) + jnp.log(1.0 + jnp.exp(-jnp.abs(v)))


def _split_hi_lo(v):
    hi = v.astype(BF16)
    lo = (v - hi.astype(F32)).astype(BF16)
    return hi, lo


def _pack_bf16_pair(a, b):
    w = pltpu.pack_elementwise([a, b], packed_dtype=BF16)
    return w if w.dtype == jnp.uint32 else lax.bitcast_convert_type(w, jnp.uint32)


def _unpack_bf16_pair(w):
    a = pltpu.unpack_elementwise(w, index=0, packed_dtype=BF16, unpacked_dtype=F32)
    b = pltpu.unpack_elementwise(w, index=1, packed_dtype=BF16, unpacked_dtype=F32)
    return a, b


def _lane_half_mask(shape):
    return lax.broadcasted_iota(jnp.int32, shape, len(shape) - 1) < HALF


def _mod_kernel(c_ref, w_ref, b_ref, o_ref):
    a = _silu(c_ref[...])
    o_ref[...] = jnp.dot(a, w_ref[...], precision=lax.Precision.HIGHEST,
                         preferred_element_type=F32) + b_ref[...]


def _mod_call(c_pad, mod_w, mod_b):
    rows, d = c_pad.shape
    cols = mod_w.shape[1]
    return pl.pallas_call(
        _mod_kernel,
        grid=(cols // d,),
        in_specs=[pl.BlockSpec((rows, d), lambda j: (0, 0)),
                  pl.BlockSpec((d, d), lambda j: (0, j)),
                  pl.BlockSpec((1, d), lambda j: (0, j))],
        out_specs=pl.BlockSpec((rows, d), lambda j: (0, j)),
        out_shape=jax.ShapeDtypeStruct((rows, cols), F32),
        compiler_params=pltpu.CompilerParams(dimension_semantics=("arbitrary",),
                                             vmem_limit_bytes=VMEM_LIMIT),
    )(c_pad, mod_w, mod_b)


IN_PROJ_SEGMENTS = ((D_SSM, BF16), (CONV_CH, BF16), (LANES, F32), (D_ATT, BF16), (D_KV, BF16), (D_KV, BF16))


def _in_proj_kernel(x_ref, sc_ref, sh_ref, g_ref, w_hbm, *refs):
    out_refs = refs[:len(IN_PROJ_SEGMENTS)]
    wraw_ref, w_ref, wsem = refs[len(IN_PROJ_SEGMENTS):]

    @pl.when(pl.program_id(0) == 0)
    def _():
        cp = pltpu.make_async_copy(w_hbm.at[0], wraw_ref, wsem)
        cp.start()
        cp.wait()
        src_dt = D_SSM + CONV_CH
        dst_q = src_dt + LANES
        src_q = src_dt + SSM_HEADS
        q_scale = ATT_HEAD_DIM ** -0.5

        def put(dst, src, scale=None):
            t = wraw_ref[src:src + LANES, :].T
            w_ref[:, dst:dst + LANES] = (t if scale is None else t * scale).astype(BF16)

        for c0 in range(0, src_dt, LANES):
            put(c0, c0)
        dt_tile = wraw_ref[src_dt:src_dt + LANES, :].T
        lane = lax.broadcasted_iota(jnp.int32, dt_tile.shape, 1)
        w_ref[:, src_dt:dst_q] = jnp.where(lane < SSM_HEADS, dt_tile, 0.0).astype(BF16)
        for c0 in range(0, D_ATT, LANES):
            put(dst_q + c0, src_q + c0, q_scale)
        for c0 in range(D_ATT, D_ATT + 2 * D_KV, LANES):
            put(dst_q + c0, src_q + c0)

    xf = x_ref[...]
    ms = jnp.mean(xf * xf, axis=-1, keepdims=True)
    h = xf * lax.rsqrt(ms + EPS) * g_ref[...]
    h = h * (1.0 + sc_ref[0]) + sh_ref[0]
    hb = h.astype(BF16)
    col = 0
    for (width, dtype), o_ref in zip(IN_PROJ_SEGMENTS, out_refs):
        o_ref[...] = jnp.dot(hb, w_ref[:, col:col + width], preferred_element_type=F32).astype(dtype)
        col += width


def _in_proj_call(x2, sc1, sh1, g1n, w_in_t, tiles_per_batch, tm):
    n, d = x2.shape
    row = lambda w: pl.BlockSpec((tm, w), lambda i: (i, 0))
    full = lambda a: pl.BlockSpec(a.shape, lambda i: (0, 0))
    per_batch = pl.BlockSpec((1, 1, d), lambda i: (i // tiles_per_batch, 0, 0))
    cols = sum(w for w, _ in IN_PROJ_SEGMENTS)
    w_in = w_in_t
    assert w_in.shape == (1, cols - (LANES - SSM_HEADS), d)
    return pl.pallas_call(
        _in_proj_kernel,
        grid=(n // tm,),
        in_specs=[row(d), per_batch, per_batch, full(g1n), pl.BlockSpec(memory_space=pl.ANY)],
        out_specs=[row(w) for w, _ in IN_PROJ_SEGMENTS],
        out_shape=[jax.ShapeDtypeStruct((n, w), dt) for w, dt in IN_PROJ_SEGMENTS],
        scratch_shapes=[pltpu.VMEM(w_in.shape[1:], w_in.dtype), pltpu.VMEM((d, cols), BF16), pltpu.SemaphoreType.DMA(())],
        compiler_params=pltpu.CompilerParams(dimension_semantics=("arbitrary",),
                                             vmem_limit_bytes=VMEM_LIMIT),
    )(x2, sc1, sh1, g1n, w_in)


SSD_SEQS = 4
CONV_HALO = 16


def _conv_shift_matrix():
    s = np.zeros((CONV_K * CHUNK, CONV_HALO + CHUNK), np.float32)
    for k in range(CONV_K):
        t = np.arange(CHUNK)
        s[k * CHUNK + t, CONV_HALO + t - (CONV_K - 1) + k] = 1.0
    return s


def _silu_tanh(v):
    hv = 0.5 * v
    return hv + hv * jnp.tanh(hv)


def _ssd_kernel(xbc_ref, z_ref, dt_ref, cw_ref, cb_ref, dtb_ref, alog_ref, dskip_ref, ng_ref, triu_ref, shift_ref,
                y_ref, state_ref, ucat_ref, ybuf_ref):
    nseq = xbc_ref.shape[0]

    @pl.when(pl.program_id(1) == 0)
    def _():
        state_ref[...] = jnp.zeros_like(state_ref)
        ucat_ref[:, 0:CONV_HALO, :] = jnp.zeros((nseq, CONV_HALO, CONV_CH), BF16)

    for q in range(nseq):
        _ssd_chunk(xbc_ref.at[q], z_ref.at[q], dt_ref.at[q], cw_ref, cb_ref, dtb_ref, alog_ref, dskip_ref, ng_ref,
                   triu_ref, shift_ref, y_ref.at[q], state_ref.at[q], ucat_ref.at[q], ybuf_ref.at[q])


def _ssd_chunk(xbc_ref, z_ref, dt_ref, cw_ref, cb_ref, dtb_ref, alog_ref, dskip_ref, ng_ref, triu_ref, shift_ref,
               y_ref, state_ref, ucat_ref, ybuf_ref):
    ucat_ref[CONV_HALO:, :] = xbc_ref[...]
    shifted = jnp.dot(shift_ref[...], ucat_ref[...], preferred_element_type=F32)
    ucat_ref[0:CONV_HALO, :] = ucat_ref[CHUNK:CHUNK + CONV_HALO, :]
    acc = cb_ref[...] + cw_ref[0:1, :] * shifted[0:CHUNK]
    for kk in range(1, CONV_K):
        acc = acc + cw_ref[kk:kk + 1, :] * shifted[kk * CHUNK:(kk + 1) * CHUNK]
    act = _silu_tanh(acc)
    xs = act[:, :D_SSM]
    gn = SSM_GROUPS * D_STATE

    dt_t = _softplus(dt_ref[...].T[0:SSM_HEADS, :] + dtb_ref[...])
    a_t = dt_t * (-jnp.exp(alog_ref[...]))
    a_hi = a_t.astype(BF16)
    a_mid = (a_t - a_hi.astype(F32)).astype(BF16)
    a_lo = (a_t - a_hi.astype(F32) - a_mid.astype(F32)).astype(BF16)
    triu = triu_ref[...]
    cs_t = (jnp.dot(a_hi, triu, preferred_element_type=F32) + jnp.dot(a_mid, triu, preferred_element_type=F32)
            + jnp.dot(a_lo, triu, preferred_element_type=F32))
    cs_end = cs_t[:, CHUNK - 1:CHUNK]
    r_t = cs_t - jnp.log(dt_t)
    w_t = jnp.exp(cs_end - cs_t) * dt_t
    chunk_decay = jnp.exp(cs_end)
    cols = jnp.concatenate([cs_t, jnp.exp(cs_t), jnp.zeros((LANES - 2 * SSM_HEADS, CHUNK), F32)], axis=0).T

    li = lax.broadcasted_iota(jnp.int32, (CHUNK, CHUNK), 0)
    si = lax.broadcasted_iota(jnp.int32, (CHUNK, CHUNK), 1)
    causal = li >= si
    low = _lane_half_mask((CHUNK, LANES))
    low_row = _lane_half_mask((1, LANES))

    heads_per_group = SSM_HEADS // SSM_GROUPS
    for g in range(SSM_GROUPS):
        b_g = act[:, D_SSM + g * D_STATE:D_SSM + (g + 1) * D_STATE]
        c_g = act[:, D_SSM + gn + g * D_STATE:D_SSM + gn + (g + 1) * D_STATE]
        b_gb = b_g.astype(BF16)
        c_gb = c_g.astype(BF16)
        cb = lax.dot_general(c_gb, b_gb, (((1,), (1,)), ((), ())), preferred_element_type=F32)
        b_t = b_g.T
        for jp in range(heads_per_group // 2):
            j = g * (heads_per_group // 2) + jp
            lanes = slice(j * LANES, (j + 1) * LANES)
            xp = xs[:, lanes]
            xpb = xp.astype(BF16)
            ydiag = jnp.zeros((CHUNK, LANES), F32)
            snew = jnp.zeros((D_STATE, LANES), F32)
            for half in range(2):
                h = 2 * j + half
                diff = cols[:, h:h + 1] - r_t[h:h + 1, :]
                m = (cb * jnp.exp(jnp.where(causal, diff, NEG_INF))).astype(BF16)
                keep = low if half == 0 else jnp.logical_not(low)
                xh = jnp.where(keep, xpb, jnp.zeros_like(xpb))
                ydiag = ydiag + jnp.dot(m, xh, preferred_element_type=F32)
                snew = snew + jnp.dot((b_t * w_t[h:h + 1, :]).astype(BF16), xh, preferred_element_type=F32)
            s_in = state_ref[:, lanes]
            yoff = jnp.dot(c_gb, s_in.astype(BF16), preferred_element_type=F32)
            h0 = 2 * j
            e0 = SSM_HEADS + h0
            escale = jnp.where(low, cols[:, e0:e0 + 1], cols[:, e0 + 1:e0 + 2])
            cdec = jnp.where(low_row, chunk_decay[h0:h0 + 1, :], chunk_decay[h0 + 1:h0 + 2, :])
            ybuf_ref[:, lanes] = ydiag + yoff * escale + xp * dskip_ref[:, lanes]
            state_ref[:, lanes] = s_in * cdec + snew

    yz = ybuf_ref[...] * _silu_tanh(z_ref[...].astype(F32))
    gw = D_SSM // SSM_GROUPS
    for g in range(SSM_GROUPS):
        part = yz[:, g * gw:(g + 1) * gw]
        ms = jnp.mean(part * part, axis=-1, keepdims=True)
        y_ref[:, g * gw:(g + 1) * gw] = (part * lax.rsqrt(ms + EPS)
                                            * ng_ref[:, g * gw:(g + 1) * gw]).astype(BF16)


def _ssd_call(xbc, z, dt, conv_w, conv_b, dtb, alog, dskip, ng, triu, shift):
    bsz, l, _ = xbc.shape
    nc = l // CHUNK
    nseq = SSD_SEQS if bsz % SSD_SEQS == 0 else 1
    chunk = lambda w: pl.BlockSpec((nseq, CHUNK, w), lambda b, c: (b, c, 0))
    full = lambda a: pl.BlockSpec(a.shape, lambda b, c: (0, 0))
    return pl.pallas_call(
        _ssd_kernel,
        grid=(bsz // nseq, nc),
        in_specs=[chunk(CONV_CH), chunk(D_SSM), chunk(LANES), full(conv_w), full(conv_b), full(dtb),
                  full(alog), full(dskip), full(ng), full(triu), full(shift)],
        out_specs=chunk(D_SSM),
        out_shape=jax.ShapeDtypeStruct((bsz, l, D_SSM), BF16),
        scratch_shapes=[pltpu.VMEM((nseq, D_STATE, D_SSM), F32),
                        pltpu.VMEM((nseq, CONV_HALO + CHUNK, CONV_CH), BF16),
                        pltpu.VMEM((nseq, CHUNK, D_SSM), F32)],
        compiler_params=pltpu.CompilerParams(dimension_semantics=("arbitrary", "arbitrary"),
                                             vmem_limit_bytes=VMEM_LIMIT),
    )(xbc, z, dt, conv_w, conv_b, dtb, alog, dskip, ng, triu, shift)


assert WINDOW == ATT_BLOCK


def _rel_bucket_table():
    qi = np.arange(ATT_BLOCK)[:, None]
    c = np.arange(ATT_BLOCK)[None, :]
    dist = np.where(c > qi, qi + ATT_BLOCK - c, qi - c)
    max_exact = REL_BUCKETS // 2
    d = np.maximum(dist, 1).astype(np.float32)
    large = max_exact + (np.log(d / np.float32(max_exact)) / np.float32(math.log(REL_MAX_DIST / max_exact))
                         * np.float32(REL_BUCKETS - max_exact)).astype(np.int32)
    large = np.minimum(large, REL_BUCKETS - 1)
    return np.where(dist < max_exact, dist, large).astype(np.int32)


def _bias_kernel(rb_ref, bucket_ref, o_ref):
    bucket = bucket_ref[...]
    from_prev = (lax.broadcasted_iota(jnp.int32, bucket.shape, 1)
                 > lax.broadcasted_iota(jnp.int32, bucket.shape, 0))
    for h in range(ATT_HEADS):
        acc = jnp.zeros(bucket.shape, F32)
        for b in range(REL_BUCKETS):
            acc = jnp.where(bucket == b, rb_ref[b, h], acc)
        o_ref[1, h] = acc
        o_ref[0, h] = jnp.where(from_prev, NEG_INF, acc)


def _bias_call(rel_bias, bucket):
    return pl.pallas_call(
        _bias_kernel,
        in_specs=[pl.BlockSpec(memory_space=pltpu.SMEM), pl.BlockSpec(memory_space=pltpu.VMEM)],
        out_shape=jax.ShapeDtypeStruct((2, ATT_HEADS) + bucket.shape, F32),
    )(rel_bias, bucket)


ATT_SEQS = 4


def _attn_kernel(sink_ref, q_ref, kp_ref, kc_ref, vp_ref, vc_ref, bias_ref, ng_ref, o_ref, obuf_ref):
    for s in range(q_ref.shape[0]):
        _attn_block(sink_ref, q_ref.at[s], kp_ref.at[s], kc_ref.at[s], vp_ref.at[s], vc_ref.at[s], bias_ref.at[0],
                    ng_ref, o_ref.at[s], obuf_ref.at[s])


def _attn_block(sink_ref, q_ref, kp_ref, kc_ref, vp_ref, vc_ref, bias_ref, ng_ref, o_ref, obuf_ref):
    qi = lax.broadcasted_iota(jnp.int32, (ATT_BLOCK, ATT_BLOCK), 0)
    ci = lax.broadcasted_iota(jnp.int32, (ATT_BLOCK, ATT_BLOCK), 1)
    from_prev = ci > qi
    low = _lane_half_mask((ATT_BLOCK, LANES))

    def band_variants(prev_ref, cur_ref):
        out = []
        for cpair in range(KV_HEADS // 2):
            lanes = slice(cpair * LANES, (cpair + 1) * LANES)
            t = jnp.concatenate([prev_ref[:, lanes], cur_ref[:, lanes]], axis=0).astype(F32)
            out.append((t.astype(BF16), pltpu.roll(t, HALF, 1).astype(BF16)))
        return out

    k_band = band_variants(kp_ref, kc_ref)
    v_band = band_variants(vp_ref, vc_ref)
    nt = (((1,), (1,)), ((), ()))

    for j in range(ATT_HEADS // 2):
        qp = q_ref[:, j * LANES:(j + 1) * LANES]
        out_pair = jnp.zeros((ATT_BLOCK, LANES), F32)
        for half in range(2):
            h = 2 * j + half
            g = h // Q_PER_KV
            swapped = int((g % 2) != half)
            keep = low if half == 0 else jnp.logical_not(low)
            qh = jnp.where(keep, qp, jnp.zeros_like(qp))
            s_band = lax.dot_general(qh, k_band[g // 2][swapped], nt, preferred_element_type=F32)
            s = jnp.where(from_prev, s_band[:, :ATT_BLOCK], s_band[:, ATT_BLOCK:]) + bias_ref[h]
            sink = sink_ref[h]
            m = jnp.maximum(jnp.max(s, axis=-1, keepdims=True), sink)
            p = jnp.exp(s - m)
            denom = jnp.sum(p, axis=-1, keepdims=True) + jnp.exp(sink - m)
            p_band = jnp.concatenate([jnp.where(from_prev, p, 0.0), jnp.where(from_prev, 0.0, p)], axis=1)
            o = jnp.dot(p_band.astype(BF16), v_band[g // 2][swapped], preferred_element_type=F32) / denom
            out_pair = out_pair + jnp.where(keep, o, 0.0)
        obuf_ref[:, j * LANES:(j + 1) * LANES] = out_pair

    att = obuf_ref[...]
    ms = jnp.mean(att * att, axis=-1, keepdims=True)
    o_ref[...] = (att * lax.rsqrt(ms + EPS) * ng_ref[...]).astype(BF16)


def _attn_call(sinks, q, k, v, bias, ng):
    bsz, l, _ = q.shape
    nb = l // ATT_BLOCK
    nseq = ATT_SEQS if bsz % ATT_SEQS == 0 else 1
    cur = lambda w: pl.BlockSpec((nseq, ATT_BLOCK, w), lambda b, i: (b, i, 0))
    prev = lambda w: pl.BlockSpec((nseq, ATT_BLOCK, w), lambda b, i: (b, jnp.maximum(i - 1, 0), 0))
    return pl.pallas_call(
        _attn_kernel,
        grid=(bsz // nseq, nb),
        in_specs=[pl.BlockSpec(memory_space=pltpu.SMEM),
                  cur(D_ATT), prev(D_KV), cur(D_KV), prev(D_KV), cur(D_KV),
                  pl.BlockSpec((1,) + bias.shape[1:], lambda b, i: (jnp.minimum(i, 1), 0, 0, 0)),
                  pl.BlockSpec(ng.shape, lambda b, i: (0, 0))],
        out_specs=cur(D_ATT),
        out_shape=jax.ShapeDtypeStruct((bsz, l, D_ATT), BF16),
        scratch_shapes=[pltpu.VMEM((nseq, ATT_BLOCK, D_ATT), F32)],
        compiler_params=pltpu.CompilerParams(dimension_semantics=("arbitrary", "arbitrary"),
                                             vmem_limit_bytes=VMEM_LIMIT),
    )(sinks, q, k, k, v, v, bias, ng)


def _out_proj_kernel(x_ref, ys_ref, ya_ref, g1_ref, sc_ref, sh_ref, g2_ref, ng_ref, wo_hbm, sg_hbm, su_hbm, sd_hbm,
                     rwh_ref, rwl_ref, rb_ref, upper_ref,
                     base_ref, h_ref, idx_ref, gate_ref, rank_ref, cnt_ref,
                     carry_ref, wo_raw, sg_raw, su_raw, sd_raw, wo_ref, sg_ref, su_ref, sd_ref, wsem):
    @pl.when(pl.program_id(0) == 0)
    def _():
        carry_ref[...] = jnp.zeros_like(carry_ref)
        staged = ((wo_hbm, wo_raw, wo_ref), (sg_hbm, sg_raw, sg_ref), (su_hbm, su_raw, su_ref),
                  (sd_hbm, sd_raw, sd_ref))
        copies = [pltpu.make_async_copy(src.at[0], raw, wsem.at[j]) for j, (src, raw, _) in enumerate(staged)]
        for cp in copies:
            cp.start()
        for cp, (_, raw, dst) in zip(copies, staged):
            cp.wait()
            dst[...] = raw[...].astype(BF16)

    mix = (jnp.dot(ys_ref[...], wo_ref[:D_SSM, :], preferred_element_type=F32)
           + jnp.dot(ya_ref[...], wo_ref[D_SSM:, :], preferred_element_type=F32))
    x1 = x_ref[...] + g1_ref[0] * mix
    ms = jnp.mean(x1 * x1, axis=-1, keepdims=True)
    h = x1 * lax.rsqrt(ms + EPS) * ng_ref[...]
    h = h * (1.0 + sc_ref[0]) + sh_ref[0]
    half = h.shape[1] // 2
    h_ref[...] = _pack_bf16_pair(h[:, :half], h[:, half:])
    hi, lo = _split_hi_lo(h)
    hi_terms = jnp.dot(hi, rwl_ref[...], preferred_element_type=F32)
    logits = (hi_terms[:, :LANES] + hi_terms[:, LANES:]
              + jnp.dot(lo, rwh_ref[...], preferred_element_type=F32))
    _route_tokens(logits.T[0:N_EXPERTS, :], rb_ref, upper_ref, idx_ref, gate_ref, rank_ref, cnt_ref, carry_ref)
    u = _silu(jnp.dot(hi, sg_ref[...], preferred_element_type=F32)) * jnp.dot(hi, su_ref[...],
                                                                              preferred_element_type=F32)
    shared = jnp.dot(u.astype(BF16), sd_ref[...], preferred_element_type=F32)
    base_ref[...] = x1 + g2_ref[0] * shared


def _out_proj_call(x2, ys, ya, g1, sc2, sh2, g2, ng, w_out, sg, su, sd, rwh, rwl, router_bias, upper,
                   tiles_per_batch, tm):
    n, d = x2.shape
    row = lambda w: pl.BlockSpec((tm, w), lambda i: (i, 0))
    tok = lambda r: pl.BlockSpec((r, tm), lambda i: (0, i))
    full = lambda a: pl.BlockSpec(a.shape, lambda i: (0, 0))
    per_batch = pl.BlockSpec((1, 1, d), lambda i: (i // tiles_per_batch, 0, 0))
    hbm = pl.BlockSpec(memory_space=pl.ANY)
    staged = (w_out, sg, su, sd)
    return pl.pallas_call(
        _out_proj_kernel,
        grid=(n // tm,),
        in_specs=[row(d), row(D_SSM), row(D_ATT), per_batch, per_batch, per_batch, per_batch, full(ng),
                  hbm, hbm, hbm, hbm, full(rwh), full(rwl), full(router_bias), full(upper)],
        out_specs=[row(d), row(d // 2), tok(TOP_K), tok(TOP_K), tok(TOP_K),
                   pl.BlockSpec((N_EXPERTS, LANES), lambda i: (0, 0))],
        out_shape=[jax.ShapeDtypeStruct((n, d), F32), jax.ShapeDtypeStruct((n, d // 2), jnp.uint32),
                   jax.ShapeDtypeStruct((TOP_K, n), jnp.int32), jax.ShapeDtypeStruct((TOP_K, n), F32),
                   jax.ShapeDtypeStruct((TOP_K, n), jnp.int32), jax.ShapeDtypeStruct((N_EXPERTS, LANES), F32)],
        scratch_shapes=([pltpu.VMEM((N_EXPERTS, LANES), F32)]
                        + [pltpu.VMEM(a.shape[1:], a.dtype) for a in staged]
                        + [pltpu.VMEM(a.shape[1:], BF16) for a in staged]
                        + [pltpu.SemaphoreType.DMA((len(staged),))]),
        compiler_params=pltpu.CompilerParams(dimension_semantics=("arbitrary",),
                                             vmem_limit_bytes=VMEM_LIMIT),
    )(x2, ys, ya, g1, sc2, sh2, g2, ng, w_out, sg, su, sd, rwh, rwl, router_bias, upper)


def _route_tokens(logits_t, rb_ref, upper_ref, idx_ref, gate_ref, rank_ref, cnt_ref, carry_ref):
    t = logits_t.shape[1]
    per_group = N_EXPERTS // ROUTE_GROUPS
    scores = 1.0 / (1.0 + jnp.exp(-logits_t))
    sel = scores + rb_ref[...]
    e_iota = lax.broadcasted_iota(jnp.int32, (N_EXPERTS, t), 0)

    sel3 = sel.reshape(ROUTE_GROUPS, per_group, t)
    w_iota = lax.broadcasted_iota(jnp.int32, sel3.shape, 1)
    m1 = jnp.max(sel3, axis=1, keepdims=True)
    first = jnp.min(jnp.where(sel3 == m1, w_iota, per_group), axis=1, keepdims=True)
    m2 = jnp.max(jnp.where(w_iota == first, NEG_INF, sel3), axis=1, keepdims=True)
    grp = (m1 + m2).reshape(ROUTE_GROUPS, t)

    g_iota = lax.broadcasted_iota(jnp.int32, (ROUTE_GROUPS, t), 0)
    gmask = jnp.zeros((ROUTE_GROUPS, t), jnp.bool_)
    for _ in range(ROUTE_TOPK_GROUPS):
        gm = jnp.max(grp, axis=0, keepdims=True)
        gfirst = jnp.min(jnp.where(grp == gm, g_iota, ROUTE_GROUPS), axis=0, keepdims=True)
        hit = g_iota == gfirst
        gmask = jnp.logical_or(gmask, hit)
        grp = jnp.where(hit, NEG_INF, grp)
    allowed = jnp.broadcast_to(gmask.reshape(ROUTE_GROUPS, 1, t),
                               (ROUTE_GROUPS, per_group, t)).reshape(N_EXPERTS, t)
    masked = jnp.where(allowed, sel, NEG_INF)

    picked = jnp.zeros((N_EXPERTS, t), jnp.bool_)
    idx_rows = []
    w_rows = []
    for _ in range(TOP_K):
        mm = jnp.max(masked, axis=0, keepdims=True)
        efirst = jnp.min(jnp.where(masked == mm, e_iota, N_EXPERTS), axis=0, keepdims=True)
        hit = e_iota == efirst
        idx_rows.append(efirst)
        w_rows.append(jnp.sum(jnp.where(hit, scores, 0.0), axis=0, keepdims=True))
        picked = jnp.logical_or(picked, hit)
        masked = jnp.where(hit, NEG_INF, masked)
    idx = jnp.concatenate(idx_rows, axis=0)
    w = jnp.concatenate(w_rows, axis=0)
    gate_ref[...] = w / jnp.sum(w, axis=0, keepdims=True) * ROUTED_SCALE
    idx_ref[...] = idx

    onehot = jnp.where(picked, 1.0, 0.0)
    sub = upper_ref.shape[0]
    carry = carry_ref[:, 0:1]
    parts = []
    for s0 in range(0, t, sub):
        oh = onehot[:, s0:s0 + sub]
        parts.append(jnp.dot(oh.astype(BF16), upper_ref[...], preferred_element_type=F32) + carry)
        carry = carry + jnp.sum(oh, axis=1, keepdims=True)
    rank_full = jnp.concatenate(parts, axis=1)
    rank_rows = [jnp.sum(jnp.where(e_iota == idx_rows[k], rank_full, 0.0), axis=0, keepdims=True)
                 for k in range(TOP_K)]
    rank_ref[...] = jnp.concatenate(rank_rows, axis=0).astype(jnp.int32)
    carry_ref[...] = jnp.broadcast_to(carry, carry_ref.shape)
    cnt_ref[...] = carry_ref[...]


PLAN_EXPERT, PLAN_FRESH, PLAN_VALID, PLAN_SEG, PLAN_NEXT, PLAN_NUSED = range(6)
PLAN_ROWS = SUBLANES


def _plan_kernel(cnt_ref, tri_ref, start_ref, plan_ref, *, nblocks):
    nbp = plan_ref.shape[1]
    cnt = cnt_ref[...].astype(jnp.int32)
    blocks = (cnt + (EXPERT_ROWS - 1)) // EXPERT_ROWS
    end = jnp.dot(tri_ref[...], blocks.astype(F32), precision=lax.Precision.HIGHEST,
                  preferred_element_type=F32).astype(jnp.int32)
    start = end - blocks
    start_ref[...] = start[:, 0:1] * EXPERT_ROWS
    nused = end[N_EXPERTS - 1:N_EXPERTS, 0:1]

    e_iota = lax.broadcasted_iota(jnp.int32, (N_EXPERTS, nbp), 0)
    blk = lax.broadcasted_iota(jnp.int32, (1, nbp), 1)
    expert = jnp.minimum(jnp.sum((end[:, 0:1] <= blk).astype(jnp.int32), axis=0, keepdims=True), N_EXPERTS - 1)
    mine = e_iota == expert
    pick = lambda col: jnp.sum(jnp.where(mine, col, 0), axis=0, keepdims=True)
    first = pick(start[:, 0:1])
    valid = jnp.clip(pick(cnt[:, 0:1]) - (blk - first) * EXPERT_ROWS, 0, EXPERT_ROWS)
    present = jnp.logical_or(blocks[:, 0:1] > 0,
                             jnp.logical_and(e_iota[:, 0:1] == N_EXPERTS - 1, nused < nblocks))
    seg = jnp.sum(jnp.logical_and(present, e_iota <= expert).astype(jnp.int32), axis=0, keepdims=True) - 1
    nxt = jnp.min(jnp.where(jnp.logical_and(present, e_iota > expert), e_iota, N_EXPERTS), axis=0, keepdims=True)
    rows = {PLAN_EXPERT: expert, PLAN_FRESH: (blk == first).astype(jnp.int32), PLAN_VALID: valid, PLAN_SEG: seg,
            PLAN_NEXT: jnp.where(nxt == N_EXPERTS, -1, nxt), PLAN_NUSED: jnp.broadcast_to(nused, (1, nbp))}
    zero = jnp.zeros((1, nbp), jnp.int32)
    plan_ref[...] = jnp.concatenate([rows.get(r, zero) for r in range(PLAN_ROWS)], axis=0)


def _plan_call(counts, nblocks):
    nbp = -(-nblocks // LANES) * LANES
    tri = jnp.asarray(np.tril(np.ones((N_EXPERTS, N_EXPERTS), np.float32)))
    return pl.pallas_call(
        functools.partial(_plan_kernel, nblocks=nblocks),
        out_shape=[jax.ShapeDtypeStruct((N_EXPERTS, 1), jnp.int32),
                   jax.ShapeDtypeStruct((PLAN_ROWS, nbp), jnp.int32)],
    )(counts, tri)


def _dest_kernel(idx_ref, rank_ref, start_ref, dest_ref):
    t = idx_ref.shape[1]
    e_iota = lax.broadcasted_iota(jnp.int32, (N_EXPERTS, t), 0)
    rows = [jnp.sum(jnp.where(e_iota == idx_ref[k:k + 1, :], start_ref[...], 0), axis=0, keepdims=True)
            for k in range(TOP_K)]
    dest = jnp.concatenate(rows, axis=0) + rank_ref[...]
    for k in range(TOP_K):
        for c in range(t // LANES):
            dest_ref[k, c:c + 1, :] = dest[k:k + 1, c * LANES:(c + 1) * LANES]


def _dest_call(idx, rank, pad_start, tile):
    n = idx.shape[1]
    tok = pl.BlockSpec((TOP_K, tile), lambda i: (0, i))
    return pl.pallas_call(
        _dest_kernel,
        grid=(n // tile,),
        in_specs=[tok, tok, pl.BlockSpec((N_EXPERTS, 1), lambda i: (0, 0))],
        out_specs=pl.BlockSpec((TOP_K, tile // LANES, LANES), lambda i: (0, i, 0)),
        out_shape=jax.ShapeDtypeStruct((TOP_K, n // LANES, LANES), jnp.int32),
        compiler_params=pltpu.CompilerParams(dimension_semantics=("arbitrary",)),
    )(idx, rank, pad_start)


def _scatter_rows_sc(rows, dest_flat, total_rows, chunk):
    n, w = rows.shape
    copies = dest_flat.shape[0] // n
    info = plsc.get_sparse_core_info()
    nc = info.num_cores
    per_worker = n // (nc * info.num_subcores)
    assert per_worker * nc * info.num_subcores == n and per_worker % chunk == 0
    mesh = plsc.VectorSubcoreMesh(core_axis_name="c", subcore_axis_name="s")

    @functools.partial(
        pl.kernel, mesh=mesh,
        out_type=jax.ShapeDtypeStruct((total_rows, w), rows.dtype),
        scratch_types=([pltpu.VMEM((chunk,), jnp.int32) for _ in range(copies)]
                       + [pltpu.VMEM((chunk, w), rows.dtype), pltpu.SemaphoreType.DMA, pltpu.SemaphoreType.DMA]),
    )
    def scatter(rows_hbm, idx_hbm, out_hbm, *scratch):
        idx_vs, (rows_v, isem, sem) = scratch[:copies], scratch[copies:]
        worker = lax.axis_index("s") * nc + lax.axis_index("c")

        @pl.loop(0, per_worker // chunk)
        def _(j):
            base = worker * per_worker + j * chunk
            loads = [pltpu.async_copy(idx_hbm.at[pl.ds(k * n + base, chunk)], idx_vs[k], isem)
                     for k in range(copies)]
            pltpu.sync_copy(rows_hbm.at[pl.ds(base, chunk)], rows_v)
            for ld in loads:
                ld.wait()
            stores = [pltpu.async_copy(rows_v, out_hbm.at[idx_vs[k]], sem) for k in range(copies)]
            for st in stores:
                st.wait()

    return scatter(rows, dest_flat)


X_RING = 3


def _expert_kernel(plan_ref, xs_hbm, wg_hbm, wu_hbm, wd_hbm, y_ref,
                   wgb_ref, wub_ref, wdb_ref, xbuf_ref, xsem, wg_raw, wu_raw, wd_raw, wsem):
    i = pl.program_id(0)
    rows, half = xbuf_ref.shape[1], xbuf_ref.shape[2]
    nused = plan_ref[PLAN_NUSED, 0]

    def w_copies(expert, slot):
        return [pltpu.make_async_copy(src.at[expert], dst.at[slot], wsem.at[slot])
                for src, dst in ((wg_hbm, wg_raw), (wu_hbm, wu_raw), (wd_hbm, wd_raw))]

    def x_copy(block):
        first = pl.multiple_of(block * rows, rows)
        slot = block % X_RING
        return pltpu.make_async_copy(xs_hbm.at[pl.ds(first, rows)], xbuf_ref.at[slot], xsem.at[slot])

    @pl.when(i == 0)
    def _():
        for b in range(X_RING - 1):
            @pl.when(b < nused)
            def _():
                x_copy(b).start()

    @pl.when(i + (X_RING - 1) < nused)
    def _():
        x_copy(i + (X_RING - 1)).start()

    @pl.when(i == 0)
    def _():
        for c in w_copies(plan_ref[PLAN_EXPERT, 0], 0):
            c.start()

    @pl.when(plan_ref[PLAN_FRESH, i] > 0)
    def _():
        slot = plan_ref[PLAN_SEG, i] % 2
        for c in w_copies(plan_ref[PLAN_EXPERT, i], slot):
            c.wait()

        @pl.when(plan_ref[PLAN_NEXT, i] >= 0)
        def _():
            for c in w_copies(plan_ref[PLAN_NEXT, i], 1 - slot):
                c.start()

        wgb_ref[...] = wg_raw[slot].astype(BF16)
        wub_ref[...] = wu_raw[slot].astype(BF16)
        wdb_ref[...] = wd_raw[slot].astype(BF16)

    @pl.when(i < nused)
    def _():
        x_copy(i).wait()
        xw = xbuf_ref[i % X_RING]
        row = lax.broadcasted_iota(jnp.int32, xw.shape, 0)
        x_lo, x_hi = _unpack_bf16_pair(jnp.where(row < plan_ref[PLAN_VALID, i], xw, jnp.uint32(0)))
        x_lo = x_lo.astype(BF16)
        x_hi = x_hi.astype(BF16)
        gate = (jnp.dot(x_lo, wgb_ref[:half, :], preferred_element_type=F32)
                + jnp.dot(x_hi, wgb_ref[half:, :], preferred_element_type=F32))
        up = (jnp.dot(x_lo, wub_ref[:half, :], preferred_element_type=F32)
              + jnp.dot(x_hi, wub_ref[half:, :], preferred_element_type=F32))
        u = (_silu(gate) * up).astype(BF16)
        y_lo = jnp.dot(u, wdb_ref[:, :half], preferred_element_type=F32)
        y_hi = jnp.dot(u, wdb_ref[:, half:], preferred_element_type=F32)
        y_ref[...] = _pack_bf16_pair(y_lo, y_hi)

    @pl.when(i >= nused)
    def _():
        y_ref[...] = jnp.zeros_like(y_ref)


def _expert_call(plan, xs, wg, wu, wd, rows):
    p, w = xs.shape
    d, f = wg.shape[1], wg.shape[2]
    hbm = pl.BlockSpec(memory_space=pl.ANY)
    grid_spec = pltpu.PrefetchScalarGridSpec(
        num_scalar_prefetch=1,
        grid=(p // rows,),
        in_specs=[hbm, hbm, hbm, hbm],
        out_specs=pl.BlockSpec((rows, w), lambda i, *_: (i, 0)),
        scratch_shapes=[pltpu.VMEM((d, f), BF16), pltpu.VMEM((d, f), BF16), pltpu.VMEM((f, d), BF16),
                        pltpu.VMEM((X_RING, rows, w), xs.dtype), pltpu.SemaphoreType.DMA((X_RING,)),
                        pltpu.VMEM((2, d, f), wg.dtype), pltpu.VMEM((2, d, f), wu.dtype),
                        pltpu.VMEM((2, f, d), wd.dtype), pltpu.SemaphoreType.DMA((2,))],
    )
    return pl.pallas_call(
        _expert_kernel,
        grid_spec=grid_spec,
        out_shape=jax.ShapeDtypeStruct((p, w), jnp.uint32),
        compiler_params=pltpu.CompilerParams(dimension_semantics=("arbitrary",),
                                             vmem_limit_bytes=VMEM_LIMIT),
    )(plan, xs, wg, wu, wd)


def _gather_rows_sc(table, idx, chunk):
    m = idx.shape[0]
    w = table.shape[1]
    info = plsc.get_sparse_core_info()
    nc = info.num_cores
    per_worker = m // (nc * info.num_subcores)
    assert per_worker * nc * info.num_subcores == m and per_worker % chunk == 0
    part = chunk // SC_GATHER_PARTS
    mesh = plsc.VectorSubcoreMesh(core_axis_name="c", subcore_axis_name="s")

    @functools.partial(
        pl.kernel, mesh=mesh,
        out_type=jax.ShapeDtypeStruct((m, w), table.dtype),
        scratch_types=([pltpu.VMEM((part,), jnp.int32) for _ in range(SC_GATHER_PARTS)]
                       + [pltpu.VMEM((part, w), table.dtype) for _ in range(SC_GATHER_PARTS)]
                       + [pltpu.SemaphoreType.DMA] * 3),
    )
    def gather(table_hbm, idx_hbm, out_hbm, *scratch):
        idx_vs = scratch[:SC_GATHER_PARTS]
        rows_vs = scratch[SC_GATHER_PARTS:2 * SC_GATHER_PARTS]
        isem, gsem, wsem = scratch[2 * SC_GATHER_PARTS:]
        worker = lax.axis_index("s") * nc + lax.axis_index("c")

        @pl.loop(0, per_worker // chunk)
        def _(j):
            base = worker * per_worker + j * chunk
            loads = [pltpu.async_copy(idx_hbm.at[pl.ds(base + p * part, part)], idx_vs[p], isem)
                     for p in range(SC_GATHER_PARTS)]
            gathers = []
            for p in range(SC_GATHER_PARTS):
                loads[p].wait()
                gathers.append(pltpu.async_copy(table_hbm.at[idx_vs[p]], rows_vs[p], gsem))
            writes = []
            for p in range(SC_GATHER_PARTS):
                gathers[p].wait()
                writes.append(pltpu.async_copy(rows_vs[p], out_hbm.at[pl.ds(base + p * part, part)], wsem))
            for wr in writes:
                wr.wait()

    return gather(table, idx)


def _combine_kernel(yk_ref, gate_ref, base_ref, g2_ref, fg_ref, o_ref):
    t = base_ref.shape[0]
    half = yk_ref.shape[2]
    gates = jnp.concatenate([gate_ref[...], jnp.zeros((LANES - TOP_K, t), F32)], axis=0).T
    r_lo = jnp.zeros((t, half), F32)
    r_hi = jnp.zeros((t, half), F32)
    for k in range(TOP_K):
        y_lo, y_hi = _unpack_bf16_pair(yk_ref[k])
        r_lo = r_lo + gates[:, k:k + 1] * y_lo
        r_hi = r_hi + gates[:, k:k + 1] * y_hi
    g2 = g2_ref[0]
    x_lo = base_ref[:, :half] + g2[:, :half] * r_lo
    x_hi = base_ref[:, half:] + g2[:, half:] * r_hi
    ms = (jnp.sum(x_lo * x_lo, axis=-1, keepdims=True)
          + jnp.sum(x_hi * x_hi, axis=-1, keepdims=True)) * (1.0 / (2 * half))
    inv = lax.rsqrt(ms + EPS)
    o_ref[:, :half] = x_lo * inv * fg_ref[:, :half]
    o_ref[:, half:] = x_hi * inv * fg_ref[:, half:]


def _combine_call(yk, gates, base, g2, fg, tiles_per_batch, tile, tile0):
    n, d = base.shape
    row = lambda w: pl.BlockSpec((tile, w), lambda i: (i + tile0, 0))
    return pl.pallas_call(
        _combine_kernel,
        grid=(yk.shape[1] // tile,),
        in_specs=[pl.BlockSpec((TOP_K, tile, yk.shape[2]), lambda i: (0, i, 0)),
                  pl.BlockSpec((TOP_K, tile), lambda i: (0, i + tile0)), row(d),
                  pl.BlockSpec((1, 1, d), lambda i: ((i + tile0) // tiles_per_batch, 0, 0)),
                  pl.BlockSpec((1, d), lambda i: (0, 0))],
        out_specs=row(d),
        out_shape=jax.ShapeDtypeStruct((n, d), F32),
        input_output_aliases={2: 0},
        compiler_params=pltpu.CompilerParams(dimension_semantics=("arbitrary",),
                                             vmem_limit_bytes=VMEM_LIMIT),
    )(yk, gates, base, g2, fg)


def _pad_cols(a, width):
    return jnp.pad(a, ((0, 0), (0, width - a.shape[1])))


def _layer(x, mod, norm1_g, norm2_g, w_in, conv_w, conv_b, dt_bias, a_log, d_skip, ssm_norm_g,
           att_norm_g, sinks, rel_bias, w_out, router_w, router_bias, exp_w_gate, exp_w_up, exp_w_down,
           sh_w_gate, sh_w_up, sh_w_down, final_g):
    bsz, l, d = x.shape
    n = bsz * l
    tm = min(ROW_TILE, l)

    sh1, sc1, g1, sh2, sc2, g2 = [m[:, None, :] for m in jnp.split(mod, 6, axis=-1)]

    assert math.frexp(ATT_HEAD_DIM ** -0.5)[0] == 0.5
    x2 = x.reshape(n, d)
    z, xbc, dt, q, k, v = _in_proj_call(x2, sc1, sh1, norm1_g[None, :], jnp.swapaxes(w_in, 1, 2), l // tm, tm)

    triu = jnp.asarray(np.triu(np.ones((CHUNK, CHUNK), np.float32))).astype(BF16)
    shift = jnp.asarray(_conv_shift_matrix()).astype(BF16)
    y_ssm = _ssd_call(xbc.reshape(bsz, l, CONV_CH), z.reshape(bsz, l, D_SSM), dt.reshape(bsz, l, LANES),
                      conv_w, conv_b[None, :], dt_bias[:, None], a_log[:, None],
                      jnp.repeat(d_skip, SSM_HEAD_DIM)[None, :], ssm_norm_g[None, :], triu, shift)

    bias = _bias_call(rel_bias, jnp.asarray(_rel_bucket_table()))
    y_att = _attn_call(sinks, q.reshape(bsz, l, D_ATT), k.reshape(bsz, l, D_KV), v.reshape(bsz, l, D_KV), bias,
                       att_norm_g[None, :])

    rw = _pad_cols(router_w, LANES)
    rwh = rw.astype(BF16)
    rwl = jnp.concatenate([rwh, (rw - rwh.astype(F32)).astype(BF16)], axis=1)
    rs = min(RANK_SUB, tm)
    upper = jnp.asarray(np.triu(np.ones((rs, rs), np.float32), 1)).astype(BF16)
    base, h2, idx, gates, rank, counts = _out_proj_call(
        x2, y_ssm.reshape(n, D_SSM), y_att.reshape(n, D_ATT), g1, sc2, sh2, g2, norm2_g[None, :],
        w_out, sh_w_gate, sh_w_up, sh_w_down, rwh, rwl, router_bias[:, None], upper, l // tm, tm)
    rt = min(ROUTE_TILE, n)

    nblocks = (n * TOP_K + N_EXPERTS * (EXPERT_ROWS - 1) + EXPERT_ROWS - 1) // EXPERT_ROWS
    pad_start, plan = _plan_call(counts, nblocks)
    dest = _dest_call(idx, rank, pad_start, rt)

    xs = _scatter_rows_sc(h2, dest.reshape(-1), nblocks * EXPERT_ROWS, SC_CHUNK)
    ys = _expert_call(plan, xs, exp_w_gate, exp_w_up, exp_w_down, EXPERT_ROWS)
    ctile = min(COMBINE_TILE, l)
    groups = COMBINE_GROUPS if bsz % COMBINE_GROUPS == 0 else 1
    ng = n // groups
    out = base
    for g in range(groups):
        idx_g = dest[:, g * ng // LANES:(g + 1) * ng // LANES, :].reshape(-1)
        yk = _gather_rows_sc(ys, idx_g, SC_CHUNK).reshape(TOP_K, ng, ys.shape[1])
        out = _combine_call(yk, gates, out, g2, final_g[None, :], l // ctile, ctile, g * ng // ctile)
    return out.reshape(bsz, l, d)


def kernel(x, c, mod_w, mod_b, norm1_g, norm2_g, w_in, conv_w, conv_b, dt_bias, a_log, d_skip, ssm_norm_g,
           att_norm_g, sinks, rel_bias, w_out, router_w, router_bias, exp_w_gate, exp_w_up, exp_w_down,
           sh_w_gate, sh_w_up, sh_w_down, final_g):
    assert mod_w.shape[0] == 1, "single-layer block"
    bsz = x.shape[0]
    c_pad = jnp.pad(c, ((0, SUBLANES - bsz % SUBLANES if bsz % SUBLANES else 0), (0, 0)))
    mod = _mod_call(c_pad, mod_w[0], mod_b[0][None, :])[:bsz]
    return _layer(x, mod, norm1_g[0], norm2_g[0], w_in, conv_w[0], conv_b[0], dt_bias[0], a_log[0], d_skip[0],
                  ssm_norm_g[0], att_norm_g[0], sinks[0], rel_bias, w_out, router_w[0], router_bias[0],
                  exp_w_gate[0], exp_w_up[0], exp_w_down[0], sh_w_gate, sh_w_up, sh_w_down, final_g)
```

```python
import functools
import math

import numpy as np
import jax
import jax.numpy as jnp
from jax import lax
from jax.experimental import pallas as pl
from jax.experimental.pallas import tpu as pltpu
from jax.experimental.pallas import tpu_sc as plsc

F32 = jnp.float32
BF16 = jnp.bfloat16

D_MODEL = 1024
SSM_HEAD_DIM = 64
D_SSM = D_MODEL
SSM_HEADS = D_SSM // SSM_HEAD_DIM
SSM_GROUPS = 4
D_STATE = 128
CONV_K = 4
CONV_CH = D_SSM + 2 * SSM_GROUPS * D_STATE
CHUNK = 128
ATT_HEAD_DIM = 64
D_ATT = D_MODEL
ATT_HEADS = D_ATT // ATT_HEAD_DIM
KV_HEADS = ATT_HEADS // 4
Q_PER_KV = ATT_HEADS // KV_HEADS
D_KV = KV_HEADS * ATT_HEAD_DIM
WINDOW = 128
ATT_BLOCK = 128
REL_BUCKETS = 32
REL_MAX_DIST = 128
N_EXPERTS = 64
TOP_K = 8
EXPERT_DIM = D_MODEL // 4
SHARED_DIM = D_MODEL // 4
ROUTE_GROUPS = 8
ROUTE_TOPK_GROUPS = 4
ROUTED_SCALE = 2.5
EPS = 1e-6

LANES = 128
SUBLANES = 8
HALF = LANES // 2

ROW_TILE = 512
ROUTE_TILE = 2048
RANK_SUB = 512
COMBINE_TILE = 512
SC_CHUNK = 128
SC_GATHER_PARTS = 4
EXPERT_ROWS = 512
COMBINE_GROUPS = 2
VMEM_LIMIT = 48 * 1024 * 1024

NEG_INF = float("-inf")


def _silu(v):
    return v * (1.0 / (1.0 + jnp.exp(-v)))


def _softplus(v):
    return jnp.maximum(v, 0.0) + jnp.log(1.0 + jnp.exp(-jnp.abs(v)))


def _split_hi_lo(v):
    hi = v.astype(BF16)
    lo = (v - hi.astype(F32)).astype(BF16)
    return hi, lo


def _pack_bf16_pair(a, b):
    w = pltpu.pack_elementwise([a, b], packed_dtype=BF16)
    return w if w.dtype == jnp.uint32 else lax.bitcast_convert_type(w, jnp.uint32)


def _unpack_bf16_pair(w):
    a = pltpu.unpack_elementwise(w, index=0, packed_dtype=BF16, unpacked_dtype=F32)
    b = pltpu.unpack_elementwise(w, index=1, packed_dtype=BF16, unpacked_dtype=F32)
    return a, b


def _lane_half_mask(shape):
    return lax.broadcasted_iota(jnp.int32, shape, len(shape) - 1) < HALF


def _mod_kernel(c_ref, w_ref, b_ref, o_ref):
    a = _silu(c_ref[...])
    o_ref[...] = jnp.dot(a, w_ref[...], precision=lax.Precision.HIGHEST,
                         preferred_element_type=F32) + b_ref[...]


def _mod_call(c_pad, mod_w, mod_b):
    rows, d = c_pad.shape
    cols = mod_w.shape[1]
    return pl.pallas_call(
        _mod_kernel,
        grid=(cols // d,),
        in_specs=[pl.BlockSpec((rows, d), lambda j: (0, 0)),
                  pl.BlockSpec((d, d), lambda j: (0, j)),
                  pl.BlockSpec((1, d), lambda j: (0, j))],
        out_specs=pl.BlockSpec((rows, d), lambda j: (0, j)),
        out_shape=jax.ShapeDtypeStruct((rows, cols), F32),
        compiler_params=pltpu.CompilerParams(dimension_semantics=("arbitrary",),
                                             vmem_limit_bytes=VMEM_LIMIT),
    )(c_pad, mod_w, mod_b)


IN_PROJ_SEGMENTS = ((D_SSM, BF16), (CONV_CH, BF16), (LANES, F32), (D_ATT, BF16), (D_KV, BF16), (D_KV, BF16))


def _in_proj_kernel(x_ref, sc_ref, sh_ref, g_ref, w_hbm, *refs):
    out_refs = refs[:len(IN_PROJ_SEGMENTS)]
    wraw_ref, w_ref, wsem = refs[len(IN_PROJ_SEGMENTS):]

    @pl.when(pl.program_id(0) == 0)
    def _():
        cp = pltpu.make_async_copy(w_hbm.at[0], wraw_ref, wsem)
        cp.start()
        cp.wait()
        src_dt = D_SSM + CONV_CH
        dst_q = src_dt + LANES
        src_q = src_dt + SSM_HEADS
        q_scale = ATT_HEAD_DIM ** -0.5

        def put(dst, src, scale=None):
            t = wraw_ref[src:src + LANES, :].T
            w_ref[:, dst:dst + LANES] = (t if scale is None else t * scale).astype(BF16)

        for c0 in range(0, src_dt, LANES):
            put(c0, c0)
        dt_tile = wraw_ref[src_dt:src_dt + LANES, :].T
        lane = lax.broadcasted_iota(jnp.int32, dt_tile.shape, 1)
        w_ref[:, src_dt:dst_q] = jnp.where(lane < SSM_HEADS, dt_tile, 0.0).astype(BF16)
        for c0 in range(0, D_ATT, LANES):
            put(dst_q + c0, src_q + c0, q_scale)
        for c0 in range(D_ATT, D_ATT + 2 * D_KV, LANES):
            put(dst_q + c0, src_q + c0)

    xf = x_ref[...]
    ms = jnp.mean(xf * xf, axis=-1, keepdims=True)
    h = xf * lax.rsqrt(ms + EPS) * g_ref[...]
    h = h * (1.0 + sc_ref[0]) + sh_ref[0]
    hb = h.astype(BF16)
    col = 0
    for (width, dtype), o_ref in zip(IN_PROJ_SEGMENTS, out_refs):
        o_ref[...] = jnp.dot(hb, w_ref[:, col:col + width], preferred_element_type=F32).astype(dtype)
        col += width


def _in_proj_call(x2, sc1, sh1, g1n, w_in_t, tiles_per_batch, tm):
    n, d = x2.shape
    row = lambda w: pl.BlockSpec((tm, w), lambda i: (i, 0))
    full = lambda a: pl.BlockSpec(a.shape, lambda i: (0, 0))
    per_batch = pl.BlockSpec((1, 1, d), lambda i: (i // tiles_per_batch, 0, 0))
    cols = sum(w for w, _ in IN_PROJ_SEGMENTS)
    w_in = w_in_t
    assert w_in.shape == (1, cols - (LANES - SSM_HEADS), d)
    return pl.pallas_call(
        _in_proj_kernel,
        grid=(n // tm,),
        in_specs=[row(d), per_batch, per_batch, full(g1n), pl.BlockSpec(memory_space=pl.ANY)],
        out_specs=[row(w) for w, _ in IN_PROJ_SEGMENTS],
        out_shape=[jax.ShapeDtypeStruct((n, w), dt) for w, dt in IN_PROJ_SEGMENTS],
        scratch_shapes=[pltpu.VMEM(w_in.shape[1:], w_in.dtype), pltpu.VMEM((d, cols), BF16), pltpu.SemaphoreType.DMA(())],
        compiler_params=pltpu.CompilerParams(dimension_semantics=("arbitrary",),
                                             vmem_limit_bytes=VMEM_LIMIT),
    )(x2, sc1, sh1, g1n, w_in)


SSD_SEQS = 4
CONV_HALO = 16


def _conv_shift_matrix():
    s = np.zeros((CONV_K * CHUNK, CONV_HALO + CHUNK), np.float32)
    for k in range(CONV_K):
        t = np.arange(CHUNK)
        s[k * CHUNK + t, CONV_HALO + t - (CONV_K - 1) + k] = 1.0
    return s


def _silu_tanh(v):
    hv = 0.5 * v
    return hv + hv * jnp.tanh(hv)


def _ssd_kernel(xbc_ref, z_ref, dt_ref, cw_ref, cb_ref, dtb_ref, alog_ref, dskip_ref, ng_ref, triu_ref, shift_ref,
                y_ref, state_ref, ucat_ref, ybuf_ref):
    nseq = xbc_ref.shape[0]

    @pl.when(pl.program_id(1) == 0)
    def _():
        state_ref[...] = jnp.zeros_like(state_ref)
        ucat_ref[:, 0:CONV_HALO, :] = jnp.zeros((nseq, CONV_HALO, CONV_CH), BF16)

    for q in range(nseq):
        _ssd_chunk(xbc_ref.at[q], z_ref.at[q], dt_ref.at[q], cw_ref, cb_ref, dtb_ref, alog_ref, dskip_ref, ng_ref,
                   triu_ref, shift_ref, y_ref.at[q], state_ref.at[q], ucat_ref.at[q], ybuf_ref.at[q])


def _ssd_chunk(xbc_ref, z_ref, dt_ref, cw_ref, cb_ref, dtb_ref, alog_ref, dskip_ref, ng_ref, triu_ref, shift_ref,
               y_ref, state_ref, ucat_ref, ybuf_ref):
    ucat_ref[CONV_HALO:, :] = xbc_ref[...]
    shifted = jnp.dot(shift_ref[...], ucat_ref[...], preferred_element_type=F32)
    ucat_ref[0:CONV_HALO, :] = ucat_ref[CHUNK:CHUNK + CONV_HALO, :]
    acc = cb_ref[...] + cw_ref[0:1, :] * shifted[0:CHUNK]
    for kk in range(1, CONV_K):
        acc = acc + cw_ref[kk:kk + 1, :] * shifted[kk * CHUNK:(kk + 1) * CHUNK]
    act = _silu_tanh(acc)
    xs = act[:, :D_SSM]
    gn = SSM_GROUPS * D_STATE

    dt_t = _softplus(dt_ref[...].T[0:SSM_HEADS, :] + dtb_ref[...])
    a_t = dt_t * (-jnp.exp(alog_ref[...]))
    a_hi = a_t.astype(BF16)
    a_mid = (a_t - a_hi.astype(F32)).astype(BF16)
    a_lo = (a_t - a_hi.astype(F32) - a_mid.astype(F32)).astype(BF16)
    triu = triu_ref[...]
    cs_t = (jnp.dot(a_hi, triu, preferred_element_type=F32) + jnp.dot(a_mid, triu, preferred_element_type=F32)
            + jnp.dot(a_lo, triu, preferred_element_type=F32))
    cs_end = cs_t[:, CHUNK - 1:CHUNK]
    r_t = cs_t - jnp.log(dt_t)
    w_t = jnp.exp(cs_end - cs_t) * dt_t
    chunk_decay = jnp.exp(cs_end)
    cols = jnp.concatenate([cs_t, jnp.exp(cs_t), jnp.zeros((LANES - 2 * SSM_HEADS, CHUNK), F32)], axis=0).T

    li = lax.broadcasted_iota(jnp.int32, (CHUNK, CHUNK), 0)
    si = lax.broadcasted_iota(jnp.int32, (CHUNK, CHUNK), 1)
    causal = li >= si
    low = _lane_half_mask((CHUNK, LANES))
    low_row = _lane_half_mask((1, LANES))

    heads_per_group = SSM_HEADS // SSM_GROUPS
    for g in range(SSM_GROUPS):
        b_g = act[:, D_SSM + g * D_STATE:D_SSM + (g + 1) * D_STATE]
        c_g = act[:, D_SSM + gn + g * D_STATE:D_SSM + gn + (g + 1) * D_STATE]
        b_gb = b_g.astype(BF16)
        c_gb = c_g.astype(BF16)
        cb = lax.dot_general(c_gb, b_gb, (((1,), (1,)), ((), ())), preferred_element_type=F32)
        b_t = b_g.T
        for jp in range(heads_per_group // 2):
            j = g * (heads_per_group // 2) + jp
            lanes = slice(j * LANES, (j + 1) * LANES)
            xp = xs[:, lanes]
            xpb = xp.astype(BF16)
            ydiag = jnp.zeros((CHUNK, LANES), F32)
            snew = jnp.zeros((D_STATE, LANES), F32)
            for half in range(2):
                h = 2 * j + half
                diff = cols[:, h:h + 1] - r_t[h:h + 1, :]
                m = (cb * jnp.exp(jnp.where(causal, diff, NEG_INF))).astype(BF16)
                keep = low if half == 0 else jnp.logical_not(low)
                xh = jnp.where(keep, xpb, jnp.zeros_like(xpb))
                ydiag = ydiag + jnp.dot(m, xh, preferred_element_type=F32)
                snew = snew + jnp.dot((b_t * w_t[h:h + 1, :]).astype(BF16), xh, preferred_element_type=F32)
            s_in = state_ref[:, lanes]
            yoff = jnp.dot(c_gb, s_in.astype(BF16), preferred_element_type=F32)
            h0 = 2 * j
            e0 = SSM_HEADS + h0
            escale = jnp.where(low, cols[:, e0:e0 + 1], cols[:, e0 + 1:e0 + 2])
            cdec = jnp.where(low_row, chunk_decay[h0:h0 + 1, :], chunk_decay[h0 + 1:h0 + 2, :])
            ybuf_ref[:, lanes] = ydiag + yoff * escale + xp * dskip_ref[:, lanes]
            state_ref[:, lanes] = s_in * cdec + snew

    yz = ybuf_ref[...] * _silu_tanh(z_ref[...].astype(F32))
    gw = D_SSM // SSM_GROUPS
    for g in range(SSM_GROUPS):
        part = yz[:, g * gw:(g + 1) * gw]
        ms = jnp.mean(part * part, axis=-1, keepdims=True)
        y_ref[:, g * gw:(g + 1) * gw] = (part * lax.rsqrt(ms + EPS)
                                            * ng_ref[:, g * gw:(g + 1) * gw]).astype(BF16)


def _ssd_call(xbc, z, dt, conv_w, conv_b, dtb, alog, dskip, ng, triu, shift):
    bsz, l, _ = xbc.shape
    nc = l // CHUNK
    nseq = math.gcd(bsz, SSD_SEQS)
    chunk = lambda w: pl.BlockSpec((nseq, CHUNK, w), lambda b, c: (b, c, 0))
    full = lambda a: pl.BlockSpec(a.shape, lambda b, c: (0, 0))
    return pl.pallas_call(
        _ssd_kernel,
        grid=(bsz // nseq, nc),
        in_specs=[chunk(CONV_CH), chunk(D_SSM), chunk(LANES), full(conv_w), full(conv_b), full(dtb),
                  full(alog), full(dskip), full(ng), full(triu), full(shift)],
        out_specs=chunk(D_SSM),
        out_shape=jax.ShapeDtypeStruct((bsz, l, D_SSM), BF16),
        scratch_shapes=[pltpu.VMEM((nseq, D_STATE, D_SSM), F32),
                        pltpu.VMEM((nseq, CONV_HALO + CHUNK, CONV_CH), BF16),
                        pltpu.VMEM((nseq, CHUNK, D_SSM), F32)],
        compiler_params=pltpu.CompilerParams(dimension_semantics=("arbitrary", "arbitrary"),
                                             vmem_limit_bytes=VMEM_LIMIT),
    )(xbc, z, dt, conv_w, conv_b, dtb, alog, dskip, ng, triu, shift)


assert WINDOW == ATT_BLOCK


def _rel_bucket_table():
    qi = np.arange(ATT_BLOCK)[:, None]
    c = np.arange(ATT_BLOCK)[None, :]
    dist = np.where(c > qi, qi + ATT_BLOCK - c, qi - c)
    max_exact = REL_BUCKETS // 2
    d = np.maximum(dist, 1).astype(np.float32)
    large = max_exact + (np.log(d / np.float32(max_exact)) / np.float32(math.log(REL_MAX_DIST / max_exact))
                         * np.float32(REL_BUCKETS - max_exact)).astype(np.int32)
    large = np.minimum(large, REL_BUCKETS - 1)
    return np.where(dist < max_exact, dist, large).astype(np.int32)


def _bias_kernel(rb_ref, bucket_ref, o_ref):
    bucket = bucket_ref[...]
    from_prev = (lax.broadcasted_iota(jnp.int32, bucket.shape, 1)
                 > lax.broadcasted_iota(jnp.int32, bucket.shape, 0))
    for h in range(ATT_HEADS):
        acc = jnp.zeros(bucket.shape, F32)
        for b in range(REL_BUCKETS):
            acc = jnp.where(bucket == b, rb_ref[b, h], acc)
        o_ref[1, h] = acc
        o_ref[0, h] = jnp.where(from_prev, NEG_INF, acc)


def _bias_call(rel_bias, bucket):
    return pl.pallas_call(
        _bias_kernel,
        in_specs=[pl.BlockSpec(memory_space=pltpu.SMEM), pl.BlockSpec(memory_space=pltpu.VMEM)],
        out_shape=jax.ShapeDtypeStruct((2, ATT_HEADS) + bucket.shape, F32),
    )(rel_bias, bucket)


ATT_SEQS = 4


def _attn_kernel(sink_ref, q_ref, kp_ref, kc_ref, vp_ref, vc_ref, bias_ref, ng_ref, o_ref, obuf_ref):
    for s in range(q_ref.shape[0]):
        _attn_block(sink_ref, q_ref.at[s], kp_ref.at[s], kc_ref.at[s], vp_ref.at[s], vc_ref.at[s], bias_ref.at[0],
                    ng_ref, o_ref.at[s], obuf_ref.at[s])


def _attn_block(sink_ref, q_ref, kp_ref, kc_ref, vp_ref, vc_ref, bias_ref, ng_ref, o_ref, obuf_ref):
    qi = lax.broadcasted_iota(jnp.int32, (ATT_BLOCK, ATT_BLOCK), 0)
    ci = lax.broadcasted_iota(jnp.int32, (ATT_BLOCK, ATT_BLOCK), 1)
    from_prev = ci > qi
    low = _lane_half_mask((ATT_BLOCK, LANES))

    def band_variants(prev_ref, cur_ref):
        out = []
        for cpair in range(KV_HEADS // 2):
            lanes = slice(cpair * LANES, (cpair + 1) * LANES)
            t = jnp.concatenate([prev_ref[:, lanes], cur_ref[:, lanes]], axis=0).astype(F32)
            out.append((t.astype(BF16), pltpu.roll(t, HALF, 1).astype(BF16)))
        return out

    k_band = band_variants(kp_ref, kc_ref)
    v_band = band_variants(vp_ref, vc_ref)
    nt = (((1,), (1,)), ((), ()))

    for j in range(ATT_HEADS // 2):
        qp = q_ref[:, j * LANES:(j + 1) * LANES]
        out_pair = jnp.zeros((ATT_BLOCK, LANES), F32)
        for half in range(2):
            h = 2 * j + half
            g = h // Q_PER_KV
            swapped = int((g % 2) != half)
            keep = low if half == 0 else jnp.logical_not(low)
            qh = jnp.where(keep, qp, jnp.zeros_like(qp))
            s_band = lax.dot_general(qh, k_band[g // 2][swapped], nt, preferred_element_type=F32)
            s = jnp.where(from_prev, s_band[:, :ATT_BLOCK], s_band[:, ATT_BLOCK:]) + bias_ref[h]
            sink = sink_ref[h]
            m = jnp.maximum(jnp.max(s, axis=-1, keepdims=True), sink)
            p = jnp.exp(s - m)
            denom = jnp.sum(p, axis=-1, keepdims=True) + jnp.exp(sink - m)
            p_band = jnp.concatenate([jnp.where(from_prev, p, 0.0), jnp.where(from_prev, 0.0, p)], axis=1)
            o = jnp.dot(p_band.astype(BF16), v_band[g // 2][swapped], preferred_element_type=F32) / denom
            out_pair = out_pair + jnp.where(keep, o, 0.0)
        obuf_ref[:, j * LANES:(j + 1) * LANES] = out_pair

    att = obuf_ref[...]
    ms = jnp.mean(att * att, axis=-1, keepdims=True)
    o_ref[...] = (att * lax.rsqrt(ms + EPS) * ng_ref[...]).astype(BF16)


def _attn_call(sinks, q, k, v, bias, ng):
    bsz, l, _ = q.shape
    nb = l // ATT_BLOCK
    nseq = math.gcd(bsz, ATT_SEQS)
    cur = lambda w: pl.BlockSpec((nseq, ATT_BLOCK, w), lambda b, i: (b, i, 0))
    prev = lambda w: pl.BlockSpec((nseq, ATT_BLOCK, w), lambda b, i: (b, jnp.maximum(i - 1, 0), 0))
    return pl.pallas_call(
        _attn_kernel,
        grid=(bsz // nseq, nb),
        in_specs=[pl.BlockSpec(memory_space=pltpu.SMEM),
                  cur(D_ATT), prev(D_KV), cur(D_KV), prev(D_KV), cur(D_KV),
                  pl.BlockSpec((1,) + bias.shape[1:], lambda b, i: (jnp.minimum(i, 1), 0, 0, 0)),
                  pl.BlockSpec(ng.shape, lambda b, i: (0, 0))],
        out_specs=cur(D_ATT),
        out_shape=jax.ShapeDtypeStruct((bsz, l, D_ATT), BF16),
        scratch_shapes=[pltpu.VMEM((nseq, ATT_BLOCK, D_ATT), F32)],
        compiler_params=pltpu.CompilerParams(dimension_semantics=("arbitrary", "arbitrary"),
                                             vmem_limit_bytes=VMEM_LIMIT),
    )(sinks, q, k, k, v, v, bias, ng)


def _out_proj_kernel(x_ref, ys_ref, ya_ref, g1_ref, sc_ref, sh_ref, g2_ref, ng_ref, wo_hbm, sg_hbm, su_hbm, sd_hbm,
                     rwh_ref, rwl_ref, rb_ref, upper_ref,
                     base_ref, h_ref, idx_ref, gate_ref, rank_ref, cnt_ref,
                     carry_ref, wo_raw, sg_raw, su_raw, sd_raw, wo_ref, sg_ref, su_ref, sd_ref, wsem):
    @pl.when(pl.program_id(0) == 0)
    def _():
        carry_ref[...] = jnp.zeros_like(carry_ref)
        staged = ((wo_hbm, wo_raw, wo_ref), (sg_hbm, sg_raw, sg_ref), (su_hbm, su_raw, su_ref),
                  (sd_hbm, sd_raw, sd_ref))
        copies = [pltpu.make_async_copy(src.at[0], raw, wsem.at[j]) for j, (src, raw, _) in enumerate(staged)]
        for cp in copies:
            cp.start()
        for cp, (_, raw, dst) in zip(copies, staged):
            cp.wait()
            dst[...] = raw[...].astype(BF16)

    mix = (jnp.dot(ys_ref[...], wo_ref[:D_SSM, :], preferred_element_type=F32)
           + jnp.dot(ya_ref[...], wo_ref[D_SSM:, :], preferred_element_type=F32))
    x1 = x_ref[...] + g1_ref[0] * mix
    ms = jnp.mean(x1 * x1, axis=-1, keepdims=True)
    h = x1 * lax.rsqrt(ms + EPS) * ng_ref[...]
    h = h * (1.0 + sc_ref[0]) + sh_ref[0]
    half = h.shape[1] // 2
    h_ref[...] = _pack_bf16_pair(h[:, :half], h[:, half:])
    hi, lo = _split_hi_lo(h)
    hi_terms = jnp.dot(hi, rwl_ref[...], preferred_element_type=F32)
    logits = (hi_terms[:, :LANES] + hi_terms[:, LANES:]
              + jnp.dot(lo, rwh_ref[...], preferred_element_type=F32))
    _route_tokens(logits.T[0:N_EXPERTS, :], rb_ref, upper_ref, idx_ref, gate_ref, rank_ref, cnt_ref, carry_ref)
    u = _silu(jnp.dot(hi, sg_ref[...], preferred_element_type=F32)) * jnp.dot(hi, su_ref[...],
                                                                              preferred_element_type=F32)
    shared = jnp.dot(u.astype(BF16), sd_ref[...], preferred_element_type=F32)
    base_ref[...] = x1 + g2_ref[0] * shared


def _out_proj_call(x2, ys, ya, g1, sc2, sh2, g2, ng, w_out, sg, su, sd, rwh, rwl, router_bias, upper,
                   tiles_per_batch, tm):
    n, d = x2.shape
    row = lambda w: pl.BlockSpec((tm, w), lambda i: (i, 0))
    tok = lambda r: pl.BlockSpec((r, tm), lambda i: (0, i))
    full = lambda a: pl.BlockSpec(a.shape, lambda i: (0, 0))
    per_batch = pl.BlockSpec((1, 1, d), lambda i: (i // tiles_per_batch, 0, 0))
    hbm = pl.BlockSpec(memory_space=pl.ANY)
    staged = (w_out, sg, su, sd)
    return pl.pallas_call(
        _out_proj_kernel,
        grid=(n // tm,),
        in_specs=[row(d), row(D_SSM), row(D_ATT), per_batch, per_batch, per_batch, per_batch, full(ng),
                  hbm, hbm, hbm, hbm, full(rwh), full(rwl), full(router_bias), full(upper)],
        out_specs=[row(d), row(d // 2), tok(TOP_K), tok(TOP_K), tok(TOP_K),
                   pl.BlockSpec((N_EXPERTS, LANES), lambda i: (0, 0))],
        out_shape=[jax.ShapeDtypeStruct((n, d), F32), jax.ShapeDtypeStruct((n, d // 2), jnp.uint32),
                   jax.ShapeDtypeStruct((TOP_K, n), jnp.int32), jax.ShapeDtypeStruct((TOP_K, n), F32),
                   jax.ShapeDtypeStruct((TOP_K, n), jnp.int32), jax.ShapeDtypeStruct((N_EXPERTS, LANES), F32)],
        scratch_shapes=([pltpu.VMEM((N_EXPERTS, LANES), F32)]
                        + [pltpu.VMEM(a.shape[1:], a.dtype) for a in staged]
                        + [pltpu.VMEM(a.shape[1:], BF16) for a in staged]
                        + [pltpu.SemaphoreType.DMA((len(staged),))]),
        compiler_params=pltpu.CompilerParams(dimension_semantics=("arbitrary",),
                                             vmem_limit_bytes=VMEM_LIMIT),
    )(x2, ys, ya, g1, sc2, sh2, g2, ng, w_out, sg, su, sd, rwh, rwl, router_bias, upper)


def _route_tokens(logits_t, rb_ref, upper_ref, idx_ref, gate_ref, rank_ref, cnt_ref, carry_ref):
    t = logits_t.shape[1]
    per_group = N_EXPERTS // ROUTE_GROUPS
    scores = 1.0 / (1.0 + jnp.exp(-logits_t))
    sel = scores + rb_ref[...]
    e_iota = lax.broadcasted_iota(jnp.int32, (N_EXPERTS, t), 0)

    sel3 = sel.reshape(ROUTE_GROUPS, per_group, t)
    w_iota = lax.broadcasted_iota(jnp.int32, sel3.shape, 1)
    m1 = jnp.max(sel3, axis=1, keepdims=True)
    first = jnp.min(jnp.where(sel3 == m1, w_iota, per_group), axis=1, keepdims=True)
    m2 = jnp.max(jnp.where(w_iota == first, NEG_INF, sel3), axis=1, keepdims=True)
    grp = (m1 + m2).reshape(ROUTE_GROUPS, t)

    g_iota = lax.broadcasted_iota(jnp.int32, (ROUTE_GROUPS, t), 0)
    gmask = jnp.zeros((ROUTE_GROUPS, t), jnp.bool_)
    for _ in range(ROUTE_TOPK_GROUPS):
        gm = jnp.max(grp, axis=0, keepdims=True)
        gfirst = jnp.min(jnp.where(grp == gm, g_iota, ROUTE_GROUPS), axis=0, keepdims=True)
        hit = g_iota == gfirst
        gmask = jnp.logical_or(gmask, hit)
        grp = jnp.where(hit, NEG_INF, grp)
    allowed = jnp.broadcast_to(gmask.reshape(ROUTE_GROUPS, 1, t),
                               (ROUTE_GROUPS, per_group, t)).reshape(N_EXPERTS, t)
    masked = jnp.where(allowed, sel, NEG_INF)

    picked = jnp.zeros((N_EXPERTS, t), jnp.bool_)
    idx_rows = []
    w_rows = []
    for _ in range(TOP_K):
        mm = jnp.max(masked, axis=0, keepdims=True)
        efirst = jnp.min(jnp.where(masked == mm, e_iota, N_EXPERTS), axis=0, keepdims=True)
        hit = e_iota == efirst
        idx_rows.append(efirst)
        w_rows.append(jnp.sum(jnp.where(hit, scores, 0.0), axis=0, keepdims=True))
        picked = jnp.logical_or(picked, hit)
        masked = jnp.where(hit, NEG_INF, masked)
    idx = jnp.concatenate(idx_rows, axis=0)
    w = jnp.concatenate(w_rows, axis=0)
    gate_ref[...] = w / jnp.sum(w, axis=0, keepdims=True) * ROUTED_SCALE
    idx_ref[...] = idx

    onehot = jnp.where(picked, 1.0, 0.0)
    sub = upper_ref.shape[0]
    carry = carry_ref[:, 0:1]
    parts = []
    for s0 in range(0, t, sub):
        oh = onehot[:, s0:s0 + sub]
        parts.append(jnp.dot(oh.astype(BF16), upper_ref[...], preferred_element_type=F32) + carry)
        carry = carry + jnp.sum(oh, axis=1, keepdims=True)
    rank_full = jnp.concatenate(parts, axis=1)
    rank_rows = [jnp.sum(jnp.where(e_iota == idx_rows[k], rank_full, 0.0), axis=0, keepdims=True)
                 for k in range(TOP_K)]
    rank_ref[...] = jnp.concatenate(rank_rows, axis=0).astype(jnp.int32)
    carry_ref[...] = jnp.broadcast_to(carry, carry_ref.shape)
    cnt_ref[...] = carry_ref[...]


PLAN_EXPERT, PLAN_FRESH, PLAN_VALID, PLAN_SEG, PLAN_NEXT, PLAN_NUSED = range(6)
PLAN_ROWS = SUBLANES


def _plan_kernel(cnt_ref, tri_ref, start_ref, plan_ref, *, nblocks):
    nbp = plan_ref.shape[1]
    cnt = cnt_ref[...].astype(jnp.int32)
    blocks = (cnt + (EXPERT_ROWS - 1)) // EXPERT_ROWS
    end = jnp.dot(tri_ref[...], blocks.astype(F32), precision=lax.Precision.HIGHEST,
                  preferred_element_type=F32).astype(jnp.int32)
    start = end - blocks
    start_ref[...] = start[:, 0:1] * EXPERT_ROWS
    nused = end[N_EXPERTS - 1:N_EXPERTS, 0:1]

    e_iota = lax.broadcasted_iota(jnp.int32, (N_EXPERTS, nbp), 0)
    blk = lax.broadcasted_iota(jnp.int32, (1, nbp), 1)
    expert = jnp.minimum(jnp.sum((end[:, 0:1] <= blk).astype(jnp.int32), axis=0, keepdims=True), N_EXPERTS - 1)
    mine = e_iota == expert
    pick = lambda col: jnp.sum(jnp.where(mine, col, 0), axis=0, keepdims=True)
    first = pick(start[:, 0:1])
    valid = jnp.clip(pick(cnt[:, 0:1]) - (blk - first) * EXPERT_ROWS, 0, EXPERT_ROWS)
    present = jnp.logical_or(blocks[:, 0:1] > 0,
                             jnp.logical_and(e_iota[:, 0:1] == N_EXPERTS - 1, nused < nblocks))
    seg = jnp.sum(jnp.logical_and(present, e_iota <= expert).astype(jnp.int32), axis=0, keepdims=True) - 1
    nxt = jnp.min(jnp.where(jnp.logical_and(present, e_iota > expert), e_iota, N_EXPERTS), axis=0, keepdims=True)
    rows = {PLAN_EXPERT: expert, PLAN_FRESH: (blk == first).astype(jnp.int32), PLAN_VALID: valid, PLAN_SEG: seg,
            PLAN_NEXT: jnp.where(nxt == N_EXPERTS, -1, nxt), PLAN_NUSED: jnp.broadcast_to(nused, (1, nbp))}
    zero = jnp.zeros((1, nbp), jnp.int32)
    plan_ref[...] = jnp.concatenate([rows.get(r, zero) for r in range(PLAN_ROWS)], axis=0)


def _plan_call(counts, nblocks):
    nbp = -(-nblocks // LANES) * LANES
    tri = jnp.asarray(np.tril(np.ones((N_EXPERTS, N_EXPERTS), np.float32)))
    return pl.pallas_call(
        functools.partial(_plan_kernel, nblocks=nblocks),
        out_shape=[jax.ShapeDtypeStruct((N_EXPERTS, 1), jnp.int32),
                   jax.ShapeDtypeStruct((PLAN_ROWS, nbp), jnp.int32)],
    )(counts, tri)


def _dest_kernel(idx_ref, rank_ref, start_ref, dest_ref):
    t = idx_ref.shape[1]
    e_iota = lax.broadcasted_iota(jnp.int32, (N_EXPERTS, t), 0)
    rows = [jnp.sum(jnp.where(e_iota == idx_ref[k:k + 1, :], start_ref[...], 0), axis=0, keepdims=True)
            for k in range(TOP_K)]
    dest = jnp.concatenate(rows, axis=0) + rank_ref[...]
    for k in range(TOP_K):
        for c in range(t // LANES):
            dest_ref[k, c:c + 1, :] = dest[k:k + 1, c * LANES:(c + 1) * LANES]


def _dest_call(idx, rank, pad_start, tile):
    n = idx.shape[1]
    tok = pl.BlockSpec((TOP_K, tile), lambda i: (0, i))
    return pl.pallas_call(
        _dest_kernel,
        grid=(n // tile,),
        in_specs=[tok, tok, pl.BlockSpec((N_EXPERTS, 1), lambda i: (0, 0))],
        out_specs=pl.BlockSpec((TOP_K, tile // LANES, LANES), lambda i: (0, i, 0)),
        out_shape=jax.ShapeDtypeStruct((TOP_K, n // LANES, LANES), jnp.int32),
        compiler_params=pltpu.CompilerParams(dimension_semantics=("arbitrary",)),
    )(idx, rank, pad_start)


def _scatter_rows_sc(rows, dest_flat, total_rows, chunk):
    n, w = rows.shape
    copies = dest_flat.shape[0] // n
    info = plsc.get_sparse_core_info()
    nc = info.num_cores
    per_worker = n // (nc * info.num_subcores)
    assert per_worker * nc * info.num_subcores == n and per_worker % chunk == 0
    mesh = plsc.VectorSubcoreMesh(core_axis_name="c", subcore_axis_name="s")

    @functools.partial(
        pl.kernel, mesh=mesh,
        out_type=jax.ShapeDtypeStruct((total_rows, w), rows.dtype),
        scratch_types=([pltpu.VMEM((chunk,), jnp.int32) for _ in range(copies)]
                       + [pltpu.VMEM((chunk, w), rows.dtype), pltpu.SemaphoreType.DMA, pltpu.SemaphoreType.DMA]),
    )
    def scatter(rows_hbm, idx_hbm, out_hbm, *scratch):
        idx_vs, (rows_v, isem, sem) = scratch[:copies], scratch[copies:]
        worker = lax.axis_index("s") * nc + lax.axis_index("c")

        @pl.loop(0, per_worker // chunk)
        def _(j):
            base = worker * per_worker + j * chunk
            loads = [pltpu.async_copy(idx_hbm.at[pl.ds(k * n + base, chunk)], idx_vs[k], isem)
                     for k in range(copies)]
            pltpu.sync_copy(rows_hbm.at[pl.ds(base, chunk)], rows_v)
            for ld in loads:
                ld.wait()
            stores = [pltpu.async_copy(rows_v, out_hbm.at[idx_vs[k]], sem) for k in range(copies)]
            for st in stores:
                st.wait()

    return scatter(rows, dest_flat)


X_RING = 3


def _expert_kernel(plan_ref, xs_hbm, wg_hbm, wu_hbm, wd_hbm, y_ref,
                   wgb_ref, wub_ref, wdb_ref, xbuf_ref, xsem, wg_raw, wu_raw, wd_raw, wsem):
    i = pl.program_id(0)
    rows, half = xbuf_ref.shape[1], xbuf_ref.shape[2]
    nused = plan_ref[PLAN_NUSED, 0]

    def w_copies(expert, slot):
        return [pltpu.make_async_copy(src.at[expert], dst.at[slot], wsem.at[slot])
                for src, dst in ((wg_hbm, wg_raw), (wu_hbm, wu_raw), (wd_hbm, wd_raw))]

    def x_copy(block):
        first = pl.multiple_of(block * rows, rows)
        slot = block % X_RING
        return pltpu.make_async_copy(xs_hbm.at[pl.ds(first, rows)], xbuf_ref.at[slot], xsem.at[slot])

    @pl.when(i == 0)
    def _():
        for b in range(X_RING - 1):
            @pl.when(b < nused)
            def _():
                x_copy(b).start()

    @pl.when(i + (X_RING - 1) < nused)
    def _():
        x_copy(i + (X_RING - 1)).start()

    @pl.when(i == 0)
    def _():
        for c in w_copies(plan_ref[PLAN_EXPERT, 0], 0):
            c.start()

    @pl.when(plan_ref[PLAN_FRESH, i] > 0)
    def _():
        slot = plan_ref[PLAN_SEG, i] % 2
        for c in w_copies(plan_ref[PLAN_EXPERT, i], slot):
            c.wait()

        @pl.when(plan_ref[PLAN_NEXT, i] >= 0)
        def _():
            for c in w_copies(plan_ref[PLAN_NEXT, i], 1 - slot):
                c.start()

        wgb_ref[...] = wg_raw[slot].astype(BF16)
        wub_ref[...] = wu_raw[slot].astype(BF16)
        wdb_ref[...] = wd_raw[slot].astype(BF16)

    @pl.when(i < nused)
    def _():
        x_copy(i).wait()
        xw = xbuf_ref[i % X_RING]
        row = lax.broadcasted_iota(jnp.int32, xw.shape, 0)
        x_lo, x_hi = _unpack_bf16_pair(jnp.where(row < plan_ref[PLAN_VALID, i], xw, jnp.uint32(0)))
        x_lo = x_lo.astype(BF16)
        x_hi = x_hi.astype(BF16)
        gate = (jnp.dot(x_lo, wgb_ref[:half, :], preferred_element_type=F32)
                + jnp.dot(x_hi, wgb_ref[half:, :], preferred_element_type=F32))
        up = (jnp.dot(x_lo, wub_ref[:half, :], preferred_element_type=F32)
              + jnp.dot(x_hi, wub_ref[half:, :], preferred_element_type=F32))
        u = (_silu(gate) * up).astype(BF16)
        y_lo = jnp.dot(u, wdb_ref[:, :half], preferred_element_type=F32)
        y_hi = jnp.dot(u, wdb_ref[:, half:], preferred_element_type=F32)
        y_ref[...] = _pack_bf16_pair(y_lo, y_hi)

    @pl.when(i >= nused)
    def _():
        y_ref[...] = jnp.zeros_like(y_ref)


def _expert_call(plan, xs, wg, wu, wd, rows):
    p, w = xs.shape
    d, f = wg.shape[1], wg.shape[2]
    hbm = pl.BlockSpec(memory_space=pl.ANY)
    grid_spec = pltpu.PrefetchScalarGridSpec(
        num_scalar_prefetch=1,
        grid=(p // rows,),
        in_specs=[hbm, hbm, hbm, hbm],
        out_specs=pl.BlockSpec((rows, w), lambda i, *_: (i, 0)),
        scratch_shapes=[pltpu.VMEM((d, f), BF16), pltpu.VMEM((d, f), BF16), pltpu.VMEM((f, d), BF16),
                        pltpu.VMEM((X_RING, rows, w), xs.dtype), pltpu.SemaphoreType.DMA((X_RING,)),
                        pltpu.VMEM((2, d, f), wg.dtype), pltpu.VMEM((2, d, f), wu.dtype),
                        pltpu.VMEM((2, f, d), wd.dtype), pltpu.SemaphoreType.DMA((2,))],
    )
    return pl.pallas_call(
        _expert_kernel,
        grid_spec=grid_spec,
        out_shape=jax.ShapeDtypeStruct((p, w), jnp.uint32),
        compiler_params=pltpu.CompilerParams(dimension_semantics=("arbitrary",),
                                             vmem_limit_bytes=VMEM_LIMIT),
    )(plan, xs, wg, wu, wd)


def _gather_rows_sc(table, idx, chunk):
    m = idx.shape[0]
    w = table.shape[1]
    info = plsc.get_sparse_core_info()
    nc = info.num_cores
    per_worker = m // (nc * info.num_subcores)
    assert per_worker * nc * info.num_subcores == m and per_worker % chunk == 0
    part = chunk // SC_GATHER_PARTS
    mesh = plsc.VectorSubcoreMesh(core_axis_name="c", subcore_axis_name="s")

    @functools.partial(
        pl.kernel, mesh=mesh,
        out_type=jax.ShapeDtypeStruct((m, w), table.dtype),
        scratch_types=([pltpu.VMEM((part,), jnp.int32) for _ in range(SC_GATHER_PARTS)]
                       + [pltpu.VMEM((part, w), table.dtype) for _ in range(SC_GATHER_PARTS)]
                       + [pltpu.SemaphoreType.DMA] * 3),
    )
    def gather(table_hbm, idx_hbm, out_hbm, *scratch):
        idx_vs = scratch[:SC_GATHER_PARTS]
        rows_vs = scratch[SC_GATHER_PARTS:2 * SC_GATHER_PARTS]
        isem, gsem, wsem = scratch[2 * SC_GATHER_PARTS:]
        worker = lax.axis_index("s") * nc + lax.axis_index("c")

        @pl.loop(0, per_worker // chunk)
        def _(j):
            base = worker * per_worker + j * chunk
            loads = [pltpu.async_copy(idx_hbm.at[pl.ds(base + p * part, part)], idx_vs[p], isem)
                     for p in range(SC_GATHER_PARTS)]
            gathers = []
            for p in range(SC_GATHER_PARTS):
                loads[p].wait()
                gathers.append(pltpu.async_copy(table_hbm.at[idx_vs[p]], rows_vs[p], gsem))
            writes = []
            for p in range(SC_GATHER_PARTS):
                gathers[p].wait()
                writes.append(pltpu.async_copy(rows_vs[p], out_hbm.at[pl.ds(base + p * part, part)], wsem))
            for wr in writes:
                wr.wait()

    return gather(table, idx)


def _combine_kernel(yk_ref, gate_ref, base_ref, g2_ref, fg_ref, o_ref):
    t = base_ref.shape[0]
    half = yk_ref.shape[2]
    gates = jnp.concatenate([gate_ref[...], jnp.zeros((LANES - TOP_K, t), F32)], axis=0).T
    r_lo = jnp.zeros((t, half), F32)
    r_hi = jnp.zeros((t, half), F32)
    for k in range(TOP_K):
        y_lo, y_hi = _unpack_bf16_pair(yk_ref[k])
        r_lo = r_lo + gates[:, k:k + 1] * y_lo
        r_hi = r_hi + gates[:, k:k + 1] * y_hi
    g2 = g2_ref[0]
    x_lo = base_ref[:, :half] + g2[:, :half] * r_lo
    x_hi = base_ref[:, half:] + g2[:, half:] * r_hi
    ms = (jnp.sum(x_lo * x_lo, axis=-1, keepdims=True)
          + jnp.sum(x_hi * x_hi, axis=-1, keepdims=True)) * (1.0 / (2 * half))
    inv = lax.rsqrt(ms + EPS)
    o_ref[:, :half] = x_lo * inv * fg_ref[:, :half]
    o_ref[:, half:] = x_hi * inv * fg_ref[:, half:]


def _combine_call(yk, gates, base, g2, fg, tiles_per_batch, tile, tile0):
    n, d = base.shape
    row = lambda w: pl.BlockSpec((tile, w), lambda i: (i + tile0, 0))
    return pl.pallas_call(
        _combine_kernel,
        grid=(yk.shape[1] // tile,),
        in_specs=[pl.BlockSpec((TOP_K, tile, yk.shape[2]), lambda i: (0, i, 0)),
                  pl.BlockSpec((TOP_K, tile), lambda i: (0, i + tile0)), row(d),
                  pl.BlockSpec((1, 1, d), lambda i: ((i + tile0) // tiles_per_batch, 0, 0)),
                  pl.BlockSpec((1, d), lambda i: (0, 0))],
        out_specs=row(d),
        out_shape=jax.ShapeDtypeStruct((n, d), F32),
        input_output_aliases={2: 0},
        compiler_params=pltpu.CompilerParams(dimension_semantics=("arbitrary",),
                                             vmem_limit_bytes=VMEM_LIMIT),
    )(yk, gates, base, g2, fg)


def _pad_cols(a, width):
    return jnp.pad(a, ((0, 0), (0, width - a.shape[1])))


def _layer(x, mod, norm1_g, norm2_g, w_in, conv_w, conv_b, dt_bias, a_log, d_skip, ssm_norm_g,
           att_norm_g, sinks, rel_bias, w_out, router_w, router_bias, exp_w_gate, exp_w_up, exp_w_down,
           sh_w_gate, sh_w_up, sh_w_down, final_g):
    bsz, l, d = x.shape
    n = bsz * l
    tm = min(ROW_TILE, l)

    sh1, sc1, g1, sh2, sc2, g2 = [m[:, None, :] for m in jnp.split(mod, 6, axis=-1)]

    assert math.frexp(ATT_HEAD_DIM ** -0.5)[0] == 0.5
    x2 = x.reshape(n, d)
    z, xbc, dt, q, k, v = _in_proj_call(x2, sc1, sh1, norm1_g[None, :], jnp.swapaxes(w_in, 1, 2), l // tm, tm)

    triu = jnp.asarray(np.triu(np.ones((CHUNK, CHUNK), np.float32))).astype(BF16)
    shift = jnp.asarray(_conv_shift_matrix()).astype(BF16)
    y_ssm = _ssd_call(xbc.reshape(bsz, l, CONV_CH), z.reshape(bsz, l, D_SSM), dt.reshape(bsz, l, LANES),
                      conv_w, conv_b[None, :], dt_bias[:, None], a_log[:, None],
                      jnp.repeat(d_skip, SSM_HEAD_DIM)[None, :], ssm_norm_g[None, :], triu, shift)

    bias = _bias_call(rel_bias, jnp.asarray(_rel_bucket_table()))
    y_att = _attn_call(sinks, q.reshape(bsz, l, D_ATT), k.reshape(bsz, l, D_KV), v.reshape(bsz, l, D_KV), bias,
                       att_norm_g[None, :])

    rw = _pad_cols(router_w, LANES)
    rwh = rw.astype(BF16)
    rwl = jnp.concatenate([rwh, (rw - rwh.astype(F32)).astype(BF16)], axis=1)
    rs = min(RANK_SUB, tm)
    upper = jnp.asarray(np.triu(np.ones((rs, rs), np.float32), 1)).astype(BF16)
    base, h2, idx, gates, rank, counts = _out_proj_call(
        x2, y_ssm.reshape(n, D_SSM), y_att.reshape(n, D_ATT), g1, sc2, sh2, g2, norm2_g[None, :],
        w_out, sh_w_gate, sh_w_up, sh_w_down, rwh, rwl, router_bias[:, None], upper, l // tm, tm)
    rt = min(ROUTE_TILE, n)

    nblocks = (n * TOP_K + N_EXPERTS * (EXPERT_ROWS - 1) + EXPERT_ROWS - 1) // EXPERT_ROWS
    pad_start, plan = _plan_call(counts, nblocks)
    dest = _dest_call(idx, rank, pad_start, rt)

    xs = _scatter_rows_sc(h2, dest.reshape(-1), nblocks * EXPERT_ROWS, SC_CHUNK)
    ys = _expert_call(plan, xs, exp_w_gate, exp_w_up, exp_w_down, EXPERT_ROWS)
    ctile = min(COMBINE_TILE, l)
    groups = COMBINE_GROUPS if bsz % COMBINE_GROUPS == 0 else 1
    ng = n // groups
    out = base
    for g in range(groups):
        idx_g = dest[:, g * ng // LANES:(g + 1) * ng // LANES, :].reshape(-1)
        yk = _gather_rows_sc(ys, idx_g, SC_CHUNK).reshape(TOP_K, ng, ys.shape[1])
        out = _combine_call(yk, gates, out, g2, final_g[None, :], l // ctile, ctile, g * ng // ctile)
    return out.reshape(bsz, l, d)


def kernel(x, c, mod_w, mod_b, norm1_g, norm2_g, w_in, conv_w, conv_b, dt_bias, a_log, d_skip, ssm_norm_g,
           att_norm_g, sinks, rel_bias, w_out, router_w, router_bias, exp_w_gate, exp_w_up, exp_w_down,
           sh_w_gate, sh_w_up, sh_w_down, final_g):
    assert mod_w.shape[0] == 1, "single-layer block"
    bsz = x.shape[0]
    c_pad = jnp.pad(c, ((0, SUBLANES - bsz % SUBLANES if bsz % SUBLANES else 0), (0, 0)))
    mod = _mod_call(c_pad, mod_w[0], mod_b[0][None, :])[:bsz]
    per = bsz // 2
    outs = [_layer(x[i * per:(i + 1) * per], mod[i * per:(i + 1) * per], norm1_g[0], norm2_g[0], w_in, conv_w[0],
                   conv_b[0], dt_bias[0], a_log[0], d_skip[0],
                   ssm_norm_g[0], att_norm_g[0], sinks[0], rel_bias, w_out, router_w[0], router_bias[0],
                   exp_w_gate[0], exp_w_up[0], exp_w_down[0], sh_w_gate, sh_w_up, sh_w_down, final_g)
            for i in range(2)]
    return jnp.concatenate(outs, axis=0)
```

```python
import functools
import math

import numpy as np
import jax
import jax.numpy as jnp
from jax import lax
from jax.experimental import pallas as pl
from jax.experimental.pallas import tpu as pltpu
from jax.experimental.pallas import tpu_sc as plsc

F32 = jnp.float32
BF16 = jnp.bfloat16

D_MODEL = 1024
SSM_HEAD_DIM = 64
D_SSM = D_MODEL
SSM_HEADS = D_SSM // SSM_HEAD_DIM
SSM_GROUPS = 4
D_STATE = 128
CONV_K = 4
CONV_CH = D_SSM + 2 * SSM_GROUPS * D_STATE
CHUNK = 128
ATT_HEAD_DIM = 64
D_ATT = D_MODEL
ATT_HEADS = D_ATT // ATT_HEAD_DIM
KV_HEADS = ATT_HEADS // 4
Q_PER_KV = ATT_HEADS // KV_HEADS
D_KV = KV_HEADS * ATT_HEAD_DIM
WINDOW = 128
ATT_BLOCK = 128
REL_BUCKETS = 32
REL_MAX_DIST = 128
N_EXPERTS = 64
TOP_K = 8
EXPERT_DIM = D_MODEL // 4
SHARED_DIM = D_MODEL // 4
ROUTE_GROUPS = 8
ROUTE_TOPK_GROUPS = 4
ROUTED_SCALE = 2.5
EPS = 1e-6

LANES = 128
SUBLANES = 8
HALF = LANES // 2

ROW_TILE = 512
ROUTE_TILE = 2048
RANK_SUB = 512
COMBINE_TILE = 512
SC_CHUNK = 128
SC_GATHER_PARTS = 4
EXPERT_ROWS = 512
COMBINE_GROUPS = 2
VMEM_LIMIT = 48 * 1024 * 1024

NEG_INF = float("-inf")


def _silu(v):
    return v * (1.0 / (1.0 + jnp.exp(-v)))


def _softplus(v):
    return jnp.maximum(v, 0.0) + jnp.log(1.0 + jnp.exp(-jnp.abs(v)))


def _split_hi_lo(v):
    hi = v.astype(BF16)
    lo = (v - hi.astype(F32)).astype(BF16)
    return hi, lo


def _pack_bf16_pair(a, b):
    w = pltpu.pack_elementwise([a, b], packed_dtype=BF16)
    return w if w.dtype == jnp.uint32 else lax.bitcast_convert_type(w, jnp.uint32)


def _unpack_bf16_pair(w):
    a = pltpu.unpack_elementwise(w, index=0, packed_dtype=BF16, unpacked_dtype=F32)
    b = pltpu.unpack_elementwise(w, index=1, packed_dtype=BF16, unpacked_dtype=F32)
    return a, b


def _lane_half_mask(shape):
    return lax.broadcasted_iota(jnp.int32, shape, len(shape) - 1) < HALF


def _mod_kernel(c_ref, w_ref, b_ref, o_ref):
    a = _silu(c_ref[...])
    o_ref[...] = jnp.dot(a, w_ref[...], precision=lax.Precision.HIGHEST,
                         preferred_element_type=F32) + b_ref[...]


def _mod_call(c_pad, mod_w, mod_b):
    rows, d = c_pad.shape
    cols = mod_w.shape[1]
    return pl.pallas_call(
        _mod_kernel,
        grid=(cols // d,),
        in_specs=[pl.BlockSpec((rows, d), lambda j: (0, 0)),
                  pl.BlockSpec((d, d), lambda j: (0, j)),
                  pl.BlockSpec((1, d), lambda j: (0, j))],
        out_specs=pl.BlockSpec((rows, d), lambda j: (0, j)),
        out_shape=jax.ShapeDtypeStruct((rows, cols), F32),
        compiler_params=pltpu.CompilerParams(dimension_semantics=("arbitrary",),
                                             vmem_limit_bytes=VMEM_LIMIT),
    )(c_pad, mod_w, mod_b)


IN_PROJ_SEGMENTS = ((D_SSM, BF16), (CONV_CH, BF16), (LANES, F32), (D_ATT, BF16), (D_KV, BF16), (D_KV, BF16))


def _in_proj_kernel(x_ref, sc_ref, sh_ref, g_ref, w_hbm, *refs):
    out_refs = refs[:len(IN_PROJ_SEGMENTS)]
    wraw_ref, w_ref, wsem = refs[len(IN_PROJ_SEGMENTS):]

    @pl.when(pl.program_id(0) == 0)
    def _():
        cp = pltpu.make_async_copy(w_hbm.at[0], wraw_ref, wsem)
        cp.start()
        cp.wait()
        src_dt = D_SSM + CONV_CH
        dst_q = src_dt + LANES
        src_q = src_dt + SSM_HEADS
        q_scale = ATT_HEAD_DIM ** -0.5

        def put(dst, src, scale=None):
            t = wraw_ref[src:src + LANES, :].T
            w_ref[:, dst:dst + LANES] = (t if scale is None else t * scale).astype(BF16)

        for c0 in range(0, src_dt, LANES):
            put(c0, c0)
        dt_tile = wraw_ref[src_dt:src_dt + LANES, :].T
        lane = lax.broadcasted_iota(jnp.int32, dt_tile.shape, 1)
        w_ref[:, src_dt:dst_q] = jnp.where(lane < SSM_HEADS, dt_tile, 0.0).astype(BF16)
        for c0 in range(0, D_ATT, LANES):
            put(dst_q + c0, src_q + c0, q_scale)
        for c0 in range(D_ATT, D_ATT + 2 * D_KV, LANES):
            put(dst_q + c0, src_q + c0)

    xf = x_ref[...]
    ms = jnp.mean(xf * xf, axis=-1, keepdims=True)
    h = xf * lax.rsqrt(ms + EPS) * g_ref[...]
    h = h * (1.0 + sc_ref[0]) + sh_ref[0]
    hb = h.astype(BF16)
    col = 0
    for (width, dtype), o_ref in zip(IN_PROJ_SEGMENTS, out_refs):
        o_ref[...] = jnp.dot(hb, w_ref[:, col:col + width], preferred_element_type=F32).astype(dtype)
        col += width


def _in_proj_call(x2, sc1, sh1, g1n, w_in_t, tiles_per_batch, tm):
    n, d = x2.shape
    row = lambda w: pl.BlockSpec((tm, w), lambda i: (i, 0))
    full = lambda a: pl.BlockSpec(a.shape, lambda i: (0, 0))
    per_batch = pl.BlockSpec((1, 1, d), lambda i: (i // tiles_per_batch, 0, 0))
    cols = sum(w for w, _ in IN_PROJ_SEGMENTS)
    w_in = w_in_t
    assert w_in.shape == (1, cols - (LANES - SSM_HEADS), d)
    return pl.pallas_call(
        _in_proj_kernel,
        grid=(n // tm,),
        in_specs=[row(d), per_batch, per_batch, full(g1n), pl.BlockSpec(memory_space=pl.ANY)],
        out_specs=[row(w) for w, _ in IN_PROJ_SEGMENTS],
        out_shape=[jax.ShapeDtypeStruct((n, w), dt) for w, dt in IN_PROJ_SEGMENTS],
        scratch_shapes=[pltpu.VMEM(w_in.shape[1:], w_in.dtype), pltpu.VMEM((d, cols), BF16), pltpu.SemaphoreType.DMA(())],
        compiler_params=pltpu.CompilerParams(dimension_semantics=("arbitrary",),
                                             vmem_limit_bytes=VMEM_LIMIT),
    )(x2, sc1, sh1, g1n, w_in)


SSD_SEQS = 4
CONV_HALO = 16


def _conv_shift_matrix():
    s = np.zeros((CONV_K * CHUNK, CONV_HALO + CHUNK), np.float32)
    for k in range(CONV_K):
        t = np.arange(CHUNK)
        s[k * CHUNK + t, CONV_HALO + t - (CONV_K - 1) + k] = 1.0
    return s


def _silu_tanh(v):
    hv = 0.5 * v
    return hv + hv * jnp.tanh(hv)


def _ssd_kernel(xbc_ref, z_ref, dt_ref, cw_ref, cb_ref, dtb_ref, alog_ref, dskip_ref, ng_ref, triu_ref, shift_ref,
                y_ref, state_ref, ucat_ref, ybuf_ref):
    nseq = xbc_ref.shape[0]

    @pl.when(pl.program_id(1) == 0)
    def _():
        state_ref[...] = jnp.zeros_like(state_ref)
        ucat_ref[:, 0:CONV_HALO, :] = jnp.zeros((nseq, CONV_HALO, CONV_CH), BF16)

    for q in range(nseq):
        _ssd_chunk(xbc_ref.at[q], z_ref.at[q], dt_ref.at[q], cw_ref, cb_ref, dtb_ref, alog_ref, dskip_ref, ng_ref,
                   triu_ref, shift_ref, y_ref.at[q], state_ref.at[q], ucat_ref.at[q], ybuf_ref.at[q])


def _ssd_chunk(xbc_ref, z_ref, dt_ref, cw_ref, cb_ref, dtb_ref, alog_ref, dskip_ref, ng_ref, triu_ref, shift_ref,
               y_ref, state_ref, ucat_ref, ybuf_ref):
    ucat_ref[CONV_HALO:, :] = xbc_ref[...]
    shifted = jnp.dot(shift_ref[...], ucat_ref[...], preferred_element_type=F32)
    ucat_ref[0:CONV_HALO, :] = ucat_ref[CHUNK:CHUNK + CONV_HALO, :]
    acc = cb_ref[...] + cw_ref[0:1, :] * shifted[0:CHUNK]
    for kk in range(1, CONV_K):
        acc = acc + cw_ref[kk:kk + 1, :] * shifted[kk * CHUNK:(kk + 1) * CHUNK]
    act = _silu_tanh(acc)
    xs = act[:, :D_SSM]
    gn = SSM_GROUPS * D_STATE

    dt_t = _softplus(dt_ref[...].T[0:SSM_HEADS, :] + dtb_ref[...])
    a_t = dt_t * (-jnp.exp(alog_ref[...]))
    a_hi = a_t.astype(BF16)
    a_mid = (a_t - a_hi.astype(F32)).astype(BF16)
    a_lo = (a_t - a_hi.astype(F32) - a_mid.astype(F32)).astype(BF16)
    triu = triu_ref[...]
    cs_t = (jnp.dot(a_hi, triu, preferred_element_type=F32) + jnp.dot(a_mid, triu, preferred_element_type=F32)
            + jnp.dot(a_lo, triu, preferred_element_type=F32))
    cs_end = cs_t[:, CHUNK - 1:CHUNK]
    r_t = cs_t - jnp.log(dt_t)
    w_t = jnp.exp(cs_end - cs_t) * dt_t
    chunk_decay = jnp.exp(cs_end)
    cols = jnp.concatenate([cs_t, jnp.exp(cs_t), jnp.zeros((LANES - 2 * SSM_HEADS, CHUNK), F32)], axis=0).T

    li = lax.broadcasted_iota(jnp.int32, (CHUNK, CHUNK), 0)
    si = lax.broadcasted_iota(jnp.int32, (CHUNK, CHUNK), 1)
    causal = li >= si
    low = _lane_half_mask((CHUNK, LANES))
    low_row = _lane_half_mask((1, LANES))

    heads_per_group = SSM_HEADS // SSM_GROUPS
    for g in range(SSM_GROUPS):
        b_g = act[:, D_SSM + g * D_STATE:D_SSM + (g + 1) * D_STATE]
        c_g = act[:, D_SSM + gn + g * D_STATE:D_SSM + gn + (g + 1) * D_STATE]
        b_gb = b_g.astype(BF16)
        c_gb = c_g.astype(BF16)
        cb = lax.dot_general(c_gb, b_gb, (((1,), (1,)), ((), ())), preferred_element_type=F32)
        b_t = b_g.T
        for jp in range(heads_per_group // 2):
            j = g * (heads_per_group // 2) + jp
            lanes = slice(j * LANES, (j + 1) * LANES)
            xp = xs[:, lanes]
            xpb = xp.astype(BF16)
            ydiag = jnp.zeros((CHUNK, LANES), F32)
            snew = jnp.zeros((D_STATE, LANES), F32)
            for half in range(2):
                h = 2 * j + half
                diff = cols[:, h:h + 1] - r_t[h:h + 1, :]
                m = (cb * jnp.exp(jnp.where(causal, diff, NEG_INF))).astype(BF16)
                keep = low if half == 0 else jnp.logical_not(low)
                xh = jnp.where(keep, xpb, jnp.zeros_like(xpb))
                ydiag = ydiag + jnp.dot(m, xh, preferred_element_type=F32)
                snew = snew + jnp.dot((b_t * w_t[h:h + 1, :]).astype(BF16), xh, preferred_element_type=F32)
            s_in = state_ref[:, lanes]
            yoff = jnp.dot(c_gb, s_in.astype(BF16), preferred_element_type=F32)
            h0 = 2 * j
            e0 = SSM_HEADS + h0
            escale = jnp.where(low, cols[:, e0:e0 + 1], cols[:, e0 + 1:e0 + 2])
            cdec = jnp.where(low_row, chunk_decay[h0:h0 + 1, :], chunk_decay[h0 + 1:h0 + 2, :])
            ybuf_ref[:, lanes] = ydiag + yoff * escale + xp * dskip_ref[:, lanes]
            state_ref[:, lanes] = s_in * cdec + snew

    yz = ybuf_ref[...] * _silu_tanh(z_ref[...].astype(F32))
    gw = D_SSM // SSM_GROUPS
    for g in range(SSM_GROUPS):
        part = yz[:, g * gw:(g + 1) * gw]
        ms = jnp.mean(part * part, axis=-1, keepdims=True)
        y_ref[:, g * gw:(g + 1) * gw] = (part * lax.rsqrt(ms + EPS)
                                            * ng_ref[:, g * gw:(g + 1) * gw]).astype(BF16)


def _ssd_call(xbc, z, dt, conv_w, conv_b, dtb, alog, dskip, ng, triu, shift):
    bsz, l, _ = xbc.shape
    nc = l // CHUNK
    nseq = SSD_SEQS if bsz % SSD_SEQS == 0 else 1
    chunk = lambda w: pl.BlockSpec((nseq, CHUNK, w), lambda b, c: (b, c, 0))
    full = lambda a: pl.BlockSpec(a.shape, lambda b, c: (0, 0))
    return pl.pallas_call(
        _ssd_kernel,
        grid=(bsz // nseq, nc),
        in_specs=[chunk(CONV_CH), chunk(D_SSM), chunk(LANES), full(conv_w), full(conv_b), full(dtb),
                  full(alog), full(dskip), full(ng), full(triu), full(shift)],
        out_specs=chunk(D_SSM),
        out_shape=jax.ShapeDtypeStruct((bsz, l, D_SSM), BF16),
        scratch_shapes=[pltpu.VMEM((nseq, D_STATE, D_SSM), F32),
                        pltpu.VMEM((nseq, CONV_HALO + CHUNK, CONV_CH), BF16),
                        pltpu.VMEM((nseq, CHUNK, D_SSM), F32)],
        compiler_params=pltpu.CompilerParams(dimension_semantics=("arbitrary", "arbitrary"),
                                             vmem_limit_bytes=VMEM_LIMIT),
    )(xbc, z, dt, conv_w, conv_b, dtb, alog, dskip, ng, triu, shift)


assert WINDOW == ATT_BLOCK


def _rel_bucket_table():
    qi = np.arange(ATT_BLOCK)[:, None]
    c = np.arange(ATT_BLOCK)[None, :]
    dist = np.where(c > qi, qi + ATT_BLOCK - c, qi - c)
    max_exact = REL_BUCKETS // 2
    d = np.maximum(dist, 1).astype(np.float32)
    large = max_exact + (np.log(d / np.float32(max_exact)) / np.float32(math.log(REL_MAX_DIST / max_exact))
                         * np.float32(REL_BUCKETS - max_exact)).astype(np.int32)
    large = np.minimum(large, REL_BUCKETS - 1)
    return np.where(dist < max_exact, dist, large).astype(np.int32)


def _bias_kernel(rb_ref, bucket_ref, o_ref):
    bucket = bucket_ref[...]
    from_prev = (lax.broadcasted_iota(jnp.int32, bucket.shape, 1)
                 > lax.broadcasted_iota(jnp.int32, bucket.shape, 0))
    for h in range(ATT_HEADS):
        acc = jnp.zeros(bucket.shape, F32)
        for b in range(REL_BUCKETS):
            acc = jnp.where(bucket == b, rb_ref[b, h], acc)
        o_ref[1, h] = acc
        o_ref[0, h] = jnp.where(from_prev, NEG_INF, acc)


def _bias_call(rel_bias, bucket):
    return pl.pallas_call(
        _bias_kernel,
        in_specs=[pl.BlockSpec(memory_space=pltpu.SMEM), pl.BlockSpec(memory_space=pltpu.VMEM)],
        out_shape=jax.ShapeDtypeStruct((2, ATT_HEADS) + bucket.shape, F32),
    )(rel_bias, bucket)


ATT_SEQS = 4


def _attn_kernel(sink_ref, q_ref, kp_ref, kc_ref, vp_ref, vc_ref, bias_ref, ng_ref, o_ref, obuf_ref):
    for s in range(q_ref.shape[0]):
        _attn_block(sink_ref, q_ref.at[s], kp_ref.at[s], kc_ref.at[s], vp_ref.at[s], vc_ref.at[s], bias_ref.at[0],
                    ng_ref, o_ref.at[s], obuf_ref.at[s])


def _attn_block(sink_ref, q_ref, kp_ref, kc_ref, vp_ref, vc_ref, bias_ref, ng_ref, o_ref, obuf_ref):
    qi = lax.broadcasted_iota(jnp.int32, (ATT_BLOCK, ATT_BLOCK), 0)
    ci = lax.broadcasted_iota(jnp.int32, (ATT_BLOCK, ATT_BLOCK), 1)
    from_prev = ci > qi
    low = _lane_half_mask((ATT_BLOCK, LANES))

    def band_variants(prev_ref, cur_ref):
        out = []
        for cpair in range(KV_HEADS // 2):
            lanes = slice(cpair * LANES, (cpair + 1) * LANES)
            t = jnp.concatenate([prev_ref[:, lanes], cur_ref[:, lanes]], axis=0).astype(F32)
            out.append((t.astype(BF16), pltpu.roll(t, HALF, 1).astype(BF16)))
        return out

    k_band = band_variants(kp_ref, kc_ref)
    v_band = band_variants(vp_ref, vc_ref)
    nt = (((1,), (1,)), ((), ()))

    for j in range(ATT_HEADS // 2):
        qp = q_ref[:, j * LANES:(j + 1) * LANES]
        out_pair = jnp.zeros((ATT_BLOCK, LANES), F32)
        for half in range(2):
            h = 2 * j + half
            g = h // Q_PER_KV
            swapped = int((g % 2) != half)
            keep = low if half == 0 else jnp.logical_not(low)
            qh = jnp.where(keep, qp, jnp.zeros_like(qp))
            s_band = lax.dot_general(qh, k_band[g // 2][swapped], nt, preferred_element_type=F32)
            s = jnp.where(from_prev, s_band[:, :ATT_BLOCK], s_band[:, ATT_BLOCK:]) + bias_ref[h]
            sink = sink_ref[h]
            m = jnp.maximum(jnp.max(s, axis=-1, keepdims=True), sink)
            p = jnp.exp(s - m)
            denom = jnp.sum(p, axis=-1, keepdims=True) + jnp.exp(sink - m)
            p_band = jnp.concatenate([jnp.where(from_prev, p, 0.0), jnp.where(from_prev, 0.0, p)], axis=1)
            o = jnp.dot(p_band.astype(BF16), v_band[g // 2][swapped], preferred_element_type=F32) / denom
            out_pair = out_pair + jnp.where(keep, o, 0.0)
        obuf_ref[:, j * LANES:(j + 1) * LANES] = out_pair

    att = obuf_ref[...]
    ms = jnp.mean(att * att, axis=-1, keepdims=True)
    o_ref[...] = (att * lax.rsqrt(ms + EPS) * ng_ref[...]).astype(BF16)


def _attn_call(sinks, q, k, v, bias, ng):
    bsz, l, _ = q.shape
    nb = l // ATT_BLOCK
    nseq = ATT_SEQS if bsz % ATT_SEQS == 0 else 1
    cur = lambda w: pl.BlockSpec((nseq, ATT_BLOCK, w), lambda b, i: (b, i, 0))
    prev = lambda w: pl.BlockSpec((nseq, ATT_BLOCK, w), lambda b, i: (b, jnp.maximum(i - 1, 0), 0))
    return pl.pallas_call(
        _attn_kernel,
        grid=(bsz // nseq, nb),
        in_specs=[pl.BlockSpec(memory_space=pltpu.SMEM),
                  cur(D_ATT), prev(D_KV), cur(D_KV), prev(D_KV), cur(D_KV),
                  pl.BlockSpec((1,) + bias.shape[1:], lambda b, i: (jnp.minimum(i, 1), 0, 0, 0)),
                  pl.BlockSpec(ng.shape, lambda b, i: (0, 0))],
        out_specs=cur(D_ATT),
        out_shape=jax.ShapeDtypeStruct((bsz, l, D_ATT), BF16),
        scratch_shapes=[pltpu.VMEM((nseq, ATT_BLOCK, D_ATT), F32)],
        compiler_params=pltpu.CompilerParams(dimension_semantics=("arbitrary", "arbitrary"),
                                             vmem_limit_bytes=VMEM_LIMIT),
    )(sinks, q, k, k, v, v, bias, ng)


def _out_proj_kernel(x_ref, ys_ref, ya_ref, g1_ref, sc_ref, sh_ref, g2_ref, ng_ref, wo_hbm, sg_hbm, su_hbm, sd_hbm,
                     rwh_ref, rwl_ref, rb_ref, upper_ref,
                     base_ref, h_ref, idx_ref, gate_ref, rank_ref, cnt_ref,
                     carry_ref, wo_raw, sg_raw, su_raw, sd_raw, wo_ref, sg_ref, su_ref, sd_ref, wsem):
    @pl.when(pl.program_id(0) == 0)
    def _():
        carry_ref[...] = jnp.zeros_like(carry_ref)
        staged = ((wo_hbm, wo_raw, wo_ref), (sg_hbm, sg_raw, sg_ref), (su_hbm, su_raw, su_ref),
                  (sd_hbm, sd_raw, sd_ref))
        copies = [pltpu.make_async_copy(src.at[0], raw, wsem.at[j]) for j, (src, raw, _) in enumerate(staged)]
        for cp in copies:
            cp.start()
        for cp, (_, raw, dst) in zip(copies, staged):
            cp.wait()
            dst[...] = raw[...].astype(BF16)

    mix = (jnp.dot(ys_ref[...], wo_ref[:D_SSM, :], preferred_element_type=F32)
           + jnp.dot(ya_ref[...], wo_ref[D_SSM:, :], preferred_element_type=F32))
    x1 = x_ref[...] + g1_ref[0] * mix
    ms = jnp.mean(x1 * x1, axis=-1, keepdims=True)
    h = x1 * lax.rsqrt(ms + EPS) * ng_ref[...]
    h = h * (1.0 + sc_ref[0]) + sh_ref[0]
    half = h.shape[1] // 2
    h_ref[...] = _pack_bf16_pair(h[:, :half], h[:, half:])
    hi, lo = _split_hi_lo(h)
    hi_terms = jnp.dot(hi, rwl_ref[...], preferred_element_type=F32)
    logits = (hi_terms[:, :LANES] + hi_terms[:, LANES:]
              + jnp.dot(lo, rwh_ref[...], preferred_element_type=F32))
    _route_tokens(logits.T[0:N_EXPERTS, :], rb_ref, upper_ref, idx_ref, gate_ref, rank_ref, cnt_ref, carry_ref)
    u = _silu(jnp.dot(hi, sg_ref[...], preferred_element_type=F32)) * jnp.dot(hi, su_ref[...],
                                                                              preferred_element_type=F32)
    shared = jnp.dot(u.astype(BF16), sd_ref[...], preferred_element_type=F32)
    base_ref[...] = x1 + g2_ref[0] * shared


def _out_proj_call(x2, ys, ya, g1, sc2, sh2, g2, ng, w_out, sg, su, sd, rwh, rwl, router_bias, upper,
                   tiles_per_batch, tm):
    n, d = x2.shape
    row = lambda w: pl.BlockSpec((tm, w), lambda i: (i, 0))
    tok = lambda r: pl.BlockSpec((r, tm), lambda i: (0, i))
    full = lambda a: pl.BlockSpec(a.shape, lambda i: (0, 0))
    per_batch = pl.BlockSpec((1, 1, d), lambda i: (i // tiles_per_batch, 0, 0))
    hbm = pl.BlockSpec(memory_space=pl.ANY)
    staged = (w_out, sg, su, sd)
    return pl.pallas_call(
        _out_proj_kernel,
        grid=(n // tm,),
        in_specs=[row(d), row(D_SSM), row(D_ATT), per_batch, per_batch, per_batch, per_batch, full(ng),
                  hbm, hbm, hbm, hbm, full(rwh), full(rwl), full(router_bias), full(upper)],
        out_specs=[row(d), row(d // 2), tok(TOP_K), tok(TOP_K), tok(TOP_K),
                   pl.BlockSpec((N_EXPERTS, LANES), lambda i: (0, 0))],
        out_shape=[jax.ShapeDtypeStruct((n, d), F32), jax.ShapeDtypeStruct((n, d // 2), jnp.uint32),
                   jax.ShapeDtypeStruct((TOP_K, n), jnp.int32), jax.ShapeDtypeStruct((TOP_K, n), F32),
                   jax.ShapeDtypeStruct((TOP_K, n), jnp.int32), jax.ShapeDtypeStruct((N_EXPERTS, LANES), F32)],
        scratch_shapes=([pltpu.VMEM((N_EXPERTS, LANES), F32)]
                        + [pltpu.VMEM(a.shape[1:], a.dtype) for a in staged]
                        + [pltpu.VMEM(a.shape[1:], BF16) for a in staged]
                        + [pltpu.SemaphoreType.DMA((len(staged),))]),
        compiler_params=pltpu.CompilerParams(dimension_semantics=("arbitrary",),
                                             vmem_limit_bytes=VMEM_LIMIT),
    )(x2, ys, ya, g1, sc2, sh2, g2, ng, w_out, sg, su, sd, rwh, rwl, router_bias, upper)


def _route_tokens(logits_t, rb_ref, upper_ref, idx_ref, gate_ref, rank_ref, cnt_ref, carry_ref):
    t = logits_t.shape[1]
    per_group = N_EXPERTS // ROUTE_GROUPS
    scores = 1.0 / (1.0 + jnp.exp(-logits_t))
    sel = scores + rb_ref[...]
    e_iota = lax.broadcasted_iota(jnp.int32, (N_EXPERTS, t), 0)

    sel3 = sel.reshape(ROUTE_GROUPS, per_group, t)
    w_iota = lax.broadcasted_iota(jnp.int32, sel3.shape, 1)
    m1 = jnp.max(sel3, axis=1, keepdims=True)
    first = jnp.min(jnp.where(sel3 == m1, w_iota, per_group), axis=1, keepdims=True)
    m2 = jnp.max(jnp.where(w_iota == first, NEG_INF, sel3), axis=1, keepdims=True)
    grp = (m1 + m2).reshape(ROUTE_GROUPS, t)

    g_iota = lax.broadcasted_iota(jnp.int32, (ROUTE_GROUPS, t), 0)
    gmask = jnp.zeros((ROUTE_GROUPS, t), jnp.bool_)
    for _ in range(ROUTE_TOPK_GROUPS):
        gm = jnp.max(grp, axis=0, keepdims=True)
        gfirst = jnp.min(jnp.where(grp == gm, g_iota, ROUTE_GROUPS), axis=0, keepdims=True)
        hit = g_iota == gfirst
        gmask = jnp.logical_or(gmask, hit)
        grp = jnp.where(hit, NEG_INF, grp)
    allowed = jnp.broadcast_to(gmask.reshape(ROUTE_GROUPS, 1, t),
                               (ROUTE_GROUPS, per_group, t)).reshape(N_EXPERTS, t)
    masked = jnp.where(allowed, sel, NEG_INF)

    picked = jnp.zeros((N_EXPERTS, t), jnp.bool_)
    idx_rows = []
    w_rows = []
    for _ in range(TOP_K):
        mm = jnp.max(masked, axis=0, keepdims=True)
        efirst = jnp.min(jnp.where(masked == mm, e_iota, N_EXPERTS), axis=0, keepdims=True)
        hit = e_iota == efirst
        idx_rows.append(efirst)
        w_rows.append(jnp.sum(jnp.where(hit, scores, 0.0), axis=0, keepdims=True))
        picked = jnp.logical_or(picked, hit)
        masked = jnp.where(hit, NEG_INF, masked)
    idx = jnp.concatenate(idx_rows, axis=0)
    w = jnp.concatenate(w_rows, axis=0)
    gate_ref[...] = w / jnp.sum(w, axis=0, keepdims=True) * ROUTED_SCALE
    idx_ref[...] = idx

    onehot = jnp.where(picked, 1.0, 0.0)
    sub = upper_ref.shape[0]
    carry = carry_ref[:, 0:1]
    parts = []
    for s0 in range(0, t, sub):
        oh = onehot[:, s0:s0 + sub]
        parts.append(jnp.dot(oh.astype(BF16), upper_ref[...], preferred_element_type=F32) + carry)
        carry = carry + jnp.sum(oh, axis=1, keepdims=True)
    rank_full = jnp.concatenate(parts, axis=1)
    rank_rows = [jnp.sum(jnp.where(e_iota == idx_rows[k], rank_full, 0.0), axis=0, keepdims=True)
                 for k in range(TOP_K)]
    rank_ref[...] = jnp.concatenate(rank_rows, axis=0).astype(jnp.int32)
    carry_ref[...] = jnp.broadcast_to(carry, carry_ref.shape)
    cnt_ref[...] = carry_ref[...]


PLAN_EXPERT, PLAN_FRESH, PLAN_VALID, PLAN_SEG, PLAN_NEXT, PLAN_NUSED = range(6)
PLAN_ROWS = SUBLANES


def _plan_kernel(cnt_ref, tri_ref, start_ref, plan_ref, *, nblocks):
    nbp = plan_ref.shape[1]
    cnt = cnt_ref[...].astype(jnp.int32)
    blocks = (cnt + (EXPERT_ROWS - 1)) // EXPERT_ROWS
    end = jnp.dot(tri_ref[...], blocks.astype(F32), precision=lax.Precision.HIGHEST,
                  preferred_element_type=F32).astype(jnp.int32)
    start = end - blocks
    start_ref[...] = start[:, 0:1] * EXPERT_ROWS
    nused = end[N_EXPERTS - 1:N_EXPERTS, 0:1]

    e_iota = lax.broadcasted_iota(jnp.int32, (N_EXPERTS, nbp), 0)
    blk = lax.broadcasted_iota(jnp.int32, (1, nbp), 1)
    expert = jnp.minimum(jnp.sum((end[:, 0:1] <= blk).astype(jnp.int32), axis=0, keepdims=True), N_EXPERTS - 1)
    mine = e_iota == expert
    pick = lambda col: jnp.sum(jnp.where(mine, col, 0), axis=0, keepdims=True)
    first = pick(start[:, 0:1])
    valid = jnp.clip(pick(cnt[:, 0:1]) - (blk - first) * EXPERT_ROWS, 0, EXPERT_ROWS)
    present = jnp.logical_or(blocks[:, 0:1] > 0,
                             jnp.logical_and(e_iota[:, 0:1] == N_EXPERTS - 1, nused < nblocks))
    seg = jnp.sum(jnp.logical_and(present, e_iota <= expert).astype(jnp.int32), axis=0, keepdims=True) - 1
    nxt = jnp.min(jnp.where(jnp.logical_and(present, e_iota > expert), e_iota, N_EXPERTS), axis=0, keepdims=True)
    rows = {PLAN_EXPERT: expert, PLAN_FRESH: (blk == first).astype(jnp.int32), PLAN_VALID: valid, PLAN_SEG: seg,
            PLAN_NEXT: jnp.where(nxt == N_EXPERTS, -1, nxt), PLAN_NUSED: jnp.broadcast_to(nused, (1, nbp))}
    zero = jnp.zeros((1, nbp), jnp.int32)
    plan_ref[...] = jnp.concatenate([rows.get(r, zero) for r in range(PLAN_ROWS)], axis=0)


def _plan_call(counts, nblocks):
    nbp = -(-nblocks // LANES) * LANES
    tri = jnp.asarray(np.tril(np.ones((N_EXPERTS, N_EXPERTS), np.float32)))
    return pl.pallas_call(
        functools.partial(_plan_kernel, nblocks=nblocks),
        out_shape=[jax.ShapeDtypeStruct((N_EXPERTS, 1), jnp.int32),
                   jax.ShapeDtypeStruct((PLAN_ROWS, nbp), jnp.int32)],
    )(counts, tri)


def _dest_kernel(idx_ref, rank_ref, start_ref, dest_ref):
    t = idx_ref.shape[1]
    e_iota = lax.broadcasted_iota(jnp.int32, (N_EXPERTS, t), 0)
    rows = [jnp.sum(jnp.where(e_iota == idx_ref[k:k + 1, :], start_ref[...], 0), axis=0, keepdims=True)
            for k in range(TOP_K)]
    dest = jnp.concatenate(rows, axis=0) + rank_ref[...]
    for k in range(TOP_K):
        for c in range(t // LANES):
            dest_ref[k, c:c + 1, :] = dest[k:k + 1, c * LANES:(c + 1) * LANES]


def _dest_call(idx, rank, pad_start, tile):
    n = idx.shape[1]
    tok = pl.BlockSpec((TOP_K, tile), lambda i: (0, i))
    return pl.pallas_call(
        _dest_kernel,
        grid=(n // tile,),
        in_specs=[tok, tok, pl.BlockSpec((N_EXPERTS, 1), lambda i: (0, 0))],
        out_specs=pl.BlockSpec((TOP_K, tile // LANES, LANES), lambda i: (0, i, 0)),
        out_shape=jax.ShapeDtypeStruct((TOP_K, n // LANES, LANES), jnp.int32),
        compiler_params=pltpu.CompilerParams(dimension_semantics=("arbitrary",)),
    )(idx, rank, pad_start)


def _scatter_rows_sc(rows, dest_flat, total_rows, chunk):
    n, w = rows.shape
    copies = dest_flat.shape[0] // n
    info = plsc.get_sparse_core_info()
    nc = info.num_cores
    per_worker = n // (nc * info.num_subcores)
    assert per_worker * nc * info.num_subcores == n and per_worker % chunk == 0
    mesh = plsc.VectorSubcoreMesh(core_axis_name="c", subcore_axis_name="s")

    @functools.partial(
        pl.kernel, mesh=mesh,
        out_type=jax.ShapeDtypeStruct((total_rows, w), rows.dtype),
        scratch_types=([pltpu.VMEM((chunk,), jnp.int32) for _ in range(copies)]
                       + [pltpu.VMEM((chunk, w), rows.dtype), pltpu.SemaphoreType.DMA, pltpu.SemaphoreType.DMA]),
    )
    def scatter(rows_hbm, idx_hbm, out_hbm, *scratch):
        idx_vs, (rows_v, isem, sem) = scratch[:copies], scratch[copies:]
        worker = lax.axis_index("s") * nc + lax.axis_index("c")

        @pl.loop(0, per_worker // chunk)
        def _(j):
            base = worker * per_worker + j * chunk
            loads = [pltpu.async_copy(idx_hbm.at[pl.ds(k * n + base, chunk)], idx_vs[k], isem)
                     for k in range(copies)]
            pltpu.sync_copy(rows_hbm.at[pl.ds(base, chunk)], rows_v)
            for ld in loads:
                ld.wait()
            stores = [pltpu.async_copy(rows_v, out_hbm.at[idx_vs[k]], sem) for k in range(copies)]
            for st in stores:
                st.wait()

    return scatter(rows, dest_flat)


X_RING = 4


def _expert_kernel(plan_ref, xs_hbm, wg_hbm, wu_hbm, wd_hbm, y_ref,
                   wgb_ref, wub_ref, wdb_ref, xbuf_ref, xsem, wg_raw, wu_raw, wd_raw, wsem):
    i = pl.program_id(0)
    rows, half = xbuf_ref.shape[1], xbuf_ref.shape[2]
    nused = plan_ref[PLAN_NUSED, 0]

    def w_copies(expert, slot):
        return [pltpu.make_async_copy(src.at[expert], dst.at[slot], wsem.at[slot])
                for src, dst in ((wg_hbm, wg_raw), (wu_hbm, wu_raw), (wd_hbm, wd_raw))]

    def x_copy(block):
        first = pl.multiple_of(block * rows, rows)
        slot = block % X_RING
        return pltpu.make_async_copy(xs_hbm.at[pl.ds(first, rows)], xbuf_ref.at[slot], xsem.at[slot])

    @pl.when(i == 0)
    def _():
        for b in range(X_RING - 1):
            @pl.when(b < nused)
            def _():
                x_copy(b).start()

    @pl.when(i + (X_RING - 1) < nused)
    def _():
        x_copy(i + (X_RING - 1)).start()

    @pl.when(i == 0)
    def _():
        for c in w_copies(plan_ref[PLAN_EXPERT, 0], 0):
            c.start()

    @pl.when(plan_ref[PLAN_FRESH, i] > 0)
    def _():
        slot = plan_ref[PLAN_SEG, i] % 2
        for c in w_copies(plan_ref[PLAN_EXPERT, i], slot):
            c.wait()

        @pl.when(plan_ref[PLAN_NEXT, i] >= 0)
        def _():
            for c in w_copies(plan_ref[PLAN_NEXT, i], 1 - slot):
                c.start()

        wgb_ref[...] = wg_raw[slot].astype(BF16)
        wub_ref[...] = wu_raw[slot].astype(BF16)
        wdb_ref[...] = wd_raw[slot].astype(BF16)

    @pl.when(i < nused)
    def _():
        x_copy(i).wait()
        xw = xbuf_ref[i % X_RING]
        row = lax.broadcasted_iota(jnp.int32, xw.shape, 0)
        x_lo, x_hi = _unpack_bf16_pair(jnp.where(row < plan_ref[PLAN_VALID, i], xw, jnp.uint32(0)))
        x_lo = x_lo.astype(BF16)
        x_hi = x_hi.astype(BF16)
        gate = (jnp.dot(x_lo, wgb_ref[:half, :], preferred_element_type=F32)
                + jnp.dot(x_hi, wgb_ref[half:, :], preferred_element_type=F32))
        up = (jnp.dot(x_lo, wub_ref[:half, :], preferred_element_type=F32)
              + jnp.dot(x_hi, wub_ref[half:, :], preferred_element_type=F32))
        u = (_silu(gate) * up).astype(BF16)
        y_lo = jnp.dot(u, wdb_ref[:, :half], preferred_element_type=F32)
        y_hi = jnp.dot(u, wdb_ref[:, half:], preferred_element_type=F32)
        y_ref[...] = _pack_bf16_pair(y_lo, y_hi)

    @pl.when(i >= nused)
    def _():
        y_ref[...] = jnp.zeros_like(y_ref)


def _expert_call(plan, xs, wg, wu, wd, rows):
    p, w = xs.shape
    d, f = wg.shape[1], wg.shape[2]
    hbm = pl.BlockSpec(memory_space=pl.ANY)
    grid_spec = pltpu.PrefetchScalarGridSpec(
        num_scalar_prefetch=1,
        grid=(p // rows,),
        in_specs=[hbm, hbm, hbm, hbm],
        out_specs=pl.BlockSpec((rows, w), lambda i, *_: (i, 0)),
        scratch_shapes=[pltpu.VMEM((d, f), BF16), pltpu.VMEM((d, f), BF16), pltpu.VMEM((f, d), BF16),
                        pltpu.VMEM((X_RING, rows, w), xs.dtype), pltpu.SemaphoreType.DMA((X_RING,)),
                        pltpu.VMEM((2, d, f), wg.dtype), pltpu.VMEM((2, d, f), wu.dtype),
                        pltpu.VMEM((2, f, d), wd.dtype), pltpu.SemaphoreType.DMA((2,))],
    )
    return pl.pallas_call(
        _expert_kernel,
        grid_spec=grid_spec,
        out_shape=jax.ShapeDtypeStruct((p, w), jnp.uint32),
        compiler_params=pltpu.CompilerParams(dimension_semantics=("arbitrary",),
                                             vmem_limit_bytes=VMEM_LIMIT),
    )(plan, xs, wg, wu, wd)


def _gather_rows_sc(table, idx, chunk):
    m = idx.shape[0]
    w = table.shape[1]
    info = plsc.get_sparse_core_info()
    nc = info.num_cores
    per_worker = m // (nc * info.num_subcores)
    assert per_worker * nc * info.num_subcores == m and per_worker % chunk == 0
    part = chunk // SC_GATHER_PARTS
    mesh = plsc.VectorSubcoreMesh(core_axis_name="c", subcore_axis_name="s")

    @functools.partial(
        pl.kernel, mesh=mesh,
        out_type=jax.ShapeDtypeStruct((m, w), table.dtype),
        scratch_types=([pltpu.VMEM((part,), jnp.int32) for _ in range(SC_GATHER_PARTS)]
                       + [pltpu.VMEM((part, w), table.dtype) for _ in range(SC_GATHER_PARTS)]
                       + [pltpu.SemaphoreType.DMA] * 3),
    )
    def gather(table_hbm, idx_hbm, out_hbm, *scratch):
        idx_vs = scratch[:SC_GATHER_PARTS]
        rows_vs = scratch[SC_GATHER_PARTS:2 * SC_GATHER_PARTS]
        isem, gsem, wsem = scratch[2 * SC_GATHER_PARTS:]
        worker = lax.axis_index("s") * nc + lax.axis_index("c")

        @pl.loop(0, per_worker // chunk)
        def _(j):
            base = worker * per_worker + j * chunk
            loads = [pltpu.async_copy(idx_hbm.at[pl.ds(base + p * part, part)], idx_vs[p], isem)
                     for p in range(SC_GATHER_PARTS)]
            gathers = []
            for p in range(SC_GATHER_PARTS):
                loads[p].wait()
                gathers.append(pltpu.async_copy(table_hbm.at[idx_vs[p]], rows_vs[p], gsem))
            writes = []
            for p in range(SC_GATHER_PARTS):
                gathers[p].wait()
                writes.append(pltpu.async_copy(rows_vs[p], out_hbm.at[pl.ds(base + p * part, part)], wsem))
            for wr in writes:
                wr.wait()

    return gather(table, idx)


def _combine_kernel(yk_ref, gate_ref, base_ref, g2_ref, fg_ref, o_ref):
    t = base_ref.shape[0]
    half = yk_ref.shape[2]
    gates = jnp.concatenate([gate_ref[...], jnp.zeros((LANES - TOP_K, t), F32)], axis=0).T
    r_lo = jnp.zeros((t, half), F32)
    r_hi = jnp.zeros((t, half), F32)
    for k in range(TOP_K):
        y_lo, y_hi = _unpack_bf16_pair(yk_ref[k])
        r_lo = r_lo + gates[:, k:k + 1] * y_lo
        r_hi = r_hi + gates[:, k:k + 1] * y_hi
    g2 = g2_ref[0]
    x_lo = base_ref[:, :half] + g2[:, :half] * r_lo
    x_hi = base_ref[:, half:] + g2[:, half:] * r_hi
    ms = (jnp.sum(x_lo * x_lo, axis=-1, keepdims=True)
          + jnp.sum(x_hi * x_hi, axis=-1, keepdims=True)) * (1.0 / (2 * half))
    inv = lax.rsqrt(ms + EPS)
    o_ref[:, :half] = x_lo * inv * fg_ref[:, :half]
    o_ref[:, half:] = x_hi * inv * fg_ref[:, half:]


def _combine_call(yk, gates, base, g2, fg, tiles_per_batch, tile, tile0):
    n, d = base.shape
    row = lambda w: pl.BlockSpec((tile, w), lambda i: (i + tile0, 0))
    return pl.pallas_call(
        _combine_kernel,
        grid=(yk.shape[1] // tile,),
        in_specs=[pl.BlockSpec((TOP_K, tile, yk.shape[2]), lambda i: (0, i, 0)),
                  pl.BlockSpec((TOP_K, tile), lambda i: (0, i + tile0)), row(d),
                  pl.BlockSpec((1, 1, d), lambda i: ((i + tile0) // tiles_per_batch, 0, 0)),
                  pl.BlockSpec((1, d), lambda i: (0, 0))],
        out_specs=row(d),
        out_shape=jax.ShapeDtypeStruct((n, d), F32),
        input_output_aliases={2: 0},
        compiler_params=pltpu.CompilerParams(dimension_semantics=("arbitrary",),
                                             vmem_limit_bytes=VMEM_LIMIT),
    )(yk, gates, base, g2, fg)


def _pad_cols(a, width):
    return jnp.pad(a, ((0, 0), (0, width - a.shape[1])))


def _layer(x, mod, norm1_g, norm2_g, w_in, conv_w, conv_b, dt_bias, a_log, d_skip, ssm_norm_g,
           att_norm_g, sinks, rel_bias, w_out, router_w, router_bias, exp_w_gate, exp_w_up, exp_w_down,
           sh_w_gate, sh_w_up, sh_w_down, final_g):
    bsz, l, d = x.shape
    n = bsz * l
    tm = min(ROW_TILE, l)

    sh1, sc1, g1, sh2, sc2, g2 = [m[:, None, :] for m in jnp.split(mod, 6, axis=-1)]

    assert math.frexp(ATT_HEAD_DIM ** -0.5)[0] == 0.5
    x2 = x.reshape(n, d)
    z, xbc, dt, q, k, v = _in_proj_call(x2, sc1, sh1, norm1_g[None, :], jnp.swapaxes(w_in, 1, 2), l // tm, tm)

    triu = jnp.asarray(np.triu(np.ones((CHUNK, CHUNK), np.float32))).astype(BF16)
    shift = jnp.asarray(_conv_shift_matrix()).astype(BF16)
    y_ssm = _ssd_call(xbc.reshape(bsz, l, CONV_CH), z.reshape(bsz, l, D_SSM), dt.reshape(bsz, l, LANES),
                      conv_w, conv_b[None, :], dt_bias[:, None], a_log[:, None],
                      jnp.repeat(d_skip, SSM_HEAD_DIM)[None, :], ssm_norm_g[None, :], triu, shift)

    bias = _bias_call(rel_bias, jnp.asarray(_rel_bucket_table()))
    y_att = _attn_call(sinks, q.reshape(bsz, l, D_ATT), k.reshape(bsz, l, D_KV), v.reshape(bsz, l, D_KV), bias,
                       att_norm_g[None, :])

    rw = _pad_cols(router_w, LANES)
    rwh = rw.astype(BF16)
    rwl = jnp.concatenate([rwh, (rw - rwh.astype(F32)).astype(BF16)], axis=1)
    rs = min(RANK_SUB, tm)
    upper = jnp.asarray(np.triu(np.ones((rs, rs), np.float32), 1)).astype(BF16)
    base, h2, idx, gates, rank, counts = _out_proj_call(
        x2, y_ssm.reshape(n, D_SSM), y_att.reshape(n, D_ATT), g1, sc2, sh2, g2, norm2_g[None, :],
        w_out, sh_w_gate, sh_w_up, sh_w_down, rwh, rwl, router_bias[:, None], upper, l // tm, tm)
    rt = min(ROUTE_TILE, n)

    nblocks = (n * TOP_K + N_EXPERTS * (EXPERT_ROWS - 1) + EXPERT_ROWS - 1) // EXPERT_ROWS
    pad_start, plan = _plan_call(counts, nblocks)
    dest = _dest_call(idx, rank, pad_start, rt)

    xs = _scatter_rows_sc(h2, dest.reshape(-1), nblocks * EXPERT_ROWS, SC_CHUNK)
    ys = _expert_call(plan, xs, exp_w_gate, exp_w_up, exp_w_down, EXPERT_ROWS)
    ctile = min(COMBINE_TILE, l)
    groups = COMBINE_GROUPS if bsz % COMBINE_GROUPS == 0 else 1
    ng = n // groups
    out = base
    for g in range(groups):
        idx_g = dest[:, g * ng // LANES:(g + 1) * ng // LANES, :].reshape(-1)
        yk = _gather_rows_sc(ys, idx_g, SC_CHUNK).reshape(TOP_K, ng, ys.shape[1])
        out = _combine_call(yk, gates, out, g2, final_g[None, :], l // ctile, ctile, g * ng // ctile)
    return out.reshape(bsz, l, d)


def kernel(x, c, mod_w, mod_b, norm1_g, norm2_g, w_in, conv_w, conv_b, dt_bias, a_log, d_skip, ssm_norm_g,
           att_norm_g, sinks, rel_bias, w_out, router_w, router_bias, exp_w_gate, exp_w_up, exp_w_down,
           sh_w_gate, sh_w_up, sh_w_down, final_g):
    assert mod_w.shape[0] == 1, "single-layer block"
    bsz = x.shape[0]
    c_pad = jnp.pad(c, ((0, SUBLANES - bsz % SUBLANES if bsz % SUBLANES else 0), (0, 0)))
    mod = _mod_call(c_pad, mod_w[0], mod_b[0][None, :])[:bsz]
    return _layer(x, mod, norm1_g[0], norm2_g[0], w_in, conv_w[0], conv_b[0], dt_bias[0], a_log[0], d_skip[0],
                  ssm_norm_g[0], att_norm_g[0], sinks[0], rel_bias, w_out, router_w[0], router_bias[0],
                  exp_w_gate[0], exp_w_up[0], exp_w_down[0], sh_w_gate, sh_w_up, sh_w_down, final_g)
```

```python
import functools
import math

import numpy as np
import jax
import jax.numpy as jnp
from jax import lax
from jax.experimental import pallas as pl
from jax.experimental.pallas import tpu as pltpu
from jax.experimental.pallas import tpu_sc as plsc

F32 = jnp.float32
BF16 = jnp.bfloat16

D_MODEL = 1024
SSM_HEAD_DIM = 64
D_SSM = D_MODEL
SSM_HEADS = D_SSM // SSM_HEAD_DIM
SSM_GROUPS = 4
D_STATE = 128
CONV_K = 4
CONV_CH = D_SSM + 2 * SSM_GROUPS * D_STATE
CHUNK = 128
ATT_HEAD_DIM = 64
D_ATT = D_MODEL
ATT_HEADS = D_ATT // ATT_HEAD_DIM
KV_HEADS = ATT_HEADS // 4
Q_PER_KV = ATT_HEADS // KV_HEADS
D_KV = KV_HEADS * ATT_HEAD_DIM
WINDOW = 128
ATT_BLOCK = 128
REL_BUCKETS = 32
REL_MAX_DIST = 128
N_EXPERTS = 64
TOP_K = 8
EXPERT_DIM = D_MODEL // 4
SHARED_DIM = D_MODEL // 4
ROUTE_GROUPS = 8
ROUTE_TOPK_GROUPS = 4
ROUTED_SCALE = 2.5
EPS = 1e-6

LANES = 128
SUBLANES = 8
HALF = LANES // 2

ROW_TILE = 512
ROUTE_TILE = 2048
RANK_SUB = 512
COMBINE_TILE = 512
SC_CHUNK = 128
SC_GATHER_PARTS = 4
EXPERT_ROWS = 1024
COMBINE_GROUPS = 2
VMEM_LIMIT = 48 * 1024 * 1024

NEG_INF = float("-inf")


def _silu(v):
    return v * (1.0 / (1.0 + jnp.exp(-v)))


def _softplus(v):
    return jnp.maximum(v, 0.0) + jnp.log(1.0 + jnp.exp(-jnp.abs(v)))


def _split_hi_lo(v):
    hi = v.astype(BF16)
    lo = (v - hi.astype(F32)).astype(BF16)
    return hi, lo


def _pack_bf16_pair(a, b):
    w = pltpu.pack_elementwise([a, b], packed_dtype=BF16)
    return w if w.dtype == jnp.uint32 else lax.bitcast_convert_type(w, jnp.uint32)


def _unpack_bf16_pair(w):
    a = pltpu.unpack_elementwise(w, index=0, packed_dtype=BF16, unpacked_dtype=F32)
    b = pltpu.unpack_elementwise(w, index=1, packed_dtype=BF16, unpacked_dtype=F32)
    return a, b


def _lane_half_mask(shape):
    return lax.broadcasted_iota(jnp.int32, shape, len(shape) - 1) < HALF


def _mod_kernel(c_ref, w_ref, b_ref, o_ref):
    a = _silu(c_ref[...])
    o_ref[...] = jnp.dot(a, w_ref[...], precision=lax.Precision.HIGHEST,
                         preferred_element_type=F32) + b_ref[...]


def _mod_call(c_pad, mod_w, mod_b):
    rows, d = c_pad.shape
    cols = mod_w.shape[1]
    return pl.pallas_call(
        _mod_kernel,
        grid=(cols // d,),
        in_specs=[pl.BlockSpec((rows, d), lambda j: (0, 0)),
                  pl.BlockSpec((d, d), lambda j: (0, j)),
                  pl.BlockSpec((1, d), lambda j: (0, j))],
        out_specs=pl.BlockSpec((rows, d), lambda j: (0, j)),
        out_shape=jax.ShapeDtypeStruct((rows, cols), F32),
        compiler_params=pltpu.CompilerParams(dimension_semantics=("arbitrary",),
                                             vmem_limit_bytes=VMEM_LIMIT),
    )(c_pad, mod_w, mod_b)


IN_PROJ_SEGMENTS = ((D_SSM, BF16), (CONV_CH, BF16), (LANES, F32), (D_ATT, BF16), (D_KV, BF16), (D_KV, BF16))


def _in_proj_kernel(x_ref, sc_ref, sh_ref, g_ref, w_hbm, *refs):
    out_refs = refs[:len(IN_PROJ_SEGMENTS)]
    wraw_ref, w_ref, wsem = refs[len(IN_PROJ_SEGMENTS):]

    @pl.when(pl.program_id(0) == 0)
    def _():
        cp = pltpu.make_async_copy(w_hbm.at[0], wraw_ref, wsem)
        cp.start()
        cp.wait()
        src_dt = D_SSM + CONV_CH
        dst_q = src_dt + LANES
        src_q = src_dt + SSM_HEADS
        q_scale = ATT_HEAD_DIM ** -0.5

        def put(dst, src, scale=None):
            t = wraw_ref[src:src + LANES, :].T
            w_ref[:, dst:dst + LANES] = (t if scale is None else t * scale).astype(BF16)

        for c0 in range(0, src_dt, LANES):
            put(c0, c0)
        dt_tile = wraw_ref[src_dt:src_dt + LANES, :].T
        lane = lax.broadcasted_iota(jnp.int32, dt_tile.shape, 1)
        w_ref[:, src_dt:dst_q] = jnp.where(lane < SSM_HEADS, dt_tile, 0.0).astype(BF16)
        for c0 in range(0, D_ATT, LANES):
            put(dst_q + c0, src_q + c0, q_scale)
        for c0 in range(D_ATT, D_ATT + 2 * D_KV, LANES):
            put(dst_q + c0, src_q + c0)

    xf = x_ref[...]
    ms = jnp.mean(xf * xf, axis=-1, keepdims=True)
    h = xf * lax.rsqrt(ms + EPS) * g_ref[...]
    h = h * (1.0 + sc_ref[0]) + sh_ref[0]
    hb = h.astype(BF16)
    col = 0
    for (width, dtype), o_ref in zip(IN_PROJ_SEGMENTS, out_refs):
        o_ref[...] = jnp.dot(hb, w_ref[:, col:col + width], preferred_element_type=F32).astype(dtype)
        col += width


def _in_proj_call(x2, sc1, sh1, g1n, w_in_t, tiles_per_batch, tm):
    n, d = x2.shape
    row = lambda w: pl.BlockSpec((tm, w), lambda i: (i, 0))
    full = lambda a: pl.BlockSpec(a.shape, lambda i: (0, 0))
    per_batch = pl.BlockSpec((1, 1, d), lambda i: (i // tiles_per_batch, 0, 0))
    cols = sum(w for w, _ in IN_PROJ_SEGMENTS)
    w_in = w_in_t
    assert w_in.shape == (1, cols - (LANES - SSM_HEADS), d)
    return pl.pallas_call(
        _in_proj_kernel,
        grid=(n // tm,),
        in_specs=[row(d), per_batch, per_batch, full(g1n), pl.BlockSpec(memory_space=pl.ANY)],
        out_specs=[row(w) for w, _ in IN_PROJ_SEGMENTS],
        out_shape=[jax.ShapeDtypeStruct((n, w), dt) for w, dt in IN_PROJ_SEGMENTS],
        scratch_shapes=[pltpu.VMEM(w_in.shape[1:], w_in.dtype), pltpu.VMEM((d, cols), BF16), pltpu.SemaphoreType.DMA(())],
        compiler_params=pltpu.CompilerParams(dimension_semantics=("arbitrary",),
                                             vmem_limit_bytes=VMEM_LIMIT),
    )(x2, sc1, sh1, g1n, w_in)


SSD_SEQS = 4
CONV_HALO = 16


def _conv_shift_matrix():
    s = np.zeros((CONV_K * CHUNK, CONV_HALO + CHUNK), np.float32)
    for k in range(CONV_K):
        t = np.arange(CHUNK)
        s[k * CHUNK + t, CONV_HALO + t - (CONV_K - 1) + k] = 1.0
    return s


def _silu_tanh(v):
    hv = 0.5 * v
    return hv + hv * jnp.tanh(hv)


def _ssd_kernel(xbc_ref, z_ref, dt_ref, cw_ref, cb_ref, dtb_ref, alog_ref, dskip_ref, ng_ref, triu_ref, shift_ref,
                y_ref, state_ref, ucat_ref, ybuf_ref):
    nseq = xbc_ref.shape[0]

    @pl.when(pl.program_id(1) == 0)
    def _():
        state_ref[...] = jnp.zeros_like(state_ref)
        ucat_ref[:, 0:CONV_HALO, :] = jnp.zeros((nseq, CONV_HALO, CONV_CH), BF16)

    for q in range(nseq):
        _ssd_chunk(xbc_ref.at[q], z_ref.at[q], dt_ref.at[q], cw_ref, cb_ref, dtb_ref, alog_ref, dskip_ref, ng_ref,
                   triu_ref, shift_ref, y_ref.at[q], state_ref.at[q], ucat_ref.at[q], ybuf_ref.at[q])


def _ssd_chunk(xbc_ref, z_ref, dt_ref, cw_ref, cb_ref, dtb_ref, alog_ref, dskip_ref, ng_ref, triu_ref, shift_ref,
               y_ref, state_ref, ucat_ref, ybuf_ref):
    ucat_ref[CONV_HALO:, :] = xbc_ref[...]
    shifted = jnp.dot(shift_ref[...], ucat_ref[...], preferred_element_type=F32)
    ucat_ref[0:CONV_HALO, :] = ucat_ref[CHUNK:CHUNK + CONV_HALO, :]
    acc = cb_ref[...] + cw_ref[0:1, :] * shifted[0:CHUNK]
    for kk in range(1, CONV_K):
        acc = acc + cw_ref[kk:kk + 1, :] * shifted[kk * CHUNK:(kk + 1) * CHUNK]
    act = _silu_tanh(acc)
    xs = act[:, :D_SSM]
    gn = SSM_GROUPS * D_STATE

    dt_t = _softplus(dt_ref[...].T[0:SSM_HEADS, :] + dtb_ref[...])
    a_t = dt_t * (-jnp.exp(alog_ref[...]))
    a_hi = a_t.astype(BF16)
    a_mid = (a_t - a_hi.astype(F32)).astype(BF16)
    a_lo = (a_t - a_hi.astype(F32) - a_mid.astype(F32)).astype(BF16)
    triu = triu_ref[...]
    cs_t = (jnp.dot(a_hi, triu, preferred_element_type=F32) + jnp.dot(a_mid, triu, preferred_element_type=F32)
            + jnp.dot(a_lo, triu, preferred_element_type=F32))
    cs_end = cs_t[:, CHUNK - 1:CHUNK]
    r_t = cs_t - jnp.log(dt_t)
    w_t = jnp.exp(cs_end - cs_t) * dt_t
    chunk_decay = jnp.exp(cs_end)
    cols = jnp.concatenate([cs_t, jnp.exp(cs_t), jnp.zeros((LANES - 2 * SSM_HEADS, CHUNK), F32)], axis=0).T

    li = lax.broadcasted_iota(jnp.int32, (CHUNK, CHUNK), 0)
    si = lax.broadcasted_iota(jnp.int32, (CHUNK, CHUNK), 1)
    causal = li >= si
    low = _lane_half_mask((CHUNK, LANES))
    low_row = _lane_half_mask((1, LANES))

    heads_per_group = SSM_HEADS // SSM_GROUPS
    for g in range(SSM_GROUPS):
        b_g = act[:, D_SSM + g * D_STATE:D_SSM + (g + 1) * D_STATE]
        c_g = act[:, D_SSM + gn + g * D_STATE:D_SSM + gn + (g + 1) * D_STATE]
        b_gb = b_g.astype(BF16)
        c_gb = c_g.astype(BF16)
        cb = lax.dot_general(c_gb, b_gb, (((1,), (1,)), ((), ())), preferred_element_type=F32)
        b_t = b_g.T
        for jp in range(heads_per_group // 2):
            j = g * (heads_per_group // 2) + jp
            lanes = slice(j * LANES, (j + 1) * LANES)
            xp = xs[:, lanes]
            xpb = xp.astype(BF16)
            ydiag = jnp.zeros((CHUNK, LANES), F32)
            snew = jnp.zeros((D_STATE, LANES), F32)
            for half in range(2):
                h = 2 * j + half
                diff = cols[:, h:h + 1] - r_t[h:h + 1, :]
                m = (cb * jnp.exp(jnp.where(causal, diff, NEG_INF))).astype(BF16)
                keep = low if half == 0 else jnp.logical_not(low)
                xh = jnp.where(keep, xpb, jnp.zeros_like(xpb))
                ydiag = ydiag + jnp.dot(m, xh, preferred_element_type=F32)
                snew = snew + jnp.dot((b_t * w_t[h:h + 1, :]).astype(BF16), xh, preferred_element_type=F32)
            s_in = state_ref[:, lanes]
            yoff = jnp.dot(c_gb, s_in.astype(BF16), preferred_element_type=F32)
            h0 = 2 * j
            e0 = SSM_HEADS + h0
            escale = jnp.where(low, cols[:, e0:e0 + 1], cols[:, e0 + 1:e0 + 2])
            cdec = jnp.where(low_row, chunk_decay[h0:h0 + 1, :], chunk_decay[h0 + 1:h0 + 2, :])
            ybuf_ref[:, lanes] = ydiag + yoff * escale + xp * dskip_ref[:, lanes]
            state_ref[:, lanes] = s_in * cdec + snew

    yz = ybuf_ref[...] * _silu_tanh(z_ref[...].astype(F32))
    gw = D_SSM // SSM_GROUPS
    for g in range(SSM_GROUPS):
        part = yz[:, g * gw:(g + 1) * gw]
        ms = jnp.mean(part * part, axis=-1, keepdims=True)
        y_ref[:, g * gw:(g + 1) * gw] = (part * lax.rsqrt(ms + EPS)
                                            * ng_ref[:, g * gw:(g + 1) * gw]).astype(BF16)


def _ssd_call(xbc, z, dt, conv_w, conv_b, dtb, alog, dskip, ng, triu, shift):
    bsz, l, _ = xbc.shape
    nc = l // CHUNK
    nseq = SSD_SEQS if bsz % SSD_SEQS == 0 else 1
    chunk = lambda w: pl.BlockSpec((nseq, CHUNK, w), lambda b, c: (b, c, 0))
    full = lambda a: pl.BlockSpec(a.shape, lambda b, c: (0, 0))
    return pl.pallas_call(
        _ssd_kernel,
        grid=(bsz // nseq, nc),
        in_specs=[chunk(CONV_CH), chunk(D_SSM), chunk(LANES), full(conv_w), full(conv_b), full(dtb),
                  full(alog), full(dskip), full(ng), full(triu), full(shift)],
        out_specs=chunk(D_SSM),
        out_shape=jax.ShapeDtypeStruct((bsz, l, D_SSM), BF16),
        scratch_shapes=[pltpu.VMEM((nseq, D_STATE, D_SSM), F32),
                        pltpu.VMEM((nseq, CONV_HALO + CHUNK, CONV_CH), BF16),
                        pltpu.VMEM((nseq, CHUNK, D_SSM), F32)],
        compiler_params=pltpu.CompilerParams(dimension_semantics=("arbitrary", "arbitrary"),
                                             vmem_limit_bytes=VMEM_LIMIT),
    )(xbc, z, dt, conv_w, conv_b, dtb, alog, dskip, ng, triu, shift)


assert WINDOW == ATT_BLOCK


def _rel_bucket_table():
    qi = np.arange(ATT_BLOCK)[:, None]
    c = np.arange(ATT_BLOCK)[None, :]
    dist = np.where(c > qi, qi + ATT_BLOCK - c, qi - c)
    max_exact = REL_BUCKETS // 2
    d = np.maximum(dist, 1).astype(np.float32)
    large = max_exact + (np.log(d / np.float32(max_exact)) / np.float32(math.log(REL_MAX_DIST / max_exact))
                         * np.float32(REL_BUCKETS - max_exact)).astype(np.int32)
    large = np.minimum(large, REL_BUCKETS - 1)
    return np.where(dist < max_exact, dist, large).astype(np.int32)


def _bias_kernel(rb_ref, bucket_ref, o_ref):
    bucket = bucket_ref[...]
    from_prev = (lax.broadcasted_iota(jnp.int32, bucket.shape, 1)
                 > lax.broadcasted_iota(jnp.int32, bucket.shape, 0))
    for h in range(ATT_HEADS):
        acc = jnp.zeros(bucket.shape, F32)
        for b in range(REL_BUCKETS):
            acc = jnp.where(bucket == b, rb_ref[b, h], acc)
        o_ref[1, h] = acc
        o_ref[0, h] = jnp.where(from_prev, NEG_INF, acc)


def _bias_call(rel_bias, bucket):
    return pl.pallas_call(
        _bias_kernel,
        in_specs=[pl.BlockSpec(memory_space=pltpu.SMEM), pl.BlockSpec(memory_space=pltpu.VMEM)],
        out_shape=jax.ShapeDtypeStruct((2, ATT_HEADS) + bucket.shape, F32),
    )(rel_bias, bucket)


ATT_SEQS = 4


def _attn_kernel(sink_ref, q_ref, kp_ref, kc_ref, vp_ref, vc_ref, bias_ref, ng_ref, o_ref, obuf_ref):
    for s in range(q_ref.shape[0]):
        _attn_block(sink_ref, q_ref.at[s], kp_ref.at[s], kc_ref.at[s], vp_ref.at[s], vc_ref.at[s], bias_ref.at[0],
                    ng_ref, o_ref.at[s], obuf_ref.at[s])


def _attn_block(sink_ref, q_ref, kp_ref, kc_ref, vp_ref, vc_ref, bias_ref, ng_ref, o_ref, obuf_ref):
    qi = lax.broadcasted_iota(jnp.int32, (ATT_BLOCK, ATT_BLOCK), 0)
    ci = lax.broadcasted_iota(jnp.int32, (ATT_BLOCK, ATT_BLOCK), 1)
    from_prev = ci > qi
    low = _lane_half_mask((ATT_BLOCK, LANES))

    def band_variants(prev_ref, cur_ref):
        out = []
        for cpair in range(KV_HEADS // 2):
            lanes = slice(cpair * LANES, (cpair + 1) * LANES)
            t = jnp.concatenate([prev_ref[:, lanes], cur_ref[:, lanes]], axis=0).astype(F32)
            out.append((t.astype(BF16), pltpu.roll(t, HALF, 1).astype(BF16)))
        return out

    k_band = band_variants(kp_ref, kc_ref)
    v_band = band_variants(vp_ref, vc_ref)
    nt = (((1,), (1,)), ((), ()))

    for j in range(ATT_HEADS // 2):
        qp = q_ref[:, j * LANES:(j + 1) * LANES]
        out_pair = jnp.zeros((ATT_BLOCK, LANES), F32)
        for half in range(2):
            h = 2 * j + half
            g = h // Q_PER_KV
            swapped = int((g % 2) != half)
            keep = low if half == 0 else jnp.logical_not(low)
            qh = jnp.where(keep, qp, jnp.zeros_like(qp))
            s_band = lax.dot_general(qh, k_band[g // 2][swapped], nt, preferred_element_type=F32)
            s = jnp.where(from_prev, s_band[:, :ATT_BLOCK], s_band[:, ATT_BLOCK:]) + bias_ref[h]
            sink = sink_ref[h]
            m = jnp.maximum(jnp.max(s, axis=-1, keepdims=True), sink)
            p = jnp.exp(s - m)
            denom = jnp.sum(p, axis=-1, keepdims=True) + jnp.exp(sink - m)
            p_band = jnp.concatenate([jnp.where(from_prev, p, 0.0), jnp.where(from_prev, 0.0, p)], axis=1)
            o = jnp.dot(p_band.astype(BF16), v_band[g // 2][swapped], preferred_element_type=F32) / denom
            out_pair = out_pair + jnp.where(keep, o, 0.0)
        obuf_ref[:, j * LANES:(j + 1) * LANES] = out_pair

    att = obuf_ref[...]
    ms = jnp.mean(att * att, axis=-1, keepdims=True)
    o_ref[...] = (att * lax.rsqrt(ms + EPS) * ng_ref[...]).astype(BF16)


def _attn_call(sinks, q, k, v, bias, ng):
    bsz, l, _ = q.shape
    nb = l // ATT_BLOCK
    nseq = ATT_SEQS if bsz % ATT_SEQS == 0 else 1
    cur = lambda w: pl.BlockSpec((nseq, ATT_BLOCK, w), lambda b, i: (b, i, 0))
    prev = lambda w: pl.BlockSpec((nseq, ATT_BLOCK, w), lambda b, i: (b, jnp.maximum(i - 1, 0), 0))
    return pl.pallas_call(
        _attn_kernel,
        grid=(bsz // nseq, nb),
        in_specs=[pl.BlockSpec(memory_space=pltpu.SMEM),
                  cur(D_ATT), prev(D_KV), cur(D_KV), prev(D_KV), cur(D_KV),
                  pl.BlockSpec((1,) + bias.shape[1:], lambda b, i: (jnp.minimum(i, 1), 0, 0, 0)),
                  pl.BlockSpec(ng.shape, lambda b, i: (0, 0))],
        out_specs=cur(D_ATT),
        out_shape=jax.ShapeDtypeStruct((bsz, l, D_ATT), BF16),
        scratch_shapes=[pltpu.VMEM((nseq, ATT_BLOCK, D_ATT), F32)],
        compiler_params=pltpu.CompilerParams(dimension_semantics=("arbitrary", "arbitrary"),
                                             vmem_limit_bytes=VMEM_LIMIT),
    )(sinks, q, k, k, v, v, bias, ng)


def _out_proj_kernel(x_ref, ys_ref, ya_ref, g1_ref, sc_ref, sh_ref, g2_ref, ng_ref, wo_hbm, sg_hbm, su_hbm, sd_hbm,
                     rwh_ref, rwl_ref, rb_ref, upper_ref,
                     base_ref, h_ref, idx_ref, gate_ref, rank_ref, cnt_ref,
                     carry_ref, wo_raw, sg_raw, su_raw, sd_raw, wo_ref, sg_ref, su_ref, sd_ref, wsem):
    @pl.when(pl.program_id(0) == 0)
    def _():
        carry_ref[...] = jnp.zeros_like(carry_ref)
        staged = ((wo_hbm, wo_raw, wo_ref), (sg_hbm, sg_raw, sg_ref), (su_hbm, su_raw, su_ref),
                  (sd_hbm, sd_raw, sd_ref))
        copies = [pltpu.make_async_copy(src.at[0], raw, wsem.at[j]) for j, (src, raw, _) in enumerate(staged)]
        for cp in copies:
            cp.start()
        for cp, (_, raw, dst) in zip(copies, staged):
            cp.wait()
            dst[...] = raw[...].astype(BF16)

    mix = (jnp.dot(ys_ref[...], wo_ref[:D_SSM, :], preferred_element_type=F32)
           + jnp.dot(ya_ref[...], wo_ref[D_SSM:, :], preferred_element_type=F32))
    x1 = x_ref[...] + g1_ref[0] * mix
    ms = jnp.mean(x1 * x1, axis=-1, keepdims=True)
    h = x1 * lax.rsqrt(ms + EPS) * ng_ref[...]
    h = h * (1.0 + sc_ref[0]) + sh_ref[0]
    half = h.shape[1] // 2
    h_ref[...] = _pack_bf16_pair(h[:, :half], h[:, half:])
    hi, lo = _split_hi_lo(h)
    hi_terms = jnp.dot(hi, rwl_ref[...], preferred_element_type=F32)
    logits = (hi_terms[:, :LANES] + hi_terms[:, LANES:]
              + jnp.dot(lo, rwh_ref[...], preferred_element_type=F32))
    _route_tokens(logits.T[0:N_EXPERTS, :], rb_ref, upper_ref, idx_ref, gate_ref, rank_ref, cnt_ref, carry_ref)
    u = _silu(jnp.dot(hi, sg_ref[...], preferred_element_type=F32)) * jnp.dot(hi, su_ref[...],
                                                                              preferred_element_type=F32)
    shared = jnp.dot(u.astype(BF16), sd_ref[...], preferred_element_type=F32)
    base_ref[...] = x1 + g2_ref[0] * shared


def _out_proj_call(x2, ys, ya, g1, sc2, sh2, g2, ng, w_out, sg, su, sd, rwh, rwl, router_bias, upper,
                   tiles_per_batch, tm):
    n, d = x2.shape
    row = lambda w: pl.BlockSpec((tm, w), lambda i: (i, 0))
    tok = lambda r: pl.BlockSpec((r, tm), lambda i: (0, i))
    full = lambda a: pl.BlockSpec(a.shape, lambda i: (0, 0))
    per_batch = pl.BlockSpec((1, 1, d), lambda i: (i // tiles_per_batch, 0, 0))
    hbm = pl.BlockSpec(memory_space=pl.ANY)
    staged = (w_out, sg, su, sd)
    return pl.pallas_call(
        _out_proj_kernel,
        grid=(n // tm,),
        in_specs=[row(d), row(D_SSM), row(D_ATT), per_batch, per_batch, per_batch, per_batch, full(ng),
                  hbm, hbm, hbm, hbm, full(rwh), full(rwl), full(router_bias), full(upper)],
        out_specs=[row(d), row(d // 2), tok(TOP_K), tok(TOP_K), tok(TOP_K),
                   pl.BlockSpec((N_EXPERTS, LANES), lambda i: (0, 0))],
        out_shape=[jax.ShapeDtypeStruct((n, d), F32), jax.ShapeDtypeStruct((n, d // 2), jnp.uint32),
                   jax.ShapeDtypeStruct((TOP_K, n), jnp.int32), jax.ShapeDtypeStruct((TOP_K, n), F32),
                   jax.ShapeDtypeStruct((TOP_K, n), jnp.int32), jax.ShapeDtypeStruct((N_EXPERTS, LANES), F32)],
        scratch_shapes=([pltpu.VMEM((N_EXPERTS, LANES), F32)]
                        + [pltpu.VMEM(a.shape[1:], a.dtype) for a in staged]
                        + [pltpu.VMEM(a.shape[1:], BF16) for a in staged]
                        + [pltpu.SemaphoreType.DMA((len(staged),))]),
        compiler_params=pltpu.CompilerParams(dimension_semantics=("arbitrary",),
                                             vmem_limit_bytes=VMEM_LIMIT),
    )(x2, ys, ya, g1, sc2, sh2, g2, ng, w_out, sg, su, sd, rwh, rwl, router_bias, upper)


def _route_tokens(logits_t, rb_ref, upper_ref, idx_ref, gate_ref, rank_ref, cnt_ref, carry_ref):
    t = logits_t.shape[1]
    per_group = N_EXPERTS // ROUTE_GROUPS
    scores = 1.0 / (1.0 + jnp.exp(-logits_t))
    sel = scores + rb_ref[...]
    e_iota = lax.broadcasted_iota(jnp.int32, (N_EXPERTS, t), 0)

    sel3 = sel.reshape(ROUTE_GROUPS, per_group, t)
    w_iota = lax.broadcasted_iota(jnp.int32, sel3.shape, 1)
    m1 = jnp.max(sel3, axis=1, keepdims=True)
    first = jnp.min(jnp.where(sel3 == m1, w_iota, per_group), axis=1, keepdims=True)
    m2 = jnp.max(jnp.where(w_iota == first, NEG_INF, sel3), axis=1, keepdims=True)
    grp = (m1 + m2).reshape(ROUTE_GROUPS, t)

    g_iota = lax.broadcasted_iota(jnp.int32, (ROUTE_GROUPS, t), 0)
    gmask = jnp.zeros((ROUTE_GROUPS, t), jnp.bool_)
    for _ in range(ROUTE_TOPK_GROUPS):
        gm = jnp.max(grp, axis=0, keepdims=True)
        gfirst = jnp.min(jnp.where(grp == gm, g_iota, ROUTE_GROUPS), axis=0, keepdims=True)
        hit = g_iota == gfirst
        gmask = jnp.logical_or(gmask, hit)
        grp = jnp.where(hit, NEG_INF, grp)
    allowed = jnp.broadcast_to(gmask.reshape(ROUTE_GROUPS, 1, t),
                               (ROUTE_GROUPS, per_group, t)).reshape(N_EXPERTS, t)
    masked = jnp.where(allowed, sel, NEG_INF)

    picked = jnp.zeros((N_EXPERTS, t), jnp.bool_)
    idx_rows = []
    w_rows = []
    for _ in range(TOP_K):
        mm = jnp.max(masked, axis=0, keepdims=True)
        efirst = jnp.min(jnp.where(masked == mm, e_iota, N_EXPERTS), axis=0, keepdims=True)
        hit = e_iota == efirst
        idx_rows.append(efirst)
        w_rows.append(jnp.sum(jnp.where(hit, scores, 0.0), axis=0, keepdims=True))
        picked = jnp.logical_or(picked, hit)
        masked = jnp.where(hit, NEG_INF, masked)
    idx = jnp.concatenate(idx_rows, axis=0)
    w = jnp.concatenate(w_rows, axis=0)
    gate_ref[...] = w / jnp.sum(w, axis=0, keepdims=True) * ROUTED_SCALE
    idx_ref[...] = idx

    onehot = jnp.where(picked, 1.0, 0.0)
    sub = upper_ref.shape[0]
    carry = carry_ref[:, 0:1]
    parts = []
    for s0 in range(0, t, sub):
        oh = onehot[:, s0:s0 + sub]
        parts.append(jnp.dot(oh.astype(BF16), upper_ref[...], preferred_element_type=F32) + carry)
        carry = carry + jnp.sum(oh, axis=1, keepdims=True)
    rank_full = jnp.concatenate(parts, axis=1)
    rank_rows = [jnp.sum(jnp.where(e_iota == idx_rows[k], rank_full, 0.0), axis=0, keepdims=True)
                 for k in range(TOP_K)]
    rank_ref[...] = jnp.concatenate(rank_rows, axis=0).astype(jnp.int32)
    carry_ref[...] = jnp.broadcast_to(carry, carry_ref.shape)
    cnt_ref[...] = carry_ref[...]


PLAN_EXPERT, PLAN_FRESH, PLAN_VALID, PLAN_SEG, PLAN_NEXT, PLAN_NUSED = range(6)
PLAN_ROWS = SUBLANES


def _plan_kernel(cnt_ref, tri_ref, start_ref, plan_ref, *, nblocks):
    nbp = plan_ref.shape[1]
    cnt = cnt_ref[...].astype(jnp.int32)
    blocks = (cnt + (EXPERT_ROWS - 1)) // EXPERT_ROWS
    end = jnp.dot(tri_ref[...], blocks.astype(F32), precision=lax.Precision.HIGHEST,
                  preferred_element_type=F32).astype(jnp.int32)
    start = end - blocks
    start_ref[...] = start[:, 0:1] * EXPERT_ROWS
    nused = end[N_EXPERTS - 1:N_EXPERTS, 0:1]

    e_iota = lax.broadcasted_iota(jnp.int32, (N_EXPERTS, nbp), 0)
    blk = lax.broadcasted_iota(jnp.int32, (1, nbp), 1)
    expert = jnp.minimum(jnp.sum((end[:, 0:1] <= blk).astype(jnp.int32), axis=0, keepdims=True), N_EXPERTS - 1)
    mine = e_iota == expert
    pick = lambda col: jnp.sum(jnp.where(mine, col, 0), axis=0, keepdims=True)
    first = pick(start[:, 0:1])
    valid = jnp.clip(pick(cnt[:, 0:1]) - (blk - first) * EXPERT_ROWS, 0, EXPERT_ROWS)
    present = jnp.logical_or(blocks[:, 0:1] > 0,
                             jnp.logical_and(e_iota[:, 0:1] == N_EXPERTS - 1, nused < nblocks))
    seg = jnp.sum(jnp.logical_and(present, e_iota <= expert).astype(jnp.int32), axis=0, keepdims=True) - 1
    nxt = jnp.min(jnp.where(jnp.logical_and(present, e_iota > expert), e_iota, N_EXPERTS), axis=0, keepdims=True)
    rows = {PLAN_EXPERT: expert, PLAN_FRESH: (blk == first).astype(jnp.int32), PLAN_VALID: valid, PLAN_SEG: seg,
            PLAN_NEXT: jnp.where(nxt == N_EXPERTS, -1, nxt), PLAN_NUSED: jnp.broadcast_to(nused, (1, nbp))}
    zero = jnp.zeros((1, nbp), jnp.int32)
    plan_ref[...] = jnp.concatenate([rows.get(r, zero) for r in range(PLAN_ROWS)], axis=0)


def _plan_call(counts, nblocks):
    nbp = -(-nblocks // LANES) * LANES
    tri = jnp.asarray(np.tril(np.ones((N_EXPERTS, N_EXPERTS), np.float32)))
    return pl.pallas_call(
        functools.partial(_plan_kernel, nblocks=nblocks),
        out_shape=[jax.ShapeDtypeStruct((N_EXPERTS, 1), jnp.int32),
                   jax.ShapeDtypeStruct((PLAN_ROWS, nbp), jnp.int32)],
    )(counts, tri)


def _dest_kernel(idx_ref, rank_ref, start_ref, dest_ref):
    t = idx_ref.shape[1]
    e_iota = lax.broadcasted_iota(jnp.int32, (N_EXPERTS, t), 0)
    rows = [jnp.sum(jnp.where(e_iota == idx_ref[k:k + 1, :], start_ref[...], 0), axis=0, keepdims=True)
            for k in range(TOP_K)]
    dest = jnp.concatenate(rows, axis=0) + rank_ref[...]
    for k in range(TOP_K):
        for c in range(t // LANES):
            dest_ref[k, c:c + 1, :] = dest[k:k + 1, c * LANES:(c + 1) * LANES]


def _dest_call(idx, rank, pad_start, tile):
    n = idx.shape[1]
    tok = pl.BlockSpec((TOP_K, tile), lambda i: (0, i))
    return pl.pallas_call(
        _dest_kernel,
        grid=(n // tile,),
        in_specs=[tok, tok, pl.BlockSpec((N_EXPERTS, 1), lambda i: (0, 0))],
        out_specs=pl.BlockSpec((TOP_K, tile // LANES, LANES), lambda i: (0, i, 0)),
        out_shape=jax.ShapeDtypeStruct((TOP_K, n // LANES, LANES), jnp.int32),
        compiler_params=pltpu.CompilerParams(dimension_semantics=("arbitrary",)),
    )(idx, rank, pad_start)


def _scatter_rows_sc(rows, dest_flat, total_rows, chunk):
    n, w = rows.shape
    copies = dest_flat.shape[0] // n
    info = plsc.get_sparse_core_info()
    nc = info.num_cores
    per_worker = n // (nc * info.num_subcores)
    assert per_worker * nc * info.num_subcores == n and per_worker % chunk == 0
    mesh = plsc.VectorSubcoreMesh(core_axis_name="c", subcore_axis_name="s")

    @functools.partial(
        pl.kernel, mesh=mesh,
        out_type=jax.ShapeDtypeStruct((total_rows, w), rows.dtype),
        scratch_types=([pltpu.VMEM((chunk,), jnp.int32) for _ in range(copies)]
                       + [pltpu.VMEM((chunk, w), rows.dtype), pltpu.SemaphoreType.DMA, pltpu.SemaphoreType.DMA]),
    )
    def scatter(rows_hbm, idx_hbm, out_hbm, *scratch):
        idx_vs, (rows_v, isem, sem) = scratch[:copies], scratch[copies:]
        worker = lax.axis_index("s") * nc + lax.axis_index("c")

        @pl.loop(0, per_worker // chunk)
        def _(j):
            base = worker * per_worker + j * chunk
            loads = [pltpu.async_copy(idx_hbm.at[pl.ds(k * n + base, chunk)], idx_vs[k], isem)
                     for k in range(copies)]
            pltpu.sync_copy(rows_hbm.at[pl.ds(base, chunk)], rows_v)
            for ld in loads:
                ld.wait()
            stores = [pltpu.async_copy(rows_v, out_hbm.at[idx_vs[k]], sem) for k in range(copies)]
            for st in stores:
                st.wait()

    return scatter(rows, dest_flat)


EXPERT_LIVE_STEP = 256
X_RING = 4


def _expert_kernel(plan_ref, xs_hbm, wg_hbm, wu_hbm, wd_hbm, y_ref,
                   wgb_ref, wub_ref, wdb_ref, xbuf_ref, xsem, wg_raw, wu_raw, wd_raw, wsem):
    i = pl.program_id(0)
    rows, half = xbuf_ref.shape[1], xbuf_ref.shape[2]
    nused = plan_ref[PLAN_NUSED, 0]

    def w_copies(expert, slot):
        return [pltpu.make_async_copy(src.at[expert], dst.at[slot], wsem.at[slot])
                for src, dst in ((wg_hbm, wg_raw), (wu_hbm, wu_raw), (wd_hbm, wd_raw))]

    def x_copy(block):
        first = pl.multiple_of(block * rows, rows)
        slot = block % X_RING
        return pltpu.make_async_copy(xs_hbm.at[pl.ds(first, rows)], xbuf_ref.at[slot], xsem.at[slot])

    @pl.when(i == 0)
    def _():
        for b in range(X_RING - 1):
            @pl.when(b < nused)
            def _():
                x_copy(b).start()

    @pl.when(i + (X_RING - 1) < nused)
    def _():
        x_copy(i + (X_RING - 1)).start()

    @pl.when(i == 0)
    def _():
        for c in w_copies(plan_ref[PLAN_EXPERT, 0], 0):
            c.start()

    @pl.when(plan_ref[PLAN_FRESH, i] > 0)
    def _():
        slot = plan_ref[PLAN_SEG, i] % 2
        for c in w_copies(plan_ref[PLAN_EXPERT, i], slot):
            c.wait()

        @pl.when(plan_ref[PLAN_NEXT, i] >= 0)
        def _():
            for c in w_copies(plan_ref[PLAN_NEXT, i], 1 - slot):
                c.start()

        wgb_ref[...] = wg_raw[slot].astype(BF16)
        wub_ref[...] = wu_raw[slot].astype(BF16)
        wdb_ref[...] = wd_raw[slot].astype(BF16)

    @pl.when(i < nused)
    def _():
        x_copy(i).wait()

    real = plan_ref[PLAN_VALID, i]
    slot = i % X_RING

    def ffn(live):
        xw = xbuf_ref[slot, 0:live, :]
        row = lax.broadcasted_iota(jnp.int32, xw.shape, 0)
        x_lo, x_hi = _unpack_bf16_pair(jnp.where(row < real, xw, jnp.uint32(0)))
        x_lo = x_lo.astype(BF16)
        x_hi = x_hi.astype(BF16)
        gate = (jnp.dot(x_lo, wgb_ref[:half, :], preferred_element_type=F32)
                + jnp.dot(x_hi, wgb_ref[half:, :], preferred_element_type=F32))
        up = (jnp.dot(x_lo, wub_ref[:half, :], preferred_element_type=F32)
              + jnp.dot(x_hi, wub_ref[half:, :], preferred_element_type=F32))
        u = (_silu(gate) * up).astype(BF16)
        y_lo = jnp.dot(u, wdb_ref[:, :half], preferred_element_type=F32)
        y_hi = jnp.dot(u, wdb_ref[:, half:], preferred_element_type=F32)
        y_ref[0:live, :] = _pack_bf16_pair(y_lo, y_hi)
        if live < rows:
            y_ref[live:, :] = jnp.zeros((rows - live, half), y_ref.dtype)

    pieces = (real + (EXPERT_LIVE_STEP - 1)) // EXPERT_LIVE_STEP
    for p in range(1, rows // EXPERT_LIVE_STEP + 1):
        @pl.when(jnp.logical_and(i < nused, pieces == p))
        def _():
            ffn(p * EXPERT_LIVE_STEP)

    @pl.when(i >= nused)
    def _():
        y_ref[...] = jnp.zeros_like(y_ref)


def _expert_call(plan, xs, wg, wu, wd, rows):
    p, w = xs.shape
    d, f = wg.shape[1], wg.shape[2]
    hbm = pl.BlockSpec(memory_space=pl.ANY)
    grid_spec = pltpu.PrefetchScalarGridSpec(
        num_scalar_prefetch=1,
        grid=(p // rows,),
        in_specs=[hbm, hbm, hbm, hbm],
        out_specs=pl.BlockSpec((rows, w), lambda i, *_: (i, 0)),
        scratch_shapes=[pltpu.VMEM((d, f), BF16), pltpu.VMEM((d, f), BF16), pltpu.VMEM((f, d), BF16),
                        pltpu.VMEM((X_RING, rows, w), xs.dtype), pltpu.SemaphoreType.DMA((X_RING,)),
                        pltpu.VMEM((2, d, f), wg.dtype), pltpu.VMEM((2, d, f), wu.dtype),
                        pltpu.VMEM((2, f, d), wd.dtype), pltpu.SemaphoreType.DMA((2,))],
    )
    return pl.pallas_call(
        _expert_kernel,
        grid_spec=grid_spec,
        out_shape=jax.ShapeDtypeStruct((p, w), jnp.uint32),
        compiler_params=pltpu.CompilerParams(dimension_semantics=("arbitrary",),
                                             vmem_limit_bytes=VMEM_LIMIT),
    )(plan, xs, wg, wu, wd)


def _gather_rows_sc(table, idx, chunk):
    m = idx.shape[0]
    w = table.shape[1]
    info = plsc.get_sparse_core_info()
    nc = info.num_cores
    per_worker = m // (nc * info.num_subcores)
    assert per_worker * nc * info.num_subcores == m and per_worker % chunk == 0
    part = chunk // SC_GATHER_PARTS
    mesh = plsc.VectorSubcoreMesh(core_axis_name="c", subcore_axis_name="s")

    @functools.partial(
        pl.kernel, mesh=mesh,
        out_type=jax.ShapeDtypeStruct((m, w), table.dtype),
        scratch_types=([pltpu.VMEM((part,), jnp.int32) for _ in range(SC_GATHER_PARTS)]
                       + [pltpu.VMEM((part, w), table.dtype) for _ in range(SC_GATHER_PARTS)]
                       + [pltpu.SemaphoreType.DMA] * 3),
    )
    def gather(table_hbm, idx_hbm, out_hbm, *scratch):
        idx_vs = scratch[:SC_GATHER_PARTS]
        rows_vs = scratch[SC_GATHER_PARTS:2 * SC_GATHER_PARTS]
        isem, gsem, wsem = scratch[2 * SC_GATHER_PARTS:]
        worker = lax.axis_index("s") * nc + lax.axis_index("c")

        @pl.loop(0, per_worker // chunk)
        def _(j):
            base = worker * per_worker + j * chunk
            loads = [pltpu.async_copy(idx_hbm.at[pl.ds(base + p * part, part)], idx_vs[p], isem)
                     for p in range(SC_GATHER_PARTS)]
            gathers = []
            for p in range(SC_GATHER_PARTS):
                loads[p].wait()
                gathers.append(pltpu.async_copy(table_hbm.at[idx_vs[p]], rows_vs[p], gsem))
            writes = []
            for p in range(SC_GATHER_PARTS):
                gathers[p].wait()
                writes.append(pltpu.async_copy(rows_vs[p], out_hbm.at[pl.ds(base + p * part, part)], wsem))
            for wr in writes:
                wr.wait()

    return gather(table, idx)


def _combine_kernel(yk_ref, gate_ref, base_ref, g2_ref, fg_ref, o_ref):
    t = base_ref.shape[0]
    half = yk_ref.shape[2]
    gates = jnp.concatenate([gate_ref[...], jnp.zeros((LANES - TOP_K, t), F32)], axis=0).T
    r_lo = jnp.zeros((t, half), F32)
    r_hi = jnp.zeros((t, half), F32)
    for k in range(TOP_K):
        y_lo, y_hi = _unpack_bf16_pair(yk_ref[k])
        r_lo = r_lo + gates[:, k:k + 1] * y_lo
        r_hi = r_hi + gates[:, k:k + 1] * y_hi
    g2 = g2_ref[0]
    x_lo = base_ref[:, :half] + g2[:, :half] * r_lo
    x_hi = base_ref[:, half:] + g2[:, half:] * r_hi
    ms = (jnp.sum(x_lo * x_lo, axis=-1, keepdims=True)
          + jnp.sum(x_hi * x_hi, axis=-1, keepdims=True)) * (1.0 / (2 * half))
    inv = lax.rsqrt(ms + EPS)
    o_ref[:, :half] = x_lo * inv * fg_ref[:, :half]
    o_ref[:, half:] = x_hi * inv * fg_ref[:, half:]


def _combine_call(yk, gates, base, g2, fg, tiles_per_batch, tile, tile0):
    n, d = base.shape
    row = lambda w: pl.BlockSpec((tile, w), lambda i: (i + tile0, 0))
    return pl.pallas_call(
        _combine_kernel,
        grid=(yk.shape[1] // tile,),
        in_specs=[pl.BlockSpec((TOP_K, tile, yk.shape[2]), lambda i: (0, i, 0)),
                  pl.BlockSpec((TOP_K, tile), lambda i: (0, i + tile0)), row(d),
                  pl.BlockSpec((1, 1, d), lambda i: ((i + tile0) // tiles_per_batch, 0, 0)),
                  pl.BlockSpec((1, d), lambda i: (0, 0))],
        out_specs=row(d),
        out_shape=jax.ShapeDtypeStruct((n, d), F32),
        input_output_aliases={2: 0},
        compiler_params=pltpu.CompilerParams(dimension_semantics=("arbitrary",),
                                             vmem_limit_bytes=VMEM_LIMIT),
    )(yk, gates, base, g2, fg)


def _pad_cols(a, width):
    return jnp.pad(a, ((0, 0), (0, width - a.shape[1])))


def _layer(x, mod, norm1_g, norm2_g, w_in, conv_w, conv_b, dt_bias, a_log, d_skip, ssm_norm_g,
           att_norm_g, sinks, rel_bias, w_out, router_w, router_bias, exp_w_gate, exp_w_up, exp_w_down,
           sh_w_gate, sh_w_up, sh_w_down, final_g):
    bsz, l, d = x.shape
    n = bsz * l
    tm = min(ROW_TILE, l)

    sh1, sc1, g1, sh2, sc2, g2 = [m[:, None, :] for m in jnp.split(mod, 6, axis=-1)]

    assert math.frexp(ATT_HEAD_DIM ** -0.5)[0] == 0.5
    x2 = x.reshape(n, d)
    z, xbc, dt, q, k, v = _in_proj_call(x2, sc1, sh1, norm1_g[None, :], jnp.swapaxes(w_in, 1, 2), l // tm, tm)

    triu = jnp.asarray(np.triu(np.ones((CHUNK, CHUNK), np.float32))).astype(BF16)
    shift = jnp.asarray(_conv_shift_matrix()).astype(BF16)
    y_ssm = _ssd_call(xbc.reshape(bsz, l, CONV_CH), z.reshape(bsz, l, D_SSM), dt.reshape(bsz, l, LANES),
                      conv_w, conv_b[None, :], dt_bias[:, None], a_log[:, None],
                      jnp.repeat(d_skip, SSM_HEAD_DIM)[None, :], ssm_norm_g[None, :], triu, shift)

    bias = _bias_call(rel_bias, jnp.asarray(_rel_bucket_table()))
    y_att = _attn_call(sinks, q.reshape(bsz, l, D_ATT), k.reshape(bsz, l, D_KV), v.reshape(bsz, l, D_KV), bias,
                       att_norm_g[None, :])

    rw = _pad_cols(router_w, LANES)
    rwh = rw.astype(BF16)
    rwl = jnp.concatenate([rwh, (rw - rwh.astype(F32)).astype(BF16)], axis=1)
    rs = min(RANK_SUB, tm)
    upper = jnp.asarray(np.triu(np.ones((rs, rs), np.float32), 1)).astype(BF16)
    base, h2, idx, gates, rank, counts = _out_proj_call(
        x2, y_ssm.reshape(n, D_SSM), y_att.reshape(n, D_ATT), g1, sc2, sh2, g2, norm2_g[None, :],
        w_out, sh_w_gate, sh_w_up, sh_w_down, rwh, rwl, router_bias[:, None], upper, l // tm, tm)
    rt = min(ROUTE_TILE, n)

    nblocks = (n * TOP_K + N_EXPERTS * (EXPERT_ROWS - 1) + EXPERT_ROWS - 1) // EXPERT_ROWS
    pad_start, plan = _plan_call(counts, nblocks)
    dest = _dest_call(idx, rank, pad_start, rt)

    xs = _scatter_rows_sc(h2, dest.reshape(-1), nblocks * EXPERT_ROWS, SC_CHUNK)
    ys = _expert_call(plan, xs, exp_w_gate, exp_w_up, exp_w_down, EXPERT_ROWS)
    ctile = min(COMBINE_TILE, l)
    groups = COMBINE_GROUPS if bsz % COMBINE_GROUPS == 0 else 1
    ng = n // groups
    out = base
    for g in range(groups):
        idx_g = dest[:, g * ng // LANES:(g + 1) * ng // LANES, :].reshape(-1)
        yk = _gather_rows_sc(ys, idx_g, SC_CHUNK).reshape(TOP_K, ng, ys.shape[1])
        out = _combine_call(yk, gates, out, g2, final_g[None, :], l // ctile, ctile, g * ng // ctile)
    return out.reshape(bsz, l, d)


def kernel(x, c, mod_w, mod_b, norm1_g, norm2_g, w_in, conv_w, conv_b, dt_bias, a_log, d_skip, ssm_norm_g,
           att_norm_g, sinks, rel_bias, w_out, router_w, router_bias, exp_w_gate, exp_w_up, exp_w_down,
           sh_w_gate, sh_w_up, sh_w_down, final_g):
    assert mod_w.shape[0] == 1, "single-layer block"
    bsz = x.shape[0]
    c_pad = jnp.pad(c, ((0, SUBLANES - bsz % SUBLANES if bsz % SUBLANES else 0), (0, 0)))
    mod = _mod_call(c_pad, mod_w[0], mod_b[0][None, :])[:bsz]
    return _layer(x, mod, norm1_g[0], norm2_g[0], w_in, conv_w[0], conv_b[0], dt_bias[0], a_log[0], d_skip[0],
                  ssm_norm_g[0], att_norm_g[0], sinks[0], rel_bias, w_out, router_w[0], router_bias[0],
                  exp_w_gate[0], exp_w_up[0], exp_w_down[0], sh_w_gate, sh_w_up, sh_w_down, final_g)
```

```python
import functools
import math

import numpy as np
import jax
import jax.numpy as jnp
from jax import lax
from jax.experimental import pallas as pl
from jax.experimental.pallas import tpu as pltpu
from jax.experimental.pallas import tpu_sc as plsc

F32 = jnp.float32
BF16 = jnp.bfloat16

D_MODEL = 1024
SSM_HEAD_DIM = 64
D_SSM = D_MODEL
SSM_HEADS = D_SSM // SSM_HEAD_DIM
SSM_GROUPS = 4
D_STATE = 128
CONV_K = 4
CONV_CH = D_SSM + 2 * SSM_GROUPS * D_STATE
CHUNK = 128
ATT_HEAD_DIM = 64
D_ATT = D_MODEL
ATT_HEADS = D_ATT // ATT_HEAD_DIM
KV_HEADS = ATT_HEADS // 4
Q_PER_KV = ATT_HEADS // KV_HEADS
D_KV = KV_HEADS * ATT_HEAD_DIM
WINDOW = 128
ATT_BLOCK = 128
REL_BUCKETS = 32
REL_MAX_DIST = 128
N_EXPERTS = 64
TOP_K = 8
EXPERT_DIM = D_MODEL // 4
SHARED_DIM = D_MODEL // 4
ROUTE_GROUPS = 8
ROUTE_TOPK_GROUPS = 4
ROUTED_SCALE = 2.5
EPS = 1e-6

LANES = 128
SUBLANES = 8
HALF = LANES // 2

ROW_TILE = 512
ROUTE_TILE = 2048
RANK_SUB = 512
COMBINE_TILE = 512
SC_CHUNK = 128
SC_GATHER_PARTS = 4
EXPERT_ROWS = 1024
COMBINE_GROUPS = 2
VMEM_LIMIT = 48 * 1024 * 1024

NEG_INF = float("-inf")


def _silu(v):
    return v * (1.0 / (1.0 + jnp.exp(-v)))


def _softplus(v):
    return jnp.maximum(v, 0.0) + jnp.log(1.0 + jnp.exp(-jnp.abs(v)))


def _split_hi_lo(v):
    hi = v.astype(BF16)
    lo = (v - hi.astype(F32)).astype(BF16)
    return hi, lo


def _pack_bf16_pair(a, b):
    w = pltpu.pack_elementwise([a, b], packed_dtype=BF16)
    return w if w.dtype == jnp.uint32 else lax.bitcast_convert_type(w, jnp.uint32)


def _unpack_bf16_pair(w):
    a = pltpu.unpack_elementwise(w, index=0, packed_dtype=BF16, unpacked_dtype=F32)
    b = pltpu.unpack_elementwise(w, index=1, packed_dtype=BF16, unpacked_dtype=F32)
    return a, b


def _lane_half_mask(shape):
    return lax.broadcasted_iota(jnp.int32, shape, len(shape) - 1) < HALF


def _mod_kernel(c_ref, w_ref, b_ref, o_ref):
    a = _silu(c_ref[...])
    o_ref[...] = jnp.dot(a, w_ref[...], precision=lax.Precision.HIGHEST,
                         preferred_element_type=F32) + b_ref[...]


def _mod_call(c_pad, mod_w, mod_b):
    rows, d = c_pad.shape
    cols = mod_w.shape[1]
    return pl.pallas_call(
        _mod_kernel,
        grid=(cols // d,),
        in_specs=[pl.BlockSpec((rows, d), lambda j: (0, 0)),
                  pl.BlockSpec((d, d), lambda j: (0, j)),
                  pl.BlockSpec((1, d), lambda j: (0, j))],
        out_specs=pl.BlockSpec((rows, d), lambda j: (0, j)),
        out_shape=jax.ShapeDtypeStruct((rows, cols), F32),
        compiler_params=pltpu.CompilerParams(dimension_semantics=("arbitrary",),
                                             vmem_limit_bytes=VMEM_LIMIT),
    )(c_pad, mod_w, mod_b)


IN_PROJ_SEGMENTS = ((D_SSM, BF16), (CONV_CH, BF16), (LANES, F32), (D_ATT, BF16), (D_KV, BF16), (D_KV, BF16))


def _in_proj_kernel(x_ref, sc_ref, sh_ref, g_ref, w_hbm, *refs):
    out_refs = refs[:len(IN_PROJ_SEGMENTS)]
    wraw_ref, w_ref, wsem = refs[len(IN_PROJ_SEGMENTS):]

    @pl.when(pl.program_id(0) == 0)
    def _():
        cp = pltpu.make_async_copy(w_hbm.at[0], wraw_ref, wsem)
        cp.start()
        cp.wait()
        src_dt = D_SSM + CONV_CH
        dst_q = src_dt + LANES
        src_q = src_dt + SSM_HEADS
        q_scale = ATT_HEAD_DIM ** -0.5

        def put(dst, src, scale=None):
            t = wraw_ref[src:src + LANES, :].T
            w_ref[:, dst:dst + LANES] = (t if scale is None else t * scale).astype(BF16)

        for c0 in range(0, src_dt, LANES):
            put(c0, c0)
        dt_tile = wraw_ref[src_dt:src_dt + LANES, :].T
        lane = lax.broadcasted_iota(jnp.int32, dt_tile.shape, 1)
        w_ref[:, src_dt:dst_q] = jnp.where(lane < SSM_HEADS, dt_tile, 0.0).astype(BF16)
        for c0 in range(0, D_ATT, LANES):
            put(dst_q + c0, src_q + c0, q_scale)
        for c0 in range(D_ATT, D_ATT + 2 * D_KV, LANES):
            put(dst_q + c0, src_q + c0)

    xf = x_ref[...]
    ms = jnp.mean(xf * xf, axis=-1, keepdims=True)
    h = xf * lax.rsqrt(ms + EPS) * g_ref[...]
    h = h * (1.0 + sc_ref[0]) + sh_ref[0]
    hb = h.astype(BF16)
    col = 0
    for (width, dtype), o_ref in zip(IN_PROJ_SEGMENTS, out_refs):
        o_ref[...] = jnp.dot(hb, w_ref[:, col:col + width], preferred_element_type=F32).astype(dtype)
        col += width


def _in_proj_call(x2, sc1, sh1, g1n, w_in_t, tiles_per_batch, tm):
    n, d = x2.shape
    row = lambda w: pl.BlockSpec((tm, w), lambda i: (i, 0))
    full = lambda a: pl.BlockSpec(a.shape, lambda i: (0, 0))
    per_batch = pl.BlockSpec((1, 1, d), lambda i: (i // tiles_per_batch, 0, 0))
    cols = sum(w for w, _ in IN_PROJ_SEGMENTS)
    w_in = w_in_t
    assert w_in.shape == (1, cols - (LANES - SSM_HEADS), d)
    return pl.pallas_call(
        _in_proj_kernel,
        grid=(n // tm,),
        in_specs=[row(d), per_batch, per_batch, full(g1n), pl.BlockSpec(memory_space=pl.ANY)],
        out_specs=[row(w) for w, _ in IN_PROJ_SEGMENTS],
        out_shape=[jax.ShapeDtypeStruct((n, w), dt) for w, dt in IN_PROJ_SEGMENTS],
        scratch_shapes=[pltpu.VMEM(w_in.shape[1:], w_in.dtype), pltpu.VMEM((d, cols), BF16), pltpu.SemaphoreType.DMA(())],
        compiler_params=pltpu.CompilerParams(dimension_semantics=("arbitrary",),
                                             vmem_limit_bytes=VMEM_LIMIT),
    )(x2, sc1, sh1, g1n, w_in)


SSD_SEQS = 4
CONV_HALO = 16


def _conv_shift_matrix():
    s = np.zeros((CONV_K * CHUNK, CONV_HALO + CHUNK), np.float32)
    for k in range(CONV_K):
        t = np.arange(CHUNK)
        s[k * CHUNK + t, CONV_HALO + t - (CONV_K - 1) + k] = 1.0
    return s


def _silu_tanh(v):
    hv = 0.5 * v
    return hv + hv * jnp.tanh(hv)


def _ssd_kernel(xbc_ref, z_ref, dt_ref, cw_ref, cb_ref, dtb_ref, alog_ref, dskip_ref, ng_ref, triu_ref, shift_ref,
                y_ref, state_ref, ucat_ref, ybuf_ref):
    nseq = xbc_ref.shape[0]

    @pl.when(pl.program_id(1) == 0)
    def _():
        state_ref[...] = jnp.zeros_like(state_ref)
        ucat_ref[:, 0:CONV_HALO, :] = jnp.zeros((nseq, CONV_HALO, CONV_CH), BF16)

    for q in range(nseq):
        _ssd_chunk(xbc_ref.at[q], z_ref.at[q], dt_ref.at[q], cw_ref, cb_ref, dtb_ref, alog_ref, dskip_ref, ng_ref,
                   triu_ref, shift_ref, y_ref.at[q], state_ref.at[q], ucat_ref.at[q], ybuf_ref.at[q])


def _ssd_chunk(xbc_ref, z_ref, dt_ref, cw_ref, cb_ref, dtb_ref, alog_ref, dskip_ref, ng_ref, triu_ref, shift_ref,
               y_ref, state_ref, ucat_ref, ybuf_ref):
    ucat_ref[CONV_HALO:, :] = xbc_ref[...]
    shifted = jnp.dot(shift_ref[...], ucat_ref[...], preferred_element_type=F32)
    ucat_ref[0:CONV_HALO, :] = ucat_ref[CHUNK:CHUNK + CONV_HALO, :]
    acc = cb_ref[...] + cw_ref[0:1, :] * shifted[0:CHUNK]
    for kk in range(1, CONV_K):
        acc = acc + cw_ref[kk:kk + 1, :] * shifted[kk * CHUNK:(kk + 1) * CHUNK]
    act = _silu_tanh(acc)
    xs = act[:, :D_SSM]
    gn = SSM_GROUPS * D_STATE

    dt_t = _softplus(dt_ref[...].T[0:SSM_HEADS, :] + dtb_ref[...])
    a_t = dt_t * (-jnp.exp(alog_ref[...]))
    a_hi = a_t.astype(BF16)
    a_mid = (a_t - a_hi.astype(F32)).astype(BF16)
    a_lo = (a_t - a_hi.astype(F32) - a_mid.astype(F32)).astype(BF16)
    triu = triu_ref[...]
    cs_t = (jnp.dot(a_hi, triu, preferred_element_type=F32) + jnp.dot(a_mid, triu, preferred_element_type=F32)
            + jnp.dot(a_lo, triu, preferred_element_type=F32))
    cs_end = cs_t[:, CHUNK - 1:CHUNK]
    r_t = cs_t - jnp.log(dt_t)
    w_t = jnp.exp(cs_end - cs_t) * dt_t
    chunk_decay = jnp.exp(cs_end)
    cols = jnp.concatenate([cs_t, jnp.exp(cs_t), jnp.zeros((LANES - 2 * SSM_HEADS, CHUNK), F32)], axis=0).T

    li = lax.broadcasted_iota(jnp.int32, (CHUNK, CHUNK), 0)
    si = lax.broadcasted_iota(jnp.int32, (CHUNK, CHUNK), 1)
    causal = li >= si
    low = _lane_half_mask((CHUNK, LANES))
    low_row = _lane_half_mask((1, LANES))

    heads_per_group = SSM_HEADS // SSM_GROUPS
    for g in range(SSM_GROUPS):
        b_g = act[:, D_SSM + g * D_STATE:D_SSM + (g + 1) * D_STATE]
        c_g = act[:, D_SSM + gn + g * D_STATE:D_SSM + gn + (g + 1) * D_STATE]
        b_gb = b_g.astype(BF16)
        c_gb = c_g.astype(BF16)
        cb = lax.dot_general(c_gb, b_gb, (((1,), (1,)), ((), ())), preferred_element_type=F32)
        b_t = b_g.T
        for jp in range(heads_per_group // 2):
            j = g * (heads_per_group // 2) + jp
            lanes = slice(j * LANES, (j + 1) * LANES)
            xp = xs[:, lanes]
            xpb = xp.astype(BF16)
            ydiag = jnp.zeros((CHUNK, LANES), F32)
            snew = jnp.zeros((D_STATE, LANES), F32)
            for half in range(2):
                h = 2 * j + half
                diff = cols[:, h:h + 1] - r_t[h:h + 1, :]
                m = (cb * jnp.exp(jnp.where(causal, diff, NEG_INF))).astype(BF16)
                keep = low if half == 0 else jnp.logical_not(low)
                xh = jnp.where(keep, xpb, jnp.zeros_like(xpb))
                ydiag = ydiag + jnp.dot(m, xh, preferred_element_type=F32)
                snew = snew + jnp.dot((b_t * w_t[h:h + 1, :]).astype(BF16), xh, preferred_element_type=F32)
            s_in = state_ref[:, lanes]
            yoff = jnp.dot(c_gb, s_in.astype(BF16), preferred_element_type=F32)
            h0 = 2 * j
            e0 = SSM_HEADS + h0
            escale = jnp.where(low, cols[:, e0:e0 + 1], cols[:, e0 + 1:e0 + 2])
            cdec = jnp.where(low_row, chunk_decay[h0:h0 + 1, :], chunk_decay[h0 + 1:h0 + 2, :])
            ybuf_ref[:, lanes] = ydiag + yoff * escale + xp * dskip_ref[:, lanes]
            state_ref[:, lanes] = s_in * cdec + snew

    yz = ybuf_ref[...] * _silu_tanh(z_ref[...].astype(F32))
    gw = D_SSM // SSM_GROUPS
    for g in range(SSM_GROUPS):
        part = yz[:, g * gw:(g + 1) * gw]
        ms = jnp.mean(part * part, axis=-1, keepdims=True)
        y_ref[:, g * gw:(g + 1) * gw] = (part * lax.rsqrt(ms + EPS)
                                            * ng_ref[:, g * gw:(g + 1) * gw]).astype(BF16)


def _ssd_call(xbc, z, dt, conv_w, conv_b, dtb, alog, dskip, ng, triu, shift):
    bsz, l, _ = xbc.shape
    nc = l // CHUNK
    nseq = SSD_SEQS if bsz % SSD_SEQS == 0 else 1
    chunk = lambda w: pl.BlockSpec((nseq, CHUNK, w), lambda b, c: (b, c, 0))
    full = lambda a: pl.BlockSpec(a.shape, lambda b, c: (0, 0))
    return pl.pallas_call(
        _ssd_kernel,
        grid=(bsz // nseq, nc),
        in_specs=[chunk(CONV_CH), chunk(D_SSM), chunk(LANES), full(conv_w), full(conv_b), full(dtb),
                  full(alog), full(dskip), full(ng), full(triu), full(shift)],
        out_specs=chunk(D_SSM),
        out_shape=jax.ShapeDtypeStruct((bsz, l, D_SSM), BF16),
        scratch_shapes=[pltpu.VMEM((nseq, D_STATE, D_SSM), F32),
                        pltpu.VMEM((nseq, CONV_HALO + CHUNK, CONV_CH), BF16),
                        pltpu.VMEM((nseq, CHUNK, D_SSM), F32)],
        compiler_params=pltpu.CompilerParams(dimension_semantics=("arbitrary", "arbitrary"),
                                             vmem_limit_bytes=VMEM_LIMIT),
    )(xbc, z, dt, conv_w, conv_b, dtb, alog, dskip, ng, triu, shift)


assert WINDOW == ATT_BLOCK


def _rel_bucket_table():
    qi = np.arange(ATT_BLOCK)[:, None]
    c = np.arange(ATT_BLOCK)[None, :]
    dist = np.where(c > qi, qi + ATT_BLOCK - c, qi - c)
    max_exact = REL_BUCKETS // 2
    d = np.maximum(dist, 1).astype(np.float32)
    large = max_exact + (np.log(d / np.float32(max_exact)) / np.float32(math.log(REL_MAX_DIST / max_exact))
                         * np.float32(REL_BUCKETS - max_exact)).astype(np.int32)
    large = np.minimum(large, REL_BUCKETS - 1)
    return np.where(dist < max_exact, dist, large).astype(np.int32)


def _bias_kernel(rb_ref, bucket_ref, o_ref):
    bucket = bucket_ref[...]
    from_prev = (lax.broadcasted_iota(jnp.int32, bucket.shape, 1)
                 > lax.broadcasted_iota(jnp.int32, bucket.shape, 0))
    for h in range(ATT_HEADS):
        acc = jnp.zeros(bucket.shape, F32)
        for b in range(REL_BUCKETS):
            acc = jnp.where(bucket == b, rb_ref[b, h], acc)
        o_ref[1, h] = acc
        o_ref[0, h] = jnp.where(from_prev, NEG_INF, acc)


def _bias_call(rel_bias, bucket):
    return pl.pallas_call(
        _bias_kernel,
        in_specs=[pl.BlockSpec(memory_space=pltpu.SMEM), pl.BlockSpec(memory_space=pltpu.VMEM)],
        out_shape=jax.ShapeDtypeStruct((2, ATT_HEADS) + bucket.shape, F32),
    )(rel_bias, bucket)


ATT_SEQS = 4


def _attn_kernel(sink_ref, q_ref, kp_ref, kc_ref, vp_ref, vc_ref, bias_ref, ng_ref, o_ref, obuf_ref):
    for s in range(q_ref.shape[0]):
        _attn_block(sink_ref, q_ref.at[s], kp_ref.at[s], kc_ref.at[s], vp_ref.at[s], vc_ref.at[s], bias_ref.at[0],
                    ng_ref, o_ref.at[s], obuf_ref.at[s])


def _attn_block(sink_ref, q_ref, kp_ref, kc_ref, vp_ref, vc_ref, bias_ref, ng_ref, o_ref, obuf_ref):
    qi = lax.broadcasted_iota(jnp.int32, (ATT_BLOCK, ATT_BLOCK), 0)
    ci = lax.broadcasted_iota(jnp.int32, (ATT_BLOCK, ATT_BLOCK), 1)
    from_prev = ci > qi
    low = _lane_half_mask((ATT_BLOCK, LANES))

    def band_variants(prev_ref, cur_ref):
        out = []
        for cpair in range(KV_HEADS // 2):
            lanes = slice(cpair * LANES, (cpair + 1) * LANES)
            t = jnp.concatenate([prev_ref[:, lanes], cur_ref[:, lanes]], axis=0).astype(F32)
            out.append((t.astype(BF16), pltpu.roll(t, HALF, 1).astype(BF16)))
        return out

    k_band = band_variants(kp_ref, kc_ref)
    v_band = band_variants(vp_ref, vc_ref)
    nt = (((1,), (1,)), ((), ()))

    for j in range(ATT_HEADS // 2):
        qp = q_ref[:, j * LANES:(j + 1) * LANES]
        out_pair = jnp.zeros((ATT_BLOCK, LANES), F32)
        for half in range(2):
            h = 2 * j + half
            g = h // Q_PER_KV
            swapped = int((g % 2) != half)
            keep = low if half == 0 else jnp.logical_not(low)
            qh = jnp.where(keep, qp, jnp.zeros_like(qp))
            s_band = lax.dot_general(qh, k_band[g // 2][swapped], nt, preferred_element_type=F32)
            s = jnp.where(from_prev, s_band[:, :ATT_BLOCK], s_band[:, ATT_BLOCK:]) + bias_ref[h]
            sink = sink_ref[h]
            m = jnp.maximum(jnp.max(s, axis=-1, keepdims=True), sink)
            p = jnp.exp(s - m)
            denom = jnp.sum(p, axis=-1, keepdims=True) + jnp.exp(sink - m)
            p_band = jnp.concatenate([jnp.where(from_prev, p, 0.0), jnp.where(from_prev, 0.0, p)], axis=1)
            o = jnp.dot(p_band.astype(BF16), v_band[g // 2][swapped], preferred_element_type=F32) / denom
            out_pair = out_pair + jnp.where(keep, o, 0.0)
        obuf_ref[:, j * LANES:(j + 1) * LANES] = out_pair

    att = obuf_ref[...]
    ms = jnp.mean(att * att, axis=-1, keepdims=True)
    o_ref[...] = (att * lax.rsqrt(ms + EPS) * ng_ref[...]).astype(BF16)


def _attn_call(sinks, q, k, v, bias, ng):
    bsz, l, _ = q.shape
    nb = l // ATT_BLOCK
    nseq = ATT_SEQS if bsz % ATT_SEQS == 0 else 1
    cur = lambda w: pl.BlockSpec((nseq, ATT_BLOCK, w), lambda b, i: (b, i, 0))
    prev = lambda w: pl.BlockSpec((nseq, ATT_BLOCK, w), lambda b, i: (b, jnp.maximum(i - 1, 0), 0))
    return pl.pallas_call(
        _attn_kernel,
        grid=(bsz // nseq, nb),
        in_specs=[pl.BlockSpec(memory_space=pltpu.SMEM),
                  cur(D_ATT), prev(D_KV), cur(D_KV), prev(D_KV), cur(D_KV),
                  pl.BlockSpec((1,) + bias.shape[1:], lambda b, i: (jnp.minimum(i, 1), 0, 0, 0)),
                  pl.BlockSpec(ng.shape, lambda b, i: (0, 0))],
        out_specs=cur(D_ATT),
        out_shape=jax.ShapeDtypeStruct((bsz, l, D_ATT), BF16),
        scratch_shapes=[pltpu.VMEM((nseq, ATT_BLOCK, D_ATT), F32)],
        compiler_params=pltpu.CompilerParams(dimension_semantics=("arbitrary", "arbitrary"),
                                             vmem_limit_bytes=VMEM_LIMIT),
    )(sinks, q, k, k, v, v, bias, ng)


def _out_proj_kernel(x_ref, ys_ref, ya_ref, g1_ref, sc_ref, sh_ref, g2_ref, ng_ref, wo_hbm, sg_hbm, su_hbm, sd_hbm,
                     rwh_ref, rwl_ref, rb_ref, upper_ref,
                     base_ref, h_ref, idx_ref, gate_ref, rank_ref, cnt_ref,
                     carry_ref, wo_raw, sg_raw, su_raw, sd_raw, wo_ref, sg_ref, su_ref, sd_ref, wsem):
    @pl.when(pl.program_id(0) == 0)
    def _():
        carry_ref[...] = jnp.zeros_like(carry_ref)
        staged = ((wo_hbm, wo_raw, wo_ref), (sg_hbm, sg_raw, sg_ref), (su_hbm, su_raw, su_ref),
                  (sd_hbm, sd_raw, sd_ref))
        copies = [pltpu.make_async_copy(src.at[0], raw, wsem.at[j]) for j, (src, raw, _) in enumerate(staged)]
        for cp in copies:
            cp.start()
        for cp, (_, raw, dst) in zip(copies, staged):
            cp.wait()
            dst[...] = raw[...].astype(BF16)

    mix = (jnp.dot(ys_ref[...], wo_ref[:D_SSM, :], preferred_element_type=F32)
           + jnp.dot(ya_ref[...], wo_ref[D_SSM:, :], preferred_element_type=F32))
    x1 = x_ref[...] + g1_ref[0] * mix
    ms = jnp.mean(x1 * x1, axis=-1, keepdims=True)
    h = x1 * lax.rsqrt(ms + EPS) * ng_ref[...]
    h = h * (1.0 + sc_ref[0]) + sh_ref[0]
    half = h.shape[1] // 2
    h_ref[...] = _pack_bf16_pair(h[:, :half], h[:, half:])
    hi, lo = _split_hi_lo(h)
    hi_terms = jnp.dot(hi, rwl_ref[...], preferred_element_type=F32)
    logits = (hi_terms[:, :LANES] + hi_terms[:, LANES:]
              + jnp.dot(lo, rwh_ref[...], preferred_element_type=F32))
    _route_tokens(logits.T[0:N_EXPERTS, :], rb_ref, upper_ref, idx_ref, gate_ref, rank_ref, cnt_ref, carry_ref)
    u = _silu(jnp.dot(hi, sg_ref[...], preferred_element_type=F32)) * jnp.dot(hi, su_ref[...],
                                                                              preferred_element_type=F32)
    shared = jnp.dot(u.astype(BF16), sd_ref[...], preferred_element_type=F32)
    base_ref[...] = x1 + g2_ref[0] * shared


def _out_proj_call(x2, ys, ya, g1, sc2, sh2, g2, ng, w_out, sg, su, sd, rwh, rwl, router_bias, upper,
                   tiles_per_batch, tm):
    n, d = x2.shape
    row = lambda w: pl.BlockSpec((tm, w), lambda i: (i, 0))
    tok = lambda r: pl.BlockSpec((r, tm), lambda i: (0, i))
    full = lambda a: pl.BlockSpec(a.shape, lambda i: (0, 0))
    per_batch = pl.BlockSpec((1, 1, d), lambda i: (i // tiles_per_batch, 0, 0))
    hbm = pl.BlockSpec(memory_space=pl.ANY)
    staged = (w_out, sg, su, sd)
    return pl.pallas_call(
        _out_proj_kernel,
        grid=(n // tm,),
        in_specs=[row(d), row(D_SSM), row(D_ATT), per_batch, per_batch, per_batch, per_batch, full(ng),
                  hbm, hbm, hbm, hbm, full(rwh), full(rwl), full(router_bias), full(upper)],
        out_specs=[row(d), row(d // 2), tok(TOP_K), tok(TOP_K), tok(TOP_K),
                   pl.BlockSpec((N_EXPERTS, LANES), lambda i: (0, 0))],
        out_shape=[jax.ShapeDtypeStruct((n, d), F32), jax.ShapeDtypeStruct((n, d // 2), jnp.uint32),
                   jax.ShapeDtypeStruct((TOP_K, n), jnp.int32), jax.ShapeDtypeStruct((TOP_K, n), F32),
                   jax.ShapeDtypeStruct((TOP_K, n), jnp.int32), jax.ShapeDtypeStruct((N_EXPERTS, LANES), F32)],
        scratch_shapes=([pltpu.VMEM((N_EXPERTS, LANES), F32)]
                        + [pltpu.VMEM(a.shape[1:], a.dtype) for a in staged]
                        + [pltpu.VMEM(a.shape[1:], BF16) for a in staged]
                        + [pltpu.SemaphoreType.DMA((len(staged),))]),
        compiler_params=pltpu.CompilerParams(dimension_semantics=("arbitrary",),
                                             vmem_limit_bytes=VMEM_LIMIT),
    )(x2, ys, ya, g1, sc2, sh2, g2, ng, w_out, sg, su, sd, rwh, rwl, router_bias, upper)


def _route_tokens(logits_t, rb_ref, upper_ref, idx_ref, gate_ref, rank_ref, cnt_ref, carry_ref):
    t = logits_t.shape[1]
    per_group = N_EXPERTS // ROUTE_GROUPS
    scores = 1.0 / (1.0 + jnp.exp(-logits_t))
    sel = scores + rb_ref[...]
    e_iota = lax.broadcasted_iota(jnp.int32, (N_EXPERTS, t), 0)

    sel3 = sel.reshape(ROUTE_GROUPS, per_group, t)
    w_iota = lax.broadcasted_iota(jnp.int32, sel3.shape, 1)
    m1 = jnp.max(sel3, axis=1, keepdims=True)
    first = jnp.min(jnp.where(sel3 == m1, w_iota, per_group), axis=1, keepdims=True)
    m2 = jnp.max(jnp.where(w_iota == first, NEG_INF, sel3), axis=1, keepdims=True)
    grp = (m1 + m2).reshape(ROUTE_GROUPS, t)

    g_iota = lax.broadcasted_iota(jnp.int32, (ROUTE_GROUPS, t), 0)
    gmask = jnp.zeros((ROUTE_GROUPS, t), jnp.bool_)
    for _ in range(ROUTE_TOPK_GROUPS):
        gm = jnp.max(grp, axis=0, keepdims=True)
        gfirst = jnp.min(jnp.where(grp == gm, g_iota, ROUTE_GROUPS), axis=0, keepdims=True)
        hit = g_iota == gfirst
        gmask = jnp.logical_or(gmask, hit)
        grp = jnp.where(hit, NEG_INF, grp)
    allowed = jnp.broadcast_to(gmask.reshape(ROUTE_GROUPS, 1, t),
                               (ROUTE_GROUPS, per_group, t)).reshape(N_EXPERTS, t)
    masked = jnp.where(allowed, sel, NEG_INF)

    picked = jnp.zeros((N_EXPERTS, t), jnp.bool_)
    idx_rows = []
    w_rows = []
    for _ in range(TOP_K):
        mm = jnp.max(masked, axis=0, keepdims=True)
        efirst = jnp.min(jnp.where(masked == mm, e_iota, N_EXPERTS), axis=0, keepdims=True)
        hit = e_iota == efirst
        idx_rows.append(efirst)
        w_rows.append(jnp.sum(jnp.where(hit, scores, 0.0), axis=0, keepdims=True))
        picked = jnp.logical_or(picked, hit)
        masked = jnp.where(hit, NEG_INF, masked)
    idx = jnp.concatenate(idx_rows, axis=0)
    w = jnp.concatenate(w_rows, axis=0)
    gate_ref[...] = w / jnp.sum(w, axis=0, keepdims=True) * ROUTED_SCALE
    idx_ref[...] = idx

    onehot = jnp.where(picked, 1.0, 0.0)
    sub = upper_ref.shape[0]
    carry = carry_ref[:, 0:1]
    parts = []
    for s0 in range(0, t, sub):
        oh = onehot[:, s0:s0 + sub]
        parts.append(jnp.dot(oh.astype(BF16), upper_ref[...], preferred_element_type=F32) + carry)
        carry = carry + jnp.sum(oh, axis=1, keepdims=True)
    rank_full = jnp.concatenate(parts, axis=1)
    rank_rows = [jnp.sum(jnp.where(e_iota == idx_rows[k], rank_full, 0.0), axis=0, keepdims=True)
                 for k in range(TOP_K)]
    rank_ref[...] = jnp.concatenate(rank_rows, axis=0).astype(jnp.int32)
    carry_ref[...] = jnp.broadcast_to(carry, carry_ref.shape)
    cnt_ref[...] = carry_ref[...]


PLAN_EXPERT, PLAN_FRESH, PLAN_VALID, PLAN_SEG, PLAN_NEXT, PLAN_NUSED = range(6)
PLAN_ROWS = SUBLANES


def _plan_kernel(cnt_ref, tri_ref, start_ref, plan_ref, *, nblocks):
    nbp = plan_ref.shape[1]
    cnt = cnt_ref[...].astype(jnp.int32)
    blocks = (cnt + (EXPERT_ROWS - 1)) // EXPERT_ROWS
    end = jnp.dot(tri_ref[...], blocks.astype(F32), precision=lax.Precision.HIGHEST,
                  preferred_element_type=F32).astype(jnp.int32)
    start = end - blocks
    start_ref[...] = start[:, 0:1] * EXPERT_ROWS
    nused = end[N_EXPERTS - 1:N_EXPERTS, 0:1]

    e_iota = lax.broadcasted_iota(jnp.int32, (N_EXPERTS, nbp), 0)
    blk = lax.broadcasted_iota(jnp.int32, (1, nbp), 1)
    expert = jnp.minimum(jnp.sum((end[:, 0:1] <= blk).astype(jnp.int32), axis=0, keepdims=True), N_EXPERTS - 1)
    mine = e_iota == expert
    pick = lambda col: jnp.sum(jnp.where(mine, col, 0), axis=0, keepdims=True)
    first = pick(start[:, 0:1])
    valid = jnp.clip(pick(cnt[:, 0:1]) - (blk - first) * EXPERT_ROWS, 0, EXPERT_ROWS)
    present = jnp.logical_or(blocks[:, 0:1] > 0,
                             jnp.logical_and(e_iota[:, 0:1] == N_EXPERTS - 1, nused < nblocks))
    seg = jnp.sum(jnp.logical_and(present, e_iota <= expert).astype(jnp.int32), axis=0, keepdims=True) - 1
    nxt = jnp.min(jnp.where(jnp.logical_and(present, e_iota > expert), e_iota, N_EXPERTS), axis=0, keepdims=True)
    rows = {PLAN_EXPERT: expert, PLAN_FRESH: (blk == first).astype(jnp.int32), PLAN_VALID: valid, PLAN_SEG: seg,
            PLAN_NEXT: jnp.where(nxt == N_EXPERTS, -1, nxt), PLAN_NUSED: jnp.broadcast_to(nused, (1, nbp))}
    zero = jnp.zeros((1, nbp), jnp.int32)
    plan_ref[...] = jnp.concatenate([rows.get(r, zero) for r in range(PLAN_ROWS)], axis=0)


def _plan_call(counts, nblocks):
    nbp = -(-nblocks // LANES) * LANES
    tri = jnp.asarray(np.tril(np.ones((N_EXPERTS, N_EXPERTS), np.float32)))
    return pl.pallas_call(
        functools.partial(_plan_kernel, nblocks=nblocks),
        out_shape=[jax.ShapeDtypeStruct((N_EXPERTS, 1), jnp.int32),
                   jax.ShapeDtypeStruct((PLAN_ROWS, nbp), jnp.int32)],
    )(counts, tri)


def _dest_kernel(idx_ref, rank_ref, start_ref, dest_ref):
    t = idx_ref.shape[1]
    e_iota = lax.broadcasted_iota(jnp.int32, (N_EXPERTS, t), 0)
    rows = [jnp.sum(jnp.where(e_iota == idx_ref[k:k + 1, :], start_ref[...], 0), axis=0, keepdims=True)
            for k in range(TOP_K)]
    dest = jnp.concatenate(rows, axis=0) + rank_ref[...]
    for k in range(TOP_K):
        for c in range(t // LANES):
            dest_ref[k, c:c + 1, :] = dest[k:k + 1, c * LANES:(c + 1) * LANES]


def _dest_call(idx, rank, pad_start, tile):
    n = idx.shape[1]
    tok = pl.BlockSpec((TOP_K, tile), lambda i: (0, i))
    return pl.pallas_call(
        _dest_kernel,
        grid=(n // tile,),
        in_specs=[tok, tok, pl.BlockSpec((N_EXPERTS, 1), lambda i: (0, 0))],
        out_specs=pl.BlockSpec((TOP_K, tile // LANES, LANES), lambda i: (0, i, 0)),
        out_shape=jax.ShapeDtypeStruct((TOP_K, n // LANES, LANES), jnp.int32),
        compiler_params=pltpu.CompilerParams(dimension_semantics=("arbitrary",)),
    )(idx, rank, pad_start)


def _scatter_rows_sc(rows, dest_flat, total_rows, chunk):
    n, w = rows.shape
    copies = dest_flat.shape[0] // n
    info = plsc.get_sparse_core_info()
    nc = info.num_cores
    per_worker = n // (nc * info.num_subcores)
    assert per_worker * nc * info.num_subcores == n and per_worker % chunk == 0
    mesh = plsc.VectorSubcoreMesh(core_axis_name="c", subcore_axis_name="s")

    @functools.partial(
        pl.kernel, mesh=mesh,
        out_type=jax.ShapeDtypeStruct((total_rows, w), rows.dtype),
        scratch_types=([pltpu.VMEM((chunk,), jnp.int32) for _ in range(copies)]
                       + [pltpu.VMEM((chunk, w), rows.dtype), pltpu.SemaphoreType.DMA, pltpu.SemaphoreType.DMA]),
    )
    def scatter(rows_hbm, idx_hbm, out_hbm, *scratch):
        idx_vs, (rows_v, isem, sem) = scratch[:copies], scratch[copies:]
        worker = lax.axis_index("s") * nc + lax.axis_index("c")

        @pl.loop(0, per_worker // chunk)
        def _(j):
            base = worker * per_worker + j * chunk
            loads = [pltpu.async_copy(idx_hbm.at[pl.ds(k * n + base, chunk)], idx_vs[k], isem)
                     for k in range(copies)]
            pltpu.sync_copy(rows_hbm.at[pl.ds(base, chunk)], rows_v)
            for ld in loads:
                ld.wait()
            stores = [pltpu.async_copy(rows_v, out_hbm.at[idx_vs[k]], sem) for k in range(copies)]
            for st in stores:
                st.wait()

    return scatter(rows, dest_flat)


EXPERT_LIVE_STEP = 128
X_RING = 4


def _expert_kernel(plan_ref, xs_hbm, wg_hbm, wu_hbm, wd_hbm, y_ref,
                   wgb_ref, wub_ref, wdb_ref, xbuf_ref, xsem, wg_raw, wu_raw, wd_raw, wsem):
    i = pl.program_id(0)
    rows, half = xbuf_ref.shape[1], xbuf_ref.shape[2]
    nused = plan_ref[PLAN_NUSED, 0]

    def w_copies(expert, slot):
        return [pltpu.make_async_copy(src.at[expert], dst.at[slot], wsem.at[slot])
                for src, dst in ((wg_hbm, wg_raw), (wu_hbm, wu_raw), (wd_hbm, wd_raw))]

    def x_copy(block):
        first = pl.multiple_of(block * rows, rows)
        slot = block % X_RING
        return pltpu.make_async_copy(xs_hbm.at[pl.ds(first, rows)], xbuf_ref.at[slot], xsem.at[slot])

    @pl.when(i == 0)
    def _():
        for b in range(X_RING - 1):
            @pl.when(b < nused)
            def _():
                x_copy(b).start()

    @pl.when(i + (X_RING - 1) < nused)
    def _():
        x_copy(i + (X_RING - 1)).start()

    @pl.when(i == 0)
    def _():
        for c in w_copies(plan_ref[PLAN_EXPERT, 0], 0):
            c.start()

    @pl.when(plan_ref[PLAN_FRESH, i] > 0)
    def _():
        slot = plan_ref[PLAN_SEG, i] % 2
        for c in w_copies(plan_ref[PLAN_EXPERT, i], slot):
            c.wait()

        @pl.when(plan_ref[PLAN_NEXT, i] >= 0)
        def _():
            for c in w_copies(plan_ref[PLAN_NEXT, i], 1 - slot):
                c.start()

        wgb_ref[...] = wg_raw[slot].astype(BF16)
        wub_ref[...] = wu_raw[slot].astype(BF16)
        wdb_ref[...] = wd_raw[slot].astype(BF16)

    @pl.when(i < nused)
    def _():
        x_copy(i).wait()

    real = plan_ref[PLAN_VALID, i]
    slot = i % X_RING

    def ffn(live):
        xw = xbuf_ref[slot, 0:live, :]
        row = lax.broadcasted_iota(jnp.int32, xw.shape, 0)
        x_lo, x_hi = _unpack_bf16_pair(jnp.where(row < real, xw, jnp.uint32(0)))
        x_lo = x_lo.astype(BF16)
        x_hi = x_hi.astype(BF16)
        gate = (jnp.dot(x_lo, wgb_ref[:half, :], preferred_element_type=F32)
                + jnp.dot(x_hi, wgb_ref[half:, :], preferred_element_type=F32))
        up = (jnp.dot(x_lo, wub_ref[:half, :], preferred_element_type=F32)
              + jnp.dot(x_hi, wub_ref[half:, :], preferred_element_type=F32))
        u = (_silu(gate) * up).astype(BF16)
        y_lo = jnp.dot(u, wdb_ref[:, :half], preferred_element_type=F32)
        y_hi = jnp.dot(u, wdb_ref[:, half:], preferred_element_type=F32)
        y_ref[0:live, :] = _pack_bf16_pair(y_lo, y_hi)
        if live < rows:
            y_ref[live:, :] = jnp.zeros((rows - live, half), y_ref.dtype)

    pieces = (real + (EXPERT_LIVE_STEP - 1)) // EXPERT_LIVE_STEP
    for p in range(1, rows // EXPERT_LIVE_STEP + 1):
        @pl.when(jnp.logical_and(i < nused, pieces == p))
        def _():
            ffn(p * EXPERT_LIVE_STEP)

    @pl.when(i >= nused)
    def _():
        y_ref[...] = jnp.zeros_like(y_ref)


def _expert_call(plan, xs, wg, wu, wd, rows):
    p, w = xs.shape
    d, f = wg.shape[1], wg.shape[2]
    hbm = pl.BlockSpec(memory_space=pl.ANY)
    grid_spec = pltpu.PrefetchScalarGridSpec(
        num_scalar_prefetch=1,
        grid=(p // rows,),
        in_specs=[hbm, hbm, hbm, hbm],
        out_specs=pl.BlockSpec((rows, w), lambda i, *_: (i, 0)),
        scratch_shapes=[pltpu.VMEM((d, f), BF16), pltpu.VMEM((d, f), BF16), pltpu.VMEM((f, d), BF16),
                        pltpu.VMEM((X_RING, rows, w), xs.dtype), pltpu.SemaphoreType.DMA((X_RING,)),
                        pltpu.VMEM((2, d, f), wg.dtype), pltpu.VMEM((2, d, f), wu.dtype),
                        pltpu.VMEM((2, f, d), wd.dtype), pltpu.SemaphoreType.DMA((2,))],
    )
    return pl.pallas_call(
        _expert_kernel,
        grid_spec=grid_spec,
        out_shape=jax.ShapeDtypeStruct((p, w), jnp.uint32),
        compiler_params=pltpu.CompilerParams(dimension_semantics=("arbitrary",),
                                             vmem_limit_bytes=VMEM_LIMIT),
    )(plan, xs, wg, wu, wd)


def _gather_rows_sc(table, idx, chunk):
    m = idx.shape[0]
    w = table.shape[1]
    info = plsc.get_sparse_core_info()
    nc = info.num_cores
    per_worker = m // (nc * info.num_subcores)
    assert per_worker * nc * info.num_subcores == m and per_worker % chunk == 0
    part = chunk // SC_GATHER_PARTS
    mesh = plsc.VectorSubcoreMesh(core_axis_name="c", subcore_axis_name="s")

    @functools.partial(
        pl.kernel, mesh=mesh,
        out_type=jax.ShapeDtypeStruct((m, w), table.dtype),
        scratch_types=([pltpu.VMEM((part,), jnp.int32) for _ in range(SC_GATHER_PARTS)]
                       + [pltpu.VMEM((part, w), table.dtype) for _ in range(SC_GATHER_PARTS)]
                       + [pltpu.SemaphoreType.DMA] * 3),
    )
    def gather(table_hbm, idx_hbm, out_hbm, *scratch):
        idx_vs = scratch[:SC_GATHER_PARTS]
        rows_vs = scratch[SC_GATHER_PARTS:2 * SC_GATHER_PARTS]
        isem, gsem, wsem = scratch[2 * SC_GATHER_PARTS:]
        worker = lax.axis_index("s") * nc + lax.axis_index("c")

        @pl.loop(0, per_worker // chunk)
        def _(j):
            base = worker * per_worker + j * chunk
            loads = [pltpu.async_copy(idx_hbm.at[pl.ds(base + p * part, part)], idx_vs[p], isem)
                     for p in range(SC_GATHER_PARTS)]
            gathers = []
            for p in range(SC_GATHER_PARTS):
                loads[p].wait()
                gathers.append(pltpu.async_copy(table_hbm.at[idx_vs[p]], rows_vs[p], gsem))
            writes = []
            for p in range(SC_GATHER_PARTS):
                gathers[p].wait()
                writes.append(pltpu.async_copy(rows_vs[p], out_hbm.at[pl.ds(base + p * part, part)], wsem))
            for wr in writes:
                wr.wait()

    return gather(table, idx)


def _combine_kernel(yk_ref, gate_ref, base_ref, g2_ref, fg_ref, o_ref):
    t = base_ref.shape[0]
    half = yk_ref.shape[2]
    gates = jnp.concatenate([gate_ref[...], jnp.zeros((LANES - TOP_K, t), F32)], axis=0).T
    r_lo = jnp.zeros((t, half), F32)
    r_hi = jnp.zeros((t, half), F32)
    for k in range(TOP_K):
        y_lo, y_hi = _unpack_bf16_pair(yk_ref[k])
        r_lo = r_lo + gates[:, k:k + 1] * y_lo
        r_hi = r_hi + gates[:, k:k + 1] * y_hi
    g2 = g2_ref[0]
    x_lo = base_ref[:, :half] + g2[:, :half] * r_lo
    x_hi = base_ref[:, half:] + g2[:, half:] * r_hi
    ms = (jnp.sum(x_lo * x_lo, axis=-1, keepdims=True)
          + jnp.sum(x_hi * x_hi, axis=-1, keepdims=True)) * (1.0 / (2 * half))
    inv = lax.rsqrt(ms + EPS)
    o_ref[:, :half] = x_lo * inv * fg_ref[:, :half]
    o_ref[:, half:] = x_hi * inv * fg_ref[:, half:]


def _combine_call(yk, gates, base, g2, fg, tiles_per_batch, tile, tile0):
    n, d = base.shape
    row = lambda w: pl.BlockSpec((tile, w), lambda i: (i + tile0, 0))
    return pl.pallas_call(
        _combine_kernel,
        grid=(yk.shape[1] // tile,),
        in_specs=[pl.BlockSpec((TOP_K, tile, yk.shape[2]), lambda i: (0, i, 0)),
                  pl.BlockSpec((TOP_K, tile), lambda i: (0, i + tile0)), row(d),
                  pl.BlockSpec((1, 1, d), lambda i: ((i + tile0) // tiles_per_batch, 0, 0)),
                  pl.BlockSpec((1, d), lambda i: (0, 0))],
        out_specs=row(d),
        out_shape=jax.ShapeDtypeStruct((n, d), F32),
        input_output_aliases={2: 0},
        compiler_params=pltpu.CompilerParams(dimension_semantics=("arbitrary",),
                                             vmem_limit_bytes=VMEM_LIMIT),
    )(yk, gates, base, g2, fg)


def _pad_cols(a, width):
    return jnp.pad(a, ((0, 0), (0, width - a.shape[1])))


def _layer(x, mod, norm1_g, norm2_g, w_in, conv_w, conv_b, dt_bias, a_log, d_skip, ssm_norm_g,
           att_norm_g, sinks, rel_bias, w_out, router_w, router_bias, exp_w_gate, exp_w_up, exp_w_down,
           sh_w_gate, sh_w_up, sh_w_down, final_g):
    bsz, l, d = x.shape
    n = bsz * l
    tm = min(ROW_TILE, l)

    sh1, sc1, g1, sh2, sc2, g2 = [m[:, None, :] for m in jnp.split(mod, 6, axis=-1)]

    assert math.frexp(ATT_HEAD_DIM ** -0.5)[0] == 0.5
    x2 = x.reshape(n, d)
    z, xbc, dt, q, k, v = _in_proj_call(x2, sc1, sh1, norm1_g[None, :], jnp.swapaxes(w_in, 1, 2), l // tm, tm)

    triu = jnp.asarray(np.triu(np.ones((CHUNK, CHUNK), np.float32))).astype(BF16)
    shift = jnp.asarray(_conv_shift_matrix()).astype(BF16)
    y_ssm = _ssd_call(xbc.reshape(bsz, l, CONV_CH), z.reshape(bsz, l, D_SSM), dt.reshape(bsz, l, LANES),
                      conv_w, conv_b[None, :], dt_bias[:, None], a_log[:, None],
                      jnp.repeat(d_skip, SSM_HEAD_DIM)[None, :], ssm_norm_g[None, :], triu, shift)

    bias = _bias_call(rel_bias, jnp.asarray(_rel_bucket_table()))
    y_att = _attn_call(sinks, q.reshape(bsz, l, D_ATT), k.reshape(bsz, l, D_KV), v.reshape(bsz, l, D_KV), bias,
                       att_norm_g[None, :])

    rw = _pad_cols(router_w, LANES)
    rwh = rw.astype(BF16)
    rwl = jnp.concatenate([rwh, (rw - rwh.astype(F32)).astype(BF16)], axis=1)
    rs = min(RANK_SUB, tm)
    upper = jnp.asarray(np.triu(np.ones((rs, rs), np.float32), 1)).astype(BF16)
    base, h2, idx, gates, rank, counts = _out_proj_call(
        x2, y_ssm.reshape(n, D_SSM), y_att.reshape(n, D_ATT), g1, sc2, sh2, g2, norm2_g[None, :],
        w_out, sh_w_gate, sh_w_up, sh_w_down, rwh, rwl, router_bias[:, None], upper, l // tm, tm)
    rt = min(ROUTE_TILE, n)

    nblocks = (n * TOP_K + N_EXPERTS * (EXPERT_ROWS - 1) + EXPERT_ROWS - 1) // EXPERT_ROWS
    pad_start, plan = _plan_call(counts, nblocks)
    dest = _dest_call(idx, rank, pad_start, rt)

    xs = _scatter_rows_sc(h2, dest.reshape(-1), nblocks * EXPERT_ROWS, SC_CHUNK)
    ys = _expert_call(plan, xs, exp_w_gate, exp_w_up, exp_w_down, EXPERT_ROWS)
    ctile = min(COMBINE_TILE, l)
    groups = COMBINE_GROUPS if bsz % COMBINE_GROUPS == 0 else 1
    ng = n // groups
    out = base
    for g in range(groups):
        idx_g = dest[:, g * ng // LANES:(g + 1) * ng // LANES, :].reshape(-1)
        yk = _gather_rows_sc(ys, idx_g, SC_CHUNK).reshape(TOP_K, ng, ys.shape[1])
        out = _combine_call(yk, gates, out, g2, final_g[None, :], l // ctile, ctile, g * ng // ctile)
    return out.reshape(bsz, l, d)


def kernel(x, c, mod_w, mod_b, norm1_g, norm2_g, w_in, conv_w, conv_b, dt_bias, a_log, d_skip, ssm_norm_g,
           att_norm_g, sinks, rel_bias, w_out, router_w, router_bias, exp_w_gate, exp_w_up, exp_w_down,
           sh_w_gate, sh_w_up, sh_w_down, final_g):
    assert mod_w.shape[0] == 1, "single-layer block"
    bsz = x.shape[0]
    c_pad = jnp.pad(c, ((0, SUBLANES - bsz % SUBLANES if bsz % SUBLANES else 0), (0, 0)))
    mod = _mod_call(c_pad, mod_w[0], mod_b[0][None, :])[:bsz]
    return _layer(x, mod, norm1_g[0], norm2_g[0], w_in, conv_w[0], conv_b[0], dt_bias[0], a_log[0], d_skip[0],
                  ssm_norm_g[0], att_norm_g[0], sinks[0], rel_bias, w_out, router_w[0], router_bias[0],
                  exp_w_gate[0], exp_w_up[0], exp_w_down[0], sh_w_gate, sh_w_up, sh_w_down, final_g)
```

```python
import functools
import math

import numpy as np
import jax
import jax.numpy as jnp
from jax import lax
from jax.experimental import pallas as pl
from jax.experimental.pallas import tpu as pltpu
from jax.experimental.pallas import tpu_sc as plsc

F32 = jnp.float32
BF16 = jnp.bfloat16

D_MODEL = 1024
SSM_HEAD_DIM = 64
D_SSM = D_MODEL
SSM_HEADS = D_SSM // SSM_HEAD_DIM
SSM_GROUPS = 4
D_STATE = 128
CONV_K = 4
CONV_CH = D_SSM + 2 * SSM_GROUPS * D_STATE
CHUNK = 128
ATT_HEAD_DIM = 64
D_ATT = D_MODEL
ATT_HEADS = D_ATT // ATT_HEAD_DIM
KV_HEADS = ATT_HEADS // 4
Q_PER_KV = ATT_HEADS // KV_HEADS
D_KV = KV_HEADS * ATT_HEAD_DIM
WINDOW = 128
ATT_BLOCK = 128
REL_BUCKETS = 32
REL_MAX_DIST = 128
N_EXPERTS = 64
TOP_K = 8
EXPERT_DIM = D_MODEL // 4
SHARED_DIM = D_MODEL // 4
ROUTE_GROUPS = 8
ROUTE_TOPK_GROUPS = 4
ROUTED_SCALE = 2.5
EPS = 1e-6

LANES = 128
SUBLANES = 8
HALF = LANES // 2

ROW_TILE = 512
ROUTE_TILE = 2048
RANK_SUB = 512
COMBINE_TILE = 512
SC_CHUNK = 128
SC_GATHER_PARTS = 4
EXPERT_ROWS = 1024
COMBINE_GROUPS = 2
VMEM_LIMIT = 48 * 1024 * 1024

NEG_INF = float("-inf")


def _silu(v):
    return v * (1.0 / (1.0 + jnp.exp(-v)))


def _softplus(v):
    return jnp.maximum(v, 0.0) + jnp.log(1.0 + jnp.exp(-jnp.abs(v)))


def _split_hi_lo(v):
    hi = v.astype(BF16)
    lo = (v - hi.astype(F32)).astype(BF16)
    return hi, lo


def _pack_bf16_pair(a, b):
    w = pltpu.pack_elementwise([a, b], packed_dtype=BF16)
    return w if w.dtype == jnp.uint32 else lax.bitcast_convert_type(w, jnp.uint32)


def _unpack_bf16_pair(w):
    a = pltpu.unpack_elementwise(w, index=0, packed_dtype=BF16, unpacked_dtype=F32)
    b = pltpu.unpack_elementwise(w, index=1, packed_dtype=BF16, unpacked_dtype=F32)
    return a, b


def _lane_half_mask(shape):
    return lax.broadcasted_iota(jnp.int32, shape, len(shape) - 1) < HALF


def _mod_kernel(c_ref, w_ref, b_ref, o_ref):
    a = _silu(c_ref[...])
    o_ref[...] = jnp.dot(a, w_ref[...], precision=lax.Precision.HIGHEST,
                         preferred_element_type=F32) + b_ref[...]


def _mod_call(c_pad, mod_w, mod_b):
    rows, d = c_pad.shape
    cols = mod_w.shape[1]
    return pl.pallas_call(
        _mod_kernel,
        grid=(cols // d,),
        in_specs=[pl.BlockSpec((rows, d), lambda j: (0, 0)),
                  pl.BlockSpec((d, d), lambda j: (0, j)),
                  pl.BlockSpec((1, d), lambda j: (0, j))],
        out_specs=pl.BlockSpec((rows, d), lambda j: (0, j)),
        out_shape=jax.ShapeDtypeStruct((rows, cols), F32),
        compiler_params=pltpu.CompilerParams(dimension_semantics=("arbitrary",),
                                             vmem_limit_bytes=VMEM_LIMIT),
    )(c_pad, mod_w, mod_b)


IN_PROJ_SEGMENTS = ((D_SSM, BF16), (CONV_CH, BF16), (LANES, F32), (D_ATT, BF16), (D_KV, BF16), (D_KV, BF16))


def _in_proj_kernel(x_ref, sc_ref, sh_ref, g_ref, w_hbm, *refs):
    out_refs = refs[:len(IN_PROJ_SEGMENTS)]
    wraw_ref, w_ref, wsem = refs[len(IN_PROJ_SEGMENTS):]

    @pl.when(pl.program_id(0) == 0)
    def _():
        cp = pltpu.make_async_copy(w_hbm.at[0], wraw_ref, wsem)
        cp.start()
        cp.wait()
        src_dt = D_SSM + CONV_CH
        dst_q = src_dt + LANES
        src_q = src_dt + SSM_HEADS
        q_scale = ATT_HEAD_DIM ** -0.5

        def put(dst, src, scale=None):
            t = wraw_ref[src:src + LANES, :].T
            w_ref[:, dst:dst + LANES] = (t if scale is None else t * scale).astype(BF16)

        for c0 in range(0, src_dt, LANES):
            put(c0, c0)
        dt_tile = wraw_ref[src_dt:src_dt + LANES, :].T
        lane = lax.broadcasted_iota(jnp.int32, dt_tile.shape, 1)
        w_ref[:, src_dt:dst_q] = jnp.where(lane < SSM_HEADS, dt_tile, 0.0).astype(BF16)
        for c0 in range(0, D_ATT, LANES):
            put(dst_q + c0, src_q + c0, q_scale)
        for c0 in range(D_ATT, D_ATT + 2 * D_KV, LANES):
            put(dst_q + c0, src_q + c0)

    xf = x_ref[...]
    ms = jnp.mean(xf * xf, axis=-1, keepdims=True)
    h = xf * lax.rsqrt(ms + EPS) * g_ref[...]
    h = h * (1.0 + sc_ref[0]) + sh_ref[0]
    hb = h.astype(BF16)
    col = 0
    for (width, dtype), o_ref in zip(IN_PROJ_SEGMENTS, out_refs):
        o_ref[...] = jnp.dot(hb, w_ref[:, col:col + width], preferred_element_type=F32).astype(dtype)
        col += width


def _in_proj_call(x2, sc1, sh1, g1n, w_in_t, tiles_per_batch, tm):
    n, d = x2.shape
    row = lambda w: pl.BlockSpec((tm, w), lambda i: (i, 0))
    full = lambda a: pl.BlockSpec(a.shape, lambda i: (0, 0))
    per_batch = pl.BlockSpec((1, 1, d), lambda i: (i // tiles_per_batch, 0, 0))
    cols = sum(w for w, _ in IN_PROJ_SEGMENTS)
    w_in = w_in_t
    assert w_in.shape == (1, cols - (LANES - SSM_HEADS), d)
    return pl.pallas_call(
        _in_proj_kernel,
        grid=(n // tm,),
        in_specs=[row(d), per_batch, per_batch, full(g1n), pl.BlockSpec(memory_space=pl.ANY)],
        out_specs=[row(w) for w, _ in IN_PROJ_SEGMENTS],
        out_shape=[jax.ShapeDtypeStruct((n, w), dt) for w, dt in IN_PROJ_SEGMENTS],
        scratch_shapes=[pltpu.VMEM(w_in.shape[1:], w_in.dtype), pltpu.VMEM((d, cols), BF16), pltpu.SemaphoreType.DMA(())],
        compiler_params=pltpu.CompilerParams(dimension_semantics=("arbitrary",),
                                             vmem_limit_bytes=VMEM_LIMIT),
    )(x2, sc1, sh1, g1n, w_in)


SSD_SEQS = 4
CONV_HALO = 16


def _conv_shift_matrix():
    s = np.zeros((CONV_K * CHUNK, CONV_HALO + CHUNK), np.float32)
    for k in range(CONV_K):
        t = np.arange(CHUNK)
        s[k * CHUNK + t, CONV_HALO + t - (CONV_K - 1) + k] = 1.0
    return s


def _silu_tanh(v):
    hv = 0.5 * v
    return hv + hv * jnp.tanh(hv)


def _ssd_kernel(xbc_ref, z_ref, dt_ref, cw_ref, cb_ref, dtb_ref, alog_ref, dskip_ref, ng_ref, triu_ref, shift_ref,
                wg_ref, wu_ref, y_ref, wgb_ref, wub_ref, state_ref, ucat_ref, ybuf_ref):
    nseq = xbc_ref.shape[0]
    wgb_ref[...] = wg_ref[...].astype(BF16)
    wub_ref[...] = wu_ref[...].astype(BF16)

    @pl.when(pl.program_id(1) == 0)
    def _():
        state_ref[...] = jnp.zeros_like(state_ref)
        ucat_ref[:, 0:CONV_HALO, :] = jnp.zeros((nseq, CONV_HALO, CONV_CH), BF16)

    for q in range(nseq):
        _ssd_chunk(xbc_ref.at[q], z_ref.at[q], dt_ref.at[q], cw_ref, cb_ref, dtb_ref, alog_ref, dskip_ref, ng_ref,
                   triu_ref, shift_ref, y_ref.at[q], state_ref.at[q], ucat_ref.at[q], ybuf_ref.at[q])


def _ssd_chunk(xbc_ref, z_ref, dt_ref, cw_ref, cb_ref, dtb_ref, alog_ref, dskip_ref, ng_ref, triu_ref, shift_ref,
               y_ref, state_ref, ucat_ref, ybuf_ref):
    ucat_ref[CONV_HALO:, :] = xbc_ref[...]
    shifted = jnp.dot(shift_ref[...], ucat_ref[...], preferred_element_type=F32)
    ucat_ref[0:CONV_HALO, :] = ucat_ref[CHUNK:CHUNK + CONV_HALO, :]
    acc = cb_ref[...] + cw_ref[0:1, :] * shifted[0:CHUNK]
    for kk in range(1, CONV_K):
        acc = acc + cw_ref[kk:kk + 1, :] * shifted[kk * CHUNK:(kk + 1) * CHUNK]
    act = _silu_tanh(acc)
    xs = act[:, :D_SSM]
    gn = SSM_GROUPS * D_STATE

    dt_t = _softplus(dt_ref[...].T[0:SSM_HEADS, :] + dtb_ref[...])
    a_t = dt_t * (-jnp.exp(alog_ref[...]))
    a_hi = a_t.astype(BF16)
    a_mid = (a_t - a_hi.astype(F32)).astype(BF16)
    a_lo = (a_t - a_hi.astype(F32) - a_mid.astype(F32)).astype(BF16)
    triu = triu_ref[...]
    cs_t = (jnp.dot(a_hi, triu, preferred_element_type=F32) + jnp.dot(a_mid, triu, preferred_element_type=F32)
            + jnp.dot(a_lo, triu, preferred_element_type=F32))
    cs_end = cs_t[:, CHUNK - 1:CHUNK]
    r_t = cs_t - jnp.log(dt_t)
    w_t = jnp.exp(cs_end - cs_t) * dt_t
    chunk_decay = jnp.exp(cs_end)
    cols = jnp.concatenate([cs_t, jnp.exp(cs_t), jnp.zeros((LANES - 2 * SSM_HEADS, CHUNK), F32)], axis=0).T

    li = lax.broadcasted_iota(jnp.int32, (CHUNK, CHUNK), 0)
    si = lax.broadcasted_iota(jnp.int32, (CHUNK, CHUNK), 1)
    causal = li >= si
    low = _lane_half_mask((CHUNK, LANES))
    low_row = _lane_half_mask((1, LANES))

    heads_per_group = SSM_HEADS // SSM_GROUPS
    for g in range(SSM_GROUPS):
        b_g = act[:, D_SSM + g * D_STATE:D_SSM + (g + 1) * D_STATE]
        c_g = act[:, D_SSM + gn + g * D_STATE:D_SSM + gn + (g + 1) * D_STATE]
        b_gb = b_g.astype(BF16)
        c_gb = c_g.astype(BF16)
        cb = lax.dot_general(c_gb, b_gb, (((1,), (1,)), ((), ())), preferred_element_type=F32)
        b_t = b_g.T
        for jp in range(heads_per_group // 2):
            j = g * (heads_per_group // 2) + jp
            lanes = slice(j * LANES, (j + 1) * LANES)
            xp = xs[:, lanes]
            xpb = xp.astype(BF16)
            ydiag = jnp.zeros((CHUNK, LANES), F32)
            snew = jnp.zeros((D_STATE, LANES), F32)
            for half in range(2):
                h = 2 * j + half
                diff = cols[:, h:h + 1] - r_t[h:h + 1, :]
                m = (cb * jnp.exp(jnp.where(causal, diff, NEG_INF))).astype(BF16)
                keep = low if half == 0 else jnp.logical_not(low)
                xh = jnp.where(keep, xpb, jnp.zeros_like(xpb))
                ydiag = ydiag + jnp.dot(m, xh, preferred_element_type=F32)
                snew = snew + jnp.dot((b_t * w_t[h:h + 1, :]).astype(BF16), xh, preferred_element_type=F32)
            s_in = state_ref[:, lanes]
            yoff = jnp.dot(c_gb, s_in.astype(BF16), preferred_element_type=F32)
            h0 = 2 * j
            e0 = SSM_HEADS + h0
            escale = jnp.where(low, cols[:, e0:e0 + 1], cols[:, e0 + 1:e0 + 2])
            cdec = jnp.where(low_row, chunk_decay[h0:h0 + 1, :], chunk_decay[h0 + 1:h0 + 2, :])
            ybuf_ref[:, lanes] = ydiag + yoff * escale + xp * dskip_ref[:, lanes]
            state_ref[:, lanes] = s_in * cdec + snew

    yz = ybuf_ref[...] * _silu_tanh(z_ref[...].astype(F32))
    gw = D_SSM // SSM_GROUPS
    for g in range(SSM_GROUPS):
        part = yz[:, g * gw:(g + 1) * gw]
        ms = jnp.mean(part * part, axis=-1, keepdims=True)
        y_ref[:, g * gw:(g + 1) * gw] = (part * lax.rsqrt(ms + EPS)
                                            * ng_ref[:, g * gw:(g + 1) * gw]).astype(BF16)


def _side_cast_specs(weights, steps, step_of):
    specs, shapes = [], []
    for w in weights:
        per = w.shape[0] // steps
        assert per * steps == w.shape[0], "expert count must divide over the host kernel's grid"
        specs.append(pl.BlockSpec((per,) + w.shape[1:], lambda *g, _s=step_of: (_s(*g), 0, 0)))
        shapes.append(jax.ShapeDtypeStruct(w.shape, BF16))
    return specs, shapes


def _ssd_call(xbc, z, dt, conv_w, conv_b, dtb, alog, dskip, ng, triu, shift, wg, wu):
    bsz, l, _ = xbc.shape
    nc = l // CHUNK
    nseq = SSD_SEQS if bsz % SSD_SEQS == 0 else 1
    chunk = lambda w: pl.BlockSpec((nseq, CHUNK, w), lambda b, c: (b, c, 0))
    full = lambda a: pl.BlockSpec(a.shape, lambda b, c: (0, 0))
    side_specs, side_shapes = _side_cast_specs((wg, wu), (bsz // nseq) * nc, lambda b, c: b * nc + c)
    return pl.pallas_call(
        _ssd_kernel,
        grid=(bsz // nseq, nc),
        in_specs=[chunk(CONV_CH), chunk(D_SSM), chunk(LANES), full(conv_w), full(conv_b), full(dtb),
                  full(alog), full(dskip), full(ng), full(triu), full(shift)] + side_specs,
        out_specs=[chunk(D_SSM)] + side_specs,
        out_shape=[jax.ShapeDtypeStruct((bsz, l, D_SSM), BF16)] + side_shapes,
        scratch_shapes=[pltpu.VMEM((nseq, D_STATE, D_SSM), F32),
                        pltpu.VMEM((nseq, CONV_HALO + CHUNK, CONV_CH), BF16),
                        pltpu.VMEM((nseq, CHUNK, D_SSM), F32)],
        compiler_params=pltpu.CompilerParams(dimension_semantics=("arbitrary", "arbitrary"),
                                             vmem_limit_bytes=VMEM_LIMIT),
    )(xbc, z, dt, conv_w, conv_b, dtb, alog, dskip, ng, triu, shift, wg, wu)


assert WINDOW == ATT_BLOCK


def _rel_bucket_table():
    qi = np.arange(ATT_BLOCK)[:, None]
    c = np.arange(ATT_BLOCK)[None, :]
    dist = np.where(c > qi, qi + ATT_BLOCK - c, qi - c)
    max_exact = REL_BUCKETS // 2
    d = np.maximum(dist, 1).astype(np.float32)
    large = max_exact + (np.log(d / np.float32(max_exact)) / np.float32(math.log(REL_MAX_DIST / max_exact))
                         * np.float32(REL_BUCKETS - max_exact)).astype(np.int32)
    large = np.minimum(large, REL_BUCKETS - 1)
    return np.where(dist < max_exact, dist, large).astype(np.int32)


def _bias_kernel(rb_ref, bucket_ref, o_ref):
    bucket = bucket_ref[...]
    from_prev = (lax.broadcasted_iota(jnp.int32, bucket.shape, 1)
                 > lax.broadcasted_iota(jnp.int32, bucket.shape, 0))
    for h in range(ATT_HEADS):
        acc = jnp.zeros(bucket.shape, F32)
        for b in range(REL_BUCKETS):
            acc = jnp.where(bucket == b, rb_ref[b, h], acc)
        o_ref[1, h] = acc
        o_ref[0, h] = jnp.where(from_prev, NEG_INF, acc)


def _bias_call(rel_bias, bucket):
    return pl.pallas_call(
        _bias_kernel,
        in_specs=[pl.BlockSpec(memory_space=pltpu.SMEM), pl.BlockSpec(memory_space=pltpu.VMEM)],
        out_shape=jax.ShapeDtypeStruct((2, ATT_HEADS) + bucket.shape, F32),
    )(rel_bias, bucket)


ATT_SEQS = 4


def _attn_kernel(sink_ref, q_ref, kp_ref, kc_ref, vp_ref, vc_ref, bias_ref, ng_ref, wd_ref, o_ref, wdb_ref, obuf_ref):
    wdb_ref[...] = wd_ref[...].astype(BF16)
    for s in range(q_ref.shape[0]):
        _attn_block(sink_ref, q_ref.at[s], kp_ref.at[s], kc_ref.at[s], vp_ref.at[s], vc_ref.at[s], bias_ref.at[0],
                    ng_ref, o_ref.at[s], obuf_ref.at[s])


def _attn_block(sink_ref, q_ref, kp_ref, kc_ref, vp_ref, vc_ref, bias_ref, ng_ref, o_ref, obuf_ref):
    qi = lax.broadcasted_iota(jnp.int32, (ATT_BLOCK, ATT_BLOCK), 0)
    ci = lax.broadcasted_iota(jnp.int32, (ATT_BLOCK, ATT_BLOCK), 1)
    from_prev = ci > qi
    low = _lane_half_mask((ATT_BLOCK, LANES))

    def band_variants(prev_ref, cur_ref):
        out = []
        for cpair in range(KV_HEADS // 2):
            lanes = slice(cpair * LANES, (cpair + 1) * LANES)
            t = jnp.concatenate([prev_ref[:, lanes], cur_ref[:, lanes]], axis=0).astype(F32)
            out.append((t.astype(BF16), pltpu.roll(t, HALF, 1).astype(BF16)))
        return out

    k_band = band_variants(kp_ref, kc_ref)
    v_band = band_variants(vp_ref, vc_ref)
    nt = (((1,), (1,)), ((), ()))

    for j in range(ATT_HEADS // 2):
        qp = q_ref[:, j * LANES:(j + 1) * LANES]
        out_pair = jnp.zeros((ATT_BLOCK, LANES), F32)
        for half in range(2):
            h = 2 * j + half
            g = h // Q_PER_KV
            swapped = int((g % 2) != half)
            keep = low if half == 0 else jnp.logical_not(low)
            qh = jnp.where(keep, qp, jnp.zeros_like(qp))
            s_band = lax.dot_general(qh, k_band[g // 2][swapped], nt, preferred_element_type=F32)
            s = jnp.where(from_prev, s_band[:, :ATT_BLOCK], s_band[:, ATT_BLOCK:]) + bias_ref[h]
            sink = sink_ref[h]
            m = jnp.maximum(jnp.max(s, axis=-1, keepdims=True), sink)
            p = jnp.exp(s - m)
            denom = jnp.sum(p, axis=-1, keepdims=True) + jnp.exp(sink - m)
            p_band = jnp.concatenate([jnp.where(from_prev, p, 0.0), jnp.where(from_prev, 0.0, p)], axis=1)
            o = jnp.dot(p_band.astype(BF16), v_band[g // 2][swapped], preferred_element_type=F32) / denom
            out_pair = out_pair + jnp.where(keep, o, 0.0)
        obuf_ref[:, j * LANES:(j + 1) * LANES] = out_pair

    att = obuf_ref[...]
    ms = jnp.mean(att * att, axis=-1, keepdims=True)
    o_ref[...] = (att * lax.rsqrt(ms + EPS) * ng_ref[...]).astype(BF16)


def _attn_call(sinks, q, k, v, bias, ng, wd):
    bsz, l, _ = q.shape
    nb = l // ATT_BLOCK
    nseq = ATT_SEQS if bsz % ATT_SEQS == 0 else 1
    cur = lambda w: pl.BlockSpec((nseq, ATT_BLOCK, w), lambda b, i: (b, i, 0))
    prev = lambda w: pl.BlockSpec((nseq, ATT_BLOCK, w), lambda b, i: (b, jnp.maximum(i - 1, 0), 0))
    side_specs, side_shapes = _side_cast_specs((wd,), (bsz // nseq) * nb, lambda b, i: b * nb + i)
    return pl.pallas_call(
        _attn_kernel,
        grid=(bsz // nseq, nb),
        in_specs=[pl.BlockSpec(memory_space=pltpu.SMEM),
                  cur(D_ATT), prev(D_KV), cur(D_KV), prev(D_KV), cur(D_KV),
                  pl.BlockSpec((1,) + bias.shape[1:], lambda b, i: (jnp.minimum(i, 1), 0, 0, 0)),
                  pl.BlockSpec(ng.shape, lambda b, i: (0, 0))] + side_specs,
        out_specs=[cur(D_ATT)] + side_specs,
        out_shape=[jax.ShapeDtypeStruct((bsz, l, D_ATT), BF16)] + side_shapes,
        scratch_shapes=[pltpu.VMEM((nseq, ATT_BLOCK, D_ATT), F32)],
        compiler_params=pltpu.CompilerParams(dimension_semantics=("arbitrary", "arbitrary"),
                                             vmem_limit_bytes=VMEM_LIMIT),
    )(sinks, q, k, k, v, v, bias, ng, wd)


def _out_proj_kernel(x_ref, ys_ref, ya_ref, g1_ref, sc_ref, sh_ref, g2_ref, ng_ref, wo_hbm, sg_hbm, su_hbm, sd_hbm,
                     rwh_ref, rwl_ref, rb_ref, upper_ref,
                     base_ref, h_ref, idx_ref, gate_ref, rank_ref, cnt_ref,
                     carry_ref, wo_raw, sg_raw, su_raw, sd_raw, wo_ref, sg_ref, su_ref, sd_ref, wsem):
    @pl.when(pl.program_id(0) == 0)
    def _():
        carry_ref[...] = jnp.zeros_like(carry_ref)
        staged = ((wo_hbm, wo_raw, wo_ref), (sg_hbm, sg_raw, sg_ref), (su_hbm, su_raw, su_ref),
                  (sd_hbm, sd_raw, sd_ref))
        copies = [pltpu.make_async_copy(src.at[0], raw, wsem.at[j]) for j, (src, raw, _) in enumerate(staged)]
        for cp in copies:
            cp.start()
        for cp, (_, raw, dst) in zip(copies, staged):
            cp.wait()
            dst[...] = raw[...].astype(BF16)

    mix = (jnp.dot(ys_ref[...], wo_ref[:D_SSM, :], preferred_element_type=F32)
           + jnp.dot(ya_ref[...], wo_ref[D_SSM:, :], preferred_element_type=F32))
    x1 = x_ref[...] + g1_ref[0] * mix
    ms = jnp.mean(x1 * x1, axis=-1, keepdims=True)
    h = x1 * lax.rsqrt(ms + EPS) * ng_ref[...]
    h = h * (1.0 + sc_ref[0]) + sh_ref[0]
    half = h.shape[1] // 2
    h_ref[...] = _pack_bf16_pair(h[:, :half], h[:, half:])
    hi, lo = _split_hi_lo(h)
    hi_terms = jnp.dot(hi, rwl_ref[...], preferred_element_type=F32)
    logits = (hi_terms[:, :LANES] + hi_terms[:, LANES:]
              + jnp.dot(lo, rwh_ref[...], preferred_element_type=F32))
    _route_tokens(logits.T[0:N_EXPERTS, :], rb_ref, upper_ref, idx_ref, gate_ref, rank_ref, cnt_ref, carry_ref)
    u = _silu(jnp.dot(hi, sg_ref[...], preferred_element_type=F32)) * jnp.dot(hi, su_ref[...],
                                                                              preferred_element_type=F32)
    shared = jnp.dot(u.astype(BF16), sd_ref[...], preferred_element_type=F32)
    base_ref[...] = x1 + g2_ref[0] * shared


def _out_proj_call(x2, ys, ya, g1, sc2, sh2, g2, ng, w_out, sg, su, sd, rwh, rwl, router_bias, upper,
                   tiles_per_batch, tm):
    n, d = x2.shape
    row = lambda w: pl.BlockSpec((tm, w), lambda i: (i, 0))
    tok = lambda r: pl.BlockSpec((r, tm), lambda i: (0, i))
    full = lambda a: pl.BlockSpec(a.shape, lambda i: (0, 0))
    per_batch = pl.BlockSpec((1, 1, d), lambda i: (i // tiles_per_batch, 0, 0))
    hbm = pl.BlockSpec(memory_space=pl.ANY)
    staged = (w_out, sg, su, sd)
    return pl.pallas_call(
        _out_proj_kernel,
        grid=(n // tm,),
        in_specs=[row(d), row(D_SSM), row(D_ATT), per_batch, per_batch, per_batch, per_batch, full(ng),
                  hbm, hbm, hbm, hbm, full(rwh), full(rwl), full(router_bias), full(upper)],
        out_specs=[row(d), row(d // 2), tok(TOP_K), tok(TOP_K), tok(TOP_K),
                   pl.BlockSpec((N_EXPERTS, LANES), lambda i: (0, 0))],
        out_shape=[jax.ShapeDtypeStruct((n, d), F32), jax.ShapeDtypeStruct((n, d // 2), jnp.uint32),
                   jax.ShapeDtypeStruct((TOP_K, n), jnp.int32), jax.ShapeDtypeStruct((TOP_K, n), F32),
                   jax.ShapeDtypeStruct((TOP_K, n), jnp.int32), jax.ShapeDtypeStruct((N_EXPERTS, LANES), F32)],
        scratch_shapes=([pltpu.VMEM((N_EXPERTS, LANES), F32)]
                        + [pltpu.VMEM(a.shape[1:], a.dtype) for a in staged]
                        + [pltpu.VMEM(a.shape[1:], BF16) for a in staged]
                        + [pltpu.SemaphoreType.DMA((len(staged),))]),
        compiler_params=pltpu.CompilerParams(dimension_semantics=("arbitrary",),
                                             vmem_limit_bytes=VMEM_LIMIT),
    )(x2, ys, ya, g1, sc2, sh2, g2, ng, w_out, sg, su, sd, rwh, rwl, router_bias, upper)


def _route_tokens(logits_t, rb_ref, upper_ref, idx_ref, gate_ref, rank_ref, cnt_ref, carry_ref):
    t = logits_t.shape[1]
    per_group = N_EXPERTS // ROUTE_GROUPS
    scores = 1.0 / (1.0 + jnp.exp(-logits_t))
    sel = scores + rb_ref[...]
    e_iota = lax.broadcasted_iota(jnp.int32, (N_EXPERTS, t), 0)

    sel3 = sel.reshape(ROUTE_GROUPS, per_group, t)
    w_iota = lax.broadcasted_iota(jnp.int32, sel3.shape, 1)
    m1 = jnp.max(sel3, axis=1, keepdims=True)
    first = jnp.min(jnp.where(sel3 == m1, w_iota, per_group), axis=1, keepdims=True)
    m2 = jnp.max(jnp.where(w_iota == first, NEG_INF, sel3), axis=1, keepdims=True)
    grp = (m1 + m2).reshape(ROUTE_GROUPS, t)

    g_iota = lax.broadcasted_iota(jnp.int32, (ROUTE_GROUPS, t), 0)
    gmask = jnp.zeros((ROUTE_GROUPS, t), jnp.bool_)
    for _ in range(ROUTE_TOPK_GROUPS):
        gm = jnp.max(grp, axis=0, keepdims=True)
        gfirst = jnp.min(jnp.where(grp == gm, g_iota, ROUTE_GROUPS), axis=0, keepdims=True)
        hit = g_iota == gfirst
        gmask = jnp.logical_or(gmask, hit)
        grp = jnp.where(hit, NEG_INF, grp)
    allowed = jnp.broadcast_to(gmask.reshape(ROUTE_GROUPS, 1, t),
                               (ROUTE_GROUPS, per_group, t)).reshape(N_EXPERTS, t)
    masked = jnp.where(allowed, sel, NEG_INF)

    picked = jnp.zeros((N_EXPERTS, t), jnp.bool_)
    idx_rows = []
    w_rows = []
    for _ in range(TOP_K):
        mm = jnp.max(masked, axis=0, keepdims=True)
        efirst = jnp.min(jnp.where(masked == mm, e_iota, N_EXPERTS), axis=0, keepdims=True)
        hit = e_iota == efirst
        idx_rows.append(efirst)
        w_rows.append(jnp.sum(jnp.where(hit, scores, 0.0), axis=0, keepdims=True))
        picked = jnp.logical_or(picked, hit)
        masked = jnp.where(hit, NEG_INF, masked)
    idx = jnp.concatenate(idx_rows, axis=0)
    w = jnp.concatenate(w_rows, axis=0)
    gate_ref[...] = w / jnp.sum(w, axis=0, keepdims=True) * ROUTED_SCALE
    idx_ref[...] = idx

    onehot = jnp.where(picked, 1.0, 0.0)
    sub = upper_ref.shape[0]
    carry = carry_ref[:, 0:1]
    parts = []
    for s0 in range(0, t, sub):
        oh = onehot[:, s0:s0 + sub]
        parts.append(jnp.dot(oh.astype(BF16), upper_ref[...], preferred_element_type=F32) + carry)
        carry = carry + jnp.sum(oh, axis=1, keepdims=True)
    rank_full = jnp.concatenate(parts, axis=1)
    rank_rows = [jnp.sum(jnp.where(e_iota == idx_rows[k], rank_full, 0.0), axis=0, keepdims=True)
                 for k in range(TOP_K)]
    rank_ref[...] = jnp.concatenate(rank_rows, axis=0).astype(jnp.int32)
    carry_ref[...] = jnp.broadcast_to(carry, carry_ref.shape)
    cnt_ref[...] = carry_ref[...]


PLAN_EXPERT, PLAN_FRESH, PLAN_VALID, PLAN_SEG, PLAN_NEXT, PLAN_NUSED = range(6)
PLAN_ROWS = SUBLANES


def _plan_kernel(cnt_ref, tri_ref, start_ref, plan_ref, *, nblocks):
    nbp = plan_ref.shape[1]
    cnt = cnt_ref[...].astype(jnp.int32)
    blocks = (cnt + (EXPERT_ROWS - 1)) // EXPERT_ROWS
    end = jnp.dot(tri_ref[...], blocks.astype(F32), precision=lax.Precision.HIGHEST,
                  preferred_element_type=F32).astype(jnp.int32)
    start = end - blocks
    start_ref[...] = start[:, 0:1] * EXPERT_ROWS
    nused = end[N_EXPERTS - 1:N_EXPERTS, 0:1]

    e_iota = lax.broadcasted_iota(jnp.int32, (N_EXPERTS, nbp), 0)
    blk = lax.broadcasted_iota(jnp.int32, (1, nbp), 1)
    expert = jnp.minimum(jnp.sum((end[:, 0:1] <= blk).astype(jnp.int32), axis=0, keepdims=True), N_EXPERTS - 1)
    mine = e_iota == expert
    pick = lambda col: jnp.sum(jnp.where(mine, col, 0), axis=0, keepdims=True)
    first = pick(start[:, 0:1])
    valid = jnp.clip(pick(cnt[:, 0:1]) - (blk - first) * EXPERT_ROWS, 0, EXPERT_ROWS)
    present = jnp.logical_or(blocks[:, 0:1] > 0,
                             jnp.logical_and(e_iota[:, 0:1] == N_EXPERTS - 1, nused < nblocks))
    seg = jnp.sum(jnp.logical_and(present, e_iota <= expert).astype(jnp.int32), axis=0, keepdims=True) - 1
    nxt = jnp.min(jnp.where(jnp.logical_and(present, e_iota > expert), e_iota, N_EXPERTS), axis=0, keepdims=True)
    rows = {PLAN_EXPERT: expert, PLAN_FRESH: (blk == first).astype(jnp.int32), PLAN_VALID: valid, PLAN_SEG: seg,
            PLAN_NEXT: jnp.where(nxt == N_EXPERTS, -1, nxt), PLAN_NUSED: jnp.broadcast_to(nused, (1, nbp))}
    zero = jnp.zeros((1, nbp), jnp.int32)
    plan_ref[...] = jnp.concatenate([rows.get(r, zero) for r in range(PLAN_ROWS)], axis=0)


def _plan_call(counts, nblocks):
    nbp = -(-nblocks // LANES) * LANES
    tri = jnp.asarray(np.tril(np.ones((N_EXPERTS, N_EXPERTS), np.float32)))
    return pl.pallas_call(
        functools.partial(_plan_kernel, nblocks=nblocks),
        out_shape=[jax.ShapeDtypeStruct((N_EXPERTS, 1), jnp.int32),
                   jax.ShapeDtypeStruct((PLAN_ROWS, nbp), jnp.int32)],
    )(counts, tri)


def _dest_kernel(idx_ref, rank_ref, start_ref, dest_ref):
    t = idx_ref.shape[1]
    e_iota = lax.broadcasted_iota(jnp.int32, (N_EXPERTS, t), 0)
    rows = [jnp.sum(jnp.where(e_iota == idx_ref[k:k + 1, :], start_ref[...], 0), axis=0, keepdims=True)
            for k in range(TOP_K)]
    dest = jnp.concatenate(rows, axis=0) + rank_ref[...]
    for k in range(TOP_K):
        for c in range(t // LANES):
            dest_ref[k, c:c + 1, :] = dest[k:k + 1, c * LANES:(c + 1) * LANES]


def _dest_call(idx, rank, pad_start, tile):
    n = idx.shape[1]
    tok = pl.BlockSpec((TOP_K, tile), lambda i: (0, i))
    return pl.pallas_call(
        _dest_kernel,
        grid=(n // tile,),
        in_specs=[tok, tok, pl.BlockSpec((N_EXPERTS, 1), lambda i: (0, 0))],
        out_specs=pl.BlockSpec((TOP_K, tile // LANES, LANES), lambda i: (0, i, 0)),
        out_shape=jax.ShapeDtypeStruct((TOP_K, n // LANES, LANES), jnp.int32),
        compiler_params=pltpu.CompilerParams(dimension_semantics=("arbitrary",)),
    )(idx, rank, pad_start)


def _scatter_rows_sc(rows, dest_flat, total_rows, chunk):
    n, w = rows.shape
    copies = dest_flat.shape[0] // n
    info = plsc.get_sparse_core_info()
    nc = info.num_cores
    per_worker = n // (nc * info.num_subcores)
    assert per_worker * nc * info.num_subcores == n and per_worker % chunk == 0
    mesh = plsc.VectorSubcoreMesh(core_axis_name="c", subcore_axis_name="s")

    @functools.partial(
        pl.kernel, mesh=mesh,
        out_type=jax.ShapeDtypeStruct((total_rows, w), rows.dtype),
        scratch_types=([pltpu.VMEM((chunk,), jnp.int32) for _ in range(copies)]
                       + [pltpu.VMEM((chunk, w), rows.dtype), pltpu.SemaphoreType.DMA, pltpu.SemaphoreType.DMA]),
    )
    def scatter(rows_hbm, idx_hbm, out_hbm, *scratch):
        idx_vs, (rows_v, isem, sem) = scratch[:copies], scratch[copies:]
        worker = lax.axis_index("s") * nc + lax.axis_index("c")

        @pl.loop(0, per_worker // chunk)
        def _(j):
            base = worker * per_worker + j * chunk
            loads = [pltpu.async_copy(idx_hbm.at[pl.ds(k * n + base, chunk)], idx_vs[k], isem)
                     for k in range(copies)]
            pltpu.sync_copy(rows_hbm.at[pl.ds(base, chunk)], rows_v)
            for ld in loads:
                ld.wait()
            stores = [pltpu.async_copy(rows_v, out_hbm.at[idx_vs[k]], sem) for k in range(copies)]
            for st in stores:
                st.wait()

    return scatter(rows, dest_flat)


EXPERT_LIVE_STEP = 256
X_RING = 4


def _expert_kernel(plan_ref, xs_hbm, wg_hbm, wu_hbm, wd_hbm, y_ref, xbuf_ref, xsem, wg_raw, wu_raw, wd_raw, wsem):
    i = pl.program_id(0)
    rows, half = xbuf_ref.shape[1], xbuf_ref.shape[2]
    nused = plan_ref[PLAN_NUSED, 0]

    def w_copies(expert, slot):
        return [pltpu.make_async_copy(src.at[expert], dst.at[slot], wsem.at[slot])
                for src, dst in ((wg_hbm, wg_raw), (wu_hbm, wu_raw), (wd_hbm, wd_raw))]

    def x_copy(block):
        first = pl.multiple_of(block * rows, rows)
        slot = block % X_RING
        return pltpu.make_async_copy(xs_hbm.at[pl.ds(first, rows)], xbuf_ref.at[slot], xsem.at[slot])

    @pl.when(i == 0)
    def _():
        for b in range(X_RING - 1):
            @pl.when(b < nused)
            def _():
                x_copy(b).start()

    @pl.when(i + (X_RING - 1) < nused)
    def _():
        x_copy(i + (X_RING - 1)).start()

    wslot = plan_ref[PLAN_SEG, i] % 2

    @pl.when(i == 0)
    def _():
        for c in w_copies(plan_ref[PLAN_EXPERT, 0], 0):
            c.start()

    @pl.when(plan_ref[PLAN_FRESH, i] > 0)
    def _():
        for c in w_copies(plan_ref[PLAN_EXPERT, i], wslot):
            c.wait()

        @pl.when(plan_ref[PLAN_NEXT, i] >= 0)
        def _():
            for c in w_copies(plan_ref[PLAN_NEXT, i], 1 - wslot):
                c.start()

    @pl.when(i < nused)
    def _():
        x_copy(i).wait()

    real = plan_ref[PLAN_VALID, i]
    slot = i % X_RING

    def ffn(live):
        xw = xbuf_ref[slot, 0:live, :]
        row = lax.broadcasted_iota(jnp.int32, xw.shape, 0)
        x_lo, x_hi = _unpack_bf16_pair(jnp.where(row < real, xw, jnp.uint32(0)))
        x_lo = x_lo.astype(BF16)
        x_hi = x_hi.astype(BF16)
        gate = (jnp.dot(x_lo, wg_raw[wslot, :half, :], preferred_element_type=F32)
                + jnp.dot(x_hi, wg_raw[wslot, half:, :], preferred_element_type=F32))
        up = (jnp.dot(x_lo, wu_raw[wslot, :half, :], preferred_element_type=F32)
              + jnp.dot(x_hi, wu_raw[wslot, half:, :], preferred_element_type=F32))
        u = (_silu(gate) * up).astype(BF16)
        y_lo = jnp.dot(u, wd_raw[wslot, :, :half], preferred_element_type=F32)
        y_hi = jnp.dot(u, wd_raw[wslot, :, half:], preferred_element_type=F32)
        y_ref[0:live, :] = _pack_bf16_pair(y_lo, y_hi)
        if live < rows:
            y_ref[live:, :] = jnp.zeros((rows - live, half), y_ref.dtype)

    pieces = (real + (EXPERT_LIVE_STEP - 1)) // EXPERT_LIVE_STEP
    for p in range(1, rows // EXPERT_LIVE_STEP + 1):
        @pl.when(jnp.logical_and(i < nused, pieces == p))
        def _():
            ffn(p * EXPERT_LIVE_STEP)

    @pl.when(i >= nused)
    def _():
        y_ref[...] = jnp.zeros_like(y_ref)


def _expert_call(plan, xs, wg, wu, wd, rows):
    p, w = xs.shape
    d, f = wg.shape[1], wg.shape[2]
    hbm = pl.BlockSpec(memory_space=pl.ANY)
    grid_spec = pltpu.PrefetchScalarGridSpec(
        num_scalar_prefetch=1,
        grid=(p // rows,),
        in_specs=[hbm, hbm, hbm, hbm],
        out_specs=pl.BlockSpec((rows, w), lambda i, *_: (i, 0)),
        scratch_shapes=[pltpu.VMEM((X_RING, rows, w), xs.dtype), pltpu.SemaphoreType.DMA((X_RING,)),
                        pltpu.VMEM((2, d, f), wg.dtype), pltpu.VMEM((2, d, f), wu.dtype),
                        pltpu.VMEM((2, f, d), wd.dtype), pltpu.SemaphoreType.DMA((2,))],
    )
    return pl.pallas_call(
        _expert_kernel,
        grid_spec=grid_spec,
        out_shape=jax.ShapeDtypeStruct((p, w), jnp.uint32),
        compiler_params=pltpu.CompilerParams(dimension_semantics=("arbitrary",),
                                             vmem_limit_bytes=VMEM_LIMIT),
    )(plan, xs, wg, wu, wd)


def _gather_rows_sc(table, idx, chunk):
    m = idx.shape[0]
    w = table.shape[1]
    info = plsc.get_sparse_core_info()
    nc = info.num_cores
    per_worker = m // (nc * info.num_subcores)
    assert per_worker * nc * info.num_subcores == m and per_worker % chunk == 0
    part = chunk // SC_GATHER_PARTS
    mesh = plsc.VectorSubcoreMesh(core_axis_name="c", subcore_axis_name="s")

    @functools.partial(
        pl.kernel, mesh=mesh,
        out_type=jax.ShapeDtypeStruct((m, w), table.dtype),
        scratch_types=([pltpu.VMEM((part,), jnp.int32) for _ in range(SC_GATHER_PARTS)]
                       + [pltpu.VMEM((part, w), table.dtype) for _ in range(SC_GATHER_PARTS)]
                       + [pltpu.SemaphoreType.DMA] * 3),
    )
    def gather(table_hbm, idx_hbm, out_hbm, *scratch):
        idx_vs = scratch[:SC_GATHER_PARTS]
        rows_vs = scratch[SC_GATHER_PARTS:2 * SC_GATHER_PARTS]
        isem, gsem, wsem = scratch[2 * SC_GATHER_PARTS:]
        worker = lax.axis_index("s") * nc + lax.axis_index("c")

        @pl.loop(0, per_worker // chunk)
        def _(j):
            base = worker * per_worker + j * chunk
            loads = [pltpu.async_copy(idx_hbm.at[pl.ds(base + p * part, part)], idx_vs[p], isem)
                     for p in range(SC_GATHER_PARTS)]
            gathers = []
            for p in range(SC_GATHER_PARTS):
                loads[p].wait()
                gathers.append(pltpu.async_copy(table_hbm.at[idx_vs[p]], rows_vs[p], gsem))
            writes = []
            for p in range(SC_GATHER_PARTS):
                gathers[p].wait()
                writes.append(pltpu.async_copy(rows_vs[p], out_hbm.at[pl.ds(base + p * part, part)], wsem))
            for wr in writes:
                wr.wait()

    return gather(table, idx)


def _combine_kernel(yk_ref, gate_ref, base_ref, g2_ref, fg_ref, o_ref):
    t = base_ref.shape[0]
    half = yk_ref.shape[2]
    gates = jnp.concatenate([gate_ref[...], jnp.zeros((LANES - TOP_K, t), F32)], axis=0).T
    r_lo = jnp.zeros((t, half), F32)
    r_hi = jnp.zeros((t, half), F32)
    for k in range(TOP_K):
        y_lo, y_hi = _unpack_bf16_pair(yk_ref[k])
        r_lo = r_lo + gates[:, k:k + 1] * y_lo
        r_hi = r_hi + gates[:, k:k + 1] * y_hi
    g2 = g2_ref[0]
    x_lo = base_ref[:, :half] + g2[:, :half] * r_lo
    x_hi = base_ref[:, half:] + g2[:, half:] * r_hi
    ms = (jnp.sum(x_lo * x_lo, axis=-1, keepdims=True)
          + jnp.sum(x_hi * x_hi, axis=-1, keepdims=True)) * (1.0 / (2 * half))
    inv = lax.rsqrt(ms + EPS)
    o_ref[:, :half] = x_lo * inv * fg_ref[:, :half]
    o_ref[:, half:] = x_hi * inv * fg_ref[:, half:]


def _combine_call(yk, gates, base, g2, fg, tiles_per_batch, tile, tile0):
    n, d = base.shape
    row = lambda w: pl.BlockSpec((tile, w), lambda i: (i + tile0, 0))
    return pl.pallas_call(
        _combine_kernel,
        grid=(yk.shape[1] // tile,),
        in_specs=[pl.BlockSpec((TOP_K, tile, yk.shape[2]), lambda i: (0, i, 0)),
                  pl.BlockSpec((TOP_K, tile), lambda i: (0, i + tile0)), row(d),
                  pl.BlockSpec((1, 1, d), lambda i: ((i + tile0) // tiles_per_batch, 0, 0)),
                  pl.BlockSpec((1, d), lambda i: (0, 0))],
        out_specs=row(d),
        out_shape=jax.ShapeDtypeStruct((n, d), F32),
        input_output_aliases={2: 0},
        compiler_params=pltpu.CompilerParams(dimension_semantics=("arbitrary",),
                                             vmem_limit_bytes=VMEM_LIMIT),
    )(yk, gates, base, g2, fg)


def _pad_cols(a, width):
    return jnp.pad(a, ((0, 0), (0, width - a.shape[1])))


def _layer(x, mod, norm1_g, norm2_g, w_in, conv_w, conv_b, dt_bias, a_log, d_skip, ssm_norm_g,
           att_norm_g, sinks, rel_bias, w_out, router_w, router_bias, exp_w_gate, exp_w_up, exp_w_down,
           sh_w_gate, sh_w_up, sh_w_down, final_g):
    bsz, l, d = x.shape
    n = bsz * l
    tm = min(ROW_TILE, l)

    sh1, sc1, g1, sh2, sc2, g2 = [m[:, None, :] for m in jnp.split(mod, 6, axis=-1)]

    assert math.frexp(ATT_HEAD_DIM ** -0.5)[0] == 0.5
    x2 = x.reshape(n, d)
    z, xbc, dt, q, k, v = _in_proj_call(x2, sc1, sh1, norm1_g[None, :], jnp.swapaxes(w_in, 1, 2), l // tm, tm)

    triu = jnp.asarray(np.triu(np.ones((CHUNK, CHUNK), np.float32))).astype(BF16)
    shift = jnp.asarray(_conv_shift_matrix()).astype(BF16)
    y_ssm, exp_wg_b, exp_wu_b = _ssd_call(
        xbc.reshape(bsz, l, CONV_CH), z.reshape(bsz, l, D_SSM), dt.reshape(bsz, l, LANES), conv_w, conv_b[None, :],
        dt_bias[:, None], a_log[:, None], jnp.repeat(d_skip, SSM_HEAD_DIM)[None, :], ssm_norm_g[None, :], triu,
        shift, exp_w_gate, exp_w_up)

    bias = _bias_call(rel_bias, jnp.asarray(_rel_bucket_table()))
    y_att, exp_wd_b = _attn_call(sinks, q.reshape(bsz, l, D_ATT), k.reshape(bsz, l, D_KV), v.reshape(bsz, l, D_KV),
                                 bias, att_norm_g[None, :], exp_w_down)

    rw = _pad_cols(router_w, LANES)
    rwh = rw.astype(BF16)
    rwl = jnp.concatenate([rwh, (rw - rwh.astype(F32)).astype(BF16)], axis=1)
    rs = min(RANK_SUB, tm)
    upper = jnp.asarray(np.triu(np.ones((rs, rs), np.float32), 1)).astype(BF16)
    base, h2, idx, gates, rank, counts = _out_proj_call(
        x2, y_ssm.reshape(n, D_SSM), y_att.reshape(n, D_ATT), g1, sc2, sh2, g2, norm2_g[None, :],
        w_out, sh_w_gate, sh_w_up, sh_w_down, rwh, rwl, router_bias[:, None], upper, l // tm, tm)
    rt = min(ROUTE_TILE, n)

    nblocks = (n * TOP_K + N_EXPERTS * (EXPERT_ROWS - 1) + EXPERT_ROWS - 1) // EXPERT_ROWS
    pad_start, plan = _plan_call(counts, nblocks)
    dest = _dest_call(idx, rank, pad_start, rt)

    xs = _scatter_rows_sc(h2, dest.reshape(-1), nblocks * EXPERT_ROWS, SC_CHUNK)
    ys = _expert_call(plan, xs, exp_wg_b, exp_wu_b, exp_wd_b, EXPERT_ROWS)
    ctile = min(COMBINE_TILE, l)
    groups = COMBINE_GROUPS if bsz % COMBINE_GROUPS == 0 else 1
    ng = n // groups
    out = base
    for g in range(groups):
        idx_g = dest[:, g * ng // LANES:(g + 1) * ng // LANES, :].reshape(-1)
        yk = _gather_rows_sc(ys, idx_g, SC_CHUNK).reshape(TOP_K, ng, ys.shape[1])
        out = _combine_call(yk, gates, out, g2, final_g[None, :], l // ctile, ctile, g * ng // ctile)
    return out.reshape(bsz, l, d)


def kernel(x, c, mod_w, mod_b, norm1_g, norm2_g, w_in, conv_w, conv_b, dt_bias, a_log, d_skip, ssm_norm_g,
           att_norm_g, sinks, rel_bias, w_out, router_w, router_bias, exp_w_gate, exp_w_up, exp_w_down,
           sh_w_gate, sh_w_up, sh_w_down, final_g):
    assert mod_w.shape[0] == 1, "single-layer block"
    bsz = x.shape[0]
    c_pad = jnp.pad(c, ((0, SUBLANES - bsz % SUBLANES if bsz % SUBLANES else 0), (0, 0)))
    mod = _mod_call(c_pad, mod_w[0], mod_b[0][None, :])[:bsz]
    return _layer(x, mod, norm1_g[0], norm2_g[0], w_in, conv_w[0], conv_b[0], dt_bias[0], a_log[0], d_skip[0],
                  ssm_norm_g[0], att_norm_g[0], sinks[0], rel_bias, w_out, router_w[0], router_bias[0],
                  exp_w_gate[0], exp_w_up[0], exp_w_down[0], sh_w_gate, sh_w_up, sh_w_down, final_g)
```

```python
import functools
import math

import numpy as np
import jax
import jax.numpy as jnp
from jax import lax
from jax.experimental import pallas as pl
from jax.experimental.pallas import tpu as pltpu
from jax.experimental.pallas import tpu_sc as plsc

F32 = jnp.float32
BF16 = jnp.bfloat16

D_MODEL = 1024
SSM_HEAD_DIM = 64
D_SSM = D_MODEL
SSM_HEADS = D_SSM // SSM_HEAD_DIM
SSM_GROUPS = 4
D_STATE = 128
CONV_K = 4
CONV_CH = D_SSM + 2 * SSM_GROUPS * D_STATE
CHUNK = 128
ATT_HEAD_DIM = 64
D_ATT = D_MODEL
ATT_HEADS = D_ATT // ATT_HEAD_DIM
KV_HEADS = ATT_HEADS // 4
Q_PER_KV = ATT_HEADS // KV_HEADS
D_KV = KV_HEADS * ATT_HEAD_DIM
WINDOW = 128
ATT_BLOCK = 128
REL_BUCKETS = 32
REL_MAX_DIST = 128
N_EXPERTS = 64
TOP_K = 8
EXPERT_DIM = D_MODEL // 4
SHARED_DIM = D_MODEL // 4
ROUTE_GROUPS = 8
ROUTE_TOPK_GROUPS = 4
ROUTED_SCALE = 2.5
EPS = 1e-6

LANES = 128
SUBLANES = 8
HALF = LANES // 2

ROW_TILE = 512
ROUTE_TILE = 2048
RANK_SUB = 512
COMBINE_TILE = 512
SC_CHUNK = 128
SC_GATHER_PARTS = 4
EXPERT_ROWS = 1024
COMBINE_GROUPS = 2
VMEM_LIMIT = 48 * 1024 * 1024

NEG_INF = float("-inf")


def _silu(v):
    return v * (1.0 / (1.0 + jnp.exp(-v)))


def _softplus(v):
    return jnp.maximum(v, 0.0) + jnp.log(1.0 + jnp.exp(-jnp.abs(v)))


def _split_hi_lo(v):
    hi = v.astype(BF16)
    lo = (v - hi.astype(F32)).astype(BF16)
    return hi, lo


def _pack_bf16_pair(a, b):
    w = pltpu.pack_elementwise([a, b], packed_dtype=BF16)
    return w if w.dtype == jnp.uint32 else lax.bitcast_convert_type(w, jnp.uint32)


def _unpack_bf16_pair(w):
    a = pltpu.unpack_elementwise(w, index=0, packed_dtype=BF16, unpacked_dtype=F32)
    b = pltpu.unpack_elementwise(w, index=1, packed_dtype=BF16, unpacked_dtype=F32)
    return a, b


def _lane_half_mask(shape):
    return lax.broadcasted_iota(jnp.int32, shape, len(shape) - 1) < HALF


def _mod_kernel(c_ref, w_ref, b_ref, o_ref):
    a = _silu(c_ref[...])
    o_ref[...] = jnp.dot(a, w_ref[...], precision=lax.Precision.HIGHEST,
                         preferred_element_type=F32) + b_ref[...]


def _mod_call(c_pad, mod_w, mod_b):
    rows, d = c_pad.shape
    cols = mod_w.shape[1]
    return pl.pallas_call(
        _mod_kernel,
        grid=(cols // d,),
        in_specs=[pl.BlockSpec((rows, d), lambda j: (0, 0)),
                  pl.BlockSpec((d, d), lambda j: (0, j)),
                  pl.BlockSpec((1, d), lambda j: (0, j))],
        out_specs=pl.BlockSpec((rows, d), lambda j: (0, j)),
        out_shape=jax.ShapeDtypeStruct((rows, cols), F32),
        compiler_params=pltpu.CompilerParams(dimension_semantics=("arbitrary",),
                                             vmem_limit_bytes=VMEM_LIMIT),
    )(c_pad, mod_w, mod_b)


IN_PROJ_SEGMENTS = ((D_SSM, BF16), (CONV_CH, BF16), (LANES, F32), (D_ATT, BF16), (D_KV, BF16), (D_KV, BF16))


def _in_proj_kernel(x_ref, sc_ref, sh_ref, g_ref, w_hbm, *refs):
    out_refs = refs[:len(IN_PROJ_SEGMENTS)]
    wraw_ref, w_ref, wsem = refs[len(IN_PROJ_SEGMENTS):]

    @pl.when(pl.program_id(0) == 0)
    def _():
        cp = pltpu.make_async_copy(w_hbm.at[0], wraw_ref, wsem)
        cp.start()
        cp.wait()
        src_dt = D_SSM + CONV_CH
        dst_q = src_dt + LANES
        src_q = src_dt + SSM_HEADS
        q_scale = ATT_HEAD_DIM ** -0.5

        def put(dst, src, scale=None):
            t = wraw_ref[src:src + LANES, :].T
            w_ref[:, dst:dst + LANES] = (t if scale is None else t * scale).astype(BF16)

        for c0 in range(0, src_dt, LANES):
            put(c0, c0)
        dt_tile = wraw_ref[src_dt:src_dt + LANES, :].T
        lane = lax.broadcasted_iota(jnp.int32, dt_tile.shape, 1)
        w_ref[:, src_dt:dst_q] = jnp.where(lane < SSM_HEADS, dt_tile, 0.0).astype(BF16)
        for c0 in range(0, D_ATT, LANES):
            put(dst_q + c0, src_q + c0, q_scale)
        for c0 in range(D_ATT, D_ATT + 2 * D_KV, LANES):
            put(dst_q + c0, src_q + c0)

    xf = x_ref[...]
    ms = jnp.mean(xf * xf, axis=-1, keepdims=True)
    h = xf * lax.rsqrt(ms + EPS) * g_ref[...]
    h = h * (1.0 + sc_ref[0]) + sh_ref[0]
    hb = h.astype(BF16)
    col = 0
    for (width, dtype), o_ref in zip(IN_PROJ_SEGMENTS, out_refs):
        o_ref[...] = jnp.dot(hb, w_ref[:, col:col + width], preferred_element_type=F32).astype(dtype)
        col += width


def _in_proj_call(x2, sc1, sh1, g1n, w_in_t, tiles_per_batch, tm):
    n, d = x2.shape
    row = lambda w: pl.BlockSpec((tm, w), lambda i: (i, 0))
    full = lambda a: pl.BlockSpec(a.shape, lambda i: (0, 0))
    per_batch = pl.BlockSpec((1, 1, d), lambda i: (i // tiles_per_batch, 0, 0))
    cols = sum(w for w, _ in IN_PROJ_SEGMENTS)
    w_in = w_in_t
    assert w_in.shape == (1, cols - (LANES - SSM_HEADS), d)
    return pl.pallas_call(
        _in_proj_kernel,
        grid=(n // tm,),
        in_specs=[row(d), per_batch, per_batch, full(g1n), pl.BlockSpec(memory_space=pl.ANY)],
        out_specs=[row(w) for w, _ in IN_PROJ_SEGMENTS],
        out_shape=[jax.ShapeDtypeStruct((n, w), dt) for w, dt in IN_PROJ_SEGMENTS],
        scratch_shapes=[pltpu.VMEM(w_in.shape[1:], w_in.dtype), pltpu.VMEM((d, cols), BF16), pltpu.SemaphoreType.DMA(())],
        compiler_params=pltpu.CompilerParams(dimension_semantics=("arbitrary",),
                                             vmem_limit_bytes=VMEM_LIMIT),
    )(x2, sc1, sh1, g1n, w_in)


SSD_SEQS = 4
CONV_HALO = 16


def _conv_shift_matrix():
    s = np.zeros((CONV_K * CHUNK, CONV_HALO + CHUNK), np.float32)
    for k in range(CONV_K):
        t = np.arange(CHUNK)
        s[k * CHUNK + t, CONV_HALO + t - (CONV_K - 1) + k] = 1.0
    return s


def _silu_tanh(v):
    hv = 0.5 * v
    return hv + hv * jnp.tanh(hv)


def _ssd_kernel(xbc_ref, z_ref, dt_ref, cw_ref, cb_ref, dtb_ref, alog_ref, dskip_ref, ng_ref, triu_ref, shift_ref,
                y_ref, state_ref, ucat_ref, ybuf_ref):
    nseq = xbc_ref.shape[0]

    @pl.when(pl.program_id(1) == 0)
    def _():
        state_ref[...] = jnp.zeros_like(state_ref)
        ucat_ref[:, 0:CONV_HALO, :] = jnp.zeros((nseq, CONV_HALO, CONV_CH), BF16)

    for q in range(nseq):
        _ssd_chunk(xbc_ref.at[q], z_ref.at[q], dt_ref.at[q], cw_ref, cb_ref, dtb_ref, alog_ref, dskip_ref, ng_ref,
                   triu_ref, shift_ref, y_ref.at[q], state_ref.at[q], ucat_ref.at[q], ybuf_ref.at[q])


def _ssd_chunk(xbc_ref, z_ref, dt_ref, cw_ref, cb_ref, dtb_ref, alog_ref, dskip_ref, ng_ref, triu_ref, shift_ref,
               y_ref, state_ref, ucat_ref, ybuf_ref):
    ucat_ref[CONV_HALO:, :] = xbc_ref[...]
    shifted = jnp.dot(shift_ref[...], ucat_ref[...], preferred_element_type=F32)
    ucat_ref[0:CONV_HALO, :] = ucat_ref[CHUNK:CHUNK + CONV_HALO, :]
    acc = cb_ref[...] + cw_ref[0:1, :] * shifted[0:CHUNK]
    for kk in range(1, CONV_K):
        acc = acc + cw_ref[kk:kk + 1, :] * shifted[kk * CHUNK:(kk + 1) * CHUNK]
    act = _silu_tanh(acc)
    xs = act[:, :D_SSM]
    gn = SSM_GROUPS * D_STATE

    dt_t = _softplus(dt_ref[...].T[0:SSM_HEADS, :] + dtb_ref[...])
    a_t = dt_t * (-jnp.exp(alog_ref[...]))
    a_hi = a_t.astype(BF16)
    a_mid = (a_t - a_hi.astype(F32)).astype(BF16)
    a_lo = (a_t - a_hi.astype(F32) - a_mid.astype(F32)).astype(BF16)
    triu = triu_ref[...]
    cs_t = (jnp.dot(a_hi, triu, preferred_element_type=F32) + jnp.dot(a_mid, triu, preferred_element_type=F32)
            + jnp.dot(a_lo, triu, preferred_element_type=F32))
    cs_end = cs_t[:, CHUNK - 1:CHUNK]
    r_t = cs_t - jnp.log(dt_t)
    w_t = jnp.exp(cs_end - cs_t) * dt_t
    chunk_decay = jnp.exp(cs_end)
    cols = jnp.concatenate([cs_t, jnp.exp(cs_t), jnp.zeros((LANES - 2 * SSM_HEADS, CHUNK), F32)], axis=0).T

    li = lax.broadcasted_iota(jnp.int32, (CHUNK, CHUNK), 0)
    si = lax.broadcasted_iota(jnp.int32, (CHUNK, CHUNK), 1)
    causal = li >= si
    low = _lane_half_mask((CHUNK, LANES))
    low_row = _lane_half_mask((1, LANES))

    heads_per_group = SSM_HEADS // SSM_GROUPS
    for g in range(SSM_GROUPS):
        b_g = act[:, D_SSM + g * D_STATE:D_SSM + (g + 1) * D_STATE]
        c_g = act[:, D_SSM + gn + g * D_STATE:D_SSM + gn + (g + 1) * D_STATE]
        b_gb = b_g.astype(BF16)
        c_gb = c_g.astype(BF16)
        cb = lax.dot_general(c_gb, b_gb, (((1,), (1,)), ((), ())), preferred_element_type=F32)
        b_t = b_g.T
        for jp in range(heads_per_group // 2):
            j = g * (heads_per_group // 2) + jp
            lanes = slice(j * LANES, (j + 1) * LANES)
            xp = xs[:, lanes]
            xpb = xp.astype(BF16)
            ydiag = jnp.zeros((CHUNK, LANES), F32)
            snew = jnp.zeros((D_STATE, LANES), F32)
            for half in range(2):
                h = 2 * j + half
                diff = cols[:, h:h + 1] - r_t[h:h + 1, :]
                m = (cb * jnp.exp(jnp.where(causal, diff, NEG_INF))).astype(BF16)
                keep = low if half == 0 else jnp.logical_not(low)
                xh = jnp.where(keep, xpb, jnp.zeros_like(xpb))
                ydiag = ydiag + jnp.dot(m, xh, preferred_element_type=F32)
                snew = snew + jnp.dot((b_t * w_t[h:h + 1, :]).astype(BF16), xh, preferred_element_type=F32)
            s_in = state_ref[:, lanes]
            yoff = jnp.dot(c_gb, s_in.astype(BF16), preferred_element_type=F32)
            h0 = 2 * j
            e0 = SSM_HEADS + h0
            escale = jnp.where(low, cols[:, e0:e0 + 1], cols[:, e0 + 1:e0 + 2])
            cdec = jnp.where(low_row, chunk_decay[h0:h0 + 1, :], chunk_decay[h0 + 1:h0 + 2, :])
            ybuf_ref[:, lanes] = ydiag + yoff * escale + xp * dskip_ref[:, lanes]
            state_ref[:, lanes] = s_in * cdec + snew

    yz = ybuf_ref[...] * _silu_tanh(z_ref[...].astype(F32))
    gw = D_SSM // SSM_GROUPS
    for g in range(SSM_GROUPS):
        part = yz[:, g * gw:(g + 1) * gw]
        ms = jnp.mean(part * part, axis=-1, keepdims=True)
        y_ref[:, g * gw:(g + 1) * gw] = (part * lax.rsqrt(ms + EPS)
                                            * ng_ref[:, g * gw:(g + 1) * gw]).astype(BF16)


def _ssd_call(xbc, z, dt, conv_w, conv_b, dtb, alog, dskip, ng, triu, shift):
    bsz, l, _ = xbc.shape
    nc = l // CHUNK
    nseq = SSD_SEQS if bsz % SSD_SEQS == 0 else 1
    chunk = lambda w: pl.BlockSpec((nseq, CHUNK, w), lambda b, c: (b, c, 0))
    full = lambda a: pl.BlockSpec(a.shape, lambda b, c: (0, 0))
    return pl.pallas_call(
        _ssd_kernel,
        grid=(bsz // nseq, nc),
        in_specs=[chunk(CONV_CH), chunk(D_SSM), chunk(LANES), full(conv_w), full(conv_b), full(dtb),
                  full(alog), full(dskip), full(ng), full(triu), full(shift)],
        out_specs=chunk(D_SSM),
        out_shape=jax.ShapeDtypeStruct((bsz, l, D_SSM), BF16),
        scratch_shapes=[pltpu.VMEM((nseq, D_STATE, D_SSM), F32),
                        pltpu.VMEM((nseq, CONV_HALO + CHUNK, CONV_CH), BF16),
                        pltpu.VMEM((nseq, CHUNK, D_SSM), F32)],
        compiler_params=pltpu.CompilerParams(dimension_semantics=("arbitrary", "arbitrary"),
                                             vmem_limit_bytes=VMEM_LIMIT),
    )(xbc, z, dt, conv_w, conv_b, dtb, alog, dskip, ng, triu, shift)


assert WINDOW == ATT_BLOCK


def _rel_bucket_table():
    qi = np.arange(ATT_BLOCK)[:, None]
    c = np.arange(ATT_BLOCK)[None, :]
    dist = np.where(c > qi, qi + ATT_BLOCK - c, qi - c)
    max_exact = REL_BUCKETS // 2
    d = np.maximum(dist, 1).astype(np.float32)
    large = max_exact + (np.log(d / np.float32(max_exact)) / np.float32(math.log(REL_MAX_DIST / max_exact))
                         * np.float32(REL_BUCKETS - max_exact)).astype(np.int32)
    large = np.minimum(large, REL_BUCKETS - 1)
    return np.where(dist < max_exact, dist, large).astype(np.int32)


def _bias_kernel(rb_ref, bucket_ref, o_ref):
    bucket = bucket_ref[...]
    from_prev = (lax.broadcasted_iota(jnp.int32, bucket.shape, 1)
                 > lax.broadcasted_iota(jnp.int32, bucket.shape, 0))
    for h in range(ATT_HEADS):
        acc = jnp.zeros(bucket.shape, F32)
        for b in range(REL_BUCKETS):
            acc = jnp.where(bucket == b, rb_ref[b, h], acc)
        o_ref[1, h] = acc
        o_ref[0, h] = jnp.where(from_prev, NEG_INF, acc)


def _bias_call(rel_bias, bucket):
    return pl.pallas_call(
        _bias_kernel,
        in_specs=[pl.BlockSpec(memory_space=pltpu.SMEM), pl.BlockSpec(memory_space=pltpu.VMEM)],
        out_shape=jax.ShapeDtypeStruct((2, ATT_HEADS) + bucket.shape, F32),
    )(rel_bias, bucket)


ATT_SEQS = 4


def _attn_kernel(sink_ref, q_ref, kp_ref, kc_ref, vp_ref, vc_ref, bias_ref, ng_ref, o_ref, obuf_ref):
    for s in range(q_ref.shape[0]):
        _attn_block(sink_ref, q_ref.at[s], kp_ref.at[s], kc_ref.at[s], vp_ref.at[s], vc_ref.at[s], bias_ref.at[0],
                    ng_ref, o_ref.at[s], obuf_ref.at[s])


def _attn_block(sink_ref, q_ref, kp_ref, kc_ref, vp_ref, vc_ref, bias_ref, ng_ref, o_ref, obuf_ref):
    qi = lax.broadcasted_iota(jnp.int32, (ATT_BLOCK, ATT_BLOCK), 0)
    ci = lax.broadcasted_iota(jnp.int32, (ATT_BLOCK, ATT_BLOCK), 1)
    from_prev = ci > qi
    low = _lane_half_mask((ATT_BLOCK, LANES))

    def band_variants(prev_ref, cur_ref):
        out = []
        for cpair in range(KV_HEADS // 2):
            lanes = slice(cpair * LANES, (cpair + 1) * LANES)
            t = jnp.concatenate([prev_ref[:, lanes], cur_ref[:, lanes]], axis=0).astype(F32)
            out.append((t.astype(BF16), pltpu.roll(t, HALF, 1).astype(BF16)))
        return out

    k_band = band_variants(kp_ref, kc_ref)
    v_band = band_variants(vp_ref, vc_ref)
    nt = (((1,), (1,)), ((), ()))

    for j in range(ATT_HEADS // 2):
        qp = q_ref[:, j * LANES:(j + 1) * LANES]
        out_pair = jnp.zeros((ATT_BLOCK, LANES), F32)
        for half in range(2):
            h = 2 * j + half
            g = h // Q_PER_KV
            swapped = int((g % 2) != half)
            keep = low if half == 0 else jnp.logical_not(low)
            qh = jnp.where(keep, qp, jnp.zeros_like(qp))
            s_band = lax.dot_general(qh, k_band[g // 2][swapped], nt, preferred_element_type=F32)
            s = jnp.where(from_prev, s_band[:, :ATT_BLOCK], s_band[:, ATT_BLOCK:]) + bias_ref[h]
            sink = sink_ref[h]
            m = jnp.maximum(jnp.max(s, axis=-1, keepdims=True), sink)
            p = jnp.exp(s - m)
            denom = jnp.sum(p, axis=-1, keepdims=True) + jnp.exp(sink - m)
            p_band = jnp.concatenate([jnp.where(from_prev, p, 0.0), jnp.where(from_prev, 0.0, p)], axis=1)
            o = jnp.dot(p_band.astype(BF16), v_band[g // 2][swapped], preferred_element_type=F32) / denom
            out_pair = out_pair + jnp.where(keep, o, 0.0)
        obuf_ref[:, j * LANES:(j + 1) * LANES] = out_pair

    att = obuf_ref[...]
    ms = jnp.mean(att * att, axis=-1, keepdims=True)
    o_ref[...] = (att * lax.rsqrt(ms + EPS) * ng_ref[...]).astype(BF16)


def _attn_call(sinks, q, k, v, bias, ng):
    bsz, l, _ = q.shape
    nb = l // ATT_BLOCK
    nseq = ATT_SEQS if bsz % ATT_SEQS == 0 else 1
    cur = lambda w: pl.BlockSpec((nseq, ATT_BLOCK, w), lambda b, i: (b, i, 0))
    prev = lambda w: pl.BlockSpec((nseq, ATT_BLOCK, w), lambda b, i: (b, jnp.maximum(i - 1, 0), 0))
    return pl.pallas_call(
        _attn_kernel,
        grid=(bsz // nseq, nb),
        in_specs=[pl.BlockSpec(memory_space=pltpu.SMEM),
                  cur(D_ATT), prev(D_KV), cur(D_KV), prev(D_KV), cur(D_KV),
                  pl.BlockSpec((1,) + bias.shape[1:], lambda b, i: (jnp.minimum(i, 1), 0, 0, 0)),
                  pl.BlockSpec(ng.shape, lambda b, i: (0, 0))],
        out_specs=cur(D_ATT),
        out_shape=jax.ShapeDtypeStruct((bsz, l, D_ATT), BF16),
        scratch_shapes=[pltpu.VMEM((nseq, ATT_BLOCK, D_ATT), F32)],
        compiler_params=pltpu.CompilerParams(dimension_semantics=("arbitrary", "arbitrary"),
                                             vmem_limit_bytes=VMEM_LIMIT),
    )(sinks, q, k, k, v, v, bias, ng)


def _stage_bf16(staged, wsem):
    copies = [pltpu.make_async_copy(src.at[0], raw, wsem.at[j]) for j, (src, raw, _) in enumerate(staged)]
    for cp in copies:
        cp.start()
    for cp, (_, raw, dst) in zip(copies, staged):
        cp.wait()
        dst[...] = raw[...].astype(BF16)


def _out_proj_kernel(x_ref, ys_ref, ya_ref, g1_ref, sc_ref, sh_ref, ng_ref, wo_hbm,
                     rwh_ref, rwl_ref, rb_ref, upper_ref,
                     x1_ref, h_ref, idx_ref, gate_ref, rank_ref, cnt_ref,
                     carry_ref, wo_raw, wo_ref, wsem):
    @pl.when(pl.program_id(0) == 0)
    def _():
        carry_ref[...] = jnp.zeros_like(carry_ref)
        _stage_bf16(((wo_hbm, wo_raw, wo_ref),), wsem)

    mix = (jnp.dot(ys_ref[...], wo_ref[:D_SSM, :], preferred_element_type=F32)
           + jnp.dot(ya_ref[...], wo_ref[D_SSM:, :], preferred_element_type=F32))
    x1 = x_ref[...] + g1_ref[0] * mix
    ms = jnp.mean(x1 * x1, axis=-1, keepdims=True)
    h = x1 * lax.rsqrt(ms + EPS) * ng_ref[...]
    h = h * (1.0 + sc_ref[0]) + sh_ref[0]
    half = h.shape[1] // 2
    h_ref[...] = _pack_bf16_pair(h[:, :half], h[:, half:])
    hi, lo = _split_hi_lo(h)
    hi_terms = jnp.dot(hi, rwl_ref[...], preferred_element_type=F32)
    logits = (hi_terms[:, :LANES] + hi_terms[:, LANES:]
              + jnp.dot(lo, rwh_ref[...], preferred_element_type=F32))
    _route_tokens(logits.T[0:N_EXPERTS, :], rb_ref, upper_ref, idx_ref, gate_ref, rank_ref, cnt_ref, carry_ref)
    x1_ref[...] = x1


def _out_proj_call(x2, ys, ya, g1, sc2, sh2, ng, w_out, rwh, rwl, router_bias, upper, tiles_per_batch, tm):
    n, d = x2.shape
    row = lambda w: pl.BlockSpec((tm, w), lambda i: (i, 0))
    tok = lambda r: pl.BlockSpec((r, tm), lambda i: (0, i))
    full = lambda a: pl.BlockSpec(a.shape, lambda i: (0, 0))
    per_batch = pl.BlockSpec((1, 1, d), lambda i: (i // tiles_per_batch, 0, 0))
    hbm = pl.BlockSpec(memory_space=pl.ANY)
    staged = (w_out,)
    return pl.pallas_call(
        _out_proj_kernel,
        grid=(n // tm,),
        in_specs=[row(d), row(D_SSM), row(D_ATT), per_batch, per_batch, per_batch, full(ng),
                  hbm, full(rwh), full(rwl), full(router_bias), full(upper)],
        out_specs=[row(d), row(d // 2), tok(TOP_K), tok(TOP_K), tok(TOP_K),
                   pl.BlockSpec((N_EXPERTS, LANES), lambda i: (0, 0))],
        out_shape=[jax.ShapeDtypeStruct((n, d), F32), jax.ShapeDtypeStruct((n, d // 2), jnp.uint32),
                   jax.ShapeDtypeStruct((TOP_K, n), jnp.int32), jax.ShapeDtypeStruct((TOP_K, n), F32),
                   jax.ShapeDtypeStruct((TOP_K, n), jnp.int32), jax.ShapeDtypeStruct((N_EXPERTS, LANES), F32)],
        scratch_shapes=([pltpu.VMEM((N_EXPERTS, LANES), F32)]
                        + [pltpu.VMEM(a.shape[1:], a.dtype) for a in staged]
                        + [pltpu.VMEM(a.shape[1:], BF16) for a in staged]
                        + [pltpu.SemaphoreType.DMA((len(staged),))]),
        compiler_params=pltpu.CompilerParams(dimension_semantics=("arbitrary",),
                                             vmem_limit_bytes=VMEM_LIMIT),
    )(x2, ys, ya, g1, sc2, sh2, ng, w_out, rwh, rwl, router_bias, upper)


def _route_tokens(logits_t, rb_ref, upper_ref, idx_ref, gate_ref, rank_ref, cnt_ref, carry_ref):
    t = logits_t.shape[1]
    per_group = N_EXPERTS // ROUTE_GROUPS
    scores = 1.0 / (1.0 + jnp.exp(-logits_t))
    sel = scores + rb_ref[...]
    e_iota = lax.broadcasted_iota(jnp.int32, (N_EXPERTS, t), 0)

    sel3 = sel.reshape(ROUTE_GROUPS, per_group, t)
    w_iota = lax.broadcasted_iota(jnp.int32, sel3.shape, 1)
    m1 = jnp.max(sel3, axis=1, keepdims=True)
    first = jnp.min(jnp.where(sel3 == m1, w_iota, per_group), axis=1, keepdims=True)
    m2 = jnp.max(jnp.where(w_iota == first, NEG_INF, sel3), axis=1, keepdims=True)
    grp = (m1 + m2).reshape(ROUTE_GROUPS, t)

    g_iota = lax.broadcasted_iota(jnp.int32, (ROUTE_GROUPS, t), 0)
    gmask = jnp.zeros((ROUTE_GROUPS, t), jnp.bool_)
    for _ in range(ROUTE_TOPK_GROUPS):
        gm = jnp.max(grp, axis=0, keepdims=True)
        gfirst = jnp.min(jnp.where(grp == gm, g_iota, ROUTE_GROUPS), axis=0, keepdims=True)
        hit = g_iota == gfirst
        gmask = jnp.logical_or(gmask, hit)
        grp = jnp.where(hit, NEG_INF, grp)
    allowed = jnp.broadcast_to(gmask.reshape(ROUTE_GROUPS, 1, t),
                               (ROUTE_GROUPS, per_group, t)).reshape(N_EXPERTS, t)
    masked = jnp.where(allowed, sel, NEG_INF)

    picked = jnp.zeros((N_EXPERTS, t), jnp.bool_)
    idx_rows = []
    w_rows = []
    for _ in range(TOP_K):
        mm = jnp.max(masked, axis=0, keepdims=True)
        efirst = jnp.min(jnp.where(masked == mm, e_iota, N_EXPERTS), axis=0, keepdims=True)
        hit = e_iota == efirst
        idx_rows.append(efirst)
        w_rows.append(jnp.sum(jnp.where(hit, scores, 0.0), axis=0, keepdims=True))
        picked = jnp.logical_or(picked, hit)
        masked = jnp.where(hit, NEG_INF, masked)
    idx = jnp.concatenate(idx_rows, axis=0)
    w = jnp.concatenate(w_rows, axis=0)
    gate_ref[...] = w / jnp.sum(w, axis=0, keepdims=True) * ROUTED_SCALE
    idx_ref[...] = idx

    onehot = jnp.where(picked, 1.0, 0.0)
    sub = upper_ref.shape[0]
    carry = carry_ref[:, 0:1]
    parts = []
    for s0 in range(0, t, sub):
        oh = onehot[:, s0:s0 + sub]
        parts.append(jnp.dot(oh.astype(BF16), upper_ref[...], preferred_element_type=F32) + carry)
        carry = carry + jnp.sum(oh, axis=1, keepdims=True)
    rank_full = jnp.concatenate(parts, axis=1)
    rank_rows = [jnp.sum(jnp.where(e_iota == idx_rows[k], rank_full, 0.0), axis=0, keepdims=True)
                 for k in range(TOP_K)]
    rank_ref[...] = jnp.concatenate(rank_rows, axis=0).astype(jnp.int32)
    carry_ref[...] = jnp.broadcast_to(carry, carry_ref.shape)
    cnt_ref[...] = carry_ref[...]


PLAN_EXPERT, PLAN_FRESH, PLAN_VALID, PLAN_SEG, PLAN_NEXT, PLAN_NUSED = range(6)
PLAN_ROWS = SUBLANES


def _plan_kernel(cnt_ref, tri_ref, start_ref, plan_ref, *, nblocks):
    nbp = plan_ref.shape[1]
    cnt = cnt_ref[...].astype(jnp.int32)
    blocks = (cnt + (EXPERT_ROWS - 1)) // EXPERT_ROWS
    end = jnp.dot(tri_ref[...], blocks.astype(F32), precision=lax.Precision.HIGHEST,
                  preferred_element_type=F32).astype(jnp.int32)
    start = end - blocks
    start_ref[...] = start[:, 0:1] * EXPERT_ROWS
    nused = end[N_EXPERTS - 1:N_EXPERTS, 0:1]

    e_iota = lax.broadcasted_iota(jnp.int32, (N_EXPERTS, nbp), 0)
    blk = lax.broadcasted_iota(jnp.int32, (1, nbp), 1)
    expert = jnp.minimum(jnp.sum((end[:, 0:1] <= blk).astype(jnp.int32), axis=0, keepdims=True), N_EXPERTS - 1)
    mine = e_iota == expert
    pick = lambda col: jnp.sum(jnp.where(mine, col, 0), axis=0, keepdims=True)
    first = pick(start[:, 0:1])
    valid = jnp.clip(pick(cnt[:, 0:1]) - (blk - first) * EXPERT_ROWS, 0, EXPERT_ROWS)
    present = jnp.logical_or(blocks[:, 0:1] > 0,
                             jnp.logical_and(e_iota[:, 0:1] == N_EXPERTS - 1, nused < nblocks))
    seg = jnp.sum(jnp.logical_and(present, e_iota <= expert).astype(jnp.int32), axis=0, keepdims=True) - 1
    nxt = jnp.min(jnp.where(jnp.logical_and(present, e_iota > expert), e_iota, N_EXPERTS), axis=0, keepdims=True)
    rows = {PLAN_EXPERT: expert, PLAN_FRESH: (blk == first).astype(jnp.int32), PLAN_VALID: valid, PLAN_SEG: seg,
            PLAN_NEXT: jnp.where(nxt == N_EXPERTS, -1, nxt), PLAN_NUSED: jnp.broadcast_to(nused, (1, nbp))}
    zero = jnp.zeros((1, nbp), jnp.int32)
    plan_ref[...] = jnp.concatenate([rows.get(r, zero) for r in range(PLAN_ROWS)], axis=0)


def _plan_call(counts, nblocks):
    nbp = -(-nblocks // LANES) * LANES
    tri = jnp.asarray(np.tril(np.ones((N_EXPERTS, N_EXPERTS), np.float32)))
    return pl.pallas_call(
        functools.partial(_plan_kernel, nblocks=nblocks),
        out_shape=[jax.ShapeDtypeStruct((N_EXPERTS, 1), jnp.int32),
                   jax.ShapeDtypeStruct((PLAN_ROWS, nbp), jnp.int32)],
    )(counts, tri)


def _dest_kernel(idx_ref, rank_ref, start_ref, dest_ref):
    t = idx_ref.shape[1]
    e_iota = lax.broadcasted_iota(jnp.int32, (N_EXPERTS, t), 0)
    rows = [jnp.sum(jnp.where(e_iota == idx_ref[k:k + 1, :], start_ref[...], 0), axis=0, keepdims=True)
            for k in range(TOP_K)]
    dest = jnp.concatenate(rows, axis=0) + rank_ref[...]
    for k in range(TOP_K):
        for c in range(t // LANES):
            dest_ref[k, c:c + 1, :] = dest[k:k + 1, c * LANES:(c + 1) * LANES]


def _dest_call(idx, rank, pad_start, tile):
    n = idx.shape[1]
    tok = pl.BlockSpec((TOP_K, tile), lambda i: (0, i))
    return pl.pallas_call(
        _dest_kernel,
        grid=(n // tile,),
        in_specs=[tok, tok, pl.BlockSpec((N_EXPERTS, 1), lambda i: (0, 0))],
        out_specs=pl.BlockSpec((TOP_K, tile // LANES, LANES), lambda i: (0, i, 0)),
        out_shape=jax.ShapeDtypeStruct((TOP_K, n // LANES, LANES), jnp.int32),
        compiler_params=pltpu.CompilerParams(dimension_semantics=("arbitrary",)),
    )(idx, rank, pad_start)


def _scatter_rows_sc(rows, dest_flat, total_rows, chunk):
    n, w = rows.shape
    copies = dest_flat.shape[0] // n
    info = plsc.get_sparse_core_info()
    nc = info.num_cores
    per_worker = n // (nc * info.num_subcores)
    assert per_worker * nc * info.num_subcores == n and per_worker % chunk == 0
    mesh = plsc.VectorSubcoreMesh(core_axis_name="c", subcore_axis_name="s")

    @functools.partial(
        pl.kernel, mesh=mesh,
        out_type=jax.ShapeDtypeStruct((total_rows, w), rows.dtype),
        scratch_types=([pltpu.VMEM((chunk,), jnp.int32) for _ in range(copies)]
                       + [pltpu.VMEM((chunk, w), rows.dtype), pltpu.SemaphoreType.DMA, pltpu.SemaphoreType.DMA]),
    )
    def scatter(rows_hbm, idx_hbm, out_hbm, *scratch):
        idx_vs, (rows_v, isem, sem) = scratch[:copies], scratch[copies:]
        worker = lax.axis_index("s") * nc + lax.axis_index("c")

        @pl.loop(0, per_worker // chunk)
        def _(j):
            base = worker * per_worker + j * chunk
            loads = [pltpu.async_copy(idx_hbm.at[pl.ds(k * n + base, chunk)], idx_vs[k], isem)
                     for k in range(copies)]
            pltpu.sync_copy(rows_hbm.at[pl.ds(base, chunk)], rows_v)
            for ld in loads:
                ld.wait()
            stores = [pltpu.async_copy(rows_v, out_hbm.at[idx_vs[k]], sem) for k in range(copies)]
            for st in stores:
                st.wait()

    return scatter(rows, dest_flat)


EXPERT_LIVE_STEP = 256
X_RING = 4


def _expert_kernel(plan_ref, xs_hbm, wg_hbm, wu_hbm, wd_hbm, y_ref,
                   wgb_ref, wub_ref, wdb_ref, xbuf_ref, xsem, wg_raw, wu_raw, wd_raw, wsem):
    i = pl.program_id(0)
    rows, half = xbuf_ref.shape[1], xbuf_ref.shape[2]
    nused = plan_ref[PLAN_NUSED, 0]

    def w_copies(expert, slot):
        return [pltpu.make_async_copy(src.at[expert], dst.at[slot], wsem.at[slot])
                for src, dst in ((wg_hbm, wg_raw), (wu_hbm, wu_raw), (wd_hbm, wd_raw))]

    def x_copy(block):
        first = pl.multiple_of(block * rows, rows)
        slot = block % X_RING
        return pltpu.make_async_copy(xs_hbm.at[pl.ds(first, rows)], xbuf_ref.at[slot], xsem.at[slot])

    @pl.when(i == 0)
    def _():
        for b in range(X_RING - 1):
            @pl.when(b < nused)
            def _():
                x_copy(b).start()

    @pl.when(i + (X_RING - 1) < nused)
    def _():
        x_copy(i + (X_RING - 1)).start()

    @pl.when(i == 0)
    def _():
        for c in w_copies(plan_ref[PLAN_EXPERT, 0], 0):
            c.start()

    @pl.when(plan_ref[PLAN_FRESH, i] > 0)
    def _():
        slot = plan_ref[PLAN_SEG, i] % 2
        for c in w_copies(plan_ref[PLAN_EXPERT, i], slot):
            c.wait()

        @pl.when(plan_ref[PLAN_NEXT, i] >= 0)
        def _():
            for c in w_copies(plan_ref[PLAN_NEXT, i], 1 - slot):
                c.start()

        wgb_ref[...] = wg_raw[slot].astype(BF16)
        wub_ref[...] = wu_raw[slot].astype(BF16)
        wdb_ref[...] = wd_raw[slot].astype(BF16)

    @pl.when(i < nused)
    def _():
        x_copy(i).wait()

    real = plan_ref[PLAN_VALID, i]
    slot = i % X_RING

    def ffn(live):
        xw = xbuf_ref[slot, 0:live, :]
        row = lax.broadcasted_iota(jnp.int32, xw.shape, 0)
        x_lo, x_hi = _unpack_bf16_pair(jnp.where(row < real, xw, jnp.uint32(0)))
        x_lo = x_lo.astype(BF16)
        x_hi = x_hi.astype(BF16)
        gate = (jnp.dot(x_lo, wgb_ref[:half, :], preferred_element_type=F32)
                + jnp.dot(x_hi, wgb_ref[half:, :], preferred_element_type=F32))
        up = (jnp.dot(x_lo, wub_ref[:half, :], preferred_element_type=F32)
              + jnp.dot(x_hi, wub_ref[half:, :], preferred_element_type=F32))
        u = (_silu(gate) * up).astype(BF16)
        y_lo = jnp.dot(u, wdb_ref[:, :half], preferred_element_type=F32)
        y_hi = jnp.dot(u, wdb_ref[:, half:], preferred_element_type=F32)
        y_ref[0:live, :] = _pack_bf16_pair(y_lo, y_hi)
        if live < rows:
            y_ref[live:, :] = jnp.zeros((rows - live, half), y_ref.dtype)

    pieces = (real + (EXPERT_LIVE_STEP - 1)) // EXPERT_LIVE_STEP
    for p in range(1, rows // EXPERT_LIVE_STEP + 1):
        @pl.when(jnp.logical_and(i < nused, pieces == p))
        def _():
            ffn(p * EXPERT_LIVE_STEP)

    @pl.when(i >= nused)
    def _():
        y_ref[...] = jnp.zeros_like(y_ref)


def _expert_call(plan, xs, wg, wu, wd, rows):
    p, w = xs.shape
    d, f = wg.shape[1], wg.shape[2]
    hbm = pl.BlockSpec(memory_space=pl.ANY)
    grid_spec = pltpu.PrefetchScalarGridSpec(
        num_scalar_prefetch=1,
        grid=(p // rows,),
        in_specs=[hbm, hbm, hbm, hbm],
        out_specs=pl.BlockSpec((rows, w), lambda i, *_: (i, 0)),
        scratch_shapes=[pltpu.VMEM((d, f), BF16), pltpu.VMEM((d, f), BF16), pltpu.VMEM((f, d), BF16),
                        pltpu.VMEM((X_RING, rows, w), xs.dtype), pltpu.SemaphoreType.DMA((X_RING,)),
                        pltpu.VMEM((2, d, f), wg.dtype), pltpu.VMEM((2, d, f), wu.dtype),
                        pltpu.VMEM((2, f, d), wd.dtype), pltpu.SemaphoreType.DMA((2,))],
    )
    return pl.pallas_call(
        _expert_kernel,
        grid_spec=grid_spec,
        out_shape=jax.ShapeDtypeStruct((p, w), jnp.uint32),
        compiler_params=pltpu.CompilerParams(dimension_semantics=("arbitrary",),
                                             vmem_limit_bytes=VMEM_LIMIT),
    )(plan, xs, wg, wu, wd)


def _gather_rows_sc(table, idx, chunk):
    m = idx.shape[0]
    w = table.shape[1]
    info = plsc.get_sparse_core_info()
    nc = info.num_cores
    per_worker = m // (nc * info.num_subcores)
    assert per_worker * nc * info.num_subcores == m and per_worker % chunk == 0
    part = chunk // SC_GATHER_PARTS
    mesh = plsc.VectorSubcoreMesh(core_axis_name="c", subcore_axis_name="s")

    @functools.partial(
        pl.kernel, mesh=mesh,
        out_type=jax.ShapeDtypeStruct((m, w), table.dtype),
        scratch_types=([pltpu.VMEM((part,), jnp.int32) for _ in range(SC_GATHER_PARTS)]
                       + [pltpu.VMEM((part, w), table.dtype) for _ in range(SC_GATHER_PARTS)]
                       + [pltpu.SemaphoreType.DMA] * 3),
    )
    def gather(table_hbm, idx_hbm, out_hbm, *scratch):
        idx_vs = scratch[:SC_GATHER_PARTS]
        rows_vs = scratch[SC_GATHER_PARTS:2 * SC_GATHER_PARTS]
        isem, gsem, wsem = scratch[2 * SC_GATHER_PARTS:]
        worker = lax.axis_index("s") * nc + lax.axis_index("c")

        @pl.loop(0, per_worker // chunk)
        def _(j):
            base = worker * per_worker + j * chunk
            loads = [pltpu.async_copy(idx_hbm.at[pl.ds(base + p * part, part)], idx_vs[p], isem)
                     for p in range(SC_GATHER_PARTS)]
            gathers = []
            for p in range(SC_GATHER_PARTS):
                loads[p].wait()
                gathers.append(pltpu.async_copy(table_hbm.at[idx_vs[p]], rows_vs[p], gsem))
            writes = []
            for p in range(SC_GATHER_PARTS):
                gathers[p].wait()
                writes.append(pltpu.async_copy(rows_vs[p], out_hbm.at[pl.ds(base + p * part, part)], wsem))
            for wr in writes:
                wr.wait()

    return gather(table, idx)


def _combine_kernel(yk_ref, gate_ref, base_ref, h_ref, g2_ref, fg_ref, sg_hbm, su_hbm, sd_hbm, o_ref,
                    sg_raw, su_raw, sd_raw, sg_ref, su_ref, sd_ref, wsem):
    @pl.when(pl.program_id(0) == 0)
    def _():
        _stage_bf16(((sg_hbm, sg_raw, sg_ref), (su_hbm, su_raw, su_ref), (sd_hbm, sd_raw, sd_ref)), wsem)

    t = base_ref.shape[0]
    half = yk_ref.shape[2]
    h_lo, h_hi = _unpack_bf16_pair(h_ref[...])
    hb = jnp.concatenate([h_lo.astype(BF16), h_hi.astype(BF16)], axis=1)
    u = _silu(jnp.dot(hb, sg_ref[...], preferred_element_type=F32)) * jnp.dot(hb, su_ref[...],
                                                                              preferred_element_type=F32)
    shared = jnp.dot(u.astype(BF16), sd_ref[...], preferred_element_type=F32)
    gates = jnp.concatenate([gate_ref[...], jnp.zeros((LANES - TOP_K, t), F32)], axis=0).T
    r_lo = shared[:, :half]
    r_hi = shared[:, half:]
    for k in range(TOP_K):
        y_lo, y_hi = _unpack_bf16_pair(yk_ref[k])
        r_lo = r_lo + gates[:, k:k + 1] * y_lo
        r_hi = r_hi + gates[:, k:k + 1] * y_hi
    g2 = g2_ref[0]
    x_lo = base_ref[:, :half] + g2[:, :half] * r_lo
    x_hi = base_ref[:, half:] + g2[:, half:] * r_hi
    ms = (jnp.sum(x_lo * x_lo, axis=-1, keepdims=True)
          + jnp.sum(x_hi * x_hi, axis=-1, keepdims=True)) * (1.0 / (2 * half))
    inv = lax.rsqrt(ms + EPS)
    o_ref[:, :half] = x_lo * inv * fg_ref[:, :half]
    o_ref[:, half:] = x_hi * inv * fg_ref[:, half:]


def _combine_call(yk, gates, base, h2, g2, fg, sg, su, sd, tiles_per_batch, tile, tile0):
    n, d = base.shape
    row = lambda w: pl.BlockSpec((tile, w), lambda i: (i + tile0, 0))
    hbm = pl.BlockSpec(memory_space=pl.ANY)
    staged = (sg, su, sd)
    return pl.pallas_call(
        _combine_kernel,
        grid=(yk.shape[1] // tile,),
        in_specs=[pl.BlockSpec((TOP_K, tile, yk.shape[2]), lambda i: (0, i, 0)),
                  pl.BlockSpec((TOP_K, tile), lambda i: (0, i + tile0)), row(d), row(h2.shape[1]),
                  pl.BlockSpec((1, 1, d), lambda i: ((i + tile0) // tiles_per_batch, 0, 0)),
                  pl.BlockSpec((1, d), lambda i: (0, 0)), hbm, hbm, hbm],
        out_specs=row(d),
        out_shape=jax.ShapeDtypeStruct((n, d), F32),
        scratch_shapes=([pltpu.VMEM(a.shape[1:], a.dtype) for a in staged]
                        + [pltpu.VMEM(a.shape[1:], BF16) for a in staged]
                        + [pltpu.SemaphoreType.DMA((len(staged),))]),
        input_output_aliases={2: 0},
        compiler_params=pltpu.CompilerParams(dimension_semantics=("arbitrary",),
                                             vmem_limit_bytes=VMEM_LIMIT),
    )(yk, gates, base, h2, g2, fg, sg, su, sd)


def _pad_cols(a, width):
    return jnp.pad(a, ((0, 0), (0, width - a.shape[1])))


def _layer(x, mod, norm1_g, norm2_g, w_in, conv_w, conv_b, dt_bias, a_log, d_skip, ssm_norm_g,
           att_norm_g, sinks, rel_bias, w_out, router_w, router_bias, exp_w_gate, exp_w_up, exp_w_down,
           sh_w_gate, sh_w_up, sh_w_down, final_g):
    bsz, l, d = x.shape
    n = bsz * l
    tm = min(ROW_TILE, l)

    sh1, sc1, g1, sh2, sc2, g2 = [m[:, None, :] for m in jnp.split(mod, 6, axis=-1)]

    assert math.frexp(ATT_HEAD_DIM ** -0.5)[0] == 0.5
    x2 = x.reshape(n, d)
    z, xbc, dt, q, k, v = _in_proj_call(x2, sc1, sh1, norm1_g[None, :], jnp.swapaxes(w_in, 1, 2), l // tm, tm)

    triu = jnp.asarray(np.triu(np.ones((CHUNK, CHUNK), np.float32))).astype(BF16)
    shift = jnp.asarray(_conv_shift_matrix()).astype(BF16)
    y_ssm = _ssd_call(xbc.reshape(bsz, l, CONV_CH), z.reshape(bsz, l, D_SSM), dt.reshape(bsz, l, LANES),
                      conv_w, conv_b[None, :], dt_bias[:, None], a_log[:, None],
                      jnp.repeat(d_skip, SSM_HEAD_DIM)[None, :], ssm_norm_g[None, :], triu, shift)

    bias = _bias_call(rel_bias, jnp.asarray(_rel_bucket_table()))
    y_att = _attn_call(sinks, q.reshape(bsz, l, D_ATT), k.reshape(bsz, l, D_KV), v.reshape(bsz, l, D_KV), bias,
                       att_norm_g[None, :])

    rw = _pad_cols(router_w, LANES)
    rwh = rw.astype(BF16)
    rwl = jnp.concatenate([rwh, (rw - rwh.astype(F32)).astype(BF16)], axis=1)
    rs = min(RANK_SUB, tm)
    upper = jnp.asarray(np.triu(np.ones((rs, rs), np.float32), 1)).astype(BF16)
    base, h2, idx, gates, rank, counts = _out_proj_call(
        x2, y_ssm.reshape(n, D_SSM), y_att.reshape(n, D_ATT), g1, sc2, sh2, norm2_g[None, :],
        w_out, rwh, rwl, router_bias[:, None], upper, l // tm, tm)
    rt = min(ROUTE_TILE, n)

    nblocks = (n * TOP_K + N_EXPERTS * (EXPERT_ROWS - 1) + EXPERT_ROWS - 1) // EXPERT_ROWS
    pad_start, plan = _plan_call(counts, nblocks)
    dest = _dest_call(idx, rank, pad_start, rt)

    xs = _scatter_rows_sc(h2, dest.reshape(-1), nblocks * EXPERT_ROWS, SC_CHUNK)
    ys = _expert_call(plan, xs, exp_w_gate, exp_w_up, exp_w_down, EXPERT_ROWS)
    ctile = min(COMBINE_TILE, l)
    groups = COMBINE_GROUPS if bsz % COMBINE_GROUPS == 0 else 1
    ng = n // groups
    out = base
    for g in range(groups):
        idx_g = dest[:, g * ng // LANES:(g + 1) * ng // LANES, :].reshape(-1)
        yk = _gather_rows_sc(ys, idx_g, SC_CHUNK).reshape(TOP_K, ng, ys.shape[1])
        out = _combine_call(yk, gates, out, h2, g2, final_g[None, :], sh_w_gate, sh_w_up, sh_w_down,
                            l // ctile, ctile, g * ng // ctile)
    return out.reshape(bsz, l, d)


def kernel(x, c, mod_w, mod_b, norm1_g, norm2_g, w_in, conv_w, conv_b, dt_bias, a_log, d_skip, ssm_norm_g,
           att_norm_g, sinks, rel_bias, w_out, router_w, router_bias, exp_w_gate, exp_w_up, exp_w_down,
           sh_w_gate, sh_w_up, sh_w_down, final_g):
    assert mod_w.shape[0] == 1, "single-layer block"
    bsz = x.shape[0]
    c_pad = jnp.pad(c, ((0, SUBLANES - bsz % SUBLANES if bsz % SUBLANES else 0), (0, 0)))
    mod = _mod_call(c_pad, mod_w[0], mod_b[0][None, :])[:bsz]
    return _layer(x, mod, norm1_g[0], norm2_g[0], w_in, conv_w[0], conv_b[0], dt_bias[0], a_log[0], d_skip[0],
                  ssm_norm_g[0], att_norm_g[0], sinks[0], rel_bias, w_out, router_w[0], router_bias[0],
                  exp_w_gate[0], exp_w_up[0], exp_w_down[0], sh_w_gate, sh_w_up, sh_w_down, final_g)
```

```python
import functools
import math

import numpy as np
import jax
import jax.numpy as jnp
from jax import lax
from jax.experimental import pallas as pl
from jax.experimental.pallas import tpu as pltpu
from jax.experimental.pallas import tpu_sc as plsc

F32 = jnp.float32
BF16 = jnp.bfloat16

D_MODEL = 1024
SSM_HEAD_DIM = 64
D_SSM = D_MODEL
SSM_HEADS = D_SSM // SSM_HEAD_DIM
SSM_GROUPS = 4
D_STATE = 128
CONV_K = 4
CONV_CH = D_SSM + 2 * SSM_GROUPS * D_STATE
CHUNK = 128
ATT_HEAD_DIM = 64
D_ATT = D_MODEL
ATT_HEADS = D_ATT // ATT_HEAD_DIM
KV_HEADS = ATT_HEADS // 4
Q_PER_KV = ATT_HEADS // KV_HEADS
D_KV = KV_HEADS * ATT_HEAD_DIM
WINDOW = 128
ATT_BLOCK = 128
REL_BUCKETS = 32
REL_MAX_DIST = 128
N_EXPERTS = 64
TOP_K = 8
EXPERT_DIM = D_MODEL // 4
SHARED_DIM = D_MODEL // 4
ROUTE_GROUPS = 8
ROUTE_TOPK_GROUPS = 4
ROUTED_SCALE = 2.5
EPS = 1e-6

LANES = 128
SUBLANES = 8
HALF = LANES // 2

ROW_TILE = 512
ROUTE_TILE = 2048
RANK_SUB = 512
COMBINE_TILE = 512
SC_CHUNK = 128
SC_GATHER_PARTS = 4
EXPERT_ROWS = 1024
COMBINE_GROUPS = 2
VMEM_LIMIT = 48 * 1024 * 1024

NEG_INF = float("-inf")


def _silu(v):
    return v * (1.0 / (1.0 + jnp.exp(-v)))


def _softplus(v):
    return jnp.maximum(v, 0.0) + jnp.log(1.0 + jnp.exp(-jnp.abs(v)))


def _split_hi_lo(v):
    hi = v.astype(BF16)
    lo = (v - hi.astype(F32)).astype(BF16)
    return hi, lo


def _pack_bf16_pair(a, b):
    w = pltpu.pack_elementwise([a, b], packed_dtype=BF16)
    return w if w.dtype == jnp.uint32 else lax.bitcast_convert_type(w, jnp.uint32)


def _unpack_bf16_pair(w):
    a = pltpu.unpack_elementwise(w, index=0, packed_dtype=BF16, unpacked_dtype=F32)
    b = pltpu.unpack_elementwise(w, index=1, packed_dtype=BF16, unpacked_dtype=F32)
    return a, b


def _lane_half_mask(shape):
    return lax.broadcasted_iota(jnp.int32, shape, len(shape) - 1) < HALF


def _mod_kernel(c_ref, w_ref, b_ref, o_ref):
    a = _silu(c_ref[...])
    o_ref[...] = jnp.dot(a, w_ref[...], precision=lax.Precision.HIGHEST,
                         preferred_element_type=F32) + b_ref[...]


def _mod_call(c_pad, mod_w, mod_b):
    rows, d = c_pad.shape
    cols = mod_w.shape[1]
    return pl.pallas_call(
        _mod_kernel,
        grid=(cols // d,),
        in_specs=[pl.BlockSpec((rows, d), lambda j: (0, 0)),
                  pl.BlockSpec((d, d), lambda j: (0, j)),
                  pl.BlockSpec((1, d), lambda j: (0, j))],
        out_specs=pl.BlockSpec((rows, d), lambda j: (0, j)),
        out_shape=jax.ShapeDtypeStruct((rows, cols), F32),
        compiler_params=pltpu.CompilerParams(dimension_semantics=("arbitrary",),
                                             vmem_limit_bytes=VMEM_LIMIT),
    )(c_pad, mod_w, mod_b)


IN_PROJ_SEGMENTS = ((D_SSM, BF16), (CONV_CH, BF16), (LANES, F32), (D_ATT, BF16), (D_KV, BF16), (D_KV, BF16))


def _in_proj_kernel(x_ref, sc_ref, sh_ref, g_ref, w_hbm, *refs):
    out_refs = refs[:len(IN_PROJ_SEGMENTS)]
    res_ref, wraw_ref, w_ref, wsem = refs[len(IN_PROJ_SEGMENTS):]

    @pl.when(pl.program_id(0) == 0)
    def _():
        cp = pltpu.make_async_copy(w_hbm.at[0], wraw_ref, wsem)
        cp.start()
        cp.wait()
        src_dt = D_SSM + CONV_CH
        dst_q = src_dt + LANES
        src_q = src_dt + SSM_HEADS
        q_scale = ATT_HEAD_DIM ** -0.5

        def put(dst, src, scale=None):
            t = wraw_ref[src:src + LANES, :].T
            w_ref[:, dst:dst + LANES] = (t if scale is None else t * scale).astype(BF16)

        for c0 in range(0, src_dt, LANES):
            put(c0, c0)
        dt_tile = wraw_ref[src_dt:src_dt + LANES, :].T
        lane = lax.broadcasted_iota(jnp.int32, dt_tile.shape, 1)
        w_ref[:, src_dt:dst_q] = jnp.where(lane < SSM_HEADS, dt_tile, 0.0).astype(BF16)
        for c0 in range(0, D_ATT, LANES):
            put(dst_q + c0, src_q + c0, q_scale)
        for c0 in range(D_ATT, D_ATT + 2 * D_KV, LANES):
            put(dst_q + c0, src_q + c0)

    xf = x_ref[...]
    res_ref[...] = xf
    ms = jnp.mean(xf * xf, axis=-1, keepdims=True)
    h = xf * lax.rsqrt(ms + EPS) * g_ref[...]
    h = h * (1.0 + sc_ref[0]) + sh_ref[0]
    hb = h.astype(BF16)
    col = 0
    for (width, dtype), o_ref in zip(IN_PROJ_SEGMENTS, out_refs):
        o_ref[...] = jnp.dot(hb, w_ref[:, col:col + width], preferred_element_type=F32).astype(dtype)
        col += width


def _in_proj_call(x2, sc1, sh1, g1n, w_in_t, tiles_per_batch, tm):
    n, d = x2.shape
    row = lambda w: pl.BlockSpec((tm, w), lambda i: (i, 0))
    full = lambda a: pl.BlockSpec(a.shape, lambda i: (0, 0))
    per_batch = pl.BlockSpec((1, 1, d), lambda i: (i // tiles_per_batch, 0, 0))
    cols = sum(w for w, _ in IN_PROJ_SEGMENTS)
    w_in = w_in_t
    assert w_in.shape == (1, cols - (LANES - SSM_HEADS), d)
    return pl.pallas_call(
        _in_proj_kernel,
        grid=(n // tm,),
        in_specs=[row(d), per_batch, per_batch, full(g1n), pl.BlockSpec(memory_space=pl.ANY)],
        out_specs=[row(w) for w, _ in IN_PROJ_SEGMENTS] + [row(d)],
        out_shape=[jax.ShapeDtypeStruct((n, w), dt) for w, dt in IN_PROJ_SEGMENTS] + [jax.ShapeDtypeStruct((n, d), F32)],
        scratch_shapes=[pltpu.VMEM(w_in.shape[1:], w_in.dtype), pltpu.VMEM((d, cols), BF16), pltpu.SemaphoreType.DMA(())],
        compiler_params=pltpu.CompilerParams(dimension_semantics=("arbitrary",),
                                             vmem_limit_bytes=VMEM_LIMIT),
    )(x2, sc1, sh1, g1n, w_in)


SSD_SEQS = 4
CONV_HALO = 16


def _conv_shift_matrix():
    s = np.zeros((CONV_K * CHUNK, CONV_HALO + CHUNK), np.float32)
    for k in range(CONV_K):
        t = np.arange(CHUNK)
        s[k * CHUNK + t, CONV_HALO + t - (CONV_K - 1) + k] = 1.0
    return s


def _silu_tanh(v):
    hv = 0.5 * v
    return hv + hv * jnp.tanh(hv)


def _ssd_kernel(xbc_ref, z_ref, dt_ref, cw_ref, cb_ref, dtb_ref, alog_ref, dskip_ref, ng_ref, triu_ref, shift_ref,
                y_ref, state_ref, ucat_ref, ybuf_ref):
    nseq = xbc_ref.shape[0]

    @pl.when(pl.program_id(1) == 0)
    def _():
        state_ref[...] = jnp.zeros_like(state_ref)
        ucat_ref[:, 0:CONV_HALO, :] = jnp.zeros((nseq, CONV_HALO, CONV_CH), BF16)

    for q in range(nseq):
        _ssd_chunk(xbc_ref.at[q], z_ref.at[q], dt_ref.at[q], cw_ref, cb_ref, dtb_ref, alog_ref, dskip_ref, ng_ref,
                   triu_ref, shift_ref, y_ref.at[q], state_ref.at[q], ucat_ref.at[q], ybuf_ref.at[q])


def _ssd_chunk(xbc_ref, z_ref, dt_ref, cw_ref, cb_ref, dtb_ref, alog_ref, dskip_ref, ng_ref, triu_ref, shift_ref,
               y_ref, state_ref, ucat_ref, ybuf_ref):
    ucat_ref[CONV_HALO:, :] = xbc_ref[...]
    shifted = jnp.dot(shift_ref[...], ucat_ref[...], preferred_element_type=F32)
    ucat_ref[0:CONV_HALO, :] = ucat_ref[CHUNK:CHUNK + CONV_HALO, :]
    acc = cb_ref[...] + cw_ref[0:1, :] * shifted[0:CHUNK]
    for kk in range(1, CONV_K):
        acc = acc + cw_ref[kk:kk + 1, :] * shifted[kk * CHUNK:(kk + 1) * CHUNK]
    act = _silu_tanh(acc)
    xs = act[:, :D_SSM]
    gn = SSM_GROUPS * D_STATE

    dt_t = _softplus(dt_ref[...].T[0:SSM_HEADS, :] + dtb_ref[...])
    a_t = dt_t * (-jnp.exp(alog_ref[...]))
    a_hi = a_t.astype(BF16)
    a_mid = (a_t - a_hi.astype(F32)).astype(BF16)
    a_lo = (a_t - a_hi.astype(F32) - a_mid.astype(F32)).astype(BF16)
    triu = triu_ref[...]
    cs_t = (jnp.dot(a_hi, triu, preferred_element_type=F32) + jnp.dot(a_mid, triu, preferred_element_type=F32)
            + jnp.dot(a_lo, triu, preferred_element_type=F32))
    cs_end = cs_t[:, CHUNK - 1:CHUNK]
    r_t = cs_t - jnp.log(dt_t)
    w_t = jnp.exp(cs_end - cs_t) * dt_t
    chunk_decay = jnp.exp(cs_end)
    cols = jnp.concatenate([cs_t, jnp.exp(cs_t), jnp.zeros((LANES - 2 * SSM_HEADS, CHUNK), F32)], axis=0).T

    li = lax.broadcasted_iota(jnp.int32, (CHUNK, CHUNK), 0)
    si = lax.broadcasted_iota(jnp.int32, (CHUNK, CHUNK), 1)
    causal = li >= si
    low = _lane_half_mask((CHUNK, LANES))
    low_row = _lane_half_mask((1, LANES))

    heads_per_group = SSM_HEADS // SSM_GROUPS
    for g in range(SSM_GROUPS):
        b_g = act[:, D_SSM + g * D_STATE:D_SSM + (g + 1) * D_STATE]
        c_g = act[:, D_SSM + gn + g * D_STATE:D_SSM + gn + (g + 1) * D_STATE]
        b_gb = b_g.astype(BF16)
        c_gb = c_g.astype(BF16)
        cb = lax.dot_general(c_gb, b_gb, (((1,), (1,)), ((), ())), preferred_element_type=F32)
        b_t = b_g.T
        for jp in range(heads_per_group // 2):
            j = g * (heads_per_group // 2) + jp
            lanes = slice(j * LANES, (j + 1) * LANES)
            xp = xs[:, lanes]
            xpb = xp.astype(BF16)
            ydiag = jnp.zeros((CHUNK, LANES), F32)
            snew = jnp.zeros((D_STATE, LANES), F32)
            for half in range(2):
                h = 2 * j + half
                diff = cols[:, h:h + 1] - r_t[h:h + 1, :]
                m = (cb * jnp.exp(jnp.where(causal, diff, NEG_INF))).astype(BF16)
                keep = low if half == 0 else jnp.logical_not(low)
                xh = jnp.where(keep, xpb, jnp.zeros_like(xpb))
                ydiag = ydiag + jnp.dot(m, xh, preferred_element_type=F32)
                snew = snew + jnp.dot((b_t * w_t[h:h + 1, :]).astype(BF16), xh, preferred_element_type=F32)
            s_in = state_ref[:, lanes]
            yoff = jnp.dot(c_gb, s_in.astype(BF16), preferred_element_type=F32)
            h0 = 2 * j
            e0 = SSM_HEADS + h0
            escale = jnp.where(low, cols[:, e0:e0 + 1], cols[:, e0 + 1:e0 + 2])
            cdec = jnp.where(low_row, chunk_decay[h0:h0 + 1, :], chunk_decay[h0 + 1:h0 + 2, :])
            ybuf_ref[:, lanes] = ydiag + yoff * escale + xp * dskip_ref[:, lanes]
            state_ref[:, lanes] = s_in * cdec + snew

    yz = ybuf_ref[...] * _silu_tanh(z_ref[...].astype(F32))
    gw = D_SSM // SSM_GROUPS
    for g in range(SSM_GROUPS):
        part = yz[:, g * gw:(g + 1) * gw]
        ms = jnp.mean(part * part, axis=-1, keepdims=True)
        y_ref[:, g * gw:(g + 1) * gw] = (part * lax.rsqrt(ms + EPS)
                                            * ng_ref[:, g * gw:(g + 1) * gw]).astype(BF16)


def _ssd_call(xbc, z, dt, conv_w, conv_b, dtb, alog, dskip, ng, triu, shift):
    bsz, l, _ = xbc.shape
    nc = l // CHUNK
    nseq = SSD_SEQS if bsz % SSD_SEQS == 0 else 1
    chunk = lambda w: pl.BlockSpec((nseq, CHUNK, w), lambda b, c: (b, c, 0))
    full = lambda a: pl.BlockSpec(a.shape, lambda b, c: (0, 0))
    return pl.pallas_call(
        _ssd_kernel,
        grid=(bsz // nseq, nc),
        in_specs=[chunk(CONV_CH), chunk(D_SSM), chunk(LANES), full(conv_w), full(conv_b), full(dtb),
                  full(alog), full(dskip), full(ng), full(triu), full(shift)],
        out_specs=chunk(D_SSM),
        out_shape=jax.ShapeDtypeStruct((bsz, l, D_SSM), BF16),
        scratch_shapes=[pltpu.VMEM((nseq, D_STATE, D_SSM), F32),
                        pltpu.VMEM((nseq, CONV_HALO + CHUNK, CONV_CH), BF16),
                        pltpu.VMEM((nseq, CHUNK, D_SSM), F32)],
        compiler_params=pltpu.CompilerParams(dimension_semantics=("arbitrary", "arbitrary"),
                                             vmem_limit_bytes=VMEM_LIMIT),
    )(xbc, z, dt, conv_w, conv_b, dtb, alog, dskip, ng, triu, shift)


assert WINDOW == ATT_BLOCK


def _rel_bucket_table():
    qi = np.arange(ATT_BLOCK)[:, None]
    c = np.arange(ATT_BLOCK)[None, :]
    dist = np.where(c > qi, qi + ATT_BLOCK - c, qi - c)
    max_exact = REL_BUCKETS // 2
    d = np.maximum(dist, 1).astype(np.float32)
    large = max_exact + (np.log(d / np.float32(max_exact)) / np.float32(math.log(REL_MAX_DIST / max_exact))
                         * np.float32(REL_BUCKETS - max_exact)).astype(np.int32)
    large = np.minimum(large, REL_BUCKETS - 1)
    return np.where(dist < max_exact, dist, large).astype(np.int32)


def _bias_kernel(rb_ref, bucket_ref, o_ref):
    bucket = bucket_ref[...]
    from_prev = (lax.broadcasted_iota(jnp.int32, bucket.shape, 1)
                 > lax.broadcasted_iota(jnp.int32, bucket.shape, 0))
    for h in range(ATT_HEADS):
        acc = jnp.zeros(bucket.shape, F32)
        for b in range(REL_BUCKETS):
            acc = jnp.where(bucket == b, rb_ref[b, h], acc)
        o_ref[1, h] = acc
        o_ref[0, h] = jnp.where(from_prev, NEG_INF, acc)


def _bias_call(rel_bias, bucket):
    return pl.pallas_call(
        _bias_kernel,
        in_specs=[pl.BlockSpec(memory_space=pltpu.SMEM), pl.BlockSpec(memory_space=pltpu.VMEM)],
        out_shape=jax.ShapeDtypeStruct((2, ATT_HEADS) + bucket.shape, F32),
    )(rel_bias, bucket)


ATT_SEQS = 4


def _attn_kernel(sink_ref, q_ref, kp_ref, kc_ref, vp_ref, vc_ref, bias_ref, ng_ref, o_ref, obuf_ref):
    for s in range(q_ref.shape[0]):
        _attn_block(sink_ref, q_ref.at[s], kp_ref.at[s], kc_ref.at[s], vp_ref.at[s], vc_ref.at[s], bias_ref.at[0],
                    ng_ref, o_ref.at[s], obuf_ref.at[s])


def _attn_block(sink_ref, q_ref, kp_ref, kc_ref, vp_ref, vc_ref, bias_ref, ng_ref, o_ref, obuf_ref):
    qi = lax.broadcasted_iota(jnp.int32, (ATT_BLOCK, ATT_BLOCK), 0)
    ci = lax.broadcasted_iota(jnp.int32, (ATT_BLOCK, ATT_BLOCK), 1)
    from_prev = ci > qi
    low = _lane_half_mask((ATT_BLOCK, LANES))

    def band_variants(prev_ref, cur_ref):
        out = []
        for cpair in range(KV_HEADS // 2):
            lanes = slice(cpair * LANES, (cpair + 1) * LANES)
            t = jnp.concatenate([prev_ref[:, lanes], cur_ref[:, lanes]], axis=0).astype(F32)
            out.append((t.astype(BF16), pltpu.roll(t, HALF, 1).astype(BF16)))
        return out

    k_band = band_variants(kp_ref, kc_ref)
    v_band = band_variants(vp_ref, vc_ref)
    nt = (((1,), (1,)), ((), ()))

    for j in range(ATT_HEADS // 2):
        qp = q_ref[:, j * LANES:(j + 1) * LANES]
        out_pair = jnp.zeros((ATT_BLOCK, LANES), F32)
        for half in range(2):
            h = 2 * j + half
            g = h // Q_PER_KV
            swapped = int((g % 2) != half)
            keep = low if half == 0 else jnp.logical_not(low)
            qh = jnp.where(keep, qp, jnp.zeros_like(qp))
            s_band = lax.dot_general(qh, k_band[g // 2][swapped], nt, preferred_element_type=F32)
            s = jnp.where(from_prev, s_band[:, :ATT_BLOCK], s_band[:, ATT_BLOCK:]) + bias_ref[h]
            sink = sink_ref[h]
            m = jnp.maximum(jnp.max(s, axis=-1, keepdims=True), sink)
            p = jnp.exp(s - m)
            denom = jnp.sum(p, axis=-1, keepdims=True) + jnp.exp(sink - m)
            p_band = jnp.concatenate([jnp.where(from_prev, p, 0.0), jnp.where(from_prev, 0.0, p)], axis=1)
            o = jnp.dot(p_band.astype(BF16), v_band[g // 2][swapped], preferred_element_type=F32) / denom
            out_pair = out_pair + jnp.where(keep, o, 0.0)
        obuf_ref[:, j * LANES:(j + 1) * LANES] = out_pair

    att = obuf_ref[...]
    ms = jnp.mean(att * att, axis=-1, keepdims=True)
    o_ref[...] = (att * lax.rsqrt(ms + EPS) * ng_ref[...]).astype(BF16)


def _attn_call(sinks, q, k, v, bias, ng):
    bsz, l, _ = q.shape
    nb = l // ATT_BLOCK
    nseq = ATT_SEQS if bsz % ATT_SEQS == 0 else 1
    cur = lambda w: pl.BlockSpec((nseq, ATT_BLOCK, w), lambda b, i: (b, i, 0))
    prev = lambda w: pl.BlockSpec((nseq, ATT_BLOCK, w), lambda b, i: (b, jnp.maximum(i - 1, 0), 0))
    return pl.pallas_call(
        _attn_kernel,
        grid=(bsz // nseq, nb),
        in_specs=[pl.BlockSpec(memory_space=pltpu.SMEM),
                  cur(D_ATT), prev(D_KV), cur(D_KV), prev(D_KV), cur(D_KV),
                  pl.BlockSpec((1,) + bias.shape[1:], lambda b, i: (jnp.minimum(i, 1), 0, 0, 0)),
                  pl.BlockSpec(ng.shape, lambda b, i: (0, 0))],
        out_specs=cur(D_ATT),
        out_shape=jax.ShapeDtypeStruct((bsz, l, D_ATT), BF16),
        scratch_shapes=[pltpu.VMEM((nseq, ATT_BLOCK, D_ATT), F32)],
        compiler_params=pltpu.CompilerParams(dimension_semantics=("arbitrary", "arbitrary"),
                                             vmem_limit_bytes=VMEM_LIMIT),
    )(sinks, q, k, k, v, v, bias, ng)


def _stage_bf16(staged, wsem):
    copies = [pltpu.make_async_copy(src.at[0], raw, wsem.at[j]) for j, (src, raw, _) in enumerate(staged)]
    for cp in copies:
        cp.start()
    for cp, (_, raw, dst) in zip(copies, staged):
        cp.wait()
        dst[...] = raw[...].astype(BF16)


def _out_proj_kernel(x_ref, ys_ref, ya_ref, g1_ref, sc_ref, sh_ref, ng_ref, wo_hbm, rwh_ref, rwl_ref, rb_ref,
                     upper_ref, carry_in_ref,
                     x1_ref, h_ref, idx_ref, gate_ref, rank_ref, slot_ref, cnt_ref,
                     carry_ref, wo_raw, wo_ref, wsem, *, cap):
    @pl.when(pl.program_id(0) == 0)
    def _():
        carry_ref[...] = carry_in_ref[...]
        _stage_bf16(((wo_hbm, wo_raw, wo_ref),), wsem)

    mix = (jnp.dot(ys_ref[...], wo_ref[:D_SSM, :], preferred_element_type=F32)
           + jnp.dot(ya_ref[...], wo_ref[D_SSM:, :], preferred_element_type=F32))
    x1 = x_ref[...] + g1_ref[0] * mix
    ms = jnp.mean(x1 * x1, axis=-1, keepdims=True)
    h = x1 * lax.rsqrt(ms + EPS) * ng_ref[...]
    h = h * (1.0 + sc_ref[0]) + sh_ref[0]
    half = h.shape[1] // 2
    h_ref[...] = _pack_bf16_pair(h[:, :half], h[:, half:])
    hi, lo = _split_hi_lo(h)
    hi_terms = jnp.dot(hi, rwl_ref[...], preferred_element_type=F32)
    logits = (hi_terms[:, :LANES] + hi_terms[:, LANES:]
              + jnp.dot(lo, rwh_ref[...], preferred_element_type=F32))
    _route_tokens(logits.T[0:N_EXPERTS, :], rb_ref, upper_ref, idx_ref, gate_ref, rank_ref, cnt_ref, carry_ref)
    slot_ref[...] = idx_ref[...] * cap + rank_ref[...]
    x1_ref[...] = x1


def _out_proj_call(x2, ys, ya, g1, sc2, sh2, ng, w_out, rwh, rwl, router_bias, upper, carry_in,
                   tiles_per_batch, tm, tile0, ntiles, cap):
    n, d = x2.shape
    nt = ntiles * tm
    row = lambda w: pl.BlockSpec((tm, w), lambda i: (i + tile0, 0))
    own = lambda w: pl.BlockSpec((tm, w), lambda i: (i, 0))
    tok = lambda r: pl.BlockSpec((r, tm), lambda i: (0, i))
    full = lambda a: pl.BlockSpec(a.shape, lambda i: (0, 0))
    per_batch = pl.BlockSpec((1, 1, d), lambda i: ((i + tile0) // tiles_per_batch, 0, 0))
    hbm = pl.BlockSpec(memory_space=pl.ANY)
    staged = (w_out,)
    return pl.pallas_call(
        functools.partial(_out_proj_kernel, cap=cap),
        grid=(ntiles,),
        in_specs=[row(d), row(D_SSM), row(D_ATT), per_batch, per_batch, per_batch, full(ng),
                  hbm, full(rwh), full(rwl), full(router_bias), full(upper), full(carry_in)],
        out_specs=[row(d), own(d // 2), tok(TOP_K), tok(TOP_K), tok(TOP_K), tok(TOP_K),
                   pl.BlockSpec((N_EXPERTS, LANES), lambda i: (0, 0))],
        out_shape=[jax.ShapeDtypeStruct((n, d), F32), jax.ShapeDtypeStruct((nt, d // 2), jnp.uint32),
                   jax.ShapeDtypeStruct((TOP_K, nt), jnp.int32), jax.ShapeDtypeStruct((TOP_K, nt), F32),
                   jax.ShapeDtypeStruct((TOP_K, nt), jnp.int32), jax.ShapeDtypeStruct((TOP_K, nt), jnp.int32),
                   jax.ShapeDtypeStruct((N_EXPERTS, LANES), F32)],
        scratch_shapes=([pltpu.VMEM((N_EXPERTS, LANES), F32)]
                        + [pltpu.VMEM(a.shape[1:], a.dtype) for a in staged]
                        + [pltpu.VMEM(a.shape[1:], BF16) for a in staged]
                        + [pltpu.SemaphoreType.DMA((len(staged),))]),
        input_output_aliases={0: 0},
        compiler_params=pltpu.CompilerParams(dimension_semantics=("arbitrary",),
                                             vmem_limit_bytes=VMEM_LIMIT),
    )(x2, ys, ya, g1, sc2, sh2, ng, w_out, rwh, rwl, router_bias, upper, carry_in)


def _route_tokens(logits_t, rb_ref, upper_ref, idx_ref, gate_ref, rank_ref, cnt_ref, carry_ref):
    t = logits_t.shape[1]
    per_group = N_EXPERTS // ROUTE_GROUPS
    scores = 1.0 / (1.0 + jnp.exp(-logits_t))
    sel = scores + rb_ref[...]
    e_iota = lax.broadcasted_iota(jnp.int32, (N_EXPERTS, t), 0)

    sel3 = sel.reshape(ROUTE_GROUPS, per_group, t)
    w_iota = lax.broadcasted_iota(jnp.int32, sel3.shape, 1)
    m1 = jnp.max(sel3, axis=1, keepdims=True)
    first = jnp.min(jnp.where(sel3 == m1, w_iota, per_group), axis=1, keepdims=True)
    m2 = jnp.max(jnp.where(w_iota == first, NEG_INF, sel3), axis=1, keepdims=True)
    grp = (m1 + m2).reshape(ROUTE_GROUPS, t)

    g_iota = lax.broadcasted_iota(jnp.int32, (ROUTE_GROUPS, t), 0)
    gmask = jnp.zeros((ROUTE_GROUPS, t), jnp.bool_)
    for _ in range(ROUTE_TOPK_GROUPS):
        gm = jnp.max(grp, axis=0, keepdims=True)
        gfirst = jnp.min(jnp.where(grp == gm, g_iota, ROUTE_GROUPS), axis=0, keepdims=True)
        hit = g_iota == gfirst
        gmask = jnp.logical_or(gmask, hit)
        grp = jnp.where(hit, NEG_INF, grp)
    allowed = jnp.broadcast_to(gmask.reshape(ROUTE_GROUPS, 1, t),
                               (ROUTE_GROUPS, per_group, t)).reshape(N_EXPERTS, t)
    masked = jnp.where(allowed, sel, NEG_INF)

    picked = jnp.zeros((N_EXPERTS, t), jnp.bool_)
    idx_rows = []
    w_rows = []
    for _ in range(TOP_K):
        mm = jnp.max(masked, axis=0, keepdims=True)
        efirst = jnp.min(jnp.where(masked == mm, e_iota, N_EXPERTS), axis=0, keepdims=True)
        hit = e_iota == efirst
        idx_rows.append(efirst)
        w_rows.append(jnp.sum(jnp.where(hit, scores, 0.0), axis=0, keepdims=True))
        picked = jnp.logical_or(picked, hit)
        masked = jnp.where(hit, NEG_INF, masked)
    idx = jnp.concatenate(idx_rows, axis=0)
    w = jnp.concatenate(w_rows, axis=0)
    gate_ref[...] = w / jnp.sum(w, axis=0, keepdims=True) * ROUTED_SCALE
    idx_ref[...] = idx

    onehot = jnp.where(picked, 1.0, 0.0)
    sub = upper_ref.shape[0]
    carry = carry_ref[:, 0:1]
    parts = []
    for s0 in range(0, t, sub):
        oh = onehot[:, s0:s0 + sub]
        parts.append(jnp.dot(oh.astype(BF16), upper_ref[...], preferred_element_type=F32) + carry)
        carry = carry + jnp.sum(oh, axis=1, keepdims=True)
    rank_full = jnp.concatenate(parts, axis=1)
    rank_rows = [jnp.sum(jnp.where(e_iota == idx_rows[k], rank_full, 0.0), axis=0, keepdims=True)
                 for k in range(TOP_K)]
    rank_ref[...] = jnp.concatenate(rank_rows, axis=0).astype(jnp.int32)
    carry_ref[...] = jnp.broadcast_to(carry, carry_ref.shape)
    cnt_ref[...] = carry_ref[...]


PLAN_EXPERT, PLAN_FRESH, PLAN_VALID, PLAN_SEG, PLAN_NEXT, PLAN_NUSED, PLAN_XROW = range(7)
PLAN_ROWS = SUBLANES


def _plan_kernel(cnt_ref, tri_ref, start_ref, plan_ref, *, nblocks, cap):
    nbp = plan_ref.shape[1]
    cnt = cnt_ref[...].astype(jnp.int32)
    blocks = (cnt + (EXPERT_ROWS - 1)) // EXPERT_ROWS
    end = jnp.dot(tri_ref[...], blocks.astype(F32), precision=lax.Precision.HIGHEST,
                  preferred_element_type=F32).astype(jnp.int32)
    start = end - blocks
    start_ref[...] = start[:, 0:1] * EXPERT_ROWS
    nused = end[N_EXPERTS - 1:N_EXPERTS, 0:1]

    e_iota = lax.broadcasted_iota(jnp.int32, (N_EXPERTS, nbp), 0)
    blk = lax.broadcasted_iota(jnp.int32, (1, nbp), 1)
    expert = jnp.minimum(jnp.sum((end[:, 0:1] <= blk).astype(jnp.int32), axis=0, keepdims=True), N_EXPERTS - 1)
    mine = e_iota == expert
    pick = lambda col: jnp.sum(jnp.where(mine, col, 0), axis=0, keepdims=True)
    first = pick(start[:, 0:1])
    valid = jnp.clip(pick(cnt[:, 0:1]) - (blk - first) * EXPERT_ROWS, 0, EXPERT_ROWS)
    present = jnp.logical_or(blocks[:, 0:1] > 0,
                             jnp.logical_and(e_iota[:, 0:1] == N_EXPERTS - 1, nused < nblocks))
    seg = jnp.sum(jnp.logical_and(present, e_iota <= expert).astype(jnp.int32), axis=0, keepdims=True) - 1
    nxt = jnp.min(jnp.where(jnp.logical_and(present, e_iota > expert), e_iota, N_EXPERTS), axis=0, keepdims=True)
    rows = {PLAN_EXPERT: expert, PLAN_FRESH: (blk == first).astype(jnp.int32), PLAN_VALID: valid, PLAN_SEG: seg,
            PLAN_NEXT: jnp.where(nxt == N_EXPERTS, -1, nxt), PLAN_NUSED: jnp.broadcast_to(nused, (1, nbp)),
            PLAN_XROW: expert * cap + (blk - first) * EXPERT_ROWS}
    zero = jnp.zeros((1, nbp), jnp.int32)
    plan_ref[...] = jnp.concatenate([rows.get(r, zero) for r in range(PLAN_ROWS)], axis=0)


def _plan_call(counts, nblocks, cap):
    nbp = -(-nblocks // LANES) * LANES
    tri = jnp.asarray(np.tril(np.ones((N_EXPERTS, N_EXPERTS), np.float32)))
    return pl.pallas_call(
        functools.partial(_plan_kernel, nblocks=nblocks, cap=cap),
        out_shape=[jax.ShapeDtypeStruct((N_EXPERTS, 1), jnp.int32),
                   jax.ShapeDtypeStruct((PLAN_ROWS, nbp), jnp.int32)],
    )(counts, tri)


def _dest_kernel(idx_ref, rank_ref, start_ref, dest_ref):
    t = idx_ref.shape[1]
    e_iota = lax.broadcasted_iota(jnp.int32, (N_EXPERTS, t), 0)
    rows = [jnp.sum(jnp.where(e_iota == idx_ref[k:k + 1, :], start_ref[...], 0), axis=0, keepdims=True)
            for k in range(TOP_K)]
    dest = jnp.concatenate(rows, axis=0) + rank_ref[...]
    for k in range(TOP_K):
        for c in range(t // LANES):
            dest_ref[k, c:c + 1, :] = dest[k:k + 1, c * LANES:(c + 1) * LANES]


def _dest_call(idx, rank, pad_start, tile):
    n = idx.shape[1]
    tok = pl.BlockSpec((TOP_K, tile), lambda i: (0, i))
    return pl.pallas_call(
        _dest_kernel,
        grid=(n // tile,),
        in_specs=[tok, tok, pl.BlockSpec((N_EXPERTS, 1), lambda i: (0, 0))],
        out_specs=pl.BlockSpec((TOP_K, tile // LANES, LANES), lambda i: (0, i, 0)),
        out_shape=jax.ShapeDtypeStruct((TOP_K, n // LANES, LANES), jnp.int32),
        compiler_params=pltpu.CompilerParams(dimension_semantics=("arbitrary",)),
    )(idx, rank, pad_start)


def _scatter_rows_sc(rows, dest_flat, into, chunk):
    n, w = rows.shape
    copies = dest_flat.shape[0] // n
    info = plsc.get_sparse_core_info()
    nc = info.num_cores
    per_worker = n // (nc * info.num_subcores)
    assert per_worker * nc * info.num_subcores == n and per_worker % chunk == 0
    mesh = plsc.VectorSubcoreMesh(core_axis_name="c", subcore_axis_name="s")
    scratch_types = ([pltpu.VMEM((chunk,), jnp.int32) for _ in range(copies)]
                     + [pltpu.VMEM((chunk, w), rows.dtype), pltpu.SemaphoreType.DMA, pltpu.SemaphoreType.DMA])

    def scatter(rows_hbm, idx_hbm, out_hbm, *scratch):
        idx_vs, (rows_v, isem, sem) = scratch[:copies], scratch[copies:]
        worker = lax.axis_index("s") * nc + lax.axis_index("c")

        @pl.loop(0, per_worker // chunk)
        def _(j):
            base = worker * per_worker + j * chunk
            loads = [pltpu.async_copy(idx_hbm.at[pl.ds(k * n + base, chunk)], idx_vs[k], isem)
                     for k in range(copies)]
            pltpu.sync_copy(rows_hbm.at[pl.ds(base, chunk)], rows_v)
            for ld in loads:
                ld.wait()
            stores = [pltpu.async_copy(rows_v, out_hbm.at[idx_vs[k]], sem) for k in range(copies)]
            for st in stores:
                st.wait()

    if isinstance(into, int):
        return pl.kernel(scatter, mesh=mesh, out_type=jax.ShapeDtypeStruct((into, w), rows.dtype),
                         scratch_types=scratch_types)(rows, dest_flat)
    out = jax.new_ref(into)
    pl.kernel(scatter, mesh=mesh, out_type=(), scratch_types=scratch_types)(rows, dest_flat, out)
    return jax.freeze(out)


EXPERT_LIVE_STEP = 256
X_RING = 4


def _expert_kernel(plan_ref, xs_hbm, wg_hbm, wu_hbm, wd_hbm, y_ref,
                   wgb_ref, wub_ref, wdb_ref, xbuf_ref, xsem, wg_raw, wu_raw, wd_raw, wsem):
    i = pl.program_id(0)
    rows, half = xbuf_ref.shape[1], xbuf_ref.shape[2]
    nused = plan_ref[PLAN_NUSED, 0]

    def w_copies(expert, slot):
        return [pltpu.make_async_copy(src.at[expert], dst.at[slot], wsem.at[slot])
                for src, dst in ((wg_hbm, wg_raw), (wu_hbm, wu_raw), (wd_hbm, wd_raw))]

    def x_copy(block):
        first = pl.multiple_of(plan_ref[PLAN_XROW, block], rows)
        slot = block % X_RING
        return pltpu.make_async_copy(xs_hbm.at[pl.ds(first, rows)], xbuf_ref.at[slot], xsem.at[slot])

    @pl.when(i == 0)
    def _():
        for b in range(X_RING - 1):
            @pl.when(b < nused)
            def _():
                x_copy(b).start()

    @pl.when(i + (X_RING - 1) < nused)
    def _():
        x_copy(i + (X_RING - 1)).start()

    @pl.when(i == 0)
    def _():
        for c in w_copies(plan_ref[PLAN_EXPERT, 0], 0):
            c.start()

    @pl.when(plan_ref[PLAN_FRESH, i] > 0)
    def _():
        slot = plan_ref[PLAN_SEG, i] % 2
        for c in w_copies(plan_ref[PLAN_EXPERT, i], slot):
            c.wait()

        @pl.when(plan_ref[PLAN_NEXT, i] >= 0)
        def _():
            for c in w_copies(plan_ref[PLAN_NEXT, i], 1 - slot):
                c.start()

        wgb_ref[...] = wg_raw[slot].astype(BF16)
        wub_ref[...] = wu_raw[slot].astype(BF16)
        wdb_ref[...] = wd_raw[slot].astype(BF16)

    @pl.when(i < nused)
    def _():
        x_copy(i).wait()

    real = plan_ref[PLAN_VALID, i]
    slot = i % X_RING

    def ffn(live):
        xw = xbuf_ref[slot, 0:live, :]
        row = lax.broadcasted_iota(jnp.int32, xw.shape, 0)
        x_lo, x_hi = _unpack_bf16_pair(jnp.where(row < real, xw, jnp.uint32(0)))
        x_lo = x_lo.astype(BF16)
        x_hi = x_hi.astype(BF16)
        gate = (jnp.dot(x_lo, wgb_ref[:half, :], preferred_element_type=F32)
                + jnp.dot(x_hi, wgb_ref[half:, :], preferred_element_type=F32))
        up = (jnp.dot(x_lo, wub_ref[:half, :], preferred_element_type=F32)
              + jnp.dot(x_hi, wub_ref[half:, :], preferred_element_type=F32))
        u = (_silu(gate) * up).astype(BF16)
        y_lo = jnp.dot(u, wdb_ref[:, :half], preferred_element_type=F32)
        y_hi = jnp.dot(u, wdb_ref[:, half:], preferred_element_type=F32)
        y_ref[0:live, :] = _pack_bf16_pair(y_lo, y_hi)
        if live < rows:
            y_ref[live:, :] = jnp.zeros((rows - live, half), y_ref.dtype)

    pieces = (real + (EXPERT_LIVE_STEP - 1)) // EXPERT_LIVE_STEP
    for p in range(1, rows // EXPERT_LIVE_STEP + 1):
        @pl.when(jnp.logical_and(i < nused, pieces == p))
        def _():
            ffn(p * EXPERT_LIVE_STEP)

    @pl.when(i >= nused)
    def _():
        y_ref[...] = jnp.zeros_like(y_ref)


def _expert_call(plan, xs, wg, wu, wd, rows, nblocks):
    p, w = nblocks * rows, xs.shape[1]
    d, f = wg.shape[1], wg.shape[2]
    hbm = pl.BlockSpec(memory_space=pl.ANY)
    grid_spec = pltpu.PrefetchScalarGridSpec(
        num_scalar_prefetch=1,
        grid=(p // rows,),
        in_specs=[hbm, hbm, hbm, hbm],
        out_specs=pl.BlockSpec((rows, w), lambda i, *_: (i, 0)),
        scratch_shapes=[pltpu.VMEM((d, f), BF16), pltpu.VMEM((d, f), BF16), pltpu.VMEM((f, d), BF16),
                        pltpu.VMEM((X_RING, rows, w), xs.dtype), pltpu.SemaphoreType.DMA((X_RING,)),
                        pltpu.VMEM((2, d, f), wg.dtype), pltpu.VMEM((2, d, f), wu.dtype),
                        pltpu.VMEM((2, f, d), wd.dtype), pltpu.SemaphoreType.DMA((2,))],
    )
    return pl.pallas_call(
        _expert_kernel,
        grid_spec=grid_spec,
        out_shape=jax.ShapeDtypeStruct((p, w), jnp.uint32),
        compiler_params=pltpu.CompilerParams(dimension_semantics=("arbitrary",),
                                             vmem_limit_bytes=VMEM_LIMIT),
    )(plan, xs, wg, wu, wd)


def _gather_rows_sc(table, idx, chunk):
    m = idx.shape[0]
    w = table.shape[1]
    info = plsc.get_sparse_core_info()
    nc = info.num_cores
    per_worker = m // (nc * info.num_subcores)
    assert per_worker * nc * info.num_subcores == m and per_worker % chunk == 0
    part = chunk // SC_GATHER_PARTS
    mesh = plsc.VectorSubcoreMesh(core_axis_name="c", subcore_axis_name="s")

    @functools.partial(
        pl.kernel, mesh=mesh,
        out_type=jax.ShapeDtypeStruct((m, w), table.dtype),
        scratch_types=([pltpu.VMEM((part,), jnp.int32) for _ in range(SC_GATHER_PARTS)]
                       + [pltpu.VMEM((part, w), table.dtype) for _ in range(SC_GATHER_PARTS)]
                       + [pltpu.SemaphoreType.DMA] * 3),
    )
    def gather(table_hbm, idx_hbm, out_hbm, *scratch):
        idx_vs = scratch[:SC_GATHER_PARTS]
        rows_vs = scratch[SC_GATHER_PARTS:2 * SC_GATHER_PARTS]
        isem, gsem, wsem = scratch[2 * SC_GATHER_PARTS:]
        worker = lax.axis_index("s") * nc + lax.axis_index("c")

        @pl.loop(0, per_worker // chunk)
        def _(j):
            base = worker * per_worker + j * chunk
            loads = [pltpu.async_copy(idx_hbm.at[pl.ds(base + p * part, part)], idx_vs[p], isem)
                     for p in range(SC_GATHER_PARTS)]
            gathers = []
            for p in range(SC_GATHER_PARTS):
                loads[p].wait()
                gathers.append(pltpu.async_copy(table_hbm.at[idx_vs[p]], rows_vs[p], gsem))
            writes = []
            for p in range(SC_GATHER_PARTS):
                gathers[p].wait()
                writes.append(pltpu.async_copy(rows_vs[p], out_hbm.at[pl.ds(base + p * part, part)], wsem))
            for wr in writes:
                wr.wait()

    return gather(table, idx)


def _combine_kernel(yk_ref, gate_ref, base_ref, h_ref, g2_ref, fg_ref, sg_hbm, su_hbm, sd_hbm, o_ref,
                    sg_raw, su_raw, sd_raw, sg_ref, su_ref, sd_ref, wsem):
    @pl.when(pl.program_id(0) == 0)
    def _():
        _stage_bf16(((sg_hbm, sg_raw, sg_ref), (su_hbm, su_raw, su_ref), (sd_hbm, sd_raw, sd_ref)), wsem)

    t = base_ref.shape[0]
    half = yk_ref.shape[2]
    h_lo, h_hi = _unpack_bf16_pair(h_ref[...])
    hb = jnp.concatenate([h_lo.astype(BF16), h_hi.astype(BF16)], axis=1)
    u = _silu(jnp.dot(hb, sg_ref[...], preferred_element_type=F32)) * jnp.dot(hb, su_ref[...],
                                                                              preferred_element_type=F32)
    shared = jnp.dot(u.astype(BF16), sd_ref[...], preferred_element_type=F32)
    gates = jnp.concatenate([gate_ref[...], jnp.zeros((LANES - TOP_K, t), F32)], axis=0).T
    r_lo = shared[:, :half]
    r_hi = shared[:, half:]
    for k in range(TOP_K):
        y_lo, y_hi = _unpack_bf16_pair(yk_ref[k])
        r_lo = r_lo + gates[:, k:k + 1] * y_lo
        r_hi = r_hi + gates[:, k:k + 1] * y_hi
    g2 = g2_ref[0]
    x_lo = base_ref[:, :half] + g2[:, :half] * r_lo
    x_hi = base_ref[:, half:] + g2[:, half:] * r_hi
    ms = (jnp.sum(x_lo * x_lo, axis=-1, keepdims=True)
          + jnp.sum(x_hi * x_hi, axis=-1, keepdims=True)) * (1.0 / (2 * half))
    inv = lax.rsqrt(ms + EPS)
    o_ref[:, :half] = x_lo * inv * fg_ref[:, :half]
    o_ref[:, half:] = x_hi * inv * fg_ref[:, half:]


def _combine_call(yk, gates, base, h2, g2, fg, sg, su, sd, tiles_per_batch, tile, tile0):
    n, d = base.shape
    row = lambda w: pl.BlockSpec((tile, w), lambda i: (i + tile0, 0))
    hbm = pl.BlockSpec(memory_space=pl.ANY)
    staged = (sg, su, sd)
    return pl.pallas_call(
        _combine_kernel,
        grid=(yk.shape[1] // tile,),
        in_specs=[pl.BlockSpec((TOP_K, tile, yk.shape[2]), lambda i: (0, i, 0)),
                  pl.BlockSpec((TOP_K, tile), lambda i: (0, i)), row(d),
                  pl.BlockSpec((tile, h2.shape[1]), lambda i: (i, 0)),
                  pl.BlockSpec((1, 1, d), lambda i: ((i + tile0) // tiles_per_batch, 0, 0)),
                  pl.BlockSpec((1, d), lambda i: (0, 0)), hbm, hbm, hbm],
        out_specs=row(d),
        out_shape=jax.ShapeDtypeStruct((n, d), F32),
        scratch_shapes=([pltpu.VMEM(a.shape[1:], a.dtype) for a in staged]
                        + [pltpu.VMEM(a.shape[1:], BF16) for a in staged]
                        + [pltpu.SemaphoreType.DMA((len(staged),))]),
        input_output_aliases={2: 0},
        compiler_params=pltpu.CompilerParams(dimension_semantics=("arbitrary",),
                                             vmem_limit_bytes=VMEM_LIMIT),
    )(yk, gates, base, h2, g2, fg, sg, su, sd)


def _pad_cols(a, width):
    return jnp.pad(a, ((0, 0), (0, width - a.shape[1])))


def _layer(x, mod, norm1_g, norm2_g, w_in, conv_w, conv_b, dt_bias, a_log, d_skip, ssm_norm_g,
           att_norm_g, sinks, rel_bias, w_out, router_w, router_bias, exp_w_gate, exp_w_up, exp_w_down,
           sh_w_gate, sh_w_up, sh_w_down, final_g):
    bsz, l, d = x.shape
    n = bsz * l
    tm = min(ROW_TILE, l)

    sh1, sc1, g1, sh2, sc2, g2 = [m[:, None, :] for m in jnp.split(mod, 6, axis=-1)]

    assert math.frexp(ATT_HEAD_DIM ** -0.5)[0] == 0.5
    x2 = x.reshape(n, d)
    z, xbc, dt, q, k, v, x1 = _in_proj_call(x2, sc1, sh1, norm1_g[None, :], jnp.swapaxes(w_in, 1, 2), l // tm, tm)

    triu = jnp.asarray(np.triu(np.ones((CHUNK, CHUNK), np.float32))).astype(BF16)
    shift = jnp.asarray(_conv_shift_matrix()).astype(BF16)
    y_ssm = _ssd_call(xbc.reshape(bsz, l, CONV_CH), z.reshape(bsz, l, D_SSM), dt.reshape(bsz, l, LANES),
                      conv_w, conv_b[None, :], dt_bias[:, None], a_log[:, None],
                      jnp.repeat(d_skip, SSM_HEAD_DIM)[None, :], ssm_norm_g[None, :], triu, shift)

    bias = _bias_call(rel_bias, jnp.asarray(_rel_bucket_table()))
    y_att = _attn_call(sinks, q.reshape(bsz, l, D_ATT), k.reshape(bsz, l, D_KV), v.reshape(bsz, l, D_KV), bias,
                       att_norm_g[None, :])

    rw = _pad_cols(router_w, LANES)
    rwh = rw.astype(BF16)
    rwl = jnp.concatenate([rwh, (rw - rwh.astype(F32)).astype(BF16)], axis=1)
    rs = min(RANK_SUB, tm)
    upper = jnp.asarray(np.triu(np.ones((rs, rs), np.float32), 1)).astype(BF16)
    groups = COMBINE_GROUPS if bsz % COMBINE_GROUPS == 0 else 1
    ng = n // groups

    cap = -(-n // EXPERT_ROWS) * EXPERT_ROWS
    counts = jnp.zeros((N_EXPERTS, LANES), F32)
    xs, parts = N_EXPERTS * cap, []
    for g in range(groups):
        x1, h2, idx, gates, rank, slot, counts = _out_proj_call(
            x1, y_ssm.reshape(n, D_SSM), y_att.reshape(n, D_ATT), g1, sc2, sh2, norm2_g[None, :],
            w_out, rwh, rwl, router_bias[:, None], upper, counts, l // tm, tm, g * ng // tm, ng // tm, cap)
        xs = _scatter_rows_sc(h2, slot.reshape(-1), xs, SC_CHUNK)
        parts.append((h2, idx, gates, rank))
    idx = jnp.concatenate([p[1] for p in parts], axis=1)
    rank = jnp.concatenate([p[3] for p in parts], axis=1)
    rt = min(ROUTE_TILE, n)

    nblocks = (n * TOP_K + N_EXPERTS * (EXPERT_ROWS - 1) + EXPERT_ROWS - 1) // EXPERT_ROWS
    pad_start, plan = _plan_call(counts, nblocks, cap)
    dest = _dest_call(idx, rank, pad_start, rt)

    ys = _expert_call(plan, xs, exp_w_gate, exp_w_up, exp_w_down, EXPERT_ROWS, nblocks)
    ctile = min(COMBINE_TILE, l)
    out = x1
    for g in range(groups):
        idx_g = dest[:, g * ng // LANES:(g + 1) * ng // LANES, :].reshape(-1)
        yk = _gather_rows_sc(ys, idx_g, SC_CHUNK).reshape(TOP_K, ng, ys.shape[1])
        out = _combine_call(yk, parts[g][2], out, parts[g][0], g2, final_g[None, :], sh_w_gate, sh_w_up,
                            sh_w_down, l // ctile, ctile, g * ng // ctile)
    return out.reshape(bsz, l, d)


def kernel(x, c, mod_w, mod_b, norm1_g, norm2_g, w_in, conv_w, conv_b, dt_bias, a_log, d_skip, ssm_norm_g,
           att_norm_g, sinks, rel_bias, w_out, router_w, router_bias, exp_w_gate, exp_w_up, exp_w_down,
           sh_w_gate, sh_w_up, sh_w_down, final_g):
    assert mod_w.shape[0] == 1, "single-layer block"
    bsz = x.shape[0]
    c_pad = jnp.pad(c, ((0, SUBLANES - bsz % SUBLANES if bsz % SUBLANES else 0), (0, 0)))
    mod = _mod_call(c_pad, mod_w[0], mod_b[0][None, :])[:bsz]
    return _layer(x, mod, norm1_g[0], norm2_g[0], w_in, conv_w[0], conv_b[0], dt_bias[0], a_log[0], d_skip[0],
                  ssm_norm_g[0], att_norm_g[0], sinks[0], rel_bias, w_out, router_w[0], router_bias[0],
                  exp_w_gate[0], exp_w_up[0], exp_w_down[0], sh_w_gate, sh_w_up, sh_w_down, final_g)
```

```python
import functools
import math

import numpy as np
import jax
import jax.numpy as jnp
from jax import lax
from jax.experimental import pallas as pl
from jax.experimental.pallas import tpu as pltpu
from jax.experimental.pallas import tpu_sc as plsc

F32 = jnp.float32
BF16 = jnp.bfloat16

D_MODEL = 1024
SSM_HEAD_DIM = 64
D_SSM = D_MODEL
SSM_HEADS = D_SSM // SSM_HEAD_DIM
SSM_GROUPS = 4
D_STATE = 128
CONV_K = 4
CONV_CH = D_SSM + 2 * SSM_GROUPS * D_STATE
CHUNK = 128
ATT_HEAD_DIM = 64
D_ATT = D_MODEL
ATT_HEADS = D_ATT // ATT_HEAD_DIM
KV_HEADS = ATT_HEADS // 4
Q_PER_KV = ATT_HEADS // KV_HEADS
D_KV = KV_HEADS * ATT_HEAD_DIM
WINDOW = 128
ATT_BLOCK = 128
REL_BUCKETS = 32
REL_MAX_DIST = 128
N_EXPERTS = 64
TOP_K = 8
EXPERT_DIM = D_MODEL // 4
SHARED_DIM = D_MODEL // 4
ROUTE_GROUPS = 8
ROUTE_TOPK_GROUPS = 4
ROUTED_SCALE = 2.5
EPS = 1e-6

LANES = 128
SUBLANES = 8
HALF = LANES // 2

ROW_TILE = 512
ROUTE_TILE = 2048
RANK_SUB = 512
COMBINE_TILE = 512
SC_CHUNK = 128
SC_GATHER_PARTS = 4
EXPERT_ROWS = 1024
COMBINE_GROUPS = 4
VMEM_LIMIT = 48 * 1024 * 1024

NEG_INF = float("-inf")


def _silu(v):
    return v * (1.0 / (1.0 + jnp.exp(-v)))


def _softplus(v):
    return jnp.maximum(v, 0.0) + jnp.log(1.0 + jnp.exp(-jnp.abs(v)))


def _split_hi_lo(v):
    hi = v.astype(BF16)
    lo = (v - hi.astype(F32)).astype(BF16)
    return hi, lo


def _pack_bf16_pair(a, b):
    w = pltpu.pack_elementwise([a, b], packed_dtype=BF16)
    return w if w.dtype == jnp.uint32 else lax.bitcast_convert_type(w, jnp.uint32)


def _unpack_bf16_pair(w):
    a = pltpu.unpack_elementwise(w, index=0, packed_dtype=BF16, unpacked_dtype=F32)
    b = pltpu.unpack_elementwise(w, index=1, packed_dtype=BF16, unpacked_dtype=F32)
    return a, b


def _lane_half_mask(shape):
    return lax.broadcasted_iota(jnp.int32, shape, len(shape) - 1) < HALF


def _mod_kernel(c_ref, w_ref, b_ref, o_ref):
    a = _silu(c_ref[...])
    o_ref[...] = jnp.dot(a, w_ref[...], precision=lax.Precision.HIGHEST,
                         preferred_element_type=F32) + b_ref[...]


def _mod_call(c_pad, mod_w, mod_b):
    rows, d = c_pad.shape
    cols = mod_w.shape[1]
    return pl.pallas_call(
        _mod_kernel,
        grid=(cols // d,),
        in_specs=[pl.BlockSpec((rows, d), lambda j: (0, 0)),
                  pl.BlockSpec((d, d), lambda j: (0, j)),
                  pl.BlockSpec((1, d), lambda j: (0, j))],
        out_specs=pl.BlockSpec((rows, d), lambda j: (0, j)),
        out_shape=jax.ShapeDtypeStruct((rows, cols), F32),
        compiler_params=pltpu.CompilerParams(dimension_semantics=("arbitrary",),
                                             vmem_limit_bytes=VMEM_LIMIT),
    )(c_pad, mod_w, mod_b)


IN_PROJ_SEGMENTS = ((D_SSM, BF16), (CONV_CH, BF16), (LANES, F32), (D_ATT, BF16), (D_KV, BF16), (D_KV, BF16))


def _in_proj_kernel(x_ref, sc_ref, sh_ref, g_ref, w_hbm, *refs):
    out_refs = refs[:len(IN_PROJ_SEGMENTS)]
    res_ref, wraw_ref, w_ref, wsem = refs[len(IN_PROJ_SEGMENTS):]

    @pl.when(pl.program_id(0) == 0)
    def _():
        cp = pltpu.make_async_copy(w_hbm.at[0], wraw_ref, wsem)
        cp.start()
        cp.wait()
        src_dt = D_SSM + CONV_CH
        dst_q = src_dt + LANES
        src_q = src_dt + SSM_HEADS
        q_scale = ATT_HEAD_DIM ** -0.5

        def put(dst, src, scale=None):
            t = wraw_ref[src:src + LANES, :].T
            w_ref[:, dst:dst + LANES] = (t if scale is None else t * scale).astype(BF16)

        for c0 in range(0, src_dt, LANES):
            put(c0, c0)
        dt_tile = wraw_ref[src_dt:src_dt + LANES, :].T
        lane = lax.broadcasted_iota(jnp.int32, dt_tile.shape, 1)
        w_ref[:, src_dt:dst_q] = jnp.where(lane < SSM_HEADS, dt_tile, 0.0).astype(BF16)
        for c0 in range(0, D_ATT, LANES):
            put(dst_q + c0, src_q + c0, q_scale)
        for c0 in range(D_ATT, D_ATT + 2 * D_KV, LANES):
            put(dst_q + c0, src_q + c0)

    xf = x_ref[...]
    res_ref[...] = xf
    ms = jnp.mean(xf * xf, axis=-1, keepdims=True)
    h = xf * lax.rsqrt(ms + EPS) * g_ref[...]
    h = h * (1.0 + sc_ref[0]) + sh_ref[0]
    hb = h.astype(BF16)
    col = 0
    for (width, dtype), o_ref in zip(IN_PROJ_SEGMENTS, out_refs):
        o_ref[...] = jnp.dot(hb, w_ref[:, col:col + width], preferred_element_type=F32).astype(dtype)
        col += width


def _in_proj_call(x2, sc1, sh1, g1n, w_in_t, tiles_per_batch, tm):
    n, d = x2.shape
    row = lambda w: pl.BlockSpec((tm, w), lambda i: (i, 0))
    full = lambda a: pl.BlockSpec(a.shape, lambda i: (0, 0))
    per_batch = pl.BlockSpec((1, 1, d), lambda i: (i // tiles_per_batch, 0, 0))
    cols = sum(w for w, _ in IN_PROJ_SEGMENTS)
    w_in = w_in_t
    assert w_in.shape == (1, cols - (LANES - SSM_HEADS), d)
    return pl.pallas_call(
        _in_proj_kernel,
        grid=(n // tm,),
        in_specs=[row(d), per_batch, per_batch, full(g1n), pl.BlockSpec(memory_space=pl.ANY)],
        out_specs=[row(w) for w, _ in IN_PROJ_SEGMENTS] + [row(d)],
        out_shape=[jax.ShapeDtypeStruct((n, w), dt) for w, dt in IN_PROJ_SEGMENTS] + [jax.ShapeDtypeStruct((n, d), F32)],
        scratch_shapes=[pltpu.VMEM(w_in.shape[1:], w_in.dtype), pltpu.VMEM((d, cols), BF16), pltpu.SemaphoreType.DMA(())],
        compiler_params=pltpu.CompilerParams(dimension_semantics=("arbitrary",),
                                             vmem_limit_bytes=VMEM_LIMIT),
    )(x2, sc1, sh1, g1n, w_in)


SSD_SEQS = 4
CONV_HALO = 16


def _conv_shift_matrix():
    s = np.zeros((CONV_K * CHUNK, CONV_HALO + CHUNK), np.float32)
    for k in range(CONV_K):
        t = np.arange(CHUNK)
        s[k * CHUNK + t, CONV_HALO + t - (CONV_K - 1) + k] = 1.0
    return s


def _silu_tanh(v):
    hv = 0.5 * v
    return hv + hv * jnp.tanh(hv)


def _ssd_kernel(xbc_ref, z_ref, dt_ref, cw_ref, cb_ref, dtb_ref, alog_ref, dskip_ref, ng_ref, triu_ref, shift_ref,
                y_ref, state_ref, ucat_ref, ybuf_ref):
    nseq = xbc_ref.shape[0]

    @pl.when(pl.program_id(1) == 0)
    def _():
        state_ref[...] = jnp.zeros_like(state_ref)
        ucat_ref[:, 0:CONV_HALO, :] = jnp.zeros((nseq, CONV_HALO, CONV_CH), BF16)

    for q in range(nseq):
        _ssd_chunk(xbc_ref.at[q], z_ref.at[q], dt_ref.at[q], cw_ref, cb_ref, dtb_ref, alog_ref, dskip_ref, ng_ref,
                   triu_ref, shift_ref, y_ref.at[q], state_ref.at[q], ucat_ref.at[q], ybuf_ref.at[q])


def _ssd_chunk(xbc_ref, z_ref, dt_ref, cw_ref, cb_ref, dtb_ref, alog_ref, dskip_ref, ng_ref, triu_ref, shift_ref,
               y_ref, state_ref, ucat_ref, ybuf_ref):
    ucat_ref[CONV_HALO:, :] = xbc_ref[...]
    shifted = jnp.dot(shift_ref[...], ucat_ref[...], preferred_element_type=F32)
    ucat_ref[0:CONV_HALO, :] = ucat_ref[CHUNK:CHUNK + CONV_HALO, :]
    acc = cb_ref[...] + cw_ref[0:1, :] * shifted[0:CHUNK]
    for kk in range(1, CONV_K):
        acc = acc + cw_ref[kk:kk + 1, :] * shifted[kk * CHUNK:(kk + 1) * CHUNK]
    act = _silu_tanh(acc)
    xs = act[:, :D_SSM]
    gn = SSM_GROUPS * D_STATE

    dt_t = _softplus(dt_ref[...].T[0:SSM_HEADS, :] + dtb_ref[...])
    a_t = dt_t * (-jnp.exp(alog_ref[...]))
    a_hi = a_t.astype(BF16)
    a_mid = (a_t - a_hi.astype(F32)).astype(BF16)
    a_lo = (a_t - a_hi.astype(F32) - a_mid.astype(F32)).astype(BF16)
    triu = triu_ref[...]
    cs_t = (jnp.dot(a_hi, triu, preferred_element_type=F32) + jnp.dot(a_mid, triu, preferred_element_type=F32)
            + jnp.dot(a_lo, triu, preferred_element_type=F32))
    cs_end = cs_t[:, CHUNK - 1:CHUNK]
    r_t = cs_t - jnp.log(dt_t)
    w_t = jnp.exp(cs_end - cs_t) * dt_t
    chunk_decay = jnp.exp(cs_end)
    cols = jnp.concatenate([cs_t, jnp.exp(cs_t), jnp.zeros((LANES - 2 * SSM_HEADS, CHUNK), F32)], axis=0).T

    li = lax.broadcasted_iota(jnp.int32, (CHUNK, CHUNK), 0)
    si = lax.broadcasted_iota(jnp.int32, (CHUNK, CHUNK), 1)
    causal = li >= si
    low = _lane_half_mask((CHUNK, LANES))
    low_row = _lane_half_mask((1, LANES))

    heads_per_group = SSM_HEADS // SSM_GROUPS
    for g in range(SSM_GROUPS):
        b_g = act[:, D_SSM + g * D_STATE:D_SSM + (g + 1) * D_STATE]
        c_g = act[:, D_SSM + gn + g * D_STATE:D_SSM + gn + (g + 1) * D_STATE]
        b_gb = b_g.astype(BF16)
        c_gb = c_g.astype(BF16)
        cb = lax.dot_general(c_gb, b_gb, (((1,), (1,)), ((), ())), preferred_element_type=F32)
        b_t = b_g.T
        for jp in range(heads_per_group // 2):
            j = g * (heads_per_group // 2) + jp
            lanes = slice(j * LANES, (j + 1) * LANES)
            xp = xs[:, lanes]
            xpb = xp.astype(BF16)
            ydiag = jnp.zeros((CHUNK, LANES), F32)
            snew = jnp.zeros((D_STATE, LANES), F32)
            for half in range(2):
                h = 2 * j + half
                diff = cols[:, h:h + 1] - r_t[h:h + 1, :]
                m = (cb * jnp.exp(jnp.where(causal, diff, NEG_INF))).astype(BF16)
                keep = low if half == 0 else jnp.logical_not(low)
                xh = jnp.where(keep, xpb, jnp.zeros_like(xpb))
                ydiag = ydiag + jnp.dot(m, xh, preferred_element_type=F32)
                snew = snew + jnp.dot((b_t * w_t[h:h + 1, :]).astype(BF16), xh, preferred_element_type=F32)
            s_in = state_ref[:, lanes]
            yoff = jnp.dot(c_gb, s_in.astype(BF16), preferred_element_type=F32)
            h0 = 2 * j
            e0 = SSM_HEADS + h0
            escale = jnp.where(low, cols[:, e0:e0 + 1], cols[:, e0 + 1:e0 + 2])
            cdec = jnp.where(low_row, chunk_decay[h0:h0 + 1, :], chunk_decay[h0 + 1:h0 + 2, :])
            ybuf_ref[:, lanes] = ydiag + yoff * escale + xp * dskip_ref[:, lanes]
            state_ref[:, lanes] = s_in * cdec + snew

    yz = ybuf_ref[...] * _silu_tanh(z_ref[...].astype(F32))
    gw = D_SSM // SSM_GROUPS
    for g in range(SSM_GROUPS):
        part = yz[:, g * gw:(g + 1) * gw]
        ms = jnp.mean(part * part, axis=-1, keepdims=True)
        y_ref[:, g * gw:(g + 1) * gw] = (part * lax.rsqrt(ms + EPS)
                                            * ng_ref[:, g * gw:(g + 1) * gw]).astype(BF16)


def _ssd_call(xbc, z, dt, conv_w, conv_b, dtb, alog, dskip, ng, triu, shift):
    bsz, l, _ = xbc.shape
    nc = l // CHUNK
    nseq = SSD_SEQS if bsz % SSD_SEQS == 0 else 1
    chunk = lambda w: pl.BlockSpec((nseq, CHUNK, w), lambda b, c: (b, c, 0))
    full = lambda a: pl.BlockSpec(a.shape, lambda b, c: (0, 0))
    return pl.pallas_call(
        _ssd_kernel,
        grid=(bsz // nseq, nc),
        in_specs=[chunk(CONV_CH), chunk(D_SSM), chunk(LANES), full(conv_w), full(conv_b), full(dtb),
                  full(alog), full(dskip), full(ng), full(triu), full(shift)],
        out_specs=chunk(D_SSM),
        out_shape=jax.ShapeDtypeStruct((bsz, l, D_SSM), BF16),
        scratch_shapes=[pltpu.VMEM((nseq, D_STATE, D_SSM), F32),
                        pltpu.VMEM((nseq, CONV_HALO + CHUNK, CONV_CH), BF16),
                        pltpu.VMEM((nseq, CHUNK, D_SSM), F32)],
        compiler_params=pltpu.CompilerParams(dimension_semantics=("arbitrary", "arbitrary"),
                                             vmem_limit_bytes=VMEM_LIMIT),
    )(xbc, z, dt, conv_w, conv_b, dtb, alog, dskip, ng, triu, shift)


assert WINDOW == ATT_BLOCK


def _rel_bucket_table():
    qi = np.arange(ATT_BLOCK)[:, None]
    c = np.arange(ATT_BLOCK)[None, :]
    dist = np.where(c > qi, qi + ATT_BLOCK - c, qi - c)
    max_exact = REL_BUCKETS // 2
    d = np.maximum(dist, 1).astype(np.float32)
    large = max_exact + (np.log(d / np.float32(max_exact)) / np.float32(math.log(REL_MAX_DIST / max_exact))
                         * np.float32(REL_BUCKETS - max_exact)).astype(np.int32)
    large = np.minimum(large, REL_BUCKETS - 1)
    return np.where(dist < max_exact, dist, large).astype(np.int32)


def _bias_kernel(rb_ref, bucket_ref, o_ref):
    bucket = bucket_ref[...]
    from_prev = (lax.broadcasted_iota(jnp.int32, bucket.shape, 1)
                 > lax.broadcasted_iota(jnp.int32, bucket.shape, 0))
    for h in range(ATT_HEADS):
        acc = jnp.zeros(bucket.shape, F32)
        for b in range(REL_BUCKETS):
            acc = jnp.where(bucket == b, rb_ref[b, h], acc)
        o_ref[1, h] = acc
        o_ref[0, h] = jnp.where(from_prev, NEG_INF, acc)


def _bias_call(rel_bias, bucket):
    return pl.pallas_call(
        _bias_kernel,
        in_specs=[pl.BlockSpec(memory_space=pltpu.SMEM), pl.BlockSpec(memory_space=pltpu.VMEM)],
        out_shape=jax.ShapeDtypeStruct((2, ATT_HEADS) + bucket.shape, F32),
    )(rel_bias, bucket)


ATT_SEQS = 4


def _attn_kernel(sink_ref, q_ref, kp_ref, kc_ref, vp_ref, vc_ref, bias_ref, ng_ref, o_ref, obuf_ref):
    for s in range(q_ref.shape[0]):
        _attn_block(sink_ref, q_ref.at[s], kp_ref.at[s], kc_ref.at[s], vp_ref.at[s], vc_ref.at[s], bias_ref.at[0],
                    ng_ref, o_ref.at[s], obuf_ref.at[s])


def _attn_block(sink_ref, q_ref, kp_ref, kc_ref, vp_ref, vc_ref, bias_ref, ng_ref, o_ref, obuf_ref):
    qi = lax.broadcasted_iota(jnp.int32, (ATT_BLOCK, ATT_BLOCK), 0)
    ci = lax.broadcasted_iota(jnp.int32, (ATT_BLOCK, ATT_BLOCK), 1)
    from_prev = ci > qi
    low = _lane_half_mask((ATT_BLOCK, LANES))

    def band_variants(prev_ref, cur_ref):
        out = []
        for cpair in range(KV_HEADS // 2):
            lanes = slice(cpair * LANES, (cpair + 1) * LANES)
            t = jnp.concatenate([prev_ref[:, lanes], cur_ref[:, lanes]], axis=0).astype(F32)
            out.append((t.astype(BF16), pltpu.roll(t, HALF, 1).astype(BF16)))
        return out

    k_band = band_variants(kp_ref, kc_ref)
    v_band = band_variants(vp_ref, vc_ref)
    nt = (((1,), (1,)), ((), ()))

    for j in range(ATT_HEADS // 2):
        qp = q_ref[:, j * LANES:(j + 1) * LANES]
        out_pair = jnp.zeros((ATT_BLOCK, LANES), F32)
        for half in range(2):
            h = 2 * j + half
            g = h // Q_PER_KV
            swapped = int((g % 2) != half)
            keep = low if half == 0 else jnp.logical_not(low)
            qh = jnp.where(keep, qp, jnp.zeros_like(qp))
            s_band = lax.dot_general(qh, k_band[g // 2][swapped], nt, preferred_element_type=F32)
            s = jnp.where(from_prev, s_band[:, :ATT_BLOCK], s_band[:, ATT_BLOCK:]) + bias_ref[h]
            sink = sink_ref[h]
            m = jnp.maximum(jnp.max(s, axis=-1, keepdims=True), sink)
            p = jnp.exp(s - m)
            denom = jnp.sum(p, axis=-1, keepdims=True) + jnp.exp(sink - m)
            p_band = jnp.concatenate([jnp.where(from_prev, p, 0.0), jnp.where(from_prev, 0.0, p)], axis=1)
            o = jnp.dot(p_band.astype(BF16), v_band[g // 2][swapped], preferred_element_type=F32) / denom
            out_pair = out_pair + jnp.where(keep, o, 0.0)
        obuf_ref[:, j * LANES:(j + 1) * LANES] = out_pair

    att = obuf_ref[...]
    ms = jnp.mean(att * att, axis=-1, keepdims=True)
    o_ref[...] = (att * lax.rsqrt(ms + EPS) * ng_ref[...]).astype(BF16)


def _attn_call(sinks, q, k, v, bias, ng):
    bsz, l, _ = q.shape
    nb = l // ATT_BLOCK
    nseq = ATT_SEQS if bsz % ATT_SEQS == 0 else 1
    cur = lambda w: pl.BlockSpec((nseq, ATT_BLOCK, w), lambda b, i: (b, i, 0))
    prev = lambda w: pl.BlockSpec((nseq, ATT_BLOCK, w), lambda b, i: (b, jnp.maximum(i - 1, 0), 0))
    return pl.pallas_call(
        _attn_kernel,
        grid=(bsz // nseq, nb),
        in_specs=[pl.BlockSpec(memory_space=pltpu.SMEM),
                  cur(D_ATT), prev(D_KV), cur(D_KV), prev(D_KV), cur(D_KV),
                  pl.BlockSpec((1,) + bias.shape[1:], lambda b, i: (jnp.minimum(i, 1), 0, 0, 0)),
                  pl.BlockSpec(ng.shape, lambda b, i: (0, 0))],
        out_specs=cur(D_ATT),
        out_shape=jax.ShapeDtypeStruct((bsz, l, D_ATT), BF16),
        scratch_shapes=[pltpu.VMEM((nseq, ATT_BLOCK, D_ATT), F32)],
        compiler_params=pltpu.CompilerParams(dimension_semantics=("arbitrary", "arbitrary"),
                                             vmem_limit_bytes=VMEM_LIMIT),
    )(sinks, q, k, k, v, v, bias, ng)


def _stage_bf16(staged, wsem):
    copies = [pltpu.make_async_copy(src.at[0], raw, wsem.at[j]) for j, (src, raw, _) in enumerate(staged)]
    for cp in copies:
        cp.start()
    for cp, (_, raw, dst) in zip(copies, staged):
        cp.wait()
        dst[...] = raw[...].astype(BF16)


def _out_proj_kernel(x_ref, ys_ref, ya_ref, g1_ref, sc_ref, sh_ref, ng_ref, wo_hbm, rwh_ref, rwl_ref, rb_ref,
                     upper_ref, carry_in_ref,
                     x1_ref, h_ref, idx_ref, gate_ref, rank_ref, slot_ref, cnt_ref,
                     carry_ref, wo_raw, wo_ref, wsem, *, cap):
    @pl.when(pl.program_id(0) == 0)
    def _():
        carry_ref[...] = carry_in_ref[...]
        _stage_bf16(((wo_hbm, wo_raw, wo_ref),), wsem)

    mix = (jnp.dot(ys_ref[...], wo_ref[:D_SSM, :], preferred_element_type=F32)
           + jnp.dot(ya_ref[...], wo_ref[D_SSM:, :], preferred_element_type=F32))
    x1 = x_ref[...] + g1_ref[0] * mix
    ms = jnp.mean(x1 * x1, axis=-1, keepdims=True)
    h = x1 * lax.rsqrt(ms + EPS) * ng_ref[...]
    h = h * (1.0 + sc_ref[0]) + sh_ref[0]
    half = h.shape[1] // 2
    h_ref[...] = _pack_bf16_pair(h[:, :half], h[:, half:])
    hi, lo = _split_hi_lo(h)
    hi_terms = jnp.dot(hi, rwl_ref[...], preferred_element_type=F32)
    logits = (hi_terms[:, :LANES] + hi_terms[:, LANES:]
              + jnp.dot(lo, rwh_ref[...], preferred_element_type=F32))
    _route_tokens(logits.T[0:N_EXPERTS, :], rb_ref, upper_ref, idx_ref, gate_ref, rank_ref, cnt_ref, carry_ref)
    slot_ref[...] = idx_ref[...] * cap + rank_ref[...]
    x1_ref[...] = x1


def _out_proj_call(x2, ys, ya, g1, sc2, sh2, ng, w_out, rwh, rwl, router_bias, upper, carry_in,
                   tiles_per_batch, tm, tile0, ntiles, cap):
    n, d = x2.shape
    nt = ntiles * tm
    row = lambda w: pl.BlockSpec((tm, w), lambda i: (i + tile0, 0))
    own = lambda w: pl.BlockSpec((tm, w), lambda i: (i, 0))
    tok = lambda r: pl.BlockSpec((r, tm), lambda i: (0, i))
    full = lambda a: pl.BlockSpec(a.shape, lambda i: (0, 0))
    per_batch = pl.BlockSpec((1, 1, d), lambda i: ((i + tile0) // tiles_per_batch, 0, 0))
    hbm = pl.BlockSpec(memory_space=pl.ANY)
    staged = (w_out,)
    return pl.pallas_call(
        functools.partial(_out_proj_kernel, cap=cap),
        grid=(ntiles,),
        in_specs=[row(d), row(D_SSM), row(D_ATT), per_batch, per_batch, per_batch, full(ng),
                  hbm, full(rwh), full(rwl), full(router_bias), full(upper), full(carry_in)],
        out_specs=[row(d), own(d // 2), tok(TOP_K), tok(TOP_K), tok(TOP_K), tok(TOP_K),
                   pl.BlockSpec((N_EXPERTS, LANES), lambda i: (0, 0))],
        out_shape=[jax.ShapeDtypeStruct((n, d), F32), jax.ShapeDtypeStruct((nt, d // 2), jnp.uint32),
                   jax.ShapeDtypeStruct((TOP_K, nt), jnp.int32), jax.ShapeDtypeStruct((TOP_K, nt), F32),
                   jax.ShapeDtypeStruct((TOP_K, nt), jnp.int32), jax.ShapeDtypeStruct((TOP_K, nt), jnp.int32),
                   jax.ShapeDtypeStruct((N_EXPERTS, LANES), F32)],
        scratch_shapes=([pltpu.VMEM((N_EXPERTS, LANES), F32)]
                        + [pltpu.VMEM(a.shape[1:], a.dtype) for a in staged]
                        + [pltpu.VMEM(a.shape[1:], BF16) for a in staged]
                        + [pltpu.SemaphoreType.DMA((len(staged),))]),
        input_output_aliases={0: 0},
        compiler_params=pltpu.CompilerParams(dimension_semantics=("arbitrary",),
                                             vmem_limit_bytes=VMEM_LIMIT),
    )(x2, ys, ya, g1, sc2, sh2, ng, w_out, rwh, rwl, router_bias, upper, carry_in)


def _route_tokens(logits_t, rb_ref, upper_ref, idx_ref, gate_ref, rank_ref, cnt_ref, carry_ref):
    t = logits_t.shape[1]
    per_group = N_EXPERTS // ROUTE_GROUPS
    scores = 1.0 / (1.0 + jnp.exp(-logits_t))
    sel = scores + rb_ref[...]
    e_iota = lax.broadcasted_iota(jnp.int32, (N_EXPERTS, t), 0)

    sel3 = sel.reshape(ROUTE_GROUPS, per_group, t)
    w_iota = lax.broadcasted_iota(jnp.int32, sel3.shape, 1)
    m1 = jnp.max(sel3, axis=1, keepdims=True)
    first = jnp.min(jnp.where(sel3 == m1, w_iota, per_group), axis=1, keepdims=True)
    m2 = jnp.max(jnp.where(w_iota == first, NEG_INF, sel3), axis=1, keepdims=True)
    grp = (m1 + m2).reshape(ROUTE_GROUPS, t)

    g_iota = lax.broadcasted_iota(jnp.int32, (ROUTE_GROUPS, t), 0)
    gmask = jnp.zeros((ROUTE_GROUPS, t), jnp.bool_)
    for _ in range(ROUTE_TOPK_GROUPS):
        gm = jnp.max(grp, axis=0, keepdims=True)
        gfirst = jnp.min(jnp.where(grp == gm, g_iota, ROUTE_GROUPS), axis=0, keepdims=True)
        hit = g_iota == gfirst
        gmask = jnp.logical_or(gmask, hit)
        grp = jnp.where(hit, NEG_INF, grp)
    allowed = jnp.broadcast_to(gmask.reshape(ROUTE_GROUPS, 1, t),
                               (ROUTE_GROUPS, per_group, t)).reshape(N_EXPERTS, t)
    masked = jnp.where(allowed, sel, NEG_INF)

    picked = jnp.zeros((N_EXPERTS, t), jnp.bool_)
    idx_rows = []
    w_rows = []
    for _ in range(TOP_K):
        mm = jnp.max(masked, axis=0, keepdims=True)
        efirst = jnp.min(jnp.where(masked == mm, e_iota, N_EXPERTS), axis=0, keepdims=True)
        hit = e_iota == efirst
        idx_rows.append(efirst)
        w_rows.append(jnp.sum(jnp.where(hit, scores, 0.0), axis=0, keepdims=True))
        picked = jnp.logical_or(picked, hit)
        masked = jnp.where(hit, NEG_INF, masked)
    idx = jnp.concatenate(idx_rows, axis=0)
    w = jnp.concatenate(w_rows, axis=0)
    gate_ref[...] = w / jnp.sum(w, axis=0, keepdims=True) * ROUTED_SCALE
    idx_ref[...] = idx

    onehot = jnp.where(picked, 1.0, 0.0)
    sub = upper_ref.shape[0]
    carry = carry_ref[:, 0:1]
    parts = []
    for s0 in range(0, t, sub):
        oh = onehot[:, s0:s0 + sub]
        parts.append(jnp.dot(oh.astype(BF16), upper_ref[...], preferred_element_type=F32) + carry)
        carry = carry + jnp.sum(oh, axis=1, keepdims=True)
    rank_full = jnp.concatenate(parts, axis=1)
    rank_rows = [jnp.sum(jnp.where(e_iota == idx_rows[k], rank_full, 0.0), axis=0, keepdims=True)
                 for k in range(TOP_K)]
    rank_ref[...] = jnp.concatenate(rank_rows, axis=0).astype(jnp.int32)
    carry_ref[...] = jnp.broadcast_to(carry, carry_ref.shape)
    cnt_ref[...] = carry_ref[...]


PLAN_EXPERT, PLAN_FRESH, PLAN_VALID, PLAN_SEG, PLAN_NEXT, PLAN_NUSED, PLAN_XROW = range(7)
PLAN_ROWS = SUBLANES


def _plan_kernel(cnt_ref, tri_ref, start_ref, plan_ref, *, nblocks, cap):
    nbp = plan_ref.shape[1]
    cnt = cnt_ref[...].astype(jnp.int32)
    blocks = (cnt + (EXPERT_ROWS - 1)) // EXPERT_ROWS
    end = jnp.dot(tri_ref[...], blocks.astype(F32), precision=lax.Precision.HIGHEST,
                  preferred_element_type=F32).astype(jnp.int32)
    start = end - blocks
    start_ref[...] = start[:, 0:1] * EXPERT_ROWS
    nused = end[N_EXPERTS - 1:N_EXPERTS, 0:1]

    e_iota = lax.broadcasted_iota(jnp.int32, (N_EXPERTS, nbp), 0)
    blk = lax.broadcasted_iota(jnp.int32, (1, nbp), 1)
    expert = jnp.minimum(jnp.sum((end[:, 0:1] <= blk).astype(jnp.int32), axis=0, keepdims=True), N_EXPERTS - 1)
    mine = e_iota == expert
    pick = lambda col: jnp.sum(jnp.where(mine, col, 0), axis=0, keepdims=True)
    first = pick(start[:, 0:1])
    valid = jnp.clip(pick(cnt[:, 0:1]) - (blk - first) * EXPERT_ROWS, 0, EXPERT_ROWS)
    present = jnp.logical_or(blocks[:, 0:1] > 0,
                             jnp.logical_and(e_iota[:, 0:1] == N_EXPERTS - 1, nused < nblocks))
    seg = jnp.sum(jnp.logical_and(present, e_iota <= expert).astype(jnp.int32), axis=0, keepdims=True) - 1
    nxt = jnp.min(jnp.where(jnp.logical_and(present, e_iota > expert), e_iota, N_EXPERTS), axis=0, keepdims=True)
    rows = {PLAN_EXPERT: expert, PLAN_FRESH: (blk == first).astype(jnp.int32), PLAN_VALID: valid, PLAN_SEG: seg,
            PLAN_NEXT: jnp.where(nxt == N_EXPERTS, -1, nxt), PLAN_NUSED: jnp.broadcast_to(nused, (1, nbp)),
            PLAN_XROW: expert * cap + (blk - first) * EXPERT_ROWS}
    zero = jnp.zeros((1, nbp), jnp.int32)
    plan_ref[...] = jnp.concatenate([rows.get(r, zero) for r in range(PLAN_ROWS)], axis=0)


def _plan_call(counts, nblocks, cap):
    nbp = -(-nblocks // LANES) * LANES
    tri = jnp.asarray(np.tril(np.ones((N_EXPERTS, N_EXPERTS), np.float32)))
    return pl.pallas_call(
        functools.partial(_plan_kernel, nblocks=nblocks, cap=cap),
        out_shape=[jax.ShapeDtypeStruct((N_EXPERTS, 1), jnp.int32),
                   jax.ShapeDtypeStruct((PLAN_ROWS, nbp), jnp.int32)],
    )(counts, tri)


def _dest_kernel(idx_ref, rank_ref, start_ref, dest_ref):
    t = idx_ref.shape[1]
    e_iota = lax.broadcasted_iota(jnp.int32, (N_EXPERTS, t), 0)
    rows = [jnp.sum(jnp.where(e_iota == idx_ref[k:k + 1, :], start_ref[...], 0), axis=0, keepdims=True)
            for k in range(TOP_K)]
    dest = jnp.concatenate(rows, axis=0) + rank_ref[...]
    for k in range(TOP_K):
        for c in range(t // LANES):
            dest_ref[k, c:c + 1, :] = dest[k:k + 1, c * LANES:(c + 1) * LANES]


def _dest_call(idx, rank, pad_start, tile):
    n = idx.shape[1]
    tok = pl.BlockSpec((TOP_K, tile), lambda i: (0, i))
    return pl.pallas_call(
        _dest_kernel,
        grid=(n // tile,),
        in_specs=[tok, tok, pl.BlockSpec((N_EXPERTS, 1), lambda i: (0, 0))],
        out_specs=pl.BlockSpec((TOP_K, tile // LANES, LANES), lambda i: (0, i, 0)),
        out_shape=jax.ShapeDtypeStruct((TOP_K, n // LANES, LANES), jnp.int32),
        compiler_params=pltpu.CompilerParams(dimension_semantics=("arbitrary",)),
    )(idx, rank, pad_start)


def _scatter_rows_sc(rows, dest_flat, into, chunk):
    n, w = rows.shape
    copies = dest_flat.shape[0] // n
    info = plsc.get_sparse_core_info()
    nc = info.num_cores
    per_worker = n // (nc * info.num_subcores)
    assert per_worker * nc * info.num_subcores == n and per_worker % chunk == 0
    mesh = plsc.VectorSubcoreMesh(core_axis_name="c", subcore_axis_name="s")
    scratch_types = ([pltpu.VMEM((chunk,), jnp.int32) for _ in range(copies)]
                     + [pltpu.VMEM((chunk, w), rows.dtype), pltpu.SemaphoreType.DMA, pltpu.SemaphoreType.DMA])

    def scatter(rows_hbm, idx_hbm, out_hbm, *scratch):
        idx_vs, (rows_v, isem, sem) = scratch[:copies], scratch[copies:]
        worker = lax.axis_index("s") * nc + lax.axis_index("c")

        @pl.loop(0, per_worker // chunk)
        def _(j):
            base = worker * per_worker + j * chunk
            loads = [pltpu.async_copy(idx_hbm.at[pl.ds(k * n + base, chunk)], idx_vs[k], isem)
                     for k in range(copies)]
            pltpu.sync_copy(rows_hbm.at[pl.ds(base, chunk)], rows_v)
            for ld in loads:
                ld.wait()
            stores = [pltpu.async_copy(rows_v, out_hbm.at[idx_vs[k]], sem) for k in range(copies)]
            for st in stores:
                st.wait()

    if isinstance(into, int):
        return pl.kernel(scatter, mesh=mesh, out_type=jax.ShapeDtypeStruct((into, w), rows.dtype),
                         scratch_types=scratch_types)(rows, dest_flat)
    out = jax.new_ref(into)
    pl.kernel(scatter, mesh=mesh, out_type=(), scratch_types=scratch_types)(rows, dest_flat, out)
    return jax.freeze(out)


EXPERT_LIVE_STEP = 256
X_RING = 4


def _expert_kernel(plan_ref, xs_hbm, wg_hbm, wu_hbm, wd_hbm, y_ref,
                   wgb_ref, wub_ref, wdb_ref, xbuf_ref, xsem, wg_raw, wu_raw, wd_raw, wsem):
    i = pl.program_id(0)
    rows, half = xbuf_ref.shape[1], xbuf_ref.shape[2]
    nused = plan_ref[PLAN_NUSED, 0]

    def w_copies(expert, slot):
        return [pltpu.make_async_copy(src.at[expert], dst.at[slot], wsem.at[slot])
                for src, dst in ((wg_hbm, wg_raw), (wu_hbm, wu_raw), (wd_hbm, wd_raw))]

    def x_copy(block):
        first = pl.multiple_of(plan_ref[PLAN_XROW, block], rows)
        slot = block % X_RING
        return pltpu.make_async_copy(xs_hbm.at[pl.ds(first, rows)], xbuf_ref.at[slot], xsem.at[slot])

    @pl.when(i == 0)
    def _():
        for b in range(X_RING - 1):
            @pl.when(b < nused)
            def _():
                x_copy(b).start()

    @pl.when(i + (X_RING - 1) < nused)
    def _():
        x_copy(i + (X_RING - 1)).start()

    @pl.when(i == 0)
    def _():
        for c in w_copies(plan_ref[PLAN_EXPERT, 0], 0):
            c.start()

    @pl.when(plan_ref[PLAN_FRESH, i] > 0)
    def _():
        slot = plan_ref[PLAN_SEG, i] % 2
        for c in w_copies(plan_ref[PLAN_EXPERT, i], slot):
            c.wait()

        @pl.when(plan_ref[PLAN_NEXT, i] >= 0)
        def _():
            for c in w_copies(plan_ref[PLAN_NEXT, i], 1 - slot):
                c.start()

        wgb_ref[...] = wg_raw[slot].astype(BF16)
        wub_ref[...] = wu_raw[slot].astype(BF16)
        wdb_ref[...] = wd_raw[slot].astype(BF16)

    @pl.when(i < nused)
    def _():
        x_copy(i).wait()

    real = plan_ref[PLAN_VALID, i]
    slot = i % X_RING

    def ffn(live):
        xw = xbuf_ref[slot, 0:live, :]
        row = lax.broadcasted_iota(jnp.int32, xw.shape, 0)
        x_lo, x_hi = _unpack_bf16_pair(jnp.where(row < real, xw, jnp.uint32(0)))
        x_lo = x_lo.astype(BF16)
        x_hi = x_hi.astype(BF16)
        gate = (jnp.dot(x_lo, wgb_ref[:half, :], preferred_element_type=F32)
                + jnp.dot(x_hi, wgb_ref[half:, :], preferred_element_type=F32))
        up = (jnp.dot(x_lo, wub_ref[:half, :], preferred_element_type=F32)
              + jnp.dot(x_hi, wub_ref[half:, :], preferred_element_type=F32))
        u = (_silu(gate) * up).astype(BF16)
        y_lo = jnp.dot(u, wdb_ref[:, :half], preferred_element_type=F32)
        y_hi = jnp.dot(u, wdb_ref[:, half:], preferred_element_type=F32)
        y_ref[0:live, :] = _pack_bf16_pair(y_lo, y_hi)
        if live < rows:
            y_ref[live:, :] = jnp.zeros((rows - live, half), y_ref.dtype)

    pieces = (real + (EXPERT_LIVE_STEP - 1)) // EXPERT_LIVE_STEP
    for p in range(1, rows // EXPERT_LIVE_STEP + 1):
        @pl.when(jnp.logical_and(i < nused, pieces == p))
        def _():
            ffn(p * EXPERT_LIVE_STEP)

    @pl.when(i >= nused)
    def _():
        y_ref[...] = jnp.zeros_like(y_ref)


def _expert_call(plan, xs, wg, wu, wd, rows, nblocks):
    p, w = nblocks * rows, xs.shape[1]
    d, f = wg.shape[1], wg.shape[2]
    hbm = pl.BlockSpec(memory_space=pl.ANY)
    grid_spec = pltpu.PrefetchScalarGridSpec(
        num_scalar_prefetch=1,
        grid=(p // rows,),
        in_specs=[hbm, hbm, hbm, hbm],
        out_specs=pl.BlockSpec((rows, w), lambda i, *_: (i, 0)),
        scratch_shapes=[pltpu.VMEM((d, f), BF16), pltpu.VMEM((d, f), BF16), pltpu.VMEM((f, d), BF16),
                        pltpu.VMEM((X_RING, rows, w), xs.dtype), pltpu.SemaphoreType.DMA((X_RING,)),
                        pltpu.VMEM((2, d, f), wg.dtype), pltpu.VMEM((2, d, f), wu.dtype),
                        pltpu.VMEM((2, f, d), wd.dtype), pltpu.SemaphoreType.DMA((2,))],
    )
    return pl.pallas_call(
        _expert_kernel,
        grid_spec=grid_spec,
        out_shape=jax.ShapeDtypeStruct((p, w), jnp.uint32),
        compiler_params=pltpu.CompilerParams(dimension_semantics=("arbitrary",),
                                             vmem_limit_bytes=VMEM_LIMIT),
    )(plan, xs, wg, wu, wd)


def _gather_rows_sc(table, idx, chunk):
    m = idx.shape[0]
    w = table.shape[1]
    info = plsc.get_sparse_core_info()
    nc = info.num_cores
    per_worker = m // (nc * info.num_subcores)
    assert per_worker * nc * info.num_subcores == m and per_worker % chunk == 0
    part = chunk // SC_GATHER_PARTS
    mesh = plsc.VectorSubcoreMesh(core_axis_name="c", subcore_axis_name="s")

    @functools.partial(
        pl.kernel, mesh=mesh,
        out_type=jax.ShapeDtypeStruct((m, w), table.dtype),
        scratch_types=([pltpu.VMEM((part,), jnp.int32) for _ in range(SC_GATHER_PARTS)]
                       + [pltpu.VMEM((part, w), table.dtype) for _ in range(SC_GATHER_PARTS)]
                       + [pltpu.SemaphoreType.DMA] * 3),
    )
    def gather(table_hbm, idx_hbm, out_hbm, *scratch):
        idx_vs = scratch[:SC_GATHER_PARTS]
        rows_vs = scratch[SC_GATHER_PARTS:2 * SC_GATHER_PARTS]
        isem, gsem, wsem = scratch[2 * SC_GATHER_PARTS:]
        worker = lax.axis_index("s") * nc + lax.axis_index("c")

        @pl.loop(0, per_worker // chunk)
        def _(j):
            base = worker * per_worker + j * chunk
            loads = [pltpu.async_copy(idx_hbm.at[pl.ds(base + p * part, part)], idx_vs[p], isem)
                     for p in range(SC_GATHER_PARTS)]
            gathers = []
            for p in range(SC_GATHER_PARTS):
                loads[p].wait()
                gathers.append(pltpu.async_copy(table_hbm.at[idx_vs[p]], rows_vs[p], gsem))
            writes = []
            for p in range(SC_GATHER_PARTS):
                gathers[p].wait()
                writes.append(pltpu.async_copy(rows_vs[p], out_hbm.at[pl.ds(base + p * part, part)], wsem))
            for wr in writes:
                wr.wait()

    return gather(table, idx)


def _combine_kernel(yk_ref, gate_ref, base_ref, h_ref, g2_ref, fg_ref, sg_hbm, su_hbm, sd_hbm, o_ref,
                    sg_raw, su_raw, sd_raw, sg_ref, su_ref, sd_ref, wsem):
    @pl.when(pl.program_id(0) == 0)
    def _():
        _stage_bf16(((sg_hbm, sg_raw, sg_ref), (su_hbm, su_raw, su_ref), (sd_hbm, sd_raw, sd_ref)), wsem)

    t = base_ref.shape[0]
    half = yk_ref.shape[2]
    h_lo, h_hi = _unpack_bf16_pair(h_ref[...])
    hb = jnp.concatenate([h_lo.astype(BF16), h_hi.astype(BF16)], axis=1)
    u = _silu(jnp.dot(hb, sg_ref[...], preferred_element_type=F32)) * jnp.dot(hb, su_ref[...],
                                                                              preferred_element_type=F32)
    shared = jnp.dot(u.astype(BF16), sd_ref[...], preferred_element_type=F32)
    gates = jnp.concatenate([gate_ref[...], jnp.zeros((LANES - TOP_K, t), F32)], axis=0).T
    r_lo = shared[:, :half]
    r_hi = shared[:, half:]
    for k in range(TOP_K):
        y_lo, y_hi = _unpack_bf16_pair(yk_ref[k])
        r_lo = r_lo + gates[:, k:k + 1] * y_lo
        r_hi = r_hi + gates[:, k:k + 1] * y_hi
    g2 = g2_ref[0]
    x_lo = base_ref[:, :half] + g2[:, :half] * r_lo
    x_hi = base_ref[:, half:] + g2[:, half:] * r_hi
    ms = (jnp.sum(x_lo * x_lo, axis=-1, keepdims=True)
          + jnp.sum(x_hi * x_hi, axis=-1, keepdims=True)) * (1.0 / (2 * half))
    inv = lax.rsqrt(ms + EPS)
    o_ref[:, :half] = x_lo * inv * fg_ref[:, :half]
    o_ref[:, half:] = x_hi * inv * fg_ref[:, half:]


def _combine_call(yk, gates, base, h2, g2, fg, sg, su, sd, tiles_per_batch, tile, tile0):
    n, d = base.shape
    row = lambda w: pl.BlockSpec((tile, w), lambda i: (i + tile0, 0))
    hbm = pl.BlockSpec(memory_space=pl.ANY)
    staged = (sg, su, sd)
    return pl.pallas_call(
        _combine_kernel,
        grid=(yk.shape[1] // tile,),
        in_specs=[pl.BlockSpec((TOP_K, tile, yk.shape[2]), lambda i: (0, i, 0)),
                  pl.BlockSpec((TOP_K, tile), lambda i: (0, i)), row(d),
                  pl.BlockSpec((tile, h2.shape[1]), lambda i: (i, 0)),
                  pl.BlockSpec((1, 1, d), lambda i: ((i + tile0) // tiles_per_batch, 0, 0)),
                  pl.BlockSpec((1, d), lambda i: (0, 0)), hbm, hbm, hbm],
        out_specs=row(d),
        out_shape=jax.ShapeDtypeStruct((n, d), F32),
        scratch_shapes=([pltpu.VMEM(a.shape[1:], a.dtype) for a in staged]
                        + [pltpu.VMEM(a.shape[1:], BF16) for a in staged]
                        + [pltpu.SemaphoreType.DMA((len(staged),))]),
        input_output_aliases={2: 0},
        compiler_params=pltpu.CompilerParams(dimension_semantics=("arbitrary",),
                                             vmem_limit_bytes=VMEM_LIMIT),
    )(yk, gates, base, h2, g2, fg, sg, su, sd)


def _pad_cols(a, width):
    return jnp.pad(a, ((0, 0), (0, width - a.shape[1])))


def _layer(x, mod, norm1_g, norm2_g, w_in, conv_w, conv_b, dt_bias, a_log, d_skip, ssm_norm_g,
           att_norm_g, sinks, rel_bias, w_out, router_w, router_bias, exp_w_gate, exp_w_up, exp_w_down,
           sh_w_gate, sh_w_up, sh_w_down, final_g):
    bsz, l, d = x.shape
    n = bsz * l
    tm = min(ROW_TILE, l)

    sh1, sc1, g1, sh2, sc2, g2 = [m[:, None, :] for m in jnp.split(mod, 6, axis=-1)]

    assert math.frexp(ATT_HEAD_DIM ** -0.5)[0] == 0.5
    x2 = x.reshape(n, d)
    z, xbc, dt, q, k, v, x1 = _in_proj_call(x2, sc1, sh1, norm1_g[None, :], jnp.swapaxes(w_in, 1, 2), l // tm, tm)

    triu = jnp.asarray(np.triu(np.ones((CHUNK, CHUNK), np.float32))).astype(BF16)
    shift = jnp.asarray(_conv_shift_matrix()).astype(BF16)
    y_ssm = _ssd_call(xbc.reshape(bsz, l, CONV_CH), z.reshape(bsz, l, D_SSM), dt.reshape(bsz, l, LANES),
                      conv_w, conv_b[None, :], dt_bias[:, None], a_log[:, None],
                      jnp.repeat(d_skip, SSM_HEAD_DIM)[None, :], ssm_norm_g[None, :], triu, shift)

    bias = _bias_call(rel_bias, jnp.asarray(_rel_bucket_table()))
    y_att = _attn_call(sinks, q.reshape(bsz, l, D_ATT), k.reshape(bsz, l, D_KV), v.reshape(bsz, l, D_KV), bias,
                       att_norm_g[None, :])

    rw = _pad_cols(router_w, LANES)
    rwh = rw.astype(BF16)
    rwl = jnp.concatenate([rwh, (rw - rwh.astype(F32)).astype(BF16)], axis=1)
    rs = min(RANK_SUB, tm)
    upper = jnp.asarray(np.triu(np.ones((rs, rs), np.float32), 1)).astype(BF16)
    groups = COMBINE_GROUPS if bsz % COMBINE_GROUPS == 0 else 1
    ng = n // groups

    cap = -(-n // EXPERT_ROWS) * EXPERT_ROWS
    counts = jnp.zeros((N_EXPERTS, LANES), F32)
    xs, parts = N_EXPERTS * cap, []
    for g in range(groups):
        x1, h2, idx, gates, rank, slot, counts = _out_proj_call(
            x1, y_ssm.reshape(n, D_SSM), y_att.reshape(n, D_ATT), g1, sc2, sh2, norm2_g[None, :],
            w_out, rwh, rwl, router_bias[:, None], upper, counts, l // tm, tm, g * ng // tm, ng // tm, cap)
        xs = _scatter_rows_sc(h2, slot.reshape(-1), xs, SC_CHUNK)
        parts.append((h2, idx, gates, rank))
    idx = jnp.concatenate([p[1] for p in parts], axis=1)
    rank = jnp.concatenate([p[3] for p in parts], axis=1)
    rt = min(ROUTE_TILE, n)

    nblocks = (n * TOP_K + N_EXPERTS * (EXPERT_ROWS - 1) + EXPERT_ROWS - 1) // EXPERT_ROWS
    pad_start, plan = _plan_call(counts, nblocks, cap)
    dest = _dest_call(idx, rank, pad_start, rt)

    ys = _expert_call(plan, xs, exp_w_gate, exp_w_up, exp_w_down, EXPERT_ROWS, nblocks)
    ctile = min(COMBINE_TILE, l)
    out = x1
    for g in range(groups):
        idx_g = dest[:, g * ng // LANES:(g + 1) * ng // LANES, :].reshape(-1)
        yk = _gather_rows_sc(ys, idx_g, SC_CHUNK).reshape(TOP_K, ng, ys.shape[1])
        out = _combine_call(yk, parts[g][2], out, parts[g][0], g2, final_g[None, :], sh_w_gate, sh_w_up,
                            sh_w_down, l // ctile, ctile, g * ng // ctile)
    return out.reshape(bsz, l, d)


def kernel(x, c, mod_w, mod_b, norm1_g, norm2_g, w_in, conv_w, conv_b, dt_bias, a_log, d_skip, ssm_norm_g,
           att_norm_g, sinks, rel_bias, w_out, router_w, router_bias, exp_w_gate, exp_w_up, exp_w_down,
           sh_w_gate, sh_w_up, sh_w_down, final_g):
    assert mod_w.shape[0] == 1, "single-layer block"
    bsz = x.shape[0]
    c_pad = jnp.pad(c, ((0, SUBLANES - bsz % SUBLANES if bsz % SUBLANES else 0), (0, 0)))
    mod = _mod_call(c_pad, mod_w[0], mod_b[0][None, :])[:bsz]
    return _layer(x, mod, norm1_g[0], norm2_g[0], w_in, conv_w[0], conv_b[0], dt_bias[0], a_log[0], d_skip[0],
                  ssm_norm_g[0], att_norm_g[0], sinks[0], rel_bias, w_out, router_w[0], router_bias[0],
                  exp_w_gate[0], exp_w_up[0], exp_w_down[0], sh_w_gate, sh_w_up, sh_w_down, final_g)
```

```python
import functools
import math

import numpy as np
import jax
import jax.numpy as jnp
from jax import lax
from jax.experimental import pallas as pl
from jax.experimental.pallas import tpu as pltpu
from jax.experimental.pallas import tpu_sc as plsc

F32 = jnp.float32
BF16 = jnp.bfloat16

D_MODEL = 1024
SSM_HEAD_DIM = 64
D_SSM = D_MODEL
SSM_HEADS = D_SSM // SSM_HEAD_DIM
SSM_GROUPS = 4
D_STATE = 128
CONV_K = 4
CONV_CH = D_SSM + 2 * SSM_GROUPS * D_STATE
CHUNK = 128
ATT_HEAD_DIM = 64
D_ATT = D_MODEL
ATT_HEADS = D_ATT // ATT_HEAD_DIM
KV_HEADS = ATT_HEADS // 4
Q_PER_KV = ATT_HEADS // KV_HEADS
D_KV = KV_HEADS * ATT_HEAD_DIM
WINDOW = 128
ATT_BLOCK = 128
REL_BUCKETS = 32
REL_MAX_DIST = 128
N_EXPERTS = 64
TOP_K = 8
EXPERT_DIM = D_MODEL // 4
SHARED_DIM = D_MODEL // 4
ROUTE_GROUPS = 8
ROUTE_TOPK_GROUPS = 4
ROUTED_SCALE = 2.5
EPS = 1e-6

LANES = 128
SUBLANES = 8
HALF = LANES // 2

ROW_TILE = 512
ROUTE_TILE = 2048
RANK_SUB = 512
COMBINE_TILE = 512
SC_CHUNK = 128
SC_GATHER_PARTS = 4
EXPERT_ROWS = 1024
COMBINE_GROUPS = 2
VMEM_LIMIT = 48 * 1024 * 1024

NEG_INF = float("-inf")


def _silu(v):
    return v * (1.0 / (1.0 + jnp.exp(-v)))


def _softplus(v):
    return jnp.maximum(v, 0.0) + jnp.log(1.0 + jnp.exp(-jnp.abs(v)))


def _split_hi_lo(v):
    hi = v.astype(BF16)
    lo = (v - hi.astype(F32)).astype(BF16)
    return hi, lo


def _pack_bf16_pair(a, b):
    w = pltpu.pack_elementwise([a, b], packed_dtype=BF16)
    return w if w.dtype == jnp.uint32 else lax.bitcast_convert_type(w, jnp.uint32)


def _unpack_bf16_pair(w):
    a = pltpu.unpack_elementwise(w, index=0, packed_dtype=BF16, unpacked_dtype=F32)
    b = pltpu.unpack_elementwise(w, index=1, packed_dtype=BF16, unpacked_dtype=F32)
    return a, b


def _lane_half_mask(shape):
    return lax.broadcasted_iota(jnp.int32, shape, len(shape) - 1) < HALF


def _mod_kernel(c_ref, w_ref, b_ref, o_ref):
    a = _silu(c_ref[...])
    o_ref[...] = jnp.dot(a, w_ref[...], precision=lax.Precision.HIGHEST,
                         preferred_element_type=F32) + b_ref[...]


def _mod_call(c_pad, mod_w, mod_b):
    rows, d = c_pad.shape
    cols = mod_w.shape[1]
    return pl.pallas_call(
        _mod_kernel,
        grid=(cols // d,),
        in_specs=[pl.BlockSpec((rows, d), lambda j: (0, 0)),
                  pl.BlockSpec((d, d), lambda j: (0, j)),
                  pl.BlockSpec((1, d), lambda j: (0, j))],
        out_specs=pl.BlockSpec((rows, d), lambda j: (0, j)),
        out_shape=jax.ShapeDtypeStruct((rows, cols), F32),
        compiler_params=pltpu.CompilerParams(dimension_semantics=("arbitrary",),
                                             vmem_limit_bytes=VMEM_LIMIT),
    )(c_pad, mod_w, mod_b)


IN_PROJ_SEGMENTS = ((D_SSM, BF16), (CONV_CH, BF16), (LANES, F32), (D_ATT, BF16), (D_KV, BF16), (D_KV, BF16))


def _in_proj_kernel(x_ref, sc_ref, sh_ref, g_ref, w_hbm, *refs):
    out_refs = refs[:len(IN_PROJ_SEGMENTS)]
    res_ref, wraw_ref, w_ref, wsem = refs[len(IN_PROJ_SEGMENTS):]

    @pl.when(pl.program_id(0) == 0)
    def _():
        cp = pltpu.make_async_copy(w_hbm.at[0], wraw_ref, wsem)
        cp.start()
        cp.wait()
        src_dt = D_SSM + CONV_CH
        dst_q = src_dt + LANES
        src_q = src_dt + SSM_HEADS
        q_scale = ATT_HEAD_DIM ** -0.5

        def put(dst, src, scale=None):
            t = wraw_ref[src:src + LANES, :].T
            w_ref[:, dst:dst + LANES] = (t if scale is None else t * scale).astype(BF16)

        for c0 in range(0, src_dt, LANES):
            put(c0, c0)
        dt_tile = wraw_ref[src_dt:src_dt + LANES, :].T
        lane = lax.broadcasted_iota(jnp.int32, dt_tile.shape, 1)
        w_ref[:, src_dt:dst_q] = jnp.where(lane < SSM_HEADS, dt_tile, 0.0).astype(BF16)
        for c0 in range(0, D_ATT, LANES):
            put(dst_q + c0, src_q + c0, q_scale)
        for c0 in range(D_ATT, D_ATT + 2 * D_KV, LANES):
            put(dst_q + c0, src_q + c0)

    xf = x_ref[...]
    res_ref[...] = xf
    ms = jnp.mean(xf * xf, axis=-1, keepdims=True)
    h = xf * lax.rsqrt(ms + EPS) * g_ref[...]
    h = h * (1.0 + sc_ref[0]) + sh_ref[0]
    hb = h.astype(BF16)
    col = 0
    for (width, dtype), o_ref in zip(IN_PROJ_SEGMENTS, out_refs):
        o_ref[...] = jnp.dot(hb, w_ref[:, col:col + width], preferred_element_type=F32).astype(dtype)
        col += width


def _in_proj_call(x2, sc1, sh1, g1n, w_in_t, tiles_per_batch, tm):
    n, d = x2.shape
    row = lambda w: pl.BlockSpec((tm, w), lambda i: (i, 0))
    full = lambda a: pl.BlockSpec(a.shape, lambda i: (0, 0))
    per_batch = pl.BlockSpec((1, 1, d), lambda i: (i // tiles_per_batch, 0, 0))
    cols = sum(w for w, _ in IN_PROJ_SEGMENTS)
    w_in = w_in_t
    assert w_in.shape == (1, cols - (LANES - SSM_HEADS), d)
    return pl.pallas_call(
        _in_proj_kernel,
        grid=(n // tm,),
        in_specs=[row(d), per_batch, per_batch, full(g1n), pl.BlockSpec(memory_space=pl.ANY)],
        out_specs=[row(w) for w, _ in IN_PROJ_SEGMENTS] + [row(d)],
        out_shape=[jax.ShapeDtypeStruct((n, w), dt) for w, dt in IN_PROJ_SEGMENTS] + [jax.ShapeDtypeStruct((n, d), F32)],
        scratch_shapes=[pltpu.VMEM(w_in.shape[1:], w_in.dtype), pltpu.VMEM((d, cols), BF16), pltpu.SemaphoreType.DMA(())],
        compiler_params=pltpu.CompilerParams(dimension_semantics=("arbitrary",),
                                             vmem_limit_bytes=VMEM_LIMIT),
    )(x2, sc1, sh1, g1n, w_in)


SSD_SEQS = 4
CONV_HALO = 16


def _conv_shift_matrix():
    s = np.zeros((CONV_K * CHUNK, CONV_HALO + CHUNK), np.float32)
    for k in range(CONV_K):
        t = np.arange(CHUNK)
        s[k * CHUNK + t, CONV_HALO + t - (CONV_K - 1) + k] = 1.0
    return s


def _silu_tanh(v):
    hv = 0.5 * v
    return hv + hv * jnp.tanh(hv)


def _ssd_kernel(xbc_ref, z_ref, dt_ref, cw_ref, cb_ref, dtb_ref, alog_ref, dskip_ref, ng_ref, triu_ref, shift_ref,
                y_ref, state_ref, ucat_ref, ybuf_ref):
    nseq = xbc_ref.shape[0]

    @pl.when(pl.program_id(1) == 0)
    def _():
        state_ref[...] = jnp.zeros_like(state_ref)
        ucat_ref[:, 0:CONV_HALO, :] = jnp.zeros((nseq, CONV_HALO, CONV_CH), BF16)

    for q in range(nseq):
        _ssd_chunk(xbc_ref.at[q], z_ref.at[q], dt_ref.at[q], cw_ref, cb_ref, dtb_ref, alog_ref, dskip_ref, ng_ref,
                   triu_ref, shift_ref, y_ref.at[q], state_ref.at[q], ucat_ref.at[q], ybuf_ref.at[q])


def _ssd_chunk(xbc_ref, z_ref, dt_ref, cw_ref, cb_ref, dtb_ref, alog_ref, dskip_ref, ng_ref, triu_ref, shift_ref,
               y_ref, state_ref, ucat_ref, ybuf_ref):
    ucat_ref[CONV_HALO:, :] = xbc_ref[...]
    shifted = jnp.dot(shift_ref[...], ucat_ref[...], preferred_element_type=F32)
    ucat_ref[0:CONV_HALO, :] = ucat_ref[CHUNK:CHUNK + CONV_HALO, :]
    acc = cb_ref[...] + cw_ref[0:1, :] * shifted[0:CHUNK]
    for kk in range(1, CONV_K):
        acc = acc + cw_ref[kk:kk + 1, :] * shifted[kk * CHUNK:(kk + 1) * CHUNK]
    act = _silu_tanh(acc)
    xs = act[:, :D_SSM]
    gn = SSM_GROUPS * D_STATE

    dt_t = _softplus(dt_ref[...].T[0:SSM_HEADS, :] + dtb_ref[...])
    a_t = dt_t * (-jnp.exp(alog_ref[...]))
    a_hi = a_t.astype(BF16)
    a_mid = (a_t - a_hi.astype(F32)).astype(BF16)
    a_lo = (a_t - a_hi.astype(F32) - a_mid.astype(F32)).astype(BF16)
    triu = triu_ref[...]
    cs_t = (jnp.dot(a_hi, triu, preferred_element_type=F32) + jnp.dot(a_mid, triu, preferred_element_type=F32)
            + jnp.dot(a_lo, triu, preferred_element_type=F32))
    cs_end = cs_t[:, CHUNK - 1:CHUNK]
    r_t = cs_t - jnp.log(dt_t)
    w_t = jnp.exp(cs_end - cs_t) * dt_t
    chunk_decay = jnp.exp(cs_end)
    cols = jnp.concatenate([cs_t, jnp.exp(cs_t), jnp.zeros((LANES - 2 * SSM_HEADS, CHUNK), F32)], axis=0).T

    li = lax.broadcasted_iota(jnp.int32, (CHUNK, CHUNK), 0)
    si = lax.broadcasted_iota(jnp.int32, (CHUNK, CHUNK), 1)
    causal = li >= si
    low = _lane_half_mask((CHUNK, LANES))
    low_row = _lane_half_mask((1, LANES))

    heads_per_group = SSM_HEADS // SSM_GROUPS
    for g in range(SSM_GROUPS):
        b_g = act[:, D_SSM + g * D_STATE:D_SSM + (g + 1) * D_STATE]
        c_g = act[:, D_SSM + gn + g * D_STATE:D_SSM + gn + (g + 1) * D_STATE]
        b_gb = b_g.astype(BF16)
        c_gb = c_g.astype(BF16)
        cb = lax.dot_general(c_gb, b_gb, (((1,), (1,)), ((), ())), preferred_element_type=F32)
        b_t = b_g.T
        for jp in range(heads_per_group // 2):
            j = g * (heads_per_group // 2) + jp
            lanes = slice(j * LANES, (j + 1) * LANES)
            xp = xs[:, lanes]
            xpb = xp.astype(BF16)
            ydiag = jnp.zeros((CHUNK, LANES), F32)
            snew = jnp.zeros((D_STATE, LANES), F32)
            for half in range(2):
                h = 2 * j + half
                diff = cols[:, h:h + 1] - r_t[h:h + 1, :]
                m = (cb * jnp.exp(jnp.where(causal, diff, NEG_INF))).astype(BF16)
                keep = low if half == 0 else jnp.logical_not(low)
                xh = jnp.where(keep, xpb, jnp.zeros_like(xpb))
                ydiag = ydiag + jnp.dot(m, xh, preferred_element_type=F32)
                snew = snew + jnp.dot((b_t * w_t[h:h + 1, :]).astype(BF16), xh, preferred_element_type=F32)
            s_in = state_ref[:, lanes]
            yoff = jnp.dot(c_gb, s_in.astype(BF16), preferred_element_type=F32)
            h0 = 2 * j
            e0 = SSM_HEADS + h0
            escale = jnp.where(low, cols[:, e0:e0 + 1], cols[:, e0 + 1:e0 + 2])
            cdec = jnp.where(low_row, chunk_decay[h0:h0 + 1, :], chunk_decay[h0 + 1:h0 + 2, :])
            ybuf_ref[:, lanes] = ydiag + yoff * escale + xp * dskip_ref[:, lanes]
            state_ref[:, lanes] = s_in * cdec + snew

    yz = ybuf_ref[...] * _silu_tanh(z_ref[...].astype(F32))
    gw = D_SSM // SSM_GROUPS
    for g in range(SSM_GROUPS):
        part = yz[:, g * gw:(g + 1) * gw]
        ms = jnp.mean(part * part, axis=-1, keepdims=True)
        y_ref[:, g * gw:(g + 1) * gw] = (part * lax.rsqrt(ms + EPS)
                                            * ng_ref[:, g * gw:(g + 1) * gw]).astype(BF16)


def _ssd_call(xbc, z, dt, conv_w, conv_b, dtb, alog, dskip, ng, triu, shift):
    bsz, l, _ = xbc.shape
    nc = l // CHUNK
    nseq = SSD_SEQS if bsz % SSD_SEQS == 0 else 1
    chunk = lambda w: pl.BlockSpec((nseq, CHUNK, w), lambda b, c: (b, c, 0))
    full = lambda a: pl.BlockSpec(a.shape, lambda b, c: (0, 0))
    return pl.pallas_call(
        _ssd_kernel,
        grid=(bsz // nseq, nc),
        in_specs=[chunk(CONV_CH), chunk(D_SSM), chunk(LANES), full(conv_w), full(conv_b), full(dtb),
                  full(alog), full(dskip), full(ng), full(triu), full(shift)],
        out_specs=chunk(D_SSM),
        out_shape=jax.ShapeDtypeStruct((bsz, l, D_SSM), BF16),
        scratch_shapes=[pltpu.VMEM((nseq, D_STATE, D_SSM), F32),
                        pltpu.VMEM((nseq, CONV_HALO + CHUNK, CONV_CH), BF16),
                        pltpu.VMEM((nseq, CHUNK, D_SSM), F32)],
        compiler_params=pltpu.CompilerParams(dimension_semantics=("arbitrary", "arbitrary"),
                                             vmem_limit_bytes=VMEM_LIMIT),
    )(xbc, z, dt, conv_w, conv_b, dtb, alog, dskip, ng, triu, shift)


assert WINDOW == ATT_BLOCK


def _rel_bucket_table():
    qi = np.arange(ATT_BLOCK)[:, None]
    c = np.arange(ATT_BLOCK)[None, :]
    dist = np.where(c > qi, qi + ATT_BLOCK - c, qi - c)
    max_exact = REL_BUCKETS // 2
    d = np.maximum(dist, 1).astype(np.float32)
    large = max_exact + (np.log(d / np.float32(max_exact)) / np.float32(math.log(REL_MAX_DIST / max_exact))
                         * np.float32(REL_BUCKETS - max_exact)).astype(np.int32)
    large = np.minimum(large, REL_BUCKETS - 1)
    return np.where(dist < max_exact, dist, large).astype(np.int32)


def _bias_kernel(rb_ref, bucket_ref, o_ref):
    bucket = bucket_ref[...]
    from_prev = (lax.broadcasted_iota(jnp.int32, bucket.shape, 1)
                 > lax.broadcasted_iota(jnp.int32, bucket.shape, 0))
    for h in range(ATT_HEADS):
        acc = jnp.zeros(bucket.shape, F32)
        for b in range(REL_BUCKETS):
            acc = jnp.where(bucket == b, rb_ref[b, h], acc)
        o_ref[1, h] = acc
        o_ref[0, h] = jnp.where(from_prev, NEG_INF, acc)


def _bias_call(rel_bias, bucket):
    return pl.pallas_call(
        _bias_kernel,
        in_specs=[pl.BlockSpec(memory_space=pltpu.SMEM), pl.BlockSpec(memory_space=pltpu.VMEM)],
        out_shape=jax.ShapeDtypeStruct((2, ATT_HEADS) + bucket.shape, F32),
    )(rel_bias, bucket)


ATT_SEQS = 4


def _attn_kernel(sink_ref, q_ref, kp_ref, kc_ref, vp_ref, vc_ref, bias_ref, ng_ref, o_ref, obuf_ref):
    for s in range(q_ref.shape[0]):
        _attn_block(sink_ref, q_ref.at[s], kp_ref.at[s], kc_ref.at[s], vp_ref.at[s], vc_ref.at[s], bias_ref.at[0],
                    ng_ref, o_ref.at[s], obuf_ref.at[s])


def _attn_block(sink_ref, q_ref, kp_ref, kc_ref, vp_ref, vc_ref, bias_ref, ng_ref, o_ref, obuf_ref):
    qi = lax.broadcasted_iota(jnp.int32, (ATT_BLOCK, ATT_BLOCK), 0)
    ci = lax.broadcasted_iota(jnp.int32, (ATT_BLOCK, ATT_BLOCK), 1)
    from_prev = ci > qi
    low = _lane_half_mask((ATT_BLOCK, LANES))

    def band_variants(prev_ref, cur_ref):
        out = []
        for cpair in range(KV_HEADS // 2):
            lanes = slice(cpair * LANES, (cpair + 1) * LANES)
            t = jnp.concatenate([prev_ref[:, lanes], cur_ref[:, lanes]], axis=0).astype(F32)
            out.append((t.astype(BF16), pltpu.roll(t, HALF, 1).astype(BF16)))
        return out

    k_band = band_variants(kp_ref, kc_ref)
    v_band = band_variants(vp_ref, vc_ref)
    nt = (((1,), (1,)), ((), ()))

    for j in range(ATT_HEADS // 2):
        qp = q_ref[:, j * LANES:(j + 1) * LANES]
        out_pair = jnp.zeros((ATT_BLOCK, LANES), F32)
        for half in range(2):
            h = 2 * j + half
            g = h // Q_PER_KV
            swapped = int((g % 2) != half)
            keep = low if half == 0 else jnp.logical_not(low)
            qh = jnp.where(keep, qp, jnp.zeros_like(qp))
            s_band = lax.dot_general(qh, k_band[g // 2][swapped], nt, preferred_element_type=F32)
            s = jnp.where(from_prev, s_band[:, :ATT_BLOCK], s_band[:, ATT_BLOCK:]) + bias_ref[h]
            sink = sink_ref[h]
            m = jnp.maximum(jnp.max(s, axis=-1, keepdims=True), sink)
            p = jnp.exp(s - m)
            denom = jnp.sum(p, axis=-1, keepdims=True) + jnp.exp(sink - m)
            p_band = jnp.concatenate([jnp.where(from_prev, p, 0.0), jnp.where(from_prev, 0.0, p)], axis=1)
            o = jnp.dot(p_band.astype(BF16), v_band[g // 2][swapped], preferred_element_type=F32) / denom
            out_pair = out_pair + jnp.where(keep, o, 0.0)
        obuf_ref[:, j * LANES:(j + 1) * LANES] = out_pair

    att = obuf_ref[...]
    ms = jnp.mean(att * att, axis=-1, keepdims=True)
    o_ref[...] = (att * lax.rsqrt(ms + EPS) * ng_ref[...]).astype(BF16)


def _attn_call(sinks, q, k, v, bias, ng):
    bsz, l, _ = q.shape
    nb = l // ATT_BLOCK
    nseq = ATT_SEQS if bsz % ATT_SEQS == 0 else 1
    cur = lambda w: pl.BlockSpec((nseq, ATT_BLOCK, w), lambda b, i: (b, i, 0))
    prev = lambda w: pl.BlockSpec((nseq, ATT_BLOCK, w), lambda b, i: (b, jnp.maximum(i - 1, 0), 0))
    return pl.pallas_call(
        _attn_kernel,
        grid=(bsz // nseq, nb),
        in_specs=[pl.BlockSpec(memory_space=pltpu.SMEM),
                  cur(D_ATT), prev(D_KV), cur(D_KV), prev(D_KV), cur(D_KV),
                  pl.BlockSpec((1,) + bias.shape[1:], lambda b, i: (jnp.minimum(i, 1), 0, 0, 0)),
                  pl.BlockSpec(ng.shape, lambda b, i: (0, 0))],
        out_specs=cur(D_ATT),
        out_shape=jax.ShapeDtypeStruct((bsz, l, D_ATT), BF16),
        scratch_shapes=[pltpu.VMEM((nseq, ATT_BLOCK, D_ATT), F32)],
        compiler_params=pltpu.CompilerParams(dimension_semantics=("arbitrary", "arbitrary"),
                                             vmem_limit_bytes=VMEM_LIMIT),
    )(sinks, q, k, k, v, v, bias, ng)


def _stage_bf16(staged, wsem):
    copies = [pltpu.make_async_copy(src.at[0], raw, wsem.at[j]) for j, (src, raw, _) in enumerate(staged)]
    for cp in copies:
        cp.start()
    for cp, (_, raw, dst) in zip(copies, staged):
        cp.wait()
        dst[...] = raw[...].astype(BF16)


def _out_proj_kernel(x_ref, ys_ref, ya_ref, g1_ref, sc_ref, sh_ref, ng_ref, wo_hbm, rwh_ref, rwl_ref, rb_ref,
                     upper_ref, carry_in_ref,
                     x1_ref, h_ref, idx_ref, gate_ref, rank_ref, slot_ref, cnt_ref,
                     carry_ref, wo_raw, wo_ref, wsem, *, cap):
    @pl.when(pl.program_id(0) == 0)
    def _():
        carry_ref[...] = carry_in_ref[...]
        _stage_bf16(((wo_hbm, wo_raw, wo_ref),), wsem)

    mix = (jnp.dot(ys_ref[...], wo_ref[:D_SSM, :], preferred_element_type=F32)
           + jnp.dot(ya_ref[...], wo_ref[D_SSM:, :], preferred_element_type=F32))
    x1 = x_ref[...] + g1_ref[0] * mix
    ms = jnp.mean(x1 * x1, axis=-1, keepdims=True)
    h = x1 * lax.rsqrt(ms + EPS) * ng_ref[...]
    h = h * (1.0 + sc_ref[0]) + sh_ref[0]
    half = h.shape[1] // 2
    h_ref[...] = _pack_bf16_pair(h[:, :half], h[:, half:])
    hi, lo = _split_hi_lo(h)
    hi_terms = jnp.dot(hi, rwl_ref[...], preferred_element_type=F32)
    logits = (hi_terms[:, :LANES] + hi_terms[:, LANES:]
              + jnp.dot(lo, rwh_ref[...], preferred_element_type=F32))
    _route_tokens(logits.T[0:N_EXPERTS, :], rb_ref, upper_ref, idx_ref, gate_ref, rank_ref, cnt_ref, carry_ref)
    slot_ref[...] = idx_ref[...] * cap + rank_ref[...]
    x1_ref[...] = x1


def _out_proj_call(x2, ys, ya, g1, sc2, sh2, ng, w_out, rwh, rwl, router_bias, upper, carry_in,
                   tiles_per_batch, tm, tile0, ntiles, cap):
    n, d = x2.shape
    nt = ntiles * tm
    row = lambda w: pl.BlockSpec((tm, w), lambda i: (i + tile0, 0))
    own = lambda w: pl.BlockSpec((tm, w), lambda i: (i, 0))
    tok = lambda r: pl.BlockSpec((r, tm), lambda i: (0, i))
    full = lambda a: pl.BlockSpec(a.shape, lambda i: (0, 0))
    per_batch = pl.BlockSpec((1, 1, d), lambda i: ((i + tile0) // tiles_per_batch, 0, 0))
    hbm = pl.BlockSpec(memory_space=pl.ANY)
    staged = (w_out,)
    return pl.pallas_call(
        functools.partial(_out_proj_kernel, cap=cap),
        grid=(ntiles,),
        in_specs=[row(d), row(D_SSM), row(D_ATT), per_batch, per_batch, per_batch, full(ng),
                  hbm, full(rwh), full(rwl), full(router_bias), full(upper), full(carry_in)],
        out_specs=[row(d), own(d // 2), tok(TOP_K), tok(TOP_K), tok(TOP_K), tok(TOP_K),
                   pl.BlockSpec((N_EXPERTS, LANES), lambda i: (0, 0))],
        out_shape=[jax.ShapeDtypeStruct((n, d), F32), jax.ShapeDtypeStruct((nt, d // 2), jnp.uint32),
                   jax.ShapeDtypeStruct((TOP_K, nt), jnp.int32), jax.ShapeDtypeStruct((TOP_K, nt), F32),
                   jax.ShapeDtypeStruct((TOP_K, nt), jnp.int32), jax.ShapeDtypeStruct((TOP_K, nt), jnp.int32),
                   jax.ShapeDtypeStruct((N_EXPERTS, LANES), F32)],
        scratch_shapes=([pltpu.VMEM((N_EXPERTS, LANES), F32)]
                        + [pltpu.VMEM(a.shape[1:], a.dtype) for a in staged]
                        + [pltpu.VMEM(a.shape[1:], BF16) for a in staged]
                        + [pltpu.SemaphoreType.DMA((len(staged),))]),
        input_output_aliases={0: 0},
        compiler_params=pltpu.CompilerParams(dimension_semantics=("arbitrary",),
                                             vmem_limit_bytes=VMEM_LIMIT),
    )(x2, ys, ya, g1, sc2, sh2, ng, w_out, rwh, rwl, router_bias, upper, carry_in)


def _route_tokens(logits_t, rb_ref, upper_ref, idx_ref, gate_ref, rank_ref, cnt_ref, carry_ref):
    t = logits_t.shape[1]
    per_group = N_EXPERTS // ROUTE_GROUPS
    scores = 1.0 / (1.0 + jnp.exp(-logits_t))
    sel = scores + rb_ref[...]
    e_iota = lax.broadcasted_iota(jnp.int32, (N_EXPERTS, t), 0)

    sel3 = sel.reshape(ROUTE_GROUPS, per_group, t)
    w_iota = lax.broadcasted_iota(jnp.int32, sel3.shape, 1)
    m1 = jnp.max(sel3, axis=1, keepdims=True)
    first = jnp.min(jnp.where(sel3 == m1, w_iota, per_group), axis=1, keepdims=True)
    m2 = jnp.max(jnp.where(w_iota == first, NEG_INF, sel3), axis=1, keepdims=True)
    grp = (m1 + m2).reshape(ROUTE_GROUPS, t)

    g_iota = lax.broadcasted_iota(jnp.int32, (ROUTE_GROUPS, t), 0)
    gmask = jnp.zeros((ROUTE_GROUPS, t), jnp.bool_)
    for _ in range(ROUTE_TOPK_GROUPS):
        gm = jnp.max(grp, axis=0, keepdims=True)
        gfirst = jnp.min(jnp.where(grp == gm, g_iota, ROUTE_GROUPS), axis=0, keepdims=True)
        hit = g_iota == gfirst
        gmask = jnp.logical_or(gmask, hit)
        grp = jnp.where(hit, NEG_INF, grp)
    allowed = jnp.broadcast_to(gmask.reshape(ROUTE_GROUPS, 1, t),
                               (ROUTE_GROUPS, per_group, t)).reshape(N_EXPERTS, t)
    masked = jnp.where(allowed, sel, NEG_INF)

    picked = jnp.zeros((N_EXPERTS, t), jnp.bool_)
    idx_rows = []
    w_rows = []
    for _ in range(TOP_K):
        mm = jnp.max(masked, axis=0, keepdims=True)
        efirst = jnp.min(jnp.where(masked == mm, e_iota, N_EXPERTS), axis=0, keepdims=True)
        hit = e_iota == efirst
        idx_rows.append(efirst)
        w_rows.append(jnp.sum(jnp.where(hit, scores, 0.0), axis=0, keepdims=True))
        picked = jnp.logical_or(picked, hit)
        masked = jnp.where(hit, NEG_INF, masked)
    idx = jnp.concatenate(idx_rows, axis=0)
    w = jnp.concatenate(w_rows, axis=0)
    gate_ref[...] = w / jnp.sum(w, axis=0, keepdims=True) * ROUTED_SCALE
    idx_ref[...] = idx

    onehot = jnp.where(picked, 1.0, 0.0)
    sub = upper_ref.shape[0]
    carry = carry_ref[:, 0:1]
    parts = []
    for s0 in range(0, t, sub):
        oh = onehot[:, s0:s0 + sub]
        parts.append(jnp.dot(oh.astype(BF16), upper_ref[...], preferred_element_type=F32) + carry)
        carry = carry + jnp.sum(oh, axis=1, keepdims=True)
    rank_full = jnp.concatenate(parts, axis=1)
    rank_rows = [jnp.sum(jnp.where(e_iota == idx_rows[k], rank_full, 0.0), axis=0, keepdims=True)
                 for k in range(TOP_K)]
    rank_ref[...] = jnp.concatenate(rank_rows, axis=0).astype(jnp.int32)
    carry_ref[...] = jnp.broadcast_to(carry, carry_ref.shape)
    cnt_ref[...] = carry_ref[...]


PLAN_EXPERT, PLAN_FRESH, PLAN_VALID, PLAN_SEG, PLAN_NEXT, PLAN_NUSED, PLAN_XROW = range(7)
PLAN_ROWS = SUBLANES


def _plan_kernel(cnt_ref, tri_ref, start_ref, plan_ref, *, nblocks, cap):
    nbp = plan_ref.shape[1]
    cnt = cnt_ref[...].astype(jnp.int32)
    blocks = (cnt + (EXPERT_ROWS - 1)) // EXPERT_ROWS
    end = jnp.dot(tri_ref[...], blocks.astype(F32), precision=lax.Precision.HIGHEST,
                  preferred_element_type=F32).astype(jnp.int32)
    start = end - blocks
    start_ref[...] = start[:, 0:1] * EXPERT_ROWS
    nused = end[N_EXPERTS - 1:N_EXPERTS, 0:1]

    e_iota = lax.broadcasted_iota(jnp.int32, (N_EXPERTS, nbp), 0)
    blk = lax.broadcasted_iota(jnp.int32, (1, nbp), 1)
    expert = jnp.minimum(jnp.sum((end[:, 0:1] <= blk).astype(jnp.int32), axis=0, keepdims=True), N_EXPERTS - 1)
    mine = e_iota == expert
    pick = lambda col: jnp.sum(jnp.where(mine, col, 0), axis=0, keepdims=True)
    first = pick(start[:, 0:1])
    valid = jnp.clip(pick(cnt[:, 0:1]) - (blk - first) * EXPERT_ROWS, 0, EXPERT_ROWS)
    present = jnp.logical_or(blocks[:, 0:1] > 0,
                             jnp.logical_and(e_iota[:, 0:1] == N_EXPERTS - 1, nused < nblocks))
    seg = jnp.sum(jnp.logical_and(present, e_iota <= expert).astype(jnp.int32), axis=0, keepdims=True) - 1
    nxt = jnp.min(jnp.where(jnp.logical_and(present, e_iota > expert), e_iota, N_EXPERTS), axis=0, keepdims=True)
    rows = {PLAN_EXPERT: expert, PLAN_FRESH: (blk == first).astype(jnp.int32), PLAN_VALID: valid, PLAN_SEG: seg,
            PLAN_NEXT: jnp.where(nxt == N_EXPERTS, -1, nxt), PLAN_NUSED: jnp.broadcast_to(nused, (1, nbp)),
            PLAN_XROW: expert * cap + (blk - first) * EXPERT_ROWS}
    zero = jnp.zeros((1, nbp), jnp.int32)
    plan_ref[...] = jnp.concatenate([rows.get(r, zero) for r in range(PLAN_ROWS)], axis=0)


def _plan_call(counts, nblocks, cap):
    nbp = -(-nblocks // LANES) * LANES
    tri = jnp.asarray(np.tril(np.ones((N_EXPERTS, N_EXPERTS), np.float32)))
    return pl.pallas_call(
        functools.partial(_plan_kernel, nblocks=nblocks, cap=cap),
        out_shape=[jax.ShapeDtypeStruct((N_EXPERTS, 1), jnp.int32),
                   jax.ShapeDtypeStruct((PLAN_ROWS, nbp), jnp.int32)],
    )(counts, tri)


def _dest_kernel(idx_ref, rank_ref, start_ref, dest_ref):
    t = idx_ref.shape[1]
    e_iota = lax.broadcasted_iota(jnp.int32, (N_EXPERTS, t), 0)
    rows = [jnp.sum(jnp.where(e_iota == idx_ref[k:k + 1, :], start_ref[...], 0), axis=0, keepdims=True)
            for k in range(TOP_K)]
    dest = jnp.concatenate(rows, axis=0) + rank_ref[...]
    for k in range(TOP_K):
        for c in range(t // LANES):
            dest_ref[k, c:c + 1, :] = dest[k:k + 1, c * LANES:(c + 1) * LANES]


def _dest_call(idx, rank, pad_start, tile):
    n = idx.shape[1]
    tok = pl.BlockSpec((TOP_K, tile), lambda i: (0, i))
    return pl.pallas_call(
        _dest_kernel,
        grid=(n // tile,),
        in_specs=[tok, tok, pl.BlockSpec((N_EXPERTS, 1), lambda i: (0, 0))],
        out_specs=pl.BlockSpec((TOP_K, tile // LANES, LANES), lambda i: (0, i, 0)),
        out_shape=jax.ShapeDtypeStruct((TOP_K, n // LANES, LANES), jnp.int32),
        compiler_params=pltpu.CompilerParams(dimension_semantics=("arbitrary",)),
    )(idx, rank, pad_start)


def _scatter_rows_sc(rows, dest_flat, into, chunk):
    n, w = rows.shape
    copies = dest_flat.shape[0] // n
    info = plsc.get_sparse_core_info()
    nc = info.num_cores
    per_worker = n // (nc * info.num_subcores)
    assert per_worker * nc * info.num_subcores == n and per_worker % chunk == 0
    mesh = plsc.VectorSubcoreMesh(core_axis_name="c", subcore_axis_name="s")
    scratch_types = ([pltpu.VMEM((chunk,), jnp.int32) for _ in range(copies)]
                     + [pltpu.VMEM((chunk, w), rows.dtype), pltpu.SemaphoreType.DMA, pltpu.SemaphoreType.DMA])

    def scatter(rows_hbm, idx_hbm, out_hbm, *scratch):
        idx_vs, (rows_v, isem, sem) = scratch[:copies], scratch[copies:]
        worker = lax.axis_index("s") * nc + lax.axis_index("c")

        @pl.loop(0, per_worker // chunk)
        def _(j):
            base = worker * per_worker + j * chunk
            loads = [pltpu.async_copy(idx_hbm.at[pl.ds(k * n + base, chunk)], idx_vs[k], isem)
                     for k in range(copies)]
            pltpu.sync_copy(rows_hbm.at[pl.ds(base, chunk)], rows_v)
            for ld in loads:
                ld.wait()
            stores = [pltpu.async_copy(rows_v, out_hbm.at[idx_vs[k]], sem) for k in range(copies)]
            for st in stores:
                st.wait()

    if isinstance(into, int):
        return pl.kernel(scatter, mesh=mesh, out_type=jax.ShapeDtypeStruct((into, w), rows.dtype),
                         scratch_types=scratch_types)(rows, dest_flat)
    out = jax.new_ref(into)
    pl.kernel(scatter, mesh=mesh, out_type=(), scratch_types=scratch_types)(rows, dest_flat, out)
    return jax.freeze(out)


EXPERT_LIVE_STEP = 256
X_RING = 4


def _expert_kernel(plan_ref, xs_hbm, wg_hbm, wu_hbm, wd_hbm, y_ref,
                   wgb_ref, wub_ref, wdb_ref, xbuf_ref, xsem, wg_raw, wu_raw, wd_raw, wsem):
    i = pl.program_id(0)
    rows, half = xbuf_ref.shape[1], xbuf_ref.shape[2]
    nused = plan_ref[PLAN_NUSED, 0]

    def w_copies(expert, slot):
        return [pltpu.make_async_copy(src.at[expert], dst.at[slot], wsem.at[slot])
                for src, dst in ((wg_hbm, wg_raw), (wu_hbm, wu_raw), (wd_hbm, wd_raw))]

    def x_copy(block):
        first = pl.multiple_of(plan_ref[PLAN_XROW, block], rows)
        slot = block % X_RING
        return pltpu.make_async_copy(xs_hbm.at[pl.ds(first, rows)], xbuf_ref.at[slot], xsem.at[slot])

    @pl.when(i == 0)
    def _():
        for b in range(X_RING - 1):
            @pl.when(b < nused)
            def _():
                x_copy(b).start()

    @pl.when(i + (X_RING - 1) < nused)
    def _():
        x_copy(i + (X_RING - 1)).start()

    @pl.when(i == 0)
    def _():
        for c in w_copies(plan_ref[PLAN_EXPERT, 0], 0):
            c.start()

    @pl.when(plan_ref[PLAN_FRESH, i] > 0)
    def _():
        slot = plan_ref[PLAN_SEG, i] % 2
        for c in w_copies(plan_ref[PLAN_EXPERT, i], slot):
            c.wait()

        @pl.when(plan_ref[PLAN_NEXT, i] >= 0)
        def _():
            for c in w_copies(plan_ref[PLAN_NEXT, i], 1 - slot):
                c.start()

        wgb_ref[...] = wg_raw[slot].astype(BF16)
        wub_ref[...] = wu_raw[slot].astype(BF16)
        wdb_ref[...] = wd_raw[slot].astype(BF16)

    @pl.when(i < nused)
    def _():
        x_copy(i).wait()

    real = plan_ref[PLAN_VALID, i]
    slot = i % X_RING

    def ffn(live):
        xw = xbuf_ref[slot, 0:live, :]
        row = lax.broadcasted_iota(jnp.int32, xw.shape, 0)
        x_lo, x_hi = _unpack_bf16_pair(jnp.where(row < real, xw, jnp.uint32(0)))
        x_lo = x_lo.astype(BF16)
        x_hi = x_hi.astype(BF16)
        gate = (jnp.dot(x_lo, wgb_ref[:half, :], preferred_element_type=F32)
                + jnp.dot(x_hi, wgb_ref[half:, :], preferred_element_type=F32))
        up = (jnp.dot(x_lo, wub_ref[:half, :], preferred_element_type=F32)
              + jnp.dot(x_hi, wub_ref[half:, :], preferred_element_type=F32))
        u = (_silu(gate) * up).astype(BF16)
        y_lo = jnp.dot(u, wdb_ref[:, :half], preferred_element_type=F32)
        y_hi = jnp.dot(u, wdb_ref[:, half:], preferred_element_type=F32)
        y_ref[0:live, :] = _pack_bf16_pair(y_lo, y_hi)
        if live < rows:
            y_ref[live:, :] = jnp.zeros((rows - live, half), y_ref.dtype)

    pieces = (real + (EXPERT_LIVE_STEP - 1)) // EXPERT_LIVE_STEP
    for p in range(1, rows // EXPERT_LIVE_STEP + 1):
        @pl.when(jnp.logical_and(i < nused, pieces == p))
        def _():
            ffn(p * EXPERT_LIVE_STEP)

    @pl.when(i == nused)
    def _():
        y_ref[...] = jnp.zeros_like(y_ref)


def _expert_call(plan, xs, wg, wu, wd, rows, nblocks):
    p, w = nblocks * rows, xs.shape[1]
    d, f = wg.shape[1], wg.shape[2]
    hbm = pl.BlockSpec(memory_space=pl.ANY)
    grid_spec = pltpu.PrefetchScalarGridSpec(
        num_scalar_prefetch=1,
        grid=(p // rows,),
        in_specs=[hbm, hbm, hbm, hbm],
        out_specs=pl.BlockSpec((rows, w), lambda i, plan: (jnp.where(i < plan[PLAN_NUSED, 0], i, nblocks - 1), 0)),
        scratch_shapes=[pltpu.VMEM((d, f), BF16), pltpu.VMEM((d, f), BF16), pltpu.VMEM((f, d), BF16),
                        pltpu.VMEM((X_RING, rows, w), xs.dtype), pltpu.SemaphoreType.DMA((X_RING,)),
                        pltpu.VMEM((2, d, f), wg.dtype), pltpu.VMEM((2, d, f), wu.dtype),
                        pltpu.VMEM((2, f, d), wd.dtype), pltpu.SemaphoreType.DMA((2,))],
    )
    return pl.pallas_call(
        _expert_kernel,
        grid_spec=grid_spec,
        out_shape=jax.ShapeDtypeStruct((p, w), jnp.uint32),
        compiler_params=pltpu.CompilerParams(dimension_semantics=("arbitrary",),
                                             vmem_limit_bytes=VMEM_LIMIT),
    )(plan, xs, wg, wu, wd)


def _gather_rows_sc(table, idx, chunk):
    m = idx.shape[0]
    w = table.shape[1]
    info = plsc.get_sparse_core_info()
    nc = info.num_cores
    per_worker = m // (nc * info.num_subcores)
    assert per_worker * nc * info.num_subcores == m and per_worker % chunk == 0
    part = chunk // SC_GATHER_PARTS
    mesh = plsc.VectorSubcoreMesh(core_axis_name="c", subcore_axis_name="s")

    @functools.partial(
        pl.kernel, mesh=mesh,
        out_type=jax.ShapeDtypeStruct((m, w), table.dtype),
        scratch_types=([pltpu.VMEM((part,), jnp.int32) for _ in range(SC_GATHER_PARTS)]
                       + [pltpu.VMEM((part, w), table.dtype) for _ in range(SC_GATHER_PARTS)]
                       + [pltpu.SemaphoreType.DMA] * 3),
    )
    def gather(table_hbm, idx_hbm, out_hbm, *scratch):
        idx_vs = scratch[:SC_GATHER_PARTS]
        rows_vs = scratch[SC_GATHER_PARTS:2 * SC_GATHER_PARTS]
        isem, gsem, wsem = scratch[2 * SC_GATHER_PARTS:]
        worker = lax.axis_index("s") * nc + lax.axis_index("c")

        @pl.loop(0, per_worker // chunk)
        def _(j):
            base = worker * per_worker + j * chunk
            loads = [pltpu.async_copy(idx_hbm.at[pl.ds(base + p * part, part)], idx_vs[p], isem)
                     for p in range(SC_GATHER_PARTS)]
            gathers = []
            for p in range(SC_GATHER_PARTS):
                loads[p].wait()
                gathers.append(pltpu.async_copy(table_hbm.at[idx_vs[p]], rows_vs[p], gsem))
            writes = []
            for p in range(SC_GATHER_PARTS):
                gathers[p].wait()
                writes.append(pltpu.async_copy(rows_vs[p], out_hbm.at[pl.ds(base + p * part, part)], wsem))
            for wr in writes:
                wr.wait()

    return gather(table, idx)


def _combine_kernel(yk_ref, gate_ref, base_ref, h_ref, g2_ref, fg_ref, sg_hbm, su_hbm, sd_hbm, o_ref,
                    sg_raw, su_raw, sd_raw, sg_ref, su_ref, sd_ref, wsem):
    @pl.when(pl.program_id(0) == 0)
    def _():
        _stage_bf16(((sg_hbm, sg_raw, sg_ref), (su_hbm, su_raw, su_ref), (sd_hbm, sd_raw, sd_ref)), wsem)

    t = base_ref.shape[0]
    half = yk_ref.shape[2]
    h_lo, h_hi = _unpack_bf16_pair(h_ref[...])
    hb = jnp.concatenate([h_lo.astype(BF16), h_hi.astype(BF16)], axis=1)
    u = _silu(jnp.dot(hb, sg_ref[...], preferred_element_type=F32)) * jnp.dot(hb, su_ref[...],
                                                                              preferred_element_type=F32)
    shared = jnp.dot(u.astype(BF16), sd_ref[...], preferred_element_type=F32)
    gates = jnp.concatenate([gate_ref[...], jnp.zeros((LANES - TOP_K, t), F32)], axis=0).T
    r_lo = shared[:, :half]
    r_hi = shared[:, half:]
    for k in range(TOP_K):
        y_lo, y_hi = _unpack_bf16_pair(yk_ref[k])
        r_lo = r_lo + gates[:, k:k + 1] * y_lo
        r_hi = r_hi + gates[:, k:k + 1] * y_hi
    g2 = g2_ref[0]
    x_lo = base_ref[:, :half] + g2[:, :half] * r_lo
    x_hi = base_ref[:, half:] + g2[:, half:] * r_hi
    ms = (jnp.sum(x_lo * x_lo, axis=-1, keepdims=True)
          + jnp.sum(x_hi * x_hi, axis=-1, keepdims=True)) * (1.0 / (2 * half))
    inv = lax.rsqrt(ms + EPS)
    o_ref[:, :half] = x_lo * inv * fg_ref[:, :half]
    o_ref[:, half:] = x_hi * inv * fg_ref[:, half:]


def _combine_call(yk, gates, base, h2, g2, fg, sg, su, sd, tiles_per_batch, tile, tile0):
    n, d = base.shape
    row = lambda w: pl.BlockSpec((tile, w), lambda i: (i + tile0, 0))
    hbm = pl.BlockSpec(memory_space=pl.ANY)
    staged = (sg, su, sd)
    return pl.pallas_call(
        _combine_kernel,
        grid=(yk.shape[1] // tile,),
        in_specs=[pl.BlockSpec((TOP_K, tile, yk.shape[2]), lambda i: (0, i, 0)),
                  pl.BlockSpec((TOP_K, tile), lambda i: (0, i)), row(d),
                  pl.BlockSpec((tile, h2.shape[1]), lambda i: (i, 0)),
                  pl.BlockSpec((1, 1, d), lambda i: ((i + tile0) // tiles_per_batch, 0, 0)),
                  pl.BlockSpec((1, d), lambda i: (0, 0)), hbm, hbm, hbm],
        out_specs=row(d),
        out_shape=jax.ShapeDtypeStruct((n, d), F32),
        scratch_shapes=([pltpu.VMEM(a.shape[1:], a.dtype) for a in staged]
                        + [pltpu.VMEM(a.shape[1:], BF16) for a in staged]
                        + [pltpu.SemaphoreType.DMA((len(staged),))]),
        input_output_aliases={2: 0},
        compiler_params=pltpu.CompilerParams(dimension_semantics=("arbitrary",),
                                             vmem_limit_bytes=VMEM_LIMIT),
    )(yk, gates, base, h2, g2, fg, sg, su, sd)


def _pad_cols(a, width):
    return jnp.pad(a, ((0, 0), (0, width - a.shape[1])))


def _layer(x, mod, norm1_g, norm2_g, w_in, conv_w, conv_b, dt_bias, a_log, d_skip, ssm_norm_g,
           att_norm_g, sinks, rel_bias, w_out, router_w, router_bias, exp_w_gate, exp_w_up, exp_w_down,
           sh_w_gate, sh_w_up, sh_w_down, final_g):
    bsz, l, d = x.shape
    n = bsz * l
    tm = min(ROW_TILE, l)

    sh1, sc1, g1, sh2, sc2, g2 = [m[:, None, :] for m in jnp.split(mod, 6, axis=-1)]

    assert math.frexp(ATT_HEAD_DIM ** -0.5)[0] == 0.5
    x2 = x.reshape(n, d)
    z, xbc, dt, q, k, v, x1 = _in_proj_call(x2, sc1, sh1, norm1_g[None, :], jnp.swapaxes(w_in, 1, 2), l // tm, tm)

    triu = jnp.asarray(np.triu(np.ones((CHUNK, CHUNK), np.float32))).astype(BF16)
    shift = jnp.asarray(_conv_shift_matrix()).astype(BF16)
    y_ssm = _ssd_call(xbc.reshape(bsz, l, CONV_CH), z.reshape(bsz, l, D_SSM), dt.reshape(bsz, l, LANES),
                      conv_w, conv_b[None, :], dt_bias[:, None], a_log[:, None],
                      jnp.repeat(d_skip, SSM_HEAD_DIM)[None, :], ssm_norm_g[None, :], triu, shift)

    bias = _bias_call(rel_bias, jnp.asarray(_rel_bucket_table()))
    y_att = _attn_call(sinks, q.reshape(bsz, l, D_ATT), k.reshape(bsz, l, D_KV), v.reshape(bsz, l, D_KV), bias,
                       att_norm_g[None, :])

    rw = _pad_cols(router_w, LANES)
    rwh = rw.astype(BF16)
    rwl = jnp.concatenate([rwh, (rw - rwh.astype(F32)).astype(BF16)], axis=1)
    rs = min(RANK_SUB, tm)
    upper = jnp.asarray(np.triu(np.ones((rs, rs), np.float32), 1)).astype(BF16)
    groups = COMBINE_GROUPS if bsz % COMBINE_GROUPS == 0 else 1
    ng = n // groups

    cap = -(-n // EXPERT_ROWS) * EXPERT_ROWS
    counts = jnp.zeros((N_EXPERTS, LANES), F32)
    xs, parts = N_EXPERTS * cap, []
    for g in range(groups):
        x1, h2, idx, gates, rank, slot, counts = _out_proj_call(
            x1, y_ssm.reshape(n, D_SSM), y_att.reshape(n, D_ATT), g1, sc2, sh2, norm2_g[None, :],
            w_out, rwh, rwl, router_bias[:, None], upper, counts, l // tm, tm, g * ng // tm, ng // tm, cap)
        xs = _scatter_rows_sc(h2, slot.reshape(-1), xs, SC_CHUNK)
        parts.append((h2, idx, gates, rank))
    idx = jnp.concatenate([p[1] for p in parts], axis=1)
    rank = jnp.concatenate([p[3] for p in parts], axis=1)
    rt = min(ROUTE_TILE, n)

    nblocks = (n * TOP_K + N_EXPERTS * (EXPERT_ROWS - 1) + EXPERT_ROWS - 1) // EXPERT_ROWS
    pad_start, plan = _plan_call(counts, nblocks, cap)
    dest = _dest_call(idx, rank, pad_start, rt)

    ys = _expert_call(plan, xs, exp_w_gate, exp_w_up, exp_w_down, EXPERT_ROWS, nblocks)
    ctile = min(COMBINE_TILE, l)
    out = x1
    for g in range(groups):
        idx_g = dest[:, g * ng // LANES:(g + 1) * ng // LANES, :].reshape(-1)
        yk = _gather_rows_sc(ys, idx_g, SC_CHUNK).reshape(TOP_K, ng, ys.shape[1])
        out = _combine_call(yk, parts[g][2], out, parts[g][0], g2, final_g[None, :], sh_w_gate, sh_w_up,
                            sh_w_down, l // ctile, ctile, g * ng // ctile)
    return out.reshape(bsz, l, d)


def kernel(x, c, mod_w, mod_b, norm1_g, norm2_g, w_in, conv_w, conv_b, dt_bias, a_log, d_skip, ssm_norm_g,
           att_norm_g, sinks, rel_bias, w_out, router_w, router_bias, exp_w_gate, exp_w_up, exp_w_down,
           sh_w_gate, sh_w_up, sh_w_down, final_g):
    assert mod_w.shape[0] == 1, "single-layer block"
    bsz = x.shape[0]
    c_pad = jnp.pad(c, ((0, SUBLANES - bsz % SUBLANES if bsz % SUBLANES else 0), (0, 0)))
    mod = _mod_call(c_pad, mod_w[0], mod_b[0][None, :])[:bsz]
    return _layer(x, mod, norm1_g[0], norm2_g[0], w_in, conv_w[0], conv_b[0], dt_bias[0], a_log[0], d_skip[0],
                  ssm_norm_g[0], att_norm_g[0], sinks[0], rel_bias, w_out, router_w[0], router_bias[0],
                  exp_w_gate[0], exp_w_up[0], exp_w_down[0], sh_w_gate, sh_w_up, sh_w_down, final_g)
```

```python
import functools
import math

import numpy as np
import jax
import jax.numpy as jnp
from jax import lax
from jax.experimental import pallas as pl
from jax.experimental.pallas import tpu as pltpu
from jax.experimental.pallas import tpu_sc as plsc

F32 = jnp.float32
BF16 = jnp.bfloat16

D_MODEL = 1024
SSM_HEAD_DIM = 64
D_SSM = D_MODEL
SSM_HEADS = D_SSM // SSM_HEAD_DIM
SSM_GROUPS = 4
D_STATE = 128
CONV_K = 4
CONV_CH = D_SSM + 2 * SSM_GROUPS * D_STATE
CHUNK = 128
ATT_HEAD_DIM = 64
D_ATT = D_MODEL
ATT_HEADS = D_ATT // ATT_HEAD_DIM
KV_HEADS = ATT_HEADS // 4
Q_PER_KV = ATT_HEADS // KV_HEADS
D_KV = KV_HEADS * ATT_HEAD_DIM
WINDOW = 128
ATT_BLOCK = 128
REL_BUCKETS = 32
REL_MAX_DIST = 128
N_EXPERTS = 64
TOP_K = 8
EXPERT_DIM = D_MODEL // 4
SHARED_DIM = D_MODEL // 4
ROUTE_GROUPS = 8
ROUTE_TOPK_GROUPS = 4
ROUTED_SCALE = 2.5
EPS = 1e-6

LANES = 128
SUBLANES = 8
HALF = LANES // 2

ROW_TILE = 512
ROUTE_TILE = 2048
RANK_SUB = 512
COMBINE_TILE = 512
SC_CHUNK = 128
SC_GATHER_PARTS = 4
EXPERT_ROWS = 1024
COMBINE_GROUPS = 2
VMEM_LIMIT = 48 * 1024 * 1024

NEG_INF = float("-inf")


def _silu(v):
    return v * (1.0 / (1.0 + jnp.exp(-v)))


def _softplus(v):
    return jnp.maximum(v, 0.0) + jnp.log(1.0 + jnp.exp(-jnp.abs(v)))


def _split_hi_lo(v):
    hi = v.astype(BF16)
    lo = (v - hi.astype(F32)).astype(BF16)
    return hi, lo


def _pack_bf16_pair(a, b):
    w = pltpu.pack_elementwise([a, b], packed_dtype=BF16)
    return w if w.dtype == jnp.uint32 else lax.bitcast_convert_type(w, jnp.uint32)


def _unpack_bf16_pair(w):
    a = pltpu.unpack_elementwise(w, index=0, packed_dtype=BF16, unpacked_dtype=F32)
    b = pltpu.unpack_elementwise(w, index=1, packed_dtype=BF16, unpacked_dtype=F32)
    return a, b


def _lane_half_mask(shape):
    return lax.broadcasted_iota(jnp.int32, shape, len(shape) - 1) < HALF


def _mod_kernel(c_ref, w_ref, b_ref, o_ref):
    a = _silu(c_ref[...])
    o_ref[...] = jnp.dot(a, w_ref[...], precision=lax.Precision.HIGHEST,
                         preferred_element_type=F32) + b_ref[...]


def _mod_call(c_pad, mod_w, mod_b):
    rows, d = c_pad.shape
    cols = mod_w.shape[1]
    return pl.pallas_call(
        _mod_kernel,
        grid=(cols // d,),
        in_specs=[pl.BlockSpec((rows, d), lambda j: (0, 0)),
                  pl.BlockSpec((d, d), lambda j: (0, j)),
                  pl.BlockSpec((1, d), lambda j: (0, j))],
        out_specs=pl.BlockSpec((rows, d), lambda j: (0, j)),
        out_shape=jax.ShapeDtypeStruct((rows, cols), F32),
        compiler_params=pltpu.CompilerParams(dimension_semantics=("arbitrary",),
                                             vmem_limit_bytes=VMEM_LIMIT),
    )(c_pad, mod_w, mod_b)


IN_PROJ_SEGMENTS = ((D_SSM, BF16), (CONV_CH, BF16), (LANES, F32), (D_ATT, BF16), (D_KV, BF16), (D_KV, BF16))


def _in_proj_kernel(x_ref, sc_ref, sh_ref, g_ref, w_hbm, *refs):
    out_refs = refs[:len(IN_PROJ_SEGMENTS)]
    res_ref, wraw_ref, w_ref, wsem = refs[len(IN_PROJ_SEGMENTS):]

    @pl.when(pl.program_id(0) == 0)
    def _():
        cp = pltpu.make_async_copy(w_hbm.at[0], wraw_ref, wsem)
        cp.start()
        cp.wait()
        src_dt = D_SSM + CONV_CH
        dst_q = src_dt + LANES
        src_q = src_dt + SSM_HEADS
        q_scale = ATT_HEAD_DIM ** -0.5

        def put(dst, src, scale=None):
            t = wraw_ref[src:src + LANES, :].T
            w_ref[:, dst:dst + LANES] = (t if scale is None else t * scale).astype(BF16)

        for c0 in range(0, src_dt, LANES):
            put(c0, c0)
        dt_tile = wraw_ref[src_dt:src_dt + LANES, :].T
        lane = lax.broadcasted_iota(jnp.int32, dt_tile.shape, 1)
        w_ref[:, src_dt:dst_q] = jnp.where(lane < SSM_HEADS, dt_tile, 0.0).astype(BF16)
        for c0 in range(0, D_ATT, LANES):
            put(dst_q + c0, src_q + c0, q_scale)
        for c0 in range(D_ATT, D_ATT + 2 * D_KV, LANES):
            put(dst_q + c0, src_q + c0)

    xf = x_ref[...]
    res_ref[...] = xf
    ms = jnp.mean(xf * xf, axis=-1, keepdims=True)
    h = xf * lax.rsqrt(ms + EPS) * g_ref[...]
    h = h * (1.0 + sc_ref[0]) + sh_ref[0]
    hb = h.astype(BF16)
    col = 0
    for (width, dtype), o_ref in zip(IN_PROJ_SEGMENTS, out_refs):
        o_ref[...] = jnp.dot(hb, w_ref[:, col:col + width], preferred_element_type=F32).astype(dtype)
        col += width


def _in_proj_call(x2, sc1, sh1, g1n, w_in_t, tiles_per_batch, tm):
    n, d = x2.shape
    row = lambda w: pl.BlockSpec((tm, w), lambda i: (i, 0))
    full = lambda a: pl.BlockSpec(a.shape, lambda i: (0, 0))
    per_batch = pl.BlockSpec((1, 1, d), lambda i: (i // tiles_per_batch, 0, 0))
    cols = sum(w for w, _ in IN_PROJ_SEGMENTS)
    w_in = w_in_t
    assert w_in.shape == (1, cols - (LANES - SSM_HEADS), d)
    return pl.pallas_call(
        _in_proj_kernel,
        grid=(n // tm,),
        in_specs=[row(d), per_batch, per_batch, full(g1n), pl.BlockSpec(memory_space=pl.ANY)],
        out_specs=[row(w) for w, _ in IN_PROJ_SEGMENTS] + [row(d)],
        out_shape=[jax.ShapeDtypeStruct((n, w), dt) for w, dt in IN_PROJ_SEGMENTS] + [jax.ShapeDtypeStruct((n, d), F32)],
        scratch_shapes=[pltpu.VMEM(w_in.shape[1:], w_in.dtype), pltpu.VMEM((d, cols), BF16), pltpu.SemaphoreType.DMA(())],
        compiler_params=pltpu.CompilerParams(dimension_semantics=("arbitrary",),
                                             vmem_limit_bytes=VMEM_LIMIT),
    )(x2, sc1, sh1, g1n, w_in)


SSD_SEQS = 4
CONV_HALO = 16


def _conv_shift_matrix():
    s = np.zeros((CONV_K * CHUNK, CONV_HALO + CHUNK), np.float32)
    for k in range(CONV_K):
        t = np.arange(CHUNK)
        s[k * CHUNK + t, CONV_HALO + t - (CONV_K - 1) + k] = 1.0
    return s


def _silu_tanh(v):
    hv = 0.5 * v
    return hv + hv * jnp.tanh(hv)


def _ssd_kernel(xbc_ref, z_ref, dt_ref, cw_ref, cb_ref, dtb_ref, alog_ref, dskip_ref, ng_ref, triu_ref, shift_ref,
                y_ref, state_ref, ucat_ref, ybuf_ref):
    nseq = xbc_ref.shape[0]

    @pl.when(pl.program_id(1) == 0)
    def _():
        state_ref[...] = jnp.zeros_like(state_ref)
        ucat_ref[:, 0:CONV_HALO, :] = jnp.zeros((nseq, CONV_HALO, CONV_CH), BF16)

    for q in range(nseq):
        _ssd_chunk(xbc_ref.at[q], z_ref.at[q], dt_ref.at[q], cw_ref, cb_ref, dtb_ref, alog_ref, dskip_ref, ng_ref,
                   triu_ref, shift_ref, y_ref.at[q], state_ref.at[q], ucat_ref.at[q], ybuf_ref.at[q])


def _ssd_chunk(xbc_ref, z_ref, dt_ref, cw_ref, cb_ref, dtb_ref, alog_ref, dskip_ref, ng_ref, triu_ref, shift_ref,
               y_ref, state_ref, ucat_ref, ybuf_ref):
    ucat_ref[CONV_HALO:, :] = xbc_ref[...]
    shifted = jnp.dot(shift_ref[...], ucat_ref[...], preferred_element_type=F32)
    ucat_ref[0:CONV_HALO, :] = ucat_ref[CHUNK:CHUNK + CONV_HALO, :]
    acc = cb_ref[...] + cw_ref[0:1, :] * shifted[0:CHUNK]
    for kk in range(1, CONV_K):
        acc = acc + cw_ref[kk:kk + 1, :] * shifted[kk * CHUNK:(kk + 1) * CHUNK]
    act = _silu_tanh(acc)
    xs = act[:, :D_SSM]
    gn = SSM_GROUPS * D_STATE

    dt_t = _softplus(dt_ref[...].T[0:SSM_HEADS, :] + dtb_ref[...])
    a_t = dt_t * (-jnp.exp(alog_ref[...]))
    a_hi = a_t.astype(BF16)
    a_mid = (a_t - a_hi.astype(F32)).astype(BF16)
    a_lo = (a_t - a_hi.astype(F32) - a_mid.astype(F32)).astype(BF16)
    triu = triu_ref[...]
    cs_t = (jnp.dot(a_hi, triu, preferred_element_type=F32) + jnp.dot(a_mid, triu, preferred_element_type=F32)
            + jnp.dot(a_lo, triu, preferred_element_type=F32))
    cs_end = cs_t[:, CHUNK - 1:CHUNK]
    r_t = cs_t - jnp.log(dt_t)
    w_t = jnp.exp(cs_end - cs_t) * dt_t
    chunk_decay = jnp.exp(cs_end)
    cols = jnp.concatenate([cs_t, jnp.exp(cs_t), jnp.zeros((LANES - 2 * SSM_HEADS, CHUNK), F32)], axis=0).T

    li = lax.broadcasted_iota(jnp.int32, (CHUNK, CHUNK), 0)
    si = lax.broadcasted_iota(jnp.int32, (CHUNK, CHUNK), 1)
    causal = li >= si
    low = _lane_half_mask((CHUNK, LANES))
    low_row = _lane_half_mask((1, LANES))

    heads_per_group = SSM_HEADS // SSM_GROUPS
    for g in range(SSM_GROUPS):
        b_g = act[:, D_SSM + g * D_STATE:D_SSM + (g + 1) * D_STATE]
        c_g = act[:, D_SSM + gn + g * D_STATE:D_SSM + gn + (g + 1) * D_STATE]
        b_gb = b_g.astype(BF16)
        c_gb = c_g.astype(BF16)
        cb = lax.dot_general(c_gb, b_gb, (((1,), (1,)), ((), ())), preferred_element_type=F32)
        b_t = b_g.T
        for jp in range(heads_per_group // 2):
            j = g * (heads_per_group // 2) + jp
            lanes = slice(j * LANES, (j + 1) * LANES)
            xp = xs[:, lanes]
            xpb = xp.astype(BF16)
            ydiag = jnp.zeros((CHUNK, LANES), F32)
            snew = jnp.zeros((D_STATE, LANES), F32)
            for half in range(2):
                h = 2 * j + half
                diff = cols[:, h:h + 1] - r_t[h:h + 1, :]
                m = (cb * jnp.exp(jnp.where(causal, diff, NEG_INF))).astype(BF16)
                keep = low if half == 0 else jnp.logical_not(low)
                xh = jnp.where(keep, xpb, jnp.zeros_like(xpb))
                ydiag = ydiag + jnp.dot(m, xh, preferred_element_type=F32)
                snew = snew + jnp.dot((b_t * w_t[h:h + 1, :]).astype(BF16), xh, preferred_element_type=F32)
            s_in = state_ref[:, lanes]
            yoff = jnp.dot(c_gb, s_in.astype(BF16), preferred_element_type=F32)
            h0 = 2 * j
            e0 = SSM_HEADS + h0
            escale = jnp.where(low, cols[:, e0:e0 + 1], cols[:, e0 + 1:e0 + 2])
            cdec = jnp.where(low_row, chunk_decay[h0:h0 + 1, :], chunk_decay[h0 + 1:h0 + 2, :])
            ybuf_ref[:, lanes] = ydiag + yoff * escale + xp * dskip_ref[:, lanes]
            state_ref[:, lanes] = s_in * cdec + snew

    yz = ybuf_ref[...] * _silu_tanh(z_ref[...].astype(F32))
    gw = D_SSM // SSM_GROUPS
    for g in range(SSM_GROUPS):
        part = yz[:, g * gw:(g + 1) * gw]
        ms = jnp.mean(part * part, axis=-1, keepdims=True)
        y_ref[:, g * gw:(g + 1) * gw] = (part * lax.rsqrt(ms + EPS)
                                            * ng_ref[:, g * gw:(g + 1) * gw]).astype(BF16)


def _ssd_call(xbc, z, dt, conv_w, conv_b, dtb, alog, dskip, ng, triu, shift):
    bsz, l, _ = xbc.shape
    nc = l // CHUNK
    nseq = SSD_SEQS if bsz % SSD_SEQS == 0 else 1
    chunk = lambda w: pl.BlockSpec((nseq, CHUNK, w), lambda b, c: (b, c, 0))
    full = lambda a: pl.BlockSpec(a.shape, lambda b, c: (0, 0))
    return pl.pallas_call(
        _ssd_kernel,
        grid=(bsz // nseq, nc),
        in_specs=[chunk(CONV_CH), chunk(D_SSM), chunk(LANES), full(conv_w), full(conv_b), full(dtb),
                  full(alog), full(dskip), full(ng), full(triu), full(shift)],
        out_specs=chunk(D_SSM),
        out_shape=jax.ShapeDtypeStruct((bsz, l, D_SSM), BF16),
        scratch_shapes=[pltpu.VMEM((nseq, D_STATE, D_SSM), F32),
                        pltpu.VMEM((nseq, CONV_HALO + CHUNK, CONV_CH), BF16),
                        pltpu.VMEM((nseq, CHUNK, D_SSM), F32)],
        compiler_params=pltpu.CompilerParams(dimension_semantics=("arbitrary", "arbitrary"),
                                             vmem_limit_bytes=VMEM_LIMIT),
    )(xbc, z, dt, conv_w, conv_b, dtb, alog, dskip, ng, triu, shift)


assert WINDOW == ATT_BLOCK


def _rel_bucket_table():
    qi = np.arange(ATT_BLOCK)[:, None]
    c = np.arange(ATT_BLOCK)[None, :]
    dist = np.where(c > qi, qi + ATT_BLOCK - c, qi - c)
    max_exact = REL_BUCKETS // 2
    d = np.maximum(dist, 1).astype(np.float32)
    large = max_exact + (np.log(d / np.float32(max_exact)) / np.float32(math.log(REL_MAX_DIST / max_exact))
                         * np.float32(REL_BUCKETS - max_exact)).astype(np.int32)
    large = np.minimum(large, REL_BUCKETS - 1)
    return np.where(dist < max_exact, dist, large).astype(np.int32)


def _bias_kernel(rb_ref, bucket_ref, o_ref):
    bucket = bucket_ref[...]
    from_prev = (lax.broadcasted_iota(jnp.int32, bucket.shape, 1)
                 > lax.broadcasted_iota(jnp.int32, bucket.shape, 0))
    for h in range(ATT_HEADS):
        acc = jnp.zeros(bucket.shape, F32)
        for b in range(REL_BUCKETS):
            acc = jnp.where(bucket == b, rb_ref[b, h], acc)
        o_ref[1, h] = acc
        o_ref[0, h] = jnp.where(from_prev, NEG_INF, acc)


def _bias_call(rel_bias, bucket):
    return pl.pallas_call(
        _bias_kernel,
        in_specs=[pl.BlockSpec(memory_space=pltpu.SMEM), pl.BlockSpec(memory_space=pltpu.VMEM)],
        out_shape=jax.ShapeDtypeStruct((2, ATT_HEADS) + bucket.shape, F32),
    )(rel_bias, bucket)


ATT_SEQS = 4


def _attn_kernel(sink_ref, q_ref, kp_ref, kc_ref, vp_ref, vc_ref, bias_ref, ng_ref, o_ref, obuf_ref):
    for s in range(q_ref.shape[0]):
        _attn_block(sink_ref, q_ref.at[s], kp_ref.at[s], kc_ref.at[s], vp_ref.at[s], vc_ref.at[s], bias_ref.at[0],
                    ng_ref, o_ref.at[s], obuf_ref.at[s])


def _attn_block(sink_ref, q_ref, kp_ref, kc_ref, vp_ref, vc_ref, bias_ref, ng_ref, o_ref, obuf_ref):
    qi = lax.broadcasted_iota(jnp.int32, (ATT_BLOCK, ATT_BLOCK), 0)
    ci = lax.broadcasted_iota(jnp.int32, (ATT_BLOCK, ATT_BLOCK), 1)
    from_prev = ci > qi
    low = _lane_half_mask((ATT_BLOCK, LANES))

    def band_variants(prev_ref, cur_ref):
        out = []
        for cpair in range(KV_HEADS // 2):
            lanes = slice(cpair * LANES, (cpair + 1) * LANES)
            t = jnp.concatenate([prev_ref[:, lanes], cur_ref[:, lanes]], axis=0).astype(F32)
            out.append((t.astype(BF16), pltpu.roll(t, HALF, 1).astype(BF16)))
        return out

    k_band = band_variants(kp_ref, kc_ref)
    v_band = band_variants(vp_ref, vc_ref)
    nt = (((1,), (1,)), ((), ()))

    for j in range(ATT_HEADS // 2):
        qp = q_ref[:, j * LANES:(j + 1) * LANES]
        out_pair = jnp.zeros((ATT_BLOCK, LANES), F32)
        for half in range(2):
            h = 2 * j + half
            g = h // Q_PER_KV
            swapped = int((g % 2) != half)
            keep = low if half == 0 else jnp.logical_not(low)
            qh = jnp.where(keep, qp, jnp.zeros_like(qp))
            s_band = lax.dot_general(qh, k_band[g // 2][swapped], nt, preferred_element_type=F32)
            s = jnp.where(from_prev, s_band[:, :ATT_BLOCK], s_band[:, ATT_BLOCK:]) + bias_ref[h]
            sink = sink_ref[h]
            m = jnp.maximum(jnp.max(s, axis=-1, keepdims=True), sink)
            p = jnp.exp(s - m)
            denom = jnp.sum(p, axis=-1, keepdims=True) + jnp.exp(sink - m)
            p_band = jnp.concatenate([jnp.where(from_prev, p, 0.0), jnp.where(from_prev, 0.0, p)], axis=1)
            o = jnp.dot(p_band.astype(BF16), v_band[g // 2][swapped], preferred_element_type=F32) / denom
            out_pair = out_pair + jnp.where(keep, o, 0.0)
        obuf_ref[:, j * LANES:(j + 1) * LANES] = out_pair

    att = obuf_ref[...]
    ms = jnp.mean(att * att, axis=-1, keepdims=True)
    o_ref[...] = (att * lax.rsqrt(ms + EPS) * ng_ref[...]).astype(BF16)


def _attn_call(sinks, q, k, v, bias, ng):
    bsz, l, _ = q.shape
    nb = l // ATT_BLOCK
    nseq = ATT_SEQS if bsz % ATT_SEQS == 0 else 1
    cur = lambda w: pl.BlockSpec((nseq, ATT_BLOCK, w), lambda b, i: (b, i, 0))
    prev = lambda w: pl.BlockSpec((nseq, ATT_BLOCK, w), lambda b, i: (b, jnp.maximum(i - 1, 0), 0))
    return pl.pallas_call(
        _attn_kernel,
        grid=(bsz // nseq, nb),
        in_specs=[pl.BlockSpec(memory_space=pltpu.SMEM),
                  cur(D_ATT), prev(D_KV), cur(D_KV), prev(D_KV), cur(D_KV),
                  pl.BlockSpec((1,) + bias.shape[1:], lambda b, i: (jnp.minimum(i, 1), 0, 0, 0)),
                  pl.BlockSpec(ng.shape, lambda b, i: (0, 0))],
        out_specs=cur(D_ATT),
        out_shape=jax.ShapeDtypeStruct((bsz, l, D_ATT), BF16),
        scratch_shapes=[pltpu.VMEM((nseq, ATT_BLOCK, D_ATT), F32)],
        compiler_params=pltpu.CompilerParams(dimension_semantics=("arbitrary", "arbitrary"),
                                             vmem_limit_bytes=VMEM_LIMIT),
    )(sinks, q, k, k, v, v, bias, ng)


def _stage_bf16(staged, wsem):
    copies = [pltpu.make_async_copy(src.at[0], raw, wsem.at[j]) for j, (src, raw, _) in enumerate(staged)]
    for cp in copies:
        cp.start()
    for cp, (_, raw, dst) in zip(copies, staged):
        cp.wait()
        dst[...] = raw[...].astype(BF16)


def _out_proj_kernel(x_ref, ys_ref, ya_ref, g1_ref, sc_ref, sh_ref, ng_ref, wo_hbm, rwh_ref, rwl_ref, rb_ref,
                     upper_ref, carry_in_ref,
                     x1_ref, h_ref, idx_ref, gate_ref, rank_ref, slot_ref, cnt_ref,
                     carry_ref, wo_raw, wo_ref, wsem, *, cap):
    @pl.when(pl.program_id(0) == 0)
    def _():
        carry_ref[...] = carry_in_ref[...]
        _stage_bf16(((wo_hbm, wo_raw, wo_ref),), wsem)

    mix = (jnp.dot(ys_ref[...], wo_ref[:D_SSM, :], preferred_element_type=F32)
           + jnp.dot(ya_ref[...], wo_ref[D_SSM:, :], preferred_element_type=F32))
    x1 = x_ref[...] + g1_ref[0] * mix
    ms = jnp.mean(x1 * x1, axis=-1, keepdims=True)
    h = x1 * lax.rsqrt(ms + EPS) * ng_ref[...]
    h = h * (1.0 + sc_ref[0]) + sh_ref[0]
    half = h.shape[1] // 2
    h_ref[...] = _pack_bf16_pair(h[:, :half], h[:, half:])
    hi, lo = _split_hi_lo(h)
    hi_terms = jnp.dot(hi, rwl_ref[...], preferred_element_type=F32)
    logits = (hi_terms[:, :LANES] + hi_terms[:, LANES:]
              + jnp.dot(lo, rwh_ref[...], preferred_element_type=F32))
    _route_tokens(logits.T[0:N_EXPERTS, :], rb_ref, upper_ref, idx_ref, gate_ref, rank_ref, cnt_ref, carry_ref)
    slot_ref[...] = idx_ref[...] * cap + rank_ref[...]
    x1_ref[...] = x1


def _out_proj_call(x2, ys, ya, g1, sc2, sh2, ng, w_out, rwh, rwl, router_bias, upper, carry_in,
                   tiles_per_batch, tm, tile0, ntiles, cap):
    n, d = x2.shape
    nt = ntiles * tm
    row = lambda w: pl.BlockSpec((tm, w), lambda i: (i + tile0, 0))
    own = lambda w: pl.BlockSpec((tm, w), lambda i: (i, 0))
    tok = lambda r: pl.BlockSpec((r, tm), lambda i: (0, i))
    full = lambda a: pl.BlockSpec(a.shape, lambda i: (0, 0))
    per_batch = pl.BlockSpec((1, 1, d), lambda i: ((i + tile0) // tiles_per_batch, 0, 0))
    hbm = pl.BlockSpec(memory_space=pl.ANY)
    staged = (w_out,)
    return pl.pallas_call(
        functools.partial(_out_proj_kernel, cap=cap),
        grid=(ntiles,),
        in_specs=[row(d), row(D_SSM), row(D_ATT), per_batch, per_batch, per_batch, full(ng),
                  hbm, full(rwh), full(rwl), full(router_bias), full(upper), full(carry_in)],
        out_specs=[row(d), own(d // 2), tok(TOP_K), tok(TOP_K), tok(TOP_K), tok(TOP_K),
                   pl.BlockSpec((N_EXPERTS, LANES), lambda i: (0, 0))],
        out_shape=[jax.ShapeDtypeStruct((n, d), F32), jax.ShapeDtypeStruct((nt, d // 2), jnp.uint32),
                   jax.ShapeDtypeStruct((TOP_K, nt), jnp.int32), jax.ShapeDtypeStruct((TOP_K, nt), F32),
                   jax.ShapeDtypeStruct((TOP_K, nt), jnp.int32), jax.ShapeDtypeStruct((TOP_K, nt), jnp.int32),
                   jax.ShapeDtypeStruct((N_EXPERTS, LANES), F32)],
        scratch_shapes=([pltpu.VMEM((N_EXPERTS, LANES), F32)]
                        + [pltpu.VMEM(a.shape[1:], a.dtype) for a in staged]
                        + [pltpu.VMEM(a.shape[1:], BF16) for a in staged]
                        + [pltpu.SemaphoreType.DMA((len(staged),))]),
        input_output_aliases={0: 0},
        compiler_params=pltpu.CompilerParams(dimension_semantics=("arbitrary",),
                                             vmem_limit_bytes=VMEM_LIMIT),
    )(x2, ys, ya, g1, sc2, sh2, ng, w_out, rwh, rwl, router_bias, upper, carry_in)


def _route_tokens(logits_t, rb_ref, upper_ref, idx_ref, gate_ref, rank_ref, cnt_ref, carry_ref):
    t = logits_t.shape[1]
    per_group = N_EXPERTS // ROUTE_GROUPS
    scores = 1.0 / (1.0 + jnp.exp(-logits_t))
    sel = scores + rb_ref[...]
    e_iota = lax.broadcasted_iota(jnp.int32, (N_EXPERTS, t), 0)

    sel3 = sel.reshape(ROUTE_GROUPS, per_group, t)
    w_iota = lax.broadcasted_iota(jnp.int32, sel3.shape, 1)
    m1 = jnp.max(sel3, axis=1, keepdims=True)
    first = jnp.min(jnp.where(sel3 == m1, w_iota, per_group), axis=1, keepdims=True)
    m2 = jnp.max(jnp.where(w_iota == first, NEG_INF, sel3), axis=1, keepdims=True)
    grp = (m1 + m2).reshape(ROUTE_GROUPS, t)

    g_iota = lax.broadcasted_iota(jnp.int32, (ROUTE_GROUPS, t), 0)
    gmask = jnp.zeros((ROUTE_GROUPS, t), jnp.bool_)
    for _ in range(ROUTE_TOPK_GROUPS):
        gm = jnp.max(grp, axis=0, keepdims=True)
        gfirst = jnp.min(jnp.where(grp == gm, g_iota, ROUTE_GROUPS), axis=0, keepdims=True)
        hit = g_iota == gfirst
        gmask = jnp.logical_or(gmask, hit)
        grp = jnp.where(hit, NEG_INF, grp)
    allowed = jnp.broadcast_to(gmask.reshape(ROUTE_GROUPS, 1, t),
                               (ROUTE_GROUPS, per_group, t)).reshape(N_EXPERTS, t)
    masked = jnp.where(allowed, sel, NEG_INF)

    picked = jnp.zeros((N_EXPERTS, t), jnp.bool_)
    idx_rows = []
    w_rows = []
    for _ in range(TOP_K):
        mm = jnp.max(masked, axis=0, keepdims=True)
        efirst = jnp.min(jnp.where(masked == mm, e_iota, N_EXPERTS), axis=0, keepdims=True)
        hit = e_iota == efirst
        idx_rows.append(efirst)
        w_rows.append(jnp.sum(jnp.where(hit, scores, 0.0), axis=0, keepdims=True))
        picked = jnp.logical_or(picked, hit)
        masked = jnp.where(hit, NEG_INF, masked)
    idx = jnp.concatenate(idx_rows, axis=0)
    w = jnp.concatenate(w_rows, axis=0)
    gate_ref[...] = w / jnp.sum(w, axis=0, keepdims=True) * ROUTED_SCALE
    idx_ref[...] = idx

    onehot = jnp.where(picked, 1.0, 0.0)
    sub = upper_ref.shape[0]
    carry = carry_ref[:, 0:1]
    parts = []
    for s0 in range(0, t, sub):
        oh = onehot[:, s0:s0 + sub]
        parts.append(jnp.dot(oh.astype(BF16), upper_ref[...], preferred_element_type=F32) + carry)
        carry = carry + jnp.sum(oh, axis=1, keepdims=True)
    rank_full = jnp.concatenate(parts, axis=1)
    rank_rows = [jnp.sum(jnp.where(e_iota == idx_rows[k], rank_full, 0.0), axis=0, keepdims=True)
                 for k in range(TOP_K)]
    rank_ref[...] = jnp.concatenate(rank_rows, axis=0).astype(jnp.int32)
    carry_ref[...] = jnp.broadcast_to(carry, carry_ref.shape)
    cnt_ref[...] = carry_ref[...]


PLAN_EXPERT, PLAN_FRESH, PLAN_VALID, PLAN_SEG, PLAN_NEXT, PLAN_NUSED, PLAN_XROW = range(7)
PLAN_ROWS = SUBLANES


def _plan_kernel(cnt_ref, tri_ref, start_ref, plan_ref, *, nblocks, cap):
    nbp = plan_ref.shape[1]
    cnt = cnt_ref[...].astype(jnp.int32)
    blocks = (cnt + (EXPERT_ROWS - 1)) // EXPERT_ROWS
    end = jnp.dot(tri_ref[...], blocks.astype(F32), precision=lax.Precision.HIGHEST,
                  preferred_element_type=F32).astype(jnp.int32)
    start = end - blocks
    start_ref[...] = start[:, 0:1] * EXPERT_ROWS
    nused = end[N_EXPERTS - 1:N_EXPERTS, 0:1]

    e_iota = lax.broadcasted_iota(jnp.int32, (N_EXPERTS, nbp), 0)
    blk = lax.broadcasted_iota(jnp.int32, (1, nbp), 1)
    expert = jnp.minimum(jnp.sum((end[:, 0:1] <= blk).astype(jnp.int32), axis=0, keepdims=True), N_EXPERTS - 1)
    mine = e_iota == expert
    pick = lambda col: jnp.sum(jnp.where(mine, col, 0), axis=0, keepdims=True)
    first = pick(start[:, 0:1])
    valid = jnp.clip(pick(cnt[:, 0:1]) - (blk - first) * EXPERT_ROWS, 0, EXPERT_ROWS)
    present = jnp.logical_or(blocks[:, 0:1] > 0,
                             jnp.logical_and(e_iota[:, 0:1] == N_EXPERTS - 1, nused < nblocks))
    seg = jnp.sum(jnp.logical_and(present, e_iota <= expert).astype(jnp.int32), axis=0, keepdims=True) - 1
    nxt = jnp.min(jnp.where(jnp.logical_and(present, e_iota > expert), e_iota, N_EXPERTS), axis=0, keepdims=True)
    rows = {PLAN_EXPERT: expert, PLAN_FRESH: (blk == first).astype(jnp.int32), PLAN_VALID: valid, PLAN_SEG: seg,
            PLAN_NEXT: jnp.where(nxt == N_EXPERTS, -1, nxt), PLAN_NUSED: jnp.broadcast_to(nused, (1, nbp)),
            PLAN_XROW: expert * cap + (blk - first) * EXPERT_ROWS}
    zero = jnp.zeros((1, nbp), jnp.int32)
    plan_ref[...] = jnp.concatenate([rows.get(r, zero) for r in range(PLAN_ROWS)], axis=0)


def _plan_call(counts, nblocks, cap):
    nbp = -(-nblocks // LANES) * LANES
    tri = jnp.asarray(np.tril(np.ones((N_EXPERTS, N_EXPERTS), np.float32)))
    return pl.pallas_call(
        functools.partial(_plan_kernel, nblocks=nblocks, cap=cap),
        out_shape=[jax.ShapeDtypeStruct((N_EXPERTS, 1), jnp.int32),
                   jax.ShapeDtypeStruct((PLAN_ROWS, nbp), jnp.int32)],
    )(counts, tri)


def _dest_kernel(idx_ref, rank_ref, start_ref, dest_ref):
    t = idx_ref.shape[1]
    e_iota = lax.broadcasted_iota(jnp.int32, (N_EXPERTS, t), 0)
    rows = [jnp.sum(jnp.where(e_iota == idx_ref[k:k + 1, :], start_ref[...], 0), axis=0, keepdims=True)
            for k in range(TOP_K)]
    dest = jnp.concatenate(rows, axis=0) + rank_ref[...]
    for k in range(TOP_K):
        for c in range(t // LANES):
            dest_ref[k, c:c + 1, :] = dest[k:k + 1, c * LANES:(c + 1) * LANES]


def _dest_call(idx, rank, pad_start, tile):
    n = idx.shape[1]
    tok = pl.BlockSpec((TOP_K, tile), lambda i: (0, i))
    return pl.pallas_call(
        _dest_kernel,
        grid=(n // tile,),
        in_specs=[tok, tok, pl.BlockSpec((N_EXPERTS, 1), lambda i: (0, 0))],
        out_specs=pl.BlockSpec((TOP_K, tile // LANES, LANES), lambda i: (0, i, 0)),
        out_shape=jax.ShapeDtypeStruct((TOP_K, n // LANES, LANES), jnp.int32),
        compiler_params=pltpu.CompilerParams(dimension_semantics=("arbitrary",)),
    )(idx, rank, pad_start)


def _scatter_rows_sc(rows, dest_flat, into, chunk):
    n, w = rows.shape
    copies = dest_flat.shape[0] // n
    info = plsc.get_sparse_core_info()
    nc = info.num_cores
    per_worker = n // (nc * info.num_subcores)
    assert per_worker * nc * info.num_subcores == n and per_worker % chunk == 0
    mesh = plsc.VectorSubcoreMesh(core_axis_name="c", subcore_axis_name="s")
    steps = per_worker // chunk
    scratch_types = ([pltpu.VMEM((chunk,), jnp.int32) for _ in range(2 * copies)]
                     + [pltpu.VMEM((chunk, w), rows.dtype) for _ in range(2)]
                     + [pltpu.SemaphoreType.DMA for _ in range(4)])

    def scatter(rows_hbm, idx_hbm, out_hbm, *scratch):
        idx_vs = (scratch[:copies], scratch[copies:2 * copies])
        rows_vs, lsems, ssems = scratch[2 * copies:2 * copies + 2], scratch[-4:-2], scratch[-2:]
        worker = lax.axis_index("s") * nc + lax.axis_index("c")

        def fetch(j):
            s, base = j % 2, worker * per_worker + j * chunk
            return ([pltpu.async_copy(idx_hbm.at[pl.ds(k * n + base, chunk)], idx_vs[s][k], lsems[s])
                     for k in range(copies)]
                    + [pltpu.async_copy(rows_hbm.at[pl.ds(base, chunk)], rows_vs[s], lsems[s])])

        stores = [[], []]
        loads = fetch(0)
        for j in range(steps):
            s = j % 2
            following = None
            if j + 1 < steps:
                for st in stores[1 - s]:
                    st.wait()
                stores[1 - s] = []
                following = fetch(j + 1)
            for ld in loads:
                ld.wait()
            stores[s] = [pltpu.async_copy(rows_vs[s], out_hbm.at[idx_vs[s][k]], ssems[s]) for k in range(copies)]
            loads = following
        for st in stores[0] + stores[1]:
            st.wait()

    if isinstance(into, int):
        return pl.kernel(scatter, mesh=mesh, out_type=jax.ShapeDtypeStruct((into, w), rows.dtype),
                         scratch_types=scratch_types)(rows, dest_flat)
    out = jax.new_ref(into)
    pl.kernel(scatter, mesh=mesh, out_type=(), scratch_types=scratch_types)(rows, dest_flat, out)
    return jax.freeze(out)


EXPERT_LIVE_STEP = 256
X_RING = 4


def _expert_kernel(plan_ref, xs_hbm, wg_hbm, wu_hbm, wd_hbm, y_ref,
                   wgb_ref, wub_ref, wdb_ref, xbuf_ref, xsem, wg_raw, wu_raw, wd_raw, wsem):
    i = pl.program_id(0)
    rows, half = xbuf_ref.shape[1], xbuf_ref.shape[2]
    nused = plan_ref[PLAN_NUSED, 0]

    def w_copies(expert, slot):
        return [pltpu.make_async_copy(src.at[expert], dst.at[slot], wsem.at[slot])
                for src, dst in ((wg_hbm, wg_raw), (wu_hbm, wu_raw), (wd_hbm, wd_raw))]

    def x_copy(block):
        first = pl.multiple_of(plan_ref[PLAN_XROW, block], rows)
        slot = block % X_RING
        return pltpu.make_async_copy(xs_hbm.at[pl.ds(first, rows)], xbuf_ref.at[slot], xsem.at[slot])

    @pl.when(i == 0)
    def _():
        for b in range(X_RING - 1):
            @pl.when(b < nused)
            def _():
                x_copy(b).start()

    @pl.when(i + (X_RING - 1) < nused)
    def _():
        x_copy(i + (X_RING - 1)).start()

    @pl.when(i == 0)
    def _():
        for c in w_copies(plan_ref[PLAN_EXPERT, 0], 0):
            c.start()

    @pl.when(plan_ref[PLAN_FRESH, i] > 0)
    def _():
        slot = plan_ref[PLAN_SEG, i] % 2
        for c in w_copies(plan_ref[PLAN_EXPERT, i], slot):
            c.wait()

        @pl.when(plan_ref[PLAN_NEXT, i] >= 0)
        def _():
            for c in w_copies(plan_ref[PLAN_NEXT, i], 1 - slot):
                c.start()

        wgb_ref[...] = wg_raw[slot].astype(BF16)
        wub_ref[...] = wu_raw[slot].astype(BF16)
        wdb_ref[...] = wd_raw[slot].astype(BF16)

    @pl.when(i < nused)
    def _():
        x_copy(i).wait()

    real = plan_ref[PLAN_VALID, i]
    slot = i % X_RING

    def ffn(live):
        xw = xbuf_ref[slot, 0:live, :]
        row = lax.broadcasted_iota(jnp.int32, xw.shape, 0)
        x_lo, x_hi = _unpack_bf16_pair(jnp.where(row < real, xw, jnp.uint32(0)))
        x_lo = x_lo.astype(BF16)
        x_hi = x_hi.astype(BF16)
        gate = (jnp.dot(x_lo, wgb_ref[:half, :], preferred_element_type=F32)
                + jnp.dot(x_hi, wgb_ref[half:, :], preferred_element_type=F32))
        up = (jnp.dot(x_lo, wub_ref[:half, :], preferred_element_type=F32)
              + jnp.dot(x_hi, wub_ref[half:, :], preferred_element_type=F32))
        u = (_silu(gate) * up).astype(BF16)
        y_lo = jnp.dot(u, wdb_ref[:, :half], preferred_element_type=F32)
        y_hi = jnp.dot(u, wdb_ref[:, half:], preferred_element_type=F32)
        y_ref[0:live, :] = _pack_bf16_pair(y_lo, y_hi)
        if live < rows:
            y_ref[live:, :] = jnp.zeros((rows - live, half), y_ref.dtype)

    pieces = (real + (EXPERT_LIVE_STEP - 1)) // EXPERT_LIVE_STEP
    for p in range(1, rows // EXPERT_LIVE_STEP + 1):
        @pl.when(jnp.logical_and(i < nused, pieces == p))
        def _():
            ffn(p * EXPERT_LIVE_STEP)

    @pl.when(i == nused)
    def _():
        y_ref[...] = jnp.zeros_like(y_ref)


def _expert_call(plan, xs, wg, wu, wd, rows, nblocks):
    p, w = nblocks * rows, xs.shape[1]
    d, f = wg.shape[1], wg.shape[2]
    hbm = pl.BlockSpec(memory_space=pl.ANY)
    grid_spec = pltpu.PrefetchScalarGridSpec(
        num_scalar_prefetch=1,
        grid=(p // rows,),
        in_specs=[hbm, hbm, hbm, hbm],
        out_specs=pl.BlockSpec((rows, w), lambda i, plan: (jnp.where(i < plan[PLAN_NUSED, 0], i, nblocks - 1), 0)),
        scratch_shapes=[pltpu.VMEM((d, f), BF16), pltpu.VMEM((d, f), BF16), pltpu.VMEM((f, d), BF16),
                        pltpu.VMEM((X_RING, rows, w), xs.dtype), pltpu.SemaphoreType.DMA((X_RING,)),
                        pltpu.VMEM((2, d, f), wg.dtype), pltpu.VMEM((2, d, f), wu.dtype),
                        pltpu.VMEM((2, f, d), wd.dtype), pltpu.SemaphoreType.DMA((2,))],
    )
    return pl.pallas_call(
        _expert_kernel,
        grid_spec=grid_spec,
        out_shape=jax.ShapeDtypeStruct((p, w), jnp.uint32),
        compiler_params=pltpu.CompilerParams(dimension_semantics=("arbitrary",),
                                             vmem_limit_bytes=VMEM_LIMIT),
    )(plan, xs, wg, wu, wd)


def _gather_rows_sc(table, idx, chunk):
    m = idx.shape[0]
    w = table.shape[1]
    info = plsc.get_sparse_core_info()
    nc = info.num_cores
    per_worker = m // (nc * info.num_subcores)
    assert per_worker * nc * info.num_subcores == m and per_worker % chunk == 0
    part = chunk // SC_GATHER_PARTS
    mesh = plsc.VectorSubcoreMesh(core_axis_name="c", subcore_axis_name="s")

    @functools.partial(
        pl.kernel, mesh=mesh,
        out_type=jax.ShapeDtypeStruct((m, w), table.dtype),
        scratch_types=([pltpu.VMEM((part,), jnp.int32) for _ in range(SC_GATHER_PARTS)]
                       + [pltpu.VMEM((part, w), table.dtype) for _ in range(SC_GATHER_PARTS)]
                       + [pltpu.SemaphoreType.DMA] * 3),
    )
    def gather(table_hbm, idx_hbm, out_hbm, *scratch):
        idx_vs = scratch[:SC_GATHER_PARTS]
        rows_vs = scratch[SC_GATHER_PARTS:2 * SC_GATHER_PARTS]
        isem, gsem, wsem = scratch[2 * SC_GATHER_PARTS:]
        worker = lax.axis_index("s") * nc + lax.axis_index("c")

        @pl.loop(0, per_worker // chunk)
        def _(j):
            base = worker * per_worker + j * chunk
            loads = [pltpu.async_copy(idx_hbm.at[pl.ds(base + p * part, part)], idx_vs[p], isem)
                     for p in range(SC_GATHER_PARTS)]
            gathers = []
            for p in range(SC_GATHER_PARTS):
                loads[p].wait()
                gathers.append(pltpu.async_copy(table_hbm.at[idx_vs[p]], rows_vs[p], gsem))
            writes = []
            for p in range(SC_GATHER_PARTS):
                gathers[p].wait()
                writes.append(pltpu.async_copy(rows_vs[p], out_hbm.at[pl.ds(base + p * part, part)], wsem))
            for wr in writes:
                wr.wait()

    return gather(table, idx)


def _combine_kernel(yk_ref, gate_ref, base_ref, h_ref, g2_ref, fg_ref, sg_hbm, su_hbm, sd_hbm, o_ref,
                    sg_raw, su_raw, sd_raw, sg_ref, su_ref, sd_ref, wsem):
    @pl.when(pl.program_id(0) == 0)
    def _():
        _stage_bf16(((sg_hbm, sg_raw, sg_ref), (su_hbm, su_raw, su_ref), (sd_hbm, sd_raw, sd_ref)), wsem)

    t = base_ref.shape[0]
    half = yk_ref.shape[2]
    h_lo, h_hi = _unpack_bf16_pair(h_ref[...])
    hb = jnp.concatenate([h_lo.astype(BF16), h_hi.astype(BF16)], axis=1)
    u = _silu(jnp.dot(hb, sg_ref[...], preferred_element_type=F32)) * jnp.dot(hb, su_ref[...],
                                                                              preferred_element_type=F32)
    shared = jnp.dot(u.astype(BF16), sd_ref[...], preferred_element_type=F32)
    gates = jnp.concatenate([gate_ref[...], jnp.zeros((LANES - TOP_K, t), F32)], axis=0).T
    r_lo = shared[:, :half]
    r_hi = shared[:, half:]
    for k in range(TOP_K):
        y_lo, y_hi = _unpack_bf16_pair(yk_ref[k])
        r_lo = r_lo + gates[:, k:k + 1] * y_lo
        r_hi = r_hi + gates[:, k:k + 1] * y_hi
    g2 = g2_ref[0]
    x_lo = base_ref[:, :half] + g2[:, :half] * r_lo
    x_hi = base_ref[:, half:] + g2[:, half:] * r_hi
    ms = (jnp.sum(x_lo * x_lo, axis=-1, keepdims=True)
          + jnp.sum(x_hi * x_hi, axis=-1, keepdims=True)) * (1.0 / (2 * half))
    inv = lax.rsqrt(ms + EPS)
    o_ref[:, :half] = x_lo * inv * fg_ref[:, :half]
    o_ref[:, half:] = x_hi * inv * fg_ref[:, half:]


def _combine_call(yk, gates, base, h2, g2, fg, sg, su, sd, tiles_per_batch, tile, tile0):
    n, d = base.shape
    row = lambda w: pl.BlockSpec((tile, w), lambda i: (i + tile0, 0))
    hbm = pl.BlockSpec(memory_space=pl.ANY)
    staged = (sg, su, sd)
    return pl.pallas_call(
        _combine_kernel,
        grid=(yk.shape[1] // tile,),
        in_specs=[pl.BlockSpec((TOP_K, tile, yk.shape[2]), lambda i: (0, i, 0)),
                  pl.BlockSpec((TOP_K, tile), lambda i: (0, i)), row(d),
                  pl.BlockSpec((tile, h2.shape[1]), lambda i: (i, 0)),
                  pl.BlockSpec((1, 1, d), lambda i: ((i + tile0) // tiles_per_batch, 0, 0)),
                  pl.BlockSpec((1, d), lambda i: (0, 0)), hbm, hbm, hbm],
        out_specs=row(d),
        out_shape=jax.ShapeDtypeStruct((n, d), F32),
        scratch_shapes=([pltpu.VMEM(a.shape[1:], a.dtype) for a in staged]
                        + [pltpu.VMEM(a.shape[1:], BF16) for a in staged]
                        + [pltpu.SemaphoreType.DMA((len(staged),))]),
        input_output_aliases={2: 0},
        compiler_params=pltpu.CompilerParams(dimension_semantics=("arbitrary",),
                                             vmem_limit_bytes=VMEM_LIMIT),
    )(yk, gates, base, h2, g2, fg, sg, su, sd)


def _pad_cols(a, width):
    return jnp.pad(a, ((0, 0), (0, width - a.shape[1])))


def _layer(x, mod, norm1_g, norm2_g, w_in, conv_w, conv_b, dt_bias, a_log, d_skip, ssm_norm_g,
           att_norm_g, sinks, rel_bias, w_out, router_w, router_bias, exp_w_gate, exp_w_up, exp_w_down,
           sh_w_gate, sh_w_up, sh_w_down, final_g):
    bsz, l, d = x.shape
    n = bsz * l
    tm = min(ROW_TILE, l)

    sh1, sc1, g1, sh2, sc2, g2 = [m[:, None, :] for m in jnp.split(mod, 6, axis=-1)]

    assert math.frexp(ATT_HEAD_DIM ** -0.5)[0] == 0.5
    x2 = x.reshape(n, d)
    z, xbc, dt, q, k, v, x1 = _in_proj_call(x2, sc1, sh1, norm1_g[None, :], jnp.swapaxes(w_in, 1, 2), l // tm, tm)

    triu = jnp.asarray(np.triu(np.ones((CHUNK, CHUNK), np.float32))).astype(BF16)
    shift = jnp.asarray(_conv_shift_matrix()).astype(BF16)
    y_ssm = _ssd_call(xbc.reshape(bsz, l, CONV_CH), z.reshape(bsz, l, D_SSM), dt.reshape(bsz, l, LANES),
                      conv_w, conv_b[None, :], dt_bias[:, None], a_log[:, None],
                      jnp.repeat(d_skip, SSM_HEAD_DIM)[None, :], ssm_norm_g[None, :], triu, shift)

    bias = _bias_call(rel_bias, jnp.asarray(_rel_bucket_table()))
    y_att = _attn_call(sinks, q.reshape(bsz, l, D_ATT), k.reshape(bsz, l, D_KV), v.reshape(bsz, l, D_KV), bias,
                       att_norm_g[None, :])

    rw = _pad_cols(router_w, LANES)
    rwh = rw.astype(BF16)
    rwl = jnp.concatenate([rwh, (rw - rwh.astype(F32)).astype(BF16)], axis=1)
    rs = min(RANK_SUB, tm)
    upper = jnp.asarray(np.triu(np.ones((rs, rs), np.float32), 1)).astype(BF16)
    groups = COMBINE_GROUPS if bsz % COMBINE_GROUPS == 0 else 1
    ng = n // groups

    cap = -(-n // EXPERT_ROWS) * EXPERT_ROWS
    counts = jnp.zeros((N_EXPERTS, LANES), F32)
    xs, parts = N_EXPERTS * cap, []
    for g in range(groups):
        x1, h2, idx, gates, rank, slot, counts = _out_proj_call(
            x1, y_ssm.reshape(n, D_SSM), y_att.reshape(n, D_ATT), g1, sc2, sh2, norm2_g[None, :],
            w_out, rwh, rwl, router_bias[:, None], upper, counts, l // tm, tm, g * ng // tm, ng // tm, cap)
        xs = _scatter_rows_sc(h2, slot.reshape(-1), xs, SC_CHUNK // 2)
        parts.append((h2, idx, gates, rank))
    idx = jnp.concatenate([p[1] for p in parts], axis=1)
    rank = jnp.concatenate([p[3] for p in parts], axis=1)
    rt = min(ROUTE_TILE, n)

    nblocks = (n * TOP_K + N_EXPERTS * (EXPERT_ROWS - 1) + EXPERT_ROWS - 1) // EXPERT_ROWS
    pad_start, plan = _plan_call(counts, nblocks, cap)
    dest = _dest_call(idx, rank, pad_start, rt)

    ys = _expert_call(plan, xs, exp_w_gate, exp_w_up, exp_w_down, EXPERT_ROWS, nblocks)
    ctile = min(COMBINE_TILE, l)
    out = x1
    for g in range(groups):
        idx_g = dest[:, g * ng // LANES:(g + 1) * ng // LANES, :].reshape(-1)
        yk = _gather_rows_sc(ys, idx_g, SC_CHUNK).reshape(TOP_K, ng, ys.shape[1])
        out = _combine_call(yk, parts[g][2], out, parts[g][0], g2, final_g[None, :], sh_w_gate, sh_w_up,
                            sh_w_down, l // ctile, ctile, g * ng // ctile)
    return out.reshape(bsz, l, d)


def kernel(x, c, mod_w, mod_b, norm1_g, norm2_g, w_in, conv_w, conv_b, dt_bias, a_log, d_skip, ssm_norm_g,
           att_norm_g, sinks, rel_bias, w_out, router_w, router_bias, exp_w_gate, exp_w_up, exp_w_down,
           sh_w_gate, sh_w_up, sh_w_down, final_g):
    assert mod_w.shape[0] == 1, "single-layer block"
    bsz = x.shape[0]
    c_pad = jnp.pad(c, ((0, SUBLANES - bsz % SUBLANES if bsz % SUBLANES else 0), (0, 0)))
    mod = _mod_call(c_pad, mod_w[0], mod_b[0][None, :])[:bsz]
    return _layer(x, mod, norm1_g[0], norm2_g[0], w_in, conv_w[0], conv_b[0], dt_bias[0], a_log[0], d_skip[0],
                  ssm_norm_g[0], att_norm_g[0], sinks[0], rel_bias, w_out, router_w[0], router_bias[0],
                  exp_w_gate[0], exp_w_up[0], exp_w_down[0], sh_w_gate, sh_w_up, sh_w_down, final_g)
```

```python
import functools
import math

import numpy as np
import jax
import jax.numpy as jnp
from jax import lax
from jax.experimental import pallas as pl
from jax.experimental.pallas import tpu as pltpu
from jax.experimental.pallas import tpu_sc as plsc

F32 = jnp.float32
BF16 = jnp.bfloat16

D_MODEL = 1024
SSM_HEAD_DIM = 64
D_SSM = D_MODEL
SSM_HEADS = D_SSM // SSM_HEAD_DIM
SSM_GROUPS = 4
D_STATE = 128
CONV_K = 4
CONV_CH = D_SSM + 2 * SSM_GROUPS * D_STATE
CHUNK = 128
ATT_HEAD_DIM = 64
D_ATT = D_MODEL
ATT_HEADS = D_ATT // ATT_HEAD_DIM
KV_HEADS = ATT_HEADS // 4
Q_PER_KV = ATT_HEADS // KV_HEADS
D_KV = KV_HEADS * ATT_HEAD_DIM
WINDOW = 128
ATT_BLOCK = 128
REL_BUCKETS = 32
REL_MAX_DIST = 128
N_EXPERTS = 64
TOP_K = 8
EXPERT_DIM = D_MODEL // 4
SHARED_DIM = D_MODEL // 4
ROUTE_GROUPS = 8
ROUTE_TOPK_GROUPS = 4
ROUTED_SCALE = 2.5
EPS = 1e-6

LANES = 128
SUBLANES = 8
HALF = LANES // 2

ROW_TILE = 512
ROUTE_TILE = 2048
RANK_SUB = 512
COMBINE_TILE = 512
SC_CHUNK = 128
SC_GATHER_PARTS = 4
EXPERT_ROWS = 1024
GROUP_SHARES = (2, 1, 1)
VMEM_LIMIT = 48 * 1024 * 1024

NEG_INF = float("-inf")


def _silu(v):
    return v * (1.0 / (1.0 + jnp.exp(-v)))


def _softplus(v):
    return jnp.maximum(v, 0.0) + jnp.log(1.0 + jnp.exp(-jnp.abs(v)))


def _split_hi_lo(v):
    hi = v.astype(BF16)
    lo = (v - hi.astype(F32)).astype(BF16)
    return hi, lo


def _pack_bf16_pair(a, b):
    w = pltpu.pack_elementwise([a, b], packed_dtype=BF16)
    return w if w.dtype == jnp.uint32 else lax.bitcast_convert_type(w, jnp.uint32)


def _unpack_bf16_pair(w):
    a = pltpu.unpack_elementwise(w, index=0, packed_dtype=BF16, unpacked_dtype=F32)
    b = pltpu.unpack_elementwise(w, index=1, packed_dtype=BF16, unpacked_dtype=F32)
    return a, b


def _lane_half_mask(shape):
    return lax.broadcasted_iota(jnp.int32, shape, len(shape) - 1) < HALF


def _mod_kernel(c_ref, w_ref, b_ref, o_ref):
    a = _silu(c_ref[...])
    o_ref[...] = jnp.dot(a, w_ref[...], precision=lax.Precision.HIGHEST,
                         preferred_element_type=F32) + b_ref[...]


def _mod_call(c_pad, mod_w, mod_b):
    rows, d = c_pad.shape
    cols = mod_w.shape[1]
    return pl.pallas_call(
        _mod_kernel,
        grid=(cols // d,),
        in_specs=[pl.BlockSpec((rows, d), lambda j: (0, 0)),
                  pl.BlockSpec((d, d), lambda j: (0, j)),
                  pl.BlockSpec((1, d), lambda j: (0, j))],
        out_specs=pl.BlockSpec((rows, d), lambda j: (0, j)),
        out_shape=jax.ShapeDtypeStruct((rows, cols), F32),
        compiler_params=pltpu.CompilerParams(dimension_semantics=("arbitrary",),
                                             vmem_limit_bytes=VMEM_LIMIT),
    )(c_pad, mod_w, mod_b)


IN_PROJ_SEGMENTS = ((D_SSM, BF16), (CONV_CH, BF16), (LANES, F32), (D_ATT, BF16), (D_KV, BF16), (D_KV, BF16))


def _in_proj_kernel(x_ref, sc_ref, sh_ref, g_ref, w_hbm, *refs):
    out_refs = refs[:len(IN_PROJ_SEGMENTS)]
    res_ref, wraw_ref, w_ref, wsem = refs[len(IN_PROJ_SEGMENTS):]

    @pl.when(pl.program_id(0) == 0)
    def _():
        cp = pltpu.make_async_copy(w_hbm.at[0], wraw_ref, wsem)
        cp.start()
        cp.wait()
        src_dt = D_SSM + CONV_CH
        dst_q = src_dt + LANES
        src_q = src_dt + SSM_HEADS
        q_scale = ATT_HEAD_DIM ** -0.5

        def put(dst, src, scale=None):
            t = wraw_ref[src:src + LANES, :].T
            w_ref[:, dst:dst + LANES] = (t if scale is None else t * scale).astype(BF16)

        for c0 in range(0, src_dt, LANES):
            put(c0, c0)
        dt_tile = wraw_ref[src_dt:src_dt + LANES, :].T
        lane = lax.broadcasted_iota(jnp.int32, dt_tile.shape, 1)
        w_ref[:, src_dt:dst_q] = jnp.where(lane < SSM_HEADS, dt_tile, 0.0).astype(BF16)
        for c0 in range(0, D_ATT, LANES):
            put(dst_q + c0, src_q + c0, q_scale)
        for c0 in range(D_ATT, D_ATT + 2 * D_KV, LANES):
            put(dst_q + c0, src_q + c0)

    xf = x_ref[...]
    res_ref[...] = xf
    ms = jnp.mean(xf * xf, axis=-1, keepdims=True)
    h = xf * lax.rsqrt(ms + EPS) * g_ref[...]
    h = h * (1.0 + sc_ref[0]) + sh_ref[0]
    hb = h.astype(BF16)
    col = 0
    for (width, dtype), o_ref in zip(IN_PROJ_SEGMENTS, out_refs):
        o_ref[...] = jnp.dot(hb, w_ref[:, col:col + width], preferred_element_type=F32).astype(dtype)
        col += width


def _in_proj_call(x2, sc1, sh1, g1n, w_in_t, tiles_per_batch, tm):
    n, d = x2.shape
    row = lambda w: pl.BlockSpec((tm, w), lambda i: (i, 0))
    full = lambda a: pl.BlockSpec(a.shape, lambda i: (0, 0))
    per_batch = pl.BlockSpec((1, 1, d), lambda i: (i // tiles_per_batch, 0, 0))
    cols = sum(w for w, _ in IN_PROJ_SEGMENTS)
    w_in = w_in_t
    assert w_in.shape == (1, cols - (LANES - SSM_HEADS), d)
    return pl.pallas_call(
        _in_proj_kernel,
        grid=(n // tm,),
        in_specs=[row(d), per_batch, per_batch, full(g1n), pl.BlockSpec(memory_space=pl.ANY)],
        out_specs=[row(w) for w, _ in IN_PROJ_SEGMENTS] + [row(d)],
        out_shape=[jax.ShapeDtypeStruct((n, w), dt) for w, dt in IN_PROJ_SEGMENTS] + [jax.ShapeDtypeStruct((n, d), F32)],
        scratch_shapes=[pltpu.VMEM(w_in.shape[1:], w_in.dtype), pltpu.VMEM((d, cols), BF16), pltpu.SemaphoreType.DMA(())],
        compiler_params=pltpu.CompilerParams(dimension_semantics=("arbitrary",),
                                             vmem_limit_bytes=VMEM_LIMIT),
    )(x2, sc1, sh1, g1n, w_in)


SSD_SEQS = 4
CONV_HALO = 16


def _conv_shift_matrix():
    s = np.zeros((CONV_K * CHUNK, CONV_HALO + CHUNK), np.float32)
    for k in range(CONV_K):
        t = np.arange(CHUNK)
        s[k * CHUNK + t, CONV_HALO + t - (CONV_K - 1) + k] = 1.0
    return s


def _silu_tanh(v):
    hv = 0.5 * v
    return hv + hv * jnp.tanh(hv)


def _ssd_kernel(xbc_ref, z_ref, dt_ref, cw_ref, cb_ref, dtb_ref, alog_ref, dskip_ref, ng_ref, triu_ref, shift_ref,
                y_ref, state_ref, ucat_ref, ybuf_ref):
    nseq = xbc_ref.shape[0]

    @pl.when(pl.program_id(1) == 0)
    def _():
        state_ref[...] = jnp.zeros_like(state_ref)
        ucat_ref[:, 0:CONV_HALO, :] = jnp.zeros((nseq, CONV_HALO, CONV_CH), BF16)

    for q in range(nseq):
        _ssd_chunk(xbc_ref.at[q], z_ref.at[q], dt_ref.at[q], cw_ref, cb_ref, dtb_ref, alog_ref, dskip_ref, ng_ref,
                   triu_ref, shift_ref, y_ref.at[q], state_ref.at[q], ucat_ref.at[q], ybuf_ref.at[q])


def _ssd_chunk(xbc_ref, z_ref, dt_ref, cw_ref, cb_ref, dtb_ref, alog_ref, dskip_ref, ng_ref, triu_ref, shift_ref,
               y_ref, state_ref, ucat_ref, ybuf_ref):
    ucat_ref[CONV_HALO:, :] = xbc_ref[...]
    shifted = jnp.dot(shift_ref[...], ucat_ref[...], preferred_element_type=F32)
    ucat_ref[0:CONV_HALO, :] = ucat_ref[CHUNK:CHUNK + CONV_HALO, :]
    acc = cb_ref[...] + cw_ref[0:1, :] * shifted[0:CHUNK]
    for kk in range(1, CONV_K):
        acc = acc + cw_ref[kk:kk + 1, :] * shifted[kk * CHUNK:(kk + 1) * CHUNK]
    act = _silu_tanh(acc)
    xs = act[:, :D_SSM]
    gn = SSM_GROUPS * D_STATE

    dt_t = _softplus(dt_ref[...].T[0:SSM_HEADS, :] + dtb_ref[...])
    a_t = dt_t * (-jnp.exp(alog_ref[...]))
    a_hi = a_t.astype(BF16)
    a_mid = (a_t - a_hi.astype(F32)).astype(BF16)
    a_lo = (a_t - a_hi.astype(F32) - a_mid.astype(F32)).astype(BF16)
    triu = triu_ref[...]
    cs_t = (jnp.dot(a_hi, triu, preferred_element_type=F32) + jnp.dot(a_mid, triu, preferred_element_type=F32)
            + jnp.dot(a_lo, triu, preferred_element_type=F32))
    cs_end = cs_t[:, CHUNK - 1:CHUNK]
    r_t = cs_t - jnp.log(dt_t)
    w_t = jnp.exp(cs_end - cs_t) * dt_t
    chunk_decay = jnp.exp(cs_end)
    cols = jnp.concatenate([cs_t, jnp.exp(cs_t), jnp.zeros((LANES - 2 * SSM_HEADS, CHUNK), F32)], axis=0).T

    li = lax.broadcasted_iota(jnp.int32, (CHUNK, CHUNK), 0)
    si = lax.broadcasted_iota(jnp.int32, (CHUNK, CHUNK), 1)
    causal = li >= si
    low = _lane_half_mask((CHUNK, LANES))
    low_row = _lane_half_mask((1, LANES))

    heads_per_group = SSM_HEADS // SSM_GROUPS
    for g in range(SSM_GROUPS):
        b_g = act[:, D_SSM + g * D_STATE:D_SSM + (g + 1) * D_STATE]
        c_g = act[:, D_SSM + gn + g * D_STATE:D_SSM + gn + (g + 1) * D_STATE]
        b_gb = b_g.astype(BF16)
        c_gb = c_g.astype(BF16)
        cb = lax.dot_general(c_gb, b_gb, (((1,), (1,)), ((), ())), preferred_element_type=F32)
        b_t = b_g.T
        for jp in range(heads_per_group // 2):
            j = g * (heads_per_group // 2) + jp
            lanes = slice(j * LANES, (j + 1) * LANES)
            xp = xs[:, lanes]
            xpb = xp.astype(BF16)
            ydiag = jnp.zeros((CHUNK, LANES), F32)
            snew = jnp.zeros((D_STATE, LANES), F32)
            for half in range(2):
                h = 2 * j + half
                diff = cols[:, h:h + 1] - r_t[h:h + 1, :]
                m = (cb * jnp.exp(jnp.where(causal, diff, NEG_INF))).astype(BF16)
                keep = low if half == 0 else jnp.logical_not(low)
                xh = jnp.where(keep, xpb, jnp.zeros_like(xpb))
                ydiag = ydiag + jnp.dot(m, xh, preferred_element_type=F32)
                snew = snew + jnp.dot((b_t * w_t[h:h + 1, :]).astype(BF16), xh, preferred_element_type=F32)
            s_in = state_ref[:, lanes]
            yoff = jnp.dot(c_gb, s_in.astype(BF16), preferred_element_type=F32)
            h0 = 2 * j
            e0 = SSM_HEADS + h0
            escale = jnp.where(low, cols[:, e0:e0 + 1], cols[:, e0 + 1:e0 + 2])
            cdec = jnp.where(low_row, chunk_decay[h0:h0 + 1, :], chunk_decay[h0 + 1:h0 + 2, :])
            ybuf_ref[:, lanes] = ydiag + yoff * escale + xp * dskip_ref[:, lanes]
            state_ref[:, lanes] = s_in * cdec + snew

    yz = ybuf_ref[...] * _silu_tanh(z_ref[...].astype(F32))
    gw = D_SSM // SSM_GROUPS
    for g in range(SSM_GROUPS):
        part = yz[:, g * gw:(g + 1) * gw]
        ms = jnp.mean(part * part, axis=-1, keepdims=True)
        y_ref[:, g * gw:(g + 1) * gw] = (part * lax.rsqrt(ms + EPS)
                                            * ng_ref[:, g * gw:(g + 1) * gw]).astype(BF16)


def _ssd_call(xbc, z, dt, conv_w, conv_b, dtb, alog, dskip, ng, triu, shift):
    bsz, l, _ = xbc.shape
    nc = l // CHUNK
    nseq = SSD_SEQS if bsz % SSD_SEQS == 0 else 1
    chunk = lambda w: pl.BlockSpec((nseq, CHUNK, w), lambda b, c: (b, c, 0))
    full = lambda a: pl.BlockSpec(a.shape, lambda b, c: (0, 0))
    return pl.pallas_call(
        _ssd_kernel,
        grid=(bsz // nseq, nc),
        in_specs=[chunk(CONV_CH), chunk(D_SSM), chunk(LANES), full(conv_w), full(conv_b), full(dtb),
                  full(alog), full(dskip), full(ng), full(triu), full(shift)],
        out_specs=chunk(D_SSM),
        out_shape=jax.ShapeDtypeStruct((bsz, l, D_SSM), BF16),
        scratch_shapes=[pltpu.VMEM((nseq, D_STATE, D_SSM), F32),
                        pltpu.VMEM((nseq, CONV_HALO + CHUNK, CONV_CH), BF16),
                        pltpu.VMEM((nseq, CHUNK, D_SSM), F32)],
        compiler_params=pltpu.CompilerParams(dimension_semantics=("arbitrary", "arbitrary"),
                                             vmem_limit_bytes=VMEM_LIMIT),
    )(xbc, z, dt, conv_w, conv_b, dtb, alog, dskip, ng, triu, shift)


assert WINDOW == ATT_BLOCK


def _rel_bucket_table():
    qi = np.arange(ATT_BLOCK)[:, None]
    c = np.arange(ATT_BLOCK)[None, :]
    dist = np.where(c > qi, qi + ATT_BLOCK - c, qi - c)
    max_exact = REL_BUCKETS // 2
    d = np.maximum(dist, 1).astype(np.float32)
    large = max_exact + (np.log(d / np.float32(max_exact)) / np.float32(math.log(REL_MAX_DIST / max_exact))
                         * np.float32(REL_BUCKETS - max_exact)).astype(np.int32)
    large = np.minimum(large, REL_BUCKETS - 1)
    return np.where(dist < max_exact, dist, large).astype(np.int32)


def _bias_kernel(rb_ref, bucket_ref, o_ref):
    bucket = bucket_ref[...]
    from_prev = (lax.broadcasted_iota(jnp.int32, bucket.shape, 1)
                 > lax.broadcasted_iota(jnp.int32, bucket.shape, 0))
    for h in range(ATT_HEADS):
        acc = jnp.zeros(bucket.shape, F32)
        for b in range(REL_BUCKETS):
            acc = jnp.where(bucket == b, rb_ref[b, h], acc)
        o_ref[1, h] = acc
        o_ref[0, h] = jnp.where(from_prev, NEG_INF, acc)


def _bias_call(rel_bias, bucket):
    return pl.pallas_call(
        _bias_kernel,
        in_specs=[pl.BlockSpec(memory_space=pltpu.SMEM), pl.BlockSpec(memory_space=pltpu.VMEM)],
        out_shape=jax.ShapeDtypeStruct((2, ATT_HEADS) + bucket.shape, F32),
    )(rel_bias, bucket)


ATT_SEQS = 4


def _attn_kernel(sink_ref, q_ref, kp_ref, kc_ref, vp_ref, vc_ref, bias_ref, ng_ref, o_ref, obuf_ref):
    for s in range(q_ref.shape[0]):
        _attn_block(sink_ref, q_ref.at[s], kp_ref.at[s], kc_ref.at[s], vp_ref.at[s], vc_ref.at[s], bias_ref.at[0],
                    ng_ref, o_ref.at[s], obuf_ref.at[s])


def _attn_block(sink_ref, q_ref, kp_ref, kc_ref, vp_ref, vc_ref, bias_ref, ng_ref, o_ref, obuf_ref):
    qi = lax.broadcasted_iota(jnp.int32, (ATT_BLOCK, ATT_BLOCK), 0)
    ci = lax.broadcasted_iota(jnp.int32, (ATT_BLOCK, ATT_BLOCK), 1)
    from_prev = ci > qi
    low = _lane_half_mask((ATT_BLOCK, LANES))

    def band_variants(prev_ref, cur_ref):
        out = []
        for cpair in range(KV_HEADS // 2):
            lanes = slice(cpair * LANES, (cpair + 1) * LANES)
            t = jnp.concatenate([prev_ref[:, lanes], cur_ref[:, lanes]], axis=0).astype(F32)
            out.append((t.astype(BF16), pltpu.roll(t, HALF, 1).astype(BF16)))
        return out

    k_band = band_variants(kp_ref, kc_ref)
    v_band = band_variants(vp_ref, vc_ref)
    nt = (((1,), (1,)), ((), ()))

    for j in range(ATT_HEADS // 2):
        qp = q_ref[:, j * LANES:(j + 1) * LANES]
        out_pair = jnp.zeros((ATT_BLOCK, LANES), F32)
        for half in range(2):
            h = 2 * j + half
            g = h // Q_PER_KV
            swapped = int((g % 2) != half)
            keep = low if half == 0 else jnp.logical_not(low)
            qh = jnp.where(keep, qp, jnp.zeros_like(qp))
            s_band = lax.dot_general(qh, k_band[g // 2][swapped], nt, preferred_element_type=F32)
            s = jnp.where(from_prev, s_band[:, :ATT_BLOCK], s_band[:, ATT_BLOCK:]) + bias_ref[h]
            sink = sink_ref[h]
            m = jnp.maximum(jnp.max(s, axis=-1, keepdims=True), sink)
            p = jnp.exp(s - m)
            denom = jnp.sum(p, axis=-1, keepdims=True) + jnp.exp(sink - m)
            p_band = jnp.concatenate([jnp.where(from_prev, p, 0.0), jnp.where(from_prev, 0.0, p)], axis=1)
            o = jnp.dot(p_band.astype(BF16), v_band[g // 2][swapped], preferred_element_type=F32) / denom
            out_pair = out_pair + jnp.where(keep, o, 0.0)
        obuf_ref[:, j * LANES:(j + 1) * LANES] = out_pair

    att = obuf_ref[...]
    ms = jnp.mean(att * att, axis=-1, keepdims=True)
    o_ref[...] = (att * lax.rsqrt(ms + EPS) * ng_ref[...]).astype(BF16)


def _attn_call(sinks, q, k, v, bias, ng):
    bsz, l, _ = q.shape
    nb = l // ATT_BLOCK
    nseq = ATT_SEQS if bsz % ATT_SEQS == 0 else 1
    cur = lambda w: pl.BlockSpec((nseq, ATT_BLOCK, w), lambda b, i: (b, i, 0))
    prev = lambda w: pl.BlockSpec((nseq, ATT_BLOCK, w), lambda b, i: (b, jnp.maximum(i - 1, 0), 0))
    return pl.pallas_call(
        _attn_kernel,
        grid=(bsz // nseq, nb),
        in_specs=[pl.BlockSpec(memory_space=pltpu.SMEM),
                  cur(D_ATT), prev(D_KV), cur(D_KV), prev(D_KV), cur(D_KV),
                  pl.BlockSpec((1,) + bias.shape[1:], lambda b, i: (jnp.minimum(i, 1), 0, 0, 0)),
                  pl.BlockSpec(ng.shape, lambda b, i: (0, 0))],
        out_specs=cur(D_ATT),
        out_shape=jax.ShapeDtypeStruct((bsz, l, D_ATT), BF16),
        scratch_shapes=[pltpu.VMEM((nseq, ATT_BLOCK, D_ATT), F32)],
        compiler_params=pltpu.CompilerParams(dimension_semantics=("arbitrary", "arbitrary"),
                                             vmem_limit_bytes=VMEM_LIMIT),
    )(sinks, q, k, k, v, v, bias, ng)


def _stage_bf16(staged, wsem):
    copies = [pltpu.make_async_copy(src.at[0], raw, wsem.at[j]) for j, (src, raw, _) in enumerate(staged)]
    for cp in copies:
        cp.start()
    for cp, (_, raw, dst) in zip(copies, staged):
        cp.wait()
        dst[...] = raw[...].astype(BF16)


def _out_proj_kernel(x_ref, ys_ref, ya_ref, g1_ref, sc_ref, sh_ref, ng_ref, wo_hbm, rwh_ref, rwl_ref, rb_ref,
                     upper_ref, carry_in_ref,
                     x1_ref, h_ref, idx_ref, gate_ref, rank_ref, slot_ref, cnt_ref,
                     carry_ref, wo_raw, wo_ref, wsem, *, cap):
    @pl.when(pl.program_id(0) == 0)
    def _():
        carry_ref[...] = carry_in_ref[...]
        _stage_bf16(((wo_hbm, wo_raw, wo_ref),), wsem)

    mix = (jnp.dot(ys_ref[...], wo_ref[:D_SSM, :], preferred_element_type=F32)
           + jnp.dot(ya_ref[...], wo_ref[D_SSM:, :], preferred_element_type=F32))
    x1 = x_ref[...] + g1_ref[0] * mix
    ms = jnp.mean(x1 * x1, axis=-1, keepdims=True)
    h = x1 * lax.rsqrt(ms + EPS) * ng_ref[...]
    h = h * (1.0 + sc_ref[0]) + sh_ref[0]
    half = h.shape[1] // 2
    h_ref[...] = _pack_bf16_pair(h[:, :half], h[:, half:])
    hi, lo = _split_hi_lo(h)
    hi_terms = jnp.dot(hi, rwl_ref[...], preferred_element_type=F32)
    logits = (hi_terms[:, :LANES] + hi_terms[:, LANES:]
              + jnp.dot(lo, rwh_ref[...], preferred_element_type=F32))
    _route_tokens(logits.T[0:N_EXPERTS, :], rb_ref, upper_ref, idx_ref, gate_ref, rank_ref, cnt_ref, carry_ref)
    slot_ref[...] = idx_ref[...] * cap + rank_ref[...]
    x1_ref[...] = x1


def _out_proj_call(x2, ys, ya, g1, sc2, sh2, ng, w_out, rwh, rwl, router_bias, upper, carry_in,
                   tiles_per_batch, tm, tile0, ntiles, cap):
    n, d = x2.shape
    nt = ntiles * tm
    row = lambda w: pl.BlockSpec((tm, w), lambda i: (i + tile0, 0))
    own = lambda w: pl.BlockSpec((tm, w), lambda i: (i, 0))
    tok = lambda r: pl.BlockSpec((r, tm), lambda i: (0, i))
    full = lambda a: pl.BlockSpec(a.shape, lambda i: (0, 0))
    per_batch = pl.BlockSpec((1, 1, d), lambda i: ((i + tile0) // tiles_per_batch, 0, 0))
    hbm = pl.BlockSpec(memory_space=pl.ANY)
    staged = (w_out,)
    return pl.pallas_call(
        functools.partial(_out_proj_kernel, cap=cap),
        grid=(ntiles,),
        in_specs=[row(d), row(D_SSM), row(D_ATT), per_batch, per_batch, per_batch, full(ng),
                  hbm, full(rwh), full(rwl), full(router_bias), full(upper), full(carry_in)],
        out_specs=[row(d), own(d // 2), tok(TOP_K), tok(TOP_K), tok(TOP_K), tok(TOP_K),
                   pl.BlockSpec((N_EXPERTS, LANES), lambda i: (0, 0))],
        out_shape=[jax.ShapeDtypeStruct((n, d), F32), jax.ShapeDtypeStruct((nt, d // 2), jnp.uint32),
                   jax.ShapeDtypeStruct((TOP_K, nt), jnp.int32), jax.ShapeDtypeStruct((TOP_K, nt), F32),
                   jax.ShapeDtypeStruct((TOP_K, nt), jnp.int32), jax.ShapeDtypeStruct((TOP_K, nt), jnp.int32),
                   jax.ShapeDtypeStruct((N_EXPERTS, LANES), F32)],
        scratch_shapes=([pltpu.VMEM((N_EXPERTS, LANES), F32)]
                        + [pltpu.VMEM(a.shape[1:], a.dtype) for a in staged]
                        + [pltpu.VMEM(a.shape[1:], BF16) for a in staged]
                        + [pltpu.SemaphoreType.DMA((len(staged),))]),
        input_output_aliases={0: 0},
        compiler_params=pltpu.CompilerParams(dimension_semantics=("arbitrary",),
                                             vmem_limit_bytes=VMEM_LIMIT),
    )(x2, ys, ya, g1, sc2, sh2, ng, w_out, rwh, rwl, router_bias, upper, carry_in)


def _route_tokens(logits_t, rb_ref, upper_ref, idx_ref, gate_ref, rank_ref, cnt_ref, carry_ref):
    t = logits_t.shape[1]
    per_group = N_EXPERTS // ROUTE_GROUPS
    scores = 1.0 / (1.0 + jnp.exp(-logits_t))
    sel = scores + rb_ref[...]
    e_iota = lax.broadcasted_iota(jnp.int32, (N_EXPERTS, t), 0)

    sel3 = sel.reshape(ROUTE_GROUPS, per_group, t)
    w_iota = lax.broadcasted_iota(jnp.int32, sel3.shape, 1)
    m1 = jnp.max(sel3, axis=1, keepdims=True)
    first = jnp.min(jnp.where(sel3 == m1, w_iota, per_group), axis=1, keepdims=True)
    m2 = jnp.max(jnp.where(w_iota == first, NEG_INF, sel3), axis=1, keepdims=True)
    grp = (m1 + m2).reshape(ROUTE_GROUPS, t)

    g_iota = lax.broadcasted_iota(jnp.int32, (ROUTE_GROUPS, t), 0)
    gmask = jnp.zeros((ROUTE_GROUPS, t), jnp.bool_)
    for _ in range(ROUTE_TOPK_GROUPS):
        gm = jnp.max(grp, axis=0, keepdims=True)
        gfirst = jnp.min(jnp.where(grp == gm, g_iota, ROUTE_GROUPS), axis=0, keepdims=True)
        hit = g_iota == gfirst
        gmask = jnp.logical_or(gmask, hit)
        grp = jnp.where(hit, NEG_INF, grp)
    allowed = jnp.broadcast_to(gmask.reshape(ROUTE_GROUPS, 1, t),
                               (ROUTE_GROUPS, per_group, t)).reshape(N_EXPERTS, t)
    masked = jnp.where(allowed, sel, NEG_INF)

    picked = jnp.zeros((N_EXPERTS, t), jnp.bool_)
    idx_rows = []
    w_rows = []
    for _ in range(TOP_K):
        mm = jnp.max(masked, axis=0, keepdims=True)
        efirst = jnp.min(jnp.where(masked == mm, e_iota, N_EXPERTS), axis=0, keepdims=True)
        hit = e_iota == efirst
        idx_rows.append(efirst)
        w_rows.append(jnp.sum(jnp.where(hit, scores, 0.0), axis=0, keepdims=True))
        picked = jnp.logical_or(picked, hit)
        masked = jnp.where(hit, NEG_INF, masked)
    idx = jnp.concatenate(idx_rows, axis=0)
    w = jnp.concatenate(w_rows, axis=0)
    gate_ref[...] = w / jnp.sum(w, axis=0, keepdims=True) * ROUTED_SCALE
    idx_ref[...] = idx

    onehot = jnp.where(picked, 1.0, 0.0)
    sub = upper_ref.shape[0]
    carry = carry_ref[:, 0:1]
    parts = []
    for s0 in range(0, t, sub):
        oh = onehot[:, s0:s0 + sub]
        parts.append(jnp.dot(oh.astype(BF16), upper_ref[...], preferred_element_type=F32) + carry)
        carry = carry + jnp.sum(oh, axis=1, keepdims=True)
    rank_full = jnp.concatenate(parts, axis=1)
    rank_rows = [jnp.sum(jnp.where(e_iota == idx_rows[k], rank_full, 0.0), axis=0, keepdims=True)
                 for k in range(TOP_K)]
    rank_ref[...] = jnp.concatenate(rank_rows, axis=0).astype(jnp.int32)
    carry_ref[...] = jnp.broadcast_to(carry, carry_ref.shape)
    cnt_ref[...] = carry_ref[...]


PLAN_EXPERT, PLAN_FRESH, PLAN_VALID, PLAN_SEG, PLAN_NEXT, PLAN_NUSED, PLAN_XROW = range(7)
PLAN_ROWS = SUBLANES


def _plan_kernel(cnt_ref, tri_ref, start_ref, plan_ref, *, nblocks, cap):
    nbp = plan_ref.shape[1]
    cnt = cnt_ref[...].astype(jnp.int32)
    blocks = (cnt + (EXPERT_ROWS - 1)) // EXPERT_ROWS
    end = jnp.dot(tri_ref[...], blocks.astype(F32), precision=lax.Precision.HIGHEST,
                  preferred_element_type=F32).astype(jnp.int32)
    start = end - blocks
    start_ref[...] = start[:, 0:1] * EXPERT_ROWS
    nused = end[N_EXPERTS - 1:N_EXPERTS, 0:1]

    e_iota = lax.broadcasted_iota(jnp.int32, (N_EXPERTS, nbp), 0)
    blk = lax.broadcasted_iota(jnp.int32, (1, nbp), 1)
    expert = jnp.minimum(jnp.sum((end[:, 0:1] <= blk).astype(jnp.int32), axis=0, keepdims=True), N_EXPERTS - 1)
    mine = e_iota == expert
    pick = lambda col: jnp.sum(jnp.where(mine, col, 0), axis=0, keepdims=True)
    first = pick(start[:, 0:1])
    valid = jnp.clip(pick(cnt[:, 0:1]) - (blk - first) * EXPERT_ROWS, 0, EXPERT_ROWS)
    present = jnp.logical_or(blocks[:, 0:1] > 0,
                             jnp.logical_and(e_iota[:, 0:1] == N_EXPERTS - 1, nused < nblocks))
    seg = jnp.sum(jnp.logical_and(present, e_iota <= expert).astype(jnp.int32), axis=0, keepdims=True) - 1
    nxt = jnp.min(jnp.where(jnp.logical_and(present, e_iota > expert), e_iota, N_EXPERTS), axis=0, keepdims=True)
    rows = {PLAN_EXPERT: expert, PLAN_FRESH: (blk == first).astype(jnp.int32), PLAN_VALID: valid, PLAN_SEG: seg,
            PLAN_NEXT: jnp.where(nxt == N_EXPERTS, -1, nxt), PLAN_NUSED: jnp.broadcast_to(nused, (1, nbp)),
            PLAN_XROW: expert * cap + (blk - first) * EXPERT_ROWS}
    zero = jnp.zeros((1, nbp), jnp.int32)
    plan_ref[...] = jnp.concatenate([rows.get(r, zero) for r in range(PLAN_ROWS)], axis=0)


def _plan_call(counts, nblocks, cap):
    nbp = -(-nblocks // LANES) * LANES
    tri = jnp.asarray(np.tril(np.ones((N_EXPERTS, N_EXPERTS), np.float32)))
    return pl.pallas_call(
        functools.partial(_plan_kernel, nblocks=nblocks, cap=cap),
        out_shape=[jax.ShapeDtypeStruct((N_EXPERTS, 1), jnp.int32),
                   jax.ShapeDtypeStruct((PLAN_ROWS, nbp), jnp.int32)],
    )(counts, tri)


def _dest_kernel(idx_ref, rank_ref, start_ref, dest_ref):
    t = idx_ref.shape[1]
    e_iota = lax.broadcasted_iota(jnp.int32, (N_EXPERTS, t), 0)
    rows = [jnp.sum(jnp.where(e_iota == idx_ref[k:k + 1, :], start_ref[...], 0), axis=0, keepdims=True)
            for k in range(TOP_K)]
    dest = jnp.concatenate(rows, axis=0) + rank_ref[...]
    for k in range(TOP_K):
        for c in range(t // LANES):
            dest_ref[k, c:c + 1, :] = dest[k:k + 1, c * LANES:(c + 1) * LANES]


def _dest_call(idx, rank, pad_start, tile):
    n = idx.shape[1]
    tok = pl.BlockSpec((TOP_K, tile), lambda i: (0, i))
    return pl.pallas_call(
        _dest_kernel,
        grid=(n // tile,),
        in_specs=[tok, tok, pl.BlockSpec((N_EXPERTS, 1), lambda i: (0, 0))],
        out_specs=pl.BlockSpec((TOP_K, tile // LANES, LANES), lambda i: (0, i, 0)),
        out_shape=jax.ShapeDtypeStruct((TOP_K, n // LANES, LANES), jnp.int32),
        compiler_params=pltpu.CompilerParams(dimension_semantics=("arbitrary",)),
    )(idx, rank, pad_start)


def _scatter_rows_sc(rows, dest_flat, into, chunk):
    n, w = rows.shape
    copies = dest_flat.shape[0] // n
    info = plsc.get_sparse_core_info()
    nc = info.num_cores
    per_worker = n // (nc * info.num_subcores)
    assert per_worker * nc * info.num_subcores == n and per_worker % chunk == 0
    mesh = plsc.VectorSubcoreMesh(core_axis_name="c", subcore_axis_name="s")
    scratch_types = ([pltpu.VMEM((chunk,), jnp.int32) for _ in range(copies)]
                     + [pltpu.VMEM((chunk, w), rows.dtype), pltpu.SemaphoreType.DMA, pltpu.SemaphoreType.DMA])

    def scatter(rows_hbm, idx_hbm, out_hbm, *scratch):
        idx_vs, (rows_v, isem, sem) = scratch[:copies], scratch[copies:]
        worker = lax.axis_index("s") * nc + lax.axis_index("c")

        @pl.loop(0, per_worker // chunk)
        def _(j):
            base = worker * per_worker + j * chunk
            loads = [pltpu.async_copy(idx_hbm.at[pl.ds(k * n + base, chunk)], idx_vs[k], isem)
                     for k in range(copies)]
            pltpu.sync_copy(rows_hbm.at[pl.ds(base, chunk)], rows_v)
            for ld in loads:
                ld.wait()
            stores = [pltpu.async_copy(rows_v, out_hbm.at[idx_vs[k]], sem) for k in range(copies)]
            for st in stores:
                st.wait()

    if isinstance(into, int):
        return pl.kernel(scatter, mesh=mesh, out_type=jax.ShapeDtypeStruct((into, w), rows.dtype),
                         scratch_types=scratch_types)(rows, dest_flat)
    out = jax.new_ref(into)
    pl.kernel(scatter, mesh=mesh, out_type=(), scratch_types=scratch_types)(rows, dest_flat, out)
    return jax.freeze(out)


EXPERT_LIVE_STEP = 256
X_RING = 4


def _expert_kernel(plan_ref, xs_hbm, wg_hbm, wu_hbm, wd_hbm, y_ref,
                   wgb_ref, wub_ref, wdb_ref, xbuf_ref, xsem, wg_raw, wu_raw, wd_raw, wsem):
    i = pl.program_id(0)
    rows, half = xbuf_ref.shape[1], xbuf_ref.shape[2]
    nused = plan_ref[PLAN_NUSED, 0]

    def w_copies(expert, slot):
        return [pltpu.make_async_copy(src.at[expert], dst.at[slot], wsem.at[slot])
                for src, dst in ((wg_hbm, wg_raw), (wu_hbm, wu_raw), (wd_hbm, wd_raw))]

    def x_copy(block):
        first = pl.multiple_of(plan_ref[PLAN_XROW, block], rows)
        slot = block % X_RING
        return pltpu.make_async_copy(xs_hbm.at[pl.ds(first, rows)], xbuf_ref.at[slot], xsem.at[slot])

    @pl.when(i == 0)
    def _():
        for b in range(X_RING - 1):
            @pl.when(b < nused)
            def _():
                x_copy(b).start()

    @pl.when(i + (X_RING - 1) < nused)
    def _():
        x_copy(i + (X_RING - 1)).start()

    @pl.when(i == 0)
    def _():
        for c in w_copies(plan_ref[PLAN_EXPERT, 0], 0):
            c.start()

    @pl.when(plan_ref[PLAN_FRESH, i] > 0)
    def _():
        slot = plan_ref[PLAN_SEG, i] % 2
        for c in w_copies(plan_ref[PLAN_EXPERT, i], slot):
            c.wait()

        @pl.when(plan_ref[PLAN_NEXT, i] >= 0)
        def _():
            for c in w_copies(plan_ref[PLAN_NEXT, i], 1 - slot):
                c.start()

        wgb_ref[...] = wg_raw[slot].astype(BF16)
        wub_ref[...] = wu_raw[slot].astype(BF16)
        wdb_ref[...] = wd_raw[slot].astype(BF16)

    @pl.when(i < nused)
    def _():
        x_copy(i).wait()

    real = plan_ref[PLAN_VALID, i]
    slot = i % X_RING

    def ffn(live):
        xw = xbuf_ref[slot, 0:live, :]
        row = lax.broadcasted_iota(jnp.int32, xw.shape, 0)
        x_lo, x_hi = _unpack_bf16_pair(jnp.where(row < real, xw, jnp.uint32(0)))
        x_lo = x_lo.astype(BF16)
        x_hi = x_hi.astype(BF16)
        gate = (jnp.dot(x_lo, wgb_ref[:half, :], preferred_element_type=F32)
                + jnp.dot(x_hi, wgb_ref[half:, :], preferred_element_type=F32))
        up = (jnp.dot(x_lo, wub_ref[:half, :], preferred_element_type=F32)
              + jnp.dot(x_hi, wub_ref[half:, :], preferred_element_type=F32))
        u = (_silu(gate) * up).astype(BF16)
        y_lo = jnp.dot(u, wdb_ref[:, :half], preferred_element_type=F32)
        y_hi = jnp.dot(u, wdb_ref[:, half:], preferred_element_type=F32)
        y_ref[0:live, :] = _pack_bf16_pair(y_lo, y_hi)
        if live < rows:
            y_ref[live:, :] = jnp.zeros((rows - live, half), y_ref.dtype)

    pieces = (real + (EXPERT_LIVE_STEP - 1)) // EXPERT_LIVE_STEP
    for p in range(1, rows // EXPERT_LIVE_STEP + 1):
        @pl.when(jnp.logical_and(i < nused, pieces == p))
        def _():
            ffn(p * EXPERT_LIVE_STEP)

    @pl.when(i == nused)
    def _():
        y_ref[...] = jnp.zeros_like(y_ref)


def _expert_call(plan, xs, wg, wu, wd, rows, nblocks):
    p, w = nblocks * rows, xs.shape[1]
    d, f = wg.shape[1], wg.shape[2]
    hbm = pl.BlockSpec(memory_space=pl.ANY)
    grid_spec = pltpu.PrefetchScalarGridSpec(
        num_scalar_prefetch=1,
        grid=(p // rows,),
        in_specs=[hbm, hbm, hbm, hbm],
        out_specs=pl.BlockSpec((rows, w), lambda i, plan: (jnp.where(i < plan[PLAN_NUSED, 0], i, nblocks - 1), 0)),
        scratch_shapes=[pltpu.VMEM((d, f), BF16), pltpu.VMEM((d, f), BF16), pltpu.VMEM((f, d), BF16),
                        pltpu.VMEM((X_RING, rows, w), xs.dtype), pltpu.SemaphoreType.DMA((X_RING,)),
                        pltpu.VMEM((2, d, f), wg.dtype), pltpu.VMEM((2, d, f), wu.dtype),
                        pltpu.VMEM((2, f, d), wd.dtype), pltpu.SemaphoreType.DMA((2,))],
    )
    return pl.pallas_call(
        _expert_kernel,
        grid_spec=grid_spec,
        out_shape=jax.ShapeDtypeStruct((p, w), jnp.uint32),
        compiler_params=pltpu.CompilerParams(dimension_semantics=("arbitrary",),
                                             vmem_limit_bytes=VMEM_LIMIT),
    )(plan, xs, wg, wu, wd)


def _gather_rows_sc(table, idx, chunk):
    m = idx.shape[0]
    w = table.shape[1]
    info = plsc.get_sparse_core_info()
    nc = info.num_cores
    per_worker = m // (nc * info.num_subcores)
    assert per_worker * nc * info.num_subcores == m and per_worker % chunk == 0
    part = chunk // SC_GATHER_PARTS
    mesh = plsc.VectorSubcoreMesh(core_axis_name="c", subcore_axis_name="s")

    @functools.partial(
        pl.kernel, mesh=mesh,
        out_type=jax.ShapeDtypeStruct((m, w), table.dtype),
        scratch_types=([pltpu.VMEM((part,), jnp.int32) for _ in range(SC_GATHER_PARTS)]
                       + [pltpu.VMEM((part, w), table.dtype) for _ in range(SC_GATHER_PARTS)]
                       + [pltpu.SemaphoreType.DMA] * 3),
    )
    def gather(table_hbm, idx_hbm, out_hbm, *scratch):
        idx_vs = scratch[:SC_GATHER_PARTS]
        rows_vs = scratch[SC_GATHER_PARTS:2 * SC_GATHER_PARTS]
        isem, gsem, wsem = scratch[2 * SC_GATHER_PARTS:]
        worker = lax.axis_index("s") * nc + lax.axis_index("c")

        @pl.loop(0, per_worker // chunk)
        def _(j):
            base = worker * per_worker + j * chunk
            loads = [pltpu.async_copy(idx_hbm.at[pl.ds(base + p * part, part)], idx_vs[p], isem)
                     for p in range(SC_GATHER_PARTS)]
            gathers = []
            for p in range(SC_GATHER_PARTS):
                loads[p].wait()
                gathers.append(pltpu.async_copy(table_hbm.at[idx_vs[p]], rows_vs[p], gsem))
            writes = []
            for p in range(SC_GATHER_PARTS):
                gathers[p].wait()
                writes.append(pltpu.async_copy(rows_vs[p], out_hbm.at[pl.ds(base + p * part, part)], wsem))
            for wr in writes:
                wr.wait()

    return gather(table, idx)


def _combine_kernel(yk_ref, gate_ref, base_ref, h_ref, g2_ref, fg_ref, sg_hbm, su_hbm, sd_hbm, o_ref,
                    sg_raw, su_raw, sd_raw, sg_ref, su_ref, sd_ref, wsem):
    @pl.when(pl.program_id(0) == 0)
    def _():
        _stage_bf16(((sg_hbm, sg_raw, sg_ref), (su_hbm, su_raw, su_ref), (sd_hbm, sd_raw, sd_ref)), wsem)

    t = base_ref.shape[0]
    half = yk_ref.shape[2]
    h_lo, h_hi = _unpack_bf16_pair(h_ref[...])
    hb = jnp.concatenate([h_lo.astype(BF16), h_hi.astype(BF16)], axis=1)
    u = _silu(jnp.dot(hb, sg_ref[...], preferred_element_type=F32)) * jnp.dot(hb, su_ref[...],
                                                                              preferred_element_type=F32)
    shared = jnp.dot(u.astype(BF16), sd_ref[...], preferred_element_type=F32)
    gates = jnp.concatenate([gate_ref[...], jnp.zeros((LANES - TOP_K, t), F32)], axis=0).T
    r_lo = shared[:, :half]
    r_hi = shared[:, half:]
    for k in range(TOP_K):
        y_lo, y_hi = _unpack_bf16_pair(yk_ref[k])
        r_lo = r_lo + gates[:, k:k + 1] * y_lo
        r_hi = r_hi + gates[:, k:k + 1] * y_hi
    g2 = g2_ref[0]
    x_lo = base_ref[:, :half] + g2[:, :half] * r_lo
    x_hi = base_ref[:, half:] + g2[:, half:] * r_hi
    ms = (jnp.sum(x_lo * x_lo, axis=-1, keepdims=True)
          + jnp.sum(x_hi * x_hi, axis=-1, keepdims=True)) * (1.0 / (2 * half))
    inv = lax.rsqrt(ms + EPS)
    o_ref[:, :half] = x_lo * inv * fg_ref[:, :half]
    o_ref[:, half:] = x_hi * inv * fg_ref[:, half:]


def _combine_call(yk, gates, base, h2, g2, fg, sg, su, sd, tiles_per_batch, tile, tile0):
    n, d = base.shape
    row = lambda w: pl.BlockSpec((tile, w), lambda i: (i + tile0, 0))
    hbm = pl.BlockSpec(memory_space=pl.ANY)
    staged = (sg, su, sd)
    return pl.pallas_call(
        _combine_kernel,
        grid=(yk.shape[1] // tile,),
        in_specs=[pl.BlockSpec((TOP_K, tile, yk.shape[2]), lambda i: (0, i, 0)),
                  pl.BlockSpec((TOP_K, tile), lambda i: (0, i)), row(d),
                  pl.BlockSpec((tile, h2.shape[1]), lambda i: (i, 0)),
                  pl.BlockSpec((1, 1, d), lambda i: ((i + tile0) // tiles_per_batch, 0, 0)),
                  pl.BlockSpec((1, d), lambda i: (0, 0)), hbm, hbm, hbm],
        out_specs=row(d),
        out_shape=jax.ShapeDtypeStruct((n, d), F32),
        scratch_shapes=([pltpu.VMEM(a.shape[1:], a.dtype) for a in staged]
                        + [pltpu.VMEM(a.shape[1:], BF16) for a in staged]
                        + [pltpu.SemaphoreType.DMA((len(staged),))]),
        input_output_aliases={2: 0},
        compiler_params=pltpu.CompilerParams(dimension_semantics=("arbitrary",),
                                             vmem_limit_bytes=VMEM_LIMIT),
    )(yk, gates, base, h2, g2, fg, sg, su, sd)


def _pad_cols(a, width):
    return jnp.pad(a, ((0, 0), (0, width - a.shape[1])))


def _layer(x, mod, norm1_g, norm2_g, w_in, conv_w, conv_b, dt_bias, a_log, d_skip, ssm_norm_g,
           att_norm_g, sinks, rel_bias, w_out, router_w, router_bias, exp_w_gate, exp_w_up, exp_w_down,
           sh_w_gate, sh_w_up, sh_w_down, final_g):
    bsz, l, d = x.shape
    n = bsz * l
    tm = min(ROW_TILE, l)

    sh1, sc1, g1, sh2, sc2, g2 = [m[:, None, :] for m in jnp.split(mod, 6, axis=-1)]

    assert math.frexp(ATT_HEAD_DIM ** -0.5)[0] == 0.5
    x2 = x.reshape(n, d)
    z, xbc, dt, q, k, v, x1 = _in_proj_call(x2, sc1, sh1, norm1_g[None, :], jnp.swapaxes(w_in, 1, 2), l // tm, tm)

    triu = jnp.asarray(np.triu(np.ones((CHUNK, CHUNK), np.float32))).astype(BF16)
    shift = jnp.asarray(_conv_shift_matrix()).astype(BF16)
    y_ssm = _ssd_call(xbc.reshape(bsz, l, CONV_CH), z.reshape(bsz, l, D_SSM), dt.reshape(bsz, l, LANES),
                      conv_w, conv_b[None, :], dt_bias[:, None], a_log[:, None],
                      jnp.repeat(d_skip, SSM_HEAD_DIM)[None, :], ssm_norm_g[None, :], triu, shift)

    bias = _bias_call(rel_bias, jnp.asarray(_rel_bucket_table()))
    y_att = _attn_call(sinks, q.reshape(bsz, l, D_ATT), k.reshape(bsz, l, D_KV), v.reshape(bsz, l, D_KV), bias,
                       att_norm_g[None, :])

    rw = _pad_cols(router_w, LANES)
    rwh = rw.astype(BF16)
    rwl = jnp.concatenate([rwh, (rw - rwh.astype(F32)).astype(BF16)], axis=1)
    rs = min(RANK_SUB, tm)
    upper = jnp.asarray(np.triu(np.ones((rs, rs), np.float32), 1)).astype(BF16)
    unit = sum(GROUP_SHARES)
    sizes = [n * s // unit for s in GROUP_SHARES] if bsz % unit == 0 else [n]
    spans = [(sum(sizes[:g]), sizes[g]) for g in range(len(sizes))]

    cap = -(-n // EXPERT_ROWS) * EXPERT_ROWS
    counts = jnp.zeros((N_EXPERTS, LANES), F32)
    xs, parts = N_EXPERTS * cap, []
    for t0, nt in spans:
        x1, h2, idx, gates, rank, slot, counts = _out_proj_call(
            x1, y_ssm.reshape(n, D_SSM), y_att.reshape(n, D_ATT), g1, sc2, sh2, norm2_g[None, :],
            w_out, rwh, rwl, router_bias[:, None], upper, counts, l // tm, tm, t0 // tm, nt // tm, cap)
        xs = _scatter_rows_sc(h2, slot.reshape(-1), xs, SC_CHUNK)
        parts.append((h2, idx, gates, rank))
    idx = jnp.concatenate([p[1] for p in parts], axis=1)
    rank = jnp.concatenate([p[3] for p in parts], axis=1)
    rt = min(ROUTE_TILE, n)

    nblocks = (n * TOP_K + N_EXPERTS * (EXPERT_ROWS - 1) + EXPERT_ROWS - 1) // EXPERT_ROWS
    pad_start, plan = _plan_call(counts, nblocks, cap)
    dest = _dest_call(idx, rank, pad_start, rt)

    ys = _expert_call(plan, xs, exp_w_gate, exp_w_up, exp_w_down, EXPERT_ROWS, nblocks)
    ctile = min(COMBINE_TILE, l)
    out = x1
    for g, (t0, nt) in enumerate(spans):
        idx_g = dest[:, t0 // LANES:(t0 + nt) // LANES, :].reshape(-1)
        yk = _gather_rows_sc(ys, idx_g, SC_CHUNK).reshape(TOP_K, nt, ys.shape[1])
        out = _combine_call(yk, parts[g][2], out, parts[g][0], g2, final_g[None, :], sh_w_gate, sh_w_up,
                            sh_w_down, l // ctile, ctile, t0 // ctile)
    return out.reshape(bsz, l, d)


def kernel(x, c, mod_w, mod_b, norm1_g, norm2_g, w_in, conv_w, conv_b, dt_bias, a_log, d_skip, ssm_norm_g,
           att_norm_g, sinks, rel_bias, w_out, router_w, router_bias, exp_w_gate, exp_w_up, exp_w_down,
           sh_w_gate, sh_w_up, sh_w_down, final_g):
    assert mod_w.shape[0] == 1, "single-layer block"
    bsz = x.shape[0]
    c_pad = jnp.pad(c, ((0, SUBLANES - bsz % SUBLANES if bsz % SUBLANES else 0), (0, 0)))
    mod = _mod_call(c_pad, mod_w[0], mod_b[0][None, :])[:bsz]
    return _layer(x, mod, norm1_g[0], norm2_g[0], w_in, conv_w[0], conv_b[0], dt_bias[0], a_log[0], d_skip[0],
                  ssm_norm_g[0], att_norm_g[0], sinks[0], rel_bias, w_out, router_w[0], router_bias[0],
                  exp_w_gate[0], exp_w_up[0], exp_w_down[0], sh_w_gate, sh_w_up, sh_w_down, final_g)
```
